```python
import jax, jax.numpy as jnp
from jax import lax
import numpy as np

D_MODEL = 1024
BATCH = 8
SEQ = 4096
DEPTH = 1

CHUNK = 64
PLE_DIM = 256
MIX_WIDTH = D_MODEL
S5_WIDTH = MIX_WIDTH // 2
RWKV_WIDTH = MIX_WIDTH - S5_WIDTH
S5_GROUP = 16
S5_GROUPS = S5_WIDTH // S5_GROUP
S5_STATE = 64
RWKV_HEAD = 64
RWKV_HEADS = RWKV_WIDTH // RWKV_HEAD
DECAY_LORA = 64
AAA_LORA = 64
GATE_LORA = 128
SHIFT_COLS = 3 * RWKV_WIDTH + DECAY_LORA + AAA_LORA + GATE_LORA
IN_COLS = S5_WIDTH + SHIFT_COLS
FFN_HIDDEN = ((8 * D_MODEL + 3 * 256 - 1) // (3 * 256)) * 256
RMS_EPS = 1e-6
GN_EPS = 64e-5
L2_EPS = 1e-12

kernel_name = "hymba_s5_rwkv7_streaming_block"


def rms_norm(x, g):
    xf = x.astype(jnp.float32)
    y = xf * lax.rsqrt(jnp.mean(xf * xf, axis=-1, keepdims=True) + RMS_EPS)
    return (y * g.astype(jnp.float32)).astype(x.dtype)


def _linear_binop(e1, e2):
    a1, b1 = e1
    a2, b2 = e2
    return a2 * a1, a2 * b1 + b2


def s5_mixer(u, lam_re, lam_im, log_step, b_re, b_im, c_re, c_im, d_skip, glu_w, glu_b):
    f32 = jnp.float32
    bsz, seq, _ = u.shape
    n_chunks = seq // CHUNK
    lam = lax.complex(lam_re.astype(f32), lam_im.astype(f32))
    lam_dt = lam * jnp.exp(log_step.astype(f32))[:, None]
    lam_bar = jnp.exp(lam_dt)
    b_mat = lax.complex(b_re.astype(f32), b_im.astype(f32))
    b_bar = ((lam_bar - 1.0) / lam)[..., None] * b_mat
    c_mat = lax.complex(c_re.astype(f32), c_im.astype(f32))
    d_g = d_skip.astype(f32).reshape(S5_GROUPS, S5_GROUP)
    lam_pow = jnp.exp(lam_dt[None] * jnp.arange(1, CHUNK + 1, dtype=f32)[:, None, None])
    ug = u.astype(f32).reshape(bsz, n_chunks, CHUNK, S5_GROUPS, S5_GROUP)
    ug = jnp.moveaxis(ug, 1, 0)

    def chunk_step(carry, u_c):
        bu = jnp.einsum('gpc,btgc->btgp', b_bar, u_c.astype(jnp.complex64))
        a = jnp.broadcast_to(lam_bar, bu.shape)
        _, local = lax.associative_scan(_linear_binop, (a, bu), axis=1)
        states = local + lam_pow[None] * carry[:, None]
        y = jnp.einsum('gcp,btgp->btgc', c_mat, states).real + d_g * u_c
        return states[:, -1], y

    carry0 = jnp.zeros((bsz, S5_GROUPS, S5_STATE), jnp.complex64)
    _, ys = lax.scan(chunk_step, carry0, ug)
    y = jnp.moveaxis(ys, 0, 1).reshape(bsz, seq, S5_WIDTH)
    z = jax.nn.gelu(y)
    return z * jax.nn.sigmoid(z @ glu_w.astype(f32) + glu_b.astype(f32))


def wkv7_scan(r, w, k, v, a, b):
    bsz, seq, nh, n = r.shape
    n_chunks = seq // CHUNK

    def to_chunks(t):
        return t.reshape(bsz, n_chunks, CHUNK, nh, n).transpose(1, 2, 0, 3, 4)

    def step(s, inp):
        r_t, w_t, k_t, v_t, a_t, b_t = inp
        sa = jnp.einsum('bhvk,bhk->bhv', s, a_t)
        s = s * w_t[:, :, None, :] + sa[..., None] * b_t[:, :, None, :] + v_t[..., None] * k_t[:, :, None, :]
        return s, jnp.einsum('bhvk,bhk->bhv', s, r_t)

    def chunk_step(s, inp_c):
        return lax.scan(step, s, inp_c)

    s0 = jnp.zeros((bsz, nh, n, n), jnp.float32)
    _, ys = lax.scan(chunk_step, s0, tuple(to_chunks(t) for t in (r, w, k, v, a, b)))
    return ys.transpose(2, 0, 1, 3, 4).reshape(bsz, seq, nh, n)


def rwkv7_mixer(z, shift_mu, w0, w2, a0, a2, g2, k_k, k_a, r_k, ln_w, ln_b):
    f32 = jnp.float32
    bsz, seq, _ = z.shape
    z = z.astype(f32)
    prev = jnp.pad(z[:, :-1], ((0, 0), (1, 0), (0, 0)))
    zs = z + (prev - z) * shift_mu.astype(f32)
    rw = RWKV_WIDTH
    splits = [rw, 2 * rw, 3 * rw, 3 * rw + DECAY_LORA, 3 * rw + DECAY_LORA + AAA_LORA]
    r, k, v, wl, al, gl = jnp.split(zs, splits, axis=-1)
    w = -jax.nn.softplus(-(w0.astype(f32) + jnp.tanh(wl) @ w2.astype(f32))) - 0.5
    decay = jnp.exp(-jnp.exp(w))
    a = jax.nn.sigmoid(a0.astype(f32) + al @ a2.astype(f32))
    g = jax.nn.sigmoid(gl) @ g2.astype(f32)

    def heads(t):
        return t.reshape(bsz, seq, RWKV_HEADS, RWKV_HEAD)

    kk = heads(k * k_k.astype(f32))
    kk = kk / jnp.maximum(jnp.sqrt(jnp.sum(kk * kk, axis=-1, keepdims=True)), L2_EPS)
    k = k * (1.0 + (a - 1.0) * k_a.astype(f32))
    r_h, k_h, v_h, a_h = heads(r), heads(k), heads(v), heads(a)
    y = wkv7_scan(r_h, heads(decay), k_h, v_h, -kk, kk * a_h)
    mu = jnp.mean(y, axis=-1, keepdims=True)
    yc = y - mu
    yn = yc * lax.rsqrt(jnp.mean(yc * yc, axis=-1, keepdims=True) + GN_EPS)
    yn = yn.reshape(bsz, seq, rw) * ln_w.astype(f32) + ln_b.astype(f32)
    bonus = jnp.sum(r_h * k_h * r_k.astype(f32), axis=-1, keepdims=True) * v_h
    return (yn + bonus.reshape(bsz, seq, rw)) * g


def _fwd_setup_inputs(seed: int = 0) -> dict:
    key = jax.random.key(seed)
    ks = jax.random.split(key, 40)
    f32 = jnp.float32
    nrm = lambda k, s, sc: jax.random.normal(k, s, f32) * sc
    D, L = DEPTH, D_MODEL
    n_idx = jnp.arange(RWKV_WIDTH, dtype=f32) / (RWKV_WIDTH - 1)
    w0_base = -7.0 + 5.0 * n_idx ** 0.85 + 0.5
    lam_im_base = jnp.pi * jnp.arange(S5_STATE, dtype=f32)
    return {
        "x": nrm(ks[0], (BATCH, SEQ, D_MODEL), 1.0),
        "p": nrm(ks[1], (DEPTH, BATCH, SEQ, PLE_DIM), 1.0),
        "norm_mix": 1.0 + nrm(ks[2], (D, L), 0.02),
        "w_in": nrm(ks[3], (D, L, IN_COLS), L ** -0.5),
        "s5_lam_re": -0.5 + nrm(ks[4], (D, S5_GROUPS, S5_STATE), 0.01),
        "s5_lam_im": lam_im_base + nrm(ks[5], (D, S5_GROUPS, S5_STATE), 0.01),
        "s5_log_step": jax.random.uniform(ks[6], (D, S5_GROUPS), f32, np.log(1e-3), np.log(1e-1)),
        "s5_b_re": nrm(ks[7], (D, S5_GROUPS, S5_STATE, S5_GROUP), (2 * S5_GROUP) ** -0.5),
        "s5_b_im": nrm(ks[8], (D, S5_GROUPS, S5_STATE, S5_GROUP), (2 * S5_GROUP) ** -0.5),
        "s5_c_re": nrm(ks[9], (D, S5_GROUPS, S5_GROUP, S5_STATE), S5_STATE ** -0.5),
        "s5_c_im": nrm(ks[10], (D, S5_GROUPS, S5_GROUP, S5_STATE), S5_STATE ** -0.5),
        "s5_d": nrm(ks[11], (D, S5_WIDTH), 1.0),
        "s5_glu_w": nrm(ks[12], (D, S5_WIDTH, S5_WIDTH), S5_WIDTH ** -0.5),
        "s5_glu_b": nrm(ks[13], (D, S5_WIDTH), 0.01),
        "rw_shift_mu": jax.random.uniform(ks[14], (D, SHIFT_COLS), f32, 0.1, 0.9),
        "rw_w0": w0_base + nrm(ks[15], (D, RWKV_WIDTH), 0.1),
        "rw_w2": nrm(ks[16], (D, DECAY_LORA, RWKV_WIDTH), 0.1),
        "rw_a0": nrm(ks[17], (D, RWKV_WIDTH), 0.1),
        "rw_a2": nrm(ks[18], (D, AAA_LORA, RWKV_WIDTH), 0.1),
        "rw_g2": nrm(ks[19], (D, GATE_LORA, RWKV_WIDTH), GATE_LORA ** -0.5),
        "rw_k_k": 0.85 + nrm(ks[20], (D, RWKV_WIDTH), 0.02),
        "rw_k_a": 1.0 + nrm(ks[21], (D, RWKV_WIDTH), 0.02),
        "rw_r_k": -0.04 + nrm(ks[22], (D, RWKV_HEADS, RWKV_HEAD), 0.02),
        "rw_ln_w": 1.0 + nrm(ks[23], (D, RWKV_WIDTH), 0.02),
        "rw_ln_b": nrm(ks[24], (D, RWKV_WIDTH), 0.01),
        "w_out": nrm(ks[25], (D, MIX_WIDTH, L), MIX_WIDTH ** -0.5),
        "norm_ffn": 1.0 + nrm(ks[26], (D, L), 0.02),
        "ffn_w1": nrm(ks[27], (D, L, FFN_HIDDEN), L ** -0.5),
        "ffn_w3": nrm(ks[28], (D, L, FFN_HIDDEN), L ** -0.5),
        "ffn_w2": nrm(ks[29], (D, FFN_HIDDEN, L), FFN_HIDDEN ** -0.5),
        "norm_ple": 1.0 + nrm(ks[30], (D, L), 0.02),
        "ple_gate_w": nrm(ks[31], (D, L, L), L ** -0.5),
        "ple_up_w": nrm(ks[32], (D, PLE_DIM, L), PLE_DIM ** -0.5),
        "final_norm": 1.0 + nrm(ks[33], (L,), 0.02),
    }


def _fwd_reference(x, p, norm_mix, w_in, s5_lam_re, s5_lam_im, s5_log_step, s5_b_re, s5_b_im,
              s5_c_re, s5_c_im, s5_d, s5_glu_w, s5_glu_b, rw_shift_mu, rw_w0, rw_w2, rw_a0,
              rw_a2, rw_g2, rw_k_k, rw_k_a, rw_r_k, rw_ln_w, rw_ln_b, w_out, norm_ffn,
              ffn_w1, ffn_w3, ffn_w2, norm_ple, ple_gate_w, ple_up_w, final_norm):
    h = x
    for i in range(DEPTH):
        xn = rms_norm(h, norm_mix[i])
        proj = xn @ w_in[i]
        s5_out = s5_mixer(proj[..., :S5_WIDTH], s5_lam_re[i], s5_lam_im[i], s5_log_step[i],
                          s5_b_re[i], s5_b_im[i], s5_c_re[i], s5_c_im[i], s5_d[i],
                          s5_glu_w[i], s5_glu_b[i])
        rw_out = rwkv7_mixer(proj[..., S5_WIDTH:], rw_shift_mu[i], rw_w0[i], rw_w2[i], rw_a0[i],
                             rw_a2[i], rw_g2[i], rw_k_k[i], rw_k_a[i], rw_r_k[i],
                             rw_ln_w[i], rw_ln_b[i])
        mixed = jnp.concatenate([s5_out, rw_out], axis=-1).astype(h.dtype) @ w_out[i]
        h = h + mixed
        hn = rms_norm(h, norm_ffn[i])
        h = h + (jax.nn.silu(hn @ ffn_w1[i]) * (hn @ ffn_w3[i])) @ ffn_w2[i]
        gate = jax.nn.sigmoid(rms_norm(h, norm_ple[i]) @ ple_gate_w[i])
        h = h + gate * (p[i] @ ple_up_w[i])
    return rms_norm(h, final_norm)


import jax as _jax
import jax.numpy as _jnp

TWIN_FORMAT = 'train_step'
FWD_PARAMS = ['x', 'p', 'norm_mix', 'w_in', 's5_lam_re', 's5_lam_im', 's5_log_step', 's5_b_re', 's5_b_im', 's5_c_re', 's5_c_im', 's5_d', 's5_glu_w', 's5_glu_b', 'rw_shift_mu', 'rw_w0', 'rw_w2', 'rw_a0', 'rw_a2', 'rw_g2', 'rw_k_k', 'rw_k_a', 'rw_r_k', 'rw_ln_w', 'rw_ln_b', 'w_out', 'norm_ffn', 'ffn_w1', 'ffn_w3', 'ffn_w2', 'norm_ple', 'ple_gate_w', 'ple_up_w', 'final_norm']
TWIN_WEIGHTS = ['norm_mix', 'w_in', 's5_lam_re', 's5_lam_im', 's5_log_step', 's5_b_re', 's5_b_im', 's5_c_re', 's5_c_im', 's5_d', 's5_glu_w', 's5_glu_b', 'rw_shift_mu', 'rw_w0', 'rw_w2', 'rw_a0', 'rw_a2', 'rw_g2', 'rw_k_k', 'rw_k_a', 'rw_r_k', 'rw_ln_w', 'rw_ln_b', 'w_out', 'norm_ffn', 'ffn_w1', 'ffn_w3', 'ffn_w2', 'norm_ple', 'ple_gate_w', 'ple_up_w', 'final_norm']
TWIN_DIFF_INPUT = 'x'
TWIN_INPUTS = ['x', 'p', 'norm_mix', 'w_in', 's5_lam_re', 's5_lam_im', 's5_log_step', 's5_b_re', 's5_b_im', 's5_c_re', 's5_c_im', 's5_d', 's5_glu_w', 's5_glu_b', 'rw_shift_mu', 'rw_w0', 'rw_w2', 'rw_a0', 'rw_a2', 'rw_g2', 'rw_k_k', 'rw_k_a', 'rw_r_k', 'rw_ln_w', 'rw_ln_b', 'w_out', 'norm_ffn', 'ffn_w1', 'ffn_w3', 'ffn_w2', 'norm_ple', 'ple_gate_w', 'ple_up_w', 'final_norm', 'loss_target', 'm_norm_mix', 'm_w_in', 'm_s5_lam_re', 'm_s5_lam_im', 'm_s5_log_step', 'm_s5_b_re', 'm_s5_b_im', 'm_s5_c_re', 'm_s5_c_im', 'm_s5_d', 'm_s5_glu_w', 'm_s5_glu_b', 'm_rw_shift_mu', 'm_rw_w0', 'm_rw_w2', 'm_rw_a0', 'm_rw_a2', 'm_rw_g2', 'm_rw_k_k', 'm_rw_k_a', 'm_rw_r_k', 'm_rw_ln_w', 'm_rw_ln_b', 'm_w_out', 'm_norm_ffn', 'm_ffn_w1', 'm_ffn_w3', 'm_ffn_w2', 'm_norm_ple', 'm_ple_gate_w', 'm_ple_up_w', 'm_final_norm', 'v_norm_mix', 'v_w_in', 'v_s5_lam_re', 'v_s5_lam_im', 'v_s5_log_step', 'v_s5_b_re', 'v_s5_b_im', 'v_s5_c_re', 'v_s5_c_im', 'v_s5_d', 'v_s5_glu_w', 'v_s5_glu_b', 'v_rw_shift_mu', 'v_rw_w0', 'v_rw_w2', 'v_rw_a0', 'v_rw_a2', 'v_rw_g2', 'v_rw_k_k', 'v_rw_k_a', 'v_rw_r_k', 'v_rw_ln_w', 'v_rw_ln_b', 'v_w_out', 'v_norm_ffn', 'v_ffn_w1', 'v_ffn_w3', 'v_ffn_w2', 'v_norm_ple', 'v_ple_gate_w', 'v_ple_up_w', 'v_final_norm']
TWIN_OUTPUTS = ['loss', 'grad_x', 'grad_norm_mix', 'grad_w_in', 'grad_s5_lam_re', 'grad_s5_lam_im', 'grad_s5_log_step', 'grad_s5_b_re', 'grad_s5_b_im', 'grad_s5_c_re', 'grad_s5_c_im', 'grad_s5_d', 'grad_s5_glu_w', 'grad_s5_glu_b', 'grad_rw_shift_mu', 'grad_rw_w0', 'grad_rw_w2', 'grad_rw_a0', 'grad_rw_a2', 'grad_rw_g2', 'grad_rw_k_k', 'grad_rw_k_a', 'grad_rw_r_k', 'grad_rw_ln_w', 'grad_rw_ln_b', 'grad_w_out', 'grad_norm_ffn', 'grad_ffn_w1', 'grad_ffn_w3', 'grad_ffn_w2', 'grad_norm_ple', 'grad_ple_gate_w', 'grad_ple_up_w', 'grad_final_norm', 'delta_norm_mix', 'delta_w_in', 'delta_s5_lam_re', 'delta_s5_lam_im', 'delta_s5_log_step', 'delta_s5_b_re', 'delta_s5_b_im', 'delta_s5_c_re', 'delta_s5_c_im', 'delta_s5_d', 'delta_s5_glu_w', 'delta_s5_glu_b', 'delta_rw_shift_mu', 'delta_rw_w0', 'delta_rw_w2', 'delta_rw_a0', 'delta_rw_a2', 'delta_rw_g2', 'delta_rw_k_k', 'delta_rw_k_a', 'delta_rw_r_k', 'delta_rw_ln_w', 'delta_rw_ln_b', 'delta_w_out', 'delta_norm_ffn', 'delta_ffn_w1', 'delta_ffn_w3', 'delta_ffn_w2', 'delta_norm_ple', 'delta_ple_gate_w', 'delta_ple_up_w', 'delta_final_norm', 'new_m_norm_mix', 'new_m_w_in', 'new_m_s5_lam_re', 'new_m_s5_lam_im', 'new_m_s5_log_step', 'new_m_s5_b_re', 'new_m_s5_b_im', 'new_m_s5_c_re', 'new_m_s5_c_im', 'new_m_s5_d', 'new_m_s5_glu_w', 'new_m_s5_glu_b', 'new_m_rw_shift_mu', 'new_m_rw_w0', 'new_m_rw_w2', 'new_m_rw_a0', 'new_m_rw_a2', 'new_m_rw_g2', 'new_m_rw_k_k', 'new_m_rw_k_a', 'new_m_rw_r_k', 'new_m_rw_ln_w', 'new_m_rw_ln_b', 'new_m_w_out', 'new_m_norm_ffn', 'new_m_ffn_w1', 'new_m_ffn_w3', 'new_m_ffn_w2', 'new_m_norm_ple', 'new_m_ple_gate_w', 'new_m_ple_up_w', 'new_m_final_norm', 'new_v_norm_mix', 'new_v_w_in', 'new_v_s5_lam_re', 'new_v_s5_lam_im', 'new_v_s5_log_step', 'new_v_s5_b_re', 'new_v_s5_b_im', 'new_v_s5_c_re', 'new_v_s5_c_im', 'new_v_s5_d', 'new_v_s5_glu_w', 'new_v_s5_glu_b', 'new_v_rw_shift_mu', 'new_v_rw_w0', 'new_v_rw_w2', 'new_v_rw_a0', 'new_v_rw_a2', 'new_v_rw_g2', 'new_v_rw_k_k', 'new_v_rw_k_a', 'new_v_rw_r_k', 'new_v_rw_ln_w', 'new_v_rw_ln_b', 'new_v_w_out', 'new_v_norm_ffn', 'new_v_ffn_w1', 'new_v_ffn_w3', 'new_v_ffn_w2', 'new_v_norm_ple', 'new_v_ple_gate_w', 'new_v_ple_up_w', 'new_v_final_norm']
TWIN_LEAF_KINDS = {'loss': 'loss', 'grad_x': 'grad_x', 'grad_norm_mix': 'grad_w', 'grad_w_in': 'grad_w', 'grad_s5_lam_re': 'grad_w', 'grad_s5_lam_im': 'grad_w', 'grad_s5_log_step': 'grad_w', 'grad_s5_b_re': 'grad_w', 'grad_s5_b_im': 'grad_w', 'grad_s5_c_re': 'grad_w', 'grad_s5_c_im': 'grad_w', 'grad_s5_d': 'grad_w', 'grad_s5_glu_w': 'grad_w', 'grad_s5_glu_b': 'grad_w', 'grad_rw_shift_mu': 'grad_w', 'grad_rw_w0': 'grad_w', 'grad_rw_w2': 'grad_w', 'grad_rw_a0': 'grad_w', 'grad_rw_a2': 'grad_w', 'grad_rw_g2': 'grad_w', 'grad_rw_k_k': 'grad_w', 'grad_rw_k_a': 'grad_w', 'grad_rw_r_k': 'grad_w', 'grad_rw_ln_w': 'grad_w', 'grad_rw_ln_b': 'grad_w', 'grad_w_out': 'grad_w', 'grad_norm_ffn': 'grad_w', 'grad_ffn_w1': 'grad_w', 'grad_ffn_w3': 'grad_w', 'grad_ffn_w2': 'grad_w', 'grad_norm_ple': 'grad_w', 'grad_ple_gate_w': 'grad_w', 'grad_ple_up_w': 'grad_w', 'grad_final_norm': 'grad_w', 'delta_norm_mix': 'delta_w', 'delta_w_in': 'delta_w', 'delta_s5_lam_re': 'delta_w', 'delta_s5_lam_im': 'delta_w', 'delta_s5_log_step': 'delta_w', 'delta_s5_b_re': 'delta_w', 'delta_s5_b_im': 'delta_w', 'delta_s5_c_re': 'delta_w', 'delta_s5_c_im': 'delta_w', 'delta_s5_d': 'delta_w', 'delta_s5_glu_w': 'delta_w', 'delta_s5_glu_b': 'delta_w', 'delta_rw_shift_mu': 'delta_w', 'delta_rw_w0': 'delta_w', 'delta_rw_w2': 'delta_w', 'delta_rw_a0': 'delta_w', 'delta_rw_a2': 'delta_w', 'delta_rw_g2': 'delta_w', 'delta_rw_k_k': 'delta_w', 'delta_rw_k_a': 'delta_w', 'delta_rw_r_k': 'delta_w', 'delta_rw_ln_w': 'delta_w', 'delta_rw_ln_b': 'delta_w', 'delta_w_out': 'delta_w', 'delta_norm_ffn': 'delta_w', 'delta_ffn_w1': 'delta_w', 'delta_ffn_w3': 'delta_w', 'delta_ffn_w2': 'delta_w', 'delta_norm_ple': 'delta_w', 'delta_ple_gate_w': 'delta_w', 'delta_ple_up_w': 'delta_w', 'delta_final_norm': 'delta_w', 'new_m_norm_mix': 'new_m', 'new_m_w_in': 'new_m', 'new_m_s5_lam_re': 'new_m', 'new_m_s5_lam_im': 'new_m', 'new_m_s5_log_step': 'new_m', 'new_m_s5_b_re': 'new_m', 'new_m_s5_b_im': 'new_m', 'new_m_s5_c_re': 'new_m', 'new_m_s5_c_im': 'new_m', 'new_m_s5_d': 'new_m', 'new_m_s5_glu_w': 'new_m', 'new_m_s5_glu_b': 'new_m', 'new_m_rw_shift_mu': 'new_m', 'new_m_rw_w0': 'new_m', 'new_m_rw_w2': 'new_m', 'new_m_rw_a0': 'new_m', 'new_m_rw_a2': 'new_m', 'new_m_rw_g2': 'new_m', 'new_m_rw_k_k': 'new_m', 'new_m_rw_k_a': 'new_m', 'new_m_rw_r_k': 'new_m', 'new_m_rw_ln_w': 'new_m', 'new_m_rw_ln_b': 'new_m', 'new_m_w_out': 'new_m', 'new_m_norm_ffn': 'new_m', 'new_m_ffn_w1': 'new_m', 'new_m_ffn_w3': 'new_m', 'new_m_ffn_w2': 'new_m', 'new_m_norm_ple': 'new_m', 'new_m_ple_gate_w': 'new_m', 'new_m_ple_up_w': 'new_m', 'new_m_final_norm': 'new_m', 'new_v_norm_mix': 'new_v', 'new_v_w_in': 'new_v', 'new_v_s5_lam_re': 'new_v', 'new_v_s5_lam_im': 'new_v', 'new_v_s5_log_step': 'new_v', 'new_v_s5_b_re': 'new_v', 'new_v_s5_b_im': 'new_v', 'new_v_s5_c_re': 'new_v', 'new_v_s5_c_im': 'new_v', 'new_v_s5_d': 'new_v', 'new_v_s5_glu_w': 'new_v', 'new_v_s5_glu_b': 'new_v', 'new_v_rw_shift_mu': 'new_v', 'new_v_rw_w0': 'new_v', 'new_v_rw_w2': 'new_v', 'new_v_rw_a0': 'new_v', 'new_v_rw_a2': 'new_v', 'new_v_rw_g2': 'new_v', 'new_v_rw_k_k': 'new_v', 'new_v_rw_k_a': 'new_v', 'new_v_rw_r_k': 'new_v', 'new_v_rw_ln_w': 'new_v', 'new_v_rw_ln_b': 'new_v', 'new_v_w_out': 'new_v', 'new_v_norm_ffn': 'new_v', 'new_v_ffn_w1': 'new_v', 'new_v_ffn_w3': 'new_v', 'new_v_ffn_w2': 'new_v', 'new_v_norm_ple': 'new_v', 'new_v_ple_gate_w': 'new_v', 'new_v_ple_up_w': 'new_v', 'new_v_final_norm': 'new_v'}


def _forward(args):
    return _fwd_reference(*[args[k] for k in FWD_PARAMS])


def _output_shape():
    def fwd():
        inp = _fwd_setup_inputs(0)
        return _fwd_reference(*[inp[k] for k in FWD_PARAMS])
    out = _jax.eval_shape(fwd)
    return out.shape, out.dtype

N_MICROBATCH = 1
ADAM_LR = 0.001
ADAM_B1 = 0.9
ADAM_B2 = 0.999
ADAM_EPS = 1e-08
ADAM_WD = 0.01
ADAM_STEP = 10
PER_EXAMPLE_BATCH_AXIS = {'x': 0, 'p': 1, 'loss_target': 0}
SHARED_INPUTS = []
_WEIGHT_DTYPES = {'norm_mix': _jnp.float32, 'w_in': _jnp.float32, 's5_lam_re': _jnp.float32, 's5_lam_im': _jnp.float32, 's5_log_step': _jnp.float32, 's5_b_re': _jnp.float32, 's5_b_im': _jnp.float32, 's5_c_re': _jnp.float32, 's5_c_im': _jnp.float32, 's5_d': _jnp.float32, 's5_glu_w': _jnp.float32, 's5_glu_b': _jnp.float32, 'rw_shift_mu': _jnp.float32, 'rw_w0': _jnp.float32, 'rw_w2': _jnp.float32, 'rw_a0': _jnp.float32, 'rw_a2': _jnp.float32, 'rw_g2': _jnp.float32, 'rw_k_k': _jnp.float32, 'rw_k_a': _jnp.float32, 'rw_r_k': _jnp.float32, 'rw_ln_w': _jnp.float32, 'rw_ln_b': _jnp.float32, 'w_out': _jnp.float32, 'norm_ffn': _jnp.float32, 'ffn_w1': _jnp.float32, 'ffn_w3': _jnp.float32, 'ffn_w2': _jnp.float32, 'norm_ple': _jnp.float32, 'ple_gate_w': _jnp.float32, 'ple_up_w': _jnp.float32, 'final_norm': _jnp.float32}
MOMENT_SCALE = {'norm_mix': 1.269039e-01, 'w_in': 8.335879e-02, 's5_lam_re': 4.444665e-03, 's5_lam_im': 4.839484e-03, 's5_log_step': 3.504949e+00, 's5_b_re': 2.971124e-03, 's5_b_im': 2.986339e-03, 's5_c_re': 4.128574e-03, 's5_c_im': 4.169783e-03, 's5_d': 6.968527e-02, 's5_glu_w': 1.824750e-02, 's5_glu_b': 2.764561e-02, 'rw_shift_mu': 1.483781e-01, 'rw_w0': 4.500601e-02, 'rw_w2': 6.348312e-03, 'rw_a0': 3.749735e-02, 'rw_a2': 3.122521e-02, 'rw_g2': 8.992165e-02, 'rw_k_k': 8.465353e-02, 'rw_k_a': 1.005463e-01, 'rw_r_k': 1.698697e-01, 'rw_ln_w': 9.568730e-02, 'rw_ln_b': 8.988082e-02, 'w_out': 7.485095e-02, 'norm_ffn': 1.134903e-01, 'ffn_w1': 4.916741e-02, 'ffn_w3': 4.748856e-02, 'ffn_w2': 7.888661e-02, 'norm_ple': 2.906914e-02, 'ple_gate_w': 2.791297e-02, 'ple_up_w': 7.096902e-02, 'final_norm': 3.201700e+01}


def _to_microbatches(a, axis):
    t = _jnp.moveaxis(a, axis, 0)
    t = t.reshape((N_MICROBATCH, t.shape[0] // N_MICROBATCH) + t.shape[1:])
    return _jnp.moveaxis(t, 1, axis + 1)


def setup_inputs(seed: int = 0) -> dict:
    inp = _fwd_setup_inputs(seed)
    key = _jax.random.fold_in(_jax.random.key(seed), 7919)
    shape, _ = _output_shape()
    out = dict(inp)
    out["loss_target"] = _jax.random.normal(_jax.random.fold_in(key, 0), shape, _jnp.float32)
    for i, name in enumerate(TWIN_WEIGHTS):
        w = inp[name].astype(_jnp.float32)
        if MOMENT_SCALE is None:
            s = _jnp.sqrt(_jnp.mean(_jnp.square(w)) + 1e-30)
        else:
            s = MOMENT_SCALE[name]
        km, kv = _jax.random.split(_jax.random.fold_in(key, i + 1))
        out[name] = w
        out["m_" + name] = s * _jax.random.normal(km, w.shape, _jnp.float32)
        out["v_" + name] = (s * s) * _jax.random.uniform(kv, w.shape, _jnp.float32, 0.5, 1.5)
    if N_MICROBATCH > 1:
        for name, axis in PER_EXAMPLE_BATCH_AXIS.items():
            out[name] = _to_microbatches(out[name], axis)
    return {'x': out['x'], 'p': out['p'], 'norm_mix': out['norm_mix'], 'w_in': out['w_in'], 's5_lam_re': out['s5_lam_re'], 's5_lam_im': out['s5_lam_im'], 's5_log_step': out['s5_log_step'], 's5_b_re': out['s5_b_re'], 's5_b_im': out['s5_b_im'], 's5_c_re': out['s5_c_re'], 's5_c_im': out['s5_c_im'], 's5_d': out['s5_d'], 's5_glu_w': out['s5_glu_w'], 's5_glu_b': out['s5_glu_b'], 'rw_shift_mu': out['rw_shift_mu'], 'rw_w0': out['rw_w0'], 'rw_w2': out['rw_w2'], 'rw_a0': out['rw_a0'], 'rw_a2': out['rw_a2'], 'rw_g2': out['rw_g2'], 'rw_k_k': out['rw_k_k'], 'rw_k_a': out['rw_k_a'], 'rw_r_k': out['rw_r_k'], 'rw_ln_w': out['rw_ln_w'], 'rw_ln_b': out['rw_ln_b'], 'w_out': out['w_out'], 'norm_ffn': out['norm_ffn'], 'ffn_w1': out['ffn_w1'], 'ffn_w3': out['ffn_w3'], 'ffn_w2': out['ffn_w2'], 'norm_ple': out['norm_ple'], 'ple_gate_w': out['ple_gate_w'], 'ple_up_w': out['ple_up_w'], 'final_norm': out['final_norm'], 'loss_target': out['loss_target'], 'm_norm_mix': out['m_norm_mix'], 'm_w_in': out['m_w_in'], 'm_s5_lam_re': out['m_s5_lam_re'], 'm_s5_lam_im': out['m_s5_lam_im'], 'm_s5_log_step': out['m_s5_log_step'], 'm_s5_b_re': out['m_s5_b_re'], 'm_s5_b_im': out['m_s5_b_im'], 'm_s5_c_re': out['m_s5_c_re'], 'm_s5_c_im': out['m_s5_c_im'], 'm_s5_d': out['m_s5_d'], 'm_s5_glu_w': out['m_s5_glu_w'], 'm_s5_glu_b': out['m_s5_glu_b'], 'm_rw_shift_mu': out['m_rw_shift_mu'], 'm_rw_w0': out['m_rw_w0'], 'm_rw_w2': out['m_rw_w2'], 'm_rw_a0': out['m_rw_a0'], 'm_rw_a2': out['m_rw_a2'], 'm_rw_g2': out['m_rw_g2'], 'm_rw_k_k': out['m_rw_k_k'], 'm_rw_k_a': out['m_rw_k_a'], 'm_rw_r_k': out['m_rw_r_k'], 'm_rw_ln_w': out['m_rw_ln_w'], 'm_rw_ln_b': out['m_rw_ln_b'], 'm_w_out': out['m_w_out'], 'm_norm_ffn': out['m_norm_ffn'], 'm_ffn_w1': out['m_ffn_w1'], 'm_ffn_w3': out['m_ffn_w3'], 'm_ffn_w2': out['m_ffn_w2'], 'm_norm_ple': out['m_norm_ple'], 'm_ple_gate_w': out['m_ple_gate_w'], 'm_ple_up_w': out['m_ple_up_w'], 'm_final_norm': out['m_final_norm'], 'v_norm_mix': out['v_norm_mix'], 'v_w_in': out['v_w_in'], 'v_s5_lam_re': out['v_s5_lam_re'], 'v_s5_lam_im': out['v_s5_lam_im'], 'v_s5_log_step': out['v_s5_log_step'], 'v_s5_b_re': out['v_s5_b_re'], 'v_s5_b_im': out['v_s5_b_im'], 'v_s5_c_re': out['v_s5_c_re'], 'v_s5_c_im': out['v_s5_c_im'], 'v_s5_d': out['v_s5_d'], 'v_s5_glu_w': out['v_s5_glu_w'], 'v_s5_glu_b': out['v_s5_glu_b'], 'v_rw_shift_mu': out['v_rw_shift_mu'], 'v_rw_w0': out['v_rw_w0'], 'v_rw_w2': out['v_rw_w2'], 'v_rw_a0': out['v_rw_a0'], 'v_rw_a2': out['v_rw_a2'], 'v_rw_g2': out['v_rw_g2'], 'v_rw_k_k': out['v_rw_k_k'], 'v_rw_k_a': out['v_rw_k_a'], 'v_rw_r_k': out['v_rw_r_k'], 'v_rw_ln_w': out['v_rw_ln_w'], 'v_rw_ln_b': out['v_rw_ln_b'], 'v_w_out': out['v_w_out'], 'v_norm_ffn': out['v_norm_ffn'], 'v_ffn_w1': out['v_ffn_w1'], 'v_ffn_w3': out['v_ffn_w3'], 'v_ffn_w2': out['v_ffn_w2'], 'v_norm_ple': out['v_norm_ple'], 'v_ple_gate_w': out['v_ple_gate_w'], 'v_ple_up_w': out['v_ple_up_w'], 'v_final_norm': out['v_final_norm']}


def _loss(weights, diff, rest, loss_target):
    with _jax.named_scope("forward"):
        args = {**rest, TWIN_DIFF_INPUT: diff, **{k: w.astype(_WEIGHT_DTYPES[k]) for k, w in weights.items()}}
        y = _forward(args)
    with _jax.named_scope("loss_head"):
        err = _jnp.square(y.astype(_jnp.float32) - loss_target)
        return 0.5 * _jnp.sum(_jnp.mean(err, axis=-1)) if err.ndim else 0.5 * err


def _adamw(w, g, m, v):
    m = ADAM_B1 * m + (1.0 - ADAM_B1) * g
    v = ADAM_B2 * v + (1.0 - ADAM_B2) * _jnp.square(g)
    m_hat = m / (1.0 - ADAM_B1 ** ADAM_STEP)
    v_hat = v / (1.0 - ADAM_B2 ** ADAM_STEP)
    delta = -ADAM_LR * (m_hat / (_jnp.sqrt(v_hat) + ADAM_EPS) + ADAM_WD * w)
    return delta, m, v


def reference(x, p, norm_mix, w_in, s5_lam_re, s5_lam_im, s5_log_step, s5_b_re, s5_b_im, s5_c_re, s5_c_im, s5_d, s5_glu_w, s5_glu_b, rw_shift_mu, rw_w0, rw_w2, rw_a0, rw_a2, rw_g2, rw_k_k, rw_k_a, rw_r_k, rw_ln_w, rw_ln_b, w_out, norm_ffn, ffn_w1, ffn_w3, ffn_w2, norm_ple, ple_gate_w, ple_up_w, final_norm, loss_target, m_norm_mix, m_w_in, m_s5_lam_re, m_s5_lam_im, m_s5_log_step, m_s5_b_re, m_s5_b_im, m_s5_c_re, m_s5_c_im, m_s5_d, m_s5_glu_w, m_s5_glu_b, m_rw_shift_mu, m_rw_w0, m_rw_w2, m_rw_a0, m_rw_a2, m_rw_g2, m_rw_k_k, m_rw_k_a, m_rw_r_k, m_rw_ln_w, m_rw_ln_b, m_w_out, m_norm_ffn, m_ffn_w1, m_ffn_w3, m_ffn_w2, m_norm_ple, m_ple_gate_w, m_ple_up_w, m_final_norm, v_norm_mix, v_w_in, v_s5_lam_re, v_s5_lam_im, v_s5_log_step, v_s5_b_re, v_s5_b_im, v_s5_c_re, v_s5_c_im, v_s5_d, v_s5_glu_w, v_s5_glu_b, v_rw_shift_mu, v_rw_w0, v_rw_w2, v_rw_a0, v_rw_a2, v_rw_g2, v_rw_k_k, v_rw_k_a, v_rw_r_k, v_rw_ln_w, v_rw_ln_b, v_w_out, v_norm_ffn, v_ffn_w1, v_ffn_w3, v_ffn_w2, v_norm_ple, v_ple_gate_w, v_ple_up_w, v_final_norm):
    given = dict(x=x, p=p, norm_mix=norm_mix, w_in=w_in, s5_lam_re=s5_lam_re, s5_lam_im=s5_lam_im, s5_log_step=s5_log_step, s5_b_re=s5_b_re, s5_b_im=s5_b_im, s5_c_re=s5_c_re, s5_c_im=s5_c_im, s5_d=s5_d, s5_glu_w=s5_glu_w, s5_glu_b=s5_glu_b, rw_shift_mu=rw_shift_mu, rw_w0=rw_w0, rw_w2=rw_w2, rw_a0=rw_a0, rw_a2=rw_a2, rw_g2=rw_g2, rw_k_k=rw_k_k, rw_k_a=rw_k_a, rw_r_k=rw_r_k, rw_ln_w=rw_ln_w, rw_ln_b=rw_ln_b, w_out=w_out, norm_ffn=norm_ffn, ffn_w1=ffn_w1, ffn_w3=ffn_w3, ffn_w2=ffn_w2, norm_ple=norm_ple, ple_gate_w=ple_gate_w, ple_up_w=ple_up_w, final_norm=final_norm, loss_target=loss_target, m_norm_mix=m_norm_mix, m_w_in=m_w_in, m_s5_lam_re=m_s5_lam_re, m_s5_lam_im=m_s5_lam_im, m_s5_log_step=m_s5_log_step, m_s5_b_re=m_s5_b_re, m_s5_b_im=m_s5_b_im, m_s5_c_re=m_s5_c_re, m_s5_c_im=m_s5_c_im, m_s5_d=m_s5_d, m_s5_glu_w=m_s5_glu_w, m_s5_glu_b=m_s5_glu_b, m_rw_shift_mu=m_rw_shift_mu, m_rw_w0=m_rw_w0, m_rw_w2=m_rw_w2, m_rw_a0=m_rw_a0, m_rw_a2=m_rw_a2, m_rw_g2=m_rw_g2, m_rw_k_k=m_rw_k_k, m_rw_k_a=m_rw_k_a, m_rw_r_k=m_rw_r_k, m_rw_ln_w=m_rw_ln_w, m_rw_ln_b=m_rw_ln_b, m_w_out=m_w_out, m_norm_ffn=m_norm_ffn, m_ffn_w1=m_ffn_w1, m_ffn_w3=m_ffn_w3, m_ffn_w2=m_ffn_w2, m_norm_ple=m_norm_ple, m_ple_gate_w=m_ple_gate_w, m_ple_up_w=m_ple_up_w, m_final_norm=m_final_norm, v_norm_mix=v_norm_mix, v_w_in=v_w_in, v_s5_lam_re=v_s5_lam_re, v_s5_lam_im=v_s5_lam_im, v_s5_log_step=v_s5_log_step, v_s5_b_re=v_s5_b_re, v_s5_b_im=v_s5_b_im, v_s5_c_re=v_s5_c_re, v_s5_c_im=v_s5_c_im, v_s5_d=v_s5_d, v_s5_glu_w=v_s5_glu_w, v_s5_glu_b=v_s5_glu_b, v_rw_shift_mu=v_rw_shift_mu, v_rw_w0=v_rw_w0, v_rw_w2=v_rw_w2, v_rw_a0=v_rw_a0, v_rw_a2=v_rw_a2, v_rw_g2=v_rw_g2, v_rw_k_k=v_rw_k_k, v_rw_k_a=v_rw_k_a, v_rw_r_k=v_rw_r_k, v_rw_ln_w=v_rw_ln_w, v_rw_ln_b=v_rw_ln_b, v_w_out=v_w_out, v_norm_ffn=v_norm_ffn, v_ffn_w1=v_ffn_w1, v_ffn_w3=v_ffn_w3, v_ffn_w2=v_ffn_w2, v_norm_ple=v_norm_ple, v_ple_gate_w=v_ple_gate_w, v_ple_up_w=v_ple_up_w, v_final_norm=v_final_norm)
    weights = {n: given[n] for n in TWIN_WEIGHTS}
    shared = {n: given[n] for n in SHARED_INPUTS}
    per_example = {n: given[n] for n in ['x', 'p']}
    grad_fn = _jax.value_and_grad(_loss, argnums=(0, 1))

    def one_microbatch(ex, loss_target):
        ex = dict(ex)
        diff = ex.pop(TWIN_DIFF_INPUT)
        return grad_fn(weights, diff, {**shared, **ex}, loss_target)

    if N_MICROBATCH == 1:
        loss, (grad_w, grad_x) = one_microbatch(per_example, given["loss_target"])
    else:
        def body(carry, xs):
            loss_sum, grad_sum = carry
            l_k, (gw_k, gx_k) = one_microbatch(xs[0], xs[1])
            with _jax.named_scope("update"):
                return (loss_sum + l_k, _jax.tree.map(_jnp.add, grad_sum, gw_k)), gx_k

        init = (_jnp.zeros((), _jnp.float32), _jax.tree.map(_jnp.zeros_like, weights))
        (loss, grad_w), grad_x = _jax.lax.scan(body, init, (per_example, given["loss_target"]))
    with _jax.named_scope("update"):
        delta_w, new_m, new_v = {}, {}, {}
        for n in TWIN_WEIGHTS:
            delta_w[n], new_m[n], new_v[n] = _adamw(weights[n], grad_w[n], given["m_" + n], given["v_" + n])
    return (loss, grad_x, *[grad_w[n] for n in TWIN_WEIGHTS], *[delta_w[n] for n in TWIN_WEIGHTS],
            *[new_m[n] for n in TWIN_WEIGHTS], *[new_v[n] for n in TWIN_WEIGHTS])
```

```python
import functools

import jax
import jax.numpy as jnp
from jax import lax
from jax.experimental import pallas as pl
from jax.experimental.pallas import tpu as pltpu

F32 = jnp.float32
BF16 = jnp.bfloat16

D_MODEL = 1024
S5_WIDTH = 512
RW_WIDTH = 512
S5_GROUP = 16
S5_GROUPS = 32
S5_STATE = 64
S5_LANES = S5_GROUPS * S5_STATE
HEAD = 64
SHIFT_COLS = 1792
IN_COLS = 2304
FFN_HIDDEN = 2816
PLE_DIM = 256
RMS_EPS = 1e-6
GN_EPS = 64e-5
L2_EPS = 1e-12
CHUNK = 64
N_DEV = 8

ADAM_LR = 0.001
ADAM_B1 = 0.9
ADAM_B2 = 0.999
ADAM_EPS = 1e-08
ADAM_WD = 0.01
ADAM_STEP = 10

VMEM_LIMIT = 56 * 1024 * 1024


def _pcall(body, **kw):
    return pl.pallas_call(body, **kw)


def _cparams(n_grid):
    return pltpu.CompilerParams(dimension_semantics=("arbitrary",) * n_grid, vmem_limit_bytes=VMEM_LIMIT)


def _dot(a, b):
    return jnp.dot(a, b, preferred_element_type=F32)


def _dot_nt(a, b):
    return lax.dot_general(a, b, (((1,), (1,)), ((), ())), preferred_element_type=F32)


def _dot_tn(a, b):
    return lax.dot_general(a, b, (((0,), (0,)), ((), ())), preferred_element_type=F32)


def _mmc(w, diff=True):
    if not diff:
        return lambda x: _dot(x.astype(BF16), w)

    @jax.custom_vjp
    def f(x):
        return _dot(x.astype(BF16), w)

    def fwd(x):
        return _dot(x.astype(BF16), w), None

    def bwd(_, dy):
        return (_dot_nt(dy.astype(BF16), w),)

    f.defvjp(fwd, bwd)
    return f


def _split_dot(x, m, n_split):
    acc = None
    rem = x
    for s in range(n_split):
        part = rem.astype(BF16)
        t = _dot(part, m)
        acc = t if acc is None else acc + t
        if s + 1 < n_split:
            rem = rem - part.astype(F32)
    return acc


def _segsum(m, diff=True):
    if not diff:
        return lambda x: _split_dot(x, m, 2)

    @jax.custom_vjp
    def f(x):
        return _split_dot(x, m, 2)

    def fwd(x):
        return _split_dot(x, m, 2), None

    def bwd(_, dy):
        return (_split_dot(dy, m, 2),)

    f.defvjp(fwd, bwd)
    return f


def _head_indicator(n):
    r = lax.broadcasted_iota(jnp.int32, (n, n), 0) // HEAD
    c = lax.broadcasted_iota(jnp.int32, (n, n), 1) // HEAD
    return (r == c).astype(BF16)


def _rms(x, g):
    return x * lax.rsqrt(jnp.mean(x * x, axis=-1, keepdims=True) + RMS_EPS) * g


def _softplus(x):
    return jnp.maximum(x, 0.0) + jnp.log(1.0 + jnp.exp(-jnp.abs(x)))


def _sigmoid(x):
    return 1.0 / (1.0 + jnp.exp(-x))


def _gelu(x):
    return 0.5 * x * (1.0 + jnp.tanh(0.7978845608028654 * (x + 0.044715 * (x * x * x))))


def _tok_call(name, fn, L, TB, tok_in, const_in, tok_out, acc_out=()):
    nb = L // TB
    g8 = TB // 8
    in_specs, args = [], []
    for spec in tok_in:
        arr, width, cb = spec[:3]
        mode = spec[3] if len(spec) > 3 else None
        if mode is None:
            in_specs.append(pl.BlockSpec((TB, width), lambda i, cb=cb: (i, cb)))
        elif mode == "prev":
            in_specs.append(pl.BlockSpec((8, width), lambda i, cb=cb: (jnp.maximum(i * g8 - 1, 0), cb)))
        else:
            in_specs.append(pl.BlockSpec((8, width), lambda i, cb=cb: (jnp.minimum((i + 1) * g8, L // 8 - 1), cb)))
        args.append(arr)
    for c in const_in:
        in_specs.append(pl.BlockSpec(c.shape, lambda i, nd=c.ndim: (0,) * nd))
        args.append(c)
    out_shape, out_specs = [], []
    for width, dt in tok_out:
        out_shape.append(jax.ShapeDtypeStruct((L, width), dt))
        out_specs.append(pl.BlockSpec((TB, width), lambda i: (i, 0)))
    for shp in acc_out:
        out_shape.append(jax.ShapeDtypeStruct(shp, F32))
        out_specs.append(pl.BlockSpec(shp, lambda i, nd=len(shp): (0,) * nd))
    n_tok, n_const, n_to = len(tok_in), len(const_in), len(tok_out)

    def body(*refs):
        i = pl.program_id(0)
        tv = [r[...] for r in refs[:n_tok]]
        cv = [r[...] for r in refs[n_tok:n_tok + n_const]]
        orefs = refs[n_tok + n_const:]
        outs = fn(i, tv, cv)
        for r, v in zip(orefs[:n_to], outs[:n_to]):
            r[...] = v.astype(r.dtype)
        for r, v in zip(orefs[n_to:], outs[n_to:]):
            @pl.when(i == 0)
            def _(r=r):
                r[...] = jnp.zeros(r.shape, r.dtype)

            r[...] += v

    res = _pcall(body, name=name, grid=(nb,), in_specs=in_specs, out_specs=out_specs, out_shape=out_shape,
                 compiler_params=_cparams(1))(*args)
    return res


def _pick_block(n, cap):
    best = None
    for b in range(128, min(n, cap) + 1, 128):
        if n % b == 0:
            best = b
    return best if best is not None else n


def _mm_tn(name, a, b):
    T, M = a.shape
    N = b.shape[1]
    bm, bn, bt = _pick_block(M, 1024), _pick_block(N, 1536), _pick_block(T, 512)

    def body(a_ref, b_ref, o_ref):
        t = pl.program_id(2)

        @pl.when(t == 0)
        def _():
            o_ref[...] = jnp.zeros(o_ref.shape, F32)

        o_ref[...] += _dot_tn(a_ref[...].astype(BF16), b_ref[...].astype(BF16))

    return _pcall(body, name=name, grid=(M // bm, N // bn, T // bt),
                  in_specs=[pl.BlockSpec((bt, bm), lambda m, n, t: (t, m)), pl.BlockSpec((bt, bn), lambda m, n, t: (t, n))],
                  out_specs=pl.BlockSpec((bm, bn), lambda m, n, t: (m, n)),
                  out_shape=jax.ShapeDtypeStruct((M, N), F32), compiler_params=_cparams(3))(a, b)


def _s5_param_fn(lam_re, lam_im, log_step, bt_re, bt_im):
    dt = jnp.exp(log_step)
    e = jnp.exp(lam_re * dt)
    lb_re = e * jnp.cos(lam_im * dt)
    lb_im = e * jnp.sin(lam_im * dt)
    den = lam_re * lam_re + lam_im * lam_im
    nr, ni = lb_re - 1.0, lb_im
    co_re = (nr * lam_re + ni * lam_im) / den
    co_im = (ni * lam_re - nr * lam_im) / den
    cr, ci = co_re[:, None, :], co_im[:, None, :]
    return lb_re, lb_im, cr * bt_re - ci * bt_im, cr * bt_im + ci * bt_re


def _s5_param_fwd(lam_re, lam_im, log_step, bt_re, bt_im):
    def body(a, b, c, d, e, o1, o2, o3, o4):
        r = _s5_param_fn(a[...], b[...], c[...], d[...], e[...])
        o1[...], o2[...], o3[...], o4[...] = r

    sh = jax.ShapeDtypeStruct
    return _pcall(body, name="s5_param_fwd",
                  out_shape=[sh(lam_re.shape, F32), sh(lam_re.shape, F32), sh(bt_re.shape, F32), sh(bt_re.shape, F32)])(
        lam_re, lam_im, log_step, bt_re, bt_im)


def _s5_param_bwd(lam_re, lam_im, log_step, bt_re, bt_im, d_lb_re, d_lb_im, d_bb_re, d_bb_im):
    def body(a, b, c, d, e, g1, g2, g3, g4, o1, o2, o3, o4, o5):
        _, vjp = jax.vjp(_s5_param_fn, a[...], b[...], c[...], d[...], e[...])
        r = vjp((g1[...], g2[...], g3[...], g4[...]))
        o1[...], o2[...], o3[...], o4[...], o5[...] = r

    sh = jax.ShapeDtypeStruct
    return _pcall(body, name="s5_param_bwd",
                  out_shape=[sh(lam_re.shape, F32), sh(lam_re.shape, F32), sh(log_step.shape, F32),
                             sh(bt_re.shape, F32), sh(bt_re.shape, F32)])(
        lam_re, lam_im, log_step, bt_re, bt_im, d_lb_re, d_lb_im, d_bb_re, d_bb_im)


def _cmul(ar, ai, br, bi):
    return ar * br - ai * bi, ar * bi + ai * br


def _scan_consts(lr, li, reverse):
    n = lr.shape[1]
    sub = lax.broadcasted_iota(jnp.int32, (8, n), 0)
    pows = [(lr, li)]
    for _ in range(7):
        pows.append(_cmul(pows[-1][0], pows[-1][1], lr, li))
    steps = []
    for s in (1, 2, 4):
        m = (sub < 8 - s) if reverse else (sub >= s)
        pr, pi = pows[s - 1]
        steps.append((s, jnp.where(m, jnp.broadcast_to(pr, (8, n)), 0.0), jnp.where(m, jnp.broadcast_to(pi, (8, n)), 0.0)))
    wr = jnp.zeros((8, n), F32)
    wi = jnp.zeros((8, n), F32)
    for r in range(8):
        e = (8 - r) if reverse else (r + 1)
        wr = jnp.where(sub == r, jnp.broadcast_to(pows[e - 1][0], (8, n)), wr)
        wi = jnp.where(sub == r, jnp.broadcast_to(pows[e - 1][1], (8, n)), wi)
    return steps, wr, wi


def _scan_rows(sre, sim, carry, lr, li, rows, reverse):
    steps, wr, wi = _scan_consts(lr, li, reverse)
    ng = rows // 8

    def step(gi, _):
        g = (ng - 1 - gi) if reverse else gi
        base = pl.multiple_of(g * 8, 8)
        xr = sre[pl.ds(base, 8), :]
        xi = sim[pl.ds(base, 8), :]
        for s, pr, pi in steps:
            sh = (8 - s) if reverse else s
            yr = pltpu.roll(xr, sh, 0)
            yi = pltpu.roll(xi, sh, 0)
            xr, xi = xr + pr * yr - pi * yi, xi + pr * yi + pi * yr
        cr = carry[0:1, :]
        ci = carry[1:2, :]
        xr, xi = xr + wr * cr - wi * ci, xi + wr * ci + wi * cr
        sre[pl.ds(base, 8), :] = xr
        sim[pl.ds(base, 8), :] = xi
        edge = 0 if reverse else 7
        carry[0:1, :] = xr[edge:edge + 1, :]
        carry[1:2, :] = xi[edge:edge + 1, :]
        return 0

    lax.fori_loop(0, ng, step, 0)


S5_Q = 4
S5_QL = S5_WIDTH // S5_Q
S5_QS = S5_LANES // S5_Q


def _s5_scan_fwd(proj, bq_re, bq_im, cq_re, cq_im, lbar, dskip, L, TB):
    nb = L // TB

    def body(u_ref, bre, bim, cre, cim, lb_ref, d_ref, y_ref, ck_ref, sre, sim, carry):
        i = pl.program_id(0)

        @pl.when(i == 0)
        def _():
            carry[...] = jnp.zeros(carry.shape, F32)

        ck_ref[0] = carry[...]
        u = u_ref[...]
        ub = u.astype(BF16)
        for q in range(S5_Q):
            uq = ub[:, q * S5_QL:(q + 1) * S5_QL]
            sre[:, q * S5_QS:(q + 1) * S5_QS] = _dot(uq, bre[q])
            sim[:, q * S5_QS:(q + 1) * S5_QS] = _dot(uq, bim[q])
        _scan_rows(sre, sim, carry, lb_ref[0:1, :], lb_ref[1:2, :], TB, False)
        for q in range(S5_Q):
            sl = slice(q * S5_QL, (q + 1) * S5_QL)
            ss = slice(q * S5_QS, (q + 1) * S5_QS)
            y_ref[:, sl] = (_dot(sre[:, ss].astype(BF16), cre[q]) - _dot(sim[:, ss].astype(BF16), cim[q])
                            + u[:, sl] * d_ref[:, sl])

    full = lambda a: pl.BlockSpec(a.shape, lambda i, nd=a.ndim: (0,) * nd)
    return _pcall(
        body, name="s5_scan_fwd", grid=(nb,),
        in_specs=[pl.BlockSpec((TB, S5_WIDTH), lambda i: (i, 0)), full(bq_re), full(bq_im), full(cq_re), full(cq_im),
                  full(lbar), full(dskip)],
        out_specs=[pl.BlockSpec((TB, S5_WIDTH), lambda i: (i, 0)), pl.BlockSpec((1, 8, S5_LANES), lambda i: (i, 0, 0))],
        out_shape=[jax.ShapeDtypeStruct((L, S5_WIDTH), F32), jax.ShapeDtypeStruct((nb, 8, S5_LANES), F32)],
        scratch_shapes=[pltpu.VMEM((TB, S5_LANES), F32), pltpu.VMEM((TB, S5_LANES), F32), pltpu.VMEM((8, S5_LANES), F32)],
        compiler_params=_cparams(1))(proj, bq_re, bq_im, cq_re, cq_im, lbar, dskip)


def _s5_scan_bwd(proj, dy, ck, bq_re, bq_im, cq_re, cq_im, lbar, dskip, L, TB):
    nb = L // TB
    ng = TB // 8

    def body(u_ref, dy_ref, ck_ref, bre, bim, cre, cim, lb_ref, d_ref,
             du_ref, dbre, dbim, dcre, dcim, dlb_ref, dd_ref, sre, sim, gre, gim, carry, gcarry):
        i = pl.program_id(0)

        @pl.when(i == 0)
        def _():
            gcarry[...] = jnp.zeros(gcarry.shape, F32)
            dbre[...] = jnp.zeros(dbre.shape, F32)
            dbim[...] = jnp.zeros(dbim.shape, F32)
            dcre[...] = jnp.zeros(dcre.shape, F32)
            dcim[...] = jnp.zeros(dcim.shape, F32)
            dlb_ref[...] = jnp.zeros(dlb_ref.shape, F32)
            dd_ref[...] = jnp.zeros(dd_ref.shape, F32)

        lr = lb_ref[0:1, :]
        li = lb_ref[1:2, :]
        u = u_ref[...]
        ub = u.astype(BF16)
        dy_v = dy_ref[...]
        dyb = dy_v.astype(BF16)
        carry[...] = ck_ref[0]
        for q in range(S5_Q):
            uq = ub[:, q * S5_QL:(q + 1) * S5_QL]
            dq = dyb[:, q * S5_QL:(q + 1) * S5_QL]
            ss = slice(q * S5_QS, (q + 1) * S5_QS)
            sre[:, ss] = _dot(uq, bre[q])
            sim[:, ss] = _dot(uq, bim[q])
            gre[:, ss] = _dot_nt(dq, cre[q])
            gim[:, ss] = -_dot_nt(dq, cim[q])
        _scan_rows(sre, sim, carry, lr, li, TB, False)
        _scan_rows(gre, gim, gcarry, lr, -li, TB, True)

        sub = lax.broadcasted_iota(jnp.int32, (8, S5_LANES), 0)
        c0r = ck_ref[0, 0:1, :]
        c0i = ck_ref[0, 1:2, :]

        def acc_step(g, acc):
            ar, ai = acc
            base = pl.multiple_of(g * 8, 8)
            pbase = pl.multiple_of(jnp.maximum(g - 1, 0) * 8, 8)
            first = g == 0
            lastr = jnp.where(first, c0r, sre[pl.ds(pbase, 8), :][7:8, :])
            lasti = jnp.where(first, c0i, sim[pl.ds(pbase, 8), :][7:8, :])
            spr = jnp.where(sub == 0, jnp.broadcast_to(lastr, sub.shape), pltpu.roll(sre[pl.ds(base, 8), :], 1, 0))
            spi = jnp.where(sub == 0, jnp.broadcast_to(lasti, sub.shape), pltpu.roll(sim[pl.ds(base, 8), :], 1, 0))
            gr = gre[pl.ds(base, 8), :]
            gi_ = gim[pl.ds(base, 8), :]
            return ar + gr * spr + gi_ * spi, ai - gr * spi + gi_ * spr

        z8 = jnp.zeros((8, S5_LANES), F32)
        ar, ai = lax.fori_loop(0, ng, acc_step, (z8, z8))
        dlb_ref[0:1, :] += jnp.sum(ar, axis=0, keepdims=True)
        dlb_ref[1:2, :] += jnp.sum(ai, axis=0, keepdims=True)

        dd_ref[...] += jnp.sum(dy_v * u, axis=0, keepdims=True)
        for q in range(S5_Q):
            sl = slice(q * S5_QL, (q + 1) * S5_QL)
            ss = slice(q * S5_QS, (q + 1) * S5_QS)
            grq = gre[:, ss].astype(BF16)
            giq = gim[:, ss].astype(BF16)
            du_ref[:, sl] = _dot_nt(grq, bre[q]) + _dot_nt(giq, bim[q]) + dy_v[:, sl] * d_ref[:, sl]
            dbre[q] += _dot_tn(ub[:, sl], grq)
            dbim[q] += _dot_tn(ub[:, sl], giq)
            dcre[q] += _dot_tn(sre[:, ss].astype(BF16), dyb[:, sl])
            dcim[q] -= _dot_tn(sim[:, ss].astype(BF16), dyb[:, sl])

    full = lambda a: pl.BlockSpec(a.shape, lambda i, nd=a.ndim: (0,) * nd)
    rev = lambda i: (nb - 1 - i, 0)
    sh = jax.ShapeDtypeStruct
    outs = [sh((L, S5_WIDTH), F32), sh(bq_re.shape, F32), sh(bq_im.shape, F32), sh(cq_re.shape, F32), sh(cq_im.shape, F32),
            sh((8, S5_LANES), F32), sh((1, S5_WIDTH), F32)]
    fo = lambda s: pl.BlockSpec(s.shape, lambda i, nd=len(s.shape): (0,) * nd)
    return _pcall(
        body, name="s5_scan_bwd", grid=(nb,),
        in_specs=[pl.BlockSpec((TB, S5_WIDTH), rev), pl.BlockSpec((TB, S5_WIDTH), rev),
                  pl.BlockSpec((1, 8, S5_LANES), lambda i: (nb - 1 - i, 0, 0)),
                  full(bq_re), full(bq_im), full(cq_re), full(cq_im), full(lbar), full(dskip)],
        out_specs=[pl.BlockSpec((TB, S5_WIDTH), rev)] + [fo(s) for s in outs[1:]],
        out_shape=outs,
        scratch_shapes=[pltpu.VMEM((TB, S5_LANES), F32)] * 4 + [pltpu.VMEM((8, S5_LANES), F32)] * 2,
        compiler_params=_cparams(1))(proj, dy, ck, bq_re, bq_im, cq_re, cq_im, lbar, dskip)


N_PAIR = RW_WIDTH // 128


def _wkv_consts():
    sub = lax.broadcasted_iota(jnp.int32, (HEAD, 128), 0)
    lane = lax.broadcasted_iota(jnp.int32, (HEAD, 128), 1)
    diag = (sub == lane % HEAD).astype(F32)
    r = lax.broadcasted_iota(jnp.int32, (128, 128), 0) // HEAD
    c = lax.broadcasted_iota(jnp.int32, (128, 128), 1) // HEAD
    ones = (r == c).astype(BF16)
    return diag, ones


def _colbcast(row, diag, ones):
    return _split_dot(diag * row, ones, 3)


def _lanesum(x, ones):
    return _split_dot(x, ones, 2)


def _subsum(x):
    return jnp.sum(x, axis=0, keepdims=True)


def _put_row(tile, j, row):
    sub = lax.broadcasted_iota(jnp.int32, tile.shape, 0)
    return jnp.where(sub == j, jnp.broadcast_to(row, tile.shape), tile)


def _wkv_fwd(r, w, k, v, a, b, L):
    nc = L // CHUNK

    def body(r_ref, w_ref, k_ref, v_ref, a_ref, b_ref, y_ref, ck_ref, s_ref):
        c = pl.program_id(0)

        @pl.when(c == 0)
        def _():
            s_ref[...] = jnp.zeros(s_ref.shape, F32)

        ck_ref[0] = s_ref[...]
        diag, ones = _wkv_consts()

        def group(gi, _):
            base = pl.multiple_of(gi * 8, 8)
            for p in range(N_PAIR):
                ls = slice(p * 128, (p + 1) * 128)
                rt, wt, kt, vt, at, bt = (x[pl.ds(base, 8), ls] for x in (r_ref, w_ref, k_ref, v_ref, a_ref, b_ref))
                s = s_ref[p]
                yt = jnp.zeros((8, 128), F32)
                for j in range(8):
                    row = lambda x: x[j:j + 1, :]
                    sa = _lanesum(s * row(at), ones)
                    vb = _colbcast(row(vt), diag, ones)
                    s = s * row(wt) + sa * row(bt) + vb * row(kt)
                    yb = _lanesum(s * row(rt), ones)
                    yt = _put_row(yt, j, _subsum(diag * yb))
                s_ref[p] = s
                y_ref[pl.ds(base, 8), ls] = yt
            return 0

        lax.fori_loop(0, CHUNK // 8, group, 0)

    blk = pl.BlockSpec((CHUNK, RW_WIDTH), lambda c: (c, 0))
    return _pcall(
        body, name="wkv_fwd", grid=(nc,), in_specs=[blk] * 6,
        out_specs=[blk, pl.BlockSpec((1, N_PAIR, HEAD, 128), lambda c: (c, 0, 0, 0))],
        out_shape=[jax.ShapeDtypeStruct((L, RW_WIDTH), F32), jax.ShapeDtypeStruct((nc, N_PAIR, HEAD, 128), F32)],
        scratch_shapes=[pltpu.VMEM((N_PAIR, HEAD, 128), F32)],
        compiler_params=_cparams(1))(r, w, k, v, a, b)


def _wkv_bwd(r, w, k, v, a, b, dy, ck, L):
    nc = L // CHUNK

    def body(r_ref, w_ref, k_ref, v_ref, a_ref, b_ref, dy_ref, ck_ref,
             dr_ref, dw_ref, dk_ref, dv_ref, da_ref, db_ref, st_ref, ds_ref):
        c = pl.program_id(0)

        @pl.when(c == 0)
        def _():
            ds_ref[...] = jnp.zeros(ds_ref.shape, F32)

        diag, ones = _wkv_consts()

        def replay(gi, _):
            base = pl.multiple_of(gi * 8, 8)
            for p in range(N_PAIR):
                ls = slice(p * 128, (p + 1) * 128)
                wt, kt, vt, at, bt = (x[pl.ds(base, 8), ls] for x in (w_ref, k_ref, v_ref, a_ref, b_ref))
                s = st_ref[p, CHUNK]
                for j in range(8):
                    row = lambda x: x[j:j + 1, :]
                    st_ref[p, base + j] = s
                    sa = _lanesum(s * row(at), ones)
                    vb = _colbcast(row(vt), diag, ones)
                    s = s * row(wt) + sa * row(bt) + vb * row(kt)
                st_ref[p, CHUNK] = s
            return 0

        for p in range(N_PAIR):
            st_ref[p, CHUNK] = ck_ref[0, p]
        lax.fori_loop(0, CHUNK // 8, replay, 0)

        def back(gi, _):
            g = CHUNK // 8 - 1 - gi
            base = pl.multiple_of(g * 8, 8)
            for p in range(N_PAIR):
                ls = slice(p * 128, (p + 1) * 128)
                rt, wt, kt, vt, at, bt, dyt = (x[pl.ds(base, 8), ls]
                                               for x in (r_ref, w_ref, k_ref, v_ref, a_ref, b_ref, dy_ref))
                ds = ds_ref[p]
                o_r = jnp.zeros((8, 128), F32)
                o_w, o_k, o_v, o_a, o_b = o_r, o_r, o_r, o_r, o_r
                for j in range(7, -1, -1):
                    row = lambda x: x[j:j + 1, :]
                    sp = st_ref[p, base + j]
                    sa = _lanesum(sp * row(at), ones)
                    vb = _colbcast(row(vt), diag, ones)
                    st = sp * row(wt) + sa * row(bt) + vb * row(kt)
                    dyb = _colbcast(row(dyt), diag, ones)
                    ds = ds + dyb * row(rt)
                    o_r = _put_row(o_r, j, _subsum(st * dyb))
                    o_w = _put_row(o_w, j, _subsum(ds * sp))
                    o_b = _put_row(o_b, j, _subsum(ds * sa))
                    o_k = _put_row(o_k, j, _subsum(ds * vb))
                    o_v = _put_row(o_v, j, _subsum(diag * _lanesum(ds * row(kt), ones)))
                    dsa = _lanesum(ds * row(bt), ones)
                    o_a = _put_row(o_a, j, _subsum(sp * dsa))
                    ds = ds * row(wt) + dsa * row(at)
                ds_ref[p] = ds
                dr_ref[pl.ds(base, 8), ls] = o_r
                dw_ref[pl.ds(base, 8), ls] = o_w
                dk_ref[pl.ds(base, 8), ls] = o_k
                dv_ref[pl.ds(base, 8), ls] = o_v
                da_ref[pl.ds(base, 8), ls] = o_a
                db_ref[pl.ds(base, 8), ls] = o_b
            return 0

        lax.fori_loop(0, CHUNK // 8, back, 0)

    blk = pl.BlockSpec((CHUNK, RW_WIDTH), lambda c: (nc - 1 - c, 0))
    sh = jax.ShapeDtypeStruct((L, RW_WIDTH), F32)
    return _pcall(
        body, name="wkv_bwd", grid=(nc,),
        in_specs=[blk] * 7 + [pl.BlockSpec((1, N_PAIR, HEAD, 128), lambda c: (nc - 1 - c, 0, 0, 0))],
        out_specs=[blk] * 6, out_shape=[sh] * 6,
        scratch_shapes=[pltpu.VMEM((N_PAIR, CHUNK + 1, HEAD, 128), F32), pltpu.VMEM((N_PAIR, HEAD, 128), F32)],
        compiler_params=_cparams(1))(r, w, k, v, a, b, dy, ck)


TB = 256


def _bf(x):
    return x.astype(BF16)


def _inproj_fwd(x, norm_mix, w_in, L):
    def fn(i, tv, cv):
        xn = _rms(tv[0], cv[0])
        return _dot(_bf(xn), cv[1]), xn

    return _tok_call("inproj_fwd", fn, L, TB, [(x, D_MODEL, 0)], [norm_mix, w_in], [(IN_COLS, F32), (D_MODEL, BF16)])


def _s5_post_fn(glu_w, wtop, diff=True):
    mg = _mmc(glu_w, diff)
    mt = _mmc(wtop, diff) if wtop is not None else None

    def f(y, glu_b, e):
        z = _gelu(y)
        out = z * _sigmoid(mg(z) + glu_b + e)
        res = mt(out) if mt is not None else out
        return res, (z, out)

    return f


def _s5_post_fwd(y, glu_w, glu_b, L):
    def fn(i, tv, cv):
        out, _ = _s5_post_fn(cv[0], None, False)(tv[0], cv[1], 0.0)
        return (out,)

    return _tok_call("s5_post_fwd", fn, L, TB, [(y, S5_WIDTH, 0)], [glu_w, glu_b], [(S5_WIDTH, F32)])[0]


def _s5_post_bwd(y, dh1, glu_w, glu_b, wtop, L):
    def fn(i, tv, cv):
        e0 = jnp.zeros((TB, S5_WIDTH), F32)
        _, vjp, (z, out) = jax.vjp(_s5_post_fn(cv[0], cv[2]), tv[0], cv[1], e0, has_aux=True)
        dy, db, de = vjp(tv[1])
        return dy, z, de, out, db

    return _tok_call("s5_post_bwd", fn, L, TB, [(y, S5_WIDTH, 0), (dh1, D_MODEL, 0)], [glu_w, glu_b, wtop],
                     [(S5_WIDTH, F32), (S5_WIDTH, BF16), (S5_WIDTH, BF16), (S5_WIDTH, BF16)], [(1, S5_WIDTH)])


RW_COLBLK = ((RW_WIDTH, 1), (RW_WIDTH, 2), (RW_WIDTH, 3), (128, 16), (128, 17))
RW_MU = ((0, 512), (512, 1024), (1024, 1536), (1536, 1664), (1664, 1792))


def _rw_pre_fn(w2pad, a2pad, g2, diff=True):
    m_w, m_a, m_g = _mmc(w2pad, diff), _mmc(a2pad, diff), _mmc(g2, diff)
    seg = _segsum(_head_indicator(RW_WIDTH), diff)

    def f(zr, zk, zv, zwa, zg, w0, a0, k_k, k_a, e_w, e_a):
        wl_t = jnp.tanh(zwa)
        wlin = w0 + m_w(wl_t) + e_w
        w = -_softplus(-wlin) - 0.5
        decay = jnp.exp(-jnp.exp(w))
        a = _sigmoid(a0 + m_a(zwa) + e_a)
        sg = _sigmoid(zg)
        g = m_g(sg)
        kk = zk * k_k
        kkn = kk / jnp.maximum(jnp.sqrt(seg(kk * kk)), L2_EPS)
        kf = zk * (1.0 + (a - 1.0) * k_a)
        return (zr, decay, kf, zv, -kkn, kkn * a, g), (wl_t, sg)

    return f


def _rw_shifted(i, tv, mu):
    sub = lax.broadcasted_iota(jnp.int32, (TB, 1), 0)
    zs, dif = [], []
    for n in range(5):
        z = tv[n]
        last = jnp.where(i == 0, 0.0, tv[5 + n][7:8, :])
        prev = jnp.where(sub == 0, last, pltpu.roll(z, 1, 0))
        m = mu[:, RW_MU[n][0]:RW_MU[n][1]]
        zs.append(z + (prev - z) * m)
        dif.append(prev - z)
    return zs, dif


def _rw_tok_in(proj):
    return [(proj, wd, cb) for wd, cb in RW_COLBLK] + [(proj, wd, cb, "prev") for wd, cb in RW_COLBLK]


def _rw_pre_fwd(proj, mu, w0, a0, k_k, k_a, w2pad, a2pad, g2, L):
    def fn(i, tv, cv):
        zs, _ = _rw_shifted(i, tv, cv[0])
        outs, _ = _rw_pre_fn(cv[5], cv[6], cv[7], False)(*zs, cv[1], cv[2], cv[3], cv[4], 0.0, 0.0)
        return outs

    return _tok_call("rw_pre_fwd", fn, L, TB, _rw_tok_in(proj), [mu, w0, a0, k_k, k_a, w2pad, a2pad, g2],
                     [(RW_WIDTH, F32)] * 7)


def _rw_pre_bwd(proj, cots, mu, w0, a0, k_k, k_a, w2pad, a2pad, g2, L):
    def fn(i, tv, cv):
        zs, dif = _rw_shifted(i, tv[:10], cv[0])
        dr1, dr2, dw, dk1, dk2, dv1, dv2, da, db, dg = tv[10:]
        e0 = jnp.zeros((TB, RW_WIDTH), F32)
        _, vjp, (wl_t, sg) = jax.vjp(_rw_pre_fn(cv[5], cv[6], cv[7]), *zs, cv[1], cv[2], cv[3], cv[4], e0, e0, has_aux=True)
        g = vjp((dr1 + dr2, dw, dk1 + dk2, dv1 + dv2, da, db, dg))
        dzs = jnp.concatenate(g[:5], axis=1)
        dmu = jnp.concatenate([jnp.sum(g[n] * dif[n], axis=0, keepdims=True) for n in range(5)], axis=1)
        return dzs, wl_t, zs[3], sg, g[9], g[10], dmu, g[5], g[6], g[7], g[8]

    tok_in = _rw_tok_in(proj) + [(c, RW_WIDTH, 0) for c in cots]
    return _tok_call("rw_pre_bwd", fn, L, TB, tok_in, [mu, w0, a0, k_k, k_a, w2pad, a2pad, g2],
                     [(SHIFT_COLS, F32), (128, BF16), (128, BF16), (128, BF16), (RW_WIDTH, BF16), (RW_WIDTH, BF16)],
                     [(1, SHIFT_COLS)] + [(1, RW_WIDTH)] * 4)


def _rw_post_fn(wbot, diff=True):
    seg = _segsum(_head_indicator(RW_WIDTH), diff)
    mb = _mmc(wbot, diff) if wbot is not None else None

    def f(y, r, kf, v, g, ln_w, ln_b, r_k):
        mean = seg(y) * (1.0 / HEAD)
        yc = y - mean
        var = seg(yc * yc) * (1.0 / HEAD)
        yn = yc * lax.rsqrt(var + GN_EPS) * ln_w + ln_b
        bonus = seg(r * kf * r_k) * v
        out = (yn + bonus) * g
        res = mb(out) if mb is not None else out
        return res, out

    return f


def _rw_post_fwd(y, r, kf, v, g, ln_w, ln_b, r_k, L):
    def fn(i, tv, cv):
        out, _ = _rw_post_fn(None, False)(*tv, *cv)
        return (out,)

    return _tok_call("rw_post_fwd", fn, L, TB, [(t, RW_WIDTH, 0) for t in (y, r, kf, v, g)], [ln_w, ln_b, r_k],
                     [(RW_WIDTH, F32)])[0]


def _rw_post_bwd(y, r, kf, v, g, dh1, ln_w, ln_b, r_k, wbot, L):
    def fn(i, tv, cv):
        _, vjp, out = jax.vjp(_rw_post_fn(cv[3]), *tv[:5], cv[0], cv[1], cv[2], has_aux=True)
        gr = vjp(tv[5])
        return gr[0], gr[1], gr[2], gr[3], gr[4], out, gr[5], gr[6], gr[7]

    return _tok_call("rw_post_bwd", fn, L, TB, [(t, RW_WIDTH, 0) for t in (y, r, kf, v, g)] + [(dh1, D_MODEL, 0)],
                     [ln_w, ln_b, r_k, wbot], [(RW_WIDTH, F32)] * 5 + [(RW_WIDTH, BF16)], [(1, RW_WIDTH)] * 3)


def _ffn_fn(w1, w3, w2, diff=True):
    m1, m3, m2 = _mmc(w1, diff), _mmc(w3, diff), _mmc(w2, diff)

    def f(h1, norm_ffn, e1, e3):
        hn = _rms(h1, norm_ffn)
        a1 = m1(hn) + e1
        a3 = m3(hn) + e3
        hm = a1 * _sigmoid(a1) * a3
        return h1 + m2(hm), (hn, hm)

    return f


TB_FFN = 256


def _mixffn_fwd(x, s5_out, rw_out, wtop, wbot, norm_ffn, w1, w3, w2, L):
    def fn(i, tv, cv):
        h1 = tv[0] + _dot(_bf(tv[1]), cv[0]) + _dot(_bf(tv[2]), cv[1])
        h2, _ = _ffn_fn(cv[3], cv[4], cv[5], False)(h1, cv[2], 0.0, 0.0)
        return h1, h2

    return _tok_call("mixffn_fwd", fn, L, TB_FFN, [(x, D_MODEL, 0), (s5_out, S5_WIDTH, 0), (rw_out, RW_WIDTH, 0)],
                     [wtop, wbot, norm_ffn, w1, w3, w2], [(D_MODEL, F32), (D_MODEL, F32)])


def _ffn_bwd(h1, dh2, norm_ffn, w1, w3, w2, L):
    def fn(i, tv, cv):
        e0 = jnp.zeros((TB_FFN, FFN_HIDDEN), F32)
        _, vjp, (hn, hm) = jax.vjp(_ffn_fn(cv[1], cv[2], cv[3]), tv[0], cv[0], e0, e0, has_aux=True)
        dh1, dn, d1, d3 = vjp(tv[1])
        return dh1, d1, d3, hm, hn, dn

    return _tok_call("ffn_bwd", fn, L, TB_FFN, [(h1, D_MODEL, 0), (dh2, D_MODEL, 0)], [norm_ffn, w1, w3, w2],
                     [(D_MODEL, F32), (FFN_HIDDEN, BF16), (FFN_HIDDEN, BF16), (FFN_HIDDEN, BF16), (D_MODEL, BF16)],
                     [(1, D_MODEL)])


def _ple_loss_fb(h2, p, target, norm_ple, final_norm, wg, wu, L):
    def fn(i, tv, cv):
        mgate, mup = _mmc(cv[2]), _mmc(cv[3], False)

        def f(h2_, norm_ple_, final_norm_, eg, eu):
            hn = _rms(h2_, norm_ple_)
            gate = _sigmoid(mgate(hn) + eg)
            h3 = h2_ + gate * (mup(tv[1]) + eu)
            out = _rms(h3, final_norm_)
            d = out - tv[2]
            return 0.5 * jnp.sum(jnp.mean(d * d, axis=-1, keepdims=True)), hn

        e0 = jnp.zeros((TB, D_MODEL), F32)
        loss, vjp, hn = jax.vjp(f, tv[0], cv[0], cv[1], e0, e0, has_aux=True)
        dh2, dnp, dfn, deg, deu = vjp(jnp.ones((), F32))
        return dh2, deg, deu, hn, jnp.full((8, 128), loss, F32), dnp, dfn

    return _tok_call("ple_loss_fb", fn, L, TB, [(h2, D_MODEL, 0), (p, PLE_DIM, 0), (target, D_MODEL, 0)],
                     [norm_ple, final_norm, wg, wu], [(D_MODEL, F32), (D_MODEL, BF16), (D_MODEL, BF16), (D_MODEL, BF16)],
                     [(8, 128), (1, D_MODEL), (1, D_MODEL)])


def _inproj_bwd(x, dh1, du, dzs, norm_mix, mu, w_u, w_z, L):
    nb = L // TB

    def fn(i, tv, cv):
        sub = lax.broadcasted_iota(jnp.int32, (TB, 1), 0)
        m = cv[1]
        b = tv[3] * m
        nxt = jnp.where(i == nb - 1, 0.0, tv[4][0:1, :] * m)
        dz = tv[3] * (1.0 - m) + jnp.where(sub == TB - 1, nxt, pltpu.roll(b, TB - 1, 0))
        dub, dzb = _bf(tv[2]), _bf(dz)
        dxn = _dot_nt(dub, cv[2]) + _dot_nt(dzb, cv[3])
        _, vjp = jax.vjp(_rms, tv[0], cv[0])
        dx, dn = vjp(dxn)
        return tv[1] + dx, jnp.concatenate([dub, dzb], axis=1), dn

    return _tok_call("inproj_bwd", fn, L, TB,
                     [(x, D_MODEL, 0), (dh1, D_MODEL, 0), (du, S5_WIDTH, 0), (dzs, SHIFT_COLS, 0), (dzs, SHIFT_COLS, 0, "next")],
                     [norm_mix, mu, w_u, w_z], [(D_MODEL, F32), (IN_COLS, BF16)], [(1, D_MODEL)])


def _eye8(dt):
    return jnp.eye(8, dtype=dt)


def _quarter_b(bb):
    return jnp.einsum("hg,qgcp->qhcgp", _eye8(bb.dtype), bb.reshape(S5_Q, 8, S5_GROUP, S5_STATE)).reshape(S5_Q, S5_QL, S5_QS)


def _unquarter_b(d):
    return jnp.einsum("qhcgp,hg->qgcp", d.reshape(S5_Q, 8, S5_GROUP, 8, S5_STATE), _eye8(d.dtype)).reshape(
        S5_GROUPS, S5_GROUP, S5_STATE)


def _quarter_c(c):
    return jnp.einsum("gh,qgcp->qgphc", _eye8(c.dtype), c.reshape(S5_Q, 8, S5_GROUP, S5_STATE)).reshape(S5_Q, S5_QS, S5_QL)


def _unquarter_c(d):
    return jnp.einsum("qgphc,gh->qgcp", d.reshape(S5_Q, 8, S5_STATE, 8, S5_GROUP), _eye8(d.dtype)).reshape(
        S5_GROUPS, S5_GROUP, S5_STATE)


def _pad_rows(w, lo, n):
    return jnp.zeros((n, w.shape[1]), w.dtype).at[lo:lo + w.shape[0]].set(w)


def _local_step(x, p, target, W):
    L = x.shape[0]
    r2 = lambda v: v.reshape(1, -1)
    w_in = W["w_in"]
    wtop, wbot = W["w_out"][:S5_WIDTH], W["w_out"][S5_WIDTH:]
    w2pad = _pad_rows(W["rw_w2"], 0, 128)
    a2pad = _pad_rows(W["rw_a2"], 64, 128)
    mu = r2(W["rw_shift_mu"])
    rw_vec = [r2(W[n]) for n in ("rw_w0", "rw_a0", "rw_k_k", "rw_k_a")]
    ln_w, ln_b, r_k = r2(W["rw_ln_w"]), r2(W["rw_ln_b"]), r2(W["rw_r_k"])

    lam_re, lam_im = W["s5_lam_re"], W["s5_lam_im"]
    log_step = W["s5_log_step"].reshape(S5_GROUPS, 1)
    bt_re, bt_im = W["s5_b_re"].transpose(0, 2, 1), W["s5_b_im"].transpose(0, 2, 1)
    lb_re, lb_im, bb_re, bb_im = _s5_param_fwd(lam_re, lam_im, log_step, bt_re, bt_im)
    bq_re, bq_im = _quarter_b(bb_re).astype(BF16), _quarter_b(bb_im).astype(BF16)
    cq_re, cq_im = _quarter_c(W["s5_c_re"]).astype(BF16), _quarter_c(W["s5_c_im"]).astype(BF16)
    lbar = jnp.concatenate([lb_re.reshape(1, -1), lb_im.reshape(1, -1), jnp.zeros((6, S5_LANES), F32)], axis=0)
    dskip = r2(W["s5_d"])
    glu_b = r2(W["s5_glu_b"])
    norm_mix, norm_ffn, norm_ple, final_norm = (r2(W[n]) for n in ("norm_mix", "norm_ffn", "norm_ple", "final_norm"))

    proj, xn = _inproj_fwd(x, norm_mix, w_in, L)
    y_s5, ck5 = _s5_scan_fwd(proj, bq_re, bq_im, cq_re, cq_im, lbar, dskip, L, TB)
    s5_out = _s5_post_fwd(y_s5, W["s5_glu_w"], glu_b, L)
    r, wd, kf, v, a_s, b_s, g = _rw_pre_fwd(proj, mu, *rw_vec, w2pad, a2pad, W["rw_g2"], L)
    y_wkv, ckw = _wkv_fwd(r, wd, kf, v, a_s, b_s, L)
    rw_out = _rw_post_fwd(y_wkv, r, kf, v, g, ln_w, ln_b, r_k, L)
    h1, h2 = _mixffn_fwd(x, s5_out, rw_out, wtop, wbot, norm_ffn, W["ffn_w1"], W["ffn_w3"], W["ffn_w2"], L)

    G = {}
    dh2, deg, deu, hn_ple, loss_acc, G["norm_ple"], G["final_norm"] = _ple_loss_fb(
        h2, p, target, norm_ple, final_norm, W["ple_gate_w"], W["ple_up_w"], L)
    G["ple_gate_w"] = _mm_tn("dw_ple_gate", hn_ple, deg)
    G["ple_up_w"] = _mm_tn("dw_ple_up", p, deu)
    dh1, da1, da3, hm, hn_ffn, G["norm_ffn"] = _ffn_bwd(h1, dh2, norm_ffn, W["ffn_w1"], W["ffn_w3"], W["ffn_w2"], L)
    G["ffn_w1"] = _mm_tn("dw_ffn_w1", hn_ffn, da1)
    G["ffn_w3"] = _mm_tn("dw_ffn_w3", hn_ffn, da3)
    G["ffn_w2"] = _mm_tn("dw_ffn_w2", hm, dh2)
    dy_s5, z_bf, dgp, s5o_bf, G["s5_glu_b"] = _s5_post_bwd(y_s5, dh1, W["s5_glu_w"], glu_b, wtop, L)
    G["s5_glu_w"] = _mm_tn("dw_s5_glu", z_bf, dgp)
    dy_wkv, dr2, dk2, dv2, dg, rwo_bf, G["rw_ln_w"], G["rw_ln_b"], G["rw_r_k"] = _rw_post_bwd(
        y_wkv, r, kf, v, g, dh1, ln_w, ln_b, r_k, wbot, L)
    G["w_out"] = jnp.concatenate([_mm_tn("dw_out_top", s5o_bf, dh1), _mm_tn("dw_out_bot", rwo_bf, dh1)], axis=0)
    dr1, dwd, dk1, dv1, da_s, db_s = _wkv_bwd(r, wd, kf, v, a_s, b_s, dy_wkv, ckw, L)
    (dzs, wl_t, zwa, sg, dwlin, dalin, G["rw_shift_mu"], G["rw_w0"], G["rw_a0"], G["rw_k_k"], G["rw_k_a"]) = _rw_pre_bwd(
        proj, (dr1, dr2, dwd, dk1, dk2, dv1, dv2, da_s, db_s, dg), mu, *rw_vec, w2pad, a2pad, W["rw_g2"], L)
    G["rw_w2"] = _mm_tn("dw_rw_w2", wl_t, dwlin)[:64]
    G["rw_a2"] = _mm_tn("dw_rw_a2", zwa, dalin)[64:]
    G["rw_g2"] = _mm_tn("dw_rw_g2", sg, dg)
    du, dbq_re, dbq_im, dcq_re, dcq_im, dlbar, G["s5_d"] = _s5_scan_bwd(
        proj, dy_s5, ck5, bq_re, bq_im, cq_re, cq_im, lbar, dskip, L, TB)
    G["s5_c_re"], G["s5_c_im"] = _unquarter_c(dcq_re), _unquarter_c(dcq_im)
    d_lam_re, d_lam_im, d_ls, d_bt_re, d_bt_im = _s5_param_bwd(
        lam_re, lam_im, log_step, bt_re, bt_im, dlbar[0].reshape(S5_GROUPS, S5_STATE), dlbar[1].reshape(S5_GROUPS, S5_STATE),
        _unquarter_b(dbq_re), _unquarter_b(dbq_im))
    G["s5_lam_re"], G["s5_lam_im"], G["s5_log_step"] = d_lam_re, d_lam_im, d_ls.reshape(S5_GROUPS)
    G["s5_b_re"], G["s5_b_im"] = d_bt_re.transpose(0, 2, 1), d_bt_im.transpose(0, 2, 1)
    dx, dproj, G["norm_mix"] = _inproj_bwd(x, dh1, du, dzs, norm_mix, mu, w_in[:, :S5_WIDTH], w_in[:, S5_WIDTH:], L)
    G["w_in"] = _mm_tn("dw_in", xn, dproj)
    return loss_acc[0, 0], dx, G


MESH_AXES = ("x", "y", "c")
_ANY = pl.BlockSpec(memory_space=pl.ANY)


def _all_gather(name, shard):
    m_per, n = shard.shape

    def body(x_ref, out_ref, send_sems, recv_sems, local_sem):
        x, y, c = lax.axis_index("x"), lax.axis_index("y"), lax.axis_index("c")
        me, sibling = (x, y, c), (x, y, 1 - c)
        chips = [(1 - x, y), (x, 1 - y), (1 - x, 1 - y)]

        def rows(px, py, pc):
            return out_ref.at[pl.ds((4 * px + 2 * py + pc) * m_per, m_per), :]

        def copy(k, block, to, src=None):
            return pltpu.make_async_remote_copy(
                src_ref=rows(*block) if src is None else src, dst_ref=rows(*block),
                send_sem=send_sems.at[k], recv_sem=recv_sems.at[k], device_id=to, device_id_type=pl.DeviceIdType.MESH)

        mine = pltpu.make_async_copy(x_ref, rows(*me), local_sem)
        mine.start()
        first = [copy(0, me, sibling, src=x_ref)]
        first += [copy(1 + j, me, (*chip, c), src=x_ref) for j, chip in enumerate(chips)]
        for cp in first:
            cp.start()
        passed = [copy(4 + j, (*chip, c), sibling) for j, chip in enumerate(chips)]
        for j, chip in enumerate(chips):
            copy(1 + j, (*chip, c), me).wait_recv()
            passed[j].start()
        copy(0, sibling, me).wait_recv()
        for j, chip in enumerate(chips):
            copy(4 + j, (*chip, 1 - c), me).wait_recv()
        for cp in first + passed:
            cp.wait_send()
        mine.wait()

    return _pcall(body, name=name, out_shape=jax.ShapeDtypeStruct((N_DEV * m_per, n), shard.dtype),
                  in_specs=[_ANY], out_specs=_ANY,
                  scratch_shapes=[pltpu.SemaphoreType.DMA((7,)), pltpu.SemaphoreType.DMA((7,)), pltpu.SemaphoreType.DMA(())])(shard)


def _exchange_slabs(name, slabs):
    _, m, n = slabs.shape

    def body(x_ref, out_ref, send_sems, recv_sems, local_sem):
        x, y, c = lax.axis_index("x"), lax.axis_index("y"), lax.axis_index("c")
        me = 4 * x + 2 * y + c
        mine = pltpu.make_async_copy(x_ref.at[me], out_ref.at[me], local_sem)
        mine.start()

        def copy(k):
            px, py, pc = x ^ ((k >> 2) & 1), y ^ ((k >> 1) & 1), c ^ (k & 1)
            return pltpu.make_async_remote_copy(
                src_ref=x_ref.at[4 * px + 2 * py + pc], dst_ref=out_ref.at[me],
                send_sem=send_sems.at[k - 1], recv_sem=recv_sems.at[k - 1],
                device_id=(px, py, pc), device_id_type=pl.DeviceIdType.MESH)

        def landing(k):
            px, py, pc = x ^ ((k >> 2) & 1), y ^ ((k >> 1) & 1), c ^ (k & 1)
            return pltpu.make_async_remote_copy(
                src_ref=x_ref.at[me], dst_ref=out_ref.at[4 * px + 2 * py + pc],
                send_sem=send_sems.at[k - 1], recv_sem=recv_sems.at[k - 1],
                device_id=(px, py, pc), device_id_type=pl.DeviceIdType.MESH)

        for k in range(1, N_DEV):
            copy(k).start()
        for k in range(1, N_DEV):
            landing(k).wait_recv()
        for k in range(1, N_DEV):
            copy(k).wait_send()
        mine.wait()

    return _pcall(body, name=name, out_shape=jax.ShapeDtypeStruct(slabs.shape, slabs.dtype), in_specs=[_ANY], out_specs=_ANY,
                  scratch_shapes=[pltpu.SemaphoreType.DMA((7,)), pltpu.SemaphoreType.DMA((7,)), pltpu.SemaphoreType.DMA(())])(slabs)


def _adamw(name, parts, w, m, v, rb):
    _, R, N = parts.shape

    def body(p_ref, w_ref, m_ref, v_ref, g_ref, d_ref, nm_ref, nv_ref):
        g = p_ref[0]
        for s in range(1, N_DEV):
            g = g + p_ref[s]
        nm = ADAM_B1 * m_ref[...] + (1.0 - ADAM_B1) * g
        nv = ADAM_B2 * v_ref[...] + (1.0 - ADAM_B2) * (g * g)
        m_hat = nm / (1.0 - ADAM_B1 ** ADAM_STEP)
        v_hat = nv / (1.0 - ADAM_B2 ** ADAM_STEP)
        g_ref[...] = g
        d_ref[...] = -ADAM_LR * (m_hat / (jnp.sqrt(v_hat) + ADAM_EPS) + ADAM_WD * w_ref[...])
        nm_ref[...] = nm
        nv_ref[...] = nv

    blk = pl.BlockSpec((rb, N), lambda i: (i, 0))
    sh = jax.ShapeDtypeStruct((R, N), F32)
    return _pcall(body, name=name, grid=(R // rb,), in_specs=[pl.BlockSpec((N_DEV, rb, N), lambda i: (0, i, 0)), blk, blk, blk],
                  out_specs=[blk] * 4, out_shape=[sh] * 4, compiler_params=_cparams(1))(parts, w, m, v)


BIG = (("w_in", 1), ("s5_glu_w", 0), ("rw_w2", 1), ("rw_a2", 1), ("rw_g2", 1), ("w_out", 0), ("ffn_w1", 1), ("ffn_w3", 1),
       ("ffn_w2", 0), ("ple_gate_w", 0), ("ple_up_w", 1))
BIG_NAMES = tuple(n for n, _ in BIG)
PACK_COLS = 1024
SMALL_ROWS = 144
WEIGHT_NAMES = ("norm_mix", "w_in", "s5_lam_re", "s5_lam_im", "s5_log_step", "s5_b_re", "s5_b_im", "s5_c_re", "s5_c_im", "s5_d",
                "s5_glu_w", "s5_glu_b", "rw_shift_mu", "rw_w0", "rw_w2", "rw_a0", "rw_a2", "rw_g2", "rw_k_k", "rw_k_a", "rw_r_k",
                "rw_ln_w", "rw_ln_b", "w_out", "norm_ffn", "ffn_w1", "ffn_w3", "ffn_w2", "norm_ple", "ple_gate_w", "ple_up_w",
                "final_norm")
SMALL_NAMES = tuple(n for n in WEIGHT_NAMES if n not in BIG_NAMES)
ARG_NAMES = ("x", "p") + WEIGHT_NAMES + ("loss_target",) + tuple("m_" + n for n in WEIGHT_NAMES) + tuple("v_" + n for n in WEIGHT_NAMES)


def _pack_rows(arrs, dt):
    return jnp.concatenate([a.reshape(-1, PACK_COLS).astype(dt) for a in arrs], axis=0)


def _pack_small(arrs):
    flat = jnp.concatenate([a.reshape(-1).astype(F32) for a in arrs])
    return jnp.pad(flat, (0, SMALL_ROWS * PACK_COLS - flat.shape[0])).reshape(SMALL_ROWS, PACK_COLS)


def _kernel_impl(ins):
    x, p, target = ins["x"][0], ins["p"][0, 0], ins["loss_target"][0]
    shard = {n: ins[n][0] for n, _ in BIG}
    small = {n: (ins[n] if n == "final_norm" else ins[n][0]) for n in SMALL_NAMES}

    pack = _pack_rows([shard[n] for n, _ in BIG], BF16)
    rows_per = pack.shape[0]
    gathered = _all_gather("ag_weights", pack).reshape(N_DEV, rows_per, PACK_COLS)
    W = dict(small)
    off = 0
    for n, ax in BIG:
        r, c = shard[n].shape
        nrow = r * c // PACK_COLS
        seg = gathered[:, off:off + nrow].reshape(N_DEV, r, c)
        W[n] = seg.reshape(N_DEV * r, c) if ax == 0 else seg.transpose(1, 0, 2).reshape(r, N_DEV * c)
        off += nrow

    loss_part, dx, G = _local_step(x, p, target, W)

    slabs = []
    for n, ax in BIG:
        r, c = shard[n].shape
        g = G[n]
        g = g.reshape(N_DEV, r, c) if ax == 0 else g.reshape(r, N_DEV, c).transpose(1, 0, 2)
        slabs.append(g.reshape(N_DEV, r * c // PACK_COLS, PACK_COLS))
    recv = _exchange_slabs("grad_exchange", jnp.concatenate(slabs, axis=1))
    gsm = _all_gather("ag_small_grads", _pack_small([G[n] for n in SMALL_NAMES])).reshape(N_DEV, SMALL_ROWS, PACK_COLS)

    pk = lambda pre: _pack_rows([ins[pre + n][0] for n, _ in BIG], F32)
    g_b, d_b, m_b, v_b = _adamw("adamw_sharded", recv, pk(""), pk("m_"), pk("v_"), 240)
    ps = lambda pre: _pack_small([ins[pre + n] for n in SMALL_NAMES])
    g_s, d_s, m_s, v_s = _adamw("adamw_replicated", gsm, ps(""), ps("m_"), ps("v_"), SMALL_ROWS)

    outs = {}
    for tag, big, sm in (("grad_", g_b, g_s), ("delta_", d_b, d_s), ("new_m_", m_b, m_s), ("new_v_", v_b, v_s)):
        off = 0
        for n, _ in BIG:
            nrow = shard[n].size // PACK_COLS
            outs[tag + n] = big[off:off + nrow].reshape(ins[n].shape)
            off += nrow
        flat = sm.reshape(-1)
        off = 0
        for n in SMALL_NAMES:
            outs[tag + n] = flat[off:off + ins[n].size].reshape(ins[n].shape)
            off += ins[n].size
    loss = lax.psum(loss_part, MESH_AXES)
    res = [loss, dx[None]]
    for tag in ("grad_", "delta_", "new_m_", "new_v_"):
        res += [outs[tag + n] for n in WEIGHT_NAMES]
    return tuple(res)


def kernel(x, p, norm_mix, w_in, s5_lam_re, s5_lam_im, s5_log_step, s5_b_re, s5_b_im, s5_c_re, s5_c_im, s5_d, s5_glu_w, s5_glu_b, rw_shift_mu, rw_w0, rw_w2, rw_a0, rw_a2, rw_g2, rw_k_k, rw_k_a, rw_r_k, rw_ln_w, rw_ln_b, w_out, norm_ffn, ffn_w1, ffn_w3, ffn_w2, norm_ple, ple_gate_w, ple_up_w, final_norm, loss_target, m_norm_mix, m_w_in, m_s5_lam_re, m_s5_lam_im, m_s5_log_step, m_s5_b_re, m_s5_b_im, m_s5_c_re, m_s5_c_im, m_s5_d, m_s5_glu_w, m_s5_glu_b, m_rw_shift_mu, m_rw_w0, m_rw_w2, m_rw_a0, m_rw_a2, m_rw_g2, m_rw_k_k, m_rw_k_a, m_rw_r_k, m_rw_ln_w, m_rw_ln_b, m_w_out, m_norm_ffn, m_ffn_w1, m_ffn_w3, m_ffn_w2, m_norm_ple, m_ple_gate_w, m_ple_up_w, m_final_norm, v_norm_mix, v_w_in, v_s5_lam_re, v_s5_lam_im, v_s5_log_step, v_s5_b_re, v_s5_b_im, v_s5_c_re, v_s5_c_im, v_s5_d, v_s5_glu_w, v_s5_glu_b, v_rw_shift_mu, v_rw_w0, v_rw_w2, v_rw_a0, v_rw_a2, v_rw_g2, v_rw_k_k, v_rw_k_a, v_rw_r_k, v_rw_ln_w, v_rw_ln_b, v_w_out, v_norm_ffn, v_ffn_w1, v_ffn_w3, v_ffn_w2, v_norm_ple, v_ple_gate_w, v_ple_up_w, v_final_norm):
    return _kernel_impl(dict(zip(ARG_NAMES, (x, p, norm_mix, w_in, s5_lam_re, s5_lam_im, s5_log_step, s5_b_re, s5_b_im, s5_c_re, s5_c_im, s5_d, s5_glu_w, s5_glu_b, rw_shift_mu, rw_w0, rw_w2, rw_a0, rw_a2, rw_g2, rw_k_k, rw_k_a, rw_r_k, rw_ln_w, rw_ln_b, w_out, norm_ffn, ffn_w1, ffn_w3, ffn_w2, norm_ple, ple_gate_w, ple_up_w, final_norm, loss_target, m_norm_mix, m_w_in, m_s5_lam_re, m_s5_lam_im, m_s5_log_step, m_s5_b_re, m_s5_b_im, m_s5_c_re, m_s5_c_im, m_s5_d, m_s5_glu_w, m_s5_glu_b, m_rw_shift_mu, m_rw_w0, m_rw_w2, m_rw_a0, m_rw_a2, m_rw_g2, m_rw_k_k, m_rw_k_a, m_rw_r_k, m_rw_ln_w, m_rw_ln_b, m_w_out, m_norm_ffn, m_ffn_w1, m_ffn_w3, m_ffn_w2, m_norm_ple, m_ple_gate_w, m_ple_up_w, m_final_norm, v_norm_mix, v_w_in, v_s5_lam_re, v_s5_lam_im, v_s5_log_step, v_s5_b_re, v_s5_b_im, v_s5_c_re, v_s5_c_im, v_s5_d, v_s5_glu_w, v_s5_glu_b, v_rw_shift_mu, v_rw_w0, v_rw_w2, v_rw_a0, v_rw_a2, v_rw_g2, v_rw_k_k, v_rw_k_a, v_rw_r_k, v_rw_ln_w, v_rw_ln_b, v_w_out, v_norm_ffn, v_ffn_w1, v_ffn_w3, v_ffn_w2, v_norm_ple, v_ple_gate_w, v_ple_up_w, v_final_norm))))
```

```python
import functools

import jax
import jax.numpy as jnp
from jax import lax
from jax.experimental import pallas as pl
from jax.experimental.pallas import tpu as pltpu

F32 = jnp.float32
BF16 = jnp.bfloat16

D_MODEL = 1024
S5_WIDTH = 512
RW_WIDTH = 512
S5_GROUP = 16
S5_GROUPS = 32
S5_STATE = 64
S5_LANES = S5_GROUPS * S5_STATE
HEAD = 64
SHIFT_COLS = 1792
IN_COLS = 2304
FFN_HIDDEN = 2816
PLE_DIM = 256
RMS_EPS = 1e-6
GN_EPS = 64e-5
L2_EPS = 1e-12
CHUNK = 64
N_DEV = 8

ADAM_LR = 0.001
ADAM_B1 = 0.9
ADAM_B2 = 0.999
ADAM_EPS = 1e-08
ADAM_WD = 0.01
ADAM_STEP = 10

VMEM_LIMIT = 56 * 1024 * 1024


def _pcall(body, **kw):
    return pl.pallas_call(body, **kw)


def _cparams(n_grid):
    return pltpu.CompilerParams(dimension_semantics=("arbitrary",) * n_grid, vmem_limit_bytes=VMEM_LIMIT)


def _dot(a, b):
    return jnp.dot(a, b, preferred_element_type=F32)


def _dot_nt(a, b):
    return lax.dot_general(a, b, (((1,), (1,)), ((), ())), preferred_element_type=F32)


def _dot_tn(a, b):
    return lax.dot_general(a, b, (((0,), (0,)), ((), ())), preferred_element_type=F32)


def _mmc(w, diff=True):
    if not diff:
        return lambda x: _dot(x.astype(BF16), w)

    @jax.custom_vjp
    def f(x):
        return _dot(x.astype(BF16), w)

    def fwd(x):
        return _dot(x.astype(BF16), w), None

    def bwd(_, dy):
        return (_dot_nt(dy.astype(BF16), w),)

    f.defvjp(fwd, bwd)
    return f


def _split_dot(x, m, n_split):
    acc = None
    rem = x
    for s in range(n_split):
        part = rem.astype(BF16)
        t = _dot(part, m)
        acc = t if acc is None else acc + t
        if s + 1 < n_split:
            rem = rem - part.astype(F32)
    return acc


def _segsum(m, diff=True):
    if not diff:
        return lambda x: _split_dot(x, m, 2)

    @jax.custom_vjp
    def f(x):
        return _split_dot(x, m, 2)

    def fwd(x):
        return _split_dot(x, m, 2), None

    def bwd(_, dy):
        return (_split_dot(dy, m, 2),)

    f.defvjp(fwd, bwd)
    return f


def _head_indicator(n):
    r = lax.broadcasted_iota(jnp.int32, (n, n), 0) // HEAD
    c = lax.broadcasted_iota(jnp.int32, (n, n), 1) // HEAD
    return (r == c).astype(BF16)


def _rms(x, g):
    return x * lax.rsqrt(jnp.mean(x * x, axis=-1, keepdims=True) + RMS_EPS) * g


def _softplus(x):
    return jnp.maximum(x, 0.0) + jnp.log(1.0 + jnp.exp(-jnp.abs(x)))


def _sigmoid(x):
    return 1.0 / (1.0 + jnp.exp(-x))


def _gelu(x):
    return 0.5 * x * (1.0 + jnp.tanh(0.7978845608028654 * (x + 0.044715 * (x * x * x))))


def _tok_call(name, fn, L, TB, tok_in, const_in, tok_out, acc_out=()):
    nb = L // TB
    g8 = TB // 8
    in_specs, args = [], []
    for spec in tok_in:
        arr, width, cb = spec[:3]
        mode = spec[3] if len(spec) > 3 else None
        if mode is None:
            in_specs.append(pl.BlockSpec((TB, width), lambda i, cb=cb: (i, cb)))
        elif mode == "prev":
            in_specs.append(pl.BlockSpec((8, width), lambda i, cb=cb: (jnp.maximum(i * g8 - 1, 0), cb)))
        else:
            in_specs.append(pl.BlockSpec((8, width), lambda i, cb=cb: (jnp.minimum((i + 1) * g8, L // 8 - 1), cb)))
        args.append(arr)
    for c in const_in:
        in_specs.append(pl.BlockSpec(c.shape, lambda i, nd=c.ndim: (0,) * nd))
        args.append(c)
    out_shape, out_specs = [], []
    for width, dt in tok_out:
        out_shape.append(jax.ShapeDtypeStruct((L, width), dt))
        out_specs.append(pl.BlockSpec((TB, width), lambda i: (i, 0)))
    for shp in acc_out:
        out_shape.append(jax.ShapeDtypeStruct(shp, F32))
        out_specs.append(pl.BlockSpec(shp, lambda i, nd=len(shp): (0,) * nd))
    n_tok, n_const, n_to = len(tok_in), len(const_in), len(tok_out)

    def body(*refs):
        i = pl.program_id(0)
        tv = [r[...] for r in refs[:n_tok]]
        cv = [r[...] for r in refs[n_tok:n_tok + n_const]]
        orefs = refs[n_tok + n_const:]
        outs = fn(i, tv, cv)
        for r, v in zip(orefs[:n_to], outs[:n_to]):
            r[...] = v.astype(r.dtype)
        for r, v in zip(orefs[n_to:], outs[n_to:]):
            @pl.when(i == 0)
            def _(r=r):
                r[...] = jnp.zeros(r.shape, r.dtype)

            r[...] += v

    res = _pcall(body, name=name, grid=(nb,), in_specs=in_specs, out_specs=out_specs, out_shape=out_shape,
                 compiler_params=_cparams(1))(*args)
    return res


def _pick_block(n, cap):
    best = None
    for b in range(128, min(n, cap) + 1, 128):
        if n % b == 0:
            best = b
    return best if best is not None else n


def _mm_tn(name, a, b):
    T, M = a.shape
    N = b.shape[1]
    bm, bn, bt = _pick_block(M, 1024), _pick_block(N, 1536), _pick_block(T, 512)

    def body(a_ref, b_ref, o_ref):
        t = pl.program_id(2)

        @pl.when(t == 0)
        def _():
            o_ref[...] = jnp.zeros(o_ref.shape, F32)

        o_ref[...] += _dot_tn(a_ref[...].astype(BF16), b_ref[...].astype(BF16))

    return _pcall(body, name=name, grid=(M // bm, N // bn, T // bt),
                  in_specs=[pl.BlockSpec((bt, bm), lambda m, n, t: (t, m)), pl.BlockSpec((bt, bn), lambda m, n, t: (t, n))],
                  out_specs=pl.BlockSpec((bm, bn), lambda m, n, t: (m, n)),
                  out_shape=jax.ShapeDtypeStruct((M, N), F32), compiler_params=_cparams(3))(a, b)


def _s5_param_fn(lam_re, lam_im, log_step, bt_re, bt_im):
    dt = jnp.exp(log_step)
    e = jnp.exp(lam_re * dt)
    lb_re = e * jnp.cos(lam_im * dt)
    lb_im = e * jnp.sin(lam_im * dt)
    den = lam_re * lam_re + lam_im * lam_im
    nr, ni = lb_re - 1.0, lb_im
    co_re = (nr * lam_re + ni * lam_im) / den
    co_im = (ni * lam_re - nr * lam_im) / den
    cr, ci = co_re[:, None, :], co_im[:, None, :]
    return lb_re, lb_im, cr * bt_re - ci * bt_im, cr * bt_im + ci * bt_re


def _s5_param_fwd(lam_re, lam_im, log_step, bt_re, bt_im):
    def body(a, b, c, d, e, o1, o2, o3, o4):
        r = _s5_param_fn(a[...], b[...], c[...], d[...], e[...])
        o1[...], o2[...], o3[...], o4[...] = r

    sh = jax.ShapeDtypeStruct
    return _pcall(body, name="s5_param_fwd",
                  out_shape=[sh(lam_re.shape, F32), sh(lam_re.shape, F32), sh(bt_re.shape, F32), sh(bt_re.shape, F32)])(
        lam_re, lam_im, log_step, bt_re, bt_im)


def _s5_param_bwd(lam_re, lam_im, log_step, bt_re, bt_im, d_lb_re, d_lb_im, d_bb_re, d_bb_im):
    def body(a, b, c, d, e, g1, g2, g3, g4, o1, o2, o3, o4, o5):
        _, vjp = jax.vjp(_s5_param_fn, a[...], b[...], c[...], d[...], e[...])
        r = vjp((g1[...], g2[...], g3[...], g4[...]))
        o1[...], o2[...], o3[...], o4[...], o5[...] = r

    sh = jax.ShapeDtypeStruct
    return _pcall(body, name="s5_param_bwd",
                  out_shape=[sh(lam_re.shape, F32), sh(lam_re.shape, F32), sh(log_step.shape, F32),
                             sh(bt_re.shape, F32), sh(bt_re.shape, F32)])(
        lam_re, lam_im, log_step, bt_re, bt_im, d_lb_re, d_lb_im, d_bb_re, d_bb_im)


def _cmul(ar, ai, br, bi):
    return ar * br - ai * bi, ar * bi + ai * br


def _scan_consts(lr, li, reverse):
    n = lr.shape[1]
    sub = lax.broadcasted_iota(jnp.int32, (8, n), 0)
    pows = [(lr, li)]
    for _ in range(7):
        pows.append(_cmul(pows[-1][0], pows[-1][1], lr, li))
    steps = []
    for s in (1, 2, 4):
        m = (sub < 8 - s) if reverse else (sub >= s)
        pr, pi = pows[s - 1]
        steps.append((s, jnp.where(m, jnp.broadcast_to(pr, (8, n)), 0.0), jnp.where(m, jnp.broadcast_to(pi, (8, n)), 0.0)))
    wr = jnp.zeros((8, n), F32)
    wi = jnp.zeros((8, n), F32)
    for r in range(8):
        e = (8 - r) if reverse else (r + 1)
        wr = jnp.where(sub == r, jnp.broadcast_to(pows[e - 1][0], (8, n)), wr)
        wi = jnp.where(sub == r, jnp.broadcast_to(pows[e - 1][1], (8, n)), wi)
    return steps, wr, wi


def _scan_rows(sre, sim, carry, lr, li, rows, reverse):
    steps, wr, wi = _scan_consts(lr, li, reverse)
    ng = rows // 8

    def step(gi, _):
        g = (ng - 1 - gi) if reverse else gi
        base = pl.multiple_of(g * 8, 8)
        xr = sre[pl.ds(base, 8), :]
        xi = sim[pl.ds(base, 8), :]
        for s, pr, pi in steps:
            sh = (8 - s) if reverse else s
            yr = pltpu.roll(xr, sh, 0)
            yi = pltpu.roll(xi, sh, 0)
            xr, xi = xr + pr * yr - pi * yi, xi + pr * yi + pi * yr
        cr = carry[0:1, :]
        ci = carry[1:2, :]
        xr, xi = xr + wr * cr - wi * ci, xi + wr * ci + wi * cr
        sre[pl.ds(base, 8), :] = xr
        sim[pl.ds(base, 8), :] = xi
        edge = 0 if reverse else 7
        carry[0:1, :] = xr[edge:edge + 1, :]
        carry[1:2, :] = xi[edge:edge + 1, :]
        return 0

    lax.fori_loop(0, ng, step, 0)


S5_Q = 4
S5_QL = S5_WIDTH // S5_Q
S5_QS = S5_LANES // S5_Q


def _s5_scan_fwd(proj, bq_re, bq_im, cq_re, cq_im, lbar, dskip, L, TB):
    nb = L // TB

    def body(u_ref, bre, bim, cre, cim, lb_ref, d_ref, y_ref, ck_ref, sre, sim, carry):
        i = pl.program_id(0)

        @pl.when(i == 0)
        def _():
            carry[...] = jnp.zeros(carry.shape, F32)

        ck_ref[0] = carry[...]
        u = u_ref[...]
        ub = u.astype(BF16)
        for q in range(S5_Q):
            uq = ub[:, q * S5_QL:(q + 1) * S5_QL]
            sre[:, q * S5_QS:(q + 1) * S5_QS] = _dot(uq, bre[q])
            sim[:, q * S5_QS:(q + 1) * S5_QS] = _dot(uq, bim[q])
        _scan_rows(sre, sim, carry, lb_ref[0:1, :], lb_ref[1:2, :], TB, False)
        for q in range(S5_Q):
            sl = slice(q * S5_QL, (q + 1) * S5_QL)
            ss = slice(q * S5_QS, (q + 1) * S5_QS)
            y_ref[:, sl] = (_dot(sre[:, ss].astype(BF16), cre[q]) - _dot(sim[:, ss].astype(BF16), cim[q])
                            + u[:, sl] * d_ref[:, sl])

    full = lambda a: pl.BlockSpec(a.shape, lambda i, nd=a.ndim: (0,) * nd)
    return _pcall(
        body, name="s5_scan_fwd", grid=(nb,),
        in_specs=[pl.BlockSpec((TB, S5_WIDTH), lambda i: (i, 0)), full(bq_re), full(bq_im), full(cq_re), full(cq_im),
                  full(lbar), full(dskip)],
        out_specs=[pl.BlockSpec((TB, S5_WIDTH), lambda i: (i, 0)), pl.BlockSpec((1, 8, S5_LANES), lambda i: (i, 0, 0))],
        out_shape=[jax.ShapeDtypeStruct((L, S5_WIDTH), F32), jax.ShapeDtypeStruct((nb, 8, S5_LANES), F32)],
        scratch_shapes=[pltpu.VMEM((TB, S5_LANES), F32), pltpu.VMEM((TB, S5_LANES), F32), pltpu.VMEM((8, S5_LANES), F32)],
        compiler_params=_cparams(1))(proj, bq_re, bq_im, cq_re, cq_im, lbar, dskip)


def _s5_scan_bwd(proj, dy, ck, bq_re, bq_im, cq_re, cq_im, lbar, dskip, L, TB):
    nb = L // TB
    ng = TB // 8

    def body(u_ref, dy_ref, ck_ref, bre, bim, cre, cim, lb_ref, d_ref,
             du_ref, dbre, dbim, dcre, dcim, dlb_ref, dd_ref, sre, sim, gre, gim, carry, gcarry):
        i = pl.program_id(0)

        @pl.when(i == 0)
        def _():
            gcarry[...] = jnp.zeros(gcarry.shape, F32)
            dbre[...] = jnp.zeros(dbre.shape, F32)
            dbim[...] = jnp.zeros(dbim.shape, F32)
            dcre[...] = jnp.zeros(dcre.shape, F32)
            dcim[...] = jnp.zeros(dcim.shape, F32)
            dlb_ref[...] = jnp.zeros(dlb_ref.shape, F32)
            dd_ref[...] = jnp.zeros(dd_ref.shape, F32)

        lr = lb_ref[0:1, :]
        li = lb_ref[1:2, :]
        u = u_ref[...]
        ub = u.astype(BF16)
        dy_v = dy_ref[...]
        dyb = dy_v.astype(BF16)
        carry[...] = ck_ref[0]
        for q in range(S5_Q):
            uq = ub[:, q * S5_QL:(q + 1) * S5_QL]
            dq = dyb[:, q * S5_QL:(q + 1) * S5_QL]
            ss = slice(q * S5_QS, (q + 1) * S5_QS)
            sre[:, ss] = _dot(uq, bre[q])
            sim[:, ss] = _dot(uq, bim[q])
            gre[:, ss] = _dot_nt(dq, cre[q])
            gim[:, ss] = -_dot_nt(dq, cim[q])
        _scan_rows(sre, sim, carry, lr, li, TB, False)
        _scan_rows(gre, gim, gcarry, lr, -li, TB, True)

        sub = lax.broadcasted_iota(jnp.int32, (8, S5_LANES), 0)
        c0r = ck_ref[0, 0:1, :]
        c0i = ck_ref[0, 1:2, :]

        def acc_step(g, acc):
            ar, ai = acc
            base = pl.multiple_of(g * 8, 8)
            pbase = pl.multiple_of(jnp.maximum(g - 1, 0) * 8, 8)
            first = g == 0
            lastr = jnp.where(first, c0r, sre[pl.ds(pbase, 8), :][7:8, :])
            lasti = jnp.where(first, c0i, sim[pl.ds(pbase, 8), :][7:8, :])
            spr = jnp.where(sub == 0, jnp.broadcast_to(lastr, sub.shape), pltpu.roll(sre[pl.ds(base, 8), :], 1, 0))
            spi = jnp.where(sub == 0, jnp.broadcast_to(lasti, sub.shape), pltpu.roll(sim[pl.ds(base, 8), :], 1, 0))
            gr = gre[pl.ds(base, 8), :]
            gi_ = gim[pl.ds(base, 8), :]
            return ar + gr * spr + gi_ * spi, ai - gr * spi + gi_ * spr

        z8 = jnp.zeros((8, S5_LANES), F32)
        ar, ai = lax.fori_loop(0, ng, acc_step, (z8, z8))
        dlb_ref[0:1, :] += jnp.sum(ar, axis=0, keepdims=True)
        dlb_ref[1:2, :] += jnp.sum(ai, axis=0, keepdims=True)

        dd_ref[...] += jnp.sum(dy_v * u, axis=0, keepdims=True)
        for q in range(S5_Q):
            sl = slice(q * S5_QL, (q + 1) * S5_QL)
            ss = slice(q * S5_QS, (q + 1) * S5_QS)
            grq = gre[:, ss].astype(BF16)
            giq = gim[:, ss].astype(BF16)
            du_ref[:, sl] = _dot_nt(grq, bre[q]) + _dot_nt(giq, bim[q]) + dy_v[:, sl] * d_ref[:, sl]
            dbre[q] += _dot_tn(ub[:, sl], grq)
            dbim[q] += _dot_tn(ub[:, sl], giq)
            dcre[q] += _dot_tn(sre[:, ss].astype(BF16), dyb[:, sl])
            dcim[q] -= _dot_tn(sim[:, ss].astype(BF16), dyb[:, sl])

    full = lambda a: pl.BlockSpec(a.shape, lambda i, nd=a.ndim: (0,) * nd)
    rev = lambda i: (nb - 1 - i, 0)
    sh = jax.ShapeDtypeStruct
    outs = [sh((L, S5_WIDTH), F32), sh(bq_re.shape, F32), sh(bq_im.shape, F32), sh(cq_re.shape, F32), sh(cq_im.shape, F32),
            sh((8, S5_LANES), F32), sh((1, S5_WIDTH), F32)]
    fo = lambda s: pl.BlockSpec(s.shape, lambda i, nd=len(s.shape): (0,) * nd)
    return _pcall(
        body, name="s5_scan_bwd", grid=(nb,),
        in_specs=[pl.BlockSpec((TB, S5_WIDTH), rev), pl.BlockSpec((TB, S5_WIDTH), rev),
                  pl.BlockSpec((1, 8, S5_LANES), lambda i: (nb - 1 - i, 0, 0)),
                  full(bq_re), full(bq_im), full(cq_re), full(cq_im), full(lbar), full(dskip)],
        out_specs=[pl.BlockSpec((TB, S5_WIDTH), rev)] + [fo(s) for s in outs[1:]],
        out_shape=outs,
        scratch_shapes=[pltpu.VMEM((TB, S5_LANES), F32)] * 4 + [pltpu.VMEM((8, S5_LANES), F32)] * 2,
        compiler_params=_cparams(1))(proj, dy, ck, bq_re, bq_im, cq_re, cq_im, lbar, dskip)


N_HEAD = RW_WIDTH // HEAD
_NN = (((1,), (0,)), ((), ()))
_NT = (((1,), (1,)), ((), ()))
_TN = (((0,), (0,)), ((), ()))


def _hi_lo(x):
    h = x.astype(BF16)
    return h, (x - h.astype(F32)).astype(BF16)


def _mm_acc(a, b, dims):
    ah, al = _hi_lo(a)
    bh, bl = _hi_lo(b)
    dg = lambda p, q: lax.dot_general(p, q, dims, preferred_element_type=F32)
    return dg(ah, bh) + dg(ah, bl) + dg(al, bh)


def _cumsum_rows(x, transpose):
    n = x.shape[0]
    ti = lax.broadcasted_iota(jnp.int32, (n, n), 0)
    tj = lax.broadcasted_iota(jnp.int32, (n, n), 1)
    m = ((tj >= ti) if transpose else (tj <= ti)).astype(BF16)
    acc, rem = None, x
    for s in range(3):
        part = rem.astype(BF16)
        t = _dot(m, part)
        acc = t if acc is None else acc + t
        if s < 2:
            rem = rem - part.astype(F32)
    return acc


def _chunk_ops(diff):
    if not diff:
        return (lambda a, b: _mm_acc(a, b, _NN), lambda a, b: _mm_acc(a, b, _NT), lambda a, b: _mm_acc(a, b, _TN),
                lambda x: _cumsum_rows(x, False))

    @jax.custom_vjp
    def nn(a, b):
        return _mm_acc(a, b, _NN)

    nn.defvjp(lambda a, b: (_mm_acc(a, b, _NN), (a, b)), lambda r, d: (_mm_acc(d, r[1], _NT), _mm_acc(r[0], d, _TN)))

    @jax.custom_vjp
    def nt(a, b):
        return _mm_acc(a, b, _NT)

    nt.defvjp(lambda a, b: (_mm_acc(a, b, _NT), (a, b)), lambda r, d: (_mm_acc(d, r[1], _NN), _mm_acc(d, r[0], _TN)))

    @jax.custom_vjp
    def tn(a, b):
        return _mm_acc(a, b, _TN)

    tn.defvjp(lambda a, b: (_mm_acc(a, b, _TN), (a, b)), lambda r, d: (_mm_acc(r[1], d, _NT), _mm_acc(r[0], d, _NN)))

    @jax.custom_vjp
    def cums(x):
        return _cumsum_rows(x, False)

    cums.defvjp(lambda x: (_cumsum_rows(x, False), None), lambda _, d: (_cumsum_rows(d, True),))
    return nn, nt, tn, cums


def _wkv_chunk(s0, r, w, k, v, a, b, ops):
    nn, nt, tn, cums = ops
    n = r.shape[0]
    ti = lax.broadcasted_iota(jnp.int32, (n, n), 0)
    tj = lax.broadcasted_iota(jnp.int32, (n, n), 1)
    incl, strict = tj <= ti, tj < ti
    logw = jnp.log(w)
    cum = cums(logw)
    g_in, g_ex, g_inv = jnp.exp(cum), jnp.exp(cum - logw), jnp.exp(-cum)
    ae, re, bi, ki = a * g_ex, r * g_in, b * g_inv, k * g_inv
    tab = jnp.where(strict, nt(ae, bi), 0.0)
    tak = jnp.where(strict, nt(ae, ki), 0.0)
    qb = jnp.where(incl, nt(re, bi), 0.0)
    qk = jnp.where(incl, nt(re, ki), 0.0)
    u = nt(ae, s0) + nn(tak, v)
    npow = tab
    steps = max(1, (n - 1).bit_length())
    for i in range(steps):
        u = u + nn(npow, u)
        if i + 1 < steps:
            npow = nn(npow, npow)
    y = nt(re, s0) + nn(qb, u) + nn(qk, v)
    g_end = jnp.exp(jnp.sum(logw, axis=0, keepdims=True))
    s1 = s0 * g_end + tn(u, bi * g_end) + tn(v, ki * g_end)
    return y, s1


def _wkv_fwd(r, w, k, v, a, b, L):
    nc = L // CHUNK

    def body(r_ref, w_ref, k_ref, v_ref, a_ref, b_ref, y_ref, ck_ref, s_ref):
        c = pl.program_id(0)

        @pl.when(c == 0)
        def _():
            s_ref[...] = jnp.zeros(s_ref.shape, F32)

        ops = _chunk_ops(False)
        for h in range(N_HEAD):
            s0 = s_ref[h]
            ck_ref[0, h] = s0
            y, s1 = _wkv_chunk(s0, r_ref[h], w_ref[h], k_ref[h], v_ref[h], a_ref[h], b_ref[h], ops)
            y_ref[h] = y
            s_ref[h] = s1

    blk = pl.BlockSpec((N_HEAD, CHUNK, HEAD), lambda c: (0, c, 0))
    return _pcall(
        body, name="wkv_fwd", grid=(nc,), in_specs=[blk] * 6,
        out_specs=[blk, pl.BlockSpec((1, N_HEAD, HEAD, HEAD), lambda c: (c, 0, 0, 0))],
        out_shape=[jax.ShapeDtypeStruct((N_HEAD, L, HEAD), F32), jax.ShapeDtypeStruct((nc, N_HEAD, HEAD, HEAD), F32)],
        scratch_shapes=[pltpu.VMEM((N_HEAD, HEAD, HEAD), F32)],
        compiler_params=_cparams(1))(r, w, k, v, a, b)


def _wkv_bwd(r, w, k, v, a, b, dy, ck, L):
    nc = L // CHUNK

    def body(r_ref, w_ref, k_ref, v_ref, a_ref, b_ref, dy_ref, ck_ref,
             dr_ref, dw_ref, dk_ref, dv_ref, da_ref, db_ref, ds_ref):
        c = pl.program_id(0)

        @pl.when(c == 0)
        def _():
            ds_ref[...] = jnp.zeros(ds_ref.shape, F32)

        ops = _chunk_ops(True)
        for h in range(N_HEAD):
            _, vjp = jax.vjp(lambda *t: _wkv_chunk(*t, ops), ck_ref[0, h], r_ref[h], w_ref[h], k_ref[h], v_ref[h],
                             a_ref[h], b_ref[h])
            g = vjp((dy_ref[h], ds_ref[h]))
            ds_ref[h] = g[0]
            for o_ref, val in zip((dr_ref, dw_ref, dk_ref, dv_ref, da_ref, db_ref), g[1:]):
                o_ref[h] = val

    blk = pl.BlockSpec((N_HEAD, CHUNK, HEAD), lambda c: (0, nc - 1 - c, 0))
    sh = jax.ShapeDtypeStruct((N_HEAD, L, HEAD), F32)
    return _pcall(
        body, name="wkv_bwd", grid=(nc,),
        in_specs=[blk] * 7 + [pl.BlockSpec((1, N_HEAD, HEAD, HEAD), lambda c: (nc - 1 - c, 0, 0, 0))],
        out_specs=[blk] * 6, out_shape=[sh] * 6,
        scratch_shapes=[pltpu.VMEM((N_HEAD, HEAD, HEAD), F32)],
        compiler_params=_cparams(1))(r, w, k, v, a, b, dy, ck)


TB = 256


def _bf(x):
    return x.astype(BF16)


def _inproj_fwd(x, norm_mix, w_in, L):
    def fn(i, tv, cv):
        xn = _rms(tv[0], cv[0])
        return _dot(_bf(xn), cv[1]), xn

    return _tok_call("inproj_fwd", fn, L, TB, [(x, D_MODEL, 0)], [norm_mix, w_in], [(IN_COLS, F32), (D_MODEL, BF16)])


def _s5_post_fn(glu_w, wtop, diff=True):
    mg = _mmc(glu_w, diff)
    mt = _mmc(wtop, diff) if wtop is not None else None

    def f(y, glu_b, e):
        z = _gelu(y)
        out = z * _sigmoid(mg(z) + glu_b + e)
        res = mt(out) if mt is not None else out
        return res, (z, out)

    return f


def _s5_post_fwd(y, glu_w, glu_b, L):
    def fn(i, tv, cv):
        out, _ = _s5_post_fn(cv[0], None, False)(tv[0], cv[1], 0.0)
        return (out,)

    return _tok_call("s5_post_fwd", fn, L, TB, [(y, S5_WIDTH, 0)], [glu_w, glu_b], [(S5_WIDTH, F32)])[0]


def _s5_post_bwd(y, dh1, glu_w, glu_b, wtop, L):
    def fn(i, tv, cv):
        e0 = jnp.zeros((TB, S5_WIDTH), F32)
        _, vjp, (z, out) = jax.vjp(_s5_post_fn(cv[0], cv[2]), tv[0], cv[1], e0, has_aux=True)
        dy, db, de = vjp(tv[1])
        return dy, z, de, out, db

    return _tok_call("s5_post_bwd", fn, L, TB, [(y, S5_WIDTH, 0), (dh1, D_MODEL, 0)], [glu_w, glu_b, wtop],
                     [(S5_WIDTH, F32), (S5_WIDTH, BF16), (S5_WIDTH, BF16), (S5_WIDTH, BF16)], [(1, S5_WIDTH)])


RW_COLBLK = ((RW_WIDTH, 1), (RW_WIDTH, 2), (RW_WIDTH, 3), (128, 16), (128, 17))
RW_MU = ((0, 512), (512, 1024), (1024, 1536), (1536, 1664), (1664, 1792))


def _rw_pre_fn(w2pad, a2pad, g2, diff=True):
    m_w, m_a, m_g = _mmc(w2pad, diff), _mmc(a2pad, diff), _mmc(g2, diff)
    seg = _segsum(_head_indicator(RW_WIDTH), diff)

    def f(zr, zk, zv, zwa, zg, w0, a0, k_k, k_a, e_w, e_a):
        wl_t = jnp.tanh(zwa)
        wlin = w0 + m_w(wl_t) + e_w
        w = -_softplus(-wlin) - 0.5
        decay = jnp.exp(-jnp.exp(w))
        a = _sigmoid(a0 + m_a(zwa) + e_a)
        sg = _sigmoid(zg)
        g = m_g(sg)
        kk = zk * k_k
        kkn = kk / jnp.maximum(jnp.sqrt(seg(kk * kk)), L2_EPS)
        kf = zk * (1.0 + (a - 1.0) * k_a)
        return (zr, decay, kf, zv, -kkn, kkn * a, g), (wl_t, sg)

    return f


def _rw_shifted(i, tv, mu):
    sub = lax.broadcasted_iota(jnp.int32, (TB, 1), 0)
    zs, dif = [], []
    for n in range(5):
        z = tv[n]
        last = jnp.where(i == 0, 0.0, tv[5 + n][7:8, :])
        prev = jnp.where(sub == 0, last, pltpu.roll(z, 1, 0))
        m = mu[:, RW_MU[n][0]:RW_MU[n][1]]
        zs.append(z + (prev - z) * m)
        dif.append(prev - z)
    return zs, dif


def _rw_tok_in(proj):
    return [(proj, wd, cb) for wd, cb in RW_COLBLK] + [(proj, wd, cb, "prev") for wd, cb in RW_COLBLK]


def _rw_pre_fwd(proj, mu, w0, a0, k_k, k_a, w2pad, a2pad, g2, L):
    def fn(i, tv, cv):
        zs, _ = _rw_shifted(i, tv, cv[0])
        outs, _ = _rw_pre_fn(cv[5], cv[6], cv[7], False)(*zs, cv[1], cv[2], cv[3], cv[4], 0.0, 0.0)
        return outs

    return _tok_call("rw_pre_fwd", fn, L, TB, _rw_tok_in(proj), [mu, w0, a0, k_k, k_a, w2pad, a2pad, g2],
                     [(RW_WIDTH, F32)] * 7)


def _rw_pre_bwd(proj, cots, mu, w0, a0, k_k, k_a, w2pad, a2pad, g2, L):
    def fn(i, tv, cv):
        zs, dif = _rw_shifted(i, tv[:10], cv[0])
        dr1, dr2, dw, dk1, dk2, dv1, dv2, da, db, dg = tv[10:]
        e0 = jnp.zeros((TB, RW_WIDTH), F32)
        _, vjp, (wl_t, sg) = jax.vjp(_rw_pre_fn(cv[5], cv[6], cv[7]), *zs, cv[1], cv[2], cv[3], cv[4], e0, e0, has_aux=True)
        g = vjp((dr1 + dr2, dw, dk1 + dk2, dv1 + dv2, da, db, dg))
        dzs = jnp.concatenate(g[:5], axis=1)
        dmu = jnp.concatenate([jnp.sum(g[n] * dif[n], axis=0, keepdims=True) for n in range(5)], axis=1)
        return dzs, wl_t, zs[3], sg, g[9], g[10], dmu, g[5], g[6], g[7], g[8]

    tok_in = _rw_tok_in(proj) + [(c, RW_WIDTH, 0) for c in cots]
    return _tok_call("rw_pre_bwd", fn, L, TB, tok_in, [mu, w0, a0, k_k, k_a, w2pad, a2pad, g2],
                     [(SHIFT_COLS, F32), (128, BF16), (128, BF16), (128, BF16), (RW_WIDTH, BF16), (RW_WIDTH, BF16)],
                     [(1, SHIFT_COLS)] + [(1, RW_WIDTH)] * 4)


def _rw_post_fn(wbot, diff=True):
    seg = _segsum(_head_indicator(RW_WIDTH), diff)
    mb = _mmc(wbot, diff) if wbot is not None else None

    def f(y, r, kf, v, g, ln_w, ln_b, r_k):
        mean = seg(y) * (1.0 / HEAD)
        yc = y - mean
        var = seg(yc * yc) * (1.0 / HEAD)
        yn = yc * lax.rsqrt(var + GN_EPS) * ln_w + ln_b
        bonus = seg(r * kf * r_k) * v
        out = (yn + bonus) * g
        res = mb(out) if mb is not None else out
        return res, out

    return f


def _rw_post_fwd(y, r, kf, v, g, ln_w, ln_b, r_k, L):
    def fn(i, tv, cv):
        out, _ = _rw_post_fn(None, False)(*tv, *cv)
        return (out,)

    return _tok_call("rw_post_fwd", fn, L, TB, [(t, RW_WIDTH, 0) for t in (y, r, kf, v, g)], [ln_w, ln_b, r_k],
                     [(RW_WIDTH, F32)])[0]


def _rw_post_bwd(y, r, kf, v, g, dh1, ln_w, ln_b, r_k, wbot, L):
    def fn(i, tv, cv):
        _, vjp, out = jax.vjp(_rw_post_fn(cv[3]), *tv[:5], cv[0], cv[1], cv[2], has_aux=True)
        gr = vjp(tv[5])
        return gr[0], gr[1], gr[2], gr[3], gr[4], out, gr[5], gr[6], gr[7]

    return _tok_call("rw_post_bwd", fn, L, TB, [(t, RW_WIDTH, 0) for t in (y, r, kf, v, g)] + [(dh1, D_MODEL, 0)],
                     [ln_w, ln_b, r_k, wbot], [(RW_WIDTH, F32)] * 5 + [(RW_WIDTH, BF16)], [(1, RW_WIDTH)] * 3)


def _ffn_fn(w1, w3, w2, diff=True):
    m1, m3, m2 = _mmc(w1, diff), _mmc(w3, diff), _mmc(w2, diff)

    def f(h1, norm_ffn, e1, e3):
        hn = _rms(h1, norm_ffn)
        a1 = m1(hn) + e1
        a3 = m3(hn) + e3
        hm = a1 * _sigmoid(a1) * a3
        return h1 + m2(hm), (hn, hm)

    return f


TB_FFN = 256


def _mixffn_fwd(x, s5_out, rw_out, wtop, wbot, norm_ffn, w1, w3, w2, L):
    def fn(i, tv, cv):
        h1 = tv[0] + _dot(_bf(tv[1]), cv[0]) + _dot(_bf(tv[2]), cv[1])
        h2, _ = _ffn_fn(cv[3], cv[4], cv[5], False)(h1, cv[2], 0.0, 0.0)
        return h1, h2

    return _tok_call("mixffn_fwd", fn, L, TB_FFN, [(x, D_MODEL, 0), (s5_out, S5_WIDTH, 0), (rw_out, RW_WIDTH, 0)],
                     [wtop, wbot, norm_ffn, w1, w3, w2], [(D_MODEL, F32), (D_MODEL, F32)])


def _ffn_bwd(h1, dh2, norm_ffn, w1, w3, w2, L):
    def fn(i, tv, cv):
        e0 = jnp.zeros((TB_FFN, FFN_HIDDEN), F32)
        _, vjp, (hn, hm) = jax.vjp(_ffn_fn(cv[1], cv[2], cv[3]), tv[0], cv[0], e0, e0, has_aux=True)
        dh1, dn, d1, d3 = vjp(tv[1])
        return dh1, d1, d3, hm, hn, dn

    return _tok_call("ffn_bwd", fn, L, TB_FFN, [(h1, D_MODEL, 0), (dh2, D_MODEL, 0)], [norm_ffn, w1, w3, w2],
                     [(D_MODEL, F32), (FFN_HIDDEN, BF16), (FFN_HIDDEN, BF16), (FFN_HIDDEN, BF16), (D_MODEL, BF16)],
                     [(1, D_MODEL)])


def _ple_loss_fb(h2, p, target, norm_ple, final_norm, wg, wu, L):
    def fn(i, tv, cv):
        mgate, mup = _mmc(cv[2]), _mmc(cv[3], False)

        def f(h2_, norm_ple_, final_norm_, eg, eu):
            hn = _rms(h2_, norm_ple_)
            gate = _sigmoid(mgate(hn) + eg)
            h3 = h2_ + gate * (mup(tv[1]) + eu)
            out = _rms(h3, final_norm_)
            d = out - tv[2]
            return 0.5 * jnp.sum(jnp.mean(d * d, axis=-1, keepdims=True)), hn

        e0 = jnp.zeros((TB, D_MODEL), F32)
        loss, vjp, hn = jax.vjp(f, tv[0], cv[0], cv[1], e0, e0, has_aux=True)
        dh2, dnp, dfn, deg, deu = vjp(jnp.ones((), F32))
        return dh2, deg, deu, hn, jnp.full((8, 128), loss, F32), dnp, dfn

    return _tok_call("ple_loss_fb", fn, L, TB, [(h2, D_MODEL, 0), (p, PLE_DIM, 0), (target, D_MODEL, 0)],
                     [norm_ple, final_norm, wg, wu], [(D_MODEL, F32), (D_MODEL, BF16), (D_MODEL, BF16), (D_MODEL, BF16)],
                     [(8, 128), (1, D_MODEL), (1, D_MODEL)])


def _inproj_bwd(x, dh1, du, dzs, norm_mix, mu, w_u, w_z, L):
    nb = L // TB

    def fn(i, tv, cv):
        sub = lax.broadcasted_iota(jnp.int32, (TB, 1), 0)
        m = cv[1]
        b = tv[3] * m
        nxt = jnp.where(i == nb - 1, 0.0, tv[4][0:1, :] * m)
        dz = tv[3] * (1.0 - m) + jnp.where(sub == TB - 1, nxt, pltpu.roll(b, TB - 1, 0))
        dub, dzb = _bf(tv[2]), _bf(dz)
        dxn = _dot_nt(dub, cv[2]) + _dot_nt(dzb, cv[3])
        _, vjp = jax.vjp(_rms, tv[0], cv[0])
        dx, dn = vjp(dxn)
        return tv[1] + dx, jnp.concatenate([dub, dzb], axis=1), dn

    return _tok_call("inproj_bwd", fn, L, TB,
                     [(x, D_MODEL, 0), (dh1, D_MODEL, 0), (du, S5_WIDTH, 0), (dzs, SHIFT_COLS, 0), (dzs, SHIFT_COLS, 0, "next")],
                     [norm_mix, mu, w_u, w_z], [(D_MODEL, F32), (IN_COLS, BF16)], [(1, D_MODEL)])


def _eye8(dt):
    return jnp.eye(8, dtype=dt)


def _quarter_b(bb):
    return jnp.einsum("hg,qgcp->qhcgp", _eye8(bb.dtype), bb.reshape(S5_Q, 8, S5_GROUP, S5_STATE)).reshape(S5_Q, S5_QL, S5_QS)


def _unquarter_b(d):
    return jnp.einsum("qhcgp,hg->qgcp", d.reshape(S5_Q, 8, S5_GROUP, 8, S5_STATE), _eye8(d.dtype)).reshape(
        S5_GROUPS, S5_GROUP, S5_STATE)


def _quarter_c(c):
    return jnp.einsum("gh,qgcp->qgphc", _eye8(c.dtype), c.reshape(S5_Q, 8, S5_GROUP, S5_STATE)).reshape(S5_Q, S5_QS, S5_QL)


def _unquarter_c(d):
    return jnp.einsum("qgphc,gh->qgcp", d.reshape(S5_Q, 8, S5_STATE, 8, S5_GROUP), _eye8(d.dtype)).reshape(
        S5_GROUPS, S5_GROUP, S5_STATE)


def _pad_rows(w, lo, n):
    return jnp.zeros((n, w.shape[1]), w.dtype).at[lo:lo + w.shape[0]].set(w)


def _local_step(x, p, target, W):
    L = x.shape[0]
    r2 = lambda v: v.reshape(1, -1)
    w_in = W["w_in"]
    wtop, wbot = W["w_out"][:S5_WIDTH], W["w_out"][S5_WIDTH:]
    w2pad = _pad_rows(W["rw_w2"], 0, 128)
    a2pad = _pad_rows(W["rw_a2"], 64, 128)
    mu = r2(W["rw_shift_mu"])
    rw_vec = [r2(W[n]) for n in ("rw_w0", "rw_a0", "rw_k_k", "rw_k_a")]
    ln_w, ln_b, r_k = r2(W["rw_ln_w"]), r2(W["rw_ln_b"]), r2(W["rw_r_k"])

    lam_re, lam_im = W["s5_lam_re"], W["s5_lam_im"]
    log_step = W["s5_log_step"].reshape(S5_GROUPS, 1)
    bt_re, bt_im = W["s5_b_re"].transpose(0, 2, 1), W["s5_b_im"].transpose(0, 2, 1)
    lb_re, lb_im, bb_re, bb_im = _s5_param_fwd(lam_re, lam_im, log_step, bt_re, bt_im)
    bq_re, bq_im = _quarter_b(bb_re).astype(BF16), _quarter_b(bb_im).astype(BF16)
    cq_re, cq_im = _quarter_c(W["s5_c_re"]).astype(BF16), _quarter_c(W["s5_c_im"]).astype(BF16)
    lbar = jnp.concatenate([lb_re.reshape(1, -1), lb_im.reshape(1, -1), jnp.zeros((6, S5_LANES), F32)], axis=0)
    dskip = r2(W["s5_d"])
    glu_b = r2(W["s5_glu_b"])
    norm_mix, norm_ffn, norm_ple, final_norm = (r2(W[n]) for n in ("norm_mix", "norm_ffn", "norm_ple", "final_norm"))

    proj, xn = _inproj_fwd(x, norm_mix, w_in, L)
    y_s5, ck5 = _s5_scan_fwd(proj, bq_re, bq_im, cq_re, cq_im, lbar, dskip, L, TB)
    s5_out = _s5_post_fwd(y_s5, W["s5_glu_w"], glu_b, L)
    r, wd, kf, v, a_s, b_s, g = _rw_pre_fwd(proj, mu, *rw_vec, w2pad, a2pad, W["rw_g2"], L)
    heads = lambda t: t.reshape(L, N_HEAD, HEAD).transpose(1, 0, 2)
    flat = lambda t: t.transpose(1, 0, 2).reshape(L, RW_WIDTH)
    scan_in = tuple(heads(t) for t in (r, wd, kf, v, a_s, b_s))
    y_h, ckw = _wkv_fwd(*scan_in, L)
    y_wkv = flat(y_h)
    rw_out = _rw_post_fwd(y_wkv, r, kf, v, g, ln_w, ln_b, r_k, L)
    h1, h2 = _mixffn_fwd(x, s5_out, rw_out, wtop, wbot, norm_ffn, W["ffn_w1"], W["ffn_w3"], W["ffn_w2"], L)

    G = {}
    dh2, deg, deu, hn_ple, loss_acc, G["norm_ple"], G["final_norm"] = _ple_loss_fb(
        h2, p, target, norm_ple, final_norm, W["ple_gate_w"], W["ple_up_w"], L)
    G["ple_gate_w"] = _mm_tn("dw_ple_gate", hn_ple, deg)
    G["ple_up_w"] = _mm_tn("dw_ple_up", p, deu)
    dh1, da1, da3, hm, hn_ffn, G["norm_ffn"] = _ffn_bwd(h1, dh2, norm_ffn, W["ffn_w1"], W["ffn_w3"], W["ffn_w2"], L)
    G["ffn_w1"] = _mm_tn("dw_ffn_w1", hn_ffn, da1)
    G["ffn_w3"] = _mm_tn("dw_ffn_w3", hn_ffn, da3)
    G["ffn_w2"] = _mm_tn("dw_ffn_w2", hm, dh2)
    dy_s5, z_bf, dgp, s5o_bf, G["s5_glu_b"] = _s5_post_bwd(y_s5, dh1, W["s5_glu_w"], glu_b, wtop, L)
    G["s5_glu_w"] = _mm_tn("dw_s5_glu", z_bf, dgp)
    dy_wkv, dr2, dk2, dv2, dg, rwo_bf, G["rw_ln_w"], G["rw_ln_b"], G["rw_r_k"] = _rw_post_bwd(
        y_wkv, r, kf, v, g, dh1, ln_w, ln_b, r_k, wbot, L)
    G["w_out"] = jnp.concatenate([_mm_tn("dw_out_top", s5o_bf, dh1), _mm_tn("dw_out_bot", rwo_bf, dh1)], axis=0)
    dr1, dwd, dk1, dv1, da_s, db_s = (flat(t) for t in _wkv_bwd(*scan_in, heads(dy_wkv), ckw, L))
    (dzs, wl_t, zwa, sg, dwlin, dalin, G["rw_shift_mu"], G["rw_w0"], G["rw_a0"], G["rw_k_k"], G["rw_k_a"]) = _rw_pre_bwd(
        proj, (dr1, dr2, dwd, dk1, dk2, dv1, dv2, da_s, db_s, dg), mu, *rw_vec, w2pad, a2pad, W["rw_g2"], L)
    G["rw_w2"] = _mm_tn("dw_rw_w2", wl_t, dwlin)[:64]
    G["rw_a2"] = _mm_tn("dw_rw_a2", zwa, dalin)[64:]
    G["rw_g2"] = _mm_tn("dw_rw_g2", sg, dg)
    du, dbq_re, dbq_im, dcq_re, dcq_im, dlbar, G["s5_d"] = _s5_scan_bwd(
        proj, dy_s5, ck5, bq_re, bq_im, cq_re, cq_im, lbar, dskip, L, TB)
    G["s5_c_re"], G["s5_c_im"] = _unquarter_c(dcq_re), _unquarter_c(dcq_im)
    d_lam_re, d_lam_im, d_ls, d_bt_re, d_bt_im = _s5_param_bwd(
        lam_re, lam_im, log_step, bt_re, bt_im, dlbar[0].reshape(S5_GROUPS, S5_STATE), dlbar[1].reshape(S5_GROUPS, S5_STATE),
        _unquarter_b(dbq_re), _unquarter_b(dbq_im))
    G["s5_lam_re"], G["s5_lam_im"], G["s5_log_step"] = d_lam_re, d_lam_im, d_ls.reshape(S5_GROUPS)
    G["s5_b_re"], G["s5_b_im"] = d_bt_re.transpose(0, 2, 1), d_bt_im.transpose(0, 2, 1)
    dx, dproj, G["norm_mix"] = _inproj_bwd(x, dh1, du, dzs, norm_mix, mu, w_in[:, :S5_WIDTH], w_in[:, S5_WIDTH:], L)
    G["w_in"] = _mm_tn("dw_in", xn, dproj)
    return loss_acc[0, 0], dx, G


MESH_AXES = ("x", "y", "c")
_ANY = pl.BlockSpec(memory_space=pl.ANY)


def _all_gather(name, shard):
    m_per, n = shard.shape

    def body(x_ref, out_ref, send_sems, recv_sems, local_sem):
        x, y, c = lax.axis_index("x"), lax.axis_index("y"), lax.axis_index("c")
        me, sibling = (x, y, c), (x, y, 1 - c)
        chips = [(1 - x, y), (x, 1 - y), (1 - x, 1 - y)]

        def rows(px, py, pc):
            return out_ref.at[pl.ds((4 * px + 2 * py + pc) * m_per, m_per), :]

        def copy(k, block, to, src=None):
            return pltpu.make_async_remote_copy(
                src_ref=rows(*block) if src is None else src, dst_ref=rows(*block),
                send_sem=send_sems.at[k], recv_sem=recv_sems.at[k], device_id=to, device_id_type=pl.DeviceIdType.MESH)

        mine = pltpu.make_async_copy(x_ref, rows(*me), local_sem)
        mine.start()
        first = [copy(0, me, sibling, src=x_ref)]
        first += [copy(1 + j, me, (*chip, c), src=x_ref) for j, chip in enumerate(chips)]
        for cp in first:
            cp.start()
        passed = [copy(4 + j, (*chip, c), sibling) for j, chip in enumerate(chips)]
        for j, chip in enumerate(chips):
            copy(1 + j, (*chip, c), me).wait_recv()
            passed[j].start()
        copy(0, sibling, me).wait_recv()
        for j, chip in enumerate(chips):
            copy(4 + j, (*chip, 1 - c), me).wait_recv()
        for cp in first + passed:
            cp.wait_send()
        mine.wait()

    return _pcall(body, name=name, out_shape=jax.ShapeDtypeStruct((N_DEV * m_per, n), shard.dtype),
                  in_specs=[_ANY], out_specs=_ANY,
                  scratch_shapes=[pltpu.SemaphoreType.DMA((7,)), pltpu.SemaphoreType.DMA((7,)), pltpu.SemaphoreType.DMA(())])(shard)


def _exchange_slabs(name, slabs):
    _, m, n = slabs.shape

    def body(x_ref, out_ref, send_sems, recv_sems, local_sem):
        x, y, c = lax.axis_index("x"), lax.axis_index("y"), lax.axis_index("c")
        me = 4 * x + 2 * y + c
        mine = pltpu.make_async_copy(x_ref.at[me], out_ref.at[me], local_sem)
        mine.start()

        def copy(k):
            px, py, pc = x ^ ((k >> 2) & 1), y ^ ((k >> 1) & 1), c ^ (k & 1)
            return pltpu.make_async_remote_copy(
                src_ref=x_ref.at[4 * px + 2 * py + pc], dst_ref=out_ref.at[me],
                send_sem=send_sems.at[k - 1], recv_sem=recv_sems.at[k - 1],
                device_id=(px, py, pc), device_id_type=pl.DeviceIdType.MESH)

        def landing(k):
            px, py, pc = x ^ ((k >> 2) & 1), y ^ ((k >> 1) & 1), c ^ (k & 1)
            return pltpu.make_async_remote_copy(
                src_ref=x_ref.at[me], dst_ref=out_ref.at[4 * px + 2 * py + pc],
                send_sem=send_sems.at[k - 1], recv_sem=recv_sems.at[k - 1],
                device_id=(px, py, pc), device_id_type=pl.DeviceIdType.MESH)

        for k in range(1, N_DEV):
            copy(k).start()
        for k in range(1, N_DEV):
            landing(k).wait_recv()
        for k in range(1, N_DEV):
            copy(k).wait_send()
        mine.wait()

    return _pcall(body, name=name, out_shape=jax.ShapeDtypeStruct(slabs.shape, slabs.dtype), in_specs=[_ANY], out_specs=_ANY,
                  scratch_shapes=[pltpu.SemaphoreType.DMA((7,)), pltpu.SemaphoreType.DMA((7,)), pltpu.SemaphoreType.DMA(())])(slabs)


def _adamw(name, parts, w, m, v, rb):
    _, R, N = parts.shape

    def body(p_ref, w_ref, m_ref, v_ref, g_ref, d_ref, nm_ref, nv_ref):
        g = p_ref[0]
        for s in range(1, N_DEV):
            g = g + p_ref[s]
        nm = ADAM_B1 * m_ref[...] + (1.0 - ADAM_B1) * g
        nv = ADAM_B2 * v_ref[...] + (1.0 - ADAM_B2) * (g * g)
        m_hat = nm / (1.0 - ADAM_B1 ** ADAM_STEP)
        v_hat = nv / (1.0 - ADAM_B2 ** ADAM_STEP)
        g_ref[...] = g
        d_ref[...] = -ADAM_LR * (m_hat / (jnp.sqrt(v_hat) + ADAM_EPS) + ADAM_WD * w_ref[...])
        nm_ref[...] = nm
        nv_ref[...] = nv

    blk = pl.BlockSpec((rb, N), lambda i: (i, 0))
    sh = jax.ShapeDtypeStruct((R, N), F32)
    return _pcall(body, name=name, grid=(R // rb,), in_specs=[pl.BlockSpec((N_DEV, rb, N), lambda i: (0, i, 0)), blk, blk, blk],
                  out_specs=[blk] * 4, out_shape=[sh] * 4, compiler_params=_cparams(1))(parts, w, m, v)


BIG = (("w_in", 1), ("s5_glu_w", 0), ("rw_w2", 1), ("rw_a2", 1), ("rw_g2", 1), ("w_out", 0), ("ffn_w1", 1), ("ffn_w3", 1),
       ("ffn_w2", 0), ("ple_gate_w", 0), ("ple_up_w", 1))
BIG_NAMES = tuple(n for n, _ in BIG)
PACK_COLS = 1024
SMALL_ROWS = 144
WEIGHT_NAMES = ("norm_mix", "w_in", "s5_lam_re", "s5_lam_im", "s5_log_step", "s5_b_re", "s5_b_im", "s5_c_re", "s5_c_im", "s5_d",
                "s5_glu_w", "s5_glu_b", "rw_shift_mu", "rw_w0", "rw_w2", "rw_a0", "rw_a2", "rw_g2", "rw_k_k", "rw_k_a", "rw_r_k",
                "rw_ln_w", "rw_ln_b", "w_out", "norm_ffn", "ffn_w1", "ffn_w3", "ffn_w2", "norm_ple", "ple_gate_w", "ple_up_w",
                "final_norm")
SMALL_NAMES = tuple(n for n in WEIGHT_NAMES if n not in BIG_NAMES)
ARG_NAMES = ("x", "p") + WEIGHT_NAMES + ("loss_target",) + tuple("m_" + n for n in WEIGHT_NAMES) + tuple("v_" + n for n in WEIGHT_NAMES)


def _pack_rows(arrs, dt):
    return jnp.concatenate([a.reshape(-1, PACK_COLS).astype(dt) for a in arrs], axis=0)


def _pack_small(arrs):
    flat = jnp.concatenate([a.reshape(-1).astype(F32) for a in arrs])
    return jnp.pad(flat, (0, SMALL_ROWS * PACK_COLS - flat.shape[0])).reshape(SMALL_ROWS, PACK_COLS)


def _kernel_impl(ins):
    x, p, target = ins["x"][0], ins["p"][0, 0], ins["loss_target"][0]
    shard = {n: ins[n][0] for n, _ in BIG}
    small = {n: (ins[n] if n == "final_norm" else ins[n][0]) for n in SMALL_NAMES}

    pack = _pack_rows([shard[n] for n, _ in BIG], BF16)
    rows_per = pack.shape[0]
    gathered = _all_gather("ag_weights", pack).reshape(N_DEV, rows_per, PACK_COLS)
    W = dict(small)
    off = 0
    for n, ax in BIG:
        r, c = shard[n].shape
        nrow = r * c // PACK_COLS
        seg = gathered[:, off:off + nrow].reshape(N_DEV, r, c)
        W[n] = seg.reshape(N_DEV * r, c) if ax == 0 else seg.transpose(1, 0, 2).reshape(r, N_DEV * c)
        off += nrow

    loss_part, dx, G = _local_step(x, p, target, W)

    slabs = []
    for n, ax in BIG:
        r, c = shard[n].shape
        g = G[n]
        g = g.reshape(N_DEV, r, c) if ax == 0 else g.reshape(r, N_DEV, c).transpose(1, 0, 2)
        slabs.append(g.reshape(N_DEV, r * c // PACK_COLS, PACK_COLS))
    recv = _exchange_slabs("grad_exchange", jnp.concatenate(slabs, axis=1))
    gsm = _all_gather("ag_small_grads", _pack_small([G[n] for n in SMALL_NAMES])).reshape(N_DEV, SMALL_ROWS, PACK_COLS)

    pk = lambda pre: _pack_rows([ins[pre + n][0] for n, _ in BIG], F32)
    g_b, d_b, m_b, v_b = _adamw("adamw_sharded", recv, pk(""), pk("m_"), pk("v_"), 240)
    ps = lambda pre: _pack_small([ins[pre + n] for n in SMALL_NAMES])
    g_s, d_s, m_s, v_s = _adamw("adamw_replicated", gsm, ps(""), ps("m_"), ps("v_"), SMALL_ROWS)

    outs = {}
    for tag, big, sm in (("grad_", g_b, g_s), ("delta_", d_b, d_s), ("new_m_", m_b, m_s), ("new_v_", v_b, v_s)):
        off = 0
        for n, _ in BIG:
            nrow = shard[n].size // PACK_COLS
            outs[tag + n] = big[off:off + nrow].reshape(ins[n].shape)
            off += nrow
        flat = sm.reshape(-1)
        off = 0
        for n in SMALL_NAMES:
            outs[tag + n] = flat[off:off + ins[n].size].reshape(ins[n].shape)
            off += ins[n].size
    loss = lax.psum(loss_part, MESH_AXES)
    res = [loss, dx[None]]
    for tag in ("grad_", "delta_", "new_m_", "new_v_"):
        res += [outs[tag + n] for n in WEIGHT_NAMES]
    return tuple(res)


def kernel(x, p, norm_mix, w_in, s5_lam_re, s5_lam_im, s5_log_step, s5_b_re, s5_b_im, s5_c_re, s5_c_im, s5_d, s5_glu_w, s5_glu_b, rw_shift_mu, rw_w0, rw_w2, rw_a0, rw_a2, rw_g2, rw_k_k, rw_k_a, rw_r_k, rw_ln_w, rw_ln_b, w_out, norm_ffn, ffn_w1, ffn_w3, ffn_w2, norm_ple, ple_gate_w, ple_up_w, final_norm, loss_target, m_norm_mix, m_w_in, m_s5_lam_re, m_s5_lam_im, m_s5_log_step, m_s5_b_re, m_s5_b_im, m_s5_c_re, m_s5_c_im, m_s5_d, m_s5_glu_w, m_s5_glu_b, m_rw_shift_mu, m_rw_w0, m_rw_w2, m_rw_a0, m_rw_a2, m_rw_g2, m_rw_k_k, m_rw_k_a, m_rw_r_k, m_rw_ln_w, m_rw_ln_b, m_w_out, m_norm_ffn, m_ffn_w1, m_ffn_w3, m_ffn_w2, m_norm_ple, m_ple_gate_w, m_ple_up_w, m_final_norm, v_norm_mix, v_w_in, v_s5_lam_re, v_s5_lam_im, v_s5_log_step, v_s5_b_re, v_s5_b_im, v_s5_c_re, v_s5_c_im, v_s5_d, v_s5_glu_w, v_s5_glu_b, v_rw_shift_mu, v_rw_w0, v_rw_w2, v_rw_a0, v_rw_a2, v_rw_g2, v_rw_k_k, v_rw_k_a, v_rw_r_k, v_rw_ln_w, v_rw_ln_b, v_w_out, v_norm_ffn, v_ffn_w1, v_ffn_w3, v_ffn_w2, v_norm_ple, v_ple_gate_w, v_ple_up_w, v_final_norm):
    return _kernel_impl(dict(zip(ARG_NAMES, (x, p, norm_mix, w_in, s5_lam_re, s5_lam_im, s5_log_step, s5_b_re, s5_b_im, s5_c_re, s5_c_im, s5_d, s5_glu_w, s5_glu_b, rw_shift_mu, rw_w0, rw_w2, rw_a0, rw_a2, rw_g2, rw_k_k, rw_k_a, rw_r_k, rw_ln_w, rw_ln_b, w_out, norm_ffn, ffn_w1, ffn_w3, ffn_w2, norm_ple, ple_gate_w, ple_up_w, final_norm, loss_target, m_norm_mix, m_w_in, m_s5_lam_re, m_s5_lam_im, m_s5_log_step, m_s5_b_re, m_s5_b_im, m_s5_c_re, m_s5_c_im, m_s5_d, m_s5_glu_w, m_s5_glu_b, m_rw_shift_mu, m_rw_w0, m_rw_w2, m_rw_a0, m_rw_a2, m_rw_g2, m_rw_k_k, m_rw_k_a, m_rw_r_k, m_rw_ln_w, m_rw_ln_b, m_w_out, m_norm_ffn, m_ffn_w1, m_ffn_w3, m_ffn_w2, m_norm_ple, m_ple_gate_w, m_ple_up_w, m_final_norm, v_norm_mix, v_w_in, v_s5_lam_re, v_s5_lam_im, v_s5_log_step, v_s5_b_re, v_s5_b_im, v_s5_c_re, v_s5_c_im, v_s5_d, v_s5_glu_w, v_s5_glu_b, v_rw_shift_mu, v_rw_w0, v_rw_w2, v_rw_a0, v_rw_a2, v_rw_g2, v_rw_k_k, v_rw_k_a, v_rw_r_k, v_rw_ln_w, v_rw_ln_b, v_w_out, v_norm_ffn, v_ffn_w1, v_ffn_w3, v_ffn_w2, v_norm_ple, v_ple_gate_w, v_ple_up_w, v_final_norm))))
```

```python
import functools

import jax
import jax.numpy as jnp
from jax import lax
from jax.experimental import pallas as pl
from jax.experimental.pallas import tpu as pltpu

F32 = jnp.float32
BF16 = jnp.bfloat16

D_MODEL = 1024
S5_WIDTH = 512
RW_WIDTH = 512
S5_GROUP = 16
S5_GROUPS = 32
S5_STATE = 64
S5_LANES = S5_GROUPS * S5_STATE
HEAD = 64
SHIFT_COLS = 1792
IN_COLS = 2304
FFN_HIDDEN = 2816
PLE_DIM = 256
RMS_EPS = 1e-6
GN_EPS = 64e-5
L2_EPS = 1e-12
CHUNK = 64
N_DEV = 8

ADAM_LR = 0.001
ADAM_B1 = 0.9
ADAM_B2 = 0.999
ADAM_EPS = 1e-08
ADAM_WD = 0.01
ADAM_STEP = 10

VMEM_LIMIT = 56 * 1024 * 1024


def _pcall(body, **kw):
    return pl.pallas_call(body, **kw)


def _cparams(n_grid):
    return pltpu.CompilerParams(dimension_semantics=("arbitrary",) * n_grid, vmem_limit_bytes=VMEM_LIMIT)


def _dot(a, b):
    return jnp.dot(a, b, preferred_element_type=F32)


def _dot_nt(a, b):
    return lax.dot_general(a, b, (((1,), (1,)), ((), ())), preferred_element_type=F32)


def _dot_tn(a, b):
    return lax.dot_general(a, b, (((0,), (0,)), ((), ())), preferred_element_type=F32)


def _mmc(w, diff=True, tr=False):
    fw, bw = (_dot_nt, _dot) if tr else (_dot, _dot_nt)
    if not diff:
        return lambda x: fw(x.astype(BF16), w)

    @jax.custom_vjp
    def f(x):
        return fw(x.astype(BF16), w)

    def fwd(x):
        return fw(x.astype(BF16), w), None

    def bwd(_, dy):
        return (bw(dy.astype(BF16), w),)

    f.defvjp(fwd, bwd)
    return f


def _split_dot(x, m, n_split):
    acc = None
    rem = x
    for s in range(n_split):
        part = rem.astype(BF16)
        t = _dot(part, m)
        acc = t if acc is None else acc + t
        if s + 1 < n_split:
            rem = rem - part.astype(F32)
    return acc


def _segsum(m, diff=True):
    if not diff:
        return lambda x: _split_dot(x, m, 2)

    @jax.custom_vjp
    def f(x):
        return _split_dot(x, m, 2)

    def fwd(x):
        return _split_dot(x, m, 2), None

    def bwd(_, dy):
        return (_split_dot(dy, m, 2),)

    f.defvjp(fwd, bwd)
    return f


def _head_indicator(n):
    r = lax.broadcasted_iota(jnp.int32, (n, n), 0) // HEAD
    c = lax.broadcasted_iota(jnp.int32, (n, n), 1) // HEAD
    return (r == c).astype(BF16)


def _rms(x, g):
    return x * lax.rsqrt(jnp.mean(x * x, axis=-1, keepdims=True) + RMS_EPS) * g


def _softplus(x):
    return jnp.maximum(x, 0.0) + jnp.log(1.0 + jnp.exp(-jnp.abs(x)))


def _sigmoid(x):
    return 1.0 / (1.0 + jnp.exp(-x))


def _gelu(x):
    return 0.5 * x * (1.0 + jnp.tanh(0.7978845608028654 * (x + 0.044715 * (x * x * x))))


def _tok_call(name, fn, L, TB, tok_in, const_in, tok_out, acc_out=()):
    nb = L // TB
    g8 = TB // 8
    in_specs, args = [], []
    for spec in tok_in:
        arr, width, cb = spec[:3]
        mode = spec[3] if len(spec) > 3 else None
        if mode is None:
            in_specs.append(pl.BlockSpec((TB, width), lambda i, cb=cb: (i, cb)))
        elif mode == "prev":
            in_specs.append(pl.BlockSpec((8, width), lambda i, cb=cb: (jnp.maximum(i * g8 - 1, 0), cb)))
        else:
            in_specs.append(pl.BlockSpec((8, width), lambda i, cb=cb: (jnp.minimum((i + 1) * g8, L // 8 - 1), cb)))
        args.append(arr)
    for c in const_in:
        in_specs.append(pl.BlockSpec(c.shape, lambda i, nd=c.ndim: (0,) * nd, pipeline_mode=pl.Buffered(1)))
        args.append(c)
    out_shape, out_specs = [], []
    for width, dt in tok_out:
        out_shape.append(jax.ShapeDtypeStruct((L, width), dt))
        out_specs.append(pl.BlockSpec((TB, width), lambda i: (i, 0)))
    for shp in acc_out:
        out_shape.append(jax.ShapeDtypeStruct(shp, F32))
        out_specs.append(pl.BlockSpec(shp, lambda i, nd=len(shp): (0,) * nd))
    n_tok, n_const, n_to = len(tok_in), len(const_in), len(tok_out)

    def body(*refs):
        i = pl.program_id(0)
        tv = [r[...] for r in refs[:n_tok]]
        cv = [r[...] for r in refs[n_tok:n_tok + n_const]]
        orefs = refs[n_tok + n_const:]
        outs = fn(i, tv, cv)
        for r, v in zip(orefs[:n_to], outs[:n_to]):
            r[...] = v.astype(r.dtype)
        for r, v in zip(orefs[n_to:], outs[n_to:]):
            @pl.when(i == 0)
            def _(r=r):
                r[...] = jnp.zeros(r.shape, r.dtype)

            r[...] += v

    res = _pcall(body, name=name, grid=(nb,), in_specs=in_specs, out_specs=out_specs, out_shape=out_shape,
                 compiler_params=_cparams(1))(*args)
    return res


def _pick_block(n, cap):
    best = None
    for b in range(128, min(n, cap) + 1, 128):
        if n % b == 0:
            best = b
    return best if best is not None else n


def _mm_tn(name, a, b):
    T, M = a.shape
    N = b.shape[1]
    bm, bn, bt = _pick_block(M, 1024), _pick_block(N, 1536), _pick_block(T, 512)

    def body(a_ref, b_ref, o_ref):
        t = pl.program_id(2)

        @pl.when(t == 0)
        def _():
            o_ref[...] = jnp.zeros(o_ref.shape, F32)

        o_ref[...] += _dot_tn(a_ref[...].astype(BF16), b_ref[...].astype(BF16))

    return _pcall(body, name=name, grid=(M // bm, N // bn, T // bt),
                  in_specs=[pl.BlockSpec((bt, bm), lambda m, n, t: (t, m)), pl.BlockSpec((bt, bn), lambda m, n, t: (t, n))],
                  out_specs=pl.BlockSpec((bm, bn), lambda m, n, t: (m, n)),
                  out_shape=jax.ShapeDtypeStruct((M, N), F32), compiler_params=_cparams(3))(a, b)


def _s5_param_fn(lam_re, lam_im, log_step, bt_re, bt_im):
    dt = jnp.exp(log_step)
    e = jnp.exp(lam_re * dt)
    lb_re = e * jnp.cos(lam_im * dt)
    lb_im = e * jnp.sin(lam_im * dt)
    den = lam_re * lam_re + lam_im * lam_im
    nr, ni = lb_re - 1.0, lb_im
    co_re = (nr * lam_re + ni * lam_im) / den
    co_im = (ni * lam_re - nr * lam_im) / den
    cr, ci = co_re[:, None, :], co_im[:, None, :]
    return lb_re, lb_im, cr * bt_re - ci * bt_im, cr * bt_im + ci * bt_re


def _s5_param_fwd(lam_re, lam_im, log_step, bt_re, bt_im):
    def body(a, b, c, d, e, o1, o2, o3, o4):
        r = _s5_param_fn(a[...], b[...], c[...], d[...], e[...])
        o1[...], o2[...], o3[...], o4[...] = r

    sh = jax.ShapeDtypeStruct
    return _pcall(body, name="s5_param_fwd",
                  out_shape=[sh(lam_re.shape, F32), sh(lam_re.shape, F32), sh(bt_re.shape, F32), sh(bt_re.shape, F32)])(
        lam_re, lam_im, log_step, bt_re, bt_im)


def _s5_param_bwd(lam_re, lam_im, log_step, bt_re, bt_im, d_lb_re, d_lb_im, d_bb_re, d_bb_im):
    def body(a, b, c, d, e, g1, g2, g3, g4, o1, o2, o3, o4, o5):
        _, vjp = jax.vjp(_s5_param_fn, a[...], b[...], c[...], d[...], e[...])
        r = vjp((g1[...], g2[...], g3[...], g4[...]))
        o1[...], o2[...], o3[...], o4[...], o5[...] = r

    sh = jax.ShapeDtypeStruct
    return _pcall(body, name="s5_param_bwd",
                  out_shape=[sh(lam_re.shape, F32), sh(lam_re.shape, F32), sh(log_step.shape, F32),
                             sh(bt_re.shape, F32), sh(bt_re.shape, F32)])(
        lam_re, lam_im, log_step, bt_re, bt_im, d_lb_re, d_lb_im, d_bb_re, d_bb_im)


def _cmul(ar, ai, br, bi):
    return ar * br - ai * bi, ar * bi + ai * br


def _scan_consts(lr, li, reverse):
    n = lr.shape[1]
    sub = lax.broadcasted_iota(jnp.int32, (8, n), 0)
    pows = [(lr, li)]
    for _ in range(7):
        pows.append(_cmul(pows[-1][0], pows[-1][1], lr, li))
    steps = []
    for s in (1, 2, 4):
        m = (sub < 8 - s) if reverse else (sub >= s)
        pr, pi = pows[s - 1]
        steps.append((s, jnp.where(m, jnp.broadcast_to(pr, (8, n)), 0.0), jnp.where(m, jnp.broadcast_to(pi, (8, n)), 0.0)))
    wr = jnp.zeros((8, n), F32)
    wi = jnp.zeros((8, n), F32)
    for r in range(8):
        e = (8 - r) if reverse else (r + 1)
        wr = jnp.where(sub == r, jnp.broadcast_to(pows[e - 1][0], (8, n)), wr)
        wi = jnp.where(sub == r, jnp.broadcast_to(pows[e - 1][1], (8, n)), wi)
    return steps, wr, wi


def _scan_rows(sre, sim, carry, lr, li, rows, reverse):
    steps, wr, wi = _scan_consts(lr, li, reverse)
    ng = rows // 8

    def step(gi, _):
        g = (ng - 1 - gi) if reverse else gi
        base = pl.multiple_of(g * 8, 8)
        xr = sre[pl.ds(base, 8), :]
        xi = sim[pl.ds(base, 8), :]
        for s, pr, pi in steps:
            sh = (8 - s) if reverse else s
            yr = pltpu.roll(xr, sh, 0)
            yi = pltpu.roll(xi, sh, 0)
            xr, xi = xr + pr * yr - pi * yi, xi + pr * yi + pi * yr
        cr = carry[0:1, :]
        ci = carry[1:2, :]
        xr, xi = xr + wr * cr - wi * ci, xi + wr * ci + wi * cr
        sre[pl.ds(base, 8), :] = xr
        sim[pl.ds(base, 8), :] = xi
        edge = 0 if reverse else 7
        carry[0:1, :] = xr[edge:edge + 1, :]
        carry[1:2, :] = xi[edge:edge + 1, :]
        return 0

    lax.fori_loop(0, ng, step, 0)


S5_Q = 4
S5_QL = S5_WIDTH // S5_Q
S5_QS = S5_LANES // S5_Q


def _s5_scan_fwd(proj, bq_re, bq_im, cq_re, cq_im, lbar, dskip, L, TB):
    nb = L // TB

    def body(u_ref, bre, bim, cre, cim, lb_ref, d_ref, y_ref, ck_ref, sre, sim, carry):
        i = pl.program_id(0)

        @pl.when(i == 0)
        def _():
            carry[...] = jnp.zeros(carry.shape, F32)

        ck_ref[0] = carry[...]
        u = u_ref[...]
        ub = u.astype(BF16)
        for q in range(S5_Q):
            uq = ub[:, q * S5_QL:(q + 1) * S5_QL]
            sre[:, q * S5_QS:(q + 1) * S5_QS] = _dot(uq, bre[q])
            sim[:, q * S5_QS:(q + 1) * S5_QS] = _dot(uq, bim[q])
        _scan_rows(sre, sim, carry, lb_ref[0:1, :], lb_ref[1:2, :], TB, False)
        for q in range(S5_Q):
            sl = slice(q * S5_QL, (q + 1) * S5_QL)
            ss = slice(q * S5_QS, (q + 1) * S5_QS)
            y_ref[:, sl] = (_dot(sre[:, ss].astype(BF16), cre[q]) - _dot(sim[:, ss].astype(BF16), cim[q])
                            + u[:, sl] * d_ref[:, sl])

    full = lambda a: pl.BlockSpec(a.shape, lambda i, nd=a.ndim: (0,) * nd)
    return _pcall(
        body, name="s5_scan_fwd", grid=(nb,),
        in_specs=[pl.BlockSpec((TB, S5_WIDTH), lambda i: (i, 0)), full(bq_re), full(bq_im), full(cq_re), full(cq_im),
                  full(lbar), full(dskip)],
        out_specs=[pl.BlockSpec((TB, S5_WIDTH), lambda i: (i, 0)), pl.BlockSpec((1, 8, S5_LANES), lambda i: (i, 0, 0))],
        out_shape=[jax.ShapeDtypeStruct((L, S5_WIDTH), F32), jax.ShapeDtypeStruct((nb, 8, S5_LANES), F32)],
        scratch_shapes=[pltpu.VMEM((TB, S5_LANES), F32), pltpu.VMEM((TB, S5_LANES), F32), pltpu.VMEM((8, S5_LANES), F32)],
        compiler_params=_cparams(1))(proj, bq_re, bq_im, cq_re, cq_im, lbar, dskip)


def _s5_scan_bwd(proj, dy, ck, bq_re, bq_im, cq_re, cq_im, lbar, dskip, L, TB):
    nb = L // TB
    ng = TB // 8

    def body(u_ref, dy_ref, ck_ref, bre, bim, cre, cim, lb_ref, d_ref,
             du_ref, dbre, dbim, dcre, dcim, dlb_ref, dd_ref, sre, sim, gre, gim, carry, gcarry):
        i = pl.program_id(0)

        @pl.when(i == 0)
        def _():
            gcarry[...] = jnp.zeros(gcarry.shape, F32)
            dbre[...] = jnp.zeros(dbre.shape, F32)
            dbim[...] = jnp.zeros(dbim.shape, F32)
            dcre[...] = jnp.zeros(dcre.shape, F32)
            dcim[...] = jnp.zeros(dcim.shape, F32)
            dlb_ref[...] = jnp.zeros(dlb_ref.shape, F32)
            dd_ref[...] = jnp.zeros(dd_ref.shape, F32)

        lr = lb_ref[0:1, :]
        li = lb_ref[1:2, :]
        u = u_ref[...]
        ub = u.astype(BF16)
        dy_v = dy_ref[...]
        dyb = dy_v.astype(BF16)
        carry[...] = ck_ref[0]
        for q in range(S5_Q):
            uq = ub[:, q * S5_QL:(q + 1) * S5_QL]
            dq = dyb[:, q * S5_QL:(q + 1) * S5_QL]
            ss = slice(q * S5_QS, (q + 1) * S5_QS)
            sre[:, ss] = _dot(uq, bre[q])
            sim[:, ss] = _dot(uq, bim[q])
            gre[:, ss] = _dot_nt(dq, cre[q])
            gim[:, ss] = -_dot_nt(dq, cim[q])
        _scan_rows(sre, sim, carry, lr, li, TB, False)
        _scan_rows(gre, gim, gcarry, lr, -li, TB, True)

        sub = lax.broadcasted_iota(jnp.int32, (8, S5_LANES), 0)
        c0r = ck_ref[0, 0:1, :]
        c0i = ck_ref[0, 1:2, :]

        def acc_step(g, acc):
            ar, ai = acc
            base = pl.multiple_of(g * 8, 8)
            pbase = pl.multiple_of(jnp.maximum(g - 1, 0) * 8, 8)
            first = g == 0
            lastr = jnp.where(first, c0r, sre[pl.ds(pbase, 8), :][7:8, :])
            lasti = jnp.where(first, c0i, sim[pl.ds(pbase, 8), :][7:8, :])
            spr = jnp.where(sub == 0, jnp.broadcast_to(lastr, sub.shape), pltpu.roll(sre[pl.ds(base, 8), :], 1, 0))
            spi = jnp.where(sub == 0, jnp.broadcast_to(lasti, sub.shape), pltpu.roll(sim[pl.ds(base, 8), :], 1, 0))
            gr = gre[pl.ds(base, 8), :]
            gi_ = gim[pl.ds(base, 8), :]
            return ar + gr * spr + gi_ * spi, ai - gr * spi + gi_ * spr

        z8 = jnp.zeros((8, S5_LANES), F32)
        ar, ai = lax.fori_loop(0, ng, acc_step, (z8, z8))
        dlb_ref[0:1, :] += jnp.sum(ar, axis=0, keepdims=True)
        dlb_ref[1:2, :] += jnp.sum(ai, axis=0, keepdims=True)

        dd_ref[...] += jnp.sum(dy_v * u, axis=0, keepdims=True)
        for q in range(S5_Q):
            sl = slice(q * S5_QL, (q + 1) * S5_QL)
            ss = slice(q * S5_QS, (q + 1) * S5_QS)
            grq = gre[:, ss].astype(BF16)
            giq = gim[:, ss].astype(BF16)
            du_ref[:, sl] = _dot_nt(grq, bre[q]) + _dot_nt(giq, bim[q]) + dy_v[:, sl] * d_ref[:, sl]
            dbre[q] += _dot_tn(ub[:, sl], grq)
            dbim[q] += _dot_tn(ub[:, sl], giq)
            dcre[q] += _dot_tn(sre[:, ss].astype(BF16), dyb[:, sl])
            dcim[q] -= _dot_tn(sim[:, ss].astype(BF16), dyb[:, sl])

    full = lambda a: pl.BlockSpec(a.shape, lambda i, nd=a.ndim: (0,) * nd)
    rev = lambda i: (nb - 1 - i, 0)
    sh = jax.ShapeDtypeStruct
    outs = [sh((L, S5_WIDTH), F32), sh(bq_re.shape, F32), sh(bq_im.shape, F32), sh(cq_re.shape, F32), sh(cq_im.shape, F32),
            sh((8, S5_LANES), F32), sh((1, S5_WIDTH), F32)]
    fo = lambda s: pl.BlockSpec(s.shape, lambda i, nd=len(s.shape): (0,) * nd)
    return _pcall(
        body, name="s5_scan_bwd", grid=(nb,),
        in_specs=[pl.BlockSpec((TB, S5_WIDTH), rev), pl.BlockSpec((TB, S5_WIDTH), rev),
                  pl.BlockSpec((1, 8, S5_LANES), lambda i: (nb - 1 - i, 0, 0)),
                  full(bq_re), full(bq_im), full(cq_re), full(cq_im), full(lbar), full(dskip)],
        out_specs=[pl.BlockSpec((TB, S5_WIDTH), rev)] + [fo(s) for s in outs[1:]],
        out_shape=outs,
        scratch_shapes=[pltpu.VMEM((TB, S5_LANES), F32)] * 4 + [pltpu.VMEM((8, S5_LANES), F32)] * 2,
        compiler_params=_cparams(1))(proj, dy, ck, bq_re, bq_im, cq_re, cq_im, lbar, dskip)


N_HEAD = RW_WIDTH // HEAD
_NN = (((2,), (1,)), ((0,), (0,)))
_NT = (((2,), (2,)), ((0,), (0,)))
_TN = (((1,), (1,)), ((0,), (0,)))


def _hi_lo(x):
    h = x.astype(BF16)
    return h, (x - h.astype(F32)).astype(BF16)


def _mm_acc(a, b, dims):
    ah, al = _hi_lo(a)
    bh, bl = _hi_lo(b)
    dg = lambda p, q: lax.dot_general(p, q, dims, preferred_element_type=F32)
    return dg(ah, bh) + dg(ah, bl) + dg(al, bh)


def _cumsum_rows(x, transpose):
    h, n, _ = x.shape
    ti = lax.broadcasted_iota(jnp.int32, (h, n, n), 1)
    tj = lax.broadcasted_iota(jnp.int32, (h, n, n), 2)
    m = ((tj >= ti) if transpose else (tj <= ti)).astype(BF16)
    acc, rem = None, x
    for s in range(3):
        part = rem.astype(BF16)
        t = lax.dot_general(m, part, _NN, preferred_element_type=F32)
        acc = t if acc is None else acc + t
        if s < 2:
            rem = rem - part.astype(F32)
    return acc


def _chunk_ops(diff):
    if not diff:
        return (lambda a, b: _mm_acc(a, b, _NN), lambda a, b: _mm_acc(a, b, _NT), lambda a, b: _mm_acc(a, b, _TN),
                lambda x: _cumsum_rows(x, False))

    @jax.custom_vjp
    def nn(a, b):
        return _mm_acc(a, b, _NN)

    nn.defvjp(lambda a, b: (_mm_acc(a, b, _NN), (a, b)), lambda r, d: (_mm_acc(d, r[1], _NT), _mm_acc(r[0], d, _TN)))

    @jax.custom_vjp
    def nt(a, b):
        return _mm_acc(a, b, _NT)

    nt.defvjp(lambda a, b: (_mm_acc(a, b, _NT), (a, b)), lambda r, d: (_mm_acc(d, r[1], _NN), _mm_acc(d, r[0], _TN)))

    @jax.custom_vjp
    def tn(a, b):
        return _mm_acc(a, b, _TN)

    tn.defvjp(lambda a, b: (_mm_acc(a, b, _TN), (a, b)), lambda r, d: (_mm_acc(r[1], d, _NT), _mm_acc(r[0], d, _NN)))

    @jax.custom_vjp
    def cums(x):
        return _cumsum_rows(x, False)

    cums.defvjp(lambda x: (_cumsum_rows(x, False), None), lambda _, d: (_cumsum_rows(d, True),))
    return nn, nt, tn, cums


def _wkv_chunk(s0, r, w, k, v, a, b, ops):
    nn, nt, tn, cums = ops
    h, n, _ = r.shape
    ti = lax.broadcasted_iota(jnp.int32, (h, n, n), 1)
    tj = lax.broadcasted_iota(jnp.int32, (h, n, n), 2)
    incl, strict = tj <= ti, tj < ti
    logw = jnp.log(w)
    cum = cums(logw)
    g_in, g_ex, g_inv = jnp.exp(cum), jnp.exp(cum - logw), jnp.exp(-cum)
    ae, re, bi, ki = a * g_ex, r * g_in, b * g_inv, k * g_inv
    tab = jnp.where(strict, nt(ae, bi), 0.0)
    tak = jnp.where(strict, nt(ae, ki), 0.0)
    qb = jnp.where(incl, nt(re, bi), 0.0)
    qk = jnp.where(incl, nt(re, ki), 0.0)
    u = nt(ae, s0) + nn(tak, v)
    npow = tab
    steps = max(1, (n - 1).bit_length())
    for i in range(steps):
        u = u + nn(npow, u)
        if i + 1 < steps:
            npow = nn(npow, npow)
    y = nt(re, s0) + nn(qb, u) + nn(qk, v)
    g_end = jnp.exp(jnp.sum(logw, axis=1, keepdims=True))
    s1 = s0 * g_end + tn(u, bi * g_end) + tn(v, ki * g_end)
    return y, s1


def _wkv_fwd(r, w, k, v, a, b, L):
    nc = L // CHUNK

    def body(r_ref, w_ref, k_ref, v_ref, a_ref, b_ref, y_ref, ck_ref, s_ref):
        c = pl.program_id(0)

        @pl.when(c == 0)
        def _():
            s_ref[...] = jnp.zeros(s_ref.shape, F32)

        s0 = s_ref[...]
        ck_ref[0] = s0
        y, s1 = _wkv_chunk(s0, r_ref[...], w_ref[...], k_ref[...], v_ref[...], a_ref[...], b_ref[...], _chunk_ops(False))
        y_ref[...] = y
        s_ref[...] = s1

    blk = pl.BlockSpec((N_HEAD, CHUNK, HEAD), lambda c: (0, c, 0))
    return _pcall(
        body, name="wkv_fwd", grid=(nc,), in_specs=[blk] * 6,
        out_specs=[blk, pl.BlockSpec((1, N_HEAD, HEAD, HEAD), lambda c: (c, 0, 0, 0))],
        out_shape=[jax.ShapeDtypeStruct((N_HEAD, L, HEAD), F32), jax.ShapeDtypeStruct((nc, N_HEAD, HEAD, HEAD), F32)],
        scratch_shapes=[pltpu.VMEM((N_HEAD, HEAD, HEAD), F32)],
        compiler_params=_cparams(1))(r, w, k, v, a, b)


def _wkv_bwd(r, w, k, v, a, b, dy, ck, L):
    nc = L // CHUNK

    def body(r_ref, w_ref, k_ref, v_ref, a_ref, b_ref, dy_ref, ck_ref,
             dr_ref, dw_ref, dk_ref, dv_ref, da_ref, db_ref, ds_ref):
        c = pl.program_id(0)

        @pl.when(c == 0)
        def _():
            ds_ref[...] = jnp.zeros(ds_ref.shape, F32)

        ops = _chunk_ops(True)
        _, vjp = jax.vjp(lambda *t: _wkv_chunk(*t, ops), ck_ref[0], r_ref[...], w_ref[...], k_ref[...], v_ref[...],
                         a_ref[...], b_ref[...])
        g = vjp((dy_ref[...], ds_ref[...]))
        ds_ref[...] = g[0]
        for o_ref, val in zip((dr_ref, dw_ref, dk_ref, dv_ref, da_ref, db_ref), g[1:]):
            o_ref[...] = val

    blk = pl.BlockSpec((N_HEAD, CHUNK, HEAD), lambda c: (0, nc - 1 - c, 0))
    sh = jax.ShapeDtypeStruct((N_HEAD, L, HEAD), F32)
    return _pcall(
        body, name="wkv_bwd", grid=(nc,),
        in_specs=[blk] * 7 + [pl.BlockSpec((1, N_HEAD, HEAD, HEAD), lambda c: (nc - 1 - c, 0, 0, 0))],
        out_specs=[blk] * 6, out_shape=[sh] * 6,
        scratch_shapes=[pltpu.VMEM((N_HEAD, HEAD, HEAD), F32)],
        compiler_params=_cparams(1))(r, w, k, v, a, b, dy, ck)


TB = 256


def _bf(x):
    return x.astype(BF16)


def _inproj_fwd(x, norm_mix, w_in, L):
    def fn(i, tv, cv):
        xn = _rms(tv[0], cv[0])
        return _dot_nt(_bf(xn), cv[1]), xn

    return _tok_call("inproj_fwd", fn, L, TB, [(x, D_MODEL, 0)], [norm_mix, w_in], [(IN_COLS, F32), (D_MODEL, BF16)])


def _s5_post_fn(glu_w, wtop, diff=True):
    mg = _mmc(glu_w, diff)
    mt = _mmc(wtop, diff) if wtop is not None else None

    def f(y, glu_b, e):
        z = _gelu(y)
        out = z * _sigmoid(mg(z) + glu_b + e)
        res = mt(out) if mt is not None else out
        return res, (z, out)

    return f


def _s5_post_fwd(y, glu_w, glu_b, L):
    def fn(i, tv, cv):
        out, _ = _s5_post_fn(cv[0], None, False)(tv[0], cv[1], 0.0)
        return (out,)

    return _tok_call("s5_post_fwd", fn, L, TB, [(y, S5_WIDTH, 0)], [glu_w, glu_b], [(S5_WIDTH, F32)])[0]


def _s5_post_bwd(y, dh1, glu_w, glu_b, wtop, L):
    def fn(i, tv, cv):
        e0 = jnp.zeros((TB, S5_WIDTH), F32)
        _, vjp, (z, out) = jax.vjp(_s5_post_fn(cv[0], cv[2]), tv[0], cv[1], e0, has_aux=True)
        dy, db, de = vjp(tv[1])
        return dy, z, de, out, db

    return _tok_call("s5_post_bwd", fn, L, TB, [(y, S5_WIDTH, 0), (dh1, D_MODEL, 0)], [glu_w, glu_b, wtop],
                     [(S5_WIDTH, F32), (S5_WIDTH, BF16), (S5_WIDTH, BF16), (S5_WIDTH, BF16)], [(1, S5_WIDTH)])


RW_COLBLK = ((RW_WIDTH, 1), (RW_WIDTH, 2), (RW_WIDTH, 3), (128, 16), (128, 17))
RW_MU = ((0, 512), (512, 1024), (1024, 1536), (1536, 1664), (1664, 1792))


def _rw_pre_fn(w2pad, a2pad, g2, diff=True):
    m_w, m_a, m_g = _mmc(w2pad, diff, True), _mmc(a2pad, diff, True), _mmc(g2, diff, True)
    seg = _segsum(_head_indicator(RW_WIDTH), diff)

    def f(zr, zk, zv, zwa, zg, w0, a0, k_k, k_a, e_w, e_a):
        wl_t = jnp.tanh(zwa)
        wlin = w0 + m_w(wl_t) + e_w
        w = -_softplus(-wlin) - 0.5
        decay = jnp.exp(-jnp.exp(w))
        a = _sigmoid(a0 + m_a(zwa) + e_a)
        sg = _sigmoid(zg)
        g = m_g(sg)
        kk = zk * k_k
        kkn = kk / jnp.maximum(jnp.sqrt(seg(kk * kk)), L2_EPS)
        kf = zk * (1.0 + (a - 1.0) * k_a)
        return (zr, decay, kf, zv, -kkn, kkn * a, g), (wl_t, sg)

    return f


def _rw_shifted(i, tv, mu):
    sub = lax.broadcasted_iota(jnp.int32, (TB, 1), 0)
    zs, dif = [], []
    for n in range(5):
        z = tv[n]
        last = jnp.where(i == 0, 0.0, tv[5 + n][7:8, :])
        prev = jnp.where(sub == 0, last, pltpu.roll(z, 1, 0))
        m = mu[:, RW_MU[n][0]:RW_MU[n][1]]
        zs.append(z + (prev - z) * m)
        dif.append(prev - z)
    return zs, dif


def _rw_tok_in(proj):
    return [(proj, wd, cb) for wd, cb in RW_COLBLK] + [(proj, wd, cb, "prev") for wd, cb in RW_COLBLK]


def _rw_pre_fwd(proj, mu, w0, a0, k_k, k_a, w2pad, a2pad, g2, L):
    def fn(i, tv, cv):
        zs, _ = _rw_shifted(i, tv, cv[0])
        outs, _ = _rw_pre_fn(cv[5], cv[6], cv[7], False)(*zs, cv[1], cv[2], cv[3], cv[4], 0.0, 0.0)
        return outs

    return _tok_call("rw_pre_fwd", fn, L, TB, _rw_tok_in(proj), [mu, w0, a0, k_k, k_a, w2pad, a2pad, g2],
                     [(RW_WIDTH, F32)] * 7)


def _rw_pre_bwd(proj, cots, mu, w0, a0, k_k, k_a, w2pad, a2pad, g2, L):
    def fn(i, tv, cv):
        zs, dif = _rw_shifted(i, tv[:10], cv[0])
        dr1, dr2, dw, dk1, dk2, dv1, dv2, da, db, dg = tv[10:]
        e0 = jnp.zeros((TB, RW_WIDTH), F32)
        _, vjp, (wl_t, sg) = jax.vjp(_rw_pre_fn(cv[5], cv[6], cv[7]), *zs, cv[1], cv[2], cv[3], cv[4], e0, e0, has_aux=True)
        g = vjp((dr1 + dr2, dw, dk1 + dk2, dv1 + dv2, da, db, dg))
        dzs = jnp.concatenate(g[:5], axis=1)
        dmu = jnp.concatenate([jnp.sum(g[n] * dif[n], axis=0, keepdims=True) for n in range(5)], axis=1)
        return dzs, wl_t, zs[3], sg, g[9], g[10], dmu, g[5], g[6], g[7], g[8]

    tok_in = _rw_tok_in(proj) + [(c, RW_WIDTH, 0) for c in cots]
    return _tok_call("rw_pre_bwd", fn, L, TB, tok_in, [mu, w0, a0, k_k, k_a, w2pad, a2pad, g2],
                     [(SHIFT_COLS, F32), (128, BF16), (128, BF16), (128, BF16), (RW_WIDTH, BF16), (RW_WIDTH, BF16)],
                     [(1, SHIFT_COLS)] + [(1, RW_WIDTH)] * 4)


def _rw_post_fn(wbot, diff=True):
    seg = _segsum(_head_indicator(RW_WIDTH), diff)
    mb = _mmc(wbot, diff) if wbot is not None else None

    def f(y, r, kf, v, g, ln_w, ln_b, r_k):
        mean = seg(y) * (1.0 / HEAD)
        yc = y - mean
        var = seg(yc * yc) * (1.0 / HEAD)
        yn = yc * lax.rsqrt(var + GN_EPS) * ln_w + ln_b
        bonus = seg(r * kf * r_k) * v
        out = (yn + bonus) * g
        res = mb(out) if mb is not None else out
        return res, out

    return f


def _rw_post_fwd(y, r, kf, v, g, ln_w, ln_b, r_k, L):
    def fn(i, tv, cv):
        out, _ = _rw_post_fn(None, False)(*tv, *cv)
        return (out,)

    return _tok_call("rw_post_fwd", fn, L, TB, [(t, RW_WIDTH, 0) for t in (y, r, kf, v, g)], [ln_w, ln_b, r_k],
                     [(RW_WIDTH, F32)])[0]


def _rw_post_bwd(y, r, kf, v, g, dh1, ln_w, ln_b, r_k, wbot, L):
    def fn(i, tv, cv):
        _, vjp, out = jax.vjp(_rw_post_fn(cv[3]), *tv[:5], cv[0], cv[1], cv[2], has_aux=True)
        gr = vjp(tv[5])
        return gr[0], gr[1], gr[2], gr[3], gr[4], out, gr[5], gr[6], gr[7]

    return _tok_call("rw_post_bwd", fn, L, TB, [(t, RW_WIDTH, 0) for t in (y, r, kf, v, g)] + [(dh1, D_MODEL, 0)],
                     [ln_w, ln_b, r_k, wbot], [(RW_WIDTH, F32)] * 5 + [(RW_WIDTH, BF16)], [(1, RW_WIDTH)] * 3)


def _ffn_fn(w1, w3, w2, diff=True):
    m1, m3, m2 = _mmc(w1, diff, True), _mmc(w3, diff, True), _mmc(w2, diff)

    def f(h1, norm_ffn, e1, e3):
        hn = _rms(h1, norm_ffn)
        a1 = m1(hn) + e1
        a3 = m3(hn) + e3
        hm = a1 * _sigmoid(a1) * a3
        return h1 + m2(hm), (hn, hm)

    return f


TB_FFN = 128


def _mixffn_fwd(x, s5_out, rw_out, wtop, wbot, norm_ffn, w1, w3, w2, L):
    def fn(i, tv, cv):
        h1 = tv[0] + _dot(_bf(tv[1]), cv[0]) + _dot(_bf(tv[2]), cv[1])
        h2, _ = _ffn_fn(cv[3], cv[4], cv[5], False)(h1, cv[2], 0.0, 0.0)
        return h1, h2

    return _tok_call("mixffn_fwd", fn, L, TB_FFN, [(x, D_MODEL, 0), (s5_out, S5_WIDTH, 0), (rw_out, RW_WIDTH, 0)],
                     [wtop, wbot, norm_ffn, w1, w3, w2], [(D_MODEL, F32), (D_MODEL, F32)])


def _ffn_bwd(h1, dh2, norm_ffn, w1, w3, w2, L):
    def fn(i, tv, cv):
        e0 = jnp.zeros((TB_FFN, FFN_HIDDEN), F32)
        _, vjp, (hn, hm) = jax.vjp(_ffn_fn(cv[1], cv[2], cv[3]), tv[0], cv[0], e0, e0, has_aux=True)
        dh1, dn, d1, d3 = vjp(tv[1])
        return dh1, d1, d3, hm, hn, dn

    return _tok_call("ffn_bwd", fn, L, TB_FFN, [(h1, D_MODEL, 0), (dh2, D_MODEL, 0)], [norm_ffn, w1, w3, w2],
                     [(D_MODEL, F32), (FFN_HIDDEN, BF16), (FFN_HIDDEN, BF16), (FFN_HIDDEN, BF16), (D_MODEL, BF16)],
                     [(1, D_MODEL)])


def _ple_loss_fb(h2, p, target, norm_ple, final_norm, wg, wu, L):
    def fn(i, tv, cv):
        mgate, mup = _mmc(cv[2]), _mmc(cv[3], False, True)

        def f(h2_, norm_ple_, final_norm_, eg, eu):
            hn = _rms(h2_, norm_ple_)
            gate = _sigmoid(mgate(hn) + eg)
            h3 = h2_ + gate * (mup(tv[1]) + eu)
            out = _rms(h3, final_norm_)
            d = out - tv[2]
            return 0.5 * jnp.sum(jnp.mean(d * d, axis=-1, keepdims=True)), hn

        e0 = jnp.zeros((TB, D_MODEL), F32)
        loss, vjp, hn = jax.vjp(f, tv[0], cv[0], cv[1], e0, e0, has_aux=True)
        dh2, dnp, dfn, deg, deu = vjp(jnp.ones((), F32))
        return dh2, deg, deu, hn, jnp.full((8, 128), loss, F32), dnp, dfn

    return _tok_call("ple_loss_fb", fn, L, TB, [(h2, D_MODEL, 0), (p, PLE_DIM, 0), (target, D_MODEL, 0)],
                     [norm_ple, final_norm, wg, wu], [(D_MODEL, F32), (D_MODEL, BF16), (D_MODEL, BF16), (D_MODEL, BF16)],
                     [(8, 128), (1, D_MODEL), (1, D_MODEL)])


def _inproj_bwd(x, dh1, du, dzs, norm_mix, mu, w_u, w_z, L):
    nb = L // TB

    def fn(i, tv, cv):
        sub = lax.broadcasted_iota(jnp.int32, (TB, 1), 0)
        m = cv[1]
        b = tv[3] * m
        nxt = jnp.where(i == nb - 1, 0.0, tv[4][0:1, :] * m)
        dz = tv[3] * (1.0 - m) + jnp.where(sub == TB - 1, nxt, pltpu.roll(b, TB - 1, 0))
        dub, dzb = _bf(tv[2]), _bf(dz)
        dxn = _dot(dub, cv[2]) + _dot(dzb, cv[3])
        _, vjp = jax.vjp(_rms, tv[0], cv[0])
        dx, dn = vjp(dxn)
        return tv[1] + dx, jnp.concatenate([dub, dzb], axis=1), dn

    return _tok_call("inproj_bwd", fn, L, TB,
                     [(x, D_MODEL, 0), (dh1, D_MODEL, 0), (du, S5_WIDTH, 0), (dzs, SHIFT_COLS, 0), (dzs, SHIFT_COLS, 0, "next")],
                     [norm_mix, mu, w_u, w_z], [(D_MODEL, F32), (IN_COLS, BF16)], [(1, D_MODEL)])


def _eye8(dt):
    return jnp.eye(8, dtype=dt)


def _quarter_b(bb):
    return jnp.einsum("hg,qgcp->qhcgp", _eye8(bb.dtype), bb.reshape(S5_Q, 8, S5_GROUP, S5_STATE)).reshape(S5_Q, S5_QL, S5_QS)


def _unquarter_b(d):
    return jnp.einsum("qhcgp,hg->qgcp", d.reshape(S5_Q, 8, S5_GROUP, 8, S5_STATE), _eye8(d.dtype)).reshape(
        S5_GROUPS, S5_GROUP, S5_STATE)


def _quarter_c(c):
    return jnp.einsum("gh,qgcp->qgphc", _eye8(c.dtype), c.reshape(S5_Q, 8, S5_GROUP, S5_STATE)).reshape(S5_Q, S5_QS, S5_QL)


def _unquarter_c(d):
    return jnp.einsum("qgphc,gh->qgcp", d.reshape(S5_Q, 8, S5_STATE, 8, S5_GROUP), _eye8(d.dtype)).reshape(
        S5_GROUPS, S5_GROUP, S5_STATE)


def _local_step(x, p, target, W):
    L = x.shape[0]
    r2 = lambda v: v.reshape(1, -1)
    w_in = W["w_in"]
    wtop, wbot = W["w_out"][:S5_WIDTH], W["w_out"][S5_WIDTH:]
    w2pad = jnp.pad(W["rw_w2"], ((0, 0), (0, 64)))
    a2pad = jnp.pad(W["rw_a2"], ((0, 0), (64, 0)))
    mu = r2(W["rw_shift_mu"])
    rw_vec = [r2(W[n]) for n in ("rw_w0", "rw_a0", "rw_k_k", "rw_k_a")]
    ln_w, ln_b, r_k = r2(W["rw_ln_w"]), r2(W["rw_ln_b"]), r2(W["rw_r_k"])

    lam_re, lam_im = W["s5_lam_re"], W["s5_lam_im"]
    log_step = W["s5_log_step"].reshape(S5_GROUPS, 1)
    bt_re, bt_im = W["s5_b_re"].transpose(0, 2, 1), W["s5_b_im"].transpose(0, 2, 1)
    lb_re, lb_im, bb_re, bb_im = _s5_param_fwd(lam_re, lam_im, log_step, bt_re, bt_im)
    bq_re, bq_im = _quarter_b(bb_re).astype(BF16), _quarter_b(bb_im).astype(BF16)
    cq_re, cq_im = _quarter_c(W["s5_c_re"]).astype(BF16), _quarter_c(W["s5_c_im"]).astype(BF16)
    lbar = jnp.concatenate([lb_re.reshape(1, -1), lb_im.reshape(1, -1), jnp.zeros((6, S5_LANES), F32)], axis=0)
    dskip = r2(W["s5_d"])
    glu_b = r2(W["s5_glu_b"])
    norm_mix, norm_ffn, norm_ple, final_norm = (r2(W[n]) for n in ("norm_mix", "norm_ffn", "norm_ple", "final_norm"))

    proj, xn = _inproj_fwd(x, norm_mix, w_in, L)
    y_s5, ck5 = _s5_scan_fwd(proj, bq_re, bq_im, cq_re, cq_im, lbar, dskip, L, TB)
    s5_out = _s5_post_fwd(y_s5, W["s5_glu_w"], glu_b, L)
    r, wd, kf, v, a_s, b_s, g = _rw_pre_fwd(proj, mu, *rw_vec, w2pad, a2pad, W["rw_g2"], L)
    heads = lambda t: t.reshape(L, N_HEAD, HEAD).transpose(1, 0, 2)
    flat = lambda t: t.transpose(1, 0, 2).reshape(L, RW_WIDTH)
    scan_in = tuple(heads(t) for t in (r, wd, kf, v, a_s, b_s))
    y_h, ckw = _wkv_fwd(*scan_in, L)
    y_wkv = flat(y_h)
    rw_out = _rw_post_fwd(y_wkv, r, kf, v, g, ln_w, ln_b, r_k, L)
    h1, h2 = _mixffn_fwd(x, s5_out, rw_out, wtop, wbot, norm_ffn, W["ffn_w1"], W["ffn_w3"], W["ffn_w2"], L)

    G = {}
    dh2, deg, deu, hn_ple, loss_acc, G["norm_ple"], G["final_norm"] = _ple_loss_fb(
        h2, p, target, norm_ple, final_norm, W["ple_gate_w"], W["ple_up_w"], L)
    G["ple_gate_w"] = _mm_tn("dw_ple_gate", hn_ple, deg)
    G["ple_up_w"] = _mm_tn("dw_ple_up", deu, p)
    dh1, da1, da3, hm, hn_ffn, G["norm_ffn"] = _ffn_bwd(h1, dh2, norm_ffn, W["ffn_w1"], W["ffn_w3"], W["ffn_w2"], L)
    G["ffn_w1"] = _mm_tn("dw_ffn_w1", da1, hn_ffn)
    G["ffn_w3"] = _mm_tn("dw_ffn_w3", da3, hn_ffn)
    G["ffn_w2"] = _mm_tn("dw_ffn_w2", hm, dh2)
    dy_s5, z_bf, dgp, s5o_bf, G["s5_glu_b"] = _s5_post_bwd(y_s5, dh1, W["s5_glu_w"], glu_b, wtop, L)
    G["s5_glu_w"] = _mm_tn("dw_s5_glu", z_bf, dgp)
    dy_wkv, dr2, dk2, dv2, dg, rwo_bf, G["rw_ln_w"], G["rw_ln_b"], G["rw_r_k"] = _rw_post_bwd(
        y_wkv, r, kf, v, g, dh1, ln_w, ln_b, r_k, wbot, L)
    G["w_out"] = jnp.concatenate([_mm_tn("dw_out_top", s5o_bf, dh1), _mm_tn("dw_out_bot", rwo_bf, dh1)], axis=0)
    dr1, dwd, dk1, dv1, da_s, db_s = (flat(t) for t in _wkv_bwd(*scan_in, heads(dy_wkv), ckw, L))
    (dzs, wl_t, zwa, sg, dwlin, dalin, G["rw_shift_mu"], G["rw_w0"], G["rw_a0"], G["rw_k_k"], G["rw_k_a"]) = _rw_pre_bwd(
        proj, (dr1, dr2, dwd, dk1, dk2, dv1, dv2, da_s, db_s, dg), mu, *rw_vec, w2pad, a2pad, W["rw_g2"], L)
    G["rw_w2"] = _mm_tn("dw_rw_w2", dwlin, wl_t)[:, :64]
    G["rw_a2"] = _mm_tn("dw_rw_a2", dalin, zwa)[:, 64:]
    G["rw_g2"] = _mm_tn("dw_rw_g2", dg, sg)
    du, dbq_re, dbq_im, dcq_re, dcq_im, dlbar, G["s5_d"] = _s5_scan_bwd(
        proj, dy_s5, ck5, bq_re, bq_im, cq_re, cq_im, lbar, dskip, L, TB)
    G["s5_c_re"], G["s5_c_im"] = _unquarter_c(dcq_re), _unquarter_c(dcq_im)
    d_lam_re, d_lam_im, d_ls, d_bt_re, d_bt_im = _s5_param_bwd(
        lam_re, lam_im, log_step, bt_re, bt_im, dlbar[0].reshape(S5_GROUPS, S5_STATE), dlbar[1].reshape(S5_GROUPS, S5_STATE),
        _unquarter_b(dbq_re), _unquarter_b(dbq_im))
    G["s5_lam_re"], G["s5_lam_im"], G["s5_log_step"] = d_lam_re, d_lam_im, d_ls.reshape(S5_GROUPS)
    G["s5_b_re"], G["s5_b_im"] = d_bt_re.transpose(0, 2, 1), d_bt_im.transpose(0, 2, 1)
    dx, dproj, G["norm_mix"] = _inproj_bwd(x, dh1, du, dzs, norm_mix, mu, w_in[:S5_WIDTH], w_in[S5_WIDTH:], L)
    G["w_in"] = _mm_tn("dw_in", dproj, xn)
    return loss_acc[0, 0], dx, G


MESH_AXES = ("x", "y", "c")
_ANY = pl.BlockSpec(memory_space=pl.ANY)


def _all_gather(name, shards):
    nt = len(shards)

    def body(*refs):
        x_refs, out_refs = refs[:nt], refs[nt:2 * nt]
        send_sems, recv_sems, local_sems = refs[2 * nt:]
        x, y, c = lax.axis_index("x"), lax.axis_index("y"), lax.axis_index("c")
        me, sibling = (x, y, c), (x, y, 1 - c)
        chips = [(1 - x, y), (x, 1 - y), (1 - x, 1 - y)]

        def rows(t, px, py, pc):
            m_per = shards[t].shape[0]
            return out_refs[t].at[pl.ds((4 * px + 2 * py + pc) * m_per, m_per), :]

        def copy(t, k, block, to, src=None):
            return pltpu.make_async_remote_copy(
                src_ref=rows(t, *block) if src is None else src, dst_ref=rows(t, *block),
                send_sem=send_sems.at[7 * t + k], recv_sem=recv_sems.at[7 * t + k],
                device_id=to, device_id_type=pl.DeviceIdType.MESH)

        mine = [pltpu.make_async_copy(x_refs[t], rows(t, *me), local_sems.at[t]) for t in range(nt)]
        for cp in mine:
            cp.start()
        first = []
        for t in range(nt):
            first.append(copy(t, 0, me, sibling, src=x_refs[t]))
            first += [copy(t, 1 + j, me, (*chip, c), src=x_refs[t]) for j, chip in enumerate(chips)]
        for cp in first:
            cp.start()
        passed = []
        for t in range(nt):
            for j, chip in enumerate(chips):
                copy(t, 1 + j, (*chip, c), me).wait_recv()
                fwd = copy(t, 4 + j, (*chip, c), sibling)
                fwd.start()
                passed.append(fwd)
        for t in range(nt):
            copy(t, 0, sibling, me).wait_recv()
            for j, chip in enumerate(chips):
                copy(t, 4 + j, (*chip, 1 - c), me).wait_recv()
        for cp in first + passed:
            cp.wait_send()
        for cp in mine:
            cp.wait()

    return _pcall(body, name=name,
                  out_shape=[jax.ShapeDtypeStruct((N_DEV * a.shape[0], a.shape[1]), a.dtype) for a in shards],
                  in_specs=[_ANY] * nt, out_specs=[_ANY] * nt,
                  scratch_shapes=[pltpu.SemaphoreType.DMA((7 * nt,)), pltpu.SemaphoreType.DMA((7 * nt,)),
                                  pltpu.SemaphoreType.DMA((nt,))])(*shards)


def _exchange_slabs(name, fulls):
    nt = len(fulls)
    m_of = [a.shape[0] // N_DEV for a in fulls]

    def body(*refs):
        x_refs, out_refs = refs[:nt], refs[nt:2 * nt]
        send_sems, recv_sems, local_sems = refs[2 * nt:]
        x, y, c = lax.axis_index("x"), lax.axis_index("y"), lax.axis_index("c")
        me = 4 * x + 2 * y + c

        def block(t, dev):
            return x_refs[t].at[pl.ds(dev * m_of[t], m_of[t]), :]

        def copy(t, k, sending):
            px, py, pc = x ^ ((k >> 2) & 1), y ^ ((k >> 1) & 1), c ^ (k & 1)
            peer = 4 * px + 2 * py + pc
            return pltpu.make_async_remote_copy(
                src_ref=block(t, peer if sending else me), dst_ref=out_refs[t].at[me if sending else peer],
                send_sem=send_sems.at[7 * t + k - 1], recv_sem=recv_sems.at[7 * t + k - 1],
                device_id=(px, py, pc), device_id_type=pl.DeviceIdType.MESH)

        mine = [pltpu.make_async_copy(block(t, me), out_refs[t].at[me], local_sems.at[t]) for t in range(nt)]
        for cp in mine:
            cp.start()
        for t in range(nt):
            for k in range(1, N_DEV):
                copy(t, k, True).start()
        for t in range(nt):
            for k in range(1, N_DEV):
                copy(t, k, False).wait_recv()
        for t in range(nt):
            for k in range(1, N_DEV):
                copy(t, k, True).wait_send()
        for cp in mine:
            cp.wait()

    return _pcall(body, name=name,
                  out_shape=[jax.ShapeDtypeStruct((N_DEV, m, a.shape[1]), a.dtype) for a, m in zip(fulls, m_of)],
                  in_specs=[_ANY] * nt, out_specs=[_ANY] * nt,
                  scratch_shapes=[pltpu.SemaphoreType.DMA((7 * nt,)), pltpu.SemaphoreType.DMA((7 * nt,)),
                                  pltpu.SemaphoreType.DMA((nt,))])(*fulls)


def _adamw(name, parts, w, m, v, rb):
    _, R, N = parts.shape

    def body(p_ref, w_ref, m_ref, v_ref, g_ref, d_ref, nm_ref, nv_ref):
        g = p_ref[0]
        for s in range(1, N_DEV):
            g = g + p_ref[s]
        nm = ADAM_B1 * m_ref[...] + (1.0 - ADAM_B1) * g
        nv = ADAM_B2 * v_ref[...] + (1.0 - ADAM_B2) * (g * g)
        m_hat = nm / (1.0 - ADAM_B1 ** ADAM_STEP)
        v_hat = nv / (1.0 - ADAM_B2 ** ADAM_STEP)
        g_ref[...] = g
        d_ref[...] = -ADAM_LR * (m_hat / (jnp.sqrt(v_hat) + ADAM_EPS) + ADAM_WD * w_ref[...])
        nm_ref[...] = nm
        nv_ref[...] = nv

    blk = pl.BlockSpec((rb, N), lambda i: (i, 0))
    sh = jax.ShapeDtypeStruct((R, N), F32)
    return _pcall(body, name=name, grid=(R // rb,), in_specs=[pl.BlockSpec((N_DEV, rb, N), lambda i: (0, i, 0)), blk, blk, blk],
                  out_specs=[blk] * 4, out_shape=[sh] * 4, compiler_params=_cparams(1))(parts, w, m, v)


BIG = (("w_in", True), ("ffn_w1", True), ("ffn_w3", True), ("w_out", False), ("ffn_w2", False), ("ple_gate_w", False))
MISC = (("s5_glu_w", False), ("rw_w2", True), ("rw_a2", True), ("rw_g2", True), ("ple_up_w", True))
SHARDED_NAMES = tuple(n for n, _ in BIG + MISC)
PACK_COLS = 1024
SMALL_ROWS = 144
WEIGHT_NAMES = ("norm_mix", "w_in", "s5_lam_re", "s5_lam_im", "s5_log_step", "s5_b_re", "s5_b_im", "s5_c_re", "s5_c_im", "s5_d",
                "s5_glu_w", "s5_glu_b", "rw_shift_mu", "rw_w0", "rw_w2", "rw_a0", "rw_a2", "rw_g2", "rw_k_k", "rw_k_a", "rw_r_k",
                "rw_ln_w", "rw_ln_b", "w_out", "norm_ffn", "ffn_w1", "ffn_w3", "ffn_w2", "norm_ple", "ple_gate_w", "ple_up_w",
                "final_norm")
SMALL_NAMES = tuple(n for n in WEIGHT_NAMES if n not in SHARDED_NAMES)
ARG_NAMES = ("x", "p") + WEIGHT_NAMES + ("loss_target",) + tuple("m_" + n for n in WEIGHT_NAMES) + tuple("v_" + n for n in WEIGHT_NAMES)


def _travel(a, tr):
    return a.T if tr else a


def _pack_misc(blocks):
    lead = blocks[0].shape[:-2]
    return jnp.concatenate([b.reshape(lead + (-1, PACK_COLS)) for b in blocks], axis=len(lead))


def _unpack_misc(packed, shapes):
    lead = packed.shape[:-2]
    out, off = [], 0
    for r, c in shapes:
        n = r * c // PACK_COLS
        out.append(lax.slice_in_dim(packed, off, off + n, axis=len(lead)).reshape(lead + (r, c)))
        off += n
    return out


def _pack_small(arrs):
    flat = jnp.concatenate([a.reshape(-1).astype(F32) for a in arrs])
    return jnp.pad(flat, (0, SMALL_ROWS * PACK_COLS - flat.shape[0])).reshape(SMALL_ROWS, PACK_COLS)


def _kernel_impl(ins):
    x, p, target = ins["x"][0], ins["p"][0, 0], ins["loss_target"][0]
    small = {n: (ins[n] if n == "final_norm" else ins[n][0]) for n in SMALL_NAMES}
    trav = lambda pre, n, tr: _travel(ins[pre + n][0], tr)
    misc_shapes = [trav("", n, tr).shape for n, tr in MISC]

    shards = [trav("", n, tr).astype(BF16) for n, tr in BIG] + [_pack_misc([trav("", n, tr).astype(BF16) for n, tr in MISC])]
    gathered = _all_gather("ag_weights", shards)
    W = dict(small)
    for (n, _), g in zip(BIG, gathered):
        W[n] = g
    misc_full = _unpack_misc(gathered[-1].reshape(N_DEV, -1, PACK_COLS), misc_shapes)
    for (n, _), g in zip(MISC, misc_full):
        W[n] = g.reshape(-1, g.shape[-1])

    loss_part, dx, G = _local_step(x, p, target, W)

    misc_g = _pack_misc([G[n].reshape((N_DEV,) + shp) for (n, _), shp in zip(MISC, misc_shapes)])
    recv = _exchange_slabs("grad_exchange", [G[n] for n, _ in BIG] + [misc_g.reshape(-1, PACK_COLS)])
    gsm = _all_gather("ag_small_grads", [_pack_small([G[n] for n in SMALL_NAMES])])[0].reshape(N_DEV, SMALL_ROWS, PACK_COLS)

    outs = {}

    def emit(names_shapes, res):
        for tag, val in zip(("grad_", "delta_", "new_m_", "new_v_"), res):
            for n, v in names_shapes(val):
                outs[tag + n] = v

    for (n, tr), rc in zip(BIG, recv[:-1]):
        res = _adamw("adamw_" + n, rc, trav("", n, tr), trav("m_", n, tr), trav("v_", n, tr), _pick_rows(rc.shape[1]))
        emit(lambda val, n=n, tr=tr: [(n, _travel(val, tr).reshape(ins[n].shape))], res)
    pm = lambda pre: _pack_misc([trav(pre, n, tr) for n, tr in MISC])
    res = _adamw("adamw_misc", recv[-1], pm(""), pm("m_"), pm("v_"), recv[-1].shape[1])
    emit(lambda val: [(n, _travel(b, tr).reshape(ins[n].shape)) for (n, tr), b in zip(MISC, _unpack_misc(val, misc_shapes))], res)
    ps = lambda pre: _pack_small([ins[pre + n] for n in SMALL_NAMES])
    res = _adamw("adamw_replicated", gsm, ps(""), ps("m_"), ps("v_"), SMALL_ROWS)

    def split_small(val):
        flat, off, o = val.reshape(-1), 0, []
        for n in SMALL_NAMES:
            o.append((n, flat[off:off + ins[n].size].reshape(ins[n].shape)))
            off += ins[n].size
        return o

    emit(split_small, res)
    loss = lax.psum(loss_part, MESH_AXES)
    res = [loss, dx[None]]
    for tag in ("grad_", "delta_", "new_m_", "new_v_"):
        res += [outs[tag + n] for n in WEIGHT_NAMES]
    return tuple(res)


def _pick_rows(r):
    best = 8
    for b in range(8, 257, 8):
        if r % b == 0:
            best = b
    return best


def kernel(x, p, norm_mix, w_in, s5_lam_re, s5_lam_im, s5_log_step, s5_b_re, s5_b_im, s5_c_re, s5_c_im, s5_d, s5_glu_w, s5_glu_b, rw_shift_mu, rw_w0, rw_w2, rw_a0, rw_a2, rw_g2, rw_k_k, rw_k_a, rw_r_k, rw_ln_w, rw_ln_b, w_out, norm_ffn, ffn_w1, ffn_w3, ffn_w2, norm_ple, ple_gate_w, ple_up_w, final_norm, loss_target, m_norm_mix, m_w_in, m_s5_lam_re, m_s5_lam_im, m_s5_log_step, m_s5_b_re, m_s5_b_im, m_s5_c_re, m_s5_c_im, m_s5_d, m_s5_glu_w, m_s5_glu_b, m_rw_shift_mu, m_rw_w0, m_rw_w2, m_rw_a0, m_rw_a2, m_rw_g2, m_rw_k_k, m_rw_k_a, m_rw_r_k, m_rw_ln_w, m_rw_ln_b, m_w_out, m_norm_ffn, m_ffn_w1, m_ffn_w3, m_ffn_w2, m_norm_ple, m_ple_gate_w, m_ple_up_w, m_final_norm, v_norm_mix, v_w_in, v_s5_lam_re, v_s5_lam_im, v_s5_log_step, v_s5_b_re, v_s5_b_im, v_s5_c_re, v_s5_c_im, v_s5_d, v_s5_glu_w, v_s5_glu_b, v_rw_shift_mu, v_rw_w0, v_rw_w2, v_rw_a0, v_rw_a2, v_rw_g2, v_rw_k_k, v_rw_k_a, v_rw_r_k, v_rw_ln_w, v_rw_ln_b, v_w_out, v_norm_ffn, v_ffn_w1, v_ffn_w3, v_ffn_w2, v_norm_ple, v_ple_gate_w, v_ple_up_w, v_final_norm):
    return _kernel_impl(dict(zip(ARG_NAMES, (x, p, norm_mix, w_in, s5_lam_re, s5_lam_im, s5_log_step, s5_b_re, s5_b_im, s5_c_re, s5_c_im, s5_d, s5_glu_w, s5_glu_b, rw_shift_mu, rw_w0, rw_w2, rw_a0, rw_a2, rw_g2, rw_k_k, rw_k_a, rw_r_k, rw_ln_w, rw_ln_b, w_out, norm_ffn, ffn_w1, ffn_w3, ffn_w2, norm_ple, ple_gate_w, ple_up_w, final_norm, loss_target, m_norm_mix, m_w_in, m_s5_lam_re, m_s5_lam_im, m_s5_log_step, m_s5_b_re, m_s5_b_im, m_s5_c_re, m_s5_c_im, m_s5_d, m_s5_glu_w, m_s5_glu_b, m_rw_shift_mu, m_rw_w0, m_rw_w2, m_rw_a0, m_rw_a2, m_rw_g2, m_rw_k_k, m_rw_k_a, m_rw_r_k, m_rw_ln_w, m_rw_ln_b, m_w_out, m_norm_ffn, m_ffn_w1, m_ffn_w3, m_ffn_w2, m_norm_ple, m_ple_gate_w, m_ple_up_w, m_final_norm, v_norm_mix, v_w_in, v_s5_lam_re, v_s5_lam_im, v_s5_log_step, v_s5_b_re, v_s5_b_im, v_s5_c_re, v_s5_c_im, v_s5_d, v_s5_glu_w, v_s5_glu_b, v_rw_shift_mu, v_rw_w0, v_rw_w2, v_rw_a0, v_rw_a2, v_rw_g2, v_rw_k_k, v_rw_k_a, v_rw_r_k, v_rw_ln_w, v_rw_ln_b, v_w_out, v_norm_ffn, v_ffn_w1, v_ffn_w3, v_ffn_w2, v_norm_ple, v_ple_gate_w, v_ple_up_w, v_final_norm))))
```

```python
import functools

import jax
import jax.numpy as jnp
from jax import lax
from jax.experimental import pallas as pl
from jax.experimental.pallas import tpu as pltpu

F32 = jnp.float32
BF16 = jnp.bfloat16

D_MODEL = 1024
S5_WIDTH = 512
RW_WIDTH = 512
S5_GROUP = 16
S5_GROUPS = 32
S5_STATE = 64
S5_LANES = S5_GROUPS * S5_STATE
HEAD = 64
SHIFT_COLS = 1792
IN_COLS = 2304
FFN_HIDDEN = 2816
PLE_DIM = 256
RMS_EPS = 1e-6
GN_EPS = 64e-5
L2_EPS = 1e-12
CHUNK = 64
N_DEV = 8

ADAM_LR = 0.001
ADAM_B1 = 0.9
ADAM_B2 = 0.999
ADAM_EPS = 1e-08
ADAM_WD = 0.01
ADAM_STEP = 10

VMEM_LIMIT = 56 * 1024 * 1024


def _pcall(body, **kw):
    return pl.pallas_call(body, **kw)


def _cparams(n_grid):
    return pltpu.CompilerParams(dimension_semantics=("arbitrary",) * n_grid, vmem_limit_bytes=VMEM_LIMIT)


def _dot(a, b):
    return jnp.dot(a, b, preferred_element_type=F32)


def _dot_nt(a, b):
    return lax.dot_general(a, b, (((1,), (1,)), ((), ())), preferred_element_type=F32)


def _dot_tn(a, b):
    return lax.dot_general(a, b, (((0,), (0,)), ((), ())), preferred_element_type=F32)


def _mmc(w, diff=True, tr=False):
    fw, bw = (_dot_nt, _dot) if tr else (_dot, _dot_nt)
    if not diff:
        return lambda x: fw(x.astype(BF16), w)

    @jax.custom_vjp
    def f(x):
        return fw(x.astype(BF16), w)

    def fwd(x):
        return fw(x.astype(BF16), w), None

    def bwd(_, dy):
        return (bw(dy.astype(BF16), w),)

    f.defvjp(fwd, bwd)
    return f


def _split_dot(x, m, n_split):
    acc = None
    rem = x
    for s in range(n_split):
        part = rem.astype(BF16)
        t = _dot(part, m)
        acc = t if acc is None else acc + t
        if s + 1 < n_split:
            rem = rem - part.astype(F32)
    return acc


def _segsum(m, diff=True):
    if not diff:
        return lambda x: _split_dot(x, m, 2)

    @jax.custom_vjp
    def f(x):
        return _split_dot(x, m, 2)

    def fwd(x):
        return _split_dot(x, m, 2), None

    def bwd(_, dy):
        return (_split_dot(dy, m, 2),)

    f.defvjp(fwd, bwd)
    return f


def _head_indicator(n):
    r = lax.broadcasted_iota(jnp.int32, (n, n), 0) // HEAD
    c = lax.broadcasted_iota(jnp.int32, (n, n), 1) // HEAD
    return (r == c).astype(BF16)


def _rms(x, g):
    return x * lax.rsqrt(jnp.mean(x * x, axis=-1, keepdims=True) + RMS_EPS) * g


def _softplus(x):
    return jnp.maximum(x, 0.0) + jnp.log(1.0 + jnp.exp(-jnp.abs(x)))


def _sigmoid(x):
    return 1.0 / (1.0 + jnp.exp(-x))


def _gelu(x):
    return 0.5 * x * (1.0 + jnp.tanh(0.7978845608028654 * (x + 0.044715 * (x * x * x))))


def _tok_call(name, fn, L, TB, tok_in, const_in, tok_out, acc_out=(), deps=()):
    nb = L // TB
    g8 = TB // 8
    in_specs, args = [], []
    for spec in tok_in:
        arr, width, cb = spec[:3]
        mode = spec[3] if len(spec) > 3 else None
        if mode is None:
            in_specs.append(pl.BlockSpec((TB, width), lambda i, cb=cb: (i, cb)))
        elif mode == "prev":
            in_specs.append(pl.BlockSpec((8, width), lambda i, cb=cb: (jnp.maximum(i * g8 - 1, 0), cb)))
        else:
            in_specs.append(pl.BlockSpec((8, width), lambda i, cb=cb: (jnp.minimum((i + 1) * g8, L // 8 - 1), cb)))
        args.append(arr)
    for c in const_in:
        in_specs.append(pl.BlockSpec(c.shape, lambda i, nd=c.ndim: (0,) * nd, pipeline_mode=pl.Buffered(1)))
        args.append(c)
    for d in deps:
        in_specs.append(pl.BlockSpec(d.shape, lambda i, nd=d.ndim: (0,) * nd))
        args.append(d)
    out_shape, out_specs = [], []
    for width, dt in tok_out:
        out_shape.append(jax.ShapeDtypeStruct((L, width), dt))
        out_specs.append(pl.BlockSpec((TB, width), lambda i: (i, 0)))
    for shp in acc_out:
        out_shape.append(jax.ShapeDtypeStruct(shp, F32))
        out_specs.append(pl.BlockSpec(shp, lambda i, nd=len(shp): (0,) * nd))
    n_tok, n_const, n_to = len(tok_in), len(const_in), len(tok_out)

    def body(*refs):
        i = pl.program_id(0)
        tv = [r[...] for r in refs[:n_tok]]
        cv = [r[...] for r in refs[n_tok:n_tok + n_const]]
        orefs = refs[n_tok + n_const + len(deps):]
        outs = fn(i, tv, cv)
        for r, v in zip(orefs[:n_to], outs[:n_to]):
            r[...] = v.astype(r.dtype)
        for r, v in zip(orefs[n_to:], outs[n_to:]):
            @pl.when(i == 0)
            def _(r=r):
                r[...] = jnp.zeros(r.shape, r.dtype)

            r[...] += v

    res = _pcall(body, name=name, grid=(nb,), in_specs=in_specs, out_specs=out_specs, out_shape=out_shape,
                 compiler_params=_cparams(1))(*args)
    return res


def _pick_block(n, cap):
    best = None
    for b in range(128, min(n, cap) + 1, 128):
        if n % b == 0:
            best = b
    return best if best is not None else n


def _mm_tn(name, a, b):
    T, M = a.shape
    N = b.shape[1]
    bm, bn, bt = _pick_block(M, 1024), _pick_block(N, 1536), _pick_block(T, 512)

    def body(a_ref, b_ref, o_ref):
        t = pl.program_id(2)

        @pl.when(t == 0)
        def _():
            o_ref[...] = jnp.zeros(o_ref.shape, F32)

        o_ref[...] += _dot_tn(a_ref[...].astype(BF16), b_ref[...].astype(BF16))

    return _pcall(body, name=name, grid=(M // bm, N // bn, T // bt),
                  in_specs=[pl.BlockSpec((bt, bm), lambda m, n, t: (t, m)), pl.BlockSpec((bt, bn), lambda m, n, t: (t, n))],
                  out_specs=pl.BlockSpec((bm, bn), lambda m, n, t: (m, n)),
                  out_shape=jax.ShapeDtypeStruct((M, N), F32), compiler_params=_cparams(3))(a, b)


def _s5_param_fn(lam_re, lam_im, log_step, bt_re, bt_im):
    dt = jnp.exp(log_step)
    e = jnp.exp(lam_re * dt)
    lb_re = e * jnp.cos(lam_im * dt)
    lb_im = e * jnp.sin(lam_im * dt)
    den = lam_re * lam_re + lam_im * lam_im
    nr, ni = lb_re - 1.0, lb_im
    co_re = (nr * lam_re + ni * lam_im) / den
    co_im = (ni * lam_re - nr * lam_im) / den
    cr, ci = co_re[:, None, :], co_im[:, None, :]
    return lb_re, lb_im, cr * bt_re - ci * bt_im, cr * bt_im + ci * bt_re


def _s5_param_fwd(lam_re, lam_im, log_step, bt_re, bt_im):
    def body(a, b, c, d, e, o1, o2, o3, o4):
        r = _s5_param_fn(a[...], b[...], c[...], d[...], e[...])
        o1[...], o2[...], o3[...], o4[...] = r

    sh = jax.ShapeDtypeStruct
    return _pcall(body, name="s5_param_fwd",
                  out_shape=[sh(lam_re.shape, F32), sh(lam_re.shape, F32), sh(bt_re.shape, F32), sh(bt_re.shape, F32)])(
        lam_re, lam_im, log_step, bt_re, bt_im)


def _s5_param_bwd(lam_re, lam_im, log_step, bt_re, bt_im, d_lb_re, d_lb_im, d_bb_re, d_bb_im):
    def body(a, b, c, d, e, g1, g2, g3, g4, o1, o2, o3, o4, o5):
        _, vjp = jax.vjp(_s5_param_fn, a[...], b[...], c[...], d[...], e[...])
        r = vjp((g1[...], g2[...], g3[...], g4[...]))
        o1[...], o2[...], o3[...], o4[...], o5[...] = r

    sh = jax.ShapeDtypeStruct
    return _pcall(body, name="s5_param_bwd",
                  out_shape=[sh(lam_re.shape, F32), sh(lam_re.shape, F32), sh(log_step.shape, F32),
                             sh(bt_re.shape, F32), sh(bt_re.shape, F32)])(
        lam_re, lam_im, log_step, bt_re, bt_im, d_lb_re, d_lb_im, d_bb_re, d_bb_im)


def _cmul(ar, ai, br, bi):
    return ar * br - ai * bi, ar * bi + ai * br


def _scan_consts(lr, li, reverse):
    n = lr.shape[1]
    sub = lax.broadcasted_iota(jnp.int32, (8, n), 0)
    pows = [(lr, li)]
    for _ in range(7):
        pows.append(_cmul(pows[-1][0], pows[-1][1], lr, li))
    steps = []
    for s in (1, 2, 4):
        m = (sub < 8 - s) if reverse else (sub >= s)
        pr, pi = pows[s - 1]
        steps.append((s, jnp.where(m, jnp.broadcast_to(pr, (8, n)), 0.0), jnp.where(m, jnp.broadcast_to(pi, (8, n)), 0.0)))
    wr = jnp.zeros((8, n), F32)
    wi = jnp.zeros((8, n), F32)
    for r in range(8):
        e = (8 - r) if reverse else (r + 1)
        wr = jnp.where(sub == r, jnp.broadcast_to(pows[e - 1][0], (8, n)), wr)
        wi = jnp.where(sub == r, jnp.broadcast_to(pows[e - 1][1], (8, n)), wi)
    return steps, wr, wi


def _scan_rows(sre, sim, carry, lr, li, rows, reverse):
    steps, wr, wi = _scan_consts(lr, li, reverse)
    ng = rows // 8

    def step(gi, _):
        g = (ng - 1 - gi) if reverse else gi
        base = pl.multiple_of(g * 8, 8)
        xr = sre[pl.ds(base, 8), :]
        xi = sim[pl.ds(base, 8), :]
        for s, pr, pi in steps:
            sh = (8 - s) if reverse else s
            yr = pltpu.roll(xr, sh, 0)
            yi = pltpu.roll(xi, sh, 0)
            xr, xi = xr + pr * yr - pi * yi, xi + pr * yi + pi * yr
        cr = carry[0:1, :]
        ci = carry[1:2, :]
        xr, xi = xr + wr * cr - wi * ci, xi + wr * ci + wi * cr
        sre[pl.ds(base, 8), :] = xr
        sim[pl.ds(base, 8), :] = xi
        edge = 0 if reverse else 7
        carry[0:1, :] = xr[edge:edge + 1, :]
        carry[1:2, :] = xi[edge:edge + 1, :]
        return 0

    lax.fori_loop(0, ng, step, 0)


S5_Q = 4
S5_QL = S5_WIDTH // S5_Q
S5_QS = S5_LANES // S5_Q


def _s5_scan_fwd(proj, bq_re, bq_im, cq_re, cq_im, lbar, dskip, L, TB):
    nb = L // TB

    def body(u_ref, bre, bim, cre, cim, lb_ref, d_ref, y_ref, ck_ref, sre, sim, carry):
        i = pl.program_id(0)

        @pl.when(i == 0)
        def _():
            carry[...] = jnp.zeros(carry.shape, F32)

        ck_ref[0] = carry[...]
        u = u_ref[...]
        ub = u.astype(BF16)
        for q in range(S5_Q):
            uq = ub[:, q * S5_QL:(q + 1) * S5_QL]
            sre[:, q * S5_QS:(q + 1) * S5_QS] = _dot(uq, bre[q])
            sim[:, q * S5_QS:(q + 1) * S5_QS] = _dot(uq, bim[q])
        _scan_rows(sre, sim, carry, lb_ref[0:1, :], lb_ref[1:2, :], TB, False)
        for q in range(S5_Q):
            sl = slice(q * S5_QL, (q + 1) * S5_QL)
            ss = slice(q * S5_QS, (q + 1) * S5_QS)
            y_ref[:, sl] = (_dot(sre[:, ss].astype(BF16), cre[q]) - _dot(sim[:, ss].astype(BF16), cim[q])
                            + u[:, sl] * d_ref[:, sl])

    full = lambda a: pl.BlockSpec(a.shape, lambda i, nd=a.ndim: (0,) * nd)
    return _pcall(
        body, name="s5_scan_fwd", grid=(nb,),
        in_specs=[pl.BlockSpec((TB, S5_WIDTH), lambda i: (i, 0)), full(bq_re), full(bq_im), full(cq_re), full(cq_im),
                  full(lbar), full(dskip)],
        out_specs=[pl.BlockSpec((TB, S5_WIDTH), lambda i: (i, 0)), pl.BlockSpec((1, 8, S5_LANES), lambda i: (i, 0, 0))],
        out_shape=[jax.ShapeDtypeStruct((L, S5_WIDTH), F32), jax.ShapeDtypeStruct((nb, 8, S5_LANES), F32)],
        scratch_shapes=[pltpu.VMEM((TB, S5_LANES), F32), pltpu.VMEM((TB, S5_LANES), F32), pltpu.VMEM((8, S5_LANES), F32)],
        compiler_params=_cparams(1))(proj, bq_re, bq_im, cq_re, cq_im, lbar, dskip)


def _s5_scan_bwd(proj, dy, ck, bq_re, bq_im, cq_re, cq_im, lbar, dskip, L, TB):
    nb = L // TB
    ng = TB // 8

    def body(u_ref, dy_ref, ck_ref, bre, bim, cre, cim, lb_ref, d_ref,
             du_ref, dbre, dbim, dcre, dcim, dlb_ref, dd_ref, sre, sim, gre, gim, carry, gcarry):
        i = pl.program_id(0)

        @pl.when(i == 0)
        def _():
            gcarry[...] = jnp.zeros(gcarry.shape, F32)
            dbre[...] = jnp.zeros(dbre.shape, F32)
            dbim[...] = jnp.zeros(dbim.shape, F32)
            dcre[...] = jnp.zeros(dcre.shape, F32)
            dcim[...] = jnp.zeros(dcim.shape, F32)
            dlb_ref[...] = jnp.zeros(dlb_ref.shape, F32)
            dd_ref[...] = jnp.zeros(dd_ref.shape, F32)

        lr = lb_ref[0:1, :]
        li = lb_ref[1:2, :]
        u = u_ref[...]
        ub = u.astype(BF16)
        dy_v = dy_ref[...]
        dyb = dy_v.astype(BF16)
        carry[...] = ck_ref[0]
        for q in range(S5_Q):
            uq = ub[:, q * S5_QL:(q + 1) * S5_QL]
            dq = dyb[:, q * S5_QL:(q + 1) * S5_QL]
            ss = slice(q * S5_QS, (q + 1) * S5_QS)
            sre[:, ss] = _dot(uq, bre[q])
            sim[:, ss] = _dot(uq, bim[q])
            gre[:, ss] = _dot_nt(dq, cre[q])
            gim[:, ss] = -_dot_nt(dq, cim[q])
        _scan_rows(sre, sim, carry, lr, li, TB, False)
        _scan_rows(gre, gim, gcarry, lr, -li, TB, True)

        sub = lax.broadcasted_iota(jnp.int32, (8, S5_LANES), 0)
        c0r = ck_ref[0, 0:1, :]
        c0i = ck_ref[0, 1:2, :]

        def acc_step(g, acc):
            ar, ai = acc
            base = pl.multiple_of(g * 8, 8)
            pbase = pl.multiple_of(jnp.maximum(g - 1, 0) * 8, 8)
            first = g == 0
            lastr = jnp.where(first, c0r, sre[pl.ds(pbase, 8), :][7:8, :])
            lasti = jnp.where(first, c0i, sim[pl.ds(pbase, 8), :][7:8, :])
            spr = jnp.where(sub == 0, jnp.broadcast_to(lastr, sub.shape), pltpu.roll(sre[pl.ds(base, 8), :], 1, 0))
            spi = jnp.where(sub == 0, jnp.broadcast_to(lasti, sub.shape), pltpu.roll(sim[pl.ds(base, 8), :], 1, 0))
            gr = gre[pl.ds(base, 8), :]
            gi_ = gim[pl.ds(base, 8), :]
            return ar + gr * spr + gi_ * spi, ai - gr * spi + gi_ * spr

        z8 = jnp.zeros((8, S5_LANES), F32)
        ar, ai = lax.fori_loop(0, ng, acc_step, (z8, z8))
        dlb_ref[0:1, :] += jnp.sum(ar, axis=0, keepdims=True)
        dlb_ref[1:2, :] += jnp.sum(ai, axis=0, keepdims=True)

        dd_ref[...] += jnp.sum(dy_v * u, axis=0, keepdims=True)
        for q in range(S5_Q):
            sl = slice(q * S5_QL, (q + 1) * S5_QL)
            ss = slice(q * S5_QS, (q + 1) * S5_QS)
            grq = gre[:, ss].astype(BF16)
            giq = gim[:, ss].astype(BF16)
            du_ref[:, sl] = _dot_nt(grq, bre[q]) + _dot_nt(giq, bim[q]) + dy_v[:, sl] * d_ref[:, sl]
            dbre[q] += _dot_tn(ub[:, sl], grq)
            dbim[q] += _dot_tn(ub[:, sl], giq)
            dcre[q] += _dot_tn(sre[:, ss].astype(BF16), dyb[:, sl])
            dcim[q] -= _dot_tn(sim[:, ss].astype(BF16), dyb[:, sl])

    full = lambda a: pl.BlockSpec(a.shape, lambda i, nd=a.ndim: (0,) * nd)
    rev = lambda i: (nb - 1 - i, 0)
    sh = jax.ShapeDtypeStruct
    outs = [sh((L, S5_WIDTH), F32), sh(bq_re.shape, F32), sh(bq_im.shape, F32), sh(cq_re.shape, F32), sh(cq_im.shape, F32),
            sh((8, S5_LANES), F32), sh((1, S5_WIDTH), F32)]
    fo = lambda s: pl.BlockSpec(s.shape, lambda i, nd=len(s.shape): (0,) * nd)
    return _pcall(
        body, name="s5_scan_bwd", grid=(nb,),
        in_specs=[pl.BlockSpec((TB, S5_WIDTH), rev), pl.BlockSpec((TB, S5_WIDTH), rev),
                  pl.BlockSpec((1, 8, S5_LANES), lambda i: (nb - 1 - i, 0, 0)),
                  full(bq_re), full(bq_im), full(cq_re), full(cq_im), full(lbar), full(dskip)],
        out_specs=[pl.BlockSpec((TB, S5_WIDTH), rev)] + [fo(s) for s in outs[1:]],
        out_shape=outs,
        scratch_shapes=[pltpu.VMEM((TB, S5_LANES), F32)] * 4 + [pltpu.VMEM((8, S5_LANES), F32)] * 2,
        compiler_params=_cparams(1))(proj, dy, ck, bq_re, bq_im, cq_re, cq_im, lbar, dskip)


N_HEAD = RW_WIDTH // HEAD
_NN = (((2,), (1,)), ((0,), (0,)))
_NT = (((2,), (2,)), ((0,), (0,)))
_TN = (((1,), (1,)), ((0,), (0,)))


def _hi_lo(x):
    h = x.astype(BF16)
    return h, (x - h.astype(F32)).astype(BF16)


def _mm_acc(a, b, dims):
    ah, al = _hi_lo(a)
    bh, bl = _hi_lo(b)
    dg = lambda p, q: lax.dot_general(p, q, dims, preferred_element_type=F32)
    return dg(ah, bh) + dg(ah, bl) + dg(al, bh)


def _cumsum_rows(x, transpose):
    h, n, _ = x.shape
    ti = lax.broadcasted_iota(jnp.int32, (h, n, n), 1)
    tj = lax.broadcasted_iota(jnp.int32, (h, n, n), 2)
    m = ((tj >= ti) if transpose else (tj <= ti)).astype(BF16)
    acc, rem = None, x
    for s in range(3):
        part = rem.astype(BF16)
        t = lax.dot_general(m, part, _NN, preferred_element_type=F32)
        acc = t if acc is None else acc + t
        if s < 2:
            rem = rem - part.astype(F32)
    return acc


def _chunk_ops(diff):
    if not diff:
        return (lambda a, b: _mm_acc(a, b, _NN), lambda a, b: _mm_acc(a, b, _NT), lambda a, b: _mm_acc(a, b, _TN),
                lambda x: _cumsum_rows(x, False))

    @jax.custom_vjp
    def nn(a, b):
        return _mm_acc(a, b, _NN)

    nn.defvjp(lambda a, b: (_mm_acc(a, b, _NN), (a, b)), lambda r, d: (_mm_acc(d, r[1], _NT), _mm_acc(r[0], d, _TN)))

    @jax.custom_vjp
    def nt(a, b):
        return _mm_acc(a, b, _NT)

    nt.defvjp(lambda a, b: (_mm_acc(a, b, _NT), (a, b)), lambda r, d: (_mm_acc(d, r[1], _NN), _mm_acc(d, r[0], _TN)))

    @jax.custom_vjp
    def tn(a, b):
        return _mm_acc(a, b, _TN)

    tn.defvjp(lambda a, b: (_mm_acc(a, b, _TN), (a, b)), lambda r, d: (_mm_acc(r[1], d, _NT), _mm_acc(r[0], d, _NN)))

    @jax.custom_vjp
    def cums(x):
        return _cumsum_rows(x, False)

    cums.defvjp(lambda x: (_cumsum_rows(x, False), None), lambda _, d: (_cumsum_rows(d, True),))
    return nn, nt, tn, cums


def _wkv_chunk(s0, r, w, k, v, a, b, ops):
    nn, nt, tn, cums = ops
    h, n, _ = r.shape
    ti = lax.broadcasted_iota(jnp.int32, (h, n, n), 1)
    tj = lax.broadcasted_iota(jnp.int32, (h, n, n), 2)
    incl, strict = tj <= ti, tj < ti
    logw = jnp.log(w)
    cum = cums(logw)
    g_in, g_ex, g_inv = jnp.exp(cum), jnp.exp(cum - logw), jnp.exp(-cum)
    ae, re, bi, ki = a * g_ex, r * g_in, b * g_inv, k * g_inv
    tab = jnp.where(strict, nt(ae, bi), 0.0)
    tak = jnp.where(strict, nt(ae, ki), 0.0)
    qb = jnp.where(incl, nt(re, bi), 0.0)
    qk = jnp.where(incl, nt(re, ki), 0.0)
    u = nt(ae, s0) + nn(tak, v)
    npow = tab
    steps = max(1, (n - 1).bit_length())
    for i in range(steps):
        u = u + nn(npow, u)
        if i + 1 < steps:
            npow = nn(npow, npow)
    y = nt(re, s0) + nn(qb, u) + nn(qk, v)
    g_end = jnp.exp(jnp.sum(logw, axis=1, keepdims=True))
    s1 = s0 * g_end + tn(u, bi * g_end) + tn(v, ki * g_end)
    return y, s1


def _wkv_fwd(r, w, k, v, a, b, L):
    nc = L // CHUNK

    def body(r_ref, w_ref, k_ref, v_ref, a_ref, b_ref, y_ref, ck_ref, s_ref):
        c = pl.program_id(0)

        @pl.when(c == 0)
        def _():
            s_ref[...] = jnp.zeros(s_ref.shape, F32)

        s0 = s_ref[...]
        ck_ref[0] = s0
        y, s1 = _wkv_chunk(s0, r_ref[...], w_ref[...], k_ref[...], v_ref[...], a_ref[...], b_ref[...], _chunk_ops(False))
        y_ref[...] = y
        s_ref[...] = s1

    blk = pl.BlockSpec((N_HEAD, CHUNK, HEAD), lambda c: (0, c, 0))
    return _pcall(
        body, name="wkv_fwd", grid=(nc,), in_specs=[blk] * 6,
        out_specs=[blk, pl.BlockSpec((1, N_HEAD, HEAD, HEAD), lambda c: (c, 0, 0, 0))],
        out_shape=[jax.ShapeDtypeStruct((N_HEAD, L, HEAD), F32), jax.ShapeDtypeStruct((nc, N_HEAD, HEAD, HEAD), F32)],
        scratch_shapes=[pltpu.VMEM((N_HEAD, HEAD, HEAD), F32)],
        compiler_params=_cparams(1))(r, w, k, v, a, b)


def _wkv_bwd(r, w, k, v, a, b, dy, ck, L, deps=()):
    nc = L // CHUNK

    def body(r_ref, w_ref, k_ref, v_ref, a_ref, b_ref, dy_ref, ck_ref, *rest):
        dr_ref, dw_ref, dk_ref, dv_ref, da_ref, db_ref, ds_ref = rest[len(deps):]
        c = pl.program_id(0)

        @pl.when(c == 0)
        def _():
            ds_ref[...] = jnp.zeros(ds_ref.shape, F32)

        ops = _chunk_ops(True)
        _, vjp = jax.vjp(lambda *t: _wkv_chunk(*t, ops), ck_ref[0], r_ref[...], w_ref[...], k_ref[...], v_ref[...],
                         a_ref[...], b_ref[...])
        g = vjp((dy_ref[...], ds_ref[...]))
        ds_ref[...] = g[0]
        for o_ref, val in zip((dr_ref, dw_ref, dk_ref, dv_ref, da_ref, db_ref), g[1:]):
            o_ref[...] = val

    blk = pl.BlockSpec((N_HEAD, CHUNK, HEAD), lambda c: (0, nc - 1 - c, 0))
    sh = jax.ShapeDtypeStruct((N_HEAD, L, HEAD), F32)
    return _pcall(
        body, name="wkv_bwd", grid=(nc,),
        in_specs=[blk] * 7 + [pl.BlockSpec((1, N_HEAD, HEAD, HEAD), lambda c: (nc - 1 - c, 0, 0, 0))]
        + [pl.BlockSpec(d.shape, lambda c, nd=d.ndim: (0,) * nd) for d in deps],
        out_specs=[blk] * 6, out_shape=[sh] * 6,
        scratch_shapes=[pltpu.VMEM((N_HEAD, HEAD, HEAD), F32)],
        compiler_params=_cparams(1))(r, w, k, v, a, b, dy, ck, *deps)


TB = 256


def _bf(x):
    return x.astype(BF16)


def _inproj_fwd(x, norm_mix, w_in, L, deps=()):
    def fn(i, tv, cv):
        xn = _rms(tv[0], cv[0])
        return _dot(_bf(xn), cv[1]), xn

    return _tok_call("inproj_fwd", fn, L, TB, [(x, D_MODEL, 0)], [norm_mix, w_in], [(IN_COLS, F32), (D_MODEL, BF16)],
                     deps=deps)


def _s5_post_fn(glu_w, wtop, diff=True):
    mg = _mmc(glu_w, diff)
    mt = _mmc(wtop, diff) if wtop is not None else None

    def f(y, glu_b, e):
        z = _gelu(y)
        out = z * _sigmoid(mg(z) + glu_b + e)
        res = mt(out) if mt is not None else out
        return res, (z, out)

    return f


def _s5_post_fwd(y, glu_w, glu_b, L):
    def fn(i, tv, cv):
        out, _ = _s5_post_fn(cv[0], None, False)(tv[0], cv[1], 0.0)
        return (out,)

    return _tok_call("s5_post_fwd", fn, L, TB, [(y, S5_WIDTH, 0)], [glu_w, glu_b], [(S5_WIDTH, F32)])[0]


def _s5_post_bwd(y, dh1, glu_w, glu_b, wtop, L):
    def fn(i, tv, cv):
        e0 = jnp.zeros((TB, S5_WIDTH), F32)
        _, vjp, (z, out) = jax.vjp(_s5_post_fn(cv[0], cv[2]), tv[0], cv[1], e0, has_aux=True)
        dy, db, de = vjp(tv[1])
        return dy, z, de, out, db

    return _tok_call("s5_post_bwd", fn, L, TB, [(y, S5_WIDTH, 0), (dh1, D_MODEL, 0)], [glu_w, glu_b, wtop],
                     [(S5_WIDTH, F32), (S5_WIDTH, BF16), (S5_WIDTH, BF16), (S5_WIDTH, BF16)], [(1, S5_WIDTH)])


RW_COLBLK = ((RW_WIDTH, 1), (RW_WIDTH, 2), (RW_WIDTH, 3), (128, 16), (128, 17))
RW_MU = ((0, 512), (512, 1024), (1024, 1536), (1536, 1664), (1664, 1792))


def _rw_pre_fn(w2pad, a2pad, g2, diff=True):
    m_w, m_a, m_g = _mmc(w2pad, diff), _mmc(a2pad, diff), _mmc(g2, diff)
    seg = _segsum(_head_indicator(RW_WIDTH), diff)

    def f(zr, zk, zv, zwa, zg, w0, a0, k_k, k_a, e_w, e_a):
        wl_t = jnp.tanh(zwa)
        wlin = w0 + m_w(wl_t) + e_w
        w = -_softplus(-wlin) - 0.5
        decay = jnp.exp(-jnp.exp(w))
        a = _sigmoid(a0 + m_a(zwa) + e_a)
        sg = _sigmoid(zg)
        g = m_g(sg)
        kk = zk * k_k
        kkn = kk / jnp.maximum(jnp.sqrt(seg(kk * kk)), L2_EPS)
        kf = zk * (1.0 + (a - 1.0) * k_a)
        return (zr, decay, kf, zv, -kkn, kkn * a, g), (wl_t, sg)

    return f


def _rw_shifted(i, tv, mu):
    sub = lax.broadcasted_iota(jnp.int32, (TB, 1), 0)
    zs, dif = [], []
    for n in range(5):
        z = tv[n]
        last = jnp.where(i == 0, 0.0, tv[5 + n][7:8, :])
        prev = jnp.where(sub == 0, last, pltpu.roll(z, 1, 0))
        m = mu[:, RW_MU[n][0]:RW_MU[n][1]]
        zs.append(z + (prev - z) * m)
        dif.append(prev - z)
    return zs, dif


def _rw_tok_in(proj):
    return [(proj, wd, cb) for wd, cb in RW_COLBLK] + [(proj, wd, cb, "prev") for wd, cb in RW_COLBLK]


def _rw_pre_fwd(proj, mu, w0, a0, k_k, k_a, w2pad, a2pad, g2, L):
    def fn(i, tv, cv):
        zs, _ = _rw_shifted(i, tv, cv[0])
        outs, _ = _rw_pre_fn(cv[5], cv[6], cv[7], False)(*zs, cv[1], cv[2], cv[3], cv[4], 0.0, 0.0)
        return outs

    return _tok_call("rw_pre_fwd", fn, L, TB, _rw_tok_in(proj), [mu, w0, a0, k_k, k_a, w2pad, a2pad, g2],
                     [(RW_WIDTH, F32)] * 7)


def _rw_pre_bwd(proj, cots, mu, w0, a0, k_k, k_a, w2pad, a2pad, g2, L):
    def fn(i, tv, cv):
        zs, dif = _rw_shifted(i, tv[:10], cv[0])
        dr1, dr2, dw, dk1, dk2, dv1, dv2, da, db, dg = tv[10:]
        e0 = jnp.zeros((TB, RW_WIDTH), F32)
        _, vjp, (wl_t, sg) = jax.vjp(_rw_pre_fn(cv[5], cv[6], cv[7]), *zs, cv[1], cv[2], cv[3], cv[4], e0, e0, has_aux=True)
        g = vjp((dr1 + dr2, dw, dk1 + dk2, dv1 + dv2, da, db, dg))
        dzs = jnp.concatenate(g[:5], axis=1)
        dmu = jnp.concatenate([jnp.sum(g[n] * dif[n], axis=0, keepdims=True) for n in range(5)], axis=1)
        return dzs, wl_t, zs[3], sg, g[9], g[10], dmu, g[5], g[6], g[7], g[8]

    tok_in = _rw_tok_in(proj) + [(c, RW_WIDTH, 0) for c in cots]
    return _tok_call("rw_pre_bwd", fn, L, TB, tok_in, [mu, w0, a0, k_k, k_a, w2pad, a2pad, g2],
                     [(SHIFT_COLS, F32), (128, BF16), (128, BF16), (128, BF16), (RW_WIDTH, BF16), (RW_WIDTH, BF16)],
                     [(1, SHIFT_COLS)] + [(1, RW_WIDTH)] * 4)


def _rw_post_fn(wbot, diff=True):
    seg = _segsum(_head_indicator(RW_WIDTH), diff)
    mb = _mmc(wbot, diff) if wbot is not None else None

    def f(y, r, kf, v, g, ln_w, ln_b, r_k):
        mean = seg(y) * (1.0 / HEAD)
        yc = y - mean
        var = seg(yc * yc) * (1.0 / HEAD)
        yn = yc * lax.rsqrt(var + GN_EPS) * ln_w + ln_b
        bonus = seg(r * kf * r_k) * v
        out = (yn + bonus) * g
        res = mb(out) if mb is not None else out
        return res, out

    return f


def _rw_post_fwd(y, r, kf, v, g, ln_w, ln_b, r_k, L):
    def fn(i, tv, cv):
        out, _ = _rw_post_fn(None, False)(*tv, *cv)
        return (out,)

    return _tok_call("rw_post_fwd", fn, L, TB, [(t, RW_WIDTH, 0) for t in (y, r, kf, v, g)], [ln_w, ln_b, r_k],
                     [(RW_WIDTH, F32)])[0]


def _rw_post_bwd(y, r, kf, v, g, dh1, ln_w, ln_b, r_k, wbot, L):
    def fn(i, tv, cv):
        _, vjp, out = jax.vjp(_rw_post_fn(cv[3]), *tv[:5], cv[0], cv[1], cv[2], has_aux=True)
        gr = vjp(tv[5])
        return gr[0], gr[1], gr[2], gr[3], gr[4], out, gr[5], gr[6], gr[7]

    return _tok_call("rw_post_bwd", fn, L, TB, [(t, RW_WIDTH, 0) for t in (y, r, kf, v, g)] + [(dh1, D_MODEL, 0)],
                     [ln_w, ln_b, r_k, wbot], [(RW_WIDTH, F32)] * 5 + [(RW_WIDTH, BF16)], [(1, RW_WIDTH)] * 3)


def _ffn_fn(w1, w3, w2, diff=True):
    m1, m3, m2 = _mmc(w1, diff), _mmc(w3, diff), _mmc(w2, diff)

    def f(h1, norm_ffn, e1, e3):
        hn = _rms(h1, norm_ffn)
        a1 = m1(hn) + e1
        a3 = m3(hn) + e3
        hm = a1 * _sigmoid(a1) * a3
        return h1 + m2(hm), (hn, hm)

    return f


TB_FFN = 256


def _mixffn_fwd(x, s5_out, rw_out, wtop, wbot, norm_ffn, w1, w3, w2, L):
    def fn(i, tv, cv):
        h1 = tv[0] + _dot(_bf(tv[1]), cv[0]) + _dot(_bf(tv[2]), cv[1])
        h2, _ = _ffn_fn(cv[3], cv[4], cv[5], False)(h1, cv[2], 0.0, 0.0)
        return h1, h2

    return _tok_call("mixffn_fwd", fn, L, TB_FFN, [(x, D_MODEL, 0), (s5_out, S5_WIDTH, 0), (rw_out, RW_WIDTH, 0)],
                     [wtop, wbot, norm_ffn, w1, w3, w2], [(D_MODEL, F32), (D_MODEL, F32)])


def _ffn_bwd(h1, dh2, norm_ffn, w1, w3, w2, L):
    def fn(i, tv, cv):
        e0 = jnp.zeros((TB_FFN, FFN_HIDDEN), F32)
        _, vjp, (hn, hm) = jax.vjp(_ffn_fn(cv[1], cv[2], cv[3]), tv[0], cv[0], e0, e0, has_aux=True)
        dh1, dn, d1, d3 = vjp(tv[1])
        return dh1, d1, d3, hm, hn, dn

    return _tok_call("ffn_bwd", fn, L, TB_FFN, [(h1, D_MODEL, 0), (dh2, D_MODEL, 0)], [norm_ffn, w1, w3, w2],
                     [(D_MODEL, F32), (FFN_HIDDEN, BF16), (FFN_HIDDEN, BF16), (FFN_HIDDEN, BF16), (D_MODEL, BF16)],
                     [(1, D_MODEL)])


def _ple_loss_fb(h2, p, target, norm_ple, final_norm, wg, wu, L):
    def fn(i, tv, cv):
        mgate, mup = _mmc(cv[2]), _mmc(cv[3], False)

        def f(h2_, norm_ple_, final_norm_, eg, eu):
            hn = _rms(h2_, norm_ple_)
            gate = _sigmoid(mgate(hn) + eg)
            h3 = h2_ + gate * (mup(tv[1]) + eu)
            out = _rms(h3, final_norm_)
            d = out - tv[2]
            return 0.5 * jnp.sum(jnp.mean(d * d, axis=-1, keepdims=True)), hn

        e0 = jnp.zeros((TB, D_MODEL), F32)
        loss, vjp, hn = jax.vjp(f, tv[0], cv[0], cv[1], e0, e0, has_aux=True)
        dh2, dnp, dfn, deg, deu = vjp(jnp.ones((), F32))
        return dh2, deg, deu, hn, jnp.full((8, 128), loss, F32), dnp, dfn

    return _tok_call("ple_loss_fb", fn, L, TB, [(h2, D_MODEL, 0), (p, PLE_DIM, 0), (target, D_MODEL, 0)],
                     [norm_ple, final_norm, wg, wu], [(D_MODEL, F32), (D_MODEL, BF16), (D_MODEL, BF16), (D_MODEL, BF16)],
                     [(8, 128), (1, D_MODEL), (1, D_MODEL)])


def _inproj_bwd(x, dh1, du, dzs, norm_mix, mu, w_u, w_z, L):
    nb = L // TB

    def fn(i, tv, cv):
        sub = lax.broadcasted_iota(jnp.int32, (TB, 1), 0)
        m = cv[1]
        b = tv[3] * m
        nxt = jnp.where(i == nb - 1, 0.0, tv[4][0:1, :] * m)
        dz = tv[3] * (1.0 - m) + jnp.where(sub == TB - 1, nxt, pltpu.roll(b, TB - 1, 0))
        dub, dzb = _bf(tv[2]), _bf(dz)
        dxn = _dot_nt(dub, cv[2]) + _dot_nt(dzb, cv[3])
        _, vjp = jax.vjp(_rms, tv[0], cv[0])
        dx, dn = vjp(dxn)
        return tv[1] + dx, jnp.concatenate([dub, dzb], axis=1), dn

    return _tok_call("inproj_bwd", fn, L, TB,
                     [(x, D_MODEL, 0), (dh1, D_MODEL, 0), (du, S5_WIDTH, 0), (dzs, SHIFT_COLS, 0), (dzs, SHIFT_COLS, 0, "next")],
                     [norm_mix, mu, w_u, w_z], [(D_MODEL, F32), (IN_COLS, BF16)], [(1, D_MODEL)])


def _eye8(dt):
    return jnp.eye(8, dtype=dt)


def _quarter_b(bb):
    return jnp.einsum("hg,qgcp->qhcgp", _eye8(bb.dtype), bb.reshape(S5_Q, 8, S5_GROUP, S5_STATE)).reshape(S5_Q, S5_QL, S5_QS)


def _unquarter_b(d):
    return jnp.einsum("qhcgp,hg->qgcp", d.reshape(S5_Q, 8, S5_GROUP, 8, S5_STATE), _eye8(d.dtype)).reshape(
        S5_GROUPS, S5_GROUP, S5_STATE)


def _quarter_c(c):
    return jnp.einsum("gh,qgcp->qgphc", _eye8(c.dtype), c.reshape(S5_Q, 8, S5_GROUP, S5_STATE)).reshape(S5_Q, S5_QS, S5_QL)


def _unquarter_c(d):
    return jnp.einsum("qgphc,gh->qgcp", d.reshape(S5_Q, 8, S5_STATE, 8, S5_GROUP), _eye8(d.dtype)).reshape(
        S5_GROUPS, S5_GROUP, S5_STATE)


def _local_step(x, p, target, W, late_weights=None, grads_ready=None, first_dep=None):
    L = x.shape[0]
    r2 = lambda v: v.reshape(1, -1)
    w_in = W["w_in"]
    w2pad = jnp.pad(W["rw_w2"], ((0, 64), (0, 0)))
    a2pad = jnp.pad(W["rw_a2"], ((64, 0), (0, 0)))
    mu = r2(W["rw_shift_mu"])
    rw_vec = [r2(W[n]) for n in ("rw_w0", "rw_a0", "rw_k_k", "rw_k_a")]
    ln_w, ln_b, r_k = r2(W["rw_ln_w"]), r2(W["rw_ln_b"]), r2(W["rw_r_k"])

    lam_re, lam_im = W["s5_lam_re"], W["s5_lam_im"]
    log_step = W["s5_log_step"].reshape(S5_GROUPS, 1)
    bt_re, bt_im = W["s5_b_re"].transpose(0, 2, 1), W["s5_b_im"].transpose(0, 2, 1)
    lb_re, lb_im, bb_re, bb_im = _s5_param_fwd(lam_re, lam_im, log_step, bt_re, bt_im)
    bq_re, bq_im = _quarter_b(bb_re).astype(BF16), _quarter_b(bb_im).astype(BF16)
    cq_re, cq_im = _quarter_c(W["s5_c_re"]).astype(BF16), _quarter_c(W["s5_c_im"]).astype(BF16)
    lbar = jnp.concatenate([lb_re.reshape(1, -1), lb_im.reshape(1, -1), jnp.zeros((6, S5_LANES), F32)], axis=0)
    dskip = r2(W["s5_d"])
    glu_b = r2(W["s5_glu_b"])
    norm_mix, norm_ffn, norm_ple, final_norm = (r2(W[n]) for n in ("norm_mix", "norm_ffn", "norm_ple", "final_norm"))

    proj, xn = _inproj_fwd(x, norm_mix, w_in, L, () if first_dep is None else (first_dep,))
    y_s5, ck5 = _s5_scan_fwd(proj, bq_re, bq_im, cq_re, cq_im, lbar, dskip, L, TB)
    s5_out = _s5_post_fwd(y_s5, W["s5_glu_w"], glu_b, L)
    r, wd, kf, v, a_s, b_s, g = _rw_pre_fwd(proj, mu, *rw_vec, w2pad, a2pad, W["rw_g2"], L)
    heads = lambda t: t.reshape(L, N_HEAD, HEAD).transpose(1, 0, 2)
    flat = lambda t: t.transpose(1, 0, 2).reshape(L, RW_WIDTH)
    scan_in = tuple(heads(t) for t in (r, wd, kf, v, a_s, b_s))
    y_h, ckw = _wkv_fwd(*scan_in, L)
    y_wkv = flat(y_h)
    rw_out = _rw_post_fwd(y_wkv, r, kf, v, g, ln_w, ln_b, r_k, L)
    if late_weights is not None:
        W = dict(W, **late_weights(rw_out))
    wtop, wbot = W["w_out"][:S5_WIDTH], W["w_out"][S5_WIDTH:]
    h1, h2 = _mixffn_fwd(x, s5_out, rw_out, wtop, wbot, norm_ffn, W["ffn_w1"], W["ffn_w3"], W["ffn_w2"], L)

    G = {}
    dh2, deg, deu, hn_ple, loss_acc, G["norm_ple"], G["final_norm"] = _ple_loss_fb(
        h2, p, target, norm_ple, final_norm, W["ple_gate_w"], W["ple_up_w"], L)
    dh1, da1, da3, hm, hn_ffn, G["norm_ffn"] = _ffn_bwd(h1, dh2, norm_ffn, W["ffn_w1"], W["ffn_w3"], W["ffn_w2"], L)
    dy_s5, z_bf, dgp, s5o_bf, G["s5_glu_b"] = _s5_post_bwd(y_s5, dh1, W["s5_glu_w"], glu_b, wtop, L)
    dy_wkv, dr2, dk2, dv2, dg, rwo_bf, G["rw_ln_w"], G["rw_ln_b"], G["rw_r_k"] = _rw_post_bwd(
        y_wkv, r, kf, v, g, dh1, ln_w, ln_b, r_k, wbot, L)
    G["ffn_w1"] = _mm_tn("dw_ffn_w1", hn_ffn, da1)
    G["ffn_w3"] = _mm_tn("dw_ffn_w3", hn_ffn, da3)
    G["ffn_w2"] = _mm_tn("dw_ffn_w2", hm, dh2)
    G["ple_gate_w"] = _mm_tn("dw_ple_gate", hn_ple, deg)
    G["w_out"] = jnp.concatenate([_mm_tn("dw_out_top", s5o_bf, dh1), _mm_tn("dw_out_bot", rwo_bf, dh1)], axis=0)
    dep = grads_ready(G) if grads_ready is not None else None
    G["ple_up_w"] = _mm_tn("dw_ple_up", p, deu)
    G["s5_glu_w"] = _mm_tn("dw_s5_glu", z_bf, dgp)
    dr1, dwd, dk1, dv1, da_s, db_s = (flat(t) for t in _wkv_bwd(*scan_in, heads(dy_wkv), ckw, L, () if dep is None else (dep,)))
    (dzs, wl_t, zwa, sg, dwlin, dalin, G["rw_shift_mu"], G["rw_w0"], G["rw_a0"], G["rw_k_k"], G["rw_k_a"]) = _rw_pre_bwd(
        proj, (dr1, dr2, dwd, dk1, dk2, dv1, dv2, da_s, db_s, dg), mu, *rw_vec, w2pad, a2pad, W["rw_g2"], L)
    G["rw_w2"] = _mm_tn("dw_rw_w2", wl_t, dwlin)[:64]
    G["rw_a2"] = _mm_tn("dw_rw_a2", zwa, dalin)[64:]
    G["rw_g2"] = _mm_tn("dw_rw_g2", sg, dg)
    du, dbq_re, dbq_im, dcq_re, dcq_im, dlbar, G["s5_d"] = _s5_scan_bwd(
        proj, dy_s5, ck5, bq_re, bq_im, cq_re, cq_im, lbar, dskip, L, TB)
    G["s5_c_re"], G["s5_c_im"] = _unquarter_c(dcq_re), _unquarter_c(dcq_im)
    d_lam_re, d_lam_im, d_ls, d_bt_re, d_bt_im = _s5_param_bwd(
        lam_re, lam_im, log_step, bt_re, bt_im, dlbar[0].reshape(S5_GROUPS, S5_STATE), dlbar[1].reshape(S5_GROUPS, S5_STATE),
        _unquarter_b(dbq_re), _unquarter_b(dbq_im))
    G["s5_lam_re"], G["s5_lam_im"], G["s5_log_step"] = d_lam_re, d_lam_im, d_ls.reshape(S5_GROUPS)
    G["s5_b_re"], G["s5_b_im"] = d_bt_re.transpose(0, 2, 1), d_bt_im.transpose(0, 2, 1)
    dx, dproj, G["norm_mix"] = _inproj_bwd(x, dh1, du, dzs, norm_mix, mu, w_in[:, :S5_WIDTH], w_in[:, S5_WIDTH:], L)
    G["w_in"] = _mm_tn("dw_in", xn, dproj)
    return loss_acc[0, 0], dx, G


MESH_AXES = ("x", "y", "c")
_ANY = pl.BlockSpec(memory_space=pl.ANY)


def _all_gather(name, shards):
    nt = len(shards)

    def body(*refs):
        x_refs, out_refs = refs[:nt], refs[nt:2 * nt]
        send_sems, recv_sems, local_sems = refs[2 * nt:]
        x, y, c = lax.axis_index("x"), lax.axis_index("y"), lax.axis_index("c")
        me, sibling = (x, y, c), (x, y, 1 - c)
        chips = [(1 - x, y), (x, 1 - y), (1 - x, 1 - y)]

        def rows(t, px, py, pc):
            m_per = shards[t].shape[0]
            return out_refs[t].at[pl.ds((4 * px + 2 * py + pc) * m_per, m_per), :]

        def copy(t, k, block, to, src=None):
            return pltpu.make_async_remote_copy(
                src_ref=rows(t, *block) if src is None else src, dst_ref=rows(t, *block),
                send_sem=send_sems.at[7 * t + k], recv_sem=recv_sems.at[7 * t + k],
                device_id=to, device_id_type=pl.DeviceIdType.MESH)

        mine = [pltpu.make_async_copy(x_refs[t], rows(t, *me), local_sems.at[t]) for t in range(nt)]
        for cp in mine:
            cp.start()
        first = []
        for t in range(nt):
            first.append(copy(t, 0, me, sibling, src=x_refs[t]))
            first += [copy(t, 1 + j, me, (*chip, c), src=x_refs[t]) for j, chip in enumerate(chips)]
        for cp in first:
            cp.start()
        passed = []
        for t in range(nt):
            for j, chip in enumerate(chips):
                copy(t, 1 + j, (*chip, c), me).wait_recv()
                fwd = copy(t, 4 + j, (*chip, c), sibling)
                fwd.start()
                passed.append(fwd)
        for t in range(nt):
            copy(t, 0, sibling, me).wait_recv()
            for j, chip in enumerate(chips):
                copy(t, 4 + j, (*chip, 1 - c), me).wait_recv()
        for cp in first + passed:
            cp.wait_send()
        for cp in mine:
            cp.wait()

    return _pcall(body, name=name,
                  out_shape=[jax.ShapeDtypeStruct((N_DEV * a.shape[0], a.shape[1]), a.dtype) for a in shards],
                  in_specs=[_ANY] * nt, out_specs=[_ANY] * nt,
                  scratch_shapes=[pltpu.SemaphoreType.DMA((7 * nt,)), pltpu.SemaphoreType.DMA((7 * nt,)),
                                  pltpu.SemaphoreType.DMA((nt,))])(*shards)


def _exchange_slabs(name, fulls):
    nt = len(fulls)
    m_of = [a.shape[0] // N_DEV for a in fulls]

    def body(*refs):
        x_refs, out_refs = refs[:nt], refs[nt:2 * nt]
        send_sems, recv_sems, local_sems = refs[2 * nt:]
        x, y, c = lax.axis_index("x"), lax.axis_index("y"), lax.axis_index("c")
        me = 4 * x + 2 * y + c

        def block(t, dev):
            return x_refs[t].at[pl.ds(dev * m_of[t], m_of[t]), :]

        def copy(t, k, sending):
            px, py, pc = x ^ ((k >> 2) & 1), y ^ ((k >> 1) & 1), c ^ (k & 1)
            peer = 4 * px + 2 * py + pc
            return pltpu.make_async_remote_copy(
                src_ref=block(t, peer if sending else me), dst_ref=out_refs[t].at[me if sending else peer],
                send_sem=send_sems.at[7 * t + k - 1], recv_sem=recv_sems.at[7 * t + k - 1],
                device_id=(px, py, pc), device_id_type=pl.DeviceIdType.MESH)

        mine = [pltpu.make_async_copy(block(t, me), out_refs[t].at[me], local_sems.at[t]) for t in range(nt)]
        for cp in mine:
            cp.start()
        for t in range(nt):
            for k in range(1, N_DEV):
                copy(t, k, True).start()
        for t in range(nt):
            for k in range(1, N_DEV):
                copy(t, k, False).wait_recv()
        for t in range(nt):
            for k in range(1, N_DEV):
                copy(t, k, True).wait_send()
        for cp in mine:
            cp.wait()

    return _pcall(body, name=name,
                  out_shape=[jax.ShapeDtypeStruct((N_DEV, m, a.shape[1]), a.dtype) for a, m in zip(fulls, m_of)],
                  in_specs=[_ANY] * nt, out_specs=[_ANY] * nt,
                  scratch_shapes=[pltpu.SemaphoreType.DMA((7 * nt,)), pltpu.SemaphoreType.DMA((7 * nt,)),
                                  pltpu.SemaphoreType.DMA((nt,))])(*fulls)


_HBM = pl.BlockSpec(memory_space=pltpu.HBM)
_SEM = pl.BlockSpec(memory_space=pltpu.SEMAPHORE)
_EFFECT = pltpu.SideEffectType.DATAFLOW_SIDE_EFFECTING


def _peer_of(k):
    x, y, c = lax.axis_index("x"), lax.axis_index("y"), lax.axis_index("c")
    px, py, pc = x ^ ((k >> 2) & 1), y ^ ((k >> 1) & 1), c ^ (k & 1)
    return (px, py, pc), 4 * px + 2 * py + pc, 4 * x + 2 * y + c


def _direct_copy(t, k, src_refs, land_refs, send_sems, recv_sems, rows_of, gather):
    dev, peer, me = _peer_of(k)
    m = rows_of[t]
    src = src_refs[t] if gather else src_refs[t].at[pl.ds(peer * m, m), :]
    return pltpu.make_async_remote_copy(
        src_ref=src, dst_ref=land_refs[t].at[pl.ds(me * m, m), :],
        send_sem=send_sems.at[7 * t + k - 1], recv_sem=recv_sems.at[7 * t + k - 1],
        device_id=dev, device_id_type=pl.DeviceIdType.MESH)


def _direct_landing(t, k, src_refs, land_refs, send_sems, recv_sems, rows_of, gather):
    dev, peer, me = _peer_of(k)
    m = rows_of[t]
    src = src_refs[t] if gather else src_refs[t].at[pl.ds(me * m, m), :]
    return pltpu.make_async_remote_copy(
        src_ref=src, dst_ref=land_refs[t].at[pl.ds(peer * m, m), :],
        send_sem=send_sems.at[7 * t + k - 1], recv_sem=recv_sems.at[7 * t + k - 1],
        device_id=dev, device_id_type=pl.DeviceIdType.MESH)


def _direct_start(name, srcs, gather):
    nt = len(srcs)
    rows_of = [a.shape[0] if gather else a.shape[0] // N_DEV for a in srcs]
    lands = [pltpu.with_memory_space_constraint(lax.empty((N_DEV * m, a.shape[1]), a.dtype), pltpu.HBM)
             for a, m in zip(srcs, rows_of)]

    def body(*refs):
        src_refs, land_refs = refs[:nt], refs[nt:2 * nt]
        send_sems, recv_sems = refs[2 * nt], refs[2 * nt + 1]
        token = refs[-1]
        for t in range(nt):
            for k in range(1, N_DEV):
                _direct_copy(t, k, src_refs, land_refs, send_sems, recv_sems, rows_of, gather).start()
        token[...] = jnp.zeros(token.shape, F32)

    out = _pcall(
        body, name=name,
        out_shape=(pltpu.SemaphoreType.DMA((7 * nt,)), pltpu.SemaphoreType.DMA((7 * nt,)),
                   *[pltpu.HBM(a.shape, a.dtype) for a in srcs], *[pltpu.HBM(a.shape, a.dtype) for a in lands],
                   jax.ShapeDtypeStruct((8, 128), F32)),
        in_specs=(_HBM,) * (2 * nt),
        out_specs=(_SEM, _SEM) + (_HBM,) * (2 * nt) + (pl.BlockSpec(memory_space=pltpu.VMEM),),
        input_output_aliases={i: 2 + i for i in range(2 * nt)},
        compiler_params=pltpu.CompilerParams(has_side_effects=_EFFECT),
    )(*[pltpu.with_memory_space_constraint(a, pltpu.HBM) for a in srcs], *lands)
    return (out[0], out[1], list(out[2:2 + nt]), list(out[2 + nt:2 + 2 * nt]), rows_of, gather), out[-1]


def _direct_wait(name, handle, after):
    send_sems, recv_sems, srcs, lands, rows_of, gather = handle
    nt = len(srcs)

    def body(*refs):
        src_refs, land_refs = refs[:nt], refs[nt:2 * nt]
        s_sems, r_sems = refs[2 * nt], refs[2 * nt + 1]
        for t in range(nt):
            for k in range(1, N_DEV):
                _direct_copy(t, k, src_refs, land_refs, s_sems, r_sems, rows_of, gather).wait_send()
                _direct_landing(t, k, src_refs, land_refs, s_sems, r_sems, rows_of, gather).wait_recv()

    out = _pcall(
        body, name=name,
        out_shape=tuple(pltpu.HBM(a.shape, a.dtype) for a in srcs) + tuple(pltpu.HBM(a.shape, a.dtype) for a in lands),
        in_specs=(_HBM,) * (2 * nt) + (_SEM, _SEM, pl.BlockSpec(memory_space=pl.ANY)),
        out_specs=(_HBM,) * (2 * nt),
        input_output_aliases={i: i for i in range(2 * nt)},
        compiler_params=pltpu.CompilerParams(has_side_effects=_EFFECT),
    )(*srcs, *lands, send_sems, recv_sems, after)
    return list(out[:nt]), list(out[nt:])


def _fill_own(name, lands, owns):
    nt = len(lands)

    def body(*refs):
        land_in, own_refs, out_refs, sems = refs[:nt], refs[nt:2 * nt], refs[2 * nt:3 * nt], refs[3 * nt]
        me = 4 * lax.axis_index("x") + 2 * lax.axis_index("y") + lax.axis_index("c")
        cps = []
        for t in range(nt):
            m = owns[t].shape[0]
            cps.append(pltpu.make_async_copy(own_refs[t], out_refs[t].at[pl.ds(me * m, m), :], sems.at[t]))
            cps[-1].start()
        for cp in cps:
            cp.wait()

    return _pcall(body, name=name, out_shape=[jax.ShapeDtypeStruct(a.shape, a.dtype) for a in lands],
                  in_specs=[_ANY] * (2 * nt), out_specs=[_ANY] * nt, input_output_aliases={i: i for i in range(nt)},
                  scratch_shapes=[pltpu.SemaphoreType.DMA((nt,))])(*lands, *owns)


def _adamw_sharded(name, own, parts, w, m, v, rb):
    R, N = own.shape

    def body(o_ref, p_ref, w_ref, m_ref, v_ref, g_ref, d_ref, nm_ref, nv_ref):
        me = 4 * lax.axis_index("x") + 2 * lax.axis_index("y") + lax.axis_index("c")
        g = o_ref[...]
        for k in range(1, N_DEV):
            g = g + p_ref[me ^ k]
        nm = ADAM_B1 * m_ref[...] + (1.0 - ADAM_B1) * g
        nv = ADAM_B2 * v_ref[...] + (1.0 - ADAM_B2) * (g * g)
        m_hat = nm / (1.0 - ADAM_B1 ** ADAM_STEP)
        v_hat = nv / (1.0 - ADAM_B2 ** ADAM_STEP)
        g_ref[...] = g
        d_ref[...] = -ADAM_LR * (m_hat / (jnp.sqrt(v_hat) + ADAM_EPS) + ADAM_WD * w_ref[...])
        nm_ref[...] = nm
        nv_ref[...] = nv

    blk = pl.BlockSpec((rb, N), lambda i: (i, 0))
    sh = jax.ShapeDtypeStruct((R, N), F32)
    return _pcall(body, name=name, grid=(R // rb,),
                  in_specs=[blk, pl.BlockSpec((N_DEV, rb, N), lambda i: (0, i, 0)), blk, blk, blk],
                  out_specs=[blk] * 4, out_shape=[sh] * 4, compiler_params=_cparams(1))(own, parts, w, m, v)


def _adamw(name, parts, w, m, v, rb):
    _, R, N = parts.shape

    def body(p_ref, w_ref, m_ref, v_ref, g_ref, d_ref, nm_ref, nv_ref):
        g = p_ref[0]
        for s in range(1, N_DEV):
            g = g + p_ref[s]
        nm = ADAM_B1 * m_ref[...] + (1.0 - ADAM_B1) * g
        nv = ADAM_B2 * v_ref[...] + (1.0 - ADAM_B2) * (g * g)
        m_hat = nm / (1.0 - ADAM_B1 ** ADAM_STEP)
        v_hat = nv / (1.0 - ADAM_B2 ** ADAM_STEP)
        g_ref[...] = g
        d_ref[...] = -ADAM_LR * (m_hat / (jnp.sqrt(v_hat) + ADAM_EPS) + ADAM_WD * w_ref[...])
        nm_ref[...] = nm
        nv_ref[...] = nv

    blk = pl.BlockSpec((rb, N), lambda i: (i, 0))
    sh = jax.ShapeDtypeStruct((R, N), F32)
    return _pcall(body, name=name, grid=(R // rb,), in_specs=[pl.BlockSpec((N_DEV, rb, N), lambda i: (0, i, 0)), blk, blk, blk],
                  out_specs=[blk] * 4, out_shape=[sh] * 4, compiler_params=_cparams(1))(parts, w, m, v)


EARLY = (("w_in", True),)
LATE = (("ffn_w1", True), ("ffn_w3", True), ("w_out", False), ("ffn_w2", False), ("ple_gate_w", False))
MISC = (("s5_glu_w", False), ("rw_w2", True), ("rw_a2", True), ("rw_g2", True), ("ple_up_w", True))
SHARDED_NAMES = tuple(n for n, _ in EARLY + LATE + MISC)
PACK_COLS = 1024
SMALL_ROWS = 144
WEIGHT_NAMES = ("norm_mix", "w_in", "s5_lam_re", "s5_lam_im", "s5_log_step", "s5_b_re", "s5_b_im", "s5_c_re", "s5_c_im", "s5_d",
                "s5_glu_w", "s5_glu_b", "rw_shift_mu", "rw_w0", "rw_w2", "rw_a0", "rw_a2", "rw_g2", "rw_k_k", "rw_k_a", "rw_r_k",
                "rw_ln_w", "rw_ln_b", "w_out", "norm_ffn", "ffn_w1", "ffn_w3", "ffn_w2", "norm_ple", "ple_gate_w", "ple_up_w",
                "final_norm")
SMALL_NAMES = tuple(n for n in WEIGHT_NAMES if n not in SHARDED_NAMES)
ARG_NAMES = ("x", "p") + WEIGHT_NAMES + ("loss_target",) + tuple("m_" + n for n in WEIGHT_NAMES) + tuple("v_" + n for n in WEIGHT_NAMES)


def _travel(a, tr):
    return a.T if tr else a


def _pack_misc(blocks):
    lead = blocks[0].shape[:-2]
    return jnp.concatenate([b.reshape(lead + (-1, PACK_COLS)) for b in blocks], axis=len(lead))


def _unpack_misc(packed, shapes):
    lead = packed.shape[:-2]
    out, off = [], 0
    for r, c in shapes:
        n = r * c // PACK_COLS
        out.append(lax.slice_in_dim(packed, off, off + n, axis=len(lead)).reshape(lead + (r, c)))
        off += n
    return out


def _pack_small(arrs):
    flat = jnp.concatenate([a.reshape(-1).astype(F32) for a in arrs])
    return jnp.pad(flat, (0, SMALL_ROWS * PACK_COLS - flat.shape[0])).reshape(SMALL_ROWS, PACK_COLS)


def _kernel_impl(ins):
    x, p, target = ins["x"][0], ins["p"][0, 0], ins["loss_target"][0]
    me = 4 * lax.axis_index("x") + 2 * lax.axis_index("y") + lax.axis_index("c")
    small = {n: (ins[n] if n == "final_norm" else ins[n][0]) for n in SMALL_NAMES}
    trav = lambda pre, n, tr: _travel(ins[pre + n][0], tr)
    misc_shapes = [trav("", n, tr).shape for n, tr in MISC]

    late_handle, late_token = _direct_start("ag_late_start", [trav("", n, tr).astype(BF16) for n, tr in LATE], True)
    early = _all_gather("ag_early", [trav("", n, tr).astype(BF16) for n, tr in EARLY]
                        + [_pack_misc([trav("", n, tr).astype(BF16) for n, tr in MISC])])
    W = dict(small)
    for (n, tr), g in zip(EARLY, early):
        W[n] = _travel(g, tr)
    for (n, tr), g in zip(MISC, _unpack_misc(early[-1].reshape(N_DEV, -1, PACK_COLS), misc_shapes)):
        W[n] = _travel(g.reshape(-1, g.shape[-1]), tr)

    def late_weights(after):
        shards, lands = _direct_wait("ag_late_wait", late_handle, after)
        full = _fill_own("ag_late_fill", lands, shards)
        return {n: _travel(g, tr) for (n, tr), g in zip(LATE, full)}

    gt = lambda G, n, tr: _travel(G[n], tr)
    started = {}

    def grads_ready(G):
        started["h"], token = _direct_start("grad_late_start", [gt(G, n, tr) for n, tr in LATE], False)
        return token

    loss_part, dx, G = _local_step(x, p, target, W, late_weights, grads_ready, late_token)

    misc_g = _pack_misc([gt(G, n, tr).reshape((N_DEV,) + shp) for (n, tr), shp in zip(MISC, misc_shapes)])
    recv = _exchange_slabs("grad_exchange", [gt(G, n, tr) for n, tr in EARLY] + [misc_g.reshape(-1, PACK_COLS)])
    gsm = _all_gather("ag_small_grads", [_pack_small([G[n] for n in SMALL_NAMES])])[0].reshape(N_DEV, SMALL_ROWS, PACK_COLS)
    late_src, late_land = _direct_wait("grad_late_wait", started["h"], gsm)

    outs = {}

    def emit(names_shapes, res):
        for tag, val in zip(("grad_", "delta_", "new_m_", "new_v_"), res):
            for n, v in names_shapes(val):
                outs[tag + n] = v

    for (n, tr), src, land in zip(LATE, late_src, late_land):
        rows = src.shape[0] // N_DEV
        own = lax.dynamic_slice_in_dim(src, me * rows, rows, axis=0)
        res = _adamw_sharded("adamw_" + n, own, land.reshape(N_DEV, rows, land.shape[1]),
                             trav("", n, tr), trav("m_", n, tr), trav("v_", n, tr), _pick_rows(rows))
        emit(lambda val, n=n, tr=tr: [(n, _travel(val, tr).reshape(ins[n].shape))], res)
    for (n, tr), rc in zip(EARLY, recv[:-1]):
        res = _adamw("adamw_" + n, rc, trav("", n, tr), trav("m_", n, tr), trav("v_", n, tr), _pick_rows(rc.shape[1]))
        emit(lambda val, n=n, tr=tr: [(n, _travel(val, tr).reshape(ins[n].shape))], res)
    pm = lambda pre: _pack_misc([trav(pre, n, tr) for n, tr in MISC])
    res = _adamw("adamw_misc", recv[-1], pm(""), pm("m_"), pm("v_"), recv[-1].shape[1])
    emit(lambda val: [(n, _travel(b, tr).reshape(ins[n].shape)) for (n, tr), b in zip(MISC, _unpack_misc(val, misc_shapes))], res)
    ps = lambda pre: _pack_small([ins[pre + n] for n in SMALL_NAMES])
    res = _adamw("adamw_replicated", gsm, ps(""), ps("m_"), ps("v_"), SMALL_ROWS)

    def split_small(val):
        flat, off, o = val.reshape(-1), 0, []
        for n in SMALL_NAMES:
            o.append((n, flat[off:off + ins[n].size].reshape(ins[n].shape)))
            off += ins[n].size
        return o

    emit(split_small, res)
    loss = lax.psum(loss_part, MESH_AXES)
    res = [loss, dx[None]]
    for tag in ("grad_", "delta_", "new_m_", "new_v_"):
        res += [outs[tag + n] for n in WEIGHT_NAMES]
    return tuple(res)


def _pick_rows(r):
    best = 8
    for b in range(8, 257, 8):
        if r % b == 0:
            best = b
    return best


def kernel(x, p, norm_mix, w_in, s5_lam_re, s5_lam_im, s5_log_step, s5_b_re, s5_b_im, s5_c_re, s5_c_im, s5_d, s5_glu_w, s5_glu_b, rw_shift_mu, rw_w0, rw_w2, rw_a0, rw_a2, rw_g2, rw_k_k, rw_k_a, rw_r_k, rw_ln_w, rw_ln_b, w_out, norm_ffn, ffn_w1, ffn_w3, ffn_w2, norm_ple, ple_gate_w, ple_up_w, final_norm, loss_target, m_norm_mix, m_w_in, m_s5_lam_re, m_s5_lam_im, m_s5_log_step, m_s5_b_re, m_s5_b_im, m_s5_c_re, m_s5_c_im, m_s5_d, m_s5_glu_w, m_s5_glu_b, m_rw_shift_mu, m_rw_w0, m_rw_w2, m_rw_a0, m_rw_a2, m_rw_g2, m_rw_k_k, m_rw_k_a, m_rw_r_k, m_rw_ln_w, m_rw_ln_b, m_w_out, m_norm_ffn, m_ffn_w1, m_ffn_w3, m_ffn_w2, m_norm_ple, m_ple_gate_w, m_ple_up_w, m_final_norm, v_norm_mix, v_w_in, v_s5_lam_re, v_s5_lam_im, v_s5_log_step, v_s5_b_re, v_s5_b_im, v_s5_c_re, v_s5_c_im, v_s5_d, v_s5_glu_w, v_s5_glu_b, v_rw_shift_mu, v_rw_w0, v_rw_w2, v_rw_a0, v_rw_a2, v_rw_g2, v_rw_k_k, v_rw_k_a, v_rw_r_k, v_rw_ln_w, v_rw_ln_b, v_w_out, v_norm_ffn, v_ffn_w1, v_ffn_w3, v_ffn_w2, v_norm_ple, v_ple_gate_w, v_ple_up_w, v_final_norm):
    return _kernel_impl(dict(zip(ARG_NAMES, (x, p, norm_mix, w_in, s5_lam_re, s5_lam_im, s5_log_step, s5_b_re, s5_b_im, s5_c_re, s5_c_im, s5_d, s5_glu_w, s5_glu_b, rw_shift_mu, rw_w0, rw_w2, rw_a0, rw_a2, rw_g2, rw_k_k, rw_k_a, rw_r_k, rw_ln_w, rw_ln_b, w_out, norm_ffn, ffn_w1, ffn_w3, ffn_w2, norm_ple, ple_gate_w, ple_up_w, final_norm, loss_target, m_norm_mix, m_w_in, m_s5_lam_re, m_s5_lam_im, m_s5_log_step, m_s5_b_re, m_s5_b_im, m_s5_c_re, m_s5_c_im, m_s5_d, m_s5_glu_w, m_s5_glu_b, m_rw_shift_mu, m_rw_w0, m_rw_w2, m_rw_a0, m_rw_a2, m_rw_g2, m_rw_k_k, m_rw_k_a, m_rw_r_k, m_rw_ln_w, m_rw_ln_b, m_w_out, m_norm_ffn, m_ffn_w1, m_ffn_w3, m_ffn_w2, m_norm_ple, m_ple_gate_w, m_ple_up_w, m_final_norm, v_norm_mix, v_w_in, v_s5_lam_re, v_s5_lam_im, v_s5_log_step, v_s5_b_re, v_s5_b_im, v_s5_c_re, v_s5_c_im, v_s5_d, v_s5_glu_w, v_s5_glu_b, v_rw_shift_mu, v_rw_w0, v_rw_w2, v_rw_a0, v_rw_a2, v_rw_g2, v_rw_k_k, v_rw_k_a, v_rw_r_k, v_rw_ln_w, v_rw_ln_b, v_w_out, v_norm_ffn, v_ffn_w1, v_ffn_w3, v_ffn_w2, v_norm_ple, v_ple_gate_w, v_ple_up_w, v_final_norm))))
```

```python
import functools

import jax
import jax.numpy as jnp
from jax import lax
from jax.experimental import pallas as pl
from jax.experimental.pallas import tpu as pltpu

F32 = jnp.float32
BF16 = jnp.bfloat16

D_MODEL = 1024
S5_WIDTH = 512
RW_WIDTH = 512
S5_GROUP = 16
S5_GROUPS = 32
S5_STATE = 64
S5_LANES = S5_GROUPS * S5_STATE
HEAD = 64
SHIFT_COLS = 1792
IN_COLS = 2304
FFN_HIDDEN = 2816
PLE_DIM = 256
RMS_EPS = 1e-6
GN_EPS = 64e-5
L2_EPS = 1e-12
CHUNK = 64
N_DEV = 8

ADAM_LR = 0.001
ADAM_B1 = 0.9
ADAM_B2 = 0.999
ADAM_EPS = 1e-08
ADAM_WD = 0.01
ADAM_STEP = 10

VMEM_LIMIT = 56 * 1024 * 1024


def _pcall(body, **kw):
    return pl.pallas_call(body, **kw)


def _cparams(n_grid):
    return pltpu.CompilerParams(dimension_semantics=("arbitrary",) * n_grid, vmem_limit_bytes=VMEM_LIMIT)


def _dot(a, b):
    return jnp.dot(a, b, preferred_element_type=F32)


def _dot_nt(a, b):
    return lax.dot_general(a, b, (((1,), (1,)), ((), ())), preferred_element_type=F32)


def _dot_tn(a, b):
    return lax.dot_general(a, b, (((0,), (0,)), ((), ())), preferred_element_type=F32)


def _mmc(w, diff=True, tr=False):
    fw, bw = (_dot_nt, _dot) if tr else (_dot, _dot_nt)
    if not diff:
        return lambda x: fw(x.astype(BF16), w)

    @jax.custom_vjp
    def f(x):
        return fw(x.astype(BF16), w)

    def fwd(x):
        return fw(x.astype(BF16), w), None

    def bwd(_, dy):
        return (bw(dy.astype(BF16), w),)

    f.defvjp(fwd, bwd)
    return f


def _split_dot(x, m, n_split):
    acc = None
    rem = x
    for s in range(n_split):
        part = rem.astype(BF16)
        t = _dot(part, m)
        acc = t if acc is None else acc + t
        if s + 1 < n_split:
            rem = rem - part.astype(F32)
    return acc


def _segsum(m, diff=True):
    if not diff:
        return lambda x: _split_dot(x, m, 2)

    @jax.custom_vjp
    def f(x):
        return _split_dot(x, m, 2)

    def fwd(x):
        return _split_dot(x, m, 2), None

    def bwd(_, dy):
        return (_split_dot(dy, m, 2),)

    f.defvjp(fwd, bwd)
    return f


def _head_indicator(n):
    r = lax.broadcasted_iota(jnp.int32, (n, n), 0) // HEAD
    c = lax.broadcasted_iota(jnp.int32, (n, n), 1) // HEAD
    return (r == c).astype(BF16)


def _rms(x, g):
    return x * lax.rsqrt(jnp.mean(x * x, axis=-1, keepdims=True) + RMS_EPS) * g


def _softplus(x):
    return jnp.maximum(x, 0.0) + jnp.log(1.0 + jnp.exp(-jnp.abs(x)))


def _sigmoid(x):
    return 1.0 / (1.0 + jnp.exp(-x))


def _gelu(x):
    return 0.5 * x * (1.0 + jnp.tanh(0.7978845608028654 * (x + 0.044715 * (x * x * x))))


def _tok_call(name, fn, L, TB, tok_in, const_in, tok_out, acc_out=(), deps=()):
    nb = L // TB
    g8 = TB // 8
    in_specs, args = [], []
    for spec in tok_in:
        if len(spec) == 1:
            arr = spec[0]
            in_specs.append(pl.BlockSpec((arr.shape[0], TB, HEAD), lambda i: (0, i, 0)))
            args.append(arr)
            continue
        arr, width, cb = spec[:3]
        mode = spec[3] if len(spec) > 3 else None
        if mode is None:
            in_specs.append(pl.BlockSpec((TB, width), lambda i, cb=cb: (i, cb)))
        elif mode == "prev":
            in_specs.append(pl.BlockSpec((8, width), lambda i, cb=cb: (jnp.maximum(i * g8 - 1, 0), cb)))
        else:
            in_specs.append(pl.BlockSpec((8, width), lambda i, cb=cb: (jnp.minimum((i + 1) * g8, L // 8 - 1), cb)))
        args.append(arr)
    for c in const_in:
        in_specs.append(pl.BlockSpec(c.shape, lambda i, nd=c.ndim: (0,) * nd, pipeline_mode=pl.Buffered(1)))
        args.append(c)
    for d in deps:
        in_specs.append(pl.BlockSpec(d.shape, lambda i, nd=d.ndim: (0,) * nd))
        args.append(d)
    out_shape, out_specs = [], []
    for width, dt in tok_out:
        if width == "heads":
            out_shape.append(jax.ShapeDtypeStruct((N_HEAD, L, HEAD), dt))
            out_specs.append(pl.BlockSpec((N_HEAD, TB, HEAD), lambda i: (0, i, 0)))
            continue
        out_shape.append(jax.ShapeDtypeStruct((L, width), dt))
        out_specs.append(pl.BlockSpec((TB, width), lambda i: (i, 0)))
    for shp in acc_out:
        out_shape.append(jax.ShapeDtypeStruct(shp, F32))
        out_specs.append(pl.BlockSpec(shp, lambda i, nd=len(shp): (0,) * nd))
    n_tok, n_const, n_to = len(tok_in), len(const_in), len(tok_out)

    def body(*refs):
        i = pl.program_id(0)
        tv = [r[...] if len(r.shape) == 2 else jnp.concatenate([r[h] for h in range(r.shape[0])], axis=1)
              for r in refs[:n_tok]]
        cv = [r[...] for r in refs[n_tok:n_tok + n_const]]
        orefs = refs[n_tok + n_const + len(deps):]
        outs = fn(i, tv, cv)
        for r, v in zip(orefs[:n_to], outs[:n_to]):
            if len(r.shape) == 3:
                for h in range(r.shape[0]):
                    r[h] = v[:, h * HEAD:(h + 1) * HEAD].astype(r.dtype)
            else:
                r[...] = v.astype(r.dtype)
        for r, v in zip(orefs[n_to:], outs[n_to:]):
            @pl.when(i == 0)
            def _(r=r):
                r[...] = jnp.zeros(r.shape, r.dtype)

            r[...] += v

    res = _pcall(body, name=name, grid=(nb,), in_specs=in_specs, out_specs=out_specs, out_shape=out_shape,
                 compiler_params=_cparams(1))(*args)
    return res


def _pick_block(n, cap):
    best = None
    for b in range(128, min(n, cap) + 1, 128):
        if n % b == 0:
            best = b
    return best if best is not None else n


def _mm_tn(name, a, b):
    T, M = a.shape
    N = b.shape[1]
    bm, bn, bt = _pick_block(M, 1024), _pick_block(N, 1536), _pick_block(T, 512)

    def body(a_ref, b_ref, o_ref):
        t = pl.program_id(2)

        @pl.when(t == 0)
        def _():
            o_ref[...] = jnp.zeros(o_ref.shape, F32)

        o_ref[...] += _dot_tn(a_ref[...].astype(BF16), b_ref[...].astype(BF16))

    return _pcall(body, name=name, grid=(M // bm, N // bn, T // bt),
                  in_specs=[pl.BlockSpec((bt, bm), lambda m, n, t: (t, m)), pl.BlockSpec((bt, bn), lambda m, n, t: (t, n))],
                  out_specs=pl.BlockSpec((bm, bn), lambda m, n, t: (m, n)),
                  out_shape=jax.ShapeDtypeStruct((M, N), F32), compiler_params=_cparams(3))(a, b)


def _s5_param_fn(lam_re, lam_im, log_step, bt_re, bt_im):
    dt = jnp.exp(log_step)
    e = jnp.exp(lam_re * dt)
    lb_re = e * jnp.cos(lam_im * dt)
    lb_im = e * jnp.sin(lam_im * dt)
    den = lam_re * lam_re + lam_im * lam_im
    nr, ni = lb_re - 1.0, lb_im
    co_re = (nr * lam_re + ni * lam_im) / den
    co_im = (ni * lam_re - nr * lam_im) / den
    cr, ci = co_re[:, None, :], co_im[:, None, :]
    return lb_re, lb_im, cr * bt_re - ci * bt_im, cr * bt_im + ci * bt_re


def _s5_param_fwd(lam_re, lam_im, log_step, bt_re, bt_im):
    def body(a, b, c, d, e, o1, o2, o3, o4):
        r = _s5_param_fn(a[...], b[...], c[...], d[...], e[...])
        o1[...], o2[...], o3[...], o4[...] = r

    sh = jax.ShapeDtypeStruct
    return _pcall(body, name="s5_param_fwd",
                  out_shape=[sh(lam_re.shape, F32), sh(lam_re.shape, F32), sh(bt_re.shape, F32), sh(bt_re.shape, F32)])(
        lam_re, lam_im, log_step, bt_re, bt_im)


def _s5_param_bwd(lam_re, lam_im, log_step, bt_re, bt_im, d_lb_re, d_lb_im, d_bb_re, d_bb_im):
    def body(a, b, c, d, e, g1, g2, g3, g4, o1, o2, o3, o4, o5):
        _, vjp = jax.vjp(_s5_param_fn, a[...], b[...], c[...], d[...], e[...])
        r = vjp((g1[...], g2[...], g3[...], g4[...]))
        o1[...], o2[...], o3[...], o4[...], o5[...] = r

    sh = jax.ShapeDtypeStruct
    return _pcall(body, name="s5_param_bwd",
                  out_shape=[sh(lam_re.shape, F32), sh(lam_re.shape, F32), sh(log_step.shape, F32),
                             sh(bt_re.shape, F32), sh(bt_re.shape, F32)])(
        lam_re, lam_im, log_step, bt_re, bt_im, d_lb_re, d_lb_im, d_bb_re, d_bb_im)


def _cmul(ar, ai, br, bi):
    return ar * br - ai * bi, ar * bi + ai * br


def _scan_consts(lr, li, reverse):
    n = lr.shape[1]
    sub = lax.broadcasted_iota(jnp.int32, (8, n), 0)
    pows = [(lr, li)]
    for _ in range(7):
        pows.append(_cmul(pows[-1][0], pows[-1][1], lr, li))
    steps = []
    for s in (1, 2, 4):
        m = (sub < 8 - s) if reverse else (sub >= s)
        pr, pi = pows[s - 1]
        steps.append((s, jnp.where(m, jnp.broadcast_to(pr, (8, n)), 0.0), jnp.where(m, jnp.broadcast_to(pi, (8, n)), 0.0)))
    wr = jnp.zeros((8, n), F32)
    wi = jnp.zeros((8, n), F32)
    for r in range(8):
        e = (8 - r) if reverse else (r + 1)
        wr = jnp.where(sub == r, jnp.broadcast_to(pows[e - 1][0], (8, n)), wr)
        wi = jnp.where(sub == r, jnp.broadcast_to(pows[e - 1][1], (8, n)), wi)
    return steps, wr, wi


def _scan_rows(sre, sim, carry, lr, li, rows, reverse):
    steps, wr, wi = _scan_consts(lr, li, reverse)
    ng = rows // 8

    def step(gi, _):
        g = (ng - 1 - gi) if reverse else gi
        base = pl.multiple_of(g * 8, 8)
        xr = sre[pl.ds(base, 8), :]
        xi = sim[pl.ds(base, 8), :]
        for s, pr, pi in steps:
            sh = (8 - s) if reverse else s
            yr = pltpu.roll(xr, sh, 0)
            yi = pltpu.roll(xi, sh, 0)
            xr, xi = xr + pr * yr - pi * yi, xi + pr * yi + pi * yr
        cr = carry[0:1, :]
        ci = carry[1:2, :]
        xr, xi = xr + wr * cr - wi * ci, xi + wr * ci + wi * cr
        sre[pl.ds(base, 8), :] = xr
        sim[pl.ds(base, 8), :] = xi
        edge = 0 if reverse else 7
        carry[0:1, :] = xr[edge:edge + 1, :]
        carry[1:2, :] = xi[edge:edge + 1, :]
        return 0

    lax.fori_loop(0, ng, step, 0)


S5_Q = 4
S5_QL = S5_WIDTH // S5_Q
S5_QS = S5_LANES // S5_Q


def _s5_scan_fwd(proj, bq_re, bq_im, cq_re, cq_im, lbar, dskip, L, TB):
    nb = L // TB

    def body(u_ref, bre, bim, cre, cim, lb_ref, d_ref, y_ref, ck_ref, sre, sim, carry):
        i = pl.program_id(0)

        @pl.when(i == 0)
        def _():
            carry[...] = jnp.zeros(carry.shape, F32)

        ck_ref[0] = carry[...]
        u = u_ref[...]
        ub = u.astype(BF16)
        for q in range(S5_Q):
            uq = ub[:, q * S5_QL:(q + 1) * S5_QL]
            sre[:, q * S5_QS:(q + 1) * S5_QS] = _dot(uq, bre[q])
            sim[:, q * S5_QS:(q + 1) * S5_QS] = _dot(uq, bim[q])
        _scan_rows(sre, sim, carry, lb_ref[0:1, :], lb_ref[1:2, :], TB, False)
        for q in range(S5_Q):
            sl = slice(q * S5_QL, (q + 1) * S5_QL)
            ss = slice(q * S5_QS, (q + 1) * S5_QS)
            y_ref[:, sl] = (_dot(sre[:, ss].astype(BF16), cre[q]) - _dot(sim[:, ss].astype(BF16), cim[q])
                            + u[:, sl] * d_ref[:, sl])

    full = lambda a: pl.BlockSpec(a.shape, lambda i, nd=a.ndim: (0,) * nd)
    return _pcall(
        body, name="s5_scan_fwd", grid=(nb,),
        in_specs=[pl.BlockSpec((TB, S5_WIDTH), lambda i: (i, 0)), full(bq_re), full(bq_im), full(cq_re), full(cq_im),
                  full(lbar), full(dskip)],
        out_specs=[pl.BlockSpec((TB, S5_WIDTH), lambda i: (i, 0)), pl.BlockSpec((1, 8, S5_LANES), lambda i: (i, 0, 0))],
        out_shape=[jax.ShapeDtypeStruct((L, S5_WIDTH), F32), jax.ShapeDtypeStruct((nb, 8, S5_LANES), F32)],
        scratch_shapes=[pltpu.VMEM((TB, S5_LANES), F32), pltpu.VMEM((TB, S5_LANES), F32), pltpu.VMEM((8, S5_LANES), F32)],
        compiler_params=_cparams(1))(proj, bq_re, bq_im, cq_re, cq_im, lbar, dskip)


def _s5_scan_bwd(proj, dy, ck, bq_re, bq_im, cq_re, cq_im, lbar, dskip, L, TB):
    nb = L // TB
    ng = TB // 8

    def body(u_ref, dy_ref, ck_ref, bre, bim, cre, cim, lb_ref, d_ref,
             du_ref, dbre, dbim, dcre, dcim, dlb_ref, dd_ref, sre, sim, gre, gim, carry, gcarry):
        i = pl.program_id(0)

        @pl.when(i == 0)
        def _():
            gcarry[...] = jnp.zeros(gcarry.shape, F32)
            dbre[...] = jnp.zeros(dbre.shape, F32)
            dbim[...] = jnp.zeros(dbim.shape, F32)
            dcre[...] = jnp.zeros(dcre.shape, F32)
            dcim[...] = jnp.zeros(dcim.shape, F32)
            dlb_ref[...] = jnp.zeros(dlb_ref.shape, F32)
            dd_ref[...] = jnp.zeros(dd_ref.shape, F32)

        lr = lb_ref[0:1, :]
        li = lb_ref[1:2, :]
        u = u_ref[...]
        ub = u.astype(BF16)
        dy_v = dy_ref[...]
        dyb = dy_v.astype(BF16)
        carry[...] = ck_ref[0]
        for q in range(S5_Q):
            uq = ub[:, q * S5_QL:(q + 1) * S5_QL]
            dq = dyb[:, q * S5_QL:(q + 1) * S5_QL]
            ss = slice(q * S5_QS, (q + 1) * S5_QS)
            sre[:, ss] = _dot(uq, bre[q])
            sim[:, ss] = _dot(uq, bim[q])
            gre[:, ss] = _dot_nt(dq, cre[q])
            gim[:, ss] = -_dot_nt(dq, cim[q])
        _scan_rows(sre, sim, carry, lr, li, TB, False)
        _scan_rows(gre, gim, gcarry, lr, -li, TB, True)

        sub = lax.broadcasted_iota(jnp.int32, (8, S5_LANES), 0)
        c0r = ck_ref[0, 0:1, :]
        c0i = ck_ref[0, 1:2, :]

        def acc_step(g, acc):
            ar, ai = acc
            base = pl.multiple_of(g * 8, 8)
            pbase = pl.multiple_of(jnp.maximum(g - 1, 0) * 8, 8)
            first = g == 0
            lastr = jnp.where(first, c0r, sre[pl.ds(pbase, 8), :][7:8, :])
            lasti = jnp.where(first, c0i, sim[pl.ds(pbase, 8), :][7:8, :])
            spr = jnp.where(sub == 0, jnp.broadcast_to(lastr, sub.shape), pltpu.roll(sre[pl.ds(base, 8), :], 1, 0))
            spi = jnp.where(sub == 0, jnp.broadcast_to(lasti, sub.shape), pltpu.roll(sim[pl.ds(base, 8), :], 1, 0))
            gr = gre[pl.ds(base, 8), :]
            gi_ = gim[pl.ds(base, 8), :]
            return ar + gr * spr + gi_ * spi, ai - gr * spi + gi_ * spr

        z8 = jnp.zeros((8, S5_LANES), F32)
        ar, ai = lax.fori_loop(0, ng, acc_step, (z8, z8))
        dlb_ref[0:1, :] += jnp.sum(ar, axis=0, keepdims=True)
        dlb_ref[1:2, :] += jnp.sum(ai, axis=0, keepdims=True)

        dd_ref[...] += jnp.sum(dy_v * u, axis=0, keepdims=True)
        for q in range(S5_Q):
            sl = slice(q * S5_QL, (q + 1) * S5_QL)
            ss = slice(q * S5_QS, (q + 1) * S5_QS)
            grq = gre[:, ss].astype(BF16)
            giq = gim[:, ss].astype(BF16)
            du_ref[:, sl] = _dot_nt(grq, bre[q]) + _dot_nt(giq, bim[q]) + dy_v[:, sl] * d_ref[:, sl]
            dbre[q] += _dot_tn(ub[:, sl], grq)
            dbim[q] += _dot_tn(ub[:, sl], giq)
            dcre[q] += _dot_tn(sre[:, ss].astype(BF16), dyb[:, sl])
            dcim[q] -= _dot_tn(sim[:, ss].astype(BF16), dyb[:, sl])

    full = lambda a: pl.BlockSpec(a.shape, lambda i, nd=a.ndim: (0,) * nd)
    rev = lambda i: (nb - 1 - i, 0)
    sh = jax.ShapeDtypeStruct
    outs = [sh((L, S5_WIDTH), F32), sh(bq_re.shape, F32), sh(bq_im.shape, F32), sh(cq_re.shape, F32), sh(cq_im.shape, F32),
            sh((8, S5_LANES), F32), sh((1, S5_WIDTH), F32)]
    fo = lambda s: pl.BlockSpec(s.shape, lambda i, nd=len(s.shape): (0,) * nd)
    return _pcall(
        body, name="s5_scan_bwd", grid=(nb,),
        in_specs=[pl.BlockSpec((TB, S5_WIDTH), rev), pl.BlockSpec((TB, S5_WIDTH), rev),
                  pl.BlockSpec((1, 8, S5_LANES), lambda i: (nb - 1 - i, 0, 0)),
                  full(bq_re), full(bq_im), full(cq_re), full(cq_im), full(lbar), full(dskip)],
        out_specs=[pl.BlockSpec((TB, S5_WIDTH), rev)] + [fo(s) for s in outs[1:]],
        out_shape=outs,
        scratch_shapes=[pltpu.VMEM((TB, S5_LANES), F32)] * 4 + [pltpu.VMEM((8, S5_LANES), F32)] * 2,
        compiler_params=_cparams(1))(proj, dy, ck, bq_re, bq_im, cq_re, cq_im, lbar, dskip)


N_HEAD = RW_WIDTH // HEAD
_NN = (((2,), (1,)), ((0,), (0,)))
_NT = (((2,), (2,)), ((0,), (0,)))
_TN = (((1,), (1,)), ((0,), (0,)))


def _hi_lo(x):
    h = x.astype(BF16)
    return h, (x - h.astype(F32)).astype(BF16)


def _mm_acc(a, b, dims):
    ah, al = _hi_lo(a)
    bh, bl = _hi_lo(b)
    dg = lambda p, q: lax.dot_general(p, q, dims, preferred_element_type=F32)
    return dg(ah, bh) + dg(ah, bl) + dg(al, bh)


def _cumsum_rows(x, transpose):
    h, n, _ = x.shape
    ti = lax.broadcasted_iota(jnp.int32, (h, n, n), 1)
    tj = lax.broadcasted_iota(jnp.int32, (h, n, n), 2)
    m = ((tj >= ti) if transpose else (tj <= ti)).astype(BF16)
    acc, rem = None, x
    for s in range(3):
        part = rem.astype(BF16)
        t = lax.dot_general(m, part, _NN, preferred_element_type=F32)
        acc = t if acc is None else acc + t
        if s < 2:
            rem = rem - part.astype(F32)
    return acc


def _chunk_ops(diff):
    if not diff:
        return (lambda a, b: _mm_acc(a, b, _NN), lambda a, b: _mm_acc(a, b, _NT), lambda a, b: _mm_acc(a, b, _TN),
                lambda x: _cumsum_rows(x, False))

    @jax.custom_vjp
    def nn(a, b):
        return _mm_acc(a, b, _NN)

    nn.defvjp(lambda a, b: (_mm_acc(a, b, _NN), (a, b)), lambda r, d: (_mm_acc(d, r[1], _NT), _mm_acc(r[0], d, _TN)))

    @jax.custom_vjp
    def nt(a, b):
        return _mm_acc(a, b, _NT)

    nt.defvjp(lambda a, b: (_mm_acc(a, b, _NT), (a, b)), lambda r, d: (_mm_acc(d, r[1], _NN), _mm_acc(d, r[0], _TN)))

    @jax.custom_vjp
    def tn(a, b):
        return _mm_acc(a, b, _TN)

    tn.defvjp(lambda a, b: (_mm_acc(a, b, _TN), (a, b)), lambda r, d: (_mm_acc(r[1], d, _NT), _mm_acc(r[0], d, _NN)))

    @jax.custom_vjp
    def cums(x):
        return _cumsum_rows(x, False)

    cums.defvjp(lambda x: (_cumsum_rows(x, False), None), lambda _, d: (_cumsum_rows(d, True),))
    return nn, nt, tn, cums


def _wkv_chunk(s0, r, w, k, v, a, b, ops):
    nn, nt, tn, cums = ops
    h, n, _ = r.shape
    ti = lax.broadcasted_iota(jnp.int32, (h, n, n), 1)
    tj = lax.broadcasted_iota(jnp.int32, (h, n, n), 2)
    incl, strict = tj <= ti, tj < ti
    logw = jnp.log(w)
    cum = cums(logw)
    g_in, g_ex, g_inv = jnp.exp(cum), jnp.exp(cum - logw), jnp.exp(-cum)
    ae, re, bi, ki = a * g_ex, r * g_in, b * g_inv, k * g_inv
    tab = jnp.where(strict, nt(ae, bi), 0.0)
    tak = jnp.where(strict, nt(ae, ki), 0.0)
    qb = jnp.where(incl, nt(re, bi), 0.0)
    qk = jnp.where(incl, nt(re, ki), 0.0)
    u = nt(ae, s0) + nn(tak, v)
    npow = tab
    steps = max(1, (n - 1).bit_length())
    for i in range(steps):
        u = u + nn(npow, u)
        if i + 1 < steps:
            npow = nn(npow, npow)
    y = nt(re, s0) + nn(qb, u) + nn(qk, v)
    g_end = jnp.exp(jnp.sum(logw, axis=1, keepdims=True))
    s1 = s0 * g_end + tn(u, bi * g_end) + tn(v, ki * g_end)
    return y, s1


def _wkv_fwd(r, w, k, v, a, b, L):
    nc = L // CHUNK

    def body(r_ref, w_ref, k_ref, v_ref, a_ref, b_ref, y_ref, ck_ref, s_ref):
        c = pl.program_id(0)

        @pl.when(c == 0)
        def _():
            s_ref[...] = jnp.zeros(s_ref.shape, F32)

        s0 = s_ref[...]
        ck_ref[0] = s0
        y, s1 = _wkv_chunk(s0, r_ref[...], w_ref[...], k_ref[...], v_ref[...], a_ref[...], b_ref[...], _chunk_ops(False))
        y_ref[...] = y
        s_ref[...] = s1

    blk = pl.BlockSpec((N_HEAD, CHUNK, HEAD), lambda c: (0, c, 0))
    return _pcall(
        body, name="wkv_fwd", grid=(nc,), in_specs=[blk] * 6,
        out_specs=[blk, pl.BlockSpec((1, N_HEAD, HEAD, HEAD), lambda c: (c, 0, 0, 0))],
        out_shape=[jax.ShapeDtypeStruct((N_HEAD, L, HEAD), F32), jax.ShapeDtypeStruct((nc, N_HEAD, HEAD, HEAD), F32)],
        scratch_shapes=[pltpu.VMEM((N_HEAD, HEAD, HEAD), F32)],
        compiler_params=_cparams(1))(r, w, k, v, a, b)


def _wkv_bwd(r, w, k, v, a, b, dy, ck, L, deps=()):
    nc = L // CHUNK

    def body(r_ref, w_ref, k_ref, v_ref, a_ref, b_ref, dy_ref, ck_ref, *rest):
        dr_ref, dw_ref, dk_ref, dv_ref, da_ref, db_ref, ds_ref = rest[len(deps):]
        c = pl.program_id(0)

        @pl.when(c == 0)
        def _():
            ds_ref[...] = jnp.zeros(ds_ref.shape, F32)

        ops = _chunk_ops(True)
        _, vjp = jax.vjp(lambda *t: _wkv_chunk(*t, ops), ck_ref[0], r_ref[...], w_ref[...], k_ref[...], v_ref[...],
                         a_ref[...], b_ref[...])
        g = vjp((dy_ref[...], ds_ref[...]))
        ds_ref[...] = g[0]
        for o_ref, val in zip((dr_ref, dw_ref, dk_ref, dv_ref, da_ref, db_ref), g[1:]):
            o_ref[...] = val

    blk = pl.BlockSpec((N_HEAD, CHUNK, HEAD), lambda c: (0, nc - 1 - c, 0))
    sh = jax.ShapeDtypeStruct((N_HEAD, L, HEAD), F32)
    return _pcall(
        body, name="wkv_bwd", grid=(nc,),
        in_specs=[blk] * 7 + [pl.BlockSpec((1, N_HEAD, HEAD, HEAD), lambda c: (nc - 1 - c, 0, 0, 0))]
        + [pl.BlockSpec(d.shape, lambda c, nd=d.ndim: (0,) * nd) for d in deps],
        out_specs=[blk] * 6, out_shape=[sh] * 6,
        scratch_shapes=[pltpu.VMEM((N_HEAD, HEAD, HEAD), F32)],
        compiler_params=_cparams(1))(r, w, k, v, a, b, dy, ck, *deps)


TB = 256


def _bf(x):
    return x.astype(BF16)


def _inproj_fwd(x, norm_mix, w_in, L, deps=()):
    def fn(i, tv, cv):
        xn = _rms(tv[0], cv[0])
        return _dot(_bf(xn), cv[1]), xn

    return _tok_call("inproj_fwd", fn, L, TB, [(x, D_MODEL, 0)], [norm_mix, w_in], [(IN_COLS, F32), (D_MODEL, BF16)],
                     deps=deps)


def _s5_post_fn(glu_w, wtop, diff=True):
    mg = _mmc(glu_w, diff)
    mt = _mmc(wtop, diff) if wtop is not None else None

    def f(y, glu_b, e):
        z = _gelu(y)
        out = z * _sigmoid(mg(z) + glu_b + e)
        res = mt(out) if mt is not None else out
        return res, (z, out)

    return f


def _s5_post_fwd(y, glu_w, glu_b, L):
    def fn(i, tv, cv):
        out, _ = _s5_post_fn(cv[0], None, False)(tv[0], cv[1], 0.0)
        return (out,)

    return _tok_call("s5_post_fwd", fn, L, TB, [(y, S5_WIDTH, 0)], [glu_w, glu_b], [(S5_WIDTH, F32)])[0]


def _s5_post_bwd(y, dh1, glu_w, glu_b, wtop, L):
    def fn(i, tv, cv):
        e0 = jnp.zeros((TB, S5_WIDTH), F32)
        _, vjp, (z, out) = jax.vjp(_s5_post_fn(cv[0], cv[2]), tv[0], cv[1], e0, has_aux=True)
        dy, db, de = vjp(tv[1])
        return dy, z, de, out, db

    return _tok_call("s5_post_bwd", fn, L, TB, [(y, S5_WIDTH, 0), (dh1, D_MODEL, 0)], [glu_w, glu_b, wtop],
                     [(S5_WIDTH, F32), (S5_WIDTH, BF16), (S5_WIDTH, BF16), (S5_WIDTH, BF16)], [(1, S5_WIDTH)])


RW_COLBLK = ((RW_WIDTH, 1), (RW_WIDTH, 2), (RW_WIDTH, 3), (128, 16), (128, 17))
RW_MU = ((0, 512), (512, 1024), (1024, 1536), (1536, 1664), (1664, 1792))


def _rw_pre_fn(w2pad, a2pad, g2, diff=True):
    m_w, m_a, m_g = _mmc(w2pad, diff), _mmc(a2pad, diff), _mmc(g2, diff)
    seg = _segsum(_head_indicator(RW_WIDTH), diff)

    def f(zr, zk, zv, zwa, zg, w0, a0, k_k, k_a, e_w, e_a):
        wl_t = jnp.tanh(zwa)
        wlin = w0 + m_w(wl_t) + e_w
        w = -_softplus(-wlin) - 0.5
        decay = jnp.exp(-jnp.exp(w))
        a = _sigmoid(a0 + m_a(zwa) + e_a)
        sg = _sigmoid(zg)
        g = m_g(sg)
        kk = zk * k_k
        kkn = kk / jnp.maximum(jnp.sqrt(seg(kk * kk)), L2_EPS)
        kf = zk * (1.0 + (a - 1.0) * k_a)
        return (zr, decay, kf, zv, -kkn, kkn * a, g), (wl_t, sg)

    return f


def _rw_shifted(i, tv, mu):
    sub = lax.broadcasted_iota(jnp.int32, (TB, 1), 0)
    zs, dif = [], []
    for n in range(5):
        z = tv[n]
        last = jnp.where(i == 0, 0.0, tv[5 + n][7:8, :])
        prev = jnp.where(sub == 0, last, pltpu.roll(z, 1, 0))
        m = mu[:, RW_MU[n][0]:RW_MU[n][1]]
        zs.append(z + (prev - z) * m)
        dif.append(prev - z)
    return zs, dif


def _rw_tok_in(proj):
    return [(proj, wd, cb) for wd, cb in RW_COLBLK] + [(proj, wd, cb, "prev") for wd, cb in RW_COLBLK]


def _rw_pre_fwd(proj, mu, w0, a0, k_k, k_a, w2pad, a2pad, g2, L):
    def fn(i, tv, cv):
        zs, _ = _rw_shifted(i, tv, cv[0])
        outs, _ = _rw_pre_fn(cv[5], cv[6], cv[7], False)(*zs, cv[1], cv[2], cv[3], cv[4], 0.0, 0.0)
        return outs

    return _tok_call("rw_pre_fwd", fn, L, TB, _rw_tok_in(proj), [mu, w0, a0, k_k, k_a, w2pad, a2pad, g2],
                     [("heads", F32)] * 6 + [(RW_WIDTH, F32)])


def _rw_pre_bwd(proj, cots, mu, w0, a0, k_k, k_a, w2pad, a2pad, g2, L):
    def fn(i, tv, cv):
        zs, dif = _rw_shifted(i, tv[:10], cv[0])
        dr1, dr2, dw, dk1, dk2, dv1, dv2, da, db, dg = tv[10:]
        e0 = jnp.zeros((TB, RW_WIDTH), F32)
        _, vjp, (wl_t, sg) = jax.vjp(_rw_pre_fn(cv[5], cv[6], cv[7]), *zs, cv[1], cv[2], cv[3], cv[4], e0, e0, has_aux=True)
        g = vjp((dr1 + dr2, dw, dk1 + dk2, dv1 + dv2, da, db, dg))
        dzs = jnp.concatenate(g[:5], axis=1)
        dmu = jnp.concatenate([jnp.sum(g[n] * dif[n], axis=0, keepdims=True) for n in range(5)], axis=1)
        return dzs, wl_t, zs[3], sg, g[9], g[10], dmu, g[5], g[6], g[7], g[8]

    tok_in = _rw_tok_in(proj) + [((c,) if c.ndim == 3 else (c, RW_WIDTH, 0)) for c in cots]
    return _tok_call("rw_pre_bwd", fn, L, TB, tok_in, [mu, w0, a0, k_k, k_a, w2pad, a2pad, g2],
                     [(SHIFT_COLS, F32), (128, BF16), (128, BF16), (128, BF16), (RW_WIDTH, BF16), (RW_WIDTH, BF16)],
                     [(1, SHIFT_COLS)] + [(1, RW_WIDTH)] * 4)


def _rw_post_fn(wbot, diff=True):
    seg = _segsum(_head_indicator(RW_WIDTH), diff)
    mb = _mmc(wbot, diff) if wbot is not None else None

    def f(y, r, kf, v, g, ln_w, ln_b, r_k):
        mean = seg(y) * (1.0 / HEAD)
        yc = y - mean
        var = seg(yc * yc) * (1.0 / HEAD)
        yn = yc * lax.rsqrt(var + GN_EPS) * ln_w + ln_b
        bonus = seg(r * kf * r_k) * v
        out = (yn + bonus) * g
        res = mb(out) if mb is not None else out
        return res, out

    return f


def _rw_post_fwd(y, r, kf, v, g, ln_w, ln_b, r_k, L):
    def fn(i, tv, cv):
        out, _ = _rw_post_fn(None, False)(*tv, *cv)
        return (out,)

    return _tok_call("rw_post_fwd", fn, L, TB, [(t,) for t in (y, r, kf, v)] + [(g, RW_WIDTH, 0)], [ln_w, ln_b, r_k],
                     [(RW_WIDTH, F32)])[0]


def _rw_post_bwd(y, r, kf, v, g, dh1, ln_w, ln_b, r_k, wbot, L):
    def fn(i, tv, cv):
        _, vjp, out = jax.vjp(_rw_post_fn(cv[3]), *tv[:5], cv[0], cv[1], cv[2], has_aux=True)
        gr = vjp(tv[5])
        return gr[0], gr[1], gr[2], gr[3], gr[4], out, gr[5], gr[6], gr[7]

    return _tok_call("rw_post_bwd", fn, L, TB, [(t,) for t in (y, r, kf, v)] + [(g, RW_WIDTH, 0), (dh1, D_MODEL, 0)],
                     [ln_w, ln_b, r_k, wbot], [("heads", F32)] + [(RW_WIDTH, F32)] * 4 + [(RW_WIDTH, BF16)], [(1, RW_WIDTH)] * 3)


def _ffn_fn(w1, w3, w2, diff=True):
    m1, m3, m2 = _mmc(w1, diff), _mmc(w3, diff), _mmc(w2, diff)

    def f(h1, norm_ffn, e1, e3):
        hn = _rms(h1, norm_ffn)
        a1 = m1(hn) + e1
        a3 = m3(hn) + e3
        hm = a1 * _sigmoid(a1) * a3
        return h1 + m2(hm), (hn, hm)

    return f


TB_FFN = 256


def _mixffn_fwd(x, s5_out, rw_out, wtop, wbot, norm_ffn, w1, w3, w2, L):
    def fn(i, tv, cv):
        h1 = tv[0] + _dot(_bf(tv[1]), cv[0]) + _dot(_bf(tv[2]), cv[1])
        h2, _ = _ffn_fn(cv[3], cv[4], cv[5], False)(h1, cv[2], 0.0, 0.0)
        return h1, h2

    return _tok_call("mixffn_fwd", fn, L, TB_FFN, [(x, D_MODEL, 0), (s5_out, S5_WIDTH, 0), (rw_out, RW_WIDTH, 0)],
                     [wtop, wbot, norm_ffn, w1, w3, w2], [(D_MODEL, F32), (D_MODEL, F32)])


def _ffn_bwd(h1, dh2, norm_ffn, w1, w3, w2, L):
    def fn(i, tv, cv):
        e0 = jnp.zeros((TB_FFN, FFN_HIDDEN), F32)
        _, vjp, (hn, hm) = jax.vjp(_ffn_fn(cv[1], cv[2], cv[3]), tv[0], cv[0], e0, e0, has_aux=True)
        dh1, dn, d1, d3 = vjp(tv[1])
        return dh1, d1, d3, hm, hn, dn

    return _tok_call("ffn_bwd", fn, L, TB_FFN, [(h1, D_MODEL, 0), (dh2, D_MODEL, 0)], [norm_ffn, w1, w3, w2],
                     [(D_MODEL, F32), (FFN_HIDDEN, BF16), (FFN_HIDDEN, BF16), (FFN_HIDDEN, BF16), (D_MODEL, BF16)],
                     [(1, D_MODEL)])


def _ple_loss_fb(h2, p, target, norm_ple, final_norm, wg, wu, L):
    def fn(i, tv, cv):
        mgate, mup = _mmc(cv[2]), _mmc(cv[3], False)

        def f(h2_, norm_ple_, final_norm_, eg, eu):
            hn = _rms(h2_, norm_ple_)
            gate = _sigmoid(mgate(hn) + eg)
            h3 = h2_ + gate * (mup(tv[1]) + eu)
            out = _rms(h3, final_norm_)
            d = out - tv[2]
            return 0.5 * jnp.sum(jnp.mean(d * d, axis=-1, keepdims=True)), hn

        e0 = jnp.zeros((TB, D_MODEL), F32)
        loss, vjp, hn = jax.vjp(f, tv[0], cv[0], cv[1], e0, e0, has_aux=True)
        dh2, dnp, dfn, deg, deu = vjp(jnp.ones((), F32))
        return dh2, deg, deu, hn, jnp.full((8, 128), loss, F32), dnp, dfn

    return _tok_call("ple_loss_fb", fn, L, TB, [(h2, D_MODEL, 0), (p, PLE_DIM, 0), (target, D_MODEL, 0)],
                     [norm_ple, final_norm, wg, wu], [(D_MODEL, F32), (D_MODEL, BF16), (D_MODEL, BF16), (D_MODEL, BF16)],
                     [(8, 128), (1, D_MODEL), (1, D_MODEL)])


def _inproj_bwd(x, dh1, du, dzs, norm_mix, mu, w_u, w_z, L):
    nb = L // TB

    def fn(i, tv, cv):
        sub = lax.broadcasted_iota(jnp.int32, (TB, 1), 0)
        m = cv[1]
        b = tv[3] * m
        nxt = jnp.where(i == nb - 1, 0.0, tv[4][0:1, :] * m)
        dz = tv[3] * (1.0 - m) + jnp.where(sub == TB - 1, nxt, pltpu.roll(b, TB - 1, 0))
        dub, dzb = _bf(tv[2]), _bf(dz)
        dxn = _dot_nt(dub, cv[2]) + _dot_nt(dzb, cv[3])
        _, vjp = jax.vjp(_rms, tv[0], cv[0])
        dx, dn = vjp(dxn)
        return tv[1] + dx, jnp.concatenate([dub, dzb], axis=1), dn

    return _tok_call("inproj_bwd", fn, L, TB,
                     [(x, D_MODEL, 0), (dh1, D_MODEL, 0), (du, S5_WIDTH, 0), (dzs, SHIFT_COLS, 0), (dzs, SHIFT_COLS, 0, "next")],
                     [norm_mix, mu, w_u, w_z], [(D_MODEL, F32), (IN_COLS, BF16)], [(1, D_MODEL)])


def _eye8(dt):
    return jnp.eye(8, dtype=dt)


def _quarter_b(bb):
    return jnp.einsum("hg,qgcp->qhcgp", _eye8(bb.dtype), bb.reshape(S5_Q, 8, S5_GROUP, S5_STATE)).reshape(S5_Q, S5_QL, S5_QS)


def _unquarter_b(d):
    return jnp.einsum("qhcgp,hg->qgcp", d.reshape(S5_Q, 8, S5_GROUP, 8, S5_STATE), _eye8(d.dtype)).reshape(
        S5_GROUPS, S5_GROUP, S5_STATE)


def _quarter_c(c):
    return jnp.einsum("gh,qgcp->qgphc", _eye8(c.dtype), c.reshape(S5_Q, 8, S5_GROUP, S5_STATE)).reshape(S5_Q, S5_QS, S5_QL)


def _unquarter_c(d):
    return jnp.einsum("qgphc,gh->qgcp", d.reshape(S5_Q, 8, S5_STATE, 8, S5_GROUP), _eye8(d.dtype)).reshape(
        S5_GROUPS, S5_GROUP, S5_STATE)


def _local_step(x, p, target, W, late_weights=None, grads_ready=None, first_dep=None):
    L = x.shape[0]
    r2 = lambda v: v.reshape(1, -1)
    w_in = W["w_in"]
    w2pad = jnp.pad(W["rw_w2"], ((0, 64), (0, 0)))
    a2pad = jnp.pad(W["rw_a2"], ((64, 0), (0, 0)))
    mu = r2(W["rw_shift_mu"])
    rw_vec = [r2(W[n]) for n in ("rw_w0", "rw_a0", "rw_k_k", "rw_k_a")]
    ln_w, ln_b, r_k = r2(W["rw_ln_w"]), r2(W["rw_ln_b"]), r2(W["rw_r_k"])

    lam_re, lam_im = W["s5_lam_re"], W["s5_lam_im"]
    log_step = W["s5_log_step"].reshape(S5_GROUPS, 1)
    bt_re, bt_im = W["s5_b_re"].transpose(0, 2, 1), W["s5_b_im"].transpose(0, 2, 1)
    lb_re, lb_im, bb_re, bb_im = _s5_param_fwd(lam_re, lam_im, log_step, bt_re, bt_im)
    bq_re, bq_im = _quarter_b(bb_re).astype(BF16), _quarter_b(bb_im).astype(BF16)
    cq_re, cq_im = _quarter_c(W["s5_c_re"]).astype(BF16), _quarter_c(W["s5_c_im"]).astype(BF16)
    lbar = jnp.concatenate([lb_re.reshape(1, -1), lb_im.reshape(1, -1), jnp.zeros((6, S5_LANES), F32)], axis=0)
    dskip = r2(W["s5_d"])
    glu_b = r2(W["s5_glu_b"])
    norm_mix, norm_ffn, norm_ple, final_norm = (r2(W[n]) for n in ("norm_mix", "norm_ffn", "norm_ple", "final_norm"))

    proj, xn = _inproj_fwd(x, norm_mix, w_in, L, () if first_dep is None else (first_dep,))
    y_s5, ck5 = _s5_scan_fwd(proj, bq_re, bq_im, cq_re, cq_im, lbar, dskip, L, TB)
    s5_out = _s5_post_fwd(y_s5, W["s5_glu_w"], glu_b, L)
    r, wd, kf, v, a_s, b_s, g = _rw_pre_fwd(proj, mu, *rw_vec, w2pad, a2pad, W["rw_g2"], L)
    scan_in = (r, wd, kf, v, a_s, b_s)
    y_wkv, ckw = _wkv_fwd(*scan_in, L)
    rw_out = _rw_post_fwd(y_wkv, r, kf, v, g, ln_w, ln_b, r_k, L)
    if late_weights is not None:
        W = dict(W, **late_weights(rw_out))
    wtop, wbot = W["w_out"][:S5_WIDTH], W["w_out"][S5_WIDTH:]
    h1, h2 = _mixffn_fwd(x, s5_out, rw_out, wtop, wbot, norm_ffn, W["ffn_w1"], W["ffn_w3"], W["ffn_w2"], L)

    G = {}
    dh2, deg, deu, hn_ple, loss_acc, G["norm_ple"], G["final_norm"] = _ple_loss_fb(
        h2, p, target, norm_ple, final_norm, W["ple_gate_w"], W["ple_up_w"], L)
    dh1, da1, da3, hm, hn_ffn, G["norm_ffn"] = _ffn_bwd(h1, dh2, norm_ffn, W["ffn_w1"], W["ffn_w3"], W["ffn_w2"], L)
    dy_s5, z_bf, dgp, s5o_bf, G["s5_glu_b"] = _s5_post_bwd(y_s5, dh1, W["s5_glu_w"], glu_b, wtop, L)
    dy_wkv, dr2, dk2, dv2, dg, rwo_bf, G["rw_ln_w"], G["rw_ln_b"], G["rw_r_k"] = _rw_post_bwd(
        y_wkv, r, kf, v, g, dh1, ln_w, ln_b, r_k, wbot, L)
    G["ffn_w1"] = _mm_tn("dw_ffn_w1", hn_ffn, da1)
    G["ffn_w3"] = _mm_tn("dw_ffn_w3", hn_ffn, da3)
    G["ffn_w2"] = _mm_tn("dw_ffn_w2", hm, dh2)
    G["ple_gate_w"] = _mm_tn("dw_ple_gate", hn_ple, deg)
    G["w_out"] = jnp.concatenate([_mm_tn("dw_out_top", s5o_bf, dh1), _mm_tn("dw_out_bot", rwo_bf, dh1)], axis=0)
    dep = grads_ready(G) if grads_ready is not None else None
    G["ple_up_w"] = _mm_tn("dw_ple_up", p, deu)
    G["s5_glu_w"] = _mm_tn("dw_s5_glu", z_bf, dgp)
    dr1, dwd, dk1, dv1, da_s, db_s = _wkv_bwd(*scan_in, dy_wkv, ckw, L, () if dep is None else (dep,))
    (dzs, wl_t, zwa, sg, dwlin, dalin, G["rw_shift_mu"], G["rw_w0"], G["rw_a0"], G["rw_k_k"], G["rw_k_a"]) = _rw_pre_bwd(
        proj, (dr1, dr2, dwd, dk1, dk2, dv1, dv2, da_s, db_s, dg), mu, *rw_vec, w2pad, a2pad, W["rw_g2"], L)
    G["rw_w2"] = _mm_tn("dw_rw_w2", wl_t, dwlin)[:64]
    G["rw_a2"] = _mm_tn("dw_rw_a2", zwa, dalin)[64:]
    G["rw_g2"] = _mm_tn("dw_rw_g2", sg, dg)
    du, dbq_re, dbq_im, dcq_re, dcq_im, dlbar, G["s5_d"] = _s5_scan_bwd(
        proj, dy_s5, ck5, bq_re, bq_im, cq_re, cq_im, lbar, dskip, L, TB)
    G["s5_c_re"], G["s5_c_im"] = _unquarter_c(dcq_re), _unquarter_c(dcq_im)
    d_lam_re, d_lam_im, d_ls, d_bt_re, d_bt_im = _s5_param_bwd(
        lam_re, lam_im, log_step, bt_re, bt_im, dlbar[0].reshape(S5_GROUPS, S5_STATE), dlbar[1].reshape(S5_GROUPS, S5_STATE),
        _unquarter_b(dbq_re), _unquarter_b(dbq_im))
    G["s5_lam_re"], G["s5_lam_im"], G["s5_log_step"] = d_lam_re, d_lam_im, d_ls.reshape(S5_GROUPS)
    G["s5_b_re"], G["s5_b_im"] = d_bt_re.transpose(0, 2, 1), d_bt_im.transpose(0, 2, 1)
    dx, dproj, G["norm_mix"] = _inproj_bwd(x, dh1, du, dzs, norm_mix, mu, w_in[:, :S5_WIDTH], w_in[:, S5_WIDTH:], L)
    G["w_in"] = _mm_tn("dw_in", xn, dproj)
    return loss_acc[0, 0], dx, G


MESH_AXES = ("x", "y", "c")
_ANY = pl.BlockSpec(memory_space=pl.ANY)


def _all_gather(name, shards):
    nt = len(shards)

    def body(*refs):
        x_refs, out_refs = refs[:nt], refs[nt:2 * nt]
        send_sems, recv_sems, local_sems = refs[2 * nt:]
        x, y, c = lax.axis_index("x"), lax.axis_index("y"), lax.axis_index("c")
        me, sibling = (x, y, c), (x, y, 1 - c)
        chips = [(1 - x, y), (x, 1 - y), (1 - x, 1 - y)]

        def rows(t, px, py, pc):
            m_per = shards[t].shape[0]
            return out_refs[t].at[pl.ds((4 * px + 2 * py + pc) * m_per, m_per), :]

        def copy(t, k, block, to, src=None):
            return pltpu.make_async_remote_copy(
                src_ref=rows(t, *block) if src is None else src, dst_ref=rows(t, *block),
                send_sem=send_sems.at[7 * t + k], recv_sem=recv_sems.at[7 * t + k],
                device_id=to, device_id_type=pl.DeviceIdType.MESH)

        mine = [pltpu.make_async_copy(x_refs[t], rows(t, *me), local_sems.at[t]) for t in range(nt)]
        for cp in mine:
            cp.start()
        first = []
        for t in range(nt):
            first.append(copy(t, 0, me, sibling, src=x_refs[t]))
            first += [copy(t, 1 + j, me, (*chip, c), src=x_refs[t]) for j, chip in enumerate(chips)]
        for cp in first:
            cp.start()
        passed = []
        for t in range(nt):
            for j, chip in enumerate(chips):
                copy(t, 1 + j, (*chip, c), me).wait_recv()
                fwd = copy(t, 4 + j, (*chip, c), sibling)
                fwd.start()
                passed.append(fwd)
        for t in range(nt):
            copy(t, 0, sibling, me).wait_recv()
            for j, chip in enumerate(chips):
                copy(t, 4 + j, (*chip, 1 - c), me).wait_recv()
        for cp in first + passed:
            cp.wait_send()
        for cp in mine:
            cp.wait()

    return _pcall(body, name=name,
                  out_shape=[jax.ShapeDtypeStruct((N_DEV * a.shape[0], a.shape[1]), a.dtype) for a in shards],
                  in_specs=[_ANY] * nt, out_specs=[_ANY] * nt,
                  scratch_shapes=[pltpu.SemaphoreType.DMA((7 * nt,)), pltpu.SemaphoreType.DMA((7 * nt,)),
                                  pltpu.SemaphoreType.DMA((nt,))])(*shards)


_HBM = pl.BlockSpec(memory_space=pltpu.HBM)
_SEM = pl.BlockSpec(memory_space=pltpu.SEMAPHORE)
_EFFECT = pltpu.SideEffectType.DATAFLOW_SIDE_EFFECTING


def _peer_of(k):
    x, y, c = lax.axis_index("x"), lax.axis_index("y"), lax.axis_index("c")
    px, py, pc = x ^ ((k >> 2) & 1), y ^ ((k >> 1) & 1), c ^ (k & 1)
    return (px, py, pc), 4 * px + 2 * py + pc, 4 * x + 2 * y + c


def _direct_copy(t, k, src_refs, land_refs, send_sems, recv_sems, rows_of, gather):
    dev, peer, me = _peer_of(k)
    m = rows_of[t]
    src = src_refs[t] if gather else src_refs[t].at[pl.ds(peer * m, m), :]
    return pltpu.make_async_remote_copy(
        src_ref=src, dst_ref=land_refs[t].at[pl.ds(me * m, m), :],
        send_sem=send_sems.at[7 * t + k - 1], recv_sem=recv_sems.at[7 * t + k - 1],
        device_id=dev, device_id_type=pl.DeviceIdType.MESH)


def _direct_landing(t, k, src_refs, land_refs, send_sems, recv_sems, rows_of, gather):
    dev, peer, me = _peer_of(k)
    m = rows_of[t]
    src = src_refs[t] if gather else src_refs[t].at[pl.ds(me * m, m), :]
    return pltpu.make_async_remote_copy(
        src_ref=src, dst_ref=land_refs[t].at[pl.ds(peer * m, m), :],
        send_sem=send_sems.at[7 * t + k - 1], recv_sem=recv_sems.at[7 * t + k - 1],
        device_id=dev, device_id_type=pl.DeviceIdType.MESH)


def _direct_start(name, srcs, gather):
    nt = len(srcs)
    rows_of = [a.shape[0] if gather else a.shape[0] // N_DEV for a in srcs]
    lands = [pltpu.with_memory_space_constraint(lax.empty((N_DEV * m, a.shape[1]), a.dtype), pltpu.HBM)
             for a, m in zip(srcs, rows_of)]

    def body(*refs):
        src_refs, land_refs = refs[:nt], refs[nt:2 * nt]
        send_sems, recv_sems = refs[2 * nt], refs[2 * nt + 1]
        token = refs[-1]
        for t in range(nt):
            for k in range(1, N_DEV):
                _direct_copy(t, k, src_refs, land_refs, send_sems, recv_sems, rows_of, gather).start()
        token[...] = jnp.zeros(token.shape, F32)

    out = _pcall(
        body, name=name,
        out_shape=(pltpu.SemaphoreType.DMA((7 * nt,)), pltpu.SemaphoreType.DMA((7 * nt,)),
                   *[pltpu.HBM(a.shape, a.dtype) for a in srcs], *[pltpu.HBM(a.shape, a.dtype) for a in lands],
                   jax.ShapeDtypeStruct((8, 128), F32)),
        in_specs=(_HBM,) * (2 * nt),
        out_specs=(_SEM, _SEM) + (_HBM,) * (2 * nt) + (pl.BlockSpec(memory_space=pltpu.VMEM),),
        input_output_aliases={i: 2 + i for i in range(2 * nt)},
        compiler_params=pltpu.CompilerParams(has_side_effects=_EFFECT),
    )(*[pltpu.with_memory_space_constraint(a, pltpu.HBM) for a in srcs], *lands)
    return (out[0], out[1], list(out[2:2 + nt]), list(out[2 + nt:2 + 2 * nt]), rows_of, gather), out[-1]


def _direct_wait(name, handle, after):
    send_sems, recv_sems, srcs, lands, rows_of, gather = handle
    nt = len(srcs)

    def body(*refs):
        src_refs, land_refs = refs[:nt], refs[nt:2 * nt]
        s_sems, r_sems = refs[2 * nt], refs[2 * nt + 1]
        for t in range(nt):
            for k in range(1, N_DEV):
                _direct_copy(t, k, src_refs, land_refs, s_sems, r_sems, rows_of, gather).wait_send()
                _direct_landing(t, k, src_refs, land_refs, s_sems, r_sems, rows_of, gather).wait_recv()

    out = _pcall(
        body, name=name,
        out_shape=tuple(pltpu.HBM(a.shape, a.dtype) for a in srcs) + tuple(pltpu.HBM(a.shape, a.dtype) for a in lands),
        in_specs=(_HBM,) * (2 * nt) + (_SEM, _SEM, pl.BlockSpec(memory_space=pl.ANY)),
        out_specs=(_HBM,) * (2 * nt),
        input_output_aliases={i: i for i in range(2 * nt)},
        compiler_params=pltpu.CompilerParams(has_side_effects=_EFFECT),
    )(*srcs, *lands, send_sems, recv_sems, after)
    return list(out[:nt]), list(out[nt:])


def _adamw_sharded(name, own, parts, w, m, v, rb):
    R, N = own.shape

    def body(o_ref, p_ref, w_ref, m_ref, v_ref, g_ref, d_ref, nm_ref, nv_ref):
        me = 4 * lax.axis_index("x") + 2 * lax.axis_index("y") + lax.axis_index("c")
        g = o_ref[...]
        for k in range(1, N_DEV):
            g = g + p_ref[me ^ k]
        nm = ADAM_B1 * m_ref[...] + (1.0 - ADAM_B1) * g
        nv = ADAM_B2 * v_ref[...] + (1.0 - ADAM_B2) * (g * g)
        m_hat = nm / (1.0 - ADAM_B1 ** ADAM_STEP)
        v_hat = nv / (1.0 - ADAM_B2 ** ADAM_STEP)
        g_ref[...] = g
        d_ref[...] = -ADAM_LR * (m_hat / (jnp.sqrt(v_hat) + ADAM_EPS) + ADAM_WD * w_ref[...])
        nm_ref[...] = nm
        nv_ref[...] = nv

    blk = pl.BlockSpec((rb, N), lambda i: (i, 0))
    sh = jax.ShapeDtypeStruct((R, N), F32)
    return _pcall(body, name=name, grid=(R // rb,),
                  in_specs=[blk, pl.BlockSpec((N_DEV, rb, N), lambda i: (0, i, 0)), blk, blk, blk],
                  out_specs=[blk] * 4, out_shape=[sh] * 4, compiler_params=_cparams(1))(own, parts, w, m, v)


def _adamw(name, parts, w, m, v, rb):
    _, R, N = parts.shape

    def body(p_ref, w_ref, m_ref, v_ref, g_ref, d_ref, nm_ref, nv_ref):
        g = p_ref[0]
        for s in range(1, N_DEV):
            g = g + p_ref[s]
        nm = ADAM_B1 * m_ref[...] + (1.0 - ADAM_B1) * g
        nv = ADAM_B2 * v_ref[...] + (1.0 - ADAM_B2) * (g * g)
        m_hat = nm / (1.0 - ADAM_B1 ** ADAM_STEP)
        v_hat = nv / (1.0 - ADAM_B2 ** ADAM_STEP)
        g_ref[...] = g
        d_ref[...] = -ADAM_LR * (m_hat / (jnp.sqrt(v_hat) + ADAM_EPS) + ADAM_WD * w_ref[...])
        nm_ref[...] = nm
        nv_ref[...] = nv

    blk = pl.BlockSpec((rb, N), lambda i: (i, 0))
    sh = jax.ShapeDtypeStruct((R, N), F32)
    return _pcall(body, name=name, grid=(R // rb,), in_specs=[pl.BlockSpec((N_DEV, rb, N), lambda i: (0, i, 0)), blk, blk, blk],
                  out_specs=[blk] * 4, out_shape=[sh] * 4, compiler_params=_cparams(1))(parts, w, m, v)


EARLY = (("w_in", True),)
LATE = (("ffn_w1", True), ("ffn_w3", True), ("w_out", False), ("ffn_w2", False), ("ple_gate_w", False))
MISC = (("s5_glu_w", False), ("rw_w2", True), ("rw_a2", True), ("rw_g2", True), ("ple_up_w", True))
SHARDED_NAMES = tuple(n for n, _ in EARLY + LATE + MISC)
PACK_COLS = 1024
SMALL_ROWS = 144
WEIGHT_NAMES = ("norm_mix", "w_in", "s5_lam_re", "s5_lam_im", "s5_log_step", "s5_b_re", "s5_b_im", "s5_c_re", "s5_c_im", "s5_d",
                "s5_glu_w", "s5_glu_b", "rw_shift_mu", "rw_w0", "rw_w2", "rw_a0", "rw_a2", "rw_g2", "rw_k_k", "rw_k_a", "rw_r_k",
                "rw_ln_w", "rw_ln_b", "w_out", "norm_ffn", "ffn_w1", "ffn_w3", "ffn_w2", "norm_ple", "ple_gate_w", "ple_up_w",
                "final_norm")
SMALL_NAMES = tuple(n for n in WEIGHT_NAMES if n not in SHARDED_NAMES)
ARG_NAMES = ("x", "p") + WEIGHT_NAMES + ("loss_target",) + tuple("m_" + n for n in WEIGHT_NAMES) + tuple("v_" + n for n in WEIGHT_NAMES)


def _travel(a, tr):
    return a.T if tr else a


def _pack_misc(blocks):
    lead = blocks[0].shape[:-2]
    return jnp.concatenate([b.reshape(lead + (-1, PACK_COLS)) for b in blocks], axis=len(lead))


def _unpack_misc(packed, shapes):
    lead = packed.shape[:-2]
    out, off = [], 0
    for r, c in shapes:
        n = r * c // PACK_COLS
        out.append(lax.slice_in_dim(packed, off, off + n, axis=len(lead)).reshape(lead + (r, c)))
        off += n
    return out


def _pack_small(arrs):
    flat = jnp.concatenate([a.reshape(-1).astype(F32) for a in arrs])
    return jnp.pad(flat, (0, SMALL_ROWS * PACK_COLS - flat.shape[0])).reshape(SMALL_ROWS, PACK_COLS)


def _kernel_impl(ins):
    x, p, target = ins["x"][0], ins["p"][0, 0], ins["loss_target"][0]
    me = 4 * lax.axis_index("x") + 2 * lax.axis_index("y") + lax.axis_index("c")
    small = {n: (ins[n] if n == "final_norm" else ins[n][0]) for n in SMALL_NAMES}
    trav = lambda pre, n, tr: _travel(ins[pre + n][0], tr)
    misc_shapes = [trav("", n, tr).shape for n, tr in MISC]

    late_handle, late_token = _direct_start("ag_late_start", [trav("", n, tr).astype(BF16) for n, tr in LATE], True)
    early = _all_gather("ag_early", [trav("", n, tr).astype(BF16) for n, tr in EARLY]
                        + [_pack_misc([trav("", n, tr).astype(BF16) for n, tr in MISC])])
    W = dict(small)
    for (n, tr), g in zip(EARLY, early):
        W[n] = _travel(g, tr)
    for (n, tr), g in zip(MISC, _unpack_misc(early[-1].reshape(N_DEV, -1, PACK_COLS), misc_shapes)):
        W[n] = _travel(g.reshape(-1, g.shape[-1]), tr)

    def late_weights(after):
        shards, lands = _direct_wait("ag_late_wait", late_handle, after)
        full = [lax.dynamic_update_slice_in_dim(ld, sh, me * sh.shape[0], axis=0) for ld, sh in zip(lands, shards)]
        return {n: _travel(g, tr) for (n, tr), g in zip(LATE, full)}

    gt = lambda G, n, tr: _travel(G[n], tr)
    started = {}

    def grads_ready(G):
        started["h"], token = _direct_start("grad_late_start", [gt(G, n, tr) for n, tr in LATE], False)
        return token

    loss_part, dx, G = _local_step(x, p, target, W, late_weights, grads_ready, late_token)

    misc_g = _pack_misc([gt(G, n, tr).reshape((N_DEV,) + shp) for (n, tr), shp in zip(MISC, misc_shapes)])
    early_handle, _ = _direct_start("grad_early_start", [gt(G, n, tr) for n, tr in EARLY] + [misc_g.reshape(-1, PACK_COLS)], False)
    small_own = _pack_small([G[n] for n in SMALL_NAMES])
    small_handle, small_token = _direct_start("grad_small_start", [small_own], True)
    late_src, late_land = _direct_wait("grad_late_wait", started["h"], small_token)

    outs = {}

    def emit(names_shapes, res):
        for tag, val in zip(("grad_", "delta_", "new_m_", "new_v_"), res):
            for n, v in names_shapes(val):
                outs[tag + n] = v

    def sharded_update(n, tr, src, land):
        rows = src.shape[0] // N_DEV
        own = lax.dynamic_slice_in_dim(src, me * rows, rows, axis=0)
        res = _adamw_sharded("adamw_" + n, own, land.reshape(N_DEV, rows, land.shape[1]),
                             trav("", n, tr), trav("m_", n, tr), trav("v_", n, tr), _pick_rows(rows))
        emit(lambda val: [(n, _travel(val, tr).reshape(ins[n].shape))], res)
        return res[0]

    for (n, tr), src, land in zip(LATE, late_src, late_land):
        last = sharded_update(n, tr, src, land)
    early_src, early_land = _direct_wait("grad_early_wait", early_handle, last)
    for (n, tr), src, land in zip(EARLY, early_src[:-1], early_land[:-1]):
        sharded_update(n, tr, src, land)
    pm = lambda pre: _pack_misc([trav(pre, n, tr) for n, tr in MISC])
    rows = early_src[-1].shape[0] // N_DEV
    res = _adamw_sharded("adamw_misc", lax.dynamic_slice_in_dim(early_src[-1], me * rows, rows, axis=0),
                         early_land[-1].reshape(N_DEV, rows, PACK_COLS), pm(""), pm("m_"), pm("v_"), rows)
    emit(lambda val: [(n, _travel(b, tr).reshape(ins[n].shape)) for (n, tr), b in zip(MISC, _unpack_misc(val, misc_shapes))], res)
    ps = lambda pre: _pack_small([ins[pre + n] for n in SMALL_NAMES])
    small_src, small_land = _direct_wait("grad_small_wait", small_handle, res[0])
    gsm = lax.dynamic_update_slice_in_dim(small_land[0], small_src[0], me * SMALL_ROWS, axis=0)
    res = _adamw("adamw_replicated", gsm.reshape(N_DEV, SMALL_ROWS, PACK_COLS), ps(""), ps("m_"), ps("v_"), SMALL_ROWS)

    def split_small(val):
        flat, off, o = val.reshape(-1), 0, []
        for n in SMALL_NAMES:
            o.append((n, flat[off:off + ins[n].size].reshape(ins[n].shape)))
            off += ins[n].size
        return o

    emit(split_small, res)
    loss = lax.psum(loss_part, MESH_AXES)
    res = [loss, dx[None]]
    for tag in ("grad_", "delta_", "new_m_", "new_v_"):
        res += [outs[tag + n] for n in WEIGHT_NAMES]
    return tuple(res)


def _pick_rows(r):
    best = 8
    for b in range(8, 257, 8):
        if r % b == 0:
            best = b
    return best


def kernel(x, p, norm_mix, w_in, s5_lam_re, s5_lam_im, s5_log_step, s5_b_re, s5_b_im, s5_c_re, s5_c_im, s5_d, s5_glu_w, s5_glu_b, rw_shift_mu, rw_w0, rw_w2, rw_a0, rw_a2, rw_g2, rw_k_k, rw_k_a, rw_r_k, rw_ln_w, rw_ln_b, w_out, norm_ffn, ffn_w1, ffn_w3, ffn_w2, norm_ple, ple_gate_w, ple_up_w, final_norm, loss_target, m_norm_mix, m_w_in, m_s5_lam_re, m_s5_lam_im, m_s5_log_step, m_s5_b_re, m_s5_b_im, m_s5_c_re, m_s5_c_im, m_s5_d, m_s5_glu_w, m_s5_glu_b, m_rw_shift_mu, m_rw_w0, m_rw_w2, m_rw_a0, m_rw_a2, m_rw_g2, m_rw_k_k, m_rw_k_a, m_rw_r_k, m_rw_ln_w, m_rw_ln_b, m_w_out, m_norm_ffn, m_ffn_w1, m_ffn_w3, m_ffn_w2, m_norm_ple, m_ple_gate_w, m_ple_up_w, m_final_norm, v_norm_mix, v_w_in, v_s5_lam_re, v_s5_lam_im, v_s5_log_step, v_s5_b_re, v_s5_b_im, v_s5_c_re, v_s5_c_im, v_s5_d, v_s5_glu_w, v_s5_glu_b, v_rw_shift_mu, v_rw_w0, v_rw_w2, v_rw_a0, v_rw_a2, v_rw_g2, v_rw_k_k, v_rw_k_a, v_rw_r_k, v_rw_ln_w, v_rw_ln_b, v_w_out, v_norm_ffn, v_ffn_w1, v_ffn_w3, v_ffn_w2, v_norm_ple, v_ple_gate_w, v_ple_up_w, v_final_norm):
    return _kernel_impl(dict(zip(ARG_NAMES, (x, p, norm_mix, w_in, s5_lam_re, s5_lam_im, s5_log_step, s5_b_re, s5_b_im, s5_c_re, s5_c_im, s5_d, s5_glu_w, s5_glu_b, rw_shift_mu, rw_w0, rw_w2, rw_a0, rw_a2, rw_g2, rw_k_k, rw_k_a, rw_r_k, rw_ln_w, rw_ln_b, w_out, norm_ffn, ffn_w1, ffn_w3, ffn_w2, norm_ple, ple_gate_w, ple_up_w, final_norm, loss_target, m_norm_mix, m_w_in, m_s5_lam_re, m_s5_lam_im, m_s5_log_step, m_s5_b_re, m_s5_b_im, m_s5_c_re, m_s5_c_im, m_s5_d, m_s5_glu_w, m_s5_glu_b, m_rw_shift_mu, m_rw_w0, m_rw_w2, m_rw_a0, m_rw_a2, m_rw_g2, m_rw_k_k, m_rw_k_a, m_rw_r_k, m_rw_ln_w, m_rw_ln_b, m_w_out, m_norm_ffn, m_ffn_w1, m_ffn_w3, m_ffn_w2, m_norm_ple, m_ple_gate_w, m_ple_up_w, m_final_norm, v_norm_mix, v_w_in, v_s5_lam_re, v_s5_lam_im, v_s5_log_step, v_s5_b_re, v_s5_b_im, v_s5_c_re, v_s5_c_im, v_s5_d, v_s5_glu_w, v_s5_glu_b, v_rw_shift_mu, v_rw_w0, v_rw_w2, v_rw_a0, v_rw_a2, v_rw_g2, v_rw_k_k, v_rw_k_a, v_rw_r_k, v_rw_ln_w, v_rw_ln_b, v_w_out, v_norm_ffn, v_ffn_w1, v_ffn_w3, v_ffn_w2, v_norm_ple, v_ple_gate_w, v_ple_up_w, v_final_norm))))
```

```python
import functools

import jax
import jax.numpy as jnp
from jax import lax
from jax.experimental import pallas as pl
from jax.experimental.pallas import tpu as pltpu

F32 = jnp.float32
BF16 = jnp.bfloat16

D_MODEL = 1024
S5_WIDTH = 512
RW_WIDTH = 512
S5_GROUP = 16
S5_GROUPS = 32
S5_STATE = 64
S5_LANES = S5_GROUPS * S5_STATE
HEAD = 64
SHIFT_COLS = 1792
IN_COLS = 2304
FFN_HIDDEN = 2816
PLE_DIM = 256
RMS_EPS = 1e-6
GN_EPS = 64e-5
L2_EPS = 1e-12
CHUNK = 64
N_DEV = 8

ADAM_LR = 0.001
ADAM_B1 = 0.9
ADAM_B2 = 0.999
ADAM_EPS = 1e-08
ADAM_WD = 0.01
ADAM_STEP = 10

VMEM_LIMIT = 56 * 1024 * 1024


def _pcall(body, **kw):
    return pl.pallas_call(body, **kw)


def _cparams(n_grid):
    return pltpu.CompilerParams(dimension_semantics=("arbitrary",) * n_grid, vmem_limit_bytes=VMEM_LIMIT)


def _dot(a, b):
    return jnp.dot(a, b, preferred_element_type=F32)


def _dot_nt(a, b):
    return lax.dot_general(a, b, (((1,), (1,)), ((), ())), preferred_element_type=F32)


def _dot_tn(a, b):
    return lax.dot_general(a, b, (((0,), (0,)), ((), ())), preferred_element_type=F32)


def _mmc(w, diff=True, tr=False):
    fw, bw = (_dot_nt, _dot) if tr else (_dot, _dot_nt)
    if not diff:
        return lambda x: fw(x.astype(BF16), w)

    @jax.custom_vjp
    def f(x):
        return fw(x.astype(BF16), w)

    def fwd(x):
        return fw(x.astype(BF16), w), None

    def bwd(_, dy):
        return (bw(dy.astype(BF16), w),)

    f.defvjp(fwd, bwd)
    return f


def _split_dot(x, m, n_split):
    acc = None
    rem = x
    for s in range(n_split):
        part = rem.astype(BF16)
        t = _dot(part, m)
        acc = t if acc is None else acc + t
        if s + 1 < n_split:
            rem = rem - part.astype(F32)
    return acc


def _segsum(m, diff=True):
    if not diff:
        return lambda x: _split_dot(x, m, 2)

    @jax.custom_vjp
    def f(x):
        return _split_dot(x, m, 2)

    def fwd(x):
        return _split_dot(x, m, 2), None

    def bwd(_, dy):
        return (_split_dot(dy, m, 2),)

    f.defvjp(fwd, bwd)
    return f


def _head_indicator(n):
    r = lax.broadcasted_iota(jnp.int32, (n, n), 0) // HEAD
    c = lax.broadcasted_iota(jnp.int32, (n, n), 1) // HEAD
    return (r == c).astype(BF16)


def _rms(x, g):
    return x * lax.rsqrt(jnp.mean(x * x, axis=-1, keepdims=True) + RMS_EPS) * g


def _softplus(x):
    return jnp.maximum(x, 0.0) + jnp.log(1.0 + jnp.exp(-jnp.abs(x)))


def _sigmoid(x):
    return 1.0 / (1.0 + jnp.exp(-x))


def _gelu(x):
    return 0.5 * x * (1.0 + jnp.tanh(0.7978845608028654 * (x + 0.044715 * (x * x * x))))


def _tok_call(name, fn, L, TB, tok_in, const_in, tok_out, acc_out=(), deps=()):
    nb = L // TB
    g8 = TB // 8
    in_specs, args = [], []
    for spec in tok_in:
        if len(spec) == 1:
            arr = spec[0]
            in_specs.append(pl.BlockSpec((arr.shape[0], TB, HEAD), lambda i: (0, i, 0)))
            args.append(arr)
            continue
        arr, width, cb = spec[:3]
        mode = spec[3] if len(spec) > 3 else None
        if mode is None:
            in_specs.append(pl.BlockSpec((TB, width), lambda i, cb=cb: (i, cb)))
        elif mode == "prev":
            in_specs.append(pl.BlockSpec((8, width), lambda i, cb=cb: (jnp.maximum(i * g8 - 1, 0), cb)))
        else:
            in_specs.append(pl.BlockSpec((8, width), lambda i, cb=cb: (jnp.minimum((i + 1) * g8, L // 8 - 1), cb)))
        args.append(arr)
    for c in const_in:
        in_specs.append(pl.BlockSpec(c.shape, lambda i, nd=c.ndim: (0,) * nd, pipeline_mode=pl.Buffered(1)))
        args.append(c)
    for d in deps:
        in_specs.append(pl.BlockSpec(d.shape, lambda i, nd=d.ndim: (0,) * nd))
        args.append(d)
    out_shape, out_specs = [], []
    for width, dt in tok_out:
        if width == "heads":
            out_shape.append(jax.ShapeDtypeStruct((N_HEAD, L, HEAD), dt))
            out_specs.append(pl.BlockSpec((N_HEAD, TB, HEAD), lambda i: (0, i, 0)))
            continue
        out_shape.append(jax.ShapeDtypeStruct((L, width), dt))
        out_specs.append(pl.BlockSpec((TB, width), lambda i: (i, 0)))
    for shp in acc_out:
        out_shape.append(jax.ShapeDtypeStruct(shp, F32))
        out_specs.append(pl.BlockSpec(shp, lambda i, nd=len(shp): (0,) * nd))
    n_tok, n_const, n_to = len(tok_in), len(const_in), len(tok_out)

    def body(*refs):
        i = pl.program_id(0)
        tv = [r[...] if len(r.shape) == 2 else jnp.concatenate([r[h] for h in range(r.shape[0])], axis=1)
              for r in refs[:n_tok]]
        cv = [r[...] for r in refs[n_tok:n_tok + n_const]]
        orefs = refs[n_tok + n_const + len(deps):]
        outs = fn(i, tv, cv)
        for r, v in zip(orefs[:n_to], outs[:n_to]):
            if len(r.shape) == 3:
                for h in range(r.shape[0]):
                    r[h] = v[:, h * HEAD:(h + 1) * HEAD].astype(r.dtype)
            else:
                r[...] = v.astype(r.dtype)
        for r, v in zip(orefs[n_to:], outs[n_to:]):
            @pl.when(i == 0)
            def _(r=r):
                r[...] = jnp.zeros(r.shape, r.dtype)

            r[...] += v

    res = _pcall(body, name=name, grid=(nb,), in_specs=in_specs, out_specs=out_specs, out_shape=out_shape,
                 compiler_params=_cparams(1))(*args)
    return res


def _pick_block(n, cap):
    best = None
    for b in range(128, min(n, cap) + 1, 128):
        if n % b == 0:
            best = b
    return best if best is not None else n


def _mm_tn(name, a, b):
    T, M = a.shape
    N = b.shape[1]
    bm, bn, bt = _pick_block(M, 1024), _pick_block(N, 1536), _pick_block(T, 512)

    def body(a_ref, b_ref, o_ref):
        t = pl.program_id(2)

        @pl.when(t == 0)
        def _():
            o_ref[...] = jnp.zeros(o_ref.shape, F32)

        o_ref[...] += _dot_tn(a_ref[...].astype(BF16), b_ref[...].astype(BF16))

    return _pcall(body, name=name, grid=(M // bm, N // bn, T // bt),
                  in_specs=[pl.BlockSpec((bt, bm), lambda m, n, t: (t, m)), pl.BlockSpec((bt, bn), lambda m, n, t: (t, n))],
                  out_specs=pl.BlockSpec((bm, bn), lambda m, n, t: (m, n)),
                  out_shape=jax.ShapeDtypeStruct((M, N), F32), compiler_params=_cparams(3))(a, b)


def _s5_param_fn(lam_re, lam_im, log_step, bt_re, bt_im):
    dt = jnp.exp(log_step)
    e = jnp.exp(lam_re * dt)
    lb_re = e * jnp.cos(lam_im * dt)
    lb_im = e * jnp.sin(lam_im * dt)
    den = lam_re * lam_re + lam_im * lam_im
    nr, ni = lb_re - 1.0, lb_im
    co_re = (nr * lam_re + ni * lam_im) / den
    co_im = (ni * lam_re - nr * lam_im) / den
    cr, ci = co_re[:, None, :], co_im[:, None, :]
    return lb_re, lb_im, cr * bt_re - ci * bt_im, cr * bt_im + ci * bt_re


def _s5_param_fwd(lam_re, lam_im, log_step, bt_re, bt_im):
    def body(a, b, c, d, e, o1, o2, o3, o4):
        r = _s5_param_fn(a[...], b[...], c[...], d[...], e[...])
        o1[...], o2[...], o3[...], o4[...] = r

    sh = jax.ShapeDtypeStruct
    return _pcall(body, name="s5_param_fwd",
                  out_shape=[sh(lam_re.shape, F32), sh(lam_re.shape, F32), sh(bt_re.shape, F32), sh(bt_re.shape, F32)])(
        lam_re, lam_im, log_step, bt_re, bt_im)


def _s5_param_bwd(lam_re, lam_im, log_step, bt_re, bt_im, d_lb_re, d_lb_im, d_bb_re, d_bb_im):
    def body(a, b, c, d, e, g1, g2, g3, g4, o1, o2, o3, o4, o5):
        _, vjp = jax.vjp(_s5_param_fn, a[...], b[...], c[...], d[...], e[...])
        r = vjp((g1[...], g2[...], g3[...], g4[...]))
        o1[...], o2[...], o3[...], o4[...], o5[...] = r

    sh = jax.ShapeDtypeStruct
    return _pcall(body, name="s5_param_bwd",
                  out_shape=[sh(lam_re.shape, F32), sh(lam_re.shape, F32), sh(log_step.shape, F32),
                             sh(bt_re.shape, F32), sh(bt_re.shape, F32)])(
        lam_re, lam_im, log_step, bt_re, bt_im, d_lb_re, d_lb_im, d_bb_re, d_bb_im)


def _cmul(ar, ai, br, bi):
    return ar * br - ai * bi, ar * bi + ai * br


def _scan_consts(lr, li, reverse):
    n = lr.shape[1]
    sub = lax.broadcasted_iota(jnp.int32, (8, n), 0)
    pows = [(lr, li)]
    for _ in range(7):
        pows.append(_cmul(pows[-1][0], pows[-1][1], lr, li))
    steps = []
    for s in (1, 2, 4):
        m = (sub < 8 - s) if reverse else (sub >= s)
        pr, pi = pows[s - 1]
        steps.append((s, jnp.where(m, jnp.broadcast_to(pr, (8, n)), 0.0), jnp.where(m, jnp.broadcast_to(pi, (8, n)), 0.0)))
    wr = jnp.zeros((8, n), F32)
    wi = jnp.zeros((8, n), F32)
    for r in range(8):
        e = (8 - r) if reverse else (r + 1)
        wr = jnp.where(sub == r, jnp.broadcast_to(pows[e - 1][0], (8, n)), wr)
        wi = jnp.where(sub == r, jnp.broadcast_to(pows[e - 1][1], (8, n)), wi)
    return steps, wr, wi


def _scan_rows(sre, sim, carry, lr, li, rows, reverse):
    steps, wr, wi = _scan_consts(lr, li, reverse)
    ng = rows // 8

    def step(gi, _):
        g = (ng - 1 - gi) if reverse else gi
        base = pl.multiple_of(g * 8, 8)
        xr = sre[pl.ds(base, 8), :]
        xi = sim[pl.ds(base, 8), :]
        for s, pr, pi in steps:
            sh = (8 - s) if reverse else s
            yr = pltpu.roll(xr, sh, 0)
            yi = pltpu.roll(xi, sh, 0)
            xr, xi = xr + pr * yr - pi * yi, xi + pr * yi + pi * yr
        cr = carry[0:1, :]
        ci = carry[1:2, :]
        xr, xi = xr + wr * cr - wi * ci, xi + wr * ci + wi * cr
        sre[pl.ds(base, 8), :] = xr
        sim[pl.ds(base, 8), :] = xi
        edge = 0 if reverse else 7
        carry[0:1, :] = xr[edge:edge + 1, :]
        carry[1:2, :] = xi[edge:edge + 1, :]
        return 0

    lax.fori_loop(0, ng, step, 0)


S5_Q = 4
S5_QL = S5_WIDTH // S5_Q
S5_QS = S5_LANES // S5_Q


def _s5_scan_fwd(proj, bq_re, bq_im, cq_re, cq_im, lbar, dskip, L, TB):
    nb = L // TB

    def body(u_ref, bre, bim, cre, cim, lb_ref, d_ref, y_ref, ck_ref, sre, sim, carry):
        i = pl.program_id(0)

        @pl.when(i == 0)
        def _():
            carry[...] = jnp.zeros(carry.shape, F32)

        ck_ref[0] = carry[...]
        u = u_ref[...]
        ub = u.astype(BF16)
        for q in range(S5_Q):
            uq = ub[:, q * S5_QL:(q + 1) * S5_QL]
            sre[:, q * S5_QS:(q + 1) * S5_QS] = _dot(uq, bre[q])
            sim[:, q * S5_QS:(q + 1) * S5_QS] = _dot(uq, bim[q])
        _scan_rows(sre, sim, carry, lb_ref[0:1, :], lb_ref[1:2, :], TB, False)
        for q in range(S5_Q):
            sl = slice(q * S5_QL, (q + 1) * S5_QL)
            ss = slice(q * S5_QS, (q + 1) * S5_QS)
            y_ref[:, sl] = (_dot(sre[:, ss].astype(BF16), cre[q]) - _dot(sim[:, ss].astype(BF16), cim[q])
                            + u[:, sl] * d_ref[:, sl])

    full = lambda a: pl.BlockSpec(a.shape, lambda i, nd=a.ndim: (0,) * nd)
    return _pcall(
        body, name="s5_scan_fwd", grid=(nb,),
        in_specs=[pl.BlockSpec((TB, S5_WIDTH), lambda i: (i, 0)), full(bq_re), full(bq_im), full(cq_re), full(cq_im),
                  full(lbar), full(dskip)],
        out_specs=[pl.BlockSpec((TB, S5_WIDTH), lambda i: (i, 0)), pl.BlockSpec((1, 8, S5_LANES), lambda i: (i, 0, 0))],
        out_shape=[jax.ShapeDtypeStruct((L, S5_WIDTH), F32), jax.ShapeDtypeStruct((nb, 8, S5_LANES), F32)],
        scratch_shapes=[pltpu.VMEM((TB, S5_LANES), F32), pltpu.VMEM((TB, S5_LANES), F32), pltpu.VMEM((8, S5_LANES), F32)],
        compiler_params=_cparams(1))(proj, bq_re, bq_im, cq_re, cq_im, lbar, dskip)


def _s5_scan_bwd(proj, dy, ck, bq_re, bq_im, cq_re, cq_im, lbar, dskip, L, TB):
    nb = L // TB
    ng = TB // 8

    def body(u_ref, dy_ref, ck_ref, bre, bim, cre, cim, lb_ref, d_ref,
             du_ref, dbre, dbim, dcre, dcim, dlb_ref, dd_ref, sre, sim, gre, gim, carry, gcarry):
        i = pl.program_id(0)

        @pl.when(i == 0)
        def _():
            gcarry[...] = jnp.zeros(gcarry.shape, F32)
            dbre[...] = jnp.zeros(dbre.shape, F32)
            dbim[...] = jnp.zeros(dbim.shape, F32)
            dcre[...] = jnp.zeros(dcre.shape, F32)
            dcim[...] = jnp.zeros(dcim.shape, F32)
            dlb_ref[...] = jnp.zeros(dlb_ref.shape, F32)
            dd_ref[...] = jnp.zeros(dd_ref.shape, F32)

        lr = lb_ref[0:1, :]
        li = lb_ref[1:2, :]
        u = u_ref[...]
        ub = u.astype(BF16)
        dy_v = dy_ref[...]
        dyb = dy_v.astype(BF16)
        carry[...] = ck_ref[0]
        for q in range(S5_Q):
            uq = ub[:, q * S5_QL:(q + 1) * S5_QL]
            dq = dyb[:, q * S5_QL:(q + 1) * S5_QL]
            ss = slice(q * S5_QS, (q + 1) * S5_QS)
            sre[:, ss] = _dot(uq, bre[q])
            sim[:, ss] = _dot(uq, bim[q])
            gre[:, ss] = _dot_nt(dq, cre[q])
            gim[:, ss] = -_dot_nt(dq, cim[q])
        _scan_rows(sre, sim, carry, lr, li, TB, False)
        _scan_rows(gre, gim, gcarry, lr, -li, TB, True)

        sub = lax.broadcasted_iota(jnp.int32, (8, S5_LANES), 0)
        c0r = ck_ref[0, 0:1, :]
        c0i = ck_ref[0, 1:2, :]

        def acc_step(g, acc):
            ar, ai = acc
            base = pl.multiple_of(g * 8, 8)
            pbase = pl.multiple_of(jnp.maximum(g - 1, 0) * 8, 8)
            first = g == 0
            lastr = jnp.where(first, c0r, sre[pl.ds(pbase, 8), :][7:8, :])
            lasti = jnp.where(first, c0i, sim[pl.ds(pbase, 8), :][7:8, :])
            spr = jnp.where(sub == 0, jnp.broadcast_to(lastr, sub.shape), pltpu.roll(sre[pl.ds(base, 8), :], 1, 0))
            spi = jnp.where(sub == 0, jnp.broadcast_to(lasti, sub.shape), pltpu.roll(sim[pl.ds(base, 8), :], 1, 0))
            gr = gre[pl.ds(base, 8), :]
            gi_ = gim[pl.ds(base, 8), :]
            return ar + gr * spr + gi_ * spi, ai - gr * spi + gi_ * spr

        z8 = jnp.zeros((8, S5_LANES), F32)
        ar, ai = lax.fori_loop(0, ng, acc_step, (z8, z8))
        dlb_ref[0:1, :] += jnp.sum(ar, axis=0, keepdims=True)
        dlb_ref[1:2, :] += jnp.sum(ai, axis=0, keepdims=True)

        dd_ref[...] += jnp.sum(dy_v * u, axis=0, keepdims=True)
        for q in range(S5_Q):
            sl = slice(q * S5_QL, (q + 1) * S5_QL)
            ss = slice(q * S5_QS, (q + 1) * S5_QS)
            grq = gre[:, ss].astype(BF16)
            giq = gim[:, ss].astype(BF16)
            du_ref[:, sl] = _dot_nt(grq, bre[q]) + _dot_nt(giq, bim[q]) + dy_v[:, sl] * d_ref[:, sl]
            dbre[q] += _dot_tn(ub[:, sl], grq)
            dbim[q] += _dot_tn(ub[:, sl], giq)
            dcre[q] += _dot_tn(sre[:, ss].astype(BF16), dyb[:, sl])
            dcim[q] -= _dot_tn(sim[:, ss].astype(BF16), dyb[:, sl])

    full = lambda a: pl.BlockSpec(a.shape, lambda i, nd=a.ndim: (0,) * nd)
    rev = lambda i: (nb - 1 - i, 0)
    sh = jax.ShapeDtypeStruct
    outs = [sh((L, S5_WIDTH), F32), sh(bq_re.shape, F32), sh(bq_im.shape, F32), sh(cq_re.shape, F32), sh(cq_im.shape, F32),
            sh((8, S5_LANES), F32), sh((1, S5_WIDTH), F32)]
    fo = lambda s: pl.BlockSpec(s.shape, lambda i, nd=len(s.shape): (0,) * nd)
    return _pcall(
        body, name="s5_scan_bwd", grid=(nb,),
        in_specs=[pl.BlockSpec((TB, S5_WIDTH), rev), pl.BlockSpec((TB, S5_WIDTH), rev),
                  pl.BlockSpec((1, 8, S5_LANES), lambda i: (nb - 1 - i, 0, 0)),
                  full(bq_re), full(bq_im), full(cq_re), full(cq_im), full(lbar), full(dskip)],
        out_specs=[pl.BlockSpec((TB, S5_WIDTH), rev)] + [fo(s) for s in outs[1:]],
        out_shape=outs,
        scratch_shapes=[pltpu.VMEM((TB, S5_LANES), F32)] * 4 + [pltpu.VMEM((8, S5_LANES), F32)] * 2,
        compiler_params=_cparams(1))(proj, dy, ck, bq_re, bq_im, cq_re, cq_im, lbar, dskip)


N_HEAD = RW_WIDTH // HEAD
_NN = (((2,), (1,)), ((0,), (0,)))
_NT = (((2,), (2,)), ((0,), (0,)))
_TN = (((1,), (1,)), ((0,), (0,)))


def _hi_lo(x):
    h = x.astype(BF16)
    return h, (x - h.astype(F32)).astype(BF16)


def _mm_acc(a, b, dims, passes=3):
    dg = lambda p, q: lax.dot_general(p, q, dims, preferred_element_type=F32)
    if passes == 1:
        return dg(a.astype(BF16), b.astype(BF16))
    ah, al = _hi_lo(a)
    bh, bl = _hi_lo(b)
    return dg(ah, bh) + dg(ah, bl) + dg(al, bh)


def _cumsum_rows(x, transpose):
    h, n, _ = x.shape
    ti = lax.broadcasted_iota(jnp.int32, (h, n, n), 1)
    tj = lax.broadcasted_iota(jnp.int32, (h, n, n), 2)
    m = ((tj >= ti) if transpose else (tj <= ti)).astype(BF16)
    acc, rem = None, x
    for s in range(3):
        part = rem.astype(BF16)
        t = lax.dot_general(m, part, _NN, preferred_element_type=F32)
        acc = t if acc is None else acc + t
        if s < 2:
            rem = rem - part.astype(F32)
    return acc


def _slices(x, axis, sizes):
    out, off = [], 0
    for n in sizes:
        out.append(lax.slice_in_dim(x, off, off + n, axis=axis))
        off += n
    return tuple(out)


def _cat_op(axis, sizes, diff):
    plain = lambda *xs: jnp.concatenate(xs, axis=axis)
    if not diff:
        return plain
    f = jax.custom_vjp(plain)
    f.defvjp(lambda *xs: (plain(*xs), None), lambda _, d: _slices(d, axis, sizes))
    return f


def _split_op(axis, sizes, diff):
    plain = lambda x: _slices(x, axis, sizes)
    if not diff:
        return plain
    f = jax.custom_vjp(plain)
    f.defvjp(lambda x: (plain(x), None), lambda _, d: (jnp.concatenate(d, axis=axis),))
    return f


def _mm_ops(diff, passes):
    mm = lambda a, b, dims: _mm_acc(a, b, dims, passes)
    if not diff:
        return (lambda a, b: mm(a, b, _NN), lambda a, b: mm(a, b, _NT), lambda a, b: mm(a, b, _TN))

    @jax.custom_vjp
    def nn(a, b):
        return mm(a, b, _NN)

    nn.defvjp(lambda a, b: (mm(a, b, _NN), (a, b)), lambda r, d: (mm(d, r[1], _NT), mm(r[0], d, _TN)))

    @jax.custom_vjp
    def nt(a, b):
        return mm(a, b, _NT)

    nt.defvjp(lambda a, b: (mm(a, b, _NT), (a, b)), lambda r, d: (mm(d, r[1], _NN), mm(d, r[0], _TN)))

    @jax.custom_vjp
    def tn(a, b):
        return mm(a, b, _TN)

    tn.defvjp(lambda a, b: (mm(a, b, _TN), (a, b)), lambda r, d: (mm(r[1], d, _NT), mm(r[0], d, _NN)))
    return nn, nt, tn


def _cums_op(diff):
    if not diff:
        return lambda x: _cumsum_rows(x, False)

    @jax.custom_vjp
    def cums(x):
        return _cumsum_rows(x, False)

    cums.defvjp(lambda x: (_cumsum_rows(x, False), None), lambda _, d: (_cumsum_rows(d, True),))
    return cums


WKV_PASSES = (3, 3, 3, 3, 3)


def _wkv_chunk(s0, r, w, k, v, a, b, diff):
    p_pair, p_val, p_solve, p_out, p_state = WKV_PASSES
    cums = _cums_op(diff)
    _, nt_pair, _ = _mm_ops(diff, p_pair)
    nn_val, _, _ = _mm_ops(diff, p_val)
    nn_solve, _, _ = _mm_ops(diff, p_solve)
    nn_out, _, _ = _mm_ops(diff, p_out)
    _, _, tn_state = _mm_ops(diff, p_state)
    h, n, d = r.shape
    cat_rows2 = _cat_op(1, (n, n), diff)
    cat_rows3 = _cat_op(1, (n, n, d), diff)
    cat_lanes2 = _cat_op(2, (d, n), diff)
    split_rows2 = _split_op(1, (n, n), diff)
    split_lanes3 = _split_op(2, (n, n, d), diff)
    split_lanes2 = _split_op(2, (d, n), diff)
    ti = lax.broadcasted_iota(jnp.int32, (h, n, n), 1)
    tj = lax.broadcasted_iota(jnp.int32, (h, n, n), 2)
    incl, strict = tj <= ti, tj < ti
    logw = jnp.log(w)
    cum = cums(logw)
    g_in, g_ex, g_inv = jnp.exp(cum), jnp.exp(cum - logw), jnp.exp(-cum)
    ae, re, bi, ki = a * g_ex, r * g_in, b * g_inv, k * g_inv
    top, bot = split_rows2(nt_pair(cat_rows2(ae, re), cat_rows3(bi, ki, s0)))
    tab, tak, u = split_lanes3(top)
    qb, qk, y = split_lanes3(bot)
    tab, tak = jnp.where(strict, tab, 0.0), jnp.where(strict, tak, 0.0)
    qb, qk = jnp.where(incl, qb, 0.0), jnp.where(incl, qk, 0.0)
    tak_v, qk_v = split_rows2(nn_val(cat_rows2(tak, qk), v))
    u = u + tak_v
    npow = tab
    steps = max(1, (n - 1).bit_length())
    for i in range(steps - 1):
        du, npow = split_lanes2(nn_solve(npow, cat_lanes2(u, npow)))
        u = u + du
    u = u + nn_solve(npow, u)
    y = y + nn_out(qb, u) + qk_v
    g_end = jnp.exp(jnp.sum(logw, axis=1, keepdims=True))
    s1 = s0 * g_end + tn_state(cat_rows2(u, v), cat_rows2(bi * g_end, ki * g_end))
    return y, s1


def _wkv_fwd(r, w, k, v, a, b, L):
    nc = L // CHUNK

    def body(r_ref, w_ref, k_ref, v_ref, a_ref, b_ref, y_ref, ck_ref, s_ref):
        c = pl.program_id(0)

        @pl.when(c == 0)
        def _():
            s_ref[...] = jnp.zeros(s_ref.shape, F32)

        s0 = s_ref[...]
        ck_ref[0] = s0
        y, s1 = _wkv_chunk(s0, r_ref[...], w_ref[...], k_ref[...], v_ref[...], a_ref[...], b_ref[...], False)
        y_ref[...] = y
        s_ref[...] = s1

    blk = pl.BlockSpec((N_HEAD, CHUNK, HEAD), lambda c: (0, c, 0))
    return _pcall(
        body, name="wkv_fwd", grid=(nc,), in_specs=[blk] * 6,
        out_specs=[blk, pl.BlockSpec((1, N_HEAD, HEAD, HEAD), lambda c: (c, 0, 0, 0))],
        out_shape=[jax.ShapeDtypeStruct((N_HEAD, L, HEAD), F32), jax.ShapeDtypeStruct((nc, N_HEAD, HEAD, HEAD), F32)],
        scratch_shapes=[pltpu.VMEM((N_HEAD, HEAD, HEAD), F32)],
        compiler_params=_cparams(1))(r, w, k, v, a, b)


def _wkv_bwd(r, w, k, v, a, b, dy, ck, L, deps=()):
    nc = L // CHUNK

    def body(r_ref, w_ref, k_ref, v_ref, a_ref, b_ref, dy_ref, ck_ref, *rest):
        dr_ref, dw_ref, dk_ref, dv_ref, da_ref, db_ref, ds_ref = rest[len(deps):]
        c = pl.program_id(0)

        @pl.when(c == 0)
        def _():
            ds_ref[...] = jnp.zeros(ds_ref.shape, F32)

        _, vjp = jax.vjp(lambda *t: _wkv_chunk(*t, True), ck_ref[0], r_ref[...], w_ref[...], k_ref[...], v_ref[...],
                         a_ref[...], b_ref[...])
        g = vjp((dy_ref[...], ds_ref[...]))
        ds_ref[...] = g[0]
        for o_ref, val in zip((dr_ref, dw_ref, dk_ref, dv_ref, da_ref, db_ref), g[1:]):
            o_ref[...] = val

    blk = pl.BlockSpec((N_HEAD, CHUNK, HEAD), lambda c: (0, nc - 1 - c, 0))
    sh = jax.ShapeDtypeStruct((N_HEAD, L, HEAD), F32)
    return _pcall(
        body, name="wkv_bwd", grid=(nc,),
        in_specs=[blk] * 7 + [pl.BlockSpec((1, N_HEAD, HEAD, HEAD), lambda c: (nc - 1 - c, 0, 0, 0))]
        + [pl.BlockSpec(d.shape, lambda c, nd=d.ndim: (0,) * nd) for d in deps],
        out_specs=[blk] * 6, out_shape=[sh] * 6,
        scratch_shapes=[pltpu.VMEM((N_HEAD, HEAD, HEAD), F32)],
        compiler_params=_cparams(1))(r, w, k, v, a, b, dy, ck, *deps)


TB = 256


def _bf(x):
    return x.astype(BF16)


def _inproj_fwd(x, norm_mix, w_in, L, deps=()):
    def fn(i, tv, cv):
        xn = _rms(tv[0], cv[0])
        return _dot(_bf(xn), cv[1]), xn

    return _tok_call("inproj_fwd", fn, L, TB, [(x, D_MODEL, 0)], [norm_mix, w_in], [(IN_COLS, F32), (D_MODEL, BF16)],
                     deps=deps)


def _s5_post_fn(glu_w, wtop, diff=True):
    mg = _mmc(glu_w, diff)
    mt = _mmc(wtop, diff) if wtop is not None else None

    def f(y, glu_b, e):
        z = _gelu(y)
        out = z * _sigmoid(mg(z) + glu_b + e)
        res = mt(out) if mt is not None else out
        return res, (z, out)

    return f


def _s5_post_fwd(y, glu_w, glu_b, L):
    def fn(i, tv, cv):
        out, _ = _s5_post_fn(cv[0], None, False)(tv[0], cv[1], 0.0)
        return (out,)

    return _tok_call("s5_post_fwd", fn, L, TB, [(y, S5_WIDTH, 0)], [glu_w, glu_b], [(S5_WIDTH, F32)])[0]


def _s5_post_bwd(y, dh1, glu_w, glu_b, wtop, L):
    def fn(i, tv, cv):
        e0 = jnp.zeros((TB, S5_WIDTH), F32)
        _, vjp, (z, out) = jax.vjp(_s5_post_fn(cv[0], cv[2]), tv[0], cv[1], e0, has_aux=True)
        dy, db, de = vjp(tv[1])
        return dy, z, de, out, db

    return _tok_call("s5_post_bwd", fn, L, TB, [(y, S5_WIDTH, 0), (dh1, D_MODEL, 0)], [glu_w, glu_b, wtop],
                     [(S5_WIDTH, F32), (S5_WIDTH, BF16), (S5_WIDTH, BF16), (S5_WIDTH, BF16)], [(1, S5_WIDTH)])


RW_COLBLK = ((RW_WIDTH, 1), (RW_WIDTH, 2), (RW_WIDTH, 3), (128, 16), (128, 17))
RW_MU = ((0, 512), (512, 1024), (1024, 1536), (1536, 1664), (1664, 1792))


def _rw_pre_fn(w2pad, a2pad, g2, diff=True):
    m_w, m_a, m_g = _mmc(w2pad, diff), _mmc(a2pad, diff), _mmc(g2, diff)
    seg = _segsum(_head_indicator(RW_WIDTH), diff)

    def f(zr, zk, zv, zwa, zg, w0, a0, k_k, k_a, e_w, e_a):
        wl_t = jnp.tanh(zwa)
        wlin = w0 + m_w(wl_t) + e_w
        w = -_softplus(-wlin) - 0.5
        decay = jnp.exp(-jnp.exp(w))
        a = _sigmoid(a0 + m_a(zwa) + e_a)
        sg = _sigmoid(zg)
        g = m_g(sg)
        kk = zk * k_k
        kkn = kk / jnp.maximum(jnp.sqrt(seg(kk * kk)), L2_EPS)
        kf = zk * (1.0 + (a - 1.0) * k_a)
        return (zr, decay, kf, zv, -kkn, kkn * a, g), (wl_t, sg)

    return f


def _rw_shifted(i, tv, mu):
    sub = lax.broadcasted_iota(jnp.int32, (TB, 1), 0)
    zs, dif = [], []
    for n in range(5):
        z = tv[n]
        last = jnp.where(i == 0, 0.0, tv[5 + n][7:8, :])
        prev = jnp.where(sub == 0, last, pltpu.roll(z, 1, 0))
        m = mu[:, RW_MU[n][0]:RW_MU[n][1]]
        zs.append(z + (prev - z) * m)
        dif.append(prev - z)
    return zs, dif


def _rw_tok_in(proj):
    return [(proj, wd, cb) for wd, cb in RW_COLBLK] + [(proj, wd, cb, "prev") for wd, cb in RW_COLBLK]


def _rw_pre_fwd(proj, mu, w0, a0, k_k, k_a, w2pad, a2pad, g2, L):
    def fn(i, tv, cv):
        zs, _ = _rw_shifted(i, tv, cv[0])
        outs, _ = _rw_pre_fn(cv[5], cv[6], cv[7], False)(*zs, cv[1], cv[2], cv[3], cv[4], 0.0, 0.0)
        return outs

    return _tok_call("rw_pre_fwd", fn, L, TB, _rw_tok_in(proj), [mu, w0, a0, k_k, k_a, w2pad, a2pad, g2],
                     [("heads", F32)] * 6 + [(RW_WIDTH, F32)])


def _rw_pre_bwd(proj, cots, mu, w0, a0, k_k, k_a, w2pad, a2pad, g2, L):
    def fn(i, tv, cv):
        zs, dif = _rw_shifted(i, tv[:10], cv[0])
        dr1, dr2, dw, dk1, dk2, dv1, dv2, da, db, dg = tv[10:]
        e0 = jnp.zeros((TB, RW_WIDTH), F32)
        _, vjp, (wl_t, sg) = jax.vjp(_rw_pre_fn(cv[5], cv[6], cv[7]), *zs, cv[1], cv[2], cv[3], cv[4], e0, e0, has_aux=True)
        g = vjp((dr1 + dr2, dw, dk1 + dk2, dv1 + dv2, da, db, dg))
        dzs = jnp.concatenate(g[:5], axis=1)
        dmu = jnp.concatenate([jnp.sum(g[n] * dif[n], axis=0, keepdims=True) for n in range(5)], axis=1)
        return dzs, wl_t, zs[3], sg, g[9], g[10], dmu, g[5], g[6], g[7], g[8]

    tok_in = _rw_tok_in(proj) + [((c,) if c.ndim == 3 else (c, RW_WIDTH, 0)) for c in cots]
    return _tok_call("rw_pre_bwd", fn, L, TB, tok_in, [mu, w0, a0, k_k, k_a, w2pad, a2pad, g2],
                     [(SHIFT_COLS, F32), (128, BF16), (128, BF16), (128, BF16), (RW_WIDTH, BF16), (RW_WIDTH, BF16)],
                     [(1, SHIFT_COLS)] + [(1, RW_WIDTH)] * 4)


def _rw_post_fn(wbot, diff=True):
    seg = _segsum(_head_indicator(RW_WIDTH), diff)
    mb = _mmc(wbot, diff) if wbot is not None else None

    def f(y, r, kf, v, g, ln_w, ln_b, r_k):
        mean = seg(y) * (1.0 / HEAD)
        yc = y - mean
        var = seg(yc * yc) * (1.0 / HEAD)
        yn = yc * lax.rsqrt(var + GN_EPS) * ln_w + ln_b
        bonus = seg(r * kf * r_k) * v
        out = (yn + bonus) * g
        res = mb(out) if mb is not None else out
        return res, out

    return f


def _rw_post_fwd(y, r, kf, v, g, ln_w, ln_b, r_k, L):
    def fn(i, tv, cv):
        out, _ = _rw_post_fn(None, False)(*tv, *cv)
        return (out,)

    return _tok_call("rw_post_fwd", fn, L, TB, [(t,) for t in (y, r, kf, v)] + [(g, RW_WIDTH, 0)], [ln_w, ln_b, r_k],
                     [(RW_WIDTH, F32)])[0]


def _rw_post_bwd(y, r, kf, v, g, dh1, ln_w, ln_b, r_k, wbot, L):
    def fn(i, tv, cv):
        _, vjp, out = jax.vjp(_rw_post_fn(cv[3]), *tv[:5], cv[0], cv[1], cv[2], has_aux=True)
        gr = vjp(tv[5])
        return gr[0], gr[1], gr[2], gr[3], gr[4], out, gr[5], gr[6], gr[7]

    return _tok_call("rw_post_bwd", fn, L, TB, [(t,) for t in (y, r, kf, v)] + [(g, RW_WIDTH, 0), (dh1, D_MODEL, 0)],
                     [ln_w, ln_b, r_k, wbot], [("heads", F32)] + [(RW_WIDTH, F32)] * 4 + [(RW_WIDTH, BF16)], [(1, RW_WIDTH)] * 3)


def _ffn_fn(w1, w3, w2, diff=True):
    m1, m3, m2 = _mmc(w1, diff), _mmc(w3, diff), _mmc(w2, diff)

    def f(h1, norm_ffn, e1, e3):
        hn = _rms(h1, norm_ffn)
        a1 = m1(hn) + e1
        a3 = m3(hn) + e3
        hm = a1 * _sigmoid(a1) * a3
        return h1 + m2(hm), (hn, hm)

    return f


TB_FFN = 256


def _mixffn_fwd(x, s5_out, rw_out, wtop, wbot, norm_ffn, w1, w3, w2, L):
    def fn(i, tv, cv):
        h1 = tv[0] + _dot(_bf(tv[1]), cv[0]) + _dot(_bf(tv[2]), cv[1])
        h2, _ = _ffn_fn(cv[3], cv[4], cv[5], False)(h1, cv[2], 0.0, 0.0)
        return h1, h2

    return _tok_call("mixffn_fwd", fn, L, TB_FFN, [(x, D_MODEL, 0), (s5_out, S5_WIDTH, 0), (rw_out, RW_WIDTH, 0)],
                     [wtop, wbot, norm_ffn, w1, w3, w2], [(D_MODEL, F32), (D_MODEL, F32)])


def _ffn_bwd(h1, dh2, norm_ffn, w1, w3, w2, L):
    def fn(i, tv, cv):
        e0 = jnp.zeros((TB_FFN, FFN_HIDDEN), F32)
        _, vjp, (hn, hm) = jax.vjp(_ffn_fn(cv[1], cv[2], cv[3]), tv[0], cv[0], e0, e0, has_aux=True)
        dh1, dn, d1, d3 = vjp(tv[1])
        return dh1, d1, d3, hm, hn, dn

    return _tok_call("ffn_bwd", fn, L, TB_FFN, [(h1, D_MODEL, 0), (dh2, D_MODEL, 0)], [norm_ffn, w1, w3, w2],
                     [(D_MODEL, F32), (FFN_HIDDEN, BF16), (FFN_HIDDEN, BF16), (FFN_HIDDEN, BF16), (D_MODEL, BF16)],
                     [(1, D_MODEL)])


def _ple_loss_fb(h2, p, target, norm_ple, final_norm, wg, wu, L):
    def fn(i, tv, cv):
        mgate, mup = _mmc(cv[2]), _mmc(cv[3], False)

        def f(h2_, norm_ple_, final_norm_, eg, eu):
            hn = _rms(h2_, norm_ple_)
            gate = _sigmoid(mgate(hn) + eg)
            h3 = h2_ + gate * (mup(tv[1]) + eu)
            out = _rms(h3, final_norm_)
            d = out - tv[2]
            return 0.5 * jnp.sum(jnp.mean(d * d, axis=-1, keepdims=True)), hn

        e0 = jnp.zeros((TB, D_MODEL), F32)
        loss, vjp, hn = jax.vjp(f, tv[0], cv[0], cv[1], e0, e0, has_aux=True)
        dh2, dnp, dfn, deg, deu = vjp(jnp.ones((), F32))
        return dh2, deg, deu, hn, jnp.full((8, 128), loss, F32), dnp, dfn

    return _tok_call("ple_loss_fb", fn, L, TB, [(h2, D_MODEL, 0), (p, PLE_DIM, 0), (target, D_MODEL, 0)],
                     [norm_ple, final_norm, wg, wu], [(D_MODEL, F32), (D_MODEL, BF16), (D_MODEL, BF16), (D_MODEL, BF16)],
                     [(8, 128), (1, D_MODEL), (1, D_MODEL)])


def _inproj_bwd(x, dh1, du, dzs, norm_mix, mu, w_u, w_z, L):
    nb = L // TB

    def fn(i, tv, cv):
        sub = lax.broadcasted_iota(jnp.int32, (TB, 1), 0)
        m = cv[1]
        b = tv[3] * m
        nxt = jnp.where(i == nb - 1, 0.0, tv[4][0:1, :] * m)
        dz = tv[3] * (1.0 - m) + jnp.where(sub == TB - 1, nxt, pltpu.roll(b, TB - 1, 0))
        dub, dzb = _bf(tv[2]), _bf(dz)
        dxn = _dot_nt(dub, cv[2]) + _dot_nt(dzb, cv[3])
        _, vjp = jax.vjp(_rms, tv[0], cv[0])
        dx, dn = vjp(dxn)
        return tv[1] + dx, jnp.concatenate([dub, dzb], axis=1), dn

    return _tok_call("inproj_bwd", fn, L, TB,
                     [(x, D_MODEL, 0), (dh1, D_MODEL, 0), (du, S5_WIDTH, 0), (dzs, SHIFT_COLS, 0), (dzs, SHIFT_COLS, 0, "next")],
                     [norm_mix, mu, w_u, w_z], [(D_MODEL, F32), (IN_COLS, BF16)], [(1, D_MODEL)])


def _eye8(dt):
    return jnp.eye(8, dtype=dt)


def _quarter_b(bb):
    return jnp.einsum("hg,qgcp->qhcgp", _eye8(bb.dtype), bb.reshape(S5_Q, 8, S5_GROUP, S5_STATE)).reshape(S5_Q, S5_QL, S5_QS)


def _unquarter_b(d):
    return jnp.einsum("qhcgp,hg->qgcp", d.reshape(S5_Q, 8, S5_GROUP, 8, S5_STATE), _eye8(d.dtype)).reshape(
        S5_GROUPS, S5_GROUP, S5_STATE)


def _quarter_c(c):
    return jnp.einsum("gh,qgcp->qgphc", _eye8(c.dtype), c.reshape(S5_Q, 8, S5_GROUP, S5_STATE)).reshape(S5_Q, S5_QS, S5_QL)


def _unquarter_c(d):
    return jnp.einsum("qgphc,gh->qgcp", d.reshape(S5_Q, 8, S5_STATE, 8, S5_GROUP), _eye8(d.dtype)).reshape(
        S5_GROUPS, S5_GROUP, S5_STATE)


def _local_step(x, p, target, W, late_weights=None, grads_ready=None, first_dep=None):
    L = x.shape[0]
    r2 = lambda v: v.reshape(1, -1)
    w_in = W["w_in"]
    w2pad = jnp.pad(W["rw_w2"], ((0, 64), (0, 0)))
    a2pad = jnp.pad(W["rw_a2"], ((64, 0), (0, 0)))
    mu = r2(W["rw_shift_mu"])
    rw_vec = [r2(W[n]) for n in ("rw_w0", "rw_a0", "rw_k_k", "rw_k_a")]
    ln_w, ln_b, r_k = r2(W["rw_ln_w"]), r2(W["rw_ln_b"]), r2(W["rw_r_k"])

    lam_re, lam_im = W["s5_lam_re"], W["s5_lam_im"]
    log_step = W["s5_log_step"].reshape(S5_GROUPS, 1)
    bt_re, bt_im = W["s5_b_re"].transpose(0, 2, 1), W["s5_b_im"].transpose(0, 2, 1)
    lb_re, lb_im, bb_re, bb_im = _s5_param_fwd(lam_re, lam_im, log_step, bt_re, bt_im)
    bq_re, bq_im = _quarter_b(bb_re).astype(BF16), _quarter_b(bb_im).astype(BF16)
    cq_re, cq_im = _quarter_c(W["s5_c_re"]).astype(BF16), _quarter_c(W["s5_c_im"]).astype(BF16)
    lbar = jnp.concatenate([lb_re.reshape(1, -1), lb_im.reshape(1, -1), jnp.zeros((6, S5_LANES), F32)], axis=0)
    dskip = r2(W["s5_d"])
    glu_b = r2(W["s5_glu_b"])
    norm_mix, norm_ffn, norm_ple, final_norm = (r2(W[n]) for n in ("norm_mix", "norm_ffn", "norm_ple", "final_norm"))

    proj, xn = _inproj_fwd(x, norm_mix, w_in, L, () if first_dep is None else (first_dep,))
    y_s5, ck5 = _s5_scan_fwd(proj, bq_re, bq_im, cq_re, cq_im, lbar, dskip, L, TB)
    s5_out = _s5_post_fwd(y_s5, W["s5_glu_w"], glu_b, L)
    r, wd, kf, v, a_s, b_s, g = _rw_pre_fwd(proj, mu, *rw_vec, w2pad, a2pad, W["rw_g2"], L)
    scan_in = (r, wd, kf, v, a_s, b_s)
    y_wkv, ckw = _wkv_fwd(*scan_in, L)
    rw_out = _rw_post_fwd(y_wkv, r, kf, v, g, ln_w, ln_b, r_k, L)
    if late_weights is not None:
        W = dict(W, **late_weights(rw_out))
    wtop, wbot = W["w_out"][:S5_WIDTH], W["w_out"][S5_WIDTH:]
    h1, h2 = _mixffn_fwd(x, s5_out, rw_out, wtop, wbot, norm_ffn, W["ffn_w1"], W["ffn_w3"], W["ffn_w2"], L)

    G = {}
    dh2, deg, deu, hn_ple, loss_acc, G["norm_ple"], G["final_norm"] = _ple_loss_fb(
        h2, p, target, norm_ple, final_norm, W["ple_gate_w"], W["ple_up_w"], L)
    dh1, da1, da3, hm, hn_ffn, G["norm_ffn"] = _ffn_bwd(h1, dh2, norm_ffn, W["ffn_w1"], W["ffn_w3"], W["ffn_w2"], L)
    dy_s5, z_bf, dgp, s5o_bf, G["s5_glu_b"] = _s5_post_bwd(y_s5, dh1, W["s5_glu_w"], glu_b, wtop, L)
    dy_wkv, dr2, dk2, dv2, dg, rwo_bf, G["rw_ln_w"], G["rw_ln_b"], G["rw_r_k"] = _rw_post_bwd(
        y_wkv, r, kf, v, g, dh1, ln_w, ln_b, r_k, wbot, L)
    G["ffn_w1"] = _mm_tn("dw_ffn_w1", hn_ffn, da1)
    G["ffn_w3"] = _mm_tn("dw_ffn_w3", hn_ffn, da3)
    G["ffn_w2"] = _mm_tn("dw_ffn_w2", hm, dh2)
    G["ple_gate_w"] = _mm_tn("dw_ple_gate", hn_ple, deg)
    G["w_out"] = jnp.concatenate([_mm_tn("dw_out_top", s5o_bf, dh1), _mm_tn("dw_out_bot", rwo_bf, dh1)], axis=0)
    dep = grads_ready(G) if grads_ready is not None else None
    G["ple_up_w"] = _mm_tn("dw_ple_up", p, deu)
    G["s5_glu_w"] = _mm_tn("dw_s5_glu", z_bf, dgp)
    dr1, dwd, dk1, dv1, da_s, db_s = _wkv_bwd(*scan_in, dy_wkv, ckw, L, () if dep is None else (dep,))
    (dzs, wl_t, zwa, sg, dwlin, dalin, G["rw_shift_mu"], G["rw_w0"], G["rw_a0"], G["rw_k_k"], G["rw_k_a"]) = _rw_pre_bwd(
        proj, (dr1, dr2, dwd, dk1, dk2, dv1, dv2, da_s, db_s, dg), mu, *rw_vec, w2pad, a2pad, W["rw_g2"], L)
    G["rw_w2"] = _mm_tn("dw_rw_w2", wl_t, dwlin)[:64]
    G["rw_a2"] = _mm_tn("dw_rw_a2", zwa, dalin)[64:]
    G["rw_g2"] = _mm_tn("dw_rw_g2", sg, dg)
    du, dbq_re, dbq_im, dcq_re, dcq_im, dlbar, G["s5_d"] = _s5_scan_bwd(
        proj, dy_s5, ck5, bq_re, bq_im, cq_re, cq_im, lbar, dskip, L, TB)
    G["s5_c_re"], G["s5_c_im"] = _unquarter_c(dcq_re), _unquarter_c(dcq_im)
    d_lam_re, d_lam_im, d_ls, d_bt_re, d_bt_im = _s5_param_bwd(
        lam_re, lam_im, log_step, bt_re, bt_im, dlbar[0].reshape(S5_GROUPS, S5_STATE), dlbar[1].reshape(S5_GROUPS, S5_STATE),
        _unquarter_b(dbq_re), _unquarter_b(dbq_im))
    G["s5_lam_re"], G["s5_lam_im"], G["s5_log_step"] = d_lam_re, d_lam_im, d_ls.reshape(S5_GROUPS)
    G["s5_b_re"], G["s5_b_im"] = d_bt_re.transpose(0, 2, 1), d_bt_im.transpose(0, 2, 1)
    dx, dproj, G["norm_mix"] = _inproj_bwd(x, dh1, du, dzs, norm_mix, mu, w_in[:, :S5_WIDTH], w_in[:, S5_WIDTH:], L)
    G["w_in"] = _mm_tn("dw_in", xn, dproj)
    return loss_acc[0, 0], dx, G


MESH_AXES = ("x", "y", "c")
_ANY = pl.BlockSpec(memory_space=pl.ANY)


def _all_gather(name, shards):
    nt = len(shards)

    def body(*refs):
        x_refs, out_refs = refs[:nt], refs[nt:2 * nt]
        send_sems, recv_sems, local_sems = refs[2 * nt:]
        x, y, c = lax.axis_index("x"), lax.axis_index("y"), lax.axis_index("c")
        me, sibling = (x, y, c), (x, y, 1 - c)
        chips = [(1 - x, y), (x, 1 - y), (1 - x, 1 - y)]

        def rows(t, px, py, pc):
            m_per = shards[t].shape[0]
            return out_refs[t].at[pl.ds((4 * px + 2 * py + pc) * m_per, m_per), :]

        def copy(t, k, block, to, src=None):
            return pltpu.make_async_remote_copy(
                src_ref=rows(t, *block) if src is None else src, dst_ref=rows(t, *block),
                send_sem=send_sems.at[7 * t + k], recv_sem=recv_sems.at[7 * t + k],
                device_id=to, device_id_type=pl.DeviceIdType.MESH)

        mine = [pltpu.make_async_copy(x_refs[t], rows(t, *me), local_sems.at[t]) for t in range(nt)]
        for cp in mine:
            cp.start()
        first = []
        for t in range(nt):
            first.append(copy(t, 0, me, sibling, src=x_refs[t]))
            first += [copy(t, 1 + j, me, (*chip, c), src=x_refs[t]) for j, chip in enumerate(chips)]
        for cp in first:
            cp.start()
        passed = []
        for t in range(nt):
            for j, chip in enumerate(chips):
                copy(t, 1 + j, (*chip, c), me).wait_recv()
                fwd = copy(t, 4 + j, (*chip, c), sibling)
                fwd.start()
                passed.append(fwd)
        for t in range(nt):
            copy(t, 0, sibling, me).wait_recv()
            for j, chip in enumerate(chips):
                copy(t, 4 + j, (*chip, 1 - c), me).wait_recv()
        for cp in first + passed:
            cp.wait_send()
        for cp in mine:
            cp.wait()

    return _pcall(body, name=name,
                  out_shape=[jax.ShapeDtypeStruct((N_DEV * a.shape[0], a.shape[1]), a.dtype) for a in shards],
                  in_specs=[_ANY] * nt, out_specs=[_ANY] * nt,
                  scratch_shapes=[pltpu.SemaphoreType.DMA((7 * nt,)), pltpu.SemaphoreType.DMA((7 * nt,)),
                                  pltpu.SemaphoreType.DMA((nt,))])(*shards)


_HBM = pl.BlockSpec(memory_space=pltpu.HBM)
_SEM = pl.BlockSpec(memory_space=pltpu.SEMAPHORE)
_EFFECT = pltpu.SideEffectType.DATAFLOW_SIDE_EFFECTING


def _peer_of(k):
    x, y, c = lax.axis_index("x"), lax.axis_index("y"), lax.axis_index("c")
    px, py, pc = x ^ ((k >> 2) & 1), y ^ ((k >> 1) & 1), c ^ (k & 1)
    return (px, py, pc), 4 * px + 2 * py + pc, 4 * x + 2 * y + c


def _direct_copy(t, k, src_refs, land_refs, send_sems, recv_sems, rows_of, gather):
    dev, peer, me = _peer_of(k)
    m = rows_of[t]
    src = src_refs[t] if gather else src_refs[t].at[pl.ds(peer * m, m), :]
    return pltpu.make_async_remote_copy(
        src_ref=src, dst_ref=land_refs[t].at[pl.ds(me * m, m), :],
        send_sem=send_sems.at[7 * t + k - 1], recv_sem=recv_sems.at[7 * t + k - 1],
        device_id=dev, device_id_type=pl.DeviceIdType.MESH)


def _direct_landing(t, k, src_refs, land_refs, send_sems, recv_sems, rows_of, gather):
    dev, peer, me = _peer_of(k)
    m = rows_of[t]
    src = src_refs[t] if gather else src_refs[t].at[pl.ds(me * m, m), :]
    return pltpu.make_async_remote_copy(
        src_ref=src, dst_ref=land_refs[t].at[pl.ds(peer * m, m), :],
        send_sem=send_sems.at[7 * t + k - 1], recv_sem=recv_sems.at[7 * t + k - 1],
        device_id=dev, device_id_type=pl.DeviceIdType.MESH)


def _direct_start(name, srcs, gather, dep=None):
    nt = len(srcs)
    rows_of = [a.shape[0] if gather else a.shape[0] // N_DEV for a in srcs]
    lands = [pltpu.with_memory_space_constraint(lax.empty((N_DEV * m, a.shape[1]), a.dtype), pltpu.HBM)
             for a, m in zip(srcs, rows_of)]

    n_dep = 0 if dep is None else 1

    def body(*refs):
        src_refs, land_refs = refs[:nt], refs[nt:2 * nt]
        send_sems, recv_sems = refs[2 * nt + n_dep], refs[2 * nt + n_dep + 1]
        token = refs[-1]
        for t in range(nt):
            for k in range(1, N_DEV):
                _direct_copy(t, k, src_refs, land_refs, send_sems, recv_sems, rows_of, gather).start()
        token[...] = jnp.zeros(token.shape, F32)

    out = _pcall(
        body, name=name,
        out_shape=(pltpu.SemaphoreType.DMA((7 * nt,)), pltpu.SemaphoreType.DMA((7 * nt,)),
                   *[pltpu.HBM(a.shape, a.dtype) for a in srcs], *[pltpu.HBM(a.shape, a.dtype) for a in lands],
                   jax.ShapeDtypeStruct((8, 128), F32)),
        in_specs=(_HBM,) * (2 * nt) + (pl.BlockSpec(memory_space=pl.ANY),) * n_dep,
        out_specs=(_SEM, _SEM) + (_HBM,) * (2 * nt) + (pl.BlockSpec(memory_space=pltpu.VMEM),),
        input_output_aliases={i: 2 + i for i in range(2 * nt)},
        compiler_params=pltpu.CompilerParams(has_side_effects=_EFFECT),
    )(*[pltpu.with_memory_space_constraint(a, pltpu.HBM) for a in srcs], *lands, *(() if dep is None else (dep,)))
    return (out[0], out[1], list(out[2:2 + nt]), list(out[2 + nt:2 + 2 * nt]), rows_of, gather), out[-1]


def _direct_wait(name, handle, after):
    send_sems, recv_sems, srcs, lands, rows_of, gather = handle
    nt = len(srcs)

    def body(*refs):
        src_refs, land_refs = refs[:nt], refs[nt:2 * nt]
        s_sems, r_sems = refs[2 * nt], refs[2 * nt + 1]
        for t in range(nt):
            for k in range(1, N_DEV):
                _direct_copy(t, k, src_refs, land_refs, s_sems, r_sems, rows_of, gather).wait_send()
                _direct_landing(t, k, src_refs, land_refs, s_sems, r_sems, rows_of, gather).wait_recv()

    out = _pcall(
        body, name=name,
        out_shape=tuple(pltpu.HBM(a.shape, a.dtype) for a in srcs) + tuple(pltpu.HBM(a.shape, a.dtype) for a in lands),
        in_specs=(_HBM,) * (2 * nt) + (_SEM, _SEM, pl.BlockSpec(memory_space=pl.ANY)),
        out_specs=(_HBM,) * (2 * nt),
        input_output_aliases={i: i for i in range(2 * nt)},
        compiler_params=pltpu.CompilerParams(has_side_effects=_EFFECT),
    )(*srcs, *lands, send_sems, recv_sems, after)
    return list(out[:nt]), list(out[nt:])


def _adamw_sharded(name, own, parts, w, m, v, rb):
    R, N = own.shape

    def body(o_ref, p_ref, w_ref, m_ref, v_ref, g_ref, d_ref, nm_ref, nv_ref):
        me = 4 * lax.axis_index("x") + 2 * lax.axis_index("y") + lax.axis_index("c")
        g = o_ref[...]
        for k in range(1, N_DEV):
            g = g + p_ref[me ^ k].astype(F32)
        nm = ADAM_B1 * m_ref[...] + (1.0 - ADAM_B1) * g
        nv = ADAM_B2 * v_ref[...] + (1.0 - ADAM_B2) * (g * g)
        m_hat = nm / (1.0 - ADAM_B1 ** ADAM_STEP)
        v_hat = nv / (1.0 - ADAM_B2 ** ADAM_STEP)
        g_ref[...] = g
        d_ref[...] = -ADAM_LR * (m_hat / (jnp.sqrt(v_hat) + ADAM_EPS) + ADAM_WD * w_ref[...])
        nm_ref[...] = nm
        nv_ref[...] = nv

    blk = pl.BlockSpec((rb, N), lambda i: (i, 0))
    sh = jax.ShapeDtypeStruct((R, N), F32)
    return _pcall(body, name=name, grid=(R // rb,),
                  in_specs=[blk, pl.BlockSpec((N_DEV, rb, N), lambda i: (0, i, 0)), blk, blk, blk],
                  out_specs=[blk] * 4, out_shape=[sh] * 4, compiler_params=_cparams(1))(own, parts, w, m, v)


def _adamw(name, parts, w, m, v, rb):
    _, R, N = parts.shape

    def body(p_ref, w_ref, m_ref, v_ref, g_ref, d_ref, nm_ref, nv_ref):
        g = p_ref[0]
        for s in range(1, N_DEV):
            g = g + p_ref[s]
        nm = ADAM_B1 * m_ref[...] + (1.0 - ADAM_B1) * g
        nv = ADAM_B2 * v_ref[...] + (1.0 - ADAM_B2) * (g * g)
        m_hat = nm / (1.0 - ADAM_B1 ** ADAM_STEP)
        v_hat = nv / (1.0 - ADAM_B2 ** ADAM_STEP)
        g_ref[...] = g
        d_ref[...] = -ADAM_LR * (m_hat / (jnp.sqrt(v_hat) + ADAM_EPS) + ADAM_WD * w_ref[...])
        nm_ref[...] = nm
        nv_ref[...] = nv

    blk = pl.BlockSpec((rb, N), lambda i: (i, 0))
    sh = jax.ShapeDtypeStruct((R, N), F32)
    return _pcall(body, name=name, grid=(R // rb,), in_specs=[pl.BlockSpec((N_DEV, rb, N), lambda i: (0, i, 0)), blk, blk, blk],
                  out_specs=[blk] * 4, out_shape=[sh] * 4, compiler_params=_cparams(1))(parts, w, m, v)


EARLY = (("w_in", True),)
LATE = (("ffn_w1", True), ("ffn_w3", True), ("w_out", False), ("ffn_w2", False), ("ple_gate_w", False))
MISC = (("s5_glu_w", False), ("rw_w2", True), ("rw_a2", True), ("rw_g2", True), ("ple_up_w", True))
SHARDED_NAMES = tuple(n for n, _ in EARLY + LATE + MISC)
PACK_COLS = 1024
SMALL_ROWS = 144
WEIGHT_NAMES = ("norm_mix", "w_in", "s5_lam_re", "s5_lam_im", "s5_log_step", "s5_b_re", "s5_b_im", "s5_c_re", "s5_c_im", "s5_d",
                "s5_glu_w", "s5_glu_b", "rw_shift_mu", "rw_w0", "rw_w2", "rw_a0", "rw_a2", "rw_g2", "rw_k_k", "rw_k_a", "rw_r_k",
                "rw_ln_w", "rw_ln_b", "w_out", "norm_ffn", "ffn_w1", "ffn_w3", "ffn_w2", "norm_ple", "ple_gate_w", "ple_up_w",
                "final_norm")
SMALL_NAMES = tuple(n for n in WEIGHT_NAMES if n not in SHARDED_NAMES)
ARG_NAMES = ("x", "p") + WEIGHT_NAMES + ("loss_target",) + tuple("m_" + n for n in WEIGHT_NAMES) + tuple("v_" + n for n in WEIGHT_NAMES)


def _travel(a, tr):
    return a.T if tr else a


def _pack_misc(blocks):
    lead = blocks[0].shape[:-2]
    return jnp.concatenate([b.reshape(lead + (-1, PACK_COLS)) for b in blocks], axis=len(lead))


def _unpack_misc(packed, shapes):
    lead = packed.shape[:-2]
    out, off = [], 0
    for r, c in shapes:
        n = r * c // PACK_COLS
        out.append(lax.slice_in_dim(packed, off, off + n, axis=len(lead)).reshape(lead + (r, c)))
        off += n
    return out


def _pack_small(arrs):
    flat = jnp.concatenate([a.reshape(-1).astype(F32) for a in arrs])
    return jnp.pad(flat, (0, SMALL_ROWS * PACK_COLS - flat.shape[0])).reshape(SMALL_ROWS, PACK_COLS)


def _kernel_impl(ins):
    x, p, target = ins["x"][0], ins["p"][0, 0], ins["loss_target"][0]
    me = 4 * lax.axis_index("x") + 2 * lax.axis_index("y") + lax.axis_index("c")
    small = {n: (ins[n] if n == "final_norm" else ins[n][0]) for n in SMALL_NAMES}
    trav = lambda pre, n, tr: _travel(ins[pre + n][0], tr)
    misc_shapes = [trav("", n, tr).shape for n, tr in MISC]

    early = _all_gather("ag_early", [trav("", n, tr).astype(BF16) for n, tr in EARLY]
                        + [_pack_misc([trav("", n, tr).astype(BF16) for n, tr in MISC])])
    late_handle, late_token = _direct_start("ag_late_start", [trav("", n, tr).astype(BF16) for n, tr in LATE], True, early[-1])
    W = dict(small)
    for (n, tr), g in zip(EARLY, early):
        W[n] = _travel(g, tr)
    for (n, tr), g in zip(MISC, _unpack_misc(early[-1].reshape(N_DEV, -1, PACK_COLS), misc_shapes)):
        W[n] = _travel(g.reshape(-1, g.shape[-1]), tr)

    def late_weights(after):
        shards, lands = _direct_wait("ag_late_wait", late_handle, after)
        full = [lax.dynamic_update_slice_in_dim(ld, sh, me * sh.shape[0], axis=0) for ld, sh in zip(lands, shards)]
        return {n: _travel(g, tr) for (n, tr), g in zip(LATE, full)}

    gt = lambda G, n, tr: _travel(G[n], tr)
    started = {}

    def grads_ready(G):
        started["h"], token = _direct_start("grad_late_start", [gt(G, n, tr) for n, tr in LATE], False)
        return token

    loss_part, dx, G = _local_step(x, p, target, W, late_weights, grads_ready, late_token)

    misc_g = _pack_misc([gt(G, n, tr).reshape((N_DEV,) + shp) for (n, tr), shp in zip(MISC, misc_shapes)])
    early_full = [gt(G, n, tr) for n, tr in EARLY] + [misc_g.reshape(-1, PACK_COLS)]
    early_handle, _ = _direct_start("grad_early_start", [a.astype(BF16) for a in early_full], False)
    small_own = _pack_small([G[n] for n in SMALL_NAMES])
    small_handle, small_token = _direct_start("grad_small_start", [small_own], True)
    late_src, late_land = _direct_wait("grad_late_wait", started["h"], small_token)

    outs = {}

    def emit(names_shapes, res):
        for tag, val in zip(("grad_", "delta_", "new_m_", "new_v_"), res):
            for n, v in names_shapes(val):
                outs[tag + n] = v

    def sharded_update(n, tr, src, land):
        rows = src.shape[0] // N_DEV
        own = lax.dynamic_slice_in_dim(src, me * rows, rows, axis=0)
        res = _adamw_sharded("adamw_" + n, own, land.reshape(N_DEV, rows, land.shape[1]),
                             trav("", n, tr), trav("m_", n, tr), trav("v_", n, tr), _pick_rows(rows))
        emit(lambda val: [(n, _travel(val, tr).reshape(ins[n].shape))], res)
        return res[0]

    for (n, tr), src, land in zip(LATE, late_src, late_land):
        last = sharded_update(n, tr, src, land)
    _, early_land = _direct_wait("grad_early_wait", early_handle, last)
    for (n, tr), src, land in zip(EARLY, early_full[:-1], early_land[:-1]):
        sharded_update(n, tr, src, land)
    pm = lambda pre: _pack_misc([trav(pre, n, tr) for n, tr in MISC])
    rows = early_full[-1].shape[0] // N_DEV
    res = _adamw_sharded("adamw_misc", lax.dynamic_slice_in_dim(early_full[-1], me * rows, rows, axis=0),
                         early_land[-1].reshape(N_DEV, rows, PACK_COLS), pm(""), pm("m_"), pm("v_"), rows)
    emit(lambda val: [(n, _travel(b, tr).reshape(ins[n].shape)) for (n, tr), b in zip(MISC, _unpack_misc(val, misc_shapes))], res)
    ps = lambda pre: _pack_small([ins[pre + n] for n in SMALL_NAMES])
    small_src, small_land = _direct_wait("grad_small_wait", small_handle, res[0])
    gsm = lax.dynamic_update_slice_in_dim(small_land[0], small_src[0], me * SMALL_ROWS, axis=0)
    res = _adamw("adamw_replicated", gsm.reshape(N_DEV, SMALL_ROWS, PACK_COLS), ps(""), ps("m_"), ps("v_"), SMALL_ROWS)

    def split_small(val):
        flat, off, o = val.reshape(-1), 0, []
        for n in SMALL_NAMES:
            o.append((n, flat[off:off + ins[n].size].reshape(ins[n].shape)))
            off += ins[n].size
        return o

    emit(split_small, res)
    loss = lax.psum(loss_part, MESH_AXES)
    res = [loss, dx[None]]
    for tag in ("grad_", "delta_", "new_m_", "new_v_"):
        res += [outs[tag + n] for n in WEIGHT_NAMES]
    return tuple(res)


def _pick_rows(r):
    best = 8
    for b in range(8, 257, 8):
        if r % b == 0:
            best = b
    return best


def kernel(x, p, norm_mix, w_in, s5_lam_re, s5_lam_im, s5_log_step, s5_b_re, s5_b_im, s5_c_re, s5_c_im, s5_d, s5_glu_w, s5_glu_b, rw_shift_mu, rw_w0, rw_w2, rw_a0, rw_a2, rw_g2, rw_k_k, rw_k_a, rw_r_k, rw_ln_w, rw_ln_b, w_out, norm_ffn, ffn_w1, ffn_w3, ffn_w2, norm_ple, ple_gate_w, ple_up_w, final_norm, loss_target, m_norm_mix, m_w_in, m_s5_lam_re, m_s5_lam_im, m_s5_log_step, m_s5_b_re, m_s5_b_im, m_s5_c_re, m_s5_c_im, m_s5_d, m_s5_glu_w, m_s5_glu_b, m_rw_shift_mu, m_rw_w0, m_rw_w2, m_rw_a0, m_rw_a2, m_rw_g2, m_rw_k_k, m_rw_k_a, m_rw_r_k, m_rw_ln_w, m_rw_ln_b, m_w_out, m_norm_ffn, m_ffn_w1, m_ffn_w3, m_ffn_w2, m_norm_ple, m_ple_gate_w, m_ple_up_w, m_final_norm, v_norm_mix, v_w_in, v_s5_lam_re, v_s5_lam_im, v_s5_log_step, v_s5_b_re, v_s5_b_im, v_s5_c_re, v_s5_c_im, v_s5_d, v_s5_glu_w, v_s5_glu_b, v_rw_shift_mu, v_rw_w0, v_rw_w2, v_rw_a0, v_rw_a2, v_rw_g2, v_rw_k_k, v_rw_k_a, v_rw_r_k, v_rw_ln_w, v_rw_ln_b, v_w_out, v_norm_ffn, v_ffn_w1, v_ffn_w3, v_ffn_w2, v_norm_ple, v_ple_gate_w, v_ple_up_w, v_final_norm):
    return _kernel_impl(dict(zip(ARG_NAMES, (x, p, norm_mix, w_in, s5_lam_re, s5_lam_im, s5_log_step, s5_b_re, s5_b_im, s5_c_re, s5_c_im, s5_d, s5_glu_w, s5_glu_b, rw_shift_mu, rw_w0, rw_w2, rw_a0, rw_a2, rw_g2, rw_k_k, rw_k_a, rw_r_k, rw_ln_w, rw_ln_b, w_out, norm_ffn, ffn_w1, ffn_w3, ffn_w2, norm_ple, ple_gate_w, ple_up_w, final_norm, loss_target, m_norm_mix, m_w_in, m_s5_lam_re, m_s5_lam_im, m_s5_log_step, m_s5_b_re, m_s5_b_im, m_s5_c_re, m_s5_c_im, m_s5_d, m_s5_glu_w, m_s5_glu_b, m_rw_shift_mu, m_rw_w0, m_rw_w2, m_rw_a0, m_rw_a2, m_rw_g2, m_rw_k_k, m_rw_k_a, m_rw_r_k, m_rw_ln_w, m_rw_ln_b, m_w_out, m_norm_ffn, m_ffn_w1, m_ffn_w3, m_ffn_w2, m_norm_ple, m_ple_gate_w, m_ple_up_w, m_final_norm, v_norm_mix, v_w_in, v_s5_lam_re, v_s5_lam_im, v_s5_log_step, v_s5_b_re, v_s5_b_im, v_s5_c_re, v_s5_c_im, v_s5_d, v_s5_glu_w, v_s5_glu_b, v_rw_shift_mu, v_rw_w0, v_rw_w2, v_rw_a0, v_rw_a2, v_rw_g2, v_rw_k_k, v_rw_k_a, v_rw_r_k, v_rw_ln_w, v_rw_ln_b, v_w_out, v_norm_ffn, v_ffn_w1, v_ffn_w3, v_ffn_w2, v_norm_ple, v_ple_gate_w, v_ple_up_w, v_final_norm))))
```

```python
import functools

import jax
import jax.numpy as jnp
from jax import lax
from jax.experimental import pallas as pl
from jax.experimental.pallas import tpu as pltpu

F32 = jnp.float32
BF16 = jnp.bfloat16

D_MODEL = 1024
S5_WIDTH = 512
RW_WIDTH = 512
S5_GROUP = 16
S5_GROUPS = 32
S5_STATE = 64
S5_LANES = S5_GROUPS * S5_STATE
HEAD = 64
SHIFT_COLS = 1792
IN_COLS = 2304
FFN_HIDDEN = 2816
PLE_DIM = 256
RMS_EPS = 1e-6
GN_EPS = 64e-5
L2_EPS = 1e-12
CHUNK = 64
N_DEV = 8

ADAM_LR = 0.001
ADAM_B1 = 0.9
ADAM_B2 = 0.999
ADAM_EPS = 1e-08
ADAM_WD = 0.01
ADAM_STEP = 10

VMEM_LIMIT = 56 * 1024 * 1024


def _pcall(body, **kw):
    return pl.pallas_call(body, **kw)


def _cparams(n_grid):
    return pltpu.CompilerParams(dimension_semantics=("arbitrary",) * n_grid, vmem_limit_bytes=VMEM_LIMIT)


def _dot(a, b):
    return jnp.dot(a, b, preferred_element_type=F32)


def _dot_nt(a, b):
    return lax.dot_general(a, b, (((1,), (1,)), ((), ())), preferred_element_type=F32)


def _dot_tn(a, b):
    return lax.dot_general(a, b, (((0,), (0,)), ((), ())), preferred_element_type=F32)


def _mmc(w, diff=True, tr=False):
    fw, bw = (_dot_nt, _dot) if tr else (_dot, _dot_nt)
    if not diff:
        return lambda x: fw(x.astype(BF16), w)

    @jax.custom_vjp
    def f(x):
        return fw(x.astype(BF16), w)

    def fwd(x):
        return fw(x.astype(BF16), w), None

    def bwd(_, dy):
        return (bw(dy.astype(BF16), w),)

    f.defvjp(fwd, bwd)
    return f


def _split_dot(x, m, n_split):
    acc = None
    rem = x
    for s in range(n_split):
        part = rem.astype(BF16)
        t = _dot(part, m)
        acc = t if acc is None else acc + t
        if s + 1 < n_split:
            rem = rem - part.astype(F32)
    return acc


def _segsum(m, diff=True):
    if not diff:
        return lambda x: _split_dot(x, m, 2)

    @jax.custom_vjp
    def f(x):
        return _split_dot(x, m, 2)

    def fwd(x):
        return _split_dot(x, m, 2), None

    def bwd(_, dy):
        return (_split_dot(dy, m, 2),)

    f.defvjp(fwd, bwd)
    return f


def _head_indicator(n):
    r = lax.broadcasted_iota(jnp.int32, (n, n), 0) // HEAD
    c = lax.broadcasted_iota(jnp.int32, (n, n), 1) // HEAD
    return (r == c).astype(BF16)


def _rms(x, g):
    return x * lax.rsqrt(jnp.mean(x * x, axis=-1, keepdims=True) + RMS_EPS) * g


def _softplus(x):
    return jnp.maximum(x, 0.0) + jnp.log(1.0 + jnp.exp(-jnp.abs(x)))


def _sigmoid(x):
    return 1.0 / (1.0 + jnp.exp(-x))


def _gelu(x):
    return 0.5 * x * (1.0 + jnp.tanh(0.7978845608028654 * (x + 0.044715 * (x * x * x))))


def _tok_call(name, fn, L, TB, tok_in, const_in, tok_out, acc_out=(), deps=()):
    nb = L // TB
    g8 = TB // 8
    in_specs, args = [], []
    for spec in tok_in:
        if len(spec) == 1:
            arr = spec[0]
            in_specs.append(pl.BlockSpec((arr.shape[0], TB, HEAD), lambda i: (0, i, 0)))
            args.append(arr)
            continue
        arr, width, cb = spec[:3]
        mode = spec[3] if len(spec) > 3 else None
        if mode is None:
            in_specs.append(pl.BlockSpec((TB, width), lambda i, cb=cb: (i, cb)))
        elif mode == "prev":
            in_specs.append(pl.BlockSpec((8, width), lambda i, cb=cb: (jnp.maximum(i * g8 - 1, 0), cb)))
        else:
            in_specs.append(pl.BlockSpec((8, width), lambda i, cb=cb: (jnp.minimum((i + 1) * g8, L // 8 - 1), cb)))
        args.append(arr)
    for c in const_in:
        in_specs.append(pl.BlockSpec(c.shape, lambda i, nd=c.ndim: (0,) * nd, pipeline_mode=pl.Buffered(1)))
        args.append(c)
    for d in deps:
        in_specs.append(pl.BlockSpec(d.shape, lambda i, nd=d.ndim: (0,) * nd))
        args.append(d)
    out_shape, out_specs = [], []
    for width, dt in tok_out:
        if width == "heads":
            out_shape.append(jax.ShapeDtypeStruct((N_HEAD, L, HEAD), dt))
            out_specs.append(pl.BlockSpec((N_HEAD, TB, HEAD), lambda i: (0, i, 0)))
            continue
        out_shape.append(jax.ShapeDtypeStruct((L, width), dt))
        out_specs.append(pl.BlockSpec((TB, width), lambda i: (i, 0)))
    for shp in acc_out:
        out_shape.append(jax.ShapeDtypeStruct(shp, F32))
        out_specs.append(pl.BlockSpec(shp, lambda i, nd=len(shp): (0,) * nd))
    n_tok, n_const, n_to = len(tok_in), len(const_in), len(tok_out)

    def body(*refs):
        i = pl.program_id(0)
        tv = [r[...] if len(r.shape) == 2 else jnp.concatenate([r[h] for h in range(r.shape[0])], axis=1)
              for r in refs[:n_tok]]
        cv = [r[...] for r in refs[n_tok:n_tok + n_const]]
        orefs = refs[n_tok + n_const + len(deps):]
        outs = fn(i, tv, cv)
        for r, v in zip(orefs[:n_to], outs[:n_to]):
            if len(r.shape) == 3:
                for h in range(r.shape[0]):
                    r[h] = v[:, h * HEAD:(h + 1) * HEAD].astype(r.dtype)
            else:
                r[...] = v.astype(r.dtype)
        for r, v in zip(orefs[n_to:], outs[n_to:]):
            @pl.when(i == 0)
            def _(r=r):
                r[...] = jnp.zeros(r.shape, r.dtype)

            r[...] += v

    res = _pcall(body, name=name, grid=(nb,), in_specs=in_specs, out_specs=out_specs, out_shape=out_shape,
                 compiler_params=_cparams(1))(*args)
    return res


def _pick_block(n, cap):
    best = None
    for b in range(128, min(n, cap) + 1, 128):
        if n % b == 0:
            best = b
    return best if best is not None else n


def _mm_tn(name, a, b):
    T, M = a.shape
    N = b.shape[1]
    bm, bn, bt = _pick_block(M, 1024), _pick_block(N, 1536), _pick_block(T, 512)

    def body(a_ref, b_ref, o_ref):
        t = pl.program_id(2)

        @pl.when(t == 0)
        def _():
            o_ref[...] = jnp.zeros(o_ref.shape, F32)

        o_ref[...] += _dot_tn(a_ref[...].astype(BF16), b_ref[...].astype(BF16))

    return _pcall(body, name=name, grid=(M // bm, N // bn, T // bt),
                  in_specs=[pl.BlockSpec((bt, bm), lambda m, n, t: (t, m)), pl.BlockSpec((bt, bn), lambda m, n, t: (t, n))],
                  out_specs=pl.BlockSpec((bm, bn), lambda m, n, t: (m, n)),
                  out_shape=jax.ShapeDtypeStruct((M, N), F32), compiler_params=_cparams(3))(a, b)


def _s5_param_fn(lam_re, lam_im, log_step, bt_re, bt_im):
    dt = jnp.exp(log_step)
    e = jnp.exp(lam_re * dt)
    lb_re = e * jnp.cos(lam_im * dt)
    lb_im = e * jnp.sin(lam_im * dt)
    den = lam_re * lam_re + lam_im * lam_im
    nr, ni = lb_re - 1.0, lb_im
    co_re = (nr * lam_re + ni * lam_im) / den
    co_im = (ni * lam_re - nr * lam_im) / den
    cr, ci = co_re[:, None, :], co_im[:, None, :]
    return lb_re, lb_im, cr * bt_re - ci * bt_im, cr * bt_im + ci * bt_re


def _s5_param_fwd(lam_re, lam_im, log_step, bt_re, bt_im):
    def body(a, b, c, d, e, o1, o2, o3, o4):
        r = _s5_param_fn(a[...], b[...], c[...], d[...], e[...])
        o1[...], o2[...], o3[...], o4[...] = r

    sh = jax.ShapeDtypeStruct
    return _pcall(body, name="s5_param_fwd",
                  out_shape=[sh(lam_re.shape, F32), sh(lam_re.shape, F32), sh(bt_re.shape, F32), sh(bt_re.shape, F32)])(
        lam_re, lam_im, log_step, bt_re, bt_im)


def _s5_param_bwd(lam_re, lam_im, log_step, bt_re, bt_im, d_lb_re, d_lb_im, d_bb_re, d_bb_im):
    def body(a, b, c, d, e, g1, g2, g3, g4, o1, o2, o3, o4, o5):
        _, vjp = jax.vjp(_s5_param_fn, a[...], b[...], c[...], d[...], e[...])
        r = vjp((g1[...], g2[...], g3[...], g4[...]))
        o1[...], o2[...], o3[...], o4[...], o5[...] = r

    sh = jax.ShapeDtypeStruct
    return _pcall(body, name="s5_param_bwd",
                  out_shape=[sh(lam_re.shape, F32), sh(lam_re.shape, F32), sh(log_step.shape, F32),
                             sh(bt_re.shape, F32), sh(bt_re.shape, F32)])(
        lam_re, lam_im, log_step, bt_re, bt_im, d_lb_re, d_lb_im, d_bb_re, d_bb_im)


def _cmul(ar, ai, br, bi):
    return ar * br - ai * bi, ar * bi + ai * br


def _scan_consts(lr, li, reverse):
    n = lr.shape[1]
    sub = lax.broadcasted_iota(jnp.int32, (8, n), 0)
    pows = [(lr, li)]
    for _ in range(7):
        pows.append(_cmul(pows[-1][0], pows[-1][1], lr, li))
    steps = []
    for s in (1, 2, 4):
        m = (sub < 8 - s) if reverse else (sub >= s)
        pr, pi = pows[s - 1]
        steps.append((s, jnp.where(m, jnp.broadcast_to(pr, (8, n)), 0.0), jnp.where(m, jnp.broadcast_to(pi, (8, n)), 0.0)))
    wr = jnp.zeros((8, n), F32)
    wi = jnp.zeros((8, n), F32)
    for r in range(8):
        e = (8 - r) if reverse else (r + 1)
        wr = jnp.where(sub == r, jnp.broadcast_to(pows[e - 1][0], (8, n)), wr)
        wi = jnp.where(sub == r, jnp.broadcast_to(pows[e - 1][1], (8, n)), wi)
    return steps, wr, wi


def _scan_rows(sre, sim, carry, lr, li, rows, reverse):
    steps, wr, wi = _scan_consts(lr, li, reverse)
    ng = rows // 8

    def step(gi, _):
        g = (ng - 1 - gi) if reverse else gi
        base = pl.multiple_of(g * 8, 8)
        xr = sre[pl.ds(base, 8), :]
        xi = sim[pl.ds(base, 8), :]
        for s, pr, pi in steps:
            sh = (8 - s) if reverse else s
            yr = pltpu.roll(xr, sh, 0)
            yi = pltpu.roll(xi, sh, 0)
            xr, xi = xr + pr * yr - pi * yi, xi + pr * yi + pi * yr
        cr = carry[0:1, :]
        ci = carry[1:2, :]
        xr, xi = xr + wr * cr - wi * ci, xi + wr * ci + wi * cr
        sre[pl.ds(base, 8), :] = xr
        sim[pl.ds(base, 8), :] = xi
        edge = 0 if reverse else 7
        carry[0:1, :] = xr[edge:edge + 1, :]
        carry[1:2, :] = xi[edge:edge + 1, :]
        return 0

    lax.fori_loop(0, ng, step, 0)


S5_Q = 4
S5_QL = S5_WIDTH // S5_Q
S5_QS = S5_LANES // S5_Q


def _s5_scan_fwd(proj, bq_re, bq_im, cq_re, cq_im, lbar, dskip, L, TB):
    nb = L // TB

    def body(u_ref, bre, bim, cre, cim, lb_ref, d_ref, y_ref, ck_ref, sre, sim, carry):
        i = pl.program_id(0)

        @pl.when(i == 0)
        def _():
            carry[...] = jnp.zeros(carry.shape, F32)

        ck_ref[0] = carry[...]
        u = u_ref[...]
        ub = u.astype(BF16)
        for q in range(S5_Q):
            uq = ub[:, q * S5_QL:(q + 1) * S5_QL]
            sre[:, q * S5_QS:(q + 1) * S5_QS] = _dot(uq, bre[q])
            sim[:, q * S5_QS:(q + 1) * S5_QS] = _dot(uq, bim[q])
        _scan_rows(sre, sim, carry, lb_ref[0:1, :], lb_ref[1:2, :], TB, False)
        for q in range(S5_Q):
            sl = slice(q * S5_QL, (q + 1) * S5_QL)
            ss = slice(q * S5_QS, (q + 1) * S5_QS)
            y_ref[:, sl] = (_dot(sre[:, ss].astype(BF16), cre[q]) - _dot(sim[:, ss].astype(BF16), cim[q])
                            + u[:, sl] * d_ref[:, sl])

    full = lambda a: pl.BlockSpec(a.shape, lambda i, nd=a.ndim: (0,) * nd)
    return _pcall(
        body, name="s5_scan_fwd", grid=(nb,),
        in_specs=[pl.BlockSpec((TB, S5_WIDTH), lambda i: (i, 0)), full(bq_re), full(bq_im), full(cq_re), full(cq_im),
                  full(lbar), full(dskip)],
        out_specs=[pl.BlockSpec((TB, S5_WIDTH), lambda i: (i, 0)), pl.BlockSpec((1, 8, S5_LANES), lambda i: (i, 0, 0))],
        out_shape=[jax.ShapeDtypeStruct((L, S5_WIDTH), F32), jax.ShapeDtypeStruct((nb, 8, S5_LANES), F32)],
        scratch_shapes=[pltpu.VMEM((TB, S5_LANES), F32), pltpu.VMEM((TB, S5_LANES), F32), pltpu.VMEM((8, S5_LANES), F32)],
        compiler_params=_cparams(1))(proj, bq_re, bq_im, cq_re, cq_im, lbar, dskip)


def _s5_scan_bwd(proj, dy, ck, bq_re, bq_im, cq_re, cq_im, lbar, dskip, L, TB):
    nb = L // TB
    ng = TB // 8

    def body(u_ref, dy_ref, ck_ref, bre, bim, cre, cim, lb_ref, d_ref,
             du_ref, dbre, dbim, dcre, dcim, dlb_ref, dd_ref, sre, sim, gre, gim, carry, gcarry):
        i = pl.program_id(0)

        @pl.when(i == 0)
        def _():
            gcarry[...] = jnp.zeros(gcarry.shape, F32)
            dbre[...] = jnp.zeros(dbre.shape, F32)
            dbim[...] = jnp.zeros(dbim.shape, F32)
            dcre[...] = jnp.zeros(dcre.shape, F32)
            dcim[...] = jnp.zeros(dcim.shape, F32)
            dlb_ref[...] = jnp.zeros(dlb_ref.shape, F32)
            dd_ref[...] = jnp.zeros(dd_ref.shape, F32)

        lr = lb_ref[0:1, :]
        li = lb_ref[1:2, :]
        u = u_ref[...]
        ub = u.astype(BF16)
        dy_v = dy_ref[...]
        dyb = dy_v.astype(BF16)
        carry[...] = ck_ref[0]
        for q in range(S5_Q):
            uq = ub[:, q * S5_QL:(q + 1) * S5_QL]
            dq = dyb[:, q * S5_QL:(q + 1) * S5_QL]
            ss = slice(q * S5_QS, (q + 1) * S5_QS)
            sre[:, ss] = _dot(uq, bre[q])
            sim[:, ss] = _dot(uq, bim[q])
            gre[:, ss] = _dot_nt(dq, cre[q])
            gim[:, ss] = -_dot_nt(dq, cim[q])
        _scan_rows(sre, sim, carry, lr, li, TB, False)
        _scan_rows(gre, gim, gcarry, lr, -li, TB, True)

        sub = lax.broadcasted_iota(jnp.int32, (8, S5_LANES), 0)
        c0r = ck_ref[0, 0:1, :]
        c0i = ck_ref[0, 1:2, :]

        def acc_step(g, acc):
            ar, ai = acc
            base = pl.multiple_of(g * 8, 8)
            pbase = pl.multiple_of(jnp.maximum(g - 1, 0) * 8, 8)
            first = g == 0
            lastr = jnp.where(first, c0r, sre[pl.ds(pbase, 8), :][7:8, :])
            lasti = jnp.where(first, c0i, sim[pl.ds(pbase, 8), :][7:8, :])
            spr = jnp.where(sub == 0, jnp.broadcast_to(lastr, sub.shape), pltpu.roll(sre[pl.ds(base, 8), :], 1, 0))
            spi = jnp.where(sub == 0, jnp.broadcast_to(lasti, sub.shape), pltpu.roll(sim[pl.ds(base, 8), :], 1, 0))
            gr = gre[pl.ds(base, 8), :]
            gi_ = gim[pl.ds(base, 8), :]
            return ar + gr * spr + gi_ * spi, ai - gr * spi + gi_ * spr

        z8 = jnp.zeros((8, S5_LANES), F32)
        ar, ai = lax.fori_loop(0, ng, acc_step, (z8, z8))
        dlb_ref[0:1, :] += jnp.sum(ar, axis=0, keepdims=True)
        dlb_ref[1:2, :] += jnp.sum(ai, axis=0, keepdims=True)

        dd_ref[...] += jnp.sum(dy_v * u, axis=0, keepdims=True)
        for q in range(S5_Q):
            sl = slice(q * S5_QL, (q + 1) * S5_QL)
            ss = slice(q * S5_QS, (q + 1) * S5_QS)
            grq = gre[:, ss].astype(BF16)
            giq = gim[:, ss].astype(BF16)
            du_ref[:, sl] = _dot_nt(grq, bre[q]) + _dot_nt(giq, bim[q]) + dy_v[:, sl] * d_ref[:, sl]
            dbre[q] += _dot_tn(ub[:, sl], grq)
            dbim[q] += _dot_tn(ub[:, sl], giq)
            dcre[q] += _dot_tn(sre[:, ss].astype(BF16), dyb[:, sl])
            dcim[q] -= _dot_tn(sim[:, ss].astype(BF16), dyb[:, sl])

    full = lambda a: pl.BlockSpec(a.shape, lambda i, nd=a.ndim: (0,) * nd)
    rev = lambda i: (nb - 1 - i, 0)
    sh = jax.ShapeDtypeStruct
    outs = [sh((L, S5_WIDTH), F32), sh(bq_re.shape, F32), sh(bq_im.shape, F32), sh(cq_re.shape, F32), sh(cq_im.shape, F32),
            sh((8, S5_LANES), F32), sh((1, S5_WIDTH), F32)]
    fo = lambda s: pl.BlockSpec(s.shape, lambda i, nd=len(s.shape): (0,) * nd)
    return _pcall(
        body, name="s5_scan_bwd", grid=(nb,),
        in_specs=[pl.BlockSpec((TB, S5_WIDTH), rev), pl.BlockSpec((TB, S5_WIDTH), rev),
                  pl.BlockSpec((1, 8, S5_LANES), lambda i: (nb - 1 - i, 0, 0)),
                  full(bq_re), full(bq_im), full(cq_re), full(cq_im), full(lbar), full(dskip)],
        out_specs=[pl.BlockSpec((TB, S5_WIDTH), rev)] + [fo(s) for s in outs[1:]],
        out_shape=outs,
        scratch_shapes=[pltpu.VMEM((TB, S5_LANES), F32)] * 4 + [pltpu.VMEM((8, S5_LANES), F32)] * 2,
        compiler_params=_cparams(1))(proj, dy, ck, bq_re, bq_im, cq_re, cq_im, lbar, dskip)


N_HEAD = RW_WIDTH // HEAD
_NN = (((2,), (1,)), ((0,), (0,)))
_NT = (((2,), (2,)), ((0,), (0,)))
_TN = (((1,), (1,)), ((0,), (0,)))


def _hi_lo(x):
    h = x.astype(BF16)
    return h, (x - h.astype(F32)).astype(BF16)


def _mm_acc(a, b, dims, passes=3):
    dg = lambda p, q: lax.dot_general(p, q, dims, preferred_element_type=F32)
    if passes == 1:
        return dg(a.astype(BF16), b.astype(BF16))
    ah, al = _hi_lo(a)
    bh, bl = _hi_lo(b)
    return dg(ah, bh) + dg(ah, bl) + dg(al, bh)


def _cumsum_rows(x, transpose):
    h, n, _ = x.shape
    ti = lax.broadcasted_iota(jnp.int32, (h, n, n), 1)
    tj = lax.broadcasted_iota(jnp.int32, (h, n, n), 2)
    m = ((tj >= ti) if transpose else (tj <= ti)).astype(BF16)
    acc, rem = None, x
    for s in range(3):
        part = rem.astype(BF16)
        t = lax.dot_general(m, part, _NN, preferred_element_type=F32)
        acc = t if acc is None else acc + t
        if s < 2:
            rem = rem - part.astype(F32)
    return acc


def _slices(x, axis, sizes):
    out, off = [], 0
    for n in sizes:
        out.append(lax.slice_in_dim(x, off, off + n, axis=axis))
        off += n
    return tuple(out)


def _cat_op(axis, sizes, diff):
    plain = lambda *xs: jnp.concatenate(xs, axis=axis)
    if not diff:
        return plain
    f = jax.custom_vjp(plain)
    f.defvjp(lambda *xs: (plain(*xs), None), lambda _, d: _slices(d, axis, sizes))
    return f


def _split_op(axis, sizes, diff):
    plain = lambda x: _slices(x, axis, sizes)
    if not diff:
        return plain
    f = jax.custom_vjp(plain)
    f.defvjp(lambda x: (plain(x), None), lambda _, d: (jnp.concatenate(d, axis=axis),))
    return f


def _mm_ops(diff, passes):
    mm = lambda a, b, dims: _mm_acc(a, b, dims, passes)
    if not diff:
        return (lambda a, b: mm(a, b, _NN), lambda a, b: mm(a, b, _NT), lambda a, b: mm(a, b, _TN))

    @jax.custom_vjp
    def nn(a, b):
        return mm(a, b, _NN)

    nn.defvjp(lambda a, b: (mm(a, b, _NN), (a, b)), lambda r, d: (mm(d, r[1], _NT), mm(r[0], d, _TN)))

    @jax.custom_vjp
    def nt(a, b):
        return mm(a, b, _NT)

    nt.defvjp(lambda a, b: (mm(a, b, _NT), (a, b)), lambda r, d: (mm(d, r[1], _NN), mm(d, r[0], _TN)))

    @jax.custom_vjp
    def tn(a, b):
        return mm(a, b, _TN)

    tn.defvjp(lambda a, b: (mm(a, b, _TN), (a, b)), lambda r, d: (mm(r[1], d, _NT), mm(r[0], d, _NN)))
    return nn, nt, tn


def _cums_op(diff):
    if not diff:
        return lambda x: _cumsum_rows(x, False)

    @jax.custom_vjp
    def cums(x):
        return _cumsum_rows(x, False)

    cums.defvjp(lambda x: (_cumsum_rows(x, False), None), lambda _, d: (_cumsum_rows(d, True),))
    return cums


WKV_PASSES = (1, 1, 1, 1, 1)


def _wkv_chunk(s0, r, w, k, v, a, b, diff):
    p_pair, p_val, p_solve, p_out, p_state = WKV_PASSES
    cums = _cums_op(diff)
    _, nt_pair, _ = _mm_ops(diff, p_pair)
    nn_val, _, _ = _mm_ops(diff, p_val)
    nn_solve, _, _ = _mm_ops(diff, p_solve)
    nn_out, _, _ = _mm_ops(diff, p_out)
    _, _, tn_state = _mm_ops(diff, p_state)
    h, n, d = r.shape
    cat_rows2 = _cat_op(1, (n, n), diff)
    cat_rows3 = _cat_op(1, (n, n, d), diff)
    cat_lanes2 = _cat_op(2, (d, n), diff)
    split_rows2 = _split_op(1, (n, n), diff)
    split_lanes3 = _split_op(2, (n, n, d), diff)
    split_lanes2 = _split_op(2, (d, n), diff)
    ti = lax.broadcasted_iota(jnp.int32, (h, n, n), 1)
    tj = lax.broadcasted_iota(jnp.int32, (h, n, n), 2)
    incl, strict = tj <= ti, tj < ti
    logw = jnp.log(w)
    cum = cums(logw)
    g_in, g_ex, g_inv = jnp.exp(cum), jnp.exp(cum - logw), jnp.exp(-cum)
    ae, re, bi, ki = a * g_ex, r * g_in, b * g_inv, k * g_inv
    top, bot = split_rows2(nt_pair(cat_rows2(ae, re), cat_rows3(bi, ki, s0)))
    tab, tak, u = split_lanes3(top)
    qb, qk, y = split_lanes3(bot)
    tab, tak = jnp.where(strict, tab, 0.0), jnp.where(strict, tak, 0.0)
    qb, qk = jnp.where(incl, qb, 0.0), jnp.where(incl, qk, 0.0)
    tak_v, qk_v = split_rows2(nn_val(cat_rows2(tak, qk), v))
    u = u + tak_v
    npow = tab
    steps = max(1, (n - 1).bit_length())
    for i in range(steps - 1):
        du, npow = split_lanes2(nn_solve(npow, cat_lanes2(u, npow)))
        u = u + du
    u = u + nn_solve(npow, u)
    y = y + nn_out(qb, u) + qk_v
    g_end = jnp.exp(jnp.sum(logw, axis=1, keepdims=True))
    s1 = s0 * g_end + tn_state(cat_rows2(u, v), cat_rows2(bi * g_end, ki * g_end))
    return y, s1


def _wkv_fwd(r, w, k, v, a, b, L):
    nc = L // CHUNK

    def body(r_ref, w_ref, k_ref, v_ref, a_ref, b_ref, y_ref, ck_ref, s_ref):
        c = pl.program_id(0)

        @pl.when(c == 0)
        def _():
            s_ref[...] = jnp.zeros(s_ref.shape, F32)

        s0 = s_ref[...]
        ck_ref[0] = s0
        y, s1 = _wkv_chunk(s0, r_ref[...], w_ref[...], k_ref[...], v_ref[...], a_ref[...], b_ref[...], False)
        y_ref[...] = y
        s_ref[...] = s1

    blk = pl.BlockSpec((N_HEAD, CHUNK, HEAD), lambda c: (0, c, 0))
    return _pcall(
        body, name="wkv_fwd", grid=(nc,), in_specs=[blk] * 6,
        out_specs=[blk, pl.BlockSpec((1, N_HEAD, HEAD, HEAD), lambda c: (c, 0, 0, 0))],
        out_shape=[jax.ShapeDtypeStruct((N_HEAD, L, HEAD), F32), jax.ShapeDtypeStruct((nc, N_HEAD, HEAD, HEAD), F32)],
        scratch_shapes=[pltpu.VMEM((N_HEAD, HEAD, HEAD), F32)],
        compiler_params=_cparams(1))(r, w, k, v, a, b)


def _wkv_bwd(r, w, k, v, a, b, dy, ck, L, deps=()):
    nc = L // CHUNK

    def body(r_ref, w_ref, k_ref, v_ref, a_ref, b_ref, dy_ref, ck_ref, *rest):
        dr_ref, dw_ref, dk_ref, dv_ref, da_ref, db_ref, ds_ref = rest[len(deps):]
        c = pl.program_id(0)

        @pl.when(c == 0)
        def _():
            ds_ref[...] = jnp.zeros(ds_ref.shape, F32)

        _, vjp = jax.vjp(lambda *t: _wkv_chunk(*t, True), ck_ref[0], r_ref[...], w_ref[...], k_ref[...], v_ref[...],
                         a_ref[...], b_ref[...])
        g = vjp((dy_ref[...], ds_ref[...]))
        ds_ref[...] = g[0]
        for o_ref, val in zip((dr_ref, dw_ref, dk_ref, dv_ref, da_ref, db_ref), g[1:]):
            o_ref[...] = val

    blk = pl.BlockSpec((N_HEAD, CHUNK, HEAD), lambda c: (0, nc - 1 - c, 0))
    sh = jax.ShapeDtypeStruct((N_HEAD, L, HEAD), F32)
    return _pcall(
        body, name="wkv_bwd", grid=(nc,),
        in_specs=[blk] * 7 + [pl.BlockSpec((1, N_HEAD, HEAD, HEAD), lambda c: (nc - 1 - c, 0, 0, 0))]
        + [pl.BlockSpec(d.shape, lambda c, nd=d.ndim: (0,) * nd) for d in deps],
        out_specs=[blk] * 6, out_shape=[sh] * 6,
        scratch_shapes=[pltpu.VMEM((N_HEAD, HEAD, HEAD), F32)],
        compiler_params=_cparams(1))(r, w, k, v, a, b, dy, ck, *deps)


TB = 256


def _bf(x):
    return x.astype(BF16)


def _inproj_fwd(x, norm_mix, w_in, L, deps=()):
    def fn(i, tv, cv):
        xn = _rms(tv[0], cv[0])
        return _dot(_bf(xn), cv[1]), xn

    return _tok_call("inproj_fwd", fn, L, TB, [(x, D_MODEL, 0)], [norm_mix, w_in], [(IN_COLS, F32), (D_MODEL, BF16)],
                     deps=deps)


def _s5_post_fn(glu_w, wtop, diff=True):
    mg = _mmc(glu_w, diff)
    mt = _mmc(wtop, diff) if wtop is not None else None

    def f(y, glu_b, e):
        z = _gelu(y)
        out = z * _sigmoid(mg(z) + glu_b + e)
        res = mt(out) if mt is not None else out
        return res, (z, out)

    return f


def _s5_post_fwd(y, glu_w, glu_b, L):
    def fn(i, tv, cv):
        out, _ = _s5_post_fn(cv[0], None, False)(tv[0], cv[1], 0.0)
        return (out,)

    return _tok_call("s5_post_fwd", fn, L, TB, [(y, S5_WIDTH, 0)], [glu_w, glu_b], [(S5_WIDTH, F32)])[0]


def _s5_post_bwd(y, dh1, glu_w, glu_b, wtop, L):
    def fn(i, tv, cv):
        e0 = jnp.zeros((TB, S5_WIDTH), F32)
        _, vjp, (z, out) = jax.vjp(_s5_post_fn(cv[0], cv[2]), tv[0], cv[1], e0, has_aux=True)
        dy, db, de = vjp(tv[1])
        return dy, z, de, out, db

    return _tok_call("s5_post_bwd", fn, L, TB, [(y, S5_WIDTH, 0), (dh1, D_MODEL, 0)], [glu_w, glu_b, wtop],
                     [(S5_WIDTH, F32), (S5_WIDTH, BF16), (S5_WIDTH, BF16), (S5_WIDTH, BF16)], [(1, S5_WIDTH)])


RW_COLBLK = ((RW_WIDTH, 1), (RW_WIDTH, 2), (RW_WIDTH, 3), (128, 16), (128, 17))
RW_MU = ((0, 512), (512, 1024), (1024, 1536), (1536, 1664), (1664, 1792))


def _rw_pre_fn(w2pad, a2pad, g2, diff=True):
    m_w, m_a, m_g = _mmc(w2pad, diff), _mmc(a2pad, diff), _mmc(g2, diff)
    seg = _segsum(_head_indicator(RW_WIDTH), diff)

    def f(zr, zk, zv, zwa, zg, w0, a0, k_k, k_a, e_w, e_a):
        wl_t = jnp.tanh(zwa)
        wlin = w0 + m_w(wl_t) + e_w
        w = -_softplus(-wlin) - 0.5
        decay = jnp.exp(-jnp.exp(w))
        a = _sigmoid(a0 + m_a(zwa) + e_a)
        sg = _sigmoid(zg)
        g = m_g(sg)
        kk = zk * k_k
        kkn = kk / jnp.maximum(jnp.sqrt(seg(kk * kk)), L2_EPS)
        kf = zk * (1.0 + (a - 1.0) * k_a)
        return (zr, decay, kf, zv, -kkn, kkn * a, g), (wl_t, sg)

    return f


def _rw_shifted(i, tv, mu):
    sub = lax.broadcasted_iota(jnp.int32, (TB, 1), 0)
    zs, dif = [], []
    for n in range(5):
        z = tv[n]
        last = jnp.where(i == 0, 0.0, tv[5 + n][7:8, :])
        prev = jnp.where(sub == 0, last, pltpu.roll(z, 1, 0))
        m = mu[:, RW_MU[n][0]:RW_MU[n][1]]
        zs.append(z + (prev - z) * m)
        dif.append(prev - z)
    return zs, dif


def _rw_tok_in(proj):
    return [(proj, wd, cb) for wd, cb in RW_COLBLK] + [(proj, wd, cb, "prev") for wd, cb in RW_COLBLK]


def _rw_pre_fwd(proj, mu, w0, a0, k_k, k_a, w2pad, a2pad, g2, L):
    def fn(i, tv, cv):
        zs, _ = _rw_shifted(i, tv, cv[0])
        outs, _ = _rw_pre_fn(cv[5], cv[6], cv[7], False)(*zs, cv[1], cv[2], cv[3], cv[4], 0.0, 0.0)
        return outs

    return _tok_call("rw_pre_fwd", fn, L, TB, _rw_tok_in(proj), [mu, w0, a0, k_k, k_a, w2pad, a2pad, g2],
                     [("heads", F32)] * 6 + [(RW_WIDTH, F32)])


def _rw_pre_bwd(proj, cots, mu, w0, a0, k_k, k_a, w2pad, a2pad, g2, L):
    def fn(i, tv, cv):
        zs, dif = _rw_shifted(i, tv[:10], cv[0])
        dr1, dr2, dw, dk1, dk2, dv1, dv2, da, db, dg = tv[10:]
        e0 = jnp.zeros((TB, RW_WIDTH), F32)
        _, vjp, (wl_t, sg) = jax.vjp(_rw_pre_fn(cv[5], cv[6], cv[7]), *zs, cv[1], cv[2], cv[3], cv[4], e0, e0, has_aux=True)
        g = vjp((dr1 + dr2, dw, dk1 + dk2, dv1 + dv2, da, db, dg))
        dzs = jnp.concatenate(g[:5], axis=1)
        dmu = jnp.concatenate([jnp.sum(g[n] * dif[n], axis=0, keepdims=True) for n in range(5)], axis=1)
        return dzs, wl_t, zs[3], sg, g[9], g[10], dmu, g[5], g[6], g[7], g[8]

    tok_in = _rw_tok_in(proj) + [((c,) if c.ndim == 3 else (c, RW_WIDTH, 0)) for c in cots]
    return _tok_call("rw_pre_bwd", fn, L, TB, tok_in, [mu, w0, a0, k_k, k_a, w2pad, a2pad, g2],
                     [(SHIFT_COLS, F32), (128, BF16), (128, BF16), (128, BF16), (RW_WIDTH, BF16), (RW_WIDTH, BF16)],
                     [(1, SHIFT_COLS)] + [(1, RW_WIDTH)] * 4)


def _rw_post_fn(wbot, diff=True):
    seg = _segsum(_head_indicator(RW_WIDTH), diff)
    mb = _mmc(wbot, diff) if wbot is not None else None

    def f(y, r, kf, v, g, ln_w, ln_b, r_k):
        mean = seg(y) * (1.0 / HEAD)
        yc = y - mean
        var = seg(yc * yc) * (1.0 / HEAD)
        yn = yc * lax.rsqrt(var + GN_EPS) * ln_w + ln_b
        bonus = seg(r * kf * r_k) * v
        out = (yn + bonus) * g
        res = mb(out) if mb is not None else out
        return res, out

    return f


def _rw_post_fwd(y, r, kf, v, g, ln_w, ln_b, r_k, L):
    def fn(i, tv, cv):
        out, _ = _rw_post_fn(None, False)(*tv, *cv)
        return (out,)

    return _tok_call("rw_post_fwd", fn, L, TB, [(t,) for t in (y, r, kf, v)] + [(g, RW_WIDTH, 0)], [ln_w, ln_b, r_k],
                     [(RW_WIDTH, F32)])[0]


def _rw_post_bwd(y, r, kf, v, g, dh1, ln_w, ln_b, r_k, wbot, L):
    def fn(i, tv, cv):
        _, vjp, out = jax.vjp(_rw_post_fn(cv[3]), *tv[:5], cv[0], cv[1], cv[2], has_aux=True)
        gr = vjp(tv[5])
        return gr[0], gr[1], gr[2], gr[3], gr[4], out, gr[5], gr[6], gr[7]

    return _tok_call("rw_post_bwd", fn, L, TB, [(t,) for t in (y, r, kf, v)] + [(g, RW_WIDTH, 0), (dh1, D_MODEL, 0)],
                     [ln_w, ln_b, r_k, wbot], [("heads", F32)] + [(RW_WIDTH, F32)] * 4 + [(RW_WIDTH, BF16)], [(1, RW_WIDTH)] * 3)


def _ffn_fn(w1, w3, w2, diff=True):
    m1, m3, m2 = _mmc(w1, diff), _mmc(w3, diff), _mmc(w2, diff)

    def f(h1, norm_ffn, e1, e3):
        hn = _rms(h1, norm_ffn)
        a1 = m1(hn) + e1
        a3 = m3(hn) + e3
        hm = a1 * _sigmoid(a1) * a3
        return h1 + m2(hm), (hn, hm)

    return f


TB_FFN = 256


def _mixffn_fwd(x, s5_out, rw_out, wtop, wbot, norm_ffn, w1, w3, w2, L):
    def fn(i, tv, cv):
        h1 = tv[0] + _dot(_bf(tv[1]), cv[0]) + _dot(_bf(tv[2]), cv[1])
        h2, _ = _ffn_fn(cv[3], cv[4], cv[5], False)(h1, cv[2], 0.0, 0.0)
        return h1, h2

    return _tok_call("mixffn_fwd", fn, L, TB_FFN, [(x, D_MODEL, 0), (s5_out, S5_WIDTH, 0), (rw_out, RW_WIDTH, 0)],
                     [wtop, wbot, norm_ffn, w1, w3, w2], [(D_MODEL, F32), (D_MODEL, F32)])


def _ffn_bwd(h1, dh2, norm_ffn, w1, w3, w2, L):
    def fn(i, tv, cv):
        e0 = jnp.zeros((TB_FFN, FFN_HIDDEN), F32)
        _, vjp, (hn, hm) = jax.vjp(_ffn_fn(cv[1], cv[2], cv[3]), tv[0], cv[0], e0, e0, has_aux=True)
        dh1, dn, d1, d3 = vjp(tv[1])
        return dh1, d1, d3, hm, hn, dn

    return _tok_call("ffn_bwd", fn, L, TB_FFN, [(h1, D_MODEL, 0), (dh2, D_MODEL, 0)], [norm_ffn, w1, w3, w2],
                     [(D_MODEL, F32), (FFN_HIDDEN, BF16), (FFN_HIDDEN, BF16), (FFN_HIDDEN, BF16), (D_MODEL, BF16)],
                     [(1, D_MODEL)])


def _ple_loss_fb(h2, p, target, norm_ple, final_norm, wg, wu, L):
    def fn(i, tv, cv):
        mgate, mup = _mmc(cv[2]), _mmc(cv[3], False)

        def f(h2_, norm_ple_, final_norm_, eg, eu):
            hn = _rms(h2_, norm_ple_)
            gate = _sigmoid(mgate(hn) + eg)
            h3 = h2_ + gate * (mup(tv[1]) + eu)
            out = _rms(h3, final_norm_)
            d = out - tv[2]
            return 0.5 * jnp.sum(jnp.mean(d * d, axis=-1, keepdims=True)), hn

        e0 = jnp.zeros((TB, D_MODEL), F32)
        loss, vjp, hn = jax.vjp(f, tv[0], cv[0], cv[1], e0, e0, has_aux=True)
        dh2, dnp, dfn, deg, deu = vjp(jnp.ones((), F32))
        return dh2, deg, deu, hn, jnp.full((8, 128), loss, F32), dnp, dfn

    return _tok_call("ple_loss_fb", fn, L, TB, [(h2, D_MODEL, 0), (p, PLE_DIM, 0), (target, D_MODEL, 0)],
                     [norm_ple, final_norm, wg, wu], [(D_MODEL, F32), (D_MODEL, BF16), (D_MODEL, BF16), (D_MODEL, BF16)],
                     [(8, 128), (1, D_MODEL), (1, D_MODEL)])


def _inproj_bwd(x, dh1, du, dzs, norm_mix, mu, w_u, w_z, L):
    nb = L // TB

    def fn(i, tv, cv):
        sub = lax.broadcasted_iota(jnp.int32, (TB, 1), 0)
        m = cv[1]
        b = tv[3] * m
        nxt = jnp.where(i == nb - 1, 0.0, tv[4][0:1, :] * m)
        dz = tv[3] * (1.0 - m) + jnp.where(sub == TB - 1, nxt, pltpu.roll(b, TB - 1, 0))
        dub, dzb = _bf(tv[2]), _bf(dz)
        dxn = _dot_nt(dub, cv[2]) + _dot_nt(dzb, cv[3])
        _, vjp = jax.vjp(_rms, tv[0], cv[0])
        dx, dn = vjp(dxn)
        return tv[1] + dx, jnp.concatenate([dub, dzb], axis=1), dn

    return _tok_call("inproj_bwd", fn, L, TB,
                     [(x, D_MODEL, 0), (dh1, D_MODEL, 0), (du, S5_WIDTH, 0), (dzs, SHIFT_COLS, 0), (dzs, SHIFT_COLS, 0, "next")],
                     [norm_mix, mu, w_u, w_z], [(D_MODEL, F32), (IN_COLS, BF16)], [(1, D_MODEL)])


def _eye8(dt):
    return jnp.eye(8, dtype=dt)


def _quarter_b(bb):
    return jnp.einsum("hg,qgcp->qhcgp", _eye8(bb.dtype), bb.reshape(S5_Q, 8, S5_GROUP, S5_STATE)).reshape(S5_Q, S5_QL, S5_QS)


def _unquarter_b(d):
    return jnp.einsum("qhcgp,hg->qgcp", d.reshape(S5_Q, 8, S5_GROUP, 8, S5_STATE), _eye8(d.dtype)).reshape(
        S5_GROUPS, S5_GROUP, S5_STATE)


def _quarter_c(c):
    return jnp.einsum("gh,qgcp->qgphc", _eye8(c.dtype), c.reshape(S5_Q, 8, S5_GROUP, S5_STATE)).reshape(S5_Q, S5_QS, S5_QL)


def _unquarter_c(d):
    return jnp.einsum("qgphc,gh->qgcp", d.reshape(S5_Q, 8, S5_STATE, 8, S5_GROUP), _eye8(d.dtype)).reshape(
        S5_GROUPS, S5_GROUP, S5_STATE)


def _local_step(x, p, target, W, late_weights=None, grads_ready=None, first_dep=None):
    L = x.shape[0]
    r2 = lambda v: v.reshape(1, -1)
    w_in = W["w_in"]
    w2pad = jnp.pad(W["rw_w2"], ((0, 64), (0, 0)))
    a2pad = jnp.pad(W["rw_a2"], ((64, 0), (0, 0)))
    mu = r2(W["rw_shift_mu"])
    rw_vec = [r2(W[n]) for n in ("rw_w0", "rw_a0", "rw_k_k", "rw_k_a")]
    ln_w, ln_b, r_k = r2(W["rw_ln_w"]), r2(W["rw_ln_b"]), r2(W["rw_r_k"])

    lam_re, lam_im = W["s5_lam_re"], W["s5_lam_im"]
    log_step = W["s5_log_step"].reshape(S5_GROUPS, 1)
    bt_re, bt_im = W["s5_b_re"].transpose(0, 2, 1), W["s5_b_im"].transpose(0, 2, 1)
    lb_re, lb_im, bb_re, bb_im = _s5_param_fwd(lam_re, lam_im, log_step, bt_re, bt_im)
    bq_re, bq_im = _quarter_b(bb_re).astype(BF16), _quarter_b(bb_im).astype(BF16)
    cq_re, cq_im = _quarter_c(W["s5_c_re"]).astype(BF16), _quarter_c(W["s5_c_im"]).astype(BF16)
    lbar = jnp.concatenate([lb_re.reshape(1, -1), lb_im.reshape(1, -1), jnp.zeros((6, S5_LANES), F32)], axis=0)
    dskip = r2(W["s5_d"])
    glu_b = r2(W["s5_glu_b"])
    norm_mix, norm_ffn, norm_ple, final_norm = (r2(W[n]) for n in ("norm_mix", "norm_ffn", "norm_ple", "final_norm"))

    proj, xn = _inproj_fwd(x, norm_mix, w_in, L, () if first_dep is None else (first_dep,))
    y_s5, ck5 = _s5_scan_fwd(proj, bq_re, bq_im, cq_re, cq_im, lbar, dskip, L, TB)
    s5_out = _s5_post_fwd(y_s5, W["s5_glu_w"], glu_b, L)
    r, wd, kf, v, a_s, b_s, g = _rw_pre_fwd(proj, mu, *rw_vec, w2pad, a2pad, W["rw_g2"], L)
    scan_in = (r, wd, kf, v, a_s, b_s)
    y_wkv, ckw = _wkv_fwd(*scan_in, L)
    rw_out = _rw_post_fwd(y_wkv, r, kf, v, g, ln_w, ln_b, r_k, L)
    if late_weights is not None:
        W = dict(W, **late_weights(rw_out))
    wtop, wbot = W["w_out"][:S5_WIDTH], W["w_out"][S5_WIDTH:]
    h1, h2 = _mixffn_fwd(x, s5_out, rw_out, wtop, wbot, norm_ffn, W["ffn_w1"], W["ffn_w3"], W["ffn_w2"], L)

    G = {}
    dh2, deg, deu, hn_ple, loss_acc, G["norm_ple"], G["final_norm"] = _ple_loss_fb(
        h2, p, target, norm_ple, final_norm, W["ple_gate_w"], W["ple_up_w"], L)
    dh1, da1, da3, hm, hn_ffn, G["norm_ffn"] = _ffn_bwd(h1, dh2, norm_ffn, W["ffn_w1"], W["ffn_w3"], W["ffn_w2"], L)
    dy_s5, z_bf, dgp, s5o_bf, G["s5_glu_b"] = _s5_post_bwd(y_s5, dh1, W["s5_glu_w"], glu_b, wtop, L)
    dy_wkv, dr2, dk2, dv2, dg, rwo_bf, G["rw_ln_w"], G["rw_ln_b"], G["rw_r_k"] = _rw_post_bwd(
        y_wkv, r, kf, v, g, dh1, ln_w, ln_b, r_k, wbot, L)
    G["ffn_w1"] = _mm_tn("dw_ffn_w1", hn_ffn, da1)
    G["ffn_w3"] = _mm_tn("dw_ffn_w3", hn_ffn, da3)
    G["ffn_w2"] = _mm_tn("dw_ffn_w2", hm, dh2)
    G["ple_gate_w"] = _mm_tn("dw_ple_gate", hn_ple, deg)
    G["w_out"] = jnp.concatenate([_mm_tn("dw_out_top", s5o_bf, dh1), _mm_tn("dw_out_bot", rwo_bf, dh1)], axis=0)
    dep = grads_ready(G) if grads_ready is not None else None
    G["ple_up_w"] = _mm_tn("dw_ple_up", p, deu)
    G["s5_glu_w"] = _mm_tn("dw_s5_glu", z_bf, dgp)
    dr1, dwd, dk1, dv1, da_s, db_s = _wkv_bwd(*scan_in, dy_wkv, ckw, L, () if dep is None else (dep,))
    (dzs, wl_t, zwa, sg, dwlin, dalin, G["rw_shift_mu"], G["rw_w0"], G["rw_a0"], G["rw_k_k"], G["rw_k_a"]) = _rw_pre_bwd(
        proj, (dr1, dr2, dwd, dk1, dk2, dv1, dv2, da_s, db_s, dg), mu, *rw_vec, w2pad, a2pad, W["rw_g2"], L)
    G["rw_w2"] = _mm_tn("dw_rw_w2", wl_t, dwlin)[:64]
    G["rw_a2"] = _mm_tn("dw_rw_a2", zwa, dalin)[64:]
    G["rw_g2"] = _mm_tn("dw_rw_g2", sg, dg)
    du, dbq_re, dbq_im, dcq_re, dcq_im, dlbar, G["s5_d"] = _s5_scan_bwd(
        proj, dy_s5, ck5, bq_re, bq_im, cq_re, cq_im, lbar, dskip, L, TB)
    G["s5_c_re"], G["s5_c_im"] = _unquarter_c(dcq_re), _unquarter_c(dcq_im)
    d_lam_re, d_lam_im, d_ls, d_bt_re, d_bt_im = _s5_param_bwd(
        lam_re, lam_im, log_step, bt_re, bt_im, dlbar[0].reshape(S5_GROUPS, S5_STATE), dlbar[1].reshape(S5_GROUPS, S5_STATE),
        _unquarter_b(dbq_re), _unquarter_b(dbq_im))
    G["s5_lam_re"], G["s5_lam_im"], G["s5_log_step"] = d_lam_re, d_lam_im, d_ls.reshape(S5_GROUPS)
    G["s5_b_re"], G["s5_b_im"] = d_bt_re.transpose(0, 2, 1), d_bt_im.transpose(0, 2, 1)
    dx, dproj, G["norm_mix"] = _inproj_bwd(x, dh1, du, dzs, norm_mix, mu, w_in[:, :S5_WIDTH], w_in[:, S5_WIDTH:], L)
    G["w_in"] = _mm_tn("dw_in", xn, dproj)
    return loss_acc[0, 0], dx, G


MESH_AXES = ("x", "y", "c")
_ANY = pl.BlockSpec(memory_space=pl.ANY)


def _all_gather(name, shards):
    nt = len(shards)

    def body(*refs):
        x_refs, out_refs = refs[:nt], refs[nt:2 * nt]
        send_sems, recv_sems, local_sems = refs[2 * nt:]
        x, y, c = lax.axis_index("x"), lax.axis_index("y"), lax.axis_index("c")
        me, sibling = (x, y, c), (x, y, 1 - c)
        chips = [(1 - x, y), (x, 1 - y), (1 - x, 1 - y)]

        def rows(t, px, py, pc):
            m_per = shards[t].shape[0]
            return out_refs[t].at[pl.ds((4 * px + 2 * py + pc) * m_per, m_per), :]

        def copy(t, k, block, to, src=None):
            return pltpu.make_async_remote_copy(
                src_ref=rows(t, *block) if src is None else src, dst_ref=rows(t, *block),
                send_sem=send_sems.at[7 * t + k], recv_sem=recv_sems.at[7 * t + k],
                device_id=to, device_id_type=pl.DeviceIdType.MESH)

        mine = [pltpu.make_async_copy(x_refs[t], rows(t, *me), local_sems.at[t]) for t in range(nt)]
        for cp in mine:
            cp.start()
        first = []
        for t in range(nt):
            first.append(copy(t, 0, me, sibling, src=x_refs[t]))
            first += [copy(t, 1 + j, me, (*chip, c), src=x_refs[t]) for j, chip in enumerate(chips)]
        for cp in first:
            cp.start()
        passed = []
        for t in range(nt):
            for j, chip in enumerate(chips):
                copy(t, 1 + j, (*chip, c), me).wait_recv()
                fwd = copy(t, 4 + j, (*chip, c), sibling)
                fwd.start()
                passed.append(fwd)
        for t in range(nt):
            copy(t, 0, sibling, me).wait_recv()
            for j, chip in enumerate(chips):
                copy(t, 4 + j, (*chip, 1 - c), me).wait_recv()
        for cp in first + passed:
            cp.wait_send()
        for cp in mine:
            cp.wait()

    return _pcall(body, name=name,
                  out_shape=[jax.ShapeDtypeStruct((N_DEV * a.shape[0], a.shape[1]), a.dtype) for a in shards],
                  in_specs=[_ANY] * nt, out_specs=[_ANY] * nt,
                  scratch_shapes=[pltpu.SemaphoreType.DMA((7 * nt,)), pltpu.SemaphoreType.DMA((7 * nt,)),
                                  pltpu.SemaphoreType.DMA((nt,))])(*shards)


_HBM = pl.BlockSpec(memory_space=pltpu.HBM)
_SEM = pl.BlockSpec(memory_space=pltpu.SEMAPHORE)
_EFFECT = pltpu.SideEffectType.DATAFLOW_SIDE_EFFECTING


def _peer_of(k):
    x, y, c = lax.axis_index("x"), lax.axis_index("y"), lax.axis_index("c")
    px, py, pc = x ^ ((k >> 2) & 1), y ^ ((k >> 1) & 1), c ^ (k & 1)
    return (px, py, pc), 4 * px + 2 * py + pc, 4 * x + 2 * y + c


def _direct_copy(t, k, src_refs, land_refs, send_sems, recv_sems, rows_of, gather):
    dev, peer, me = _peer_of(k)
    m = rows_of[t]
    src = src_refs[t] if gather else src_refs[t].at[pl.ds(peer * m, m), :]
    return pltpu.make_async_remote_copy(
        src_ref=src, dst_ref=land_refs[t].at[pl.ds(me * m, m), :],
        send_sem=send_sems.at[7 * t + k - 1], recv_sem=recv_sems.at[7 * t + k - 1],
        device_id=dev, device_id_type=pl.DeviceIdType.MESH)


def _direct_landing(t, k, src_refs, land_refs, send_sems, recv_sems, rows_of, gather):
    dev, peer, me = _peer_of(k)
    m = rows_of[t]
    src = src_refs[t] if gather else src_refs[t].at[pl.ds(me * m, m), :]
    return pltpu.make_async_remote_copy(
        src_ref=src, dst_ref=land_refs[t].at[pl.ds(peer * m, m), :],
        send_sem=send_sems.at[7 * t + k - 1], recv_sem=recv_sems.at[7 * t + k - 1],
        device_id=dev, device_id_type=pl.DeviceIdType.MESH)


def _direct_start(name, srcs, gather, dep=None):
    nt = len(srcs)
    rows_of = [a.shape[0] if gather else a.shape[0] // N_DEV for a in srcs]
    lands = [pltpu.with_memory_space_constraint(lax.empty((N_DEV * m, a.shape[1]), a.dtype), pltpu.HBM)
             for a, m in zip(srcs, rows_of)]

    n_dep = 0 if dep is None else 1

    def body(*refs):
        src_refs, land_refs = refs[:nt], refs[nt:2 * nt]
        send_sems, recv_sems = refs[2 * nt + n_dep], refs[2 * nt + n_dep + 1]
        token = refs[-1]
        for t in range(nt):
            for k in range(1, N_DEV):
                _direct_copy(t, k, src_refs, land_refs, send_sems, recv_sems, rows_of, gather).start()
        token[...] = jnp.zeros(token.shape, F32)

    out = _pcall(
        body, name=name,
        out_shape=(pltpu.SemaphoreType.DMA((7 * nt,)), pltpu.SemaphoreType.DMA((7 * nt,)),
                   *[pltpu.HBM(a.shape, a.dtype) for a in srcs], *[pltpu.HBM(a.shape, a.dtype) for a in lands],
                   jax.ShapeDtypeStruct((8, 128), F32)),
        in_specs=(_HBM,) * (2 * nt) + (pl.BlockSpec(memory_space=pl.ANY),) * n_dep,
        out_specs=(_SEM, _SEM) + (_HBM,) * (2 * nt) + (pl.BlockSpec(memory_space=pltpu.VMEM),),
        input_output_aliases={i: 2 + i for i in range(2 * nt)},
        compiler_params=pltpu.CompilerParams(has_side_effects=_EFFECT),
    )(*[pltpu.with_memory_space_constraint(a, pltpu.HBM) for a in srcs], *lands, *(() if dep is None else (dep,)))
    return (out[0], out[1], list(out[2:2 + nt]), list(out[2 + nt:2 + 2 * nt]), rows_of, gather), out[-1]


def _direct_wait(name, handle, after):
    send_sems, recv_sems, srcs, lands, rows_of, gather = handle
    nt = len(srcs)

    def body(*refs):
        src_refs, land_refs = refs[:nt], refs[nt:2 * nt]
        s_sems, r_sems = refs[2 * nt], refs[2 * nt + 1]
        for t in range(nt):
            for k in range(1, N_DEV):
                _direct_copy(t, k, src_refs, land_refs, s_sems, r_sems, rows_of, gather).wait_send()
                _direct_landing(t, k, src_refs, land_refs, s_sems, r_sems, rows_of, gather).wait_recv()

    out = _pcall(
        body, name=name,
        out_shape=tuple(pltpu.HBM(a.shape, a.dtype) for a in srcs) + tuple(pltpu.HBM(a.shape, a.dtype) for a in lands),
        in_specs=(_HBM,) * (2 * nt) + (_SEM, _SEM, pl.BlockSpec(memory_space=pl.ANY)),
        out_specs=(_HBM,) * (2 * nt),
        input_output_aliases={i: i for i in range(2 * nt)},
        compiler_params=pltpu.CompilerParams(has_side_effects=_EFFECT),
    )(*srcs, *lands, send_sems, recv_sems, after)
    return list(out[:nt]), list(out[nt:])


def _adamw_sharded(name, own, parts, w, m, v, rb):
    R, N = own.shape

    def body(o_ref, p_ref, w_ref, m_ref, v_ref, g_ref, d_ref, nm_ref, nv_ref):
        me = 4 * lax.axis_index("x") + 2 * lax.axis_index("y") + lax.axis_index("c")
        g = o_ref[...]
        for k in range(1, N_DEV):
            g = g + p_ref[me ^ k].astype(F32)
        nm = ADAM_B1 * m_ref[...] + (1.0 - ADAM_B1) * g
        nv = ADAM_B2 * v_ref[...] + (1.0 - ADAM_B2) * (g * g)
        m_hat = nm / (1.0 - ADAM_B1 ** ADAM_STEP)
        v_hat = nv / (1.0 - ADAM_B2 ** ADAM_STEP)
        g_ref[...] = g
        d_ref[...] = -ADAM_LR * (m_hat / (jnp.sqrt(v_hat) + ADAM_EPS) + ADAM_WD * w_ref[...])
        nm_ref[...] = nm
        nv_ref[...] = nv

    blk = pl.BlockSpec((rb, N), lambda i: (i, 0))
    sh = jax.ShapeDtypeStruct((R, N), F32)
    return _pcall(body, name=name, grid=(R // rb,),
                  in_specs=[blk, pl.BlockSpec((N_DEV, rb, N), lambda i: (0, i, 0)), blk, blk, blk],
                  out_specs=[blk] * 4, out_shape=[sh] * 4, compiler_params=_cparams(1))(own, parts, w, m, v)


def _adamw(name, parts, w, m, v, rb):
    _, R, N = parts.shape

    def body(p_ref, w_ref, m_ref, v_ref, g_ref, d_ref, nm_ref, nv_ref):
        g = p_ref[0]
        for s in range(1, N_DEV):
            g = g + p_ref[s]
        nm = ADAM_B1 * m_ref[...] + (1.0 - ADAM_B1) * g
        nv = ADAM_B2 * v_ref[...] + (1.0 - ADAM_B2) * (g * g)
        m_hat = nm / (1.0 - ADAM_B1 ** ADAM_STEP)
        v_hat = nv / (1.0 - ADAM_B2 ** ADAM_STEP)
        g_ref[...] = g
        d_ref[...] = -ADAM_LR * (m_hat / (jnp.sqrt(v_hat) + ADAM_EPS) + ADAM_WD * w_ref[...])
        nm_ref[...] = nm
        nv_ref[...] = nv

    blk = pl.BlockSpec((rb, N), lambda i: (i, 0))
    sh = jax.ShapeDtypeStruct((R, N), F32)
    return _pcall(body, name=name, grid=(R // rb,), in_specs=[pl.BlockSpec((N_DEV, rb, N), lambda i: (0, i, 0)), blk, blk, blk],
                  out_specs=[blk] * 4, out_shape=[sh] * 4, compiler_params=_cparams(1))(parts, w, m, v)


EARLY = (("w_in", True),)
LATE = (("ffn_w1", True), ("ffn_w3", True), ("w_out", False), ("ffn_w2", False), ("ple_gate_w", False))
MISC = (("s5_glu_w", False), ("rw_w2", True), ("rw_a2", True), ("rw_g2", True), ("ple_up_w", True))
SHARDED_NAMES = tuple(n for n, _ in EARLY + LATE + MISC)
PACK_COLS = 1024
SMALL_ROWS = 144
WEIGHT_NAMES = ("norm_mix", "w_in", "s5_lam_re", "s5_lam_im", "s5_log_step", "s5_b_re", "s5_b_im", "s5_c_re", "s5_c_im", "s5_d",
                "s5_glu_w", "s5_glu_b", "rw_shift_mu", "rw_w0", "rw_w2", "rw_a0", "rw_a2", "rw_g2", "rw_k_k", "rw_k_a", "rw_r_k",
                "rw_ln_w", "rw_ln_b", "w_out", "norm_ffn", "ffn_w1", "ffn_w3", "ffn_w2", "norm_ple", "ple_gate_w", "ple_up_w",
                "final_norm")
SMALL_NAMES = tuple(n for n in WEIGHT_NAMES if n not in SHARDED_NAMES)
ARG_NAMES = ("x", "p") + WEIGHT_NAMES + ("loss_target",) + tuple("m_" + n for n in WEIGHT_NAMES) + tuple("v_" + n for n in WEIGHT_NAMES)


def _travel(a, tr):
    return a.T if tr else a


def _pack_misc(blocks):
    lead = blocks[0].shape[:-2]
    return jnp.concatenate([b.reshape(lead + (-1, PACK_COLS)) for b in blocks], axis=len(lead))


def _unpack_misc(packed, shapes):
    lead = packed.shape[:-2]
    out, off = [], 0
    for r, c in shapes:
        n = r * c // PACK_COLS
        out.append(lax.slice_in_dim(packed, off, off + n, axis=len(lead)).reshape(lead + (r, c)))
        off += n
    return out


def _pack_small(arrs):
    flat = jnp.concatenate([a.reshape(-1).astype(F32) for a in arrs])
    return jnp.pad(flat, (0, SMALL_ROWS * PACK_COLS - flat.shape[0])).reshape(SMALL_ROWS, PACK_COLS)


def _kernel_impl(ins):
    x, p, target = ins["x"][0], ins["p"][0, 0], ins["loss_target"][0]
    me = 4 * lax.axis_index("x") + 2 * lax.axis_index("y") + lax.axis_index("c")
    small = {n: (ins[n] if n == "final_norm" else ins[n][0]) for n in SMALL_NAMES}
    trav = lambda pre, n, tr: _travel(ins[pre + n][0], tr)
    misc_shapes = [trav("", n, tr).shape for n, tr in MISC]

    early = _all_gather("ag_early", [trav("", n, tr).astype(BF16) for n, tr in EARLY]
                        + [_pack_misc([trav("", n, tr).astype(BF16) for n, tr in MISC])])
    late_handle, late_token = _direct_start("ag_late_start", [trav("", n, tr).astype(BF16) for n, tr in LATE], True, early[-1])
    W = dict(small)
    for (n, tr), g in zip(EARLY, early):
        W[n] = _travel(g, tr)
    for (n, tr), g in zip(MISC, _unpack_misc(early[-1].reshape(N_DEV, -1, PACK_COLS), misc_shapes)):
        W[n] = _travel(g.reshape(-1, g.shape[-1]), tr)

    def late_weights(after):
        shards, lands = _direct_wait("ag_late_wait", late_handle, after)
        full = [lax.dynamic_update_slice_in_dim(ld, sh, me * sh.shape[0], axis=0) for ld, sh in zip(lands, shards)]
        return {n: _travel(g, tr) for (n, tr), g in zip(LATE, full)}

    gt = lambda G, n, tr: _travel(G[n], tr)
    started = {}

    def grads_ready(G):
        started["h"], token = _direct_start("grad_late_start", [gt(G, n, tr) for n, tr in LATE], False)
        return token

    loss_part, dx, G = _local_step(x, p, target, W, late_weights, grads_ready, late_token)

    misc_g = _pack_misc([gt(G, n, tr).reshape((N_DEV,) + shp) for (n, tr), shp in zip(MISC, misc_shapes)])
    early_full = [gt(G, n, tr) for n, tr in EARLY] + [misc_g.reshape(-1, PACK_COLS)]
    early_handle, _ = _direct_start("grad_early_start", [a.astype(BF16) for a in early_full], False)
    small_own = _pack_small([G[n] for n in SMALL_NAMES])
    small_handle, small_token = _direct_start("grad_small_start", [small_own], True)
    late_src, late_land = _direct_wait("grad_late_wait", started["h"], small_token)

    outs = {}

    def emit(names_shapes, res):
        for tag, val in zip(("grad_", "delta_", "new_m_", "new_v_"), res):
            for n, v in names_shapes(val):
                outs[tag + n] = v

    def sharded_update(n, tr, src, land):
        rows = src.shape[0] // N_DEV
        own = lax.dynamic_slice_in_dim(src, me * rows, rows, axis=0)
        res = _adamw_sharded("adamw_" + n, own, land.reshape(N_DEV, rows, land.shape[1]),
                             trav("", n, tr), trav("m_", n, tr), trav("v_", n, tr), _pick_rows(rows))
        emit(lambda val: [(n, _travel(val, tr).reshape(ins[n].shape))], res)
        return res[0]

    for (n, tr), src, land in zip(LATE, late_src, late_land):
        last = sharded_update(n, tr, src, land)
    _, early_land = _direct_wait("grad_early_wait", early_handle, last)
    for (n, tr), src, land in zip(EARLY, early_full[:-1], early_land[:-1]):
        sharded_update(n, tr, src, land)
    pm = lambda pre: _pack_misc([trav(pre, n, tr) for n, tr in MISC])
    rows = early_full[-1].shape[0] // N_DEV
    res = _adamw_sharded("adamw_misc", lax.dynamic_slice_in_dim(early_full[-1], me * rows, rows, axis=0),
                         early_land[-1].reshape(N_DEV, rows, PACK_COLS), pm(""), pm("m_"), pm("v_"), rows)
    emit(lambda val: [(n, _travel(b, tr).reshape(ins[n].shape)) for (n, tr), b in zip(MISC, _unpack_misc(val, misc_shapes))], res)
    ps = lambda pre: _pack_small([ins[pre + n] for n in SMALL_NAMES])
    small_src, small_land = _direct_wait("grad_small_wait", small_handle, res[0])
    gsm = lax.dynamic_update_slice_in_dim(small_land[0], small_src[0], me * SMALL_ROWS, axis=0)
    res = _adamw("adamw_replicated", gsm.reshape(N_DEV, SMALL_ROWS, PACK_COLS), ps(""), ps("m_"), ps("v_"), SMALL_ROWS)

    def split_small(val):
        flat, off, o = val.reshape(-1), 0, []
        for n in SMALL_NAMES:
            o.append((n, flat[off:off + ins[n].size].reshape(ins[n].shape)))
            off += ins[n].size
        return o

    emit(split_small, res)
    loss = lax.psum(loss_part, MESH_AXES)
    res = [loss, dx[None]]
    for tag in ("grad_", "delta_", "new_m_", "new_v_"):
        res += [outs[tag + n] for n in WEIGHT_NAMES]
    return tuple(res)


def _pick_rows(r):
    best = 8
    for b in range(8, 257, 8):
        if r % b == 0:
            best = b
    return best


def kernel(x, p, norm_mix, w_in, s5_lam_re, s5_lam_im, s5_log_step, s5_b_re, s5_b_im, s5_c_re, s5_c_im, s5_d, s5_glu_w, s5_glu_b, rw_shift_mu, rw_w0, rw_w2, rw_a0, rw_a2, rw_g2, rw_k_k, rw_k_a, rw_r_k, rw_ln_w, rw_ln_b, w_out, norm_ffn, ffn_w1, ffn_w3, ffn_w2, norm_ple, ple_gate_w, ple_up_w, final_norm, loss_target, m_norm_mix, m_w_in, m_s5_lam_re, m_s5_lam_im, m_s5_log_step, m_s5_b_re, m_s5_b_im, m_s5_c_re, m_s5_c_im, m_s5_d, m_s5_glu_w, m_s5_glu_b, m_rw_shift_mu, m_rw_w0, m_rw_w2, m_rw_a0, m_rw_a2, m_rw_g2, m_rw_k_k, m_rw_k_a, m_rw_r_k, m_rw_ln_w, m_rw_ln_b, m_w_out, m_norm_ffn, m_ffn_w1, m_ffn_w3, m_ffn_w2, m_norm_ple, m_ple_gate_w, m_ple_up_w, m_final_norm, v_norm_mix, v_w_in, v_s5_lam_re, v_s5_lam_im, v_s5_log_step, v_s5_b_re, v_s5_b_im, v_s5_c_re, v_s5_c_im, v_s5_d, v_s5_glu_w, v_s5_glu_b, v_rw_shift_mu, v_rw_w0, v_rw_w2, v_rw_a0, v_rw_a2, v_rw_g2, v_rw_k_k, v_rw_k_a, v_rw_r_k, v_rw_ln_w, v_rw_ln_b, v_w_out, v_norm_ffn, v_ffn_w1, v_ffn_w3, v_ffn_w2, v_norm_ple, v_ple_gate_w, v_ple_up_w, v_final_norm):
    return _kernel_impl(dict(zip(ARG_NAMES, (x, p, norm_mix, w_in, s5_lam_re, s5_lam_im, s5_log_step, s5_b_re, s5_b_im, s5_c_re, s5_c_im, s5_d, s5_glu_w, s5_glu_b, rw_shift_mu, rw_w0, rw_w2, rw_a0, rw_a2, rw_g2, rw_k_k, rw_k_a, rw_r_k, rw_ln_w, rw_ln_b, w_out, norm_ffn, ffn_w1, ffn_w3, ffn_w2, norm_ple, ple_gate_w, ple_up_w, final_norm, loss_target, m_norm_mix, m_w_in, m_s5_lam_re, m_s5_lam_im, m_s5_log_step, m_s5_b_re, m_s5_b_im, m_s5_c_re, m_s5_c_im, m_s5_d, m_s5_glu_w, m_s5_glu_b, m_rw_shift_mu, m_rw_w0, m_rw_w2, m_rw_a0, m_rw_a2, m_rw_g2, m_rw_k_k, m_rw_k_a, m_rw_r_k, m_rw_ln_w, m_rw_ln_b, m_w_out, m_norm_ffn, m_ffn_w1, m_ffn_w3, m_ffn_w2, m_norm_ple, m_ple_gate_w, m_ple_up_w, m_final_norm, v_norm_mix, v_w_in, v_s5_lam_re, v_s5_lam_im, v_s5_log_step, v_s5_b_re, v_s5_b_im, v_s5_c_re, v_s5_c_im, v_s5_d, v_s5_glu_w, v_s5_glu_b, v_rw_shift_mu, v_rw_w0, v_rw_w2, v_rw_a0, v_rw_a2, v_rw_g2, v_rw_k_k, v_rw_k_a, v_rw_r_k, v_rw_ln_w, v_rw_ln_b, v_w_out, v_norm_ffn, v_ffn_w1, v_ffn_w3, v_ffn_w2, v_norm_ple, v_ple_gate_w, v_ple_up_w, v_final_norm))))
```

```python
import functools

import jax
import jax.numpy as jnp
from jax import lax
from jax.experimental import pallas as pl
from jax.experimental.pallas import tpu as pltpu

F32 = jnp.float32
BF16 = jnp.bfloat16

D_MODEL = 1024
S5_WIDTH = 512
RW_WIDTH = 512
S5_GROUP = 16
S5_GROUPS = 32
S5_STATE = 64
S5_LANES = S5_GROUPS * S5_STATE
HEAD = 64
SHIFT_COLS = 1792
IN_COLS = 2304
FFN_HIDDEN = 2816
PLE_DIM = 256
RMS_EPS = 1e-6
GN_EPS = 64e-5
L2_EPS = 1e-12
CHUNK = 64
N_DEV = 8

ADAM_LR = 0.001
ADAM_B1 = 0.9
ADAM_B2 = 0.999
ADAM_EPS = 1e-08
ADAM_WD = 0.01
ADAM_STEP = 10

VMEM_LIMIT = 56 * 1024 * 1024


def _pcall(body, **kw):
    return pl.pallas_call(body, **kw)


def _cparams(n_grid):
    return pltpu.CompilerParams(dimension_semantics=("arbitrary",) * n_grid, vmem_limit_bytes=VMEM_LIMIT)


def _dot(a, b):
    return jnp.dot(a, b, preferred_element_type=F32)


def _dot_nt(a, b):
    return lax.dot_general(a, b, (((1,), (1,)), ((), ())), preferred_element_type=F32)


def _dot_tn(a, b):
    return lax.dot_general(a, b, (((0,), (0,)), ((), ())), preferred_element_type=F32)


def _mmc(w, diff=True, tr=False):
    fw, bw = (_dot_nt, _dot) if tr else (_dot, _dot_nt)
    if not diff:
        return lambda x: fw(x.astype(BF16), w)

    @jax.custom_vjp
    def f(x):
        return fw(x.astype(BF16), w)

    def fwd(x):
        return fw(x.astype(BF16), w), None

    def bwd(_, dy):
        return (bw(dy.astype(BF16), w),)

    f.defvjp(fwd, bwd)
    return f


def _split_dot(x, m, n_split):
    acc = None
    rem = x
    for s in range(n_split):
        part = rem.astype(BF16)
        t = _dot(part, m)
        acc = t if acc is None else acc + t
        if s + 1 < n_split:
            rem = rem - part.astype(F32)
    return acc


def _segsum(m, diff=True):
    if not diff:
        return lambda x: _split_dot(x, m, 2)

    @jax.custom_vjp
    def f(x):
        return _split_dot(x, m, 2)

    def fwd(x):
        return _split_dot(x, m, 2), None

    def bwd(_, dy):
        return (_split_dot(dy, m, 2),)

    f.defvjp(fwd, bwd)
    return f


def _head_indicator(n):
    r = lax.broadcasted_iota(jnp.int32, (n, n), 0) // HEAD
    c = lax.broadcasted_iota(jnp.int32, (n, n), 1) // HEAD
    return (r == c).astype(BF16)


def _rms(x, g):
    return x * lax.rsqrt(jnp.mean(x * x, axis=-1, keepdims=True) + RMS_EPS) * g


def _softplus(x):
    return jnp.maximum(x, 0.0) + jnp.log(1.0 + jnp.exp(-jnp.abs(x)))


def _sigmoid(x):
    return 1.0 / (1.0 + jnp.exp(-x))


def _gelu(x):
    return 0.5 * x * (1.0 + jnp.tanh(0.7978845608028654 * (x + 0.044715 * (x * x * x))))


def _tok_call(name, fn, L, TB, tok_in, const_in, tok_out, acc_out=(), deps=()):
    nb = L // TB
    g8 = TB // 8
    in_specs, args = [], []
    for spec in tok_in:
        if len(spec) == 1:
            arr = spec[0]
            in_specs.append(pl.BlockSpec((arr.shape[0], TB, HEAD), lambda i: (0, i, 0)))
            args.append(arr)
            continue
        arr, width, cb = spec[:3]
        mode = spec[3] if len(spec) > 3 else None
        if mode is None:
            in_specs.append(pl.BlockSpec((TB, width), lambda i, cb=cb: (i, cb)))
        elif mode == "prev":
            in_specs.append(pl.BlockSpec((8, width), lambda i, cb=cb: (jnp.maximum(i * g8 - 1, 0), cb)))
        else:
            in_specs.append(pl.BlockSpec((8, width), lambda i, cb=cb: (jnp.minimum((i + 1) * g8, L // 8 - 1), cb)))
        args.append(arr)
    for c in const_in:
        in_specs.append(pl.BlockSpec(c.shape, lambda i, nd=c.ndim: (0,) * nd, pipeline_mode=pl.Buffered(1)))
        args.append(c)
    for d in deps:
        in_specs.append(pl.BlockSpec(d.shape, lambda i, nd=d.ndim: (0,) * nd))
        args.append(d)
    out_shape, out_specs = [], []
    for width, dt in tok_out:
        if width == "heads":
            out_shape.append(jax.ShapeDtypeStruct((N_HEAD, L, HEAD), dt))
            out_specs.append(pl.BlockSpec((N_HEAD, TB, HEAD), lambda i: (0, i, 0)))
            continue
        out_shape.append(jax.ShapeDtypeStruct((L, width), dt))
        out_specs.append(pl.BlockSpec((TB, width), lambda i: (i, 0)))
    for shp in acc_out:
        out_shape.append(jax.ShapeDtypeStruct(shp, F32))
        out_specs.append(pl.BlockSpec(shp, lambda i, nd=len(shp): (0,) * nd))
    n_tok, n_const, n_to = len(tok_in), len(const_in), len(tok_out)

    def body(*refs):
        i = pl.program_id(0)
        tv = [r[...] if len(r.shape) == 2 else jnp.concatenate([r[h] for h in range(r.shape[0])], axis=1)
              for r in refs[:n_tok]]
        cv = [r[...] for r in refs[n_tok:n_tok + n_const]]
        orefs = refs[n_tok + n_const + len(deps):]
        outs = fn(i, tv, cv)
        for r, v in zip(orefs[:n_to], outs[:n_to]):
            if len(r.shape) == 3:
                for h in range(r.shape[0]):
                    r[h] = v[:, h * HEAD:(h + 1) * HEAD].astype(r.dtype)
            else:
                r[...] = v.astype(r.dtype)
        for r, v in zip(orefs[n_to:], outs[n_to:]):
            @pl.when(i == 0)
            def _(r=r):
                r[...] = jnp.zeros(r.shape, r.dtype)

            r[...] += v

    res = _pcall(body, name=name, grid=(nb,), in_specs=in_specs, out_specs=out_specs, out_shape=out_shape,
                 compiler_params=_cparams(1))(*args)
    return res


def _pick_block(n, cap):
    best = None
    for b in range(128, min(n, cap) + 1, 128):
        if n % b == 0:
            best = b
    return best if best is not None else n


def _mm_tn(name, a, b):
    T, M = a.shape
    N = b.shape[1]
    bm, bn, bt = _pick_block(M, 1536), _pick_block(N, 1536), _pick_block(T, 512)

    def body(a_ref, b_ref, o_ref):
        t = pl.program_id(2)

        @pl.when(t == 0)
        def _():
            o_ref[...] = jnp.zeros(o_ref.shape, F32)

        o_ref[...] += _dot_tn(a_ref[...].astype(BF16), b_ref[...].astype(BF16))

    return _pcall(body, name=name, grid=(M // bm, N // bn, T // bt),
                  in_specs=[pl.BlockSpec((bt, bm), lambda m, n, t: (t, m)), pl.BlockSpec((bt, bn), lambda m, n, t: (t, n))],
                  out_specs=pl.BlockSpec((bm, bn), lambda m, n, t: (m, n)),
                  out_shape=jax.ShapeDtypeStruct((M, N), F32), compiler_params=_cparams(3))(a, b)


def _s5_param_fn(lam_re, lam_im, log_step, bt_re, bt_im):
    dt = jnp.exp(log_step)
    e = jnp.exp(lam_re * dt)
    lb_re = e * jnp.cos(lam_im * dt)
    lb_im = e * jnp.sin(lam_im * dt)
    den = lam_re * lam_re + lam_im * lam_im
    nr, ni = lb_re - 1.0, lb_im
    co_re = (nr * lam_re + ni * lam_im) / den
    co_im = (ni * lam_re - nr * lam_im) / den
    cr, ci = co_re[:, None, :], co_im[:, None, :]
    return lb_re, lb_im, cr * bt_re - ci * bt_im, cr * bt_im + ci * bt_re


def _s5_param_fwd(lam_re, lam_im, log_step, bt_re, bt_im):
    def body(a, b, c, d, e, o1, o2, o3, o4):
        r = _s5_param_fn(a[...], b[...], c[...], d[...], e[...])
        o1[...], o2[...], o3[...], o4[...] = r

    sh = jax.ShapeDtypeStruct
    return _pcall(body, name="s5_param_fwd",
                  out_shape=[sh(lam_re.shape, F32), sh(lam_re.shape, F32), sh(bt_re.shape, F32), sh(bt_re.shape, F32)])(
        lam_re, lam_im, log_step, bt_re, bt_im)


def _s5_param_bwd(lam_re, lam_im, log_step, bt_re, bt_im, d_lb_re, d_lb_im, d_bb_re, d_bb_im):
    def body(a, b, c, d, e, g1, g2, g3, g4, o1, o2, o3, o4, o5):
        _, vjp = jax.vjp(_s5_param_fn, a[...], b[...], c[...], d[...], e[...])
        r = vjp((g1[...], g2[...], g3[...], g4[...]))
        o1[...], o2[...], o3[...], o4[...], o5[...] = r

    sh = jax.ShapeDtypeStruct
    return _pcall(body, name="s5_param_bwd",
                  out_shape=[sh(lam_re.shape, F32), sh(lam_re.shape, F32), sh(log_step.shape, F32),
                             sh(bt_re.shape, F32), sh(bt_re.shape, F32)])(
        lam_re, lam_im, log_step, bt_re, bt_im, d_lb_re, d_lb_im, d_bb_re, d_bb_im)


def _cmul(ar, ai, br, bi):
    return ar * br - ai * bi, ar * bi + ai * br


def _scan_consts(lr, li, reverse):
    n = lr.shape[1]
    sub = lax.broadcasted_iota(jnp.int32, (8, n), 0)
    pows = [(lr, li)]
    for _ in range(7):
        pows.append(_cmul(pows[-1][0], pows[-1][1], lr, li))
    steps = []
    for s in (1, 2, 4):
        m = (sub < 8 - s) if reverse else (sub >= s)
        pr, pi = pows[s - 1]
        steps.append((s, jnp.where(m, jnp.broadcast_to(pr, (8, n)), 0.0), jnp.where(m, jnp.broadcast_to(pi, (8, n)), 0.0)))
    wr = jnp.zeros((8, n), F32)
    wi = jnp.zeros((8, n), F32)
    for r in range(8):
        e = (8 - r) if reverse else (r + 1)
        wr = jnp.where(sub == r, jnp.broadcast_to(pows[e - 1][0], (8, n)), wr)
        wi = jnp.where(sub == r, jnp.broadcast_to(pows[e - 1][1], (8, n)), wi)
    return steps, wr, wi


def _scan_rows(sre, sim, carry, lr, li, rows, reverse):
    steps, wr, wi = _scan_consts(lr, li, reverse)
    ng = rows // 8

    def step(gi, _):
        g = (ng - 1 - gi) if reverse else gi
        base = pl.multiple_of(g * 8, 8)
        xr = sre[pl.ds(base, 8), :]
        xi = sim[pl.ds(base, 8), :]
        for s, pr, pi in steps:
            sh = (8 - s) if reverse else s
            yr = pltpu.roll(xr, sh, 0)
            yi = pltpu.roll(xi, sh, 0)
            xr, xi = xr + pr * yr - pi * yi, xi + pr * yi + pi * yr
        cr = carry[0:1, :]
        ci = carry[1:2, :]
        xr, xi = xr + wr * cr - wi * ci, xi + wr * ci + wi * cr
        sre[pl.ds(base, 8), :] = xr
        sim[pl.ds(base, 8), :] = xi
        edge = 0 if reverse else 7
        carry[0:1, :] = xr[edge:edge + 1, :]
        carry[1:2, :] = xi[edge:edge + 1, :]
        return 0

    lax.fori_loop(0, ng, step, 0)


S5_Q = 4
S5_QL = S5_WIDTH // S5_Q
S5_QS = S5_LANES // S5_Q


def _s5_scan_fwd(proj, bq_re, bq_im, cq_re, cq_im, lbar, dskip, L, TB):
    nb = L // TB

    def body(u_ref, bre, bim, cre, cim, lb_ref, d_ref, y_ref, ck_ref, sre, sim, carry):
        i = pl.program_id(0)

        @pl.when(i == 0)
        def _():
            carry[...] = jnp.zeros(carry.shape, F32)

        ck_ref[0] = carry[...]
        u = u_ref[...]
        ub = u.astype(BF16)
        for q in range(S5_Q):
            uq = ub[:, q * S5_QL:(q + 1) * S5_QL]
            sre[:, q * S5_QS:(q + 1) * S5_QS] = _dot(uq, bre[q])
            sim[:, q * S5_QS:(q + 1) * S5_QS] = _dot(uq, bim[q])
        _scan_rows(sre, sim, carry, lb_ref[0:1, :], lb_ref[1:2, :], TB, False)
        for q in range(S5_Q):
            sl = slice(q * S5_QL, (q + 1) * S5_QL)
            ss = slice(q * S5_QS, (q + 1) * S5_QS)
            y_ref[:, sl] = (_dot(sre[:, ss].astype(BF16), cre[q]) - _dot(sim[:, ss].astype(BF16), cim[q])
                            + u[:, sl] * d_ref[:, sl])

    full = lambda a: pl.BlockSpec(a.shape, lambda i, nd=a.ndim: (0,) * nd)
    return _pcall(
        body, name="s5_scan_fwd", grid=(nb,),
        in_specs=[pl.BlockSpec((TB, S5_WIDTH), lambda i: (i, 0)), full(bq_re), full(bq_im), full(cq_re), full(cq_im),
                  full(lbar), full(dskip)],
        out_specs=[pl.BlockSpec((TB, S5_WIDTH), lambda i: (i, 0)), pl.BlockSpec((1, 8, S5_LANES), lambda i: (i, 0, 0))],
        out_shape=[jax.ShapeDtypeStruct((L, S5_WIDTH), F32), jax.ShapeDtypeStruct((nb, 8, S5_LANES), F32)],
        scratch_shapes=[pltpu.VMEM((TB, S5_LANES), F32), pltpu.VMEM((TB, S5_LANES), F32), pltpu.VMEM((8, S5_LANES), F32)],
        compiler_params=_cparams(1))(proj, bq_re, bq_im, cq_re, cq_im, lbar, dskip)


def _s5_scan_bwd(proj, dy, ck, bq_re, bq_im, cq_re, cq_im, lbar, dskip, L, TB):
    nb = L // TB
    ng = TB // 8

    def body(u_ref, dy_ref, ck_ref, bre, bim, cre, cim, lb_ref, d_ref,
             du_ref, dbre, dbim, dcre, dcim, dlb_ref, dd_ref, sre, sim, gre, gim, carry, gcarry):
        i = pl.program_id(0)

        @pl.when(i == 0)
        def _():
            gcarry[...] = jnp.zeros(gcarry.shape, F32)
            dbre[...] = jnp.zeros(dbre.shape, F32)
            dbim[...] = jnp.zeros(dbim.shape, F32)
            dcre[...] = jnp.zeros(dcre.shape, F32)
            dcim[...] = jnp.zeros(dcim.shape, F32)
            dlb_ref[...] = jnp.zeros(dlb_ref.shape, F32)
            dd_ref[...] = jnp.zeros(dd_ref.shape, F32)

        lr = lb_ref[0:1, :]
        li = lb_ref[1:2, :]
        u = u_ref[...]
        ub = u.astype(BF16)
        dy_v = dy_ref[...]
        dyb = dy_v.astype(BF16)
        carry[...] = ck_ref[0]
        for q in range(S5_Q):
            uq = ub[:, q * S5_QL:(q + 1) * S5_QL]
            dq = dyb[:, q * S5_QL:(q + 1) * S5_QL]
            ss = slice(q * S5_QS, (q + 1) * S5_QS)
            sre[:, ss] = _dot(uq, bre[q])
            sim[:, ss] = _dot(uq, bim[q])
            gre[:, ss] = _dot_nt(dq, cre[q])
            gim[:, ss] = -_dot_nt(dq, cim[q])
        _scan_rows(sre, sim, carry, lr, li, TB, False)
        _scan_rows(gre, gim, gcarry, lr, -li, TB, True)

        sub = lax.broadcasted_iota(jnp.int32, (8, S5_LANES), 0)
        c0r = ck_ref[0, 0:1, :]
        c0i = ck_ref[0, 1:2, :]

        def acc_step(g, acc):
            ar, ai = acc
            base = pl.multiple_of(g * 8, 8)
            pbase = pl.multiple_of(jnp.maximum(g - 1, 0) * 8, 8)
            first = g == 0
            lastr = jnp.where(first, c0r, sre[pl.ds(pbase, 8), :][7:8, :])
            lasti = jnp.where(first, c0i, sim[pl.ds(pbase, 8), :][7:8, :])
            spr = jnp.where(sub == 0, jnp.broadcast_to(lastr, sub.shape), pltpu.roll(sre[pl.ds(base, 8), :], 1, 0))
            spi = jnp.where(sub == 0, jnp.broadcast_to(lasti, sub.shape), pltpu.roll(sim[pl.ds(base, 8), :], 1, 0))
            gr = gre[pl.ds(base, 8), :]
            gi_ = gim[pl.ds(base, 8), :]
            return ar + gr * spr + gi_ * spi, ai - gr * spi + gi_ * spr

        z8 = jnp.zeros((8, S5_LANES), F32)
        ar, ai = lax.fori_loop(0, ng, acc_step, (z8, z8))
        dlb_ref[0:1, :] += jnp.sum(ar, axis=0, keepdims=True)
        dlb_ref[1:2, :] += jnp.sum(ai, axis=0, keepdims=True)

        dd_ref[...] += jnp.sum(dy_v * u, axis=0, keepdims=True)
        for q in range(S5_Q):
            sl = slice(q * S5_QL, (q + 1) * S5_QL)
            ss = slice(q * S5_QS, (q + 1) * S5_QS)
            grq = gre[:, ss].astype(BF16)
            giq = gim[:, ss].astype(BF16)
            du_ref[:, sl] = _dot_nt(grq, bre[q]) + _dot_nt(giq, bim[q]) + dy_v[:, sl] * d_ref[:, sl]
            dbre[q] += _dot_tn(ub[:, sl], grq)
            dbim[q] += _dot_tn(ub[:, sl], giq)
            dcre[q] += _dot_tn(sre[:, ss].astype(BF16), dyb[:, sl])
            dcim[q] -= _dot_tn(sim[:, ss].astype(BF16), dyb[:, sl])

    full = lambda a: pl.BlockSpec(a.shape, lambda i, nd=a.ndim: (0,) * nd)
    rev = lambda i: (nb - 1 - i, 0)
    sh = jax.ShapeDtypeStruct
    outs = [sh((L, S5_WIDTH), F32), sh(bq_re.shape, F32), sh(bq_im.shape, F32), sh(cq_re.shape, F32), sh(cq_im.shape, F32),
            sh((8, S5_LANES), F32), sh((1, S5_WIDTH), F32)]
    fo = lambda s: pl.BlockSpec(s.shape, lambda i, nd=len(s.shape): (0,) * nd)
    return _pcall(
        body, name="s5_scan_bwd", grid=(nb,),
        in_specs=[pl.BlockSpec((TB, S5_WIDTH), rev), pl.BlockSpec((TB, S5_WIDTH), rev),
                  pl.BlockSpec((1, 8, S5_LANES), lambda i: (nb - 1 - i, 0, 0)),
                  full(bq_re), full(bq_im), full(cq_re), full(cq_im), full(lbar), full(dskip)],
        out_specs=[pl.BlockSpec((TB, S5_WIDTH), rev)] + [fo(s) for s in outs[1:]],
        out_shape=outs,
        scratch_shapes=[pltpu.VMEM((TB, S5_LANES), F32)] * 4 + [pltpu.VMEM((8, S5_LANES), F32)] * 2,
        compiler_params=_cparams(1))(proj, dy, ck, bq_re, bq_im, cq_re, cq_im, lbar, dskip)


N_HEAD = RW_WIDTH // HEAD
_NN = (((2,), (1,)), ((0,), (0,)))
_NT = (((2,), (2,)), ((0,), (0,)))
_TN = (((1,), (1,)), ((0,), (0,)))


def _hi_lo(x):
    h = x.astype(BF16)
    return h, (x - h.astype(F32)).astype(BF16)


def _mm_acc(a, b, dims, passes=3):
    dg = lambda p, q: lax.dot_general(p, q, dims, preferred_element_type=F32)
    if passes == 1:
        return dg(a.astype(BF16), b.astype(BF16))
    ah, al = _hi_lo(a)
    bh, bl = _hi_lo(b)
    return dg(ah, bh) + dg(ah, bl) + dg(al, bh)


def _cumsum_rows(x, transpose):
    h, n, _ = x.shape
    ti = lax.broadcasted_iota(jnp.int32, (h, n, n), 1)
    tj = lax.broadcasted_iota(jnp.int32, (h, n, n), 2)
    m = ((tj >= ti) if transpose else (tj <= ti)).astype(BF16)
    acc, rem = None, x
    for s in range(3):
        part = rem.astype(BF16)
        t = lax.dot_general(m, part, _NN, preferred_element_type=F32)
        acc = t if acc is None else acc + t
        if s < 2:
            rem = rem - part.astype(F32)
    return acc


def _slices(x, axis, sizes):
    out, off = [], 0
    for n in sizes:
        out.append(lax.slice_in_dim(x, off, off + n, axis=axis))
        off += n
    return tuple(out)


def _cat_op(axis, sizes, diff):
    plain = lambda *xs: jnp.concatenate(xs, axis=axis)
    if not diff:
        return plain
    f = jax.custom_vjp(plain)
    f.defvjp(lambda *xs: (plain(*xs), None), lambda _, d: _slices(d, axis, sizes))
    return f


def _split_op(axis, sizes, diff):
    plain = lambda x: _slices(x, axis, sizes)
    if not diff:
        return plain
    f = jax.custom_vjp(plain)
    f.defvjp(lambda x: (plain(x), None), lambda _, d: (jnp.concatenate(d, axis=axis),))
    return f


def _mm_ops(diff, passes):
    mm = lambda a, b, dims: _mm_acc(a, b, dims, passes)
    if not diff:
        return (lambda a, b: mm(a, b, _NN), lambda a, b: mm(a, b, _NT), lambda a, b: mm(a, b, _TN))

    @jax.custom_vjp
    def nn(a, b):
        return mm(a, b, _NN)

    nn.defvjp(lambda a, b: (mm(a, b, _NN), (a, b)), lambda r, d: (mm(d, r[1], _NT), mm(r[0], d, _TN)))

    @jax.custom_vjp
    def nt(a, b):
        return mm(a, b, _NT)

    nt.defvjp(lambda a, b: (mm(a, b, _NT), (a, b)), lambda r, d: (mm(d, r[1], _NN), mm(d, r[0], _TN)))

    @jax.custom_vjp
    def tn(a, b):
        return mm(a, b, _TN)

    tn.defvjp(lambda a, b: (mm(a, b, _TN), (a, b)), lambda r, d: (mm(r[1], d, _NT), mm(r[0], d, _NN)))
    return nn, nt, tn


def _cums_op(diff):
    if not diff:
        return lambda x: _cumsum_rows(x, False)

    @jax.custom_vjp
    def cums(x):
        return _cumsum_rows(x, False)

    cums.defvjp(lambda x: (_cumsum_rows(x, False), None), lambda _, d: (_cumsum_rows(d, True),))
    return cums


WKV_PASSES = (1, 1, 1, 1, 1)


WKV_SUB = 4
WKV_BLOCK = CHUNK * WKV_SUB


def _wkv_block(s0, r, w, k, v, a, b, diff):
    p_pair, p_val, p_solve, p_out, p_state = WKV_PASSES
    cums = _cums_op(diff)
    _, nt_pair, _ = _mm_ops(diff, p_pair)
    nn_val, _, _ = _mm_ops(diff, p_val)
    nn_solve, _, _ = _mm_ops(diff, p_solve)
    nn_out, nt_out, _ = _mm_ops(diff, p_out)
    nn_state, _, tn_state = _mm_ops(diff, p_state)
    h, d, n, sub = s0.shape[0], s0.shape[2], CHUNK, WKV_SUB
    hb = h * sub
    to_chunks = lambda t: _cat_op(0, (h,) * sub, diff)(*_split_op(1, (n,) * sub, diff)(t))
    r, w, k, v, a, b = (to_chunks(t) for t in (r, w, k, v, a, b))
    cat_rows2 = _cat_op(1, (n, n), diff)
    cat_lanes2 = _cat_op(2, (n, n), diff)
    split_rows2 = _split_op(1, (n, n), diff)
    split_lanes2 = _split_op(2, (n, n), diff)
    ti = lax.broadcasted_iota(jnp.int32, (hb, n, n), 1)
    tj = lax.broadcasted_iota(jnp.int32, (hb, n, n), 2)
    incl, strict = tj <= ti, tj < ti
    logw = jnp.log(w)
    cum = cums(logw)
    g_in, g_ex, g_inv = jnp.exp(cum), jnp.exp(cum - logw), jnp.exp(-cum)
    ae, re, bi, ki = a * g_ex, r * g_in, b * g_inv, k * g_inv
    top, bot = split_rows2(nt_pair(cat_rows2(ae, re), cat_rows2(bi, ki)))
    tab, tak = split_lanes2(top)
    qb, qk = split_lanes2(bot)
    tab, tak = jnp.where(strict, tab, 0.0), jnp.where(strict, tak, 0.0)
    qb, qk = jnp.where(incl, qb, 0.0), jnp.where(incl, qk, 0.0)
    tak_v, qk_v = split_rows2(nn_val(cat_rows2(tak, qk), v))
    x = cat_lanes2(ae, tak_v)
    npow = tab
    steps = max(1, (n - 1).bit_length())
    for i in range(steps):
        x = x + nn_solve(npow, x)
        if i + 1 < steps:
            npow = nn_solve(npow, npow)
    ae_m, uc = split_lanes2(x)
    qx = nn_out(qb, x)
    q_ae, q_uc = split_lanes2(qx)
    re_m = re + q_ae
    yc = q_uc + qk_v
    g_end = jnp.exp(jnp.sum(logw, axis=1, keepdims=True))
    bg, kg = bi * g_end, ki * g_end
    tm = tn_state(ae_m, bg)
    sc = tn_state(cat_rows2(uc, v), cat_rows2(bg, kg))
    per_chunk = _split_op(0, (h,) * sub, diff)
    re_m, yc, g_end, tm, sc = (per_chunk(t) for t in (re_m, yc, g_end, tm, sc))
    ys, s = [], s0
    for i in range(sub):
        ys.append(nt_out(re_m[i], s) + yc[i])
        s = s * g_end[i] + nn_state(s, tm[i]) + sc[i]
    return _cat_op(1, (n,) * sub, diff)(*ys), s


def _wkv_fwd(r, w, k, v, a, b, L):
    nc = L // WKV_BLOCK

    def body(r_ref, w_ref, k_ref, v_ref, a_ref, b_ref, y_ref, ck_ref, s_ref):
        c = pl.program_id(0)

        @pl.when(c == 0)
        def _():
            s_ref[...] = jnp.zeros(s_ref.shape, F32)

        s0 = s_ref[...]
        ck_ref[0] = s0
        y, s1 = _wkv_block(s0, r_ref[...], w_ref[...], k_ref[...], v_ref[...], a_ref[...], b_ref[...], False)
        y_ref[...] = y
        s_ref[...] = s1

    blk = pl.BlockSpec((N_HEAD, WKV_BLOCK, HEAD), lambda c: (0, c, 0))
    return _pcall(
        body, name="wkv_fwd", grid=(nc,), in_specs=[blk] * 6,
        out_specs=[blk, pl.BlockSpec((1, N_HEAD, HEAD, HEAD), lambda c: (c, 0, 0, 0))],
        out_shape=[jax.ShapeDtypeStruct((N_HEAD, L, HEAD), F32), jax.ShapeDtypeStruct((nc, N_HEAD, HEAD, HEAD), F32)],
        scratch_shapes=[pltpu.VMEM((N_HEAD, HEAD, HEAD), F32)],
        compiler_params=_cparams(1))(r, w, k, v, a, b)


def _wkv_bwd(r, w, k, v, a, b, dy, ck, L, deps=()):
    nc = L // WKV_BLOCK

    def body(r_ref, w_ref, k_ref, v_ref, a_ref, b_ref, dy_ref, ck_ref, *rest):
        dr_ref, dw_ref, dk_ref, dv_ref, da_ref, db_ref, ds_ref = rest[len(deps):]
        c = pl.program_id(0)

        @pl.when(c == 0)
        def _():
            ds_ref[...] = jnp.zeros(ds_ref.shape, F32)

        _, vjp = jax.vjp(lambda *t: _wkv_block(*t, True), ck_ref[0], r_ref[...], w_ref[...], k_ref[...], v_ref[...],
                         a_ref[...], b_ref[...])
        g = vjp((dy_ref[...], ds_ref[...]))
        ds_ref[...] = g[0]
        for o_ref, val in zip((dr_ref, dw_ref, dk_ref, dv_ref, da_ref, db_ref), g[1:]):
            o_ref[...] = val

    blk = pl.BlockSpec((N_HEAD, WKV_BLOCK, HEAD), lambda c: (0, nc - 1 - c, 0))
    sh = jax.ShapeDtypeStruct((N_HEAD, L, HEAD), F32)
    return _pcall(
        body, name="wkv_bwd", grid=(nc,),
        in_specs=[blk] * 7 + [pl.BlockSpec((1, N_HEAD, HEAD, HEAD), lambda c: (nc - 1 - c, 0, 0, 0))]
        + [pl.BlockSpec(d.shape, lambda c, nd=d.ndim: (0,) * nd) for d in deps],
        out_specs=[blk] * 6, out_shape=[sh] * 6,
        scratch_shapes=[pltpu.VMEM((N_HEAD, HEAD, HEAD), F32)],
        compiler_params=_cparams(1))(r, w, k, v, a, b, dy, ck, *deps)


TB = 256


def _bf(x):
    return x.astype(BF16)


def _inproj_fwd(x, norm_mix, w_in, L, deps=()):
    def fn(i, tv, cv):
        xn = _rms(tv[0], cv[0])
        return _dot(_bf(xn), cv[1]), xn

    return _tok_call("inproj_fwd", fn, L, TB, [(x, D_MODEL, 0)], [norm_mix, w_in], [(IN_COLS, F32), (D_MODEL, BF16)],
                     deps=deps)


def _s5_post_fn(glu_w, wtop, diff=True):
    mg = _mmc(glu_w, diff)
    mt = _mmc(wtop, diff) if wtop is not None else None

    def f(y, glu_b, e):
        z = _gelu(y)
        out = z * _sigmoid(mg(z) + glu_b + e)
        res = mt(out) if mt is not None else out
        return res, (z, out)

    return f


def _s5_post_fwd(y, glu_w, glu_b, L):
    def fn(i, tv, cv):
        out, _ = _s5_post_fn(cv[0], None, False)(tv[0], cv[1], 0.0)
        return (out,)

    return _tok_call("s5_post_fwd", fn, L, TB, [(y, S5_WIDTH, 0)], [glu_w, glu_b], [(S5_WIDTH, F32)])[0]


def _s5_post_bwd(y, dh1, glu_w, glu_b, wtop, L):
    def fn(i, tv, cv):
        e0 = jnp.zeros((TB, S5_WIDTH), F32)
        _, vjp, (z, out) = jax.vjp(_s5_post_fn(cv[0], cv[2]), tv[0], cv[1], e0, has_aux=True)
        dy, db, de = vjp(tv[1])
        return dy, z, de, out, db

    return _tok_call("s5_post_bwd", fn, L, TB, [(y, S5_WIDTH, 0), (dh1, D_MODEL, 0)], [glu_w, glu_b, wtop],
                     [(S5_WIDTH, F32), (S5_WIDTH, BF16), (S5_WIDTH, BF16), (S5_WIDTH, BF16)], [(1, S5_WIDTH)])


RW_COLBLK = ((RW_WIDTH, 1), (RW_WIDTH, 2), (RW_WIDTH, 3), (128, 16), (128, 17))
RW_MU = ((0, 512), (512, 1024), (1024, 1536), (1536, 1664), (1664, 1792))


def _rw_pre_fn(w2pad, a2pad, g2, diff=True):
    m_w, m_a, m_g = _mmc(w2pad, diff), _mmc(a2pad, diff), _mmc(g2, diff)
    seg = _segsum(_head_indicator(RW_WIDTH), diff)

    def f(zr, zk, zv, zwa, zg, w0, a0, k_k, k_a, e_w, e_a):
        wl_t = jnp.tanh(zwa)
        wlin = w0 + m_w(wl_t) + e_w
        w = -_softplus(-wlin) - 0.5
        decay = jnp.exp(-jnp.exp(w))
        a = _sigmoid(a0 + m_a(zwa) + e_a)
        sg = _sigmoid(zg)
        g = m_g(sg)
        kk = zk * k_k
        kkn = kk / jnp.maximum(jnp.sqrt(seg(kk * kk)), L2_EPS)
        kf = zk * (1.0 + (a - 1.0) * k_a)
        return (zr, decay, kf, zv, -kkn, kkn * a, g), (wl_t, sg)

    return f


def _rw_shifted(i, tv, mu):
    sub = lax.broadcasted_iota(jnp.int32, (TB, 1), 0)
    zs, dif = [], []
    for n in range(5):
        z = tv[n]
        last = jnp.where(i == 0, 0.0, tv[5 + n][7:8, :])
        prev = jnp.where(sub == 0, last, pltpu.roll(z, 1, 0))
        m = mu[:, RW_MU[n][0]:RW_MU[n][1]]
        zs.append(z + (prev - z) * m)
        dif.append(prev - z)
    return zs, dif


def _rw_tok_in(proj):
    return [(proj, wd, cb) for wd, cb in RW_COLBLK] + [(proj, wd, cb, "prev") for wd, cb in RW_COLBLK]


def _rw_pre_fwd(proj, mu, w0, a0, k_k, k_a, w2pad, a2pad, g2, L):
    def fn(i, tv, cv):
        zs, _ = _rw_shifted(i, tv, cv[0])
        outs, _ = _rw_pre_fn(cv[5], cv[6], cv[7], False)(*zs, cv[1], cv[2], cv[3], cv[4], 0.0, 0.0)
        return outs

    return _tok_call("rw_pre_fwd", fn, L, TB, _rw_tok_in(proj), [mu, w0, a0, k_k, k_a, w2pad, a2pad, g2],
                     [("heads", F32)] * 6 + [(RW_WIDTH, F32)])


def _rw_pre_bwd(proj, cots, mu, w0, a0, k_k, k_a, w2pad, a2pad, g2, L):
    def fn(i, tv, cv):
        zs, dif = _rw_shifted(i, tv[:10], cv[0])
        dr1, dr2, dw, dk1, dk2, dv1, dv2, da, db, dg = tv[10:]
        e0 = jnp.zeros((TB, RW_WIDTH), F32)
        _, vjp, (wl_t, sg) = jax.vjp(_rw_pre_fn(cv[5], cv[6], cv[7]), *zs, cv[1], cv[2], cv[3], cv[4], e0, e0, has_aux=True)
        g = vjp((dr1 + dr2, dw, dk1 + dk2, dv1 + dv2, da, db, dg))
        dzs = jnp.concatenate(g[:5], axis=1)
        dmu = jnp.concatenate([jnp.sum(g[n] * dif[n], axis=0, keepdims=True) for n in range(5)], axis=1)
        return dzs, wl_t, zs[3], sg, g[9], g[10], dmu, g[5], g[6], g[7], g[8]

    tok_in = _rw_tok_in(proj) + [((c,) if c.ndim == 3 else (c, RW_WIDTH, 0)) for c in cots]
    return _tok_call("rw_pre_bwd", fn, L, TB, tok_in, [mu, w0, a0, k_k, k_a, w2pad, a2pad, g2],
                     [(SHIFT_COLS, F32), (128, BF16), (128, BF16), (128, BF16), (RW_WIDTH, BF16), (RW_WIDTH, BF16)],
                     [(1, SHIFT_COLS)] + [(1, RW_WIDTH)] * 4)


def _rw_post_fn(wbot, diff=True):
    seg = _segsum(_head_indicator(RW_WIDTH), diff)
    mb = _mmc(wbot, diff) if wbot is not None else None

    def f(y, r, kf, v, g, ln_w, ln_b, r_k):
        mean = seg(y) * (1.0 / HEAD)
        yc = y - mean
        var = seg(yc * yc) * (1.0 / HEAD)
        yn = yc * lax.rsqrt(var + GN_EPS) * ln_w + ln_b
        bonus = seg(r * kf * r_k) * v
        out = (yn + bonus) * g
        res = mb(out) if mb is not None else out
        return res, out

    return f


def _rw_post_fwd(y, r, kf, v, g, ln_w, ln_b, r_k, L):
    def fn(i, tv, cv):
        out, _ = _rw_post_fn(None, False)(*tv, *cv)
        return (out,)

    return _tok_call("rw_post_fwd", fn, L, TB, [(t,) for t in (y, r, kf, v)] + [(g, RW_WIDTH, 0)], [ln_w, ln_b, r_k],
                     [(RW_WIDTH, F32)])[0]


def _rw_post_bwd(y, r, kf, v, g, dh1, ln_w, ln_b, r_k, wbot, L):
    def fn(i, tv, cv):
        _, vjp, out = jax.vjp(_rw_post_fn(cv[3]), *tv[:5], cv[0], cv[1], cv[2], has_aux=True)
        gr = vjp(tv[5])
        return gr[0], gr[1], gr[2], gr[3], gr[4], out, gr[5], gr[6], gr[7]

    return _tok_call("rw_post_bwd", fn, L, TB, [(t,) for t in (y, r, kf, v)] + [(g, RW_WIDTH, 0), (dh1, D_MODEL, 0)],
                     [ln_w, ln_b, r_k, wbot], [("heads", F32)] + [(RW_WIDTH, F32)] * 4 + [(RW_WIDTH, BF16)], [(1, RW_WIDTH)] * 3)


def _ffn_fn(w1, w3, w2, diff=True):
    m1, m3, m2 = _mmc(w1, diff), _mmc(w3, diff), _mmc(w2, diff)

    def f(h1, norm_ffn, e1, e3):
        hn = _rms(h1, norm_ffn)
        a1 = m1(hn) + e1
        a3 = m3(hn) + e3
        hm = a1 * _sigmoid(a1) * a3
        return h1 + m2(hm), (hn, hm)

    return f


TB_FFN = 256


def _mixffn_fwd(x, s5_out, rw_out, wtop, wbot, norm_ffn, w1, w3, w2, L):
    def fn(i, tv, cv):
        h1 = tv[0] + _dot(_bf(tv[1]), cv[0]) + _dot(_bf(tv[2]), cv[1])
        h2, _ = _ffn_fn(cv[3], cv[4], cv[5], False)(h1, cv[2], 0.0, 0.0)
        return h1, h2

    return _tok_call("mixffn_fwd", fn, L, TB_FFN, [(x, D_MODEL, 0), (s5_out, S5_WIDTH, 0), (rw_out, RW_WIDTH, 0)],
                     [wtop, wbot, norm_ffn, w1, w3, w2], [(D_MODEL, F32), (D_MODEL, F32)])


def _ffn_bwd(h1, dh2, norm_ffn, w1, w3, w2, L):
    def fn(i, tv, cv):
        e0 = jnp.zeros((TB_FFN, FFN_HIDDEN), F32)
        _, vjp, (hn, hm) = jax.vjp(_ffn_fn(cv[1], cv[2], cv[3]), tv[0], cv[0], e0, e0, has_aux=True)
        dh1, dn, d1, d3 = vjp(tv[1])
        return dh1, d1, d3, hm, hn, dn

    return _tok_call("ffn_bwd", fn, L, TB_FFN, [(h1, D_MODEL, 0), (dh2, D_MODEL, 0)], [norm_ffn, w1, w3, w2],
                     [(D_MODEL, F32), (FFN_HIDDEN, BF16), (FFN_HIDDEN, BF16), (FFN_HIDDEN, BF16), (D_MODEL, BF16)],
                     [(1, D_MODEL)])


def _ple_loss_fb(h2, p, target, norm_ple, final_norm, wg, wu, L):
    def fn(i, tv, cv):
        mgate, mup = _mmc(cv[2]), _mmc(cv[3], False)

        def f(h2_, norm_ple_, final_norm_, eg, eu):
            hn = _rms(h2_, norm_ple_)
            gate = _sigmoid(mgate(hn) + eg)
            h3 = h2_ + gate * (mup(tv[1]) + eu)
            out = _rms(h3, final_norm_)
            d = out - tv[2]
            return 0.5 * jnp.sum(jnp.mean(d * d, axis=-1, keepdims=True)), hn

        e0 = jnp.zeros((TB, D_MODEL), F32)
        loss, vjp, hn = jax.vjp(f, tv[0], cv[0], cv[1], e0, e0, has_aux=True)
        dh2, dnp, dfn, deg, deu = vjp(jnp.ones((), F32))
        return dh2, dh2, deg, deu, hn, jnp.full((8, 128), loss, F32), dnp, dfn

    return _tok_call("ple_loss_fb", fn, L, TB, [(h2, D_MODEL, 0), (p, PLE_DIM, 0), (target, D_MODEL, 0)],
                     [norm_ple, final_norm, wg, wu],
                     [(D_MODEL, F32), (D_MODEL, BF16), (D_MODEL, BF16), (D_MODEL, BF16), (D_MODEL, BF16)],
                     [(8, 128), (1, D_MODEL), (1, D_MODEL)])


def _inproj_bwd(x, dh1, du, dzs, norm_mix, mu, w_u, w_z, L):
    nb = L // TB

    def fn(i, tv, cv):
        sub = lax.broadcasted_iota(jnp.int32, (TB, 1), 0)
        m = cv[1]
        b = tv[3] * m
        nxt = jnp.where(i == nb - 1, 0.0, tv[4][0:1, :] * m)
        dz = tv[3] * (1.0 - m) + jnp.where(sub == TB - 1, nxt, pltpu.roll(b, TB - 1, 0))
        dub, dzb = _bf(tv[2]), _bf(dz)
        dxn = _dot_nt(dub, cv[2]) + _dot_nt(dzb, cv[3])
        _, vjp = jax.vjp(_rms, tv[0], cv[0])
        dx, dn = vjp(dxn)
        return tv[1] + dx, jnp.concatenate([dub, dzb], axis=1), dn

    return _tok_call("inproj_bwd", fn, L, TB,
                     [(x, D_MODEL, 0), (dh1, D_MODEL, 0), (du, S5_WIDTH, 0), (dzs, SHIFT_COLS, 0), (dzs, SHIFT_COLS, 0, "next")],
                     [norm_mix, mu, w_u, w_z], [(D_MODEL, F32), (IN_COLS, BF16)], [(1, D_MODEL)])


def _eye8(dt):
    return jnp.eye(8, dtype=dt)


def _quarter_b(bb):
    return jnp.einsum("hg,qgcp->qhcgp", _eye8(bb.dtype), bb.reshape(S5_Q, 8, S5_GROUP, S5_STATE)).reshape(S5_Q, S5_QL, S5_QS)


def _unquarter_b(d):
    return jnp.einsum("qhcgp,hg->qgcp", d.reshape(S5_Q, 8, S5_GROUP, 8, S5_STATE), _eye8(d.dtype)).reshape(
        S5_GROUPS, S5_GROUP, S5_STATE)


def _quarter_c(c):
    return jnp.einsum("gh,qgcp->qgphc", _eye8(c.dtype), c.reshape(S5_Q, 8, S5_GROUP, S5_STATE)).reshape(S5_Q, S5_QS, S5_QL)


def _unquarter_c(d):
    return jnp.einsum("qgphc,gh->qgcp", d.reshape(S5_Q, 8, S5_STATE, 8, S5_GROUP), _eye8(d.dtype)).reshape(
        S5_GROUPS, S5_GROUP, S5_STATE)


def _local_step(x, p, target, W, late_weights=None, grads_ready=None, first_dep=None):
    L = x.shape[0]
    r2 = lambda v: v.reshape(1, -1)
    w_in = W["w_in"]
    w2pad = jnp.pad(W["rw_w2"], ((0, 64), (0, 0)))
    a2pad = jnp.pad(W["rw_a2"], ((64, 0), (0, 0)))
    mu = r2(W["rw_shift_mu"])
    rw_vec = [r2(W[n]) for n in ("rw_w0", "rw_a0", "rw_k_k", "rw_k_a")]
    ln_w, ln_b, r_k = r2(W["rw_ln_w"]), r2(W["rw_ln_b"]), r2(W["rw_r_k"])

    lam_re, lam_im = W["s5_lam_re"], W["s5_lam_im"]
    log_step = W["s5_log_step"].reshape(S5_GROUPS, 1)
    bt_re, bt_im = W["s5_b_re"].transpose(0, 2, 1), W["s5_b_im"].transpose(0, 2, 1)
    lb_re, lb_im, bb_re, bb_im = _s5_param_fwd(lam_re, lam_im, log_step, bt_re, bt_im)
    bq_re, bq_im = _quarter_b(bb_re).astype(BF16), _quarter_b(bb_im).astype(BF16)
    cq_re, cq_im = _quarter_c(W["s5_c_re"]).astype(BF16), _quarter_c(W["s5_c_im"]).astype(BF16)
    lbar = jnp.concatenate([lb_re.reshape(1, -1), lb_im.reshape(1, -1), jnp.zeros((6, S5_LANES), F32)], axis=0)
    dskip = r2(W["s5_d"])
    glu_b = r2(W["s5_glu_b"])
    norm_mix, norm_ffn, norm_ple, final_norm = (r2(W[n]) for n in ("norm_mix", "norm_ffn", "norm_ple", "final_norm"))

    proj, xn = _inproj_fwd(x, norm_mix, w_in, L, () if first_dep is None else (first_dep,))
    y_s5, ck5 = _s5_scan_fwd(proj, bq_re, bq_im, cq_re, cq_im, lbar, dskip, L, TB)
    s5_out = _s5_post_fwd(y_s5, W["s5_glu_w"], glu_b, L)
    r, wd, kf, v, a_s, b_s, g = _rw_pre_fwd(proj, mu, *rw_vec, w2pad, a2pad, W["rw_g2"], L)
    scan_in = (r, wd, kf, v, a_s, b_s)
    y_wkv, ckw = _wkv_fwd(*scan_in, L)
    rw_out = _rw_post_fwd(y_wkv, r, kf, v, g, ln_w, ln_b, r_k, L)
    if late_weights is not None:
        W = dict(W, **late_weights(rw_out))
    wtop, wbot = W["w_out"][:S5_WIDTH], W["w_out"][S5_WIDTH:]
    h1, h2 = _mixffn_fwd(x, s5_out, rw_out, wtop, wbot, norm_ffn, W["ffn_w1"], W["ffn_w3"], W["ffn_w2"], L)

    G = {}
    dh2, dh2_bf, deg, deu, hn_ple, loss_acc, G["norm_ple"], G["final_norm"] = _ple_loss_fb(
        h2, p, target, norm_ple, final_norm, W["ple_gate_w"], W["ple_up_w"], L)
    dh1, da1, da3, hm, hn_ffn, G["norm_ffn"] = _ffn_bwd(h1, dh2, norm_ffn, W["ffn_w1"], W["ffn_w3"], W["ffn_w2"], L)
    dy_s5, z_bf, dgp, s5o_bf, G["s5_glu_b"] = _s5_post_bwd(y_s5, dh1, W["s5_glu_w"], glu_b, wtop, L)
    dy_wkv, dr2, dk2, dv2, dg, rwo_bf, G["rw_ln_w"], G["rw_ln_b"], G["rw_r_k"] = _rw_post_bwd(
        y_wkv, r, kf, v, g, dh1, ln_w, ln_b, r_k, wbot, L)
    G["ffn_w1"] = _mm_tn("dw_ffn_w1", hn_ffn, da1)
    G["ffn_w3"] = _mm_tn("dw_ffn_w3", hn_ffn, da3)
    G["ffn_w2"] = _mm_tn("dw_ffn_w2", hm, dh2_bf)
    G["ple_gate_w"] = _mm_tn("dw_ple_gate", hn_ple, deg)
    G["w_out"] = jnp.concatenate([_mm_tn("dw_out_top", s5o_bf, dh1), _mm_tn("dw_out_bot", rwo_bf, dh1)], axis=0)
    dep = grads_ready(G) if grads_ready is not None else None
    G["ple_up_w"] = _mm_tn("dw_ple_up", p, deu)
    G["s5_glu_w"] = _mm_tn("dw_s5_glu", z_bf, dgp)
    dr1, dwd, dk1, dv1, da_s, db_s = _wkv_bwd(*scan_in, dy_wkv, ckw, L, () if dep is None else (dep,))
    (dzs, wl_t, zwa, sg, dwlin, dalin, G["rw_shift_mu"], G["rw_w0"], G["rw_a0"], G["rw_k_k"], G["rw_k_a"]) = _rw_pre_bwd(
        proj, (dr1, dr2, dwd, dk1, dk2, dv1, dv2, da_s, db_s, dg), mu, *rw_vec, w2pad, a2pad, W["rw_g2"], L)
    G["rw_w2"] = _mm_tn("dw_rw_w2", wl_t, dwlin)[:64]
    G["rw_a2"] = _mm_tn("dw_rw_a2", zwa, dalin)[64:]
    G["rw_g2"] = _mm_tn("dw_rw_g2", sg, dg)
    du, dbq_re, dbq_im, dcq_re, dcq_im, dlbar, G["s5_d"] = _s5_scan_bwd(
        proj, dy_s5, ck5, bq_re, bq_im, cq_re, cq_im, lbar, dskip, L, TB)
    G["s5_c_re"], G["s5_c_im"] = _unquarter_c(dcq_re), _unquarter_c(dcq_im)
    d_lam_re, d_lam_im, d_ls, d_bt_re, d_bt_im = _s5_param_bwd(
        lam_re, lam_im, log_step, bt_re, bt_im, dlbar[0].reshape(S5_GROUPS, S5_STATE), dlbar[1].reshape(S5_GROUPS, S5_STATE),
        _unquarter_b(dbq_re), _unquarter_b(dbq_im))
    G["s5_lam_re"], G["s5_lam_im"], G["s5_log_step"] = d_lam_re, d_lam_im, d_ls.reshape(S5_GROUPS)
    G["s5_b_re"], G["s5_b_im"] = d_bt_re.transpose(0, 2, 1), d_bt_im.transpose(0, 2, 1)
    dx, dproj, G["norm_mix"] = _inproj_bwd(x, dh1, du, dzs, norm_mix, mu, w_in[:, :S5_WIDTH], w_in[:, S5_WIDTH:], L)
    G["w_in"] = _mm_tn("dw_in", xn, dproj)
    return loss_acc[0, 0], dx, G


MESH_AXES = ("x", "y", "c")
_ANY = pl.BlockSpec(memory_space=pl.ANY)


def _all_gather(name, shards):
    nt = len(shards)

    def body(*refs):
        x_refs, out_refs = refs[:nt], refs[nt:2 * nt]
        send_sems, recv_sems, local_sems = refs[2 * nt:]
        x, y, c = lax.axis_index("x"), lax.axis_index("y"), lax.axis_index("c")
        me, sibling = (x, y, c), (x, y, 1 - c)
        chips = [(1 - x, y), (x, 1 - y), (1 - x, 1 - y)]

        def rows(t, px, py, pc):
            m_per = shards[t].shape[0]
            return out_refs[t].at[pl.ds((4 * px + 2 * py + pc) * m_per, m_per), :]

        def copy(t, k, block, to, src=None):
            return pltpu.make_async_remote_copy(
                src_ref=rows(t, *block) if src is None else src, dst_ref=rows(t, *block),
                send_sem=send_sems.at[7 * t + k], recv_sem=recv_sems.at[7 * t + k],
                device_id=to, device_id_type=pl.DeviceIdType.MESH)

        mine = [pltpu.make_async_copy(x_refs[t], rows(t, *me), local_sems.at[t]) for t in range(nt)]
        for cp in mine:
            cp.start()
        first = []
        for t in range(nt):
            first.append(copy(t, 0, me, sibling, src=x_refs[t]))
            first += [copy(t, 1 + j, me, (*chip, c), src=x_refs[t]) for j, chip in enumerate(chips)]
        for cp in first:
            cp.start()
        passed = []
        for t in range(nt):
            for j, chip in enumerate(chips):
                copy(t, 1 + j, (*chip, c), me).wait_recv()
                fwd = copy(t, 4 + j, (*chip, c), sibling)
                fwd.start()
                passed.append(fwd)
        for t in range(nt):
            copy(t, 0, sibling, me).wait_recv()
            for j, chip in enumerate(chips):
                copy(t, 4 + j, (*chip, 1 - c), me).wait_recv()
        for cp in first + passed:
            cp.wait_send()
        for cp in mine:
            cp.wait()

    return _pcall(body, name=name,
                  out_shape=[jax.ShapeDtypeStruct((N_DEV * a.shape[0], a.shape[1]), a.dtype) for a in shards],
                  in_specs=[_ANY] * nt, out_specs=[_ANY] * nt,
                  scratch_shapes=[pltpu.SemaphoreType.DMA((7 * nt,)), pltpu.SemaphoreType.DMA((7 * nt,)),
                                  pltpu.SemaphoreType.DMA((nt,))])(*shards)


_HBM = pl.BlockSpec(memory_space=pltpu.HBM)
_SEM = pl.BlockSpec(memory_space=pltpu.SEMAPHORE)
_EFFECT = pltpu.SideEffectType.DATAFLOW_SIDE_EFFECTING


def _peer_of(k):
    x, y, c = lax.axis_index("x"), lax.axis_index("y"), lax.axis_index("c")
    px, py, pc = x ^ ((k >> 2) & 1), y ^ ((k >> 1) & 1), c ^ (k & 1)
    return (px, py, pc), 4 * px + 2 * py + pc, 4 * x + 2 * y + c


def _direct_copy(t, k, src_refs, land_refs, send_sems, recv_sems, rows_of, gather):
    dev, peer, me = _peer_of(k)
    m = rows_of[t]
    src = src_refs[t] if gather else src_refs[t].at[pl.ds(peer * m, m), :]
    return pltpu.make_async_remote_copy(
        src_ref=src, dst_ref=land_refs[t].at[pl.ds(me * m, m), :],
        send_sem=send_sems.at[7 * t + k - 1], recv_sem=recv_sems.at[7 * t + k - 1],
        device_id=dev, device_id_type=pl.DeviceIdType.MESH)


def _direct_landing(t, k, src_refs, land_refs, send_sems, recv_sems, rows_of, gather):
    dev, peer, me = _peer_of(k)
    m = rows_of[t]
    src = src_refs[t] if gather else src_refs[t].at[pl.ds(me * m, m), :]
    return pltpu.make_async_remote_copy(
        src_ref=src, dst_ref=land_refs[t].at[pl.ds(peer * m, m), :],
        send_sem=send_sems.at[7 * t + k - 1], recv_sem=recv_sems.at[7 * t + k - 1],
        device_id=dev, device_id_type=pl.DeviceIdType.MESH)


def _direct_start(name, srcs, gather, dep=None):
    nt = len(srcs)
    rows_of = [a.shape[0] if gather else a.shape[0] // N_DEV for a in srcs]
    lands = [pltpu.with_memory_space_constraint(lax.empty((N_DEV * m, a.shape[1]), a.dtype), pltpu.HBM)
             for a, m in zip(srcs, rows_of)]

    n_dep = 0 if dep is None else 1

    def body(*refs):
        src_refs, land_refs = refs[:nt], refs[nt:2 * nt]
        send_sems, recv_sems = refs[2 * nt + n_dep], refs[2 * nt + n_dep + 1]
        token = refs[-1]
        for t in range(nt):
            for k in range(1, N_DEV):
                _direct_copy(t, k, src_refs, land_refs, send_sems, recv_sems, rows_of, gather).start()
        token[...] = jnp.zeros(token.shape, F32)

    out = _pcall(
        body, name=name,
        out_shape=(pltpu.SemaphoreType.DMA((7 * nt,)), pltpu.SemaphoreType.DMA((7 * nt,)),
                   *[pltpu.HBM(a.shape, a.dtype) for a in srcs], *[pltpu.HBM(a.shape, a.dtype) for a in lands],
                   jax.ShapeDtypeStruct((8, 128), F32)),
        in_specs=(_HBM,) * (2 * nt) + (pl.BlockSpec(memory_space=pl.ANY),) * n_dep,
        out_specs=(_SEM, _SEM) + (_HBM,) * (2 * nt) + (pl.BlockSpec(memory_space=pltpu.VMEM),),
        input_output_aliases={i: 2 + i for i in range(2 * nt)},
        compiler_params=pltpu.CompilerParams(has_side_effects=_EFFECT),
    )(*[pltpu.with_memory_space_constraint(a, pltpu.HBM) for a in srcs], *lands, *(() if dep is None else (dep,)))
    return (out[0], out[1], list(out[2:2 + nt]), list(out[2 + nt:2 + 2 * nt]), rows_of, gather), out[-1]


def _direct_wait(name, handle, after):
    send_sems, recv_sems, srcs, lands, rows_of, gather = handle
    nt = len(srcs)
    after = list(after) if isinstance(after, (list, tuple)) else [after]

    def body(*refs):
        src_refs, land_refs = refs[:nt], refs[nt:2 * nt]
        s_sems, r_sems = refs[2 * nt], refs[2 * nt + 1]
        for t in range(nt):
            for k in range(1, N_DEV):
                _direct_copy(t, k, src_refs, land_refs, s_sems, r_sems, rows_of, gather).wait_send()
                _direct_landing(t, k, src_refs, land_refs, s_sems, r_sems, rows_of, gather).wait_recv()

    out = _pcall(
        body, name=name,
        out_shape=tuple(pltpu.HBM(a.shape, a.dtype) for a in srcs) + tuple(pltpu.HBM(a.shape, a.dtype) for a in lands),
        in_specs=(_HBM,) * (2 * nt) + (_SEM, _SEM) + (pl.BlockSpec(memory_space=pl.ANY),) * len(after),
        out_specs=(_HBM,) * (2 * nt),
        input_output_aliases={i: i for i in range(2 * nt)},
        compiler_params=pltpu.CompilerParams(has_side_effects=_EFFECT),
    )(*srcs, *lands, send_sems, recv_sems, *after)
    return list(out[:nt]), list(out[nt:])


def _adamw_sharded(name, own, parts, w, m, v, rb):
    R, N = own.shape

    def body(o_ref, p_ref, w_ref, m_ref, v_ref, g_ref, d_ref, nm_ref, nv_ref):
        me = 4 * lax.axis_index("x") + 2 * lax.axis_index("y") + lax.axis_index("c")
        g = o_ref[...]
        for k in range(1, N_DEV):
            g = g + p_ref[me ^ k].astype(F32)
        nm = ADAM_B1 * m_ref[...] + (1.0 - ADAM_B1) * g
        nv = ADAM_B2 * v_ref[...] + (1.0 - ADAM_B2) * (g * g)
        m_hat = nm / (1.0 - ADAM_B1 ** ADAM_STEP)
        v_hat = nv / (1.0 - ADAM_B2 ** ADAM_STEP)
        g_ref[...] = g
        d_ref[...] = -ADAM_LR * (m_hat / (jnp.sqrt(v_hat) + ADAM_EPS) + ADAM_WD * w_ref[...])
        nm_ref[...] = nm
        nv_ref[...] = nv

    blk = pl.BlockSpec((rb, N), lambda i: (i, 0))
    sh = jax.ShapeDtypeStruct((R, N), F32)
    return _pcall(body, name=name, grid=(R // rb,),
                  in_specs=[blk, pl.BlockSpec((N_DEV, rb, N), lambda i: (0, i, 0)), blk, blk, blk],
                  out_specs=[blk] * 4, out_shape=[sh] * 4, compiler_params=_cparams(1))(own, parts, w, m, v)


def _adamw(name, parts, w, m, v, rb):
    _, R, N = parts.shape

    def body(p_ref, w_ref, m_ref, v_ref, g_ref, d_ref, nm_ref, nv_ref):
        g = p_ref[0]
        for s in range(1, N_DEV):
            g = g + p_ref[s]
        nm = ADAM_B1 * m_ref[...] + (1.0 - ADAM_B1) * g
        nv = ADAM_B2 * v_ref[...] + (1.0 - ADAM_B2) * (g * g)
        m_hat = nm / (1.0 - ADAM_B1 ** ADAM_STEP)
        v_hat = nv / (1.0 - ADAM_B2 ** ADAM_STEP)
        g_ref[...] = g
        d_ref[...] = -ADAM_LR * (m_hat / (jnp.sqrt(v_hat) + ADAM_EPS) + ADAM_WD * w_ref[...])
        nm_ref[...] = nm
        nv_ref[...] = nv

    blk = pl.BlockSpec((rb, N), lambda i: (i, 0))
    sh = jax.ShapeDtypeStruct((R, N), F32)
    return _pcall(body, name=name, grid=(R // rb,), in_specs=[pl.BlockSpec((N_DEV, rb, N), lambda i: (0, i, 0)), blk, blk, blk],
                  out_specs=[blk] * 4, out_shape=[sh] * 4, compiler_params=_cparams(1))(parts, w, m, v)


EARLY = (("w_in", True),)
LATE = (("ffn_w1", True), ("ffn_w3", True), ("w_out", False), ("ffn_w2", False), ("ple_gate_w", False))
MISC = (("s5_glu_w", False), ("rw_w2", True), ("rw_a2", True), ("rw_g2", True), ("ple_up_w", True))
SHARDED_NAMES = tuple(n for n, _ in EARLY + LATE + MISC)
PACK_COLS = 1024
SMALL_ROWS = 144
WEIGHT_NAMES = ("norm_mix", "w_in", "s5_lam_re", "s5_lam_im", "s5_log_step", "s5_b_re", "s5_b_im", "s5_c_re", "s5_c_im", "s5_d",
                "s5_glu_w", "s5_glu_b", "rw_shift_mu", "rw_w0", "rw_w2", "rw_a0", "rw_a2", "rw_g2", "rw_k_k", "rw_k_a", "rw_r_k",
                "rw_ln_w", "rw_ln_b", "w_out", "norm_ffn", "ffn_w1", "ffn_w3", "ffn_w2", "norm_ple", "ple_gate_w", "ple_up_w",
                "final_norm")
SMALL_NAMES = tuple(n for n in WEIGHT_NAMES if n not in SHARDED_NAMES)
ARG_NAMES = ("x", "p") + WEIGHT_NAMES + ("loss_target",) + tuple("m_" + n for n in WEIGHT_NAMES) + tuple("v_" + n for n in WEIGHT_NAMES)


def _travel(a, tr):
    return a.T if tr else a


def _pack_misc(blocks):
    lead = blocks[0].shape[:-2]
    return jnp.concatenate([b.reshape(lead + (-1, PACK_COLS)) for b in blocks], axis=len(lead))


def _unpack_misc(packed, shapes):
    lead = packed.shape[:-2]
    out, off = [], 0
    for r, c in shapes:
        n = r * c // PACK_COLS
        out.append(lax.slice_in_dim(packed, off, off + n, axis=len(lead)).reshape(lead + (r, c)))
        off += n
    return out


def _pack_small(arrs):
    flat = jnp.concatenate([a.reshape(-1).astype(F32) for a in arrs])
    return jnp.pad(flat, (0, SMALL_ROWS * PACK_COLS - flat.shape[0])).reshape(SMALL_ROWS, PACK_COLS)


def _kernel_impl(ins):
    x, p, target = ins["x"][0], ins["p"][0, 0], ins["loss_target"][0]
    me = 4 * lax.axis_index("x") + 2 * lax.axis_index("y") + lax.axis_index("c")
    small = {n: (ins[n] if n == "final_norm" else ins[n][0]) for n in SMALL_NAMES}
    trav = lambda pre, n, tr: _travel(ins[pre + n][0], tr)
    misc_shapes = [trav("", n, tr).shape for n, tr in MISC]

    early = _all_gather("ag_early", [trav("", n, tr).astype(BF16) for n, tr in EARLY]
                        + [_pack_misc([trav("", n, tr).astype(BF16) for n, tr in MISC])])
    late_handle, late_token = _direct_start("ag_late_start", [trav("", n, tr).astype(BF16) for n, tr in LATE], True, early[-1])
    W = dict(small)
    for (n, tr), g in zip(EARLY, early):
        W[n] = _travel(g, tr)
    for (n, tr), g in zip(MISC, _unpack_misc(early[-1].reshape(N_DEV, -1, PACK_COLS), misc_shapes)):
        W[n] = _travel(g.reshape(-1, g.shape[-1]), tr)

    def late_weights(after):
        shards, lands = _direct_wait("ag_late_wait", late_handle, after)
        full = [lax.dynamic_update_slice_in_dim(ld, sh, me * sh.shape[0], axis=0) for ld, sh in zip(lands, shards)]
        return {n: _travel(g, tr) for (n, tr), g in zip(LATE, full)}

    gt = lambda G, n, tr: _travel(G[n], tr)
    started = {}

    def grads_ready(G):
        started["h"], token = _direct_start("grad_late_start", [gt(G, n, tr) for n, tr in LATE], False)
        return token

    loss_part, dx, G = _local_step(x, p, target, W, late_weights, grads_ready, late_token)

    misc_g = _pack_misc([gt(G, n, tr).reshape((N_DEV,) + shp) for (n, tr), shp in zip(MISC, misc_shapes)])
    early_full = [gt(G, n, tr) for n, tr in EARLY] + [misc_g.reshape(-1, PACK_COLS)]
    early_handle, _ = _direct_start("grad_early_start", [a.astype(BF16) for a in early_full], False)
    small_own = _pack_small([G[n] for n in SMALL_NAMES])
    small_handle, small_token = _direct_start("grad_small_start", [small_own], True)
    late_src, late_land = _direct_wait("grad_late_wait", started["h"], small_token)

    outs = {}

    def emit(names_shapes, res):
        for tag, val in zip(("grad_", "delta_", "new_m_", "new_v_"), res):
            for n, v in names_shapes(val):
                outs[tag + n] = v

    def sharded_update(n, tr, src, land):
        rows = src.shape[0] // N_DEV
        own = lax.dynamic_slice_in_dim(src, me * rows, rows, axis=0)
        res = _adamw_sharded("adamw_" + n, own, land.reshape(N_DEV, rows, land.shape[1]),
                             trav("", n, tr), trav("m_", n, tr), trav("v_", n, tr), _pick_rows(rows))
        emit(lambda val: [(n, _travel(val, tr).reshape(ins[n].shape))], res)
        return list(res)

    done = []
    for (n, tr), src, land in zip(LATE, late_src, late_land):
        done += sharded_update(n, tr, src, land)
    _, early_land = _direct_wait("grad_early_wait", early_handle, done)
    for (n, tr), src, land in zip(EARLY, early_full[:-1], early_land[:-1]):
        sharded_update(n, tr, src, land)
    pm = lambda pre: _pack_misc([trav(pre, n, tr) for n, tr in MISC])
    rows = early_full[-1].shape[0] // N_DEV
    res = _adamw_sharded("adamw_misc", lax.dynamic_slice_in_dim(early_full[-1], me * rows, rows, axis=0),
                         early_land[-1].reshape(N_DEV, rows, PACK_COLS), pm(""), pm("m_"), pm("v_"), rows)
    emit(lambda val: [(n, _travel(b, tr).reshape(ins[n].shape)) for (n, tr), b in zip(MISC, _unpack_misc(val, misc_shapes))], res)
    ps = lambda pre: _pack_small([ins[pre + n] for n in SMALL_NAMES])
    small_src, small_land = _direct_wait("grad_small_wait", small_handle, res[0])
    gsm = lax.dynamic_update_slice_in_dim(small_land[0], small_src[0], me * SMALL_ROWS, axis=0)
    res = _adamw("adamw_replicated", gsm.reshape(N_DEV, SMALL_ROWS, PACK_COLS), ps(""), ps("m_"), ps("v_"), SMALL_ROWS)

    def split_small(val):
        flat, off, o = val.reshape(-1), 0, []
        for n in SMALL_NAMES:
            o.append((n, flat[off:off + ins[n].size].reshape(ins[n].shape)))
            off += ins[n].size
        return o

    emit(split_small, res)
    loss = lax.psum(loss_part, MESH_AXES)
    res = [loss, dx[None]]
    for tag in ("grad_", "delta_", "new_m_", "new_v_"):
        res += [outs[tag + n] for n in WEIGHT_NAMES]
    return tuple(res)


def _pick_rows(r):
    best = 8
    for b in range(8, 257, 8):
        if r % b == 0:
            best = b
    return best


def kernel(x, p, norm_mix, w_in, s5_lam_re, s5_lam_im, s5_log_step, s5_b_re, s5_b_im, s5_c_re, s5_c_im, s5_d, s5_glu_w, s5_glu_b, rw_shift_mu, rw_w0, rw_w2, rw_a0, rw_a2, rw_g2, rw_k_k, rw_k_a, rw_r_k, rw_ln_w, rw_ln_b, w_out, norm_ffn, ffn_w1, ffn_w3, ffn_w2, norm_ple, ple_gate_w, ple_up_w, final_norm, loss_target, m_norm_mix, m_w_in, m_s5_lam_re, m_s5_lam_im, m_s5_log_step, m_s5_b_re, m_s5_b_im, m_s5_c_re, m_s5_c_im, m_s5_d, m_s5_glu_w, m_s5_glu_b, m_rw_shift_mu, m_rw_w0, m_rw_w2, m_rw_a0, m_rw_a2, m_rw_g2, m_rw_k_k, m_rw_k_a, m_rw_r_k, m_rw_ln_w, m_rw_ln_b, m_w_out, m_norm_ffn, m_ffn_w1, m_ffn_w3, m_ffn_w2, m_norm_ple, m_ple_gate_w, m_ple_up_w, m_final_norm, v_norm_mix, v_w_in, v_s5_lam_re, v_s5_lam_im, v_s5_log_step, v_s5_b_re, v_s5_b_im, v_s5_c_re, v_s5_c_im, v_s5_d, v_s5_glu_w, v_s5_glu_b, v_rw_shift_mu, v_rw_w0, v_rw_w2, v_rw_a0, v_rw_a2, v_rw_g2, v_rw_k_k, v_rw_k_a, v_rw_r_k, v_rw_ln_w, v_rw_ln_b, v_w_out, v_norm_ffn, v_ffn_w1, v_ffn_w3, v_ffn_w2, v_norm_ple, v_ple_gate_w, v_ple_up_w, v_final_norm):
    return _kernel_impl(dict(zip(ARG_NAMES, (x, p, norm_mix, w_in, s5_lam_re, s5_lam_im, s5_log_step, s5_b_re, s5_b_im, s5_c_re, s5_c_im, s5_d, s5_glu_w, s5_glu_b, rw_shift_mu, rw_w0, rw_w2, rw_a0, rw_a2, rw_g2, rw_k_k, rw_k_a, rw_r_k, rw_ln_w, rw_ln_b, w_out, norm_ffn, ffn_w1, ffn_w3, ffn_w2, norm_ple, ple_gate_w, ple_up_w, final_norm, loss_target, m_norm_mix, m_w_in, m_s5_lam_re, m_s5_lam_im, m_s5_log_step, m_s5_b_re, m_s5_b_im, m_s5_c_re, m_s5_c_im, m_s5_d, m_s5_glu_w, m_s5_glu_b, m_rw_shift_mu, m_rw_w0, m_rw_w2, m_rw_a0, m_rw_a2, m_rw_g2, m_rw_k_k, m_rw_k_a, m_rw_r_k, m_rw_ln_w, m_rw_ln_b, m_w_out, m_norm_ffn, m_ffn_w1, m_ffn_w3, m_ffn_w2, m_norm_ple, m_ple_gate_w, m_ple_up_w, m_final_norm, v_norm_mix, v_w_in, v_s5_lam_re, v_s5_lam_im, v_s5_log_step, v_s5_b_re, v_s5_b_im, v_s5_c_re, v_s5_c_im, v_s5_d, v_s5_glu_w, v_s5_glu_b, v_rw_shift_mu, v_rw_w0, v_rw_w2, v_rw_a0, v_rw_a2, v_rw_g2, v_rw_k_k, v_rw_k_a, v_rw_r_k, v_rw_ln_w, v_rw_ln_b, v_w_out, v_norm_ffn, v_ffn_w1, v_ffn_w3, v_ffn_w2, v_norm_ple, v_ple_gate_w, v_ple_up_w, v_final_norm))))
```

```python
import functools

import jax
import jax.numpy as jnp
from jax import lax
from jax.experimental import pallas as pl
from jax.experimental.pallas import tpu as pltpu

F32 = jnp.float32
BF16 = jnp.bfloat16

D_MODEL = 1024
S5_WIDTH = 512
RW_WIDTH = 512
S5_GROUP = 16
S5_GROUPS = 32
S5_STATE = 64
S5_LANES = S5_GROUPS * S5_STATE
HEAD = 64
SHIFT_COLS = 1792
IN_COLS = 2304
FFN_HIDDEN = 2816
PLE_DIM = 256
RMS_EPS = 1e-6
GN_EPS = 64e-5
L2_EPS = 1e-12
CHUNK = 64
N_DEV = 8

ADAM_LR = 0.001
ADAM_B1 = 0.9
ADAM_B2 = 0.999
ADAM_EPS = 1e-08
ADAM_WD = 0.01
ADAM_STEP = 10

VMEM_LIMIT = 56 * 1024 * 1024
_ANY = pl.BlockSpec(memory_space=pl.ANY)


def _pcall(body, **kw):
    return pl.pallas_call(body, **kw)


def _cparams(n_grid):
    return pltpu.CompilerParams(dimension_semantics=("arbitrary",) * n_grid, vmem_limit_bytes=VMEM_LIMIT)


def _dot(a, b):
    return jnp.dot(a, b, preferred_element_type=F32)


def _dot_nt(a, b):
    return lax.dot_general(a, b, (((1,), (1,)), ((), ())), preferred_element_type=F32)


def _dot_tn(a, b):
    return lax.dot_general(a, b, (((0,), (0,)), ((), ())), preferred_element_type=F32)


def _mmc(w, diff=True, tr=False):
    fw, bw = (_dot_nt, _dot) if tr else (_dot, _dot_nt)
    if not diff:
        return lambda x: fw(x.astype(BF16), w)

    @jax.custom_vjp
    def f(x):
        return fw(x.astype(BF16), w)

    def fwd(x):
        return fw(x.astype(BF16), w), None

    def bwd(_, dy):
        return (bw(dy.astype(BF16), w),)

    f.defvjp(fwd, bwd)
    return f


def _split_dot(x, m, n_split):
    acc = None
    rem = x
    for s in range(n_split):
        part = rem.astype(BF16)
        t = _dot(part, m)
        acc = t if acc is None else acc + t
        if s + 1 < n_split:
            rem = rem - part.astype(F32)
    return acc


def _segsum(m, diff=True):
    if not diff:
        return lambda x: _split_dot(x, m, 2)

    @jax.custom_vjp
    def f(x):
        return _split_dot(x, m, 2)

    def fwd(x):
        return _split_dot(x, m, 2), None

    def bwd(_, dy):
        return (_split_dot(dy, m, 2),)

    f.defvjp(fwd, bwd)
    return f


def _head_indicator(n):
    r = lax.broadcasted_iota(jnp.int32, (n, n), 0) // HEAD
    c = lax.broadcasted_iota(jnp.int32, (n, n), 1) // HEAD
    return (r == c).astype(BF16)


def _rms(x, g):
    return x * lax.rsqrt(jnp.mean(x * x, axis=-1, keepdims=True) + RMS_EPS) * g


def _softplus(x):
    return jnp.maximum(x, 0.0) + jnp.log(1.0 + jnp.exp(-jnp.abs(x)))


def _sigmoid(x):
    return 1.0 / (1.0 + jnp.exp(-x))


def _gelu(x):
    return 0.5 * x * (1.0 + jnp.tanh(0.7978845608028654 * (x + 0.044715 * (x * x * x))))


def _tok_call(name, fn, L, TB, tok_in, const_in, tok_out, acc_out=(), deps=()):
    nb = L // TB
    g8 = TB // 8
    in_specs, args = [], []
    for spec in tok_in:
        if len(spec) == 1:
            arr = spec[0]
            in_specs.append(pl.BlockSpec((arr.shape[0], TB, HEAD), lambda i: (0, i, 0)))
            args.append(arr)
            continue
        arr, width, cb = spec[:3]
        mode = spec[3] if len(spec) > 3 else None
        if mode is None:
            in_specs.append(pl.BlockSpec((TB, width), lambda i, cb=cb: (i, cb)))
        elif mode == "prev":
            in_specs.append(pl.BlockSpec((8, width), lambda i, cb=cb: (jnp.maximum(i * g8 - 1, 0), cb)))
        else:
            in_specs.append(pl.BlockSpec((8, width), lambda i, cb=cb: (jnp.minimum((i + 1) * g8, L // 8 - 1), cb)))
        args.append(arr)
    for c in const_in:
        in_specs.append(pl.BlockSpec(c.shape, lambda i, nd=c.ndim: (0,) * nd, pipeline_mode=pl.Buffered(1)))
        args.append(c)
    for d in deps:
        in_specs.append(pl.BlockSpec(d.shape, lambda i, nd=d.ndim: (0,) * nd))
        args.append(d)
    out_shape, out_specs = [], []
    for width, dt in tok_out:
        if width == "heads":
            out_shape.append(jax.ShapeDtypeStruct((N_HEAD, L, HEAD), dt))
            out_specs.append(pl.BlockSpec((N_HEAD, TB, HEAD), lambda i: (0, i, 0)))
            continue
        out_shape.append(jax.ShapeDtypeStruct((L, width), dt))
        out_specs.append(pl.BlockSpec((TB, width), lambda i: (i, 0)))
    for shp in acc_out:
        out_shape.append(jax.ShapeDtypeStruct(shp, F32))
        out_specs.append(pl.BlockSpec(shp, lambda i, nd=len(shp): (0,) * nd))
    n_tok, n_const, n_to = len(tok_in), len(const_in), len(tok_out)

    def body(*refs):
        i = pl.program_id(0)
        tv = [r[...] if len(r.shape) == 2 else jnp.concatenate([r[h] for h in range(r.shape[0])], axis=1)
              for r in refs[:n_tok]]
        cv = [r[...] for r in refs[n_tok:n_tok + n_const]]
        orefs = refs[n_tok + n_const + len(deps):]
        outs = fn(i, tv, cv)
        for r, v in zip(orefs[:n_to], outs[:n_to]):
            if len(r.shape) == 3:
                for h in range(r.shape[0]):
                    r[h] = v[:, h * HEAD:(h + 1) * HEAD].astype(r.dtype)
            else:
                r[...] = v.astype(r.dtype)
        for r, v in zip(orefs[n_to:], outs[n_to:]):
            @pl.when(i == 0)
            def _(r=r):
                r[...] = jnp.zeros(r.shape, r.dtype)

            r[...] += v

    res = _pcall(body, name=name, grid=(nb,), in_specs=in_specs, out_specs=out_specs, out_shape=out_shape,
                 compiler_params=_cparams(1))(*args)
    return res


def _pick_block(n, cap):
    best = None
    for b in range(128, min(n, cap) + 1, 128):
        if n % b == 0:
            best = b
    return best if best is not None else n


def _mm_tn(name, a, b):
    T, M = a.shape
    N = b.shape[1]
    bm, bn, bt = _pick_block(M, 1536), _pick_block(N, 1536), _pick_block(T, 512)

    def body(a_ref, b_ref, o_ref):
        t = pl.program_id(2)

        @pl.when(t == 0)
        def _():
            o_ref[...] = jnp.zeros(o_ref.shape, F32)

        o_ref[...] += _dot_tn(a_ref[...].astype(BF16), b_ref[...].astype(BF16))

    return _pcall(body, name=name, grid=(M // bm, N // bn, T // bt),
                  in_specs=[pl.BlockSpec((bt, bm), lambda m, n, t: (t, m)), pl.BlockSpec((bt, bn), lambda m, n, t: (t, n))],
                  out_specs=pl.BlockSpec((bm, bn), lambda m, n, t: (m, n)),
                  out_shape=jax.ShapeDtypeStruct((M, N), F32), compiler_params=_cparams(3))(a, b)


def _s5_param_fn(lam_re, lam_im, log_step, bt_re, bt_im):
    dt = jnp.exp(log_step)
    e = jnp.exp(lam_re * dt)
    lb_re = e * jnp.cos(lam_im * dt)
    lb_im = e * jnp.sin(lam_im * dt)
    den = lam_re * lam_re + lam_im * lam_im
    nr, ni = lb_re - 1.0, lb_im
    co_re = (nr * lam_re + ni * lam_im) / den
    co_im = (ni * lam_re - nr * lam_im) / den
    cr, ci = co_re[:, None, :], co_im[:, None, :]
    return lb_re, lb_im, cr * bt_re - ci * bt_im, cr * bt_im + ci * bt_re


def _s5_param_fwd(lam_re, lam_im, log_step, bt_re, bt_im):
    def body(a, b, c, d, e, o1, o2, o3, o4):
        r = _s5_param_fn(a[...], b[...], c[...], d[...], e[...])
        o1[...], o2[...], o3[...], o4[...] = r

    sh = jax.ShapeDtypeStruct
    return _pcall(body, name="s5_param_fwd",
                  out_shape=[sh(lam_re.shape, F32), sh(lam_re.shape, F32), sh(bt_re.shape, F32), sh(bt_re.shape, F32)])(
        lam_re, lam_im, log_step, bt_re, bt_im)


def _s5_param_bwd(lam_re, lam_im, log_step, bt_re, bt_im, d_lb_re, d_lb_im, d_bb_re, d_bb_im):
    def body(a, b, c, d, e, g1, g2, g3, g4, o1, o2, o3, o4, o5):
        _, vjp = jax.vjp(_s5_param_fn, a[...], b[...], c[...], d[...], e[...])
        r = vjp((g1[...], g2[...], g3[...], g4[...]))
        o1[...], o2[...], o3[...], o4[...], o5[...] = r

    sh = jax.ShapeDtypeStruct
    return _pcall(body, name="s5_param_bwd",
                  out_shape=[sh(lam_re.shape, F32), sh(lam_re.shape, F32), sh(log_step.shape, F32),
                             sh(bt_re.shape, F32), sh(bt_re.shape, F32)])(
        lam_re, lam_im, log_step, bt_re, bt_im, d_lb_re, d_lb_im, d_bb_re, d_bb_im)


def _cmul(ar, ai, br, bi):
    return ar * br - ai * bi, ar * bi + ai * br


def _scan_consts(lr, li, reverse):
    n = lr.shape[1]
    sub = lax.broadcasted_iota(jnp.int32, (8, n), 0)
    pows = [(lr, li)]
    for _ in range(7):
        pows.append(_cmul(pows[-1][0], pows[-1][1], lr, li))
    steps = []
    for s in (1, 2, 4):
        m = (sub < 8 - s) if reverse else (sub >= s)
        pr, pi = pows[s - 1]
        steps.append((s, jnp.where(m, jnp.broadcast_to(pr, (8, n)), 0.0), jnp.where(m, jnp.broadcast_to(pi, (8, n)), 0.0)))
    wr = jnp.zeros((8, n), F32)
    wi = jnp.zeros((8, n), F32)
    for r in range(8):
        e = (8 - r) if reverse else (r + 1)
        wr = jnp.where(sub == r, jnp.broadcast_to(pows[e - 1][0], (8, n)), wr)
        wi = jnp.where(sub == r, jnp.broadcast_to(pows[e - 1][1], (8, n)), wi)
    return steps, wr, wi


S5_Q = 4
S5_QL = S5_WIDTH // S5_Q
S5_QS = S5_LANES // S5_Q
S5_NT = S5_LANES // 128
S5_QT = S5_QS // 128


def _s5_power_table(lb_ref, pw_re, pw_im, seg):
    for j in range(S5_NT):
        lr = jnp.broadcast_to(lb_ref[0:1, j * 128:(j + 1) * 128], (8, 128))
        li = jnp.broadcast_to(lb_ref[1:2, j * 128:(j + 1) * 128], (8, 128))

        def step(i, c, lr=lr, li=li, j=j):
            pw_re[j, i] = c[0]
            pw_im[j, i] = c[1]
            return _cmul(c[0], c[1], lr, li)

        lax.fori_loop(0, seg, step, (lr, li))


def _seg_scan(sre, sim, carry, lb_ref, pw_re, pw_im, rows, reverse):
    seg = rows // 8
    sgn = -1.0 if reverse else 1.0
    sub = lax.broadcasted_iota(jnp.int32, (8, 128), 0)
    rows_at = lambda i: pl.ds(pl.multiple_of(i * 8, 8), 8)
    entering = {}
    half_tiles = S5_NT // 2
    for half in range(2):
        tiles = list(range(half * half_tiles, (half + 1) * half_tiles))
        lam8 = [(jnp.broadcast_to(lb_ref[0:1, j * 128:(j + 1) * 128], (8, 128)),
                 sgn * jnp.broadcast_to(lb_ref[1:2, j * 128:(j + 1) * 128], (8, 128))) for j in tiles]

        def p1(ii, c):
            i = (seg - 1 - ii) if reverse else ii
            out = []
            for n, j in enumerate(tiles):
                lr, li = lam8[n]
                cr, ci = c[2 * n], c[2 * n + 1]
                nr = lr * cr - li * ci + sre[j, rows_at(i), :]
                ni = lr * ci + li * cr + sim[j, rows_at(i), :]
                sre[j, rows_at(i), :] = nr
                sim[j, rows_at(i), :] = ni
                out += [nr, ni]
            return tuple(out)

        ends = lax.fori_loop(0, seg, p1, tuple(jnp.zeros((8, 128), F32) for _ in range(2 * len(tiles))))
        cs = []
        for n, j in enumerate(tiles):
            ls = slice(j * 128, (j + 1) * 128)
            steps, wr, wi = _scan_consts(pw_re[j, seg - 1][0:1, :], sgn * pw_im[j, seg - 1][0:1, :], reverse)
            tr, ti = ends[2 * n], ends[2 * n + 1]
            for sft, pr, pi in steps:
                sh = (8 - sft) if reverse else sft
                yr, yi = pltpu.roll(tr, sh, 0), pltpu.roll(ti, sh, 0)
                tr, ti = tr + pr * yr - pi * yi, ti + pr * yi + pi * yr
            cin_r, cin_i = carry[0:1, ls], carry[1:2, ls]
            tr, ti = tr + wr * cin_r - wi * cin_i, ti + wr * cin_i + wi * cin_r
            edge_out, edge_in, sh = (0, 7, 7) if reverse else (7, 0, 1)
            carry[0:1, ls] = tr[edge_out:edge_out + 1, :]
            carry[1:2, ls] = ti[edge_out:edge_out + 1, :]
            cr = jnp.where(sub == edge_in, jnp.broadcast_to(cin_r, (8, 128)), pltpu.roll(tr, sh, 0))
            ci = jnp.where(sub == edge_in, jnp.broadcast_to(cin_i, (8, 128)), pltpu.roll(ti, sh, 0))
            cs += [cr, ci]
            entering[j] = (cr, ci)

        def p2(i, _):
            k = (seg - 1 - i) if reverse else i
            for n, j in enumerate(tiles):
                pr, pi = pw_re[j, k], pw_im[j, k]
                cr, ci = cs[2 * n], cs[2 * n + 1]
                if reverse:
                    sre[j, rows_at(i), :] = sre[j, rows_at(i), :] + pr * cr + pi * ci
                    sim[j, rows_at(i), :] = sim[j, rows_at(i), :] + pr * ci - pi * cr
                else:
                    sre[j, rows_at(i), :] = sre[j, rows_at(i), :] + pr * cr - pi * ci
                    sim[j, rows_at(i), :] = sim[j, rows_at(i), :] + pr * ci + pi * cr
            return 0

        lax.fori_loop(0, seg, p2, 0, unroll=2)
    return entering


class _SegIO:
    def __init__(self, hbm, buf, sems, rows, width, col0=0):
        self.hbm, self.buf, self.sems, self.rows, self.seg, self.width, self.col0 = hbm, buf, sems, rows, rows // 8, width, col0

    def _copies(self, blk, slot, to_vmem):
        out = []
        for r in range(8):
            h = self.hbm.at[pl.ds(blk * self.rows + r * self.seg, self.seg), pl.ds(self.col0, self.width)]
            v = self.buf.at[slot, :, r, :]
            out.append(pltpu.make_async_copy(h, v, self.sems.at[slot, r]) if to_vmem
                       else pltpu.make_async_copy(v, h, self.sems.at[slot, r]))
        return out

    def start(self, blk, slot, to_vmem):
        for cp in self._copies(blk, slot, to_vmem):
            cp.start()

    def wait(self, blk, slot, to_vmem):
        for cp in self._copies(blk, slot, to_vmem):
            cp.wait()

    def value(self, slot):
        return self.buf[slot].reshape(self.rows, self.width)

    def store(self, slot, val):
        self.buf[slot] = val.reshape(self.seg, 8, self.width)


def _seg_pipeline(i, nb, blk_of, ins, outs, compute):
    slot = i % 2

    @pl.when(i == 0)
    def _():
        for io in ins:
            io.start(blk_of(0), 0, True)

    @pl.when(i + 1 < nb)
    def _():
        for io in ins:
            io.start(blk_of(i + 1), 1 - slot, True)

    for io in ins:
        io.wait(blk_of(i), slot, True)

    @pl.when(i >= 2)
    def _():
        for io in outs:
            io.wait(blk_of(i - 2), slot, False)

    compute(slot)
    for io in outs:
        io.start(blk_of(i), slot, False)

    @pl.when(i == nb - 1)
    def _():
        for io in outs:
            if nb >= 2:
                io.wait(blk_of(i - 1), 1 - slot, False)
            io.wait(blk_of(i), slot, False)


def _s5_scan_fwd(proj, bq_re, bq_im, cq_re, cq_im, lbar, dskip, L, TB):
    nb = L // TB
    seg = TB // 8

    def body(u_hbm, bre, bim, cre, cim, lb_ref, d_ref, y_hbm, ck_ref, sre, sim, carry, pw_re, pw_im,
             ubuf, ybuf, sem_u, sem_y):
        i = pl.program_id(0)
        u_io = _SegIO(u_hbm, ubuf, sem_u, TB, S5_WIDTH)
        y_io = _SegIO(y_hbm, ybuf, sem_y, TB, S5_WIDTH)

        @pl.when(i == 0)
        def _():
            carry[...] = jnp.zeros(carry.shape, F32)
            _s5_power_table(lb_ref, pw_re, pw_im, seg)

        ck_ref[0] = carry[...]

        def compute(slot):
            u = u_io.value(slot)
            ub = u.astype(BF16)
            for q in range(S5_Q):
                uq = ub[:, q * S5_QL:(q + 1) * S5_QL]
                vr, vi = _dot(uq, bre[q]), _dot(uq, bim[q])
                for jj in range(S5_QT):
                    sre[q * S5_QT + jj] = vr[:, jj * 128:(jj + 1) * 128]
                    sim[q * S5_QT + jj] = vi[:, jj * 128:(jj + 1) * 128]
            _seg_scan(sre, sim, carry, lb_ref, pw_re, pw_im, TB, False)
            ys = []
            for q in range(S5_Q):
                sl = slice(q * S5_QL, (q + 1) * S5_QL)
                sr = jnp.concatenate([sre[q * S5_QT + jj] for jj in range(S5_QT)], axis=1).astype(BF16)
                si = jnp.concatenate([sim[q * S5_QT + jj] for jj in range(S5_QT)], axis=1).astype(BF16)
                ys.append(_dot(sr, cre[q]) - _dot(si, cim[q]) + u[:, sl] * d_ref[:, sl])
            y_io.store(slot, jnp.concatenate(ys, axis=1))

        _seg_pipeline(i, nb, lambda st: st, [u_io], [y_io], compute)

    full = lambda a: pl.BlockSpec(a.shape, lambda i, nd=a.ndim: (0,) * nd)
    st = pltpu.VMEM((S5_NT, TB, 128), F32)
    pw = pltpu.VMEM((S5_NT, seg, 8, 128), F32)
    io = pltpu.VMEM((2, seg, 8, S5_WIDTH), F32)
    return _pcall(
        body, name="s5_scan_fwd", grid=(nb,),
        in_specs=[_ANY, full(bq_re), full(bq_im), full(cq_re), full(cq_im), full(lbar), full(dskip)],
        out_specs=[_ANY, pl.BlockSpec((1, 8, S5_LANES), lambda i: (i, 0, 0))],
        out_shape=[jax.ShapeDtypeStruct((L, S5_WIDTH), F32), jax.ShapeDtypeStruct((nb, 8, S5_LANES), F32)],
        scratch_shapes=[st, st, pltpu.VMEM((8, S5_LANES), F32), pw, pw, io, io,
                        pltpu.SemaphoreType.DMA((2, 8)), pltpu.SemaphoreType.DMA((2, 8))],
        compiler_params=_cparams(1))(proj, bq_re, bq_im, cq_re, cq_im, lbar, dskip)


def _s5_scan_bwd(proj, dy, ck, bq_re, bq_im, cq_re, cq_im, lbar, dskip, L, TB):
    nb = L // TB
    seg = TB // 8

    def body(u_hbm, dy_hbm, ck_ref, bre, bim, cre, cim, lb_ref, d_ref,
             du_hbm, dbre, dbim, dcre, dcim, dlb_ref, dd_ref, sre, sim, gre, gim, carry, gcarry, pw_re, pw_im,
             ubuf, dybuf, dubuf, sem_u, sem_dy, sem_du):
        i = pl.program_id(0)
        u_io = _SegIO(u_hbm, ubuf, sem_u, TB, S5_WIDTH)
        dy_io = _SegIO(dy_hbm, dybuf, sem_dy, TB, S5_WIDTH)
        du_io = _SegIO(du_hbm, dubuf, sem_du, TB, S5_WIDTH)

        @pl.when(i == 0)
        def _():
            gcarry[...] = jnp.zeros(gcarry.shape, F32)
            dbre[...] = jnp.zeros(dbre.shape, F32)
            dbim[...] = jnp.zeros(dbim.shape, F32)
            dcre[...] = jnp.zeros(dcre.shape, F32)
            dcim[...] = jnp.zeros(dcim.shape, F32)
            dlb_ref[...] = jnp.zeros(dlb_ref.shape, F32)
            dd_ref[...] = jnp.zeros(dd_ref.shape, F32)
            _s5_power_table(lb_ref, pw_re, pw_im, seg)

        def compute(slot):
            u = u_io.value(slot)
            dy_v = dy_io.value(slot)
            ub = u.astype(BF16)
            dyb = dy_v.astype(BF16)
            carry[...] = ck_ref[0]
            for q in range(S5_Q):
                uq = ub[:, q * S5_QL:(q + 1) * S5_QL]
                dq = dyb[:, q * S5_QL:(q + 1) * S5_QL]
                vr, vi = _dot(uq, bre[q]), _dot(uq, bim[q])
                hr, hi = _dot_nt(dq, cre[q]), -_dot_nt(dq, cim[q])
                for jj in range(S5_QT):
                    ls = slice(jj * 128, (jj + 1) * 128)
                    sre[q * S5_QT + jj] = vr[:, ls]
                    sim[q * S5_QT + jj] = vi[:, ls]
                    gre[q * S5_QT + jj] = hr[:, ls]
                    gim[q * S5_QT + jj] = hi[:, ls]
            entering = _seg_scan(sre, sim, carry, lb_ref, pw_re, pw_im, TB, False)
            _seg_scan(gre, gim, gcarry, lb_ref, pw_re, pw_im, TB, True)

            rows_at = lambda k: pl.ds(pl.multiple_of(k * 8, 8), 8)
            for j in range(S5_NT):
                er, ei = entering[j]
                gr0, gi0 = gre[j, rows_at(0), :], gim[j, rows_at(0), :]
                acc0 = (gr0 * er + gi0 * ei, gi0 * er - gr0 * ei)

                def acc_step(k, acc, j=j):
                    gr, gi_ = gre[j, rows_at(k), :], gim[j, rows_at(k), :]
                    spr, spi = sre[j, rows_at(k - 1), :], sim[j, rows_at(k - 1), :]
                    return acc[0] + gr * spr + gi_ * spi, acc[1] - gr * spi + gi_ * spr

                ar, ai = lax.fori_loop(1, seg, acc_step, acc0, unroll=2 if (seg - 1) % 2 == 0 else 1)
                ls = slice(j * 128, (j + 1) * 128)
                dlb_ref[0:1, ls] += jnp.sum(ar, axis=0, keepdims=True)
                dlb_ref[1:2, ls] += jnp.sum(ai, axis=0, keepdims=True)

            dd_ref[...] += jnp.sum(dy_v * u, axis=0, keepdims=True)
            dus = []
            for q in range(S5_Q):
                sl = slice(q * S5_QL, (q + 1) * S5_QL)
                cat = lambda ref: jnp.concatenate([ref[q * S5_QT + jj] for jj in range(S5_QT)], axis=1).astype(BF16)
                grq, giq = cat(gre), cat(gim)
                dus.append(_dot_nt(grq, bre[q]) + _dot_nt(giq, bim[q]) + dy_v[:, sl] * d_ref[:, sl])
                dbre[q] += _dot_tn(ub[:, sl], grq)
                dbim[q] += _dot_tn(ub[:, sl], giq)
                dcre[q] += _dot_tn(cat(sre), dyb[:, sl])
                dcim[q] -= _dot_tn(cat(sim), dyb[:, sl])
            du_io.store(slot, jnp.concatenate(dus, axis=1))

        _seg_pipeline(i, nb, lambda st: nb - 1 - st, [u_io, dy_io], [du_io], compute)

    full = lambda a: pl.BlockSpec(a.shape, lambda i, nd=a.ndim: (0,) * nd)
    sh = jax.ShapeDtypeStruct
    outs = [sh((L, S5_WIDTH), F32), sh(bq_re.shape, F32), sh(bq_im.shape, F32), sh(cq_re.shape, F32), sh(cq_im.shape, F32),
            sh((8, S5_LANES), F32), sh((1, S5_WIDTH), F32)]
    fo = lambda s: pl.BlockSpec(s.shape, lambda i, nd=len(s.shape): (0,) * nd)
    st = pltpu.VMEM((S5_NT, TB, 128), F32)
    pw = pltpu.VMEM((S5_NT, seg, 8, 128), F32)
    io = pltpu.VMEM((2, seg, 8, S5_WIDTH), F32)
    sem = pltpu.SemaphoreType.DMA((2, 8))
    return _pcall(
        body, name="s5_scan_bwd", grid=(nb,),
        in_specs=[_ANY, _ANY, pl.BlockSpec((1, 8, S5_LANES), lambda i: (nb - 1 - i, 0, 0)),
                  full(bq_re), full(bq_im), full(cq_re), full(cq_im), full(lbar), full(dskip)],
        out_specs=[_ANY] + [fo(s) for s in outs[1:]],
        out_shape=outs,
        scratch_shapes=[st] * 4 + [pltpu.VMEM((8, S5_LANES), F32)] * 2 + [pw, pw, io, io, io, sem, sem, sem],
        compiler_params=_cparams(1))(proj, dy, ck, bq_re, bq_im, cq_re, cq_im, lbar, dskip)


N_HEAD = RW_WIDTH // HEAD
_NN = (((2,), (1,)), ((0,), (0,)))
_NT = (((2,), (2,)), ((0,), (0,)))
_TN = (((1,), (1,)), ((0,), (0,)))


def _hi_lo(x):
    h = x.astype(BF16)
    return h, (x - h.astype(F32)).astype(BF16)


def _mm_acc(a, b, dims, passes=3):
    dg = lambda p, q: lax.dot_general(p, q, dims, preferred_element_type=F32)
    if passes == 1:
        return dg(a.astype(BF16), b.astype(BF16))
    ah, al = _hi_lo(a)
    bh, bl = _hi_lo(b)
    return dg(ah, bh) + dg(ah, bl) + dg(al, bh)


def _cumsum_rows(x, transpose):
    h, n, _ = x.shape
    ti = lax.broadcasted_iota(jnp.int32, (h, n, n), 1)
    tj = lax.broadcasted_iota(jnp.int32, (h, n, n), 2)
    m = ((tj >= ti) if transpose else (tj <= ti)).astype(BF16)
    acc, rem = None, x
    for s in range(3):
        part = rem.astype(BF16)
        t = lax.dot_general(m, part, _NN, preferred_element_type=F32)
        acc = t if acc is None else acc + t
        if s < 2:
            rem = rem - part.astype(F32)
    return acc


def _slices(x, axis, sizes):
    out, off = [], 0
    for n in sizes:
        out.append(lax.slice_in_dim(x, off, off + n, axis=axis))
        off += n
    return tuple(out)


def _cat_op(axis, sizes, diff):
    plain = lambda *xs: jnp.concatenate(xs, axis=axis)
    if not diff:
        return plain
    f = jax.custom_vjp(plain)
    f.defvjp(lambda *xs: (plain(*xs), None), lambda _, d: _slices(d, axis, sizes))
    return f


def _split_op(axis, sizes, diff):
    plain = lambda x: _slices(x, axis, sizes)
    if not diff:
        return plain
    f = jax.custom_vjp(plain)
    f.defvjp(lambda x: (plain(x), None), lambda _, d: (jnp.concatenate(d, axis=axis),))
    return f


def _mm_ops(diff, passes):
    mm = lambda a, b, dims: _mm_acc(a, b, dims, passes)
    if not diff:
        return (lambda a, b: mm(a, b, _NN), lambda a, b: mm(a, b, _NT), lambda a, b: mm(a, b, _TN))

    @jax.custom_vjp
    def nn(a, b):
        return mm(a, b, _NN)

    nn.defvjp(lambda a, b: (mm(a, b, _NN), (a, b)), lambda r, d: (mm(d, r[1], _NT), mm(r[0], d, _TN)))

    @jax.custom_vjp
    def nt(a, b):
        return mm(a, b, _NT)

    nt.defvjp(lambda a, b: (mm(a, b, _NT), (a, b)), lambda r, d: (mm(d, r[1], _NN), mm(d, r[0], _TN)))

    @jax.custom_vjp
    def tn(a, b):
        return mm(a, b, _TN)

    tn.defvjp(lambda a, b: (mm(a, b, _TN), (a, b)), lambda r, d: (mm(r[1], d, _NT), mm(r[0], d, _NN)))
    return nn, nt, tn


def _cums_op(diff):
    if not diff:
        return lambda x: _cumsum_rows(x, False)

    @jax.custom_vjp
    def cums(x):
        return _cumsum_rows(x, False)

    cums.defvjp(lambda x: (_cumsum_rows(x, False), None), lambda _, d: (_cumsum_rows(d, True),))
    return cums


WKV_PASSES = (1, 1, 1, 1, 1)


WKV_SUB = 4
WKV_BLOCK = CHUNK * WKV_SUB


def _wkv_block(s0, r, w, k, v, a, b, diff):
    p_pair, p_val, p_solve, p_out, p_state = WKV_PASSES
    cums = _cums_op(diff)
    _, nt_pair, _ = _mm_ops(diff, p_pair)
    nn_val, _, _ = _mm_ops(diff, p_val)
    nn_solve, _, _ = _mm_ops(diff, p_solve)
    nn_out, nt_out, _ = _mm_ops(diff, p_out)
    nn_state, _, tn_state = _mm_ops(diff, p_state)
    h, d, n, sub = s0.shape[0], s0.shape[2], CHUNK, WKV_SUB
    hb = h * sub
    to_chunks = lambda t: _cat_op(0, (h,) * sub, diff)(*_split_op(1, (n,) * sub, diff)(t))
    r, w, k, v, a, b = (to_chunks(t) for t in (r, w, k, v, a, b))
    cat_rows2 = _cat_op(1, (n, n), diff)
    cat_lanes2 = _cat_op(2, (n, n), diff)
    split_rows2 = _split_op(1, (n, n), diff)
    split_lanes2 = _split_op(2, (n, n), diff)
    ti = lax.broadcasted_iota(jnp.int32, (hb, n, n), 1)
    tj = lax.broadcasted_iota(jnp.int32, (hb, n, n), 2)
    incl, strict = tj <= ti, tj < ti
    logw = jnp.log(w)
    cum = cums(logw)
    g_in, g_ex, g_inv = jnp.exp(cum), jnp.exp(cum - logw), jnp.exp(-cum)
    ae, re, bi, ki = a * g_ex, r * g_in, b * g_inv, k * g_inv
    top, bot = split_rows2(nt_pair(cat_rows2(ae, re), cat_rows2(bi, ki)))
    tab, tak = split_lanes2(top)
    qb, qk = split_lanes2(bot)
    tab, tak = jnp.where(strict, tab, 0.0), jnp.where(strict, tak, 0.0)
    qb, qk = jnp.where(incl, qb, 0.0), jnp.where(incl, qk, 0.0)
    tak_v, qk_v = split_rows2(nn_val(cat_rows2(tak, qk), v))
    x = cat_lanes2(ae, tak_v)
    npow = tab
    steps = max(1, (n - 1).bit_length())
    for i in range(steps):
        x = x + nn_solve(npow, x)
        if i + 1 < steps:
            npow = nn_solve(npow, npow)
    ae_m, uc = split_lanes2(x)
    qx = nn_out(qb, x)
    q_ae, q_uc = split_lanes2(qx)
    re_m = re + q_ae
    yc = q_uc + qk_v
    g_end = jnp.exp(jnp.sum(logw, axis=1, keepdims=True))
    bg, kg = bi * g_end, ki * g_end
    tm = tn_state(ae_m, bg)
    sc = tn_state(cat_rows2(uc, v), cat_rows2(bg, kg))
    per_chunk = _split_op(0, (h,) * sub, diff)
    re_m, yc, g_end, tm, sc = (per_chunk(t) for t in (re_m, yc, g_end, tm, sc))
    ys, s = [], s0
    for i in range(sub):
        ys.append(nt_out(re_m[i], s) + yc[i])
        s = s * g_end[i] + nn_state(s, tm[i]) + sc[i]
    return _cat_op(1, (n,) * sub, diff)(*ys), s


def _wkv_fwd(r, w, k, v, a, b, L):
    nc = L // WKV_BLOCK

    def body(r_ref, w_ref, k_ref, v_ref, a_ref, b_ref, y_ref, ck_ref, s_ref):
        c = pl.program_id(0)

        @pl.when(c == 0)
        def _():
            s_ref[...] = jnp.zeros(s_ref.shape, F32)

        s0 = s_ref[...]
        ck_ref[0] = s0
        y, s1 = _wkv_block(s0, r_ref[...], w_ref[...], k_ref[...], v_ref[...], a_ref[...], b_ref[...], False)
        y_ref[...] = y
        s_ref[...] = s1

    blk = pl.BlockSpec((N_HEAD, WKV_BLOCK, HEAD), lambda c: (0, c, 0))
    return _pcall(
        body, name="wkv_fwd", grid=(nc,), in_specs=[blk] * 6,
        out_specs=[blk, pl.BlockSpec((1, N_HEAD, HEAD, HEAD), lambda c: (c, 0, 0, 0))],
        out_shape=[jax.ShapeDtypeStruct((N_HEAD, L, HEAD), F32), jax.ShapeDtypeStruct((nc, N_HEAD, HEAD, HEAD), F32)],
        scratch_shapes=[pltpu.VMEM((N_HEAD, HEAD, HEAD), F32)],
        compiler_params=_cparams(1))(r, w, k, v, a, b)


def _wkv_bwd(r, w, k, v, a, b, dy, ck, L, deps=()):
    nc = L // WKV_BLOCK

    def body(r_ref, w_ref, k_ref, v_ref, a_ref, b_ref, dy_ref, ck_ref, *rest):
        dr_ref, dw_ref, dk_ref, dv_ref, da_ref, db_ref, ds_ref = rest[len(deps):]
        c = pl.program_id(0)

        @pl.when(c == 0)
        def _():
            ds_ref[...] = jnp.zeros(ds_ref.shape, F32)

        _, vjp = jax.vjp(lambda *t: _wkv_block(*t, True), ck_ref[0], r_ref[...], w_ref[...], k_ref[...], v_ref[...],
                         a_ref[...], b_ref[...])
        g = vjp((dy_ref[...], ds_ref[...]))
        ds_ref[...] = g[0]
        for o_ref, val in zip((dr_ref, dw_ref, dk_ref, dv_ref, da_ref, db_ref), g[1:]):
            o_ref[...] = val

    blk = pl.BlockSpec((N_HEAD, WKV_BLOCK, HEAD), lambda c: (0, nc - 1 - c, 0))
    sh = jax.ShapeDtypeStruct((N_HEAD, L, HEAD), F32)
    return _pcall(
        body, name="wkv_bwd", grid=(nc,),
        in_specs=[blk] * 7 + [pl.BlockSpec((1, N_HEAD, HEAD, HEAD), lambda c: (nc - 1 - c, 0, 0, 0))]
        + [pl.BlockSpec(d.shape, lambda c, nd=d.ndim: (0,) * nd) for d in deps],
        out_specs=[blk] * 6, out_shape=[sh] * 6,
        scratch_shapes=[pltpu.VMEM((N_HEAD, HEAD, HEAD), F32)],
        compiler_params=_cparams(1))(r, w, k, v, a, b, dy, ck, *deps)


TB = 256


def _bf(x):
    return x.astype(BF16)


def _inproj_fwd(x, norm_mix, w_in, L, deps=()):
    def fn(i, tv, cv):
        xn = _rms(tv[0], cv[0])
        return _dot(_bf(xn), cv[1]), xn

    return _tok_call("inproj_fwd", fn, L, TB, [(x, D_MODEL, 0)], [norm_mix, w_in], [(IN_COLS, F32), (D_MODEL, BF16)],
                     deps=deps)


def _s5_post_fn(glu_w, wtop, diff=True):
    mg = _mmc(glu_w, diff)
    mt = _mmc(wtop, diff) if wtop is not None else None

    def f(y, glu_b, e):
        z = _gelu(y)
        out = z * _sigmoid(mg(z) + glu_b + e)
        res = mt(out) if mt is not None else out
        return res, (z, out)

    return f


def _s5_post_fwd(y, glu_w, glu_b, L):
    def fn(i, tv, cv):
        out, _ = _s5_post_fn(cv[0], None, False)(tv[0], cv[1], 0.0)
        return (out,)

    return _tok_call("s5_post_fwd", fn, L, TB, [(y, S5_WIDTH, 0)], [glu_w, glu_b], [(S5_WIDTH, F32)])[0]


def _s5_post_bwd(y, dh1, glu_w, glu_b, wtop, L, deps=()):
    def fn(i, tv, cv):
        e0 = jnp.zeros((TB, S5_WIDTH), F32)
        _, vjp, (z, out) = jax.vjp(_s5_post_fn(cv[0], cv[2]), tv[0], cv[1], e0, has_aux=True)
        dy, db, de = vjp(tv[1])
        return dy, z, de, out, db

    return _tok_call("s5_post_bwd", fn, L, TB, [(y, S5_WIDTH, 0), (dh1, D_MODEL, 0)], [glu_w, glu_b, wtop],
                     [(S5_WIDTH, F32), (S5_WIDTH, BF16), (S5_WIDTH, BF16), (S5_WIDTH, BF16)], [(1, S5_WIDTH)], deps=deps)


RW_COLBLK = ((RW_WIDTH, 1), (RW_WIDTH, 2), (RW_WIDTH, 3), (128, 16), (128, 17))
RW_MU = ((0, 512), (512, 1024), (1024, 1536), (1536, 1664), (1664, 1792))


def _rw_pre_fn(w2pad, a2pad, g2, diff=True):
    m_w, m_a, m_g = _mmc(w2pad, diff), _mmc(a2pad, diff), _mmc(g2, diff)
    seg = _segsum(_head_indicator(RW_WIDTH), diff)

    def f(zr, zk, zv, zwa, zg, w0, a0, k_k, k_a, e_w, e_a):
        wl_t = jnp.tanh(zwa)
        wlin = w0 + m_w(wl_t) + e_w
        w = -_softplus(-wlin) - 0.5
        decay = jnp.exp(-jnp.exp(w))
        a = _sigmoid(a0 + m_a(zwa) + e_a)
        sg = _sigmoid(zg)
        g = m_g(sg)
        kk = zk * k_k
        kkn = kk / jnp.maximum(jnp.sqrt(seg(kk * kk)), L2_EPS)
        kf = zk * (1.0 + (a - 1.0) * k_a)
        return (zr, decay, kf, zv, -kkn, kkn * a, g), (wl_t, sg)

    return f


def _rw_shifted(i, tv, mu):
    sub = lax.broadcasted_iota(jnp.int32, (TB, 1), 0)
    zs, dif = [], []
    for n in range(5):
        z = tv[n]
        last = jnp.where(i == 0, 0.0, tv[5 + n][7:8, :])
        prev = jnp.where(sub == 0, last, pltpu.roll(z, 1, 0))
        m = mu[:, RW_MU[n][0]:RW_MU[n][1]]
        zs.append(z + (prev - z) * m)
        dif.append(prev - z)
    return zs, dif


def _rw_tok_in(proj):
    return [(proj, wd, cb) for wd, cb in RW_COLBLK] + [(proj, wd, cb, "prev") for wd, cb in RW_COLBLK]


def _rw_pre_fwd(proj, mu, w0, a0, k_k, k_a, w2pad, a2pad, g2, L):
    def fn(i, tv, cv):
        zs, _ = _rw_shifted(i, tv, cv[0])
        outs, _ = _rw_pre_fn(cv[5], cv[6], cv[7], False)(*zs, cv[1], cv[2], cv[3], cv[4], 0.0, 0.0)
        return outs

    return _tok_call("rw_pre_fwd", fn, L, TB, _rw_tok_in(proj), [mu, w0, a0, k_k, k_a, w2pad, a2pad, g2],
                     [("heads", F32)] * 6 + [(RW_WIDTH, F32)])


def _rw_pre_bwd(proj, cots, mu, w0, a0, k_k, k_a, w2pad, a2pad, g2, L):
    def fn(i, tv, cv):
        zs, dif = _rw_shifted(i, tv[:10], cv[0])
        dr1, dr2, dw, dk1, dk2, dv1, dv2, da, db, dg = tv[10:]
        e0 = jnp.zeros((TB, RW_WIDTH), F32)
        _, vjp, (wl_t, sg) = jax.vjp(_rw_pre_fn(cv[5], cv[6], cv[7]), *zs, cv[1], cv[2], cv[3], cv[4], e0, e0, has_aux=True)
        g = vjp((dr1 + dr2, dw, dk1 + dk2, dv1 + dv2, da, db, dg))
        dzs = jnp.concatenate(g[:5], axis=1)
        dmu = jnp.concatenate([jnp.sum(g[n] * dif[n], axis=0, keepdims=True) for n in range(5)], axis=1)
        return dzs, wl_t, zs[3], sg, g[9], g[10], dmu, g[5], g[6], g[7], g[8]

    tok_in = _rw_tok_in(proj) + [((c,) if c.ndim == 3 else (c, RW_WIDTH, 0)) for c in cots]
    return _tok_call("rw_pre_bwd", fn, L, TB, tok_in, [mu, w0, a0, k_k, k_a, w2pad, a2pad, g2],
                     [(SHIFT_COLS, F32), (128, BF16), (128, BF16), (128, BF16), (RW_WIDTH, BF16), (RW_WIDTH, BF16)],
                     [(1, SHIFT_COLS)] + [(1, RW_WIDTH)] * 4)


def _rw_post_fn(wbot, diff=True):
    seg = _segsum(_head_indicator(RW_WIDTH), diff)
    mb = _mmc(wbot, diff) if wbot is not None else None

    def f(y, r, kf, v, g, ln_w, ln_b, r_k):
        mean = seg(y) * (1.0 / HEAD)
        yc = y - mean
        var = seg(yc * yc) * (1.0 / HEAD)
        yn = yc * lax.rsqrt(var + GN_EPS) * ln_w + ln_b
        bonus = seg(r * kf * r_k) * v
        out = (yn + bonus) * g
        res = mb(out) if mb is not None else out
        return res, out

    return f


def _rw_post_fwd(y, r, kf, v, g, ln_w, ln_b, r_k, L):
    def fn(i, tv, cv):
        out, _ = _rw_post_fn(None, False)(*tv, *cv)
        return (out,)

    return _tok_call("rw_post_fwd", fn, L, TB, [(t,) for t in (y, r, kf, v)] + [(g, RW_WIDTH, 0)], [ln_w, ln_b, r_k],
                     [(RW_WIDTH, F32)])[0]


def _rw_post_bwd(y, r, kf, v, g, dh1, ln_w, ln_b, r_k, wbot, L):
    def fn(i, tv, cv):
        _, vjp, out = jax.vjp(_rw_post_fn(cv[3]), *tv[:5], cv[0], cv[1], cv[2], has_aux=True)
        gr = vjp(tv[5])
        return gr[0], gr[1], gr[2], gr[3], gr[4], out, gr[5], gr[6], gr[7]

    return _tok_call("rw_post_bwd", fn, L, TB, [(t,) for t in (y, r, kf, v)] + [(g, RW_WIDTH, 0), (dh1, D_MODEL, 0)],
                     [ln_w, ln_b, r_k, wbot], [("heads", F32)] + [(RW_WIDTH, F32)] * 4 + [(RW_WIDTH, BF16)], [(1, RW_WIDTH)] * 3)


def _ffn_fn(w1, w3, w2, diff=True):
    m1, m3, m2 = _mmc(w1, diff), _mmc(w3, diff), _mmc(w2, diff)

    def f(h1, norm_ffn, e1, e3):
        hn = _rms(h1, norm_ffn)
        a1 = m1(hn) + e1
        a3 = m3(hn) + e3
        hm = a1 * _sigmoid(a1) * a3
        return h1 + m2(hm), (hn, hm)

    return f


TB_FFN = 256


def _mixffn_fwd(x, s5_out, rw_out, wtop, wbot, norm_ffn, w1, w3, w2, L):
    def fn(i, tv, cv):
        h1 = tv[0] + _dot(_bf(tv[1]), cv[0]) + _dot(_bf(tv[2]), cv[1])
        h2, _ = _ffn_fn(cv[3], cv[4], cv[5], False)(h1, cv[2], 0.0, 0.0)
        return h1, h2

    return _tok_call("mixffn_fwd", fn, L, TB_FFN, [(x, D_MODEL, 0), (s5_out, S5_WIDTH, 0), (rw_out, RW_WIDTH, 0)],
                     [wtop, wbot, norm_ffn, w1, w3, w2], [(D_MODEL, F32), (D_MODEL, F32)])


def _ffn_bwd(h1, dh2, norm_ffn, w1, w3, w2, L):
    def fn(i, tv, cv):
        e0 = jnp.zeros((TB_FFN, FFN_HIDDEN), F32)
        _, vjp, (hn, hm) = jax.vjp(_ffn_fn(cv[1], cv[2], cv[3]), tv[0], cv[0], e0, e0, has_aux=True)
        dh1, dn, d1, d3 = vjp(tv[1])
        return dh1, d1, d3, hm, hn, dn

    return _tok_call("ffn_bwd", fn, L, TB_FFN, [(h1, D_MODEL, 0), (dh2, D_MODEL, 0)], [norm_ffn, w1, w3, w2],
                     [(D_MODEL, F32), (FFN_HIDDEN, BF16), (FFN_HIDDEN, BF16), (FFN_HIDDEN, BF16), (D_MODEL, BF16)],
                     [(1, D_MODEL)])


def _ple_loss_fb(h2, p, target, norm_ple, final_norm, wg, wu, L):
    def fn(i, tv, cv):
        mgate, mup = _mmc(cv[2]), _mmc(cv[3], False)

        def f(h2_, norm_ple_, final_norm_, eg, eu):
            hn = _rms(h2_, norm_ple_)
            gate = _sigmoid(mgate(hn) + eg)
            h3 = h2_ + gate * (mup(tv[1]) + eu)
            out = _rms(h3, final_norm_)
            d = out - tv[2]
            return 0.5 * jnp.sum(jnp.mean(d * d, axis=-1, keepdims=True)), hn

        e0 = jnp.zeros((TB, D_MODEL), F32)
        loss, vjp, hn = jax.vjp(f, tv[0], cv[0], cv[1], e0, e0, has_aux=True)
        dh2, dnp, dfn, deg, deu = vjp(jnp.ones((), F32))
        return dh2, dh2, deg, deu, hn, jnp.full((8, 128), loss, F32), dnp, dfn

    return _tok_call("ple_loss_fb", fn, L, TB, [(h2, D_MODEL, 0), (p, PLE_DIM, 0), (target, D_MODEL, 0)],
                     [norm_ple, final_norm, wg, wu],
                     [(D_MODEL, F32), (D_MODEL, BF16), (D_MODEL, BF16), (D_MODEL, BF16), (D_MODEL, BF16)],
                     [(8, 128), (1, D_MODEL), (1, D_MODEL)])


def _inproj_bwd(x, dh1, du, dzs, norm_mix, mu, w_u, w_z, L):
    nb = L // TB

    def fn(i, tv, cv):
        sub = lax.broadcasted_iota(jnp.int32, (TB, 1), 0)
        m = cv[1]
        b = tv[3] * m
        nxt = jnp.where(i == nb - 1, 0.0, tv[4][0:1, :] * m)
        dz = tv[3] * (1.0 - m) + jnp.where(sub == TB - 1, nxt, pltpu.roll(b, TB - 1, 0))
        dub, dzb = _bf(tv[2]), _bf(dz)
        dxn = _dot_nt(dub, cv[2]) + _dot_nt(dzb, cv[3])
        _, vjp = jax.vjp(_rms, tv[0], cv[0])
        dx, dn = vjp(dxn)
        return tv[1] + dx, jnp.concatenate([dub, dzb], axis=1), dn

    return _tok_call("inproj_bwd", fn, L, TB,
                     [(x, D_MODEL, 0), (dh1, D_MODEL, 0), (du, S5_WIDTH, 0), (dzs, SHIFT_COLS, 0), (dzs, SHIFT_COLS, 0, "next")],
                     [norm_mix, mu, w_u, w_z], [(D_MODEL, F32), (IN_COLS, BF16)], [(1, D_MODEL)])


def _eye8(dt):
    return jnp.eye(8, dtype=dt)


def _quarter_b(bb):
    return jnp.einsum("hg,qgcp->qhcgp", _eye8(bb.dtype), bb.reshape(S5_Q, 8, S5_GROUP, S5_STATE)).reshape(S5_Q, S5_QL, S5_QS)


def _unquarter_b(d):
    return jnp.einsum("qhcgp,hg->qgcp", d.reshape(S5_Q, 8, S5_GROUP, 8, S5_STATE), _eye8(d.dtype)).reshape(
        S5_GROUPS, S5_GROUP, S5_STATE)


def _quarter_c(c):
    return jnp.einsum("gh,qgcp->qgphc", _eye8(c.dtype), c.reshape(S5_Q, 8, S5_GROUP, S5_STATE)).reshape(S5_Q, S5_QS, S5_QL)


def _unquarter_c(d):
    return jnp.einsum("qgphc,gh->qgcp", d.reshape(S5_Q, 8, S5_STATE, 8, S5_GROUP), _eye8(d.dtype)).reshape(
        S5_GROUPS, S5_GROUP, S5_STATE)


def _local_step(x, p, target, W, late_weights=None, grads_ready=None, first_dep=None):
    L = x.shape[0]
    r2 = lambda v: v.reshape(1, -1)
    w_in = W["w_in"]
    w2pad = jnp.pad(W["rw_w2"], ((0, 64), (0, 0)))
    a2pad = jnp.pad(W["rw_a2"], ((64, 0), (0, 0)))
    mu = r2(W["rw_shift_mu"])
    rw_vec = [r2(W[n]) for n in ("rw_w0", "rw_a0", "rw_k_k", "rw_k_a")]
    ln_w, ln_b, r_k = r2(W["rw_ln_w"]), r2(W["rw_ln_b"]), r2(W["rw_r_k"])

    lam_re, lam_im = W["s5_lam_re"], W["s5_lam_im"]
    log_step = W["s5_log_step"].reshape(S5_GROUPS, 1)
    bt_re, bt_im = W["s5_b_re"].transpose(0, 2, 1), W["s5_b_im"].transpose(0, 2, 1)
    lb_re, lb_im, bb_re, bb_im = _s5_param_fwd(lam_re, lam_im, log_step, bt_re, bt_im)
    bq_re, bq_im = _quarter_b(bb_re).astype(BF16), _quarter_b(bb_im).astype(BF16)
    cq_re, cq_im = _quarter_c(W["s5_c_re"]).astype(BF16), _quarter_c(W["s5_c_im"]).astype(BF16)
    lbar = jnp.concatenate([lb_re.reshape(1, -1), lb_im.reshape(1, -1), jnp.zeros((6, S5_LANES), F32)], axis=0)
    dskip = r2(W["s5_d"])
    glu_b = r2(W["s5_glu_b"])
    norm_mix, norm_ffn, norm_ple, final_norm = (r2(W[n]) for n in ("norm_mix", "norm_ffn", "norm_ple", "final_norm"))

    proj, xn = _inproj_fwd(x, norm_mix, w_in, L, () if first_dep is None else (first_dep,))
    y_s5, ck5 = _s5_scan_fwd(proj, bq_re, bq_im, cq_re, cq_im, lbar, dskip, L, TB)
    s5_out = _s5_post_fwd(y_s5, W["s5_glu_w"], glu_b, L)
    r, wd, kf, v, a_s, b_s, g = _rw_pre_fwd(proj, mu, *rw_vec, w2pad, a2pad, W["rw_g2"], L)
    scan_in = (r, wd, kf, v, a_s, b_s)
    y_wkv, ckw = _wkv_fwd(*scan_in, L)
    rw_out = _rw_post_fwd(y_wkv, r, kf, v, g, ln_w, ln_b, r_k, L)
    if late_weights is not None:
        W = dict(W, **late_weights(rw_out))
    wtop, wbot = W["w_out"][:S5_WIDTH], W["w_out"][S5_WIDTH:]
    h1, h2 = _mixffn_fwd(x, s5_out, rw_out, wtop, wbot, norm_ffn, W["ffn_w1"], W["ffn_w3"], W["ffn_w2"], L)

    G = {}
    dh2, dh2_bf, deg, deu, hn_ple, loss_acc, G["norm_ple"], G["final_norm"] = _ple_loss_fb(
        h2, p, target, norm_ple, final_norm, W["ple_gate_w"], W["ple_up_w"], L)
    dh1, da1, da3, hm, hn_ffn, G["norm_ffn"] = _ffn_bwd(h1, dh2, norm_ffn, W["ffn_w1"], W["ffn_w3"], W["ffn_w2"], L)
    G["ffn_w1"] = _mm_tn("dw_ffn_w1", hn_ffn, da1)
    G["ffn_w3"] = _mm_tn("dw_ffn_w3", hn_ffn, da3)
    G["ffn_w2"] = _mm_tn("dw_ffn_w2", hm, dh2_bf)
    G["ple_gate_w"] = _mm_tn("dw_ple_gate", hn_ple, deg)
    dep_a = grads_ready(0, G) if grads_ready is not None else None
    dy_s5, z_bf, dgp, s5o_bf, G["s5_glu_b"] = _s5_post_bwd(y_s5, dh1, W["s5_glu_w"], glu_b, wtop, L,
                                                           () if dep_a is None else (dep_a,))
    dy_wkv, dr2, dk2, dv2, dg, rwo_bf, G["rw_ln_w"], G["rw_ln_b"], G["rw_r_k"] = _rw_post_bwd(
        y_wkv, r, kf, v, g, dh1, ln_w, ln_b, r_k, wbot, L)
    G["w_out"] = jnp.concatenate([_mm_tn("dw_out_top", s5o_bf, dh1), _mm_tn("dw_out_bot", rwo_bf, dh1)], axis=0)
    dep = grads_ready(1, G) if grads_ready is not None else None
    G["ple_up_w"] = _mm_tn("dw_ple_up", p, deu)
    G["s5_glu_w"] = _mm_tn("dw_s5_glu", z_bf, dgp)
    dr1, dwd, dk1, dv1, da_s, db_s = _wkv_bwd(*scan_in, dy_wkv, ckw, L, () if dep is None else (dep,))
    (dzs, wl_t, zwa, sg, dwlin, dalin, G["rw_shift_mu"], G["rw_w0"], G["rw_a0"], G["rw_k_k"], G["rw_k_a"]) = _rw_pre_bwd(
        proj, (dr1, dr2, dwd, dk1, dk2, dv1, dv2, da_s, db_s, dg), mu, *rw_vec, w2pad, a2pad, W["rw_g2"], L)
    G["rw_w2"] = _mm_tn("dw_rw_w2", wl_t, dwlin)[:64]
    G["rw_a2"] = _mm_tn("dw_rw_a2", zwa, dalin)[64:]
    G["rw_g2"] = _mm_tn("dw_rw_g2", sg, dg)
    du, dbq_re, dbq_im, dcq_re, dcq_im, dlbar, G["s5_d"] = _s5_scan_bwd(
        proj, dy_s5, ck5, bq_re, bq_im, cq_re, cq_im, lbar, dskip, L, TB)
    G["s5_c_re"], G["s5_c_im"] = _unquarter_c(dcq_re), _unquarter_c(dcq_im)
    d_lam_re, d_lam_im, d_ls, d_bt_re, d_bt_im = _s5_param_bwd(
        lam_re, lam_im, log_step, bt_re, bt_im, dlbar[0].reshape(S5_GROUPS, S5_STATE), dlbar[1].reshape(S5_GROUPS, S5_STATE),
        _unquarter_b(dbq_re), _unquarter_b(dbq_im))
    G["s5_lam_re"], G["s5_lam_im"], G["s5_log_step"] = d_lam_re, d_lam_im, d_ls.reshape(S5_GROUPS)
    G["s5_b_re"], G["s5_b_im"] = d_bt_re.transpose(0, 2, 1), d_bt_im.transpose(0, 2, 1)
    dx, dproj, G["norm_mix"] = _inproj_bwd(x, dh1, du, dzs, norm_mix, mu, w_in[:, :S5_WIDTH], w_in[:, S5_WIDTH:], L)
    G["w_in"] = _mm_tn("dw_in", xn, dproj)
    return loss_acc[0, 0], dx, G


MESH_AXES = ("x", "y", "c")


def _all_gather(name, shards):
    nt = len(shards)

    def body(*refs):
        x_refs, out_refs = refs[:nt], refs[nt:2 * nt]
        send_sems, recv_sems, local_sems = refs[2 * nt:]
        x, y, c = lax.axis_index("x"), lax.axis_index("y"), lax.axis_index("c")
        me, sibling = (x, y, c), (x, y, 1 - c)
        chips = [(1 - x, y), (x, 1 - y), (1 - x, 1 - y)]

        def rows(t, px, py, pc):
            m_per = shards[t].shape[0]
            return out_refs[t].at[pl.ds((4 * px + 2 * py + pc) * m_per, m_per), :]

        def copy(t, k, block, to, src=None):
            return pltpu.make_async_remote_copy(
                src_ref=rows(t, *block) if src is None else src, dst_ref=rows(t, *block),
                send_sem=send_sems.at[7 * t + k], recv_sem=recv_sems.at[7 * t + k],
                device_id=to, device_id_type=pl.DeviceIdType.MESH)

        mine = [pltpu.make_async_copy(x_refs[t], rows(t, *me), local_sems.at[t]) for t in range(nt)]
        for cp in mine:
            cp.start()
        first = []
        for t in range(nt):
            first.append(copy(t, 0, me, sibling, src=x_refs[t]))
            first += [copy(t, 1 + j, me, (*chip, c), src=x_refs[t]) for j, chip in enumerate(chips)]
        for cp in first:
            cp.start()
        passed = []
        for t in range(nt):
            for j, chip in enumerate(chips):
                copy(t, 1 + j, (*chip, c), me).wait_recv()
                fwd = copy(t, 4 + j, (*chip, c), sibling)
                fwd.start()
                passed.append(fwd)
        for t in range(nt):
            copy(t, 0, sibling, me).wait_recv()
            for j, chip in enumerate(chips):
                copy(t, 4 + j, (*chip, 1 - c), me).wait_recv()
        for cp in first + passed:
            cp.wait_send()
        for cp in mine:
            cp.wait()

    return _pcall(body, name=name,
                  out_shape=[jax.ShapeDtypeStruct((N_DEV * a.shape[0], a.shape[1]), a.dtype) for a in shards],
                  in_specs=[_ANY] * nt, out_specs=[_ANY] * nt,
                  scratch_shapes=[pltpu.SemaphoreType.DMA((7 * nt,)), pltpu.SemaphoreType.DMA((7 * nt,)),
                                  pltpu.SemaphoreType.DMA((nt,))])(*shards)


_HBM = pl.BlockSpec(memory_space=pltpu.HBM)
_SEM = pl.BlockSpec(memory_space=pltpu.SEMAPHORE)
_EFFECT = pltpu.SideEffectType.DATAFLOW_SIDE_EFFECTING


def _peer_of(k):
    x, y, c = lax.axis_index("x"), lax.axis_index("y"), lax.axis_index("c")
    px, py, pc = x ^ ((k >> 2) & 1), y ^ ((k >> 1) & 1), c ^ (k & 1)
    return (px, py, pc), 4 * px + 2 * py + pc, 4 * x + 2 * y + c


def _direct_copy(t, k, src_refs, land_refs, send_sems, recv_sems, rows_of, gather):
    dev, peer, me = _peer_of(k)
    m = rows_of[t]
    src = src_refs[t] if gather else src_refs[t].at[pl.ds(peer * m, m), :]
    return pltpu.make_async_remote_copy(
        src_ref=src, dst_ref=land_refs[t].at[pl.ds(me * m, m), :],
        send_sem=send_sems.at[7 * t + k - 1], recv_sem=recv_sems.at[7 * t + k - 1],
        device_id=dev, device_id_type=pl.DeviceIdType.MESH)


def _direct_landing(t, k, src_refs, land_refs, send_sems, recv_sems, rows_of, gather):
    dev, peer, me = _peer_of(k)
    m = rows_of[t]
    src = src_refs[t] if gather else src_refs[t].at[pl.ds(me * m, m), :]
    return pltpu.make_async_remote_copy(
        src_ref=src, dst_ref=land_refs[t].at[pl.ds(peer * m, m), :],
        send_sem=send_sems.at[7 * t + k - 1], recv_sem=recv_sems.at[7 * t + k - 1],
        device_id=dev, device_id_type=pl.DeviceIdType.MESH)


def _direct_start(name, srcs, gather, dep=None):
    nt = len(srcs)
    rows_of = [a.shape[0] if gather else a.shape[0] // N_DEV for a in srcs]
    lands = [pltpu.with_memory_space_constraint(lax.empty((N_DEV * m, a.shape[1]), a.dtype), pltpu.HBM)
             for a, m in zip(srcs, rows_of)]

    n_dep = 0 if dep is None else 1

    def body(*refs):
        src_refs, land_refs = refs[:nt], refs[nt:2 * nt]
        send_sems, recv_sems = refs[2 * nt + n_dep], refs[2 * nt + n_dep + 1]
        token = refs[-1]
        for t in range(nt):
            for k in range(1, N_DEV):
                _direct_copy(t, k, src_refs, land_refs, send_sems, recv_sems, rows_of, gather).start()
        token[...] = jnp.zeros(token.shape, F32)

    out = _pcall(
        body, name=name,
        out_shape=(pltpu.SemaphoreType.DMA((7 * nt,)), pltpu.SemaphoreType.DMA((7 * nt,)),
                   *[pltpu.HBM(a.shape, a.dtype) for a in srcs], *[pltpu.HBM(a.shape, a.dtype) for a in lands],
                   jax.ShapeDtypeStruct((8, 128), F32)),
        in_specs=(_HBM,) * (2 * nt) + (pl.BlockSpec(memory_space=pl.ANY),) * n_dep,
        out_specs=(_SEM, _SEM) + (_HBM,) * (2 * nt) + (pl.BlockSpec(memory_space=pltpu.VMEM),),
        input_output_aliases={i: 2 + i for i in range(2 * nt)},
        compiler_params=pltpu.CompilerParams(has_side_effects=_EFFECT),
    )(*[pltpu.with_memory_space_constraint(a, pltpu.HBM) for a in srcs], *lands, *(() if dep is None else (dep,)))
    return (out[0], out[1], list(out[2:2 + nt]), list(out[2 + nt:2 + 2 * nt]), rows_of, gather), out[-1]


def _direct_wait(name, handle, after):
    send_sems, recv_sems, srcs, lands, rows_of, gather = handle
    nt = len(srcs)
    after = list(after) if isinstance(after, (list, tuple)) else [after]

    def body(*refs):
        src_refs, land_refs = refs[:nt], refs[nt:2 * nt]
        s_sems, r_sems = refs[2 * nt], refs[2 * nt + 1]
        for t in range(nt):
            for k in range(1, N_DEV):
                _direct_copy(t, k, src_refs, land_refs, s_sems, r_sems, rows_of, gather).wait_send()
                _direct_landing(t, k, src_refs, land_refs, s_sems, r_sems, rows_of, gather).wait_recv()

    out = _pcall(
        body, name=name,
        out_shape=tuple(pltpu.HBM(a.shape, a.dtype) for a in srcs) + tuple(pltpu.HBM(a.shape, a.dtype) for a in lands),
        in_specs=(_HBM,) * (2 * nt) + (_SEM, _SEM) + (pl.BlockSpec(memory_space=pl.ANY),) * len(after),
        out_specs=(_HBM,) * (2 * nt),
        input_output_aliases={i: i for i in range(2 * nt)},
        compiler_params=pltpu.CompilerParams(has_side_effects=_EFFECT),
    )(*srcs, *lands, send_sems, recv_sems, *after)
    return list(out[:nt]), list(out[nt:])


def _adamw_sharded(name, own, parts, w, m, v, rb):
    R, N = own.shape

    def body(o_ref, p_ref, w_ref, m_ref, v_ref, g_ref, d_ref, nm_ref, nv_ref):
        me = 4 * lax.axis_index("x") + 2 * lax.axis_index("y") + lax.axis_index("c")
        g = o_ref[...]
        for k in range(1, N_DEV):
            g = g + p_ref[me ^ k].astype(F32)
        nm = ADAM_B1 * m_ref[...] + (1.0 - ADAM_B1) * g
        nv = ADAM_B2 * v_ref[...] + (1.0 - ADAM_B2) * (g * g)
        m_hat = nm / (1.0 - ADAM_B1 ** ADAM_STEP)
        v_hat = nv / (1.0 - ADAM_B2 ** ADAM_STEP)
        g_ref[...] = g
        d_ref[...] = -ADAM_LR * (m_hat / (jnp.sqrt(v_hat) + ADAM_EPS) + ADAM_WD * w_ref[...])
        nm_ref[...] = nm
        nv_ref[...] = nv

    blk = pl.BlockSpec((rb, N), lambda i: (i, 0))
    sh = jax.ShapeDtypeStruct((R, N), F32)
    return _pcall(body, name=name, grid=(R // rb,),
                  in_specs=[blk, pl.BlockSpec((N_DEV, rb, N), lambda i: (0, i, 0)), blk, blk, blk],
                  out_specs=[blk] * 4, out_shape=[sh] * 4, compiler_params=_cparams(1))(own, parts, w, m, v)


def _adamw(name, parts, w, m, v, rb):
    _, R, N = parts.shape

    def body(p_ref, w_ref, m_ref, v_ref, g_ref, d_ref, nm_ref, nv_ref):
        g = p_ref[0]
        for s in range(1, N_DEV):
            g = g + p_ref[s]
        nm = ADAM_B1 * m_ref[...] + (1.0 - ADAM_B1) * g
        nv = ADAM_B2 * v_ref[...] + (1.0 - ADAM_B2) * (g * g)
        m_hat = nm / (1.0 - ADAM_B1 ** ADAM_STEP)
        v_hat = nv / (1.0 - ADAM_B2 ** ADAM_STEP)
        g_ref[...] = g
        d_ref[...] = -ADAM_LR * (m_hat / (jnp.sqrt(v_hat) + ADAM_EPS) + ADAM_WD * w_ref[...])
        nm_ref[...] = nm
        nv_ref[...] = nv

    blk = pl.BlockSpec((rb, N), lambda i: (i, 0))
    sh = jax.ShapeDtypeStruct((R, N), F32)
    return _pcall(body, name=name, grid=(R // rb,), in_specs=[pl.BlockSpec((N_DEV, rb, N), lambda i: (0, i, 0)), blk, blk, blk],
                  out_specs=[blk] * 4, out_shape=[sh] * 4, compiler_params=_cparams(1))(parts, w, m, v)


EARLY = (("w_in", True),)
LATE = (("ffn_w1", True), ("ffn_w3", True), ("ffn_w2", False), ("ple_gate_w", False), ("w_out", False))
GRAD_STAGES = (LATE[:4], LATE[4:])
MISC = (("s5_glu_w", False), ("rw_w2", True), ("rw_a2", True), ("rw_g2", True), ("ple_up_w", True))
SHARDED_NAMES = tuple(n for n, _ in EARLY + LATE + MISC)
PACK_COLS = 1024
SMALL_ROWS = 144
WEIGHT_NAMES = ("norm_mix", "w_in", "s5_lam_re", "s5_lam_im", "s5_log_step", "s5_b_re", "s5_b_im", "s5_c_re", "s5_c_im", "s5_d",
                "s5_glu_w", "s5_glu_b", "rw_shift_mu", "rw_w0", "rw_w2", "rw_a0", "rw_a2", "rw_g2", "rw_k_k", "rw_k_a", "rw_r_k",
                "rw_ln_w", "rw_ln_b", "w_out", "norm_ffn", "ffn_w1", "ffn_w3", "ffn_w2", "norm_ple", "ple_gate_w", "ple_up_w",
                "final_norm")
SMALL_NAMES = tuple(n for n in WEIGHT_NAMES if n not in SHARDED_NAMES)
ARG_NAMES = ("x", "p") + WEIGHT_NAMES + ("loss_target",) + tuple("m_" + n for n in WEIGHT_NAMES) + tuple("v_" + n for n in WEIGHT_NAMES)


def _travel(a, tr):
    return a.T if tr else a


def _pack_misc(blocks):
    lead = blocks[0].shape[:-2]
    return jnp.concatenate([b.reshape(lead + (-1, PACK_COLS)) for b in blocks], axis=len(lead))


def _unpack_misc(packed, shapes):
    lead = packed.shape[:-2]
    out, off = [], 0
    for r, c in shapes:
        n = r * c // PACK_COLS
        out.append(lax.slice_in_dim(packed, off, off + n, axis=len(lead)).reshape(lead + (r, c)))
        off += n
    return out


def _pack_small(arrs):
    flat = jnp.concatenate([a.reshape(-1).astype(F32) for a in arrs])
    return jnp.pad(flat, (0, SMALL_ROWS * PACK_COLS - flat.shape[0])).reshape(SMALL_ROWS, PACK_COLS)


def _kernel_impl(ins):
    x, p, target = ins["x"][0], ins["p"][0, 0], ins["loss_target"][0]
    me = 4 * lax.axis_index("x") + 2 * lax.axis_index("y") + lax.axis_index("c")
    small = {n: (ins[n] if n == "final_norm" else ins[n][0]) for n in SMALL_NAMES}
    trav = lambda pre, n, tr: _travel(ins[pre + n][0], tr)
    misc_shapes = [trav("", n, tr).shape for n, tr in MISC]

    early = _all_gather("ag_early", [trav("", n, tr).astype(BF16) for n, tr in EARLY]
                        + [_pack_misc([trav("", n, tr).astype(BF16) for n, tr in MISC])])
    late_handle, late_token = _direct_start("ag_late_start", [trav("", n, tr).astype(BF16) for n, tr in LATE], True, early[-1])
    W = dict(small)
    for (n, tr), g in zip(EARLY, early):
        W[n] = _travel(g, tr)
    for (n, tr), g in zip(MISC, _unpack_misc(early[-1].reshape(N_DEV, -1, PACK_COLS), misc_shapes)):
        W[n] = _travel(g.reshape(-1, g.shape[-1]), tr)

    def late_weights(after):
        shards, lands = _direct_wait("ag_late_wait", late_handle, after)
        full = [lax.dynamic_update_slice_in_dim(ld, sh, me * sh.shape[0], axis=0) for ld, sh in zip(lands, shards)]
        return {n: _travel(g, tr) for (n, tr), g in zip(LATE, full)}

    gt = lambda G, n, tr: _travel(G[n], tr)
    started = {}

    def grads_ready(stage, G):
        full = [gt(G, n, tr) for n, tr in GRAD_STAGES[stage]]
        started[stage] = (full, *_direct_start("grad_late_start%d" % stage, [a.astype(BF16) for a in full], False))
        return started[stage][2]

    loss_part, dx, G = _local_step(x, p, target, W, late_weights, grads_ready, late_token)

    misc_g = _pack_misc([gt(G, n, tr).reshape((N_DEV,) + shp) for (n, tr), shp in zip(MISC, misc_shapes)])
    early_full = [gt(G, n, tr) for n, tr in EARLY] + [misc_g.reshape(-1, PACK_COLS)]
    early_handle, _ = _direct_start("grad_early_start", [a.astype(BF16) for a in early_full], False)
    small_own = _pack_small([G[n] for n in SMALL_NAMES])
    small_handle, small_token = _direct_start("grad_small_start", [small_own], True)
    late_src, late_land = [], []
    for stage in range(len(GRAD_STAGES)):
        full, handle, _ = started[stage]
        _, land = _direct_wait("grad_late_wait%d" % stage, handle, small_token)
        late_src += full
        late_land += land

    outs = {}

    def emit(names_shapes, res):
        for tag, val in zip(("grad_", "delta_", "new_m_", "new_v_"), res):
            for n, v in names_shapes(val):
                outs[tag + n] = v

    def sharded_update(n, tr, src, land):
        rows = src.shape[0] // N_DEV
        own = lax.dynamic_slice_in_dim(src, me * rows, rows, axis=0)
        res = _adamw_sharded("adamw_" + n, own, land.reshape(N_DEV, rows, land.shape[1]),
                             trav("", n, tr), trav("m_", n, tr), trav("v_", n, tr), _pick_rows(rows))
        emit(lambda val: [(n, _travel(val, tr).reshape(ins[n].shape))], res)
        return list(res)

    done = []
    for (n, tr), src, land in zip(LATE, late_src, late_land):
        done += sharded_update(n, tr, src, land)
    _, early_land = _direct_wait("grad_early_wait", early_handle, done)
    for (n, tr), src, land in zip(EARLY, early_full[:-1], early_land[:-1]):
        sharded_update(n, tr, src, land)
    pm = lambda pre: _pack_misc([trav(pre, n, tr) for n, tr in MISC])
    rows = early_full[-1].shape[0] // N_DEV
    res = _adamw_sharded("adamw_misc", lax.dynamic_slice_in_dim(early_full[-1], me * rows, rows, axis=0),
                         early_land[-1].reshape(N_DEV, rows, PACK_COLS), pm(""), pm("m_"), pm("v_"), rows)
    emit(lambda val: [(n, _travel(b, tr).reshape(ins[n].shape)) for (n, tr), b in zip(MISC, _unpack_misc(val, misc_shapes))], res)
    ps = lambda pre: _pack_small([ins[pre + n] for n in SMALL_NAMES])
    small_src, small_land = _direct_wait("grad_small_wait", small_handle, res[0])
    gsm = lax.dynamic_update_slice_in_dim(small_land[0], small_src[0], me * SMALL_ROWS, axis=0)
    res = _adamw("adamw_replicated", gsm.reshape(N_DEV, SMALL_ROWS, PACK_COLS), ps(""), ps("m_"), ps("v_"), SMALL_ROWS)

    def split_small(val):
        flat, off, o = val.reshape(-1), 0, []
        for n in SMALL_NAMES:
            o.append((n, flat[off:off + ins[n].size].reshape(ins[n].shape)))
            off += ins[n].size
        return o

    emit(split_small, res)
    loss = lax.psum(loss_part, MESH_AXES)
    res = [loss, dx[None]]
    for tag in ("grad_", "delta_", "new_m_", "new_v_"):
        res += [outs[tag + n] for n in WEIGHT_NAMES]
    return tuple(res)


def _pick_rows(r):
    best = 8
    for b in range(8, 257, 8):
        if r % b == 0:
            best = b
    return best


def kernel(x, p, norm_mix, w_in, s5_lam_re, s5_lam_im, s5_log_step, s5_b_re, s5_b_im, s5_c_re, s5_c_im, s5_d, s5_glu_w, s5_glu_b, rw_shift_mu, rw_w0, rw_w2, rw_a0, rw_a2, rw_g2, rw_k_k, rw_k_a, rw_r_k, rw_ln_w, rw_ln_b, w_out, norm_ffn, ffn_w1, ffn_w3, ffn_w2, norm_ple, ple_gate_w, ple_up_w, final_norm, loss_target, m_norm_mix, m_w_in, m_s5_lam_re, m_s5_lam_im, m_s5_log_step, m_s5_b_re, m_s5_b_im, m_s5_c_re, m_s5_c_im, m_s5_d, m_s5_glu_w, m_s5_glu_b, m_rw_shift_mu, m_rw_w0, m_rw_w2, m_rw_a0, m_rw_a2, m_rw_g2, m_rw_k_k, m_rw_k_a, m_rw_r_k, m_rw_ln_w, m_rw_ln_b, m_w_out, m_norm_ffn, m_ffn_w1, m_ffn_w3, m_ffn_w2, m_norm_ple, m_ple_gate_w, m_ple_up_w, m_final_norm, v_norm_mix, v_w_in, v_s5_lam_re, v_s5_lam_im, v_s5_log_step, v_s5_b_re, v_s5_b_im, v_s5_c_re, v_s5_c_im, v_s5_d, v_s5_glu_w, v_s5_glu_b, v_rw_shift_mu, v_rw_w0, v_rw_w2, v_rw_a0, v_rw_a2, v_rw_g2, v_rw_k_k, v_rw_k_a, v_rw_r_k, v_rw_ln_w, v_rw_ln_b, v_w_out, v_norm_ffn, v_ffn_w1, v_ffn_w3, v_ffn_w2, v_norm_ple, v_ple_gate_w, v_ple_up_w, v_final_norm):
    return _kernel_impl(dict(zip(ARG_NAMES, (x, p, norm_mix, w_in, s5_lam_re, s5_lam_im, s5_log_step, s5_b_re, s5_b_im, s5_c_re, s5_c_im, s5_d, s5_glu_w, s5_glu_b, rw_shift_mu, rw_w0, rw_w2, rw_a0, rw_a2, rw_g2, rw_k_k, rw_k_a, rw_r_k, rw_ln_w, rw_ln_b, w_out, norm_ffn, ffn_w1, ffn_w3, ffn_w2, norm_ple, ple_gate_w, ple_up_w, final_norm, loss_target, m_norm_mix, m_w_in, m_s5_lam_re, m_s5_lam_im, m_s5_log_step, m_s5_b_re, m_s5_b_im, m_s5_c_re, m_s5_c_im, m_s5_d, m_s5_glu_w, m_s5_glu_b, m_rw_shift_mu, m_rw_w0, m_rw_w2, m_rw_a0, m_rw_a2, m_rw_g2, m_rw_k_k, m_rw_k_a, m_rw_r_k, m_rw_ln_w, m_rw_ln_b, m_w_out, m_norm_ffn, m_ffn_w1, m_ffn_w3, m_ffn_w2, m_norm_ple, m_ple_gate_w, m_ple_up_w, m_final_norm, v_norm_mix, v_w_in, v_s5_lam_re, v_s5_lam_im, v_s5_log_step, v_s5_b_re, v_s5_b_im, v_s5_c_re, v_s5_c_im, v_s5_d, v_s5_glu_w, v_s5_glu_b, v_rw_shift_mu, v_rw_w0, v_rw_w2, v_rw_a0, v_rw_a2, v_rw_g2, v_rw_k_k, v_rw_k_a, v_rw_r_k, v_rw_ln_w, v_rw_ln_b, v_w_out, v_norm_ffn, v_ffn_w1, v_ffn_w3, v_ffn_w2, v_norm_ple, v_ple_gate_w, v_ple_up_w, v_final_norm))))
```

```python
import functools

import jax
import jax.numpy as jnp
from jax import lax
from jax.experimental import pallas as pl
from jax.experimental.pallas import tpu as pltpu

F32 = jnp.float32
BF16 = jnp.bfloat16

D_MODEL = 1024
S5_WIDTH = 512
RW_WIDTH = 512
S5_GROUP = 16
S5_GROUPS = 32
S5_STATE = 64
S5_LANES = S5_GROUPS * S5_STATE
HEAD = 64
SHIFT_COLS = 1792
IN_COLS = 2304
FFN_HIDDEN = 2816
PLE_DIM = 256
RMS_EPS = 1e-6
GN_EPS = 64e-5
L2_EPS = 1e-12
CHUNK = 64
N_DEV = 8

ADAM_LR = 0.001
ADAM_B1 = 0.9
ADAM_B2 = 0.999
ADAM_EPS = 1e-08
ADAM_WD = 0.01
ADAM_STEP = 10

VMEM_LIMIT = 56 * 1024 * 1024
_ANY = pl.BlockSpec(memory_space=pl.ANY)


def _pcall(body, **kw):
    return pl.pallas_call(body, **kw)


def _cparams(n_grid):
    return pltpu.CompilerParams(dimension_semantics=("arbitrary",) * n_grid, vmem_limit_bytes=VMEM_LIMIT)


def _dot(a, b):
    return jnp.dot(a, b, preferred_element_type=F32)


def _dot_nt(a, b):
    return lax.dot_general(a, b, (((1,), (1,)), ((), ())), preferred_element_type=F32)


def _dot_tn(a, b):
    return lax.dot_general(a, b, (((0,), (0,)), ((), ())), preferred_element_type=F32)


def _mmc(w, diff=True, tr=False):
    fw, bw = (_dot_nt, _dot) if tr else (_dot, _dot_nt)
    if not diff:
        return lambda x: fw(x.astype(BF16), w)

    @jax.custom_vjp
    def f(x):
        return fw(x.astype(BF16), w)

    def fwd(x):
        return fw(x.astype(BF16), w), None

    def bwd(_, dy):
        return (bw(dy.astype(BF16), w),)

    f.defvjp(fwd, bwd)
    return f


def _split_dot(x, m, n_split):
    acc = None
    rem = x
    for s in range(n_split):
        part = rem.astype(BF16)
        t = _dot(part, m)
        acc = t if acc is None else acc + t
        if s + 1 < n_split:
            rem = rem - part.astype(F32)
    return acc


def _segsum(m, diff=True):
    if not diff:
        return lambda x: _split_dot(x, m, 2)

    @jax.custom_vjp
    def f(x):
        return _split_dot(x, m, 2)

    def fwd(x):
        return _split_dot(x, m, 2), None

    def bwd(_, dy):
        return (_split_dot(dy, m, 2),)

    f.defvjp(fwd, bwd)
    return f


def _head_indicator(n):
    r = lax.broadcasted_iota(jnp.int32, (n, n), 0) // HEAD
    c = lax.broadcasted_iota(jnp.int32, (n, n), 1) // HEAD
    return (r == c).astype(BF16)


def _rms(x, g):
    return x * lax.rsqrt(jnp.mean(x * x, axis=-1, keepdims=True) + RMS_EPS) * g


def _softplus(x):
    return jnp.maximum(x, 0.0) + jnp.log(1.0 + jnp.exp(-jnp.abs(x)))


def _sigmoid(x):
    return 1.0 / (1.0 + jnp.exp(-x))


def _gelu(x):
    return 0.5 * x * (1.0 + jnp.tanh(0.7978845608028654 * (x + 0.044715 * (x * x * x))))


def _tok_call(name, fn, L, TB, tok_in, const_in, tok_out, acc_out=(), deps=()):
    nb = L // TB
    g8 = TB // 8
    in_specs, args = [], []
    for spec in tok_in:
        if len(spec) == 1:
            arr = spec[0]
            in_specs.append(pl.BlockSpec((arr.shape[0], TB, HEAD), lambda i: (0, i, 0)))
            args.append(arr)
            continue
        arr, width, cb = spec[:3]
        mode = spec[3] if len(spec) > 3 else None
        if mode is None:
            in_specs.append(pl.BlockSpec((TB, width), lambda i, cb=cb: (i, cb)))
        elif mode == "prev":
            in_specs.append(pl.BlockSpec((8, width), lambda i, cb=cb: (jnp.maximum(i * g8 - 1, 0), cb)))
        else:
            in_specs.append(pl.BlockSpec((8, width), lambda i, cb=cb: (jnp.minimum((i + 1) * g8, L // 8 - 1), cb)))
        args.append(arr)
    for c in const_in:
        in_specs.append(pl.BlockSpec(c.shape, lambda i, nd=c.ndim: (0,) * nd, pipeline_mode=pl.Buffered(1)))
        args.append(c)
    for d in deps:
        in_specs.append(pl.BlockSpec(d.shape, lambda i, nd=d.ndim: (0,) * nd))
        args.append(d)
    out_shape, out_specs = [], []
    for width, dt in tok_out:
        if width == "heads":
            out_shape.append(jax.ShapeDtypeStruct((N_HEAD, L, HEAD), dt))
            out_specs.append(pl.BlockSpec((N_HEAD, TB, HEAD), lambda i: (0, i, 0)))
            continue
        out_shape.append(jax.ShapeDtypeStruct((L, width), dt))
        out_specs.append(pl.BlockSpec((TB, width), lambda i: (i, 0)))
    for shp in acc_out:
        out_shape.append(jax.ShapeDtypeStruct(shp, F32))
        out_specs.append(pl.BlockSpec(shp, lambda i, nd=len(shp): (0,) * nd))
    n_tok, n_const, n_to = len(tok_in), len(const_in), len(tok_out)

    def body(*refs):
        i = pl.program_id(0)
        tv = [r[...] if len(r.shape) == 2 else jnp.concatenate([r[h] for h in range(r.shape[0])], axis=1)
              for r in refs[:n_tok]]
        cv = [r[...] for r in refs[n_tok:n_tok + n_const]]
        orefs = refs[n_tok + n_const + len(deps):]
        outs = fn(i, tv, cv)
        for r, v in zip(orefs[:n_to], outs[:n_to]):
            if len(r.shape) == 3:
                for h in range(r.shape[0]):
                    r[h] = v[:, h * HEAD:(h + 1) * HEAD].astype(r.dtype)
            else:
                r[...] = v.astype(r.dtype)
        for r, v in zip(orefs[n_to:], outs[n_to:]):
            @pl.when(i == 0)
            def _(r=r):
                r[...] = jnp.zeros(r.shape, r.dtype)

            r[...] += v

    res = _pcall(body, name=name, grid=(nb,), in_specs=in_specs, out_specs=out_specs, out_shape=out_shape,
                 compiler_params=_cparams(1))(*args)
    return res


def _pick_block(n, cap):
    best = None
    for b in range(128, min(n, cap) + 1, 128):
        if n % b == 0:
            best = b
    return best if best is not None else n


def _mm_tn(name, a, b, with_bf16=False):
    T, M = a.shape
    N = b.shape[1]
    bm, bn, bt = _pick_block(M, 1536), _pick_block(N, 1536), _pick_block(T, 512)
    nt = T // bt

    def body(a_ref, b_ref, o_ref, *rest):
        t = pl.program_id(2)

        @pl.when(t == 0)
        def _():
            o_ref[...] = jnp.zeros(o_ref.shape, F32)

        o_ref[...] += _dot_tn(a_ref[...].astype(BF16), b_ref[...].astype(BF16))
        if with_bf16:
            @pl.when(t == nt - 1)
            def _():
                rest[0][...] = o_ref[...].astype(BF16)

    oblk = pl.BlockSpec((bm, bn), lambda m, n, t: (m, n))
    return _pcall(body, name=name, grid=(M // bm, N // bn, nt),
                  in_specs=[pl.BlockSpec((bt, bm), lambda m, n, t: (t, m)), pl.BlockSpec((bt, bn), lambda m, n, t: (t, n))],
                  out_specs=[oblk, oblk] if with_bf16 else oblk,
                  out_shape=[jax.ShapeDtypeStruct((M, N), F32), jax.ShapeDtypeStruct((M, N), BF16)] if with_bf16
                  else jax.ShapeDtypeStruct((M, N), F32),
                  compiler_params=_cparams(3))(a, b)


def _s5_param_fn(lam_re, lam_im, log_step, bt_re, bt_im):
    dt = jnp.exp(log_step)
    e = jnp.exp(lam_re * dt)
    lb_re = e * jnp.cos(lam_im * dt)
    lb_im = e * jnp.sin(lam_im * dt)
    den = lam_re * lam_re + lam_im * lam_im
    nr, ni = lb_re - 1.0, lb_im
    co_re = (nr * lam_re + ni * lam_im) / den
    co_im = (ni * lam_re - nr * lam_im) / den
    cr, ci = co_re[:, None, :], co_im[:, None, :]
    return lb_re, lb_im, cr * bt_re - ci * bt_im, cr * bt_im + ci * bt_re


def _s5_param_fwd(lam_re, lam_im, log_step, bt_re, bt_im):
    def body(a, b, c, d, e, o1, o2, o3, o4):
        r = _s5_param_fn(a[...], b[...], c[...], d[...], e[...])
        o1[...], o2[...], o3[...], o4[...] = r

    sh = jax.ShapeDtypeStruct
    return _pcall(body, name="s5_param_fwd",
                  out_shape=[sh(lam_re.shape, F32), sh(lam_re.shape, F32), sh(bt_re.shape, F32), sh(bt_re.shape, F32)])(
        lam_re, lam_im, log_step, bt_re, bt_im)


def _s5_param_bwd(lam_re, lam_im, log_step, bt_re, bt_im, d_lb_re, d_lb_im, d_bb_re, d_bb_im):
    def body(a, b, c, d, e, g1, g2, g3, g4, o1, o2, o3, o4, o5):
        _, vjp = jax.vjp(_s5_param_fn, a[...], b[...], c[...], d[...], e[...])
        r = vjp((g1[...], g2[...], g3[...], g4[...]))
        o1[...], o2[...], o3[...], o4[...], o5[...] = r

    sh = jax.ShapeDtypeStruct
    return _pcall(body, name="s5_param_bwd",
                  out_shape=[sh(lam_re.shape, F32), sh(lam_re.shape, F32), sh(log_step.shape, F32),
                             sh(bt_re.shape, F32), sh(bt_re.shape, F32)])(
        lam_re, lam_im, log_step, bt_re, bt_im, d_lb_re, d_lb_im, d_bb_re, d_bb_im)


def _cmul(ar, ai, br, bi):
    return ar * br - ai * bi, ar * bi + ai * br


def _scan_consts(lr, li, reverse):
    n = lr.shape[1]
    sub = lax.broadcasted_iota(jnp.int32, (8, n), 0)
    pows = [(lr, li)]
    for _ in range(7):
        pows.append(_cmul(pows[-1][0], pows[-1][1], lr, li))
    steps = []
    for s in (1, 2, 4):
        m = (sub < 8 - s) if reverse else (sub >= s)
        pr, pi = pows[s - 1]
        steps.append((s, jnp.where(m, jnp.broadcast_to(pr, (8, n)), 0.0), jnp.where(m, jnp.broadcast_to(pi, (8, n)), 0.0)))
    wr = jnp.zeros((8, n), F32)
    wi = jnp.zeros((8, n), F32)
    for r in range(8):
        e = (8 - r) if reverse else (r + 1)
        wr = jnp.where(sub == r, jnp.broadcast_to(pows[e - 1][0], (8, n)), wr)
        wi = jnp.where(sub == r, jnp.broadcast_to(pows[e - 1][1], (8, n)), wi)
    return steps, wr, wi


S5_Q = 4
S5_QL = S5_WIDTH // S5_Q
S5_QS = S5_LANES // S5_Q
S5_NT = S5_LANES // 128
S5_QT = S5_QS // 128


def _s5_power_table(lb_ref, pw_re, pw_im, seg):
    for j in range(S5_NT):
        lr = jnp.broadcast_to(lb_ref[0:1, j * 128:(j + 1) * 128], (8, 128))
        li = jnp.broadcast_to(lb_ref[1:2, j * 128:(j + 1) * 128], (8, 128))

        def step(i, c, lr=lr, li=li, j=j):
            pw_re[j, i] = c[0]
            pw_im[j, i] = c[1]
            return _cmul(c[0], c[1], lr, li)

        lax.fori_loop(0, seg, step, (lr, li))


def _seg_scan(sre, sim, carry, lb_ref, pw_re, pw_im, rows, reverse):
    seg = rows // 8
    sgn = -1.0 if reverse else 1.0
    sub = lax.broadcasted_iota(jnp.int32, (8, 128), 0)
    rows_at = lambda i: pl.ds(pl.multiple_of(i * 8, 8), 8)
    entering = {}
    half_tiles = S5_NT // 2
    for half in range(2):
        tiles = list(range(half * half_tiles, (half + 1) * half_tiles))
        lam8 = [(jnp.broadcast_to(lb_ref[0:1, j * 128:(j + 1) * 128], (8, 128)),
                 sgn * jnp.broadcast_to(lb_ref[1:2, j * 128:(j + 1) * 128], (8, 128))) for j in tiles]

        def p1(ii, c):
            i = (seg - 1 - ii) if reverse else ii
            out = []
            for n, j in enumerate(tiles):
                lr, li = lam8[n]
                cr, ci = c[2 * n], c[2 * n + 1]
                nr = lr * cr - li * ci + sre[j, rows_at(i), :]
                ni = lr * ci + li * cr + sim[j, rows_at(i), :]
                sre[j, rows_at(i), :] = nr
                sim[j, rows_at(i), :] = ni
                out += [nr, ni]
            return tuple(out)

        ends = lax.fori_loop(0, seg, p1, tuple(jnp.zeros((8, 128), F32) for _ in range(2 * len(tiles))))
        cs = []
        for n, j in enumerate(tiles):
            ls = slice(j * 128, (j + 1) * 128)
            steps, wr, wi = _scan_consts(pw_re[j, seg - 1][0:1, :], sgn * pw_im[j, seg - 1][0:1, :], reverse)
            tr, ti = ends[2 * n], ends[2 * n + 1]
            for sft, pr, pi in steps:
                sh = (8 - sft) if reverse else sft
                yr, yi = pltpu.roll(tr, sh, 0), pltpu.roll(ti, sh, 0)
                tr, ti = tr + pr * yr - pi * yi, ti + pr * yi + pi * yr
            cin_r, cin_i = carry[0:1, ls], carry[1:2, ls]
            tr, ti = tr + wr * cin_r - wi * cin_i, ti + wr * cin_i + wi * cin_r
            edge_out, edge_in, sh = (0, 7, 7) if reverse else (7, 0, 1)
            carry[0:1, ls] = tr[edge_out:edge_out + 1, :]
            carry[1:2, ls] = ti[edge_out:edge_out + 1, :]
            cr = jnp.where(sub == edge_in, jnp.broadcast_to(cin_r, (8, 128)), pltpu.roll(tr, sh, 0))
            ci = jnp.where(sub == edge_in, jnp.broadcast_to(cin_i, (8, 128)), pltpu.roll(ti, sh, 0))
            cs += [cr, ci]
            entering[j] = (cr, ci)

        def p2(i, _):
            k = (seg - 1 - i) if reverse else i
            for n, j in enumerate(tiles):
                pr, pi = pw_re[j, k], pw_im[j, k]
                cr, ci = cs[2 * n], cs[2 * n + 1]
                if reverse:
                    sre[j, rows_at(i), :] = sre[j, rows_at(i), :] + pr * cr + pi * ci
                    sim[j, rows_at(i), :] = sim[j, rows_at(i), :] + pr * ci - pi * cr
                else:
                    sre[j, rows_at(i), :] = sre[j, rows_at(i), :] + pr * cr - pi * ci
                    sim[j, rows_at(i), :] = sim[j, rows_at(i), :] + pr * ci + pi * cr
            return 0

        lax.fori_loop(0, seg, p2, 0, unroll=2)
    return entering


class _SegIO:
    def __init__(self, hbm, buf, sems, rows, width, col0=0):
        self.hbm, self.buf, self.sems, self.rows, self.seg, self.width, self.col0 = hbm, buf, sems, rows, rows // 8, width, col0

    def _copies(self, blk, slot, to_vmem):
        out = []
        for r in range(8):
            h = self.hbm.at[pl.ds(blk * self.rows + r * self.seg, self.seg), pl.ds(self.col0, self.width)]
            v = self.buf.at[slot, :, r, :]
            out.append(pltpu.make_async_copy(h, v, self.sems.at[slot, r]) if to_vmem
                       else pltpu.make_async_copy(v, h, self.sems.at[slot, r]))
        return out

    def start(self, blk, slot, to_vmem):
        for cp in self._copies(blk, slot, to_vmem):
            cp.start()

    def wait(self, blk, slot, to_vmem):
        for cp in self._copies(blk, slot, to_vmem):
            cp.wait()

    def value(self, slot):
        return self.buf[slot].reshape(self.rows, self.width)

    def store(self, slot, val):
        self.buf[slot] = val.reshape(self.seg, 8, self.width)


def _seg_pipeline(i, nb, blk_of, ins, outs, compute):
    slot = i % 2

    @pl.when(i == 0)
    def _():
        for io in ins:
            io.start(blk_of(0), 0, True)

    @pl.when(i + 1 < nb)
    def _():
        for io in ins:
            io.start(blk_of(i + 1), 1 - slot, True)

    for io in ins:
        io.wait(blk_of(i), slot, True)

    @pl.when(i >= 2)
    def _():
        for io in outs:
            io.wait(blk_of(i - 2), slot, False)

    compute(slot)
    for io in outs:
        io.start(blk_of(i), slot, False)

    @pl.when(i == nb - 1)
    def _():
        for io in outs:
            if nb >= 2:
                io.wait(blk_of(i - 1), 1 - slot, False)
            io.wait(blk_of(i), slot, False)


def _s5_scan_fwd(proj, bq_re, bq_im, cq_re, cq_im, lbar, dskip, L, TB):
    nb = L // TB
    seg = TB // 8

    def body(u_hbm, bre, bim, cre, cim, lb_ref, d_ref, y_hbm, ck_ref, sre, sim, carry, pw_re, pw_im,
             ubuf, ybuf, sem_u, sem_y):
        i = pl.program_id(0)
        u_io = _SegIO(u_hbm, ubuf, sem_u, TB, S5_WIDTH)
        y_io = _SegIO(y_hbm, ybuf, sem_y, TB, S5_WIDTH)

        @pl.when(i == 0)
        def _():
            carry[...] = jnp.zeros(carry.shape, F32)
            _s5_power_table(lb_ref, pw_re, pw_im, seg)

        ck_ref[0] = carry[...]

        def compute(slot):
            u = u_io.value(slot)
            ub = u.astype(BF16)
            for q in range(S5_Q):
                uq = ub[:, q * S5_QL:(q + 1) * S5_QL]
                vr, vi = _dot(uq, bre[q]), _dot(uq, bim[q])
                for jj in range(S5_QT):
                    sre[q * S5_QT + jj] = vr[:, jj * 128:(jj + 1) * 128]
                    sim[q * S5_QT + jj] = vi[:, jj * 128:(jj + 1) * 128]
            _seg_scan(sre, sim, carry, lb_ref, pw_re, pw_im, TB, False)
            ys = []
            for q in range(S5_Q):
                sl = slice(q * S5_QL, (q + 1) * S5_QL)
                sr = jnp.concatenate([sre[q * S5_QT + jj] for jj in range(S5_QT)], axis=1).astype(BF16)
                si = jnp.concatenate([sim[q * S5_QT + jj] for jj in range(S5_QT)], axis=1).astype(BF16)
                ys.append(_dot(sr, cre[q]) - _dot(si, cim[q]) + u[:, sl] * d_ref[:, sl])
            y_io.store(slot, jnp.concatenate(ys, axis=1))

        _seg_pipeline(i, nb, lambda st: st, [u_io], [y_io], compute)

    full = lambda a: pl.BlockSpec(a.shape, lambda i, nd=a.ndim: (0,) * nd)
    st = pltpu.VMEM((S5_NT, TB, 128), F32)
    pw = pltpu.VMEM((S5_NT, seg, 8, 128), F32)
    io = pltpu.VMEM((2, seg, 8, S5_WIDTH), F32)
    return _pcall(
        body, name="s5_scan_fwd", grid=(nb,),
        in_specs=[_ANY, full(bq_re), full(bq_im), full(cq_re), full(cq_im), full(lbar), full(dskip)],
        out_specs=[_ANY, pl.BlockSpec((1, 8, S5_LANES), lambda i: (i, 0, 0))],
        out_shape=[jax.ShapeDtypeStruct((L, S5_WIDTH), F32), jax.ShapeDtypeStruct((nb, 8, S5_LANES), F32)],
        scratch_shapes=[st, st, pltpu.VMEM((8, S5_LANES), F32), pw, pw, io, io,
                        pltpu.SemaphoreType.DMA((2, 8)), pltpu.SemaphoreType.DMA((2, 8))],
        compiler_params=_cparams(1))(proj, bq_re, bq_im, cq_re, cq_im, lbar, dskip)


def _s5_scan_bwd(proj, dy, ck, bq_re, bq_im, cq_re, cq_im, lbar, dskip, L, TB):
    nb = L // TB
    seg = TB // 8

    def body(u_hbm, dy_hbm, ck_ref, bre, bim, cre, cim, lb_ref, d_ref,
             du_hbm, dbre, dbim, dcre, dcim, dlb_ref, dd_ref, sre, sim, gre, gim, carry, gcarry, pw_re, pw_im,
             ubuf, dybuf, dubuf, sem_u, sem_dy, sem_du):
        i = pl.program_id(0)
        u_io = _SegIO(u_hbm, ubuf, sem_u, TB, S5_WIDTH)
        dy_io = _SegIO(dy_hbm, dybuf, sem_dy, TB, S5_WIDTH)
        du_io = _SegIO(du_hbm, dubuf, sem_du, TB, S5_WIDTH)

        @pl.when(i == 0)
        def _():
            gcarry[...] = jnp.zeros(gcarry.shape, F32)
            dbre[...] = jnp.zeros(dbre.shape, F32)
            dbim[...] = jnp.zeros(dbim.shape, F32)
            dcre[...] = jnp.zeros(dcre.shape, F32)
            dcim[...] = jnp.zeros(dcim.shape, F32)
            dlb_ref[...] = jnp.zeros(dlb_ref.shape, F32)
            dd_ref[...] = jnp.zeros(dd_ref.shape, F32)
            _s5_power_table(lb_ref, pw_re, pw_im, seg)

        def compute(slot):
            u = u_io.value(slot)
            dy_v = dy_io.value(slot)
            ub = u.astype(BF16)
            dyb = dy_v.astype(BF16)
            carry[...] = ck_ref[0]
            for q in range(S5_Q):
                uq = ub[:, q * S5_QL:(q + 1) * S5_QL]
                dq = dyb[:, q * S5_QL:(q + 1) * S5_QL]
                vr, vi = _dot(uq, bre[q]), _dot(uq, bim[q])
                hr, hi = _dot_nt(dq, cre[q]), -_dot_nt(dq, cim[q])
                for jj in range(S5_QT):
                    ls = slice(jj * 128, (jj + 1) * 128)
                    sre[q * S5_QT + jj] = vr[:, ls]
                    sim[q * S5_QT + jj] = vi[:, ls]
                    gre[q * S5_QT + jj] = hr[:, ls]
                    gim[q * S5_QT + jj] = hi[:, ls]
            entering = _seg_scan(sre, sim, carry, lb_ref, pw_re, pw_im, TB, False)
            _seg_scan(gre, gim, gcarry, lb_ref, pw_re, pw_im, TB, True)

            rows_at = lambda k: pl.ds(pl.multiple_of(k * 8, 8), 8)
            for j in range(S5_NT):
                er, ei = entering[j]
                gr0, gi0 = gre[j, rows_at(0), :], gim[j, rows_at(0), :]
                acc0 = (gr0 * er + gi0 * ei, gi0 * er - gr0 * ei)

                def acc_step(k, acc, j=j):
                    gr, gi_ = gre[j, rows_at(k), :], gim[j, rows_at(k), :]
                    spr, spi = sre[j, rows_at(k - 1), :], sim[j, rows_at(k - 1), :]
                    return acc[0] + gr * spr + gi_ * spi, acc[1] - gr * spi + gi_ * spr

                ar, ai = lax.fori_loop(1, seg, acc_step, acc0, unroll=2 if (seg - 1) % 2 == 0 else 1)
                ls = slice(j * 128, (j + 1) * 128)
                dlb_ref[0:1, ls] += jnp.sum(ar, axis=0, keepdims=True)
                dlb_ref[1:2, ls] += jnp.sum(ai, axis=0, keepdims=True)

            dd_ref[...] += jnp.sum(dy_v * u, axis=0, keepdims=True)
            dus = []
            for q in range(S5_Q):
                sl = slice(q * S5_QL, (q + 1) * S5_QL)
                cat = lambda ref: jnp.concatenate([ref[q * S5_QT + jj] for jj in range(S5_QT)], axis=1).astype(BF16)
                grq, giq = cat(gre), cat(gim)
                dus.append(_dot_nt(grq, bre[q]) + _dot_nt(giq, bim[q]) + dy_v[:, sl] * d_ref[:, sl])
                dbre[q] += _dot_tn(ub[:, sl], grq)
                dbim[q] += _dot_tn(ub[:, sl], giq)
                dcre[q] += _dot_tn(cat(sre), dyb[:, sl])
                dcim[q] -= _dot_tn(cat(sim), dyb[:, sl])
            du_io.store(slot, jnp.concatenate(dus, axis=1))

        _seg_pipeline(i, nb, lambda st: nb - 1 - st, [u_io, dy_io], [du_io], compute)

    full = lambda a: pl.BlockSpec(a.shape, lambda i, nd=a.ndim: (0,) * nd)
    sh = jax.ShapeDtypeStruct
    outs = [sh((L, S5_WIDTH), F32), sh(bq_re.shape, F32), sh(bq_im.shape, F32), sh(cq_re.shape, F32), sh(cq_im.shape, F32),
            sh((8, S5_LANES), F32), sh((1, S5_WIDTH), F32)]
    fo = lambda s: pl.BlockSpec(s.shape, lambda i, nd=len(s.shape): (0,) * nd)
    st = pltpu.VMEM((S5_NT, TB, 128), F32)
    pw = pltpu.VMEM((S5_NT, seg, 8, 128), F32)
    io = pltpu.VMEM((2, seg, 8, S5_WIDTH), F32)
    sem = pltpu.SemaphoreType.DMA((2, 8))
    return _pcall(
        body, name="s5_scan_bwd", grid=(nb,),
        in_specs=[_ANY, _ANY, pl.BlockSpec((1, 8, S5_LANES), lambda i: (nb - 1 - i, 0, 0)),
                  full(bq_re), full(bq_im), full(cq_re), full(cq_im), full(lbar), full(dskip)],
        out_specs=[_ANY] + [fo(s) for s in outs[1:]],
        out_shape=outs,
        scratch_shapes=[st] * 4 + [pltpu.VMEM((8, S5_LANES), F32)] * 2 + [pw, pw, io, io, io, sem, sem, sem],
        compiler_params=_cparams(1))(proj, dy, ck, bq_re, bq_im, cq_re, cq_im, lbar, dskip)


N_HEAD = RW_WIDTH // HEAD
_NN = (((2,), (1,)), ((0,), (0,)))
_NT = (((2,), (2,)), ((0,), (0,)))
_TN = (((1,), (1,)), ((0,), (0,)))


def _hi_lo(x):
    h = x.astype(BF16)
    return h, (x - h.astype(F32)).astype(BF16)


def _mm_acc(a, b, dims, passes=3):
    dg = lambda p, q: lax.dot_general(p, q, dims, preferred_element_type=F32)
    if passes == 1:
        return dg(a.astype(BF16), b.astype(BF16))
    ah, al = _hi_lo(a)
    bh, bl = _hi_lo(b)
    return dg(ah, bh) + dg(ah, bl) + dg(al, bh)


def _cumsum_rows(x, transpose):
    h, n, _ = x.shape
    ti = lax.broadcasted_iota(jnp.int32, (h, n, n), 1)
    tj = lax.broadcasted_iota(jnp.int32, (h, n, n), 2)
    m = ((tj >= ti) if transpose else (tj <= ti)).astype(BF16)
    acc, rem = None, x
    for s in range(3):
        part = rem.astype(BF16)
        t = lax.dot_general(m, part, _NN, preferred_element_type=F32)
        acc = t if acc is None else acc + t
        if s < 2:
            rem = rem - part.astype(F32)
    return acc


def _slices(x, axis, sizes):
    out, off = [], 0
    for n in sizes:
        out.append(lax.slice_in_dim(x, off, off + n, axis=axis))
        off += n
    return tuple(out)


def _cat_op(axis, sizes, diff):
    plain = lambda *xs: jnp.concatenate(xs, axis=axis)
    if not diff:
        return plain
    f = jax.custom_vjp(plain)
    f.defvjp(lambda *xs: (plain(*xs), None), lambda _, d: _slices(d, axis, sizes))
    return f


def _split_op(axis, sizes, diff):
    plain = lambda x: _slices(x, axis, sizes)
    if not diff:
        return plain
    f = jax.custom_vjp(plain)
    f.defvjp(lambda x: (plain(x), None), lambda _, d: (jnp.concatenate(d, axis=axis),))
    return f


def _mm_ops(diff, passes):
    mm = lambda a, b, dims: _mm_acc(a, b, dims, passes)
    if not diff:
        return (lambda a, b: mm(a, b, _NN), lambda a, b: mm(a, b, _NT), lambda a, b: mm(a, b, _TN))

    @jax.custom_vjp
    def nn(a, b):
        return mm(a, b, _NN)

    nn.defvjp(lambda a, b: (mm(a, b, _NN), (a, b)), lambda r, d: (mm(d, r[1], _NT), mm(r[0], d, _TN)))

    @jax.custom_vjp
    def nt(a, b):
        return mm(a, b, _NT)

    nt.defvjp(lambda a, b: (mm(a, b, _NT), (a, b)), lambda r, d: (mm(d, r[1], _NN), mm(d, r[0], _TN)))

    @jax.custom_vjp
    def tn(a, b):
        return mm(a, b, _TN)

    tn.defvjp(lambda a, b: (mm(a, b, _TN), (a, b)), lambda r, d: (mm(r[1], d, _NT), mm(r[0], d, _NN)))
    return nn, nt, tn


def _cums_op(diff):
    if not diff:
        return lambda x: _cumsum_rows(x, False)

    @jax.custom_vjp
    def cums(x):
        return _cumsum_rows(x, False)

    cums.defvjp(lambda x: (_cumsum_rows(x, False), None), lambda _, d: (_cumsum_rows(d, True),))
    return cums


WKV_PASSES = (1, 1, 1, 1, 1)


WKV_SUB = 4
WKV_BLOCK = CHUNK * WKV_SUB


def _wkv_block(s0, r, w, k, v, a, b, diff):
    p_pair, p_val, p_solve, p_out, p_state = WKV_PASSES
    cums = _cums_op(diff)
    _, nt_pair, _ = _mm_ops(diff, p_pair)
    nn_val, _, _ = _mm_ops(diff, p_val)
    nn_solve, _, _ = _mm_ops(diff, p_solve)
    nn_out, nt_out, _ = _mm_ops(diff, p_out)
    nn_state, _, tn_state = _mm_ops(diff, p_state)
    h, d, n, sub = s0.shape[0], s0.shape[2], CHUNK, WKV_SUB
    hb = h * sub
    to_chunks = lambda t: _cat_op(0, (h,) * sub, diff)(*_split_op(1, (n,) * sub, diff)(t))
    r, w, k, v, a, b = (to_chunks(t) for t in (r, w, k, v, a, b))
    cat_rows2 = _cat_op(1, (n, n), diff)
    cat_lanes2 = _cat_op(2, (n, n), diff)
    split_rows2 = _split_op(1, (n, n), diff)
    split_lanes2 = _split_op(2, (n, n), diff)
    ti = lax.broadcasted_iota(jnp.int32, (hb, n, n), 1)
    tj = lax.broadcasted_iota(jnp.int32, (hb, n, n), 2)
    incl, strict = tj <= ti, tj < ti
    logw = jnp.log(w)
    cum = cums(logw)
    g_in, g_ex, g_inv = jnp.exp(cum), jnp.exp(cum - logw), jnp.exp(-cum)
    ae, re, bi, ki = a * g_ex, r * g_in, b * g_inv, k * g_inv
    top, bot = split_rows2(nt_pair(cat_rows2(ae, re), cat_rows2(bi, ki)))
    tab, tak = split_lanes2(top)
    qb, qk = split_lanes2(bot)
    tab, tak = jnp.where(strict, tab, 0.0), jnp.where(strict, tak, 0.0)
    qb, qk = jnp.where(incl, qb, 0.0), jnp.where(incl, qk, 0.0)
    tak_v, qk_v = split_rows2(nn_val(cat_rows2(tak, qk), v))
    x = cat_lanes2(ae, tak_v)
    npow = tab
    steps = max(1, (n - 1).bit_length())
    for i in range(steps):
        x = x + nn_solve(npow, x)
        if i + 1 < steps:
            npow = nn_solve(npow, npow)
    ae_m, uc = split_lanes2(x)
    qx = nn_out(qb, x)
    q_ae, q_uc = split_lanes2(qx)
    re_m = re + q_ae
    yc = q_uc + qk_v
    g_end = jnp.exp(jnp.sum(logw, axis=1, keepdims=True))
    bg, kg = bi * g_end, ki * g_end
    tm = tn_state(ae_m, bg)
    sc = tn_state(cat_rows2(uc, v), cat_rows2(bg, kg))
    per_chunk = _split_op(0, (h,) * sub, diff)
    re_m, yc, g_end, tm, sc = (per_chunk(t) for t in (re_m, yc, g_end, tm, sc))
    ys, s = [], s0
    for i in range(sub):
        ys.append(nt_out(re_m[i], s) + yc[i])
        s = s * g_end[i] + nn_state(s, tm[i]) + sc[i]
    return _cat_op(1, (n,) * sub, diff)(*ys), s


def _wkv_fwd(r, w, k, v, a, b, L):
    nc = L // WKV_BLOCK

    def body(r_ref, w_ref, k_ref, v_ref, a_ref, b_ref, y_ref, ck_ref, s_ref):
        c = pl.program_id(0)

        @pl.when(c == 0)
        def _():
            s_ref[...] = jnp.zeros(s_ref.shape, F32)

        s0 = s_ref[...]
        ck_ref[0] = s0
        y, s1 = _wkv_block(s0, r_ref[...], w_ref[...], k_ref[...], v_ref[...], a_ref[...], b_ref[...], False)
        y_ref[...] = y
        s_ref[...] = s1

    blk = pl.BlockSpec((N_HEAD, WKV_BLOCK, HEAD), lambda c: (0, c, 0))
    return _pcall(
        body, name="wkv_fwd", grid=(nc,), in_specs=[blk] * 6,
        out_specs=[blk, pl.BlockSpec((1, N_HEAD, HEAD, HEAD), lambda c: (c, 0, 0, 0))],
        out_shape=[jax.ShapeDtypeStruct((N_HEAD, L, HEAD), F32), jax.ShapeDtypeStruct((nc, N_HEAD, HEAD, HEAD), F32)],
        scratch_shapes=[pltpu.VMEM((N_HEAD, HEAD, HEAD), F32)],
        compiler_params=_cparams(1))(r, w, k, v, a, b)


def _wkv_bwd(r, w, k, v, a, b, dy, ck, L, deps=()):
    nc = L // WKV_BLOCK

    def body(r_ref, w_ref, k_ref, v_ref, a_ref, b_ref, dy_ref, ck_ref, *rest):
        dr_ref, dw_ref, dk_ref, dv_ref, da_ref, db_ref, ds_ref = rest[len(deps):]
        c = pl.program_id(0)

        @pl.when(c == 0)
        def _():
            ds_ref[...] = jnp.zeros(ds_ref.shape, F32)

        _, vjp = jax.vjp(lambda *t: _wkv_block(*t, True), ck_ref[0], r_ref[...], w_ref[...], k_ref[...], v_ref[...],
                         a_ref[...], b_ref[...])
        g = vjp((dy_ref[...], ds_ref[...]))
        ds_ref[...] = g[0]
        for o_ref, val in zip((dr_ref, dw_ref, dk_ref, dv_ref, da_ref, db_ref), g[1:]):
            o_ref[...] = val

    blk = pl.BlockSpec((N_HEAD, WKV_BLOCK, HEAD), lambda c: (0, nc - 1 - c, 0))
    sh = jax.ShapeDtypeStruct((N_HEAD, L, HEAD), F32)
    return _pcall(
        body, name="wkv_bwd", grid=(nc,),
        in_specs=[blk] * 7 + [pl.BlockSpec((1, N_HEAD, HEAD, HEAD), lambda c: (nc - 1 - c, 0, 0, 0))]
        + [pl.BlockSpec(d.shape, lambda c, nd=d.ndim: (0,) * nd) for d in deps],
        out_specs=[blk] * 6, out_shape=[sh] * 6,
        scratch_shapes=[pltpu.VMEM((N_HEAD, HEAD, HEAD), F32)],
        compiler_params=_cparams(1))(r, w, k, v, a, b, dy, ck, *deps)


TB = 256


def _bf(x):
    return x.astype(BF16)


def _inproj_fwd(x, norm_mix, w_in, L, deps=()):
    def fn(i, tv, cv):
        xn = _rms(tv[0], cv[0])
        return _dot(_bf(xn), cv[1]), xn

    return _tok_call("inproj_fwd", fn, L, TB, [(x, D_MODEL, 0)], [norm_mix, w_in], [(IN_COLS, F32), (D_MODEL, BF16)],
                     deps=deps)


def _s5_post_fn(glu_w, wtop, diff=True):
    mg = _mmc(glu_w, diff)
    mt = _mmc(wtop, diff) if wtop is not None else None

    def f(y, glu_b, e):
        z = _gelu(y)
        out = z * _sigmoid(mg(z) + glu_b + e)
        res = mt(out) if mt is not None else out
        return res, (z, out)

    return f


def _s5_post_fwd(y, glu_w, glu_b, L):
    def fn(i, tv, cv):
        out, _ = _s5_post_fn(cv[0], None, False)(tv[0], cv[1], 0.0)
        return (out,)

    return _tok_call("s5_post_fwd", fn, L, TB, [(y, S5_WIDTH, 0)], [glu_w, glu_b], [(S5_WIDTH, F32)])[0]


def _s5_post_bwd(y, dh1, glu_w, glu_b, wtop, L, deps=()):
    def fn(i, tv, cv):
        e0 = jnp.zeros((TB, S5_WIDTH), F32)
        _, vjp, (z, out) = jax.vjp(_s5_post_fn(cv[0], cv[2]), tv[0], cv[1], e0, has_aux=True)
        dy, db, de = vjp(tv[1])
        return dy, z, de, out, db

    return _tok_call("s5_post_bwd", fn, L, TB, [(y, S5_WIDTH, 0), (dh1, D_MODEL, 0)], [glu_w, glu_b, wtop],
                     [(S5_WIDTH, F32), (S5_WIDTH, BF16), (S5_WIDTH, BF16), (S5_WIDTH, BF16)], [(1, S5_WIDTH)], deps=deps)


RW_COLBLK = ((RW_WIDTH, 1), (RW_WIDTH, 2), (RW_WIDTH, 3), (128, 16), (128, 17))
RW_MU = ((0, 512), (512, 1024), (1024, 1536), (1536, 1664), (1664, 1792))


def _rw_pre_fn(w2pad, a2pad, g2, diff=True):
    m_w, m_a, m_g = _mmc(w2pad, diff), _mmc(a2pad, diff), _mmc(g2, diff)
    seg = _segsum(_head_indicator(RW_WIDTH), diff)

    def f(zr, zk, zv, zwa, zg, w0, a0, k_k, k_a, e_w, e_a):
        wl_t = jnp.tanh(zwa)
        wlin = w0 + m_w(wl_t) + e_w
        w = -_softplus(-wlin) - 0.5
        decay = jnp.exp(-jnp.exp(w))
        a = _sigmoid(a0 + m_a(zwa) + e_a)
        sg = _sigmoid(zg)
        g = m_g(sg)
        kk = zk * k_k
        kkn = kk / jnp.maximum(jnp.sqrt(seg(kk * kk)), L2_EPS)
        kf = zk * (1.0 + (a - 1.0) * k_a)
        return (zr, decay, kf, zv, -kkn, kkn * a, g), (wl_t, sg)

    return f


def _rw_shifted(i, tv, mu):
    sub = lax.broadcasted_iota(jnp.int32, (TB, 1), 0)
    zs, dif = [], []
    for n in range(5):
        z = tv[n]
        last = jnp.where(i == 0, 0.0, tv[5 + n][7:8, :])
        prev = jnp.where(sub == 0, last, pltpu.roll(z, 1, 0))
        m = mu[:, RW_MU[n][0]:RW_MU[n][1]]
        zs.append(z + (prev - z) * m)
        dif.append(prev - z)
    return zs, dif


def _rw_tok_in(proj):
    return [(proj, wd, cb) for wd, cb in RW_COLBLK] + [(proj, wd, cb, "prev") for wd, cb in RW_COLBLK]


def _rw_pre_fwd(proj, mu, w0, a0, k_k, k_a, w2pad, a2pad, g2, L):
    def fn(i, tv, cv):
        zs, _ = _rw_shifted(i, tv, cv[0])
        outs, _ = _rw_pre_fn(cv[5], cv[6], cv[7], False)(*zs, cv[1], cv[2], cv[3], cv[4], 0.0, 0.0)
        return outs

    return _tok_call("rw_pre_fwd", fn, L, TB, _rw_tok_in(proj), [mu, w0, a0, k_k, k_a, w2pad, a2pad, g2],
                     [("heads", F32)] * 6 + [(RW_WIDTH, F32)])


def _rw_pre_bwd(proj, cots, mu, w0, a0, k_k, k_a, w2pad, a2pad, g2, L):
    def fn(i, tv, cv):
        zs, dif = _rw_shifted(i, tv[:10], cv[0])
        dr1, dr2, dw, dk1, dk2, dv1, dv2, da, db, dg = tv[10:]
        e0 = jnp.zeros((TB, RW_WIDTH), F32)
        _, vjp, (wl_t, sg) = jax.vjp(_rw_pre_fn(cv[5], cv[6], cv[7]), *zs, cv[1], cv[2], cv[3], cv[4], e0, e0, has_aux=True)
        g = vjp((dr1 + dr2, dw, dk1 + dk2, dv1 + dv2, da, db, dg))
        dzs = jnp.concatenate(g[:5], axis=1)
        dmu = jnp.concatenate([jnp.sum(g[n] * dif[n], axis=0, keepdims=True) for n in range(5)], axis=1)
        return dzs, wl_t, zs[3], sg, g[9], g[10], dmu, g[5], g[6], g[7], g[8]

    tok_in = _rw_tok_in(proj) + [((c,) if c.ndim == 3 else (c, RW_WIDTH, 0)) for c in cots]
    return _tok_call("rw_pre_bwd", fn, L, TB, tok_in, [mu, w0, a0, k_k, k_a, w2pad, a2pad, g2],
                     [(SHIFT_COLS, F32), (128, BF16), (128, BF16), (128, BF16), (RW_WIDTH, BF16), (RW_WIDTH, BF16)],
                     [(1, SHIFT_COLS)] + [(1, RW_WIDTH)] * 4)


def _rw_post_fn(wbot, diff=True):
    seg = _segsum(_head_indicator(RW_WIDTH), diff)
    mb = _mmc(wbot, diff) if wbot is not None else None

    def f(y, r, kf, v, g, ln_w, ln_b, r_k):
        mean = seg(y) * (1.0 / HEAD)
        yc = y - mean
        var = seg(yc * yc) * (1.0 / HEAD)
        yn = yc * lax.rsqrt(var + GN_EPS) * ln_w + ln_b
        bonus = seg(r * kf * r_k) * v
        out = (yn + bonus) * g
        res = mb(out) if mb is not None else out
        return res, out

    return f


def _rw_post_fwd(y, r, kf, v, g, ln_w, ln_b, r_k, L):
    def fn(i, tv, cv):
        out, _ = _rw_post_fn(None, False)(*tv, *cv)
        return (out,)

    return _tok_call("rw_post_fwd", fn, L, TB, [(t,) for t in (y, r, kf, v)] + [(g, RW_WIDTH, 0)], [ln_w, ln_b, r_k],
                     [(RW_WIDTH, F32)])[0]


def _rw_post_bwd(y, r, kf, v, g, dh1, ln_w, ln_b, r_k, wbot, L):
    def fn(i, tv, cv):
        _, vjp, out = jax.vjp(_rw_post_fn(cv[3]), *tv[:5], cv[0], cv[1], cv[2], has_aux=True)
        gr = vjp(tv[5])
        return gr[0], gr[1], gr[2], gr[3], gr[4], out, gr[5], gr[6], gr[7]

    return _tok_call("rw_post_bwd", fn, L, TB, [(t,) for t in (y, r, kf, v)] + [(g, RW_WIDTH, 0), (dh1, D_MODEL, 0)],
                     [ln_w, ln_b, r_k, wbot], [("heads", F32)] + [(RW_WIDTH, F32)] * 4 + [(RW_WIDTH, BF16)], [(1, RW_WIDTH)] * 3)


def _ffn_fn(w1, w3, w2, diff=True):
    m1, m3, m2 = _mmc(w1, diff), _mmc(w3, diff), _mmc(w2, diff)

    def f(h1, norm_ffn, e1, e3):
        hn = _rms(h1, norm_ffn)
        a1 = m1(hn) + e1
        a3 = m3(hn) + e3
        hm = a1 * _sigmoid(a1) * a3
        return h1 + m2(hm), (hn, hm)

    return f


TB_FFN = 256


def _mixffn_fwd(x, s5_out, rw_out, wtop, wbot, norm_ffn, w1, w3, w2, L):
    def fn(i, tv, cv):
        h1 = tv[0] + _dot(_bf(tv[1]), cv[0]) + _dot(_bf(tv[2]), cv[1])
        h2, _ = _ffn_fn(cv[3], cv[4], cv[5], False)(h1, cv[2], 0.0, 0.0)
        return h1, h2

    return _tok_call("mixffn_fwd", fn, L, TB_FFN, [(x, D_MODEL, 0), (s5_out, S5_WIDTH, 0), (rw_out, RW_WIDTH, 0)],
                     [wtop, wbot, norm_ffn, w1, w3, w2], [(D_MODEL, F32), (D_MODEL, F32)])


def _ffn_bwd(h1, dh2, norm_ffn, w1, w3, w2, L):
    def fn(i, tv, cv):
        e0 = jnp.zeros((TB_FFN, FFN_HIDDEN), F32)
        _, vjp, (hn, hm) = jax.vjp(_ffn_fn(cv[1], cv[2], cv[3]), tv[0], cv[0], e0, e0, has_aux=True)
        dh1, dn, d1, d3 = vjp(tv[1])
        return dh1, d1, d3, hm, hn, dn

    return _tok_call("ffn_bwd", fn, L, TB_FFN, [(h1, D_MODEL, 0), (dh2, D_MODEL, 0)], [norm_ffn, w1, w3, w2],
                     [(D_MODEL, F32), (FFN_HIDDEN, BF16), (FFN_HIDDEN, BF16), (FFN_HIDDEN, BF16), (D_MODEL, BF16)],
                     [(1, D_MODEL)])


def _ple_loss_fb(h2, p, target, norm_ple, final_norm, wg, wu, L):
    def fn(i, tv, cv):
        mgate, mup = _mmc(cv[2]), _mmc(cv[3], False)

        def f(h2_, norm_ple_, final_norm_, eg, eu):
            hn = _rms(h2_, norm_ple_)
            gate = _sigmoid(mgate(hn) + eg)
            h3 = h2_ + gate * (mup(tv[1]) + eu)
            out = _rms(h3, final_norm_)
            d = out - tv[2]
            return 0.5 * jnp.sum(jnp.mean(d * d, axis=-1, keepdims=True)), hn

        e0 = jnp.zeros((TB, D_MODEL), F32)
        loss, vjp, hn = jax.vjp(f, tv[0], cv[0], cv[1], e0, e0, has_aux=True)
        dh2, dnp, dfn, deg, deu = vjp(jnp.ones((), F32))
        return dh2, dh2, deg, deu, hn, jnp.full((8, 128), loss, F32), dnp, dfn

    return _tok_call("ple_loss_fb", fn, L, TB, [(h2, D_MODEL, 0), (p, PLE_DIM, 0), (target, D_MODEL, 0)],
                     [norm_ple, final_norm, wg, wu],
                     [(D_MODEL, F32), (D_MODEL, BF16), (D_MODEL, BF16), (D_MODEL, BF16), (D_MODEL, BF16)],
                     [(8, 128), (1, D_MODEL), (1, D_MODEL)])


def _inproj_bwd(x, dh1, du, dzs, norm_mix, mu, w_u, w_z, L):
    nb = L // TB

    def fn(i, tv, cv):
        sub = lax.broadcasted_iota(jnp.int32, (TB, 1), 0)
        m = cv[1]
        b = tv[3] * m
        nxt = jnp.where(i == nb - 1, 0.0, tv[4][0:1, :] * m)
        dz = tv[3] * (1.0 - m) + jnp.where(sub == TB - 1, nxt, pltpu.roll(b, TB - 1, 0))
        dub, dzb = _bf(tv[2]), _bf(dz)
        dxn = _dot_nt(dub, cv[2]) + _dot_nt(dzb, cv[3])
        _, vjp = jax.vjp(_rms, tv[0], cv[0])
        dx, dn = vjp(dxn)
        return tv[1] + dx, jnp.concatenate([dub, dzb], axis=1), dn

    return _tok_call("inproj_bwd", fn, L, TB,
                     [(x, D_MODEL, 0), (dh1, D_MODEL, 0), (du, S5_WIDTH, 0), (dzs, SHIFT_COLS, 0), (dzs, SHIFT_COLS, 0, "next")],
                     [norm_mix, mu, w_u, w_z], [(D_MODEL, F32), (IN_COLS, BF16)], [(1, D_MODEL)])


def _eye8(dt):
    return jnp.eye(8, dtype=dt)


def _quarter_b(bb):
    return jnp.einsum("hg,qgcp->qhcgp", _eye8(bb.dtype), bb.reshape(S5_Q, 8, S5_GROUP, S5_STATE)).reshape(S5_Q, S5_QL, S5_QS)


def _unquarter_b(d):
    return jnp.einsum("qhcgp,hg->qgcp", d.reshape(S5_Q, 8, S5_GROUP, 8, S5_STATE), _eye8(d.dtype)).reshape(
        S5_GROUPS, S5_GROUP, S5_STATE)


def _quarter_c(c):
    return jnp.einsum("gh,qgcp->qgphc", _eye8(c.dtype), c.reshape(S5_Q, 8, S5_GROUP, S5_STATE)).reshape(S5_Q, S5_QS, S5_QL)


def _unquarter_c(d):
    return jnp.einsum("qgphc,gh->qgcp", d.reshape(S5_Q, 8, S5_STATE, 8, S5_GROUP), _eye8(d.dtype)).reshape(
        S5_GROUPS, S5_GROUP, S5_STATE)


def _local_step(x, p, target, W, late_weights=None, grads_ready=None, first_dep=None):
    L = x.shape[0]
    r2 = lambda v: v.reshape(1, -1)
    w_in = W["w_in"]
    w2pad = jnp.pad(W["rw_w2"], ((0, 64), (0, 0)))
    a2pad = jnp.pad(W["rw_a2"], ((64, 0), (0, 0)))
    mu = r2(W["rw_shift_mu"])
    rw_vec = [r2(W[n]) for n in ("rw_w0", "rw_a0", "rw_k_k", "rw_k_a")]
    ln_w, ln_b, r_k = r2(W["rw_ln_w"]), r2(W["rw_ln_b"]), r2(W["rw_r_k"])

    lam_re, lam_im = W["s5_lam_re"], W["s5_lam_im"]
    log_step = W["s5_log_step"].reshape(S5_GROUPS, 1)
    bt_re, bt_im = W["s5_b_re"].transpose(0, 2, 1), W["s5_b_im"].transpose(0, 2, 1)
    lb_re, lb_im, bb_re, bb_im = _s5_param_fwd(lam_re, lam_im, log_step, bt_re, bt_im)
    bq_re, bq_im = _quarter_b(bb_re).astype(BF16), _quarter_b(bb_im).astype(BF16)
    cq_re, cq_im = _quarter_c(W["s5_c_re"]).astype(BF16), _quarter_c(W["s5_c_im"]).astype(BF16)
    lbar = jnp.concatenate([lb_re.reshape(1, -1), lb_im.reshape(1, -1), jnp.zeros((6, S5_LANES), F32)], axis=0)
    dskip = r2(W["s5_d"])
    glu_b = r2(W["s5_glu_b"])
    norm_mix, norm_ffn, norm_ple, final_norm = (r2(W[n]) for n in ("norm_mix", "norm_ffn", "norm_ple", "final_norm"))

    proj, xn = _inproj_fwd(x, norm_mix, w_in, L, () if first_dep is None else (first_dep,))
    y_s5, ck5 = _s5_scan_fwd(proj, bq_re, bq_im, cq_re, cq_im, lbar, dskip, L, TB)
    s5_out = _s5_post_fwd(y_s5, W["s5_glu_w"], glu_b, L)
    r, wd, kf, v, a_s, b_s, g = _rw_pre_fwd(proj, mu, *rw_vec, w2pad, a2pad, W["rw_g2"], L)
    scan_in = (r, wd, kf, v, a_s, b_s)
    y_wkv, ckw = _wkv_fwd(*scan_in, L)
    rw_out = _rw_post_fwd(y_wkv, r, kf, v, g, ln_w, ln_b, r_k, L)
    if late_weights is not None:
        W = dict(W, **late_weights(rw_out))
    wtop, wbot = W["w_out"][:S5_WIDTH], W["w_out"][S5_WIDTH:]
    h1, h2 = _mixffn_fwd(x, s5_out, rw_out, wtop, wbot, norm_ffn, W["ffn_w1"], W["ffn_w3"], W["ffn_w2"], L)

    G = {}
    dh2, dh2_bf, deg, deu, hn_ple, loss_acc, G["norm_ple"], G["final_norm"] = _ple_loss_fb(
        h2, p, target, norm_ple, final_norm, W["ple_gate_w"], W["ple_up_w"], L)
    dh1, da1, da3, hm, hn_ffn, G["norm_ffn"] = _ffn_bwd(h1, dh2, norm_ffn, W["ffn_w1"], W["ffn_w3"], W["ffn_w2"], L)
    Gb = {}
    G["ffn_w1"], Gb["ffn_w1"] = _mm_tn("dw_ffn_w1", hn_ffn, da1, True)
    G["ffn_w3"], Gb["ffn_w3"] = _mm_tn("dw_ffn_w3", hn_ffn, da3, True)
    G["ffn_w2"], Gb["ffn_w2"] = _mm_tn("dw_ffn_w2", hm, dh2_bf, True)
    G["ple_gate_w"], Gb["ple_gate_w"] = _mm_tn("dw_ple_gate", hn_ple, deg, True)
    dep_a = grads_ready(0, G, Gb) if grads_ready is not None else None
    dy_s5, z_bf, dgp, s5o_bf, G["s5_glu_b"] = _s5_post_bwd(y_s5, dh1, W["s5_glu_w"], glu_b, wtop, L,
                                                           () if dep_a is None else (dep_a,))
    dy_wkv, dr2, dk2, dv2, dg, rwo_bf, G["rw_ln_w"], G["rw_ln_b"], G["rw_r_k"] = _rw_post_bwd(
        y_wkv, r, kf, v, g, dh1, ln_w, ln_b, r_k, wbot, L)
    top, top_b = _mm_tn("dw_out_top", s5o_bf, dh1, True)
    bot, bot_b = _mm_tn("dw_out_bot", rwo_bf, dh1, True)
    G["w_out"], Gb["w_out"] = jnp.concatenate([top, bot], axis=0), jnp.concatenate([top_b, bot_b], axis=0)
    dep = grads_ready(1, G, Gb) if grads_ready is not None else None
    G["ple_up_w"] = _mm_tn("dw_ple_up", p, deu)
    G["s5_glu_w"] = _mm_tn("dw_s5_glu", z_bf, dgp)
    dr1, dwd, dk1, dv1, da_s, db_s = _wkv_bwd(*scan_in, dy_wkv, ckw, L, () if dep is None else (dep,))
    (dzs, wl_t, zwa, sg, dwlin, dalin, G["rw_shift_mu"], G["rw_w0"], G["rw_a0"], G["rw_k_k"], G["rw_k_a"]) = _rw_pre_bwd(
        proj, (dr1, dr2, dwd, dk1, dk2, dv1, dv2, da_s, db_s, dg), mu, *rw_vec, w2pad, a2pad, W["rw_g2"], L)
    G["rw_w2"] = _mm_tn("dw_rw_w2", wl_t, dwlin)[:64]
    G["rw_a2"] = _mm_tn("dw_rw_a2", zwa, dalin)[64:]
    G["rw_g2"] = _mm_tn("dw_rw_g2", sg, dg)
    du, dbq_re, dbq_im, dcq_re, dcq_im, dlbar, G["s5_d"] = _s5_scan_bwd(
        proj, dy_s5, ck5, bq_re, bq_im, cq_re, cq_im, lbar, dskip, L, TB)
    G["s5_c_re"], G["s5_c_im"] = _unquarter_c(dcq_re), _unquarter_c(dcq_im)
    d_lam_re, d_lam_im, d_ls, d_bt_re, d_bt_im = _s5_param_bwd(
        lam_re, lam_im, log_step, bt_re, bt_im, dlbar[0].reshape(S5_GROUPS, S5_STATE), dlbar[1].reshape(S5_GROUPS, S5_STATE),
        _unquarter_b(dbq_re), _unquarter_b(dbq_im))
    G["s5_lam_re"], G["s5_lam_im"], G["s5_log_step"] = d_lam_re, d_lam_im, d_ls.reshape(S5_GROUPS)
    G["s5_b_re"], G["s5_b_im"] = d_bt_re.transpose(0, 2, 1), d_bt_im.transpose(0, 2, 1)
    dx, dproj, G["norm_mix"] = _inproj_bwd(x, dh1, du, dzs, norm_mix, mu, w_in[:, :S5_WIDTH], w_in[:, S5_WIDTH:], L)
    G["w_in"], Gb["w_in"] = _mm_tn("dw_in", xn, dproj, True)
    return loss_acc[0, 0], dx, G, Gb


MESH_AXES = ("x", "y", "c")


def _all_gather(name, shards):
    nt = len(shards)

    def body(*refs):
        x_refs, out_refs = refs[:nt], refs[nt:2 * nt]
        send_sems, recv_sems, local_sems = refs[2 * nt:]
        x, y, c = lax.axis_index("x"), lax.axis_index("y"), lax.axis_index("c")
        me, sibling = (x, y, c), (x, y, 1 - c)
        chips = [(1 - x, y), (x, 1 - y), (1 - x, 1 - y)]

        def rows(t, px, py, pc):
            m_per = shards[t].shape[0]
            return out_refs[t].at[pl.ds((4 * px + 2 * py + pc) * m_per, m_per), :]

        def copy(t, k, block, to, src=None):
            return pltpu.make_async_remote_copy(
                src_ref=rows(t, *block) if src is None else src, dst_ref=rows(t, *block),
                send_sem=send_sems.at[7 * t + k], recv_sem=recv_sems.at[7 * t + k],
                device_id=to, device_id_type=pl.DeviceIdType.MESH)

        mine = [pltpu.make_async_copy(x_refs[t], rows(t, *me), local_sems.at[t]) for t in range(nt)]
        for cp in mine:
            cp.start()
        first = []
        for t in range(nt):
            first.append(copy(t, 0, me, sibling, src=x_refs[t]))
            first += [copy(t, 1 + j, me, (*chip, c), src=x_refs[t]) for j, chip in enumerate(chips)]
        for cp in first:
            cp.start()
        passed = []
        for t in range(nt):
            for j, chip in enumerate(chips):
                copy(t, 1 + j, (*chip, c), me).wait_recv()
                fwd = copy(t, 4 + j, (*chip, c), sibling)
                fwd.start()
                passed.append(fwd)
        for t in range(nt):
            copy(t, 0, sibling, me).wait_recv()
            for j, chip in enumerate(chips):
                copy(t, 4 + j, (*chip, 1 - c), me).wait_recv()
        for cp in first + passed:
            cp.wait_send()
        for cp in mine:
            cp.wait()

    return _pcall(body, name=name,
                  out_shape=[jax.ShapeDtypeStruct((N_DEV * a.shape[0], a.shape[1]), a.dtype) for a in shards],
                  in_specs=[_ANY] * nt, out_specs=[_ANY] * nt,
                  scratch_shapes=[pltpu.SemaphoreType.DMA((7 * nt,)), pltpu.SemaphoreType.DMA((7 * nt,)),
                                  pltpu.SemaphoreType.DMA((nt,))])(*shards)


_HBM = pl.BlockSpec(memory_space=pltpu.HBM)
_SEM = pl.BlockSpec(memory_space=pltpu.SEMAPHORE)
_EFFECT = pltpu.SideEffectType.DATAFLOW_SIDE_EFFECTING


def _peer_of(k):
    x, y, c = lax.axis_index("x"), lax.axis_index("y"), lax.axis_index("c")
    px, py, pc = x ^ ((k >> 2) & 1), y ^ ((k >> 1) & 1), c ^ (k & 1)
    return (px, py, pc), 4 * px + 2 * py + pc, 4 * x + 2 * y + c


def _direct_copy(t, k, src_refs, land_refs, send_sems, recv_sems, rows_of, gather):
    dev, peer, me = _peer_of(k)
    m = rows_of[t]
    src = src_refs[t] if gather else src_refs[t].at[pl.ds(peer * m, m), :]
    return pltpu.make_async_remote_copy(
        src_ref=src, dst_ref=land_refs[t].at[pl.ds(me * m, m), :],
        send_sem=send_sems.at[7 * t + k - 1], recv_sem=recv_sems.at[7 * t + k - 1],
        device_id=dev, device_id_type=pl.DeviceIdType.MESH)


def _direct_landing(t, k, src_refs, land_refs, send_sems, recv_sems, rows_of, gather):
    dev, peer, me = _peer_of(k)
    m = rows_of[t]
    src = src_refs[t] if gather else src_refs[t].at[pl.ds(me * m, m), :]
    return pltpu.make_async_remote_copy(
        src_ref=src, dst_ref=land_refs[t].at[pl.ds(peer * m, m), :],
        send_sem=send_sems.at[7 * t + k - 1], recv_sem=recv_sems.at[7 * t + k - 1],
        device_id=dev, device_id_type=pl.DeviceIdType.MESH)


def _direct_start(name, srcs, gather, dep=None):
    nt = len(srcs)
    rows_of = [a.shape[0] if gather else a.shape[0] // N_DEV for a in srcs]
    lands = [pltpu.with_memory_space_constraint(lax.empty((N_DEV * m, a.shape[1]), a.dtype), pltpu.HBM)
             for a, m in zip(srcs, rows_of)]

    n_dep = 0 if dep is None else 1

    def body(*refs):
        src_refs, land_refs = refs[:nt], refs[nt:2 * nt]
        send_sems, recv_sems = refs[2 * nt + n_dep], refs[2 * nt + n_dep + 1]
        token = refs[-1]
        for t in range(nt):
            for k in range(1, N_DEV):
                _direct_copy(t, k, src_refs, land_refs, send_sems, recv_sems, rows_of, gather).start()
        token[...] = jnp.zeros(token.shape, F32)

    out = _pcall(
        body, name=name,
        out_shape=(pltpu.SemaphoreType.DMA((7 * nt,)), pltpu.SemaphoreType.DMA((7 * nt,)),
                   *[pltpu.HBM(a.shape, a.dtype) for a in srcs], *[pltpu.HBM(a.shape, a.dtype) for a in lands],
                   jax.ShapeDtypeStruct((8, 128), F32)),
        in_specs=(_HBM,) * (2 * nt) + (pl.BlockSpec(memory_space=pl.ANY),) * n_dep,
        out_specs=(_SEM, _SEM) + (_HBM,) * (2 * nt) + (pl.BlockSpec(memory_space=pltpu.VMEM),),
        input_output_aliases={i: 2 + i for i in range(2 * nt)},
        compiler_params=pltpu.CompilerParams(has_side_effects=_EFFECT),
    )(*[pltpu.with_memory_space_constraint(a, pltpu.HBM) for a in srcs], *lands, *(() if dep is None else (dep,)))
    return (out[0], out[1], list(out[2:2 + nt]), list(out[2 + nt:2 + 2 * nt]), rows_of, gather), out[-1]


def _direct_wait(name, handle, after):
    send_sems, recv_sems, srcs, lands, rows_of, gather = handle
    nt = len(srcs)
    after = list(after) if isinstance(after, (list, tuple)) else [after]

    def body(*refs):
        src_refs, land_refs = refs[:nt], refs[nt:2 * nt]
        s_sems, r_sems = refs[2 * nt], refs[2 * nt + 1]
        for t in range(nt):
            for k in range(1, N_DEV):
                _direct_copy(t, k, src_refs, land_refs, s_sems, r_sems, rows_of, gather).wait_send()
                _direct_landing(t, k, src_refs, land_refs, s_sems, r_sems, rows_of, gather).wait_recv()

    out = _pcall(
        body, name=name,
        out_shape=tuple(pltpu.HBM(a.shape, a.dtype) for a in srcs) + tuple(pltpu.HBM(a.shape, a.dtype) for a in lands),
        in_specs=(_HBM,) * (2 * nt) + (_SEM, _SEM) + (pl.BlockSpec(memory_space=pl.ANY),) * len(after),
        out_specs=(_HBM,) * (2 * nt),
        input_output_aliases={i: i for i in range(2 * nt)},
        compiler_params=pltpu.CompilerParams(has_side_effects=_EFFECT),
    )(*srcs, *lands, send_sems, recv_sems, *after)
    return list(out[:nt]), list(out[nt:])


def _adamw_sharded(name, own, parts, w, m, v, rb):
    R, N = own.shape

    def body(o_ref, p_ref, w_ref, m_ref, v_ref, g_ref, d_ref, nm_ref, nv_ref):
        me = 4 * lax.axis_index("x") + 2 * lax.axis_index("y") + lax.axis_index("c")
        g = o_ref[...]
        for k in range(1, N_DEV):
            g = g + p_ref[me ^ k].astype(F32)
        nm = ADAM_B1 * m_ref[...] + (1.0 - ADAM_B1) * g
        nv = ADAM_B2 * v_ref[...] + (1.0 - ADAM_B2) * (g * g)
        m_hat = nm / (1.0 - ADAM_B1 ** ADAM_STEP)
        v_hat = nv / (1.0 - ADAM_B2 ** ADAM_STEP)
        g_ref[...] = g
        d_ref[...] = -ADAM_LR * (m_hat / (jnp.sqrt(v_hat) + ADAM_EPS) + ADAM_WD * w_ref[...])
        nm_ref[...] = nm
        nv_ref[...] = nv

    blk = pl.BlockSpec((rb, N), lambda i: (i, 0))
    sh = jax.ShapeDtypeStruct((R, N), F32)
    return _pcall(body, name=name, grid=(R // rb,),
                  in_specs=[blk, pl.BlockSpec((N_DEV, rb, N), lambda i: (0, i, 0)), blk, blk, blk],
                  out_specs=[blk] * 4, out_shape=[sh] * 4, compiler_params=_cparams(1))(own, parts, w, m, v)


def _adamw(name, parts, w, m, v, rb):
    _, R, N = parts.shape

    def body(p_ref, w_ref, m_ref, v_ref, g_ref, d_ref, nm_ref, nv_ref):
        g = p_ref[0]
        for s in range(1, N_DEV):
            g = g + p_ref[s]
        nm = ADAM_B1 * m_ref[...] + (1.0 - ADAM_B1) * g
        nv = ADAM_B2 * v_ref[...] + (1.0 - ADAM_B2) * (g * g)
        m_hat = nm / (1.0 - ADAM_B1 ** ADAM_STEP)
        v_hat = nv / (1.0 - ADAM_B2 ** ADAM_STEP)
        g_ref[...] = g
        d_ref[...] = -ADAM_LR * (m_hat / (jnp.sqrt(v_hat) + ADAM_EPS) + ADAM_WD * w_ref[...])
        nm_ref[...] = nm
        nv_ref[...] = nv

    blk = pl.BlockSpec((rb, N), lambda i: (i, 0))
    sh = jax.ShapeDtypeStruct((R, N), F32)
    return _pcall(body, name=name, grid=(R // rb,), in_specs=[pl.BlockSpec((N_DEV, rb, N), lambda i: (0, i, 0)), blk, blk, blk],
                  out_specs=[blk] * 4, out_shape=[sh] * 4, compiler_params=_cparams(1))(parts, w, m, v)


EARLY = (("w_in", True),)
LATE = (("ffn_w1", True), ("ffn_w3", True), ("ffn_w2", False), ("ple_gate_w", False), ("w_out", False))
GRAD_STAGES = (LATE[:4], LATE[4:])
MISC = (("s5_glu_w", False), ("rw_w2", True), ("rw_a2", True), ("rw_g2", True), ("ple_up_w", True))
SHARDED_NAMES = tuple(n for n, _ in EARLY + LATE + MISC)
PACK_COLS = 1024
SMALL_ROWS = 144
WEIGHT_NAMES = ("norm_mix", "w_in", "s5_lam_re", "s5_lam_im", "s5_log_step", "s5_b_re", "s5_b_im", "s5_c_re", "s5_c_im", "s5_d",
                "s5_glu_w", "s5_glu_b", "rw_shift_mu", "rw_w0", "rw_w2", "rw_a0", "rw_a2", "rw_g2", "rw_k_k", "rw_k_a", "rw_r_k",
                "rw_ln_w", "rw_ln_b", "w_out", "norm_ffn", "ffn_w1", "ffn_w3", "ffn_w2", "norm_ple", "ple_gate_w", "ple_up_w",
                "final_norm")
SMALL_NAMES = tuple(n for n in WEIGHT_NAMES if n not in SHARDED_NAMES)
ARG_NAMES = ("x", "p") + WEIGHT_NAMES + ("loss_target",) + tuple("m_" + n for n in WEIGHT_NAMES) + tuple("v_" + n for n in WEIGHT_NAMES)


def _travel(a, tr):
    return a.T if tr else a


def _pack_misc(blocks):
    lead = blocks[0].shape[:-2]
    return jnp.concatenate([b.reshape(lead + (-1, PACK_COLS)) for b in blocks], axis=len(lead))


def _unpack_misc(packed, shapes):
    lead = packed.shape[:-2]
    out, off = [], 0
    for r, c in shapes:
        n = r * c // PACK_COLS
        out.append(lax.slice_in_dim(packed, off, off + n, axis=len(lead)).reshape(lead + (r, c)))
        off += n
    return out


def _pack_small(arrs):
    flat = jnp.concatenate([a.reshape(-1).astype(F32) for a in arrs])
    return jnp.pad(flat, (0, SMALL_ROWS * PACK_COLS - flat.shape[0])).reshape(SMALL_ROWS, PACK_COLS)


def _kernel_impl(ins):
    x, p, target = ins["x"][0], ins["p"][0, 0], ins["loss_target"][0]
    me = 4 * lax.axis_index("x") + 2 * lax.axis_index("y") + lax.axis_index("c")
    small = {n: (ins[n] if n == "final_norm" else ins[n][0]) for n in SMALL_NAMES}
    trav = lambda pre, n, tr: _travel(ins[pre + n][0], tr)
    misc_shapes = [trav("", n, tr).shape for n, tr in MISC]

    early = _all_gather("ag_early", [trav("", n, tr).astype(BF16) for n, tr in EARLY]
                        + [_pack_misc([trav("", n, tr).astype(BF16) for n, tr in MISC])])
    late_handle, late_token = _direct_start("ag_late_start", [trav("", n, tr).astype(BF16) for n, tr in LATE], True, early[-1])
    W = dict(small)
    for (n, tr), g in zip(EARLY, early):
        W[n] = _travel(g, tr)
    for (n, tr), g in zip(MISC, _unpack_misc(early[-1].reshape(N_DEV, -1, PACK_COLS), misc_shapes)):
        W[n] = _travel(g.reshape(-1, g.shape[-1]), tr)

    def late_weights(after):
        shards, lands = _direct_wait("ag_late_wait", late_handle, after)
        full = [lax.dynamic_update_slice_in_dim(ld, sh, me * sh.shape[0], axis=0) for ld, sh in zip(lands, shards)]
        return {n: _travel(g, tr) for (n, tr), g in zip(LATE, full)}

    gt = lambda G, n, tr: _travel(G[n], tr)
    started = {}

    def own_block(G, n, tr):
        rows = ins[n].shape[2 if tr else 1]
        return _travel(lax.dynamic_slice_in_dim(G[n], me * rows, rows, axis=1 if tr else 0), tr)

    def grads_ready(stage, G, Gb):
        owns = [own_block(G, n, tr) for n, tr in GRAD_STAGES[stage]]
        started[stage] = (owns, *_direct_start("grad_late_start%d" % stage, [gt(Gb, n, tr) for n, tr in GRAD_STAGES[stage]], False))
        return started[stage][2]

    loss_part, dx, G, Gb = _local_step(x, p, target, W, late_weights, grads_ready, late_token)

    misc_g = _pack_misc([gt(G, n, tr).reshape((N_DEV,) + shp) for (n, tr), shp in zip(MISC, misc_shapes)])
    misc_full = misc_g.reshape(-1, PACK_COLS)
    early_own = [own_block(G, n, tr) for n, tr in EARLY] + [lax.dynamic_slice_in_dim(misc_full, me * misc_g.shape[1], misc_g.shape[1], axis=0)]
    early_handle, _ = _direct_start("grad_early_start", [gt(Gb, n, tr) for n, tr in EARLY] + [misc_full.astype(BF16)], False)
    small_own = _pack_small([G[n] for n in SMALL_NAMES])
    small_handle, small_token = _direct_start("grad_small_start", [small_own], True)
    late_own, late_land = [], []
    for stage in range(len(GRAD_STAGES)):
        owns, handle, _ = started[stage]
        _, land = _direct_wait("grad_late_wait%d" % stage, handle, small_token)
        late_own += owns
        late_land += land

    outs = {}

    def emit(names_shapes, res):
        for tag, val in zip(("grad_", "delta_", "new_m_", "new_v_"), res):
            for n, v in names_shapes(val):
                outs[tag + n] = v

    def sharded_update(n, tr, own, land):
        rows = own.shape[0]
        res = _adamw_sharded("adamw_" + n, own, land.reshape(N_DEV, rows, land.shape[1]),
                             trav("", n, tr), trav("m_", n, tr), trav("v_", n, tr), _pick_rows(rows))
        emit(lambda val: [(n, _travel(val, tr).reshape(ins[n].shape))], res)
        return list(res)

    for (n, tr), own, land in zip(LATE, late_own, late_land):
        sharded_update(n, tr, own, land)
    _, early_land = _direct_wait("grad_early_wait", early_handle, list(outs.values()))
    for (n, tr), own, land in zip(EARLY, early_own[:-1], early_land[:-1]):
        sharded_update(n, tr, own, land)
    pm = lambda pre: _pack_misc([trav(pre, n, tr) for n, tr in MISC])
    rows = early_own[-1].shape[0]
    res = _adamw_sharded("adamw_misc", early_own[-1],
                         early_land[-1].reshape(N_DEV, rows, PACK_COLS), pm(""), pm("m_"), pm("v_"), rows)
    emit(lambda val: [(n, _travel(b, tr).reshape(ins[n].shape)) for (n, tr), b in zip(MISC, _unpack_misc(val, misc_shapes))], res)
    ps = lambda pre: _pack_small([ins[pre + n] for n in SMALL_NAMES])
    small_src, small_land = _direct_wait("grad_small_wait", small_handle, res[0])
    gsm = lax.dynamic_update_slice_in_dim(small_land[0], small_src[0], me * SMALL_ROWS, axis=0)
    res = _adamw("adamw_replicated", gsm.reshape(N_DEV, SMALL_ROWS, PACK_COLS), ps(""), ps("m_"), ps("v_"), SMALL_ROWS)

    def split_small(val):
        flat, off, o = val.reshape(-1), 0, []
        for n in SMALL_NAMES:
            o.append((n, flat[off:off + ins[n].size].reshape(ins[n].shape)))
            off += ins[n].size
        return o

    emit(split_small, res)
    loss = lax.psum(loss_part, MESH_AXES)
    res = [loss, dx[None]]
    for tag in ("grad_", "delta_", "new_m_", "new_v_"):
        res += [outs[tag + n] for n in WEIGHT_NAMES]
    return tuple(res)


def _pick_rows(r):
    best = 8
    for b in range(8, 257, 8):
        if r % b == 0:
            best = b
    return best


def kernel(x, p, norm_mix, w_in, s5_lam_re, s5_lam_im, s5_log_step, s5_b_re, s5_b_im, s5_c_re, s5_c_im, s5_d, s5_glu_w, s5_glu_b, rw_shift_mu, rw_w0, rw_w2, rw_a0, rw_a2, rw_g2, rw_k_k, rw_k_a, rw_r_k, rw_ln_w, rw_ln_b, w_out, norm_ffn, ffn_w1, ffn_w3, ffn_w2, norm_ple, ple_gate_w, ple_up_w, final_norm, loss_target, m_norm_mix, m_w_in, m_s5_lam_re, m_s5_lam_im, m_s5_log_step, m_s5_b_re, m_s5_b_im, m_s5_c_re, m_s5_c_im, m_s5_d, m_s5_glu_w, m_s5_glu_b, m_rw_shift_mu, m_rw_w0, m_rw_w2, m_rw_a0, m_rw_a2, m_rw_g2, m_rw_k_k, m_rw_k_a, m_rw_r_k, m_rw_ln_w, m_rw_ln_b, m_w_out, m_norm_ffn, m_ffn_w1, m_ffn_w3, m_ffn_w2, m_norm_ple, m_ple_gate_w, m_ple_up_w, m_final_norm, v_norm_mix, v_w_in, v_s5_lam_re, v_s5_lam_im, v_s5_log_step, v_s5_b_re, v_s5_b_im, v_s5_c_re, v_s5_c_im, v_s5_d, v_s5_glu_w, v_s5_glu_b, v_rw_shift_mu, v_rw_w0, v_rw_w2, v_rw_a0, v_rw_a2, v_rw_g2, v_rw_k_k, v_rw_k_a, v_rw_r_k, v_rw_ln_w, v_rw_ln_b, v_w_out, v_norm_ffn, v_ffn_w1, v_ffn_w3, v_ffn_w2, v_norm_ple, v_ple_gate_w, v_ple_up_w, v_final_norm):
    return _kernel_impl(dict(zip(ARG_NAMES, (x, p, norm_mix, w_in, s5_lam_re, s5_lam_im, s5_log_step, s5_b_re, s5_b_im, s5_c_re, s5_c_im, s5_d, s5_glu_w, s5_glu_b, rw_shift_mu, rw_w0, rw_w2, rw_a0, rw_a2, rw_g2, rw_k_k, rw_k_a, rw_r_k, rw_ln_w, rw_ln_b, w_out, norm_ffn, ffn_w1, ffn_w3, ffn_w2, norm_ple, ple_gate_w, ple_up_w, final_norm, loss_target, m_norm_mix, m_w_in, m_s5_lam_re, m_s5_lam_im, m_s5_log_step, m_s5_b_re, m_s5_b_im, m_s5_c_re, m_s5_c_im, m_s5_d, m_s5_glu_w, m_s5_glu_b, m_rw_shift_mu, m_rw_w0, m_rw_w2, m_rw_a0, m_rw_a2, m_rw_g2, m_rw_k_k, m_rw_k_a, m_rw_r_k, m_rw_ln_w, m_rw_ln_b, m_w_out, m_norm_ffn, m_ffn_w1, m_ffn_w3, m_ffn_w2, m_norm_ple, m_ple_gate_w, m_ple_up_w, m_final_norm, v_norm_mix, v_w_in, v_s5_lam_re, v_s5_lam_im, v_s5_log_step, v_s5_b_re, v_s5_b_im, v_s5_c_re, v_s5_c_im, v_s5_d, v_s5_glu_w, v_s5_glu_b, v_rw_shift_mu, v_rw_w0, v_rw_w2, v_rw_a0, v_rw_a2, v_rw_g2, v_rw_k_k, v_rw_k_a, v_rw_r_k, v_rw_ln_w, v_rw_ln_b, v_w_out, v_norm_ffn, v_ffn_w1, v_ffn_w3, v_ffn_w2, v_norm_ple, v_ple_gate_w, v_ple_up_w, v_final_norm))))
```

```python
import functools

import jax
import jax.numpy as jnp
from jax import lax
from jax.experimental import pallas as pl
from jax.experimental.pallas import tpu as pltpu

F32 = jnp.float32
BF16 = jnp.bfloat16

D_MODEL = 1024
S5_WIDTH = 512
RW_WIDTH = 512
S5_GROUP = 16
S5_GROUPS = 32
S5_STATE = 64
S5_LANES = S5_GROUPS * S5_STATE
HEAD = 64
SHIFT_COLS = 1792
IN_COLS = 2304
FFN_HIDDEN = 2816
PLE_DIM = 256
RMS_EPS = 1e-6
GN_EPS = 64e-5
L2_EPS = 1e-12
CHUNK = 64
N_DEV = 8

ADAM_LR = 0.001
ADAM_B1 = 0.9
ADAM_B2 = 0.999
ADAM_EPS = 1e-08
ADAM_WD = 0.01
ADAM_STEP = 10

VMEM_LIMIT = 56 * 1024 * 1024
_ANY = pl.BlockSpec(memory_space=pl.ANY)


def _pcall(body, **kw):
    return pl.pallas_call(body, **kw)


def _cparams(n_grid):
    return pltpu.CompilerParams(dimension_semantics=("arbitrary",) * n_grid, vmem_limit_bytes=VMEM_LIMIT)


def _dot(a, b):
    return jnp.dot(a, b, preferred_element_type=F32)


def _dot_nt(a, b):
    return lax.dot_general(a, b, (((1,), (1,)), ((), ())), preferred_element_type=F32)


def _dot_tn(a, b):
    return lax.dot_general(a, b, (((0,), (0,)), ((), ())), preferred_element_type=F32)


def _mmc(w, diff=True, tr=False):
    fw, bw = (_dot_nt, _dot) if tr else (_dot, _dot_nt)
    if not diff:
        return lambda x: fw(x.astype(BF16), w)

    @jax.custom_vjp
    def f(x):
        return fw(x.astype(BF16), w)

    def fwd(x):
        return fw(x.astype(BF16), w), None

    def bwd(_, dy):
        return (bw(dy.astype(BF16), w),)

    f.defvjp(fwd, bwd)
    return f


def _split_dot(x, m, n_split):
    acc = None
    rem = x
    for s in range(n_split):
        part = rem.astype(BF16)
        t = _dot(part, m)
        acc = t if acc is None else acc + t
        if s + 1 < n_split:
            rem = rem - part.astype(F32)
    return acc


def _segsum(m, diff=True):
    if not diff:
        return lambda x: _split_dot(x, m, 2)

    @jax.custom_vjp
    def f(x):
        return _split_dot(x, m, 2)

    def fwd(x):
        return _split_dot(x, m, 2), None

    def bwd(_, dy):
        return (_split_dot(dy, m, 2),)

    f.defvjp(fwd, bwd)
    return f


def _head_indicator(n):
    r = lax.broadcasted_iota(jnp.int32, (n, n), 0) // HEAD
    c = lax.broadcasted_iota(jnp.int32, (n, n), 1) // HEAD
    return (r == c).astype(BF16)


def _rms(x, g):
    return x * lax.rsqrt(jnp.mean(x * x, axis=-1, keepdims=True) + RMS_EPS) * g


def _softplus(x):
    return jnp.maximum(x, 0.0) + jnp.log(1.0 + jnp.exp(-jnp.abs(x)))


def _sigmoid(x):
    return 1.0 / (1.0 + jnp.exp(-x))


def _gelu(x):
    return 0.5 * x * (1.0 + jnp.tanh(0.7978845608028654 * (x + 0.044715 * (x * x * x))))


def _tok_call(name, fn, L, TB, tok_in, const_in, tok_out, acc_out=(), deps=()):
    nb = L // TB
    g8 = TB // 8
    in_specs, args = [], []
    for spec in tok_in:
        if len(spec) == 1:
            arr = spec[0]
            in_specs.append(pl.BlockSpec((arr.shape[0], TB, HEAD), lambda i: (0, i, 0)))
            args.append(arr)
            continue
        arr, width, cb = spec[:3]
        mode = spec[3] if len(spec) > 3 else None
        if mode is None:
            in_specs.append(pl.BlockSpec((TB, width), lambda i, cb=cb: (i, cb)))
        elif mode == "prev":
            in_specs.append(pl.BlockSpec((8, width), lambda i, cb=cb: (jnp.maximum(i * g8 - 1, 0), cb)))
        else:
            in_specs.append(pl.BlockSpec((8, width), lambda i, cb=cb: (jnp.minimum((i + 1) * g8, L // 8 - 1), cb)))
        args.append(arr)
    for c in const_in:
        in_specs.append(pl.BlockSpec(c.shape, lambda i, nd=c.ndim: (0,) * nd, pipeline_mode=pl.Buffered(1)))
        args.append(c)
    for d in deps:
        in_specs.append(pl.BlockSpec(d.shape, lambda i, nd=d.ndim: (0,) * nd))
        args.append(d)
    out_shape, out_specs = [], []
    for width, dt in tok_out:
        if width == "heads":
            out_shape.append(jax.ShapeDtypeStruct((N_HEAD, L, HEAD), dt))
            out_specs.append(pl.BlockSpec((N_HEAD, TB, HEAD), lambda i: (0, i, 0)))
            continue
        out_shape.append(jax.ShapeDtypeStruct((L, width), dt))
        out_specs.append(pl.BlockSpec((TB, width), lambda i: (i, 0)))
    for shp in acc_out:
        out_shape.append(jax.ShapeDtypeStruct(shp, F32))
        out_specs.append(pl.BlockSpec(shp, lambda i, nd=len(shp): (0,) * nd))
    n_tok, n_const, n_to = len(tok_in), len(const_in), len(tok_out)

    def body(*refs):
        i = pl.program_id(0)
        tv = [r[...] if len(r.shape) == 2 else jnp.concatenate([r[h] for h in range(r.shape[0])], axis=1)
              for r in refs[:n_tok]]
        cv = [r[...] for r in refs[n_tok:n_tok + n_const]]
        orefs = refs[n_tok + n_const + len(deps):]
        outs = fn(i, tv, cv)
        for r, v in zip(orefs[:n_to], outs[:n_to]):
            if len(r.shape) == 3:
                for h in range(r.shape[0]):
                    r[h] = v[:, h * HEAD:(h + 1) * HEAD].astype(r.dtype)
            else:
                r[...] = v.astype(r.dtype)
        for r, v in zip(orefs[n_to:], outs[n_to:]):
            @pl.when(i == 0)
            def _(r=r):
                r[...] = jnp.zeros(r.shape, r.dtype)

            r[...] += v

    res = _pcall(body, name=name, grid=(nb,), in_specs=in_specs, out_specs=out_specs, out_shape=out_shape,
                 compiler_params=_cparams(1))(*args)
    return res


def _pick_block(n, cap):
    best = None
    for b in range(128, min(n, cap) + 1, 128):
        if n % b == 0:
            best = b
    return best if best is not None else n


def _mm_tn(name, a, b):
    T, M = a.shape
    N = b.shape[1]
    bm, bn, bt = _pick_block(M, 1536), _pick_block(N, 1536), _pick_block(T, 512)

    def body(a_ref, b_ref, o_ref):
        t = pl.program_id(2)

        @pl.when(t == 0)
        def _():
            o_ref[...] = jnp.zeros(o_ref.shape, F32)

        o_ref[...] += _dot_tn(a_ref[...].astype(BF16), b_ref[...].astype(BF16))

    return _pcall(body, name=name, grid=(M // bm, N // bn, T // bt),
                  in_specs=[pl.BlockSpec((bt, bm), lambda m, n, t: (t, m)), pl.BlockSpec((bt, bn), lambda m, n, t: (t, n))],
                  out_specs=pl.BlockSpec((bm, bn), lambda m, n, t: (m, n)),
                  out_shape=jax.ShapeDtypeStruct((M, N), F32), compiler_params=_cparams(3))(a, b)


def _s5_param_fn(lam_re, lam_im, log_step, bt_re, bt_im):
    dt = jnp.exp(log_step)
    e = jnp.exp(lam_re * dt)
    lb_re = e * jnp.cos(lam_im * dt)
    lb_im = e * jnp.sin(lam_im * dt)
    den = lam_re * lam_re + lam_im * lam_im
    nr, ni = lb_re - 1.0, lb_im
    co_re = (nr * lam_re + ni * lam_im) / den
    co_im = (ni * lam_re - nr * lam_im) / den
    cr, ci = co_re[:, None, :], co_im[:, None, :]
    return lb_re, lb_im, cr * bt_re - ci * bt_im, cr * bt_im + ci * bt_re


def _s5_param_fwd(lam_re, lam_im, log_step, bt_re, bt_im):
    def body(a, b, c, d, e, o1, o2, o3, o4):
        r = _s5_param_fn(a[...], b[...], c[...], d[...], e[...])
        o1[...], o2[...], o3[...], o4[...] = r

    sh = jax.ShapeDtypeStruct
    return _pcall(body, name="s5_param_fwd",
                  out_shape=[sh(lam_re.shape, F32), sh(lam_re.shape, F32), sh(bt_re.shape, F32), sh(bt_re.shape, F32)])(
        lam_re, lam_im, log_step, bt_re, bt_im)


def _s5_param_bwd(lam_re, lam_im, log_step, bt_re, bt_im, d_lb_re, d_lb_im, d_bb_re, d_bb_im):
    def body(a, b, c, d, e, g1, g2, g3, g4, o1, o2, o3, o4, o5):
        _, vjp = jax.vjp(_s5_param_fn, a[...], b[...], c[...], d[...], e[...])
        r = vjp((g1[...], g2[...], g3[...], g4[...]))
        o1[...], o2[...], o3[...], o4[...], o5[...] = r

    sh = jax.ShapeDtypeStruct
    return _pcall(body, name="s5_param_bwd",
                  out_shape=[sh(lam_re.shape, F32), sh(lam_re.shape, F32), sh(log_step.shape, F32),
                             sh(bt_re.shape, F32), sh(bt_re.shape, F32)])(
        lam_re, lam_im, log_step, bt_re, bt_im, d_lb_re, d_lb_im, d_bb_re, d_bb_im)


def _cmul(ar, ai, br, bi):
    return ar * br - ai * bi, ar * bi + ai * br


def _scan_consts(lr, li, reverse):
    n = lr.shape[1]
    sub = lax.broadcasted_iota(jnp.int32, (8, n), 0)
    pows = [(lr, li)]
    for _ in range(7):
        pows.append(_cmul(pows[-1][0], pows[-1][1], lr, li))
    steps = []
    for s in (1, 2, 4):
        m = (sub < 8 - s) if reverse else (sub >= s)
        pr, pi = pows[s - 1]
        steps.append((s, jnp.where(m, jnp.broadcast_to(pr, (8, n)), 0.0), jnp.where(m, jnp.broadcast_to(pi, (8, n)), 0.0)))
    wr = jnp.zeros((8, n), F32)
    wi = jnp.zeros((8, n), F32)
    for r in range(8):
        e = (8 - r) if reverse else (r + 1)
        wr = jnp.where(sub == r, jnp.broadcast_to(pows[e - 1][0], (8, n)), wr)
        wi = jnp.where(sub == r, jnp.broadcast_to(pows[e - 1][1], (8, n)), wi)
    return steps, wr, wi


S5_Q = 4
S5_QL = S5_WIDTH // S5_Q
S5_QS = S5_LANES // S5_Q
S5_NT = S5_LANES // 128
S5_QT = S5_QS // 128


def _s5_power_table(lb_ref, pw_re, pw_im, seg):
    for j in range(S5_NT):
        lr = jnp.broadcast_to(lb_ref[0:1, j * 128:(j + 1) * 128], (8, 128))
        li = jnp.broadcast_to(lb_ref[1:2, j * 128:(j + 1) * 128], (8, 128))

        def step(i, c, lr=lr, li=li, j=j):
            pw_re[j, i] = c[0]
            pw_im[j, i] = c[1]
            return _cmul(c[0], c[1], lr, li)

        lax.fori_loop(0, seg, step, (lr, li))


def _seg_scan(sre, sim, carry, lb_ref, pw_re, pw_im, rows, reverse):
    seg = rows // 8
    sgn = -1.0 if reverse else 1.0
    sub = lax.broadcasted_iota(jnp.int32, (8, 128), 0)
    rows_at = lambda i: pl.ds(pl.multiple_of(i * 8, 8), 8)
    entering = {}
    half_tiles = S5_NT // 2
    for half in range(2):
        tiles = list(range(half * half_tiles, (half + 1) * half_tiles))
        lam8 = [(jnp.broadcast_to(lb_ref[0:1, j * 128:(j + 1) * 128], (8, 128)),
                 sgn * jnp.broadcast_to(lb_ref[1:2, j * 128:(j + 1) * 128], (8, 128))) for j in tiles]

        def p1(ii, c):
            i = (seg - 1 - ii) if reverse else ii
            out = []
            for n, j in enumerate(tiles):
                lr, li = lam8[n]
                cr, ci = c[2 * n], c[2 * n + 1]
                nr = lr * cr - li * ci + sre[j, rows_at(i), :]
                ni = lr * ci + li * cr + sim[j, rows_at(i), :]
                sre[j, rows_at(i), :] = nr
                sim[j, rows_at(i), :] = ni
                out += [nr, ni]
            return tuple(out)

        ends = lax.fori_loop(0, seg, p1, tuple(jnp.zeros((8, 128), F32) for _ in range(2 * len(tiles))))
        cs = []
        for n, j in enumerate(tiles):
            ls = slice(j * 128, (j + 1) * 128)
            steps, wr, wi = _scan_consts(pw_re[j, seg - 1][0:1, :], sgn * pw_im[j, seg - 1][0:1, :], reverse)
            tr, ti = ends[2 * n], ends[2 * n + 1]
            for sft, pr, pi in steps:
                sh = (8 - sft) if reverse else sft
                yr, yi = pltpu.roll(tr, sh, 0), pltpu.roll(ti, sh, 0)
                tr, ti = tr + pr * yr - pi * yi, ti + pr * yi + pi * yr
            cin_r, cin_i = carry[0:1, ls], carry[1:2, ls]
            tr, ti = tr + wr * cin_r - wi * cin_i, ti + wr * cin_i + wi * cin_r
            edge_out, edge_in, sh = (0, 7, 7) if reverse else (7, 0, 1)
            carry[0:1, ls] = tr[edge_out:edge_out + 1, :]
            carry[1:2, ls] = ti[edge_out:edge_out + 1, :]
            cr = jnp.where(sub == edge_in, jnp.broadcast_to(cin_r, (8, 128)), pltpu.roll(tr, sh, 0))
            ci = jnp.where(sub == edge_in, jnp.broadcast_to(cin_i, (8, 128)), pltpu.roll(ti, sh, 0))
            cs += [cr, ci]
            entering[j] = (cr, ci)

        def p2(i, _):
            k = (seg - 1 - i) if reverse else i
            for n, j in enumerate(tiles):
                pr, pi = pw_re[j, k], pw_im[j, k]
                cr, ci = cs[2 * n], cs[2 * n + 1]
                if reverse:
                    sre[j, rows_at(i), :] = sre[j, rows_at(i), :] + pr * cr + pi * ci
                    sim[j, rows_at(i), :] = sim[j, rows_at(i), :] + pr * ci - pi * cr
                else:
                    sre[j, rows_at(i), :] = sre[j, rows_at(i), :] + pr * cr - pi * ci
                    sim[j, rows_at(i), :] = sim[j, rows_at(i), :] + pr * ci + pi * cr
            return 0

        lax.fori_loop(0, seg, p2, 0, unroll=2)
    return entering


class _SegIO:
    def __init__(self, hbm, buf, sems, rows, width, col0=0):
        self.hbm, self.buf, self.sems, self.rows, self.seg, self.width, self.col0 = hbm, buf, sems, rows, rows // 8, width, col0

    def _copies(self, blk, slot, to_vmem):
        out = []
        for r in range(8):
            h = self.hbm.at[pl.ds(blk * self.rows + r * self.seg, self.seg), pl.ds(self.col0, self.width)]
            v = self.buf.at[slot, :, r, :]
            out.append(pltpu.make_async_copy(h, v, self.sems.at[slot, r]) if to_vmem
                       else pltpu.make_async_copy(v, h, self.sems.at[slot, r]))
        return out

    def start(self, blk, slot, to_vmem):
        for cp in self._copies(blk, slot, to_vmem):
            cp.start()

    def wait(self, blk, slot, to_vmem):
        for cp in self._copies(blk, slot, to_vmem):
            cp.wait()

    def value(self, slot):
        return self.buf[slot].reshape(self.rows, self.width)

    def store(self, slot, val):
        self.buf[slot] = val.reshape(self.seg, 8, self.width)


def _seg_pipeline(i, nb, blk_of, ins, outs, compute):
    slot = i % 2

    @pl.when(i == 0)
    def _():
        for io in ins:
            io.start(blk_of(0), 0, True)

    @pl.when(i + 1 < nb)
    def _():
        for io in ins:
            io.start(blk_of(i + 1), 1 - slot, True)

    for io in ins:
        io.wait(blk_of(i), slot, True)

    @pl.when(i >= 2)
    def _():
        for io in outs:
            io.wait(blk_of(i - 2), slot, False)

    compute(slot)
    for io in outs:
        io.start(blk_of(i), slot, False)

    @pl.when(i == nb - 1)
    def _():
        for io in outs:
            if nb >= 2:
                io.wait(blk_of(i - 1), 1 - slot, False)
            io.wait(blk_of(i), slot, False)


def _s5_scan_fwd(proj, bq_re, bq_im, cq_re, cq_im, lbar, dskip, L, TB):
    nb = L // TB
    seg = TB // 8

    def body(u_hbm, bre, bim, cre, cim, lb_ref, d_ref, y_hbm, ck_ref, sre, sim, carry, pw_re, pw_im,
             ubuf, ybuf, sem_u, sem_y):
        i = pl.program_id(0)
        u_io = _SegIO(u_hbm, ubuf, sem_u, TB, S5_WIDTH)
        y_io = _SegIO(y_hbm, ybuf, sem_y, TB, S5_WIDTH)

        @pl.when(i == 0)
        def _():
            carry[...] = jnp.zeros(carry.shape, F32)
            _s5_power_table(lb_ref, pw_re, pw_im, seg)

        ck_ref[0] = carry[...]

        def compute(slot):
            u = u_io.value(slot)
            ub = u.astype(BF16)
            for q in range(S5_Q):
                uq = ub[:, q * S5_QL:(q + 1) * S5_QL]
                vr, vi = _dot(uq, bre[q]), _dot(uq, bim[q])
                for jj in range(S5_QT):
                    sre[q * S5_QT + jj] = vr[:, jj * 128:(jj + 1) * 128]
                    sim[q * S5_QT + jj] = vi[:, jj * 128:(jj + 1) * 128]
            _seg_scan(sre, sim, carry, lb_ref, pw_re, pw_im, TB, False)
            ys = []
            for q in range(S5_Q):
                sl = slice(q * S5_QL, (q + 1) * S5_QL)
                sr = jnp.concatenate([sre[q * S5_QT + jj] for jj in range(S5_QT)], axis=1).astype(BF16)
                si = jnp.concatenate([sim[q * S5_QT + jj] for jj in range(S5_QT)], axis=1).astype(BF16)
                ys.append(_dot(sr, cre[q]) - _dot(si, cim[q]) + u[:, sl] * d_ref[:, sl])
            y_io.store(slot, jnp.concatenate(ys, axis=1))

        _seg_pipeline(i, nb, lambda st: st, [u_io], [y_io], compute)

    full = lambda a: pl.BlockSpec(a.shape, lambda i, nd=a.ndim: (0,) * nd)
    st = pltpu.VMEM((S5_NT, TB, 128), F32)
    pw = pltpu.VMEM((S5_NT, seg, 8, 128), F32)
    io = pltpu.VMEM((2, seg, 8, S5_WIDTH), F32)
    return _pcall(
        body, name="s5_scan_fwd", grid=(nb,),
        in_specs=[_ANY, full(bq_re), full(bq_im), full(cq_re), full(cq_im), full(lbar), full(dskip)],
        out_specs=[_ANY, pl.BlockSpec((1, 8, S5_LANES), lambda i: (i, 0, 0))],
        out_shape=[jax.ShapeDtypeStruct((L, S5_WIDTH), F32), jax.ShapeDtypeStruct((nb, 8, S5_LANES), F32)],
        scratch_shapes=[st, st, pltpu.VMEM((8, S5_LANES), F32), pw, pw, io, io,
                        pltpu.SemaphoreType.DMA((2, 8)), pltpu.SemaphoreType.DMA((2, 8))],
        compiler_params=_cparams(1))(proj, bq_re, bq_im, cq_re, cq_im, lbar, dskip)


def _s5_scan_bwd(proj, dy, ck, bq_re, bq_im, cq_re, cq_im, lbar, dskip, L, TB):
    nb = L // TB
    seg = TB // 8

    def body(u_hbm, dy_hbm, ck_ref, bre, bim, cre, cim, lb_ref, d_ref,
             du_hbm, dbre, dbim, dcre, dcim, dlb_ref, dd_ref, sre, sim, gre, gim, carry, gcarry, pw_re, pw_im,
             ubuf, dybuf, dubuf, sem_u, sem_dy, sem_du):
        i = pl.program_id(0)
        u_io = _SegIO(u_hbm, ubuf, sem_u, TB, S5_WIDTH)
        dy_io = _SegIO(dy_hbm, dybuf, sem_dy, TB, S5_WIDTH)
        du_io = _SegIO(du_hbm, dubuf, sem_du, TB, S5_WIDTH)

        @pl.when(i == 0)
        def _():
            gcarry[...] = jnp.zeros(gcarry.shape, F32)
            dbre[...] = jnp.zeros(dbre.shape, F32)
            dbim[...] = jnp.zeros(dbim.shape, F32)
            dcre[...] = jnp.zeros(dcre.shape, F32)
            dcim[...] = jnp.zeros(dcim.shape, F32)
            dlb_ref[...] = jnp.zeros(dlb_ref.shape, F32)
            dd_ref[...] = jnp.zeros(dd_ref.shape, F32)
            _s5_power_table(lb_ref, pw_re, pw_im, seg)

        def compute(slot):
            u = u_io.value(slot)
            dy_v = dy_io.value(slot)
            ub = u.astype(BF16)
            dyb = dy_v.astype(BF16)
            carry[...] = ck_ref[0]
            for q in range(S5_Q):
                uq = ub[:, q * S5_QL:(q + 1) * S5_QL]
                dq = dyb[:, q * S5_QL:(q + 1) * S5_QL]
                vr, vi = _dot(uq, bre[q]), _dot(uq, bim[q])
                hr, hi = _dot_nt(dq, cre[q]), -_dot_nt(dq, cim[q])
                for jj in range(S5_QT):
                    ls = slice(jj * 128, (jj + 1) * 128)
                    sre[q * S5_QT + jj] = vr[:, ls]
                    sim[q * S5_QT + jj] = vi[:, ls]
                    gre[q * S5_QT + jj] = hr[:, ls]
                    gim[q * S5_QT + jj] = hi[:, ls]
            entering = _seg_scan(sre, sim, carry, lb_ref, pw_re, pw_im, TB, False)
            _seg_scan(gre, gim, gcarry, lb_ref, pw_re, pw_im, TB, True)

            rows_at = lambda k: pl.ds(pl.multiple_of(k * 8, 8), 8)
            for j in range(S5_NT):
                er, ei = entering[j]
                gr0, gi0 = gre[j, rows_at(0), :], gim[j, rows_at(0), :]
                acc0 = (gr0 * er + gi0 * ei, gi0 * er - gr0 * ei)

                def acc_step(k, acc, j=j):
                    gr, gi_ = gre[j, rows_at(k), :], gim[j, rows_at(k), :]
                    spr, spi = sre[j, rows_at(k - 1), :], sim[j, rows_at(k - 1), :]
                    return acc[0] + gr * spr + gi_ * spi, acc[1] - gr * spi + gi_ * spr

                ar, ai = lax.fori_loop(1, seg, acc_step, acc0, unroll=2 if (seg - 1) % 2 == 0 else 1)
                ls = slice(j * 128, (j + 1) * 128)
                dlb_ref[0:1, ls] += jnp.sum(ar, axis=0, keepdims=True)
                dlb_ref[1:2, ls] += jnp.sum(ai, axis=0, keepdims=True)

            dd_ref[...] += jnp.sum(dy_v * u, axis=0, keepdims=True)
            dus = []
            for q in range(S5_Q):
                sl = slice(q * S5_QL, (q + 1) * S5_QL)
                cat = lambda ref: jnp.concatenate([ref[q * S5_QT + jj] for jj in range(S5_QT)], axis=1).astype(BF16)
                grq, giq = cat(gre), cat(gim)
                dus.append(_dot_nt(grq, bre[q]) + _dot_nt(giq, bim[q]) + dy_v[:, sl] * d_ref[:, sl])
                dbre[q] += _dot_tn(ub[:, sl], grq)
                dbim[q] += _dot_tn(ub[:, sl], giq)
                dcre[q] += _dot_tn(cat(sre), dyb[:, sl])
                dcim[q] -= _dot_tn(cat(sim), dyb[:, sl])
            du_io.store(slot, jnp.concatenate(dus, axis=1))

        _seg_pipeline(i, nb, lambda st: nb - 1 - st, [u_io, dy_io], [du_io], compute)

    full = lambda a: pl.BlockSpec(a.shape, lambda i, nd=a.ndim: (0,) * nd)
    sh = jax.ShapeDtypeStruct
    outs = [sh((L, S5_WIDTH), F32), sh(bq_re.shape, F32), sh(bq_im.shape, F32), sh(cq_re.shape, F32), sh(cq_im.shape, F32),
            sh((8, S5_LANES), F32), sh((1, S5_WIDTH), F32)]
    fo = lambda s: pl.BlockSpec(s.shape, lambda i, nd=len(s.shape): (0,) * nd)
    st = pltpu.VMEM((S5_NT, TB, 128), F32)
    pw = pltpu.VMEM((S5_NT, seg, 8, 128), F32)
    io = pltpu.VMEM((2, seg, 8, S5_WIDTH), F32)
    sem = pltpu.SemaphoreType.DMA((2, 8))
    return _pcall(
        body, name="s5_scan_bwd", grid=(nb,),
        in_specs=[_ANY, _ANY, pl.BlockSpec((1, 8, S5_LANES), lambda i: (nb - 1 - i, 0, 0)),
                  full(bq_re), full(bq_im), full(cq_re), full(cq_im), full(lbar), full(dskip)],
        out_specs=[_ANY] + [fo(s) for s in outs[1:]],
        out_shape=outs,
        scratch_shapes=[st] * 4 + [pltpu.VMEM((8, S5_LANES), F32)] * 2 + [pw, pw, io, io, io, sem, sem, sem],
        compiler_params=_cparams(1))(proj, dy, ck, bq_re, bq_im, cq_re, cq_im, lbar, dskip)


N_HEAD = RW_WIDTH // HEAD
_NN = (((2,), (1,)), ((0,), (0,)))
_NT = (((2,), (2,)), ((0,), (0,)))
_TN = (((1,), (1,)), ((0,), (0,)))


def _hi_lo(x):
    h = x.astype(BF16)
    return h, (x - h.astype(F32)).astype(BF16)


def _mm_acc(a, b, dims, passes=3):
    dg = lambda p, q: lax.dot_general(p, q, dims, preferred_element_type=F32)
    if passes == 1:
        return dg(a.astype(BF16), b.astype(BF16))
    ah, al = _hi_lo(a)
    bh, bl = _hi_lo(b)
    return dg(ah, bh) + dg(ah, bl) + dg(al, bh)


def _cumsum_rows(x, transpose):
    h, n, _ = x.shape
    ti = lax.broadcasted_iota(jnp.int32, (h, n, n), 1)
    tj = lax.broadcasted_iota(jnp.int32, (h, n, n), 2)
    m = ((tj >= ti) if transpose else (tj <= ti)).astype(BF16)
    acc, rem = None, x
    for s in range(3):
        part = rem.astype(BF16)
        t = lax.dot_general(m, part, _NN, preferred_element_type=F32)
        acc = t if acc is None else acc + t
        if s < 2:
            rem = rem - part.astype(F32)
    return acc


def _slices(x, axis, sizes):
    out, off = [], 0
    for n in sizes:
        out.append(lax.slice_in_dim(x, off, off + n, axis=axis))
        off += n
    return tuple(out)


def _cat_op(axis, sizes, diff):
    plain = lambda *xs: jnp.concatenate(xs, axis=axis)
    if not diff:
        return plain
    f = jax.custom_vjp(plain)
    f.defvjp(lambda *xs: (plain(*xs), None), lambda _, d: _slices(d, axis, sizes))
    return f


def _split_op(axis, sizes, diff):
    plain = lambda x: _slices(x, axis, sizes)
    if not diff:
        return plain
    f = jax.custom_vjp(plain)
    f.defvjp(lambda x: (plain(x), None), lambda _, d: (jnp.concatenate(d, axis=axis),))
    return f


def _mm_ops(diff, passes):
    mm = lambda a, b, dims: _mm_acc(a, b, dims, passes)
    if not diff:
        return (lambda a, b: mm(a, b, _NN), lambda a, b: mm(a, b, _NT), lambda a, b: mm(a, b, _TN))

    @jax.custom_vjp
    def nn(a, b):
        return mm(a, b, _NN)

    nn.defvjp(lambda a, b: (mm(a, b, _NN), (a, b)), lambda r, d: (mm(d, r[1], _NT), mm(r[0], d, _TN)))

    @jax.custom_vjp
    def nt(a, b):
        return mm(a, b, _NT)

    nt.defvjp(lambda a, b: (mm(a, b, _NT), (a, b)), lambda r, d: (mm(d, r[1], _NN), mm(d, r[0], _TN)))

    @jax.custom_vjp
    def tn(a, b):
        return mm(a, b, _TN)

    tn.defvjp(lambda a, b: (mm(a, b, _TN), (a, b)), lambda r, d: (mm(r[1], d, _NT), mm(r[0], d, _NN)))
    return nn, nt, tn


def _cums_op(diff):
    if not diff:
        return lambda x: _cumsum_rows(x, False)

    @jax.custom_vjp
    def cums(x):
        return _cumsum_rows(x, False)

    cums.defvjp(lambda x: (_cumsum_rows(x, False), None), lambda _, d: (_cumsum_rows(d, True),))
    return cums


WKV_PASSES = (1, 1, 1, 1, 1)


WKV_SUB = 4
WKV_BLOCK = CHUNK * WKV_SUB


def _wkv_block(s0, r, w, k, v, a, b, diff):
    p_pair, p_val, p_solve, p_out, p_state = WKV_PASSES
    cums = _cums_op(diff)
    _, nt_pair, _ = _mm_ops(diff, p_pair)
    nn_val, _, _ = _mm_ops(diff, p_val)
    nn_solve, _, _ = _mm_ops(diff, p_solve)
    nn_out, nt_out, _ = _mm_ops(diff, p_out)
    nn_state, _, tn_state = _mm_ops(diff, p_state)
    h, d, n, sub = s0.shape[0], s0.shape[2], CHUNK, WKV_SUB
    hb = h * sub
    to_chunks = lambda t: _cat_op(0, (h,) * sub, diff)(*_split_op(1, (n,) * sub, diff)(t))
    r, w, k, v, a, b = (to_chunks(t) for t in (r, w, k, v, a, b))
    cat_rows2 = _cat_op(1, (n, n), diff)
    cat_lanes2 = _cat_op(2, (n, n), diff)
    split_rows2 = _split_op(1, (n, n), diff)
    split_lanes2 = _split_op(2, (n, n), diff)
    ti = lax.broadcasted_iota(jnp.int32, (hb, n, n), 1)
    tj = lax.broadcasted_iota(jnp.int32, (hb, n, n), 2)
    incl, strict = tj <= ti, tj < ti
    logw = jnp.log(w)
    cum = cums(logw)
    g_in, g_ex, g_inv = jnp.exp(cum), jnp.exp(cum - logw), jnp.exp(-cum)
    ae, re, bi, ki = a * g_ex, r * g_in, b * g_inv, k * g_inv
    top, bot = split_rows2(nt_pair(cat_rows2(ae, re), cat_rows2(bi, ki)))
    tab, tak = split_lanes2(top)
    qb, qk = split_lanes2(bot)
    tab, tak = jnp.where(strict, tab, 0.0), jnp.where(strict, tak, 0.0)
    qb, qk = jnp.where(incl, qb, 0.0), jnp.where(incl, qk, 0.0)
    tak_v, qk_v = split_rows2(nn_val(cat_rows2(tak, qk), v))
    x = cat_lanes2(ae, tak_v)
    npow = tab
    steps = max(1, (n - 1).bit_length())
    for i in range(steps):
        x = x + nn_solve(npow, x)
        if i + 1 < steps:
            npow = nn_solve(npow, npow)
    ae_m, uc = split_lanes2(x)
    qx = nn_out(qb, x)
    q_ae, q_uc = split_lanes2(qx)
    re_m = re + q_ae
    yc = q_uc + qk_v
    g_end = jnp.exp(jnp.sum(logw, axis=1, keepdims=True))
    bg, kg = bi * g_end, ki * g_end
    tm = tn_state(ae_m, bg)
    sc = tn_state(cat_rows2(uc, v), cat_rows2(bg, kg))
    per_chunk = _split_op(0, (h,) * sub, diff)
    re_m, yc, g_end, tm, sc = (per_chunk(t) for t in (re_m, yc, g_end, tm, sc))
    ys, s = [], s0
    for i in range(sub):
        ys.append(nt_out(re_m[i], s) + yc[i])
        s = s * g_end[i] + nn_state(s, tm[i]) + sc[i]
    return _cat_op(1, (n,) * sub, diff)(*ys), s


def _wkv_fwd(r, w, k, v, a, b, L):
    nc = L // WKV_BLOCK

    def body(r_ref, w_ref, k_ref, v_ref, a_ref, b_ref, y_ref, ck_ref, s_ref):
        c = pl.program_id(0)

        @pl.when(c == 0)
        def _():
            s_ref[...] = jnp.zeros(s_ref.shape, F32)

        s0 = s_ref[...]
        ck_ref[0] = s0
        y, s1 = _wkv_block(s0, r_ref[...], w_ref[...], k_ref[...], v_ref[...], a_ref[...], b_ref[...], False)
        y_ref[...] = y
        s_ref[...] = s1

    blk = pl.BlockSpec((N_HEAD, WKV_BLOCK, HEAD), lambda c: (0, c, 0))
    return _pcall(
        body, name="wkv_fwd", grid=(nc,), in_specs=[blk] * 6,
        out_specs=[blk, pl.BlockSpec((1, N_HEAD, HEAD, HEAD), lambda c: (c, 0, 0, 0))],
        out_shape=[jax.ShapeDtypeStruct((N_HEAD, L, HEAD), F32), jax.ShapeDtypeStruct((nc, N_HEAD, HEAD, HEAD), F32)],
        scratch_shapes=[pltpu.VMEM((N_HEAD, HEAD, HEAD), F32)],
        compiler_params=_cparams(1))(r, w, k, v, a, b)


def _wkv_bwd(r, w, k, v, a, b, dy, ck, L, deps=()):
    nc = L // WKV_BLOCK

    def body(r_ref, w_ref, k_ref, v_ref, a_ref, b_ref, dy_ref, ck_ref, *rest):
        dr_ref, dw_ref, dk_ref, dv_ref, da_ref, db_ref, ds_ref = rest[len(deps):]
        c = pl.program_id(0)

        @pl.when(c == 0)
        def _():
            ds_ref[...] = jnp.zeros(ds_ref.shape, F32)

        _, vjp = jax.vjp(lambda *t: _wkv_block(*t, True), ck_ref[0], r_ref[...], w_ref[...], k_ref[...], v_ref[...],
                         a_ref[...], b_ref[...])
        g = vjp((dy_ref[...], ds_ref[...]))
        ds_ref[...] = g[0]
        for o_ref, val in zip((dr_ref, dw_ref, dk_ref, dv_ref, da_ref, db_ref), g[1:]):
            o_ref[...] = val

    blk = pl.BlockSpec((N_HEAD, WKV_BLOCK, HEAD), lambda c: (0, nc - 1 - c, 0))
    sh = jax.ShapeDtypeStruct((N_HEAD, L, HEAD), F32)
    return _pcall(
        body, name="wkv_bwd", grid=(nc,),
        in_specs=[blk] * 7 + [pl.BlockSpec((1, N_HEAD, HEAD, HEAD), lambda c: (nc - 1 - c, 0, 0, 0))]
        + [pl.BlockSpec(d.shape, lambda c, nd=d.ndim: (0,) * nd) for d in deps],
        out_specs=[blk] * 6, out_shape=[sh] * 6,
        scratch_shapes=[pltpu.VMEM((N_HEAD, HEAD, HEAD), F32)],
        compiler_params=_cparams(1))(r, w, k, v, a, b, dy, ck, *deps)


TB = 256


def _bf(x):
    return x.astype(BF16)


def _inproj_fwd(x, norm_mix, w_in, L, deps=()):
    def fn(i, tv, cv):
        xn = _rms(tv[0], cv[0])
        return _dot(_bf(xn), cv[1]), xn

    return _tok_call("inproj_fwd", fn, L, TB, [(x, D_MODEL, 0)], [norm_mix, w_in], [(IN_COLS, F32), (D_MODEL, BF16)],
                     deps=deps)


def _s5_post_fn(glu_w, wtop, diff=True):
    mg = _mmc(glu_w, diff)
    mt = _mmc(wtop, diff) if wtop is not None else None

    def f(y, glu_b, e):
        z = _gelu(y)
        out = z * _sigmoid(mg(z) + glu_b + e)
        res = mt(out) if mt is not None else out
        return res, (z, out)

    return f


def _s5_post_fwd(y, glu_w, glu_b, L):
    def fn(i, tv, cv):
        out, _ = _s5_post_fn(cv[0], None, False)(tv[0], cv[1], 0.0)
        return (out,)

    return _tok_call("s5_post_fwd", fn, L, TB, [(y, S5_WIDTH, 0)], [glu_w, glu_b], [(S5_WIDTH, F32)])[0]


def _s5_post_bwd(y, dh1, glu_w, glu_b, wtop, L, deps=()):
    def fn(i, tv, cv):
        e0 = jnp.zeros((TB, S5_WIDTH), F32)
        _, vjp, (z, out) = jax.vjp(_s5_post_fn(cv[0], cv[2]), tv[0], cv[1], e0, has_aux=True)
        dy, db, de = vjp(tv[1])
        return dy, z, de, out, db

    return _tok_call("s5_post_bwd", fn, L, TB, [(y, S5_WIDTH, 0), (dh1, D_MODEL, 0)], [glu_w, glu_b, wtop],
                     [(S5_WIDTH, F32), (S5_WIDTH, BF16), (S5_WIDTH, BF16), (S5_WIDTH, BF16)], [(1, S5_WIDTH)], deps=deps)


RW_COLBLK = ((RW_WIDTH, 1), (RW_WIDTH, 2), (RW_WIDTH, 3), (128, 16), (128, 17))
RW_MU = ((0, 512), (512, 1024), (1024, 1536), (1536, 1664), (1664, 1792))


def _rw_pre_fn(w2pad, a2pad, g2, diff=True):
    m_w, m_a, m_g = _mmc(w2pad, diff), _mmc(a2pad, diff), _mmc(g2, diff)
    seg = _segsum(_head_indicator(RW_WIDTH), diff)

    def f(zr, zk, zv, zwa, zg, w0, a0, k_k, k_a, e_w, e_a):
        wl_t = jnp.tanh(zwa)
        wlin = w0 + m_w(wl_t) + e_w
        w = -_softplus(-wlin) - 0.5
        decay = jnp.exp(-jnp.exp(w))
        a = _sigmoid(a0 + m_a(zwa) + e_a)
        sg = _sigmoid(zg)
        g = m_g(sg)
        kk = zk * k_k
        kkn = kk / jnp.maximum(jnp.sqrt(seg(kk * kk)), L2_EPS)
        kf = zk * (1.0 + (a - 1.0) * k_a)
        return (zr, decay, kf, zv, -kkn, kkn * a, g), (wl_t, sg)

    return f


def _rw_shifted(i, tv, mu):
    sub = lax.broadcasted_iota(jnp.int32, (TB, 1), 0)
    zs, dif = [], []
    for n in range(5):
        z = tv[n]
        last = jnp.where(i == 0, 0.0, tv[5 + n][7:8, :])
        prev = jnp.where(sub == 0, last, pltpu.roll(z, 1, 0))
        m = mu[:, RW_MU[n][0]:RW_MU[n][1]]
        zs.append(z + (prev - z) * m)
        dif.append(prev - z)
    return zs, dif


def _rw_tok_in(proj):
    return [(proj, wd, cb) for wd, cb in RW_COLBLK] + [(proj, wd, cb, "prev") for wd, cb in RW_COLBLK]


def _rw_pre_fwd(proj, mu, w0, a0, k_k, k_a, w2pad, a2pad, g2, L):
    def fn(i, tv, cv):
        zs, _ = _rw_shifted(i, tv, cv[0])
        outs, _ = _rw_pre_fn(cv[5], cv[6], cv[7], False)(*zs, cv[1], cv[2], cv[3], cv[4], 0.0, 0.0)
        return outs

    return _tok_call("rw_pre_fwd", fn, L, TB, _rw_tok_in(proj), [mu, w0, a0, k_k, k_a, w2pad, a2pad, g2],
                     [("heads", F32)] * 6 + [(RW_WIDTH, F32)])


def _rw_pre_bwd(proj, cots, mu, w0, a0, k_k, k_a, w2pad, a2pad, g2, L):
    def fn(i, tv, cv):
        zs, dif = _rw_shifted(i, tv[:10], cv[0])
        dr1, dr2, dw, dk1, dk2, dv1, dv2, da, db, dg = tv[10:]
        e0 = jnp.zeros((TB, RW_WIDTH), F32)
        _, vjp, (wl_t, sg) = jax.vjp(_rw_pre_fn(cv[5], cv[6], cv[7]), *zs, cv[1], cv[2], cv[3], cv[4], e0, e0, has_aux=True)
        g = vjp((dr1 + dr2, dw, dk1 + dk2, dv1 + dv2, da, db, dg))
        dzs = jnp.concatenate(g[:5], axis=1)
        dmu = jnp.concatenate([jnp.sum(g[n] * dif[n], axis=0, keepdims=True) for n in range(5)], axis=1)
        return dzs, wl_t, zs[3], sg, g[9], g[10], dmu, g[5], g[6], g[7], g[8]

    tok_in = _rw_tok_in(proj) + [((c,) if c.ndim == 3 else (c, RW_WIDTH, 0)) for c in cots]
    return _tok_call("rw_pre_bwd", fn, L, TB, tok_in, [mu, w0, a0, k_k, k_a, w2pad, a2pad, g2],
                     [(SHIFT_COLS, F32), (128, BF16), (128, BF16), (128, BF16), (RW_WIDTH, BF16), (RW_WIDTH, BF16)],
                     [(1, SHIFT_COLS)] + [(1, RW_WIDTH)] * 4)


def _rw_post_fn(wbot, diff=True):
    seg = _segsum(_head_indicator(RW_WIDTH), diff)
    mb = _mmc(wbot, diff) if wbot is not None else None

    def f(y, r, kf, v, g, ln_w, ln_b, r_k):
        mean = seg(y) * (1.0 / HEAD)
        yc = y - mean
        var = seg(yc * yc) * (1.0 / HEAD)
        yn = yc * lax.rsqrt(var + GN_EPS) * ln_w + ln_b
        bonus = seg(r * kf * r_k) * v
        out = (yn + bonus) * g
        res = mb(out) if mb is not None else out
        return res, out

    return f


def _rw_post_fwd(y, r, kf, v, g, ln_w, ln_b, r_k, L):
    def fn(i, tv, cv):
        out, _ = _rw_post_fn(None, False)(*tv, *cv)
        return (out,)

    return _tok_call("rw_post_fwd", fn, L, TB, [(t,) for t in (y, r, kf, v)] + [(g, RW_WIDTH, 0)], [ln_w, ln_b, r_k],
                     [(RW_WIDTH, F32)])[0]


def _rw_post_bwd(y, r, kf, v, g, dh1, ln_w, ln_b, r_k, wbot, L):
    def fn(i, tv, cv):
        _, vjp, out = jax.vjp(_rw_post_fn(cv[3]), *tv[:5], cv[0], cv[1], cv[2], has_aux=True)
        gr = vjp(tv[5])
        return gr[0], gr[1], gr[2], gr[3], gr[4], out, gr[5], gr[6], gr[7]

    return _tok_call("rw_post_bwd", fn, L, TB, [(t,) for t in (y, r, kf, v)] + [(g, RW_WIDTH, 0), (dh1, D_MODEL, 0)],
                     [ln_w, ln_b, r_k, wbot], [("heads", F32)] + [(RW_WIDTH, F32)] * 4 + [(RW_WIDTH, BF16)], [(1, RW_WIDTH)] * 3)


def _ffn_fn(w1, w3, w2, diff=True):
    m1, m3, m2 = _mmc(w1, diff), _mmc(w3, diff), _mmc(w2, diff)

    def f(h1, norm_ffn, e1, e3):
        hn = _rms(h1, norm_ffn)
        a1 = m1(hn) + e1
        a3 = m3(hn) + e3
        hm = a1 * _sigmoid(a1) * a3
        return h1 + m2(hm), (hn, hm)

    return f


TB_FFN = 256


def _mixffn_fwd(x, s5_out, rw_out, wtop, wbot, norm_ffn, w1, w3, w2, L):
    def fn(i, tv, cv):
        h1 = tv[0] + _dot(_bf(tv[1]), cv[0]) + _dot(_bf(tv[2]), cv[1])
        h2, _ = _ffn_fn(cv[3], cv[4], cv[5], False)(h1, cv[2], 0.0, 0.0)
        return h1, h2

    return _tok_call("mixffn_fwd", fn, L, TB_FFN, [(x, D_MODEL, 0), (s5_out, S5_WIDTH, 0), (rw_out, RW_WIDTH, 0)],
                     [wtop, wbot, norm_ffn, w1, w3, w2], [(D_MODEL, F32), (D_MODEL, F32)])


def _ffn_bwd(h1, dh2, norm_ffn, w1, w3, w2, L):
    def fn(i, tv, cv):
        e0 = jnp.zeros((TB_FFN, FFN_HIDDEN), F32)
        _, vjp, (hn, hm) = jax.vjp(_ffn_fn(cv[1], cv[2], cv[3]), tv[0], cv[0], e0, e0, has_aux=True)
        dh1, dn, d1, d3 = vjp(tv[1])
        return dh1, d1, d3, hm, hn, dn

    return _tok_call("ffn_bwd", fn, L, TB_FFN, [(h1, D_MODEL, 0), (dh2, D_MODEL, 0)], [norm_ffn, w1, w3, w2],
                     [(D_MODEL, F32), (FFN_HIDDEN, BF16), (FFN_HIDDEN, BF16), (FFN_HIDDEN, BF16), (D_MODEL, BF16)],
                     [(1, D_MODEL)])


def _ple_loss_fb(h2, p, target, norm_ple, final_norm, wg, wu, L):
    def fn(i, tv, cv):
        mgate, mup = _mmc(cv[2]), _mmc(cv[3], False)

        def f(h2_, norm_ple_, final_norm_, eg, eu):
            hn = _rms(h2_, norm_ple_)
            gate = _sigmoid(mgate(hn) + eg)
            h3 = h2_ + gate * (mup(tv[1]) + eu)
            out = _rms(h3, final_norm_)
            d = out - tv[2]
            return 0.5 * jnp.sum(jnp.mean(d * d, axis=-1, keepdims=True)), hn

        e0 = jnp.zeros((TB, D_MODEL), F32)
        loss, vjp, hn = jax.vjp(f, tv[0], cv[0], cv[1], e0, e0, has_aux=True)
        dh2, dnp, dfn, deg, deu = vjp(jnp.ones((), F32))
        return dh2, dh2, deg, deu, hn, jnp.full((8, 128), loss, F32), dnp, dfn

    return _tok_call("ple_loss_fb", fn, L, TB, [(h2, D_MODEL, 0), (p, PLE_DIM, 0), (target, D_MODEL, 0)],
                     [norm_ple, final_norm, wg, wu],
                     [(D_MODEL, F32), (D_MODEL, BF16), (D_MODEL, BF16), (D_MODEL, BF16), (D_MODEL, BF16)],
                     [(8, 128), (1, D_MODEL), (1, D_MODEL)])


def _inproj_bwd(x, dh1, du, dzs, norm_mix, mu, w_u, w_z, L):
    nb = L // TB

    def fn(i, tv, cv):
        sub = lax.broadcasted_iota(jnp.int32, (TB, 1), 0)
        m = cv[1]
        b = tv[3] * m
        nxt = jnp.where(i == nb - 1, 0.0, tv[4][0:1, :] * m)
        dz = tv[3] * (1.0 - m) + jnp.where(sub == TB - 1, nxt, pltpu.roll(b, TB - 1, 0))
        dub, dzb = _bf(tv[2]), _bf(dz)
        dxn = _dot_nt(dub, cv[2]) + _dot_nt(dzb, cv[3])
        _, vjp = jax.vjp(_rms, tv[0], cv[0])
        dx, dn = vjp(dxn)
        return tv[1] + dx, jnp.concatenate([dub, dzb], axis=1), dn

    return _tok_call("inproj_bwd", fn, L, TB,
                     [(x, D_MODEL, 0), (dh1, D_MODEL, 0), (du, S5_WIDTH, 0), (dzs, SHIFT_COLS, 0), (dzs, SHIFT_COLS, 0, "next")],
                     [norm_mix, mu, w_u, w_z], [(D_MODEL, F32), (IN_COLS, BF16)], [(1, D_MODEL)])


def _eye8(dt):
    return jnp.eye(8, dtype=dt)


def _quarter_b(bb):
    return jnp.einsum("hg,qgcp->qhcgp", _eye8(bb.dtype), bb.reshape(S5_Q, 8, S5_GROUP, S5_STATE)).reshape(S5_Q, S5_QL, S5_QS)


def _unquarter_b(d):
    return jnp.einsum("qhcgp,hg->qgcp", d.reshape(S5_Q, 8, S5_GROUP, 8, S5_STATE), _eye8(d.dtype)).reshape(
        S5_GROUPS, S5_GROUP, S5_STATE)


def _quarter_c(c):
    return jnp.einsum("gh,qgcp->qgphc", _eye8(c.dtype), c.reshape(S5_Q, 8, S5_GROUP, S5_STATE)).reshape(S5_Q, S5_QS, S5_QL)


def _unquarter_c(d):
    return jnp.einsum("qgphc,gh->qgcp", d.reshape(S5_Q, 8, S5_STATE, 8, S5_GROUP), _eye8(d.dtype)).reshape(
        S5_GROUPS, S5_GROUP, S5_STATE)


def _local_step(x, p, target, W, late_weights=None, grads_ready=None, first_dep=None):
    L = x.shape[0]
    r2 = lambda v: v.reshape(1, -1)
    w_in = W["w_in"]
    w2pad = jnp.pad(W["rw_w2"], ((0, 64), (0, 0)))
    a2pad = jnp.pad(W["rw_a2"], ((64, 0), (0, 0)))
    mu = r2(W["rw_shift_mu"])
    rw_vec = [r2(W[n]) for n in ("rw_w0", "rw_a0", "rw_k_k", "rw_k_a")]
    ln_w, ln_b, r_k = r2(W["rw_ln_w"]), r2(W["rw_ln_b"]), r2(W["rw_r_k"])

    lam_re, lam_im = W["s5_lam_re"], W["s5_lam_im"]
    log_step = W["s5_log_step"].reshape(S5_GROUPS, 1)
    bt_re, bt_im = W["s5_b_re"].transpose(0, 2, 1), W["s5_b_im"].transpose(0, 2, 1)
    lb_re, lb_im, bb_re, bb_im = _s5_param_fwd(lam_re, lam_im, log_step, bt_re, bt_im)
    bq_re, bq_im = _quarter_b(bb_re).astype(BF16), _quarter_b(bb_im).astype(BF16)
    cq_re, cq_im = _quarter_c(W["s5_c_re"]).astype(BF16), _quarter_c(W["s5_c_im"]).astype(BF16)
    lbar = jnp.concatenate([lb_re.reshape(1, -1), lb_im.reshape(1, -1), jnp.zeros((6, S5_LANES), F32)], axis=0)
    dskip = r2(W["s5_d"])
    glu_b = r2(W["s5_glu_b"])
    norm_mix, norm_ffn, norm_ple, final_norm = (r2(W[n]) for n in ("norm_mix", "norm_ffn", "norm_ple", "final_norm"))

    proj, xn = _inproj_fwd(x, norm_mix, w_in, L, () if first_dep is None else (first_dep,))
    y_s5, ck5 = _s5_scan_fwd(proj, bq_re, bq_im, cq_re, cq_im, lbar, dskip, L, TB)
    s5_out = _s5_post_fwd(y_s5, W["s5_glu_w"], glu_b, L)
    r, wd, kf, v, a_s, b_s, g = _rw_pre_fwd(proj, mu, *rw_vec, w2pad, a2pad, W["rw_g2"], L)
    scan_in = (r, wd, kf, v, a_s, b_s)
    y_wkv, ckw = _wkv_fwd(*scan_in, L)
    rw_out = _rw_post_fwd(y_wkv, r, kf, v, g, ln_w, ln_b, r_k, L)
    if late_weights is not None:
        W = dict(W, **late_weights(rw_out))
    wtop, wbot = W["w_out"][:S5_WIDTH], W["w_out"][S5_WIDTH:]
    h1, h2 = _mixffn_fwd(x, s5_out, rw_out, wtop, wbot, norm_ffn, W["ffn_w1"], W["ffn_w3"], W["ffn_w2"], L)

    G = {}
    dh2, dh2_bf, deg, deu, hn_ple, loss_acc, G["norm_ple"], G["final_norm"] = _ple_loss_fb(
        h2, p, target, norm_ple, final_norm, W["ple_gate_w"], W["ple_up_w"], L)
    dh1, da1, da3, hm, hn_ffn, G["norm_ffn"] = _ffn_bwd(h1, dh2, norm_ffn, W["ffn_w1"], W["ffn_w3"], W["ffn_w2"], L)
    G["ffn_w1"] = _mm_tn("dw_ffn_w1", hn_ffn, da1)
    G["ffn_w3"] = _mm_tn("dw_ffn_w3", hn_ffn, da3)
    G["ffn_w2"] = _mm_tn("dw_ffn_w2", hm, dh2_bf)
    G["ple_gate_w"] = _mm_tn("dw_ple_gate", hn_ple, deg)
    dep_a = grads_ready(0, G) if grads_ready is not None else None
    dy_s5, z_bf, dgp, s5o_bf, G["s5_glu_b"] = _s5_post_bwd(y_s5, dh1, W["s5_glu_w"], glu_b, wtop, L,
                                                           () if dep_a is None else (dep_a,))
    dy_wkv, dr2, dk2, dv2, dg, rwo_bf, G["rw_ln_w"], G["rw_ln_b"], G["rw_r_k"] = _rw_post_bwd(
        y_wkv, r, kf, v, g, dh1, ln_w, ln_b, r_k, wbot, L)
    G["w_out"] = jnp.concatenate([_mm_tn("dw_out_top", s5o_bf, dh1), _mm_tn("dw_out_bot", rwo_bf, dh1)], axis=0)
    dep = grads_ready(1, G) if grads_ready is not None else None
    G["ple_up_w"] = _mm_tn("dw_ple_up", p, deu)
    G["s5_glu_w"] = _mm_tn("dw_s5_glu", z_bf, dgp)
    dr1, dwd, dk1, dv1, da_s, db_s = _wkv_bwd(*scan_in, dy_wkv, ckw, L, () if dep is None else (dep,))
    (dzs, wl_t, zwa, sg, dwlin, dalin, G["rw_shift_mu"], G["rw_w0"], G["rw_a0"], G["rw_k_k"], G["rw_k_a"]) = _rw_pre_bwd(
        proj, (dr1, dr2, dwd, dk1, dk2, dv1, dv2, da_s, db_s, dg), mu, *rw_vec, w2pad, a2pad, W["rw_g2"], L)
    G["rw_w2"] = _mm_tn("dw_rw_w2", wl_t, dwlin)[:64]
    G["rw_a2"] = _mm_tn("dw_rw_a2", zwa, dalin)[64:]
    G["rw_g2"] = _mm_tn("dw_rw_g2", sg, dg)
    du, dbq_re, dbq_im, dcq_re, dcq_im, dlbar, G["s5_d"] = _s5_scan_bwd(
        proj, dy_s5, ck5, bq_re, bq_im, cq_re, cq_im, lbar, dskip, L, TB)
    G["s5_c_re"], G["s5_c_im"] = _unquarter_c(dcq_re), _unquarter_c(dcq_im)
    d_lam_re, d_lam_im, d_ls, d_bt_re, d_bt_im = _s5_param_bwd(
        lam_re, lam_im, log_step, bt_re, bt_im, dlbar[0].reshape(S5_GROUPS, S5_STATE), dlbar[1].reshape(S5_GROUPS, S5_STATE),
        _unquarter_b(dbq_re), _unquarter_b(dbq_im))
    G["s5_lam_re"], G["s5_lam_im"], G["s5_log_step"] = d_lam_re, d_lam_im, d_ls.reshape(S5_GROUPS)
    G["s5_b_re"], G["s5_b_im"] = d_bt_re.transpose(0, 2, 1), d_bt_im.transpose(0, 2, 1)
    dx, dproj, G["norm_mix"] = _inproj_bwd(x, dh1, du, dzs, norm_mix, mu, w_in[:, :S5_WIDTH], w_in[:, S5_WIDTH:], L)
    G["w_in"] = _mm_tn("dw_in", xn, dproj)
    return loss_acc[0, 0], dx, G


MESH_AXES = ("x", "y", "c")


def _all_gather(name, shards):
    nt = len(shards)

    def body(*refs):
        x_refs, out_refs = refs[:nt], refs[nt:2 * nt]
        send_sems, recv_sems, local_sems = refs[2 * nt:]
        x, y, c = lax.axis_index("x"), lax.axis_index("y"), lax.axis_index("c")
        me, sibling = (x, y, c), (x, y, 1 - c)
        chips = [(1 - x, y), (x, 1 - y), (1 - x, 1 - y)]

        def rows(t, px, py, pc):
            m_per = shards[t].shape[0]
            return out_refs[t].at[pl.ds((4 * px + 2 * py + pc) * m_per, m_per), :]

        def copy(t, k, block, to, src=None):
            return pltpu.make_async_remote_copy(
                src_ref=rows(t, *block) if src is None else src, dst_ref=rows(t, *block),
                send_sem=send_sems.at[7 * t + k], recv_sem=recv_sems.at[7 * t + k],
                device_id=to, device_id_type=pl.DeviceIdType.MESH)

        mine = [pltpu.make_async_copy(x_refs[t], rows(t, *me), local_sems.at[t]) for t in range(nt)]
        for cp in mine:
            cp.start()
        first = []
        for t in range(nt):
            first.append(copy(t, 0, me, sibling, src=x_refs[t]))
            first += [copy(t, 1 + j, me, (*chip, c), src=x_refs[t]) for j, chip in enumerate(chips)]
        for cp in first:
            cp.start()
        passed = []
        for t in range(nt):
            for j, chip in enumerate(chips):
                copy(t, 1 + j, (*chip, c), me).wait_recv()
                fwd = copy(t, 4 + j, (*chip, c), sibling)
                fwd.start()
                passed.append(fwd)
        for t in range(nt):
            copy(t, 0, sibling, me).wait_recv()
            for j, chip in enumerate(chips):
                copy(t, 4 + j, (*chip, 1 - c), me).wait_recv()
        for cp in first + passed:
            cp.wait_send()
        for cp in mine:
            cp.wait()

    return _pcall(body, name=name,
                  out_shape=[jax.ShapeDtypeStruct((N_DEV * a.shape[0], a.shape[1]), a.dtype) for a in shards],
                  in_specs=[_ANY] * nt, out_specs=[_ANY] * nt,
                  scratch_shapes=[pltpu.SemaphoreType.DMA((7 * nt,)), pltpu.SemaphoreType.DMA((7 * nt,)),
                                  pltpu.SemaphoreType.DMA((nt,))])(*shards)


_HBM = pl.BlockSpec(memory_space=pltpu.HBM)
_SEM = pl.BlockSpec(memory_space=pltpu.SEMAPHORE)
_EFFECT = pltpu.SideEffectType.DATAFLOW_SIDE_EFFECTING


def _peer_of(k):
    x, y, c = lax.axis_index("x"), lax.axis_index("y"), lax.axis_index("c")
    px, py, pc = x ^ ((k >> 2) & 1), y ^ ((k >> 1) & 1), c ^ (k & 1)
    return (px, py, pc), 4 * px + 2 * py + pc, 4 * x + 2 * y + c


def _direct_copy(t, k, src_refs, land_refs, send_sems, recv_sems, rows_of, gather):
    dev, peer, me = _peer_of(k)
    m = rows_of[t]
    src = src_refs[t] if gather else src_refs[t].at[pl.ds(peer * m, m), :]
    return pltpu.make_async_remote_copy(
        src_ref=src, dst_ref=land_refs[t].at[pl.ds(me * m, m), :],
        send_sem=send_sems.at[7 * t + k - 1], recv_sem=recv_sems.at[7 * t + k - 1],
        device_id=dev, device_id_type=pl.DeviceIdType.MESH)


def _direct_landing(t, k, src_refs, land_refs, send_sems, recv_sems, rows_of, gather):
    dev, peer, me = _peer_of(k)
    m = rows_of[t]
    src = src_refs[t] if gather else src_refs[t].at[pl.ds(me * m, m), :]
    return pltpu.make_async_remote_copy(
        src_ref=src, dst_ref=land_refs[t].at[pl.ds(peer * m, m), :],
        send_sem=send_sems.at[7 * t + k - 1], recv_sem=recv_sems.at[7 * t + k - 1],
        device_id=dev, device_id_type=pl.DeviceIdType.MESH)


def _direct_start(name, srcs, gather, dep=None):
    nt = len(srcs)
    rows_of = [a.shape[0] if gather else a.shape[0] // N_DEV for a in srcs]
    lands = [pltpu.with_memory_space_constraint(lax.empty((N_DEV * m, a.shape[1]), a.dtype), pltpu.HBM)
             for a, m in zip(srcs, rows_of)]

    n_dep = 0 if dep is None else 1

    def body(*refs):
        src_refs, land_refs = refs[:nt], refs[nt:2 * nt]
        send_sems, recv_sems = refs[2 * nt + n_dep], refs[2 * nt + n_dep + 1]
        token = refs[-1]
        for t in range(nt):
            for k in range(1, N_DEV):
                _direct_copy(t, k, src_refs, land_refs, send_sems, recv_sems, rows_of, gather).start()
        token[...] = jnp.zeros(token.shape, F32)

    out = _pcall(
        body, name=name,
        out_shape=(pltpu.SemaphoreType.DMA((7 * nt,)), pltpu.SemaphoreType.DMA((7 * nt,)),
                   *[pltpu.HBM(a.shape, a.dtype) for a in srcs], *[pltpu.HBM(a.shape, a.dtype) for a in lands],
                   jax.ShapeDtypeStruct((8, 128), F32)),
        in_specs=(_HBM,) * (2 * nt) + (pl.BlockSpec(memory_space=pl.ANY),) * n_dep,
        out_specs=(_SEM, _SEM) + (_HBM,) * (2 * nt) + (pl.BlockSpec(memory_space=pltpu.VMEM),),
        input_output_aliases={i: 2 + i for i in range(2 * nt)},
        compiler_params=pltpu.CompilerParams(has_side_effects=_EFFECT),
    )(*[pltpu.with_memory_space_constraint(a, pltpu.HBM) for a in srcs], *lands, *(() if dep is None else (dep,)))
    return (out[0], out[1], list(out[2:2 + nt]), list(out[2 + nt:2 + 2 * nt]), rows_of, gather), out[-1]


def _direct_wait(name, handle, after):
    send_sems, recv_sems, srcs, lands, rows_of, gather = handle
    nt = len(srcs)
    after = list(after) if isinstance(after, (list, tuple)) else [after]

    def body(*refs):
        src_refs, land_refs = refs[:nt], refs[nt:2 * nt]
        s_sems, r_sems = refs[2 * nt], refs[2 * nt + 1]
        for t in range(nt):
            for k in range(1, N_DEV):
                _direct_copy(t, k, src_refs, land_refs, s_sems, r_sems, rows_of, gather).wait_send()
                _direct_landing(t, k, src_refs, land_refs, s_sems, r_sems, rows_of, gather).wait_recv()

    out = _pcall(
        body, name=name,
        out_shape=tuple(pltpu.HBM(a.shape, a.dtype) for a in srcs) + tuple(pltpu.HBM(a.shape, a.dtype) for a in lands),
        in_specs=(_HBM,) * (2 * nt) + (_SEM, _SEM) + (pl.BlockSpec(memory_space=pl.ANY),) * len(after),
        out_specs=(_HBM,) * (2 * nt),
        input_output_aliases={i: i for i in range(2 * nt)},
        compiler_params=pltpu.CompilerParams(has_side_effects=_EFFECT),
    )(*srcs, *lands, send_sems, recv_sems, *after)
    return list(out[:nt]), list(out[nt:])


def _adamw_sharded(name, own, parts, w, m, v, rb, deps=()):
    R, N = own.shape

    def body(o_ref, p_ref, w_ref, m_ref, v_ref, *rest):
        g_ref, d_ref, nm_ref, nv_ref = rest[len(deps):]
        me = 4 * lax.axis_index("x") + 2 * lax.axis_index("y") + lax.axis_index("c")
        g = o_ref[...]
        for k in range(1, N_DEV):
            g = g + p_ref[me ^ k].astype(F32)
        nm = ADAM_B1 * m_ref[...] + (1.0 - ADAM_B1) * g
        nv = ADAM_B2 * v_ref[...] + (1.0 - ADAM_B2) * (g * g)
        m_hat = nm / (1.0 - ADAM_B1 ** ADAM_STEP)
        v_hat = nv / (1.0 - ADAM_B2 ** ADAM_STEP)
        g_ref[...] = g
        d_ref[...] = -ADAM_LR * (m_hat / (jnp.sqrt(v_hat) + ADAM_EPS) + ADAM_WD * w_ref[...])
        nm_ref[...] = nm
        nv_ref[...] = nv

    blk = pl.BlockSpec((rb, N), lambda i: (i, 0))
    sh = jax.ShapeDtypeStruct((R, N), F32)
    return _pcall(body, name=name, grid=(R // rb,),
                  in_specs=[blk, pl.BlockSpec((N_DEV, rb, N), lambda i: (0, i, 0)), blk, blk, blk]
                  + [pl.BlockSpec(d.shape, lambda i, nd=d.ndim: (0,) * nd) for d in deps],
                  out_specs=[blk] * 4, out_shape=[sh] * 4, compiler_params=_cparams(1))(own, parts, w, m, v, *deps)


def _adamw(name, parts, w, m, v, rb):
    _, R, N = parts.shape

    def body(p_ref, w_ref, m_ref, v_ref, g_ref, d_ref, nm_ref, nv_ref):
        g = p_ref[0]
        for s in range(1, N_DEV):
            g = g + p_ref[s]
        nm = ADAM_B1 * m_ref[...] + (1.0 - ADAM_B1) * g
        nv = ADAM_B2 * v_ref[...] + (1.0 - ADAM_B2) * (g * g)
        m_hat = nm / (1.0 - ADAM_B1 ** ADAM_STEP)
        v_hat = nv / (1.0 - ADAM_B2 ** ADAM_STEP)
        g_ref[...] = g
        d_ref[...] = -ADAM_LR * (m_hat / (jnp.sqrt(v_hat) + ADAM_EPS) + ADAM_WD * w_ref[...])
        nm_ref[...] = nm
        nv_ref[...] = nv

    blk = pl.BlockSpec((rb, N), lambda i: (i, 0))
    sh = jax.ShapeDtypeStruct((R, N), F32)
    return _pcall(body, name=name, grid=(R // rb,), in_specs=[pl.BlockSpec((N_DEV, rb, N), lambda i: (0, i, 0)), blk, blk, blk],
                  out_specs=[blk] * 4, out_shape=[sh] * 4, compiler_params=_cparams(1))(parts, w, m, v)


EARLY = (("w_in", True),)
LATE = (("ffn_w1", True), ("ffn_w3", True), ("ffn_w2", False), ("ple_gate_w", False), ("w_out", False))
GRAD_STAGES = (LATE[:4], LATE[4:])
MISC = (("s5_glu_w", False), ("rw_w2", True), ("rw_a2", True), ("rw_g2", True), ("ple_up_w", True))
SHARDED_NAMES = tuple(n for n, _ in EARLY + LATE + MISC)
PACK_COLS = 1024
SMALL_ROWS = 144
WEIGHT_NAMES = ("norm_mix", "w_in", "s5_lam_re", "s5_lam_im", "s5_log_step", "s5_b_re", "s5_b_im", "s5_c_re", "s5_c_im", "s5_d",
                "s5_glu_w", "s5_glu_b", "rw_shift_mu", "rw_w0", "rw_w2", "rw_a0", "rw_a2", "rw_g2", "rw_k_k", "rw_k_a", "rw_r_k",
                "rw_ln_w", "rw_ln_b", "w_out", "norm_ffn", "ffn_w1", "ffn_w3", "ffn_w2", "norm_ple", "ple_gate_w", "ple_up_w",
                "final_norm")
SMALL_NAMES = tuple(n for n in WEIGHT_NAMES if n not in SHARDED_NAMES)
ARG_NAMES = ("x", "p") + WEIGHT_NAMES + ("loss_target",) + tuple("m_" + n for n in WEIGHT_NAMES) + tuple("v_" + n for n in WEIGHT_NAMES)


def _travel(a, tr):
    return a.T if tr else a


def _pack_misc(blocks):
    lead = blocks[0].shape[:-2]
    return jnp.concatenate([b.reshape(lead + (-1, PACK_COLS)) for b in blocks], axis=len(lead))


def _unpack_misc(packed, shapes):
    lead = packed.shape[:-2]
    out, off = [], 0
    for r, c in shapes:
        n = r * c // PACK_COLS
        out.append(lax.slice_in_dim(packed, off, off + n, axis=len(lead)).reshape(lead + (r, c)))
        off += n
    return out


def _pack_small(arrs):
    flat = jnp.concatenate([a.reshape(-1).astype(F32) for a in arrs])
    return jnp.pad(flat, (0, SMALL_ROWS * PACK_COLS - flat.shape[0])).reshape(SMALL_ROWS, PACK_COLS)


def _kernel_impl(ins):
    x, p, target = ins["x"][0], ins["p"][0, 0], ins["loss_target"][0]
    me = 4 * lax.axis_index("x") + 2 * lax.axis_index("y") + lax.axis_index("c")
    small = {n: (ins[n] if n == "final_norm" else ins[n][0]) for n in SMALL_NAMES}
    trav = lambda pre, n, tr: _travel(ins[pre + n][0], tr)
    misc_shapes = [trav("", n, tr).shape for n, tr in MISC]

    early = _all_gather("ag_early", [trav("", n, tr).astype(BF16) for n, tr in EARLY]
                        + [_pack_misc([trav("", n, tr).astype(BF16) for n, tr in MISC])])
    late_handle, late_token = _direct_start("ag_late_start", [trav("", n, tr).astype(BF16) for n, tr in LATE], True, early[-1])
    W = dict(small)
    for (n, tr), g in zip(EARLY, early):
        W[n] = _travel(g, tr)
    for (n, tr), g in zip(MISC, _unpack_misc(early[-1].reshape(N_DEV, -1, PACK_COLS), misc_shapes)):
        W[n] = _travel(g.reshape(-1, g.shape[-1]), tr)

    def late_weights(after):
        shards, lands = _direct_wait("ag_late_wait", late_handle, after)
        full = [lax.dynamic_update_slice_in_dim(ld, sh, me * sh.shape[0], axis=0) for ld, sh in zip(lands, shards)]
        return {n: _travel(g, tr) for (n, tr), g in zip(LATE, full)}

    gt = lambda G, n, tr: _travel(G[n], tr)
    started = {}

    def grads_ready(stage, G):
        full = [gt(G, n, tr) for n, tr in GRAD_STAGES[stage]]
        started[stage] = (full, *_direct_start("grad_late_start%d" % stage, [a.astype(BF16) for a in full], False))
        return started[stage][2]

    loss_part, dx, G = _local_step(x, p, target, W, late_weights, grads_ready, late_token)

    misc_g = _pack_misc([gt(G, n, tr).reshape((N_DEV,) + shp) for (n, tr), shp in zip(MISC, misc_shapes)])
    early_full = [gt(G, n, tr) for n, tr in EARLY] + [misc_g.reshape(-1, PACK_COLS)]
    early_handle, early_token = _direct_start("grad_early_start", [a.astype(BF16) for a in early_full], False)
    small_own = _pack_small([G[n] for n in SMALL_NAMES])
    small_handle, small_token = _direct_start("grad_small_start", [small_own], True)
    late_src, late_land = [], []
    for stage in range(len(GRAD_STAGES)):
        full, handle, _ = started[stage]
        _, land = _direct_wait("grad_late_wait%d" % stage, handle, small_token)
        late_src += full
        late_land += land

    outs = {}

    def emit(names_shapes, res):
        for tag, val in zip(("grad_", "delta_", "new_m_", "new_v_"), res):
            for n, v in names_shapes(val):
                outs[tag + n] = v

    def sharded_update(n, tr, src, land, deps=()):
        rows = src.shape[0] // N_DEV
        own = lax.dynamic_slice_in_dim(src, me * rows, rows, axis=0)
        res = _adamw_sharded("adamw_" + n, own, land.reshape(N_DEV, rows, land.shape[1]),
                             trav("", n, tr), trav("m_", n, tr), trav("v_", n, tr), _pick_rows(rows), deps)
        emit(lambda val: [(n, _travel(val, tr).reshape(ins[n].shape))], res)
        return list(res)

    for (n, tr), src, land in zip(LATE, late_src, late_land):
        sharded_update(n, tr, src, land, (early_token,))
    _, early_land = _direct_wait("grad_early_wait", early_handle, list(outs.values()))
    for (n, tr), src, land in zip(EARLY, early_full[:-1], early_land[:-1]):
        sharded_update(n, tr, src, land)
    pm = lambda pre: _pack_misc([trav(pre, n, tr) for n, tr in MISC])
    rows = early_full[-1].shape[0] // N_DEV
    res = _adamw_sharded("adamw_misc", lax.dynamic_slice_in_dim(early_full[-1], me * rows, rows, axis=0),
                         early_land[-1].reshape(N_DEV, rows, PACK_COLS), pm(""), pm("m_"), pm("v_"), rows)
    emit(lambda val: [(n, _travel(b, tr).reshape(ins[n].shape)) for (n, tr), b in zip(MISC, _unpack_misc(val, misc_shapes))], res)
    ps = lambda pre: _pack_small([ins[pre + n] for n in SMALL_NAMES])
    small_src, small_land = _direct_wait("grad_small_wait", small_handle, res[0])
    gsm = lax.dynamic_update_slice_in_dim(small_land[0], small_src[0], me * SMALL_ROWS, axis=0)
    res = _adamw("adamw_replicated", gsm.reshape(N_DEV, SMALL_ROWS, PACK_COLS), ps(""), ps("m_"), ps("v_"), SMALL_ROWS)

    def split_small(val):
        flat, off, o = val.reshape(-1), 0, []
        for n in SMALL_NAMES:
            o.append((n, flat[off:off + ins[n].size].reshape(ins[n].shape)))
            off += ins[n].size
        return o

    emit(split_small, res)
    loss = lax.psum(loss_part, MESH_AXES)
    res = [loss, dx[None]]
    for tag in ("grad_", "delta_", "new_m_", "new_v_"):
        res += [outs[tag + n] for n in WEIGHT_NAMES]
    return tuple(res)


def _pick_rows(r):
    best = 8
    for b in range(8, 257, 8):
        if r % b == 0:
            best = b
    return best


def kernel(x, p, norm_mix, w_in, s5_lam_re, s5_lam_im, s5_log_step, s5_b_re, s5_b_im, s5_c_re, s5_c_im, s5_d, s5_glu_w, s5_glu_b, rw_shift_mu, rw_w0, rw_w2, rw_a0, rw_a2, rw_g2, rw_k_k, rw_k_a, rw_r_k, rw_ln_w, rw_ln_b, w_out, norm_ffn, ffn_w1, ffn_w3, ffn_w2, norm_ple, ple_gate_w, ple_up_w, final_norm, loss_target, m_norm_mix, m_w_in, m_s5_lam_re, m_s5_lam_im, m_s5_log_step, m_s5_b_re, m_s5_b_im, m_s5_c_re, m_s5_c_im, m_s5_d, m_s5_glu_w, m_s5_glu_b, m_rw_shift_mu, m_rw_w0, m_rw_w2, m_rw_a0, m_rw_a2, m_rw_g2, m_rw_k_k, m_rw_k_a, m_rw_r_k, m_rw_ln_w, m_rw_ln_b, m_w_out, m_norm_ffn, m_ffn_w1, m_ffn_w3, m_ffn_w2, m_norm_ple, m_ple_gate_w, m_ple_up_w, m_final_norm, v_norm_mix, v_w_in, v_s5_lam_re, v_s5_lam_im, v_s5_log_step, v_s5_b_re, v_s5_b_im, v_s5_c_re, v_s5_c_im, v_s5_d, v_s5_glu_w, v_s5_glu_b, v_rw_shift_mu, v_rw_w0, v_rw_w2, v_rw_a0, v_rw_a2, v_rw_g2, v_rw_k_k, v_rw_k_a, v_rw_r_k, v_rw_ln_w, v_rw_ln_b, v_w_out, v_norm_ffn, v_ffn_w1, v_ffn_w3, v_ffn_w2, v_norm_ple, v_ple_gate_w, v_ple_up_w, v_final_norm):
    return _kernel_impl(dict(zip(ARG_NAMES, (x, p, norm_mix, w_in, s5_lam_re, s5_lam_im, s5_log_step, s5_b_re, s5_b_im, s5_c_re, s5_c_im, s5_d, s5_glu_w, s5_glu_b, rw_shift_mu, rw_w0, rw_w2, rw_a0, rw_a2, rw_g2, rw_k_k, rw_k_a, rw_r_k, rw_ln_w, rw_ln_b, w_out, norm_ffn, ffn_w1, ffn_w3, ffn_w2, norm_ple, ple_gate_w, ple_up_w, final_norm, loss_target, m_norm_mix, m_w_in, m_s5_lam_re, m_s5_lam_im, m_s5_log_step, m_s5_b_re, m_s5_b_im, m_s5_c_re, m_s5_c_im, m_s5_d, m_s5_glu_w, m_s5_glu_b, m_rw_shift_mu, m_rw_w0, m_rw_w2, m_rw_a0, m_rw_a2, m_rw_g2, m_rw_k_k, m_rw_k_a, m_rw_r_k, m_rw_ln_w, m_rw_ln_b, m_w_out, m_norm_ffn, m_ffn_w1, m_ffn_w3, m_ffn_w2, m_norm_ple, m_ple_gate_w, m_ple_up_w, m_final_norm, v_norm_mix, v_w_in, v_s5_lam_re, v_s5_lam_im, v_s5_log_step, v_s5_b_re, v_s5_b_im, v_s5_c_re, v_s5_c_im, v_s5_d, v_s5_glu_w, v_s5_glu_b, v_rw_shift_mu, v_rw_w0, v_rw_w2, v_rw_a0, v_rw_a2, v_rw_g2, v_rw_k_k, v_rw_k_a, v_rw_r_k, v_rw_ln_w, v_rw_ln_b, v_w_out, v_norm_ffn, v_ffn_w1, v_ffn_w3, v_ffn_w2, v_norm_ple, v_ple_gate_w, v_ple_up_w, v_final_norm))))
```

```python
import functools

import jax
import jax.numpy as jnp
from jax import lax
from jax.experimental import pallas as pl
from jax.experimental.pallas import tpu as pltpu

F32 = jnp.float32
BF16 = jnp.bfloat16

D_MODEL = 1024
S5_WIDTH = 512
RW_WIDTH = 512
S5_GROUP = 16
S5_GROUPS = 32
S5_STATE = 64
S5_LANES = S5_GROUPS * S5_STATE
HEAD = 64
SHIFT_COLS = 1792
IN_COLS = 2304
FFN_HIDDEN = 2816
PLE_DIM = 256
RMS_EPS = 1e-6
GN_EPS = 64e-5
L2_EPS = 1e-12
CHUNK = 64
N_DEV = 8

ADAM_LR = 0.001
ADAM_B1 = 0.9
ADAM_B2 = 0.999
ADAM_EPS = 1e-08
ADAM_WD = 0.01
ADAM_STEP = 10

VMEM_LIMIT = 56 * 1024 * 1024
_ANY = pl.BlockSpec(memory_space=pl.ANY)


def _pcall(body, **kw):
    return pl.pallas_call(body, **kw)


def _cparams(n_grid):
    return pltpu.CompilerParams(dimension_semantics=("arbitrary",) * n_grid, vmem_limit_bytes=VMEM_LIMIT)


def _dot(a, b):
    return jnp.dot(a, b, preferred_element_type=F32)


def _dot_nt(a, b):
    return lax.dot_general(a, b, (((1,), (1,)), ((), ())), preferred_element_type=F32)


def _dot_tn(a, b):
    return lax.dot_general(a, b, (((0,), (0,)), ((), ())), preferred_element_type=F32)


def _mmc(w, diff=True, tr=False):
    fw, bw = (_dot_nt, _dot) if tr else (_dot, _dot_nt)
    if not diff:
        return lambda x: fw(x.astype(BF16), w)

    @jax.custom_vjp
    def f(x):
        return fw(x.astype(BF16), w)

    def fwd(x):
        return fw(x.astype(BF16), w), None

    def bwd(_, dy):
        return (bw(dy.astype(BF16), w),)

    f.defvjp(fwd, bwd)
    return f


def _split_dot(x, m, n_split):
    acc = None
    rem = x
    for s in range(n_split):
        part = rem.astype(BF16)
        t = _dot(part, m)
        acc = t if acc is None else acc + t
        if s + 1 < n_split:
            rem = rem - part.astype(F32)
    return acc


def _segsum(m, diff=True):
    if not diff:
        return lambda x: _split_dot(x, m, 2)

    @jax.custom_vjp
    def f(x):
        return _split_dot(x, m, 2)

    def fwd(x):
        return _split_dot(x, m, 2), None

    def bwd(_, dy):
        return (_split_dot(dy, m, 2),)

    f.defvjp(fwd, bwd)
    return f


def _head_indicator(n):
    r = lax.broadcasted_iota(jnp.int32, (n, n), 0) // HEAD
    c = lax.broadcasted_iota(jnp.int32, (n, n), 1) // HEAD
    return (r == c).astype(BF16)


def _rms(x, g):
    return x * lax.rsqrt(jnp.mean(x * x, axis=-1, keepdims=True) + RMS_EPS) * g


def _softplus(x):
    return jnp.maximum(x, 0.0) + jnp.log(1.0 + jnp.exp(-jnp.abs(x)))


def _sigmoid(x):
    return 1.0 / (1.0 + jnp.exp(-x))


def _gelu(x):
    return 0.5 * x * (1.0 + jnp.tanh(0.7978845608028654 * (x + 0.044715 * (x * x * x))))


def _tok_call(name, fn, L, TB, tok_in, const_in, tok_out, acc_out=(), deps=()):
    nb = L // TB
    g8 = TB // 8
    in_specs, args = [], []
    for spec in tok_in:
        if len(spec) == 1:
            arr = spec[0]
            in_specs.append(pl.BlockSpec((arr.shape[0], TB, HEAD), lambda i: (0, i, 0)))
            args.append(arr)
            continue
        arr, width, cb = spec[:3]
        mode = spec[3] if len(spec) > 3 else None
        if mode is None:
            in_specs.append(pl.BlockSpec((TB, width), lambda i, cb=cb: (i, cb)))
        elif mode == "prev":
            in_specs.append(pl.BlockSpec((8, width), lambda i, cb=cb: (jnp.maximum(i * g8 - 1, 0), cb)))
        else:
            in_specs.append(pl.BlockSpec((8, width), lambda i, cb=cb: (jnp.minimum((i + 1) * g8, L // 8 - 1), cb)))
        args.append(arr)
    for c in const_in:
        in_specs.append(pl.BlockSpec(c.shape, lambda i, nd=c.ndim: (0,) * nd, pipeline_mode=pl.Buffered(1)))
        args.append(c)
    for d in deps:
        in_specs.append(pl.BlockSpec(d.shape, lambda i, nd=d.ndim: (0,) * nd))
        args.append(d)
    out_shape, out_specs = [], []
    for width, dt in tok_out:
        if width == "heads":
            out_shape.append(jax.ShapeDtypeStruct((N_HEAD, L, HEAD), dt))
            out_specs.append(pl.BlockSpec((N_HEAD, TB, HEAD), lambda i: (0, i, 0)))
            continue
        out_shape.append(jax.ShapeDtypeStruct((L, width), dt))
        out_specs.append(pl.BlockSpec((TB, width), lambda i: (i, 0)))
    for shp in acc_out:
        out_shape.append(jax.ShapeDtypeStruct(shp, F32))
        out_specs.append(pl.BlockSpec(shp, lambda i, nd=len(shp): (0,) * nd))
    n_tok, n_const, n_to = len(tok_in), len(const_in), len(tok_out)

    def body(*refs):
        i = pl.program_id(0)
        tv = [r[...] if len(r.shape) == 2 else jnp.concatenate([r[h] for h in range(r.shape[0])], axis=1)
              for r in refs[:n_tok]]
        cv = [r[...] for r in refs[n_tok:n_tok + n_const]]
        orefs = refs[n_tok + n_const + len(deps):]
        outs = fn(i, tv, cv)
        for r, v in zip(orefs[:n_to], outs[:n_to]):
            if len(r.shape) == 3:
                for h in range(r.shape[0]):
                    r[h] = v[:, h * HEAD:(h + 1) * HEAD].astype(r.dtype)
            else:
                r[...] = v.astype(r.dtype)
        for r, v in zip(orefs[n_to:], outs[n_to:]):
            @pl.when(i == 0)
            def _(r=r):
                r[...] = jnp.zeros(r.shape, r.dtype)

            r[...] += v

    res = _pcall(body, name=name, grid=(nb,), in_specs=in_specs, out_specs=out_specs, out_shape=out_shape,
                 compiler_params=_cparams(1))(*args)
    return res


def _pick_block(n, cap):
    best = None
    for b in range(128, min(n, cap) + 1, 128):
        if n % b == 0:
            best = b
    return best if best is not None else n


def _mm_tn(name, a, b):
    T, M = a.shape
    N = b.shape[1]
    bm, bn, bt = _pick_block(M, 1536), _pick_block(N, 1536), _pick_block(T, 512)

    def body(a_ref, b_ref, o_ref):
        t = pl.program_id(2)

        @pl.when(t == 0)
        def _():
            o_ref[...] = jnp.zeros(o_ref.shape, F32)

        o_ref[...] += _dot_tn(a_ref[...].astype(BF16), b_ref[...].astype(BF16))

    return _pcall(body, name=name, grid=(M // bm, N // bn, T // bt),
                  in_specs=[pl.BlockSpec((bt, bm), lambda m, n, t: (t, m)), pl.BlockSpec((bt, bn), lambda m, n, t: (t, n))],
                  out_specs=pl.BlockSpec((bm, bn), lambda m, n, t: (m, n)),
                  out_shape=jax.ShapeDtypeStruct((M, N), F32), compiler_params=_cparams(3))(a, b)


def _s5_param_fn(lam_re, lam_im, log_step, bt_re, bt_im):
    dt = jnp.exp(log_step)
    e = jnp.exp(lam_re * dt)
    lb_re = e * jnp.cos(lam_im * dt)
    lb_im = e * jnp.sin(lam_im * dt)
    den = lam_re * lam_re + lam_im * lam_im
    nr, ni = lb_re - 1.0, lb_im
    co_re = (nr * lam_re + ni * lam_im) / den
    co_im = (ni * lam_re - nr * lam_im) / den
    cr, ci = co_re[:, None, :], co_im[:, None, :]
    return lb_re, lb_im, cr * bt_re - ci * bt_im, cr * bt_im + ci * bt_re


def _s5_param_fwd(lam_re, lam_im, log_step, bt_re, bt_im):
    def body(a, b, c, d, e, o1, o2, o3, o4):
        r = _s5_param_fn(a[...], b[...], c[...], d[...], e[...])
        o1[...], o2[...], o3[...], o4[...] = r

    sh = jax.ShapeDtypeStruct
    return _pcall(body, name="s5_param_fwd",
                  out_shape=[sh(lam_re.shape, F32), sh(lam_re.shape, F32), sh(bt_re.shape, F32), sh(bt_re.shape, F32)])(
        lam_re, lam_im, log_step, bt_re, bt_im)


def _s5_param_bwd(lam_re, lam_im, log_step, bt_re, bt_im, d_lb_re, d_lb_im, d_bb_re, d_bb_im):
    def body(a, b, c, d, e, g1, g2, g3, g4, o1, o2, o3, o4, o5):
        _, vjp = jax.vjp(_s5_param_fn, a[...], b[...], c[...], d[...], e[...])
        r = vjp((g1[...], g2[...], g3[...], g4[...]))
        o1[...], o2[...], o3[...], o4[...], o5[...] = r

    sh = jax.ShapeDtypeStruct
    return _pcall(body, name="s5_param_bwd",
                  out_shape=[sh(lam_re.shape, F32), sh(lam_re.shape, F32), sh(log_step.shape, F32),
                             sh(bt_re.shape, F32), sh(bt_re.shape, F32)])(
        lam_re, lam_im, log_step, bt_re, bt_im, d_lb_re, d_lb_im, d_bb_re, d_bb_im)


def _cmul(ar, ai, br, bi):
    return ar * br - ai * bi, ar * bi + ai * br


def _scan_consts(lr, li, reverse):
    n = lr.shape[1]
    sub = lax.broadcasted_iota(jnp.int32, (8, n), 0)
    pows = [(lr, li)]
    for _ in range(7):
        pows.append(_cmul(pows[-1][0], pows[-1][1], lr, li))
    steps = []
    for s in (1, 2, 4):
        m = (sub < 8 - s) if reverse else (sub >= s)
        pr, pi = pows[s - 1]
        steps.append((s, jnp.where(m, jnp.broadcast_to(pr, (8, n)), 0.0), jnp.where(m, jnp.broadcast_to(pi, (8, n)), 0.0)))
    wr = jnp.zeros((8, n), F32)
    wi = jnp.zeros((8, n), F32)
    for r in range(8):
        e = (8 - r) if reverse else (r + 1)
        wr = jnp.where(sub == r, jnp.broadcast_to(pows[e - 1][0], (8, n)), wr)
        wi = jnp.where(sub == r, jnp.broadcast_to(pows[e - 1][1], (8, n)), wi)
    return steps, wr, wi


S5_Q = 4
S5_QL = S5_WIDTH // S5_Q
S5_QS = S5_LANES // S5_Q
S5_NT = S5_LANES // 128
S5_QT = S5_QS // 128


def _s5_power_table(lb_ref, pw_re, pw_im, seg):
    for j in range(S5_NT):
        lr = jnp.broadcast_to(lb_ref[0:1, j * 128:(j + 1) * 128], (8, 128))
        li = jnp.broadcast_to(lb_ref[1:2, j * 128:(j + 1) * 128], (8, 128))

        def step(i, c, lr=lr, li=li, j=j):
            pw_re[j, i] = c[0]
            pw_im[j, i] = c[1]
            return _cmul(c[0], c[1], lr, li)

        lax.fori_loop(0, seg, step, (lr, li))


def _seg_scan(sre, sim, carry, lb_ref, pw_re, pw_im, rows, reverse):
    seg = rows // 8
    sgn = -1.0 if reverse else 1.0
    sub = lax.broadcasted_iota(jnp.int32, (8, 128), 0)
    rows_at = lambda i: pl.ds(pl.multiple_of(i * 8, 8), 8)
    entering = {}
    half_tiles = S5_NT // 2
    for half in range(2):
        tiles = list(range(half * half_tiles, (half + 1) * half_tiles))
        lam8 = [(jnp.broadcast_to(lb_ref[0:1, j * 128:(j + 1) * 128], (8, 128)),
                 sgn * jnp.broadcast_to(lb_ref[1:2, j * 128:(j + 1) * 128], (8, 128))) for j in tiles]

        def p1(ii, c):
            i = (seg - 1 - ii) if reverse else ii
            out = []
            for n, j in enumerate(tiles):
                lr, li = lam8[n]
                cr, ci = c[2 * n], c[2 * n + 1]
                nr = lr * cr - li * ci + sre[j, rows_at(i), :]
                ni = lr * ci + li * cr + sim[j, rows_at(i), :]
                sre[j, rows_at(i), :] = nr
                sim[j, rows_at(i), :] = ni
                out += [nr, ni]
            return tuple(out)

        ends = lax.fori_loop(0, seg, p1, tuple(jnp.zeros((8, 128), F32) for _ in range(2 * len(tiles))))
        cs = []
        for n, j in enumerate(tiles):
            ls = slice(j * 128, (j + 1) * 128)
            steps, wr, wi = _scan_consts(pw_re[j, seg - 1][0:1, :], sgn * pw_im[j, seg - 1][0:1, :], reverse)
            tr, ti = ends[2 * n], ends[2 * n + 1]
            for sft, pr, pi in steps:
                sh = (8 - sft) if reverse else sft
                yr, yi = pltpu.roll(tr, sh, 0), pltpu.roll(ti, sh, 0)
                tr, ti = tr + pr * yr - pi * yi, ti + pr * yi + pi * yr
            cin_r, cin_i = carry[0:1, ls], carry[1:2, ls]
            tr, ti = tr + wr * cin_r - wi * cin_i, ti + wr * cin_i + wi * cin_r
            edge_out, edge_in, sh = (0, 7, 7) if reverse else (7, 0, 1)
            carry[0:1, ls] = tr[edge_out:edge_out + 1, :]
            carry[1:2, ls] = ti[edge_out:edge_out + 1, :]
            cr = jnp.where(sub == edge_in, jnp.broadcast_to(cin_r, (8, 128)), pltpu.roll(tr, sh, 0))
            ci = jnp.where(sub == edge_in, jnp.broadcast_to(cin_i, (8, 128)), pltpu.roll(ti, sh, 0))
            cs += [cr, ci]
            entering[j] = (cr, ci)

        def p2(i, _):
            k = (seg - 1 - i) if reverse else i
            for n, j in enumerate(tiles):
                pr, pi = pw_re[j, k], pw_im[j, k]
                cr, ci = cs[2 * n], cs[2 * n + 1]
                if reverse:
                    sre[j, rows_at(i), :] = sre[j, rows_at(i), :] + pr * cr + pi * ci
                    sim[j, rows_at(i), :] = sim[j, rows_at(i), :] + pr * ci - pi * cr
                else:
                    sre[j, rows_at(i), :] = sre[j, rows_at(i), :] + pr * cr - pi * ci
                    sim[j, rows_at(i), :] = sim[j, rows_at(i), :] + pr * ci + pi * cr
            return 0

        lax.fori_loop(0, seg, p2, 0, unroll=2)
    return entering


class _SegIO:
    def __init__(self, hbm, buf, sems, rows, width, col0=0):
        self.hbm, self.buf, self.sems, self.rows, self.seg, self.width, self.col0 = hbm, buf, sems, rows, rows // 8, width, col0

    def _copies(self, blk, slot, to_vmem):
        out = []
        for r in range(8):
            h = self.hbm.at[pl.ds(blk * self.rows + r * self.seg, self.seg), pl.ds(self.col0, self.width)]
            v = self.buf.at[slot, :, r, :]
            out.append(pltpu.make_async_copy(h, v, self.sems.at[slot, r]) if to_vmem
                       else pltpu.make_async_copy(v, h, self.sems.at[slot, r]))
        return out

    def start(self, blk, slot, to_vmem):
        for cp in self._copies(blk, slot, to_vmem):
            cp.start()

    def wait(self, blk, slot, to_vmem):
        for cp in self._copies(blk, slot, to_vmem):
            cp.wait()

    def value(self, slot):
        return self.buf[slot].reshape(self.rows, self.width)

    def store(self, slot, val):
        self.buf[slot] = val.reshape(self.seg, 8, self.width)


def _seg_pipeline(i, nb, blk_of, ins, outs, compute):
    slot = i % 2

    @pl.when(i == 0)
    def _():
        for io in ins:
            io.start(blk_of(0), 0, True)

    @pl.when(i + 1 < nb)
    def _():
        for io in ins:
            io.start(blk_of(i + 1), 1 - slot, True)

    for io in ins:
        io.wait(blk_of(i), slot, True)

    @pl.when(i >= 2)
    def _():
        for io in outs:
            io.wait(blk_of(i - 2), slot, False)

    compute(slot)
    for io in outs:
        io.start(blk_of(i), slot, False)

    @pl.when(i == nb - 1)
    def _():
        for io in outs:
            if nb >= 2:
                io.wait(blk_of(i - 1), 1 - slot, False)
            io.wait(blk_of(i), slot, False)


def _s5_scan_fwd(proj, bq_re, bq_im, cq_re, cq_im, lbar, dskip, L, TB):
    nb = L // TB
    seg = TB // 8

    def body(u_hbm, bre, bim, cre, cim, lb_ref, d_ref, y_hbm, ck_ref, sre, sim, carry, pw_re, pw_im,
             ubuf, ybuf, sem_u, sem_y):
        i = pl.program_id(0)
        u_io = _SegIO(u_hbm, ubuf, sem_u, TB, S5_WIDTH)
        y_io = _SegIO(y_hbm, ybuf, sem_y, TB, S5_WIDTH)

        @pl.when(i == 0)
        def _():
            carry[...] = jnp.zeros(carry.shape, F32)
            _s5_power_table(lb_ref, pw_re, pw_im, seg)

        ck_ref[0] = carry[...]

        def compute(slot):
            u = u_io.value(slot)
            ub = u.astype(BF16)
            for q in range(S5_Q):
                uq = ub[:, q * S5_QL:(q + 1) * S5_QL]
                vr, vi = _dot(uq, bre[q]), _dot(uq, bim[q])
                for jj in range(S5_QT):
                    sre[q * S5_QT + jj] = vr[:, jj * 128:(jj + 1) * 128]
                    sim[q * S5_QT + jj] = vi[:, jj * 128:(jj + 1) * 128]
            _seg_scan(sre, sim, carry, lb_ref, pw_re, pw_im, TB, False)
            ys = []
            for q in range(S5_Q):
                sl = slice(q * S5_QL, (q + 1) * S5_QL)
                sr = jnp.concatenate([sre[q * S5_QT + jj] for jj in range(S5_QT)], axis=1).astype(BF16)
                si = jnp.concatenate([sim[q * S5_QT + jj] for jj in range(S5_QT)], axis=1).astype(BF16)
                ys.append(_dot(sr, cre[q]) - _dot(si, cim[q]) + u[:, sl] * d_ref[:, sl])
            y_io.store(slot, jnp.concatenate(ys, axis=1))

        _seg_pipeline(i, nb, lambda st: st, [u_io], [y_io], compute)

    full = lambda a: pl.BlockSpec(a.shape, lambda i, nd=a.ndim: (0,) * nd)
    st = pltpu.VMEM((S5_NT, TB, 128), F32)
    pw = pltpu.VMEM((S5_NT, seg, 8, 128), F32)
    io = pltpu.VMEM((2, seg, 8, S5_WIDTH), F32)
    return _pcall(
        body, name="s5_scan_fwd", grid=(nb,),
        in_specs=[_ANY, full(bq_re), full(bq_im), full(cq_re), full(cq_im), full(lbar), full(dskip)],
        out_specs=[_ANY, pl.BlockSpec((1, 8, S5_LANES), lambda i: (i, 0, 0))],
        out_shape=[jax.ShapeDtypeStruct((L, S5_WIDTH), F32), jax.ShapeDtypeStruct((nb, 8, S5_LANES), F32)],
        scratch_shapes=[st, st, pltpu.VMEM((8, S5_LANES), F32), pw, pw, io, io,
                        pltpu.SemaphoreType.DMA((2, 8)), pltpu.SemaphoreType.DMA((2, 8))],
        compiler_params=_cparams(1))(proj, bq_re, bq_im, cq_re, cq_im, lbar, dskip)


def _s5_scan_bwd(proj, dy, ck, bq_re, bq_im, cq_re, cq_im, lbar, dskip, L, TB):
    nb = L // TB
    seg = TB // 8

    def body(u_hbm, dy_hbm, ck_ref, bre, bim, cre, cim, lb_ref, d_ref,
             du_hbm, dbre, dbim, dcre, dcim, dlb_ref, dd_ref, sre, sim, gre, gim, carry, gcarry, pw_re, pw_im,
             ubuf, dybuf, dubuf, sem_u, sem_dy, sem_du):
        i = pl.program_id(0)
        u_io = _SegIO(u_hbm, ubuf, sem_u, TB, S5_WIDTH)
        dy_io = _SegIO(dy_hbm, dybuf, sem_dy, TB, S5_WIDTH)
        du_io = _SegIO(du_hbm, dubuf, sem_du, TB, S5_WIDTH)

        @pl.when(i == 0)
        def _():
            gcarry[...] = jnp.zeros(gcarry.shape, F32)
            dbre[...] = jnp.zeros(dbre.shape, F32)
            dbim[...] = jnp.zeros(dbim.shape, F32)
            dcre[...] = jnp.zeros(dcre.shape, F32)
            dcim[...] = jnp.zeros(dcim.shape, F32)
            dlb_ref[...] = jnp.zeros(dlb_ref.shape, F32)
            dd_ref[...] = jnp.zeros(dd_ref.shape, F32)
            _s5_power_table(lb_ref, pw_re, pw_im, seg)

        def compute(slot):
            u = u_io.value(slot)
            dy_v = dy_io.value(slot)
            ub = u.astype(BF16)
            dyb = dy_v.astype(BF16)
            carry[...] = ck_ref[0]
            for q in range(S5_Q):
                uq = ub[:, q * S5_QL:(q + 1) * S5_QL]
                dq = dyb[:, q * S5_QL:(q + 1) * S5_QL]
                vr, vi = _dot(uq, bre[q]), _dot(uq, bim[q])
                hr, hi = _dot_nt(dq, cre[q]), -_dot_nt(dq, cim[q])
                for jj in range(S5_QT):
                    ls = slice(jj * 128, (jj + 1) * 128)
                    sre[q * S5_QT + jj] = vr[:, ls]
                    sim[q * S5_QT + jj] = vi[:, ls]
                    gre[q * S5_QT + jj] = hr[:, ls]
                    gim[q * S5_QT + jj] = hi[:, ls]
            entering = _seg_scan(sre, sim, carry, lb_ref, pw_re, pw_im, TB, False)
            _seg_scan(gre, gim, gcarry, lb_ref, pw_re, pw_im, TB, True)

            rows_at = lambda k: pl.ds(pl.multiple_of(k * 8, 8), 8)
            for j in range(S5_NT):
                er, ei = entering[j]
                gr0, gi0 = gre[j, rows_at(0), :], gim[j, rows_at(0), :]
                acc0 = (gr0 * er + gi0 * ei, gi0 * er - gr0 * ei)

                def acc_step(k, acc, j=j):
                    gr, gi_ = gre[j, rows_at(k), :], gim[j, rows_at(k), :]
                    spr, spi = sre[j, rows_at(k - 1), :], sim[j, rows_at(k - 1), :]
                    return acc[0] + gr * spr + gi_ * spi, acc[1] - gr * spi + gi_ * spr

                ar, ai = lax.fori_loop(1, seg, acc_step, acc0, unroll=2 if (seg - 1) % 2 == 0 else 1)
                ls = slice(j * 128, (j + 1) * 128)
                dlb_ref[0:1, ls] += jnp.sum(ar, axis=0, keepdims=True)
                dlb_ref[1:2, ls] += jnp.sum(ai, axis=0, keepdims=True)

            dd_ref[...] += jnp.sum(dy_v * u, axis=0, keepdims=True)
            dus = []
            for q in range(S5_Q):
                sl = slice(q * S5_QL, (q + 1) * S5_QL)
                cat = lambda ref: jnp.concatenate([ref[q * S5_QT + jj] for jj in range(S5_QT)], axis=1).astype(BF16)
                grq, giq = cat(gre), cat(gim)
                dus.append(_dot_nt(grq, bre[q]) + _dot_nt(giq, bim[q]) + dy_v[:, sl] * d_ref[:, sl])
                dbre[q] += _dot_tn(ub[:, sl], grq)
                dbim[q] += _dot_tn(ub[:, sl], giq)
                dcre[q] += _dot_tn(cat(sre), dyb[:, sl])
                dcim[q] -= _dot_tn(cat(sim), dyb[:, sl])
            du_io.store(slot, jnp.concatenate(dus, axis=1))

        _seg_pipeline(i, nb, lambda st: nb - 1 - st, [u_io, dy_io], [du_io], compute)

    full = lambda a: pl.BlockSpec(a.shape, lambda i, nd=a.ndim: (0,) * nd)
    sh = jax.ShapeDtypeStruct
    outs = [sh((L, S5_WIDTH), F32), sh(bq_re.shape, F32), sh(bq_im.shape, F32), sh(cq_re.shape, F32), sh(cq_im.shape, F32),
            sh((8, S5_LANES), F32), sh((1, S5_WIDTH), F32)]
    fo = lambda s: pl.BlockSpec(s.shape, lambda i, nd=len(s.shape): (0,) * nd)
    st = pltpu.VMEM((S5_NT, TB, 128), F32)
    pw = pltpu.VMEM((S5_NT, seg, 8, 128), F32)
    io = pltpu.VMEM((2, seg, 8, S5_WIDTH), F32)
    sem = pltpu.SemaphoreType.DMA((2, 8))
    return _pcall(
        body, name="s5_scan_bwd", grid=(nb,),
        in_specs=[_ANY, _ANY, pl.BlockSpec((1, 8, S5_LANES), lambda i: (nb - 1 - i, 0, 0)),
                  full(bq_re), full(bq_im), full(cq_re), full(cq_im), full(lbar), full(dskip)],
        out_specs=[_ANY] + [fo(s) for s in outs[1:]],
        out_shape=outs,
        scratch_shapes=[st] * 4 + [pltpu.VMEM((8, S5_LANES), F32)] * 2 + [pw, pw, io, io, io, sem, sem, sem],
        compiler_params=_cparams(1))(proj, dy, ck, bq_re, bq_im, cq_re, cq_im, lbar, dskip)


N_HEAD = RW_WIDTH // HEAD
_NN = (((2,), (1,)), ((0,), (0,)))
_NT = (((2,), (2,)), ((0,), (0,)))
_TN = (((1,), (1,)), ((0,), (0,)))


def _hi_lo(x):
    h = x.astype(BF16)
    return h, (x - h.astype(F32)).astype(BF16)


def _mm_acc(a, b, dims, passes=3):
    dg = lambda p, q: lax.dot_general(p, q, dims, preferred_element_type=F32)
    if passes == 1:
        return dg(a.astype(BF16), b.astype(BF16))
    ah, al = _hi_lo(a)
    bh, bl = _hi_lo(b)
    return dg(ah, bh) + dg(ah, bl) + dg(al, bh)


def _cumsum_rows(x, transpose):
    h, n, _ = x.shape
    ti = lax.broadcasted_iota(jnp.int32, (h, n, n), 1)
    tj = lax.broadcasted_iota(jnp.int32, (h, n, n), 2)
    m = ((tj >= ti) if transpose else (tj <= ti)).astype(BF16)
    acc, rem = None, x
    for s in range(3):
        part = rem.astype(BF16)
        t = lax.dot_general(m, part, _NN, preferred_element_type=F32)
        acc = t if acc is None else acc + t
        if s < 2:
            rem = rem - part.astype(F32)
    return acc


def _slices(x, axis, sizes):
    out, off = [], 0
    for n in sizes:
        out.append(lax.slice_in_dim(x, off, off + n, axis=axis))
        off += n
    return tuple(out)


def _cat_op(axis, sizes, diff):
    plain = lambda *xs: jnp.concatenate(xs, axis=axis)
    if not diff:
        return plain
    f = jax.custom_vjp(plain)
    f.defvjp(lambda *xs: (plain(*xs), None), lambda _, d: _slices(d, axis, sizes))
    return f


def _split_op(axis, sizes, diff):
    plain = lambda x: _slices(x, axis, sizes)
    if not diff:
        return plain
    f = jax.custom_vjp(plain)
    f.defvjp(lambda x: (plain(x), None), lambda _, d: (jnp.concatenate(d, axis=axis),))
    return f


def _mm_ops(diff, passes):
    mm = lambda a, b, dims: _mm_acc(a, b, dims, passes)
    if not diff:
        return (lambda a, b: mm(a, b, _NN), lambda a, b: mm(a, b, _NT), lambda a, b: mm(a, b, _TN))

    @jax.custom_vjp
    def nn(a, b):
        return mm(a, b, _NN)

    nn.defvjp(lambda a, b: (mm(a, b, _NN), (a, b)), lambda r, d: (mm(d, r[1], _NT), mm(r[0], d, _TN)))

    @jax.custom_vjp
    def nt(a, b):
        return mm(a, b, _NT)

    nt.defvjp(lambda a, b: (mm(a, b, _NT), (a, b)), lambda r, d: (mm(d, r[1], _NN), mm(d, r[0], _TN)))

    @jax.custom_vjp
    def tn(a, b):
        return mm(a, b, _TN)

    tn.defvjp(lambda a, b: (mm(a, b, _TN), (a, b)), lambda r, d: (mm(r[1], d, _NT), mm(r[0], d, _NN)))
    return nn, nt, tn


def _cums_op(diff):
    if not diff:
        return lambda x: _cumsum_rows(x, False)

    @jax.custom_vjp
    def cums(x):
        return _cumsum_rows(x, False)

    cums.defvjp(lambda x: (_cumsum_rows(x, False), None), lambda _, d: (_cumsum_rows(d, True),))
    return cums


WKV_PASSES = (1, 1, 1, 1, 1)


WKV_SUB = 4
WKV_BLOCK = CHUNK * WKV_SUB


def _wkv_block(s0, r, w, k, v, a, b, diff):
    p_pair, p_val, p_solve, p_out, p_state = WKV_PASSES
    cums = _cums_op(diff)
    _, nt_pair, _ = _mm_ops(diff, p_pair)
    nn_val, _, _ = _mm_ops(diff, p_val)
    nn_solve, _, _ = _mm_ops(diff, p_solve)
    nn_out, nt_out, _ = _mm_ops(diff, p_out)
    nn_state, _, tn_state = _mm_ops(diff, p_state)
    h, d, n, sub = s0.shape[0], s0.shape[2], CHUNK, WKV_SUB
    hb = h * sub
    to_chunks = lambda t: _cat_op(0, (h,) * sub, diff)(*_split_op(1, (n,) * sub, diff)(t))
    r, w, k, v, a, b = (to_chunks(t) for t in (r, w, k, v, a, b))
    cat_rows2 = _cat_op(1, (n, n), diff)
    cat_lanes2 = _cat_op(2, (n, n), diff)
    split_rows2 = _split_op(1, (n, n), diff)
    split_lanes2 = _split_op(2, (n, n), diff)
    ti = lax.broadcasted_iota(jnp.int32, (hb, n, n), 1)
    tj = lax.broadcasted_iota(jnp.int32, (hb, n, n), 2)
    incl, strict = tj <= ti, tj < ti
    logw = jnp.log(w)
    cum = cums(logw)
    g_in, g_ex, g_inv = jnp.exp(cum), jnp.exp(cum - logw), jnp.exp(-cum)
    ae, re, bi, ki = a * g_ex, r * g_in, b * g_inv, k * g_inv
    top, bot = split_rows2(nt_pair(cat_rows2(ae, re), cat_rows2(bi, ki)))
    tab, tak = split_lanes2(top)
    qb, qk = split_lanes2(bot)
    tab, tak = jnp.where(strict, tab, 0.0), jnp.where(strict, tak, 0.0)
    qb, qk = jnp.where(incl, qb, 0.0), jnp.where(incl, qk, 0.0)
    tak_v, qk_v = split_rows2(nn_val(cat_rows2(tak, qk), v))
    x = cat_lanes2(ae, tak_v)
    npow = tab
    steps = max(1, (n - 1).bit_length())
    for i in range(steps):
        x = x + nn_solve(npow, x)
        if i + 1 < steps:
            npow = nn_solve(npow, npow)
    ae_m, uc = split_lanes2(x)
    qx = nn_out(qb, x)
    q_ae, q_uc = split_lanes2(qx)
    re_m = re + q_ae
    yc = q_uc + qk_v
    g_end = jnp.exp(jnp.sum(logw, axis=1, keepdims=True))
    bg, kg = bi * g_end, ki * g_end
    tm = tn_state(ae_m, bg)
    sc = tn_state(cat_rows2(uc, v), cat_rows2(bg, kg))
    per_chunk = _split_op(0, (h,) * sub, diff)
    re_m, yc, g_end, tm, sc = (per_chunk(t) for t in (re_m, yc, g_end, tm, sc))
    ys, s = [], s0
    for i in range(sub):
        ys.append(nt_out(re_m[i], s) + yc[i])
        s = s * g_end[i] + nn_state(s, tm[i]) + sc[i]
    return _cat_op(1, (n,) * sub, diff)(*ys), s


def _wkv_fwd(r, w, k, v, a, b, L):
    nc = L // WKV_BLOCK

    def body(r_ref, w_ref, k_ref, v_ref, a_ref, b_ref, y_ref, ck_ref, s_ref):
        c = pl.program_id(0)

        @pl.when(c == 0)
        def _():
            s_ref[...] = jnp.zeros(s_ref.shape, F32)

        s0 = s_ref[...]
        ck_ref[0] = s0
        y, s1 = _wkv_block(s0, r_ref[...], w_ref[...], k_ref[...], v_ref[...], a_ref[...], b_ref[...], False)
        y_ref[...] = y
        s_ref[...] = s1

    blk = pl.BlockSpec((N_HEAD, WKV_BLOCK, HEAD), lambda c: (0, c, 0))
    return _pcall(
        body, name="wkv_fwd", grid=(nc,), in_specs=[blk] * 6,
        out_specs=[blk, pl.BlockSpec((1, N_HEAD, HEAD, HEAD), lambda c: (c, 0, 0, 0))],
        out_shape=[jax.ShapeDtypeStruct((N_HEAD, L, HEAD), F32), jax.ShapeDtypeStruct((nc, N_HEAD, HEAD, HEAD), F32)],
        scratch_shapes=[pltpu.VMEM((N_HEAD, HEAD, HEAD), F32)],
        compiler_params=_cparams(1))(r, w, k, v, a, b)


def _wkv_bwd(r, w, k, v, a, b, dy, ck, L, deps=()):
    nc = L // WKV_BLOCK

    def body(r_ref, w_ref, k_ref, v_ref, a_ref, b_ref, dy_ref, ck_ref, *rest):
        dr_ref, dw_ref, dk_ref, dv_ref, da_ref, db_ref, ds_ref = rest[len(deps):]
        c = pl.program_id(0)

        @pl.when(c == 0)
        def _():
            ds_ref[...] = jnp.zeros(ds_ref.shape, F32)

        _, vjp = jax.vjp(lambda *t: _wkv_block(*t, True), ck_ref[0], r_ref[...], w_ref[...], k_ref[...], v_ref[...],
                         a_ref[...], b_ref[...])
        g = vjp((dy_ref[...], ds_ref[...]))
        ds_ref[...] = g[0]
        for o_ref, val in zip((dr_ref, dw_ref, dk_ref, dv_ref, da_ref, db_ref), g[1:]):
            o_ref[...] = val

    blk = pl.BlockSpec((N_HEAD, WKV_BLOCK, HEAD), lambda c: (0, nc - 1 - c, 0))
    sh = jax.ShapeDtypeStruct((N_HEAD, L, HEAD), F32)
    return _pcall(
        body, name="wkv_bwd", grid=(nc,),
        in_specs=[blk] * 7 + [pl.BlockSpec((1, N_HEAD, HEAD, HEAD), lambda c: (nc - 1 - c, 0, 0, 0))]
        + [pl.BlockSpec(d.shape, lambda c, nd=d.ndim: (0,) * nd) for d in deps],
        out_specs=[blk] * 6, out_shape=[sh] * 6,
        scratch_shapes=[pltpu.VMEM((N_HEAD, HEAD, HEAD), F32)],
        compiler_params=_cparams(1))(r, w, k, v, a, b, dy, ck, *deps)


TB = 256


def _bf(x):
    return x.astype(BF16)


def _inproj_fwd(x, norm_mix, w_in, L, deps=()):
    def fn(i, tv, cv):
        xn = _rms(tv[0], cv[0])
        return _dot(_bf(xn), cv[1]), xn

    return _tok_call("inproj_fwd", fn, L, TB, [(x, D_MODEL, 0)], [norm_mix, w_in], [(IN_COLS, F32), (D_MODEL, BF16)],
                     deps=deps)


def _s5_post_fn(glu_w, wtop, diff=True):
    mg = _mmc(glu_w, diff)
    mt = _mmc(wtop, diff) if wtop is not None else None

    def f(y, glu_b, e):
        z = _gelu(y)
        out = z * _sigmoid(mg(z) + glu_b + e)
        res = mt(out) if mt is not None else out
        return res, (z, out)

    return f


def _s5_post_fwd(y, glu_w, glu_b, L):
    def fn(i, tv, cv):
        out, _ = _s5_post_fn(cv[0], None, False)(tv[0], cv[1], 0.0)
        return (out,)

    return _tok_call("s5_post_fwd", fn, L, TB, [(y, S5_WIDTH, 0)], [glu_w, glu_b], [(S5_WIDTH, F32)])[0]


def _s5_post_bwd(y, dh1, glu_w, glu_b, wtop, L, deps=()):
    def fn(i, tv, cv):
        e0 = jnp.zeros((TB, S5_WIDTH), F32)
        _, vjp, (z, out) = jax.vjp(_s5_post_fn(cv[0], cv[2]), tv[0], cv[1], e0, has_aux=True)
        dy, db, de = vjp(tv[1])
        return dy, db, _dot_tn(_bf(z), _bf(de)), _dot_tn(_bf(out), _bf(tv[1]))

    return _tok_call("s5_post_bwd", fn, L, TB, [(y, S5_WIDTH, 0), (dh1, D_MODEL, 0)], [glu_w, glu_b, wtop],
                     [(S5_WIDTH, F32)], [(1, S5_WIDTH), (S5_WIDTH, S5_WIDTH), (S5_WIDTH, D_MODEL)], deps=deps)


RW_COLBLK = ((RW_WIDTH, 1), (RW_WIDTH, 2), (RW_WIDTH, 3), (128, 16), (128, 17))
RW_MU = ((0, 512), (512, 1024), (1024, 1536), (1536, 1664), (1664, 1792))


def _rw_pre_fn(w2pad, a2pad, g2, diff=True):
    m_w, m_a, m_g = _mmc(w2pad, diff), _mmc(a2pad, diff), _mmc(g2, diff)
    seg = _segsum(_head_indicator(RW_WIDTH), diff)

    def f(zr, zk, zv, zwa, zg, w0, a0, k_k, k_a, e_w, e_a):
        wl_t = jnp.tanh(zwa)
        wlin = w0 + m_w(wl_t) + e_w
        w = -_softplus(-wlin) - 0.5
        decay = jnp.exp(-jnp.exp(w))
        a = _sigmoid(a0 + m_a(zwa) + e_a)
        sg = _sigmoid(zg)
        g = m_g(sg)
        kk = zk * k_k
        kkn = kk / jnp.maximum(jnp.sqrt(seg(kk * kk)), L2_EPS)
        kf = zk * (1.0 + (a - 1.0) * k_a)
        return (zr, decay, kf, zv, -kkn, kkn * a, g), (wl_t, sg)

    return f


def _rw_shifted(i, tv, mu):
    sub = lax.broadcasted_iota(jnp.int32, (TB, 1), 0)
    zs, dif = [], []
    for n in range(5):
        z = tv[n]
        last = jnp.where(i == 0, 0.0, tv[5 + n][7:8, :])
        prev = jnp.where(sub == 0, last, pltpu.roll(z, 1, 0))
        m = mu[:, RW_MU[n][0]:RW_MU[n][1]]
        zs.append(z + (prev - z) * m)
        dif.append(prev - z)
    return zs, dif


def _rw_tok_in(proj):
    return [(proj, wd, cb) for wd, cb in RW_COLBLK] + [(proj, wd, cb, "prev") for wd, cb in RW_COLBLK]


def _rw_pre_fwd(proj, mu, w0, a0, k_k, k_a, w2pad, a2pad, g2, L):
    def fn(i, tv, cv):
        zs, _ = _rw_shifted(i, tv, cv[0])
        outs, _ = _rw_pre_fn(cv[5], cv[6], cv[7], False)(*zs, cv[1], cv[2], cv[3], cv[4], 0.0, 0.0)
        return outs

    return _tok_call("rw_pre_fwd", fn, L, TB, _rw_tok_in(proj), [mu, w0, a0, k_k, k_a, w2pad, a2pad, g2],
                     [("heads", F32)] * 6 + [(RW_WIDTH, F32)])


def _rw_pre_bwd(proj, cots, mu, w0, a0, k_k, k_a, w2pad, a2pad, g2, L):
    def fn(i, tv, cv):
        zs, dif = _rw_shifted(i, tv[:10], cv[0])
        dr1, dr2, dw, dk1, dk2, dv1, dv2, da, db, dg = tv[10:]
        e0 = jnp.zeros((TB, RW_WIDTH), F32)
        _, vjp, (wl_t, sg) = jax.vjp(_rw_pre_fn(cv[5], cv[6], cv[7]), *zs, cv[1], cv[2], cv[3], cv[4], e0, e0, has_aux=True)
        g = vjp((dr1 + dr2, dw, dk1 + dk2, dv1 + dv2, da, db, dg))
        dzs = jnp.concatenate(g[:5], axis=1)
        dmu = jnp.concatenate([jnp.sum(g[n] * dif[n], axis=0, keepdims=True) for n in range(5)], axis=1)
        lora = (_dot_tn(_bf(wl_t), _bf(g[9])), _dot_tn(_bf(zs[3]), _bf(g[10])), _dot_tn(_bf(sg), _bf(dg)))
        return (dzs, dmu, g[5], g[6], g[7], g[8]) + lora

    tok_in = _rw_tok_in(proj) + [((c,) if c.ndim == 3 else (c, RW_WIDTH, 0)) for c in cots]
    return _tok_call("rw_pre_bwd", fn, L, TB, tok_in, [mu, w0, a0, k_k, k_a, w2pad, a2pad, g2],
                     [(SHIFT_COLS, F32)], [(1, SHIFT_COLS)] + [(1, RW_WIDTH)] * 4 + [(128, RW_WIDTH)] * 3)


def _rw_post_fn(wbot, diff=True):
    seg = _segsum(_head_indicator(RW_WIDTH), diff)
    mb = _mmc(wbot, diff) if wbot is not None else None

    def f(y, r, kf, v, g, ln_w, ln_b, r_k):
        mean = seg(y) * (1.0 / HEAD)
        yc = y - mean
        var = seg(yc * yc) * (1.0 / HEAD)
        yn = yc * lax.rsqrt(var + GN_EPS) * ln_w + ln_b
        bonus = seg(r * kf * r_k) * v
        out = (yn + bonus) * g
        res = mb(out) if mb is not None else out
        return res, out

    return f


def _rw_post_fwd(y, r, kf, v, g, ln_w, ln_b, r_k, L):
    def fn(i, tv, cv):
        out, _ = _rw_post_fn(None, False)(*tv, *cv)
        return (out,)

    return _tok_call("rw_post_fwd", fn, L, TB, [(t,) for t in (y, r, kf, v)] + [(g, RW_WIDTH, 0)], [ln_w, ln_b, r_k],
                     [(RW_WIDTH, F32)])[0]


def _rw_post_bwd(y, r, kf, v, g, dh1, ln_w, ln_b, r_k, wbot, L):
    def fn(i, tv, cv):
        _, vjp, out = jax.vjp(_rw_post_fn(cv[3]), *tv[:5], cv[0], cv[1], cv[2], has_aux=True)
        gr = vjp(tv[5])
        return gr[0], gr[1], gr[2], gr[3], gr[4], gr[5], gr[6], gr[7], _dot_tn(_bf(out), _bf(tv[5]))

    return _tok_call("rw_post_bwd", fn, L, TB, [(t,) for t in (y, r, kf, v)] + [(g, RW_WIDTH, 0), (dh1, D_MODEL, 0)],
                     [ln_w, ln_b, r_k, wbot], [("heads", F32)] + [(RW_WIDTH, F32)] * 4,
                     [(1, RW_WIDTH)] * 3 + [(RW_WIDTH, D_MODEL)])


def _ffn_fn(w1, w3, w2, diff=True):
    m1, m3, m2 = _mmc(w1, diff), _mmc(w3, diff), _mmc(w2, diff)

    def f(h1, norm_ffn, e1, e3):
        hn = _rms(h1, norm_ffn)
        a1 = m1(hn) + e1
        a3 = m3(hn) + e3
        hm = a1 * _sigmoid(a1) * a3
        return h1 + m2(hm), (hn, hm)

    return f


TB_FFN = 256


def _mixffn_fwd(x, s5_out, rw_out, wtop, wbot, norm_ffn, w1, w3, w2, L):
    def fn(i, tv, cv):
        h1 = tv[0] + _dot(_bf(tv[1]), cv[0]) + _dot(_bf(tv[2]), cv[1])
        h2, _ = _ffn_fn(cv[3], cv[4], cv[5], False)(h1, cv[2], 0.0, 0.0)
        return h1, h2

    return _tok_call("mixffn_fwd", fn, L, TB_FFN, [(x, D_MODEL, 0), (s5_out, S5_WIDTH, 0), (rw_out, RW_WIDTH, 0)],
                     [wtop, wbot, norm_ffn, w1, w3, w2], [(D_MODEL, F32), (D_MODEL, F32)])


def _ffn_bwd(h1, dh2, norm_ffn, w1, w3, w2, L):
    def fn(i, tv, cv):
        e0 = jnp.zeros((TB_FFN, FFN_HIDDEN), F32)
        _, vjp, (hn, hm) = jax.vjp(_ffn_fn(cv[1], cv[2], cv[3]), tv[0], cv[0], e0, e0, has_aux=True)
        dh1, dn, d1, d3 = vjp(tv[1])
        return dh1, d1, d3, hm, hn, dn

    return _tok_call("ffn_bwd", fn, L, TB_FFN, [(h1, D_MODEL, 0), (dh2, D_MODEL, 0)], [norm_ffn, w1, w3, w2],
                     [(D_MODEL, F32), (FFN_HIDDEN, BF16), (FFN_HIDDEN, BF16), (FFN_HIDDEN, BF16), (D_MODEL, BF16)],
                     [(1, D_MODEL)])


def _ple_loss_fb(h2, p, target, norm_ple, final_norm, wg, wu, L):
    def fn(i, tv, cv):
        mgate, mup = _mmc(cv[2]), _mmc(cv[3], False)

        def f(h2_, norm_ple_, final_norm_, eg, eu):
            hn = _rms(h2_, norm_ple_)
            gate = _sigmoid(mgate(hn) + eg)
            h3 = h2_ + gate * (mup(tv[1]) + eu)
            out = _rms(h3, final_norm_)
            d = out - tv[2]
            return 0.5 * jnp.sum(jnp.mean(d * d, axis=-1, keepdims=True)), hn

        e0 = jnp.zeros((TB, D_MODEL), F32)
        loss, vjp, hn = jax.vjp(f, tv[0], cv[0], cv[1], e0, e0, has_aux=True)
        dh2, dnp, dfn, deg, deu = vjp(jnp.ones((), F32))
        return (dh2, dh2, jnp.full((8, 128), loss, F32), dnp, dfn,
                _dot_tn(_bf(hn), _bf(deg)), _dot_tn(_bf(tv[1]), _bf(deu)))

    return _tok_call("ple_loss_fb", fn, L, TB, [(h2, D_MODEL, 0), (p, PLE_DIM, 0), (target, D_MODEL, 0)],
                     [norm_ple, final_norm, wg, wu], [(D_MODEL, F32), (D_MODEL, BF16)],
                     [(8, 128), (1, D_MODEL), (1, D_MODEL), (D_MODEL, D_MODEL), (PLE_DIM, D_MODEL)])


def _inproj_bwd(x, dh1, du, dzs, norm_mix, mu, w_u, w_z, L):
    nb = L // TB

    def fn(i, tv, cv):
        sub = lax.broadcasted_iota(jnp.int32, (TB, 1), 0)
        m = cv[1]
        b = tv[3] * m
        nxt = jnp.where(i == nb - 1, 0.0, tv[4][0:1, :] * m)
        dz = tv[3] * (1.0 - m) + jnp.where(sub == TB - 1, nxt, pltpu.roll(b, TB - 1, 0))
        dub, dzb = _bf(tv[2]), _bf(dz)
        dxn = _dot_nt(dub, cv[2]) + _dot_nt(dzb, cv[3])
        _, vjp = jax.vjp(_rms, tv[0], cv[0])
        dx, dn = vjp(dxn)
        return tv[1] + dx, jnp.concatenate([dub, dzb], axis=1), dn

    return _tok_call("inproj_bwd", fn, L, TB,
                     [(x, D_MODEL, 0), (dh1, D_MODEL, 0), (du, S5_WIDTH, 0), (dzs, SHIFT_COLS, 0), (dzs, SHIFT_COLS, 0, "next")],
                     [norm_mix, mu, w_u, w_z], [(D_MODEL, F32), (IN_COLS, BF16)], [(1, D_MODEL)])


def _eye8(dt):
    return jnp.eye(8, dtype=dt)


def _quarter_b(bb):
    return jnp.einsum("hg,qgcp->qhcgp", _eye8(bb.dtype), bb.reshape(S5_Q, 8, S5_GROUP, S5_STATE)).reshape(S5_Q, S5_QL, S5_QS)


def _unquarter_b(d):
    return jnp.einsum("qhcgp,hg->qgcp", d.reshape(S5_Q, 8, S5_GROUP, 8, S5_STATE), _eye8(d.dtype)).reshape(
        S5_GROUPS, S5_GROUP, S5_STATE)


def _quarter_c(c):
    return jnp.einsum("gh,qgcp->qgphc", _eye8(c.dtype), c.reshape(S5_Q, 8, S5_GROUP, S5_STATE)).reshape(S5_Q, S5_QS, S5_QL)


def _unquarter_c(d):
    return jnp.einsum("qgphc,gh->qgcp", d.reshape(S5_Q, 8, S5_STATE, 8, S5_GROUP), _eye8(d.dtype)).reshape(
        S5_GROUPS, S5_GROUP, S5_STATE)


def _local_step(x, p, target, W, late_weights=None, grads_ready=None, first_dep=None):
    L = x.shape[0]
    r2 = lambda v: v.reshape(1, -1)
    w_in = W["w_in"]
    w2pad = jnp.pad(W["rw_w2"], ((0, 64), (0, 0)))
    a2pad = jnp.pad(W["rw_a2"], ((64, 0), (0, 0)))
    mu = r2(W["rw_shift_mu"])
    rw_vec = [r2(W[n]) for n in ("rw_w0", "rw_a0", "rw_k_k", "rw_k_a")]
    ln_w, ln_b, r_k = r2(W["rw_ln_w"]), r2(W["rw_ln_b"]), r2(W["rw_r_k"])

    lam_re, lam_im = W["s5_lam_re"], W["s5_lam_im"]
    log_step = W["s5_log_step"].reshape(S5_GROUPS, 1)
    bt_re, bt_im = W["s5_b_re"].transpose(0, 2, 1), W["s5_b_im"].transpose(0, 2, 1)
    lb_re, lb_im, bb_re, bb_im = _s5_param_fwd(lam_re, lam_im, log_step, bt_re, bt_im)
    bq_re, bq_im = _quarter_b(bb_re).astype(BF16), _quarter_b(bb_im).astype(BF16)
    cq_re, cq_im = _quarter_c(W["s5_c_re"]).astype(BF16), _quarter_c(W["s5_c_im"]).astype(BF16)
    lbar = jnp.concatenate([lb_re.reshape(1, -1), lb_im.reshape(1, -1), jnp.zeros((6, S5_LANES), F32)], axis=0)
    dskip = r2(W["s5_d"])
    glu_b = r2(W["s5_glu_b"])
    norm_mix, norm_ffn, norm_ple, final_norm = (r2(W[n]) for n in ("norm_mix", "norm_ffn", "norm_ple", "final_norm"))

    proj, xn = _inproj_fwd(x, norm_mix, w_in, L, () if first_dep is None else (first_dep,))
    y_s5, ck5 = _s5_scan_fwd(proj, bq_re, bq_im, cq_re, cq_im, lbar, dskip, L, TB)
    s5_out = _s5_post_fwd(y_s5, W["s5_glu_w"], glu_b, L)
    r, wd, kf, v, a_s, b_s, g = _rw_pre_fwd(proj, mu, *rw_vec, w2pad, a2pad, W["rw_g2"], L)
    scan_in = (r, wd, kf, v, a_s, b_s)
    y_wkv, ckw = _wkv_fwd(*scan_in, L)
    rw_out = _rw_post_fwd(y_wkv, r, kf, v, g, ln_w, ln_b, r_k, L)
    if late_weights is not None:
        W = dict(W, **late_weights(rw_out))
    wtop, wbot = W["w_out"][:S5_WIDTH], W["w_out"][S5_WIDTH:]
    h1, h2 = _mixffn_fwd(x, s5_out, rw_out, wtop, wbot, norm_ffn, W["ffn_w1"], W["ffn_w3"], W["ffn_w2"], L)

    G = {}
    dh2, dh2_bf, loss_acc, G["norm_ple"], G["final_norm"], G["ple_gate_w"], G["ple_up_w"] = _ple_loss_fb(
        h2, p, target, norm_ple, final_norm, W["ple_gate_w"], W["ple_up_w"], L)
    dh1, da1, da3, hm, hn_ffn, G["norm_ffn"] = _ffn_bwd(h1, dh2, norm_ffn, W["ffn_w1"], W["ffn_w3"], W["ffn_w2"], L)
    G["ffn_w1"] = _mm_tn("dw_ffn_w1", hn_ffn, da1)
    G["ffn_w3"] = _mm_tn("dw_ffn_w3", hn_ffn, da3)
    G["ffn_w2"] = _mm_tn("dw_ffn_w2", hm, dh2_bf)
    dep_a = grads_ready(0, G) if grads_ready is not None else None
    dy_s5, G["s5_glu_b"], G["s5_glu_w"], d_wtop = _s5_post_bwd(y_s5, dh1, W["s5_glu_w"], glu_b, wtop, L,
                                                               () if dep_a is None else (dep_a,))
    dy_wkv, dr2, dk2, dv2, dg, G["rw_ln_w"], G["rw_ln_b"], G["rw_r_k"], d_wbot = _rw_post_bwd(
        y_wkv, r, kf, v, g, dh1, ln_w, ln_b, r_k, wbot, L)
    G["w_out"] = jnp.concatenate([d_wtop, d_wbot], axis=0)
    dep = grads_ready(1, G) if grads_ready is not None else None
    dr1, dwd, dk1, dv1, da_s, db_s = _wkv_bwd(*scan_in, dy_wkv, ckw, L, () if dep is None else (dep,))
    (dzs, G["rw_shift_mu"], G["rw_w0"], G["rw_a0"], G["rw_k_k"], G["rw_k_a"], d_w2pad, d_a2pad, G["rw_g2"]) = _rw_pre_bwd(
        proj, (dr1, dr2, dwd, dk1, dk2, dv1, dv2, da_s, db_s, dg), mu, *rw_vec, w2pad, a2pad, W["rw_g2"], L)
    G["rw_w2"], G["rw_a2"] = d_w2pad[:64], d_a2pad[64:]
    du, dbq_re, dbq_im, dcq_re, dcq_im, dlbar, G["s5_d"] = _s5_scan_bwd(
        proj, dy_s5, ck5, bq_re, bq_im, cq_re, cq_im, lbar, dskip, L, TB)
    G["s5_c_re"], G["s5_c_im"] = _unquarter_c(dcq_re), _unquarter_c(dcq_im)
    d_lam_re, d_lam_im, d_ls, d_bt_re, d_bt_im = _s5_param_bwd(
        lam_re, lam_im, log_step, bt_re, bt_im, dlbar[0].reshape(S5_GROUPS, S5_STATE), dlbar[1].reshape(S5_GROUPS, S5_STATE),
        _unquarter_b(dbq_re), _unquarter_b(dbq_im))
    G["s5_lam_re"], G["s5_lam_im"], G["s5_log_step"] = d_lam_re, d_lam_im, d_ls.reshape(S5_GROUPS)
    G["s5_b_re"], G["s5_b_im"] = d_bt_re.transpose(0, 2, 1), d_bt_im.transpose(0, 2, 1)
    dx, dproj, G["norm_mix"] = _inproj_bwd(x, dh1, du, dzs, norm_mix, mu, w_in[:, :S5_WIDTH], w_in[:, S5_WIDTH:], L)
    G["w_in"] = _mm_tn("dw_in", xn, dproj)
    return loss_acc[0, 0], dx, G


MESH_AXES = ("x", "y", "c")


def _all_gather(name, shards):
    nt = len(shards)

    def body(*refs):
        x_refs, out_refs = refs[:nt], refs[nt:2 * nt]
        send_sems, recv_sems, local_sems = refs[2 * nt:]
        x, y, c = lax.axis_index("x"), lax.axis_index("y"), lax.axis_index("c")
        me, sibling = (x, y, c), (x, y, 1 - c)
        chips = [(1 - x, y), (x, 1 - y), (1 - x, 1 - y)]

        def rows(t, px, py, pc):
            m_per = shards[t].shape[0]
            return out_refs[t].at[pl.ds((4 * px + 2 * py + pc) * m_per, m_per), :]

        def copy(t, k, block, to, src=None):
            return pltpu.make_async_remote_copy(
                src_ref=rows(t, *block) if src is None else src, dst_ref=rows(t, *block),
                send_sem=send_sems.at[7 * t + k], recv_sem=recv_sems.at[7 * t + k],
                device_id=to, device_id_type=pl.DeviceIdType.MESH)

        mine = [pltpu.make_async_copy(x_refs[t], rows(t, *me), local_sems.at[t]) for t in range(nt)]
        for cp in mine:
            cp.start()
        first = []
        for t in range(nt):
            first.append(copy(t, 0, me, sibling, src=x_refs[t]))
            first += [copy(t, 1 + j, me, (*chip, c), src=x_refs[t]) for j, chip in enumerate(chips)]
        for cp in first:
            cp.start()
        passed = []
        for t in range(nt):
            for j, chip in enumerate(chips):
                copy(t, 1 + j, (*chip, c), me).wait_recv()
                fwd = copy(t, 4 + j, (*chip, c), sibling)
                fwd.start()
                passed.append(fwd)
        for t in range(nt):
            copy(t, 0, sibling, me).wait_recv()
            for j, chip in enumerate(chips):
                copy(t, 4 + j, (*chip, 1 - c), me).wait_recv()
        for cp in first + passed:
            cp.wait_send()
        for cp in mine:
            cp.wait()

    return _pcall(body, name=name,
                  out_shape=[jax.ShapeDtypeStruct((N_DEV * a.shape[0], a.shape[1]), a.dtype) for a in shards],
                  in_specs=[_ANY] * nt, out_specs=[_ANY] * nt,
                  scratch_shapes=[pltpu.SemaphoreType.DMA((7 * nt,)), pltpu.SemaphoreType.DMA((7 * nt,)),
                                  pltpu.SemaphoreType.DMA((nt,))])(*shards)


_HBM = pl.BlockSpec(memory_space=pltpu.HBM)
_SEM = pl.BlockSpec(memory_space=pltpu.SEMAPHORE)
_EFFECT = pltpu.SideEffectType.DATAFLOW_SIDE_EFFECTING


def _peer_of(k):
    x, y, c = lax.axis_index("x"), lax.axis_index("y"), lax.axis_index("c")
    px, py, pc = x ^ ((k >> 2) & 1), y ^ ((k >> 1) & 1), c ^ (k & 1)
    return (px, py, pc), 4 * px + 2 * py + pc, 4 * x + 2 * y + c


def _direct_copy(t, k, src_refs, land_refs, send_sems, recv_sems, rows_of, gather):
    dev, peer, me = _peer_of(k)
    m = rows_of[t]
    src = src_refs[t] if gather else src_refs[t].at[pl.ds(peer * m, m), :]
    return pltpu.make_async_remote_copy(
        src_ref=src, dst_ref=land_refs[t].at[pl.ds(me * m, m), :],
        send_sem=send_sems.at[7 * t + k - 1], recv_sem=recv_sems.at[7 * t + k - 1],
        device_id=dev, device_id_type=pl.DeviceIdType.MESH)


def _direct_landing(t, k, src_refs, land_refs, send_sems, recv_sems, rows_of, gather):
    dev, peer, me = _peer_of(k)
    m = rows_of[t]
    src = src_refs[t] if gather else src_refs[t].at[pl.ds(me * m, m), :]
    return pltpu.make_async_remote_copy(
        src_ref=src, dst_ref=land_refs[t].at[pl.ds(peer * m, m), :],
        send_sem=send_sems.at[7 * t + k - 1], recv_sem=recv_sems.at[7 * t + k - 1],
        device_id=dev, device_id_type=pl.DeviceIdType.MESH)


def _direct_start(name, srcs, gather, dep=None):
    nt = len(srcs)
    rows_of = [a.shape[0] if gather else a.shape[0] // N_DEV for a in srcs]
    lands = [pltpu.with_memory_space_constraint(lax.empty((N_DEV * m, a.shape[1]), a.dtype), pltpu.HBM)
             for a, m in zip(srcs, rows_of)]

    n_dep = 0 if dep is None else 1

    def body(*refs):
        src_refs, land_refs = refs[:nt], refs[nt:2 * nt]
        send_sems, recv_sems = refs[2 * nt + n_dep], refs[2 * nt + n_dep + 1]
        token = refs[-1]
        for t in range(nt):
            for k in range(1, N_DEV):
                _direct_copy(t, k, src_refs, land_refs, send_sems, recv_sems, rows_of, gather).start()
        token[...] = jnp.zeros(token.shape, F32)

    out = _pcall(
        body, name=name,
        out_shape=(pltpu.SemaphoreType.DMA((7 * nt,)), pltpu.SemaphoreType.DMA((7 * nt,)),
                   *[pltpu.HBM(a.shape, a.dtype) for a in srcs], *[pltpu.HBM(a.shape, a.dtype) for a in lands],
                   jax.ShapeDtypeStruct((8, 128), F32)),
        in_specs=(_HBM,) * (2 * nt) + (pl.BlockSpec(memory_space=pl.ANY),) * n_dep,
        out_specs=(_SEM, _SEM) + (_HBM,) * (2 * nt) + (pl.BlockSpec(memory_space=pltpu.VMEM),),
        input_output_aliases={i: 2 + i for i in range(2 * nt)},
        compiler_params=pltpu.CompilerParams(has_side_effects=_EFFECT),
    )(*[pltpu.with_memory_space_constraint(a, pltpu.HBM) for a in srcs], *lands, *(() if dep is None else (dep,)))
    return (out[0], out[1], list(out[2:2 + nt]), list(out[2 + nt:2 + 2 * nt]), rows_of, gather), out[-1]


def _direct_wait(name, handle, after):
    send_sems, recv_sems, srcs, lands, rows_of, gather = handle
    nt = len(srcs)
    after = list(after) if isinstance(after, (list, tuple)) else [after]

    def body(*refs):
        src_refs, land_refs = refs[:nt], refs[nt:2 * nt]
        s_sems, r_sems = refs[2 * nt], refs[2 * nt + 1]
        for t in range(nt):
            for k in range(1, N_DEV):
                _direct_copy(t, k, src_refs, land_refs, s_sems, r_sems, rows_of, gather).wait_send()
                _direct_landing(t, k, src_refs, land_refs, s_sems, r_sems, rows_of, gather).wait_recv()

    out = _pcall(
        body, name=name,
        out_shape=tuple(pltpu.HBM(a.shape, a.dtype) for a in srcs) + tuple(pltpu.HBM(a.shape, a.dtype) for a in lands),
        in_specs=(_HBM,) * (2 * nt) + (_SEM, _SEM) + (pl.BlockSpec(memory_space=pl.ANY),) * len(after),
        out_specs=(_HBM,) * (2 * nt),
        input_output_aliases={i: i for i in range(2 * nt)},
        compiler_params=pltpu.CompilerParams(has_side_effects=_EFFECT),
    )(*srcs, *lands, send_sems, recv_sems, *after)
    return list(out[:nt]), list(out[nt:])


def _adamw_sharded(name, own, parts, w, m, v, rb, deps=()):
    R, N = own.shape

    def body(o_ref, p_ref, w_ref, m_ref, v_ref, *rest):
        g_ref, d_ref, nm_ref, nv_ref = rest[len(deps):]
        me = 4 * lax.axis_index("x") + 2 * lax.axis_index("y") + lax.axis_index("c")
        g = o_ref[...]
        for k in range(1, N_DEV):
            g = g + p_ref[me ^ k].astype(F32)
        nm = ADAM_B1 * m_ref[...] + (1.0 - ADAM_B1) * g
        nv = ADAM_B2 * v_ref[...] + (1.0 - ADAM_B2) * (g * g)
        m_hat = nm / (1.0 - ADAM_B1 ** ADAM_STEP)
        v_hat = nv / (1.0 - ADAM_B2 ** ADAM_STEP)
        g_ref[...] = g
        d_ref[...] = -ADAM_LR * (m_hat / (jnp.sqrt(v_hat) + ADAM_EPS) + ADAM_WD * w_ref[...])
        nm_ref[...] = nm
        nv_ref[...] = nv

    blk = pl.BlockSpec((rb, N), lambda i: (i, 0))
    sh = jax.ShapeDtypeStruct((R, N), F32)
    return _pcall(body, name=name, grid=(R // rb,),
                  in_specs=[blk, pl.BlockSpec((N_DEV, rb, N), lambda i: (0, i, 0)), blk, blk, blk]
                  + [pl.BlockSpec(d.shape, lambda i, nd=d.ndim: (0,) * nd) for d in deps],
                  out_specs=[blk] * 4, out_shape=[sh] * 4, compiler_params=_cparams(1))(own, parts, w, m, v, *deps)


def _adamw(name, parts, w, m, v, rb):
    _, R, N = parts.shape

    def body(p_ref, w_ref, m_ref, v_ref, g_ref, d_ref, nm_ref, nv_ref):
        g = p_ref[0]
        for s in range(1, N_DEV):
            g = g + p_ref[s]
        nm = ADAM_B1 * m_ref[...] + (1.0 - ADAM_B1) * g
        nv = ADAM_B2 * v_ref[...] + (1.0 - ADAM_B2) * (g * g)
        m_hat = nm / (1.0 - ADAM_B1 ** ADAM_STEP)
        v_hat = nv / (1.0 - ADAM_B2 ** ADAM_STEP)
        g_ref[...] = g
        d_ref[...] = -ADAM_LR * (m_hat / (jnp.sqrt(v_hat) + ADAM_EPS) + ADAM_WD * w_ref[...])
        nm_ref[...] = nm
        nv_ref[...] = nv

    blk = pl.BlockSpec((rb, N), lambda i: (i, 0))
    sh = jax.ShapeDtypeStruct((R, N), F32)
    return _pcall(body, name=name, grid=(R // rb,), in_specs=[pl.BlockSpec((N_DEV, rb, N), lambda i: (0, i, 0)), blk, blk, blk],
                  out_specs=[blk] * 4, out_shape=[sh] * 4, compiler_params=_cparams(1))(parts, w, m, v)


EARLY = (("w_in", True),)
LATE = (("ffn_w1", True), ("ffn_w3", True), ("ffn_w2", False), ("ple_gate_w", False), ("w_out", False))
GRAD_STAGES = (LATE[:4], LATE[4:])
MISC = (("s5_glu_w", False), ("rw_w2", True), ("rw_a2", True), ("rw_g2", True), ("ple_up_w", True))
SHARDED_NAMES = tuple(n for n, _ in EARLY + LATE + MISC)
PACK_COLS = 1024
SMALL_ROWS = 144
WEIGHT_NAMES = ("norm_mix", "w_in", "s5_lam_re", "s5_lam_im", "s5_log_step", "s5_b_re", "s5_b_im", "s5_c_re", "s5_c_im", "s5_d",
                "s5_glu_w", "s5_glu_b", "rw_shift_mu", "rw_w0", "rw_w2", "rw_a0", "rw_a2", "rw_g2", "rw_k_k", "rw_k_a", "rw_r_k",
                "rw_ln_w", "rw_ln_b", "w_out", "norm_ffn", "ffn_w1", "ffn_w3", "ffn_w2", "norm_ple", "ple_gate_w", "ple_up_w",
                "final_norm")
SMALL_NAMES = tuple(n for n in WEIGHT_NAMES if n not in SHARDED_NAMES)
ARG_NAMES = ("x", "p") + WEIGHT_NAMES + ("loss_target",) + tuple("m_" + n for n in WEIGHT_NAMES) + tuple("v_" + n for n in WEIGHT_NAMES)


def _travel(a, tr):
    return a.T if tr else a


def _pack_misc(blocks):
    lead = blocks[0].shape[:-2]
    return jnp.concatenate([b.reshape(lead + (-1, PACK_COLS)) for b in blocks], axis=len(lead))


def _unpack_misc(packed, shapes):
    lead = packed.shape[:-2]
    out, off = [], 0
    for r, c in shapes:
        n = r * c // PACK_COLS
        out.append(lax.slice_in_dim(packed, off, off + n, axis=len(lead)).reshape(lead + (r, c)))
        off += n
    return out


def _pack_small(arrs):
    flat = jnp.concatenate([a.reshape(-1).astype(F32) for a in arrs])
    return jnp.pad(flat, (0, SMALL_ROWS * PACK_COLS - flat.shape[0])).reshape(SMALL_ROWS, PACK_COLS)


def _kernel_impl(ins):
    x, p, target = ins["x"][0], ins["p"][0, 0], ins["loss_target"][0]
    me = 4 * lax.axis_index("x") + 2 * lax.axis_index("y") + lax.axis_index("c")
    small = {n: (ins[n] if n == "final_norm" else ins[n][0]) for n in SMALL_NAMES}
    trav = lambda pre, n, tr: _travel(ins[pre + n][0], tr)
    misc_shapes = [trav("", n, tr).shape for n, tr in MISC]

    early = _all_gather("ag_early", [trav("", n, tr).astype(BF16) for n, tr in EARLY]
                        + [_pack_misc([trav("", n, tr).astype(BF16) for n, tr in MISC])])
    late_handle, late_token = _direct_start("ag_late_start", [trav("", n, tr).astype(BF16) for n, tr in LATE], True, early[-1])
    W = dict(small)
    for (n, tr), g in zip(EARLY, early):
        W[n] = _travel(g, tr)
    for (n, tr), g in zip(MISC, _unpack_misc(early[-1].reshape(N_DEV, -1, PACK_COLS), misc_shapes)):
        W[n] = _travel(g.reshape(-1, g.shape[-1]), tr)

    def late_weights(after):
        shards, lands = _direct_wait("ag_late_wait", late_handle, after)
        full = [lax.dynamic_update_slice_in_dim(ld, sh, me * sh.shape[0], axis=0) for ld, sh in zip(lands, shards)]
        return {n: _travel(g, tr) for (n, tr), g in zip(LATE, full)}

    gt = lambda G, n, tr: _travel(G[n], tr)
    started = {}

    def grads_ready(stage, G):
        full = [gt(G, n, tr) for n, tr in GRAD_STAGES[stage]]
        started[stage] = (full, *_direct_start("grad_late_start%d" % stage, [a.astype(BF16) for a in full], False))
        return started[stage][2]

    loss_part, dx, G = _local_step(x, p, target, W, late_weights, grads_ready, late_token)

    misc_g = _pack_misc([gt(G, n, tr).reshape((N_DEV,) + shp) for (n, tr), shp in zip(MISC, misc_shapes)])
    early_full = [gt(G, n, tr) for n, tr in EARLY] + [misc_g.reshape(-1, PACK_COLS)]
    early_handle, early_token = _direct_start("grad_early_start", [a.astype(BF16) for a in early_full], False)
    small_own = _pack_small([G[n] for n in SMALL_NAMES])
    small_handle, small_token = _direct_start("grad_small_start", [small_own], True)
    late_src, late_land = [], []
    for stage in range(len(GRAD_STAGES)):
        full, handle, _ = started[stage]
        _, land = _direct_wait("grad_late_wait%d" % stage, handle, small_token)
        late_src += full
        late_land += land

    outs = {}

    def emit(names_shapes, res):
        for tag, val in zip(("grad_", "delta_", "new_m_", "new_v_"), res):
            for n, v in names_shapes(val):
                outs[tag + n] = v

    def sharded_update(n, tr, src, land, deps=()):
        rows = src.shape[0] // N_DEV
        own = lax.dynamic_slice_in_dim(src, me * rows, rows, axis=0)
        res = _adamw_sharded("adamw_" + n, own, land.reshape(N_DEV, rows, land.shape[1]),
                             trav("", n, tr), trav("m_", n, tr), trav("v_", n, tr), _pick_rows(rows), deps)
        emit(lambda val: [(n, _travel(val, tr).reshape(ins[n].shape))], res)
        return list(res)

    for (n, tr), src, land in zip(LATE, late_src, late_land):
        sharded_update(n, tr, src, land, (early_token,))
    _, early_land = _direct_wait("grad_early_wait", early_handle, list(outs.values()))
    for (n, tr), src, land in zip(EARLY, early_full[:-1], early_land[:-1]):
        sharded_update(n, tr, src, land)
    pm = lambda pre: _pack_misc([trav(pre, n, tr) for n, tr in MISC])
    rows = early_full[-1].shape[0] // N_DEV
    res = _adamw_sharded("adamw_misc", lax.dynamic_slice_in_dim(early_full[-1], me * rows, rows, axis=0),
                         early_land[-1].reshape(N_DEV, rows, PACK_COLS), pm(""), pm("m_"), pm("v_"), rows)
    emit(lambda val: [(n, _travel(b, tr).reshape(ins[n].shape)) for (n, tr), b in zip(MISC, _unpack_misc(val, misc_shapes))], res)
    ps = lambda pre: _pack_small([ins[pre + n] for n in SMALL_NAMES])
    small_src, small_land = _direct_wait("grad_small_wait", small_handle, res[0])
    gsm = lax.dynamic_update_slice_in_dim(small_land[0], small_src[0], me * SMALL_ROWS, axis=0)
    res = _adamw("adamw_replicated", gsm.reshape(N_DEV, SMALL_ROWS, PACK_COLS), ps(""), ps("m_"), ps("v_"), SMALL_ROWS)

    def split_small(val):
        flat, off, o = val.reshape(-1), 0, []
        for n in SMALL_NAMES:
            o.append((n, flat[off:off + ins[n].size].reshape(ins[n].shape)))
            off += ins[n].size
        return o

    emit(split_small, res)
    loss = lax.psum(loss_part, MESH_AXES)
    res = [loss, dx[None]]
    for tag in ("grad_", "delta_", "new_m_", "new_v_"):
        res += [outs[tag + n] for n in WEIGHT_NAMES]
    return tuple(res)


def _pick_rows(r):
    best = 8
    for b in range(8, 257, 8):
        if r % b == 0:
            best = b
    return best


def kernel(x, p, norm_mix, w_in, s5_lam_re, s5_lam_im, s5_log_step, s5_b_re, s5_b_im, s5_c_re, s5_c_im, s5_d, s5_glu_w, s5_glu_b, rw_shift_mu, rw_w0, rw_w2, rw_a0, rw_a2, rw_g2, rw_k_k, rw_k_a, rw_r_k, rw_ln_w, rw_ln_b, w_out, norm_ffn, ffn_w1, ffn_w3, ffn_w2, norm_ple, ple_gate_w, ple_up_w, final_norm, loss_target, m_norm_mix, m_w_in, m_s5_lam_re, m_s5_lam_im, m_s5_log_step, m_s5_b_re, m_s5_b_im, m_s5_c_re, m_s5_c_im, m_s5_d, m_s5_glu_w, m_s5_glu_b, m_rw_shift_mu, m_rw_w0, m_rw_w2, m_rw_a0, m_rw_a2, m_rw_g2, m_rw_k_k, m_rw_k_a, m_rw_r_k, m_rw_ln_w, m_rw_ln_b, m_w_out, m_norm_ffn, m_ffn_w1, m_ffn_w3, m_ffn_w2, m_norm_ple, m_ple_gate_w, m_ple_up_w, m_final_norm, v_norm_mix, v_w_in, v_s5_lam_re, v_s5_lam_im, v_s5_log_step, v_s5_b_re, v_s5_b_im, v_s5_c_re, v_s5_c_im, v_s5_d, v_s5_glu_w, v_s5_glu_b, v_rw_shift_mu, v_rw_w0, v_rw_w2, v_rw_a0, v_rw_a2, v_rw_g2, v_rw_k_k, v_rw_k_a, v_rw_r_k, v_rw_ln_w, v_rw_ln_b, v_w_out, v_norm_ffn, v_ffn_w1, v_ffn_w3, v_ffn_w2, v_norm_ple, v_ple_gate_w, v_ple_up_w, v_final_norm):
    return _kernel_impl(dict(zip(ARG_NAMES, (x, p, norm_mix, w_in, s5_lam_re, s5_lam_im, s5_log_step, s5_b_re, s5_b_im, s5_c_re, s5_c_im, s5_d, s5_glu_w, s5_glu_b, rw_shift_mu, rw_w0, rw_w2, rw_a0, rw_a2, rw_g2, rw_k_k, rw_k_a, rw_r_k, rw_ln_w, rw_ln_b, w_out, norm_ffn, ffn_w1, ffn_w3, ffn_w2, norm_ple, ple_gate_w, ple_up_w, final_norm, loss_target, m_norm_mix, m_w_in, m_s5_lam_re, m_s5_lam_im, m_s5_log_step, m_s5_b_re, m_s5_b_im, m_s5_c_re, m_s5_c_im, m_s5_d, m_s5_glu_w, m_s5_glu_b, m_rw_shift_mu, m_rw_w0, m_rw_w2, m_rw_a0, m_rw_a2, m_rw_g2, m_rw_k_k, m_rw_k_a, m_rw_r_k, m_rw_ln_w, m_rw_ln_b, m_w_out, m_norm_ffn, m_ffn_w1, m_ffn_w3, m_ffn_w2, m_norm_ple, m_ple_gate_w, m_ple_up_w, m_final_norm, v_norm_mix, v_w_in, v_s5_lam_re, v_s5_lam_im, v_s5_log_step, v_s5_b_re, v_s5_b_im, v_s5_c_re, v_s5_c_im, v_s5_d, v_s5_glu_w, v_s5_glu_b, v_rw_shift_mu, v_rw_w0, v_rw_w2, v_rw_a0, v_rw_a2, v_rw_g2, v_rw_k_k, v_rw_k_a, v_rw_r_k, v_rw_ln_w, v_rw_ln_b, v_w_out, v_norm_ffn, v_ffn_w1, v_ffn_w3, v_ffn_w2, v_norm_ple, v_ple_gate_w, v_ple_up_w, v_final_norm))))
```

```python
import functools

import jax
import jax.numpy as jnp
from jax import lax
from jax.experimental import pallas as pl
from jax.experimental.pallas import tpu as pltpu

F32 = jnp.float32
BF16 = jnp.bfloat16

D_MODEL = 1024
S5_WIDTH = 512
RW_WIDTH = 512
S5_GROUP = 16
S5_GROUPS = 32
S5_STATE = 64
S5_LANES = S5_GROUPS * S5_STATE
HEAD = 64
SHIFT_COLS = 1792
IN_COLS = 2304
FFN_HIDDEN = 2816
PLE_DIM = 256
RMS_EPS = 1e-6
GN_EPS = 64e-5
L2_EPS = 1e-12
CHUNK = 64
N_DEV = 8

ADAM_LR = 0.001
ADAM_B1 = 0.9
ADAM_B2 = 0.999
ADAM_EPS = 1e-08
ADAM_WD = 0.01
ADAM_STEP = 10

VMEM_LIMIT = 56 * 1024 * 1024
_ANY = pl.BlockSpec(memory_space=pl.ANY)


def _pcall(body, **kw):
    return pl.pallas_call(body, **kw)


def _cparams(n_grid):
    return pltpu.CompilerParams(dimension_semantics=("arbitrary",) * n_grid, vmem_limit_bytes=VMEM_LIMIT)


def _dot(a, b):
    return jnp.dot(a, b, preferred_element_type=F32)


def _dot_nt(a, b):
    return lax.dot_general(a, b, (((1,), (1,)), ((), ())), preferred_element_type=F32)


def _dot_tn(a, b):
    return lax.dot_general(a, b, (((0,), (0,)), ((), ())), preferred_element_type=F32)


def _mmc(w, diff=True, tr=False):
    fw, bw = (_dot_nt, _dot) if tr else (_dot, _dot_nt)
    if not diff:
        return lambda x: fw(x.astype(BF16), w)

    @jax.custom_vjp
    def f(x):
        return fw(x.astype(BF16), w)

    def fwd(x):
        return fw(x.astype(BF16), w), None

    def bwd(_, dy):
        return (bw(dy.astype(BF16), w),)

    f.defvjp(fwd, bwd)
    return f


def _split_dot(x, m, n_split):
    acc = None
    rem = x
    for s in range(n_split):
        part = rem.astype(BF16)
        t = _dot(part, m)
        acc = t if acc is None else acc + t
        if s + 1 < n_split:
            rem = rem - part.astype(F32)
    return acc


def _segsum(m, diff=True):
    if not diff:
        return lambda x: _split_dot(x, m, 2)

    @jax.custom_vjp
    def f(x):
        return _split_dot(x, m, 2)

    def fwd(x):
        return _split_dot(x, m, 2), None

    def bwd(_, dy):
        return (_split_dot(dy, m, 2),)

    f.defvjp(fwd, bwd)
    return f


def _head_indicator(n):
    r = lax.broadcasted_iota(jnp.int32, (n, n), 0) // HEAD
    c = lax.broadcasted_iota(jnp.int32, (n, n), 1) // HEAD
    return (r == c).astype(BF16)


def _rms(x, g):
    return x * lax.rsqrt(jnp.mean(x * x, axis=-1, keepdims=True) + RMS_EPS) * g


def _softplus(x):
    return jnp.maximum(x, 0.0) + jnp.log(1.0 + jnp.exp(-jnp.abs(x)))


def _sigmoid(x):
    return 1.0 / (1.0 + jnp.exp(-x))


def _gelu(x):
    return 0.5 * x * (1.0 + jnp.tanh(0.7978845608028654 * (x + 0.044715 * (x * x * x))))


def _tok_call(name, fn, L, TB, tok_in, const_in, tok_out, acc_out=(), deps=()):
    nb = L // TB
    g8 = TB // 8
    in_specs, args = [], []
    for spec in tok_in:
        if len(spec) == 1:
            arr = spec[0]
            in_specs.append(pl.BlockSpec((arr.shape[0], TB, HEAD), lambda i: (0, i, 0)))
            args.append(arr)
            continue
        arr, width, cb = spec[:3]
        mode = spec[3] if len(spec) > 3 else None
        if mode is None:
            in_specs.append(pl.BlockSpec((TB, width), lambda i, cb=cb: (i, cb)))
        elif mode == "prev":
            in_specs.append(pl.BlockSpec((8, width), lambda i, cb=cb: (jnp.maximum(i * g8 - 1, 0), cb)))
        else:
            in_specs.append(pl.BlockSpec((8, width), lambda i, cb=cb: (jnp.minimum((i + 1) * g8, L // 8 - 1), cb)))
        args.append(arr)
    for c in const_in:
        in_specs.append(pl.BlockSpec(c.shape, lambda i, nd=c.ndim: (0,) * nd, pipeline_mode=pl.Buffered(1)))
        args.append(c)
    for d in deps:
        in_specs.append(pl.BlockSpec(d.shape, lambda i, nd=d.ndim: (0,) * nd))
        args.append(d)
    out_shape, out_specs = [], []
    for width, dt in tok_out:
        if width == "heads":
            out_shape.append(jax.ShapeDtypeStruct((N_HEAD, L, HEAD), dt))
            out_specs.append(pl.BlockSpec((N_HEAD, TB, HEAD), lambda i: (0, i, 0)))
            continue
        out_shape.append(jax.ShapeDtypeStruct((L, width), dt))
        out_specs.append(pl.BlockSpec((TB, width), lambda i: (i, 0)))
    for shp in acc_out:
        out_shape.append(jax.ShapeDtypeStruct(shp, F32))
        out_specs.append(pl.BlockSpec(shp, lambda i, nd=len(shp): (0,) * nd))
    n_tok, n_const, n_to = len(tok_in), len(const_in), len(tok_out)

    def body(*refs):
        i = pl.program_id(0)
        tv = [r[...] if len(r.shape) == 2 else jnp.concatenate([r[h] for h in range(r.shape[0])], axis=1)
              for r in refs[:n_tok]]
        cv = [r[...] for r in refs[n_tok:n_tok + n_const]]
        orefs = refs[n_tok + n_const + len(deps):]
        outs = fn(i, tv, cv)
        for r, v in zip(orefs[:n_to], outs[:n_to]):
            if len(r.shape) == 3:
                for h in range(r.shape[0]):
                    r[h] = v[:, h * HEAD:(h + 1) * HEAD].astype(r.dtype)
            else:
                r[...] = v.astype(r.dtype)
        for r, v in zip(orefs[n_to:], outs[n_to:]):
            @pl.when(i == 0)
            def _(r=r):
                r[...] = jnp.zeros(r.shape, r.dtype)

            r[...] += v

    res = _pcall(body, name=name, grid=(nb,), in_specs=in_specs, out_specs=out_specs, out_shape=out_shape,
                 compiler_params=_cparams(1))(*args)
    return res


def _pick_block(n, cap):
    best = None
    for b in range(128, min(n, cap) + 1, 128):
        if n % b == 0:
            best = b
    return best if best is not None else n


def _mm_tn(name, a, b):
    T, M = a.shape
    N = b.shape[1]
    bm, bn, bt = _pick_block(M, 1536), _pick_block(N, 1536), _pick_block(T, 512)

    def body(a_ref, b_ref, o_ref):
        t = pl.program_id(2)

        @pl.when(t == 0)
        def _():
            o_ref[...] = jnp.zeros(o_ref.shape, F32)

        o_ref[...] += _dot_tn(a_ref[...].astype(BF16), b_ref[...].astype(BF16))

    return _pcall(body, name=name, grid=(M // bm, N // bn, T // bt),
                  in_specs=[pl.BlockSpec((bt, bm), lambda m, n, t: (t, m)), pl.BlockSpec((bt, bn), lambda m, n, t: (t, n))],
                  out_specs=pl.BlockSpec((bm, bn), lambda m, n, t: (m, n)),
                  out_shape=jax.ShapeDtypeStruct((M, N), F32), compiler_params=_cparams(3))(a, b)


def _s5_param_fn(lam_re, lam_im, log_step, bt_re, bt_im):
    dt = jnp.exp(log_step)
    e = jnp.exp(lam_re * dt)
    lb_re = e * jnp.cos(lam_im * dt)
    lb_im = e * jnp.sin(lam_im * dt)
    den = lam_re * lam_re + lam_im * lam_im
    nr, ni = lb_re - 1.0, lb_im
    co_re = (nr * lam_re + ni * lam_im) / den
    co_im = (ni * lam_re - nr * lam_im) / den
    cr, ci = co_re[:, None, :], co_im[:, None, :]
    return lb_re, lb_im, cr * bt_re - ci * bt_im, cr * bt_im + ci * bt_re


def _s5_param_fwd(lam_re, lam_im, log_step, bt_re, bt_im):
    def body(a, b, c, d, e, o1, o2, o3, o4):
        r = _s5_param_fn(a[...], b[...], c[...], d[...], e[...])
        o1[...], o2[...], o3[...], o4[...] = r

    sh = jax.ShapeDtypeStruct
    return _pcall(body, name="s5_param_fwd",
                  out_shape=[sh(lam_re.shape, F32), sh(lam_re.shape, F32), sh(bt_re.shape, F32), sh(bt_re.shape, F32)])(
        lam_re, lam_im, log_step, bt_re, bt_im)


def _s5_param_bwd(lam_re, lam_im, log_step, bt_re, bt_im, d_lb_re, d_lb_im, d_bb_re, d_bb_im):
    def body(a, b, c, d, e, g1, g2, g3, g4, o1, o2, o3, o4, o5):
        _, vjp = jax.vjp(_s5_param_fn, a[...], b[...], c[...], d[...], e[...])
        r = vjp((g1[...], g2[...], g3[...], g4[...]))
        o1[...], o2[...], o3[...], o4[...], o5[...] = r

    sh = jax.ShapeDtypeStruct
    return _pcall(body, name="s5_param_bwd",
                  out_shape=[sh(lam_re.shape, F32), sh(lam_re.shape, F32), sh(log_step.shape, F32),
                             sh(bt_re.shape, F32), sh(bt_re.shape, F32)])(
        lam_re, lam_im, log_step, bt_re, bt_im, d_lb_re, d_lb_im, d_bb_re, d_bb_im)


def _cmul(ar, ai, br, bi):
    return ar * br - ai * bi, ar * bi + ai * br


def _scan_consts(lr, li, reverse):
    n = lr.shape[1]
    sub = lax.broadcasted_iota(jnp.int32, (8, n), 0)
    pows = [(lr, li)]
    for _ in range(7):
        pows.append(_cmul(pows[-1][0], pows[-1][1], lr, li))
    steps = []
    for s in (1, 2, 4):
        m = (sub < 8 - s) if reverse else (sub >= s)
        pr, pi = pows[s - 1]
        steps.append((s, jnp.where(m, jnp.broadcast_to(pr, (8, n)), 0.0), jnp.where(m, jnp.broadcast_to(pi, (8, n)), 0.0)))
    wr = jnp.zeros((8, n), F32)
    wi = jnp.zeros((8, n), F32)
    for r in range(8):
        e = (8 - r) if reverse else (r + 1)
        wr = jnp.where(sub == r, jnp.broadcast_to(pows[e - 1][0], (8, n)), wr)
        wi = jnp.where(sub == r, jnp.broadcast_to(pows[e - 1][1], (8, n)), wi)
    return steps, wr, wi


S5_Q = 4
S5_QL = S5_WIDTH // S5_Q
S5_QS = S5_LANES // S5_Q
S5_NT = S5_LANES // 128
S5_QT = S5_QS // 128


def _s5_power_table(lb_ref, pw_re, pw_im, seg):
    for j in range(S5_NT):
        lr = jnp.broadcast_to(lb_ref[0:1, j * 128:(j + 1) * 128], (8, 128))
        li = jnp.broadcast_to(lb_ref[1:2, j * 128:(j + 1) * 128], (8, 128))

        def step(i, c, lr=lr, li=li, j=j):
            pw_re[j, i] = c[0]
            pw_im[j, i] = c[1]
            return _cmul(c[0], c[1], lr, li)

        lax.fori_loop(0, seg, step, (lr, li))


def _seg_scan(sre, sim, carry, lb_ref, pw_re, pw_im, rows, reverse):
    seg = rows // 8
    sgn = -1.0 if reverse else 1.0
    sub = lax.broadcasted_iota(jnp.int32, (8, 128), 0)
    rows_at = lambda i: pl.ds(pl.multiple_of(i * 8, 8), 8)
    entering = {}
    half_tiles = S5_NT // 2
    for half in range(2):
        tiles = list(range(half * half_tiles, (half + 1) * half_tiles))
        lam8 = [(jnp.broadcast_to(lb_ref[0:1, j * 128:(j + 1) * 128], (8, 128)),
                 sgn * jnp.broadcast_to(lb_ref[1:2, j * 128:(j + 1) * 128], (8, 128))) for j in tiles]

        def p1(ii, c):
            i = (seg - 1 - ii) if reverse else ii
            out = []
            for n, j in enumerate(tiles):
                lr, li = lam8[n]
                cr, ci = c[2 * n], c[2 * n + 1]
                nr = lr * cr - li * ci + sre[j, rows_at(i), :]
                ni = lr * ci + li * cr + sim[j, rows_at(i), :]
                sre[j, rows_at(i), :] = nr
                sim[j, rows_at(i), :] = ni
                out += [nr, ni]
            return tuple(out)

        ends = lax.fori_loop(0, seg, p1, tuple(jnp.zeros((8, 128), F32) for _ in range(2 * len(tiles))))
        cs = []
        for n, j in enumerate(tiles):
            ls = slice(j * 128, (j + 1) * 128)
            steps, wr, wi = _scan_consts(pw_re[j, seg - 1][0:1, :], sgn * pw_im[j, seg - 1][0:1, :], reverse)
            tr, ti = ends[2 * n], ends[2 * n + 1]
            for sft, pr, pi in steps:
                sh = (8 - sft) if reverse else sft
                yr, yi = pltpu.roll(tr, sh, 0), pltpu.roll(ti, sh, 0)
                tr, ti = tr + pr * yr - pi * yi, ti + pr * yi + pi * yr
            cin_r, cin_i = carry[0:1, ls], carry[1:2, ls]
            tr, ti = tr + wr * cin_r - wi * cin_i, ti + wr * cin_i + wi * cin_r
            edge_out, edge_in, sh = (0, 7, 7) if reverse else (7, 0, 1)
            carry[0:1, ls] = tr[edge_out:edge_out + 1, :]
            carry[1:2, ls] = ti[edge_out:edge_out + 1, :]
            cr = jnp.where(sub == edge_in, jnp.broadcast_to(cin_r, (8, 128)), pltpu.roll(tr, sh, 0))
            ci = jnp.where(sub == edge_in, jnp.broadcast_to(cin_i, (8, 128)), pltpu.roll(ti, sh, 0))
            cs += [cr, ci]
            entering[j] = (cr, ci)

        def p2(i, _):
            k = (seg - 1 - i) if reverse else i
            for n, j in enumerate(tiles):
                pr, pi = pw_re[j, k], pw_im[j, k]
                cr, ci = cs[2 * n], cs[2 * n + 1]
                if reverse:
                    sre[j, rows_at(i), :] = sre[j, rows_at(i), :] + pr * cr + pi * ci
                    sim[j, rows_at(i), :] = sim[j, rows_at(i), :] + pr * ci - pi * cr
                else:
                    sre[j, rows_at(i), :] = sre[j, rows_at(i), :] + pr * cr - pi * ci
                    sim[j, rows_at(i), :] = sim[j, rows_at(i), :] + pr * ci + pi * cr
            return 0

        lax.fori_loop(0, seg, p2, 0, unroll=2)
    return entering


class _SegIO:
    def __init__(self, hbm, buf, sems, rows, width, col0=0):
        self.hbm, self.buf, self.sems, self.rows, self.seg, self.width, self.col0 = hbm, buf, sems, rows, rows // 8, width, col0

    def _copies(self, blk, slot, to_vmem):
        out = []
        for r in range(8):
            h = self.hbm.at[pl.ds(blk * self.rows + r * self.seg, self.seg), pl.ds(self.col0, self.width)]
            v = self.buf.at[slot, :, r, :]
            out.append(pltpu.make_async_copy(h, v, self.sems.at[slot, r]) if to_vmem
                       else pltpu.make_async_copy(v, h, self.sems.at[slot, r]))
        return out

    def start(self, blk, slot, to_vmem):
        for cp in self._copies(blk, slot, to_vmem):
            cp.start()

    def wait(self, blk, slot, to_vmem):
        for cp in self._copies(blk, slot, to_vmem):
            cp.wait()

    def value(self, slot):
        return self.buf[slot].reshape(self.rows, self.width)

    def store(self, slot, val):
        self.buf[slot] = val.reshape(self.seg, 8, self.width)


def _seg_pipeline(i, nb, blk_of, ins, outs, compute):
    slot = i % 2

    @pl.when(i == 0)
    def _():
        for io in ins:
            io.start(blk_of(0), 0, True)

    @pl.when(i + 1 < nb)
    def _():
        for io in ins:
            io.start(blk_of(i + 1), 1 - slot, True)

    for io in ins:
        io.wait(blk_of(i), slot, True)

    @pl.when(i >= 2)
    def _():
        for io in outs:
            io.wait(blk_of(i - 2), slot, False)

    compute(slot)
    for io in outs:
        io.start(blk_of(i), slot, False)

    @pl.when(i == nb - 1)
    def _():
        for io in outs:
            if nb >= 2:
                io.wait(blk_of(i - 1), 1 - slot, False)
            io.wait(blk_of(i), slot, False)


def _s5_scan_fwd(proj, bq_re, bq_im, cq_re, cq_im, lbar, dskip, L, TB):
    nb = L // TB
    seg = TB // 8

    def body(u_hbm, bre, bim, cre, cim, lb_ref, d_ref, y_hbm, ck_ref, sre, sim, carry, pw_re, pw_im,
             ubuf, ybuf, sem_u, sem_y):
        i = pl.program_id(0)
        u_io = _SegIO(u_hbm, ubuf, sem_u, TB, S5_WIDTH)
        y_io = _SegIO(y_hbm, ybuf, sem_y, TB, S5_WIDTH)

        @pl.when(i == 0)
        def _():
            carry[...] = jnp.zeros(carry.shape, F32)
            _s5_power_table(lb_ref, pw_re, pw_im, seg)

        ck_ref[0] = carry[...]

        def compute(slot):
            u = u_io.value(slot)
            ub = u.astype(BF16)
            for q in range(S5_Q):
                uq = ub[:, q * S5_QL:(q + 1) * S5_QL]
                vr, vi = _dot(uq, bre[q]), _dot(uq, bim[q])
                for jj in range(S5_QT):
                    sre[q * S5_QT + jj] = vr[:, jj * 128:(jj + 1) * 128]
                    sim[q * S5_QT + jj] = vi[:, jj * 128:(jj + 1) * 128]
            _seg_scan(sre, sim, carry, lb_ref, pw_re, pw_im, TB, False)
            ys = []
            for q in range(S5_Q):
                sl = slice(q * S5_QL, (q + 1) * S5_QL)
                sr = jnp.concatenate([sre[q * S5_QT + jj] for jj in range(S5_QT)], axis=1).astype(BF16)
                si = jnp.concatenate([sim[q * S5_QT + jj] for jj in range(S5_QT)], axis=1).astype(BF16)
                ys.append(_dot(sr, cre[q]) - _dot(si, cim[q]) + u[:, sl] * d_ref[:, sl])
            y_io.store(slot, jnp.concatenate(ys, axis=1))

        _seg_pipeline(i, nb, lambda st: st, [u_io], [y_io], compute)

    full = lambda a: pl.BlockSpec(a.shape, lambda i, nd=a.ndim: (0,) * nd)
    st = pltpu.VMEM((S5_NT, TB, 128), F32)
    pw = pltpu.VMEM((S5_NT, seg, 8, 128), F32)
    io = pltpu.VMEM((2, seg, 8, S5_WIDTH), F32)
    return _pcall(
        body, name="s5_scan_fwd", grid=(nb,),
        in_specs=[_ANY, full(bq_re), full(bq_im), full(cq_re), full(cq_im), full(lbar), full(dskip)],
        out_specs=[_ANY, pl.BlockSpec((1, 8, S5_LANES), lambda i: (i, 0, 0))],
        out_shape=[jax.ShapeDtypeStruct((L, S5_WIDTH), F32), jax.ShapeDtypeStruct((nb, 8, S5_LANES), F32)],
        scratch_shapes=[st, st, pltpu.VMEM((8, S5_LANES), F32), pw, pw, io, io,
                        pltpu.SemaphoreType.DMA((2, 8)), pltpu.SemaphoreType.DMA((2, 8))],
        compiler_params=_cparams(1))(proj, bq_re, bq_im, cq_re, cq_im, lbar, dskip)


def _s5_scan_bwd(proj, dy, ck, bq_re, bq_im, cq_re, cq_im, lbar, dskip, L, TB):
    nb = L // TB
    seg = TB // 8

    def body(u_hbm, dy_hbm, ck_ref, bre, bim, cre, cim, lb_ref, d_ref,
             du_hbm, dbre, dbim, dcre, dcim, dlb_ref, dd_ref, sre, sim, gre, gim, carry, gcarry, pw_re, pw_im,
             ubuf, dybuf, dubuf, sem_u, sem_dy, sem_du):
        i = pl.program_id(0)
        u_io = _SegIO(u_hbm, ubuf, sem_u, TB, S5_WIDTH)
        dy_io = _SegIO(dy_hbm, dybuf, sem_dy, TB, S5_WIDTH)
        du_io = _SegIO(du_hbm, dubuf, sem_du, TB, S5_WIDTH)

        @pl.when(i == 0)
        def _():
            gcarry[...] = jnp.zeros(gcarry.shape, F32)
            dbre[...] = jnp.zeros(dbre.shape, F32)
            dbim[...] = jnp.zeros(dbim.shape, F32)
            dcre[...] = jnp.zeros(dcre.shape, F32)
            dcim[...] = jnp.zeros(dcim.shape, F32)
            dlb_ref[...] = jnp.zeros(dlb_ref.shape, F32)
            dd_ref[...] = jnp.zeros(dd_ref.shape, F32)
            _s5_power_table(lb_ref, pw_re, pw_im, seg)

        def compute(slot):
            u = u_io.value(slot)
            dy_v = dy_io.value(slot)
            ub = u.astype(BF16)
            dyb = dy_v.astype(BF16)
            carry[...] = ck_ref[0]
            for q in range(S5_Q):
                uq = ub[:, q * S5_QL:(q + 1) * S5_QL]
                dq = dyb[:, q * S5_QL:(q + 1) * S5_QL]
                vr, vi = _dot(uq, bre[q]), _dot(uq, bim[q])
                hr, hi = _dot_nt(dq, cre[q]), -_dot_nt(dq, cim[q])
                for jj in range(S5_QT):
                    ls = slice(jj * 128, (jj + 1) * 128)
                    sre[q * S5_QT + jj] = vr[:, ls]
                    sim[q * S5_QT + jj] = vi[:, ls]
                    gre[q * S5_QT + jj] = hr[:, ls]
                    gim[q * S5_QT + jj] = hi[:, ls]
            entering = _seg_scan(sre, sim, carry, lb_ref, pw_re, pw_im, TB, False)
            _seg_scan(gre, gim, gcarry, lb_ref, pw_re, pw_im, TB, True)

            rows_at = lambda k: pl.ds(pl.multiple_of(k * 8, 8), 8)
            for j in range(S5_NT):
                er, ei = entering[j]
                gr0, gi0 = gre[j, rows_at(0), :], gim[j, rows_at(0), :]
                acc0 = (gr0 * er + gi0 * ei, gi0 * er - gr0 * ei)

                def acc_step(k, acc, j=j):
                    gr, gi_ = gre[j, rows_at(k), :], gim[j, rows_at(k), :]
                    spr, spi = sre[j, rows_at(k - 1), :], sim[j, rows_at(k - 1), :]
                    return acc[0] + gr * spr + gi_ * spi, acc[1] - gr * spi + gi_ * spr

                ar, ai = lax.fori_loop(1, seg, acc_step, acc0, unroll=2 if (seg - 1) % 2 == 0 else 1)
                ls = slice(j * 128, (j + 1) * 128)
                dlb_ref[0:1, ls] += jnp.sum(ar, axis=0, keepdims=True)
                dlb_ref[1:2, ls] += jnp.sum(ai, axis=0, keepdims=True)

            dd_ref[...] += jnp.sum(dy_v * u, axis=0, keepdims=True)
            dus = []
            for q in range(S5_Q):
                sl = slice(q * S5_QL, (q + 1) * S5_QL)
                cat = lambda ref: jnp.concatenate([ref[q * S5_QT + jj] for jj in range(S5_QT)], axis=1).astype(BF16)
                grq, giq = cat(gre), cat(gim)
                dus.append(_dot_nt(grq, bre[q]) + _dot_nt(giq, bim[q]) + dy_v[:, sl] * d_ref[:, sl])
                dbre[q] += _dot_tn(ub[:, sl], grq)
                dbim[q] += _dot_tn(ub[:, sl], giq)
                dcre[q] += _dot_tn(cat(sre), dyb[:, sl])
                dcim[q] -= _dot_tn(cat(sim), dyb[:, sl])
            du_io.store(slot, jnp.concatenate(dus, axis=1))

        _seg_pipeline(i, nb, lambda st: nb - 1 - st, [u_io, dy_io], [du_io], compute)

    full = lambda a: pl.BlockSpec(a.shape, lambda i, nd=a.ndim: (0,) * nd)
    sh = jax.ShapeDtypeStruct
    outs = [sh((L, S5_WIDTH), F32), sh(bq_re.shape, F32), sh(bq_im.shape, F32), sh(cq_re.shape, F32), sh(cq_im.shape, F32),
            sh((8, S5_LANES), F32), sh((1, S5_WIDTH), F32)]
    fo = lambda s: pl.BlockSpec(s.shape, lambda i, nd=len(s.shape): (0,) * nd)
    st = pltpu.VMEM((S5_NT, TB, 128), F32)
    pw = pltpu.VMEM((S5_NT, seg, 8, 128), F32)
    io = pltpu.VMEM((2, seg, 8, S5_WIDTH), F32)
    sem = pltpu.SemaphoreType.DMA((2, 8))
    return _pcall(
        body, name="s5_scan_bwd", grid=(nb,),
        in_specs=[_ANY, _ANY, pl.BlockSpec((1, 8, S5_LANES), lambda i: (nb - 1 - i, 0, 0)),
                  full(bq_re), full(bq_im), full(cq_re), full(cq_im), full(lbar), full(dskip)],
        out_specs=[_ANY] + [fo(s) for s in outs[1:]],
        out_shape=outs,
        scratch_shapes=[st] * 4 + [pltpu.VMEM((8, S5_LANES), F32)] * 2 + [pw, pw, io, io, io, sem, sem, sem],
        compiler_params=_cparams(1))(proj, dy, ck, bq_re, bq_im, cq_re, cq_im, lbar, dskip)


N_HEAD = RW_WIDTH // HEAD
_NN = (((2,), (1,)), ((0,), (0,)))
_NT = (((2,), (2,)), ((0,), (0,)))
_TN = (((1,), (1,)), ((0,), (0,)))


def _hi_lo(x):
    h = x.astype(BF16)
    return h, (x - h.astype(F32)).astype(BF16)


def _mm_acc(a, b, dims, passes=3):
    dg = lambda p, q: lax.dot_general(p, q, dims, preferred_element_type=F32)
    if passes == 1:
        return dg(a.astype(BF16), b.astype(BF16))
    ah, al = _hi_lo(a)
    bh, bl = _hi_lo(b)
    return dg(ah, bh) + dg(ah, bl) + dg(al, bh)


def _cumsum_rows(x, transpose):
    h, n, _ = x.shape
    ti = lax.broadcasted_iota(jnp.int32, (h, n, n), 1)
    tj = lax.broadcasted_iota(jnp.int32, (h, n, n), 2)
    m = ((tj >= ti) if transpose else (tj <= ti)).astype(BF16)
    acc, rem = None, x
    for s in range(3):
        part = rem.astype(BF16)
        t = lax.dot_general(m, part, _NN, preferred_element_type=F32)
        acc = t if acc is None else acc + t
        if s < 2:
            rem = rem - part.astype(F32)
    return acc


def _slices(x, axis, sizes):
    out, off = [], 0
    for n in sizes:
        out.append(lax.slice_in_dim(x, off, off + n, axis=axis))
        off += n
    return tuple(out)


def _cat_op(axis, sizes, diff):
    plain = lambda *xs: jnp.concatenate(xs, axis=axis)
    if not diff:
        return plain
    f = jax.custom_vjp(plain)
    f.defvjp(lambda *xs: (plain(*xs), None), lambda _, d: _slices(d, axis, sizes))
    return f


def _split_op(axis, sizes, diff):
    plain = lambda x: _slices(x, axis, sizes)
    if not diff:
        return plain
    f = jax.custom_vjp(plain)
    f.defvjp(lambda x: (plain(x), None), lambda _, d: (jnp.concatenate(d, axis=axis),))
    return f


def _mm_ops(diff, passes):
    mm = lambda a, b, dims: _mm_acc(a, b, dims, passes)
    if not diff:
        return (lambda a, b: mm(a, b, _NN), lambda a, b: mm(a, b, _NT), lambda a, b: mm(a, b, _TN))

    @jax.custom_vjp
    def nn(a, b):
        return mm(a, b, _NN)

    nn.defvjp(lambda a, b: (mm(a, b, _NN), (a, b)), lambda r, d: (mm(d, r[1], _NT), mm(r[0], d, _TN)))

    @jax.custom_vjp
    def nt(a, b):
        return mm(a, b, _NT)

    nt.defvjp(lambda a, b: (mm(a, b, _NT), (a, b)), lambda r, d: (mm(d, r[1], _NN), mm(d, r[0], _TN)))

    @jax.custom_vjp
    def tn(a, b):
        return mm(a, b, _TN)

    tn.defvjp(lambda a, b: (mm(a, b, _TN), (a, b)), lambda r, d: (mm(r[1], d, _NT), mm(r[0], d, _NN)))
    return nn, nt, tn


def _cums_op(diff):
    if not diff:
        return lambda x: _cumsum_rows(x, False)

    @jax.custom_vjp
    def cums(x):
        return _cumsum_rows(x, False)

    cums.defvjp(lambda x: (_cumsum_rows(x, False), None), lambda _, d: (_cumsum_rows(d, True),))
    return cums


WKV_PASSES = (1, 1, 1, 1, 1)


WKV_SUB = 4
WKV_BLOCK = CHUNK * WKV_SUB


def _wkv_block(s0, r, w, k, v, a, b, diff):
    p_pair, p_val, p_solve, p_out, p_state = WKV_PASSES
    cums = _cums_op(diff)
    _, nt_pair, _ = _mm_ops(diff, p_pair)
    nn_val, _, _ = _mm_ops(diff, p_val)
    nn_solve, _, _ = _mm_ops(diff, p_solve)
    nn_out, nt_out, _ = _mm_ops(diff, p_out)
    nn_state, _, tn_state = _mm_ops(diff, p_state)
    h, d, n, sub = s0.shape[0], s0.shape[2], CHUNK, WKV_SUB
    hb = h * sub
    to_chunks = lambda t: _cat_op(0, (h,) * sub, diff)(*_split_op(1, (n,) * sub, diff)(t))
    r, w, k, v, a, b = (to_chunks(t) for t in (r, w, k, v, a, b))
    cat_rows2 = _cat_op(1, (n, n), diff)
    cat_lanes2 = _cat_op(2, (n, n), diff)
    split_rows2 = _split_op(1, (n, n), diff)
    split_lanes2 = _split_op(2, (n, n), diff)
    ti = lax.broadcasted_iota(jnp.int32, (hb, n, n), 1)
    tj = lax.broadcasted_iota(jnp.int32, (hb, n, n), 2)
    incl, strict = tj <= ti, tj < ti
    logw = jnp.log(w)
    cum = cums(logw)
    g_in, g_ex, g_inv = jnp.exp(cum), jnp.exp(cum - logw), jnp.exp(-cum)
    ae, re, bi, ki = a * g_ex, r * g_in, b * g_inv, k * g_inv
    top, bot = split_rows2(nt_pair(cat_rows2(ae, re), cat_rows2(bi, ki)))
    tab, tak = split_lanes2(top)
    qb, qk = split_lanes2(bot)
    tab, tak = jnp.where(strict, tab, 0.0), jnp.where(strict, tak, 0.0)
    qb, qk = jnp.where(incl, qb, 0.0), jnp.where(incl, qk, 0.0)
    tak_v, qk_v = split_rows2(nn_val(cat_rows2(tak, qk), v))
    x = cat_lanes2(ae, tak_v)
    npow = tab
    steps = max(1, (n - 1).bit_length())
    for i in range(steps):
        x = x + nn_solve(npow, x)
        if i + 1 < steps:
            npow = nn_solve(npow, npow)
    ae_m, uc = split_lanes2(x)
    qx = nn_out(qb, x)
    q_ae, q_uc = split_lanes2(qx)
    re_m = re + q_ae
    yc = q_uc + qk_v
    g_end = jnp.exp(jnp.sum(logw, axis=1, keepdims=True))
    bg, kg = bi * g_end, ki * g_end
    tm = tn_state(ae_m, bg)
    sc = tn_state(cat_rows2(uc, v), cat_rows2(bg, kg))
    per_chunk = _split_op(0, (h,) * sub, diff)
    re_m, yc, g_end, tm, sc = (per_chunk(t) for t in (re_m, yc, g_end, tm, sc))
    ys, s = [], s0
    for i in range(sub):
        ys.append(nt_out(re_m[i], s) + yc[i])
        s = s * g_end[i] + nn_state(s, tm[i]) + sc[i]
    return _cat_op(1, (n,) * sub, diff)(*ys), s


def _wkv_fwd(r, w, k, v, a, b, L):
    nc = L // WKV_BLOCK

    def body(r_ref, w_ref, k_ref, v_ref, a_ref, b_ref, y_ref, ck_ref, s_ref):
        c = pl.program_id(0)

        @pl.when(c == 0)
        def _():
            s_ref[...] = jnp.zeros(s_ref.shape, F32)

        s0 = s_ref[...]
        ck_ref[0] = s0
        y, s1 = _wkv_block(s0, r_ref[...], w_ref[...], k_ref[...], v_ref[...], a_ref[...], b_ref[...], False)
        y_ref[...] = y
        s_ref[...] = s1

    blk = pl.BlockSpec((N_HEAD, WKV_BLOCK, HEAD), lambda c: (0, c, 0))
    return _pcall(
        body, name="wkv_fwd", grid=(nc,), in_specs=[blk] * 6,
        out_specs=[blk, pl.BlockSpec((1, N_HEAD, HEAD, HEAD), lambda c: (c, 0, 0, 0))],
        out_shape=[jax.ShapeDtypeStruct((N_HEAD, L, HEAD), F32), jax.ShapeDtypeStruct((nc, N_HEAD, HEAD, HEAD), F32)],
        scratch_shapes=[pltpu.VMEM((N_HEAD, HEAD, HEAD), F32)],
        compiler_params=_cparams(1))(r, w, k, v, a, b)


def _wkv_bwd(r, w, k, v, a, b, dy, ck, L, deps=()):
    nc = L // WKV_BLOCK

    def body(r_ref, w_ref, k_ref, v_ref, a_ref, b_ref, dy_ref, ck_ref, *rest):
        dr_ref, dw_ref, dk_ref, dv_ref, da_ref, db_ref, ds_ref = rest[len(deps):]
        c = pl.program_id(0)

        @pl.when(c == 0)
        def _():
            ds_ref[...] = jnp.zeros(ds_ref.shape, F32)

        _, vjp = jax.vjp(lambda *t: _wkv_block(*t, True), ck_ref[0], r_ref[...], w_ref[...], k_ref[...], v_ref[...],
                         a_ref[...], b_ref[...])
        g = vjp((dy_ref[...], ds_ref[...]))
        ds_ref[...] = g[0]
        for o_ref, val in zip((dr_ref, dw_ref, dk_ref, dv_ref, da_ref, db_ref), g[1:]):
            o_ref[...] = val

    blk = pl.BlockSpec((N_HEAD, WKV_BLOCK, HEAD), lambda c: (0, nc - 1 - c, 0))
    sh = jax.ShapeDtypeStruct((N_HEAD, L, HEAD), F32)
    return _pcall(
        body, name="wkv_bwd", grid=(nc,),
        in_specs=[blk] * 7 + [pl.BlockSpec((1, N_HEAD, HEAD, HEAD), lambda c: (nc - 1 - c, 0, 0, 0))]
        + [pl.BlockSpec(d.shape, lambda c, nd=d.ndim: (0,) * nd) for d in deps],
        out_specs=[blk] * 6, out_shape=[sh] * 6,
        scratch_shapes=[pltpu.VMEM((N_HEAD, HEAD, HEAD), F32)],
        compiler_params=_cparams(1))(r, w, k, v, a, b, dy, ck, *deps)


TB = 256


def _bf(x):
    return x.astype(BF16)


def _inproj_fwd(x, norm_mix, w_in, L, deps=()):
    def fn(i, tv, cv):
        xn = _rms(tv[0], cv[0])
        return _dot(_bf(xn), cv[1]), xn

    return _tok_call("inproj_fwd", fn, L, TB, [(x, D_MODEL, 0)], [norm_mix, w_in], [(IN_COLS, F32), (D_MODEL, BF16)],
                     deps=deps)


def _s5_post_fn(glu_w, wtop, diff=True):
    mg = _mmc(glu_w, diff)
    mt = _mmc(wtop, diff) if wtop is not None else None

    def f(y, glu_b, e):
        z = _gelu(y)
        out = z * _sigmoid(mg(z) + glu_b + e)
        res = mt(out) if mt is not None else out
        return res, (z, out)

    return f


def _s5_post_fwd(y, glu_w, glu_b, L):
    def fn(i, tv, cv):
        out, _ = _s5_post_fn(cv[0], None, False)(tv[0], cv[1], 0.0)
        return (out,)

    return _tok_call("s5_post_fwd", fn, L, TB, [(y, S5_WIDTH, 0)], [glu_w, glu_b], [(S5_WIDTH, F32)])[0]


def _s5_post_bwd(y, dh1, glu_w, glu_b, wtop, L, deps=()):
    def fn(i, tv, cv):
        e0 = jnp.zeros((TB, S5_WIDTH), F32)
        _, vjp, (z, out) = jax.vjp(_s5_post_fn(cv[0], cv[2]), tv[0], cv[1], e0, has_aux=True)
        dy, db, de = vjp(tv[1])
        return dy, db, _dot_tn(_bf(z), _bf(de)), _dot_tn(_bf(out), _bf(tv[1]))

    return _tok_call("s5_post_bwd", fn, L, TB, [(y, S5_WIDTH, 0), (dh1, D_MODEL, 0)], [glu_w, glu_b, wtop],
                     [(S5_WIDTH, F32)], [(1, S5_WIDTH), (S5_WIDTH, S5_WIDTH), (S5_WIDTH, D_MODEL)], deps=deps)


RW_COLBLK = ((RW_WIDTH, 1), (RW_WIDTH, 2), (RW_WIDTH, 3), (128, 16), (128, 17))
RW_MU = ((0, 512), (512, 1024), (1024, 1536), (1536, 1664), (1664, 1792))


def _rw_pre_fn(w2pad, a2pad, g2, diff=True):
    m_w, m_a, m_g = _mmc(w2pad, diff), _mmc(a2pad, diff), _mmc(g2, diff)
    seg = _segsum(_head_indicator(RW_WIDTH), diff)

    def f(zr, zk, zv, zwa, zg, w0, a0, k_k, k_a, e_w, e_a):
        wl_t = jnp.tanh(zwa)
        wlin = w0 + m_w(wl_t) + e_w
        w = -_softplus(-wlin) - 0.5
        decay = jnp.exp(-jnp.exp(w))
        a = _sigmoid(a0 + m_a(zwa) + e_a)
        sg = _sigmoid(zg)
        g = m_g(sg)
        kk = zk * k_k
        kkn = kk / jnp.maximum(jnp.sqrt(seg(kk * kk)), L2_EPS)
        kf = zk * (1.0 + (a - 1.0) * k_a)
        return (zr, decay, kf, zv, -kkn, kkn * a, g), (wl_t, sg)

    return f


def _rw_shifted(i, tv, mu):
    sub = lax.broadcasted_iota(jnp.int32, (TB, 1), 0)
    zs, dif = [], []
    for n in range(5):
        z = tv[n]
        last = jnp.where(i == 0, 0.0, tv[5 + n][7:8, :])
        prev = jnp.where(sub == 0, last, pltpu.roll(z, 1, 0))
        m = mu[:, RW_MU[n][0]:RW_MU[n][1]]
        zs.append(z + (prev - z) * m)
        dif.append(prev - z)
    return zs, dif


def _rw_tok_in(proj):
    return [(proj, wd, cb) for wd, cb in RW_COLBLK] + [(proj, wd, cb, "prev") for wd, cb in RW_COLBLK]


def _rw_pre_fwd(proj, mu, w0, a0, k_k, k_a, w2pad, a2pad, g2, L):
    def fn(i, tv, cv):
        zs, _ = _rw_shifted(i, tv, cv[0])
        outs, _ = _rw_pre_fn(cv[5], cv[6], cv[7], False)(*zs, cv[1], cv[2], cv[3], cv[4], 0.0, 0.0)
        return outs

    return _tok_call("rw_pre_fwd", fn, L, TB, _rw_tok_in(proj), [mu, w0, a0, k_k, k_a, w2pad, a2pad, g2],
                     [("heads", F32)] * 6 + [(RW_WIDTH, F32)])


def _rw_pre_bwd(proj, cots, mu, w0, a0, k_k, k_a, w2pad, a2pad, g2, L):
    def fn(i, tv, cv):
        zs, dif = _rw_shifted(i, tv[:10], cv[0])
        dr1, dr2, dw, dk1, dk2, dv1, dv2, da, db, dg = tv[10:]
        e0 = jnp.zeros((TB, RW_WIDTH), F32)
        _, vjp, (wl_t, sg) = jax.vjp(_rw_pre_fn(cv[5], cv[6], cv[7]), *zs, cv[1], cv[2], cv[3], cv[4], e0, e0, has_aux=True)
        g = vjp((dr1 + dr2, dw, dk1 + dk2, dv1 + dv2, da, db, dg))
        dzs = jnp.concatenate(g[:5], axis=1)
        dmu = jnp.concatenate([jnp.sum(g[n] * dif[n], axis=0, keepdims=True) for n in range(5)], axis=1)
        lora = (_dot_tn(_bf(wl_t), _bf(g[9])), _dot_tn(_bf(zs[3]), _bf(g[10])), _dot_tn(_bf(sg), _bf(dg)))
        return (dzs, dmu, g[5], g[6], g[7], g[8]) + lora

    tok_in = _rw_tok_in(proj) + [((c,) if c.ndim == 3 else (c, RW_WIDTH, 0)) for c in cots]
    return _tok_call("rw_pre_bwd", fn, L, TB, tok_in, [mu, w0, a0, k_k, k_a, w2pad, a2pad, g2],
                     [(SHIFT_COLS, F32)], [(1, SHIFT_COLS)] + [(1, RW_WIDTH)] * 4 + [(128, RW_WIDTH)] * 3)


def _rw_post_fn(wbot, diff=True):
    seg = _segsum(_head_indicator(RW_WIDTH), diff)
    mb = _mmc(wbot, diff) if wbot is not None else None

    def f(y, r, kf, v, g, ln_w, ln_b, r_k):
        mean = seg(y) * (1.0 / HEAD)
        yc = y - mean
        var = seg(yc * yc) * (1.0 / HEAD)
        yn = yc * lax.rsqrt(var + GN_EPS) * ln_w + ln_b
        bonus = seg(r * kf * r_k) * v
        out = (yn + bonus) * g
        res = mb(out) if mb is not None else out
        return res, out

    return f


def _rw_post_fwd(y, r, kf, v, g, ln_w, ln_b, r_k, L):
    def fn(i, tv, cv):
        out, _ = _rw_post_fn(None, False)(*tv, *cv)
        return (out,)

    return _tok_call("rw_post_fwd", fn, L, TB, [(t,) for t in (y, r, kf, v)] + [(g, RW_WIDTH, 0)], [ln_w, ln_b, r_k],
                     [(RW_WIDTH, F32)])[0]


def _rw_post_bwd(y, r, kf, v, g, dh1, ln_w, ln_b, r_k, wbot, L):
    def fn(i, tv, cv):
        _, vjp, out = jax.vjp(_rw_post_fn(cv[3]), *tv[:5], cv[0], cv[1], cv[2], has_aux=True)
        gr = vjp(tv[5])
        return gr[0], gr[1], gr[2], gr[3], gr[4], gr[5], gr[6], gr[7], _dot_tn(_bf(out), _bf(tv[5]))

    return _tok_call("rw_post_bwd", fn, L, TB, [(t,) for t in (y, r, kf, v)] + [(g, RW_WIDTH, 0), (dh1, D_MODEL, 0)],
                     [ln_w, ln_b, r_k, wbot], [("heads", F32)] + [(RW_WIDTH, F32)] * 4,
                     [(1, RW_WIDTH)] * 3 + [(RW_WIDTH, D_MODEL)])


def _ffn_fn(w1, w3, w2, diff=True):
    m1, m3, m2 = _mmc(w1, diff), _mmc(w3, diff), _mmc(w2, diff)

    def f(h1, norm_ffn, e1, e3):
        hn = _rms(h1, norm_ffn)
        a1 = m1(hn) + e1
        a3 = m3(hn) + e3
        hm = a1 * _sigmoid(a1) * a3
        return h1 + m2(hm), (hn, hm)

    return f


TB_FFN = 256


def _mixffn_fwd(x, s5_out, rw_out, wtop, wbot, norm_ffn, w1, w3, w2, L):
    def fn(i, tv, cv):
        h1 = tv[0] + _dot(_bf(tv[1]), cv[0]) + _dot(_bf(tv[2]), cv[1])
        h2, _ = _ffn_fn(cv[3], cv[4], cv[5], False)(h1, cv[2], 0.0, 0.0)
        return h1, h2

    return _tok_call("mixffn_fwd", fn, L, TB_FFN, [(x, D_MODEL, 0), (s5_out, S5_WIDTH, 0), (rw_out, RW_WIDTH, 0)],
                     [wtop, wbot, norm_ffn, w1, w3, w2], [(D_MODEL, F32), (D_MODEL, F32)])


def _ffn_bwd(h1, dh2, norm_ffn, w1, w3, w2, L):
    def fn(i, tv, cv):
        e0 = jnp.zeros((TB_FFN, FFN_HIDDEN), F32)
        _, vjp, (hn, hm) = jax.vjp(_ffn_fn(cv[1], cv[2], cv[3]), tv[0], cv[0], e0, e0, has_aux=True)
        dh1, dn, d1, d3 = vjp(tv[1])
        return dh1, d1, d3, hm, hn, dn

    return _tok_call("ffn_bwd", fn, L, TB_FFN, [(h1, D_MODEL, 0), (dh2, D_MODEL, 0)], [norm_ffn, w1, w3, w2],
                     [(D_MODEL, F32), (FFN_HIDDEN, BF16), (FFN_HIDDEN, BF16), (FFN_HIDDEN, BF16), (D_MODEL, BF16)],
                     [(1, D_MODEL)])


def _ple_loss_fb(h2, p, target, norm_ple, final_norm, wg, wu, L):
    def fn(i, tv, cv):
        mgate, mup = _mmc(cv[2]), _mmc(cv[3], False)

        def f(h2_, norm_ple_, final_norm_, eg, eu):
            hn = _rms(h2_, norm_ple_)
            gate = _sigmoid(mgate(hn) + eg)
            h3 = h2_ + gate * (mup(tv[1]) + eu)
            out = _rms(h3, final_norm_)
            d = out - tv[2]
            return 0.5 * jnp.sum(jnp.mean(d * d, axis=-1, keepdims=True)), hn

        e0 = jnp.zeros((TB, D_MODEL), F32)
        loss, vjp, hn = jax.vjp(f, tv[0], cv[0], cv[1], e0, e0, has_aux=True)
        dh2, dnp, dfn, deg, deu = vjp(jnp.ones((), F32))
        return (dh2, dh2, jnp.full((8, 128), loss, F32), dnp, dfn,
                _dot_tn(_bf(hn), _bf(deg)), _dot_tn(_bf(tv[1]), _bf(deu)))

    return _tok_call("ple_loss_fb", fn, L, TB, [(h2, D_MODEL, 0), (p, PLE_DIM, 0), (target, D_MODEL, 0)],
                     [norm_ple, final_norm, wg, wu], [(D_MODEL, F32), (D_MODEL, BF16)],
                     [(8, 128), (1, D_MODEL), (1, D_MODEL), (D_MODEL, D_MODEL), (PLE_DIM, D_MODEL)])


def _inproj_bwd(x, dh1, du, dzs, norm_mix, mu, w_u, w_z, L):
    nb = L // TB

    def fn(i, tv, cv):
        sub = lax.broadcasted_iota(jnp.int32, (TB, 1), 0)
        m = cv[1]
        b = tv[3] * m
        nxt = jnp.where(i == nb - 1, 0.0, tv[4][0:1, :] * m)
        dz = tv[3] * (1.0 - m) + jnp.where(sub == TB - 1, nxt, pltpu.roll(b, TB - 1, 0))
        dub, dzb = _bf(tv[2]), _bf(dz)
        dxn = _dot_nt(dub, cv[2]) + _dot_nt(dzb, cv[3])
        _, vjp = jax.vjp(_rms, tv[0], cv[0])
        dx, dn = vjp(dxn)
        return tv[1] + dx, jnp.concatenate([dub, dzb], axis=1), dn

    return _tok_call("inproj_bwd", fn, L, TB,
                     [(x, D_MODEL, 0), (dh1, D_MODEL, 0), (du, S5_WIDTH, 0), (dzs, SHIFT_COLS, 0), (dzs, SHIFT_COLS, 0, "next")],
                     [norm_mix, mu, w_u, w_z], [(D_MODEL, F32), (IN_COLS, BF16)], [(1, D_MODEL)])


def _eye8(dt):
    return jnp.eye(8, dtype=dt)


def _quarter_b(bb):
    return jnp.einsum("hg,qgcp->qhcgp", _eye8(bb.dtype), bb.reshape(S5_Q, 8, S5_GROUP, S5_STATE)).reshape(S5_Q, S5_QL, S5_QS)


def _unquarter_b(d):
    return jnp.einsum("qhcgp,hg->qgcp", d.reshape(S5_Q, 8, S5_GROUP, 8, S5_STATE), _eye8(d.dtype)).reshape(
        S5_GROUPS, S5_GROUP, S5_STATE)


def _quarter_c(c):
    return jnp.einsum("gh,qgcp->qgphc", _eye8(c.dtype), c.reshape(S5_Q, 8, S5_GROUP, S5_STATE)).reshape(S5_Q, S5_QS, S5_QL)


def _unquarter_c(d):
    return jnp.einsum("qgphc,gh->qgcp", d.reshape(S5_Q, 8, S5_STATE, 8, S5_GROUP), _eye8(d.dtype)).reshape(
        S5_GROUPS, S5_GROUP, S5_STATE)


def _local_step(x, p, target, W, late_weights=None, grads_ready=None, first_dep=None):
    L = x.shape[0]
    r2 = lambda v: v.reshape(1, -1)
    w_in = W["w_in"]
    w2pad = jnp.pad(W["rw_w2"], ((0, 64), (0, 0)))
    a2pad = jnp.pad(W["rw_a2"], ((64, 0), (0, 0)))
    mu = r2(W["rw_shift_mu"])
    rw_vec = [r2(W[n]) for n in ("rw_w0", "rw_a0", "rw_k_k", "rw_k_a")]
    ln_w, ln_b, r_k = r2(W["rw_ln_w"]), r2(W["rw_ln_b"]), r2(W["rw_r_k"])

    lam_re, lam_im = W["s5_lam_re"], W["s5_lam_im"]
    log_step = W["s5_log_step"].reshape(S5_GROUPS, 1)
    bt_re, bt_im = W["s5_b_re"].transpose(0, 2, 1), W["s5_b_im"].transpose(0, 2, 1)
    lb_re, lb_im, bb_re, bb_im = _s5_param_fwd(lam_re, lam_im, log_step, bt_re, bt_im)
    bq_re, bq_im = _quarter_b(bb_re).astype(BF16), _quarter_b(bb_im).astype(BF16)
    cq_re, cq_im = _quarter_c(W["s5_c_re"]).astype(BF16), _quarter_c(W["s5_c_im"]).astype(BF16)
    lbar = jnp.concatenate([lb_re.reshape(1, -1), lb_im.reshape(1, -1), jnp.zeros((6, S5_LANES), F32)], axis=0)
    dskip = r2(W["s5_d"])
    glu_b = r2(W["s5_glu_b"])
    norm_mix, norm_ffn, norm_ple, final_norm = (r2(W[n]) for n in ("norm_mix", "norm_ffn", "norm_ple", "final_norm"))

    proj, xn = _inproj_fwd(x, norm_mix, w_in, L, () if first_dep is None else (first_dep,))
    y_s5, ck5 = _s5_scan_fwd(proj, bq_re, bq_im, cq_re, cq_im, lbar, dskip, L, TB)
    s5_out = _s5_post_fwd(y_s5, W["s5_glu_w"], glu_b, L)
    r, wd, kf, v, a_s, b_s, g = _rw_pre_fwd(proj, mu, *rw_vec, w2pad, a2pad, W["rw_g2"], L)
    scan_in = (r, wd, kf, v, a_s, b_s)
    y_wkv, ckw = _wkv_fwd(*scan_in, L)
    rw_out = _rw_post_fwd(y_wkv, r, kf, v, g, ln_w, ln_b, r_k, L)
    if late_weights is not None:
        W = dict(W, **late_weights(rw_out))
    wtop, wbot = W["w_out"][:S5_WIDTH], W["w_out"][S5_WIDTH:]
    h1, h2 = _mixffn_fwd(x, s5_out, rw_out, wtop, wbot, norm_ffn, W["ffn_w1"], W["ffn_w3"], W["ffn_w2"], L)

    G = {}
    dh2, dh2_bf, loss_acc, G["norm_ple"], G["final_norm"], G["ple_gate_w"], G["ple_up_w"] = _ple_loss_fb(
        h2, p, target, norm_ple, final_norm, W["ple_gate_w"], W["ple_up_w"], L)
    dh1, da1, da3, hm, hn_ffn, G["norm_ffn"] = _ffn_bwd(h1, dh2, norm_ffn, W["ffn_w1"], W["ffn_w3"], W["ffn_w2"], L)
    G["ffn_w1"] = _mm_tn("dw_ffn_w1", hn_ffn, da1)
    G["ffn_w3"] = _mm_tn("dw_ffn_w3", hn_ffn, da3)
    G["ffn_w2"] = _mm_tn("dw_ffn_w2", hm, dh2_bf)
    dep_a = grads_ready(0, G) if grads_ready is not None else None
    dy_s5, G["s5_glu_b"], G["s5_glu_w"], d_wtop = _s5_post_bwd(y_s5, dh1, W["s5_glu_w"], glu_b, wtop, L,
                                                               () if dep_a is None else (dep_a,))
    dy_wkv, dr2, dk2, dv2, dg, G["rw_ln_w"], G["rw_ln_b"], G["rw_r_k"], d_wbot = _rw_post_bwd(
        y_wkv, r, kf, v, g, dh1, ln_w, ln_b, r_k, wbot, L)
    G["w_out"] = jnp.concatenate([d_wtop, d_wbot], axis=0)
    dep = grads_ready(1, G) if grads_ready is not None else None
    dr1, dwd, dk1, dv1, da_s, db_s = _wkv_bwd(*scan_in, dy_wkv, ckw, L, () if dep is None else (dep,))
    (dzs, G["rw_shift_mu"], G["rw_w0"], G["rw_a0"], G["rw_k_k"], G["rw_k_a"], d_w2pad, d_a2pad, G["rw_g2"]) = _rw_pre_bwd(
        proj, (dr1, dr2, dwd, dk1, dk2, dv1, dv2, da_s, db_s, dg), mu, *rw_vec, w2pad, a2pad, W["rw_g2"], L)
    G["rw_w2"], G["rw_a2"] = d_w2pad[:64], d_a2pad[64:]
    du, dbq_re, dbq_im, dcq_re, dcq_im, dlbar, G["s5_d"] = _s5_scan_bwd(
        proj, dy_s5, ck5, bq_re, bq_im, cq_re, cq_im, lbar, dskip, L, TB)
    G["s5_c_re"], G["s5_c_im"] = _unquarter_c(dcq_re), _unquarter_c(dcq_im)
    d_lam_re, d_lam_im, d_ls, d_bt_re, d_bt_im = _s5_param_bwd(
        lam_re, lam_im, log_step, bt_re, bt_im, dlbar[0].reshape(S5_GROUPS, S5_STATE), dlbar[1].reshape(S5_GROUPS, S5_STATE),
        _unquarter_b(dbq_re), _unquarter_b(dbq_im))
    G["s5_lam_re"], G["s5_lam_im"], G["s5_log_step"] = d_lam_re, d_lam_im, d_ls.reshape(S5_GROUPS)
    G["s5_b_re"], G["s5_b_im"] = d_bt_re.transpose(0, 2, 1), d_bt_im.transpose(0, 2, 1)
    dx, dproj, G["norm_mix"] = _inproj_bwd(x, dh1, du, dzs, norm_mix, mu, w_in[:, :S5_WIDTH], w_in[:, S5_WIDTH:], L)
    G["w_in"] = _mm_tn("dw_in", xn, dproj)
    return loss_acc[0, 0], dx, G


MESH_AXES = ("x", "y", "c")


def _all_gather(name, shards):
    nt = len(shards)

    def body(*refs):
        x_refs, out_refs = refs[:nt], refs[nt:2 * nt]
        send_sems, recv_sems, local_sems = refs[2 * nt:]
        x, y, c = lax.axis_index("x"), lax.axis_index("y"), lax.axis_index("c")
        me, sibling = (x, y, c), (x, y, 1 - c)
        chips = [(1 - x, y), (x, 1 - y), (1 - x, 1 - y)]

        def rows(t, px, py, pc):
            m_per = shards[t].shape[0]
            return out_refs[t].at[pl.ds((4 * px + 2 * py + pc) * m_per, m_per), :]

        def copy(t, k, block, to, src=None):
            return pltpu.make_async_remote_copy(
                src_ref=rows(t, *block) if src is None else src, dst_ref=rows(t, *block),
                send_sem=send_sems.at[7 * t + k], recv_sem=recv_sems.at[7 * t + k],
                device_id=to, device_id_type=pl.DeviceIdType.MESH)

        mine = [pltpu.make_async_copy(x_refs[t], rows(t, *me), local_sems.at[t]) for t in range(nt)]
        for cp in mine:
            cp.start()
        first = []
        for t in range(nt):
            first.append(copy(t, 0, me, sibling, src=x_refs[t]))
            first += [copy(t, 1 + j, me, (*chip, c), src=x_refs[t]) for j, chip in enumerate(chips)]
        for cp in first:
            cp.start()
        passed = []
        for t in range(nt):
            for j, chip in enumerate(chips):
                copy(t, 1 + j, (*chip, c), me).wait_recv()
                fwd = copy(t, 4 + j, (*chip, c), sibling)
                fwd.start()
                passed.append(fwd)
        for t in range(nt):
            copy(t, 0, sibling, me).wait_recv()
            for j, chip in enumerate(chips):
                copy(t, 4 + j, (*chip, 1 - c), me).wait_recv()
        for cp in first + passed:
            cp.wait_send()
        for cp in mine:
            cp.wait()

    return _pcall(body, name=name,
                  out_shape=[jax.ShapeDtypeStruct((N_DEV * a.shape[0], a.shape[1]), a.dtype) for a in shards],
                  in_specs=[_ANY] * nt, out_specs=[_ANY] * nt,
                  scratch_shapes=[pltpu.SemaphoreType.DMA((7 * nt,)), pltpu.SemaphoreType.DMA((7 * nt,)),
                                  pltpu.SemaphoreType.DMA((nt,))])(*shards)


_HBM = pl.BlockSpec(memory_space=pltpu.HBM)
_SEM = pl.BlockSpec(memory_space=pltpu.SEMAPHORE)
_EFFECT = pltpu.SideEffectType.DATAFLOW_SIDE_EFFECTING


def _peer_of(k):
    x, y, c = lax.axis_index("x"), lax.axis_index("y"), lax.axis_index("c")
    px, py, pc = x ^ ((k >> 2) & 1), y ^ ((k >> 1) & 1), c ^ (k & 1)
    return (px, py, pc), 4 * px + 2 * py + pc, 4 * x + 2 * y + c


def _direct_copy(t, k, src_refs, land_refs, send_sems, recv_sems, rows_of, gather):
    dev, peer, me = _peer_of(k)
    m = rows_of[t]
    src = src_refs[t] if gather else src_refs[t].at[pl.ds(peer * m, m), :]
    return pltpu.make_async_remote_copy(
        src_ref=src, dst_ref=land_refs[t].at[pl.ds(me * m, m), :],
        send_sem=send_sems.at[7 * t + k - 1], recv_sem=recv_sems.at[7 * t + k - 1],
        device_id=dev, device_id_type=pl.DeviceIdType.MESH)


def _direct_landing(t, k, src_refs, land_refs, send_sems, recv_sems, rows_of, gather):
    dev, peer, me = _peer_of(k)
    m = rows_of[t]
    src = src_refs[t] if gather else src_refs[t].at[pl.ds(me * m, m), :]
    return pltpu.make_async_remote_copy(
        src_ref=src, dst_ref=land_refs[t].at[pl.ds(peer * m, m), :],
        send_sem=send_sems.at[7 * t + k - 1], recv_sem=recv_sems.at[7 * t + k - 1],
        device_id=dev, device_id_type=pl.DeviceIdType.MESH)


def _direct_start(name, srcs, gather, dep=None):
    nt = len(srcs)
    rows_of = [a.shape[0] if gather else a.shape[0] // N_DEV for a in srcs]
    lands = [pltpu.with_memory_space_constraint(lax.empty((N_DEV * m, a.shape[1]), a.dtype), pltpu.HBM)
             for a, m in zip(srcs, rows_of)]

    n_dep = 0 if dep is None else 1

    def body(*refs):
        src_refs, land_refs = refs[:nt], refs[nt:2 * nt]
        send_sems, recv_sems = refs[2 * nt + n_dep], refs[2 * nt + n_dep + 1]
        token = refs[-1]
        for t in range(nt):
            for k in range(1, N_DEV):
                _direct_copy(t, k, src_refs, land_refs, send_sems, recv_sems, rows_of, gather).start()
        token[...] = jnp.zeros(token.shape, F32)

    out = _pcall(
        body, name=name,
        out_shape=(pltpu.SemaphoreType.DMA((7 * nt,)), pltpu.SemaphoreType.DMA((7 * nt,)),
                   *[pltpu.HBM(a.shape, a.dtype) for a in srcs], *[pltpu.HBM(a.shape, a.dtype) for a in lands],
                   jax.ShapeDtypeStruct((8, 128), F32)),
        in_specs=(_HBM,) * (2 * nt) + (pl.BlockSpec(memory_space=pl.ANY),) * n_dep,
        out_specs=(_SEM, _SEM) + (_HBM,) * (2 * nt) + (pl.BlockSpec(memory_space=pltpu.VMEM),),
        input_output_aliases={i: 2 + i for i in range(2 * nt)},
        compiler_params=pltpu.CompilerParams(has_side_effects=_EFFECT),
    )(*[pltpu.with_memory_space_constraint(a, pltpu.HBM) for a in srcs], *lands, *(() if dep is None else (dep,)))
    return (out[0], out[1], list(out[2:2 + nt]), list(out[2 + nt:2 + 2 * nt]), rows_of, gather), out[-1]


def _direct_wait(name, handle, after):
    send_sems, recv_sems, srcs, lands, rows_of, gather = handle
    nt = len(srcs)
    after = list(after) if isinstance(after, (list, tuple)) else [after]

    def body(*refs):
        src_refs, land_refs = refs[:nt], refs[nt:2 * nt]
        s_sems, r_sems = refs[2 * nt], refs[2 * nt + 1]
        for t in range(nt):
            for k in range(1, N_DEV):
                _direct_copy(t, k, src_refs, land_refs, s_sems, r_sems, rows_of, gather).wait_send()
                _direct_landing(t, k, src_refs, land_refs, s_sems, r_sems, rows_of, gather).wait_recv()

    out = _pcall(
        body, name=name,
        out_shape=tuple(pltpu.HBM(a.shape, a.dtype) for a in srcs) + tuple(pltpu.HBM(a.shape, a.dtype) for a in lands),
        in_specs=(_HBM,) * (2 * nt) + (_SEM, _SEM) + (pl.BlockSpec(memory_space=pl.ANY),) * len(after),
        out_specs=(_HBM,) * (2 * nt),
        input_output_aliases={i: i for i in range(2 * nt)},
        compiler_params=pltpu.CompilerParams(has_side_effects=_EFFECT),
    )(*srcs, *lands, send_sems, recv_sems, *after)
    return list(out[:nt]), list(out[nt:])


def _adamw_sharded(name, own, parts, w, m, v, rb, deps=()):
    R, N = own.shape

    def body(o_ref, p_ref, w_ref, m_ref, v_ref, *rest):
        g_ref, d_ref, nm_ref, nv_ref = rest[len(deps):]
        me = 4 * lax.axis_index("x") + 2 * lax.axis_index("y") + lax.axis_index("c")
        g = o_ref[...]
        for k in range(1, N_DEV):
            g = g + p_ref[me ^ k].astype(F32)
        nm = ADAM_B1 * m_ref[...] + (1.0 - ADAM_B1) * g
        nv = ADAM_B2 * v_ref[...] + (1.0 - ADAM_B2) * (g * g)
        m_hat = nm / (1.0 - ADAM_B1 ** ADAM_STEP)
        v_hat = nv / (1.0 - ADAM_B2 ** ADAM_STEP)
        g_ref[...] = g
        d_ref[...] = -ADAM_LR * (m_hat / (jnp.sqrt(v_hat) + ADAM_EPS) + ADAM_WD * w_ref[...])
        nm_ref[...] = nm
        nv_ref[...] = nv

    blk = pl.BlockSpec((rb, N), lambda i: (i, 0))
    sh = jax.ShapeDtypeStruct((R, N), F32)
    return _pcall(body, name=name, grid=(R // rb,),
                  in_specs=[blk, pl.BlockSpec((N_DEV, rb, N), lambda i: (0, i, 0)), blk, blk, blk]
                  + [pl.BlockSpec(d.shape, lambda i, nd=d.ndim: (0,) * nd) for d in deps],
                  out_specs=[blk] * 4, out_shape=[sh] * 4, compiler_params=_cparams(1))(own, parts, w, m, v, *deps)


SMALL_CLASSES = (
    (("norm_mix", 1, 1024), ("norm_ffn", 1, 1024), ("norm_ple", 1, 1024), ("final_norm", 1, 1024)),
    (("s5_d", 1, 512), ("s5_glu_b", 1, 512), ("rw_w0", 1, 512), ("rw_a0", 1, 512), ("rw_k_k", 1, 512), ("rw_k_a", 1, 512),
     ("rw_ln_w", 1, 512), ("rw_ln_b", 1, 512), ("rw_r_k", 1, 512)),
    (("rw_shift_mu", 1, 1792),),
    (("s5_lam_re", 32, 64), ("s5_lam_im", 32, 64), ("s5_c_re", 512, 64), ("s5_c_im", 512, 64)),
    (("s5_log_step", 1, 32),),
    (("s5_b_re", 2048, 16), ("s5_b_im", 2048, 16)),
)


def _class_rows(cls):
    return -(-sum(r for _, r, _ in cls) // 8) * 8


def _stack_class(cls, arrs):
    a = jnp.concatenate(arrs, axis=0) if len(arrs) > 1 else arrs[0]
    pad = _class_rows(cls) - a.shape[0]
    return jnp.pad(a, ((0, pad), (0, 0))) if pad else a


def _adamw_small(grads, w, m, v):
    names = [n for cls in SMALL_CLASSES for n, _, _ in cls]
    n_cls, n_par = len(SMALL_CLASSES), len(names)

    def body(*refs):
        g_refs = refs[:n_cls]
        w_refs, m_refs, v_refs = (refs[n_cls + i * n_par:n_cls + (i + 1) * n_par] for i in range(3))
        o_refs = refs[n_cls + 3 * n_par:]
        p = 0
        for cls, g_ref in zip(SMALL_CLASSES, g_refs):
            rc = _class_rows(cls)
            tot = g_ref[0:rc, :]
            for s_ in range(1, N_DEV):
                tot = tot + g_ref[s_ * rc:(s_ + 1) * rc, :]
            off = 0
            for _, r, _ in cls:
                g = tot[off:off + r, :]
                off += r
                nm = ADAM_B1 * m_refs[p][...] + (1.0 - ADAM_B1) * g
                nv = ADAM_B2 * v_refs[p][...] + (1.0 - ADAM_B2) * (g * g)
                m_hat = nm / (1.0 - ADAM_B1 ** ADAM_STEP)
                v_hat = nv / (1.0 - ADAM_B2 ** ADAM_STEP)
                o_refs[4 * p][...] = g
                o_refs[4 * p + 1][...] = -ADAM_LR * (m_hat / (jnp.sqrt(v_hat) + ADAM_EPS) + ADAM_WD * w_refs[p][...])
                o_refs[4 * p + 2][...] = nm
                o_refs[4 * p + 3][...] = nv
                p += 1

    shapes = [(r, c) for cls in SMALL_CLASSES for _, r, c in cls]
    out = _pcall(body, name="adamw_replicated",
                 out_shape=[jax.ShapeDtypeStruct(sh, F32) for sh in shapes for _ in range(4)],
                 compiler_params=pltpu.CompilerParams(vmem_limit_bytes=VMEM_LIMIT))(*grads, *w, *m, *v)
    return {n: out[4 * i:4 * i + 4] for i, n in enumerate(names)}


EARLY = (("w_in", True),)
LATE = (("ffn_w1", True), ("ffn_w3", True), ("ffn_w2", False), ("ple_gate_w", False), ("w_out", False))
GRAD_STAGES = (LATE[:4], LATE[4:])
MISC = (("s5_glu_w", False), ("rw_w2", True), ("rw_a2", True), ("rw_g2", True), ("ple_up_w", True))
SHARDED_NAMES = tuple(n for n, _ in EARLY + LATE + MISC)
PACK_COLS = 1024
WEIGHT_NAMES = ("norm_mix", "w_in", "s5_lam_re", "s5_lam_im", "s5_log_step", "s5_b_re", "s5_b_im", "s5_c_re", "s5_c_im", "s5_d",
                "s5_glu_w", "s5_glu_b", "rw_shift_mu", "rw_w0", "rw_w2", "rw_a0", "rw_a2", "rw_g2", "rw_k_k", "rw_k_a", "rw_r_k",
                "rw_ln_w", "rw_ln_b", "w_out", "norm_ffn", "ffn_w1", "ffn_w3", "ffn_w2", "norm_ple", "ple_gate_w", "ple_up_w",
                "final_norm")
SMALL_NAMES = tuple(n for n in WEIGHT_NAMES if n not in SHARDED_NAMES)
ARG_NAMES = ("x", "p") + WEIGHT_NAMES + ("loss_target",) + tuple("m_" + n for n in WEIGHT_NAMES) + tuple("v_" + n for n in WEIGHT_NAMES)


def _travel(a, tr):
    return a.T if tr else a


def _pack_misc(blocks):
    lead = blocks[0].shape[:-2]
    return jnp.concatenate([b.reshape(lead + (-1, PACK_COLS)) for b in blocks], axis=len(lead))


def _unpack_misc(packed, shapes):
    lead = packed.shape[:-2]
    out, off = [], 0
    for r, c in shapes:
        n = r * c // PACK_COLS
        out.append(lax.slice_in_dim(packed, off, off + n, axis=len(lead)).reshape(lead + (r, c)))
        off += n
    return out


def _kernel_impl(ins):
    x, p, target = ins["x"][0], ins["p"][0, 0], ins["loss_target"][0]
    me = 4 * lax.axis_index("x") + 2 * lax.axis_index("y") + lax.axis_index("c")
    small = {n: (ins[n] if n == "final_norm" else ins[n][0]) for n in SMALL_NAMES}
    trav = lambda pre, n, tr: _travel(ins[pre + n][0], tr)
    misc_shapes = [trav("", n, tr).shape for n, tr in MISC]

    early = _all_gather("ag_early", [trav("", n, tr).astype(BF16) for n, tr in EARLY]
                        + [_pack_misc([trav("", n, tr).astype(BF16) for n, tr in MISC])])
    late_handle, late_token = _direct_start("ag_late_start", [trav("", n, tr).astype(BF16) for n, tr in LATE], True, early[-1])
    W = dict(small)
    for (n, tr), g in zip(EARLY, early):
        W[n] = _travel(g, tr)
    for (n, tr), g in zip(MISC, _unpack_misc(early[-1].reshape(N_DEV, -1, PACK_COLS), misc_shapes)):
        W[n] = _travel(g.reshape(-1, g.shape[-1]), tr)

    def late_weights(after):
        shards, lands = _direct_wait("ag_late_wait", late_handle, after)
        full = [lax.dynamic_update_slice_in_dim(ld, sh, me * sh.shape[0], axis=0) for ld, sh in zip(lands, shards)]
        return {n: _travel(g, tr) for (n, tr), g in zip(LATE, full)}

    gt = lambda G, n, tr: _travel(G[n], tr)
    started = {}

    def grads_ready(stage, G):
        full = [gt(G, n, tr) for n, tr in GRAD_STAGES[stage]]
        started[stage] = (full, *_direct_start("grad_late_start%d" % stage, [a.astype(BF16) for a in full], False))
        return started[stage][2]

    loss_part, dx, G = _local_step(x, p, target, W, late_weights, grads_ready, late_token)

    misc_g = _pack_misc([gt(G, n, tr).reshape((N_DEV,) + shp) for (n, tr), shp in zip(MISC, misc_shapes)])
    early_full = [gt(G, n, tr) for n, tr in EARLY] + [misc_g.reshape(-1, PACK_COLS)]
    early_handle, early_token = _direct_start("grad_early_start", [a.astype(BF16) for a in early_full], False)
    view2 = lambda a, r, c: a.reshape(r, c)
    small_own = [_stack_class(cls, [view2(G[n], r, c) for n, r, c in cls]) for cls in SMALL_CLASSES]
    small_handle, small_token = _direct_start("grad_small_start", small_own, True)
    late_src, late_land = [], []
    for stage in range(len(GRAD_STAGES)):
        full, handle, _ = started[stage]
        _, land = _direct_wait("grad_late_wait%d" % stage, handle, small_token)
        late_src += full
        late_land += land

    outs = {}

    def emit(names_shapes, res):
        for tag, val in zip(("grad_", "delta_", "new_m_", "new_v_"), res):
            for n, v in names_shapes(val):
                outs[tag + n] = v

    def sharded_update(n, tr, src, land, deps=()):
        rows = src.shape[0] // N_DEV
        own = lax.dynamic_slice_in_dim(src, me * rows, rows, axis=0)
        res = _adamw_sharded("adamw_" + n, own, land.reshape(N_DEV, rows, land.shape[1]),
                             trav("", n, tr), trav("m_", n, tr), trav("v_", n, tr), _pick_rows(rows), deps)
        emit(lambda val: [(n, _travel(val, tr).reshape(ins[n].shape))], res)
        return list(res)

    for (n, tr), src, land in zip(LATE, late_src, late_land):
        sharded_update(n, tr, src, land, (early_token,))
    _, early_land = _direct_wait("grad_early_wait", early_handle, list(outs.values()))
    for (n, tr), src, land in zip(EARLY, early_full[:-1], early_land[:-1]):
        sharded_update(n, tr, src, land)
    pm = lambda pre: _pack_misc([trav(pre, n, tr) for n, tr in MISC])
    rows = early_full[-1].shape[0] // N_DEV
    res = _adamw_sharded("adamw_misc", lax.dynamic_slice_in_dim(early_full[-1], me * rows, rows, axis=0),
                         early_land[-1].reshape(N_DEV, rows, PACK_COLS), pm(""), pm("m_"), pm("v_"), rows)
    emit(lambda val: [(n, _travel(b, tr).reshape(ins[n].shape)) for (n, tr), b in zip(MISC, _unpack_misc(val, misc_shapes))], res)
    small_src, small_land = _direct_wait("grad_small_wait", small_handle, res[0])
    small_all = [lax.dynamic_update_slice_in_dim(ld, sr, me * sr.shape[0], axis=0) for ld, sr in zip(small_land, small_src)]
    flat_small = [(n, r, c) for cls in SMALL_CLASSES for n, r, c in cls]
    res = _adamw_small(small_all, *[[view2(ins[pre + n], r, c) for n, r, c in flat_small] for pre in ("", "m_", "v_")])
    for n, _, _ in flat_small:
        for tag, val in zip(("grad_", "delta_", "new_m_", "new_v_"), res[n]):
            outs[tag + n] = val.reshape(ins[n].shape)
    loss = lax.psum(loss_part, MESH_AXES)
    res = [loss, dx[None]]
    for tag in ("grad_", "delta_", "new_m_", "new_v_"):
        res += [outs[tag + n] for n in WEIGHT_NAMES]
    return tuple(res)


def _pick_rows(r):
    best = 8
    for b in range(8, 257, 8):
        if r % b == 0:
            best = b
    return best


def kernel(x, p, norm_mix, w_in, s5_lam_re, s5_lam_im, s5_log_step, s5_b_re, s5_b_im, s5_c_re, s5_c_im, s5_d, s5_glu_w, s5_glu_b, rw_shift_mu, rw_w0, rw_w2, rw_a0, rw_a2, rw_g2, rw_k_k, rw_k_a, rw_r_k, rw_ln_w, rw_ln_b, w_out, norm_ffn, ffn_w1, ffn_w3, ffn_w2, norm_ple, ple_gate_w, ple_up_w, final_norm, loss_target, m_norm_mix, m_w_in, m_s5_lam_re, m_s5_lam_im, m_s5_log_step, m_s5_b_re, m_s5_b_im, m_s5_c_re, m_s5_c_im, m_s5_d, m_s5_glu_w, m_s5_glu_b, m_rw_shift_mu, m_rw_w0, m_rw_w2, m_rw_a0, m_rw_a2, m_rw_g2, m_rw_k_k, m_rw_k_a, m_rw_r_k, m_rw_ln_w, m_rw_ln_b, m_w_out, m_norm_ffn, m_ffn_w1, m_ffn_w3, m_ffn_w2, m_norm_ple, m_ple_gate_w, m_ple_up_w, m_final_norm, v_norm_mix, v_w_in, v_s5_lam_re, v_s5_lam_im, v_s5_log_step, v_s5_b_re, v_s5_b_im, v_s5_c_re, v_s5_c_im, v_s5_d, v_s5_glu_w, v_s5_glu_b, v_rw_shift_mu, v_rw_w0, v_rw_w2, v_rw_a0, v_rw_a2, v_rw_g2, v_rw_k_k, v_rw_k_a, v_rw_r_k, v_rw_ln_w, v_rw_ln_b, v_w_out, v_norm_ffn, v_ffn_w1, v_ffn_w3, v_ffn_w2, v_norm_ple, v_ple_gate_w, v_ple_up_w, v_final_norm):
    return _kernel_impl(dict(zip(ARG_NAMES, (x, p, norm_mix, w_in, s5_lam_re, s5_lam_im, s5_log_step, s5_b_re, s5_b_im, s5_c_re, s5_c_im, s5_d, s5_glu_w, s5_glu_b, rw_shift_mu, rw_w0, rw_w2, rw_a0, rw_a2, rw_g2, rw_k_k, rw_k_a, rw_r_k, rw_ln_w, rw_ln_b, w_out, norm_ffn, ffn_w1, ffn_w3, ffn_w2, norm_ple, ple_gate_w, ple_up_w, final_norm, loss_target, m_norm_mix, m_w_in, m_s5_lam_re, m_s5_lam_im, m_s5_log_step, m_s5_b_re, m_s5_b_im, m_s5_c_re, m_s5_c_im, m_s5_d, m_s5_glu_w, m_s5_glu_b, m_rw_shift_mu, m_rw_w0, m_rw_w2, m_rw_a0, m_rw_a2, m_rw_g2, m_rw_k_k, m_rw_k_a, m_rw_r_k, m_rw_ln_w, m_rw_ln_b, m_w_out, m_norm_ffn, m_ffn_w1, m_ffn_w3, m_ffn_w2, m_norm_ple, m_ple_gate_w, m_ple_up_w, m_final_norm, v_norm_mix, v_w_in, v_s5_lam_re, v_s5_lam_im, v_s5_log_step, v_s5_b_re, v_s5_b_im, v_s5_c_re, v_s5_c_im, v_s5_d, v_s5_glu_w, v_s5_glu_b, v_rw_shift_mu, v_rw_w0, v_rw_w2, v_rw_a0, v_rw_a2, v_rw_g2, v_rw_k_k, v_rw_k_a, v_rw_r_k, v_rw_ln_w, v_rw_ln_b, v_w_out, v_norm_ffn, v_ffn_w1, v_ffn_w3, v_ffn_w2, v_norm_ple, v_ple_gate_w, v_ple_up_w, v_final_norm))))
```

```python
import functools

import jax
import jax.numpy as jnp
from jax import lax
from jax.experimental import pallas as pl
from jax.experimental.pallas import tpu as pltpu

F32 = jnp.float32
BF16 = jnp.bfloat16

D_MODEL = 1024
S5_WIDTH = 512
RW_WIDTH = 512
S5_GROUP = 16
S5_GROUPS = 32
S5_STATE = 64
S5_LANES = S5_GROUPS * S5_STATE
HEAD = 64
SHIFT_COLS = 1792
IN_COLS = 2304
FFN_HIDDEN = 2816
PLE_DIM = 256
RMS_EPS = 1e-6
GN_EPS = 64e-5
L2_EPS = 1e-12
CHUNK = 64
N_DEV = 8

ADAM_LR = 0.001
ADAM_B1 = 0.9
ADAM_B2 = 0.999
ADAM_EPS = 1e-08
ADAM_WD = 0.01
ADAM_STEP = 10

VMEM_LIMIT = 56 * 1024 * 1024
_ANY = pl.BlockSpec(memory_space=pl.ANY)


def _pcall(body, **kw):
    return pl.pallas_call(body, **kw)


def _cparams(n_grid):
    return pltpu.CompilerParams(dimension_semantics=("arbitrary",) * n_grid, vmem_limit_bytes=VMEM_LIMIT)


def _dot(a, b):
    return jnp.dot(a, b, preferred_element_type=F32)


def _dot_nt(a, b):
    return lax.dot_general(a, b, (((1,), (1,)), ((), ())), preferred_element_type=F32)


def _dot_tn(a, b):
    return lax.dot_general(a, b, (((0,), (0,)), ((), ())), preferred_element_type=F32)


def _mmc(w, diff=True, tr=False):
    fw, bw = (_dot_nt, _dot) if tr else (_dot, _dot_nt)
    if not diff:
        return lambda x: fw(x.astype(BF16), w)

    @jax.custom_vjp
    def f(x):
        return fw(x.astype(BF16), w)

    def fwd(x):
        return fw(x.astype(BF16), w), None

    def bwd(_, dy):
        return (bw(dy.astype(BF16), w),)

    f.defvjp(fwd, bwd)
    return f


def _split_dot(x, m, n_split):
    acc = None
    rem = x
    for s in range(n_split):
        part = rem.astype(BF16)
        t = _dot(part, m)
        acc = t if acc is None else acc + t
        if s + 1 < n_split:
            rem = rem - part.astype(F32)
    return acc


def _segsum(m, diff=True):
    if not diff:
        return lambda x: _split_dot(x, m, 2)

    @jax.custom_vjp
    def f(x):
        return _split_dot(x, m, 2)

    def fwd(x):
        return _split_dot(x, m, 2), None

    def bwd(_, dy):
        return (_split_dot(dy, m, 2),)

    f.defvjp(fwd, bwd)
    return f


def _head_indicator(n):
    r = lax.broadcasted_iota(jnp.int32, (n, n), 0) // HEAD
    c = lax.broadcasted_iota(jnp.int32, (n, n), 1) // HEAD
    return (r == c).astype(BF16)


def _rms(x, g):
    return x * lax.rsqrt(jnp.mean(x * x, axis=-1, keepdims=True) + RMS_EPS) * g


def _softplus(x):
    return jnp.maximum(x, 0.0) + jnp.log(1.0 + jnp.exp(-jnp.abs(x)))


def _sigmoid(x):
    return 1.0 / (1.0 + jnp.exp(-x))


def _gelu(x):
    return 0.5 * x * (1.0 + jnp.tanh(0.7978845608028654 * (x + 0.044715 * (x * x * x))))


def _tok_call(name, fn, L, TB, tok_in, const_in, tok_out, acc_out=(), deps=()):
    nb = L // TB
    g8 = TB // 8
    in_specs, args = [], []
    for spec in tok_in:
        if len(spec) == 1:
            arr = spec[0]
            in_specs.append(pl.BlockSpec((arr.shape[0], TB, HEAD), lambda i: (0, i, 0)))
            args.append(arr)
            continue
        arr, width, cb = spec[:3]
        mode = spec[3] if len(spec) > 3 else None
        if mode is None:
            in_specs.append(pl.BlockSpec((TB, width), lambda i, cb=cb: (i, cb)))
        elif mode == "prev":
            in_specs.append(pl.BlockSpec((8, width), lambda i, cb=cb: (jnp.maximum(i * g8 - 1, 0), cb)))
        else:
            in_specs.append(pl.BlockSpec((8, width), lambda i, cb=cb: (jnp.minimum((i + 1) * g8, L // 8 - 1), cb)))
        args.append(arr)
    for c in const_in:
        in_specs.append(pl.BlockSpec(c.shape, lambda i, nd=c.ndim: (0,) * nd, pipeline_mode=pl.Buffered(1)))
        args.append(c)
    for d in deps:
        in_specs.append(pl.BlockSpec(d.shape, lambda i, nd=d.ndim: (0,) * nd))
        args.append(d)
    out_shape, out_specs = [], []
    for width, dt in tok_out:
        if width == "heads":
            out_shape.append(jax.ShapeDtypeStruct((N_HEAD, L, HEAD), dt))
            out_specs.append(pl.BlockSpec((N_HEAD, TB, HEAD), lambda i: (0, i, 0)))
            continue
        out_shape.append(jax.ShapeDtypeStruct((L, width), dt))
        out_specs.append(pl.BlockSpec((TB, width), lambda i: (i, 0)))
    for shp in acc_out:
        out_shape.append(jax.ShapeDtypeStruct(shp, F32))
        out_specs.append(pl.BlockSpec(shp, lambda i, nd=len(shp): (0,) * nd))
    n_tok, n_const, n_to = len(tok_in), len(const_in), len(tok_out)

    def body(*refs):
        i = pl.program_id(0)
        tv = [r[...] if len(r.shape) == 2 else jnp.concatenate([r[h] for h in range(r.shape[0])], axis=1)
              for r in refs[:n_tok]]
        cv = [r[...] for r in refs[n_tok:n_tok + n_const]]
        orefs = refs[n_tok + n_const + len(deps):]
        outs = fn(i, tv, cv)
        for r, v in zip(orefs[:n_to], outs[:n_to]):
            if len(r.shape) == 3:
                for h in range(r.shape[0]):
                    r[h] = v[:, h * HEAD:(h + 1) * HEAD].astype(r.dtype)
            else:
                r[...] = v.astype(r.dtype)
        for r, v in zip(orefs[n_to:], outs[n_to:]):
            @pl.when(i == 0)
            def _(r=r):
                r[...] = jnp.zeros(r.shape, r.dtype)

            r[...] += v

    res = _pcall(body, name=name, grid=(nb,), in_specs=in_specs, out_specs=out_specs, out_shape=out_shape,
                 compiler_params=_cparams(1))(*args)
    return res


def _pick_block(n, cap):
    best = None
    for b in range(128, min(n, cap) + 1, 128):
        if n % b == 0:
            best = b
    return best if best is not None else n


def _mm_tn(name, a, b):
    T, M = a.shape
    N = b.shape[1]
    bm, bn, bt = _pick_block(M, 1536), _pick_block(N, 1536), _pick_block(T, 512)

    def body(a_ref, b_ref, o_ref):
        t = pl.program_id(2)

        @pl.when(t == 0)
        def _():
            o_ref[...] = jnp.zeros(o_ref.shape, F32)

        o_ref[...] += _dot_tn(a_ref[...].astype(BF16), b_ref[...].astype(BF16))

    return _pcall(body, name=name, grid=(M // bm, N // bn, T // bt),
                  in_specs=[pl.BlockSpec((bt, bm), lambda m, n, t: (t, m)), pl.BlockSpec((bt, bn), lambda m, n, t: (t, n))],
                  out_specs=pl.BlockSpec((bm, bn), lambda m, n, t: (m, n)),
                  out_shape=jax.ShapeDtypeStruct((M, N), F32), compiler_params=_cparams(3))(a, b)


def _s5_param_fn(lam_re, lam_im, log_step, bt_re, bt_im):
    dt = jnp.exp(log_step)
    e = jnp.exp(lam_re * dt)
    lb_re = e * jnp.cos(lam_im * dt)
    lb_im = e * jnp.sin(lam_im * dt)
    den = lam_re * lam_re + lam_im * lam_im
    nr, ni = lb_re - 1.0, lb_im
    co_re = (nr * lam_re + ni * lam_im) / den
    co_im = (ni * lam_re - nr * lam_im) / den
    cr, ci = co_re[:, None, :], co_im[:, None, :]
    return lb_re, lb_im, cr * bt_re - ci * bt_im, cr * bt_im + ci * bt_re


def _s5_param_fwd(lam_re, lam_im, log_step, bt_re, bt_im):
    def body(a, b, c, d, e, o1, o2, o3, o4):
        r = _s5_param_fn(a[...], b[...], c[...], d[...], e[...])
        o1[...], o2[...], o3[...], o4[...] = r

    sh = jax.ShapeDtypeStruct
    return _pcall(body, name="s5_param_fwd",
                  out_shape=[sh(lam_re.shape, F32), sh(lam_re.shape, F32), sh(bt_re.shape, F32), sh(bt_re.shape, F32)])(
        lam_re, lam_im, log_step, bt_re, bt_im)


def _s5_param_bwd(lam_re, lam_im, log_step, bt_re, bt_im, d_lb_re, d_lb_im, d_bb_re, d_bb_im):
    def body(a, b, c, d, e, g1, g2, g3, g4, o1, o2, o3, o4, o5):
        _, vjp = jax.vjp(_s5_param_fn, a[...], b[...], c[...], d[...], e[...])
        r = vjp((g1[...], g2[...], g3[...], g4[...]))
        o1[...], o2[...], o3[...], o4[...], o5[...] = r

    sh = jax.ShapeDtypeStruct
    return _pcall(body, name="s5_param_bwd",
                  out_shape=[sh(lam_re.shape, F32), sh(lam_re.shape, F32), sh(log_step.shape, F32),
                             sh(bt_re.shape, F32), sh(bt_re.shape, F32)])(
        lam_re, lam_im, log_step, bt_re, bt_im, d_lb_re, d_lb_im, d_bb_re, d_bb_im)


def _cmul(ar, ai, br, bi):
    return ar * br - ai * bi, ar * bi + ai * br


def _scan_consts(lr, li, reverse):
    n = lr.shape[1]
    sub = lax.broadcasted_iota(jnp.int32, (8, n), 0)
    pows = [(lr, li)]
    for _ in range(7):
        pows.append(_cmul(pows[-1][0], pows[-1][1], lr, li))
    steps = []
    for s in (1, 2, 4):
        m = (sub < 8 - s) if reverse else (sub >= s)
        pr, pi = pows[s - 1]
        steps.append((s, jnp.where(m, jnp.broadcast_to(pr, (8, n)), 0.0), jnp.where(m, jnp.broadcast_to(pi, (8, n)), 0.0)))
    wr = jnp.zeros((8, n), F32)
    wi = jnp.zeros((8, n), F32)
    for r in range(8):
        e = (8 - r) if reverse else (r + 1)
        wr = jnp.where(sub == r, jnp.broadcast_to(pows[e - 1][0], (8, n)), wr)
        wi = jnp.where(sub == r, jnp.broadcast_to(pows[e - 1][1], (8, n)), wi)
    return steps, wr, wi


S5_Q = 4
S5_QL = S5_WIDTH // S5_Q
S5_QS = S5_LANES // S5_Q
S5_NT = S5_LANES // 128
S5_QT = S5_QS // 128


def _s5_power_table(lb_ref, pw_re, pw_im, seg):
    for j in range(S5_NT):
        lr = jnp.broadcast_to(lb_ref[0:1, j * 128:(j + 1) * 128], (8, 128))
        li = jnp.broadcast_to(lb_ref[1:2, j * 128:(j + 1) * 128], (8, 128))

        def step(i, c, lr=lr, li=li, j=j):
            pw_re[j, i] = c[0]
            pw_im[j, i] = c[1]
            return _cmul(c[0], c[1], lr, li)

        lax.fori_loop(0, seg, step, (lr, li))


def _seg_scan(sre, sim, carry, lb_ref, pw_re, pw_im, rows, reverse):
    seg = rows // 8
    sgn = -1.0 if reverse else 1.0
    sub = lax.broadcasted_iota(jnp.int32, (8, 128), 0)
    rows_at = lambda i: pl.ds(pl.multiple_of(i * 8, 8), 8)
    entering = {}
    half_tiles = S5_NT // 2
    for half in range(2):
        tiles = list(range(half * half_tiles, (half + 1) * half_tiles))
        lam8 = [(jnp.broadcast_to(lb_ref[0:1, j * 128:(j + 1) * 128], (8, 128)),
                 sgn * jnp.broadcast_to(lb_ref[1:2, j * 128:(j + 1) * 128], (8, 128))) for j in tiles]

        def p1(ii, c):
            i = (seg - 1 - ii) if reverse else ii
            out = []
            for n, j in enumerate(tiles):
                lr, li = lam8[n]
                cr, ci = c[2 * n], c[2 * n + 1]
                nr = lr * cr - li * ci + sre[j, rows_at(i), :]
                ni = lr * ci + li * cr + sim[j, rows_at(i), :]
                sre[j, rows_at(i), :] = nr
                sim[j, rows_at(i), :] = ni
                out += [nr, ni]
            return tuple(out)

        ends = lax.fori_loop(0, seg, p1, tuple(jnp.zeros((8, 128), F32) for _ in range(2 * len(tiles))))
        cs = []
        for n, j in enumerate(tiles):
            ls = slice(j * 128, (j + 1) * 128)
            steps, wr, wi = _scan_consts(pw_re[j, seg - 1][0:1, :], sgn * pw_im[j, seg - 1][0:1, :], reverse)
            tr, ti = ends[2 * n], ends[2 * n + 1]
            for sft, pr, pi in steps:
                sh = (8 - sft) if reverse else sft
                yr, yi = pltpu.roll(tr, sh, 0), pltpu.roll(ti, sh, 0)
                tr, ti = tr + pr * yr - pi * yi, ti + pr * yi + pi * yr
            cin_r, cin_i = carry[0:1, ls], carry[1:2, ls]
            tr, ti = tr + wr * cin_r - wi * cin_i, ti + wr * cin_i + wi * cin_r
            edge_out, edge_in, sh = (0, 7, 7) if reverse else (7, 0, 1)
            carry[0:1, ls] = tr[edge_out:edge_out + 1, :]
            carry[1:2, ls] = ti[edge_out:edge_out + 1, :]
            cr = jnp.where(sub == edge_in, jnp.broadcast_to(cin_r, (8, 128)), pltpu.roll(tr, sh, 0))
            ci = jnp.where(sub == edge_in, jnp.broadcast_to(cin_i, (8, 128)), pltpu.roll(ti, sh, 0))
            cs += [cr, ci]
            entering[j] = (cr, ci)

        def p2(i, _):
            k = (seg - 1 - i) if reverse else i
            for n, j in enumerate(tiles):
                pr, pi = pw_re[j, k], pw_im[j, k]
                cr, ci = cs[2 * n], cs[2 * n + 1]
                if reverse:
                    sre[j, rows_at(i), :] = sre[j, rows_at(i), :] + pr * cr + pi * ci
                    sim[j, rows_at(i), :] = sim[j, rows_at(i), :] + pr * ci - pi * cr
                else:
                    sre[j, rows_at(i), :] = sre[j, rows_at(i), :] + pr * cr - pi * ci
                    sim[j, rows_at(i), :] = sim[j, rows_at(i), :] + pr * ci + pi * cr
            return 0

        lax.fori_loop(0, seg, p2, 0, unroll=2)
    return entering


class _SegIO:
    def __init__(self, hbm, buf, sems, rows, width, col0=0):
        self.hbm, self.buf, self.sems, self.rows, self.seg, self.width, self.col0 = hbm, buf, sems, rows, rows // 8, width, col0

    def _copies(self, blk, slot, to_vmem):
        out = []
        for r in range(8):
            h = self.hbm.at[pl.ds(blk * self.rows + r * self.seg, self.seg), pl.ds(self.col0, self.width)]
            v = self.buf.at[slot, :, r, :]
            out.append(pltpu.make_async_copy(h, v, self.sems.at[slot, r]) if to_vmem
                       else pltpu.make_async_copy(v, h, self.sems.at[slot, r]))
        return out

    def start(self, blk, slot, to_vmem):
        for cp in self._copies(blk, slot, to_vmem):
            cp.start()

    def wait(self, blk, slot, to_vmem):
        for cp in self._copies(blk, slot, to_vmem):
            cp.wait()

    def value(self, slot):
        return self.buf[slot].reshape(self.rows, self.width)

    def store(self, slot, val):
        self.buf[slot] = val.reshape(self.seg, 8, self.width)


def _seg_pipeline(i, nb, blk_of, ins, outs, compute):
    slot = i % 2

    @pl.when(i == 0)
    def _():
        for io in ins:
            io.start(blk_of(0), 0, True)

    @pl.when(i + 1 < nb)
    def _():
        for io in ins:
            io.start(blk_of(i + 1), 1 - slot, True)

    for io in ins:
        io.wait(blk_of(i), slot, True)

    @pl.when(i >= 2)
    def _():
        for io in outs:
            io.wait(blk_of(i - 2), slot, False)

    compute(slot)
    for io in outs:
        io.start(blk_of(i), slot, False)

    @pl.when(i == nb - 1)
    def _():
        for io in outs:
            if nb >= 2:
                io.wait(blk_of(i - 1), 1 - slot, False)
            io.wait(blk_of(i), slot, False)


def _s5_scan_fwd(proj, bq_re, bq_im, cq_re, cq_im, lbar, dskip, L, TB):
    nb = L // TB
    seg = TB // 8

    def body(u_hbm, bre, bim, cre, cim, lb_ref, d_ref, y_hbm, ck_ref, sre, sim, carry, pw_re, pw_im,
             ubuf, ybuf, sem_u, sem_y):
        i = pl.program_id(0)
        u_io = _SegIO(u_hbm, ubuf, sem_u, TB, S5_WIDTH)
        y_io = _SegIO(y_hbm, ybuf, sem_y, TB, S5_WIDTH)

        @pl.when(i == 0)
        def _():
            carry[...] = jnp.zeros(carry.shape, F32)
            _s5_power_table(lb_ref, pw_re, pw_im, seg)

        ck_ref[0] = carry[...]

        def compute(slot):
            u = u_io.value(slot)
            ub = u.astype(BF16)
            for q in range(S5_Q):
                uq = ub[:, q * S5_QL:(q + 1) * S5_QL]
                vr, vi = _dot(uq, bre[q]), _dot(uq, bim[q])
                for jj in range(S5_QT):
                    sre[q * S5_QT + jj] = vr[:, jj * 128:(jj + 1) * 128]
                    sim[q * S5_QT + jj] = vi[:, jj * 128:(jj + 1) * 128]
            _seg_scan(sre, sim, carry, lb_ref, pw_re, pw_im, TB, False)
            ys = []
            for q in range(S5_Q):
                sl = slice(q * S5_QL, (q + 1) * S5_QL)
                sr = jnp.concatenate([sre[q * S5_QT + jj] for jj in range(S5_QT)], axis=1).astype(BF16)
                si = jnp.concatenate([sim[q * S5_QT + jj] for jj in range(S5_QT)], axis=1).astype(BF16)
                ys.append(_dot(sr, cre[q]) - _dot(si, cim[q]) + u[:, sl] * d_ref[:, sl])
            y_io.store(slot, jnp.concatenate(ys, axis=1))

        _seg_pipeline(i, nb, lambda st: st, [u_io], [y_io], compute)

    full = lambda a: pl.BlockSpec(a.shape, lambda i, nd=a.ndim: (0,) * nd)
    st = pltpu.VMEM((S5_NT, TB, 128), F32)
    pw = pltpu.VMEM((S5_NT, seg, 8, 128), F32)
    io = pltpu.VMEM((2, seg, 8, S5_WIDTH), F32)
    return _pcall(
        body, name="s5_scan_fwd", grid=(nb,),
        in_specs=[_ANY, full(bq_re), full(bq_im), full(cq_re), full(cq_im), full(lbar), full(dskip)],
        out_specs=[_ANY, pl.BlockSpec((1, 8, S5_LANES), lambda i: (i, 0, 0))],
        out_shape=[jax.ShapeDtypeStruct((L, S5_WIDTH), F32), jax.ShapeDtypeStruct((nb, 8, S5_LANES), F32)],
        scratch_shapes=[st, st, pltpu.VMEM((8, S5_LANES), F32), pw, pw, io, io,
                        pltpu.SemaphoreType.DMA((2, 8)), pltpu.SemaphoreType.DMA((2, 8))],
        compiler_params=_cparams(1))(proj, bq_re, bq_im, cq_re, cq_im, lbar, dskip)


def _s5_scan_bwd(proj, dy, ck, bq_re, bq_im, cq_re, cq_im, lbar, dskip, L, TB):
    nb = L // TB
    seg = TB // 8

    def body(u_hbm, dy_hbm, ck_ref, bre, bim, cre, cim, lb_ref, d_ref,
             du_hbm, dbre, dbim, dcre, dcim, dlb_ref, dd_ref, sre, sim, gre, gim, carry, gcarry, pw_re, pw_im,
             ubuf, dybuf, dubuf, sem_u, sem_dy, sem_du):
        i = pl.program_id(0)
        u_io = _SegIO(u_hbm, ubuf, sem_u, TB, S5_WIDTH)
        dy_io = _SegIO(dy_hbm, dybuf, sem_dy, TB, S5_WIDTH)
        du_io = _SegIO(du_hbm, dubuf, sem_du, TB, S5_WIDTH)

        @pl.when(i == 0)
        def _():
            gcarry[...] = jnp.zeros(gcarry.shape, F32)
            dbre[...] = jnp.zeros(dbre.shape, F32)
            dbim[...] = jnp.zeros(dbim.shape, F32)
            dcre[...] = jnp.zeros(dcre.shape, F32)
            dcim[...] = jnp.zeros(dcim.shape, F32)
            dlb_ref[...] = jnp.zeros(dlb_ref.shape, F32)
            dd_ref[...] = jnp.zeros(dd_ref.shape, F32)
            _s5_power_table(lb_ref, pw_re, pw_im, seg)

        def compute(slot):
            u = u_io.value(slot)
            dy_v = dy_io.value(slot)
            ub = u.astype(BF16)
            dyb = dy_v.astype(BF16)
            carry[...] = ck_ref[0]
            for q in range(S5_Q):
                uq = ub[:, q * S5_QL:(q + 1) * S5_QL]
                dq = dyb[:, q * S5_QL:(q + 1) * S5_QL]
                vr, vi = _dot(uq, bre[q]), _dot(uq, bim[q])
                hr, hi = _dot_nt(dq, cre[q]), -_dot_nt(dq, cim[q])
                for jj in range(S5_QT):
                    ls = slice(jj * 128, (jj + 1) * 128)
                    sre[q * S5_QT + jj] = vr[:, ls]
                    sim[q * S5_QT + jj] = vi[:, ls]
                    gre[q * S5_QT + jj] = hr[:, ls]
                    gim[q * S5_QT + jj] = hi[:, ls]
            entering = _seg_scan(sre, sim, carry, lb_ref, pw_re, pw_im, TB, False)
            _seg_scan(gre, gim, gcarry, lb_ref, pw_re, pw_im, TB, True)

            rows_at = lambda k: pl.ds(pl.multiple_of(k * 8, 8), 8)
            for j in range(S5_NT):
                er, ei = entering[j]
                gr0, gi0 = gre[j, rows_at(0), :], gim[j, rows_at(0), :]
                acc0 = (gr0 * er + gi0 * ei, gi0 * er - gr0 * ei)

                def acc_step(k, acc, j=j):
                    gr, gi_ = gre[j, rows_at(k), :], gim[j, rows_at(k), :]
                    spr, spi = sre[j, rows_at(k - 1), :], sim[j, rows_at(k - 1), :]
                    return acc[0] + gr * spr + gi_ * spi, acc[1] - gr * spi + gi_ * spr

                ar, ai = lax.fori_loop(1, seg, acc_step, acc0, unroll=2 if (seg - 1) % 2 == 0 else 1)
                ls = slice(j * 128, (j + 1) * 128)
                dlb_ref[0:1, ls] += jnp.sum(ar, axis=0, keepdims=True)
                dlb_ref[1:2, ls] += jnp.sum(ai, axis=0, keepdims=True)

            dd_ref[...] += jnp.sum(dy_v * u, axis=0, keepdims=True)
            dus = []
            for q in range(S5_Q):
                sl = slice(q * S5_QL, (q + 1) * S5_QL)
                cat = lambda ref: jnp.concatenate([ref[q * S5_QT + jj] for jj in range(S5_QT)], axis=1).astype(BF16)
                grq, giq = cat(gre), cat(gim)
                dus.append(_dot_nt(grq, bre[q]) + _dot_nt(giq, bim[q]) + dy_v[:, sl] * d_ref[:, sl])
                dbre[q] += _dot_tn(ub[:, sl], grq)
                dbim[q] += _dot_tn(ub[:, sl], giq)
                dcre[q] += _dot_tn(cat(sre), dyb[:, sl])
                dcim[q] -= _dot_tn(cat(sim), dyb[:, sl])
            du_io.store(slot, jnp.concatenate(dus, axis=1))

        _seg_pipeline(i, nb, lambda st: nb - 1 - st, [u_io, dy_io], [du_io], compute)

    full = lambda a: pl.BlockSpec(a.shape, lambda i, nd=a.ndim: (0,) * nd)
    sh = jax.ShapeDtypeStruct
    outs = [sh((L, S5_WIDTH), F32), sh(bq_re.shape, F32), sh(bq_im.shape, F32), sh(cq_re.shape, F32), sh(cq_im.shape, F32),
            sh((8, S5_LANES), F32), sh((1, S5_WIDTH), F32)]
    fo = lambda s: pl.BlockSpec(s.shape, lambda i, nd=len(s.shape): (0,) * nd)
    st = pltpu.VMEM((S5_NT, TB, 128), F32)
    pw = pltpu.VMEM((S5_NT, seg, 8, 128), F32)
    io = pltpu.VMEM((2, seg, 8, S5_WIDTH), F32)
    sem = pltpu.SemaphoreType.DMA((2, 8))
    return _pcall(
        body, name="s5_scan_bwd", grid=(nb,),
        in_specs=[_ANY, _ANY, pl.BlockSpec((1, 8, S5_LANES), lambda i: (nb - 1 - i, 0, 0)),
                  full(bq_re), full(bq_im), full(cq_re), full(cq_im), full(lbar), full(dskip)],
        out_specs=[_ANY] + [fo(s) for s in outs[1:]],
        out_shape=outs,
        scratch_shapes=[st] * 4 + [pltpu.VMEM((8, S5_LANES), F32)] * 2 + [pw, pw, io, io, io, sem, sem, sem],
        compiler_params=_cparams(1))(proj, dy, ck, bq_re, bq_im, cq_re, cq_im, lbar, dskip)


N_HEAD = RW_WIDTH // HEAD
_NN = (((2,), (1,)), ((0,), (0,)))
_NT = (((2,), (2,)), ((0,), (0,)))
_TN = (((1,), (1,)), ((0,), (0,)))


def _hi_lo(x):
    h = x.astype(BF16)
    return h, (x - h.astype(F32)).astype(BF16)


def _mm_acc(a, b, dims, passes=3):
    dg = lambda p, q: lax.dot_general(p, q, dims, preferred_element_type=F32)
    if passes == 1:
        return dg(a.astype(BF16), b.astype(BF16))
    ah, al = _hi_lo(a)
    bh, bl = _hi_lo(b)
    return dg(ah, bh) + dg(ah, bl) + dg(al, bh)


def _cumsum_rows(x, transpose):
    h, n, _ = x.shape
    ti = lax.broadcasted_iota(jnp.int32, (h, n, n), 1)
    tj = lax.broadcasted_iota(jnp.int32, (h, n, n), 2)
    m = ((tj >= ti) if transpose else (tj <= ti)).astype(BF16)
    acc, rem = None, x
    for s in range(3):
        part = rem.astype(BF16)
        t = lax.dot_general(m, part, _NN, preferred_element_type=F32)
        acc = t if acc is None else acc + t
        if s < 2:
            rem = rem - part.astype(F32)
    return acc


def _slices(x, axis, sizes):
    out, off = [], 0
    for n in sizes:
        out.append(lax.slice_in_dim(x, off, off + n, axis=axis))
        off += n
    return tuple(out)


def _cat_op(axis, sizes, diff):
    plain = lambda *xs: jnp.concatenate(xs, axis=axis)
    if not diff:
        return plain
    f = jax.custom_vjp(plain)
    f.defvjp(lambda *xs: (plain(*xs), None), lambda _, d: _slices(d, axis, sizes))
    return f


def _split_op(axis, sizes, diff):
    plain = lambda x: _slices(x, axis, sizes)
    if not diff:
        return plain
    f = jax.custom_vjp(plain)
    f.defvjp(lambda x: (plain(x), None), lambda _, d: (jnp.concatenate(d, axis=axis),))
    return f


def _mm_ops(diff, passes):
    mm = lambda a, b, dims: _mm_acc(a, b, dims, passes)
    if not diff:
        return (lambda a, b: mm(a, b, _NN), lambda a, b: mm(a, b, _NT), lambda a, b: mm(a, b, _TN))

    @jax.custom_vjp
    def nn(a, b):
        return mm(a, b, _NN)

    nn.defvjp(lambda a, b: (mm(a, b, _NN), (a, b)), lambda r, d: (mm(d, r[1], _NT), mm(r[0], d, _TN)))

    @jax.custom_vjp
    def nt(a, b):
        return mm(a, b, _NT)

    nt.defvjp(lambda a, b: (mm(a, b, _NT), (a, b)), lambda r, d: (mm(d, r[1], _NN), mm(d, r[0], _TN)))

    @jax.custom_vjp
    def tn(a, b):
        return mm(a, b, _TN)

    tn.defvjp(lambda a, b: (mm(a, b, _TN), (a, b)), lambda r, d: (mm(r[1], d, _NT), mm(r[0], d, _NN)))
    return nn, nt, tn


def _cums_op(diff):
    if not diff:
        return lambda x: _cumsum_rows(x, False)

    @jax.custom_vjp
    def cums(x):
        return _cumsum_rows(x, False)

    cums.defvjp(lambda x: (_cumsum_rows(x, False), None), lambda _, d: (_cumsum_rows(d, True),))
    return cums


WKV_PASSES = (1, 1, 1, 1, 1)


WKV_SUB = 4
WKV_BLOCK = CHUNK * WKV_SUB


def _wkv_block(s0, r, w, k, v, a, b, diff):
    p_pair, p_val, p_solve, p_out, p_state = WKV_PASSES
    cums = _cums_op(diff)
    _, nt_pair, _ = _mm_ops(diff, p_pair)
    nn_val, _, _ = _mm_ops(diff, p_val)
    nn_solve, _, _ = _mm_ops(diff, p_solve)
    nn_out, nt_out, _ = _mm_ops(diff, p_out)
    nn_state, _, tn_state = _mm_ops(diff, p_state)
    h, d, n, sub = s0.shape[0], s0.shape[2], CHUNK, WKV_SUB
    hb = h * sub
    to_chunks = lambda t: _cat_op(0, (h,) * sub, diff)(*_split_op(1, (n,) * sub, diff)(t))
    r, w, k, v, a, b = (to_chunks(t) for t in (r, w, k, v, a, b))
    cat_rows2 = _cat_op(1, (n, n), diff)
    cat_lanes2 = _cat_op(2, (n, n), diff)
    split_rows2 = _split_op(1, (n, n), diff)
    split_lanes2 = _split_op(2, (n, n), diff)
    ti = lax.broadcasted_iota(jnp.int32, (hb, n, n), 1)
    tj = lax.broadcasted_iota(jnp.int32, (hb, n, n), 2)
    incl, strict = tj <= ti, tj < ti
    logw = jnp.log(w)
    cum = cums(logw)
    g_in, g_ex, g_inv = jnp.exp(cum), jnp.exp(cum - logw), jnp.exp(-cum)
    ae, re, bi, ki = a * g_ex, r * g_in, b * g_inv, k * g_inv
    top, bot = split_rows2(nt_pair(cat_rows2(ae, re), cat_rows2(bi, ki)))
    tab, tak = split_lanes2(top)
    qb, qk = split_lanes2(bot)
    tab, tak = jnp.where(strict, tab, 0.0), jnp.where(strict, tak, 0.0)
    qb, qk = jnp.where(incl, qb, 0.0), jnp.where(incl, qk, 0.0)
    tak_v, qk_v = split_rows2(nn_val(cat_rows2(tak, qk), v))
    x = cat_lanes2(ae, tak_v)
    npow = tab
    steps = max(1, (n - 1).bit_length())
    for i in range(steps):
        x = x + nn_solve(npow, x)
        if i + 1 < steps:
            npow = nn_solve(npow, npow)
    ae_m, uc = split_lanes2(x)
    qx = nn_out(qb, x)
    q_ae, q_uc = split_lanes2(qx)
    re_m = re + q_ae
    yc = q_uc + qk_v
    g_end = jnp.exp(jnp.sum(logw, axis=1, keepdims=True))
    bg, kg = bi * g_end, ki * g_end
    tm = tn_state(ae_m, bg)
    sc = tn_state(cat_rows2(uc, v), cat_rows2(bg, kg))
    per_chunk = _split_op(0, (h,) * sub, diff)
    re_m, yc, g_end, tm, sc = (per_chunk(t) for t in (re_m, yc, g_end, tm, sc))
    ys, s = [], s0
    for i in range(sub):
        ys.append(nt_out(re_m[i], s) + yc[i])
        s = s * g_end[i] + nn_state(s, tm[i]) + sc[i]
    return _cat_op(1, (n,) * sub, diff)(*ys), s


def _wkv_fwd(r, w, k, v, a, b, L):
    nc = L // WKV_BLOCK

    def body(r_ref, w_ref, k_ref, v_ref, a_ref, b_ref, y_ref, ck_ref, s_ref):
        c = pl.program_id(0)

        @pl.when(c == 0)
        def _():
            s_ref[...] = jnp.zeros(s_ref.shape, F32)

        s0 = s_ref[...]
        ck_ref[0] = s0
        y, s1 = _wkv_block(s0, r_ref[...], w_ref[...], k_ref[...], v_ref[...], a_ref[...], b_ref[...], False)
        y_ref[...] = y
        s_ref[...] = s1

    blk = pl.BlockSpec((N_HEAD, WKV_BLOCK, HEAD), lambda c: (0, c, 0))
    return _pcall(
        body, name="wkv_fwd", grid=(nc,), in_specs=[blk] * 6,
        out_specs=[blk, pl.BlockSpec((1, N_HEAD, HEAD, HEAD), lambda c: (c, 0, 0, 0))],
        out_shape=[jax.ShapeDtypeStruct((N_HEAD, L, HEAD), F32), jax.ShapeDtypeStruct((nc, N_HEAD, HEAD, HEAD), F32)],
        scratch_shapes=[pltpu.VMEM((N_HEAD, HEAD, HEAD), F32)],
        compiler_params=_cparams(1))(r, w, k, v, a, b)


def _wkv_bwd(r, w, k, v, a, b, dy, ck, L, deps=()):
    nc = L // WKV_BLOCK

    def body(r_ref, w_ref, k_ref, v_ref, a_ref, b_ref, dy_ref, ck_ref, *rest):
        dr_ref, dw_ref, dk_ref, dv_ref, da_ref, db_ref, ds_ref = rest[len(deps):]
        c = pl.program_id(0)

        @pl.when(c == 0)
        def _():
            ds_ref[...] = jnp.zeros(ds_ref.shape, F32)

        _, vjp = jax.vjp(lambda *t: _wkv_block(*t, True), ck_ref[0], r_ref[...], w_ref[...], k_ref[...], v_ref[...],
                         a_ref[...], b_ref[...])
        g = vjp((dy_ref[...], ds_ref[...]))
        ds_ref[...] = g[0]
        for o_ref, val in zip((dr_ref, dw_ref, dk_ref, dv_ref, da_ref, db_ref), g[1:]):
            o_ref[...] = val

    blk = pl.BlockSpec((N_HEAD, WKV_BLOCK, HEAD), lambda c: (0, nc - 1 - c, 0))
    sh = jax.ShapeDtypeStruct((N_HEAD, L, HEAD), F32)
    return _pcall(
        body, name="wkv_bwd", grid=(nc,),
        in_specs=[blk] * 7 + [pl.BlockSpec((1, N_HEAD, HEAD, HEAD), lambda c: (nc - 1 - c, 0, 0, 0))]
        + [pl.BlockSpec(d.shape, lambda c, nd=d.ndim: (0,) * nd) for d in deps],
        out_specs=[blk] * 6, out_shape=[sh] * 6,
        scratch_shapes=[pltpu.VMEM((N_HEAD, HEAD, HEAD), F32)],
        compiler_params=_cparams(1))(r, w, k, v, a, b, dy, ck, *deps)


TB = 256


def _bf(x):
    return x.astype(BF16)


def _inproj_fwd(x, norm_mix, w_in, L, deps=()):
    def fn(i, tv, cv):
        xn = _rms(tv[0], cv[0])
        return _dot(_bf(xn), cv[1]), xn

    return _tok_call("inproj_fwd", fn, L, TB, [(x, D_MODEL, 0)], [norm_mix, w_in], [(IN_COLS, F32), (D_MODEL, BF16)],
                     deps=deps)


def _s5_post_fn(glu_w, wtop, diff=True):
    mg = _mmc(glu_w, diff)
    mt = _mmc(wtop, diff) if wtop is not None else None

    def f(y, glu_b, e):
        z = _gelu(y)
        out = z * _sigmoid(mg(z) + glu_b + e)
        res = mt(out) if mt is not None else out
        return res, (z, out)

    return f


def _s5_post_fwd(y, glu_w, glu_b, L):
    def fn(i, tv, cv):
        out, _ = _s5_post_fn(cv[0], None, False)(tv[0], cv[1], 0.0)
        return (out,)

    return _tok_call("s5_post_fwd", fn, L, TB, [(y, S5_WIDTH, 0)], [glu_w, glu_b], [(S5_WIDTH, F32)])[0]


def _s5_post_bwd(y, dh1, glu_w, glu_b, wtop, L, deps=()):
    def fn(i, tv, cv):
        e0 = jnp.zeros((TB, S5_WIDTH), F32)
        _, vjp, (z, out) = jax.vjp(_s5_post_fn(cv[0], cv[2]), tv[0], cv[1], e0, has_aux=True)
        dy, db, de = vjp(tv[1])
        return dy, db, _dot_tn(_bf(z), _bf(de)), _dot_tn(_bf(out), _bf(tv[1]))

    return _tok_call("s5_post_bwd", fn, L, TB, [(y, S5_WIDTH, 0), (dh1, D_MODEL, 0)], [glu_w, glu_b, wtop],
                     [(S5_WIDTH, F32)], [(1, S5_WIDTH), (S5_WIDTH, S5_WIDTH), (S5_WIDTH, D_MODEL)], deps=deps)


RW_COLBLK = ((RW_WIDTH, 1), (RW_WIDTH, 2), (RW_WIDTH, 3), (128, 16), (128, 17))
RW_MU = ((0, 512), (512, 1024), (1024, 1536), (1536, 1664), (1664, 1792))


def _rw_pre_fn(w2pad, a2pad, g2, diff=True):
    m_w, m_a, m_g = _mmc(w2pad, diff), _mmc(a2pad, diff), _mmc(g2, diff)
    seg = _segsum(_head_indicator(RW_WIDTH), diff)

    def f(zr, zk, zv, zwa, zg, w0, a0, k_k, k_a, e_w, e_a):
        wl_t = jnp.tanh(zwa)
        wlin = w0 + m_w(wl_t) + e_w
        w = -_softplus(-wlin) - 0.5
        decay = jnp.exp(-jnp.exp(w))
        a = _sigmoid(a0 + m_a(zwa) + e_a)
        sg = _sigmoid(zg)
        g = m_g(sg)
        kk = zk * k_k
        kkn = kk / jnp.maximum(jnp.sqrt(seg(kk * kk)), L2_EPS)
        kf = zk * (1.0 + (a - 1.0) * k_a)
        return (zr, decay, kf, zv, -kkn, kkn * a, g), (wl_t, sg)

    return f


def _rw_shifted(i, tv, mu):
    sub = lax.broadcasted_iota(jnp.int32, (TB, 1), 0)
    zs, dif = [], []
    for n in range(5):
        z = tv[n]
        last = jnp.where(i == 0, 0.0, tv[5 + n][7:8, :])
        prev = jnp.where(sub == 0, last, pltpu.roll(z, 1, 0))
        m = mu[:, RW_MU[n][0]:RW_MU[n][1]]
        zs.append(z + (prev - z) * m)
        dif.append(prev - z)
    return zs, dif


def _rw_tok_in(proj):
    return [(proj, wd, cb) for wd, cb in RW_COLBLK] + [(proj, wd, cb, "prev") for wd, cb in RW_COLBLK]


def _rw_pre_fwd(proj, mu, w0, a0, k_k, k_a, w2pad, a2pad, g2, L):
    def fn(i, tv, cv):
        zs, _ = _rw_shifted(i, tv, cv[0])
        outs, _ = _rw_pre_fn(cv[5], cv[6], cv[7], False)(*zs, cv[1], cv[2], cv[3], cv[4], 0.0, 0.0)
        return outs

    return _tok_call("rw_pre_fwd", fn, L, TB, _rw_tok_in(proj), [mu, w0, a0, k_k, k_a, w2pad, a2pad, g2],
                     [("heads", F32)] * 6 + [(RW_WIDTH, F32)])


def _rw_pre_bwd(proj, cots, mu, w0, a0, k_k, k_a, w2pad, a2pad, g2, L):
    def fn(i, tv, cv):
        zs, dif = _rw_shifted(i, tv[:10], cv[0])
        dr1, dr2, dw, dk1, dk2, dv1, dv2, da, db, dg = tv[10:]
        e0 = jnp.zeros((TB, RW_WIDTH), F32)
        _, vjp, (wl_t, sg) = jax.vjp(_rw_pre_fn(cv[5], cv[6], cv[7]), *zs, cv[1], cv[2], cv[3], cv[4], e0, e0, has_aux=True)
        g = vjp((dr1 + dr2, dw, dk1 + dk2, dv1 + dv2, da, db, dg))
        dzs = jnp.concatenate(g[:5], axis=1)
        dmu = jnp.concatenate([jnp.sum(g[n] * dif[n], axis=0, keepdims=True) for n in range(5)], axis=1)
        lora = (_dot_tn(_bf(wl_t), _bf(g[9])), _dot_tn(_bf(zs[3]), _bf(g[10])), _dot_tn(_bf(sg), _bf(dg)))
        return (dzs, dmu, g[5], g[6], g[7], g[8]) + lora

    tok_in = _rw_tok_in(proj) + [((c,) if c.ndim == 3 else (c, RW_WIDTH, 0)) for c in cots]
    return _tok_call("rw_pre_bwd", fn, L, TB, tok_in, [mu, w0, a0, k_k, k_a, w2pad, a2pad, g2],
                     [(SHIFT_COLS, F32)], [(1, SHIFT_COLS)] + [(1, RW_WIDTH)] * 4 + [(128, RW_WIDTH)] * 3)


def _rw_post_fn(wbot, diff=True):
    seg = _segsum(_head_indicator(RW_WIDTH), diff)
    mb = _mmc(wbot, diff) if wbot is not None else None

    def f(y, r, kf, v, g, ln_w, ln_b, r_k):
        mean = seg(y) * (1.0 / HEAD)
        yc = y - mean
        var = seg(yc * yc) * (1.0 / HEAD)
        yn = yc * lax.rsqrt(var + GN_EPS) * ln_w + ln_b
        bonus = seg(r * kf * r_k) * v
        out = (yn + bonus) * g
        res = mb(out) if mb is not None else out
        return res, out

    return f


def _rw_post_fwd(y, r, kf, v, g, ln_w, ln_b, r_k, L):
    def fn(i, tv, cv):
        out, _ = _rw_post_fn(None, False)(*tv, *cv)
        return (out,)

    return _tok_call("rw_post_fwd", fn, L, TB, [(t,) for t in (y, r, kf, v)] + [(g, RW_WIDTH, 0)], [ln_w, ln_b, r_k],
                     [(RW_WIDTH, F32)])[0]


def _rw_post_bwd(y, r, kf, v, g, dh1, ln_w, ln_b, r_k, wbot, L):
    def fn(i, tv, cv):
        _, vjp, out = jax.vjp(_rw_post_fn(cv[3]), *tv[:5], cv[0], cv[1], cv[2], has_aux=True)
        gr = vjp(tv[5])
        return gr[0], gr[1], gr[2], gr[3], gr[4], gr[5], gr[6], gr[7], _dot_tn(_bf(out), _bf(tv[5]))

    return _tok_call("rw_post_bwd", fn, L, TB, [(t,) for t in (y, r, kf, v)] + [(g, RW_WIDTH, 0), (dh1, D_MODEL, 0)],
                     [ln_w, ln_b, r_k, wbot], [("heads", F32)] + [(RW_WIDTH, F32)] * 4,
                     [(1, RW_WIDTH)] * 3 + [(RW_WIDTH, D_MODEL)])


def _ffn_fn(w1, w3, w2, diff=True):
    m1, m3, m2 = _mmc(w1, diff), _mmc(w3, diff), _mmc(w2, diff)

    def f(h1, norm_ffn, e1, e3):
        hn = _rms(h1, norm_ffn)
        a1 = m1(hn) + e1
        a3 = m3(hn) + e3
        hm = a1 * _sigmoid(a1) * a3
        return h1 + m2(hm), (hn, hm)

    return f


TB_FFN = 256


def _mixffn_fwd(x, s5_out, rw_out, wtop, wbot, norm_ffn, w1, w3, w2, L):
    def fn(i, tv, cv):
        h1 = tv[0] + _dot(_bf(tv[1]), cv[0]) + _dot(_bf(tv[2]), cv[1])
        h2, _ = _ffn_fn(cv[3], cv[4], cv[5], False)(h1, cv[2], 0.0, 0.0)
        return h1, h2

    return _tok_call("mixffn_fwd", fn, L, TB_FFN, [(x, D_MODEL, 0), (s5_out, S5_WIDTH, 0), (rw_out, RW_WIDTH, 0)],
                     [wtop, wbot, norm_ffn, w1, w3, w2], [(D_MODEL, F32), (D_MODEL, F32)])


def _ffn_bwd(h1, dh2, norm_ffn, w1, w3, w2, L):
    def fn(i, tv, cv):
        e0 = jnp.zeros((TB_FFN, FFN_HIDDEN), F32)
        _, vjp, (hn, hm) = jax.vjp(_ffn_fn(cv[1], cv[2], cv[3]), tv[0], cv[0], e0, e0, has_aux=True)
        dh1, dn, d1, d3 = vjp(tv[1])
        return dh1, d1, d3, hm, hn, dn

    return _tok_call("ffn_bwd", fn, L, TB_FFN, [(h1, D_MODEL, 0), (dh2, D_MODEL, 0)], [norm_ffn, w1, w3, w2],
                     [(D_MODEL, F32), (FFN_HIDDEN, BF16), (FFN_HIDDEN, BF16), (FFN_HIDDEN, BF16), (D_MODEL, BF16)],
                     [(1, D_MODEL)])


def _ple_loss_fb(h2, p, target, norm_ple, final_norm, wg, wu, L):
    def fn(i, tv, cv):
        mgate, mup = _mmc(cv[2]), _mmc(cv[3], False)

        def f(h2_, norm_ple_, final_norm_, eg, eu):
            hn = _rms(h2_, norm_ple_)
            gate = _sigmoid(mgate(hn) + eg)
            h3 = h2_ + gate * (mup(tv[1]) + eu)
            out = _rms(h3, final_norm_)
            d = out - tv[2]
            return 0.5 * jnp.sum(jnp.mean(d * d, axis=-1, keepdims=True)), hn

        e0 = jnp.zeros((TB, D_MODEL), F32)
        loss, vjp, hn = jax.vjp(f, tv[0], cv[0], cv[1], e0, e0, has_aux=True)
        dh2, dnp, dfn, deg, deu = vjp(jnp.ones((), F32))
        return (dh2, dh2, jnp.full((8, 128), loss, F32), dnp, dfn,
                _dot_tn(_bf(hn), _bf(deg)), _dot_tn(_bf(tv[1]), _bf(deu)))

    return _tok_call("ple_loss_fb", fn, L, TB, [(h2, D_MODEL, 0), (p, PLE_DIM, 0), (target, D_MODEL, 0)],
                     [norm_ple, final_norm, wg, wu], [(D_MODEL, F32), (D_MODEL, BF16)],
                     [(8, 128), (1, D_MODEL), (1, D_MODEL), (D_MODEL, D_MODEL), (PLE_DIM, D_MODEL)])


def _inproj_bwd(x, dh1, du, dzs, norm_mix, mu, w_u, w_z, L):
    nb = L // TB

    def fn(i, tv, cv):
        sub = lax.broadcasted_iota(jnp.int32, (TB, 1), 0)
        m = cv[1]
        b = tv[3] * m
        nxt = jnp.where(i == nb - 1, 0.0, tv[4][0:1, :] * m)
        dz = tv[3] * (1.0 - m) + jnp.where(sub == TB - 1, nxt, pltpu.roll(b, TB - 1, 0))
        dub, dzb = _bf(tv[2]), _bf(dz)
        dxn = _dot_nt(dub, cv[2]) + _dot_nt(dzb, cv[3])
        _, vjp = jax.vjp(_rms, tv[0], cv[0])
        dx, dn = vjp(dxn)
        return tv[1] + dx, jnp.concatenate([dub, dzb], axis=1), dn

    return _tok_call("inproj_bwd", fn, L, TB,
                     [(x, D_MODEL, 0), (dh1, D_MODEL, 0), (du, S5_WIDTH, 0), (dzs, SHIFT_COLS, 0), (dzs, SHIFT_COLS, 0, "next")],
                     [norm_mix, mu, w_u, w_z], [(D_MODEL, F32), (IN_COLS, BF16)], [(1, D_MODEL)])


def _eye8(dt):
    return jnp.eye(8, dtype=dt)


def _quarter_b(bb):
    return jnp.einsum("hg,qgcp->qhcgp", _eye8(bb.dtype), bb.reshape(S5_Q, 8, S5_GROUP, S5_STATE)).reshape(S5_Q, S5_QL, S5_QS)


def _unquarter_b(d):
    return jnp.einsum("qhcgp,hg->qgcp", d.reshape(S5_Q, 8, S5_GROUP, 8, S5_STATE), _eye8(d.dtype)).reshape(
        S5_GROUPS, S5_GROUP, S5_STATE)


def _quarter_c(c):
    return jnp.einsum("gh,qgcp->qgphc", _eye8(c.dtype), c.reshape(S5_Q, 8, S5_GROUP, S5_STATE)).reshape(S5_Q, S5_QS, S5_QL)


def _unquarter_c(d):
    return jnp.einsum("qgphc,gh->qgcp", d.reshape(S5_Q, 8, S5_STATE, 8, S5_GROUP), _eye8(d.dtype)).reshape(
        S5_GROUPS, S5_GROUP, S5_STATE)


def _local_step(x, p, target, W, late_weights=None, grads_ready=None, first_dep=None):
    L = x.shape[0]
    r2 = lambda v: v.reshape(1, -1)
    w_in = W["w_in"]
    w2pad = jnp.pad(W["rw_w2"], ((0, 64), (0, 0)))
    a2pad = jnp.pad(W["rw_a2"], ((64, 0), (0, 0)))
    mu = r2(W["rw_shift_mu"])
    rw_vec = [r2(W[n]) for n in ("rw_w0", "rw_a0", "rw_k_k", "rw_k_a")]
    ln_w, ln_b, r_k = r2(W["rw_ln_w"]), r2(W["rw_ln_b"]), r2(W["rw_r_k"])

    lam_re, lam_im = W["s5_lam_re"], W["s5_lam_im"]
    log_step = W["s5_log_step"].reshape(S5_GROUPS, 1)
    bt_re, bt_im = W["s5_b_re"].transpose(0, 2, 1), W["s5_b_im"].transpose(0, 2, 1)
    lb_re, lb_im, bb_re, bb_im = _s5_param_fwd(lam_re, lam_im, log_step, bt_re, bt_im)
    bq_re, bq_im = _quarter_b(bb_re).astype(BF16), _quarter_b(bb_im).astype(BF16)
    cq_re, cq_im = _quarter_c(W["s5_c_re"]).astype(BF16), _quarter_c(W["s5_c_im"]).astype(BF16)
    lbar = jnp.concatenate([lb_re.reshape(1, -1), lb_im.reshape(1, -1), jnp.zeros((6, S5_LANES), F32)], axis=0)
    dskip = r2(W["s5_d"])
    glu_b = r2(W["s5_glu_b"])
    norm_mix, norm_ffn, norm_ple, final_norm = (r2(W[n]) for n in ("norm_mix", "norm_ffn", "norm_ple", "final_norm"))

    proj, xn = _inproj_fwd(x, norm_mix, w_in, L, () if first_dep is None else (first_dep,))
    y_s5, ck5 = _s5_scan_fwd(proj, bq_re, bq_im, cq_re, cq_im, lbar, dskip, L, TB)
    s5_out = _s5_post_fwd(y_s5, W["s5_glu_w"], glu_b, L)
    r, wd, kf, v, a_s, b_s, g = _rw_pre_fwd(proj, mu, *rw_vec, w2pad, a2pad, W["rw_g2"], L)
    scan_in = (r, wd, kf, v, a_s, b_s)
    y_wkv, ckw = _wkv_fwd(*scan_in, L)
    rw_out = _rw_post_fwd(y_wkv, r, kf, v, g, ln_w, ln_b, r_k, L)
    if late_weights is not None:
        W = dict(W, **late_weights(rw_out))
    wtop, wbot = W["w_out"][:S5_WIDTH], W["w_out"][S5_WIDTH:]
    h1, h2 = _mixffn_fwd(x, s5_out, rw_out, wtop, wbot, norm_ffn, W["ffn_w1"], W["ffn_w3"], W["ffn_w2"], L)

    G = {}
    dh2, dh2_bf, loss_acc, G["norm_ple"], G["final_norm"], G["ple_gate_w"], G["ple_up_w"] = _ple_loss_fb(
        h2, p, target, norm_ple, final_norm, W["ple_gate_w"], W["ple_up_w"], L)
    dh1, da1, da3, hm, hn_ffn, G["norm_ffn"] = _ffn_bwd(h1, dh2, norm_ffn, W["ffn_w1"], W["ffn_w3"], W["ffn_w2"], L)
    G["ffn_w1"] = _mm_tn("dw_ffn_w1", hn_ffn, da1)
    G["ffn_w3"] = _mm_tn("dw_ffn_w3", hn_ffn, da3)
    G["ffn_w2"] = _mm_tn("dw_ffn_w2", hm, dh2_bf)
    dep_a = grads_ready(0, G) if grads_ready is not None else None
    dy_s5, G["s5_glu_b"], G["s5_glu_w"], d_wtop = _s5_post_bwd(y_s5, dh1, W["s5_glu_w"], glu_b, wtop, L,
                                                               () if dep_a is None else (dep_a,))
    dy_wkv, dr2, dk2, dv2, dg, G["rw_ln_w"], G["rw_ln_b"], G["rw_r_k"], d_wbot = _rw_post_bwd(
        y_wkv, r, kf, v, g, dh1, ln_w, ln_b, r_k, wbot, L)
    G["w_out"] = jnp.concatenate([d_wtop, d_wbot], axis=0)
    dep = grads_ready(1, G) if grads_ready is not None else None
    dr1, dwd, dk1, dv1, da_s, db_s = _wkv_bwd(*scan_in, dy_wkv, ckw, L, () if dep is None else (dep,))
    (dzs, G["rw_shift_mu"], G["rw_w0"], G["rw_a0"], G["rw_k_k"], G["rw_k_a"], d_w2pad, d_a2pad, G["rw_g2"]) = _rw_pre_bwd(
        proj, (dr1, dr2, dwd, dk1, dk2, dv1, dv2, da_s, db_s, dg), mu, *rw_vec, w2pad, a2pad, W["rw_g2"], L)
    G["rw_w2"], G["rw_a2"] = d_w2pad[:64], d_a2pad[64:]
    du, dbq_re, dbq_im, dcq_re, dcq_im, dlbar, G["s5_d"] = _s5_scan_bwd(
        proj, dy_s5, ck5, bq_re, bq_im, cq_re, cq_im, lbar, dskip, L, TB)
    G["s5_c_re"], G["s5_c_im"] = _unquarter_c(dcq_re), _unquarter_c(dcq_im)
    d_lam_re, d_lam_im, d_ls, d_bt_re, d_bt_im = _s5_param_bwd(
        lam_re, lam_im, log_step, bt_re, bt_im, dlbar[0].reshape(S5_GROUPS, S5_STATE), dlbar[1].reshape(S5_GROUPS, S5_STATE),
        _unquarter_b(dbq_re), _unquarter_b(dbq_im))
    G["s5_lam_re"], G["s5_lam_im"], G["s5_log_step"] = d_lam_re, d_lam_im, d_ls.reshape(S5_GROUPS)
    G["s5_b_re"], G["s5_b_im"] = d_bt_re.transpose(0, 2, 1), d_bt_im.transpose(0, 2, 1)
    dx, dproj, G["norm_mix"] = _inproj_bwd(x, dh1, du, dzs, norm_mix, mu, w_in[:, :S5_WIDTH], w_in[:, S5_WIDTH:], L)
    G["w_in"] = _mm_tn("dw_in", xn, dproj)
    return loss_acc[0, 0], dx, G


MESH_AXES = ("x", "y", "c")


def _all_gather(name, shards):
    nt = len(shards)

    def body(*refs):
        x_refs, out_refs = refs[:nt], refs[nt:2 * nt]
        send_sems, recv_sems, local_sems = refs[2 * nt:]
        x, y, c = lax.axis_index("x"), lax.axis_index("y"), lax.axis_index("c")
        me, sibling = (x, y, c), (x, y, 1 - c)
        chips = [(1 - x, y), (x, 1 - y), (1 - x, 1 - y)]

        def rows(t, px, py, pc):
            m_per = shards[t].shape[0]
            return out_refs[t].at[pl.ds((4 * px + 2 * py + pc) * m_per, m_per), :]

        def copy(t, k, block, to, src=None):
            return pltpu.make_async_remote_copy(
                src_ref=rows(t, *block) if src is None else src, dst_ref=rows(t, *block),
                send_sem=send_sems.at[7 * t + k], recv_sem=recv_sems.at[7 * t + k],
                device_id=to, device_id_type=pl.DeviceIdType.MESH)

        mine = [pltpu.make_async_copy(x_refs[t], rows(t, *me), local_sems.at[t]) for t in range(nt)]
        for cp in mine:
            cp.start()
        first = []
        for t in range(nt):
            first.append(copy(t, 0, me, sibling, src=x_refs[t]))
            first += [copy(t, 1 + j, me, (*chip, c), src=x_refs[t]) for j, chip in enumerate(chips)]
        for cp in first:
            cp.start()
        passed = []
        for t in range(nt):
            for j, chip in enumerate(chips):
                copy(t, 1 + j, (*chip, c), me).wait_recv()
                fwd = copy(t, 4 + j, (*chip, c), sibling)
                fwd.start()
                passed.append(fwd)
        for t in range(nt):
            copy(t, 0, sibling, me).wait_recv()
            for j, chip in enumerate(chips):
                copy(t, 4 + j, (*chip, 1 - c), me).wait_recv()
        for cp in first + passed:
            cp.wait_send()
        for cp in mine:
            cp.wait()

    return _pcall(body, name=name,
                  out_shape=[jax.ShapeDtypeStruct((N_DEV * a.shape[0], a.shape[1]), a.dtype) for a in shards],
                  in_specs=[_ANY] * nt, out_specs=[_ANY] * nt,
                  scratch_shapes=[pltpu.SemaphoreType.DMA((7 * nt,)), pltpu.SemaphoreType.DMA((7 * nt,)),
                                  pltpu.SemaphoreType.DMA((nt,))])(*shards)


_HBM = pl.BlockSpec(memory_space=pltpu.HBM)
_SEM = pl.BlockSpec(memory_space=pltpu.SEMAPHORE)
_EFFECT = pltpu.SideEffectType.DATAFLOW_SIDE_EFFECTING


def _peer_of(k):
    x, y, c = lax.axis_index("x"), lax.axis_index("y"), lax.axis_index("c")
    px, py, pc = x ^ ((k >> 2) & 1), y ^ ((k >> 1) & 1), c ^ (k & 1)
    return (px, py, pc), 4 * px + 2 * py + pc, 4 * x + 2 * y + c


def _direct_copy(t, k, src_refs, land_refs, send_sems, recv_sems, rows_of, gather):
    dev, peer, me = _peer_of(k)
    m = rows_of[t]
    src = src_refs[t] if gather else src_refs[t].at[pl.ds(peer * m, m), :]
    return pltpu.make_async_remote_copy(
        src_ref=src, dst_ref=land_refs[t].at[pl.ds(me * m, m), :],
        send_sem=send_sems.at[7 * t + k - 1], recv_sem=recv_sems.at[7 * t + k - 1],
        device_id=dev, device_id_type=pl.DeviceIdType.MESH)


def _direct_landing(t, k, src_refs, land_refs, send_sems, recv_sems, rows_of, gather):
    dev, peer, me = _peer_of(k)
    m = rows_of[t]
    src = src_refs[t] if gather else src_refs[t].at[pl.ds(me * m, m), :]
    return pltpu.make_async_remote_copy(
        src_ref=src, dst_ref=land_refs[t].at[pl.ds(peer * m, m), :],
        send_sem=send_sems.at[7 * t + k - 1], recv_sem=recv_sems.at[7 * t + k - 1],
        device_id=dev, device_id_type=pl.DeviceIdType.MESH)


def _direct_start(name, srcs, gather, dep=None):
    nt = len(srcs)
    rows_of = [a.shape[0] if gather else a.shape[0] // N_DEV for a in srcs]
    lands = [pltpu.with_memory_space_constraint(lax.empty((N_DEV * m, a.shape[1]), a.dtype), pltpu.HBM)
             for a, m in zip(srcs, rows_of)]

    n_dep = 0 if dep is None else 1

    def body(*refs):
        src_refs, land_refs = refs[:nt], refs[nt:2 * nt]
        send_sems, recv_sems = refs[2 * nt + n_dep], refs[2 * nt + n_dep + 1]
        token = refs[-1]
        for t in range(nt):
            for k in range(1, N_DEV):
                _direct_copy(t, k, src_refs, land_refs, send_sems, recv_sems, rows_of, gather).start()
        token[...] = jnp.zeros(token.shape, F32)

    out = _pcall(
        body, name=name,
        out_shape=(pltpu.SemaphoreType.DMA((7 * nt,)), pltpu.SemaphoreType.DMA((7 * nt,)),
                   *[pltpu.HBM(a.shape, a.dtype) for a in srcs], *[pltpu.HBM(a.shape, a.dtype) for a in lands],
                   jax.ShapeDtypeStruct((8, 128), F32)),
        in_specs=(_HBM,) * (2 * nt) + (pl.BlockSpec(memory_space=pl.ANY),) * n_dep,
        out_specs=(_SEM, _SEM) + (_HBM,) * (2 * nt) + (pl.BlockSpec(memory_space=pltpu.VMEM),),
        input_output_aliases={i: 2 + i for i in range(2 * nt)},
        compiler_params=pltpu.CompilerParams(has_side_effects=_EFFECT),
    )(*[pltpu.with_memory_space_constraint(a, pltpu.HBM) for a in srcs], *lands, *(() if dep is None else (dep,)))
    return (out[0], out[1], list(out[2:2 + nt]), list(out[2 + nt:2 + 2 * nt]), rows_of, gather), out[-1]


def _direct_wait(name, handle, after):
    send_sems, recv_sems, srcs, lands, rows_of, gather = handle
    nt = len(srcs)
    after = list(after) if isinstance(after, (list, tuple)) else [after]

    def body(*refs):
        src_refs, land_refs = refs[:nt], refs[nt:2 * nt]
        s_sems, r_sems = refs[2 * nt], refs[2 * nt + 1]
        for t in range(nt):
            for k in range(1, N_DEV):
                _direct_copy(t, k, src_refs, land_refs, s_sems, r_sems, rows_of, gather).wait_send()
                _direct_landing(t, k, src_refs, land_refs, s_sems, r_sems, rows_of, gather).wait_recv()

    out = _pcall(
        body, name=name,
        out_shape=tuple(pltpu.HBM(a.shape, a.dtype) for a in srcs) + tuple(pltpu.HBM(a.shape, a.dtype) for a in lands),
        in_specs=(_HBM,) * (2 * nt) + (_SEM, _SEM) + (pl.BlockSpec(memory_space=pl.ANY),) * len(after),
        out_specs=(_HBM,) * (2 * nt),
        input_output_aliases={i: i for i in range(2 * nt)},
        compiler_params=pltpu.CompilerParams(has_side_effects=_EFFECT),
    )(*srcs, *lands, send_sems, recv_sems, *after)
    return list(out[:nt]), list(out[nt:])


def _adamw_sharded(name, own, parts, w, m, v, rb, deps=()):
    R, N = own.shape

    def body(o_ref, p_ref, w_ref, m_ref, v_ref, *rest):
        g_ref, d_ref, nm_ref, nv_ref = rest[len(deps):]
        me = 4 * lax.axis_index("x") + 2 * lax.axis_index("y") + lax.axis_index("c")
        g = o_ref[...]
        for k in range(1, N_DEV):
            g = g + p_ref[me ^ k].astype(F32)
        nm = ADAM_B1 * m_ref[...] + (1.0 - ADAM_B1) * g
        nv = ADAM_B2 * v_ref[...] + (1.0 - ADAM_B2) * (g * g)
        m_hat = nm / (1.0 - ADAM_B1 ** ADAM_STEP)
        v_hat = nv / (1.0 - ADAM_B2 ** ADAM_STEP)
        g_ref[...] = g
        d_ref[...] = -ADAM_LR * (m_hat / (jnp.sqrt(v_hat) + ADAM_EPS) + ADAM_WD * w_ref[...])
        nm_ref[...] = nm
        nv_ref[...] = nv

    blk = pl.BlockSpec((rb, N), lambda i: (i, 0))
    sh = jax.ShapeDtypeStruct((R, N), F32)
    return _pcall(body, name=name, grid=(R // rb,),
                  in_specs=[blk, pl.BlockSpec((N_DEV, rb, N), lambda i: (0, i, 0)), blk, blk, blk]
                  + [pl.BlockSpec(d.shape, lambda i, nd=d.ndim: (0,) * nd) for d in deps],
                  out_specs=[blk] * 4, out_shape=[sh] * 4, compiler_params=_cparams(1))(own, parts, w, m, v, *deps)


SMALL_CLASSES = (
    (("s5_b_re", 32, 1024), ("s5_b_im", 32, 1024),
     ("norm_mix", 1, 1024), ("norm_ffn", 1, 1024), ("norm_ple", 1, 1024), ("final_norm", 1, 1024)),
    (("s5_d", 1, 512), ("s5_glu_b", 1, 512), ("rw_w0", 1, 512), ("rw_a0", 1, 512), ("rw_k_k", 1, 512), ("rw_k_a", 1, 512),
     ("rw_ln_w", 1, 512), ("rw_ln_b", 1, 512), ("rw_r_k", 1, 512)),
    (("rw_shift_mu", 1, 1792),),
    (("s5_lam_re", 32, 64), ("s5_lam_im", 32, 64), ("s5_c_re", 512, 64), ("s5_c_im", 512, 64)),
    (("s5_log_step", 1, 32),),
)


def _class_rows(cls):
    return -(-sum(r for _, r, _ in cls) // 8) * 8


def _stack_class(cls, arrs):
    a = jnp.concatenate(arrs, axis=0) if len(arrs) > 1 else arrs[0]
    pad = _class_rows(cls) - a.shape[0]
    return jnp.pad(a, ((0, pad), (0, 0))) if pad else a


def _adamw_small(grads, w, m, v):
    names = [n for cls in SMALL_CLASSES for n, _, _ in cls]
    n_cls, n_par = len(SMALL_CLASSES), len(names)

    def body(*refs):
        g_refs = refs[:n_cls]
        w_refs, m_refs, v_refs = (refs[n_cls + i * n_par:n_cls + (i + 1) * n_par] for i in range(3))
        o_refs = refs[n_cls + 3 * n_par:]
        p = 0
        for cls, g_ref in zip(SMALL_CLASSES, g_refs):
            rc = _class_rows(cls)
            tot = g_ref[0:rc, :]
            for s_ in range(1, N_DEV):
                tot = tot + g_ref[s_ * rc:(s_ + 1) * rc, :]
            off = 0
            for _, r, _ in cls:
                g = tot[off:off + r, :]
                off += r
                nm = ADAM_B1 * m_refs[p][...] + (1.0 - ADAM_B1) * g
                nv = ADAM_B2 * v_refs[p][...] + (1.0 - ADAM_B2) * (g * g)
                m_hat = nm / (1.0 - ADAM_B1 ** ADAM_STEP)
                v_hat = nv / (1.0 - ADAM_B2 ** ADAM_STEP)
                o_refs[4 * p][...] = g
                o_refs[4 * p + 1][...] = -ADAM_LR * (m_hat / (jnp.sqrt(v_hat) + ADAM_EPS) + ADAM_WD * w_refs[p][...])
                o_refs[4 * p + 2][...] = nm
                o_refs[4 * p + 3][...] = nv
                p += 1

    shapes = [(r, c) for cls in SMALL_CLASSES for _, r, c in cls]
    out = _pcall(body, name="adamw_replicated",
                 out_shape=[jax.ShapeDtypeStruct(sh, F32) for sh in shapes for _ in range(4)],
                 compiler_params=pltpu.CompilerParams(vmem_limit_bytes=VMEM_LIMIT))(*grads, *w, *m, *v)
    return {n: out[4 * i:4 * i + 4] for i, n in enumerate(names)}


EARLY = (("w_in", True),)
LATE = (("ffn_w1", True), ("ffn_w3", True), ("ffn_w2", False), ("ple_gate_w", False), ("w_out", False))
GRAD_STAGES = (LATE[:4], LATE[4:])
MISC = (("s5_glu_w", False), ("rw_w2", True), ("rw_a2", True), ("rw_g2", True), ("ple_up_w", True))
SHARDED_NAMES = tuple(n for n, _ in EARLY + LATE + MISC)
PACK_COLS = 1024
WEIGHT_NAMES = ("norm_mix", "w_in", "s5_lam_re", "s5_lam_im", "s5_log_step", "s5_b_re", "s5_b_im", "s5_c_re", "s5_c_im", "s5_d",
                "s5_glu_w", "s5_glu_b", "rw_shift_mu", "rw_w0", "rw_w2", "rw_a0", "rw_a2", "rw_g2", "rw_k_k", "rw_k_a", "rw_r_k",
                "rw_ln_w", "rw_ln_b", "w_out", "norm_ffn", "ffn_w1", "ffn_w3", "ffn_w2", "norm_ple", "ple_gate_w", "ple_up_w",
                "final_norm")
SMALL_NAMES = tuple(n for n in WEIGHT_NAMES if n not in SHARDED_NAMES)
ARG_NAMES = ("x", "p") + WEIGHT_NAMES + ("loss_target",) + tuple("m_" + n for n in WEIGHT_NAMES) + tuple("v_" + n for n in WEIGHT_NAMES)


def _travel(a, tr):
    return a.T if tr else a


def _pack_misc(blocks):
    lead = blocks[0].shape[:-2]
    return jnp.concatenate([b.reshape(lead + (-1, PACK_COLS)) for b in blocks], axis=len(lead))


def _unpack_misc(packed, shapes):
    lead = packed.shape[:-2]
    out, off = [], 0
    for r, c in shapes:
        n = r * c // PACK_COLS
        out.append(lax.slice_in_dim(packed, off, off + n, axis=len(lead)).reshape(lead + (r, c)))
        off += n
    return out


def _kernel_impl(ins):
    x, p, target = ins["x"][0], ins["p"][0, 0], ins["loss_target"][0]
    me = 4 * lax.axis_index("x") + 2 * lax.axis_index("y") + lax.axis_index("c")
    small = {n: (ins[n] if n == "final_norm" else ins[n][0]) for n in SMALL_NAMES}
    trav = lambda pre, n, tr: _travel(ins[pre + n][0], tr)
    misc_shapes = [trav("", n, tr).shape for n, tr in MISC]

    early = _all_gather("ag_early", [trav("", n, tr).astype(BF16) for n, tr in EARLY]
                        + [_pack_misc([trav("", n, tr).astype(BF16) for n, tr in MISC])])
    late_handle, late_token = _direct_start("ag_late_start", [trav("", n, tr).astype(BF16) for n, tr in LATE], True, early[-1])
    W = dict(small)
    for (n, tr), g in zip(EARLY, early):
        W[n] = _travel(g, tr)
    for (n, tr), g in zip(MISC, _unpack_misc(early[-1].reshape(N_DEV, -1, PACK_COLS), misc_shapes)):
        W[n] = _travel(g.reshape(-1, g.shape[-1]), tr)

    def late_weights(after):
        shards, lands = _direct_wait("ag_late_wait", late_handle, after)
        full = [lax.dynamic_update_slice_in_dim(ld, sh, me * sh.shape[0], axis=0) for ld, sh in zip(lands, shards)]
        return {n: _travel(g, tr) for (n, tr), g in zip(LATE, full)}

    gt = lambda G, n, tr: _travel(G[n], tr)
    started = {}

    def grads_ready(stage, G):
        full = [gt(G, n, tr) for n, tr in GRAD_STAGES[stage]]
        started[stage] = (full, *_direct_start("grad_late_start%d" % stage, [a.astype(BF16) for a in full], False))
        return started[stage][2]

    loss_part, dx, G = _local_step(x, p, target, W, late_weights, grads_ready, late_token)

    misc_g = _pack_misc([gt(G, n, tr).reshape((N_DEV,) + shp) for (n, tr), shp in zip(MISC, misc_shapes)])
    early_full = [gt(G, n, tr) for n, tr in EARLY] + [misc_g.reshape(-1, PACK_COLS)]
    early_handle, early_token = _direct_start("grad_early_start", [a.astype(BF16) for a in early_full], False)
    view2 = lambda a, r, c: a.reshape(r, c)
    small_own = [_stack_class(cls, [view2(G[n], r, c) for n, r, c in cls]) for cls in SMALL_CLASSES]
    small_handle, small_token = _direct_start("grad_small_start", small_own, True)
    late_src, late_land = [], []
    for stage in range(len(GRAD_STAGES)):
        full, handle, _ = started[stage]
        _, land = _direct_wait("grad_late_wait%d" % stage, handle, small_token)
        late_src += full
        late_land += land

    outs = {}

    def emit(names_shapes, res):
        for tag, val in zip(("grad_", "delta_", "new_m_", "new_v_"), res):
            for n, v in names_shapes(val):
                outs[tag + n] = v

    def sharded_update(n, tr, src, land, deps=()):
        rows = src.shape[0] // N_DEV
        own = lax.dynamic_slice_in_dim(src, me * rows, rows, axis=0)
        res = _adamw_sharded("adamw_" + n, own, land.reshape(N_DEV, rows, land.shape[1]),
                             trav("", n, tr), trav("m_", n, tr), trav("v_", n, tr), _pick_rows(rows), deps)
        emit(lambda val: [(n, _travel(val, tr).reshape(ins[n].shape))], res)
        return list(res)

    for (n, tr), src, land in zip(LATE, late_src, late_land):
        sharded_update(n, tr, src, land, (early_token,))
    _, early_land = _direct_wait("grad_early_wait", early_handle, list(outs.values()))
    for (n, tr), src, land in zip(EARLY, early_full[:-1], early_land[:-1]):
        sharded_update(n, tr, src, land)
    pm = lambda pre: _pack_misc([trav(pre, n, tr) for n, tr in MISC])
    rows = early_full[-1].shape[0] // N_DEV
    res = _adamw_sharded("adamw_misc", lax.dynamic_slice_in_dim(early_full[-1], me * rows, rows, axis=0),
                         early_land[-1].reshape(N_DEV, rows, PACK_COLS), pm(""), pm("m_"), pm("v_"), rows)
    emit(lambda val: [(n, _travel(b, tr).reshape(ins[n].shape)) for (n, tr), b in zip(MISC, _unpack_misc(val, misc_shapes))], res)
    small_src, small_land = _direct_wait("grad_small_wait", small_handle, res[0])
    small_all = [lax.dynamic_update_slice_in_dim(ld, sr, me * sr.shape[0], axis=0) for ld, sr in zip(small_land, small_src)]
    flat_small = [(n, r, c) for cls in SMALL_CLASSES for n, r, c in cls]
    res = _adamw_small(small_all, *[[view2(ins[pre + n], r, c) for n, r, c in flat_small] for pre in ("", "m_", "v_")])
    for n, _, _ in flat_small:
        for tag, val in zip(("grad_", "delta_", "new_m_", "new_v_"), res[n]):
            outs[tag + n] = val.reshape(ins[n].shape)
    loss = lax.psum(loss_part, MESH_AXES)
    res = [loss, dx[None]]
    for tag in ("grad_", "delta_", "new_m_", "new_v_"):
        res += [outs[tag + n] for n in WEIGHT_NAMES]
    return tuple(res)


def _pick_rows(r):
    best = 8
    for b in range(8, 257, 8):
        if r % b == 0:
            best = b
    return best


def kernel(x, p, norm_mix, w_in, s5_lam_re, s5_lam_im, s5_log_step, s5_b_re, s5_b_im, s5_c_re, s5_c_im, s5_d, s5_glu_w, s5_glu_b, rw_shift_mu, rw_w0, rw_w2, rw_a0, rw_a2, rw_g2, rw_k_k, rw_k_a, rw_r_k, rw_ln_w, rw_ln_b, w_out, norm_ffn, ffn_w1, ffn_w3, ffn_w2, norm_ple, ple_gate_w, ple_up_w, final_norm, loss_target, m_norm_mix, m_w_in, m_s5_lam_re, m_s5_lam_im, m_s5_log_step, m_s5_b_re, m_s5_b_im, m_s5_c_re, m_s5_c_im, m_s5_d, m_s5_glu_w, m_s5_glu_b, m_rw_shift_mu, m_rw_w0, m_rw_w2, m_rw_a0, m_rw_a2, m_rw_g2, m_rw_k_k, m_rw_k_a, m_rw_r_k, m_rw_ln_w, m_rw_ln_b, m_w_out, m_norm_ffn, m_ffn_w1, m_ffn_w3, m_ffn_w2, m_norm_ple, m_ple_gate_w, m_ple_up_w, m_final_norm, v_norm_mix, v_w_in, v_s5_lam_re, v_s5_lam_im, v_s5_log_step, v_s5_b_re, v_s5_b_im, v_s5_c_re, v_s5_c_im, v_s5_d, v_s5_glu_w, v_s5_glu_b, v_rw_shift_mu, v_rw_w0, v_rw_w2, v_rw_a0, v_rw_a2, v_rw_g2, v_rw_k_k, v_rw_k_a, v_rw_r_k, v_rw_ln_w, v_rw_ln_b, v_w_out, v_norm_ffn, v_ffn_w1, v_ffn_w3, v_ffn_w2, v_norm_ple, v_ple_gate_w, v_ple_up_w, v_final_norm):
    return _kernel_impl(dict(zip(ARG_NAMES, (x, p, norm_mix, w_in, s5_lam_re, s5_lam_im, s5_log_step, s5_b_re, s5_b_im, s5_c_re, s5_c_im, s5_d, s5_glu_w, s5_glu_b, rw_shift_mu, rw_w0, rw_w2, rw_a0, rw_a2, rw_g2, rw_k_k, rw_k_a, rw_r_k, rw_ln_w, rw_ln_b, w_out, norm_ffn, ffn_w1, ffn_w3, ffn_w2, norm_ple, ple_gate_w, ple_up_w, final_norm, loss_target, m_norm_mix, m_w_in, m_s5_lam_re, m_s5_lam_im, m_s5_log_step, m_s5_b_re, m_s5_b_im, m_s5_c_re, m_s5_c_im, m_s5_d, m_s5_glu_w, m_s5_glu_b, m_rw_shift_mu, m_rw_w0, m_rw_w2, m_rw_a0, m_rw_a2, m_rw_g2, m_rw_k_k, m_rw_k_a, m_rw_r_k, m_rw_ln_w, m_rw_ln_b, m_w_out, m_norm_ffn, m_ffn_w1, m_ffn_w3, m_ffn_w2, m_norm_ple, m_ple_gate_w, m_ple_up_w, m_final_norm, v_norm_mix, v_w_in, v_s5_lam_re, v_s5_lam_im, v_s5_log_step, v_s5_b_re, v_s5_b_im, v_s5_c_re, v_s5_c_im, v_s5_d, v_s5_glu_w, v_s5_glu_b, v_rw_shift_mu, v_rw_w0, v_rw_w2, v_rw_a0, v_rw_a2, v_rw_g2, v_rw_k_k, v_rw_k_a, v_rw_r_k, v_rw_ln_w, v_rw_ln_b, v_w_out, v_norm_ffn, v_ffn_w1, v_ffn_w3, v_ffn_w2, v_norm_ple, v_ple_gate_w, v_ple_up_w, v_final_norm))))
```

```python
import functools

import jax
import jax.numpy as jnp
from jax import lax
from jax.experimental import pallas as pl
from jax.experimental.pallas import tpu as pltpu

F32 = jnp.float32
BF16 = jnp.bfloat16

D_MODEL = 1024
S5_WIDTH = 512
RW_WIDTH = 512
S5_GROUP = 16
S5_GROUPS = 32
S5_STATE = 64
S5_LANES = S5_GROUPS * S5_STATE
HEAD = 64
SHIFT_COLS = 1792
IN_COLS = 2304
FFN_HIDDEN = 2816
PLE_DIM = 256
RMS_EPS = 1e-6
GN_EPS = 64e-5
L2_EPS = 1e-12
CHUNK = 64
N_DEV = 8

ADAM_LR = 0.001
ADAM_B1 = 0.9
ADAM_B2 = 0.999
ADAM_EPS = 1e-08
ADAM_WD = 0.01
ADAM_STEP = 10

VMEM_LIMIT = 56 * 1024 * 1024
_ANY = pl.BlockSpec(memory_space=pl.ANY)


def _pcall(body, **kw):
    return pl.pallas_call(body, **kw)


def _cparams(n_grid):
    return pltpu.CompilerParams(dimension_semantics=("arbitrary",) * n_grid, vmem_limit_bytes=VMEM_LIMIT)


def _dot(a, b):
    return jnp.dot(a, b, preferred_element_type=F32)


def _dot_nt(a, b):
    return lax.dot_general(a, b, (((1,), (1,)), ((), ())), preferred_element_type=F32)


def _dot_tn(a, b):
    return lax.dot_general(a, b, (((0,), (0,)), ((), ())), preferred_element_type=F32)


def _mmc(w, diff=True, tr=False):
    fw, bw = (_dot_nt, _dot) if tr else (_dot, _dot_nt)
    if not diff:
        return lambda x: fw(x.astype(BF16), w)

    @jax.custom_vjp
    def f(x):
        return fw(x.astype(BF16), w)

    def fwd(x):
        return fw(x.astype(BF16), w), None

    def bwd(_, dy):
        return (bw(dy.astype(BF16), w),)

    f.defvjp(fwd, bwd)
    return f


def _split_dot(x, m, n_split):
    acc = None
    rem = x
    for s in range(n_split):
        part = rem.astype(BF16)
        t = _dot(part, m)
        acc = t if acc is None else acc + t
        if s + 1 < n_split:
            rem = rem - part.astype(F32)
    return acc


def _segsum(m, diff=True):
    if not diff:
        return lambda x: _split_dot(x, m, 2)

    @jax.custom_vjp
    def f(x):
        return _split_dot(x, m, 2)

    def fwd(x):
        return _split_dot(x, m, 2), None

    def bwd(_, dy):
        return (_split_dot(dy, m, 2),)

    f.defvjp(fwd, bwd)
    return f


def _head_indicator(n):
    r = lax.broadcasted_iota(jnp.int32, (n, n), 0) // HEAD
    c = lax.broadcasted_iota(jnp.int32, (n, n), 1) // HEAD
    return (r == c).astype(BF16)


def _rms(x, g):
    return x * lax.rsqrt(jnp.mean(x * x, axis=-1, keepdims=True) + RMS_EPS) * g


def _softplus(x):
    return jnp.maximum(x, 0.0) + jnp.log(1.0 + jnp.exp(-jnp.abs(x)))


def _sigmoid(x):
    return 1.0 / (1.0 + jnp.exp(-x))


def _gelu(x):
    return 0.5 * x * (1.0 + jnp.tanh(0.7978845608028654 * (x + 0.044715 * (x * x * x))))


def _tok_call(name, fn, L, TB, tok_in, const_in, tok_out, acc_out=(), deps=()):
    nb = L // TB
    g8 = TB // 8
    in_specs, args = [], []
    for spec in tok_in:
        if len(spec) == 1:
            arr = spec[0]
            in_specs.append(pl.BlockSpec((arr.shape[0], TB, HEAD), lambda i: (0, i, 0)))
            args.append(arr)
            continue
        arr, width, cb = spec[:3]
        mode = spec[3] if len(spec) > 3 else None
        if mode is None:
            in_specs.append(pl.BlockSpec((TB, width), lambda i, cb=cb: (i, cb)))
        elif mode == "prev":
            in_specs.append(pl.BlockSpec((8, width), lambda i, cb=cb: (jnp.maximum(i * g8 - 1, 0), cb)))
        else:
            in_specs.append(pl.BlockSpec((8, width), lambda i, cb=cb: (jnp.minimum((i + 1) * g8, L // 8 - 1), cb)))
        args.append(arr)
    for c in const_in:
        in_specs.append(pl.BlockSpec(c.shape, lambda i, nd=c.ndim: (0,) * nd, pipeline_mode=pl.Buffered(1)))
        args.append(c)
    for d in deps:
        in_specs.append(pl.BlockSpec(d.shape, lambda i, nd=d.ndim: (0,) * nd))
        args.append(d)
    out_shape, out_specs = [], []
    for width, dt in tok_out:
        if width == "heads":
            out_shape.append(jax.ShapeDtypeStruct((N_HEAD, L, HEAD), dt))
            out_specs.append(pl.BlockSpec((N_HEAD, TB, HEAD), lambda i: (0, i, 0)))
            continue
        out_shape.append(jax.ShapeDtypeStruct((L, width), dt))
        out_specs.append(pl.BlockSpec((TB, width), lambda i: (i, 0)))
    for shp in acc_out:
        out_shape.append(jax.ShapeDtypeStruct(shp, F32))
        out_specs.append(pl.BlockSpec(shp, lambda i, nd=len(shp): (0,) * nd))
    n_tok, n_const, n_to = len(tok_in), len(const_in), len(tok_out)

    def body(*refs):
        i = pl.program_id(0)
        tv = [r[...] if len(r.shape) == 2 else jnp.concatenate([r[h] for h in range(r.shape[0])], axis=1)
              for r in refs[:n_tok]]
        cv = [r[...] for r in refs[n_tok:n_tok + n_const]]
        orefs = refs[n_tok + n_const + len(deps):]
        outs = fn(i, tv, cv)
        for r, v in zip(orefs[:n_to], outs[:n_to]):
            if len(r.shape) == 3:
                for h in range(r.shape[0]):
                    r[h] = v[:, h * HEAD:(h + 1) * HEAD].astype(r.dtype)
            else:
                r[...] = v.astype(r.dtype)
        for r, v in zip(orefs[n_to:], outs[n_to:]):
            @pl.when(i == 0)
            def _(r=r):
                r[...] = jnp.zeros(r.shape, r.dtype)

            r[...] += v

    res = _pcall(body, name=name, grid=(nb,), in_specs=in_specs, out_specs=out_specs, out_shape=out_shape,
                 compiler_params=_cparams(1))(*args)
    return res


def _pick_block(n, cap):
    best = None
    for b in range(128, min(n, cap) + 1, 128):
        if n % b == 0:
            best = b
    return best if best is not None else n


def _mm_tn(name, a, b):
    T, M = a.shape
    N = b.shape[1]
    bm, bn, bt = _pick_block(M, 1536), _pick_block(N, 1536), _pick_block(T, 512)

    def body(a_ref, b_ref, o_ref):
        t = pl.program_id(2)

        @pl.when(t == 0)
        def _():
            o_ref[...] = jnp.zeros(o_ref.shape, F32)

        o_ref[...] += _dot_tn(a_ref[...].astype(BF16), b_ref[...].astype(BF16))

    return _pcall(body, name=name, grid=(M // bm, N // bn, T // bt),
                  in_specs=[pl.BlockSpec((bt, bm), lambda m, n, t: (t, m)), pl.BlockSpec((bt, bn), lambda m, n, t: (t, n))],
                  out_specs=pl.BlockSpec((bm, bn), lambda m, n, t: (m, n)),
                  out_shape=jax.ShapeDtypeStruct((M, N), F32), compiler_params=_cparams(3))(a, b)


def _s5_param_fn(lam_re, lam_im, log_step, bt_re, bt_im):
    dt = jnp.exp(log_step)
    e = jnp.exp(lam_re * dt)
    lb_re = e * jnp.cos(lam_im * dt)
    lb_im = e * jnp.sin(lam_im * dt)
    den = lam_re * lam_re + lam_im * lam_im
    nr, ni = lb_re - 1.0, lb_im
    co_re = (nr * lam_re + ni * lam_im) / den
    co_im = (ni * lam_re - nr * lam_im) / den
    cr, ci = co_re[:, None, :], co_im[:, None, :]
    return lb_re, lb_im, cr * bt_re - ci * bt_im, cr * bt_im + ci * bt_re


def _s5_param_fwd(lam_re, lam_im, log_step, bt_re, bt_im):
    def body(a, b, c, d, e, o1, o2, o3, o4):
        r = _s5_param_fn(a[...], b[...], c[...], d[...], e[...])
        o1[...], o2[...], o3[...], o4[...] = r

    sh = jax.ShapeDtypeStruct
    return _pcall(body, name="s5_param_fwd",
                  out_shape=[sh(lam_re.shape, F32), sh(lam_re.shape, F32), sh(bt_re.shape, F32), sh(bt_re.shape, F32)])(
        lam_re, lam_im, log_step, bt_re, bt_im)


def _s5_param_bwd(lam_re, lam_im, log_step, bt_re, bt_im, d_lb_re, d_lb_im, d_bb_re, d_bb_im):
    def body(a, b, c, d, e, g1, g2, g3, g4, o1, o2, o3, o4, o5):
        _, vjp = jax.vjp(_s5_param_fn, a[...], b[...], c[...], d[...], e[...])
        r = vjp((g1[...], g2[...], g3[...], g4[...]))
        o1[...], o2[...], o3[...], o4[...], o5[...] = r

    sh = jax.ShapeDtypeStruct
    return _pcall(body, name="s5_param_bwd",
                  out_shape=[sh(lam_re.shape, F32), sh(lam_re.shape, F32), sh(log_step.shape, F32),
                             sh(bt_re.shape, F32), sh(bt_re.shape, F32)])(
        lam_re, lam_im, log_step, bt_re, bt_im, d_lb_re, d_lb_im, d_bb_re, d_bb_im)


def _cmul(ar, ai, br, bi):
    return ar * br - ai * bi, ar * bi + ai * br


def _scan_consts(lr, li, reverse):
    n = lr.shape[1]
    sub = lax.broadcasted_iota(jnp.int32, (8, n), 0)
    pows = [(lr, li)]
    for _ in range(7):
        pows.append(_cmul(pows[-1][0], pows[-1][1], lr, li))
    steps = []
    for s in (1, 2, 4):
        m = (sub < 8 - s) if reverse else (sub >= s)
        pr, pi = pows[s - 1]
        steps.append((s, jnp.where(m, jnp.broadcast_to(pr, (8, n)), 0.0), jnp.where(m, jnp.broadcast_to(pi, (8, n)), 0.0)))
    wr = jnp.zeros((8, n), F32)
    wi = jnp.zeros((8, n), F32)
    for r in range(8):
        e = (8 - r) if reverse else (r + 1)
        wr = jnp.where(sub == r, jnp.broadcast_to(pows[e - 1][0], (8, n)), wr)
        wi = jnp.where(sub == r, jnp.broadcast_to(pows[e - 1][1], (8, n)), wi)
    return steps, wr, wi


S5_Q = 4
S5_QL = S5_WIDTH // S5_Q
S5_QS = S5_LANES // S5_Q
S5_NT = S5_LANES // 128
S5_QT = S5_QS // 128


def _s5_power_table(lb_ref, pw_re, pw_im, seg):
    for j in range(S5_NT):
        lr = jnp.broadcast_to(lb_ref[0:1, j * 128:(j + 1) * 128], (8, 128))
        li = jnp.broadcast_to(lb_ref[1:2, j * 128:(j + 1) * 128], (8, 128))

        def step(i, c, lr=lr, li=li, j=j):
            pw_re[j, i] = c[0]
            pw_im[j, i] = c[1]
            return _cmul(c[0], c[1], lr, li)

        lax.fori_loop(0, seg, step, (lr, li))


def _seg_scan(sre, sim, carry, lb_ref, pw_re, pw_im, rows, reverse):
    seg = rows // 8
    sgn = -1.0 if reverse else 1.0
    sub = lax.broadcasted_iota(jnp.int32, (8, 128), 0)
    rows_at = lambda i: pl.ds(pl.multiple_of(i * 8, 8), 8)
    entering = {}
    half_tiles = S5_NT // 2
    for half in range(2):
        tiles = list(range(half * half_tiles, (half + 1) * half_tiles))
        lam8 = [(jnp.broadcast_to(lb_ref[0:1, j * 128:(j + 1) * 128], (8, 128)),
                 sgn * jnp.broadcast_to(lb_ref[1:2, j * 128:(j + 1) * 128], (8, 128))) for j in tiles]

        def p1(ii, c):
            i = (seg - 1 - ii) if reverse else ii
            out = []
            for n, j in enumerate(tiles):
                lr, li = lam8[n]
                cr, ci = c[2 * n], c[2 * n + 1]
                nr = lr * cr - li * ci + sre[j, rows_at(i), :]
                ni = lr * ci + li * cr + sim[j, rows_at(i), :]
                sre[j, rows_at(i), :] = nr
                sim[j, rows_at(i), :] = ni
                out += [nr, ni]
            return tuple(out)

        ends = lax.fori_loop(0, seg, p1, tuple(jnp.zeros((8, 128), F32) for _ in range(2 * len(tiles))))
        cs = []
        for n, j in enumerate(tiles):
            ls = slice(j * 128, (j + 1) * 128)
            steps, wr, wi = _scan_consts(pw_re[j, seg - 1][0:1, :], sgn * pw_im[j, seg - 1][0:1, :], reverse)
            tr, ti = ends[2 * n], ends[2 * n + 1]
            for sft, pr, pi in steps:
                sh = (8 - sft) if reverse else sft
                yr, yi = pltpu.roll(tr, sh, 0), pltpu.roll(ti, sh, 0)
                tr, ti = tr + pr * yr - pi * yi, ti + pr * yi + pi * yr
            cin_r, cin_i = carry[0:1, ls], carry[1:2, ls]
            tr, ti = tr + wr * cin_r - wi * cin_i, ti + wr * cin_i + wi * cin_r
            edge_out, edge_in, sh = (0, 7, 7) if reverse else (7, 0, 1)
            carry[0:1, ls] = tr[edge_out:edge_out + 1, :]
            carry[1:2, ls] = ti[edge_out:edge_out + 1, :]
            cr = jnp.where(sub == edge_in, jnp.broadcast_to(cin_r, (8, 128)), pltpu.roll(tr, sh, 0))
            ci = jnp.where(sub == edge_in, jnp.broadcast_to(cin_i, (8, 128)), pltpu.roll(ti, sh, 0))
            cs += [cr, ci]
            entering[j] = (cr, ci)

        def p2(i, _):
            k = (seg - 1 - i) if reverse else i
            for n, j in enumerate(tiles):
                pr, pi = pw_re[j, k], pw_im[j, k]
                cr, ci = cs[2 * n], cs[2 * n + 1]
                if reverse:
                    sre[j, rows_at(i), :] = sre[j, rows_at(i), :] + pr * cr + pi * ci
                    sim[j, rows_at(i), :] = sim[j, rows_at(i), :] + pr * ci - pi * cr
                else:
                    sre[j, rows_at(i), :] = sre[j, rows_at(i), :] + pr * cr - pi * ci
                    sim[j, rows_at(i), :] = sim[j, rows_at(i), :] + pr * ci + pi * cr
            return 0

        lax.fori_loop(0, seg, p2, 0, unroll=2)
    return entering


class _SegIO:
    def __init__(self, hbm, buf, sems, rows, width, col0=0):
        self.hbm, self.buf, self.sems, self.rows, self.seg, self.width, self.col0 = hbm, buf, sems, rows, rows // 8, width, col0

    def _copies(self, blk, slot, to_vmem):
        out = []
        for r in range(8):
            h = self.hbm.at[pl.ds(blk * self.rows + r * self.seg, self.seg), pl.ds(self.col0, self.width)]
            v = self.buf.at[slot, :, r, :]
            out.append(pltpu.make_async_copy(h, v, self.sems.at[slot, r]) if to_vmem
                       else pltpu.make_async_copy(v, h, self.sems.at[slot, r]))
        return out

    def start(self, blk, slot, to_vmem):
        for cp in self._copies(blk, slot, to_vmem):
            cp.start()

    def wait(self, blk, slot, to_vmem):
        for cp in self._copies(blk, slot, to_vmem):
            cp.wait()

    def value(self, slot):
        return self.buf[slot].reshape(self.rows, self.width)

    def store(self, slot, val):
        self.buf[slot] = val.reshape(self.seg, 8, self.width)


def _seg_pipeline(i, nb, blk_of, ins, outs, compute):
    slot = i % 2

    @pl.when(i == 0)
    def _():
        for io in ins:
            io.start(blk_of(0), 0, True)

    @pl.when(i + 1 < nb)
    def _():
        for io in ins:
            io.start(blk_of(i + 1), 1 - slot, True)

    for io in ins:
        io.wait(blk_of(i), slot, True)

    @pl.when(i >= 2)
    def _():
        for io in outs:
            io.wait(blk_of(i - 2), slot, False)

    compute(slot)
    for io in outs:
        io.start(blk_of(i), slot, False)

    @pl.when(i == nb - 1)
    def _():
        for io in outs:
            if nb >= 2:
                io.wait(blk_of(i - 1), 1 - slot, False)
            io.wait(blk_of(i), slot, False)


def _s5_scan_fwd(proj, bq_re, bq_im, cq_re, cq_im, lbar, dskip, L, TB):
    nb = L // TB
    seg = TB // 8

    def body(u_hbm, bre, bim, cre, cim, lb_ref, d_ref, y_hbm, ck_ref, sre, sim, carry, pw_re, pw_im,
             ubuf, ybuf, sem_u, sem_y):
        i = pl.program_id(0)
        u_io = _SegIO(u_hbm, ubuf, sem_u, TB, S5_WIDTH)
        y_io = _SegIO(y_hbm, ybuf, sem_y, TB, S5_WIDTH)

        @pl.when(i == 0)
        def _():
            carry[...] = jnp.zeros(carry.shape, F32)
            _s5_power_table(lb_ref, pw_re, pw_im, seg)

        ck_ref[0] = carry[...]

        def compute(slot):
            u = u_io.value(slot)
            ub = u.astype(BF16)
            for q in range(S5_Q):
                uq = ub[:, q * S5_QL:(q + 1) * S5_QL]
                vr, vi = _dot(uq, bre[q]), _dot(uq, bim[q])
                for jj in range(S5_QT):
                    sre[q * S5_QT + jj] = vr[:, jj * 128:(jj + 1) * 128]
                    sim[q * S5_QT + jj] = vi[:, jj * 128:(jj + 1) * 128]
            _seg_scan(sre, sim, carry, lb_ref, pw_re, pw_im, TB, False)
            ys = []
            for q in range(S5_Q):
                sl = slice(q * S5_QL, (q + 1) * S5_QL)
                sr = jnp.concatenate([sre[q * S5_QT + jj] for jj in range(S5_QT)], axis=1).astype(BF16)
                si = jnp.concatenate([sim[q * S5_QT + jj] for jj in range(S5_QT)], axis=1).astype(BF16)
                ys.append(_dot(sr, cre[q]) - _dot(si, cim[q]) + u[:, sl] * d_ref[:, sl])
            y_io.store(slot, jnp.concatenate(ys, axis=1))

        _seg_pipeline(i, nb, lambda st: st, [u_io], [y_io], compute)

    full = lambda a: pl.BlockSpec(a.shape, lambda i, nd=a.ndim: (0,) * nd)
    st = pltpu.VMEM((S5_NT, TB, 128), F32)
    pw = pltpu.VMEM((S5_NT, seg, 8, 128), F32)
    io = pltpu.VMEM((2, seg, 8, S5_WIDTH), F32)
    return _pcall(
        body, name="s5_scan_fwd", grid=(nb,),
        in_specs=[_ANY, full(bq_re), full(bq_im), full(cq_re), full(cq_im), full(lbar), full(dskip)],
        out_specs=[_ANY, pl.BlockSpec((1, 8, S5_LANES), lambda i: (i, 0, 0))],
        out_shape=[jax.ShapeDtypeStruct((L, S5_WIDTH), F32), jax.ShapeDtypeStruct((nb, 8, S5_LANES), F32)],
        scratch_shapes=[st, st, pltpu.VMEM((8, S5_LANES), F32), pw, pw, io, io,
                        pltpu.SemaphoreType.DMA((2, 8)), pltpu.SemaphoreType.DMA((2, 8))],
        compiler_params=_cparams(1))(proj, bq_re, bq_im, cq_re, cq_im, lbar, dskip)


def _s5_scan_bwd(proj, dy, ck, bq_re, bq_im, cq_re, cq_im, lbar, dskip, L, TB):
    nb = L // TB
    seg = TB // 8

    def body(u_hbm, dy_hbm, ck_ref, bre, bim, cre, cim, lb_ref, d_ref,
             du_hbm, dbre, dbim, dcre, dcim, dlb_ref, dd_ref, sre, sim, gre, gim, carry, gcarry, pw_re, pw_im,
             ubuf, dybuf, dubuf, sem_u, sem_dy, sem_du):
        i = pl.program_id(0)
        u_io = _SegIO(u_hbm, ubuf, sem_u, TB, S5_WIDTH)
        dy_io = _SegIO(dy_hbm, dybuf, sem_dy, TB, S5_WIDTH)
        du_io = _SegIO(du_hbm, dubuf, sem_du, TB, S5_WIDTH)

        @pl.when(i == 0)
        def _():
            gcarry[...] = jnp.zeros(gcarry.shape, F32)
            dbre[...] = jnp.zeros(dbre.shape, F32)
            dbim[...] = jnp.zeros(dbim.shape, F32)
            dcre[...] = jnp.zeros(dcre.shape, F32)
            dcim[...] = jnp.zeros(dcim.shape, F32)
            dlb_ref[...] = jnp.zeros(dlb_ref.shape, F32)
            dd_ref[...] = jnp.zeros(dd_ref.shape, F32)
            _s5_power_table(lb_ref, pw_re, pw_im, seg)

        def compute(slot):
            u = u_io.value(slot)
            dy_v = dy_io.value(slot)
            ub = u.astype(BF16)
            dyb = dy_v.astype(BF16)
            carry[...] = ck_ref[0]
            for q in range(S5_Q):
                uq = ub[:, q * S5_QL:(q + 1) * S5_QL]
                dq = dyb[:, q * S5_QL:(q + 1) * S5_QL]
                vr, vi = _dot(uq, bre[q]), _dot(uq, bim[q])
                hr, hi = _dot_nt(dq, cre[q]), -_dot_nt(dq, cim[q])
                for jj in range(S5_QT):
                    ls = slice(jj * 128, (jj + 1) * 128)
                    sre[q * S5_QT + jj] = vr[:, ls]
                    sim[q * S5_QT + jj] = vi[:, ls]
                    gre[q * S5_QT + jj] = hr[:, ls]
                    gim[q * S5_QT + jj] = hi[:, ls]
            entering = _seg_scan(sre, sim, carry, lb_ref, pw_re, pw_im, TB, False)
            _seg_scan(gre, gim, gcarry, lb_ref, pw_re, pw_im, TB, True)

            rows_at = lambda k: pl.ds(pl.multiple_of(k * 8, 8), 8)
            for half in range(2):
                tiles = list(range(half * (S5_NT // 2), (half + 1) * (S5_NT // 2)))
                acc0 = []
                for j in tiles:
                    er, ei = entering[j]
                    gr0, gi0 = gre[j, rows_at(0), :], gim[j, rows_at(0), :]
                    acc0 += [gr0 * er + gi0 * ei, gi0 * er - gr0 * ei]

                def acc_step(k, acc, tiles=tiles):
                    out = []
                    for n, j in enumerate(tiles):
                        gr, gi_ = gre[j, rows_at(k), :], gim[j, rows_at(k), :]
                        spr, spi = sre[j, rows_at(k - 1), :], sim[j, rows_at(k - 1), :]
                        out += [acc[2 * n] + gr * spr + gi_ * spi, acc[2 * n + 1] - gr * spi + gi_ * spr]
                    return tuple(out)

                acc = lax.fori_loop(1, seg, acc_step, tuple(acc0))
                for n, j in enumerate(tiles):
                    ls = slice(j * 128, (j + 1) * 128)
                    dlb_ref[0:1, ls] += jnp.sum(acc[2 * n], axis=0, keepdims=True)
                    dlb_ref[1:2, ls] += jnp.sum(acc[2 * n + 1], axis=0, keepdims=True)

            dd_ref[...] += jnp.sum(dy_v * u, axis=0, keepdims=True)
            dus = []
            for q in range(S5_Q):
                sl = slice(q * S5_QL, (q + 1) * S5_QL)
                cat = lambda ref: jnp.concatenate([ref[q * S5_QT + jj] for jj in range(S5_QT)], axis=1).astype(BF16)
                grq, giq = cat(gre), cat(gim)
                dus.append(_dot_nt(grq, bre[q]) + _dot_nt(giq, bim[q]) + dy_v[:, sl] * d_ref[:, sl])
                dbre[q] += _dot_tn(ub[:, sl], grq)
                dbim[q] += _dot_tn(ub[:, sl], giq)
                dcre[q] += _dot_tn(cat(sre), dyb[:, sl])
                dcim[q] -= _dot_tn(cat(sim), dyb[:, sl])
            du_io.store(slot, jnp.concatenate(dus, axis=1))

        _seg_pipeline(i, nb, lambda st: nb - 1 - st, [u_io, dy_io], [du_io], compute)

    full = lambda a: pl.BlockSpec(a.shape, lambda i, nd=a.ndim: (0,) * nd)
    sh = jax.ShapeDtypeStruct
    outs = [sh((L, S5_WIDTH), F32), sh(bq_re.shape, F32), sh(bq_im.shape, F32), sh(cq_re.shape, F32), sh(cq_im.shape, F32),
            sh((8, S5_LANES), F32), sh((1, S5_WIDTH), F32)]
    fo = lambda s: pl.BlockSpec(s.shape, lambda i, nd=len(s.shape): (0,) * nd)
    st = pltpu.VMEM((S5_NT, TB, 128), F32)
    pw = pltpu.VMEM((S5_NT, seg, 8, 128), F32)
    io = pltpu.VMEM((2, seg, 8, S5_WIDTH), F32)
    sem = pltpu.SemaphoreType.DMA((2, 8))
    return _pcall(
        body, name="s5_scan_bwd", grid=(nb,),
        in_specs=[_ANY, _ANY, pl.BlockSpec((1, 8, S5_LANES), lambda i: (nb - 1 - i, 0, 0)),
                  full(bq_re), full(bq_im), full(cq_re), full(cq_im), full(lbar), full(dskip)],
        out_specs=[_ANY] + [fo(s) for s in outs[1:]],
        out_shape=outs,
        scratch_shapes=[st] * 4 + [pltpu.VMEM((8, S5_LANES), F32)] * 2 + [pw, pw, io, io, io, sem, sem, sem],
        compiler_params=_cparams(1))(proj, dy, ck, bq_re, bq_im, cq_re, cq_im, lbar, dskip)


N_HEAD = RW_WIDTH // HEAD
_NN = (((2,), (1,)), ((0,), (0,)))
_NT = (((2,), (2,)), ((0,), (0,)))
_TN = (((1,), (1,)), ((0,), (0,)))


def _hi_lo(x):
    h = x.astype(BF16)
    return h, (x - h.astype(F32)).astype(BF16)


def _mm_acc(a, b, dims, passes=3):
    dg = lambda p, q: lax.dot_general(p, q, dims, preferred_element_type=F32)
    if passes == 1:
        return dg(a.astype(BF16), b.astype(BF16))
    ah, al = _hi_lo(a)
    bh, bl = _hi_lo(b)
    return dg(ah, bh) + dg(ah, bl) + dg(al, bh)


def _cumsum_rows(x, transpose):
    h, n, _ = x.shape
    ti = lax.broadcasted_iota(jnp.int32, (h, n, n), 1)
    tj = lax.broadcasted_iota(jnp.int32, (h, n, n), 2)
    m = ((tj >= ti) if transpose else (tj <= ti)).astype(BF16)
    acc, rem = None, x
    for s in range(3):
        part = rem.astype(BF16)
        t = lax.dot_general(m, part, _NN, preferred_element_type=F32)
        acc = t if acc is None else acc + t
        if s < 2:
            rem = rem - part.astype(F32)
    return acc


def _slices(x, axis, sizes):
    out, off = [], 0
    for n in sizes:
        out.append(lax.slice_in_dim(x, off, off + n, axis=axis))
        off += n
    return tuple(out)


def _cat_op(axis, sizes, diff):
    plain = lambda *xs: jnp.concatenate(xs, axis=axis)
    if not diff:
        return plain
    f = jax.custom_vjp(plain)
    f.defvjp(lambda *xs: (plain(*xs), None), lambda _, d: _slices(d, axis, sizes))
    return f


def _split_op(axis, sizes, diff):
    plain = lambda x: _slices(x, axis, sizes)
    if not diff:
        return plain
    f = jax.custom_vjp(plain)
    f.defvjp(lambda x: (plain(x), None), lambda _, d: (jnp.concatenate(d, axis=axis),))
    return f


def _mm_ops(diff, passes):
    mm = lambda a, b, dims: _mm_acc(a, b, dims, passes)
    if not diff:
        return (lambda a, b: mm(a, b, _NN), lambda a, b: mm(a, b, _NT), lambda a, b: mm(a, b, _TN))

    @jax.custom_vjp
    def nn(a, b):
        return mm(a, b, _NN)

    nn.defvjp(lambda a, b: (mm(a, b, _NN), (a, b)), lambda r, d: (mm(d, r[1], _NT), mm(r[0], d, _TN)))

    @jax.custom_vjp
    def nt(a, b):
        return mm(a, b, _NT)

    nt.defvjp(lambda a, b: (mm(a, b, _NT), (a, b)), lambda r, d: (mm(d, r[1], _NN), mm(d, r[0], _TN)))

    @jax.custom_vjp
    def tn(a, b):
        return mm(a, b, _TN)

    tn.defvjp(lambda a, b: (mm(a, b, _TN), (a, b)), lambda r, d: (mm(r[1], d, _NT), mm(r[0], d, _NN)))
    return nn, nt, tn


def _cums_op(diff):
    if not diff:
        return lambda x: _cumsum_rows(x, False)

    @jax.custom_vjp
    def cums(x):
        return _cumsum_rows(x, False)

    cums.defvjp(lambda x: (_cumsum_rows(x, False), None), lambda _, d: (_cumsum_rows(d, True),))
    return cums


WKV_PASSES = (1, 1, 1, 1, 1)


WKV_SUB = 4
WKV_BLOCK = CHUNK * WKV_SUB


def _wkv_block(s0, r, w, k, v, a, b, diff):
    p_pair, p_val, p_solve, p_out, p_state = WKV_PASSES
    cums = _cums_op(diff)
    _, nt_pair, _ = _mm_ops(diff, p_pair)
    nn_val, _, _ = _mm_ops(diff, p_val)
    nn_solve, _, _ = _mm_ops(diff, p_solve)
    nn_out, nt_out, _ = _mm_ops(diff, p_out)
    nn_state, _, tn_state = _mm_ops(diff, p_state)
    h, d, n, sub = s0.shape[0], s0.shape[2], CHUNK, WKV_SUB
    hb = h * sub
    to_chunks = lambda t: _cat_op(0, (h,) * sub, diff)(*_split_op(1, (n,) * sub, diff)(t))
    r, w, k, v, a, b = (to_chunks(t) for t in (r, w, k, v, a, b))
    cat_rows2 = _cat_op(1, (n, n), diff)
    cat_lanes2 = _cat_op(2, (n, n), diff)
    split_rows2 = _split_op(1, (n, n), diff)
    split_lanes2 = _split_op(2, (n, n), diff)
    ti = lax.broadcasted_iota(jnp.int32, (hb, n, n), 1)
    tj = lax.broadcasted_iota(jnp.int32, (hb, n, n), 2)
    incl, strict = tj <= ti, tj < ti
    logw = jnp.log(w)
    cum = cums(logw)
    g_in, g_ex, g_inv = jnp.exp(cum), jnp.exp(cum - logw), jnp.exp(-cum)
    ae, re, bi, ki = a * g_ex, r * g_in, b * g_inv, k * g_inv
    top, bot = split_rows2(nt_pair(cat_rows2(ae, re), cat_rows2(bi, ki)))
    tab, tak = split_lanes2(top)
    qb, qk = split_lanes2(bot)
    tab, tak = jnp.where(strict, tab, 0.0), jnp.where(strict, tak, 0.0)
    qb, qk = jnp.where(incl, qb, 0.0), jnp.where(incl, qk, 0.0)
    tak_v, qk_v = split_rows2(nn_val(cat_rows2(tak, qk), v))
    x = cat_lanes2(ae, tak_v)
    npow = tab
    steps = max(1, (n - 1).bit_length())
    for i in range(steps):
        x = x + nn_solve(npow, x)
        if i + 1 < steps:
            npow = nn_solve(npow, npow)
    ae_m, uc = split_lanes2(x)
    qx = nn_out(qb, x)
    q_ae, q_uc = split_lanes2(qx)
    re_m = re + q_ae
    yc = q_uc + qk_v
    g_end = jnp.exp(jnp.sum(logw, axis=1, keepdims=True))
    bg, kg = bi * g_end, ki * g_end
    tm = tn_state(ae_m, bg)
    sc = tn_state(cat_rows2(uc, v), cat_rows2(bg, kg))
    per_chunk = _split_op(0, (h,) * sub, diff)
    re_m, yc, g_end, tm, sc = (per_chunk(t) for t in (re_m, yc, g_end, tm, sc))
    ys, s = [], s0
    for i in range(sub):
        ys.append(nt_out(re_m[i], s) + yc[i])
        s = s * g_end[i] + nn_state(s, tm[i]) + sc[i]
    return _cat_op(1, (n,) * sub, diff)(*ys), s


def _wkv_fwd(r, w, k, v, a, b, L):
    nc = L // WKV_BLOCK

    def body(r_ref, w_ref, k_ref, v_ref, a_ref, b_ref, y_ref, ck_ref, s_ref):
        c = pl.program_id(0)

        @pl.when(c == 0)
        def _():
            s_ref[...] = jnp.zeros(s_ref.shape, F32)

        s0 = s_ref[...]
        ck_ref[0] = s0
        y, s1 = _wkv_block(s0, r_ref[...], w_ref[...], k_ref[...], v_ref[...], a_ref[...], b_ref[...], False)
        y_ref[...] = y
        s_ref[...] = s1

    blk = pl.BlockSpec((N_HEAD, WKV_BLOCK, HEAD), lambda c: (0, c, 0))
    return _pcall(
        body, name="wkv_fwd", grid=(nc,), in_specs=[blk] * 6,
        out_specs=[blk, pl.BlockSpec((1, N_HEAD, HEAD, HEAD), lambda c: (c, 0, 0, 0))],
        out_shape=[jax.ShapeDtypeStruct((N_HEAD, L, HEAD), F32), jax.ShapeDtypeStruct((nc, N_HEAD, HEAD, HEAD), F32)],
        scratch_shapes=[pltpu.VMEM((N_HEAD, HEAD, HEAD), F32)],
        compiler_params=_cparams(1))(r, w, k, v, a, b)


def _wkv_bwd(r, w, k, v, a, b, dy, ck, L, deps=()):
    nc = L // WKV_BLOCK

    def body(r_ref, w_ref, k_ref, v_ref, a_ref, b_ref, dy_ref, ck_ref, *rest):
        dr_ref, dw_ref, dk_ref, dv_ref, da_ref, db_ref, ds_ref = rest[len(deps):]
        c = pl.program_id(0)

        @pl.when(c == 0)
        def _():
            ds_ref[...] = jnp.zeros(ds_ref.shape, F32)

        _, vjp = jax.vjp(lambda *t: _wkv_block(*t, True), ck_ref[0], r_ref[...], w_ref[...], k_ref[...], v_ref[...],
                         a_ref[...], b_ref[...])
        g = vjp((dy_ref[...], ds_ref[...]))
        ds_ref[...] = g[0]
        for o_ref, val in zip((dr_ref, dw_ref, dk_ref, dv_ref, da_ref, db_ref), g[1:]):
            o_ref[...] = val

    blk = pl.BlockSpec((N_HEAD, WKV_BLOCK, HEAD), lambda c: (0, nc - 1 - c, 0))
    sh = jax.ShapeDtypeStruct((N_HEAD, L, HEAD), F32)
    return _pcall(
        body, name="wkv_bwd", grid=(nc,),
        in_specs=[blk] * 7 + [pl.BlockSpec((1, N_HEAD, HEAD, HEAD), lambda c: (nc - 1 - c, 0, 0, 0))]
        + [pl.BlockSpec(d.shape, lambda c, nd=d.ndim: (0,) * nd) for d in deps],
        out_specs=[blk] * 6, out_shape=[sh] * 6,
        scratch_shapes=[pltpu.VMEM((N_HEAD, HEAD, HEAD), F32)],
        compiler_params=_cparams(1))(r, w, k, v, a, b, dy, ck, *deps)


TB = 256


def _bf(x):
    return x.astype(BF16)


def _inproj_fwd(x, norm_mix, w_in, L, deps=()):
    def fn(i, tv, cv):
        xn = _rms(tv[0], cv[0])
        return _dot(_bf(xn), cv[1]), xn

    return _tok_call("inproj_fwd", fn, L, TB, [(x, D_MODEL, 0)], [norm_mix, w_in], [(IN_COLS, F32), (D_MODEL, BF16)],
                     deps=deps)


def _s5_post_fn(glu_w, wtop, diff=True):
    mg = _mmc(glu_w, diff)
    mt = _mmc(wtop, diff) if wtop is not None else None

    def f(y, glu_b, e):
        z = _gelu(y)
        out = z * _sigmoid(mg(z) + glu_b + e)
        res = mt(out) if mt is not None else out
        return res, (z, out)

    return f


def _s5_post_fwd(y, glu_w, glu_b, L):
    def fn(i, tv, cv):
        out, _ = _s5_post_fn(cv[0], None, False)(tv[0], cv[1], 0.0)
        return (out,)

    return _tok_call("s5_post_fwd", fn, L, TB, [(y, S5_WIDTH, 0)], [glu_w, glu_b], [(S5_WIDTH, F32)])[0]


def _s5_post_bwd(y, dh1, glu_w, glu_b, wtop, L, deps=()):
    def fn(i, tv, cv):
        e0 = jnp.zeros((TB, S5_WIDTH), F32)
        _, vjp, (z, out) = jax.vjp(_s5_post_fn(cv[0], cv[2]), tv[0], cv[1], e0, has_aux=True)
        dy, db, de = vjp(tv[1])
        return dy, db, _dot_tn(_bf(z), _bf(de)), _dot_tn(_bf(out), _bf(tv[1]))

    return _tok_call("s5_post_bwd", fn, L, TB, [(y, S5_WIDTH, 0), (dh1, D_MODEL, 0)], [glu_w, glu_b, wtop],
                     [(S5_WIDTH, F32)], [(1, S5_WIDTH), (S5_WIDTH, S5_WIDTH), (S5_WIDTH, D_MODEL)], deps=deps)


RW_COLBLK = ((RW_WIDTH, 1), (RW_WIDTH, 2), (RW_WIDTH, 3), (128, 16), (128, 17))
RW_MU = ((0, 512), (512, 1024), (1024, 1536), (1536, 1664), (1664, 1792))


def _rw_pre_fn(w2pad, a2pad, g2, diff=True):
    m_w, m_a, m_g = _mmc(w2pad, diff), _mmc(a2pad, diff), _mmc(g2, diff)
    seg = _segsum(_head_indicator(RW_WIDTH), diff)

    def f(zr, zk, zv, zwa, zg, w0, a0, k_k, k_a, e_w, e_a):
        wl_t = jnp.tanh(zwa)
        wlin = w0 + m_w(wl_t) + e_w
        w = -_softplus(-wlin) - 0.5
        decay = jnp.exp(-jnp.exp(w))
        a = _sigmoid(a0 + m_a(zwa) + e_a)
        sg = _sigmoid(zg)
        g = m_g(sg)
        kk = zk * k_k
        kkn = kk / jnp.maximum(jnp.sqrt(seg(kk * kk)), L2_EPS)
        kf = zk * (1.0 + (a - 1.0) * k_a)
        return (zr, decay, kf, zv, -kkn, kkn * a, g), (wl_t, sg)

    return f


def _rw_shifted(i, tv, mu):
    sub = lax.broadcasted_iota(jnp.int32, (TB, 1), 0)
    zs, dif = [], []
    for n in range(5):
        z = tv[n]
        last = jnp.where(i == 0, 0.0, tv[5 + n][7:8, :])
        prev = jnp.where(sub == 0, last, pltpu.roll(z, 1, 0))
        m = mu[:, RW_MU[n][0]:RW_MU[n][1]]
        zs.append(z + (prev - z) * m)
        dif.append(prev - z)
    return zs, dif


def _rw_tok_in(proj):
    return [(proj, wd, cb) for wd, cb in RW_COLBLK] + [(proj, wd, cb, "prev") for wd, cb in RW_COLBLK]


def _rw_pre_fwd(proj, mu, w0, a0, k_k, k_a, w2pad, a2pad, g2, L):
    def fn(i, tv, cv):
        zs, _ = _rw_shifted(i, tv, cv[0])
        outs, _ = _rw_pre_fn(cv[5], cv[6], cv[7], False)(*zs, cv[1], cv[2], cv[3], cv[4], 0.0, 0.0)
        return outs

    return _tok_call("rw_pre_fwd", fn, L, TB, _rw_tok_in(proj), [mu, w0, a0, k_k, k_a, w2pad, a2pad, g2],
                     [("heads", F32)] * 6 + [(RW_WIDTH, F32)])


def _rw_pre_bwd(proj, cots, mu, w0, a0, k_k, k_a, w2pad, a2pad, g2, L):
    def fn(i, tv, cv):
        zs, dif = _rw_shifted(i, tv[:10], cv[0])
        dr1, dr2, dw, dk1, dk2, dv1, dv2, da, db, dg = tv[10:]
        e0 = jnp.zeros((TB, RW_WIDTH), F32)
        _, vjp, (wl_t, sg) = jax.vjp(_rw_pre_fn(cv[5], cv[6], cv[7]), *zs, cv[1], cv[2], cv[3], cv[4], e0, e0, has_aux=True)
        g = vjp((dr1 + dr2, dw, dk1 + dk2, dv1 + dv2, da, db, dg))
        dzs = jnp.concatenate(g[:5], axis=1)
        dmu = jnp.concatenate([jnp.sum(g[n] * dif[n], axis=0, keepdims=True) for n in range(5)], axis=1)
        lora = (_dot_tn(_bf(wl_t), _bf(g[9])), _dot_tn(_bf(zs[3]), _bf(g[10])), _dot_tn(_bf(sg), _bf(dg)))
        return (dzs, dmu, g[5], g[6], g[7], g[8]) + lora

    tok_in = _rw_tok_in(proj) + [((c,) if c.ndim == 3 else (c, RW_WIDTH, 0)) for c in cots]
    return _tok_call("rw_pre_bwd", fn, L, TB, tok_in, [mu, w0, a0, k_k, k_a, w2pad, a2pad, g2],
                     [(SHIFT_COLS, F32)], [(1, SHIFT_COLS)] + [(1, RW_WIDTH)] * 4 + [(128, RW_WIDTH)] * 3)


def _rw_post_fn(wbot, diff=True):
    seg = _segsum(_head_indicator(RW_WIDTH), diff)
    mb = _mmc(wbot, diff) if wbot is not None else None

    def f(y, r, kf, v, g, ln_w, ln_b, r_k):
        mean = seg(y) * (1.0 / HEAD)
        yc = y - mean
        var = seg(yc * yc) * (1.0 / HEAD)
        yn = yc * lax.rsqrt(var + GN_EPS) * ln_w + ln_b
        bonus = seg(r * kf * r_k) * v
        out = (yn + bonus) * g
        res = mb(out) if mb is not None else out
        return res, out

    return f


def _rw_post_fwd(y, r, kf, v, g, ln_w, ln_b, r_k, L):
    def fn(i, tv, cv):
        out, _ = _rw_post_fn(None, False)(*tv, *cv)
        return (out,)

    return _tok_call("rw_post_fwd", fn, L, TB, [(t,) for t in (y, r, kf, v)] + [(g, RW_WIDTH, 0)], [ln_w, ln_b, r_k],
                     [(RW_WIDTH, F32)])[0]


def _rw_post_bwd(y, r, kf, v, g, dh1, ln_w, ln_b, r_k, wbot, L):
    def fn(i, tv, cv):
        _, vjp, out = jax.vjp(_rw_post_fn(cv[3]), *tv[:5], cv[0], cv[1], cv[2], has_aux=True)
        gr = vjp(tv[5])
        return gr[0], gr[1], gr[2], gr[3], gr[4], gr[5], gr[6], gr[7], _dot_tn(_bf(out), _bf(tv[5]))

    return _tok_call("rw_post_bwd", fn, L, TB, [(t,) for t in (y, r, kf, v)] + [(g, RW_WIDTH, 0), (dh1, D_MODEL, 0)],
                     [ln_w, ln_b, r_k, wbot], [("heads", F32)] + [(RW_WIDTH, F32)] * 4,
                     [(1, RW_WIDTH)] * 3 + [(RW_WIDTH, D_MODEL)])


def _ffn_fn(w1, w3, w2, diff=True):
    m1, m3, m2 = _mmc(w1, diff), _mmc(w3, diff), _mmc(w2, diff)

    def f(h1, norm_ffn, e1, e3):
        hn = _rms(h1, norm_ffn)
        a1 = m1(hn) + e1
        a3 = m3(hn) + e3
        hm = a1 * _sigmoid(a1) * a3
        return h1 + m2(hm), (hn, hm)

    return f


TB_FFN = 256


def _mixffn_fwd(x, s5_out, rw_out, wtop, wbot, norm_ffn, w1, w3, w2, L):
    def fn(i, tv, cv):
        h1 = tv[0] + _dot(_bf(tv[1]), cv[0]) + _dot(_bf(tv[2]), cv[1])
        h2, _ = _ffn_fn(cv[3], cv[4], cv[5], False)(h1, cv[2], 0.0, 0.0)
        return h1, h2

    return _tok_call("mixffn_fwd", fn, L, TB_FFN, [(x, D_MODEL, 0), (s5_out, S5_WIDTH, 0), (rw_out, RW_WIDTH, 0)],
                     [wtop, wbot, norm_ffn, w1, w3, w2], [(D_MODEL, F32), (D_MODEL, F32)])


def _ffn_bwd(h1, dh2, norm_ffn, w1, w3, w2, L):
    def fn(i, tv, cv):
        e0 = jnp.zeros((TB_FFN, FFN_HIDDEN), F32)
        _, vjp, (hn, hm) = jax.vjp(_ffn_fn(cv[1], cv[2], cv[3]), tv[0], cv[0], e0, e0, has_aux=True)
        dh1, dn, d1, d3 = vjp(tv[1])
        return dh1, d1, d3, hm, hn, dn

    return _tok_call("ffn_bwd", fn, L, TB_FFN, [(h1, D_MODEL, 0), (dh2, D_MODEL, 0)], [norm_ffn, w1, w3, w2],
                     [(D_MODEL, F32), (FFN_HIDDEN, BF16), (FFN_HIDDEN, BF16), (FFN_HIDDEN, BF16), (D_MODEL, BF16)],
                     [(1, D_MODEL)])


def _ple_loss_fb(h2, p, target, norm_ple, final_norm, wg, wu, L):
    def fn(i, tv, cv):
        mgate, mup = _mmc(cv[2]), _mmc(cv[3], False)

        def f(h2_, norm_ple_, final_norm_, eg, eu):
            hn = _rms(h2_, norm_ple_)
            gate = _sigmoid(mgate(hn) + eg)
            h3 = h2_ + gate * (mup(tv[1]) + eu)
            out = _rms(h3, final_norm_)
            d = out - tv[2]
            return 0.5 * jnp.sum(jnp.mean(d * d, axis=-1, keepdims=True)), hn

        e0 = jnp.zeros((TB, D_MODEL), F32)
        loss, vjp, hn = jax.vjp(f, tv[0], cv[0], cv[1], e0, e0, has_aux=True)
        dh2, dnp, dfn, deg, deu = vjp(jnp.ones((), F32))
        return (dh2, dh2, jnp.full((8, 128), loss, F32), dnp, dfn,
                _dot_tn(_bf(hn), _bf(deg)), _dot_tn(_bf(tv[1]), _bf(deu)))

    return _tok_call("ple_loss_fb", fn, L, TB, [(h2, D_MODEL, 0), (p, PLE_DIM, 0), (target, D_MODEL, 0)],
                     [norm_ple, final_norm, wg, wu], [(D_MODEL, F32), (D_MODEL, BF16)],
                     [(8, 128), (1, D_MODEL), (1, D_MODEL), (D_MODEL, D_MODEL), (PLE_DIM, D_MODEL)])


def _inproj_bwd(x, dh1, du, dzs, norm_mix, mu, w_u, w_z, L):
    nb = L // TB

    def fn(i, tv, cv):
        sub = lax.broadcasted_iota(jnp.int32, (TB, 1), 0)
        m = cv[1]
        b = tv[3] * m
        nxt = jnp.where(i == nb - 1, 0.0, tv[4][0:1, :] * m)
        dz = tv[3] * (1.0 - m) + jnp.where(sub == TB - 1, nxt, pltpu.roll(b, TB - 1, 0))
        dub, dzb = _bf(tv[2]), _bf(dz)
        dxn = _dot_nt(dub, cv[2]) + _dot_nt(dzb, cv[3])
        _, vjp = jax.vjp(_rms, tv[0], cv[0])
        dx, dn = vjp(dxn)
        return tv[1] + dx, jnp.concatenate([dub, dzb], axis=1), dn

    return _tok_call("inproj_bwd", fn, L, TB,
                     [(x, D_MODEL, 0), (dh1, D_MODEL, 0), (du, S5_WIDTH, 0), (dzs, SHIFT_COLS, 0), (dzs, SHIFT_COLS, 0, "next")],
                     [norm_mix, mu, w_u, w_z], [(D_MODEL, F32), (IN_COLS, BF16)], [(1, D_MODEL)])


def _eye8(dt):
    return jnp.eye(8, dtype=dt)


def _quarter_b(bb):
    return jnp.einsum("hg,qgcp->qhcgp", _eye8(bb.dtype), bb.reshape(S5_Q, 8, S5_GROUP, S5_STATE)).reshape(S5_Q, S5_QL, S5_QS)


def _unquarter_b(d):
    return jnp.einsum("qhcgp,hg->qgcp", d.reshape(S5_Q, 8, S5_GROUP, 8, S5_STATE), _eye8(d.dtype)).reshape(
        S5_GROUPS, S5_GROUP, S5_STATE)


def _quarter_c(c):
    return jnp.einsum("gh,qgcp->qgphc", _eye8(c.dtype), c.reshape(S5_Q, 8, S5_GROUP, S5_STATE)).reshape(S5_Q, S5_QS, S5_QL)


def _unquarter_c(d):
    return jnp.einsum("qgphc,gh->qgcp", d.reshape(S5_Q, 8, S5_STATE, 8, S5_GROUP), _eye8(d.dtype)).reshape(
        S5_GROUPS, S5_GROUP, S5_STATE)


def _local_step(x, p, target, W, late_weights=None, grads_ready=None, first_dep=None):
    L = x.shape[0]
    r2 = lambda v: v.reshape(1, -1)
    w_in = W["w_in"]
    w2pad = jnp.pad(W["rw_w2"], ((0, 64), (0, 0)))
    a2pad = jnp.pad(W["rw_a2"], ((64, 0), (0, 0)))
    mu = r2(W["rw_shift_mu"])
    rw_vec = [r2(W[n]) for n in ("rw_w0", "rw_a0", "rw_k_k", "rw_k_a")]
    ln_w, ln_b, r_k = r2(W["rw_ln_w"]), r2(W["rw_ln_b"]), r2(W["rw_r_k"])

    lam_re, lam_im = W["s5_lam_re"], W["s5_lam_im"]
    log_step = W["s5_log_step"].reshape(S5_GROUPS, 1)
    bt_re, bt_im = W["s5_b_re"].transpose(0, 2, 1), W["s5_b_im"].transpose(0, 2, 1)
    lb_re, lb_im, bb_re, bb_im = _s5_param_fwd(lam_re, lam_im, log_step, bt_re, bt_im)
    bq_re, bq_im = _quarter_b(bb_re).astype(BF16), _quarter_b(bb_im).astype(BF16)
    cq_re, cq_im = _quarter_c(W["s5_c_re"]).astype(BF16), _quarter_c(W["s5_c_im"]).astype(BF16)
    lbar = jnp.concatenate([lb_re.reshape(1, -1), lb_im.reshape(1, -1), jnp.zeros((6, S5_LANES), F32)], axis=0)
    dskip = r2(W["s5_d"])
    glu_b = r2(W["s5_glu_b"])
    norm_mix, norm_ffn, norm_ple, final_norm = (r2(W[n]) for n in ("norm_mix", "norm_ffn", "norm_ple", "final_norm"))

    proj, xn = _inproj_fwd(x, norm_mix, w_in, L, () if first_dep is None else (first_dep,))
    y_s5, ck5 = _s5_scan_fwd(proj, bq_re, bq_im, cq_re, cq_im, lbar, dskip, L, TB)
    s5_out = _s5_post_fwd(y_s5, W["s5_glu_w"], glu_b, L)
    r, wd, kf, v, a_s, b_s, g = _rw_pre_fwd(proj, mu, *rw_vec, w2pad, a2pad, W["rw_g2"], L)
    scan_in = (r, wd, kf, v, a_s, b_s)
    y_wkv, ckw = _wkv_fwd(*scan_in, L)
    rw_out = _rw_post_fwd(y_wkv, r, kf, v, g, ln_w, ln_b, r_k, L)
    if late_weights is not None:
        W = dict(W, **late_weights(rw_out))
    wtop, wbot = W["w_out"][:S5_WIDTH], W["w_out"][S5_WIDTH:]
    h1, h2 = _mixffn_fwd(x, s5_out, rw_out, wtop, wbot, norm_ffn, W["ffn_w1"], W["ffn_w3"], W["ffn_w2"], L)

    G = {}
    dh2, dh2_bf, loss_acc, G["norm_ple"], G["final_norm"], G["ple_gate_w"], G["ple_up_w"] = _ple_loss_fb(
        h2, p, target, norm_ple, final_norm, W["ple_gate_w"], W["ple_up_w"], L)
    dh1, da1, da3, hm, hn_ffn, G["norm_ffn"] = _ffn_bwd(h1, dh2, norm_ffn, W["ffn_w1"], W["ffn_w3"], W["ffn_w2"], L)
    G["ffn_w1"] = _mm_tn("dw_ffn_w1", hn_ffn, da1)
    G["ffn_w3"] = _mm_tn("dw_ffn_w3", hn_ffn, da3)
    G["ffn_w2"] = _mm_tn("dw_ffn_w2", hm, dh2_bf)
    dep_a = grads_ready(0, G) if grads_ready is not None else None
    dy_s5, G["s5_glu_b"], G["s5_glu_w"], d_wtop = _s5_post_bwd(y_s5, dh1, W["s5_glu_w"], glu_b, wtop, L,
                                                               () if dep_a is None else (dep_a,))
    dy_wkv, dr2, dk2, dv2, dg, G["rw_ln_w"], G["rw_ln_b"], G["rw_r_k"], d_wbot = _rw_post_bwd(
        y_wkv, r, kf, v, g, dh1, ln_w, ln_b, r_k, wbot, L)
    G["w_out"] = jnp.concatenate([d_wtop, d_wbot], axis=0)
    dep = grads_ready(1, G) if grads_ready is not None else None
    dr1, dwd, dk1, dv1, da_s, db_s = _wkv_bwd(*scan_in, dy_wkv, ckw, L, () if dep is None else (dep,))
    (dzs, G["rw_shift_mu"], G["rw_w0"], G["rw_a0"], G["rw_k_k"], G["rw_k_a"], d_w2pad, d_a2pad, G["rw_g2"]) = _rw_pre_bwd(
        proj, (dr1, dr2, dwd, dk1, dk2, dv1, dv2, da_s, db_s, dg), mu, *rw_vec, w2pad, a2pad, W["rw_g2"], L)
    G["rw_w2"], G["rw_a2"] = d_w2pad[:64], d_a2pad[64:]
    du, dbq_re, dbq_im, dcq_re, dcq_im, dlbar, G["s5_d"] = _s5_scan_bwd(
        proj, dy_s5, ck5, bq_re, bq_im, cq_re, cq_im, lbar, dskip, L, TB)
    G["s5_c_re"], G["s5_c_im"] = _unquarter_c(dcq_re), _unquarter_c(dcq_im)
    d_lam_re, d_lam_im, d_ls, d_bt_re, d_bt_im = _s5_param_bwd(
        lam_re, lam_im, log_step, bt_re, bt_im, dlbar[0].reshape(S5_GROUPS, S5_STATE), dlbar[1].reshape(S5_GROUPS, S5_STATE),
        _unquarter_b(dbq_re), _unquarter_b(dbq_im))
    G["s5_lam_re"], G["s5_lam_im"], G["s5_log_step"] = d_lam_re, d_lam_im, d_ls.reshape(S5_GROUPS)
    G["s5_b_re"], G["s5_b_im"] = d_bt_re.transpose(0, 2, 1), d_bt_im.transpose(0, 2, 1)
    dx, dproj, G["norm_mix"] = _inproj_bwd(x, dh1, du, dzs, norm_mix, mu, w_in[:, :S5_WIDTH], w_in[:, S5_WIDTH:], L)
    G["w_in"] = _mm_tn("dw_in", xn, dproj)
    return loss_acc[0, 0], dx, G


MESH_AXES = ("x", "y", "c")


_HBM = pl.BlockSpec(memory_space=pltpu.HBM)
_SEM = pl.BlockSpec(memory_space=pltpu.SEMAPHORE)
_EFFECT = pltpu.SideEffectType.DATAFLOW_SIDE_EFFECTING


def _peer_of(k):
    x, y, c = lax.axis_index("x"), lax.axis_index("y"), lax.axis_index("c")
    px, py, pc = x ^ ((k >> 2) & 1), y ^ ((k >> 1) & 1), c ^ (k & 1)
    return (px, py, pc), 4 * px + 2 * py + pc, 4 * x + 2 * y + c


def _direct_copy(t, k, src_refs, land_refs, send_sems, recv_sems, rows_of, gather):
    dev, peer, me = _peer_of(k)
    m = rows_of[t]
    src = src_refs[t] if gather else src_refs[t].at[pl.ds(peer * m, m), :]
    return pltpu.make_async_remote_copy(
        src_ref=src, dst_ref=land_refs[t].at[pl.ds(me * m, m), :],
        send_sem=send_sems.at[7 * t + k - 1], recv_sem=recv_sems.at[7 * t + k - 1],
        device_id=dev, device_id_type=pl.DeviceIdType.MESH)


def _direct_landing(t, k, src_refs, land_refs, send_sems, recv_sems, rows_of, gather):
    dev, peer, me = _peer_of(k)
    m = rows_of[t]
    src = src_refs[t] if gather else src_refs[t].at[pl.ds(me * m, m), :]
    return pltpu.make_async_remote_copy(
        src_ref=src, dst_ref=land_refs[t].at[pl.ds(peer * m, m), :],
        send_sem=send_sems.at[7 * t + k - 1], recv_sem=recv_sems.at[7 * t + k - 1],
        device_id=dev, device_id_type=pl.DeviceIdType.MESH)


def _direct_start(name, srcs, gather, dep=None):
    nt = len(srcs)
    rows_of = [a.shape[0] if gather else a.shape[0] // N_DEV for a in srcs]
    lands = [pltpu.with_memory_space_constraint(lax.empty((N_DEV * m, a.shape[1]), a.dtype), pltpu.HBM)
             for a, m in zip(srcs, rows_of)]

    n_dep = 0 if dep is None else 1

    def body(*refs):
        src_refs, land_refs = refs[:nt], refs[nt:2 * nt]
        send_sems, recv_sems = refs[2 * nt + n_dep], refs[2 * nt + n_dep + 1]
        token = refs[-1]
        for t in range(nt):
            for k in range(1, N_DEV):
                _direct_copy(t, k, src_refs, land_refs, send_sems, recv_sems, rows_of, gather).start()
        token[...] = jnp.zeros(token.shape, F32)

    out = _pcall(
        body, name=name,
        out_shape=(pltpu.SemaphoreType.DMA((7 * nt,)), pltpu.SemaphoreType.DMA((7 * nt,)),
                   *[pltpu.HBM(a.shape, a.dtype) for a in srcs], *[pltpu.HBM(a.shape, a.dtype) for a in lands],
                   jax.ShapeDtypeStruct((8, 128), F32)),
        in_specs=(_HBM,) * (2 * nt) + (pl.BlockSpec(memory_space=pl.ANY),) * n_dep,
        out_specs=(_SEM, _SEM) + (_HBM,) * (2 * nt) + (pl.BlockSpec(memory_space=pltpu.VMEM),),
        input_output_aliases={i: 2 + i for i in range(2 * nt)},
        compiler_params=pltpu.CompilerParams(has_side_effects=_EFFECT),
    )(*[pltpu.with_memory_space_constraint(a, pltpu.HBM) for a in srcs], *lands, *(() if dep is None else (dep,)))
    return (out[0], out[1], list(out[2:2 + nt]), list(out[2 + nt:2 + 2 * nt]), rows_of, gather), out[-1]


def _direct_wait(name, handle, after):
    send_sems, recv_sems, srcs, lands, rows_of, gather = handle
    nt = len(srcs)
    after = list(after) if isinstance(after, (list, tuple)) else [after]

    def body(*refs):
        src_refs, land_refs = refs[:nt], refs[nt:2 * nt]
        s_sems, r_sems = refs[2 * nt], refs[2 * nt + 1]
        for t in range(nt):
            for k in range(1, N_DEV):
                _direct_copy(t, k, src_refs, land_refs, s_sems, r_sems, rows_of, gather).wait_send()
                _direct_landing(t, k, src_refs, land_refs, s_sems, r_sems, rows_of, gather).wait_recv()

    out = _pcall(
        body, name=name,
        out_shape=tuple(pltpu.HBM(a.shape, a.dtype) for a in srcs) + tuple(pltpu.HBM(a.shape, a.dtype) for a in lands),
        in_specs=(_HBM,) * (2 * nt) + (_SEM, _SEM) + (pl.BlockSpec(memory_space=pl.ANY),) * len(after),
        out_specs=(_HBM,) * (2 * nt),
        input_output_aliases={i: i for i in range(2 * nt)},
        compiler_params=pltpu.CompilerParams(has_side_effects=_EFFECT),
    )(*srcs, *lands, send_sems, recv_sems, *after)
    return list(out[:nt]), list(out[nt:])


def _adamw_sharded(name, own, parts, w, m, v, rb, deps=()):
    R, N = own.shape

    def body(o_ref, p_ref, w_ref, m_ref, v_ref, *rest):
        g_ref, d_ref, nm_ref, nv_ref = rest[len(deps):]
        me = 4 * lax.axis_index("x") + 2 * lax.axis_index("y") + lax.axis_index("c")
        g = o_ref[...]
        for k in range(1, N_DEV):
            g = g + p_ref[me ^ k].astype(F32)
        nm = ADAM_B1 * m_ref[...] + (1.0 - ADAM_B1) * g
        nv = ADAM_B2 * v_ref[...] + (1.0 - ADAM_B2) * (g * g)
        m_hat = nm / (1.0 - ADAM_B1 ** ADAM_STEP)
        v_hat = nv / (1.0 - ADAM_B2 ** ADAM_STEP)
        g_ref[...] = g
        d_ref[...] = -ADAM_LR * (m_hat / (jnp.sqrt(v_hat) + ADAM_EPS) + ADAM_WD * w_ref[...])
        nm_ref[...] = nm
        nv_ref[...] = nv

    blk = pl.BlockSpec((rb, N), lambda i: (i, 0))
    sh = jax.ShapeDtypeStruct((R, N), F32)
    return _pcall(body, name=name, grid=(R // rb,),
                  in_specs=[blk, pl.BlockSpec((N_DEV, rb, N), lambda i: (0, i, 0)), blk, blk, blk]
                  + [pl.BlockSpec(d.shape, lambda i, nd=d.ndim: (0,) * nd) for d in deps],
                  out_specs=[blk] * 4, out_shape=[sh] * 4, compiler_params=_cparams(1))(own, parts, w, m, v, *deps)


SMALL_CLASSES = (
    (("s5_b_re", 32, 1024), ("s5_b_im", 32, 1024),
     ("norm_mix", 1, 1024), ("norm_ffn", 1, 1024), ("norm_ple", 1, 1024), ("final_norm", 1, 1024)),
    (("s5_d", 1, 512), ("s5_glu_b", 1, 512), ("rw_w0", 1, 512), ("rw_a0", 1, 512), ("rw_k_k", 1, 512), ("rw_k_a", 1, 512),
     ("rw_ln_w", 1, 512), ("rw_ln_b", 1, 512), ("rw_r_k", 1, 512)),
    (("rw_shift_mu", 1, 1792),),
    (("s5_lam_re", 32, 64), ("s5_lam_im", 32, 64), ("s5_c_re", 512, 64), ("s5_c_im", 512, 64)),
    (("s5_log_step", 1, 32),),
)


def _class_rows(cls):
    return -(-sum(r for _, r, _ in cls) // 8) * 8


def _stack_class(cls, arrs):
    a = jnp.concatenate(arrs, axis=0) if len(arrs) > 1 else arrs[0]
    pad = _class_rows(cls) - a.shape[0]
    return jnp.pad(a, ((0, pad), (0, 0))) if pad else a


def _adamw_small(grads, w, m, v):
    names = [n for cls in SMALL_CLASSES for n, _, _ in cls]
    n_cls, n_par = len(SMALL_CLASSES), len(names)

    def body(*refs):
        g_refs = refs[:n_cls]
        w_refs, m_refs, v_refs = (refs[n_cls + i * n_par:n_cls + (i + 1) * n_par] for i in range(3))
        o_refs = refs[n_cls + 3 * n_par:]
        p = 0
        for cls, g_ref in zip(SMALL_CLASSES, g_refs):
            rc = _class_rows(cls)
            tot = g_ref[0:rc, :]
            for s_ in range(1, N_DEV):
                tot = tot + g_ref[s_ * rc:(s_ + 1) * rc, :]
            off = 0
            for _, r, _ in cls:
                g = tot[off:off + r, :]
                off += r
                nm = ADAM_B1 * m_refs[p][...] + (1.0 - ADAM_B1) * g
                nv = ADAM_B2 * v_refs[p][...] + (1.0 - ADAM_B2) * (g * g)
                m_hat = nm / (1.0 - ADAM_B1 ** ADAM_STEP)
                v_hat = nv / (1.0 - ADAM_B2 ** ADAM_STEP)
                o_refs[4 * p][...] = g
                o_refs[4 * p + 1][...] = -ADAM_LR * (m_hat / (jnp.sqrt(v_hat) + ADAM_EPS) + ADAM_WD * w_refs[p][...])
                o_refs[4 * p + 2][...] = nm
                o_refs[4 * p + 3][...] = nv
                p += 1

    shapes = [(r, c) for cls in SMALL_CLASSES for _, r, c in cls]
    out = _pcall(body, name="adamw_replicated",
                 out_shape=[jax.ShapeDtypeStruct(sh, F32) for sh in shapes for _ in range(4)],
                 compiler_params=pltpu.CompilerParams(vmem_limit_bytes=VMEM_LIMIT))(*grads, *w, *m, *v)
    return {n: out[4 * i:4 * i + 4] for i, n in enumerate(names)}


EARLY = (("w_in", True),)
LATE = (("ffn_w1", True), ("ffn_w3", True), ("ffn_w2", False), ("ple_gate_w", False), ("w_out", False))
GRAD_STAGES = (LATE[:4], LATE[4:])
MISC = (("s5_glu_w", False), ("rw_w2", True), ("rw_a2", True), ("rw_g2", True), ("ple_up_w", True))
SHARDED_NAMES = tuple(n for n, _ in EARLY + LATE + MISC)
PACK_COLS = 1024
WEIGHT_NAMES = ("norm_mix", "w_in", "s5_lam_re", "s5_lam_im", "s5_log_step", "s5_b_re", "s5_b_im", "s5_c_re", "s5_c_im", "s5_d",
                "s5_glu_w", "s5_glu_b", "rw_shift_mu", "rw_w0", "rw_w2", "rw_a0", "rw_a2", "rw_g2", "rw_k_k", "rw_k_a", "rw_r_k",
                "rw_ln_w", "rw_ln_b", "w_out", "norm_ffn", "ffn_w1", "ffn_w3", "ffn_w2", "norm_ple", "ple_gate_w", "ple_up_w",
                "final_norm")
SMALL_NAMES = tuple(n for n in WEIGHT_NAMES if n not in SHARDED_NAMES)
ARG_NAMES = ("x", "p") + WEIGHT_NAMES + ("loss_target",) + tuple("m_" + n for n in WEIGHT_NAMES) + tuple("v_" + n for n in WEIGHT_NAMES)


def _travel(a, tr):
    return a.T if tr else a


def _pack_misc(blocks):
    lead = blocks[0].shape[:-2]
    return jnp.concatenate([b.reshape(lead + (-1, PACK_COLS)) for b in blocks], axis=len(lead))


def _unpack_misc(packed, shapes):
    lead = packed.shape[:-2]
    out, off = [], 0
    for r, c in shapes:
        n = r * c // PACK_COLS
        out.append(lax.slice_in_dim(packed, off, off + n, axis=len(lead)).reshape(lead + (r, c)))
        off += n
    return out


def _kernel_impl(ins):
    x, p, target = ins["x"][0], ins["p"][0, 0], ins["loss_target"][0]
    me = 4 * lax.axis_index("x") + 2 * lax.axis_index("y") + lax.axis_index("c")
    small = {n: (ins[n] if n == "final_norm" else ins[n][0]) for n in SMALL_NAMES}
    trav = lambda pre, n, tr: _travel(ins[pre + n][0], tr)
    misc_shapes = [trav("", n, tr).shape for n, tr in MISC]

    early_shards = [trav("", n, tr).astype(BF16) for n, tr in EARLY] + [_pack_misc([trav("", n, tr).astype(BF16) for n, tr in MISC])]
    early_handle_w, early_token_w = _direct_start("ag_early_start", early_shards, True)
    late_handle, late_token = _direct_start("ag_late_start", [trav("", n, tr).astype(BF16) for n, tr in LATE], True, early_token_w)
    opt_in = {n: tuple(trav(pre, n, tr) for pre in ("", "m_", "v_")) for n, tr in EARLY + LATE}
    pm = lambda pre: _pack_misc([trav(pre, n, tr) for n, tr in MISC])
    opt_misc = (pm(""), pm("m_"), pm("v_"))
    shards_w, lands_w = _direct_wait("ag_early_wait", early_handle_w, [a for t in opt_in.values() for a in t] + list(opt_misc))
    early = [lax.dynamic_update_slice_in_dim(ld, sh, me * sh.shape[0], axis=0) for ld, sh in zip(lands_w, shards_w)]
    W = dict(small)
    for (n, tr), g in zip(EARLY, early):
        W[n] = _travel(g, tr)
    for (n, tr), g in zip(MISC, _unpack_misc(early[-1].reshape(N_DEV, -1, PACK_COLS), misc_shapes)):
        W[n] = _travel(g.reshape(-1, g.shape[-1]), tr)

    def late_weights(after):
        shards, lands = _direct_wait("ag_late_wait", late_handle, after)
        full = [lax.dynamic_update_slice_in_dim(ld, sh, me * sh.shape[0], axis=0) for ld, sh in zip(lands, shards)]
        return {n: _travel(g, tr) for (n, tr), g in zip(LATE, full)}

    gt = lambda G, n, tr: _travel(G[n], tr)
    started = {}

    def grads_ready(stage, G):
        full = [gt(G, n, tr) for n, tr in GRAD_STAGES[stage]]
        started[stage] = (full, *_direct_start("grad_late_start%d" % stage, [a.astype(BF16) for a in full], False))
        return started[stage][2]

    loss_part, dx, G = _local_step(x, p, target, W, late_weights, grads_ready, late_token)

    misc_g = _pack_misc([gt(G, n, tr).reshape((N_DEV,) + shp) for (n, tr), shp in zip(MISC, misc_shapes)])
    early_full = [gt(G, n, tr) for n, tr in EARLY] + [misc_g.reshape(-1, PACK_COLS)]
    early_handle, early_token = _direct_start("grad_early_start", [a.astype(BF16) for a in early_full], False)
    view2 = lambda a, r, c: a.reshape(r, c)
    small_own = [_stack_class(cls, [view2(G[n], r, c) for n, r, c in cls]) for cls in SMALL_CLASSES]
    small_handle, small_token = _direct_start("grad_small_start", small_own, True)
    late_src, late_land = [], []
    for stage in range(len(GRAD_STAGES)):
        full, handle, _ = started[stage]
        _, land = _direct_wait("grad_late_wait%d" % stage, handle, small_token)
        late_src += full
        late_land += land

    outs = {}

    def emit(names_shapes, res):
        for tag, val in zip(("grad_", "delta_", "new_m_", "new_v_"), res):
            for n, v in names_shapes(val):
                outs[tag + n] = v

    def sharded_update(n, tr, src, land, deps=()):
        rows = src.shape[0] // N_DEV
        own = lax.dynamic_slice_in_dim(src, me * rows, rows, axis=0)
        res = _adamw_sharded("adamw_" + n, own, land.reshape(N_DEV, rows, land.shape[1]), *opt_in[n], _pick_rows(rows), deps)
        emit(lambda val: [(n, _travel(val, tr).reshape(ins[n].shape))], res)
        return list(res)

    for (n, tr), src, land in zip(LATE, late_src, late_land):
        sharded_update(n, tr, src, land, (early_token,))
    _, early_land = _direct_wait("grad_early_wait", early_handle, list(outs.values()))
    for (n, tr), src, land in zip(EARLY, early_full[:-1], early_land[:-1]):
        sharded_update(n, tr, src, land)
    rows = early_full[-1].shape[0] // N_DEV
    res = _adamw_sharded("adamw_misc", lax.dynamic_slice_in_dim(early_full[-1], me * rows, rows, axis=0),
                         early_land[-1].reshape(N_DEV, rows, PACK_COLS), *opt_misc, rows)
    emit(lambda val: [(n, _travel(b, tr).reshape(ins[n].shape)) for (n, tr), b in zip(MISC, _unpack_misc(val, misc_shapes))], res)
    small_src, small_land = _direct_wait("grad_small_wait", small_handle, res[0])
    small_all = [lax.dynamic_update_slice_in_dim(ld, sr, me * sr.shape[0], axis=0) for ld, sr in zip(small_land, small_src)]
    flat_small = [(n, r, c) for cls in SMALL_CLASSES for n, r, c in cls]
    res = _adamw_small(small_all, *[[view2(ins[pre + n], r, c) for n, r, c in flat_small] for pre in ("", "m_", "v_")])
    for n, _, _ in flat_small:
        for tag, val in zip(("grad_", "delta_", "new_m_", "new_v_"), res[n]):
            outs[tag + n] = val.reshape(ins[n].shape)
    loss = lax.psum(loss_part, MESH_AXES)
    res = [loss, dx[None]]
    for tag in ("grad_", "delta_", "new_m_", "new_v_"):
        res += [outs[tag + n] for n in WEIGHT_NAMES]
    return tuple(res)


def _pick_rows(r):
    best = 8
    for b in range(8, 257, 8):
        if r % b == 0:
            best = b
    return best


def kernel(x, p, norm_mix, w_in, s5_lam_re, s5_lam_im, s5_log_step, s5_b_re, s5_b_im, s5_c_re, s5_c_im, s5_d, s5_glu_w, s5_glu_b, rw_shift_mu, rw_w0, rw_w2, rw_a0, rw_a2, rw_g2, rw_k_k, rw_k_a, rw_r_k, rw_ln_w, rw_ln_b, w_out, norm_ffn, ffn_w1, ffn_w3, ffn_w2, norm_ple, ple_gate_w, ple_up_w, final_norm, loss_target, m_norm_mix, m_w_in, m_s5_lam_re, m_s5_lam_im, m_s5_log_step, m_s5_b_re, m_s5_b_im, m_s5_c_re, m_s5_c_im, m_s5_d, m_s5_glu_w, m_s5_glu_b, m_rw_shift_mu, m_rw_w0, m_rw_w2, m_rw_a0, m_rw_a2, m_rw_g2, m_rw_k_k, m_rw_k_a, m_rw_r_k, m_rw_ln_w, m_rw_ln_b, m_w_out, m_norm_ffn, m_ffn_w1, m_ffn_w3, m_ffn_w2, m_norm_ple, m_ple_gate_w, m_ple_up_w, m_final_norm, v_norm_mix, v_w_in, v_s5_lam_re, v_s5_lam_im, v_s5_log_step, v_s5_b_re, v_s5_b_im, v_s5_c_re, v_s5_c_im, v_s5_d, v_s5_glu_w, v_s5_glu_b, v_rw_shift_mu, v_rw_w0, v_rw_w2, v_rw_a0, v_rw_a2, v_rw_g2, v_rw_k_k, v_rw_k_a, v_rw_r_k, v_rw_ln_w, v_rw_ln_b, v_w_out, v_norm_ffn, v_ffn_w1, v_ffn_w3, v_ffn_w2, v_norm_ple, v_ple_gate_w, v_ple_up_w, v_final_norm):
    return _kernel_impl(dict(zip(ARG_NAMES, (x, p, norm_mix, w_in, s5_lam_re, s5_lam_im, s5_log_step, s5_b_re, s5_b_im, s5_c_re, s5_c_im, s5_d, s5_glu_w, s5_glu_b, rw_shift_mu, rw_w0, rw_w2, rw_a0, rw_a2, rw_g2, rw_k_k, rw_k_a, rw_r_k, rw_ln_w, rw_ln_b, w_out, norm_ffn, ffn_w1, ffn_w3, ffn_w2, norm_ple, ple_gate_w, ple_up_w, final_norm, loss_target, m_norm_mix, m_w_in, m_s5_lam_re, m_s5_lam_im, m_s5_log_step, m_s5_b_re, m_s5_b_im, m_s5_c_re, m_s5_c_im, m_s5_d, m_s5_glu_w, m_s5_glu_b, m_rw_shift_mu, m_rw_w0, m_rw_w2, m_rw_a0, m_rw_a2, m_rw_g2, m_rw_k_k, m_rw_k_a, m_rw_r_k, m_rw_ln_w, m_rw_ln_b, m_w_out, m_norm_ffn, m_ffn_w1, m_ffn_w3, m_ffn_w2, m_norm_ple, m_ple_gate_w, m_ple_up_w, m_final_norm, v_norm_mix, v_w_in, v_s5_lam_re, v_s5_lam_im, v_s5_log_step, v_s5_b_re, v_s5_b_im, v_s5_c_re, v_s5_c_im, v_s5_d, v_s5_glu_w, v_s5_glu_b, v_rw_shift_mu, v_rw_w0, v_rw_w2, v_rw_a0, v_rw_a2, v_rw_g2, v_rw_k_k, v_rw_k_a, v_rw_r_k, v_rw_ln_w, v_rw_ln_b, v_w_out, v_norm_ffn, v_ffn_w1, v_ffn_w3, v_ffn_w2, v_norm_ple, v_ple_gate_w, v_ple_up_w, v_final_norm))))
```

```python
import functools

import jax
import jax.numpy as jnp
from jax import lax
from jax.experimental import pallas as pl
from jax.experimental.pallas import tpu as pltpu

F32 = jnp.float32
BF16 = jnp.bfloat16

D_MODEL = 1024
S5_WIDTH = 512
RW_WIDTH = 512
S5_GROUP = 16
S5_GROUPS = 32
S5_STATE = 64
S5_LANES = S5_GROUPS * S5_STATE
HEAD = 64
SHIFT_COLS = 1792
IN_COLS = 2304
FFN_HIDDEN = 2816
PLE_DIM = 256
RMS_EPS = 1e-6
GN_EPS = 64e-5
L2_EPS = 1e-12
CHUNK = 64
N_DEV = 8

ADAM_LR = 0.001
ADAM_B1 = 0.9
ADAM_B2 = 0.999
ADAM_EPS = 1e-08
ADAM_WD = 0.01
ADAM_STEP = 10

VMEM_LIMIT = 56 * 1024 * 1024
_ANY = pl.BlockSpec(memory_space=pl.ANY)


def _pcall(body, **kw):
    return pl.pallas_call(body, **kw)


def _cparams(n_grid):
    return pltpu.CompilerParams(dimension_semantics=("arbitrary",) * n_grid, vmem_limit_bytes=VMEM_LIMIT)


def _dot(a, b):
    return jnp.dot(a, b, preferred_element_type=F32)


def _dot_nt(a, b):
    return lax.dot_general(a, b, (((1,), (1,)), ((), ())), preferred_element_type=F32)


def _dot_tn(a, b):
    return lax.dot_general(a, b, (((0,), (0,)), ((), ())), preferred_element_type=F32)


def _mmc(w, diff=True, tr=False):
    fw, bw = (_dot_nt, _dot) if tr else (_dot, _dot_nt)
    if not diff:
        return lambda x: fw(x.astype(BF16), w)

    @jax.custom_vjp
    def f(x):
        return fw(x.astype(BF16), w)

    def fwd(x):
        return fw(x.astype(BF16), w), None

    def bwd(_, dy):
        return (bw(dy.astype(BF16), w),)

    f.defvjp(fwd, bwd)
    return f


def _split_dot(x, m, n_split):
    acc = None
    rem = x
    for s in range(n_split):
        part = rem.astype(BF16)
        t = _dot(part, m)
        acc = t if acc is None else acc + t
        if s + 1 < n_split:
            rem = rem - part.astype(F32)
    return acc


def _segsum(m, diff=True):
    if not diff:
        return lambda x: _split_dot(x, m, 2)

    @jax.custom_vjp
    def f(x):
        return _split_dot(x, m, 2)

    def fwd(x):
        return _split_dot(x, m, 2), None

    def bwd(_, dy):
        return (_split_dot(dy, m, 2),)

    f.defvjp(fwd, bwd)
    return f


def _head_indicator(n):
    r = lax.broadcasted_iota(jnp.int32, (n, n), 0) // HEAD
    c = lax.broadcasted_iota(jnp.int32, (n, n), 1) // HEAD
    return (r == c).astype(BF16)


def _rms(x, g):
    return x * lax.rsqrt(jnp.mean(x * x, axis=-1, keepdims=True) + RMS_EPS) * g


def _softplus(x):
    return jnp.maximum(x, 0.0) + jnp.log(1.0 + jnp.exp(-jnp.abs(x)))


def _sigmoid(x):
    return 1.0 / (1.0 + jnp.exp(-x))


def _gelu(x):
    return 0.5 * x * (1.0 + jnp.tanh(0.7978845608028654 * (x + 0.044715 * (x * x * x))))


def _tok_call(name, fn, L, TB, tok_in, const_in, tok_out, acc_out=(), deps=()):
    nb = L // TB
    g8 = TB // 8
    in_specs, args = [], []
    for spec in tok_in:
        if len(spec) == 1:
            arr = spec[0]
            in_specs.append(pl.BlockSpec((arr.shape[0], TB, HEAD), lambda i: (0, i, 0)))
            args.append(arr)
            continue
        arr, width, cb = spec[:3]
        mode = spec[3] if len(spec) > 3 else None
        if mode is None:
            in_specs.append(pl.BlockSpec((TB, width), lambda i, cb=cb: (i, cb)))
        elif mode == "prev":
            in_specs.append(pl.BlockSpec((8, width), lambda i, cb=cb: (jnp.maximum(i * g8 - 1, 0), cb)))
        else:
            in_specs.append(pl.BlockSpec((8, width), lambda i, cb=cb: (jnp.minimum((i + 1) * g8, L // 8 - 1), cb)))
        args.append(arr)
    for c in const_in:
        in_specs.append(pl.BlockSpec(c.shape, lambda i, nd=c.ndim: (0,) * nd, pipeline_mode=pl.Buffered(1)))
        args.append(c)
    for d in deps:
        in_specs.append(pl.BlockSpec(d.shape, lambda i, nd=d.ndim: (0,) * nd))
        args.append(d)
    out_shape, out_specs = [], []
    for width, dt in tok_out:
        if width == "heads":
            out_shape.append(jax.ShapeDtypeStruct((N_HEAD, L, HEAD), dt))
            out_specs.append(pl.BlockSpec((N_HEAD, TB, HEAD), lambda i: (0, i, 0)))
            continue
        out_shape.append(jax.ShapeDtypeStruct((L, width), dt))
        out_specs.append(pl.BlockSpec((TB, width), lambda i: (i, 0)))
    for shp in acc_out:
        out_shape.append(jax.ShapeDtypeStruct(shp, F32))
        out_specs.append(pl.BlockSpec(shp, lambda i, nd=len(shp): (0,) * nd))
    n_tok, n_const, n_to = len(tok_in), len(const_in), len(tok_out)

    def body(*refs):
        i = pl.program_id(0)
        tv = [r[...] if len(r.shape) == 2 else jnp.concatenate([r[h] for h in range(r.shape[0])], axis=1)
              for r in refs[:n_tok]]
        cv = [r[...] for r in refs[n_tok:n_tok + n_const]]
        orefs = refs[n_tok + n_const + len(deps):]
        outs = fn(i, tv, cv)
        for r, v in zip(orefs[:n_to], outs[:n_to]):
            if len(r.shape) == 3:
                for h in range(r.shape[0]):
                    r[h] = v[:, h * HEAD:(h + 1) * HEAD].astype(r.dtype)
            else:
                r[...] = v.astype(r.dtype)
        for r, v in zip(orefs[n_to:], outs[n_to:]):
            @pl.when(i == 0)
            def _(r=r):
                r[...] = jnp.zeros(r.shape, r.dtype)

            r[...] += v

    res = _pcall(body, name=name, grid=(nb,), in_specs=in_specs, out_specs=out_specs, out_shape=out_shape,
                 compiler_params=_cparams(1))(*args)
    return res


def _pick_block(n, cap):
    best = None
    for b in range(128, min(n, cap) + 1, 128):
        if n % b == 0:
            best = b
    return best if best is not None else n


def _mm_tn(name, a, b):
    T, M = a.shape
    N = b.shape[1]
    bm, bn, bt = _pick_block(M, 1536), _pick_block(N, 1536), _pick_block(T, 512)

    def body(a_ref, b_ref, o_ref):
        t = pl.program_id(2)

        @pl.when(t == 0)
        def _():
            o_ref[...] = jnp.zeros(o_ref.shape, F32)

        o_ref[...] += _dot_tn(a_ref[...].astype(BF16), b_ref[...].astype(BF16))

    return _pcall(body, name=name, grid=(M // bm, N // bn, T // bt),
                  in_specs=[pl.BlockSpec((bt, bm), lambda m, n, t: (t, m)), pl.BlockSpec((bt, bn), lambda m, n, t: (t, n))],
                  out_specs=pl.BlockSpec((bm, bn), lambda m, n, t: (m, n)),
                  out_shape=jax.ShapeDtypeStruct((M, N), F32), compiler_params=_cparams(3))(a, b)


def _s5_param_fn(lam_re, lam_im, log_step, bt_re, bt_im):
    dt = jnp.exp(log_step)
    e = jnp.exp(lam_re * dt)
    lb_re = e * jnp.cos(lam_im * dt)
    lb_im = e * jnp.sin(lam_im * dt)
    den = lam_re * lam_re + lam_im * lam_im
    nr, ni = lb_re - 1.0, lb_im
    co_re = (nr * lam_re + ni * lam_im) / den
    co_im = (ni * lam_re - nr * lam_im) / den
    cr, ci = co_re[:, None, :], co_im[:, None, :]
    return lb_re, lb_im, cr * bt_re - ci * bt_im, cr * bt_im + ci * bt_re


def _s5_param_fwd(lam_re, lam_im, log_step, bt_re, bt_im):
    def body(a, b, c, d, e, o1, o2, o3, o4):
        r = _s5_param_fn(a[...], b[...], c[...], d[...], e[...])
        o1[...], o2[...], o3[...], o4[...] = r

    sh = jax.ShapeDtypeStruct
    return _pcall(body, name="s5_param_fwd",
                  out_shape=[sh(lam_re.shape, F32), sh(lam_re.shape, F32), sh(bt_re.shape, F32), sh(bt_re.shape, F32)])(
        lam_re, lam_im, log_step, bt_re, bt_im)


def _s5_param_bwd(lam_re, lam_im, log_step, bt_re, bt_im, d_lb_re, d_lb_im, d_bb_re, d_bb_im):
    def body(a, b, c, d, e, g1, g2, g3, g4, o1, o2, o3, o4, o5):
        _, vjp = jax.vjp(_s5_param_fn, a[...], b[...], c[...], d[...], e[...])
        r = vjp((g1[...], g2[...], g3[...], g4[...]))
        o1[...], o2[...], o3[...], o4[...], o5[...] = r

    sh = jax.ShapeDtypeStruct
    return _pcall(body, name="s5_param_bwd",
                  out_shape=[sh(lam_re.shape, F32), sh(lam_re.shape, F32), sh(log_step.shape, F32),
                             sh(bt_re.shape, F32), sh(bt_re.shape, F32)])(
        lam_re, lam_im, log_step, bt_re, bt_im, d_lb_re, d_lb_im, d_bb_re, d_bb_im)


def _cmul(ar, ai, br, bi):
    return ar * br - ai * bi, ar * bi + ai * br


def _scan_consts(lr, li, reverse):
    n = lr.shape[1]
    sub = lax.broadcasted_iota(jnp.int32, (8, n), 0)
    pows = [(lr, li)]
    for _ in range(7):
        pows.append(_cmul(pows[-1][0], pows[-1][1], lr, li))
    steps = []
    for s in (1, 2, 4):
        m = (sub < 8 - s) if reverse else (sub >= s)
        pr, pi = pows[s - 1]
        steps.append((s, jnp.where(m, jnp.broadcast_to(pr, (8, n)), 0.0), jnp.where(m, jnp.broadcast_to(pi, (8, n)), 0.0)))
    wr = jnp.zeros((8, n), F32)
    wi = jnp.zeros((8, n), F32)
    for r in range(8):
        e = (8 - r) if reverse else (r + 1)
        wr = jnp.where(sub == r, jnp.broadcast_to(pows[e - 1][0], (8, n)), wr)
        wi = jnp.where(sub == r, jnp.broadcast_to(pows[e - 1][1], (8, n)), wi)
    return steps, wr, wi


S5_Q = 4
S5_QL = S5_WIDTH // S5_Q
S5_QS = S5_LANES // S5_Q
S5_NT = S5_LANES // 128
S5_QT = S5_QS // 128


def _s5_power_table(lb_ref, pw_re, pw_im, seg):
    for j in range(S5_NT):
        lr = jnp.broadcast_to(lb_ref[0:1, j * 128:(j + 1) * 128], (8, 128))
        li = jnp.broadcast_to(lb_ref[1:2, j * 128:(j + 1) * 128], (8, 128))

        def step(i, c, lr=lr, li=li, j=j):
            pw_re[j, i] = c[0]
            pw_im[j, i] = c[1]
            return _cmul(c[0], c[1], lr, li)

        lax.fori_loop(0, seg, step, (lr, li))


def _seg_scan(sre, sim, carry, lb_ref, pw_re, pw_im, rows, reverse):
    seg = rows // 8
    sgn = -1.0 if reverse else 1.0
    sub = lax.broadcasted_iota(jnp.int32, (8, 128), 0)
    rows_at = lambda i: pl.ds(pl.multiple_of(i * 8, 8), 8)
    entering = {}
    half_tiles = S5_NT // 2
    for half in range(2):
        tiles = list(range(half * half_tiles, (half + 1) * half_tiles))
        lam8 = [(jnp.broadcast_to(lb_ref[0:1, j * 128:(j + 1) * 128], (8, 128)),
                 sgn * jnp.broadcast_to(lb_ref[1:2, j * 128:(j + 1) * 128], (8, 128))) for j in tiles]

        def p1(ii, c):
            i = (seg - 1 - ii) if reverse else ii
            out = []
            for n, j in enumerate(tiles):
                lr, li = lam8[n]
                cr, ci = c[2 * n], c[2 * n + 1]
                nr = lr * cr - li * ci + sre[j, rows_at(i), :]
                ni = lr * ci + li * cr + sim[j, rows_at(i), :]
                sre[j, rows_at(i), :] = nr
                sim[j, rows_at(i), :] = ni
                out += [nr, ni]
            return tuple(out)

        ends = lax.fori_loop(0, seg, p1, tuple(jnp.zeros((8, 128), F32) for _ in range(2 * len(tiles))))
        cs = []
        for n, j in enumerate(tiles):
            ls = slice(j * 128, (j + 1) * 128)
            steps, wr, wi = _scan_consts(pw_re[j, seg - 1][0:1, :], sgn * pw_im[j, seg - 1][0:1, :], reverse)
            tr, ti = ends[2 * n], ends[2 * n + 1]
            for sft, pr, pi in steps:
                sh = (8 - sft) if reverse else sft
                yr, yi = pltpu.roll(tr, sh, 0), pltpu.roll(ti, sh, 0)
                tr, ti = tr + pr * yr - pi * yi, ti + pr * yi + pi * yr
            cin_r, cin_i = carry[0:1, ls], carry[1:2, ls]
            tr, ti = tr + wr * cin_r - wi * cin_i, ti + wr * cin_i + wi * cin_r
            edge_out, edge_in, sh = (0, 7, 7) if reverse else (7, 0, 1)
            carry[0:1, ls] = tr[edge_out:edge_out + 1, :]
            carry[1:2, ls] = ti[edge_out:edge_out + 1, :]
            cr = jnp.where(sub == edge_in, jnp.broadcast_to(cin_r, (8, 128)), pltpu.roll(tr, sh, 0))
            ci = jnp.where(sub == edge_in, jnp.broadcast_to(cin_i, (8, 128)), pltpu.roll(ti, sh, 0))
            cs += [cr, ci]
            entering[j] = (cr, ci)

        def p2(i, _):
            k = (seg - 1 - i) if reverse else i
            for n, j in enumerate(tiles):
                pr, pi = pw_re[j, k], pw_im[j, k]
                cr, ci = cs[2 * n], cs[2 * n + 1]
                if reverse:
                    sre[j, rows_at(i), :] = sre[j, rows_at(i), :] + pr * cr + pi * ci
                    sim[j, rows_at(i), :] = sim[j, rows_at(i), :] + pr * ci - pi * cr
                else:
                    sre[j, rows_at(i), :] = sre[j, rows_at(i), :] + pr * cr - pi * ci
                    sim[j, rows_at(i), :] = sim[j, rows_at(i), :] + pr * ci + pi * cr
            return 0

        lax.fori_loop(0, seg, p2, 0, unroll=2)
    return entering


class _SegIO:
    def __init__(self, hbm, buf, sems, rows, width, col0=0):
        self.hbm, self.buf, self.sems, self.rows, self.seg, self.width, self.col0 = hbm, buf, sems, rows, rows // 8, width, col0

    def _copies(self, blk, slot, to_vmem):
        out = []
        for r in range(8):
            h = self.hbm.at[pl.ds(blk * self.rows + r * self.seg, self.seg), pl.ds(self.col0, self.width)]
            v = self.buf.at[slot, :, r, :]
            out.append(pltpu.make_async_copy(h, v, self.sems.at[slot, r]) if to_vmem
                       else pltpu.make_async_copy(v, h, self.sems.at[slot, r]))
        return out

    def start(self, blk, slot, to_vmem):
        for cp in self._copies(blk, slot, to_vmem):
            cp.start()

    def wait(self, blk, slot, to_vmem):
        for cp in self._copies(blk, slot, to_vmem):
            cp.wait()

    def value(self, slot):
        return self.buf[slot].reshape(self.rows, self.width)

    def store(self, slot, val):
        self.buf[slot] = val.reshape(self.seg, 8, self.width)


def _seg_pipeline(i, nb, blk_of, ins, outs, compute):
    slot = i % 2

    @pl.when(i == 0)
    def _():
        for io in ins:
            io.start(blk_of(0), 0, True)

    @pl.when(i + 1 < nb)
    def _():
        for io in ins:
            io.start(blk_of(i + 1), 1 - slot, True)

    for io in ins:
        io.wait(blk_of(i), slot, True)

    @pl.when(i >= 2)
    def _():
        for io in outs:
            io.wait(blk_of(i - 2), slot, False)

    compute(slot)
    for io in outs:
        io.start(blk_of(i), slot, False)

    @pl.when(i == nb - 1)
    def _():
        for io in outs:
            if nb >= 2:
                io.wait(blk_of(i - 1), 1 - slot, False)
            io.wait(blk_of(i), slot, False)


def _s5_scan_fwd(proj, bq_re, bq_im, cq_re, cq_im, lbar, dskip, L, TB):
    nb = L // TB
    seg = TB // 8

    def body(u_hbm, bre, bim, cre, cim, lb_ref, d_ref, y_hbm, ck_ref, sre, sim, carry, pw_re, pw_im,
             ubuf, ybuf, sem_u, sem_y):
        i = pl.program_id(0)
        u_io = _SegIO(u_hbm, ubuf, sem_u, TB, S5_WIDTH)
        y_io = _SegIO(y_hbm, ybuf, sem_y, TB, S5_WIDTH)

        @pl.when(i == 0)
        def _():
            carry[...] = jnp.zeros(carry.shape, F32)
            _s5_power_table(lb_ref, pw_re, pw_im, seg)

        ck_ref[0] = carry[...]

        def compute(slot):
            u = u_io.value(slot)
            ub = u.astype(BF16)
            for q in range(S5_Q):
                uq = ub[:, q * S5_QL:(q + 1) * S5_QL]
                vr, vi = _dot(uq, bre[q]), _dot(uq, bim[q])
                for jj in range(S5_QT):
                    sre[q * S5_QT + jj] = vr[:, jj * 128:(jj + 1) * 128]
                    sim[q * S5_QT + jj] = vi[:, jj * 128:(jj + 1) * 128]
            _seg_scan(sre, sim, carry, lb_ref, pw_re, pw_im, TB, False)
            ys = []
            for q in range(S5_Q):
                sl = slice(q * S5_QL, (q + 1) * S5_QL)
                sr = jnp.concatenate([sre[q * S5_QT + jj] for jj in range(S5_QT)], axis=1).astype(BF16)
                si = jnp.concatenate([sim[q * S5_QT + jj] for jj in range(S5_QT)], axis=1).astype(BF16)
                ys.append(_dot(sr, cre[q]) - _dot(si, cim[q]) + u[:, sl] * d_ref[:, sl])
            y_io.store(slot, jnp.concatenate(ys, axis=1))

        _seg_pipeline(i, nb, lambda st: st, [u_io], [y_io], compute)

    full = lambda a: pl.BlockSpec(a.shape, lambda i, nd=a.ndim: (0,) * nd)
    st = pltpu.VMEM((S5_NT, TB, 128), F32)
    pw = pltpu.VMEM((S5_NT, seg, 8, 128), F32)
    io = pltpu.VMEM((2, seg, 8, S5_WIDTH), F32)
    return _pcall(
        body, name="s5_scan_fwd", grid=(nb,),
        in_specs=[_ANY, full(bq_re), full(bq_im), full(cq_re), full(cq_im), full(lbar), full(dskip)],
        out_specs=[_ANY, pl.BlockSpec((1, 8, S5_LANES), lambda i: (i, 0, 0))],
        out_shape=[jax.ShapeDtypeStruct((L, S5_WIDTH), F32), jax.ShapeDtypeStruct((nb, 8, S5_LANES), F32)],
        scratch_shapes=[st, st, pltpu.VMEM((8, S5_LANES), F32), pw, pw, io, io,
                        pltpu.SemaphoreType.DMA((2, 8)), pltpu.SemaphoreType.DMA((2, 8))],
        compiler_params=_cparams(1))(proj, bq_re, bq_im, cq_re, cq_im, lbar, dskip)


def _s5_scan_bwd(proj, dy, ck, bq_re, bq_im, cq_re, cq_im, lbar, dskip, L, TB):
    nb = L // TB
    seg = TB // 8

    def body(u_hbm, dy_hbm, ck_ref, bre, bim, cre, cim, lb_ref, d_ref,
             du_hbm, dbre, dbim, dcre, dcim, dlb_ref, dd_ref, sre, sim, gre, gim, carry, gcarry, pw_re, pw_im,
             ubuf, dybuf, dubuf, sem_u, sem_dy, sem_du):
        i = pl.program_id(0)
        u_io = _SegIO(u_hbm, ubuf, sem_u, TB, S5_WIDTH)
        dy_io = _SegIO(dy_hbm, dybuf, sem_dy, TB, S5_WIDTH)
        du_io = _SegIO(du_hbm, dubuf, sem_du, TB, S5_WIDTH)

        @pl.when(i == 0)
        def _():
            gcarry[...] = jnp.zeros(gcarry.shape, F32)
            dbre[...] = jnp.zeros(dbre.shape, F32)
            dbim[...] = jnp.zeros(dbim.shape, F32)
            dcre[...] = jnp.zeros(dcre.shape, F32)
            dcim[...] = jnp.zeros(dcim.shape, F32)
            dlb_ref[...] = jnp.zeros(dlb_ref.shape, F32)
            dd_ref[...] = jnp.zeros(dd_ref.shape, F32)
            _s5_power_table(lb_ref, pw_re, pw_im, seg)

        def compute(slot):
            u = u_io.value(slot)
            dy_v = dy_io.value(slot)
            ub = u.astype(BF16)
            dyb = dy_v.astype(BF16)
            carry[...] = ck_ref[0]
            for q in range(S5_Q):
                uq = ub[:, q * S5_QL:(q + 1) * S5_QL]
                dq = dyb[:, q * S5_QL:(q + 1) * S5_QL]
                vr, vi = _dot(uq, bre[q]), _dot(uq, bim[q])
                hr, hi = _dot_nt(dq, cre[q]), -_dot_nt(dq, cim[q])
                for jj in range(S5_QT):
                    ls = slice(jj * 128, (jj + 1) * 128)
                    sre[q * S5_QT + jj] = vr[:, ls]
                    sim[q * S5_QT + jj] = vi[:, ls]
                    gre[q * S5_QT + jj] = hr[:, ls]
                    gim[q * S5_QT + jj] = hi[:, ls]
            entering = _seg_scan(sre, sim, carry, lb_ref, pw_re, pw_im, TB, False)
            _seg_scan(gre, gim, gcarry, lb_ref, pw_re, pw_im, TB, True)

            rows_at = lambda k: pl.ds(pl.multiple_of(k * 8, 8), 8)
            for half in range(2):
                tiles = list(range(half * (S5_NT // 2), (half + 1) * (S5_NT // 2)))
                acc0 = []
                for j in tiles:
                    er, ei = entering[j]
                    gr0, gi0 = gre[j, rows_at(0), :], gim[j, rows_at(0), :]
                    acc0 += [gr0 * er + gi0 * ei, gi0 * er - gr0 * ei]

                def acc_step(k, acc, tiles=tiles):
                    out = []
                    for n, j in enumerate(tiles):
                        gr, gi_ = gre[j, rows_at(k), :], gim[j, rows_at(k), :]
                        spr, spi = sre[j, rows_at(k - 1), :], sim[j, rows_at(k - 1), :]
                        out += [acc[2 * n] + gr * spr + gi_ * spi, acc[2 * n + 1] - gr * spi + gi_ * spr]
                    return tuple(out)

                acc = lax.fori_loop(1, seg, acc_step, tuple(acc0))
                for n, j in enumerate(tiles):
                    ls = slice(j * 128, (j + 1) * 128)
                    dlb_ref[0:1, ls] += jnp.sum(acc[2 * n], axis=0, keepdims=True)
                    dlb_ref[1:2, ls] += jnp.sum(acc[2 * n + 1], axis=0, keepdims=True)

            dd_ref[...] += jnp.sum(dy_v * u, axis=0, keepdims=True)
            dus = []
            for q in range(S5_Q):
                sl = slice(q * S5_QL, (q + 1) * S5_QL)
                cat = lambda ref: jnp.concatenate([ref[q * S5_QT + jj] for jj in range(S5_QT)], axis=1).astype(BF16)
                grq, giq = cat(gre), cat(gim)
                dus.append(_dot_nt(grq, bre[q]) + _dot_nt(giq, bim[q]) + dy_v[:, sl] * d_ref[:, sl])
                dbre[q] += _dot_tn(ub[:, sl], grq)
                dbim[q] += _dot_tn(ub[:, sl], giq)
                dcre[q] += _dot_tn(cat(sre), dyb[:, sl])
                dcim[q] -= _dot_tn(cat(sim), dyb[:, sl])
            du_io.store(slot, jnp.concatenate(dus, axis=1))

        _seg_pipeline(i, nb, lambda st: nb - 1 - st, [u_io, dy_io], [du_io], compute)

    full = lambda a: pl.BlockSpec(a.shape, lambda i, nd=a.ndim: (0,) * nd)
    sh = jax.ShapeDtypeStruct
    outs = [sh((L, S5_WIDTH), F32), sh(bq_re.shape, F32), sh(bq_im.shape, F32), sh(cq_re.shape, F32), sh(cq_im.shape, F32),
            sh((8, S5_LANES), F32), sh((1, S5_WIDTH), F32)]
    fo = lambda s: pl.BlockSpec(s.shape, lambda i, nd=len(s.shape): (0,) * nd)
    st = pltpu.VMEM((S5_NT, TB, 128), F32)
    pw = pltpu.VMEM((S5_NT, seg, 8, 128), F32)
    io = pltpu.VMEM((2, seg, 8, S5_WIDTH), F32)
    sem = pltpu.SemaphoreType.DMA((2, 8))
    return _pcall(
        body, name="s5_scan_bwd", grid=(nb,),
        in_specs=[_ANY, _ANY, pl.BlockSpec((1, 8, S5_LANES), lambda i: (nb - 1 - i, 0, 0)),
                  full(bq_re), full(bq_im), full(cq_re), full(cq_im), full(lbar), full(dskip)],
        out_specs=[_ANY] + [fo(s) for s in outs[1:]],
        out_shape=outs,
        scratch_shapes=[st] * 4 + [pltpu.VMEM((8, S5_LANES), F32)] * 2 + [pw, pw, io, io, io, sem, sem, sem],
        compiler_params=_cparams(1))(proj, dy, ck, bq_re, bq_im, cq_re, cq_im, lbar, dskip)


N_HEAD = RW_WIDTH // HEAD
_NN = (((2,), (1,)), ((0,), (0,)))
_NT = (((2,), (2,)), ((0,), (0,)))
_TN = (((1,), (1,)), ((0,), (0,)))


def _hi_lo(x):
    h = x.astype(BF16)
    return h, (x - h.astype(F32)).astype(BF16)


def _mm_acc(a, b, dims, passes=3):
    dg = lambda p, q: lax.dot_general(p, q, dims, preferred_element_type=F32)
    if passes == 1:
        return dg(a.astype(BF16), b.astype(BF16))
    ah, al = _hi_lo(a)
    bh, bl = _hi_lo(b)
    return dg(ah, bh) + dg(ah, bl) + dg(al, bh)


def _cumsum_rows(x, transpose):
    h, n, _ = x.shape
    ti = lax.broadcasted_iota(jnp.int32, (h, n, n), 1)
    tj = lax.broadcasted_iota(jnp.int32, (h, n, n), 2)
    m = ((tj >= ti) if transpose else (tj <= ti)).astype(BF16)
    acc, rem = None, x
    for s in range(3):
        part = rem.astype(BF16)
        t = lax.dot_general(m, part, _NN, preferred_element_type=F32)
        acc = t if acc is None else acc + t
        if s < 2:
            rem = rem - part.astype(F32)
    return acc


def _slices(x, axis, sizes):
    out, off = [], 0
    for n in sizes:
        out.append(lax.slice_in_dim(x, off, off + n, axis=axis))
        off += n
    return tuple(out)


def _cat_op(axis, sizes, diff):
    plain = lambda *xs: jnp.concatenate(xs, axis=axis)
    if not diff:
        return plain
    f = jax.custom_vjp(plain)
    f.defvjp(lambda *xs: (plain(*xs), None), lambda _, d: _slices(d, axis, sizes))
    return f


def _split_op(axis, sizes, diff):
    plain = lambda x: _slices(x, axis, sizes)
    if not diff:
        return plain
    f = jax.custom_vjp(plain)
    f.defvjp(lambda x: (plain(x), None), lambda _, d: (jnp.concatenate(d, axis=axis),))
    return f


def _mm_ops(diff, passes):
    mm = lambda a, b, dims: _mm_acc(a, b, dims, passes)
    if not diff:
        return (lambda a, b: mm(a, b, _NN), lambda a, b: mm(a, b, _NT), lambda a, b: mm(a, b, _TN))

    @jax.custom_vjp
    def nn(a, b):
        return mm(a, b, _NN)

    nn.defvjp(lambda a, b: (mm(a, b, _NN), (a, b)), lambda r, d: (mm(d, r[1], _NT), mm(r[0], d, _TN)))

    @jax.custom_vjp
    def nt(a, b):
        return mm(a, b, _NT)

    nt.defvjp(lambda a, b: (mm(a, b, _NT), (a, b)), lambda r, d: (mm(d, r[1], _NN), mm(d, r[0], _TN)))

    @jax.custom_vjp
    def tn(a, b):
        return mm(a, b, _TN)

    tn.defvjp(lambda a, b: (mm(a, b, _TN), (a, b)), lambda r, d: (mm(r[1], d, _NT), mm(r[0], d, _NN)))
    return nn, nt, tn


def _cums_op(diff):
    if not diff:
        return lambda x: _cumsum_rows(x, False)

    @jax.custom_vjp
    def cums(x):
        return _cumsum_rows(x, False)

    cums.defvjp(lambda x: (_cumsum_rows(x, False), None), lambda _, d: (_cumsum_rows(d, True),))
    return cums


WKV_PASSES = (1, 1, 1, 1, 1)


WKV_SUB = 4
WKV_BLOCK = CHUNK * WKV_SUB


def _wkv_block(s0, r, w, k, v, a, b, diff):
    p_pair, p_val, p_solve, p_out, p_state = WKV_PASSES
    cums = _cums_op(diff)
    _, nt_pair, _ = _mm_ops(diff, p_pair)
    nn_val, _, _ = _mm_ops(diff, p_val)
    nn_solve, _, _ = _mm_ops(diff, p_solve)
    nn_out, nt_out, _ = _mm_ops(diff, p_out)
    nn_state, _, tn_state = _mm_ops(diff, p_state)
    h, d, n, sub = s0.shape[0], s0.shape[2], CHUNK, WKV_SUB
    hb = h * sub
    to_chunks = lambda t: _cat_op(0, (h,) * sub, diff)(*_split_op(1, (n,) * sub, diff)(t))
    r, w, k, v, a, b = (to_chunks(t) for t in (r, w, k, v, a, b))
    cat_rows2 = _cat_op(1, (n, n), diff)
    cat_lanes2 = _cat_op(2, (n, n), diff)
    split_rows2 = _split_op(1, (n, n), diff)
    split_lanes2 = _split_op(2, (n, n), diff)
    ti = lax.broadcasted_iota(jnp.int32, (hb, n, n), 1)
    tj = lax.broadcasted_iota(jnp.int32, (hb, n, n), 2)
    incl, strict = tj <= ti, tj < ti
    logw = jnp.log(w)
    cum = cums(logw)
    g_in, g_ex, g_inv = jnp.exp(cum), jnp.exp(cum - logw), jnp.exp(-cum)
    ae, re, bi, ki = a * g_ex, r * g_in, b * g_inv, k * g_inv
    top, bot = split_rows2(nt_pair(cat_rows2(ae, re), cat_rows2(bi, ki)))
    tab, tak = split_lanes2(top)
    qb, qk = split_lanes2(bot)
    tab, tak = jnp.where(strict, tab, 0.0), jnp.where(strict, tak, 0.0)
    qb, qk = jnp.where(incl, qb, 0.0), jnp.where(incl, qk, 0.0)
    tak_v, qk_v = split_rows2(nn_val(cat_rows2(tak, qk), v))
    x = cat_lanes2(ae, tak_v)
    npow = tab
    steps = max(1, (n - 1).bit_length())
    for i in range(steps):
        x = x + nn_solve(npow, x)
        if i + 1 < steps:
            npow = nn_solve(npow, npow)
    ae_m, uc = split_lanes2(x)
    qx = nn_out(qb, x)
    q_ae, q_uc = split_lanes2(qx)
    re_m = re + q_ae
    yc = q_uc + qk_v
    g_end = jnp.exp(jnp.sum(logw, axis=1, keepdims=True))
    bg, kg = bi * g_end, ki * g_end
    tm = tn_state(ae_m, bg)
    sc = tn_state(cat_rows2(uc, v), cat_rows2(bg, kg))
    per_chunk = _split_op(0, (h,) * sub, diff)
    re_m, yc, g_end, tm, sc = (per_chunk(t) for t in (re_m, yc, g_end, tm, sc))
    ys, s = [], s0
    for i in range(sub):
        ys.append(nt_out(re_m[i], s) + yc[i])
        s = s * g_end[i] + nn_state(s, tm[i]) + sc[i]
    return _cat_op(1, (n,) * sub, diff)(*ys), s


def _wkv_fwd(r, w, k, v, a, b, L):
    nc = L // WKV_BLOCK

    def body(r_ref, w_ref, k_ref, v_ref, a_ref, b_ref, y_ref, ck_ref, s_ref):
        c = pl.program_id(0)

        @pl.when(c == 0)
        def _():
            s_ref[...] = jnp.zeros(s_ref.shape, F32)

        s0 = s_ref[...]
        ck_ref[0] = s0
        y, s1 = _wkv_block(s0, r_ref[...], w_ref[...], k_ref[...], v_ref[...], a_ref[...], b_ref[...], False)
        y_ref[...] = y
        s_ref[...] = s1

    blk = pl.BlockSpec((N_HEAD, WKV_BLOCK, HEAD), lambda c: (0, c, 0))
    return _pcall(
        body, name="wkv_fwd", grid=(nc,), in_specs=[blk] * 6,
        out_specs=[blk, pl.BlockSpec((1, N_HEAD, HEAD, HEAD), lambda c: (c, 0, 0, 0))],
        out_shape=[jax.ShapeDtypeStruct((N_HEAD, L, HEAD), F32), jax.ShapeDtypeStruct((nc, N_HEAD, HEAD, HEAD), F32)],
        scratch_shapes=[pltpu.VMEM((N_HEAD, HEAD, HEAD), F32)],
        compiler_params=_cparams(1))(r, w, k, v, a, b)


def _wkv_bwd(r, w, k, v, a, b, dy, ck, L, deps=()):
    nc = L // WKV_BLOCK

    def body(r_ref, w_ref, k_ref, v_ref, a_ref, b_ref, dy_ref, ck_ref, *rest):
        dr_ref, dw_ref, dk_ref, dv_ref, da_ref, db_ref, ds_ref = rest[len(deps):]
        c = pl.program_id(0)

        @pl.when(c == 0)
        def _():
            ds_ref[...] = jnp.zeros(ds_ref.shape, F32)

        _, vjp = jax.vjp(lambda *t: _wkv_block(*t, True), ck_ref[0], r_ref[...], w_ref[...], k_ref[...], v_ref[...],
                         a_ref[...], b_ref[...])
        g = vjp((dy_ref[...], ds_ref[...]))
        ds_ref[...] = g[0]
        for o_ref, val in zip((dr_ref, dw_ref, dk_ref, dv_ref, da_ref, db_ref), g[1:]):
            o_ref[...] = val

    blk = pl.BlockSpec((N_HEAD, WKV_BLOCK, HEAD), lambda c: (0, nc - 1 - c, 0))
    sh = jax.ShapeDtypeStruct((N_HEAD, L, HEAD), F32)
    return _pcall(
        body, name="wkv_bwd", grid=(nc,),
        in_specs=[blk] * 7 + [pl.BlockSpec((1, N_HEAD, HEAD, HEAD), lambda c: (nc - 1 - c, 0, 0, 0))]
        + [pl.BlockSpec(d.shape, lambda c, nd=d.ndim: (0,) * nd) for d in deps],
        out_specs=[blk] * 6, out_shape=[sh] * 6,
        scratch_shapes=[pltpu.VMEM((N_HEAD, HEAD, HEAD), F32)],
        compiler_params=_cparams(1))(r, w, k, v, a, b, dy, ck, *deps)


TB = 256


def _bf(x):
    return x.astype(BF16)


def _inproj_fwd(x, norm_mix, w_in, L, deps=()):
    def fn(i, tv, cv):
        xn = _rms(tv[0], cv[0])
        return _dot(_bf(xn), cv[1]), xn

    return _tok_call("inproj_fwd", fn, L, TB, [(x, D_MODEL, 0)], [norm_mix, w_in], [(IN_COLS, F32), (D_MODEL, BF16)],
                     deps=deps)


def _s5_post_fn(glu_w, wtop, diff=True):
    mg = _mmc(glu_w, diff)
    mt = _mmc(wtop, diff) if wtop is not None else None

    def f(y, glu_b, e):
        z = _gelu(y)
        out = z * _sigmoid(mg(z) + glu_b + e)
        res = mt(out) if mt is not None else out
        return res, (z, out)

    return f


def _s5_post_fwd(y, glu_w, glu_b, L):
    def fn(i, tv, cv):
        out, _ = _s5_post_fn(cv[0], None, False)(tv[0], cv[1], 0.0)
        return (out,)

    return _tok_call("s5_post_fwd", fn, L, TB, [(y, S5_WIDTH, 0)], [glu_w, glu_b], [(S5_WIDTH, F32)])[0]


def _s5_post_bwd(y, dh1, glu_w, glu_b, wtop, L, deps=()):
    def fn(i, tv, cv):
        e0 = jnp.zeros((TB, S5_WIDTH), F32)
        _, vjp, (z, out) = jax.vjp(_s5_post_fn(cv[0], cv[2]), tv[0], cv[1], e0, has_aux=True)
        dy, db, de = vjp(tv[1])
        return dy, db, _dot_tn(_bf(z), _bf(de)), _dot_tn(_bf(out), _bf(tv[1]))

    return _tok_call("s5_post_bwd", fn, L, TB, [(y, S5_WIDTH, 0), (dh1, D_MODEL, 0)], [glu_w, glu_b, wtop],
                     [(S5_WIDTH, F32)], [(1, S5_WIDTH), (S5_WIDTH, S5_WIDTH), (S5_WIDTH, D_MODEL)], deps=deps)


RW_COLBLK = ((RW_WIDTH, 1), (RW_WIDTH, 2), (RW_WIDTH, 3), (128, 16), (128, 17))
RW_MU = ((0, 512), (512, 1024), (1024, 1536), (1536, 1664), (1664, 1792))


def _rw_pre_fn(w2pad, a2pad, g2, diff=True):
    m_w, m_a, m_g = _mmc(w2pad, diff), _mmc(a2pad, diff), _mmc(g2, diff)
    seg = _segsum(_head_indicator(RW_WIDTH), diff)

    def f(zr, zk, zv, zwa, zg, w0, a0, k_k, k_a, e_w, e_a):
        wl_t = jnp.tanh(zwa)
        wlin = w0 + m_w(wl_t) + e_w
        w = -_softplus(-wlin) - 0.5
        decay = jnp.exp(-jnp.exp(w))
        a = _sigmoid(a0 + m_a(zwa) + e_a)
        sg = _sigmoid(zg)
        g = m_g(sg)
        kk = zk * k_k
        kkn = kk / jnp.maximum(jnp.sqrt(seg(kk * kk)), L2_EPS)
        kf = zk * (1.0 + (a - 1.0) * k_a)
        return (zr, decay, kf, zv, -kkn, kkn * a, g), (wl_t, sg)

    return f


def _rw_shifted(i, tv, mu):
    sub = lax.broadcasted_iota(jnp.int32, (TB, 1), 0)
    zs, dif = [], []
    for n in range(5):
        z = tv[n]
        last = jnp.where(i == 0, 0.0, tv[5 + n][7:8, :])
        prev = jnp.where(sub == 0, last, pltpu.roll(z, 1, 0))
        m = mu[:, RW_MU[n][0]:RW_MU[n][1]]
        zs.append(z + (prev - z) * m)
        dif.append(prev - z)
    return zs, dif


def _rw_tok_in(proj):
    return [(proj, wd, cb) for wd, cb in RW_COLBLK] + [(proj, wd, cb, "prev") for wd, cb in RW_COLBLK]


def _rw_pre_fwd(proj, mu, w0, a0, k_k, k_a, w2pad, a2pad, g2, L):
    def fn(i, tv, cv):
        zs, _ = _rw_shifted(i, tv, cv[0])
        outs, _ = _rw_pre_fn(cv[5], cv[6], cv[7], False)(*zs, cv[1], cv[2], cv[3], cv[4], 0.0, 0.0)
        return outs

    return _tok_call("rw_pre_fwd", fn, L, TB, _rw_tok_in(proj), [mu, w0, a0, k_k, k_a, w2pad, a2pad, g2],
                     [("heads", F32)] * 6 + [(RW_WIDTH, F32)])


def _rw_pre_bwd(proj, cots, mu, w0, a0, k_k, k_a, w2pad, a2pad, g2, L):
    def fn(i, tv, cv):
        zs, dif = _rw_shifted(i, tv[:10], cv[0])
        dr1, dr2, dw, dk1, dk2, dv1, dv2, da, db, dg = tv[10:]
        e0 = jnp.zeros((TB, RW_WIDTH), F32)
        _, vjp, (wl_t, sg) = jax.vjp(_rw_pre_fn(cv[5], cv[6], cv[7]), *zs, cv[1], cv[2], cv[3], cv[4], e0, e0, has_aux=True)
        g = vjp((dr1 + dr2, dw, dk1 + dk2, dv1 + dv2, da, db, dg))
        dzs = jnp.concatenate(g[:5], axis=1)
        dmu = jnp.concatenate([jnp.sum(g[n] * dif[n], axis=0, keepdims=True) for n in range(5)], axis=1)
        lora = (_dot_tn(_bf(wl_t), _bf(g[9])), _dot_tn(_bf(zs[3]), _bf(g[10])), _dot_tn(_bf(sg), _bf(dg)))
        return (dzs, dmu, g[5], g[6], g[7], g[8]) + lora

    tok_in = _rw_tok_in(proj) + [((c,) if c.ndim == 3 else (c, RW_WIDTH, 0)) for c in cots]
    return _tok_call("rw_pre_bwd", fn, L, TB, tok_in, [mu, w0, a0, k_k, k_a, w2pad, a2pad, g2],
                     [(SHIFT_COLS, F32)], [(1, SHIFT_COLS)] + [(1, RW_WIDTH)] * 4 + [(128, RW_WIDTH)] * 3)


def _rw_post_fn(wbot, diff=True):
    seg = _segsum(_head_indicator(RW_WIDTH), diff)
    mb = _mmc(wbot, diff) if wbot is not None else None

    def f(y, r, kf, v, g, ln_w, ln_b, r_k):
        mean = seg(y) * (1.0 / HEAD)
        yc = y - mean
        var = seg(yc * yc) * (1.0 / HEAD)
        yn = yc * lax.rsqrt(var + GN_EPS) * ln_w + ln_b
        bonus = seg(r * kf * r_k) * v
        out = (yn + bonus) * g
        res = mb(out) if mb is not None else out
        return res, out

    return f


def _rw_post_fwd(y, r, kf, v, g, ln_w, ln_b, r_k, L):
    def fn(i, tv, cv):
        out, _ = _rw_post_fn(None, False)(*tv, *cv)
        return (out,)

    return _tok_call("rw_post_fwd", fn, L, TB, [(t,) for t in (y, r, kf, v)] + [(g, RW_WIDTH, 0)], [ln_w, ln_b, r_k],
                     [(RW_WIDTH, F32)])[0]


def _rw_post_bwd(y, r, kf, v, g, dh1, ln_w, ln_b, r_k, wbot, L):
    def fn(i, tv, cv):
        _, vjp, out = jax.vjp(_rw_post_fn(cv[3]), *tv[:5], cv[0], cv[1], cv[2], has_aux=True)
        gr = vjp(tv[5])
        return gr[0], gr[1], gr[2], gr[3], gr[4], gr[5], gr[6], gr[7], _dot_tn(_bf(out), _bf(tv[5]))

    return _tok_call("rw_post_bwd", fn, L, TB, [(t,) for t in (y, r, kf, v)] + [(g, RW_WIDTH, 0), (dh1, D_MODEL, 0)],
                     [ln_w, ln_b, r_k, wbot], [("heads", F32)] + [(RW_WIDTH, F32)] * 4,
                     [(1, RW_WIDTH)] * 3 + [(RW_WIDTH, D_MODEL)])


def _ffn_fn(w1, w3, w2, diff=True):
    m1, m3, m2 = _mmc(w1, diff), _mmc(w3, diff), _mmc(w2, diff)

    def f(h1, norm_ffn, e1, e3):
        hn = _rms(h1, norm_ffn)
        a1 = m1(hn) + e1
        a3 = m3(hn) + e3
        hm = a1 * _sigmoid(a1) * a3
        return h1 + m2(hm), (hn, hm)

    return f


TB_FFN = 256


def _mixffn_fwd(x, s5_out, rw_out, wtop, wbot, norm_ffn, w1, w3, w2, L):
    def fn(i, tv, cv):
        h1 = tv[0] + _dot(_bf(tv[1]), cv[0]) + _dot(_bf(tv[2]), cv[1])
        h2, _ = _ffn_fn(cv[3], cv[4], cv[5], False)(h1, cv[2], 0.0, 0.0)
        return h1, h2

    return _tok_call("mixffn_fwd", fn, L, TB_FFN, [(x, D_MODEL, 0), (s5_out, S5_WIDTH, 0), (rw_out, RW_WIDTH, 0)],
                     [wtop, wbot, norm_ffn, w1, w3, w2], [(D_MODEL, F32), (D_MODEL, F32)])


def _ffn_bwd(h1, dh2, norm_ffn, w1, w3, w2, L):
    def fn(i, tv, cv):
        e0 = jnp.zeros((TB_FFN, FFN_HIDDEN), F32)
        _, vjp, (hn, hm) = jax.vjp(_ffn_fn(cv[1], cv[2], cv[3]), tv[0], cv[0], e0, e0, has_aux=True)
        dh1, dn, d1, d3 = vjp(tv[1])
        return dh1, d1, d3, hm, hn, dn

    return _tok_call("ffn_bwd", fn, L, TB_FFN, [(h1, D_MODEL, 0), (dh2, D_MODEL, 0)], [norm_ffn, w1, w3, w2],
                     [(D_MODEL, F32), (FFN_HIDDEN, BF16), (FFN_HIDDEN, BF16), (FFN_HIDDEN, BF16), (D_MODEL, BF16)],
                     [(1, D_MODEL)])


def _ple_loss_fb(h2, p, target, norm_ple, final_norm, wg, wu, L):
    def fn(i, tv, cv):
        mgate, mup = _mmc(cv[2]), _mmc(cv[3], False)

        def f(h2_, norm_ple_, final_norm_, eg, eu):
            hn = _rms(h2_, norm_ple_)
            gate = _sigmoid(mgate(hn) + eg)
            h3 = h2_ + gate * (mup(tv[1]) + eu)
            out = _rms(h3, final_norm_)
            d = out - tv[2]
            return 0.5 * jnp.sum(jnp.mean(d * d, axis=-1, keepdims=True)), hn

        e0 = jnp.zeros((TB, D_MODEL), F32)
        loss, vjp, hn = jax.vjp(f, tv[0], cv[0], cv[1], e0, e0, has_aux=True)
        dh2, dnp, dfn, deg, deu = vjp(jnp.ones((), F32))
        return (dh2, dh2, jnp.full((8, 128), loss, F32), dnp, dfn,
                _dot_tn(_bf(hn), _bf(deg)), _dot_tn(_bf(tv[1]), _bf(deu)))

    return _tok_call("ple_loss_fb", fn, L, TB, [(h2, D_MODEL, 0), (p, PLE_DIM, 0), (target, D_MODEL, 0)],
                     [norm_ple, final_norm, wg, wu], [(D_MODEL, F32), (D_MODEL, BF16)],
                     [(8, 128), (1, D_MODEL), (1, D_MODEL), (D_MODEL, D_MODEL), (PLE_DIM, D_MODEL)])


def _inproj_bwd(x, dh1, du, dzs, norm_mix, mu, w_u, w_z, L):
    nb = L // TB

    def fn(i, tv, cv):
        sub = lax.broadcasted_iota(jnp.int32, (TB, 1), 0)
        m = cv[1]
        b = tv[3] * m
        nxt = jnp.where(i == nb - 1, 0.0, tv[4][0:1, :] * m)
        dz = tv[3] * (1.0 - m) + jnp.where(sub == TB - 1, nxt, pltpu.roll(b, TB - 1, 0))
        dub, dzb = _bf(tv[2]), _bf(dz)
        dxn = _dot_nt(dub, cv[2]) + _dot_nt(dzb, cv[3])
        _, vjp = jax.vjp(_rms, tv[0], cv[0])
        dx, dn = vjp(dxn)
        return tv[1] + dx, jnp.concatenate([dub, dzb], axis=1), dn

    return _tok_call("inproj_bwd", fn, L, TB,
                     [(x, D_MODEL, 0), (dh1, D_MODEL, 0), (du, S5_WIDTH, 0), (dzs, SHIFT_COLS, 0), (dzs, SHIFT_COLS, 0, "next")],
                     [norm_mix, mu, w_u, w_z], [(D_MODEL, F32), (IN_COLS, BF16)], [(1, D_MODEL)])


def _eye8(dt):
    return jnp.eye(8, dtype=dt)


def _quarter_b(bb):
    return jnp.einsum("hg,qgcp->qhcgp", _eye8(bb.dtype), bb.reshape(S5_Q, 8, S5_GROUP, S5_STATE)).reshape(S5_Q, S5_QL, S5_QS)


def _unquarter_b(d):
    return jnp.einsum("qhcgp,hg->qgcp", d.reshape(S5_Q, 8, S5_GROUP, 8, S5_STATE), _eye8(d.dtype)).reshape(
        S5_GROUPS, S5_GROUP, S5_STATE)


def _quarter_c(c):
    return jnp.einsum("gh,qgcp->qgphc", _eye8(c.dtype), c.reshape(S5_Q, 8, S5_GROUP, S5_STATE)).reshape(S5_Q, S5_QS, S5_QL)


def _unquarter_c(d):
    return jnp.einsum("qgphc,gh->qgcp", d.reshape(S5_Q, 8, S5_STATE, 8, S5_GROUP), _eye8(d.dtype)).reshape(
        S5_GROUPS, S5_GROUP, S5_STATE)


def _local_step(x, p, target, W, late_weights=None, grads_ready=None, first_dep=None):
    L = x.shape[0]
    r2 = lambda v: v.reshape(1, -1)
    w_in = W["w_in"]
    w2pad = jnp.pad(W["rw_w2"], ((0, 64), (0, 0)))
    a2pad = jnp.pad(W["rw_a2"], ((64, 0), (0, 0)))
    mu = r2(W["rw_shift_mu"])
    rw_vec = [r2(W[n]) for n in ("rw_w0", "rw_a0", "rw_k_k", "rw_k_a")]
    ln_w, ln_b, r_k = r2(W["rw_ln_w"]), r2(W["rw_ln_b"]), r2(W["rw_r_k"])

    lam_re, lam_im = W["s5_lam_re"], W["s5_lam_im"]
    log_step = W["s5_log_step"].reshape(S5_GROUPS, 1)
    bt_re, bt_im = W["s5_b_re"].transpose(0, 2, 1), W["s5_b_im"].transpose(0, 2, 1)
    lb_re, lb_im, bb_re, bb_im = _s5_param_fwd(lam_re, lam_im, log_step, bt_re, bt_im)
    bq_re, bq_im = _quarter_b(bb_re).astype(BF16), _quarter_b(bb_im).astype(BF16)
    cq_re, cq_im = _quarter_c(W["s5_c_re"]).astype(BF16), _quarter_c(W["s5_c_im"]).astype(BF16)
    lbar = jnp.concatenate([lb_re.reshape(1, -1), lb_im.reshape(1, -1), jnp.zeros((6, S5_LANES), F32)], axis=0)
    dskip = r2(W["s5_d"])
    glu_b = r2(W["s5_glu_b"])
    norm_mix, norm_ffn, norm_ple, final_norm = (r2(W[n]) for n in ("norm_mix", "norm_ffn", "norm_ple", "final_norm"))

    proj, xn = _inproj_fwd(x, norm_mix, w_in, L, () if first_dep is None else (first_dep,))
    y_s5, ck5 = _s5_scan_fwd(proj, bq_re, bq_im, cq_re, cq_im, lbar, dskip, L, TB)
    s5_out = _s5_post_fwd(y_s5, W["s5_glu_w"], glu_b, L)
    r, wd, kf, v, a_s, b_s, g = _rw_pre_fwd(proj, mu, *rw_vec, w2pad, a2pad, W["rw_g2"], L)
    scan_in = (r, wd, kf, v, a_s, b_s)
    y_wkv, ckw = _wkv_fwd(*scan_in, L)
    rw_out = _rw_post_fwd(y_wkv, r, kf, v, g, ln_w, ln_b, r_k, L)
    if late_weights is not None:
        W = dict(W, **late_weights(rw_out))
    wtop, wbot = W["w_out"][:S5_WIDTH], W["w_out"][S5_WIDTH:]
    h1, h2 = _mixffn_fwd(x, s5_out, rw_out, wtop, wbot, norm_ffn, W["ffn_w1"], W["ffn_w3"], W["ffn_w2"], L)

    G = {}
    dh2, dh2_bf, loss_acc, G["norm_ple"], G["final_norm"], G["ple_gate_w"], G["ple_up_w"] = _ple_loss_fb(
        h2, p, target, norm_ple, final_norm, W["ple_gate_w"], W["ple_up_w"], L)
    dh1, da1, da3, hm, hn_ffn, G["norm_ffn"] = _ffn_bwd(h1, dh2, norm_ffn, W["ffn_w1"], W["ffn_w3"], W["ffn_w2"], L)
    G["ffn_w1"] = _mm_tn("dw_ffn_w1", hn_ffn, da1)
    G["ffn_w3"] = _mm_tn("dw_ffn_w3", hn_ffn, da3)
    G["ffn_w2"] = _mm_tn("dw_ffn_w2", hm, dh2_bf)
    dep_a = grads_ready(0, G) if grads_ready is not None else None
    dy_s5, G["s5_glu_b"], G["s5_glu_w"], d_wtop = _s5_post_bwd(y_s5, dh1, W["s5_glu_w"], glu_b, wtop, L,
                                                               () if dep_a is None else (dep_a,))
    dy_wkv, dr2, dk2, dv2, dg, G["rw_ln_w"], G["rw_ln_b"], G["rw_r_k"], d_wbot = _rw_post_bwd(
        y_wkv, r, kf, v, g, dh1, ln_w, ln_b, r_k, wbot, L)
    G["w_out"] = jnp.concatenate([d_wtop, d_wbot], axis=0)
    dep = grads_ready(1, G) if grads_ready is not None else None
    dr1, dwd, dk1, dv1, da_s, db_s = _wkv_bwd(*scan_in, dy_wkv, ckw, L, () if dep is None else (dep,))
    (dzs, G["rw_shift_mu"], G["rw_w0"], G["rw_a0"], G["rw_k_k"], G["rw_k_a"], d_w2pad, d_a2pad, G["rw_g2"]) = _rw_pre_bwd(
        proj, (dr1, dr2, dwd, dk1, dk2, dv1, dv2, da_s, db_s, dg), mu, *rw_vec, w2pad, a2pad, W["rw_g2"], L)
    G["rw_w2"], G["rw_a2"] = d_w2pad[:64], d_a2pad[64:]
    du, dbq_re, dbq_im, dcq_re, dcq_im, dlbar, G["s5_d"] = _s5_scan_bwd(
        proj, dy_s5, ck5, bq_re, bq_im, cq_re, cq_im, lbar, dskip, L, TB)
    G["s5_c_re"], G["s5_c_im"] = _unquarter_c(dcq_re), _unquarter_c(dcq_im)
    d_lam_re, d_lam_im, d_ls, d_bt_re, d_bt_im = _s5_param_bwd(
        lam_re, lam_im, log_step, bt_re, bt_im, dlbar[0].reshape(S5_GROUPS, S5_STATE), dlbar[1].reshape(S5_GROUPS, S5_STATE),
        _unquarter_b(dbq_re), _unquarter_b(dbq_im))
    G["s5_lam_re"], G["s5_lam_im"], G["s5_log_step"] = d_lam_re, d_lam_im, d_ls.reshape(S5_GROUPS)
    G["s5_b_re"], G["s5_b_im"] = d_bt_re.transpose(0, 2, 1), d_bt_im.transpose(0, 2, 1)
    dx, dproj, G["norm_mix"] = _inproj_bwd(x, dh1, du, dzs, norm_mix, mu, w_in[:, :S5_WIDTH], w_in[:, S5_WIDTH:], L)
    G["w_in"] = _mm_tn("dw_in", xn, dproj)
    return loss_acc[0, 0], dx, G


MESH_AXES = ("x", "y", "c")


def _all_gather(name, shards):
    nt = len(shards)

    def body(*refs):
        x_refs, out_refs = refs[:nt], refs[nt:2 * nt]
        send_sems, recv_sems, local_sems = refs[2 * nt:]
        x, y, c = lax.axis_index("x"), lax.axis_index("y"), lax.axis_index("c")
        me, sibling = (x, y, c), (x, y, 1 - c)
        chips = [(1 - x, y), (x, 1 - y), (1 - x, 1 - y)]

        def rows(t, px, py, pc):
            m_per = shards[t].shape[0]
            return out_refs[t].at[pl.ds((4 * px + 2 * py + pc) * m_per, m_per), :]

        def copy(t, k, block, to, src=None):
            return pltpu.make_async_remote_copy(
                src_ref=rows(t, *block) if src is None else src, dst_ref=rows(t, *block),
                send_sem=send_sems.at[7 * t + k], recv_sem=recv_sems.at[7 * t + k],
                device_id=to, device_id_type=pl.DeviceIdType.MESH)

        mine = [pltpu.make_async_copy(x_refs[t], rows(t, *me), local_sems.at[t]) for t in range(nt)]
        for cp in mine:
            cp.start()
        first = []
        for t in range(nt):
            first.append(copy(t, 0, me, sibling, src=x_refs[t]))
            first += [copy(t, 1 + j, me, (*chip, c), src=x_refs[t]) for j, chip in enumerate(chips)]
        for cp in first:
            cp.start()
        passed = []
        for t in range(nt):
            for j, chip in enumerate(chips):
                copy(t, 1 + j, (*chip, c), me).wait_recv()
                fwd = copy(t, 4 + j, (*chip, c), sibling)
                fwd.start()
                passed.append(fwd)
        for t in range(nt):
            copy(t, 0, sibling, me).wait_recv()
            for j, chip in enumerate(chips):
                copy(t, 4 + j, (*chip, 1 - c), me).wait_recv()
        for cp in first + passed:
            cp.wait_send()
        for cp in mine:
            cp.wait()

    return _pcall(body, name=name,
                  out_shape=[jax.ShapeDtypeStruct((N_DEV * a.shape[0], a.shape[1]), a.dtype) for a in shards],
                  in_specs=[_ANY] * nt, out_specs=[_ANY] * nt,
                  scratch_shapes=[pltpu.SemaphoreType.DMA((7 * nt,)), pltpu.SemaphoreType.DMA((7 * nt,)),
                                  pltpu.SemaphoreType.DMA((nt,))])(*shards)


_HBM = pl.BlockSpec(memory_space=pltpu.HBM)
_SEM = pl.BlockSpec(memory_space=pltpu.SEMAPHORE)
_EFFECT = pltpu.SideEffectType.DATAFLOW_SIDE_EFFECTING


def _peer_of(k):
    x, y, c = lax.axis_index("x"), lax.axis_index("y"), lax.axis_index("c")
    px, py, pc = x ^ ((k >> 2) & 1), y ^ ((k >> 1) & 1), c ^ (k & 1)
    return (px, py, pc), 4 * px + 2 * py + pc, 4 * x + 2 * y + c


def _direct_copy(t, k, src_refs, land_refs, send_sems, recv_sems, rows_of, gather):
    dev, peer, me = _peer_of(k)
    m = rows_of[t]
    src = src_refs[t] if gather else src_refs[t].at[pl.ds(peer * m, m), :]
    return pltpu.make_async_remote_copy(
        src_ref=src, dst_ref=land_refs[t].at[pl.ds(me * m, m), :],
        send_sem=send_sems.at[7 * t + k - 1], recv_sem=recv_sems.at[7 * t + k - 1],
        device_id=dev, device_id_type=pl.DeviceIdType.MESH)


def _direct_landing(t, k, src_refs, land_refs, send_sems, recv_sems, rows_of, gather):
    dev, peer, me = _peer_of(k)
    m = rows_of[t]
    src = src_refs[t] if gather else src_refs[t].at[pl.ds(me * m, m), :]
    return pltpu.make_async_remote_copy(
        src_ref=src, dst_ref=land_refs[t].at[pl.ds(peer * m, m), :],
        send_sem=send_sems.at[7 * t + k - 1], recv_sem=recv_sems.at[7 * t + k - 1],
        device_id=dev, device_id_type=pl.DeviceIdType.MESH)


def _direct_start(name, srcs, gather, dep=None):
    nt = len(srcs)
    rows_of = [a.shape[0] if gather else a.shape[0] // N_DEV for a in srcs]
    lands = [pltpu.with_memory_space_constraint(lax.empty((N_DEV * m, a.shape[1]), a.dtype), pltpu.HBM)
             for a, m in zip(srcs, rows_of)]

    n_dep = 0 if dep is None else 1

    def body(*refs):
        src_refs, land_refs = refs[:nt], refs[nt:2 * nt]
        send_sems, recv_sems = refs[2 * nt + n_dep], refs[2 * nt + n_dep + 1]
        token = refs[-1]
        for t in range(nt):
            for k in range(1, N_DEV):
                _direct_copy(t, k, src_refs, land_refs, send_sems, recv_sems, rows_of, gather).start()
        token[...] = jnp.zeros(token.shape, F32)

    out = _pcall(
        body, name=name,
        out_shape=(pltpu.SemaphoreType.DMA((7 * nt,)), pltpu.SemaphoreType.DMA((7 * nt,)),
                   *[pltpu.HBM(a.shape, a.dtype) for a in srcs], *[pltpu.HBM(a.shape, a.dtype) for a in lands],
                   jax.ShapeDtypeStruct((8, 128), F32)),
        in_specs=(_HBM,) * (2 * nt) + (pl.BlockSpec(memory_space=pl.ANY),) * n_dep,
        out_specs=(_SEM, _SEM) + (_HBM,) * (2 * nt) + (pl.BlockSpec(memory_space=pltpu.VMEM),),
        input_output_aliases={i: 2 + i for i in range(2 * nt)},
        compiler_params=pltpu.CompilerParams(has_side_effects=_EFFECT),
    )(*[pltpu.with_memory_space_constraint(a, pltpu.HBM) for a in srcs], *lands, *(() if dep is None else (dep,)))
    return (out[0], out[1], list(out[2:2 + nt]), list(out[2 + nt:2 + 2 * nt]), rows_of, gather), out[-1]


def _direct_wait(name, handle, after):
    send_sems, recv_sems, srcs, lands, rows_of, gather = handle
    nt = len(srcs)
    after = list(after) if isinstance(after, (list, tuple)) else [after]

    def body(*refs):
        src_refs, land_refs = refs[:nt], refs[nt:2 * nt]
        s_sems, r_sems = refs[2 * nt], refs[2 * nt + 1]
        for t in range(nt):
            for k in range(1, N_DEV):
                _direct_copy(t, k, src_refs, land_refs, s_sems, r_sems, rows_of, gather).wait_send()
                _direct_landing(t, k, src_refs, land_refs, s_sems, r_sems, rows_of, gather).wait_recv()

    out = _pcall(
        body, name=name,
        out_shape=tuple(pltpu.HBM(a.shape, a.dtype) for a in srcs) + tuple(pltpu.HBM(a.shape, a.dtype) for a in lands),
        in_specs=(_HBM,) * (2 * nt) + (_SEM, _SEM) + (pl.BlockSpec(memory_space=pl.ANY),) * len(after),
        out_specs=(_HBM,) * (2 * nt),
        input_output_aliases={i: i for i in range(2 * nt)},
        compiler_params=pltpu.CompilerParams(has_side_effects=_EFFECT),
    )(*srcs, *lands, send_sems, recv_sems, *after)
    return list(out[:nt]), list(out[nt:])


def _adamw_sharded(name, own, parts, w, m, v, rb, deps=()):
    R, N = own.shape

    def body(o_ref, p_ref, w_ref, m_ref, v_ref, *rest):
        g_ref, d_ref, nm_ref, nv_ref = rest[len(deps):]
        me = 4 * lax.axis_index("x") + 2 * lax.axis_index("y") + lax.axis_index("c")
        g = o_ref[...]
        for k in range(1, N_DEV):
            g = g + p_ref[me ^ k].astype(F32)
        nm = ADAM_B1 * m_ref[...] + (1.0 - ADAM_B1) * g
        nv = ADAM_B2 * v_ref[...] + (1.0 - ADAM_B2) * (g * g)
        m_hat = nm / (1.0 - ADAM_B1 ** ADAM_STEP)
        v_hat = nv / (1.0 - ADAM_B2 ** ADAM_STEP)
        g_ref[...] = g
        d_ref[...] = -ADAM_LR * (m_hat / (jnp.sqrt(v_hat) + ADAM_EPS) + ADAM_WD * w_ref[...])
        nm_ref[...] = nm
        nv_ref[...] = nv

    blk = pl.BlockSpec((rb, N), lambda i: (i, 0))
    sh = jax.ShapeDtypeStruct((R, N), F32)
    return _pcall(body, name=name, grid=(R // rb,),
                  in_specs=[blk, pl.BlockSpec((N_DEV, rb, N), lambda i: (0, i, 0)), blk, blk, blk]
                  + [pl.BlockSpec(d.shape, lambda i, nd=d.ndim: (0,) * nd) for d in deps],
                  out_specs=[blk] * 4, out_shape=[sh] * 4, compiler_params=_cparams(1))(own, parts, w, m, v, *deps)


LOSS_SLOT = "loss_partials"
SMALL_CLASSES = (
    (("s5_b_re", 32, 1024), ("s5_b_im", 32, 1024),
     ("norm_mix", 1, 1024), ("norm_ffn", 1, 1024), ("norm_ple", 1, 1024), ("final_norm", 1, 1024)),
    (("s5_d", 1, 512), ("s5_glu_b", 1, 512), ("rw_w0", 1, 512), ("rw_a0", 1, 512), ("rw_k_k", 1, 512), ("rw_k_a", 1, 512),
     ("rw_ln_w", 1, 512), ("rw_ln_b", 1, 512), ("rw_r_k", 1, 512)),
    (("rw_shift_mu", 1, 1792),),
    (("s5_lam_re", 32, 64), ("s5_lam_im", 32, 64), ("s5_c_re", 512, 64), ("s5_c_im", 512, 64)),
    (("s5_log_step", 1, 32), (LOSS_SLOT, 1, 32)),
)


def _class_rows(cls):
    return -(-sum(r for _, r, _ in cls) // 8) * 8


def _stack_class(cls, arrs):
    a = jnp.concatenate(arrs, axis=0) if len(arrs) > 1 else arrs[0]
    pad = _class_rows(cls) - a.shape[0]
    return jnp.pad(a, ((0, pad), (0, 0))) if pad else a


def _adamw_small(grads, w, m, v):
    names = [n for cls in SMALL_CLASSES for n, _, _ in cls]
    n_cls, n_par = len(SMALL_CLASSES), len(names)

    def body(*refs):
        g_refs = refs[:n_cls]
        w_refs, m_refs, v_refs = (refs[n_cls + i * n_par:n_cls + (i + 1) * n_par] for i in range(3))
        o_refs = refs[n_cls + 3 * n_par:]
        p = 0
        for cls, g_ref in zip(SMALL_CLASSES, g_refs):
            rc = _class_rows(cls)
            tot = g_ref[0:rc, :]
            for s_ in range(1, N_DEV):
                tot = tot + g_ref[s_ * rc:(s_ + 1) * rc, :]
            off = 0
            for _, r, _ in cls:
                g = tot[off:off + r, :]
                off += r
                nm = ADAM_B1 * m_refs[p][...] + (1.0 - ADAM_B1) * g
                nv = ADAM_B2 * v_refs[p][...] + (1.0 - ADAM_B2) * (g * g)
                m_hat = nm / (1.0 - ADAM_B1 ** ADAM_STEP)
                v_hat = nv / (1.0 - ADAM_B2 ** ADAM_STEP)
                o_refs[4 * p][...] = g
                o_refs[4 * p + 1][...] = -ADAM_LR * (m_hat / (jnp.sqrt(v_hat) + ADAM_EPS) + ADAM_WD * w_refs[p][...])
                o_refs[4 * p + 2][...] = nm
                o_refs[4 * p + 3][...] = nv
                p += 1

    shapes = [(r, c) for cls in SMALL_CLASSES for _, r, c in cls]
    out = _pcall(body, name="adamw_replicated",
                 out_shape=[jax.ShapeDtypeStruct(sh, F32) for sh in shapes for _ in range(4)],
                 compiler_params=pltpu.CompilerParams(vmem_limit_bytes=VMEM_LIMIT))(*grads, *w, *m, *v)
    return {n: out[4 * i:4 * i + 4] for i, n in enumerate(names)}


EARLY = (("w_in", True),)
LATE = (("ffn_w1", True), ("ffn_w3", True), ("ffn_w2", False), ("ple_gate_w", False), ("w_out", False))
GRAD_STAGES = (LATE[:4], LATE[4:])
MISC = (("s5_glu_w", False), ("rw_w2", True), ("rw_a2", True), ("rw_g2", True), ("ple_up_w", True))
SHARDED_NAMES = tuple(n for n, _ in EARLY + LATE + MISC)
PACK_COLS = 1024
WEIGHT_NAMES = ("norm_mix", "w_in", "s5_lam_re", "s5_lam_im", "s5_log_step", "s5_b_re", "s5_b_im", "s5_c_re", "s5_c_im", "s5_d",
                "s5_glu_w", "s5_glu_b", "rw_shift_mu", "rw_w0", "rw_w2", "rw_a0", "rw_a2", "rw_g2", "rw_k_k", "rw_k_a", "rw_r_k",
                "rw_ln_w", "rw_ln_b", "w_out", "norm_ffn", "ffn_w1", "ffn_w3", "ffn_w2", "norm_ple", "ple_gate_w", "ple_up_w",
                "final_norm")
SMALL_NAMES = tuple(n for n in WEIGHT_NAMES if n not in SHARDED_NAMES)
ARG_NAMES = ("x", "p") + WEIGHT_NAMES + ("loss_target",) + tuple("m_" + n for n in WEIGHT_NAMES) + tuple("v_" + n for n in WEIGHT_NAMES)


def _travel(a, tr):
    return a.T if tr else a


def _pack_misc(blocks):
    lead = blocks[0].shape[:-2]
    return jnp.concatenate([b.reshape(lead + (-1, PACK_COLS)) for b in blocks], axis=len(lead))


def _unpack_misc(packed, shapes):
    lead = packed.shape[:-2]
    out, off = [], 0
    for r, c in shapes:
        n = r * c // PACK_COLS
        out.append(lax.slice_in_dim(packed, off, off + n, axis=len(lead)).reshape(lead + (r, c)))
        off += n
    return out


def _kernel_impl(ins):
    x, p, target = ins["x"][0], ins["p"][0, 0], ins["loss_target"][0]
    me = 4 * lax.axis_index("x") + 2 * lax.axis_index("y") + lax.axis_index("c")
    small = {n: (ins[n] if n == "final_norm" else ins[n][0]) for n in SMALL_NAMES}
    trav = lambda pre, n, tr: _travel(ins[pre + n][0], tr)
    misc_shapes = [trav("", n, tr).shape for n, tr in MISC]

    early = _all_gather("ag_early", [trav("", n, tr).astype(BF16) for n, tr in EARLY]
                        + [_pack_misc([trav("", n, tr).astype(BF16) for n, tr in MISC])])
    late_handle, late_token = _direct_start("ag_late_start", [trav("", n, tr).astype(BF16) for n, tr in LATE], True, early[-1])
    W = dict(small)
    for (n, tr), g in zip(EARLY, early):
        W[n] = _travel(g, tr)
    for (n, tr), g in zip(MISC, _unpack_misc(early[-1].reshape(N_DEV, -1, PACK_COLS), misc_shapes)):
        W[n] = _travel(g.reshape(-1, g.shape[-1]), tr)

    def late_weights(after):
        shards, lands = _direct_wait("ag_late_wait", late_handle, after)
        full = [lax.dynamic_update_slice_in_dim(ld, sh, me * sh.shape[0], axis=0) for ld, sh in zip(lands, shards)]
        return {n: _travel(g, tr) for (n, tr), g in zip(LATE, full)}

    gt = lambda G, n, tr: _travel(G[n], tr)
    started = {}

    def grads_ready(stage, G):
        full = [gt(G, n, tr) for n, tr in GRAD_STAGES[stage]]
        started[stage] = (full, *_direct_start("grad_late_start%d" % stage, [a.astype(BF16) for a in full], False))
        return started[stage][2]

    loss_part, dx, G = _local_step(x, p, target, W, late_weights, grads_ready, late_token)

    misc_g = _pack_misc([gt(G, n, tr).reshape((N_DEV,) + shp) for (n, tr), shp in zip(MISC, misc_shapes)])
    early_full = [gt(G, n, tr) for n, tr in EARLY] + [misc_g.reshape(-1, PACK_COLS)]
    early_handle, early_token = _direct_start("grad_early_start", [a.astype(BF16) for a in early_full], False)
    view2 = lambda a, r, c: a.reshape(r, c)
    G[LOSS_SLOT] = jnp.full((1, 32), loss_part, F32)
    small_own = [_stack_class(cls, [view2(G[n], r, c) for n, r, c in cls]) for cls in SMALL_CLASSES]
    small_handle, small_token = _direct_start("grad_small_start", small_own, True)
    late_src, late_land = [], []
    for stage in range(len(GRAD_STAGES)):
        full, handle, _ = started[stage]
        _, land = _direct_wait("grad_late_wait%d" % stage, handle, small_token)
        late_src += full
        late_land += land

    outs = {}

    def emit(names_shapes, res):
        for tag, val in zip(("grad_", "delta_", "new_m_", "new_v_"), res):
            for n, v in names_shapes(val):
                outs[tag + n] = v

    def sharded_update(n, tr, src, land, deps=()):
        rows = src.shape[0] // N_DEV
        own = lax.dynamic_slice_in_dim(src, me * rows, rows, axis=0)
        res = _adamw_sharded("adamw_" + n, own, land.reshape(N_DEV, rows, land.shape[1]),
                             trav("", n, tr), trav("m_", n, tr), trav("v_", n, tr), _pick_rows(rows), deps)
        emit(lambda val: [(n, _travel(val, tr).reshape(ins[n].shape))], res)
        return list(res)

    for (n, tr), src, land in zip(LATE, late_src, late_land):
        sharded_update(n, tr, src, land, (early_token,))
    _, early_land = _direct_wait("grad_early_wait", early_handle, list(outs.values()))
    for (n, tr), src, land in zip(EARLY, early_full[:-1], early_land[:-1]):
        sharded_update(n, tr, src, land)
    pm = lambda pre: _pack_misc([trav(pre, n, tr) for n, tr in MISC])
    rows = early_full[-1].shape[0] // N_DEV
    res = _adamw_sharded("adamw_misc", lax.dynamic_slice_in_dim(early_full[-1], me * rows, rows, axis=0),
                         early_land[-1].reshape(N_DEV, rows, PACK_COLS), pm(""), pm("m_"), pm("v_"), rows)
    emit(lambda val: [(n, _travel(b, tr).reshape(ins[n].shape)) for (n, tr), b in zip(MISC, _unpack_misc(val, misc_shapes))], res)
    small_src, small_land = _direct_wait("grad_small_wait", small_handle, res[0])
    small_all = [lax.dynamic_update_slice_in_dim(ld, sr, me * sr.shape[0], axis=0) for ld, sr in zip(small_land, small_src)]
    flat_small = [(n, r, c) for cls in SMALL_CLASSES for n, r, c in cls]
    ins = dict(ins, **{pre + LOSS_SLOT: jnp.zeros((1, 32), F32) for pre in ("", "m_", "v_")})
    res = _adamw_small(small_all, *[[view2(ins[pre + n], r, c) for n, r, c in flat_small] for pre in ("", "m_", "v_")])
    loss = res.pop(LOSS_SLOT)[0][0, 0]
    for n, _, _ in flat_small[:-1]:
        for tag, val in zip(("grad_", "delta_", "new_m_", "new_v_"), res[n]):
            outs[tag + n] = val.reshape(ins[n].shape)
    res = [loss, dx[None]]
    for tag in ("grad_", "delta_", "new_m_", "new_v_"):
        res += [outs[tag + n] for n in WEIGHT_NAMES]
    return tuple(res)


def _pick_rows(r):
    best = 8
    for b in range(8, 257, 8):
        if r % b == 0:
            best = b
    return best


def kernel(x, p, norm_mix, w_in, s5_lam_re, s5_lam_im, s5_log_step, s5_b_re, s5_b_im, s5_c_re, s5_c_im, s5_d, s5_glu_w, s5_glu_b, rw_shift_mu, rw_w0, rw_w2, rw_a0, rw_a2, rw_g2, rw_k_k, rw_k_a, rw_r_k, rw_ln_w, rw_ln_b, w_out, norm_ffn, ffn_w1, ffn_w3, ffn_w2, norm_ple, ple_gate_w, ple_up_w, final_norm, loss_target, m_norm_mix, m_w_in, m_s5_lam_re, m_s5_lam_im, m_s5_log_step, m_s5_b_re, m_s5_b_im, m_s5_c_re, m_s5_c_im, m_s5_d, m_s5_glu_w, m_s5_glu_b, m_rw_shift_mu, m_rw_w0, m_rw_w2, m_rw_a0, m_rw_a2, m_rw_g2, m_rw_k_k, m_rw_k_a, m_rw_r_k, m_rw_ln_w, m_rw_ln_b, m_w_out, m_norm_ffn, m_ffn_w1, m_ffn_w3, m_ffn_w2, m_norm_ple, m_ple_gate_w, m_ple_up_w, m_final_norm, v_norm_mix, v_w_in, v_s5_lam_re, v_s5_lam_im, v_s5_log_step, v_s5_b_re, v_s5_b_im, v_s5_c_re, v_s5_c_im, v_s5_d, v_s5_glu_w, v_s5_glu_b, v_rw_shift_mu, v_rw_w0, v_rw_w2, v_rw_a0, v_rw_a2, v_rw_g2, v_rw_k_k, v_rw_k_a, v_rw_r_k, v_rw_ln_w, v_rw_ln_b, v_w_out, v_norm_ffn, v_ffn_w1, v_ffn_w3, v_ffn_w2, v_norm_ple, v_ple_gate_w, v_ple_up_w, v_final_norm):
    return _kernel_impl(dict(zip(ARG_NAMES, (x, p, norm_mix, w_in, s5_lam_re, s5_lam_im, s5_log_step, s5_b_re, s5_b_im, s5_c_re, s5_c_im, s5_d, s5_glu_w, s5_glu_b, rw_shift_mu, rw_w0, rw_w2, rw_a0, rw_a2, rw_g2, rw_k_k, rw_k_a, rw_r_k, rw_ln_w, rw_ln_b, w_out, norm_ffn, ffn_w1, ffn_w3, ffn_w2, norm_ple, ple_gate_w, ple_up_w, final_norm, loss_target, m_norm_mix, m_w_in, m_s5_lam_re, m_s5_lam_im, m_s5_log_step, m_s5_b_re, m_s5_b_im, m_s5_c_re, m_s5_c_im, m_s5_d, m_s5_glu_w, m_s5_glu_b, m_rw_shift_mu, m_rw_w0, m_rw_w2, m_rw_a0, m_rw_a2, m_rw_g2, m_rw_k_k, m_rw_k_a, m_rw_r_k, m_rw_ln_w, m_rw_ln_b, m_w_out, m_norm_ffn, m_ffn_w1, m_ffn_w3, m_ffn_w2, m_norm_ple, m_ple_gate_w, m_ple_up_w, m_final_norm, v_norm_mix, v_w_in, v_s5_lam_re, v_s5_lam_im, v_s5_log_step, v_s5_b_re, v_s5_b_im, v_s5_c_re, v_s5_c_im, v_s5_d, v_s5_glu_w, v_s5_glu_b, v_rw_shift_mu, v_rw_w0, v_rw_w2, v_rw_a0, v_rw_a2, v_rw_g2, v_rw_k_k, v_rw_k_a, v_rw_r_k, v_rw_ln_w, v_rw_ln_b, v_w_out, v_norm_ffn, v_ffn_w1, v_ffn_w3, v_ffn_w2, v_norm_ple, v_ple_gate_w, v_ple_up_w, v_final_norm))))
```

```python
import functools

import jax
import jax.numpy as jnp
from jax import lax
from jax.experimental import pallas as pl
from jax.experimental.pallas import tpu as pltpu

F32 = jnp.float32
BF16 = jnp.bfloat16

D_MODEL = 1024
S5_WIDTH = 512
RW_WIDTH = 512
S5_GROUP = 16
S5_GROUPS = 32
S5_STATE = 64
S5_LANES = S5_GROUPS * S5_STATE
HEAD = 64
SHIFT_COLS = 1792
IN_COLS = 2304
FFN_HIDDEN = 2816
PLE_DIM = 256
RMS_EPS = 1e-6
GN_EPS = 64e-5
L2_EPS = 1e-12
CHUNK = 64
N_DEV = 8

ADAM_LR = 0.001
ADAM_B1 = 0.9
ADAM_B2 = 0.999
ADAM_EPS = 1e-08
ADAM_WD = 0.01
ADAM_STEP = 10

VMEM_LIMIT = 56 * 1024 * 1024
_ANY = pl.BlockSpec(memory_space=pl.ANY)


def _pcall(body, **kw):
    return pl.pallas_call(body, **kw)


def _cparams(n_grid):
    return pltpu.CompilerParams(dimension_semantics=("arbitrary",) * n_grid, vmem_limit_bytes=VMEM_LIMIT)


def _dot(a, b):
    return jnp.dot(a, b, preferred_element_type=F32)


def _dot_nt(a, b):
    return lax.dot_general(a, b, (((1,), (1,)), ((), ())), preferred_element_type=F32)


def _dot_tn(a, b):
    return lax.dot_general(a, b, (((0,), (0,)), ((), ())), preferred_element_type=F32)


def _mmc(w, diff=True, tr=False):
    fw, bw = (_dot_nt, _dot) if tr else (_dot, _dot_nt)
    if not diff:
        return lambda x: fw(x.astype(BF16), w)

    @jax.custom_vjp
    def f(x):
        return fw(x.astype(BF16), w)

    def fwd(x):
        return fw(x.astype(BF16), w), None

    def bwd(_, dy):
        return (bw(dy.astype(BF16), w),)

    f.defvjp(fwd, bwd)
    return f


def _split_dot(x, m, n_split):
    acc = None
    rem = x
    for s in range(n_split):
        part = rem.astype(BF16)
        t = _dot(part, m)
        acc = t if acc is None else acc + t
        if s + 1 < n_split:
            rem = rem - part.astype(F32)
    return acc


def _segsum(m, diff=True):
    if not diff:
        return lambda x: _split_dot(x, m, 2)

    @jax.custom_vjp
    def f(x):
        return _split_dot(x, m, 2)

    def fwd(x):
        return _split_dot(x, m, 2), None

    def bwd(_, dy):
        return (_split_dot(dy, m, 2),)

    f.defvjp(fwd, bwd)
    return f


def _head_indicator(n):
    r = lax.broadcasted_iota(jnp.int32, (n, n), 0) // HEAD
    c = lax.broadcasted_iota(jnp.int32, (n, n), 1) // HEAD
    return (r == c).astype(BF16)


def _rms(x, g):
    return x * lax.rsqrt(jnp.mean(x * x, axis=-1, keepdims=True) + RMS_EPS) * g


def _softplus(x):
    return jnp.maximum(x, 0.0) + jnp.log(1.0 + jnp.exp(-jnp.abs(x)))


def _sigmoid(x):
    return 1.0 / (1.0 + jnp.exp(-x))


def _gelu(x):
    return 0.5 * x * (1.0 + jnp.tanh(0.7978845608028654 * (x + 0.044715 * (x * x * x))))


def _tok_call(name, fn, L, TB, tok_in, const_in, tok_out, acc_out=(), deps=()):
    nb = L // TB
    g8 = TB // 8
    in_specs, args = [], []
    for spec in tok_in:
        if len(spec) == 1:
            arr = spec[0]
            in_specs.append(pl.BlockSpec((arr.shape[0], TB, HEAD), lambda i: (0, i, 0)))
            args.append(arr)
            continue
        arr, width, cb = spec[:3]
        mode = spec[3] if len(spec) > 3 else None
        if mode is None:
            in_specs.append(pl.BlockSpec((TB, width), lambda i, cb=cb: (i, cb)))
        elif mode == "prev":
            in_specs.append(pl.BlockSpec((8, width), lambda i, cb=cb: (jnp.maximum(i * g8 - 1, 0), cb)))
        else:
            in_specs.append(pl.BlockSpec((8, width), lambda i, cb=cb: (jnp.minimum((i + 1) * g8, L // 8 - 1), cb)))
        args.append(arr)
    for c in const_in:
        in_specs.append(pl.BlockSpec(c.shape, lambda i, nd=c.ndim: (0,) * nd, pipeline_mode=pl.Buffered(1)))
        args.append(c)
    for d in deps:
        in_specs.append(pl.BlockSpec(d.shape, lambda i, nd=d.ndim: (0,) * nd))
        args.append(d)
    out_shape, out_specs = [], []
    for width, dt in tok_out:
        if width == "heads":
            out_shape.append(jax.ShapeDtypeStruct((N_HEAD, L, HEAD), dt))
            out_specs.append(pl.BlockSpec((N_HEAD, TB, HEAD), lambda i: (0, i, 0)))
            continue
        out_shape.append(jax.ShapeDtypeStruct((L, width), dt))
        out_specs.append(pl.BlockSpec((TB, width), lambda i: (i, 0)))
    for shp in acc_out:
        out_shape.append(jax.ShapeDtypeStruct(shp, F32))
        out_specs.append(pl.BlockSpec(shp, lambda i, nd=len(shp): (0,) * nd))
    n_tok, n_const, n_to = len(tok_in), len(const_in), len(tok_out)

    def body(*refs):
        i = pl.program_id(0)
        tv = [r[...] if len(r.shape) == 2 else jnp.concatenate([r[h] for h in range(r.shape[0])], axis=1)
              for r in refs[:n_tok]]
        cv = [r[...] for r in refs[n_tok:n_tok + n_const]]
        orefs = refs[n_tok + n_const + len(deps):]
        outs = fn(i, tv, cv)
        for r, v in zip(orefs[:n_to], outs[:n_to]):
            if len(r.shape) == 3:
                for h in range(r.shape[0]):
                    r[h] = v[:, h * HEAD:(h + 1) * HEAD].astype(r.dtype)
            else:
                r[...] = v.astype(r.dtype)
        for r, v in zip(orefs[n_to:], outs[n_to:]):
            @pl.when(i == 0)
            def _(r=r):
                r[...] = jnp.zeros(r.shape, r.dtype)

            r[...] += v

    res = _pcall(body, name=name, grid=(nb,), in_specs=in_specs, out_specs=out_specs, out_shape=out_shape,
                 compiler_params=_cparams(1))(*args)
    return res


def _pick_block(n, cap):
    best = None
    for b in range(128, min(n, cap) + 1, 128):
        if n % b == 0:
            best = b
    return best if best is not None else n


def _mm_tn(name, a, b):
    T, M = a.shape
    N = b.shape[1]
    bm, bn, bt = _pick_block(M, 1536), _pick_block(N, 1536), _pick_block(T, 512)
    nt = T // bt

    def body(a_ref, b_ref, o_ref, ob_ref):
        t = pl.program_id(2)

        @pl.when(t == 0)
        def _():
            o_ref[...] = jnp.zeros(o_ref.shape, F32)

        o_ref[...] += _dot_tn(a_ref[...].astype(BF16), b_ref[...].astype(BF16))

        @pl.when(t == nt - 1)
        def _():
            ob_ref[...] = o_ref[...].astype(BF16)

    oblk = pl.BlockSpec((bm, bn), lambda m, n, t: (m, n))
    return _pcall(body, name=name, grid=(M // bm, N // bn, nt),
                  in_specs=[pl.BlockSpec((bt, bm), lambda m, n, t: (t, m)), pl.BlockSpec((bt, bn), lambda m, n, t: (t, n))],
                  out_specs=[oblk, oblk],
                  out_shape=[jax.ShapeDtypeStruct((M, N), F32), jax.ShapeDtypeStruct((M, N), BF16)],
                  compiler_params=_cparams(3))(a, b)


def _s5_param_fn(lam_re, lam_im, log_step, bt_re, bt_im):
    dt = jnp.exp(log_step)
    e = jnp.exp(lam_re * dt)
    lb_re = e * jnp.cos(lam_im * dt)
    lb_im = e * jnp.sin(lam_im * dt)
    den = lam_re * lam_re + lam_im * lam_im
    nr, ni = lb_re - 1.0, lb_im
    co_re = (nr * lam_re + ni * lam_im) / den
    co_im = (ni * lam_re - nr * lam_im) / den
    cr, ci = co_re[:, None, :], co_im[:, None, :]
    return lb_re, lb_im, cr * bt_re - ci * bt_im, cr * bt_im + ci * bt_re


def _s5_param_fwd(lam_re, lam_im, log_step, bt_re, bt_im):
    def body(a, b, c, d, e, o1, o2, o3, o4):
        r = _s5_param_fn(a[...], b[...], c[...], d[...], e[...])
        o1[...], o2[...], o3[...], o4[...] = r

    sh = jax.ShapeDtypeStruct
    return _pcall(body, name="s5_param_fwd",
                  out_shape=[sh(lam_re.shape, F32), sh(lam_re.shape, F32), sh(bt_re.shape, F32), sh(bt_re.shape, F32)])(
        lam_re, lam_im, log_step, bt_re, bt_im)


def _s5_param_bwd(lam_re, lam_im, log_step, bt_re, bt_im, d_lb_re, d_lb_im, d_bb_re, d_bb_im):
    def body(a, b, c, d, e, g1, g2, g3, g4, o1, o2, o3, o4, o5):
        _, vjp = jax.vjp(_s5_param_fn, a[...], b[...], c[...], d[...], e[...])
        r = vjp((g1[...], g2[...], g3[...], g4[...]))
        o1[...], o2[...], o3[...], o4[...], o5[...] = r

    sh = jax.ShapeDtypeStruct
    return _pcall(body, name="s5_param_bwd",
                  out_shape=[sh(lam_re.shape, F32), sh(lam_re.shape, F32), sh(log_step.shape, F32),
                             sh(bt_re.shape, F32), sh(bt_re.shape, F32)])(
        lam_re, lam_im, log_step, bt_re, bt_im, d_lb_re, d_lb_im, d_bb_re, d_bb_im)


def _cmul(ar, ai, br, bi):
    return ar * br - ai * bi, ar * bi + ai * br


def _scan_consts(lr, li, reverse):
    n = lr.shape[1]
    sub = lax.broadcasted_iota(jnp.int32, (8, n), 0)
    pows = [(lr, li)]
    for _ in range(7):
        pows.append(_cmul(pows[-1][0], pows[-1][1], lr, li))
    steps = []
    for s in (1, 2, 4):
        m = (sub < 8 - s) if reverse else (sub >= s)
        pr, pi = pows[s - 1]
        steps.append((s, jnp.where(m, jnp.broadcast_to(pr, (8, n)), 0.0), jnp.where(m, jnp.broadcast_to(pi, (8, n)), 0.0)))
    wr = jnp.zeros((8, n), F32)
    wi = jnp.zeros((8, n), F32)
    for r in range(8):
        e = (8 - r) if reverse else (r + 1)
        wr = jnp.where(sub == r, jnp.broadcast_to(pows[e - 1][0], (8, n)), wr)
        wi = jnp.where(sub == r, jnp.broadcast_to(pows[e - 1][1], (8, n)), wi)
    return steps, wr, wi


S5_Q = 4
S5_QL = S5_WIDTH // S5_Q
S5_QS = S5_LANES // S5_Q
S5_NT = S5_LANES // 128
S5_QT = S5_QS // 128


def _s5_power_table(lb_ref, pw_re, pw_im, seg):
    for j in range(S5_NT):
        lr = jnp.broadcast_to(lb_ref[0:1, j * 128:(j + 1) * 128], (8, 128))
        li = jnp.broadcast_to(lb_ref[1:2, j * 128:(j + 1) * 128], (8, 128))

        def step(i, c, lr=lr, li=li, j=j):
            pw_re[j, i] = c[0]
            pw_im[j, i] = c[1]
            return _cmul(c[0], c[1], lr, li)

        lax.fori_loop(0, seg, step, (lr, li))


def _seg_scan(sre, sim, carry, lb_ref, pw_re, pw_im, rows, reverse):
    seg = rows // 8
    sgn = -1.0 if reverse else 1.0
    sub = lax.broadcasted_iota(jnp.int32, (8, 128), 0)
    rows_at = lambda i: pl.ds(pl.multiple_of(i * 8, 8), 8)
    entering = {}
    half_tiles = S5_NT // 2
    for half in range(2):
        tiles = list(range(half * half_tiles, (half + 1) * half_tiles))
        lam8 = [(jnp.broadcast_to(lb_ref[0:1, j * 128:(j + 1) * 128], (8, 128)),
                 sgn * jnp.broadcast_to(lb_ref[1:2, j * 128:(j + 1) * 128], (8, 128))) for j in tiles]

        def p1(ii, c):
            i = (seg - 1 - ii) if reverse else ii
            out = []
            for n, j in enumerate(tiles):
                lr, li = lam8[n]
                cr, ci = c[2 * n], c[2 * n + 1]
                nr = lr * cr - li * ci + sre[j, rows_at(i), :]
                ni = lr * ci + li * cr + sim[j, rows_at(i), :]
                sre[j, rows_at(i), :] = nr
                sim[j, rows_at(i), :] = ni
                out += [nr, ni]
            return tuple(out)

        ends = lax.fori_loop(0, seg, p1, tuple(jnp.zeros((8, 128), F32) for _ in range(2 * len(tiles))))
        cs = []
        for n, j in enumerate(tiles):
            ls = slice(j * 128, (j + 1) * 128)
            steps, wr, wi = _scan_consts(pw_re[j, seg - 1][0:1, :], sgn * pw_im[j, seg - 1][0:1, :], reverse)
            tr, ti = ends[2 * n], ends[2 * n + 1]
            for sft, pr, pi in steps:
                sh = (8 - sft) if reverse else sft
                yr, yi = pltpu.roll(tr, sh, 0), pltpu.roll(ti, sh, 0)
                tr, ti = tr + pr * yr - pi * yi, ti + pr * yi + pi * yr
            cin_r, cin_i = carry[0:1, ls], carry[1:2, ls]
            tr, ti = tr + wr * cin_r - wi * cin_i, ti + wr * cin_i + wi * cin_r
            edge_out, edge_in, sh = (0, 7, 7) if reverse else (7, 0, 1)
            carry[0:1, ls] = tr[edge_out:edge_out + 1, :]
            carry[1:2, ls] = ti[edge_out:edge_out + 1, :]
            cr = jnp.where(sub == edge_in, jnp.broadcast_to(cin_r, (8, 128)), pltpu.roll(tr, sh, 0))
            ci = jnp.where(sub == edge_in, jnp.broadcast_to(cin_i, (8, 128)), pltpu.roll(ti, sh, 0))
            cs += [cr, ci]
            entering[j] = (cr, ci)

        def p2(i, _):
            k = (seg - 1 - i) if reverse else i
            for n, j in enumerate(tiles):
                pr, pi = pw_re[j, k], pw_im[j, k]
                cr, ci = cs[2 * n], cs[2 * n + 1]
                if reverse:
                    sre[j, rows_at(i), :] = sre[j, rows_at(i), :] + pr * cr + pi * ci
                    sim[j, rows_at(i), :] = sim[j, rows_at(i), :] + pr * ci - pi * cr
                else:
                    sre[j, rows_at(i), :] = sre[j, rows_at(i), :] + pr * cr - pi * ci
                    sim[j, rows_at(i), :] = sim[j, rows_at(i), :] + pr * ci + pi * cr
            return 0

        lax.fori_loop(0, seg, p2, 0, unroll=2)
    return entering


class _SegIO:
    def __init__(self, hbm, buf, sems, rows, width, col0=0):
        self.hbm, self.buf, self.sems, self.rows, self.seg, self.width, self.col0 = hbm, buf, sems, rows, rows // 8, width, col0

    def _copies(self, blk, slot, to_vmem):
        out = []
        for r in range(8):
            h = self.hbm.at[pl.ds(blk * self.rows + r * self.seg, self.seg), pl.ds(self.col0, self.width)]
            v = self.buf.at[slot, :, r, :]
            out.append(pltpu.make_async_copy(h, v, self.sems.at[slot, r]) if to_vmem
                       else pltpu.make_async_copy(v, h, self.sems.at[slot, r]))
        return out

    def start(self, blk, slot, to_vmem):
        for cp in self._copies(blk, slot, to_vmem):
            cp.start()

    def wait(self, blk, slot, to_vmem):
        for cp in self._copies(blk, slot, to_vmem):
            cp.wait()

    def value(self, slot):
        return self.buf[slot].reshape(self.rows, self.width)

    def store(self, slot, val):
        self.buf[slot] = val.reshape(self.seg, 8, self.width)


def _seg_pipeline(i, nb, blk_of, ins, outs, compute):
    slot = i % 2

    @pl.when(i == 0)
    def _():
        for io in ins:
            io.start(blk_of(0), 0, True)

    @pl.when(i + 1 < nb)
    def _():
        for io in ins:
            io.start(blk_of(i + 1), 1 - slot, True)

    for io in ins:
        io.wait(blk_of(i), slot, True)

    @pl.when(i >= 2)
    def _():
        for io in outs:
            io.wait(blk_of(i - 2), slot, False)

    compute(slot)
    for io in outs:
        io.start(blk_of(i), slot, False)

    @pl.when(i == nb - 1)
    def _():
        for io in outs:
            if nb >= 2:
                io.wait(blk_of(i - 1), 1 - slot, False)
            io.wait(blk_of(i), slot, False)


def _s5_scan_fwd(proj, bq_re, bq_im, cq_re, cq_im, lbar, dskip, L, TB):
    nb = L // TB
    seg = TB // 8

    def body(u_hbm, bre, bim, cre, cim, lb_ref, d_ref, y_hbm, ck_ref, sre, sim, carry, pw_re, pw_im,
             ubuf, ybuf, sem_u, sem_y):
        i = pl.program_id(0)
        u_io = _SegIO(u_hbm, ubuf, sem_u, TB, S5_WIDTH)
        y_io = _SegIO(y_hbm, ybuf, sem_y, TB, S5_WIDTH)

        @pl.when(i == 0)
        def _():
            carry[...] = jnp.zeros(carry.shape, F32)
            _s5_power_table(lb_ref, pw_re, pw_im, seg)

        ck_ref[0] = carry[...]

        def compute(slot):
            u = u_io.value(slot)
            ub = u.astype(BF16)
            for q in range(S5_Q):
                uq = ub[:, q * S5_QL:(q + 1) * S5_QL]
                vr, vi = _dot(uq, bre[q]), _dot(uq, bim[q])
                for jj in range(S5_QT):
                    sre[q * S5_QT + jj] = vr[:, jj * 128:(jj + 1) * 128]
                    sim[q * S5_QT + jj] = vi[:, jj * 128:(jj + 1) * 128]
            _seg_scan(sre, sim, carry, lb_ref, pw_re, pw_im, TB, False)
            ys = []
            for q in range(S5_Q):
                sl = slice(q * S5_QL, (q + 1) * S5_QL)
                sr = jnp.concatenate([sre[q * S5_QT + jj] for jj in range(S5_QT)], axis=1).astype(BF16)
                si = jnp.concatenate([sim[q * S5_QT + jj] for jj in range(S5_QT)], axis=1).astype(BF16)
                ys.append(_dot(sr, cre[q]) - _dot(si, cim[q]) + u[:, sl] * d_ref[:, sl])
            y_io.store(slot, jnp.concatenate(ys, axis=1))

        _seg_pipeline(i, nb, lambda st: st, [u_io], [y_io], compute)

    full = lambda a: pl.BlockSpec(a.shape, lambda i, nd=a.ndim: (0,) * nd)
    st = pltpu.VMEM((S5_NT, TB, 128), F32)
    pw = pltpu.VMEM((S5_NT, seg, 8, 128), F32)
    io = pltpu.VMEM((2, seg, 8, S5_WIDTH), F32)
    return _pcall(
        body, name="s5_scan_fwd", grid=(nb,),
        in_specs=[_ANY, full(bq_re), full(bq_im), full(cq_re), full(cq_im), full(lbar), full(dskip)],
        out_specs=[_ANY, pl.BlockSpec((1, 8, S5_LANES), lambda i: (i, 0, 0))],
        out_shape=[jax.ShapeDtypeStruct((L, S5_WIDTH), F32), jax.ShapeDtypeStruct((nb, 8, S5_LANES), F32)],
        scratch_shapes=[st, st, pltpu.VMEM((8, S5_LANES), F32), pw, pw, io, io,
                        pltpu.SemaphoreType.DMA((2, 8)), pltpu.SemaphoreType.DMA((2, 8))],
        compiler_params=_cparams(1))(proj, bq_re, bq_im, cq_re, cq_im, lbar, dskip)


def _s5_scan_bwd(proj, dy, ck, bq_re, bq_im, cq_re, cq_im, lbar, dskip, L, TB):
    nb = L // TB
    seg = TB // 8

    def body(u_hbm, dy_hbm, ck_ref, bre, bim, cre, cim, lb_ref, d_ref,
             du_hbm, dbre, dbim, dcre, dcim, dlb_ref, dd_ref, sre, sim, gre, gim, carry, gcarry, pw_re, pw_im,
             ubuf, dybuf, dubuf, sem_u, sem_dy, sem_du):
        i = pl.program_id(0)
        u_io = _SegIO(u_hbm, ubuf, sem_u, TB, S5_WIDTH)
        dy_io = _SegIO(dy_hbm, dybuf, sem_dy, TB, S5_WIDTH)
        du_io = _SegIO(du_hbm, dubuf, sem_du, TB, S5_WIDTH)

        @pl.when(i == 0)
        def _():
            gcarry[...] = jnp.zeros(gcarry.shape, F32)
            dbre[...] = jnp.zeros(dbre.shape, F32)
            dbim[...] = jnp.zeros(dbim.shape, F32)
            dcre[...] = jnp.zeros(dcre.shape, F32)
            dcim[...] = jnp.zeros(dcim.shape, F32)
            dlb_ref[...] = jnp.zeros(dlb_ref.shape, F32)
            dd_ref[...] = jnp.zeros(dd_ref.shape, F32)
            _s5_power_table(lb_ref, pw_re, pw_im, seg)

        def compute(slot):
            u = u_io.value(slot)
            dy_v = dy_io.value(slot)
            ub = u.astype(BF16)
            dyb = dy_v.astype(BF16)
            carry[...] = ck_ref[0]
            for q in range(S5_Q):
                uq = ub[:, q * S5_QL:(q + 1) * S5_QL]
                dq = dyb[:, q * S5_QL:(q + 1) * S5_QL]
                vr, vi = _dot(uq, bre[q]), _dot(uq, bim[q])
                hr, hi = _dot_nt(dq, cre[q]), -_dot_nt(dq, cim[q])
                for jj in range(S5_QT):
                    ls = slice(jj * 128, (jj + 1) * 128)
                    sre[q * S5_QT + jj] = vr[:, ls]
                    sim[q * S5_QT + jj] = vi[:, ls]
                    gre[q * S5_QT + jj] = hr[:, ls]
                    gim[q * S5_QT + jj] = hi[:, ls]
            entering = _seg_scan(sre, sim, carry, lb_ref, pw_re, pw_im, TB, False)
            _seg_scan(gre, gim, gcarry, lb_ref, pw_re, pw_im, TB, True)

            rows_at = lambda k: pl.ds(pl.multiple_of(k * 8, 8), 8)
            for half in range(2):
                tiles = list(range(half * (S5_NT // 2), (half + 1) * (S5_NT // 2)))
                acc0 = []
                for j in tiles:
                    er, ei = entering[j]
                    gr0, gi0 = gre[j, rows_at(0), :], gim[j, rows_at(0), :]
                    acc0 += [gr0 * er + gi0 * ei, gi0 * er - gr0 * ei]

                def acc_step(k, acc, tiles=tiles):
                    out = []
                    for n, j in enumerate(tiles):
                        gr, gi_ = gre[j, rows_at(k), :], gim[j, rows_at(k), :]
                        spr, spi = sre[j, rows_at(k - 1), :], sim[j, rows_at(k - 1), :]
                        out += [acc[2 * n] + gr * spr + gi_ * spi, acc[2 * n + 1] - gr * spi + gi_ * spr]
                    return tuple(out)

                acc = lax.fori_loop(1, seg, acc_step, tuple(acc0))
                for n, j in enumerate(tiles):
                    ls = slice(j * 128, (j + 1) * 128)
                    dlb_ref[0:1, ls] += jnp.sum(acc[2 * n], axis=0, keepdims=True)
                    dlb_ref[1:2, ls] += jnp.sum(acc[2 * n + 1], axis=0, keepdims=True)

            dd_ref[...] += jnp.sum(dy_v * u, axis=0, keepdims=True)
            dus = []
            for q in range(S5_Q):
                sl = slice(q * S5_QL, (q + 1) * S5_QL)
                cat = lambda ref: jnp.concatenate([ref[q * S5_QT + jj] for jj in range(S5_QT)], axis=1).astype(BF16)
                grq, giq = cat(gre), cat(gim)
                dus.append(_dot_nt(grq, bre[q]) + _dot_nt(giq, bim[q]) + dy_v[:, sl] * d_ref[:, sl])
                dbre[q] += _dot_tn(ub[:, sl], grq)
                dbim[q] += _dot_tn(ub[:, sl], giq)
                dcre[q] += _dot_tn(cat(sre), dyb[:, sl])
                dcim[q] -= _dot_tn(cat(sim), dyb[:, sl])
            du_io.store(slot, jnp.concatenate(dus, axis=1))

        _seg_pipeline(i, nb, lambda st: nb - 1 - st, [u_io, dy_io], [du_io], compute)

    full = lambda a: pl.BlockSpec(a.shape, lambda i, nd=a.ndim: (0,) * nd)
    sh = jax.ShapeDtypeStruct
    outs = [sh((L, S5_WIDTH), F32), sh(bq_re.shape, F32), sh(bq_im.shape, F32), sh(cq_re.shape, F32), sh(cq_im.shape, F32),
            sh((8, S5_LANES), F32), sh((1, S5_WIDTH), F32)]
    fo = lambda s: pl.BlockSpec(s.shape, lambda i, nd=len(s.shape): (0,) * nd)
    st = pltpu.VMEM((S5_NT, TB, 128), F32)
    pw = pltpu.VMEM((S5_NT, seg, 8, 128), F32)
    io = pltpu.VMEM((2, seg, 8, S5_WIDTH), F32)
    sem = pltpu.SemaphoreType.DMA((2, 8))
    return _pcall(
        body, name="s5_scan_bwd", grid=(nb,),
        in_specs=[_ANY, _ANY, pl.BlockSpec((1, 8, S5_LANES), lambda i: (nb - 1 - i, 0, 0)),
                  full(bq_re), full(bq_im), full(cq_re), full(cq_im), full(lbar), full(dskip)],
        out_specs=[_ANY] + [fo(s) for s in outs[1:]],
        out_shape=outs,
        scratch_shapes=[st] * 4 + [pltpu.VMEM((8, S5_LANES), F32)] * 2 + [pw, pw, io, io, io, sem, sem, sem],
        compiler_params=_cparams(1))(proj, dy, ck, bq_re, bq_im, cq_re, cq_im, lbar, dskip)


N_HEAD = RW_WIDTH // HEAD
_NN = (((2,), (1,)), ((0,), (0,)))
_NT = (((2,), (2,)), ((0,), (0,)))
_TN = (((1,), (1,)), ((0,), (0,)))


def _hi_lo(x):
    h = x.astype(BF16)
    return h, (x - h.astype(F32)).astype(BF16)


def _mm_acc(a, b, dims, passes=3):
    dg = lambda p, q: lax.dot_general(p, q, dims, preferred_element_type=F32)
    if passes == 1:
        return dg(a.astype(BF16), b.astype(BF16))
    ah, al = _hi_lo(a)
    bh, bl = _hi_lo(b)
    return dg(ah, bh) + dg(ah, bl) + dg(al, bh)


def _cumsum_rows(x, transpose):
    h, n, _ = x.shape
    ti = lax.broadcasted_iota(jnp.int32, (h, n, n), 1)
    tj = lax.broadcasted_iota(jnp.int32, (h, n, n), 2)
    m = ((tj >= ti) if transpose else (tj <= ti)).astype(BF16)
    acc, rem = None, x
    for s in range(3):
        part = rem.astype(BF16)
        t = lax.dot_general(m, part, _NN, preferred_element_type=F32)
        acc = t if acc is None else acc + t
        if s < 2:
            rem = rem - part.astype(F32)
    return acc


def _slices(x, axis, sizes):
    out, off = [], 0
    for n in sizes:
        out.append(lax.slice_in_dim(x, off, off + n, axis=axis))
        off += n
    return tuple(out)


def _cat_op(axis, sizes, diff):
    plain = lambda *xs: jnp.concatenate(xs, axis=axis)
    if not diff:
        return plain
    f = jax.custom_vjp(plain)
    f.defvjp(lambda *xs: (plain(*xs), None), lambda _, d: _slices(d, axis, sizes))
    return f


def _split_op(axis, sizes, diff):
    plain = lambda x: _slices(x, axis, sizes)
    if not diff:
        return plain
    f = jax.custom_vjp(plain)
    f.defvjp(lambda x: (plain(x), None), lambda _, d: (jnp.concatenate(d, axis=axis),))
    return f


def _mm_ops(diff, passes):
    mm = lambda a, b, dims: _mm_acc(a, b, dims, passes)
    if not diff:
        return (lambda a, b: mm(a, b, _NN), lambda a, b: mm(a, b, _NT), lambda a, b: mm(a, b, _TN))

    @jax.custom_vjp
    def nn(a, b):
        return mm(a, b, _NN)

    nn.defvjp(lambda a, b: (mm(a, b, _NN), (a, b)), lambda r, d: (mm(d, r[1], _NT), mm(r[0], d, _TN)))

    @jax.custom_vjp
    def nt(a, b):
        return mm(a, b, _NT)

    nt.defvjp(lambda a, b: (mm(a, b, _NT), (a, b)), lambda r, d: (mm(d, r[1], _NN), mm(d, r[0], _TN)))

    @jax.custom_vjp
    def tn(a, b):
        return mm(a, b, _TN)

    tn.defvjp(lambda a, b: (mm(a, b, _TN), (a, b)), lambda r, d: (mm(r[1], d, _NT), mm(r[0], d, _NN)))
    return nn, nt, tn


def _cums_op(diff):
    if not diff:
        return lambda x: _cumsum_rows(x, False)

    @jax.custom_vjp
    def cums(x):
        return _cumsum_rows(x, False)

    cums.defvjp(lambda x: (_cumsum_rows(x, False), None), lambda _, d: (_cumsum_rows(d, True),))
    return cums


WKV_PASSES = (1, 1, 1, 1, 1)


WKV_SUB = 4
WKV_BLOCK = CHUNK * WKV_SUB


def _wkv_block(s0, r, w, k, v, a, b, diff):
    p_pair, p_val, p_solve, p_out, p_state = WKV_PASSES
    cums = _cums_op(diff)
    _, nt_pair, _ = _mm_ops(diff, p_pair)
    nn_val, _, _ = _mm_ops(diff, p_val)
    nn_solve, _, _ = _mm_ops(diff, p_solve)
    nn_out, nt_out, _ = _mm_ops(diff, p_out)
    nn_state, _, tn_state = _mm_ops(diff, p_state)
    h, d, n, sub = s0.shape[0], s0.shape[2], CHUNK, WKV_SUB
    hb = h * sub
    to_chunks = lambda t: _cat_op(0, (h,) * sub, diff)(*_split_op(1, (n,) * sub, diff)(t))
    r, w, k, v, a, b = (to_chunks(t) for t in (r, w, k, v, a, b))
    cat_rows2 = _cat_op(1, (n, n), diff)
    cat_lanes2 = _cat_op(2, (n, n), diff)
    split_rows2 = _split_op(1, (n, n), diff)
    split_lanes2 = _split_op(2, (n, n), diff)
    ti = lax.broadcasted_iota(jnp.int32, (hb, n, n), 1)
    tj = lax.broadcasted_iota(jnp.int32, (hb, n, n), 2)
    incl, strict = tj <= ti, tj < ti
    logw = jnp.log(w)
    cum = cums(logw)
    g_in, g_ex, g_inv = jnp.exp(cum), jnp.exp(cum - logw), jnp.exp(-cum)
    ae, re, bi, ki = a * g_ex, r * g_in, b * g_inv, k * g_inv
    top, bot = split_rows2(nt_pair(cat_rows2(ae, re), cat_rows2(bi, ki)))
    tab, tak = split_lanes2(top)
    qb, qk = split_lanes2(bot)
    tab, tak = jnp.where(strict, tab, 0.0), jnp.where(strict, tak, 0.0)
    qb, qk = jnp.where(incl, qb, 0.0), jnp.where(incl, qk, 0.0)
    tak_v, qk_v = split_rows2(nn_val(cat_rows2(tak, qk), v))
    x = cat_lanes2(ae, tak_v)
    npow = tab
    steps = max(1, (n - 1).bit_length())
    for i in range(steps):
        x = x + nn_solve(npow, x)
        if i + 1 < steps:
            npow = nn_solve(npow, npow)
    ae_m, uc = split_lanes2(x)
    qx = nn_out(qb, x)
    q_ae, q_uc = split_lanes2(qx)
    re_m = re + q_ae
    yc = q_uc + qk_v
    g_end = jnp.exp(jnp.sum(logw, axis=1, keepdims=True))
    bg, kg = bi * g_end, ki * g_end
    tm = tn_state(ae_m, bg)
    sc = tn_state(cat_rows2(uc, v), cat_rows2(bg, kg))
    per_chunk = _split_op(0, (h,) * sub, diff)
    re_m, yc, g_end, tm, sc = (per_chunk(t) for t in (re_m, yc, g_end, tm, sc))
    ys, s = [], s0
    for i in range(sub):
        ys.append(nt_out(re_m[i], s) + yc[i])
        s = s * g_end[i] + nn_state(s, tm[i]) + sc[i]
    return _cat_op(1, (n,) * sub, diff)(*ys), s


def _wkv_fwd(r, w, k, v, a, b, L):
    nc = L // WKV_BLOCK

    def body(r_ref, w_ref, k_ref, v_ref, a_ref, b_ref, y_ref, ck_ref, s_ref):
        c = pl.program_id(0)

        @pl.when(c == 0)
        def _():
            s_ref[...] = jnp.zeros(s_ref.shape, F32)

        s0 = s_ref[...]
        ck_ref[0] = s0
        y, s1 = _wkv_block(s0, r_ref[...], w_ref[...], k_ref[...], v_ref[...], a_ref[...], b_ref[...], False)
        y_ref[...] = y
        s_ref[...] = s1

    blk = pl.BlockSpec((N_HEAD, WKV_BLOCK, HEAD), lambda c: (0, c, 0))
    return _pcall(
        body, name="wkv_fwd", grid=(nc,), in_specs=[blk] * 6,
        out_specs=[blk, pl.BlockSpec((1, N_HEAD, HEAD, HEAD), lambda c: (c, 0, 0, 0))],
        out_shape=[jax.ShapeDtypeStruct((N_HEAD, L, HEAD), F32), jax.ShapeDtypeStruct((nc, N_HEAD, HEAD, HEAD), F32)],
        scratch_shapes=[pltpu.VMEM((N_HEAD, HEAD, HEAD), F32)],
        compiler_params=_cparams(1))(r, w, k, v, a, b)


def _wkv_bwd(r, w, k, v, a, b, dy, ck, L, deps=()):
    nc = L // WKV_BLOCK

    def body(r_ref, w_ref, k_ref, v_ref, a_ref, b_ref, dy_ref, ck_ref, *rest):
        dr_ref, dw_ref, dk_ref, dv_ref, da_ref, db_ref, ds_ref = rest[len(deps):]
        c = pl.program_id(0)

        @pl.when(c == 0)
        def _():
            ds_ref[...] = jnp.zeros(ds_ref.shape, F32)

        _, vjp = jax.vjp(lambda *t: _wkv_block(*t, True), ck_ref[0], r_ref[...], w_ref[...], k_ref[...], v_ref[...],
                         a_ref[...], b_ref[...])
        g = vjp((dy_ref[...], ds_ref[...]))
        ds_ref[...] = g[0]
        for o_ref, val in zip((dr_ref, dw_ref, dk_ref, dv_ref, da_ref, db_ref), g[1:]):
            o_ref[...] = val

    blk = pl.BlockSpec((N_HEAD, WKV_BLOCK, HEAD), lambda c: (0, nc - 1 - c, 0))
    sh = jax.ShapeDtypeStruct((N_HEAD, L, HEAD), F32)
    return _pcall(
        body, name="wkv_bwd", grid=(nc,),
        in_specs=[blk] * 7 + [pl.BlockSpec((1, N_HEAD, HEAD, HEAD), lambda c: (nc - 1 - c, 0, 0, 0))]
        + [pl.BlockSpec(d.shape, lambda c, nd=d.ndim: (0,) * nd) for d in deps],
        out_specs=[blk] * 6, out_shape=[sh] * 6,
        scratch_shapes=[pltpu.VMEM((N_HEAD, HEAD, HEAD), F32)],
        compiler_params=_cparams(1))(r, w, k, v, a, b, dy, ck, *deps)


TB = 256


def _bf(x):
    return x.astype(BF16)


def _inproj_fwd(x, norm_mix, w_in, L, deps=()):
    def fn(i, tv, cv):
        xn = _rms(tv[0], cv[0])
        return _dot(_bf(xn), cv[1]), xn

    return _tok_call("inproj_fwd", fn, L, TB, [(x, D_MODEL, 0)], [norm_mix, w_in], [(IN_COLS, F32), (D_MODEL, BF16)],
                     deps=deps)


def _s5_post_fn(glu_w, wtop, diff=True):
    mg = _mmc(glu_w, diff)
    mt = _mmc(wtop, diff) if wtop is not None else None

    def f(y, glu_b, e):
        z = _gelu(y)
        out = z * _sigmoid(mg(z) + glu_b + e)
        res = mt(out) if mt is not None else out
        return res, (z, out)

    return f


def _s5_post_fwd(y, glu_w, glu_b, L):
    def fn(i, tv, cv):
        out, _ = _s5_post_fn(cv[0], None, False)(tv[0], cv[1], 0.0)
        return (out,)

    return _tok_call("s5_post_fwd", fn, L, TB, [(y, S5_WIDTH, 0)], [glu_w, glu_b], [(S5_WIDTH, F32)])[0]


def _s5_post_bwd(y, dh1, glu_w, glu_b, wtop, L, deps=()):
    def fn(i, tv, cv):
        e0 = jnp.zeros((TB, S5_WIDTH), F32)
        _, vjp, (z, out) = jax.vjp(_s5_post_fn(cv[0], cv[2]), tv[0], cv[1], e0, has_aux=True)
        dy, db, de = vjp(tv[1])
        return dy, db, _dot_tn(_bf(z), _bf(de)), _dot_tn(_bf(out), _bf(tv[1]))

    return _tok_call("s5_post_bwd", fn, L, TB, [(y, S5_WIDTH, 0), (dh1, D_MODEL, 0)], [glu_w, glu_b, wtop],
                     [(S5_WIDTH, F32)], [(1, S5_WIDTH), (S5_WIDTH, S5_WIDTH), (S5_WIDTH, D_MODEL)], deps=deps)


RW_COLBLK = ((RW_WIDTH, 1), (RW_WIDTH, 2), (RW_WIDTH, 3), (128, 16), (128, 17))
RW_MU = ((0, 512), (512, 1024), (1024, 1536), (1536, 1664), (1664, 1792))


def _rw_pre_fn(w2pad, a2pad, g2, diff=True):
    m_w, m_a, m_g = _mmc(w2pad, diff), _mmc(a2pad, diff), _mmc(g2, diff)
    seg = _segsum(_head_indicator(RW_WIDTH), diff)

    def f(zr, zk, zv, zwa, zg, w0, a0, k_k, k_a, e_w, e_a):
        wl_t = jnp.tanh(zwa)
        wlin = w0 + m_w(wl_t) + e_w
        w = -_softplus(-wlin) - 0.5
        decay = jnp.exp(-jnp.exp(w))
        a = _sigmoid(a0 + m_a(zwa) + e_a)
        sg = _sigmoid(zg)
        g = m_g(sg)
        kk = zk * k_k
        kkn = kk / jnp.maximum(jnp.sqrt(seg(kk * kk)), L2_EPS)
        kf = zk * (1.0 + (a - 1.0) * k_a)
        return (zr, decay, kf, zv, -kkn, kkn * a, g), (wl_t, sg)

    return f


def _rw_shifted(i, tv, mu):
    sub = lax.broadcasted_iota(jnp.int32, (TB, 1), 0)
    zs, dif = [], []
    for n in range(5):
        z = tv[n]
        last = jnp.where(i == 0, 0.0, tv[5 + n][7:8, :])
        prev = jnp.where(sub == 0, last, pltpu.roll(z, 1, 0))
        m = mu[:, RW_MU[n][0]:RW_MU[n][1]]
        zs.append(z + (prev - z) * m)
        dif.append(prev - z)
    return zs, dif


def _rw_tok_in(proj):
    return [(proj, wd, cb) for wd, cb in RW_COLBLK] + [(proj, wd, cb, "prev") for wd, cb in RW_COLBLK]


def _rw_pre_fwd(proj, mu, w0, a0, k_k, k_a, w2pad, a2pad, g2, L):
    def fn(i, tv, cv):
        zs, _ = _rw_shifted(i, tv, cv[0])
        outs, _ = _rw_pre_fn(cv[5], cv[6], cv[7], False)(*zs, cv[1], cv[2], cv[3], cv[4], 0.0, 0.0)
        return outs

    return _tok_call("rw_pre_fwd", fn, L, TB, _rw_tok_in(proj), [mu, w0, a0, k_k, k_a, w2pad, a2pad, g2],
                     [("heads", F32)] * 6 + [(RW_WIDTH, F32)])


def _rw_pre_bwd(proj, cots, mu, w0, a0, k_k, k_a, w2pad, a2pad, g2, L):
    def fn(i, tv, cv):
        zs, dif = _rw_shifted(i, tv[:10], cv[0])
        dr1, dr2, dw, dk1, dk2, dv1, dv2, da, db, dg = tv[10:]
        e0 = jnp.zeros((TB, RW_WIDTH), F32)
        _, vjp, (wl_t, sg) = jax.vjp(_rw_pre_fn(cv[5], cv[6], cv[7]), *zs, cv[1], cv[2], cv[3], cv[4], e0, e0, has_aux=True)
        g = vjp((dr1 + dr2, dw, dk1 + dk2, dv1 + dv2, da, db, dg))
        dzs = jnp.concatenate(g[:5], axis=1)
        dmu = jnp.concatenate([jnp.sum(g[n] * dif[n], axis=0, keepdims=True) for n in range(5)], axis=1)
        lora = (_dot_tn(_bf(wl_t), _bf(g[9])), _dot_tn(_bf(zs[3]), _bf(g[10])), _dot_tn(_bf(sg), _bf(dg)))
        return (dzs, dmu, g[5], g[6], g[7], g[8]) + lora

    tok_in = _rw_tok_in(proj) + [((c,) if c.ndim == 3 else (c, RW_WIDTH, 0)) for c in cots]
    return _tok_call("rw_pre_bwd", fn, L, TB, tok_in, [mu, w0, a0, k_k, k_a, w2pad, a2pad, g2],
                     [(SHIFT_COLS, F32)], [(1, SHIFT_COLS)] + [(1, RW_WIDTH)] * 4 + [(128, RW_WIDTH)] * 3)


def _rw_post_fn(wbot, diff=True):
    seg = _segsum(_head_indicator(RW_WIDTH), diff)
    mb = _mmc(wbot, diff) if wbot is not None else None

    def f(y, r, kf, v, g, ln_w, ln_b, r_k):
        mean = seg(y) * (1.0 / HEAD)
        yc = y - mean
        var = seg(yc * yc) * (1.0 / HEAD)
        yn = yc * lax.rsqrt(var + GN_EPS) * ln_w + ln_b
        bonus = seg(r * kf * r_k) * v
        out = (yn + bonus) * g
        res = mb(out) if mb is not None else out
        return res, out

    return f


def _rw_post_fwd(y, r, kf, v, g, ln_w, ln_b, r_k, L):
    def fn(i, tv, cv):
        out, _ = _rw_post_fn(None, False)(*tv, *cv)
        return (out,)

    return _tok_call("rw_post_fwd", fn, L, TB, [(t,) for t in (y, r, kf, v)] + [(g, RW_WIDTH, 0)], [ln_w, ln_b, r_k],
                     [(RW_WIDTH, F32)])[0]


def _rw_post_bwd(y, r, kf, v, g, dh1, ln_w, ln_b, r_k, wbot, L):
    def fn(i, tv, cv):
        _, vjp, out = jax.vjp(_rw_post_fn(cv[3]), *tv[:5], cv[0], cv[1], cv[2], has_aux=True)
        gr = vjp(tv[5])
        return gr[0], gr[1], gr[2], gr[3], gr[4], gr[5], gr[6], gr[7], _dot_tn(_bf(out), _bf(tv[5]))

    return _tok_call("rw_post_bwd", fn, L, TB, [(t,) for t in (y, r, kf, v)] + [(g, RW_WIDTH, 0), (dh1, D_MODEL, 0)],
                     [ln_w, ln_b, r_k, wbot], [("heads", F32)] + [(RW_WIDTH, F32)] * 4,
                     [(1, RW_WIDTH)] * 3 + [(RW_WIDTH, D_MODEL)])


def _ffn_fn(w1, w3, w2, diff=True):
    m1, m3, m2 = _mmc(w1, diff), _mmc(w3, diff), _mmc(w2, diff)

    def f(h1, norm_ffn, e1, e3):
        hn = _rms(h1, norm_ffn)
        a1 = m1(hn) + e1
        a3 = m3(hn) + e3
        hm = a1 * _sigmoid(a1) * a3
        return h1 + m2(hm), (hn, hm)

    return f


TB_FFN = 256


def _mixffn_fwd(x, s5_out, rw_out, wtop, wbot, norm_ffn, w1, w3, w2, L):
    def fn(i, tv, cv):
        h1 = tv[0] + _dot(_bf(tv[1]), cv[0]) + _dot(_bf(tv[2]), cv[1])
        h2, _ = _ffn_fn(cv[3], cv[4], cv[5], False)(h1, cv[2], 0.0, 0.0)
        return h1, h2

    return _tok_call("mixffn_fwd", fn, L, TB_FFN, [(x, D_MODEL, 0), (s5_out, S5_WIDTH, 0), (rw_out, RW_WIDTH, 0)],
                     [wtop, wbot, norm_ffn, w1, w3, w2], [(D_MODEL, F32), (D_MODEL, F32)])


def _ffn_bwd(h1, dh2, norm_ffn, w1, w3, w2, L):
    def fn(i, tv, cv):
        e0 = jnp.zeros((TB_FFN, FFN_HIDDEN), F32)
        _, vjp, (hn, hm) = jax.vjp(_ffn_fn(cv[1], cv[2], cv[3]), tv[0], cv[0], e0, e0, has_aux=True)
        dh1, dn, d1, d3 = vjp(tv[1])
        return dh1, d1, d3, hm, hn, dn

    return _tok_call("ffn_bwd", fn, L, TB_FFN, [(h1, D_MODEL, 0), (dh2, D_MODEL, 0)], [norm_ffn, w1, w3, w2],
                     [(D_MODEL, F32), (FFN_HIDDEN, BF16), (FFN_HIDDEN, BF16), (FFN_HIDDEN, BF16), (D_MODEL, BF16)],
                     [(1, D_MODEL)])


def _ple_loss_fb(h2, p, target, norm_ple, final_norm, wg, wu, L):
    def fn(i, tv, cv):
        mgate, mup = _mmc(cv[2]), _mmc(cv[3], False)

        def f(h2_, norm_ple_, final_norm_, eg, eu):
            hn = _rms(h2_, norm_ple_)
            gate = _sigmoid(mgate(hn) + eg)
            h3 = h2_ + gate * (mup(tv[1]) + eu)
            out = _rms(h3, final_norm_)
            d = out - tv[2]
            return 0.5 * jnp.sum(jnp.mean(d * d, axis=-1, keepdims=True)), hn

        e0 = jnp.zeros((TB, D_MODEL), F32)
        loss, vjp, hn = jax.vjp(f, tv[0], cv[0], cv[1], e0, e0, has_aux=True)
        dh2, dnp, dfn, deg, deu = vjp(jnp.ones((), F32))
        return (dh2, dh2, jnp.full((8, 128), loss, F32), dnp, dfn,
                _dot_tn(_bf(hn), _bf(deg)), _dot_tn(_bf(tv[1]), _bf(deu)))

    return _tok_call("ple_loss_fb", fn, L, TB, [(h2, D_MODEL, 0), (p, PLE_DIM, 0), (target, D_MODEL, 0)],
                     [norm_ple, final_norm, wg, wu], [(D_MODEL, F32), (D_MODEL, BF16)],
                     [(8, 128), (1, D_MODEL), (1, D_MODEL), (D_MODEL, D_MODEL), (PLE_DIM, D_MODEL)])


def _inproj_bwd(x, dh1, du, dzs, norm_mix, mu, w_u, w_z, L):
    nb = L // TB

    def fn(i, tv, cv):
        sub = lax.broadcasted_iota(jnp.int32, (TB, 1), 0)
        m = cv[1]
        b = tv[3] * m
        nxt = jnp.where(i == nb - 1, 0.0, tv[4][0:1, :] * m)
        dz = tv[3] * (1.0 - m) + jnp.where(sub == TB - 1, nxt, pltpu.roll(b, TB - 1, 0))
        dub, dzb = _bf(tv[2]), _bf(dz)
        dxn = _dot_nt(dub, cv[2]) + _dot_nt(dzb, cv[3])
        _, vjp = jax.vjp(_rms, tv[0], cv[0])
        dx, dn = vjp(dxn)
        return tv[1] + dx, jnp.concatenate([dub, dzb], axis=1), dn

    return _tok_call("inproj_bwd", fn, L, TB,
                     [(x, D_MODEL, 0), (dh1, D_MODEL, 0), (du, S5_WIDTH, 0), (dzs, SHIFT_COLS, 0), (dzs, SHIFT_COLS, 0, "next")],
                     [norm_mix, mu, w_u, w_z], [(D_MODEL, F32), (IN_COLS, BF16)], [(1, D_MODEL)])


def _eye8(dt):
    return jnp.eye(8, dtype=dt)


def _quarter_b(bb):
    return jnp.einsum("hg,qgcp->qhcgp", _eye8(bb.dtype), bb.reshape(S5_Q, 8, S5_GROUP, S5_STATE)).reshape(S5_Q, S5_QL, S5_QS)


def _unquarter_b(d):
    return jnp.einsum("qhcgp,hg->qgcp", d.reshape(S5_Q, 8, S5_GROUP, 8, S5_STATE), _eye8(d.dtype)).reshape(
        S5_GROUPS, S5_GROUP, S5_STATE)


def _quarter_c(c):
    return jnp.einsum("gh,qgcp->qgphc", _eye8(c.dtype), c.reshape(S5_Q, 8, S5_GROUP, S5_STATE)).reshape(S5_Q, S5_QS, S5_QL)


def _unquarter_c(d):
    return jnp.einsum("qgphc,gh->qgcp", d.reshape(S5_Q, 8, S5_STATE, 8, S5_GROUP), _eye8(d.dtype)).reshape(
        S5_GROUPS, S5_GROUP, S5_STATE)


def _local_step(x, p, target, W, late_weights=None, grads_ready=None, first_dep=None):
    L = x.shape[0]
    r2 = lambda v: v.reshape(1, -1)
    w_in = W["w_in"]
    w2pad = jnp.pad(W["rw_w2"], ((0, 64), (0, 0)))
    a2pad = jnp.pad(W["rw_a2"], ((64, 0), (0, 0)))
    mu = r2(W["rw_shift_mu"])
    rw_vec = [r2(W[n]) for n in ("rw_w0", "rw_a0", "rw_k_k", "rw_k_a")]
    ln_w, ln_b, r_k = r2(W["rw_ln_w"]), r2(W["rw_ln_b"]), r2(W["rw_r_k"])

    lam_re, lam_im = W["s5_lam_re"], W["s5_lam_im"]
    log_step = W["s5_log_step"].reshape(S5_GROUPS, 1)
    bt_re, bt_im = W["s5_b_re"].transpose(0, 2, 1), W["s5_b_im"].transpose(0, 2, 1)
    lb_re, lb_im, bb_re, bb_im = _s5_param_fwd(lam_re, lam_im, log_step, bt_re, bt_im)
    bq_re, bq_im = _quarter_b(bb_re).astype(BF16), _quarter_b(bb_im).astype(BF16)
    cq_re, cq_im = _quarter_c(W["s5_c_re"]).astype(BF16), _quarter_c(W["s5_c_im"]).astype(BF16)
    lbar = jnp.concatenate([lb_re.reshape(1, -1), lb_im.reshape(1, -1), jnp.zeros((6, S5_LANES), F32)], axis=0)
    dskip = r2(W["s5_d"])
    glu_b = r2(W["s5_glu_b"])
    norm_mix, norm_ffn, norm_ple, final_norm = (r2(W[n]) for n in ("norm_mix", "norm_ffn", "norm_ple", "final_norm"))

    proj, xn = _inproj_fwd(x, norm_mix, w_in, L, () if first_dep is None else (first_dep,))
    y_s5, ck5 = _s5_scan_fwd(proj, bq_re, bq_im, cq_re, cq_im, lbar, dskip, L, TB)
    s5_out = _s5_post_fwd(y_s5, W["s5_glu_w"], glu_b, L)
    r, wd, kf, v, a_s, b_s, g = _rw_pre_fwd(proj, mu, *rw_vec, w2pad, a2pad, W["rw_g2"], L)
    scan_in = (r, wd, kf, v, a_s, b_s)
    y_wkv, ckw = _wkv_fwd(*scan_in, L)
    rw_out = _rw_post_fwd(y_wkv, r, kf, v, g, ln_w, ln_b, r_k, L)
    if late_weights is not None:
        W = dict(W, **late_weights(rw_out))
    wtop, wbot = W["w_out"][:S5_WIDTH], W["w_out"][S5_WIDTH:]
    h1, h2 = _mixffn_fwd(x, s5_out, rw_out, wtop, wbot, norm_ffn, W["ffn_w1"], W["ffn_w3"], W["ffn_w2"], L)

    G = {}
    dh2, dh2_bf, loss_acc, G["norm_ple"], G["final_norm"], G["ple_gate_w"], G["ple_up_w"] = _ple_loss_fb(
        h2, p, target, norm_ple, final_norm, W["ple_gate_w"], W["ple_up_w"], L)
    dh1, da1, da3, hm, hn_ffn, G["norm_ffn"] = _ffn_bwd(h1, dh2, norm_ffn, W["ffn_w1"], W["ffn_w3"], W["ffn_w2"], L)
    Gb = {}
    G["ffn_w1"], Gb["ffn_w1"] = _mm_tn("dw_ffn_w1", hn_ffn, da1)
    G["ffn_w3"], Gb["ffn_w3"] = _mm_tn("dw_ffn_w3", hn_ffn, da3)
    G["ffn_w2"], Gb["ffn_w2"] = _mm_tn("dw_ffn_w2", hm, dh2_bf)
    dep_a = grads_ready(0, G, Gb) if grads_ready is not None else None
    dy_s5, G["s5_glu_b"], G["s5_glu_w"], d_wtop = _s5_post_bwd(y_s5, dh1, W["s5_glu_w"], glu_b, wtop, L,
                                                               () if dep_a is None else (dep_a,))
    dy_wkv, dr2, dk2, dv2, dg, G["rw_ln_w"], G["rw_ln_b"], G["rw_r_k"], d_wbot = _rw_post_bwd(
        y_wkv, r, kf, v, g, dh1, ln_w, ln_b, r_k, wbot, L)
    G["w_out"] = jnp.concatenate([d_wtop, d_wbot], axis=0)
    dep = grads_ready(1, G, Gb) if grads_ready is not None else None
    dr1, dwd, dk1, dv1, da_s, db_s = _wkv_bwd(*scan_in, dy_wkv, ckw, L, () if dep is None else (dep,))
    (dzs, G["rw_shift_mu"], G["rw_w0"], G["rw_a0"], G["rw_k_k"], G["rw_k_a"], d_w2pad, d_a2pad, G["rw_g2"]) = _rw_pre_bwd(
        proj, (dr1, dr2, dwd, dk1, dk2, dv1, dv2, da_s, db_s, dg), mu, *rw_vec, w2pad, a2pad, W["rw_g2"], L)
    G["rw_w2"], G["rw_a2"] = d_w2pad[:64], d_a2pad[64:]
    du, dbq_re, dbq_im, dcq_re, dcq_im, dlbar, G["s5_d"] = _s5_scan_bwd(
        proj, dy_s5, ck5, bq_re, bq_im, cq_re, cq_im, lbar, dskip, L, TB)
    G["s5_c_re"], G["s5_c_im"] = _unquarter_c(dcq_re), _unquarter_c(dcq_im)
    d_lam_re, d_lam_im, d_ls, d_bt_re, d_bt_im = _s5_param_bwd(
        lam_re, lam_im, log_step, bt_re, bt_im, dlbar[0].reshape(S5_GROUPS, S5_STATE), dlbar[1].reshape(S5_GROUPS, S5_STATE),
        _unquarter_b(dbq_re), _unquarter_b(dbq_im))
    G["s5_lam_re"], G["s5_lam_im"], G["s5_log_step"] = d_lam_re, d_lam_im, d_ls.reshape(S5_GROUPS)
    G["s5_b_re"], G["s5_b_im"] = d_bt_re.transpose(0, 2, 1), d_bt_im.transpose(0, 2, 1)
    dx, dproj, G["norm_mix"] = _inproj_bwd(x, dh1, du, dzs, norm_mix, mu, w_in[:, :S5_WIDTH], w_in[:, S5_WIDTH:], L)
    G["w_in"], Gb["w_in"] = _mm_tn("dw_in", xn, dproj)
    return loss_acc[0, 0], dx, G, Gb


MESH_AXES = ("x", "y", "c")


def _all_gather(name, shards):
    nt = len(shards)

    def body(*refs):
        x_refs, out_refs = refs[:nt], refs[nt:2 * nt]
        send_sems, recv_sems, local_sems = refs[2 * nt:]
        x, y, c = lax.axis_index("x"), lax.axis_index("y"), lax.axis_index("c")
        me, sibling = (x, y, c), (x, y, 1 - c)
        chips = [(1 - x, y), (x, 1 - y), (1 - x, 1 - y)]

        def rows(t, px, py, pc):
            m_per = shards[t].shape[0]
            return out_refs[t].at[pl.ds((4 * px + 2 * py + pc) * m_per, m_per), :]

        def copy(t, k, block, to, src=None):
            return pltpu.make_async_remote_copy(
                src_ref=rows(t, *block) if src is None else src, dst_ref=rows(t, *block),
                send_sem=send_sems.at[7 * t + k], recv_sem=recv_sems.at[7 * t + k],
                device_id=to, device_id_type=pl.DeviceIdType.MESH)

        mine = [pltpu.make_async_copy(x_refs[t], rows(t, *me), local_sems.at[t]) for t in range(nt)]
        for cp in mine:
            cp.start()
        first = []
        for t in range(nt):
            first.append(copy(t, 0, me, sibling, src=x_refs[t]))
            first += [copy(t, 1 + j, me, (*chip, c), src=x_refs[t]) for j, chip in enumerate(chips)]
        for cp in first:
            cp.start()
        passed = []
        for t in range(nt):
            for j, chip in enumerate(chips):
                copy(t, 1 + j, (*chip, c), me).wait_recv()
                fwd = copy(t, 4 + j, (*chip, c), sibling)
                fwd.start()
                passed.append(fwd)
        for t in range(nt):
            copy(t, 0, sibling, me).wait_recv()
            for j, chip in enumerate(chips):
                copy(t, 4 + j, (*chip, 1 - c), me).wait_recv()
        for cp in first + passed:
            cp.wait_send()
        for cp in mine:
            cp.wait()

    return _pcall(body, name=name,
                  out_shape=[jax.ShapeDtypeStruct((N_DEV * a.shape[0], a.shape[1]), a.dtype) for a in shards],
                  in_specs=[_ANY] * nt, out_specs=[_ANY] * nt,
                  scratch_shapes=[pltpu.SemaphoreType.DMA((7 * nt,)), pltpu.SemaphoreType.DMA((7 * nt,)),
                                  pltpu.SemaphoreType.DMA((nt,))])(*shards)


_HBM = pl.BlockSpec(memory_space=pltpu.HBM)
_SEM = pl.BlockSpec(memory_space=pltpu.SEMAPHORE)
_EFFECT = pltpu.SideEffectType.DATAFLOW_SIDE_EFFECTING


def _peer_of(k):
    x, y, c = lax.axis_index("x"), lax.axis_index("y"), lax.axis_index("c")
    px, py, pc = x ^ ((k >> 2) & 1), y ^ ((k >> 1) & 1), c ^ (k & 1)
    return (px, py, pc), 4 * px + 2 * py + pc, 4 * x + 2 * y + c


def _direct_copy(t, k, src_refs, land_refs, send_sems, recv_sems, rows_of, gather):
    dev, peer, me = _peer_of(k)
    m = rows_of[t]
    src = src_refs[t] if gather else src_refs[t].at[pl.ds(peer * m, m), :]
    return pltpu.make_async_remote_copy(
        src_ref=src, dst_ref=land_refs[t].at[pl.ds(me * m, m), :],
        send_sem=send_sems.at[7 * t + k - 1], recv_sem=recv_sems.at[7 * t + k - 1],
        device_id=dev, device_id_type=pl.DeviceIdType.MESH)


def _direct_landing(t, k, src_refs, land_refs, send_sems, recv_sems, rows_of, gather):
    dev, peer, me = _peer_of(k)
    m = rows_of[t]
    src = src_refs[t] if gather else src_refs[t].at[pl.ds(me * m, m), :]
    return pltpu.make_async_remote_copy(
        src_ref=src, dst_ref=land_refs[t].at[pl.ds(peer * m, m), :],
        send_sem=send_sems.at[7 * t + k - 1], recv_sem=recv_sems.at[7 * t + k - 1],
        device_id=dev, device_id_type=pl.DeviceIdType.MESH)


def _direct_start(name, srcs, gather, dep=None):
    nt = len(srcs)
    rows_of = [a.shape[0] if gather else a.shape[0] // N_DEV for a in srcs]
    lands = [pltpu.with_memory_space_constraint(lax.empty((N_DEV * m, a.shape[1]), a.dtype), pltpu.HBM)
             for a, m in zip(srcs, rows_of)]

    n_dep = 0 if dep is None else 1

    def body(*refs):
        src_refs, land_refs = refs[:nt], refs[nt:2 * nt]
        send_sems, recv_sems = refs[2 * nt + n_dep], refs[2 * nt + n_dep + 1]
        token = refs[-1]
        for t in range(nt):
            for k in range(1, N_DEV):
                _direct_copy(t, k, src_refs, land_refs, send_sems, recv_sems, rows_of, gather).start()
        token[...] = jnp.zeros(token.shape, F32)

    out = _pcall(
        body, name=name,
        out_shape=(pltpu.SemaphoreType.DMA((7 * nt,)), pltpu.SemaphoreType.DMA((7 * nt,)),
                   *[pltpu.HBM(a.shape, a.dtype) for a in srcs], *[pltpu.HBM(a.shape, a.dtype) for a in lands],
                   jax.ShapeDtypeStruct((8, 128), F32)),
        in_specs=(_HBM,) * (2 * nt) + (pl.BlockSpec(memory_space=pl.ANY),) * n_dep,
        out_specs=(_SEM, _SEM) + (_HBM,) * (2 * nt) + (pl.BlockSpec(memory_space=pltpu.VMEM),),
        input_output_aliases={i: 2 + i for i in range(2 * nt)},
        compiler_params=pltpu.CompilerParams(has_side_effects=_EFFECT),
    )(*[pltpu.with_memory_space_constraint(a, pltpu.HBM) for a in srcs], *lands, *(() if dep is None else (dep,)))
    return (out[0], out[1], list(out[2:2 + nt]), list(out[2 + nt:2 + 2 * nt]), rows_of, gather), out[-1]


def _direct_wait(name, handle, after):
    send_sems, recv_sems, srcs, lands, rows_of, gather = handle
    nt = len(srcs)
    after = list(after) if isinstance(after, (list, tuple)) else [after]

    def body(*refs):
        src_refs, land_refs = refs[:nt], refs[nt:2 * nt]
        s_sems, r_sems = refs[2 * nt], refs[2 * nt + 1]
        for t in range(nt):
            for k in range(1, N_DEV):
                _direct_copy(t, k, src_refs, land_refs, s_sems, r_sems, rows_of, gather).wait_send()
                _direct_landing(t, k, src_refs, land_refs, s_sems, r_sems, rows_of, gather).wait_recv()

    out = _pcall(
        body, name=name,
        out_shape=tuple(pltpu.HBM(a.shape, a.dtype) for a in srcs) + tuple(pltpu.HBM(a.shape, a.dtype) for a in lands),
        in_specs=(_HBM,) * (2 * nt) + (_SEM, _SEM) + (pl.BlockSpec(memory_space=pl.ANY),) * len(after),
        out_specs=(_HBM,) * (2 * nt),
        input_output_aliases={i: i for i in range(2 * nt)},
        compiler_params=pltpu.CompilerParams(has_side_effects=_EFFECT),
    )(*srcs, *lands, send_sems, recv_sems, *after)
    return list(out[:nt]), list(out[nt:])


def _adamw_sharded(name, own, parts, w, m, v, rb, deps=()):
    R, N = own.shape

    def body(o_ref, p_ref, w_ref, m_ref, v_ref, *rest):
        g_ref, d_ref, nm_ref, nv_ref = rest[len(deps):]
        me = 4 * lax.axis_index("x") + 2 * lax.axis_index("y") + lax.axis_index("c")
        g = o_ref[...]
        for k in range(1, N_DEV):
            g = g + p_ref[me ^ k].astype(F32)
        nm = ADAM_B1 * m_ref[...] + (1.0 - ADAM_B1) * g
        nv = ADAM_B2 * v_ref[...] + (1.0 - ADAM_B2) * (g * g)
        m_hat = nm / (1.0 - ADAM_B1 ** ADAM_STEP)
        v_hat = nv / (1.0 - ADAM_B2 ** ADAM_STEP)
        g_ref[...] = g
        d_ref[...] = -ADAM_LR * (m_hat / (jnp.sqrt(v_hat) + ADAM_EPS) + ADAM_WD * w_ref[...])
        nm_ref[...] = nm
        nv_ref[...] = nv

    blk = pl.BlockSpec((rb, N), lambda i: (i, 0))
    sh = jax.ShapeDtypeStruct((R, N), F32)
    return _pcall(body, name=name, grid=(R // rb,),
                  in_specs=[blk, pl.BlockSpec((N_DEV, rb, N), lambda i: (0, i, 0)), blk, blk, blk]
                  + [pl.BlockSpec(d.shape, lambda i, nd=d.ndim: (0,) * nd) for d in deps],
                  out_specs=[blk] * 4, out_shape=[sh] * 4, compiler_params=_cparams(1))(own, parts, w, m, v, *deps)


LOSS_SLOT = "loss_partials"
SMALL_CLASSES = (
    (("s5_b_re", 32, 1024), ("s5_b_im", 32, 1024),
     ("norm_mix", 1, 1024), ("norm_ffn", 1, 1024), ("norm_ple", 1, 1024), ("final_norm", 1, 1024)),
    (("s5_d", 1, 512), ("s5_glu_b", 1, 512), ("rw_w0", 1, 512), ("rw_a0", 1, 512), ("rw_k_k", 1, 512), ("rw_k_a", 1, 512),
     ("rw_ln_w", 1, 512), ("rw_ln_b", 1, 512), ("rw_r_k", 1, 512)),
    (("rw_shift_mu", 1, 1792),),
    (("s5_lam_re", 32, 64), ("s5_lam_im", 32, 64), ("s5_c_re", 512, 64), ("s5_c_im", 512, 64)),
    (("s5_log_step", 1, 32), (LOSS_SLOT, 1, 32)),
)


def _class_rows(cls):
    return -(-sum(r for _, r, _ in cls) // 8) * 8


def _stack_class(cls, arrs):
    a = jnp.concatenate(arrs, axis=0) if len(arrs) > 1 else arrs[0]
    pad = _class_rows(cls) - a.shape[0]
    return jnp.pad(a, ((0, pad), (0, 0))) if pad else a


def _adamw_small(grads, w, m, v):
    names = [n for cls in SMALL_CLASSES for n, _, _ in cls]
    n_cls, n_par = len(SMALL_CLASSES), len(names)

    def body(*refs):
        g_refs = refs[:n_cls]
        w_refs, m_refs, v_refs = (refs[n_cls + i * n_par:n_cls + (i + 1) * n_par] for i in range(3))
        o_refs = refs[n_cls + 3 * n_par:]
        p = 0
        for cls, g_ref in zip(SMALL_CLASSES, g_refs):
            rc = _class_rows(cls)
            tot = g_ref[0:rc, :]
            for s_ in range(1, N_DEV):
                tot = tot + g_ref[s_ * rc:(s_ + 1) * rc, :]
            off = 0
            for _, r, _ in cls:
                g = tot[off:off + r, :]
                off += r
                nm = ADAM_B1 * m_refs[p][...] + (1.0 - ADAM_B1) * g
                nv = ADAM_B2 * v_refs[p][...] + (1.0 - ADAM_B2) * (g * g)
                m_hat = nm / (1.0 - ADAM_B1 ** ADAM_STEP)
                v_hat = nv / (1.0 - ADAM_B2 ** ADAM_STEP)
                o_refs[4 * p][...] = g
                o_refs[4 * p + 1][...] = -ADAM_LR * (m_hat / (jnp.sqrt(v_hat) + ADAM_EPS) + ADAM_WD * w_refs[p][...])
                o_refs[4 * p + 2][...] = nm
                o_refs[4 * p + 3][...] = nv
                p += 1

    shapes = [(r, c) for cls in SMALL_CLASSES for _, r, c in cls]
    out = _pcall(body, name="adamw_replicated",
                 out_shape=[jax.ShapeDtypeStruct(sh, F32) for sh in shapes for _ in range(4)],
                 compiler_params=pltpu.CompilerParams(vmem_limit_bytes=VMEM_LIMIT))(*grads, *w, *m, *v)
    return {n: out[4 * i:4 * i + 4] for i, n in enumerate(names)}


EARLY = (("w_in", True),)
LATE = (("ffn_w1", True), ("ffn_w3", True), ("ffn_w2", False), ("ple_gate_w", False), ("w_out", False))
GRAD_STAGES = (LATE[:4], LATE[4:])
MISC = (("s5_glu_w", False), ("rw_w2", True), ("rw_a2", True), ("rw_g2", True), ("ple_up_w", True))
SHARDED_NAMES = tuple(n for n, _ in EARLY + LATE + MISC)
PACK_COLS = 1024
WEIGHT_NAMES = ("norm_mix", "w_in", "s5_lam_re", "s5_lam_im", "s5_log_step", "s5_b_re", "s5_b_im", "s5_c_re", "s5_c_im", "s5_d",
                "s5_glu_w", "s5_glu_b", "rw_shift_mu", "rw_w0", "rw_w2", "rw_a0", "rw_a2", "rw_g2", "rw_k_k", "rw_k_a", "rw_r_k",
                "rw_ln_w", "rw_ln_b", "w_out", "norm_ffn", "ffn_w1", "ffn_w3", "ffn_w2", "norm_ple", "ple_gate_w", "ple_up_w",
                "final_norm")
SMALL_NAMES = tuple(n for n in WEIGHT_NAMES if n not in SHARDED_NAMES)
ARG_NAMES = ("x", "p") + WEIGHT_NAMES + ("loss_target",) + tuple("m_" + n for n in WEIGHT_NAMES) + tuple("v_" + n for n in WEIGHT_NAMES)


def _travel(a, tr):
    return a.T if tr else a


def _pack_misc(blocks):
    lead = blocks[0].shape[:-2]
    return jnp.concatenate([b.reshape(lead + (-1, PACK_COLS)) for b in blocks], axis=len(lead))


def _unpack_misc(packed, shapes):
    lead = packed.shape[:-2]
    out, off = [], 0
    for r, c in shapes:
        n = r * c // PACK_COLS
        out.append(lax.slice_in_dim(packed, off, off + n, axis=len(lead)).reshape(lead + (r, c)))
        off += n
    return out


def _kernel_impl(ins):
    x, p, target = ins["x"][0], ins["p"][0, 0], ins["loss_target"][0]
    me = 4 * lax.axis_index("x") + 2 * lax.axis_index("y") + lax.axis_index("c")
    small = {n: (ins[n] if n == "final_norm" else ins[n][0]) for n in SMALL_NAMES}
    trav = lambda pre, n, tr: _travel(ins[pre + n][0], tr)
    misc_shapes = [trav("", n, tr).shape for n, tr in MISC]

    early = _all_gather("ag_early", [trav("", n, tr).astype(BF16) for n, tr in EARLY]
                        + [_pack_misc([trav("", n, tr).astype(BF16) for n, tr in MISC])])
    late_handle, late_token = _direct_start("ag_late_start", [trav("", n, tr).astype(BF16) for n, tr in LATE], True, early[-1])
    W = dict(small)
    for (n, tr), g in zip(EARLY, early):
        W[n] = _travel(g, tr)
    for (n, tr), g in zip(MISC, _unpack_misc(early[-1].reshape(N_DEV, -1, PACK_COLS), misc_shapes)):
        W[n] = _travel(g.reshape(-1, g.shape[-1]), tr)

    def late_weights(after):
        shards, lands = _direct_wait("ag_late_wait", late_handle, after)
        full = [lax.dynamic_update_slice_in_dim(ld, sh, me * sh.shape[0], axis=0) for ld, sh in zip(lands, shards)]
        return {n: _travel(g, tr) for (n, tr), g in zip(LATE, full)}

    gt = lambda G, n, tr: _travel(G[n], tr)
    started = {}

    def own_block(G, n, tr):
        rows = ins[n].shape[2 if tr else 1]
        return _travel(lax.dynamic_slice_in_dim(G[n], me * rows, rows, axis=1 if tr else 0), tr)

    def wire(G, Gb, n, tr):
        return _travel(Gb[n], tr) if n in Gb else _travel(G[n], tr).astype(BF16)

    def grads_ready(stage, G, Gb):
        owns = [own_block(G, n, tr) for n, tr in GRAD_STAGES[stage]]
        started[stage] = (owns, *_direct_start("grad_late_start%d" % stage, [wire(G, Gb, n, tr) for n, tr in GRAD_STAGES[stage]], False))
        return started[stage][2]

    loss_part, dx, G, Gb = _local_step(x, p, target, W, late_weights, grads_ready, late_token)

    misc_g = _pack_misc([gt(G, n, tr).reshape((N_DEV,) + shp) for (n, tr), shp in zip(MISC, misc_shapes)])
    misc_full = misc_g.reshape(-1, PACK_COLS)
    early_own = [own_block(G, n, tr) for n, tr in EARLY] + [lax.dynamic_slice_in_dim(misc_full, me * misc_g.shape[1], misc_g.shape[1], axis=0)]
    early_handle, early_token = _direct_start("grad_early_start", [wire(G, Gb, n, tr) for n, tr in EARLY] + [misc_full.astype(BF16)], False)
    view2 = lambda a, r, c: a.reshape(r, c)
    G[LOSS_SLOT] = jnp.full((1, 32), loss_part, F32)
    small_own = [_stack_class(cls, [view2(G[n], r, c) for n, r, c in cls]) for cls in SMALL_CLASSES]
    small_handle, small_token = _direct_start("grad_small_start", small_own, True)
    late_own, late_land = [], []
    for stage in range(len(GRAD_STAGES)):
        owns, handle, _ = started[stage]
        _, land = _direct_wait("grad_late_wait%d" % stage, handle, small_token)
        late_own += owns
        late_land += land

    outs = {}

    def emit(names_shapes, res):
        for tag, val in zip(("grad_", "delta_", "new_m_", "new_v_"), res):
            for n, v in names_shapes(val):
                outs[tag + n] = v

    def sharded_update(n, tr, own, land, deps=()):
        rows = own.shape[0]
        res = _adamw_sharded("adamw_" + n, own, land.reshape(N_DEV, rows, land.shape[1]),
                             trav("", n, tr), trav("m_", n, tr), trav("v_", n, tr), _pick_rows(rows), deps)
        emit(lambda val: [(n, _travel(val, tr).reshape(ins[n].shape))], res)
        return list(res)

    for (n, tr), own, land in zip(LATE, late_own, late_land):
        sharded_update(n, tr, own, land, (early_token,))
    _, early_land = _direct_wait("grad_early_wait", early_handle, list(outs.values()))
    for (n, tr), own, land in zip(EARLY, early_own[:-1], early_land[:-1]):
        sharded_update(n, tr, own, land)
    pm = lambda pre: _pack_misc([trav(pre, n, tr) for n, tr in MISC])
    rows = early_own[-1].shape[0]
    res = _adamw_sharded("adamw_misc", early_own[-1], early_land[-1].reshape(N_DEV, rows, PACK_COLS), pm(""), pm("m_"), pm("v_"), rows)
    emit(lambda val: [(n, _travel(b, tr).reshape(ins[n].shape)) for (n, tr), b in zip(MISC, _unpack_misc(val, misc_shapes))], res)
    small_src, small_land = _direct_wait("grad_small_wait", small_handle, res[0])
    small_all = [lax.dynamic_update_slice_in_dim(ld, sr, me * sr.shape[0], axis=0) for ld, sr in zip(small_land, small_src)]
    flat_small = [(n, r, c) for cls in SMALL_CLASSES for n, r, c in cls]
    ins = dict(ins, **{pre + LOSS_SLOT: jnp.zeros((1, 32), F32) for pre in ("", "m_", "v_")})
    res = _adamw_small(small_all, *[[view2(ins[pre + n], r, c) for n, r, c in flat_small] for pre in ("", "m_", "v_")])
    loss = res.pop(LOSS_SLOT)[0][0, 0]
    for n, _, _ in flat_small[:-1]:
        for tag, val in zip(("grad_", "delta_", "new_m_", "new_v_"), res[n]):
            outs[tag + n] = val.reshape(ins[n].shape)
    res = [loss, dx[None]]
    for tag in ("grad_", "delta_", "new_m_", "new_v_"):
        res += [outs[tag + n] for n in WEIGHT_NAMES]
    return tuple(res)


def _pick_rows(r):
    best = 8
    for b in range(8, 257, 8):
        if r % b == 0:
            best = b
    return best


def kernel(x, p, norm_mix, w_in, s5_lam_re, s5_lam_im, s5_log_step, s5_b_re, s5_b_im, s5_c_re, s5_c_im, s5_d, s5_glu_w, s5_glu_b, rw_shift_mu, rw_w0, rw_w2, rw_a0, rw_a2, rw_g2, rw_k_k, rw_k_a, rw_r_k, rw_ln_w, rw_ln_b, w_out, norm_ffn, ffn_w1, ffn_w3, ffn_w2, norm_ple, ple_gate_w, ple_up_w, final_norm, loss_target, m_norm_mix, m_w_in, m_s5_lam_re, m_s5_lam_im, m_s5_log_step, m_s5_b_re, m_s5_b_im, m_s5_c_re, m_s5_c_im, m_s5_d, m_s5_glu_w, m_s5_glu_b, m_rw_shift_mu, m_rw_w0, m_rw_w2, m_rw_a0, m_rw_a2, m_rw_g2, m_rw_k_k, m_rw_k_a, m_rw_r_k, m_rw_ln_w, m_rw_ln_b, m_w_out, m_norm_ffn, m_ffn_w1, m_ffn_w3, m_ffn_w2, m_norm_ple, m_ple_gate_w, m_ple_up_w, m_final_norm, v_norm_mix, v_w_in, v_s5_lam_re, v_s5_lam_im, v_s5_log_step, v_s5_b_re, v_s5_b_im, v_s5_c_re, v_s5_c_im, v_s5_d, v_s5_glu_w, v_s5_glu_b, v_rw_shift_mu, v_rw_w0, v_rw_w2, v_rw_a0, v_rw_a2, v_rw_g2, v_rw_k_k, v_rw_k_a, v_rw_r_k, v_rw_ln_w, v_rw_ln_b, v_w_out, v_norm_ffn, v_ffn_w1, v_ffn_w3, v_ffn_w2, v_norm_ple, v_ple_gate_w, v_ple_up_w, v_final_norm):
    return _kernel_impl(dict(zip(ARG_NAMES, (x, p, norm_mix, w_in, s5_lam_re, s5_lam_im, s5_log_step, s5_b_re, s5_b_im, s5_c_re, s5_c_im, s5_d, s5_glu_w, s5_glu_b, rw_shift_mu, rw_w0, rw_w2, rw_a0, rw_a2, rw_g2, rw_k_k, rw_k_a, rw_r_k, rw_ln_w, rw_ln_b, w_out, norm_ffn, ffn_w1, ffn_w3, ffn_w2, norm_ple, ple_gate_w, ple_up_w, final_norm, loss_target, m_norm_mix, m_w_in, m_s5_lam_re, m_s5_lam_im, m_s5_log_step, m_s5_b_re, m_s5_b_im, m_s5_c_re, m_s5_c_im, m_s5_d, m_s5_glu_w, m_s5_glu_b, m_rw_shift_mu, m_rw_w0, m_rw_w2, m_rw_a0, m_rw_a2, m_rw_g2, m_rw_k_k, m_rw_k_a, m_rw_r_k, m_rw_ln_w, m_rw_ln_b, m_w_out, m_norm_ffn, m_ffn_w1, m_ffn_w3, m_ffn_w2, m_norm_ple, m_ple_gate_w, m_ple_up_w, m_final_norm, v_norm_mix, v_w_in, v_s5_lam_re, v_s5_lam_im, v_s5_log_step, v_s5_b_re, v_s5_b_im, v_s5_c_re, v_s5_c_im, v_s5_d, v_s5_glu_w, v_s5_glu_b, v_rw_shift_mu, v_rw_w0, v_rw_w2, v_rw_a0, v_rw_a2, v_rw_g2, v_rw_k_k, v_rw_k_a, v_rw_r_k, v_rw_ln_w, v_rw_ln_b, v_w_out, v_norm_ffn, v_ffn_w1, v_ffn_w3, v_ffn_w2, v_norm_ple, v_ple_gate_w, v_ple_up_w, v_final_norm))))
```

```python
import functools

import jax
import jax.numpy as jnp
from jax import lax
from jax.experimental import pallas as pl
from jax.experimental.pallas import tpu as pltpu

F32 = jnp.float32
BF16 = jnp.bfloat16

D_MODEL = 1024
S5_WIDTH = 512
RW_WIDTH = 512
S5_GROUP = 16
S5_GROUPS = 32
S5_STATE = 64
S5_LANES = S5_GROUPS * S5_STATE
HEAD = 64
SHIFT_COLS = 1792
IN_COLS = 2304
FFN_HIDDEN = 2816
PLE_DIM = 256
RMS_EPS = 1e-6
GN_EPS = 64e-5
L2_EPS = 1e-12
CHUNK = 64
N_DEV = 8

ADAM_LR = 0.001
ADAM_B1 = 0.9
ADAM_B2 = 0.999
ADAM_EPS = 1e-08
ADAM_WD = 0.01
ADAM_STEP = 10

VMEM_LIMIT = 56 * 1024 * 1024
_ANY = pl.BlockSpec(memory_space=pl.ANY)


def _pcall(body, **kw):
    return pl.pallas_call(body, **kw)


def _cparams(n_grid):
    return pltpu.CompilerParams(dimension_semantics=("arbitrary",) * n_grid, vmem_limit_bytes=VMEM_LIMIT)


def _dot(a, b):
    return jnp.dot(a, b, preferred_element_type=F32)


def _dot_nt(a, b):
    return lax.dot_general(a, b, (((1,), (1,)), ((), ())), preferred_element_type=F32)


def _dot_tn(a, b):
    return lax.dot_general(a, b, (((0,), (0,)), ((), ())), preferred_element_type=F32)


def _mmc(w, diff=True, tr=False):
    fw, bw = (_dot_nt, _dot) if tr else (_dot, _dot_nt)
    if not diff:
        return lambda x: fw(x.astype(BF16), w)

    @jax.custom_vjp
    def f(x):
        return fw(x.astype(BF16), w)

    def fwd(x):
        return fw(x.astype(BF16), w), None

    def bwd(_, dy):
        return (bw(dy.astype(BF16), w),)

    f.defvjp(fwd, bwd)
    return f


def _split_dot(x, m, n_split):
    acc = None
    rem = x
    for s in range(n_split):
        part = rem.astype(BF16)
        t = _dot(part, m)
        acc = t if acc is None else acc + t
        if s + 1 < n_split:
            rem = rem - part.astype(F32)
    return acc


def _segsum(m, diff=True):
    if not diff:
        return lambda x: _split_dot(x, m, 2)

    @jax.custom_vjp
    def f(x):
        return _split_dot(x, m, 2)

    def fwd(x):
        return _split_dot(x, m, 2), None

    def bwd(_, dy):
        return (_split_dot(dy, m, 2),)

    f.defvjp(fwd, bwd)
    return f


def _head_indicator(n):
    r = lax.broadcasted_iota(jnp.int32, (n, n), 0) // HEAD
    c = lax.broadcasted_iota(jnp.int32, (n, n), 1) // HEAD
    return (r == c).astype(BF16)


def _rms(x, g):
    return x * lax.rsqrt(jnp.mean(x * x, axis=-1, keepdims=True) + RMS_EPS) * g


def _softplus(x):
    return jnp.maximum(x, 0.0) + jnp.log(1.0 + jnp.exp(-jnp.abs(x)))


def _sigmoid(x):
    return 1.0 / (1.0 + jnp.exp(-x))


def _gelu(x):
    return 0.5 * x * (1.0 + jnp.tanh(0.7978845608028654 * (x + 0.044715 * (x * x * x))))


def _tok_call(name, fn, L, TB, tok_in, const_in, tok_out, acc_out=(), deps=()):
    nb = L // TB
    g8 = TB // 8
    in_specs, args = [], []
    for spec in tok_in:
        if len(spec) == 1:
            arr = spec[0]
            in_specs.append(pl.BlockSpec((arr.shape[0], TB, HEAD), lambda i: (0, i, 0)))
            args.append(arr)
            continue
        arr, width, cb = spec[:3]
        mode = spec[3] if len(spec) > 3 else None
        if mode is None:
            in_specs.append(pl.BlockSpec((TB, width), lambda i, cb=cb: (i, cb)))
        elif mode == "prev":
            in_specs.append(pl.BlockSpec((8, width), lambda i, cb=cb: (jnp.maximum(i * g8 - 1, 0), cb)))
        else:
            in_specs.append(pl.BlockSpec((8, width), lambda i, cb=cb: (jnp.minimum((i + 1) * g8, L // 8 - 1), cb)))
        args.append(arr)
    for c in const_in:
        in_specs.append(pl.BlockSpec(c.shape, lambda i, nd=c.ndim: (0,) * nd, pipeline_mode=pl.Buffered(1)))
        args.append(c)
    for d in deps:
        in_specs.append(pl.BlockSpec(d.shape, lambda i, nd=d.ndim: (0,) * nd))
        args.append(d)
    out_shape, out_specs = [], []
    for width, dt in tok_out:
        if width == "heads":
            out_shape.append(jax.ShapeDtypeStruct((N_HEAD, L, HEAD), dt))
            out_specs.append(pl.BlockSpec((N_HEAD, TB, HEAD), lambda i: (0, i, 0)))
            continue
        out_shape.append(jax.ShapeDtypeStruct((L, width), dt))
        out_specs.append(pl.BlockSpec((TB, width), lambda i: (i, 0)))
    for shp in acc_out:
        out_shape.append(jax.ShapeDtypeStruct(shp, F32))
        out_specs.append(pl.BlockSpec(shp, lambda i, nd=len(shp): (0,) * nd))
    n_tok, n_const, n_to = len(tok_in), len(const_in), len(tok_out)

    def body(*refs):
        i = pl.program_id(0)
        tv = [r[...] if len(r.shape) == 2 else jnp.concatenate([r[h] for h in range(r.shape[0])], axis=1)
              for r in refs[:n_tok]]
        cv = [r[...] for r in refs[n_tok:n_tok + n_const]]
        orefs = refs[n_tok + n_const + len(deps):]
        outs = fn(i, tv, cv)
        for r, v in zip(orefs[:n_to], outs[:n_to]):
            if len(r.shape) == 3:
                for h in range(r.shape[0]):
                    r[h] = v[:, h * HEAD:(h + 1) * HEAD].astype(r.dtype)
            else:
                r[...] = v.astype(r.dtype)
        for r, v in zip(orefs[n_to:], outs[n_to:]):
            @pl.when(i == 0)
            def _(r=r):
                r[...] = jnp.zeros(r.shape, r.dtype)

            r[...] += v

    res = _pcall(body, name=name, grid=(nb,), in_specs=in_specs, out_specs=out_specs, out_shape=out_shape,
                 compiler_params=_cparams(1))(*args)
    return res


def _pick_block(n, cap):
    best = None
    for b in range(128, min(n, cap) + 1, 128):
        if n % b == 0:
            best = b
    return best if best is not None else n


def _mm_tn(name, a, b):
    T, M = a.shape
    N = b.shape[1]
    bm, bn, bt = _pick_block(M, 1536), _pick_block(N, 1536), _pick_block(T, 1024)

    def body(a_ref, b_ref, o_ref):
        t = pl.program_id(2)

        @pl.when(t == 0)
        def _():
            o_ref[...] = jnp.zeros(o_ref.shape, F32)

        o_ref[...] += _dot_tn(a_ref[...].astype(BF16), b_ref[...].astype(BF16))

    return _pcall(body, name=name, grid=(M // bm, N // bn, T // bt),
                  in_specs=[pl.BlockSpec((bt, bm), lambda m, n, t: (t, m)), pl.BlockSpec((bt, bn), lambda m, n, t: (t, n))],
                  out_specs=pl.BlockSpec((bm, bn), lambda m, n, t: (m, n)),
                  out_shape=jax.ShapeDtypeStruct((M, N), F32), compiler_params=_cparams(3))(a, b)


def _s5_param_fn(lam_re, lam_im, log_step, bt_re, bt_im):
    dt = jnp.exp(log_step)
    e = jnp.exp(lam_re * dt)
    lb_re = e * jnp.cos(lam_im * dt)
    lb_im = e * jnp.sin(lam_im * dt)
    den = lam_re * lam_re + lam_im * lam_im
    nr, ni = lb_re - 1.0, lb_im
    co_re = (nr * lam_re + ni * lam_im) / den
    co_im = (ni * lam_re - nr * lam_im) / den
    cr, ci = co_re[:, None, :], co_im[:, None, :]
    return lb_re, lb_im, cr * bt_re - ci * bt_im, cr * bt_im + ci * bt_re


def _s5_param_fwd(lam_re, lam_im, log_step, bt_re, bt_im):
    def body(a, b, c, d, e, o1, o2, o3, o4):
        r = _s5_param_fn(a[...], b[...], c[...], d[...], e[...])
        o1[...], o2[...], o3[...], o4[...] = r

    sh = jax.ShapeDtypeStruct
    return _pcall(body, name="s5_param_fwd",
                  out_shape=[sh(lam_re.shape, F32), sh(lam_re.shape, F32), sh(bt_re.shape, F32), sh(bt_re.shape, F32)])(
        lam_re, lam_im, log_step, bt_re, bt_im)


def _s5_param_bwd(lam_re, lam_im, log_step, bt_re, bt_im, d_lb_re, d_lb_im, d_bb_re, d_bb_im):
    def body(a, b, c, d, e, g1, g2, g3, g4, o1, o2, o3, o4, o5):
        _, vjp = jax.vjp(_s5_param_fn, a[...], b[...], c[...], d[...], e[...])
        r = vjp((g1[...], g2[...], g3[...], g4[...]))
        o1[...], o2[...], o3[...], o4[...], o5[...] = r

    sh = jax.ShapeDtypeStruct
    return _pcall(body, name="s5_param_bwd",
                  out_shape=[sh(lam_re.shape, F32), sh(lam_re.shape, F32), sh(log_step.shape, F32),
                             sh(bt_re.shape, F32), sh(bt_re.shape, F32)])(
        lam_re, lam_im, log_step, bt_re, bt_im, d_lb_re, d_lb_im, d_bb_re, d_bb_im)


def _cmul(ar, ai, br, bi):
    return ar * br - ai * bi, ar * bi + ai * br


def _scan_consts(lr, li, reverse):
    n = lr.shape[1]
    sub = lax.broadcasted_iota(jnp.int32, (8, n), 0)
    pows = [(lr, li)]
    for _ in range(7):
        pows.append(_cmul(pows[-1][0], pows[-1][1], lr, li))
    steps = []
    for s in (1, 2, 4):
        m = (sub < 8 - s) if reverse else (sub >= s)
        pr, pi = pows[s - 1]
        steps.append((s, jnp.where(m, jnp.broadcast_to(pr, (8, n)), 0.0), jnp.where(m, jnp.broadcast_to(pi, (8, n)), 0.0)))
    wr = jnp.zeros((8, n), F32)
    wi = jnp.zeros((8, n), F32)
    for r in range(8):
        e = (8 - r) if reverse else (r + 1)
        wr = jnp.where(sub == r, jnp.broadcast_to(pows[e - 1][0], (8, n)), wr)
        wi = jnp.where(sub == r, jnp.broadcast_to(pows[e - 1][1], (8, n)), wi)
    return steps, wr, wi


S5_Q = 4
S5_QL = S5_WIDTH // S5_Q
S5_QS = S5_LANES // S5_Q
S5_NT = S5_LANES // 128
S5_QT = S5_QS // 128


def _s5_power_table(lb_ref, pw_re, pw_im, seg):
    for j in range(S5_NT):
        lr = jnp.broadcast_to(lb_ref[0:1, j * 128:(j + 1) * 128], (8, 128))
        li = jnp.broadcast_to(lb_ref[1:2, j * 128:(j + 1) * 128], (8, 128))

        def step(i, c, lr=lr, li=li, j=j):
            pw_re[j, i] = c[0]
            pw_im[j, i] = c[1]
            return _cmul(c[0], c[1], lr, li)

        lax.fori_loop(0, seg, step, (lr, li))


def _seg_scan(sre, sim, carry, lb_ref, pw_re, pw_im, rows, reverse):
    seg = rows // 8
    sgn = -1.0 if reverse else 1.0
    sub = lax.broadcasted_iota(jnp.int32, (8, 128), 0)
    rows_at = lambda i: pl.ds(pl.multiple_of(i * 8, 8), 8)
    entering = {}
    half_tiles = S5_NT // 2
    for half in range(2):
        tiles = list(range(half * half_tiles, (half + 1) * half_tiles))
        lam8 = [(jnp.broadcast_to(lb_ref[0:1, j * 128:(j + 1) * 128], (8, 128)),
                 sgn * jnp.broadcast_to(lb_ref[1:2, j * 128:(j + 1) * 128], (8, 128))) for j in tiles]

        def p1(ii, c):
            i = (seg - 1 - ii) if reverse else ii
            out = []
            for n, j in enumerate(tiles):
                lr, li = lam8[n]
                cr, ci = c[2 * n], c[2 * n + 1]
                nr = lr * cr - li * ci + sre[j, rows_at(i), :]
                ni = lr * ci + li * cr + sim[j, rows_at(i), :]
                sre[j, rows_at(i), :] = nr
                sim[j, rows_at(i), :] = ni
                out += [nr, ni]
            return tuple(out)

        ends = lax.fori_loop(0, seg, p1, tuple(jnp.zeros((8, 128), F32) for _ in range(2 * len(tiles))))
        cs = []
        for n, j in enumerate(tiles):
            ls = slice(j * 128, (j + 1) * 128)
            steps, wr, wi = _scan_consts(pw_re[j, seg - 1][0:1, :], sgn * pw_im[j, seg - 1][0:1, :], reverse)
            tr, ti = ends[2 * n], ends[2 * n + 1]
            for sft, pr, pi in steps:
                sh = (8 - sft) if reverse else sft
                yr, yi = pltpu.roll(tr, sh, 0), pltpu.roll(ti, sh, 0)
                tr, ti = tr + pr * yr - pi * yi, ti + pr * yi + pi * yr
            cin_r, cin_i = carry[0:1, ls], carry[1:2, ls]
            tr, ti = tr + wr * cin_r - wi * cin_i, ti + wr * cin_i + wi * cin_r
            edge_out, edge_in, sh = (0, 7, 7) if reverse else (7, 0, 1)
            carry[0:1, ls] = tr[edge_out:edge_out + 1, :]
            carry[1:2, ls] = ti[edge_out:edge_out + 1, :]
            cr = jnp.where(sub == edge_in, jnp.broadcast_to(cin_r, (8, 128)), pltpu.roll(tr, sh, 0))
            ci = jnp.where(sub == edge_in, jnp.broadcast_to(cin_i, (8, 128)), pltpu.roll(ti, sh, 0))
            cs += [cr, ci]
            entering[j] = (cr, ci)

        def p2(i, _):
            k = (seg - 1 - i) if reverse else i
            for n, j in enumerate(tiles):
                pr, pi = pw_re[j, k], pw_im[j, k]
                cr, ci = cs[2 * n], cs[2 * n + 1]
                if reverse:
                    sre[j, rows_at(i), :] = sre[j, rows_at(i), :] + pr * cr + pi * ci
                    sim[j, rows_at(i), :] = sim[j, rows_at(i), :] + pr * ci - pi * cr
                else:
                    sre[j, rows_at(i), :] = sre[j, rows_at(i), :] + pr * cr - pi * ci
                    sim[j, rows_at(i), :] = sim[j, rows_at(i), :] + pr * ci + pi * cr
            return 0

        lax.fori_loop(0, seg, p2, 0, unroll=2)
    return entering


class _SegIO:
    def __init__(self, hbm, buf, sems, rows, width, col0=0):
        self.hbm, self.buf, self.sems, self.rows, self.seg, self.width, self.col0 = hbm, buf, sems, rows, rows // 8, width, col0

    def _copies(self, blk, slot, to_vmem):
        out = []
        for r in range(8):
            h = self.hbm.at[pl.ds(blk * self.rows + r * self.seg, self.seg), pl.ds(self.col0, self.width)]
            v = self.buf.at[slot, :, r, :]
            out.append(pltpu.make_async_copy(h, v, self.sems.at[slot, r]) if to_vmem
                       else pltpu.make_async_copy(v, h, self.sems.at[slot, r]))
        return out

    def start(self, blk, slot, to_vmem):
        for cp in self._copies(blk, slot, to_vmem):
            cp.start()

    def wait(self, blk, slot, to_vmem):
        for cp in self._copies(blk, slot, to_vmem):
            cp.wait()

    def value(self, slot):
        return self.buf[slot].reshape(self.rows, self.width)

    def store(self, slot, val):
        self.buf[slot] = val.reshape(self.seg, 8, self.width)


def _seg_pipeline(i, nb, blk_of, ins, outs, compute):
    slot = i % 2

    @pl.when(i == 0)
    def _():
        for io in ins:
            io.start(blk_of(0), 0, True)

    @pl.when(i + 1 < nb)
    def _():
        for io in ins:
            io.start(blk_of(i + 1), 1 - slot, True)

    for io in ins:
        io.wait(blk_of(i), slot, True)

    @pl.when(i >= 2)
    def _():
        for io in outs:
            io.wait(blk_of(i - 2), slot, False)

    compute(slot)
    for io in outs:
        io.start(blk_of(i), slot, False)

    @pl.when(i == nb - 1)
    def _():
        for io in outs:
            if nb >= 2:
                io.wait(blk_of(i - 1), 1 - slot, False)
            io.wait(blk_of(i), slot, False)


def _s5_scan_fwd(proj, bq_re, bq_im, cq_re, cq_im, lbar, dskip, L, TB):
    nb = L // TB
    seg = TB // 8

    def body(u_hbm, bre, bim, cre, cim, lb_ref, d_ref, y_hbm, ck_ref, sre, sim, carry, pw_re, pw_im,
             ubuf, ybuf, sem_u, sem_y):
        i = pl.program_id(0)
        u_io = _SegIO(u_hbm, ubuf, sem_u, TB, S5_WIDTH)
        y_io = _SegIO(y_hbm, ybuf, sem_y, TB, S5_WIDTH)

        @pl.when(i == 0)
        def _():
            carry[...] = jnp.zeros(carry.shape, F32)
            _s5_power_table(lb_ref, pw_re, pw_im, seg)

        ck_ref[0] = carry[...]

        def compute(slot):
            u = u_io.value(slot)
            ub = u.astype(BF16)
            for q in range(S5_Q):
                uq = ub[:, q * S5_QL:(q + 1) * S5_QL]
                vr, vi = _dot(uq, bre[q]), _dot(uq, bim[q])
                for jj in range(S5_QT):
                    sre[q * S5_QT + jj] = vr[:, jj * 128:(jj + 1) * 128]
                    sim[q * S5_QT + jj] = vi[:, jj * 128:(jj + 1) * 128]
            _seg_scan(sre, sim, carry, lb_ref, pw_re, pw_im, TB, False)
            ys = []
            for q in range(S5_Q):
                sl = slice(q * S5_QL, (q + 1) * S5_QL)
                sr = jnp.concatenate([sre[q * S5_QT + jj] for jj in range(S5_QT)], axis=1).astype(BF16)
                si = jnp.concatenate([sim[q * S5_QT + jj] for jj in range(S5_QT)], axis=1).astype(BF16)
                ys.append(_dot(sr, cre[q]) - _dot(si, cim[q]) + u[:, sl] * d_ref[:, sl])
            y_io.store(slot, jnp.concatenate(ys, axis=1))

        _seg_pipeline(i, nb, lambda st: st, [u_io], [y_io], compute)

    full = lambda a: pl.BlockSpec(a.shape, lambda i, nd=a.ndim: (0,) * nd)
    st = pltpu.VMEM((S5_NT, TB, 128), F32)
    pw = pltpu.VMEM((S5_NT, seg, 8, 128), F32)
    io = pltpu.VMEM((2, seg, 8, S5_WIDTH), F32)
    return _pcall(
        body, name="s5_scan_fwd", grid=(nb,),
        in_specs=[_ANY, full(bq_re), full(bq_im), full(cq_re), full(cq_im), full(lbar), full(dskip)],
        out_specs=[_ANY, pl.BlockSpec((1, 8, S5_LANES), lambda i: (i, 0, 0))],
        out_shape=[jax.ShapeDtypeStruct((L, S5_WIDTH), F32), jax.ShapeDtypeStruct((nb, 8, S5_LANES), F32)],
        scratch_shapes=[st, st, pltpu.VMEM((8, S5_LANES), F32), pw, pw, io, io,
                        pltpu.SemaphoreType.DMA((2, 8)), pltpu.SemaphoreType.DMA((2, 8))],
        compiler_params=_cparams(1))(proj, bq_re, bq_im, cq_re, cq_im, lbar, dskip)


def _s5_scan_bwd(proj, dy, ck, bq_re, bq_im, cq_re, cq_im, lbar, dskip, L, TB):
    nb = L // TB
    seg = TB // 8

    def body(u_hbm, dy_hbm, ck_ref, bre, bim, cre, cim, lb_ref, d_ref,
             du_hbm, dbre, dbim, dcre, dcim, dlb_ref, dd_ref, sre, sim, gre, gim, carry, gcarry, pw_re, pw_im,
             ubuf, dybuf, dubuf, sem_u, sem_dy, sem_du):
        i = pl.program_id(0)
        u_io = _SegIO(u_hbm, ubuf, sem_u, TB, S5_WIDTH)
        dy_io = _SegIO(dy_hbm, dybuf, sem_dy, TB, S5_WIDTH)
        du_io = _SegIO(du_hbm, dubuf, sem_du, TB, S5_WIDTH)

        @pl.when(i == 0)
        def _():
            gcarry[...] = jnp.zeros(gcarry.shape, F32)
            dbre[...] = jnp.zeros(dbre.shape, F32)
            dbim[...] = jnp.zeros(dbim.shape, F32)
            dcre[...] = jnp.zeros(dcre.shape, F32)
            dcim[...] = jnp.zeros(dcim.shape, F32)
            dlb_ref[...] = jnp.zeros(dlb_ref.shape, F32)
            dd_ref[...] = jnp.zeros(dd_ref.shape, F32)
            _s5_power_table(lb_ref, pw_re, pw_im, seg)

        def compute(slot):
            u = u_io.value(slot)
            dy_v = dy_io.value(slot)
            ub = u.astype(BF16)
            dyb = dy_v.astype(BF16)
            carry[...] = ck_ref[0]
            for q in range(S5_Q):
                uq = ub[:, q * S5_QL:(q + 1) * S5_QL]
                dq = dyb[:, q * S5_QL:(q + 1) * S5_QL]
                vr, vi = _dot(uq, bre[q]), _dot(uq, bim[q])
                hr, hi = _dot_nt(dq, cre[q]), -_dot_nt(dq, cim[q])
                for jj in range(S5_QT):
                    ls = slice(jj * 128, (jj + 1) * 128)
                    sre[q * S5_QT + jj] = vr[:, ls]
                    sim[q * S5_QT + jj] = vi[:, ls]
                    gre[q * S5_QT + jj] = hr[:, ls]
                    gim[q * S5_QT + jj] = hi[:, ls]
            entering = _seg_scan(sre, sim, carry, lb_ref, pw_re, pw_im, TB, False)
            _seg_scan(gre, gim, gcarry, lb_ref, pw_re, pw_im, TB, True)

            rows_at = lambda k: pl.ds(pl.multiple_of(k * 8, 8), 8)
            for half in range(2):
                tiles = list(range(half * (S5_NT // 2), (half + 1) * (S5_NT // 2)))
                acc0 = []
                for j in tiles:
                    er, ei = entering[j]
                    gr0, gi0 = gre[j, rows_at(0), :], gim[j, rows_at(0), :]
                    acc0 += [gr0 * er + gi0 * ei, gi0 * er - gr0 * ei]

                def acc_step(k, acc, tiles=tiles):
                    out = []
                    for n, j in enumerate(tiles):
                        gr, gi_ = gre[j, rows_at(k), :], gim[j, rows_at(k), :]
                        spr, spi = sre[j, rows_at(k - 1), :], sim[j, rows_at(k - 1), :]
                        out += [acc[2 * n] + gr * spr + gi_ * spi, acc[2 * n + 1] - gr * spi + gi_ * spr]
                    return tuple(out)

                acc = lax.fori_loop(1, seg, acc_step, tuple(acc0))
                for n, j in enumerate(tiles):
                    ls = slice(j * 128, (j + 1) * 128)
                    dlb_ref[0:1, ls] += jnp.sum(acc[2 * n], axis=0, keepdims=True)
                    dlb_ref[1:2, ls] += jnp.sum(acc[2 * n + 1], axis=0, keepdims=True)

            dd_ref[...] += jnp.sum(dy_v * u, axis=0, keepdims=True)
            dus = []
            for q in range(S5_Q):
                sl = slice(q * S5_QL, (q + 1) * S5_QL)
                cat = lambda ref: jnp.concatenate([ref[q * S5_QT + jj] for jj in range(S5_QT)], axis=1).astype(BF16)
                grq, giq = cat(gre), cat(gim)
                dus.append(_dot_nt(grq, bre[q]) + _dot_nt(giq, bim[q]) + dy_v[:, sl] * d_ref[:, sl])
                dbre[q] += _dot_tn(ub[:, sl], grq)
                dbim[q] += _dot_tn(ub[:, sl], giq)
                dcre[q] += _dot_tn(cat(sre), dyb[:, sl])
                dcim[q] -= _dot_tn(cat(sim), dyb[:, sl])
            du_io.store(slot, jnp.concatenate(dus, axis=1))

        _seg_pipeline(i, nb, lambda st: nb - 1 - st, [u_io, dy_io], [du_io], compute)

    full = lambda a: pl.BlockSpec(a.shape, lambda i, nd=a.ndim: (0,) * nd)
    sh = jax.ShapeDtypeStruct
    outs = [sh((L, S5_WIDTH), F32), sh(bq_re.shape, F32), sh(bq_im.shape, F32), sh(cq_re.shape, F32), sh(cq_im.shape, F32),
            sh((8, S5_LANES), F32), sh((1, S5_WIDTH), F32)]
    fo = lambda s: pl.BlockSpec(s.shape, lambda i, nd=len(s.shape): (0,) * nd)
    st = pltpu.VMEM((S5_NT, TB, 128), F32)
    pw = pltpu.VMEM((S5_NT, seg, 8, 128), F32)
    io = pltpu.VMEM((2, seg, 8, S5_WIDTH), F32)
    sem = pltpu.SemaphoreType.DMA((2, 8))
    return _pcall(
        body, name="s5_scan_bwd", grid=(nb,),
        in_specs=[_ANY, _ANY, pl.BlockSpec((1, 8, S5_LANES), lambda i: (nb - 1 - i, 0, 0)),
                  full(bq_re), full(bq_im), full(cq_re), full(cq_im), full(lbar), full(dskip)],
        out_specs=[_ANY] + [fo(s) for s in outs[1:]],
        out_shape=outs,
        scratch_shapes=[st] * 4 + [pltpu.VMEM((8, S5_LANES), F32)] * 2 + [pw, pw, io, io, io, sem, sem, sem],
        compiler_params=_cparams(1))(proj, dy, ck, bq_re, bq_im, cq_re, cq_im, lbar, dskip)


N_HEAD = RW_WIDTH // HEAD
_NN = (((2,), (1,)), ((0,), (0,)))
_NT = (((2,), (2,)), ((0,), (0,)))
_TN = (((1,), (1,)), ((0,), (0,)))


def _hi_lo(x):
    h = x.astype(BF16)
    return h, (x - h.astype(F32)).astype(BF16)


def _mm_acc(a, b, dims, passes=3):
    dg = lambda p, q: lax.dot_general(p, q, dims, preferred_element_type=F32)
    if passes == 1:
        return dg(a.astype(BF16), b.astype(BF16))
    ah, al = _hi_lo(a)
    bh, bl = _hi_lo(b)
    return dg(ah, bh) + dg(ah, bl) + dg(al, bh)


def _cumsum_rows(x, transpose):
    h, n, _ = x.shape
    ti = lax.broadcasted_iota(jnp.int32, (h, n, n), 1)
    tj = lax.broadcasted_iota(jnp.int32, (h, n, n), 2)
    m = ((tj >= ti) if transpose else (tj <= ti)).astype(BF16)
    acc, rem = None, x
    for s in range(3):
        part = rem.astype(BF16)
        t = lax.dot_general(m, part, _NN, preferred_element_type=F32)
        acc = t if acc is None else acc + t
        if s < 2:
            rem = rem - part.astype(F32)
    return acc


def _slices(x, axis, sizes):
    out, off = [], 0
    for n in sizes:
        out.append(lax.slice_in_dim(x, off, off + n, axis=axis))
        off += n
    return tuple(out)


def _cat_op(axis, sizes, diff):
    plain = lambda *xs: jnp.concatenate(xs, axis=axis)
    if not diff:
        return plain
    f = jax.custom_vjp(plain)
    f.defvjp(lambda *xs: (plain(*xs), None), lambda _, d: _slices(d, axis, sizes))
    return f


def _split_op(axis, sizes, diff):
    plain = lambda x: _slices(x, axis, sizes)
    if not diff:
        return plain
    f = jax.custom_vjp(plain)
    f.defvjp(lambda x: (plain(x), None), lambda _, d: (jnp.concatenate(d, axis=axis),))
    return f


def _mm_ops(diff, passes):
    mm = lambda a, b, dims: _mm_acc(a, b, dims, passes)
    if not diff:
        return (lambda a, b: mm(a, b, _NN), lambda a, b: mm(a, b, _NT), lambda a, b: mm(a, b, _TN))

    @jax.custom_vjp
    def nn(a, b):
        return mm(a, b, _NN)

    nn.defvjp(lambda a, b: (mm(a, b, _NN), (a, b)), lambda r, d: (mm(d, r[1], _NT), mm(r[0], d, _TN)))

    @jax.custom_vjp
    def nt(a, b):
        return mm(a, b, _NT)

    nt.defvjp(lambda a, b: (mm(a, b, _NT), (a, b)), lambda r, d: (mm(d, r[1], _NN), mm(d, r[0], _TN)))

    @jax.custom_vjp
    def tn(a, b):
        return mm(a, b, _TN)

    tn.defvjp(lambda a, b: (mm(a, b, _TN), (a, b)), lambda r, d: (mm(r[1], d, _NT), mm(r[0], d, _NN)))
    return nn, nt, tn


def _cums_op(diff):
    if not diff:
        return lambda x: _cumsum_rows(x, False)

    @jax.custom_vjp
    def cums(x):
        return _cumsum_rows(x, False)

    cums.defvjp(lambda x: (_cumsum_rows(x, False), None), lambda _, d: (_cumsum_rows(d, True),))
    return cums


WKV_PASSES = (1, 1, 1, 1, 1)


WKV_SUB = 4
WKV_BLOCK = CHUNK * WKV_SUB


def _wkv_block(s0, r, w, k, v, a, b, diff):
    p_pair, p_val, p_solve, p_out, p_state = WKV_PASSES
    cums = _cums_op(diff)
    _, nt_pair, _ = _mm_ops(diff, p_pair)
    nn_val, _, _ = _mm_ops(diff, p_val)
    nn_solve, _, _ = _mm_ops(diff, p_solve)
    nn_out, nt_out, _ = _mm_ops(diff, p_out)
    nn_state, _, tn_state = _mm_ops(diff, p_state)
    h, d, n, sub = s0.shape[0], s0.shape[2], CHUNK, WKV_SUB
    hb = h * sub
    to_chunks = lambda t: _cat_op(0, (h,) * sub, diff)(*_split_op(1, (n,) * sub, diff)(t))
    r, w, k, v, a, b = (to_chunks(t) for t in (r, w, k, v, a, b))
    cat_rows2 = _cat_op(1, (n, n), diff)
    cat_lanes2 = _cat_op(2, (n, n), diff)
    split_rows2 = _split_op(1, (n, n), diff)
    split_lanes2 = _split_op(2, (n, n), diff)
    ti = lax.broadcasted_iota(jnp.int32, (hb, n, n), 1)
    tj = lax.broadcasted_iota(jnp.int32, (hb, n, n), 2)
    incl, strict = tj <= ti, tj < ti
    logw = jnp.log(w)
    cum = cums(logw)
    g_in, g_ex, g_inv = jnp.exp(cum), jnp.exp(cum - logw), jnp.exp(-cum)
    ae, re, bi, ki = a * g_ex, r * g_in, b * g_inv, k * g_inv
    top, bot = split_rows2(nt_pair(cat_rows2(ae, re), cat_rows2(bi, ki)))
    tab, tak = split_lanes2(top)
    qb, qk = split_lanes2(bot)
    tab, tak = jnp.where(strict, tab, 0.0), jnp.where(strict, tak, 0.0)
    qb, qk = jnp.where(incl, qb, 0.0), jnp.where(incl, qk, 0.0)
    tak_v, qk_v = split_rows2(nn_val(cat_rows2(tak, qk), v))
    x = cat_lanes2(ae, tak_v)
    npow = tab
    steps = max(1, (n - 1).bit_length())
    for i in range(steps):
        x = x + nn_solve(npow, x)
        if i + 1 < steps:
            npow = nn_solve(npow, npow)
    ae_m, uc = split_lanes2(x)
    qx = nn_out(qb, x)
    q_ae, q_uc = split_lanes2(qx)
    re_m = re + q_ae
    yc = q_uc + qk_v
    g_end = jnp.exp(jnp.sum(logw, axis=1, keepdims=True))
    bg, kg = bi * g_end, ki * g_end
    tm = tn_state(ae_m, bg)
    sc = tn_state(cat_rows2(uc, v), cat_rows2(bg, kg))
    per_chunk = _split_op(0, (h,) * sub, diff)
    re_m, yc, g_end, tm, sc = (per_chunk(t) for t in (re_m, yc, g_end, tm, sc))
    ys, s = [], s0
    for i in range(sub):
        ys.append(nt_out(re_m[i], s) + yc[i])
        s = s * g_end[i] + nn_state(s, tm[i]) + sc[i]
    return _cat_op(1, (n,) * sub, diff)(*ys), s


def _wkv_fwd(r, w, k, v, a, b, L):
    nc = L // WKV_BLOCK

    def body(r_ref, w_ref, k_ref, v_ref, a_ref, b_ref, y_ref, ck_ref, s_ref):
        c = pl.program_id(0)

        @pl.when(c == 0)
        def _():
            s_ref[...] = jnp.zeros(s_ref.shape, F32)

        s0 = s_ref[...]
        ck_ref[0] = s0
        y, s1 = _wkv_block(s0, r_ref[...], w_ref[...], k_ref[...], v_ref[...], a_ref[...], b_ref[...], False)
        y_ref[...] = y
        s_ref[...] = s1

    blk = pl.BlockSpec((N_HEAD, WKV_BLOCK, HEAD), lambda c: (0, c, 0))
    return _pcall(
        body, name="wkv_fwd", grid=(nc,), in_specs=[blk] * 6,
        out_specs=[blk, pl.BlockSpec((1, N_HEAD, HEAD, HEAD), lambda c: (c, 0, 0, 0))],
        out_shape=[jax.ShapeDtypeStruct((N_HEAD, L, HEAD), F32), jax.ShapeDtypeStruct((nc, N_HEAD, HEAD, HEAD), F32)],
        scratch_shapes=[pltpu.VMEM((N_HEAD, HEAD, HEAD), F32)],
        compiler_params=_cparams(1))(r, w, k, v, a, b)


def _wkv_bwd(r, w, k, v, a, b, dy, ck, L, deps=()):
    nc = L // WKV_BLOCK

    def body(r_ref, w_ref, k_ref, v_ref, a_ref, b_ref, dy_ref, ck_ref, *rest):
        dr_ref, dw_ref, dk_ref, dv_ref, da_ref, db_ref, ds_ref = rest[len(deps):]
        c = pl.program_id(0)

        @pl.when(c == 0)
        def _():
            ds_ref[...] = jnp.zeros(ds_ref.shape, F32)

        _, vjp = jax.vjp(lambda *t: _wkv_block(*t, True), ck_ref[0], r_ref[...], w_ref[...], k_ref[...], v_ref[...],
                         a_ref[...], b_ref[...])
        g = vjp((dy_ref[...], ds_ref[...]))
        ds_ref[...] = g[0]
        for o_ref, val in zip((dr_ref, dw_ref, dk_ref, dv_ref, da_ref, db_ref), g[1:]):
            o_ref[...] = val

    blk = pl.BlockSpec((N_HEAD, WKV_BLOCK, HEAD), lambda c: (0, nc - 1 - c, 0))
    sh = jax.ShapeDtypeStruct((N_HEAD, L, HEAD), F32)
    return _pcall(
        body, name="wkv_bwd", grid=(nc,),
        in_specs=[blk] * 7 + [pl.BlockSpec((1, N_HEAD, HEAD, HEAD), lambda c: (nc - 1 - c, 0, 0, 0))]
        + [pl.BlockSpec(d.shape, lambda c, nd=d.ndim: (0,) * nd) for d in deps],
        out_specs=[blk] * 6, out_shape=[sh] * 6,
        scratch_shapes=[pltpu.VMEM((N_HEAD, HEAD, HEAD), F32)],
        compiler_params=_cparams(1))(r, w, k, v, a, b, dy, ck, *deps)


TB = 256


def _bf(x):
    return x.astype(BF16)


def _inproj_fwd(x, norm_mix, w_in, L, deps=()):
    def fn(i, tv, cv):
        xn = _rms(tv[0], cv[0])
        return _dot(_bf(xn), cv[1]), xn

    return _tok_call("inproj_fwd", fn, L, 2 * TB, [(x, D_MODEL, 0)], [norm_mix, w_in], [(IN_COLS, F32), (D_MODEL, BF16)],
                     deps=deps)


def _s5_post_fn(glu_w, wtop, diff=True):
    mg = _mmc(glu_w, diff)
    mt = _mmc(wtop, diff) if wtop is not None else None

    def f(y, glu_b, e):
        z = _gelu(y)
        out = z * _sigmoid(mg(z) + glu_b + e)
        res = mt(out) if mt is not None else out
        return res, (z, out)

    return f


def _s5_post_fwd(y, glu_w, glu_b, L):
    def fn(i, tv, cv):
        out, _ = _s5_post_fn(cv[0], None, False)(tv[0], cv[1], 0.0)
        return (out,)

    return _tok_call("s5_post_fwd", fn, L, TB, [(y, S5_WIDTH, 0)], [glu_w, glu_b], [(S5_WIDTH, F32)])[0]


def _s5_post_bwd(y, dh1, glu_w, glu_b, wtop, L, deps=()):
    def fn(i, tv, cv):
        e0 = jnp.zeros((TB, S5_WIDTH), F32)
        _, vjp, (z, out) = jax.vjp(_s5_post_fn(cv[0], cv[2]), tv[0], cv[1], e0, has_aux=True)
        dy, db, de = vjp(tv[1])
        return dy, db, _dot_tn(_bf(z), _bf(de)), _dot_tn(_bf(out), _bf(tv[1]))

    return _tok_call("s5_post_bwd", fn, L, TB, [(y, S5_WIDTH, 0), (dh1, D_MODEL, 0)], [glu_w, glu_b, wtop],
                     [(S5_WIDTH, F32)], [(1, S5_WIDTH), (S5_WIDTH, S5_WIDTH), (S5_WIDTH, D_MODEL)], deps=deps)


RW_COLBLK = ((RW_WIDTH, 1), (RW_WIDTH, 2), (RW_WIDTH, 3), (128, 16), (128, 17))
RW_MU = ((0, 512), (512, 1024), (1024, 1536), (1536, 1664), (1664, 1792))


def _rw_pre_fn(w2pad, a2pad, g2, diff=True):
    m_w, m_a, m_g = _mmc(w2pad, diff), _mmc(a2pad, diff), _mmc(g2, diff)
    seg = _segsum(_head_indicator(RW_WIDTH), diff)

    def f(zr, zk, zv, zwa, zg, w0, a0, k_k, k_a, e_w, e_a):
        wl_t = jnp.tanh(zwa)
        wlin = w0 + m_w(wl_t) + e_w
        w = -_softplus(-wlin) - 0.5
        decay = jnp.exp(-jnp.exp(w))
        a = _sigmoid(a0 + m_a(zwa) + e_a)
        sg = _sigmoid(zg)
        g = m_g(sg)
        kk = zk * k_k
        kkn = kk / jnp.maximum(jnp.sqrt(seg(kk * kk)), L2_EPS)
        kf = zk * (1.0 + (a - 1.0) * k_a)
        return (zr, decay, kf, zv, -kkn, kkn * a, g), (wl_t, sg)

    return f


def _rw_shifted(i, tv, mu):
    sub = lax.broadcasted_iota(jnp.int32, (TB, 1), 0)
    zs, dif = [], []
    for n in range(5):
        z = tv[n]
        last = jnp.where(i == 0, 0.0, tv[5 + n][7:8, :])
        prev = jnp.where(sub == 0, last, pltpu.roll(z, 1, 0))
        m = mu[:, RW_MU[n][0]:RW_MU[n][1]]
        zs.append(z + (prev - z) * m)
        dif.append(prev - z)
    return zs, dif


def _rw_tok_in(proj):
    return [(proj, wd, cb) for wd, cb in RW_COLBLK] + [(proj, wd, cb, "prev") for wd, cb in RW_COLBLK]


def _rw_pre_fwd(proj, mu, w0, a0, k_k, k_a, w2pad, a2pad, g2, L):
    def fn(i, tv, cv):
        zs, _ = _rw_shifted(i, tv, cv[0])
        outs, _ = _rw_pre_fn(cv[5], cv[6], cv[7], False)(*zs, cv[1], cv[2], cv[3], cv[4], 0.0, 0.0)
        return outs

    return _tok_call("rw_pre_fwd", fn, L, TB, _rw_tok_in(proj), [mu, w0, a0, k_k, k_a, w2pad, a2pad, g2],
                     [("heads", F32)] * 6 + [(RW_WIDTH, F32)])


def _rw_pre_bwd(proj, cots, mu, w0, a0, k_k, k_a, w2pad, a2pad, g2, L):
    def fn(i, tv, cv):
        zs, dif = _rw_shifted(i, tv[:10], cv[0])
        dr1, dr2, dw, dk1, dk2, dv1, dv2, da, db, dg = tv[10:]
        e0 = jnp.zeros((TB, RW_WIDTH), F32)
        _, vjp, (wl_t, sg) = jax.vjp(_rw_pre_fn(cv[5], cv[6], cv[7]), *zs, cv[1], cv[2], cv[3], cv[4], e0, e0, has_aux=True)
        g = vjp((dr1 + dr2, dw, dk1 + dk2, dv1 + dv2, da, db, dg))
        dzs = jnp.concatenate(g[:5], axis=1)
        dmu = jnp.concatenate([jnp.sum(g[n] * dif[n], axis=0, keepdims=True) for n in range(5)], axis=1)
        lora = (_dot_tn(_bf(wl_t), _bf(g[9])), _dot_tn(_bf(zs[3]), _bf(g[10])), _dot_tn(_bf(sg), _bf(dg)))
        return (dzs, dmu, g[5], g[6], g[7], g[8]) + lora

    tok_in = _rw_tok_in(proj) + [((c,) if c.ndim == 3 else (c, RW_WIDTH, 0)) for c in cots]
    return _tok_call("rw_pre_bwd", fn, L, TB, tok_in, [mu, w0, a0, k_k, k_a, w2pad, a2pad, g2],
                     [(SHIFT_COLS, F32)], [(1, SHIFT_COLS)] + [(1, RW_WIDTH)] * 4 + [(128, RW_WIDTH)] * 3)


def _rw_post_fn(wbot, diff=True):
    seg = _segsum(_head_indicator(RW_WIDTH), diff)
    mb = _mmc(wbot, diff) if wbot is not None else None

    def f(y, r, kf, v, g, ln_w, ln_b, r_k):
        mean = seg(y) * (1.0 / HEAD)
        yc = y - mean
        var = seg(yc * yc) * (1.0 / HEAD)
        yn = yc * lax.rsqrt(var + GN_EPS) * ln_w + ln_b
        bonus = seg(r * kf * r_k) * v
        out = (yn + bonus) * g
        res = mb(out) if mb is not None else out
        return res, out

    return f


def _rw_post_fwd(y, r, kf, v, g, ln_w, ln_b, r_k, L):
    def fn(i, tv, cv):
        out, _ = _rw_post_fn(None, False)(*tv, *cv)
        return (out,)

    return _tok_call("rw_post_fwd", fn, L, TB, [(t,) for t in (y, r, kf, v)] + [(g, RW_WIDTH, 0)], [ln_w, ln_b, r_k],
                     [(RW_WIDTH, F32)])[0]


def _rw_post_bwd(y, r, kf, v, g, dh1, ln_w, ln_b, r_k, wbot, L):
    def fn(i, tv, cv):
        _, vjp, out = jax.vjp(_rw_post_fn(cv[3]), *tv[:5], cv[0], cv[1], cv[2], has_aux=True)
        gr = vjp(tv[5])
        return gr[0], gr[1], gr[2], gr[3], gr[4], gr[5], gr[6], gr[7], _dot_tn(_bf(out), _bf(tv[5]))

    return _tok_call("rw_post_bwd", fn, L, TB, [(t,) for t in (y, r, kf, v)] + [(g, RW_WIDTH, 0), (dh1, D_MODEL, 0)],
                     [ln_w, ln_b, r_k, wbot], [("heads", F32)] + [(RW_WIDTH, F32)] * 4,
                     [(1, RW_WIDTH)] * 3 + [(RW_WIDTH, D_MODEL)])


def _ffn_fn(w1, w3, w2, diff=True):
    m1, m3, m2 = _mmc(w1, diff), _mmc(w3, diff), _mmc(w2, diff)

    def f(h1, norm_ffn, e1, e3):
        hn = _rms(h1, norm_ffn)
        a1 = m1(hn) + e1
        a3 = m3(hn) + e3
        hm = a1 * _sigmoid(a1) * a3
        return h1 + m2(hm), (hn, hm)

    return f


TB_FFN = 256


def _mixffn_fwd(x, s5_out, rw_out, wtop, wbot, norm_ffn, w1, w3, w2, L):
    def fn(i, tv, cv):
        h1 = tv[0] + _dot(_bf(tv[1]), cv[0]) + _dot(_bf(tv[2]), cv[1])
        h2, _ = _ffn_fn(cv[3], cv[4], cv[5], False)(h1, cv[2], 0.0, 0.0)
        return h1, h2

    return _tok_call("mixffn_fwd", fn, L, 2 * TB_FFN, [(x, D_MODEL, 0), (s5_out, S5_WIDTH, 0), (rw_out, RW_WIDTH, 0)],
                     [wtop, wbot, norm_ffn, w1, w3, w2], [(D_MODEL, F32), (D_MODEL, F32)])


def _ffn_bwd(h1, dh2, norm_ffn, w1, w3, w2, L):
    def fn(i, tv, cv):
        e0 = jnp.zeros((TB_FFN, FFN_HIDDEN), F32)
        _, vjp, (hn, hm) = jax.vjp(_ffn_fn(cv[1], cv[2], cv[3]), tv[0], cv[0], e0, e0, has_aux=True)
        dh1, dn, d1, d3 = vjp(tv[1])
        return dh1, d1, d3, hm, hn, dn

    return _tok_call("ffn_bwd", fn, L, TB_FFN, [(h1, D_MODEL, 0), (dh2, D_MODEL, 0)], [norm_ffn, w1, w3, w2],
                     [(D_MODEL, F32), (FFN_HIDDEN, BF16), (FFN_HIDDEN, BF16), (FFN_HIDDEN, BF16), (D_MODEL, BF16)],
                     [(1, D_MODEL)])


def _ple_loss_fb(h2, p, target, norm_ple, final_norm, wg, wu, L):
    def fn(i, tv, cv):
        mgate, mup = _mmc(cv[2]), _mmc(cv[3], False)

        def f(h2_, norm_ple_, final_norm_, eg, eu):
            hn = _rms(h2_, norm_ple_)
            gate = _sigmoid(mgate(hn) + eg)
            h3 = h2_ + gate * (mup(tv[1]) + eu)
            out = _rms(h3, final_norm_)
            d = out - tv[2]
            return 0.5 * jnp.sum(jnp.mean(d * d, axis=-1, keepdims=True)), hn

        e0 = jnp.zeros((TB, D_MODEL), F32)
        loss, vjp, hn = jax.vjp(f, tv[0], cv[0], cv[1], e0, e0, has_aux=True)
        dh2, dnp, dfn, deg, deu = vjp(jnp.ones((), F32))
        return (dh2, dh2, jnp.full((8, 128), loss, F32), dnp, dfn,
                _dot_tn(_bf(hn), _bf(deg)), _dot_tn(_bf(tv[1]), _bf(deu)))

    return _tok_call("ple_loss_fb", fn, L, TB, [(h2, D_MODEL, 0), (p, PLE_DIM, 0), (target, D_MODEL, 0)],
                     [norm_ple, final_norm, wg, wu], [(D_MODEL, F32), (D_MODEL, BF16)],
                     [(8, 128), (1, D_MODEL), (1, D_MODEL), (D_MODEL, D_MODEL), (PLE_DIM, D_MODEL)])


def _inproj_bwd(x, dh1, du, dzs, norm_mix, mu, w_u, w_z, L):
    nb = L // TB

    def fn(i, tv, cv):
        sub = lax.broadcasted_iota(jnp.int32, (TB, 1), 0)
        m = cv[1]
        b = tv[3] * m
        nxt = jnp.where(i == nb - 1, 0.0, tv[4][0:1, :] * m)
        dz = tv[3] * (1.0 - m) + jnp.where(sub == TB - 1, nxt, pltpu.roll(b, TB - 1, 0))
        dub, dzb = _bf(tv[2]), _bf(dz)
        dxn = _dot_nt(dub, cv[2]) + _dot_nt(dzb, cv[3])
        _, vjp = jax.vjp(_rms, tv[0], cv[0])
        dx, dn = vjp(dxn)
        return tv[1] + dx, jnp.concatenate([dub, dzb], axis=1), dn

    return _tok_call("inproj_bwd", fn, L, TB,
                     [(x, D_MODEL, 0), (dh1, D_MODEL, 0), (du, S5_WIDTH, 0), (dzs, SHIFT_COLS, 0), (dzs, SHIFT_COLS, 0, "next")],
                     [norm_mix, mu, w_u, w_z], [(D_MODEL, F32), (IN_COLS, BF16)], [(1, D_MODEL)])


def _eye8(dt):
    return jnp.eye(8, dtype=dt)


def _quarter_b(bb):
    return jnp.einsum("hg,qgcp->qhcgp", _eye8(bb.dtype), bb.reshape(S5_Q, 8, S5_GROUP, S5_STATE)).reshape(S5_Q, S5_QL, S5_QS)


def _unquarter_b(d):
    return jnp.einsum("qhcgp,hg->qgcp", d.reshape(S5_Q, 8, S5_GROUP, 8, S5_STATE), _eye8(d.dtype)).reshape(
        S5_GROUPS, S5_GROUP, S5_STATE)


def _quarter_c(c):
    return jnp.einsum("gh,qgcp->qgphc", _eye8(c.dtype), c.reshape(S5_Q, 8, S5_GROUP, S5_STATE)).reshape(S5_Q, S5_QS, S5_QL)


def _unquarter_c(d):
    return jnp.einsum("qgphc,gh->qgcp", d.reshape(S5_Q, 8, S5_STATE, 8, S5_GROUP), _eye8(d.dtype)).reshape(
        S5_GROUPS, S5_GROUP, S5_STATE)


def _local_step(x, p, target, W, late_weights=None, grads_ready=None, first_dep=None):
    L = x.shape[0]
    r2 = lambda v: v.reshape(1, -1)
    w_in = W["w_in"]
    w2pad = jnp.pad(W["rw_w2"], ((0, 64), (0, 0)))
    a2pad = jnp.pad(W["rw_a2"], ((64, 0), (0, 0)))
    mu = r2(W["rw_shift_mu"])
    rw_vec = [r2(W[n]) for n in ("rw_w0", "rw_a0", "rw_k_k", "rw_k_a")]
    ln_w, ln_b, r_k = r2(W["rw_ln_w"]), r2(W["rw_ln_b"]), r2(W["rw_r_k"])

    lam_re, lam_im = W["s5_lam_re"], W["s5_lam_im"]
    log_step = W["s5_log_step"].reshape(S5_GROUPS, 1)
    bt_re, bt_im = W["s5_b_re"].transpose(0, 2, 1), W["s5_b_im"].transpose(0, 2, 1)
    lb_re, lb_im, bb_re, bb_im = _s5_param_fwd(lam_re, lam_im, log_step, bt_re, bt_im)
    bq_re, bq_im = _quarter_b(bb_re).astype(BF16), _quarter_b(bb_im).astype(BF16)
    cq_re, cq_im = _quarter_c(W["s5_c_re"]).astype(BF16), _quarter_c(W["s5_c_im"]).astype(BF16)
    lbar = jnp.concatenate([lb_re.reshape(1, -1), lb_im.reshape(1, -1), jnp.zeros((6, S5_LANES), F32)], axis=0)
    dskip = r2(W["s5_d"])
    glu_b = r2(W["s5_glu_b"])
    norm_mix, norm_ffn, norm_ple, final_norm = (r2(W[n]) for n in ("norm_mix", "norm_ffn", "norm_ple", "final_norm"))

    proj, xn = _inproj_fwd(x, norm_mix, w_in, L, () if first_dep is None else (first_dep,))
    y_s5, ck5 = _s5_scan_fwd(proj, bq_re, bq_im, cq_re, cq_im, lbar, dskip, L, TB)
    s5_out = _s5_post_fwd(y_s5, W["s5_glu_w"], glu_b, L)
    r, wd, kf, v, a_s, b_s, g = _rw_pre_fwd(proj, mu, *rw_vec, w2pad, a2pad, W["rw_g2"], L)
    scan_in = (r, wd, kf, v, a_s, b_s)
    y_wkv, ckw = _wkv_fwd(*scan_in, L)
    rw_out = _rw_post_fwd(y_wkv, r, kf, v, g, ln_w, ln_b, r_k, L)
    if late_weights is not None:
        W = dict(W, **late_weights(rw_out))
    wtop, wbot = W["w_out"][:S5_WIDTH], W["w_out"][S5_WIDTH:]
    h1, h2 = _mixffn_fwd(x, s5_out, rw_out, wtop, wbot, norm_ffn, W["ffn_w1"], W["ffn_w3"], W["ffn_w2"], L)

    G = {}
    dh2, dh2_bf, loss_acc, G["norm_ple"], G["final_norm"], G["ple_gate_w"], G["ple_up_w"] = _ple_loss_fb(
        h2, p, target, norm_ple, final_norm, W["ple_gate_w"], W["ple_up_w"], L)
    dh1, da1, da3, hm, hn_ffn, G["norm_ffn"] = _ffn_bwd(h1, dh2, norm_ffn, W["ffn_w1"], W["ffn_w3"], W["ffn_w2"], L)
    G["ffn_w1"] = _mm_tn("dw_ffn_w1", hn_ffn, da1)
    G["ffn_w3"] = _mm_tn("dw_ffn_w3", hn_ffn, da3)
    G["ffn_w2"] = _mm_tn("dw_ffn_w2", hm, dh2_bf)
    dep_a = grads_ready(0, G) if grads_ready is not None else None
    dy_s5, G["s5_glu_b"], G["s5_glu_w"], d_wtop = _s5_post_bwd(y_s5, dh1, W["s5_glu_w"], glu_b, wtop, L,
                                                               () if dep_a is None else (dep_a,))
    dy_wkv, dr2, dk2, dv2, dg, G["rw_ln_w"], G["rw_ln_b"], G["rw_r_k"], d_wbot = _rw_post_bwd(
        y_wkv, r, kf, v, g, dh1, ln_w, ln_b, r_k, wbot, L)
    G["w_out"] = jnp.concatenate([d_wtop, d_wbot], axis=0)
    dep = grads_ready(1, G) if grads_ready is not None else None
    dr1, dwd, dk1, dv1, da_s, db_s = _wkv_bwd(*scan_in, dy_wkv, ckw, L, () if dep is None else (dep,))
    (dzs, G["rw_shift_mu"], G["rw_w0"], G["rw_a0"], G["rw_k_k"], G["rw_k_a"], d_w2pad, d_a2pad, G["rw_g2"]) = _rw_pre_bwd(
        proj, (dr1, dr2, dwd, dk1, dk2, dv1, dv2, da_s, db_s, dg), mu, *rw_vec, w2pad, a2pad, W["rw_g2"], L)
    G["rw_w2"], G["rw_a2"] = d_w2pad[:64], d_a2pad[64:]
    du, dbq_re, dbq_im, dcq_re, dcq_im, dlbar, G["s5_d"] = _s5_scan_bwd(
        proj, dy_s5, ck5, bq_re, bq_im, cq_re, cq_im, lbar, dskip, L, TB)
    G["s5_c_re"], G["s5_c_im"] = _unquarter_c(dcq_re), _unquarter_c(dcq_im)
    d_lam_re, d_lam_im, d_ls, d_bt_re, d_bt_im = _s5_param_bwd(
        lam_re, lam_im, log_step, bt_re, bt_im, dlbar[0].reshape(S5_GROUPS, S5_STATE), dlbar[1].reshape(S5_GROUPS, S5_STATE),
        _unquarter_b(dbq_re), _unquarter_b(dbq_im))
    G["s5_lam_re"], G["s5_lam_im"], G["s5_log_step"] = d_lam_re, d_lam_im, d_ls.reshape(S5_GROUPS)
    G["s5_b_re"], G["s5_b_im"] = d_bt_re.transpose(0, 2, 1), d_bt_im.transpose(0, 2, 1)
    dx, dproj, G["norm_mix"] = _inproj_bwd(x, dh1, du, dzs, norm_mix, mu, w_in[:, :S5_WIDTH], w_in[:, S5_WIDTH:], L)
    G["w_in"] = _mm_tn("dw_in", xn, dproj)
    return loss_acc[0, 0], dx, G


MESH_AXES = ("x", "y", "c")


def _all_gather(name, shards):
    nt = len(shards)

    def body(*refs):
        x_refs, out_refs = refs[:nt], refs[nt:2 * nt]
        send_sems, recv_sems, local_sems = refs[2 * nt:]
        x, y, c = lax.axis_index("x"), lax.axis_index("y"), lax.axis_index("c")
        me, sibling = (x, y, c), (x, y, 1 - c)
        chips = [(1 - x, y), (x, 1 - y), (1 - x, 1 - y)]

        def rows(t, px, py, pc):
            m_per = shards[t].shape[0]
            return out_refs[t].at[pl.ds((4 * px + 2 * py + pc) * m_per, m_per), :]

        def copy(t, k, block, to, src=None):
            return pltpu.make_async_remote_copy(
                src_ref=rows(t, *block) if src is None else src, dst_ref=rows(t, *block),
                send_sem=send_sems.at[7 * t + k], recv_sem=recv_sems.at[7 * t + k],
                device_id=to, device_id_type=pl.DeviceIdType.MESH)

        mine = [pltpu.make_async_copy(x_refs[t], rows(t, *me), local_sems.at[t]) for t in range(nt)]
        for cp in mine:
            cp.start()
        first = []
        for t in range(nt):
            first.append(copy(t, 0, me, sibling, src=x_refs[t]))
            first += [copy(t, 1 + j, me, (*chip, c), src=x_refs[t]) for j, chip in enumerate(chips)]
        for cp in first:
            cp.start()
        passed = []
        for t in range(nt):
            for j, chip in enumerate(chips):
                copy(t, 1 + j, (*chip, c), me).wait_recv()
                fwd = copy(t, 4 + j, (*chip, c), sibling)
                fwd.start()
                passed.append(fwd)
        for t in range(nt):
            copy(t, 0, sibling, me).wait_recv()
            for j, chip in enumerate(chips):
                copy(t, 4 + j, (*chip, 1 - c), me).wait_recv()
        for cp in first + passed:
            cp.wait_send()
        for cp in mine:
            cp.wait()

    return _pcall(body, name=name,
                  out_shape=[jax.ShapeDtypeStruct((N_DEV * a.shape[0], a.shape[1]), a.dtype) for a in shards],
                  in_specs=[_ANY] * nt, out_specs=[_ANY] * nt,
                  scratch_shapes=[pltpu.SemaphoreType.DMA((7 * nt,)), pltpu.SemaphoreType.DMA((7 * nt,)),
                                  pltpu.SemaphoreType.DMA((nt,))])(*shards)


_HBM = pl.BlockSpec(memory_space=pltpu.HBM)
_SEM = pl.BlockSpec(memory_space=pltpu.SEMAPHORE)
_EFFECT = pltpu.SideEffectType.DATAFLOW_SIDE_EFFECTING


def _peer_of(k):
    x, y, c = lax.axis_index("x"), lax.axis_index("y"), lax.axis_index("c")
    px, py, pc = x ^ ((k >> 2) & 1), y ^ ((k >> 1) & 1), c ^ (k & 1)
    return (px, py, pc), 4 * px + 2 * py + pc, 4 * x + 2 * y + c


def _direct_copy(t, k, src_refs, land_refs, send_sems, recv_sems, rows_of, gather):
    dev, peer, me = _peer_of(k)
    m = rows_of[t]
    src = src_refs[t] if gather else src_refs[t].at[pl.ds(peer * m, m), :]
    return pltpu.make_async_remote_copy(
        src_ref=src, dst_ref=land_refs[t].at[pl.ds(me * m, m), :],
        send_sem=send_sems.at[7 * t + k - 1], recv_sem=recv_sems.at[7 * t + k - 1],
        device_id=dev, device_id_type=pl.DeviceIdType.MESH)


def _direct_landing(t, k, src_refs, land_refs, send_sems, recv_sems, rows_of, gather):
    dev, peer, me = _peer_of(k)
    m = rows_of[t]
    src = src_refs[t] if gather else src_refs[t].at[pl.ds(me * m, m), :]
    return pltpu.make_async_remote_copy(
        src_ref=src, dst_ref=land_refs[t].at[pl.ds(peer * m, m), :],
        send_sem=send_sems.at[7 * t + k - 1], recv_sem=recv_sems.at[7 * t + k - 1],
        device_id=dev, device_id_type=pl.DeviceIdType.MESH)


def _direct_start(name, srcs, gather, dep=None):
    nt = len(srcs)
    rows_of = [a.shape[0] if gather else a.shape[0] // N_DEV for a in srcs]
    lands = [pltpu.with_memory_space_constraint(lax.empty((N_DEV * m, a.shape[1]), a.dtype), pltpu.HBM)
             for a, m in zip(srcs, rows_of)]

    n_dep = 0 if dep is None else 1

    def body(*refs):
        src_refs, land_refs = refs[:nt], refs[nt:2 * nt]
        send_sems, recv_sems = refs[2 * nt + n_dep], refs[2 * nt + n_dep + 1]
        token = refs[-1]
        for t in range(nt):
            for k in range(1, N_DEV):
                _direct_copy(t, k, src_refs, land_refs, send_sems, recv_sems, rows_of, gather).start()
        token[...] = jnp.zeros(token.shape, F32)

    out = _pcall(
        body, name=name,
        out_shape=(pltpu.SemaphoreType.DMA((7 * nt,)), pltpu.SemaphoreType.DMA((7 * nt,)),
                   *[pltpu.HBM(a.shape, a.dtype) for a in srcs], *[pltpu.HBM(a.shape, a.dtype) for a in lands],
                   jax.ShapeDtypeStruct((8, 128), F32)),
        in_specs=(_HBM,) * (2 * nt) + (pl.BlockSpec(memory_space=pl.ANY),) * n_dep,
        out_specs=(_SEM, _SEM) + (_HBM,) * (2 * nt) + (pl.BlockSpec(memory_space=pltpu.VMEM),),
        input_output_aliases={i: 2 + i for i in range(2 * nt)},
        compiler_params=pltpu.CompilerParams(has_side_effects=_EFFECT),
    )(*[pltpu.with_memory_space_constraint(a, pltpu.HBM) for a in srcs], *lands, *(() if dep is None else (dep,)))
    return (out[0], out[1], list(out[2:2 + nt]), list(out[2 + nt:2 + 2 * nt]), rows_of, gather), out[-1]


def _direct_wait(name, handle, after):
    send_sems, recv_sems, srcs, lands, rows_of, gather = handle
    nt = len(srcs)
    after = list(after) if isinstance(after, (list, tuple)) else [after]

    def body(*refs):
        src_refs, land_refs = refs[:nt], refs[nt:2 * nt]
        s_sems, r_sems = refs[2 * nt], refs[2 * nt + 1]
        for t in range(nt):
            for k in range(1, N_DEV):
                _direct_copy(t, k, src_refs, land_refs, s_sems, r_sems, rows_of, gather).wait_send()
                _direct_landing(t, k, src_refs, land_refs, s_sems, r_sems, rows_of, gather).wait_recv()

    out = _pcall(
        body, name=name,
        out_shape=tuple(pltpu.HBM(a.shape, a.dtype) for a in srcs) + tuple(pltpu.HBM(a.shape, a.dtype) for a in lands),
        in_specs=(_HBM,) * (2 * nt) + (_SEM, _SEM) + (pl.BlockSpec(memory_space=pl.ANY),) * len(after),
        out_specs=(_HBM,) * (2 * nt),
        input_output_aliases={i: i for i in range(2 * nt)},
        compiler_params=pltpu.CompilerParams(has_side_effects=_EFFECT),
    )(*srcs, *lands, send_sems, recv_sems, *after)
    return list(out[:nt]), list(out[nt:])


def _adamw_sharded(name, own, parts, w, m, v, rb, deps=()):
    R, N = own.shape

    def body(o_ref, p_ref, w_ref, m_ref, v_ref, *rest):
        g_ref, d_ref, nm_ref, nv_ref = rest[len(deps):]
        me = 4 * lax.axis_index("x") + 2 * lax.axis_index("y") + lax.axis_index("c")
        g = o_ref[...]
        for k in range(1, N_DEV):
            g = g + p_ref[me ^ k].astype(F32)
        nm = ADAM_B1 * m_ref[...] + (1.0 - ADAM_B1) * g
        nv = ADAM_B2 * v_ref[...] + (1.0 - ADAM_B2) * (g * g)
        m_hat = nm / (1.0 - ADAM_B1 ** ADAM_STEP)
        v_hat = nv / (1.0 - ADAM_B2 ** ADAM_STEP)
        g_ref[...] = g
        d_ref[...] = -ADAM_LR * (m_hat / (jnp.sqrt(v_hat) + ADAM_EPS) + ADAM_WD * w_ref[...])
        nm_ref[...] = nm
        nv_ref[...] = nv

    blk = pl.BlockSpec((rb, N), lambda i: (i, 0))
    sh = jax.ShapeDtypeStruct((R, N), F32)
    return _pcall(body, name=name, grid=(R // rb,),
                  in_specs=[blk, pl.BlockSpec((N_DEV, rb, N), lambda i: (0, i, 0)), blk, blk, blk]
                  + [pl.BlockSpec(d.shape, lambda i, nd=d.ndim: (0,) * nd) for d in deps],
                  out_specs=[blk] * 4, out_shape=[sh] * 4, compiler_params=_cparams(1))(own, parts, w, m, v, *deps)


LOSS_SLOT = "loss_partials"
SMALL_CLASSES = (
    (("s5_b_re", 32, 1024), ("s5_b_im", 32, 1024),
     ("norm_mix", 1, 1024), ("norm_ffn", 1, 1024), ("norm_ple", 1, 1024), ("final_norm", 1, 1024)),
    (("s5_d", 1, 512), ("s5_glu_b", 1, 512), ("rw_w0", 1, 512), ("rw_a0", 1, 512), ("rw_k_k", 1, 512), ("rw_k_a", 1, 512),
     ("rw_ln_w", 1, 512), ("rw_ln_b", 1, 512), ("rw_r_k", 1, 512)),
    (("rw_shift_mu", 1, 1792),),
    (("s5_lam_re", 32, 64), ("s5_lam_im", 32, 64), ("s5_c_re", 512, 64), ("s5_c_im", 512, 64)),
    (("s5_log_step", 1, 32), (LOSS_SLOT, 1, 32)),
)


def _class_rows(cls):
    return -(-sum(r for _, r, _ in cls) // 8) * 8


def _stack_class(cls, arrs):
    a = jnp.concatenate(arrs, axis=0) if len(arrs) > 1 else arrs[0]
    pad = _class_rows(cls) - a.shape[0]
    return jnp.pad(a, ((0, pad), (0, 0))) if pad else a


def _adamw_small(grads, w, m, v):
    names = [n for cls in SMALL_CLASSES for n, _, _ in cls]
    n_cls, n_par = len(SMALL_CLASSES), len(names)

    def body(*refs):
        g_refs = refs[:n_cls]
        w_refs, m_refs, v_refs = (refs[n_cls + i * n_par:n_cls + (i + 1) * n_par] for i in range(3))
        o_refs = refs[n_cls + 3 * n_par:]
        p = 0
        for cls, g_ref in zip(SMALL_CLASSES, g_refs):
            rc = _class_rows(cls)
            tot = g_ref[0:rc, :]
            for s_ in range(1, N_DEV):
                tot = tot + g_ref[s_ * rc:(s_ + 1) * rc, :]
            off = 0
            for _, r, _ in cls:
                g = tot[off:off + r, :]
                off += r
                nm = ADAM_B1 * m_refs[p][...] + (1.0 - ADAM_B1) * g
                nv = ADAM_B2 * v_refs[p][...] + (1.0 - ADAM_B2) * (g * g)
                m_hat = nm / (1.0 - ADAM_B1 ** ADAM_STEP)
                v_hat = nv / (1.0 - ADAM_B2 ** ADAM_STEP)
                o_refs[4 * p][...] = g
                o_refs[4 * p + 1][...] = -ADAM_LR * (m_hat / (jnp.sqrt(v_hat) + ADAM_EPS) + ADAM_WD * w_refs[p][...])
                o_refs[4 * p + 2][...] = nm
                o_refs[4 * p + 3][...] = nv
                p += 1

    shapes = [(r, c) for cls in SMALL_CLASSES for _, r, c in cls]
    out = _pcall(body, name="adamw_replicated",
                 out_shape=[jax.ShapeDtypeStruct(sh, F32) for sh in shapes for _ in range(4)],
                 compiler_params=pltpu.CompilerParams(vmem_limit_bytes=VMEM_LIMIT))(*grads, *w, *m, *v)
    return {n: out[4 * i:4 * i + 4] for i, n in enumerate(names)}


EARLY = (("w_in", True),)
LATE = (("ffn_w1", True), ("ffn_w3", True), ("ffn_w2", False), ("ple_gate_w", False), ("w_out", False))
GRAD_STAGES = (LATE[:4], LATE[4:])
MISC = (("s5_glu_w", False), ("rw_w2", True), ("rw_a2", True), ("rw_g2", True), ("ple_up_w", True))
SHARDED_NAMES = tuple(n for n, _ in EARLY + LATE + MISC)
PACK_COLS = 1024
WEIGHT_NAMES = ("norm_mix", "w_in", "s5_lam_re", "s5_lam_im", "s5_log_step", "s5_b_re", "s5_b_im", "s5_c_re", "s5_c_im", "s5_d",
                "s5_glu_w", "s5_glu_b", "rw_shift_mu", "rw_w0", "rw_w2", "rw_a0", "rw_a2", "rw_g2", "rw_k_k", "rw_k_a", "rw_r_k",
                "rw_ln_w", "rw_ln_b", "w_out", "norm_ffn", "ffn_w1", "ffn_w3", "ffn_w2", "norm_ple", "ple_gate_w", "ple_up_w",
                "final_norm")
SMALL_NAMES = tuple(n for n in WEIGHT_NAMES if n not in SHARDED_NAMES)
ARG_NAMES = ("x", "p") + WEIGHT_NAMES + ("loss_target",) + tuple("m_" + n for n in WEIGHT_NAMES) + tuple("v_" + n for n in WEIGHT_NAMES)


def _travel(a, tr):
    return a.T if tr else a


def _pack_misc(blocks):
    lead = blocks[0].shape[:-2]
    return jnp.concatenate([b.reshape(lead + (-1, PACK_COLS)) for b in blocks], axis=len(lead))


def _unpack_misc(packed, shapes):
    lead = packed.shape[:-2]
    out, off = [], 0
    for r, c in shapes:
        n = r * c // PACK_COLS
        out.append(lax.slice_in_dim(packed, off, off + n, axis=len(lead)).reshape(lead + (r, c)))
        off += n
    return out


def _kernel_impl(ins):
    x, p, target = ins["x"][0], ins["p"][0, 0], ins["loss_target"][0]
    me = 4 * lax.axis_index("x") + 2 * lax.axis_index("y") + lax.axis_index("c")
    small = {n: (ins[n] if n == "final_norm" else ins[n][0]) for n in SMALL_NAMES}
    trav = lambda pre, n, tr: _travel(ins[pre + n][0], tr)
    misc_shapes = [trav("", n, tr).shape for n, tr in MISC]

    early = _all_gather("ag_early", [trav("", n, tr).astype(BF16) for n, tr in EARLY]
                        + [_pack_misc([trav("", n, tr).astype(BF16) for n, tr in MISC])])
    late_handle, late_token = _direct_start("ag_late_start", [trav("", n, tr).astype(BF16) for n, tr in LATE], True, early[-1])
    W = dict(small)
    for (n, tr), g in zip(EARLY, early):
        W[n] = _travel(g, tr)
    for (n, tr), g in zip(MISC, _unpack_misc(early[-1].reshape(N_DEV, -1, PACK_COLS), misc_shapes)):
        W[n] = _travel(g.reshape(-1, g.shape[-1]), tr)

    def late_weights(after):
        shards, lands = _direct_wait("ag_late_wait", late_handle, after)
        full = [lax.dynamic_update_slice_in_dim(ld, sh, me * sh.shape[0], axis=0) for ld, sh in zip(lands, shards)]
        return {n: _travel(g, tr) for (n, tr), g in zip(LATE, full)}

    gt = lambda G, n, tr: _travel(G[n], tr)
    started = {}

    def grads_ready(stage, G):
        full = [gt(G, n, tr) for n, tr in GRAD_STAGES[stage]]
        started[stage] = (full, *_direct_start("grad_late_start%d" % stage, [a.astype(BF16) for a in full], False))
        return started[stage][2]

    loss_part, dx, G = _local_step(x, p, target, W, late_weights, grads_ready, late_token)

    misc_g = _pack_misc([gt(G, n, tr).reshape((N_DEV,) + shp) for (n, tr), shp in zip(MISC, misc_shapes)])
    early_full = [gt(G, n, tr) for n, tr in EARLY] + [misc_g.reshape(-1, PACK_COLS)]
    early_handle, early_token = _direct_start("grad_early_start", [a.astype(BF16) for a in early_full], False)
    view2 = lambda a, r, c: a.reshape(r, c)
    G[LOSS_SLOT] = jnp.full((1, 32), loss_part, F32)
    small_own = [_stack_class(cls, [view2(G[n], r, c) for n, r, c in cls]) for cls in SMALL_CLASSES]
    small_handle, small_token = _direct_start("grad_small_start", small_own, True)
    late_src, late_land = [], []
    for stage in range(len(GRAD_STAGES)):
        full, handle, _ = started[stage]
        _, land = _direct_wait("grad_late_wait%d" % stage, handle, small_token)
        late_src += full
        late_land += land

    outs = {}

    def emit(names_shapes, res):
        for tag, val in zip(("grad_", "delta_", "new_m_", "new_v_"), res):
            for n, v in names_shapes(val):
                outs[tag + n] = v

    def sharded_update(n, tr, src, land, deps=()):
        rows = src.shape[0] // N_DEV
        own = lax.dynamic_slice_in_dim(src, me * rows, rows, axis=0)
        res = _adamw_sharded("adamw_" + n, own, land.reshape(N_DEV, rows, land.shape[1]),
                             trav("", n, tr), trav("m_", n, tr), trav("v_", n, tr), _pick_rows(rows), deps)
        emit(lambda val: [(n, _travel(val, tr).reshape(ins[n].shape))], res)
        return list(res)

    for (n, tr), src, land in zip(LATE, late_src, late_land):
        sharded_update(n, tr, src, land, (early_token,))
    _, early_land = _direct_wait("grad_early_wait", early_handle, list(outs.values()))
    for (n, tr), src, land in zip(EARLY, early_full[:-1], early_land[:-1]):
        sharded_update(n, tr, src, land)
    pm = lambda pre: _pack_misc([trav(pre, n, tr) for n, tr in MISC])
    rows = early_full[-1].shape[0] // N_DEV
    res = _adamw_sharded("adamw_misc", lax.dynamic_slice_in_dim(early_full[-1], me * rows, rows, axis=0),
                         early_land[-1].reshape(N_DEV, rows, PACK_COLS), pm(""), pm("m_"), pm("v_"), rows)
    emit(lambda val: [(n, _travel(b, tr).reshape(ins[n].shape)) for (n, tr), b in zip(MISC, _unpack_misc(val, misc_shapes))], res)
    small_src, small_land = _direct_wait("grad_small_wait", small_handle, res[0])
    small_all = [lax.dynamic_update_slice_in_dim(ld, sr, me * sr.shape[0], axis=0) for ld, sr in zip(small_land, small_src)]
    flat_small = [(n, r, c) for cls in SMALL_CLASSES for n, r, c in cls]
    ins = dict(ins, **{pre + LOSS_SLOT: jnp.zeros((1, 32), F32) for pre in ("", "m_", "v_")})
    res = _adamw_small(small_all, *[[view2(ins[pre + n], r, c) for n, r, c in flat_small] for pre in ("", "m_", "v_")])
    loss = res.pop(LOSS_SLOT)[0][0, 0]
    for n, _, _ in flat_small[:-1]:
        for tag, val in zip(("grad_", "delta_", "new_m_", "new_v_"), res[n]):
            outs[tag + n] = val.reshape(ins[n].shape)
    res = [loss, dx[None]]
    for tag in ("grad_", "delta_", "new_m_", "new_v_"):
        res += [outs[tag + n] for n in WEIGHT_NAMES]
    return tuple(res)


def _pick_rows(r):
    best = 8
    for b in range(8, 257, 8):
        if r % b == 0:
            best = b
    return best


def kernel(x, p, norm_mix, w_in, s5_lam_re, s5_lam_im, s5_log_step, s5_b_re, s5_b_im, s5_c_re, s5_c_im, s5_d, s5_glu_w, s5_glu_b, rw_shift_mu, rw_w0, rw_w2, rw_a0, rw_a2, rw_g2, rw_k_k, rw_k_a, rw_r_k, rw_ln_w, rw_ln_b, w_out, norm_ffn, ffn_w1, ffn_w3, ffn_w2, norm_ple, ple_gate_w, ple_up_w, final_norm, loss_target, m_norm_mix, m_w_in, m_s5_lam_re, m_s5_lam_im, m_s5_log_step, m_s5_b_re, m_s5_b_im, m_s5_c_re, m_s5_c_im, m_s5_d, m_s5_glu_w, m_s5_glu_b, m_rw_shift_mu, m_rw_w0, m_rw_w2, m_rw_a0, m_rw_a2, m_rw_g2, m_rw_k_k, m_rw_k_a, m_rw_r_k, m_rw_ln_w, m_rw_ln_b, m_w_out, m_norm_ffn, m_ffn_w1, m_ffn_w3, m_ffn_w2, m_norm_ple, m_ple_gate_w, m_ple_up_w, m_final_norm, v_norm_mix, v_w_in, v_s5_lam_re, v_s5_lam_im, v_s5_log_step, v_s5_b_re, v_s5_b_im, v_s5_c_re, v_s5_c_im, v_s5_d, v_s5_glu_w, v_s5_glu_b, v_rw_shift_mu, v_rw_w0, v_rw_w2, v_rw_a0, v_rw_a2, v_rw_g2, v_rw_k_k, v_rw_k_a, v_rw_r_k, v_rw_ln_w, v_rw_ln_b, v_w_out, v_norm_ffn, v_ffn_w1, v_ffn_w3, v_ffn_w2, v_norm_ple, v_ple_gate_w, v_ple_up_w, v_final_norm):
    return _kernel_impl(dict(zip(ARG_NAMES, (x, p, norm_mix, w_in, s5_lam_re, s5_lam_im, s5_log_step, s5_b_re, s5_b_im, s5_c_re, s5_c_im, s5_d, s5_glu_w, s5_glu_b, rw_shift_mu, rw_w0, rw_w2, rw_a0, rw_a2, rw_g2, rw_k_k, rw_k_a, rw_r_k, rw_ln_w, rw_ln_b, w_out, norm_ffn, ffn_w1, ffn_w3, ffn_w2, norm_ple, ple_gate_w, ple_up_w, final_norm, loss_target, m_norm_mix, m_w_in, m_s5_lam_re, m_s5_lam_im, m_s5_log_step, m_s5_b_re, m_s5_b_im, m_s5_c_re, m_s5_c_im, m_s5_d, m_s5_glu_w, m_s5_glu_b, m_rw_shift_mu, m_rw_w0, m_rw_w2, m_rw_a0, m_rw_a2, m_rw_g2, m_rw_k_k, m_rw_k_a, m_rw_r_k, m_rw_ln_w, m_rw_ln_b, m_w_out, m_norm_ffn, m_ffn_w1, m_ffn_w3, m_ffn_w2, m_norm_ple, m_ple_gate_w, m_ple_up_w, m_final_norm, v_norm_mix, v_w_in, v_s5_lam_re, v_s5_lam_im, v_s5_log_step, v_s5_b_re, v_s5_b_im, v_s5_c_re, v_s5_c_im, v_s5_d, v_s5_glu_w, v_s5_glu_b, v_rw_shift_mu, v_rw_w0, v_rw_w2, v_rw_a0, v_rw_a2, v_rw_g2, v_rw_k_k, v_rw_k_a, v_rw_r_k, v_rw_ln_w, v_rw_ln_b, v_w_out, v_norm_ffn, v_ffn_w1, v_ffn_w3, v_ffn_w2, v_norm_ple, v_ple_gate_w, v_ple_up_w, v_final_norm))))
```

```python
import jax
import jax.numpy as jnp
from jax import lax
from jax.experimental import pallas as pl
from jax.experimental.pallas import tpu as pltpu

F32 = jnp.float32
BF16 = jnp.bfloat16

D_MODEL = 1024
S5_WIDTH = 512
RW_WIDTH = 512
S5_GROUP = 16
S5_GROUPS = 32
S5_STATE = 64
S5_LANES = S5_GROUPS * S5_STATE
HEAD = 64
SHIFT_COLS = 1792
IN_COLS = 2304
FFN_HIDDEN = 2816
PLE_DIM = 256
RMS_EPS = 1e-6
GN_EPS = 64e-5
L2_EPS = 1e-12
CHUNK = 64
N_DEV = 8

ADAM_LR = 0.001
ADAM_B1 = 0.9
ADAM_B2 = 0.999
ADAM_EPS = 1e-08
ADAM_WD = 0.01
ADAM_STEP = 10

VMEM_LIMIT = 56 * 1024 * 1024
_ANY = pl.BlockSpec(memory_space=pl.ANY)


def _pcall(body, **kw):
    return pl.pallas_call(body, **kw)


def _cparams(n_grid):
    return pltpu.CompilerParams(dimension_semantics=("arbitrary",) * n_grid, vmem_limit_bytes=VMEM_LIMIT)


def _dot(a, b):
    return jnp.dot(a, b, preferred_element_type=F32)


def _dot_nt(a, b):
    return lax.dot_general(a, b, (((1,), (1,)), ((), ())), preferred_element_type=F32)


def _dot_tn(a, b):
    return lax.dot_general(a, b, (((0,), (0,)), ((), ())), preferred_element_type=F32)


def _mmc(w, diff=True, tr=False):
    fw, bw = (_dot_nt, _dot) if tr else (_dot, _dot_nt)
    if not diff:
        return lambda x: fw(x.astype(BF16), w)

    @jax.custom_vjp
    def f(x):
        return fw(x.astype(BF16), w)

    def fwd(x):
        return fw(x.astype(BF16), w), None

    def bwd(_, dy):
        return (bw(dy.astype(BF16), w),)

    f.defvjp(fwd, bwd)
    return f


def _split_dot(x, m, n_split):
    acc = None
    rem = x
    for s in range(n_split):
        part = rem.astype(BF16)
        t = _dot(part, m)
        acc = t if acc is None else acc + t
        if s + 1 < n_split:
            rem = rem - part.astype(F32)
    return acc


def _segsum(m, diff=True):
    if not diff:
        return lambda x: _split_dot(x, m, 2)

    @jax.custom_vjp
    def f(x):
        return _split_dot(x, m, 2)

    def fwd(x):
        return _split_dot(x, m, 2), None

    def bwd(_, dy):
        return (_split_dot(dy, m, 2),)

    f.defvjp(fwd, bwd)
    return f


def _head_indicator(n):
    r = lax.broadcasted_iota(jnp.int32, (n, n), 0) // HEAD
    c = lax.broadcasted_iota(jnp.int32, (n, n), 1) // HEAD
    return (r == c).astype(BF16)


def _rms(x, g):
    return x * lax.rsqrt(jnp.mean(x * x, axis=-1, keepdims=True) + RMS_EPS) * g


def _softplus(x):
    return jnp.maximum(x, 0.0) + jnp.log(1.0 + jnp.exp(-jnp.abs(x)))


def _sigmoid(x):
    return 1.0 / (1.0 + jnp.exp(-x))


def _gelu(x):
    return 0.5 * x * (1.0 + jnp.tanh(0.7978845608028654 * (x + 0.044715 * (x * x * x))))


def _tok_call(name, fn, L, TB, tok_in, const_in, tok_out, acc_out=(), deps=()):
    nb = L // TB
    g8 = TB // 8
    in_specs, args = [], []
    for spec in tok_in:
        if len(spec) == 1:
            arr = spec[0]
            in_specs.append(pl.BlockSpec((arr.shape[0], TB, HEAD), lambda i: (0, i, 0)))
            args.append(arr)
            continue
        arr, width, cb = spec[:3]
        mode = spec[3] if len(spec) > 3 else None
        if mode is None:
            in_specs.append(pl.BlockSpec((TB, width), lambda i, cb=cb: (i, cb)))
        elif mode == "prev":
            in_specs.append(pl.BlockSpec((8, width), lambda i, cb=cb: (jnp.maximum(i * g8 - 1, 0), cb)))
        else:
            in_specs.append(pl.BlockSpec((8, width), lambda i, cb=cb: (jnp.minimum((i + 1) * g8, L // 8 - 1), cb)))
        args.append(arr)
    for c in const_in:
        in_specs.append(pl.BlockSpec(c.shape, lambda i, nd=c.ndim: (0,) * nd, pipeline_mode=pl.Buffered(1)))
        args.append(c)
    for d in deps:
        in_specs.append(pl.BlockSpec(d.shape, lambda i, nd=d.ndim: (0,) * nd))
        args.append(d)
    out_shape, out_specs = [], []
    for width, dt in tok_out:
        if width == "heads":
            out_shape.append(jax.ShapeDtypeStruct((N_HEAD, L, HEAD), dt))
            out_specs.append(pl.BlockSpec((N_HEAD, TB, HEAD), lambda i: (0, i, 0)))
            continue
        out_shape.append(jax.ShapeDtypeStruct((L, width), dt))
        out_specs.append(pl.BlockSpec((TB, width), lambda i: (i, 0)))
    for shp in acc_out:
        out_shape.append(jax.ShapeDtypeStruct(shp, F32))
        out_specs.append(pl.BlockSpec(shp, lambda i, nd=len(shp): (0,) * nd))
    n_tok, n_const, n_to = len(tok_in), len(const_in), len(tok_out)

    def body(*refs):
        i = pl.program_id(0)
        tv = [r[...] if len(r.shape) == 2 else jnp.concatenate([r[h] for h in range(r.shape[0])], axis=1)
              for r in refs[:n_tok]]
        cv = [r[...] for r in refs[n_tok:n_tok + n_const]]
        orefs = refs[n_tok + n_const + len(deps):]
        outs = fn(i, tv, cv)
        for r, v in zip(orefs[:n_to], outs[:n_to]):
            if len(r.shape) == 3:
                for h in range(r.shape[0]):
                    r[h] = v[:, h * HEAD:(h + 1) * HEAD].astype(r.dtype)
            else:
                r[...] = v.astype(r.dtype)
        for r, v in zip(orefs[n_to:], outs[n_to:]):
            @pl.when(i == 0)
            def _(r=r):
                r[...] = jnp.zeros(r.shape, r.dtype)

            r[...] += v

    res = _pcall(body, name=name, grid=(nb,), in_specs=in_specs, out_specs=out_specs, out_shape=out_shape,
                 compiler_params=_cparams(1))(*args)
    return res


def _pick_block(n, cap):
    best = None
    for b in range(128, min(n, cap) + 1, 128):
        if n % b == 0:
            best = b
    return best if best is not None else n


def _mm_tn(name, a, b):
    T, M = a.shape
    N = b.shape[1]
    bm, bn, bt = _pick_block(M, 1536), _pick_block(N, 1536), _pick_block(T, 1024)

    def body(a_ref, b_ref, o_ref):
        t = pl.program_id(2)

        @pl.when(t == 0)
        def _():
            o_ref[...] = jnp.zeros(o_ref.shape, F32)

        o_ref[...] += _dot_tn(a_ref[...].astype(BF16), b_ref[...].astype(BF16))

    return _pcall(body, name=name, grid=(M // bm, N // bn, T // bt),
                  in_specs=[pl.BlockSpec((bt, bm), lambda m, n, t: (t, m)), pl.BlockSpec((bt, bn), lambda m, n, t: (t, n))],
                  out_specs=pl.BlockSpec((bm, bn), lambda m, n, t: (m, n)),
                  out_shape=jax.ShapeDtypeStruct((M, N), F32), compiler_params=_cparams(3))(a, b)


def _s5_param_fn(lam_re, lam_im, log_step, bt_re, bt_im):
    dt = jnp.exp(log_step)
    e = jnp.exp(lam_re * dt)
    lb_re = e * jnp.cos(lam_im * dt)
    lb_im = e * jnp.sin(lam_im * dt)
    den = lam_re * lam_re + lam_im * lam_im
    nr, ni = lb_re - 1.0, lb_im
    co_re = (nr * lam_re + ni * lam_im) / den
    co_im = (ni * lam_re - nr * lam_im) / den
    cr, ci = co_re[:, None, :], co_im[:, None, :]
    return lb_re, lb_im, cr * bt_re - ci * bt_im, cr * bt_im + ci * bt_re


def _s5_param_fwd(lam_re, lam_im, log_step, bt_re, bt_im):
    def body(a, b, c, d, e, o1, o2, o3, o4):
        r = _s5_param_fn(a[...], b[...], c[...], d[...], e[...])
        o1[...], o2[...], o3[...], o4[...] = r

    sh = jax.ShapeDtypeStruct
    return _pcall(body, name="s5_param_fwd",
                  out_shape=[sh(lam_re.shape, F32), sh(lam_re.shape, F32), sh(bt_re.shape, F32), sh(bt_re.shape, F32)])(
        lam_re, lam_im, log_step, bt_re, bt_im)


def _s5_param_bwd(lam_re, lam_im, log_step, bt_re, bt_im, d_lb_re, d_lb_im, d_bb_re, d_bb_im):
    def body(a, b, c, d, e, g1, g2, g3, g4, o1, o2, o3, o4, o5):
        _, vjp = jax.vjp(_s5_param_fn, a[...], b[...], c[...], d[...], e[...])
        r = vjp((g1[...], g2[...], g3[...], g4[...]))
        o1[...], o2[...], o3[...], o4[...], o5[...] = r

    sh = jax.ShapeDtypeStruct
    return _pcall(body, name="s5_param_bwd",
                  out_shape=[sh(lam_re.shape, F32), sh(lam_re.shape, F32), sh(log_step.shape, F32),
                             sh(bt_re.shape, F32), sh(bt_re.shape, F32)])(
        lam_re, lam_im, log_step, bt_re, bt_im, d_lb_re, d_lb_im, d_bb_re, d_bb_im)


def _cmul(ar, ai, br, bi):
    return ar * br - ai * bi, ar * bi + ai * br


def _scan_consts(lr, li, reverse):
    n = lr.shape[1]
    sub = lax.broadcasted_iota(jnp.int32, (8, n), 0)
    pows = [(lr, li)]
    for _ in range(7):
        pows.append(_cmul(pows[-1][0], pows[-1][1], lr, li))
    steps = []
    for s in (1, 2, 4):
        m = (sub < 8 - s) if reverse else (sub >= s)
        pr, pi = pows[s - 1]
        steps.append((s, jnp.where(m, jnp.broadcast_to(pr, (8, n)), 0.0), jnp.where(m, jnp.broadcast_to(pi, (8, n)), 0.0)))
    wr = jnp.zeros((8, n), F32)
    wi = jnp.zeros((8, n), F32)
    for r in range(8):
        e = (8 - r) if reverse else (r + 1)
        wr = jnp.where(sub == r, jnp.broadcast_to(pows[e - 1][0], (8, n)), wr)
        wi = jnp.where(sub == r, jnp.broadcast_to(pows[e - 1][1], (8, n)), wi)
    return steps, wr, wi


S5_Q = 4
S5_QL = S5_WIDTH // S5_Q
S5_QS = S5_LANES // S5_Q
S5_NT = S5_LANES // 128
S5_QT = S5_QS // 128


def _s5_power_table(lb_ref, pw_re, pw_im, seg):
    for j in range(S5_NT):
        lr = jnp.broadcast_to(lb_ref[0:1, j * 128:(j + 1) * 128], (8, 128))
        li = jnp.broadcast_to(lb_ref[1:2, j * 128:(j + 1) * 128], (8, 128))

        def step(i, c, lr=lr, li=li, j=j):
            pw_re[j, i] = c[0]
            pw_im[j, i] = c[1]
            return _cmul(c[0], c[1], lr, li)

        lax.fori_loop(0, seg, step, (lr, li))


def _seg_scan(sre, sim, carry, lb_ref, pw_re, pw_im, rows, reverse):
    seg = rows // 8
    sgn = -1.0 if reverse else 1.0
    sub = lax.broadcasted_iota(jnp.int32, (8, 128), 0)
    rows_at = lambda i: pl.ds(pl.multiple_of(i * 8, 8), 8)
    entering = {}
    half_tiles = S5_NT // 2
    for half in range(2):
        tiles = list(range(half * half_tiles, (half + 1) * half_tiles))
        lam8 = [(jnp.broadcast_to(lb_ref[0:1, j * 128:(j + 1) * 128], (8, 128)),
                 sgn * jnp.broadcast_to(lb_ref[1:2, j * 128:(j + 1) * 128], (8, 128))) for j in tiles]

        def p1(ii, c):
            i = (seg - 1 - ii) if reverse else ii
            out = []
            for n, j in enumerate(tiles):
                lr, li = lam8[n]
                cr, ci = c[2 * n], c[2 * n + 1]
                nr = lr * cr - li * ci + sre[j, rows_at(i), :]
                ni = lr * ci + li * cr + sim[j, rows_at(i), :]
                sre[j, rows_at(i), :] = nr
                sim[j, rows_at(i), :] = ni
                out += [nr, ni]
            return tuple(out)

        ends = lax.fori_loop(0, seg, p1, tuple(jnp.zeros((8, 128), F32) for _ in range(2 * len(tiles))))
        cs = []
        for n, j in enumerate(tiles):
            ls = slice(j * 128, (j + 1) * 128)
            steps, wr, wi = _scan_consts(pw_re[j, seg - 1][0:1, :], sgn * pw_im[j, seg - 1][0:1, :], reverse)
            tr, ti = ends[2 * n], ends[2 * n + 1]
            for sft, pr, pi in steps:
                sh = (8 - sft) if reverse else sft
                yr, yi = pltpu.roll(tr, sh, 0), pltpu.roll(ti, sh, 0)
                tr, ti = tr + pr * yr - pi * yi, ti + pr * yi + pi * yr
            cin_r, cin_i = carry[0:1, ls], carry[1:2, ls]
            tr, ti = tr + wr * cin_r - wi * cin_i, ti + wr * cin_i + wi * cin_r
            edge_out, edge_in, sh = (0, 7, 7) if reverse else (7, 0, 1)
            carry[0:1, ls] = tr[edge_out:edge_out + 1, :]
            carry[1:2, ls] = ti[edge_out:edge_out + 1, :]
            cr = jnp.where(sub == edge_in, jnp.broadcast_to(cin_r, (8, 128)), pltpu.roll(tr, sh, 0))
            ci = jnp.where(sub == edge_in, jnp.broadcast_to(cin_i, (8, 128)), pltpu.roll(ti, sh, 0))
            cs += [cr, ci]
            entering[j] = (cr, ci)

        def p2(i, _):
            k = (seg - 1 - i) if reverse else i
            for n, j in enumerate(tiles):
                pr, pi = pw_re[j, k], pw_im[j, k]
                cr, ci = cs[2 * n], cs[2 * n + 1]
                if reverse:
                    sre[j, rows_at(i), :] = sre[j, rows_at(i), :] + pr * cr + pi * ci
                    sim[j, rows_at(i), :] = sim[j, rows_at(i), :] + pr * ci - pi * cr
                else:
                    sre[j, rows_at(i), :] = sre[j, rows_at(i), :] + pr * cr - pi * ci
                    sim[j, rows_at(i), :] = sim[j, rows_at(i), :] + pr * ci + pi * cr
            return 0

        lax.fori_loop(0, seg, p2, 0, unroll=2)
    return entering


class _SegIO:
    def __init__(self, hbm, buf, sems, rows, width, col0=0):
        self.hbm, self.buf, self.sems, self.rows, self.seg, self.width, self.col0 = hbm, buf, sems, rows, rows // 8, width, col0

    def _copies(self, blk, slot, to_vmem):
        out = []
        for r in range(8):
            h = self.hbm.at[pl.ds(blk * self.rows + r * self.seg, self.seg), pl.ds(self.col0, self.width)]
            v = self.buf.at[slot, :, r, :]
            out.append(pltpu.make_async_copy(h, v, self.sems.at[slot, r]) if to_vmem
                       else pltpu.make_async_copy(v, h, self.sems.at[slot, r]))
        return out

    def start(self, blk, slot, to_vmem):
        for cp in self._copies(blk, slot, to_vmem):
            cp.start()

    def wait(self, blk, slot, to_vmem):
        for cp in self._copies(blk, slot, to_vmem):
            cp.wait()

    def value(self, slot):
        return self.buf[slot].reshape(self.rows, self.width)

    def store(self, slot, val):
        self.buf[slot] = val.reshape(self.seg, 8, self.width)


def _seg_pipeline(i, nb, blk_of, ins, outs, compute):
    slot = i % 2

    @pl.when(i == 0)
    def _():
        for io in ins:
            io.start(blk_of(0), 0, True)

    @pl.when(i + 1 < nb)
    def _():
        for io in ins:
            io.start(blk_of(i + 1), 1 - slot, True)

    for io in ins:
        io.wait(blk_of(i), slot, True)

    @pl.when(i >= 2)
    def _():
        for io in outs:
            io.wait(blk_of(i - 2), slot, False)

    compute(slot)
    for io in outs:
        io.start(blk_of(i), slot, False)

    @pl.when(i == nb - 1)
    def _():
        for io in outs:
            if nb >= 2:
                io.wait(blk_of(i - 1), 1 - slot, False)
            io.wait(blk_of(i), slot, False)


def _s5_scan_fwd(proj, bq_re, bq_im, cq_re, cq_im, lbar, dskip, L, TB):
    nb = L // TB
    seg = TB // 8

    def body(u_hbm, bre, bim, cre, cim, lb_ref, d_ref, y_hbm, ck_ref, sre, sim, carry, pw_re, pw_im,
             ubuf, ybuf, sem_u, sem_y):
        i = pl.program_id(0)
        u_io = _SegIO(u_hbm, ubuf, sem_u, TB, S5_WIDTH)
        y_io = _SegIO(y_hbm, ybuf, sem_y, TB, S5_WIDTH)

        @pl.when(i == 0)
        def _():
            carry[...] = jnp.zeros(carry.shape, F32)
            _s5_power_table(lb_ref, pw_re, pw_im, seg)

        ck_ref[0] = carry[...]

        def compute(slot):
            u = u_io.value(slot)
            ub = u.astype(BF16)
            for q in range(S5_Q):
                uq = ub[:, q * S5_QL:(q + 1) * S5_QL]
                vr, vi = _dot(uq, bre[q]), _dot(uq, bim[q])
                for jj in range(S5_QT):
                    sre[q * S5_QT + jj] = vr[:, jj * 128:(jj + 1) * 128]
                    sim[q * S5_QT + jj] = vi[:, jj * 128:(jj + 1) * 128]
            _seg_scan(sre, sim, carry, lb_ref, pw_re, pw_im, TB, False)
            ys = []
            for q in range(S5_Q):
                sl = slice(q * S5_QL, (q + 1) * S5_QL)
                sr = jnp.concatenate([sre[q * S5_QT + jj] for jj in range(S5_QT)], axis=1).astype(BF16)
                si = jnp.concatenate([sim[q * S5_QT + jj] for jj in range(S5_QT)], axis=1).astype(BF16)
                ys.append(_dot(sr, cre[q]) - _dot(si, cim[q]) + u[:, sl] * d_ref[:, sl])
            y_io.store(slot, jnp.concatenate(ys, axis=1))

        _seg_pipeline(i, nb, lambda st: st, [u_io], [y_io], compute)

    full = lambda a: pl.BlockSpec(a.shape, lambda i, nd=a.ndim: (0,) * nd)
    st = pltpu.VMEM((S5_NT, TB, 128), F32)
    pw = pltpu.VMEM((S5_NT, seg, 8, 128), F32)
    io = pltpu.VMEM((2, seg, 8, S5_WIDTH), F32)
    return _pcall(
        body, name="s5_scan_fwd", grid=(nb,),
        in_specs=[_ANY, full(bq_re), full(bq_im), full(cq_re), full(cq_im), full(lbar), full(dskip)],
        out_specs=[_ANY, pl.BlockSpec((1, 8, S5_LANES), lambda i: (i, 0, 0))],
        out_shape=[jax.ShapeDtypeStruct((L, S5_WIDTH), F32), jax.ShapeDtypeStruct((nb, 8, S5_LANES), F32)],
        scratch_shapes=[st, st, pltpu.VMEM((8, S5_LANES), F32), pw, pw, io, io,
                        pltpu.SemaphoreType.DMA((2, 8)), pltpu.SemaphoreType.DMA((2, 8))],
        compiler_params=_cparams(1))(proj, bq_re, bq_im, cq_re, cq_im, lbar, dskip)


def _s5_scan_bwd(proj, dy, ck, bq_re, bq_im, cq_re, cq_im, lbar, dskip, L, TB):
    nb = L // TB
    seg = TB // 8

    def body(u_hbm, dy_hbm, ck_ref, bre, bim, cre, cim, lb_ref, d_ref,
             du_hbm, dbre, dbim, dcre, dcim, dlb_ref, dd_ref, sre, sim, gre, gim, carry, gcarry, pw_re, pw_im,
             ubuf, dybuf, dubuf, sem_u, sem_dy, sem_du):
        i = pl.program_id(0)
        u_io = _SegIO(u_hbm, ubuf, sem_u, TB, S5_WIDTH)
        dy_io = _SegIO(dy_hbm, dybuf, sem_dy, TB, S5_WIDTH)
        du_io = _SegIO(du_hbm, dubuf, sem_du, TB, S5_WIDTH)

        @pl.when(i == 0)
        def _():
            gcarry[...] = jnp.zeros(gcarry.shape, F32)
            dbre[...] = jnp.zeros(dbre.shape, F32)
            dbim[...] = jnp.zeros(dbim.shape, F32)
            dcre[...] = jnp.zeros(dcre.shape, F32)
            dcim[...] = jnp.zeros(dcim.shape, F32)
            dlb_ref[...] = jnp.zeros(dlb_ref.shape, F32)
            dd_ref[...] = jnp.zeros(dd_ref.shape, F32)
            _s5_power_table(lb_ref, pw_re, pw_im, seg)

        def compute(slot):
            u = u_io.value(slot)
            dy_v = dy_io.value(slot)
            ub = u.astype(BF16)
            dyb = dy_v.astype(BF16)
            carry[...] = ck_ref[0]
            for q in range(S5_Q):
                uq = ub[:, q * S5_QL:(q + 1) * S5_QL]
                dq = dyb[:, q * S5_QL:(q + 1) * S5_QL]
                vr, vi = _dot(uq, bre[q]), _dot(uq, bim[q])
                hr, hi = _dot_nt(dq, cre[q]), -_dot_nt(dq, cim[q])
                for jj in range(S5_QT):
                    ls = slice(jj * 128, (jj + 1) * 128)
                    sre[q * S5_QT + jj] = vr[:, ls]
                    sim[q * S5_QT + jj] = vi[:, ls]
                    gre[q * S5_QT + jj] = hr[:, ls]
                    gim[q * S5_QT + jj] = hi[:, ls]
            entering = _seg_scan(sre, sim, carry, lb_ref, pw_re, pw_im, TB, False)
            _seg_scan(gre, gim, gcarry, lb_ref, pw_re, pw_im, TB, True)

            rows_at = lambda k: pl.ds(pl.multiple_of(k * 8, 8), 8)
            for half in range(2):
                tiles = list(range(half * (S5_NT // 2), (half + 1) * (S5_NT // 2)))
                acc0 = []
                for j in tiles:
                    er, ei = entering[j]
                    gr0, gi0 = gre[j, rows_at(0), :], gim[j, rows_at(0), :]
                    acc0 += [gr0 * er + gi0 * ei, gi0 * er - gr0 * ei]

                def acc_step(k, acc, tiles=tiles):
                    out = []
                    for n, j in enumerate(tiles):
                        gr, gi_ = gre[j, rows_at(k), :], gim[j, rows_at(k), :]
                        spr, spi = sre[j, rows_at(k - 1), :], sim[j, rows_at(k - 1), :]
                        out += [acc[2 * n] + gr * spr + gi_ * spi, acc[2 * n + 1] - gr * spi + gi_ * spr]
                    return tuple(out)

                acc = lax.fori_loop(1, seg, acc_step, tuple(acc0))
                for n, j in enumerate(tiles):
                    ls = slice(j * 128, (j + 1) * 128)
                    dlb_ref[0:1, ls] += jnp.sum(acc[2 * n], axis=0, keepdims=True)
                    dlb_ref[1:2, ls] += jnp.sum(acc[2 * n + 1], axis=0, keepdims=True)

            dd_ref[...] += jnp.sum(dy_v * u, axis=0, keepdims=True)
            dus = []
            for q in range(S5_Q):
                sl = slice(q * S5_QL, (q + 1) * S5_QL)
                cat = lambda ref: jnp.concatenate([ref[q * S5_QT + jj] for jj in range(S5_QT)], axis=1).astype(BF16)
                grq, giq = cat(gre), cat(gim)
                dus.append(_dot_nt(grq, bre[q]) + _dot_nt(giq, bim[q]) + dy_v[:, sl] * d_ref[:, sl])
                dbre[q] += _dot_tn(ub[:, sl], grq)
                dbim[q] += _dot_tn(ub[:, sl], giq)
                dcre[q] += _dot_tn(cat(sre), dyb[:, sl])
                dcim[q] -= _dot_tn(cat(sim), dyb[:, sl])
            du_io.store(slot, jnp.concatenate(dus, axis=1))

        _seg_pipeline(i, nb, lambda st: nb - 1 - st, [u_io, dy_io], [du_io], compute)

    full = lambda a: pl.BlockSpec(a.shape, lambda i, nd=a.ndim: (0,) * nd)
    sh = jax.ShapeDtypeStruct
    outs = [sh((L, S5_WIDTH), F32), sh(bq_re.shape, F32), sh(bq_im.shape, F32), sh(cq_re.shape, F32), sh(cq_im.shape, F32),
            sh((8, S5_LANES), F32), sh((1, S5_WIDTH), F32)]
    fo = lambda s: pl.BlockSpec(s.shape, lambda i, nd=len(s.shape): (0,) * nd)
    st = pltpu.VMEM((S5_NT, TB, 128), F32)
    pw = pltpu.VMEM((S5_NT, seg, 8, 128), F32)
    io = pltpu.VMEM((2, seg, 8, S5_WIDTH), F32)
    sem = pltpu.SemaphoreType.DMA((2, 8))
    return _pcall(
        body, name="s5_scan_bwd", grid=(nb,),
        in_specs=[_ANY, _ANY, pl.BlockSpec((1, 8, S5_LANES), lambda i: (nb - 1 - i, 0, 0)),
                  full(bq_re), full(bq_im), full(cq_re), full(cq_im), full(lbar), full(dskip)],
        out_specs=[_ANY] + [fo(s) for s in outs[1:]],
        out_shape=outs,
        scratch_shapes=[st] * 4 + [pltpu.VMEM((8, S5_LANES), F32)] * 2 + [pw, pw, io, io, io, sem, sem, sem],
        compiler_params=_cparams(1))(proj, dy, ck, bq_re, bq_im, cq_re, cq_im, lbar, dskip)


N_HEAD = RW_WIDTH // HEAD
_NN = (((2,), (1,)), ((0,), (0,)))
_NT = (((2,), (2,)), ((0,), (0,)))
_TN = (((1,), (1,)), ((0,), (0,)))


def _hi_lo(x):
    h = x.astype(BF16)
    return h, (x - h.astype(F32)).astype(BF16)


def _mm_acc(a, b, dims, passes=3):
    dg = lambda p, q: lax.dot_general(p, q, dims, preferred_element_type=F32)
    if passes == 1:
        return dg(a.astype(BF16), b.astype(BF16))
    ah, al = _hi_lo(a)
    bh, bl = _hi_lo(b)
    return dg(ah, bh) + dg(ah, bl) + dg(al, bh)


def _cumsum_rows(x, transpose):
    h, n, _ = x.shape
    ti = lax.broadcasted_iota(jnp.int32, (h, n, n), 1)
    tj = lax.broadcasted_iota(jnp.int32, (h, n, n), 2)
    m = ((tj >= ti) if transpose else (tj <= ti)).astype(BF16)
    acc, rem = None, x
    for s in range(3):
        part = rem.astype(BF16)
        t = lax.dot_general(m, part, _NN, preferred_element_type=F32)
        acc = t if acc is None else acc + t
        if s < 2:
            rem = rem - part.astype(F32)
    return acc


def _slices(x, axis, sizes):
    out, off = [], 0
    for n in sizes:
        out.append(lax.slice_in_dim(x, off, off + n, axis=axis))
        off += n
    return tuple(out)


def _cat_op(axis, sizes, diff):
    plain = lambda *xs: jnp.concatenate(xs, axis=axis)
    if not diff:
        return plain
    f = jax.custom_vjp(plain)
    f.defvjp(lambda *xs: (plain(*xs), None), lambda _, d: _slices(d, axis, sizes))
    return f


def _split_op(axis, sizes, diff):
    plain = lambda x: _slices(x, axis, sizes)
    if not diff:
        return plain
    f = jax.custom_vjp(plain)
    f.defvjp(lambda x: (plain(x), None), lambda _, d: (jnp.concatenate(d, axis=axis),))
    return f


def _mm_ops(diff, passes):
    mm = lambda a, b, dims: _mm_acc(a, b, dims, passes)
    if not diff:
        return (lambda a, b: mm(a, b, _NN), lambda a, b: mm(a, b, _NT), lambda a, b: mm(a, b, _TN))

    @jax.custom_vjp
    def nn(a, b):
        return mm(a, b, _NN)

    nn.defvjp(lambda a, b: (mm(a, b, _NN), (a, b)), lambda r, d: (mm(d, r[1], _NT), mm(r[0], d, _TN)))

    @jax.custom_vjp
    def nt(a, b):
        return mm(a, b, _NT)

    nt.defvjp(lambda a, b: (mm(a, b, _NT), (a, b)), lambda r, d: (mm(d, r[1], _NN), mm(d, r[0], _TN)))

    @jax.custom_vjp
    def tn(a, b):
        return mm(a, b, _TN)

    tn.defvjp(lambda a, b: (mm(a, b, _TN), (a, b)), lambda r, d: (mm(r[1], d, _NT), mm(r[0], d, _NN)))
    return nn, nt, tn


def _cums_op(diff):
    if not diff:
        return lambda x: _cumsum_rows(x, False)

    @jax.custom_vjp
    def cums(x):
        return _cumsum_rows(x, False)

    cums.defvjp(lambda x: (_cumsum_rows(x, False), None), lambda _, d: (_cumsum_rows(d, True),))
    return cums


WKV_PASSES = (1, 1, 1, 1, 1)


WKV_SUB = 4
WKV_BLOCK = CHUNK * WKV_SUB


def _wkv_block(s0, r, w, k, v, a, b, diff):
    p_pair, p_val, p_solve, p_out, p_state = WKV_PASSES
    cums = _cums_op(diff)
    _, nt_pair, _ = _mm_ops(diff, p_pair)
    nn_val, _, _ = _mm_ops(diff, p_val)
    nn_solve, _, _ = _mm_ops(diff, p_solve)
    nn_out, nt_out, _ = _mm_ops(diff, p_out)
    nn_state, _, tn_state = _mm_ops(diff, p_state)
    h, d, n, sub = s0.shape[0], s0.shape[2], CHUNK, WKV_SUB
    hb = h * sub
    to_chunks = lambda t: _cat_op(0, (h,) * sub, diff)(*_split_op(1, (n,) * sub, diff)(t))
    r, w, k, v, a, b = (to_chunks(t) for t in (r, w, k, v, a, b))
    cat_rows2 = _cat_op(1, (n, n), diff)
    cat_lanes2 = _cat_op(2, (n, n), diff)
    split_rows2 = _split_op(1, (n, n), diff)
    split_lanes2 = _split_op(2, (n, n), diff)
    ti = lax.broadcasted_iota(jnp.int32, (hb, n, n), 1)
    tj = lax.broadcasted_iota(jnp.int32, (hb, n, n), 2)
    incl, strict = tj <= ti, tj < ti
    logw = jnp.log(w)
    cum = cums(logw)
    g_in, g_ex, g_inv = jnp.exp(cum), jnp.exp(cum - logw), jnp.exp(-cum)
    ae, re, bi, ki = a * g_ex, r * g_in, b * g_inv, k * g_inv
    top, bot = split_rows2(nt_pair(cat_rows2(ae, re), cat_rows2(bi, ki)))
    tab, tak = split_lanes2(top)
    qb, qk = split_lanes2(bot)
    tab, tak = jnp.where(strict, tab, 0.0), jnp.where(strict, tak, 0.0)
    qb, qk = jnp.where(incl, qb, 0.0), jnp.where(incl, qk, 0.0)
    tak_v, qk_v = split_rows2(nn_val(cat_rows2(tak, qk), v))
    x = cat_lanes2(ae, tak_v)
    npow = tab
    steps = max(1, (n - 1).bit_length())
    for i in range(steps):
        x = x + nn_solve(npow, x)
        if i + 1 < steps:
            npow = nn_solve(npow, npow)
    ae_m, uc = split_lanes2(x)
    qx = nn_out(qb, x)
    q_ae, q_uc = split_lanes2(qx)
    re_m = re + q_ae
    yc = q_uc + qk_v
    g_end = jnp.exp(jnp.sum(logw, axis=1, keepdims=True))
    bg, kg = bi * g_end, ki * g_end
    tm = tn_state(ae_m, bg)
    sc = tn_state(cat_rows2(uc, v), cat_rows2(bg, kg))
    per_chunk = _split_op(0, (h,) * sub, diff)
    re_m, yc, g_end, tm, sc = (per_chunk(t) for t in (re_m, yc, g_end, tm, sc))
    ys, s = [], s0
    for i in range(sub):
        ys.append(nt_out(re_m[i], s) + yc[i])
        s = s * g_end[i] + nn_state(s, tm[i]) + sc[i]
    return _cat_op(1, (n,) * sub, diff)(*ys), s


def _wkv_fwd(r, w, k, v, a, b, L):
    nc = L // WKV_BLOCK

    def body(r_ref, w_ref, k_ref, v_ref, a_ref, b_ref, y_ref, ck_ref, s_ref):
        c = pl.program_id(0)

        @pl.when(c == 0)
        def _():
            s_ref[...] = jnp.zeros(s_ref.shape, F32)

        s0 = s_ref[...]
        ck_ref[0] = s0
        y, s1 = _wkv_block(s0, r_ref[...], w_ref[...], k_ref[...], v_ref[...], a_ref[...], b_ref[...], False)
        y_ref[...] = y
        s_ref[...] = s1

    blk = pl.BlockSpec((N_HEAD, WKV_BLOCK, HEAD), lambda c: (0, c, 0))
    return _pcall(
        body, name="wkv_fwd", grid=(nc,), in_specs=[blk] * 6,
        out_specs=[blk, pl.BlockSpec((1, N_HEAD, HEAD, HEAD), lambda c: (c, 0, 0, 0))],
        out_shape=[jax.ShapeDtypeStruct((N_HEAD, L, HEAD), F32), jax.ShapeDtypeStruct((nc, N_HEAD, HEAD, HEAD), F32)],
        scratch_shapes=[pltpu.VMEM((N_HEAD, HEAD, HEAD), F32)],
        compiler_params=_cparams(1))(r, w, k, v, a, b)


def _wkv_bwd(r, w, k, v, a, b, dy, ck, L, deps=()):
    nc = L // WKV_BLOCK

    def body(r_ref, w_ref, k_ref, v_ref, a_ref, b_ref, dy_ref, ck_ref, *rest):
        dr_ref, dw_ref, dk_ref, dv_ref, da_ref, db_ref, ds_ref = rest[len(deps):]
        c = pl.program_id(0)

        @pl.when(c == 0)
        def _():
            ds_ref[...] = jnp.zeros(ds_ref.shape, F32)

        _, vjp = jax.vjp(lambda *t: _wkv_block(*t, True), ck_ref[0], r_ref[...], w_ref[...], k_ref[...], v_ref[...],
                         a_ref[...], b_ref[...])
        g = vjp((dy_ref[...], ds_ref[...]))
        ds_ref[...] = g[0]
        for o_ref, val in zip((dr_ref, dw_ref, dk_ref, dv_ref, da_ref, db_ref), g[1:]):
            o_ref[...] = val

    blk = pl.BlockSpec((N_HEAD, WKV_BLOCK, HEAD), lambda c: (0, nc - 1 - c, 0))
    sh = jax.ShapeDtypeStruct((N_HEAD, L, HEAD), F32)
    return _pcall(
        body, name="wkv_bwd", grid=(nc,),
        in_specs=[blk] * 7 + [pl.BlockSpec((1, N_HEAD, HEAD, HEAD), lambda c: (nc - 1 - c, 0, 0, 0))]
        + [pl.BlockSpec(d.shape, lambda c, nd=d.ndim: (0,) * nd) for d in deps],
        out_specs=[blk] * 6, out_shape=[sh] * 6,
        scratch_shapes=[pltpu.VMEM((N_HEAD, HEAD, HEAD), F32)],
        compiler_params=_cparams(1))(r, w, k, v, a, b, dy, ck, *deps)


TB = 256


def _bf(x):
    return x.astype(BF16)


def _inproj_fwd(x, norm_mix, w_in, L, deps=()):
    def fn(i, tv, cv):
        xn = _rms(tv[0], cv[0])
        return _dot(_bf(xn), cv[1]), xn

    return _tok_call("inproj_fwd", fn, L, 2 * TB, [(x, D_MODEL, 0)], [norm_mix, w_in], [(IN_COLS, F32), (D_MODEL, BF16)],
                     deps=deps)


def _s5_post_fn(glu_w, wtop, diff=True):
    mg = _mmc(glu_w, diff)
    mt = _mmc(wtop, diff) if wtop is not None else None

    def f(y, glu_b, e):
        z = _gelu(y)
        out = z * _sigmoid(mg(z) + glu_b + e)
        res = mt(out) if mt is not None else out
        return res, (z, out)

    return f


def _s5_post_fwd(y, glu_w, glu_b, L):
    def fn(i, tv, cv):
        out, _ = _s5_post_fn(cv[0], None, False)(tv[0], cv[1], 0.0)
        return (out,)

    return _tok_call("s5_post_fwd", fn, L, 2 * TB, [(y, S5_WIDTH, 0)], [glu_w, glu_b], [(S5_WIDTH, F32)])[0]


def _s5_post_bwd(y, dh1, glu_w, glu_b, wtop, L, deps=()):
    def fn(i, tv, cv):
        e0 = jnp.zeros((TB, S5_WIDTH), F32)
        _, vjp, (z, out) = jax.vjp(_s5_post_fn(cv[0], cv[2]), tv[0], cv[1], e0, has_aux=True)
        dy, db, de = vjp(tv[1])
        return dy, db, _dot_tn(_bf(z), _bf(de)), _dot_tn(_bf(out), _bf(tv[1]))

    return _tok_call("s5_post_bwd", fn, L, TB, [(y, S5_WIDTH, 0), (dh1, D_MODEL, 0)], [glu_w, glu_b, wtop],
                     [(S5_WIDTH, F32)], [(1, S5_WIDTH), (S5_WIDTH, S5_WIDTH), (S5_WIDTH, D_MODEL)], deps=deps)


RW_COLBLK = ((RW_WIDTH, 1), (RW_WIDTH, 2), (RW_WIDTH, 3), (128, 16), (128, 17))
RW_MU = ((0, 512), (512, 1024), (1024, 1536), (1536, 1664), (1664, 1792))


def _rw_pre_fn(w2pad, a2pad, g2, diff=True):
    m_w, m_a, m_g = _mmc(w2pad, diff), _mmc(a2pad, diff), _mmc(g2, diff)
    seg = _segsum(_head_indicator(RW_WIDTH), diff)

    def f(zr, zk, zv, zwa, zg, w0, a0, k_k, k_a, e_w, e_a):
        wl_t = jnp.tanh(zwa)
        wlin = w0 + m_w(wl_t) + e_w
        w = -_softplus(-wlin) - 0.5
        decay = jnp.exp(-jnp.exp(w))
        a = _sigmoid(a0 + m_a(zwa) + e_a)
        sg = _sigmoid(zg)
        g = m_g(sg)
        kk = zk * k_k
        kkn = kk / jnp.maximum(jnp.sqrt(seg(kk * kk)), L2_EPS)
        kf = zk * (1.0 + (a - 1.0) * k_a)
        return (zr, decay, kf, zv, -kkn, kkn * a, g), (wl_t, sg)

    return f


def _rw_shifted(i, tv, mu):
    sub = lax.broadcasted_iota(jnp.int32, (TB, 1), 0)
    zs, dif = [], []
    for n in range(5):
        z = tv[n]
        last = jnp.where(i == 0, 0.0, tv[5 + n][7:8, :])
        prev = jnp.where(sub == 0, last, pltpu.roll(z, 1, 0))
        m = mu[:, RW_MU[n][0]:RW_MU[n][1]]
        zs.append(z + (prev - z) * m)
        dif.append(prev - z)
    return zs, dif


def _rw_tok_in(proj):
    return [(proj, wd, cb) for wd, cb in RW_COLBLK] + [(proj, wd, cb, "prev") for wd, cb in RW_COLBLK]


def _rw_pre_fwd(proj, mu, w0, a0, k_k, k_a, w2pad, a2pad, g2, L):
    def fn(i, tv, cv):
        zs, _ = _rw_shifted(i, tv, cv[0])
        outs, _ = _rw_pre_fn(cv[5], cv[6], cv[7], False)(*zs, cv[1], cv[2], cv[3], cv[4], 0.0, 0.0)
        return outs

    return _tok_call("rw_pre_fwd", fn, L, TB, _rw_tok_in(proj), [mu, w0, a0, k_k, k_a, w2pad, a2pad, g2],
                     [("heads", F32)] * 6 + [(RW_WIDTH, F32)])


def _rw_pre_bwd(proj, cots, mu, w0, a0, k_k, k_a, w2pad, a2pad, g2, L):
    def fn(i, tv, cv):
        zs, dif = _rw_shifted(i, tv[:10], cv[0])
        dr1, dr2, dw, dk1, dk2, dv1, dv2, da, db, dg = tv[10:]
        e0 = jnp.zeros((TB, RW_WIDTH), F32)
        _, vjp, (wl_t, sg) = jax.vjp(_rw_pre_fn(cv[5], cv[6], cv[7]), *zs, cv[1], cv[2], cv[3], cv[4], e0, e0, has_aux=True)
        g = vjp((dr1 + dr2, dw, dk1 + dk2, dv1 + dv2, da, db, dg))
        dzs = jnp.concatenate(g[:5], axis=1)
        dmu = jnp.concatenate([jnp.sum(g[n] * dif[n], axis=0, keepdims=True) for n in range(5)], axis=1)
        lora = (_dot_tn(_bf(wl_t), _bf(g[9])), _dot_tn(_bf(zs[3]), _bf(g[10])), _dot_tn(_bf(sg), _bf(dg)))
        return (dzs, dmu, g[5], g[6], g[7], g[8]) + lora

    tok_in = _rw_tok_in(proj) + [((c,) if c.ndim == 3 else (c, RW_WIDTH, 0)) for c in cots]
    return _tok_call("rw_pre_bwd", fn, L, TB, tok_in, [mu, w0, a0, k_k, k_a, w2pad, a2pad, g2],
                     [(SHIFT_COLS, F32)], [(1, SHIFT_COLS)] + [(1, RW_WIDTH)] * 4 + [(128, RW_WIDTH)] * 3)


def _rw_post_fn(wbot, diff=True):
    seg = _segsum(_head_indicator(RW_WIDTH), diff)
    mb = _mmc(wbot, diff) if wbot is not None else None

    def f(y, r, kf, v, g, ln_w, ln_b, r_k):
        mean = seg(y) * (1.0 / HEAD)
        yc = y - mean
        var = seg(yc * yc) * (1.0 / HEAD)
        yn = yc * lax.rsqrt(var + GN_EPS) * ln_w + ln_b
        bonus = seg(r * kf * r_k) * v
        out = (yn + bonus) * g
        res = mb(out) if mb is not None else out
        return res, out

    return f


def _rw_post_fwd(y, r, kf, v, g, ln_w, ln_b, r_k, L):
    def fn(i, tv, cv):
        out, _ = _rw_post_fn(None, False)(*tv, *cv)
        return (out,)

    return _tok_call("rw_post_fwd", fn, L, 2 * TB, [(t,) for t in (y, r, kf, v)] + [(g, RW_WIDTH, 0)], [ln_w, ln_b, r_k],
                     [(RW_WIDTH, F32)])[0]


def _rw_post_bwd(y, r, kf, v, g, dh1, ln_w, ln_b, r_k, wbot, L):
    def fn(i, tv, cv):
        _, vjp, out = jax.vjp(_rw_post_fn(cv[3]), *tv[:5], cv[0], cv[1], cv[2], has_aux=True)
        gr = vjp(tv[5])
        return gr[0], gr[1], gr[2], gr[3], gr[4], gr[5], gr[6], gr[7], _dot_tn(_bf(out), _bf(tv[5]))

    return _tok_call("rw_post_bwd", fn, L, TB, [(t,) for t in (y, r, kf, v)] + [(g, RW_WIDTH, 0), (dh1, D_MODEL, 0)],
                     [ln_w, ln_b, r_k, wbot], [("heads", F32)] + [(RW_WIDTH, F32)] * 4,
                     [(1, RW_WIDTH)] * 3 + [(RW_WIDTH, D_MODEL)])


def _ffn_fn(w1, w3, w2, diff=True):
    m1, m3, m2 = _mmc(w1, diff), _mmc(w3, diff), _mmc(w2, diff)

    def f(h1, norm_ffn, e1, e3):
        hn = _rms(h1, norm_ffn)
        a1 = m1(hn) + e1
        a3 = m3(hn) + e3
        hm = a1 * _sigmoid(a1) * a3
        return h1 + m2(hm), (hn, hm)

    return f


TB_FFN = 256


def _mixffn_fwd(x, s5_out, rw_out, wtop, wbot, norm_ffn, w1, w3, w2, L):
    def fn(i, tv, cv):
        h1 = tv[0] + _dot(_bf(tv[1]), cv[0]) + _dot(_bf(tv[2]), cv[1])
        h2, _ = _ffn_fn(cv[3], cv[4], cv[5], False)(h1, cv[2], 0.0, 0.0)
        return h1, h2

    return _tok_call("mixffn_fwd", fn, L, 2 * TB_FFN, [(x, D_MODEL, 0), (s5_out, S5_WIDTH, 0), (rw_out, RW_WIDTH, 0)],
                     [wtop, wbot, norm_ffn, w1, w3, w2], [(D_MODEL, F32), (D_MODEL, F32)])


def _ffn_bwd(h1, dh2, norm_ffn, w1, w3, w2, L):
    def fn(i, tv, cv):
        e0 = jnp.zeros((TB_FFN, FFN_HIDDEN), F32)
        _, vjp, (hn, hm) = jax.vjp(_ffn_fn(cv[1], cv[2], cv[3]), tv[0], cv[0], e0, e0, has_aux=True)
        dh1, dn, d1, d3 = vjp(tv[1])
        return dh1, d1, d3, hm, hn, dn

    return _tok_call("ffn_bwd", fn, L, TB_FFN, [(h1, D_MODEL, 0), (dh2, D_MODEL, 0)], [norm_ffn, w1, w3, w2],
                     [(D_MODEL, F32), (FFN_HIDDEN, BF16), (FFN_HIDDEN, BF16), (FFN_HIDDEN, BF16), (D_MODEL, BF16)],
                     [(1, D_MODEL)])


def _ple_loss_fb(h2, p, target, norm_ple, final_norm, wg, wu, L):
    def fn(i, tv, cv):
        mgate, mup = _mmc(cv[2]), _mmc(cv[3], False)

        def f(h2_, norm_ple_, final_norm_, eg, eu):
            hn = _rms(h2_, norm_ple_)
            gate = _sigmoid(mgate(hn) + eg)
            h3 = h2_ + gate * (mup(tv[1]) + eu)
            out = _rms(h3, final_norm_)
            d = out - tv[2]
            return 0.5 * jnp.sum(jnp.mean(d * d, axis=-1, keepdims=True)), hn

        e0 = jnp.zeros((TB, D_MODEL), F32)
        loss, vjp, hn = jax.vjp(f, tv[0], cv[0], cv[1], e0, e0, has_aux=True)
        dh2, dnp, dfn, deg, deu = vjp(jnp.ones((), F32))
        return (dh2, dh2, jnp.full((8, 128), loss, F32), dnp, dfn,
                _dot_tn(_bf(hn), _bf(deg)), _dot_tn(_bf(tv[1]), _bf(deu)))

    return _tok_call("ple_loss_fb", fn, L, TB, [(h2, D_MODEL, 0), (p, PLE_DIM, 0), (target, D_MODEL, 0)],
                     [norm_ple, final_norm, wg, wu], [(D_MODEL, F32), (D_MODEL, BF16)],
                     [(8, 128), (1, D_MODEL), (1, D_MODEL), (D_MODEL, D_MODEL), (PLE_DIM, D_MODEL)])


def _inproj_bwd(x, dh1, du, dzs, norm_mix, mu, w_u, w_z, L):
    nb = L // TB

    def fn(i, tv, cv):
        sub = lax.broadcasted_iota(jnp.int32, (TB, 1), 0)
        m = cv[1]
        b = tv[3] * m
        nxt = jnp.where(i == nb - 1, 0.0, tv[4][0:1, :] * m)
        dz = tv[3] * (1.0 - m) + jnp.where(sub == TB - 1, nxt, pltpu.roll(b, TB - 1, 0))
        dub, dzb = _bf(tv[2]), _bf(dz)
        dxn = _dot_nt(dub, cv[2]) + _dot_nt(dzb, cv[3])
        _, vjp = jax.vjp(_rms, tv[0], cv[0])
        dx, dn = vjp(dxn)
        return tv[1] + dx, jnp.concatenate([dub, dzb], axis=1), dn

    return _tok_call("inproj_bwd", fn, L, TB,
                     [(x, D_MODEL, 0), (dh1, D_MODEL, 0), (du, S5_WIDTH, 0), (dzs, SHIFT_COLS, 0), (dzs, SHIFT_COLS, 0, "next")],
                     [norm_mix, mu, w_u, w_z], [(D_MODEL, F32), (IN_COLS, BF16)], [(1, D_MODEL)])


def _eye8(dt):
    return jnp.eye(8, dtype=dt)


def _quarter_b(bb):
    return jnp.einsum("hg,qgcp->qhcgp", _eye8(bb.dtype), bb.reshape(S5_Q, 8, S5_GROUP, S5_STATE)).reshape(S5_Q, S5_QL, S5_QS)


def _unquarter_b(d):
    return jnp.einsum("qhcgp,hg->qgcp", d.reshape(S5_Q, 8, S5_GROUP, 8, S5_STATE), _eye8(d.dtype)).reshape(
        S5_GROUPS, S5_GROUP, S5_STATE)


def _quarter_c(c):
    return jnp.einsum("gh,qgcp->qgphc", _eye8(c.dtype), c.reshape(S5_Q, 8, S5_GROUP, S5_STATE)).reshape(S5_Q, S5_QS, S5_QL)


def _unquarter_c(d):
    return jnp.einsum("qgphc,gh->qgcp", d.reshape(S5_Q, 8, S5_STATE, 8, S5_GROUP), _eye8(d.dtype)).reshape(
        S5_GROUPS, S5_GROUP, S5_STATE)


def _local_step(x, p, target, W, late_weights=None, grads_ready=None, first_dep=None):
    L = x.shape[0]
    r2 = lambda v: v.reshape(1, -1)
    w_in = W["w_in"]
    w2pad = jnp.pad(W["rw_w2"], ((0, 64), (0, 0)))
    a2pad = jnp.pad(W["rw_a2"], ((64, 0), (0, 0)))
    mu = r2(W["rw_shift_mu"])
    rw_vec = [r2(W[n]) for n in ("rw_w0", "rw_a0", "rw_k_k", "rw_k_a")]
    ln_w, ln_b, r_k = r2(W["rw_ln_w"]), r2(W["rw_ln_b"]), r2(W["rw_r_k"])

    lam_re, lam_im = W["s5_lam_re"], W["s5_lam_im"]
    log_step = W["s5_log_step"].reshape(S5_GROUPS, 1)
    bt_re, bt_im = W["s5_b_re"].transpose(0, 2, 1), W["s5_b_im"].transpose(0, 2, 1)
    lb_re, lb_im, bb_re, bb_im = _s5_param_fwd(lam_re, lam_im, log_step, bt_re, bt_im)
    bq_re, bq_im = _quarter_b(bb_re).astype(BF16), _quarter_b(bb_im).astype(BF16)
    cq_re, cq_im = _quarter_c(W["s5_c_re"]).astype(BF16), _quarter_c(W["s5_c_im"]).astype(BF16)
    lbar = jnp.concatenate([lb_re.reshape(1, -1), lb_im.reshape(1, -1), jnp.zeros((6, S5_LANES), F32)], axis=0)
    dskip = r2(W["s5_d"])
    glu_b = r2(W["s5_glu_b"])
    norm_mix, norm_ffn, norm_ple, final_norm = (r2(W[n]) for n in ("norm_mix", "norm_ffn", "norm_ple", "final_norm"))

    proj, xn = _inproj_fwd(x, norm_mix, w_in, L, () if first_dep is None else (first_dep,))
    y_s5, ck5 = _s5_scan_fwd(proj, bq_re, bq_im, cq_re, cq_im, lbar, dskip, L, TB)
    s5_out = _s5_post_fwd(y_s5, W["s5_glu_w"], glu_b, L)
    r, wd, kf, v, a_s, b_s, g = _rw_pre_fwd(proj, mu, *rw_vec, w2pad, a2pad, W["rw_g2"], L)
    scan_in = (r, wd, kf, v, a_s, b_s)
    y_wkv, ckw = _wkv_fwd(*scan_in, L)
    rw_out = _rw_post_fwd(y_wkv, r, kf, v, g, ln_w, ln_b, r_k, L)
    if late_weights is not None:
        W = dict(W, **late_weights(rw_out))
    wtop, wbot = W["w_out"][:S5_WIDTH], W["w_out"][S5_WIDTH:]
    h1, h2 = _mixffn_fwd(x, s5_out, rw_out, wtop, wbot, norm_ffn, W["ffn_w1"], W["ffn_w3"], W["ffn_w2"], L)

    G = {}
    dh2, dh2_bf, loss_acc, G["norm_ple"], G["final_norm"], G["ple_gate_w"], G["ple_up_w"] = _ple_loss_fb(
        h2, p, target, norm_ple, final_norm, W["ple_gate_w"], W["ple_up_w"], L)
    dh1, da1, da3, hm, hn_ffn, G["norm_ffn"] = _ffn_bwd(h1, dh2, norm_ffn, W["ffn_w1"], W["ffn_w3"], W["ffn_w2"], L)
    G["ffn_w1"] = _mm_tn("dw_ffn_w1", hn_ffn, da1)
    G["ffn_w3"] = _mm_tn("dw_ffn_w3", hn_ffn, da3)
    G["ffn_w2"] = _mm_tn("dw_ffn_w2", hm, dh2_bf)
    dep_a = grads_ready(0, G) if grads_ready is not None else None
    dy_s5, G["s5_glu_b"], G["s5_glu_w"], d_wtop = _s5_post_bwd(y_s5, dh1, W["s5_glu_w"], glu_b, wtop, L,
                                                               () if dep_a is None else (dep_a,))
    dy_wkv, dr2, dk2, dv2, dg, G["rw_ln_w"], G["rw_ln_b"], G["rw_r_k"], d_wbot = _rw_post_bwd(
        y_wkv, r, kf, v, g, dh1, ln_w, ln_b, r_k, wbot, L)
    G["w_out"] = jnp.concatenate([d_wtop, d_wbot], axis=0)
    dep = grads_ready(1, G) if grads_ready is not None else None
    dr1, dwd, dk1, dv1, da_s, db_s = _wkv_bwd(*scan_in, dy_wkv, ckw, L, () if dep is None else (dep,))
    (dzs, G["rw_shift_mu"], G["rw_w0"], G["rw_a0"], G["rw_k_k"], G["rw_k_a"], d_w2pad, d_a2pad, G["rw_g2"]) = _rw_pre_bwd(
        proj, (dr1, dr2, dwd, dk1, dk2, dv1, dv2, da_s, db_s, dg), mu, *rw_vec, w2pad, a2pad, W["rw_g2"], L)
    G["rw_w2"], G["rw_a2"] = d_w2pad[:64], d_a2pad[64:]
    du, dbq_re, dbq_im, dcq_re, dcq_im, dlbar, G["s5_d"] = _s5_scan_bwd(
        proj, dy_s5, ck5, bq_re, bq_im, cq_re, cq_im, lbar, dskip, L, TB)
    G["s5_c_re"], G["s5_c_im"] = _unquarter_c(dcq_re), _unquarter_c(dcq_im)
    d_lam_re, d_lam_im, d_ls, d_bt_re, d_bt_im = _s5_param_bwd(
        lam_re, lam_im, log_step, bt_re, bt_im, dlbar[0].reshape(S5_GROUPS, S5_STATE), dlbar[1].reshape(S5_GROUPS, S5_STATE),
        _unquarter_b(dbq_re), _unquarter_b(dbq_im))
    G["s5_lam_re"], G["s5_lam_im"], G["s5_log_step"] = d_lam_re, d_lam_im, d_ls.reshape(S5_GROUPS)
    G["s5_b_re"], G["s5_b_im"] = d_bt_re.transpose(0, 2, 1), d_bt_im.transpose(0, 2, 1)
    dx, dproj, G["norm_mix"] = _inproj_bwd(x, dh1, du, dzs, norm_mix, mu, w_in[:, :S5_WIDTH], w_in[:, S5_WIDTH:], L)
    G["w_in"] = _mm_tn("dw_in", xn, dproj)
    return loss_acc[0, 0], dx, G


def _all_gather(name, shards):
    nt = len(shards)

    def body(*refs):
        x_refs, out_refs = refs[:nt], refs[nt:2 * nt]
        send_sems, recv_sems, local_sems = refs[2 * nt:]
        x, y, c = lax.axis_index("x"), lax.axis_index("y"), lax.axis_index("c")
        me, sibling = (x, y, c), (x, y, 1 - c)
        chips = [(1 - x, y), (x, 1 - y), (1 - x, 1 - y)]

        def rows(t, px, py, pc):
            m_per = shards[t].shape[0]
            return out_refs[t].at[pl.ds((4 * px + 2 * py + pc) * m_per, m_per), :]

        def copy(t, k, block, to, src=None):
            return pltpu.make_async_remote_copy(
                src_ref=rows(t, *block) if src is None else src, dst_ref=rows(t, *block),
                send_sem=send_sems.at[7 * t + k], recv_sem=recv_sems.at[7 * t + k],
                device_id=to, device_id_type=pl.DeviceIdType.MESH)

        mine = [pltpu.make_async_copy(x_refs[t], rows(t, *me), local_sems.at[t]) for t in range(nt)]
        for cp in mine:
            cp.start()
        first = []
        for t in range(nt):
            first.append(copy(t, 0, me, sibling, src=x_refs[t]))
            first += [copy(t, 1 + j, me, (*chip, c), src=x_refs[t]) for j, chip in enumerate(chips)]
        for cp in first:
            cp.start()
        passed = []
        for t in range(nt):
            for j, chip in enumerate(chips):
                copy(t, 1 + j, (*chip, c), me).wait_recv()
                fwd = copy(t, 4 + j, (*chip, c), sibling)
                fwd.start()
                passed.append(fwd)
        for t in range(nt):
            copy(t, 0, sibling, me).wait_recv()
            for j, chip in enumerate(chips):
                copy(t, 4 + j, (*chip, 1 - c), me).wait_recv()
        for cp in first + passed:
            cp.wait_send()
        for cp in mine:
            cp.wait()

    return _pcall(body, name=name,
                  out_shape=[jax.ShapeDtypeStruct((N_DEV * a.shape[0], a.shape[1]), a.dtype) for a in shards],
                  in_specs=[_ANY] * nt, out_specs=[_ANY] * nt,
                  scratch_shapes=[pltpu.SemaphoreType.DMA((7 * nt,)), pltpu.SemaphoreType.DMA((7 * nt,)),
                                  pltpu.SemaphoreType.DMA((nt,))])(*shards)


_HBM = pl.BlockSpec(memory_space=pltpu.HBM)
_SEM = pl.BlockSpec(memory_space=pltpu.SEMAPHORE)
_EFFECT = pltpu.SideEffectType.DATAFLOW_SIDE_EFFECTING


def _peer_of(k):
    x, y, c = lax.axis_index("x"), lax.axis_index("y"), lax.axis_index("c")
    px, py, pc = x ^ ((k >> 2) & 1), y ^ ((k >> 1) & 1), c ^ (k & 1)
    return (px, py, pc), 4 * px + 2 * py + pc, 4 * x + 2 * y + c


def _direct_copy(t, k, src_refs, land_refs, send_sems, recv_sems, rows_of, gather):
    dev, peer, me = _peer_of(k)
    m = rows_of[t]
    src = src_refs[t] if gather else src_refs[t].at[pl.ds(peer * m, m), :]
    return pltpu.make_async_remote_copy(
        src_ref=src, dst_ref=land_refs[t].at[pl.ds(me * m, m), :],
        send_sem=send_sems.at[7 * t + k - 1], recv_sem=recv_sems.at[7 * t + k - 1],
        device_id=dev, device_id_type=pl.DeviceIdType.MESH)


def _direct_landing(t, k, src_refs, land_refs, send_sems, recv_sems, rows_of, gather):
    dev, peer, me = _peer_of(k)
    m = rows_of[t]
    src = src_refs[t] if gather else src_refs[t].at[pl.ds(me * m, m), :]
    return pltpu.make_async_remote_copy(
        src_ref=src, dst_ref=land_refs[t].at[pl.ds(peer * m, m), :],
        send_sem=send_sems.at[7 * t + k - 1], recv_sem=recv_sems.at[7 * t + k - 1],
        device_id=dev, device_id_type=pl.DeviceIdType.MESH)


def _direct_start(name, srcs, gather, dep=None):
    nt = len(srcs)
    rows_of = [a.shape[0] if gather else a.shape[0] // N_DEV for a in srcs]
    lands = [pltpu.with_memory_space_constraint(lax.empty((N_DEV * m, a.shape[1]), a.dtype), pltpu.HBM)
             for a, m in zip(srcs, rows_of)]

    n_dep = 0 if dep is None else 1

    def body(*refs):
        src_refs, land_refs = refs[:nt], refs[nt:2 * nt]
        send_sems, recv_sems = refs[2 * nt + n_dep], refs[2 * nt + n_dep + 1]
        token = refs[-1]
        for t in range(nt):
            for k in range(1, N_DEV):
                _direct_copy(t, k, src_refs, land_refs, send_sems, recv_sems, rows_of, gather).start()
        token[...] = jnp.zeros(token.shape, F32)

    out = _pcall(
        body, name=name,
        out_shape=(pltpu.SemaphoreType.DMA((7 * nt,)), pltpu.SemaphoreType.DMA((7 * nt,)),
                   *[pltpu.HBM(a.shape, a.dtype) for a in srcs], *[pltpu.HBM(a.shape, a.dtype) for a in lands],
                   jax.ShapeDtypeStruct((8, 128), F32)),
        in_specs=(_HBM,) * (2 * nt) + (pl.BlockSpec(memory_space=pl.ANY),) * n_dep,
        out_specs=(_SEM, _SEM) + (_HBM,) * (2 * nt) + (pl.BlockSpec(memory_space=pltpu.VMEM),),
        input_output_aliases={i: 2 + i for i in range(2 * nt)},
        compiler_params=pltpu.CompilerParams(has_side_effects=_EFFECT),
    )(*[pltpu.with_memory_space_constraint(a, pltpu.HBM) for a in srcs], *lands, *(() if dep is None else (dep,)))
    return (out[0], out[1], list(out[2:2 + nt]), list(out[2 + nt:2 + 2 * nt]), rows_of, gather), out[-1]


def _direct_wait(name, handle, after):
    send_sems, recv_sems, srcs, lands, rows_of, gather = handle
    nt = len(srcs)
    after = list(after) if isinstance(after, (list, tuple)) else [after]

    def body(*refs):
        src_refs, land_refs = refs[:nt], refs[nt:2 * nt]
        s_sems, r_sems = refs[2 * nt], refs[2 * nt + 1]
        for t in range(nt):
            for k in range(1, N_DEV):
                _direct_copy(t, k, src_refs, land_refs, s_sems, r_sems, rows_of, gather).wait_send()
                _direct_landing(t, k, src_refs, land_refs, s_sems, r_sems, rows_of, gather).wait_recv()

    out = _pcall(
        body, name=name,
        out_shape=tuple(pltpu.HBM(a.shape, a.dtype) for a in srcs) + tuple(pltpu.HBM(a.shape, a.dtype) for a in lands),
        in_specs=(_HBM,) * (2 * nt) + (_SEM, _SEM) + (pl.BlockSpec(memory_space=pl.ANY),) * len(after),
        out_specs=(_HBM,) * (2 * nt),
        input_output_aliases={i: i for i in range(2 * nt)},
        compiler_params=pltpu.CompilerParams(has_side_effects=_EFFECT),
    )(*srcs, *lands, send_sems, recv_sems, *after)
    return list(out[:nt]), list(out[nt:])


def _adamw_sharded(name, own, parts, w, m, v, rb, deps=()):
    R, N = own.shape

    def body(o_ref, p_ref, w_ref, m_ref, v_ref, *rest):
        g_ref, d_ref, nm_ref, nv_ref = rest[len(deps):]
        me = 4 * lax.axis_index("x") + 2 * lax.axis_index("y") + lax.axis_index("c")
        g = o_ref[...]
        for k in range(1, N_DEV):
            g = g + p_ref[me ^ k].astype(F32)
        nm = ADAM_B1 * m_ref[...] + (1.0 - ADAM_B1) * g
        nv = ADAM_B2 * v_ref[...] + (1.0 - ADAM_B2) * (g * g)
        m_hat = nm / (1.0 - ADAM_B1 ** ADAM_STEP)
        v_hat = nv / (1.0 - ADAM_B2 ** ADAM_STEP)
        g_ref[...] = g
        d_ref[...] = -ADAM_LR * (m_hat / (jnp.sqrt(v_hat) + ADAM_EPS) + ADAM_WD * w_ref[...])
        nm_ref[...] = nm
        nv_ref[...] = nv

    blk = pl.BlockSpec((rb, N), lambda i: (i, 0))
    sh = jax.ShapeDtypeStruct((R, N), F32)
    return _pcall(body, name=name, grid=(R // rb,),
                  in_specs=[blk, pl.BlockSpec((N_DEV, rb, N), lambda i: (0, i, 0)), blk, blk, blk]
                  + [pl.BlockSpec(d.shape, lambda i, nd=d.ndim: (0,) * nd) for d in deps],
                  out_specs=[blk] * 4, out_shape=[sh] * 4, compiler_params=_cparams(1))(own, parts, w, m, v, *deps)


LOSS_SLOT = "loss_partials"
SMALL_CLASSES = (
    (("s5_b_re", 32, 1024), ("s5_b_im", 32, 1024),
     ("norm_mix", 1, 1024), ("norm_ffn", 1, 1024), ("norm_ple", 1, 1024), ("final_norm", 1, 1024)),
    (("s5_d", 1, 512), ("s5_glu_b", 1, 512), ("rw_w0", 1, 512), ("rw_a0", 1, 512), ("rw_k_k", 1, 512), ("rw_k_a", 1, 512),
     ("rw_ln_w", 1, 512), ("rw_ln_b", 1, 512), ("rw_r_k", 1, 512)),
    (("rw_shift_mu", 1, 1792),),
    (("s5_lam_re", 32, 64), ("s5_lam_im", 32, 64), ("s5_c_re", 512, 64), ("s5_c_im", 512, 64)),
    (("s5_log_step", 1, 32), (LOSS_SLOT, 1, 32)),
)


def _class_rows(cls):
    return -(-sum(r for _, r, _ in cls) // 8) * 8


def _stack_class(cls, arrs):
    a = jnp.concatenate(arrs, axis=0) if len(arrs) > 1 else arrs[0]
    pad = _class_rows(cls) - a.shape[0]
    return jnp.pad(a, ((0, pad), (0, 0))) if pad else a


def _adamw_small(grads, w, m, v):
    names = [n for cls in SMALL_CLASSES for n, _, _ in cls]
    n_cls, n_par = len(SMALL_CLASSES), len(names)

    def body(*refs):
        g_refs = refs[:n_cls]
        w_refs, m_refs, v_refs = (refs[n_cls + i * n_par:n_cls + (i + 1) * n_par] for i in range(3))
        o_refs = refs[n_cls + 3 * n_par:]
        p = 0
        for cls, g_ref in zip(SMALL_CLASSES, g_refs):
            rc = _class_rows(cls)
            tot = g_ref[0:rc, :]
            for s_ in range(1, N_DEV):
                tot = tot + g_ref[s_ * rc:(s_ + 1) * rc, :]
            off = 0
            for _, r, _ in cls:
                g = tot[off:off + r, :]
                off += r
                nm = ADAM_B1 * m_refs[p][...] + (1.0 - ADAM_B1) * g
                nv = ADAM_B2 * v_refs[p][...] + (1.0 - ADAM_B2) * (g * g)
                m_hat = nm / (1.0 - ADAM_B1 ** ADAM_STEP)
                v_hat = nv / (1.0 - ADAM_B2 ** ADAM_STEP)
                o_refs[4 * p][...] = g
                o_refs[4 * p + 1][...] = -ADAM_LR * (m_hat / (jnp.sqrt(v_hat) + ADAM_EPS) + ADAM_WD * w_refs[p][...])
                o_refs[4 * p + 2][...] = nm
                o_refs[4 * p + 3][...] = nv
                p += 1

    shapes = [(r, c) for cls in SMALL_CLASSES for _, r, c in cls]
    out = _pcall(body, name="adamw_replicated",
                 out_shape=[jax.ShapeDtypeStruct(sh, F32) for sh in shapes for _ in range(4)],
                 compiler_params=pltpu.CompilerParams(vmem_limit_bytes=VMEM_LIMIT))(*grads, *w, *m, *v)
    return {n: out[4 * i:4 * i + 4] for i, n in enumerate(names)}


EARLY = (("w_in", True),)
LATE = (("ffn_w1", True), ("ffn_w3", True), ("ffn_w2", False), ("ple_gate_w", False), ("w_out", False))
GRAD_STAGES = (LATE[:4], LATE[4:])
MISC = (("s5_glu_w", False), ("rw_w2", True), ("rw_a2", True), ("rw_g2", True), ("ple_up_w", True))
SHARDED_NAMES = tuple(n for n, _ in EARLY + LATE + MISC)
PACK_COLS = 1024
WEIGHT_NAMES = ("norm_mix", "w_in", "s5_lam_re", "s5_lam_im", "s5_log_step", "s5_b_re", "s5_b_im", "s5_c_re", "s5_c_im", "s5_d",
                "s5_glu_w", "s5_glu_b", "rw_shift_mu", "rw_w0", "rw_w2", "rw_a0", "rw_a2", "rw_g2", "rw_k_k", "rw_k_a", "rw_r_k",
                "rw_ln_w", "rw_ln_b", "w_out", "norm_ffn", "ffn_w1", "ffn_w3", "ffn_w2", "norm_ple", "ple_gate_w", "ple_up_w",
                "final_norm")
SMALL_NAMES = tuple(n for n in WEIGHT_NAMES if n not in SHARDED_NAMES)
ARG_NAMES = ("x", "p") + WEIGHT_NAMES + ("loss_target",) + tuple("m_" + n for n in WEIGHT_NAMES) + tuple("v_" + n for n in WEIGHT_NAMES)


def _travel(a, tr):
    return a.T if tr else a


def _pack_misc(blocks):
    lead = blocks[0].shape[:-2]
    return jnp.concatenate([b.reshape(lead + (-1, PACK_COLS)) for b in blocks], axis=len(lead))


def _unpack_misc(packed, shapes):
    lead = packed.shape[:-2]
    out, off = [], 0
    for r, c in shapes:
        n = r * c // PACK_COLS
        out.append(lax.slice_in_dim(packed, off, off + n, axis=len(lead)).reshape(lead + (r, c)))
        off += n
    return out


def _kernel_impl(ins):
    x, p, target = ins["x"][0], ins["p"][0, 0], ins["loss_target"][0]
    me = 4 * lax.axis_index("x") + 2 * lax.axis_index("y") + lax.axis_index("c")
    small = {n: (ins[n] if n == "final_norm" else ins[n][0]) for n in SMALL_NAMES}
    trav = lambda pre, n, tr: _travel(ins[pre + n][0], tr)
    misc_shapes = [trav("", n, tr).shape for n, tr in MISC]

    early = _all_gather("ag_early", [trav("", n, tr).astype(BF16) for n, tr in EARLY]
                        + [_pack_misc([trav("", n, tr).astype(BF16) for n, tr in MISC])])
    late_handle, late_token = _direct_start("ag_late_start", [trav("", n, tr).astype(BF16) for n, tr in LATE], True, early[-1])
    W = dict(small)
    for (n, tr), g in zip(EARLY, early):
        W[n] = _travel(g, tr)
    for (n, tr), g in zip(MISC, _unpack_misc(early[-1].reshape(N_DEV, -1, PACK_COLS), misc_shapes)):
        W[n] = _travel(g.reshape(-1, g.shape[-1]), tr)

    def late_weights(after):
        shards, lands = _direct_wait("ag_late_wait", late_handle, after)
        full = [lax.dynamic_update_slice_in_dim(ld, sh, me * sh.shape[0], axis=0) for ld, sh in zip(lands, shards)]
        return {n: _travel(g, tr) for (n, tr), g in zip(LATE, full)}

    gt = lambda G, n, tr: _travel(G[n], tr)
    started = {}

    def grads_ready(stage, G):
        full = [gt(G, n, tr) for n, tr in GRAD_STAGES[stage]]
        started[stage] = (full, *_direct_start("grad_late_start%d" % stage, [a.astype(BF16) for a in full], False))
        return started[stage][2]

    loss_part, dx, G = _local_step(x, p, target, W, late_weights, grads_ready, late_token)

    misc_g = _pack_misc([gt(G, n, tr).reshape((N_DEV,) + shp) for (n, tr), shp in zip(MISC, misc_shapes)])
    early_full = [gt(G, n, tr) for n, tr in EARLY] + [misc_g.reshape(-1, PACK_COLS)]
    early_handle, early_token = _direct_start("grad_early_start", [a.astype(BF16) for a in early_full], False)
    view2 = lambda a, r, c: a.reshape(r, c)
    G[LOSS_SLOT] = jnp.full((1, 32), loss_part, F32)
    small_own = [_stack_class(cls, [view2(G[n], r, c) for n, r, c in cls]) for cls in SMALL_CLASSES]
    small_handle, small_token = _direct_start("grad_small_start", small_own, True)
    late_src, late_land = [], []
    for stage in range(len(GRAD_STAGES)):
        full, handle, _ = started[stage]
        _, land = _direct_wait("grad_late_wait%d" % stage, handle, small_token)
        late_src += full
        late_land += land

    outs = {}

    def emit(names_shapes, res):
        for tag, val in zip(("grad_", "delta_", "new_m_", "new_v_"), res):
            for n, v in names_shapes(val):
                outs[tag + n] = v

    def sharded_update(n, tr, src, land, deps=()):
        rows = src.shape[0] // N_DEV
        own = lax.dynamic_slice_in_dim(src, me * rows, rows, axis=0)
        res = _adamw_sharded("adamw_" + n, own, land.reshape(N_DEV, rows, land.shape[1]),
                             trav("", n, tr), trav("m_", n, tr), trav("v_", n, tr), _pick_rows(rows), deps)
        emit(lambda val: [(n, _travel(val, tr).reshape(ins[n].shape))], res)
        return list(res)

    for (n, tr), src, land in zip(LATE, late_src, late_land):
        sharded_update(n, tr, src, land, (early_token,))
    _, early_land = _direct_wait("grad_early_wait", early_handle, list(outs.values()))
    for (n, tr), src, land in zip(EARLY, early_full[:-1], early_land[:-1]):
        sharded_update(n, tr, src, land)
    pm = lambda pre: _pack_misc([trav(pre, n, tr) for n, tr in MISC])
    rows = early_full[-1].shape[0] // N_DEV
    res = _adamw_sharded("adamw_misc", lax.dynamic_slice_in_dim(early_full[-1], me * rows, rows, axis=0),
                         early_land[-1].reshape(N_DEV, rows, PACK_COLS), pm(""), pm("m_"), pm("v_"), rows)
    emit(lambda val: [(n, _travel(b, tr).reshape(ins[n].shape)) for (n, tr), b in zip(MISC, _unpack_misc(val, misc_shapes))], res)
    small_src, small_land = _direct_wait("grad_small_wait", small_handle, res[0])
    small_all = [lax.dynamic_update_slice_in_dim(ld, sr, me * sr.shape[0], axis=0) for ld, sr in zip(small_land, small_src)]
    flat_small = [(n, r, c) for cls in SMALL_CLASSES for n, r, c in cls]
    ins = dict(ins, **{pre + LOSS_SLOT: jnp.zeros((1, 32), F32) for pre in ("", "m_", "v_")})
    res = _adamw_small(small_all, *[[view2(ins[pre + n], r, c) for n, r, c in flat_small] for pre in ("", "m_", "v_")])
    loss = res.pop(LOSS_SLOT)[0][0, 0]
    for n, _, _ in flat_small[:-1]:
        for tag, val in zip(("grad_", "delta_", "new_m_", "new_v_"), res[n]):
            outs[tag + n] = val.reshape(ins[n].shape)
    res = [loss, dx[None]]
    for tag in ("grad_", "delta_", "new_m_", "new_v_"):
        res += [outs[tag + n] for n in WEIGHT_NAMES]
    return tuple(res)


def _pick_rows(r):
    best = 8
    for b in range(8, 257, 8):
        if r % b == 0:
            best = b
    return best


def kernel(x, p, norm_mix, w_in, s5_lam_re, s5_lam_im, s5_log_step, s5_b_re, s5_b_im, s5_c_re, s5_c_im, s5_d, s5_glu_w, s5_glu_b, rw_shift_mu, rw_w0, rw_w2, rw_a0, rw_a2, rw_g2, rw_k_k, rw_k_a, rw_r_k, rw_ln_w, rw_ln_b, w_out, norm_ffn, ffn_w1, ffn_w3, ffn_w2, norm_ple, ple_gate_w, ple_up_w, final_norm, loss_target, m_norm_mix, m_w_in, m_s5_lam_re, m_s5_lam_im, m_s5_log_step, m_s5_b_re, m_s5_b_im, m_s5_c_re, m_s5_c_im, m_s5_d, m_s5_glu_w, m_s5_glu_b, m_rw_shift_mu, m_rw_w0, m_rw_w2, m_rw_a0, m_rw_a2, m_rw_g2, m_rw_k_k, m_rw_k_a, m_rw_r_k, m_rw_ln_w, m_rw_ln_b, m_w_out, m_norm_ffn, m_ffn_w1, m_ffn_w3, m_ffn_w2, m_norm_ple, m_ple_gate_w, m_ple_up_w, m_final_norm, v_norm_mix, v_w_in, v_s5_lam_re, v_s5_lam_im, v_s5_log_step, v_s5_b_re, v_s5_b_im, v_s5_c_re, v_s5_c_im, v_s5_d, v_s5_glu_w, v_s5_glu_b, v_rw_shift_mu, v_rw_w0, v_rw_w2, v_rw_a0, v_rw_a2, v_rw_g2, v_rw_k_k, v_rw_k_a, v_rw_r_k, v_rw_ln_w, v_rw_ln_b, v_w_out, v_norm_ffn, v_ffn_w1, v_ffn_w3, v_ffn_w2, v_norm_ple, v_ple_gate_w, v_ple_up_w, v_final_norm):
    return _kernel_impl(dict(zip(ARG_NAMES, (x, p, norm_mix, w_in, s5_lam_re, s5_lam_im, s5_log_step, s5_b_re, s5_b_im, s5_c_re, s5_c_im, s5_d, s5_glu_w, s5_glu_b, rw_shift_mu, rw_w0, rw_w2, rw_a0, rw_a2, rw_g2, rw_k_k, rw_k_a, rw_r_k, rw_ln_w, rw_ln_b, w_out, norm_ffn, ffn_w1, ffn_w3, ffn_w2, norm_ple, ple_gate_w, ple_up_w, final_norm, loss_target, m_norm_mix, m_w_in, m_s5_lam_re, m_s5_lam_im, m_s5_log_step, m_s5_b_re, m_s5_b_im, m_s5_c_re, m_s5_c_im, m_s5_d, m_s5_glu_w, m_s5_glu_b, m_rw_shift_mu, m_rw_w0, m_rw_w2, m_rw_a0, m_rw_a2, m_rw_g2, m_rw_k_k, m_rw_k_a, m_rw_r_k, m_rw_ln_w, m_rw_ln_b, m_w_out, m_norm_ffn, m_ffn_w1, m_ffn_w3, m_ffn_w2, m_norm_ple, m_ple_gate_w, m_ple_up_w, m_final_norm, v_norm_mix, v_w_in, v_s5_lam_re, v_s5_lam_im, v_s5_log_step, v_s5_b_re, v_s5_b_im, v_s5_c_re, v_s5_c_im, v_s5_d, v_s5_glu_w, v_s5_glu_b, v_rw_shift_mu, v_rw_w0, v_rw_w2, v_rw_a0, v_rw_a2, v_rw_g2, v_rw_k_k, v_rw_k_a, v_rw_r_k, v_rw_ln_w, v_rw_ln_b, v_w_out, v_norm_ffn, v_ffn_w1, v_ffn_w3, v_ffn_w2, v_norm_ple, v_ple_gate_w, v_ple_up_w, v_final_norm))))
```

```python
import jax
import jax.numpy as jnp
from jax import lax
from jax.experimental import pallas as pl
from jax.experimental.pallas import tpu as pltpu

F32 = jnp.float32
BF16 = jnp.bfloat16

D_MODEL = 1024
S5_WIDTH = 512
RW_WIDTH = 512
S5_GROUP = 16
S5_GROUPS = 32
S5_STATE = 64
S5_LANES = S5_GROUPS * S5_STATE
HEAD = 64
SHIFT_COLS = 1792
IN_COLS = 2304
FFN_HIDDEN = 2816
PLE_DIM = 256
RMS_EPS = 1e-6
GN_EPS = 64e-5
L2_EPS = 1e-12
CHUNK = 64
N_DEV = 8

ADAM_LR = 0.001
ADAM_B1 = 0.9
ADAM_B2 = 0.999
ADAM_EPS = 1e-08
ADAM_WD = 0.01
ADAM_STEP = 10

VMEM_LIMIT = 56 * 1024 * 1024
_ANY = pl.BlockSpec(memory_space=pl.ANY)


def _pcall(body, **kw):
    return pl.pallas_call(body, **kw)


def _cparams(n_grid):
    return pltpu.CompilerParams(dimension_semantics=("arbitrary",) * n_grid, vmem_limit_bytes=VMEM_LIMIT)


def _dot(a, b):
    return jnp.dot(a, b, preferred_element_type=F32)


def _dot_nt(a, b):
    return lax.dot_general(a, b, (((1,), (1,)), ((), ())), preferred_element_type=F32)


def _dot_tn(a, b):
    return lax.dot_general(a, b, (((0,), (0,)), ((), ())), preferred_element_type=F32)


def _mmc(w, diff=True, tr=False):
    fw, bw = (_dot_nt, _dot) if tr else (_dot, _dot_nt)
    if not diff:
        return lambda x: fw(x.astype(BF16), w)

    @jax.custom_vjp
    def f(x):
        return fw(x.astype(BF16), w)

    def fwd(x):
        return fw(x.astype(BF16), w), None

    def bwd(_, dy):
        return (bw(dy.astype(BF16), w),)

    f.defvjp(fwd, bwd)
    return f


def _split_dot(x, m, n_split):
    acc = None
    rem = x
    for s in range(n_split):
        part = rem.astype(BF16)
        t = _dot(part, m)
        acc = t if acc is None else acc + t
        if s + 1 < n_split:
            rem = rem - part.astype(F32)
    return acc


def _segsum(m, diff=True):
    if not diff:
        return lambda x: _split_dot(x, m, 2)

    @jax.custom_vjp
    def f(x):
        return _split_dot(x, m, 2)

    def fwd(x):
        return _split_dot(x, m, 2), None

    def bwd(_, dy):
        return (_split_dot(dy, m, 2),)

    f.defvjp(fwd, bwd)
    return f


def _head_indicator(n):
    r = lax.broadcasted_iota(jnp.int32, (n, n), 0) // HEAD
    c = lax.broadcasted_iota(jnp.int32, (n, n), 1) // HEAD
    return (r == c).astype(BF16)


def _rms(x, g):
    return x * lax.rsqrt(jnp.mean(x * x, axis=-1, keepdims=True) + RMS_EPS) * g


def _softplus(x):
    return jnp.maximum(x, 0.0) + jnp.log(1.0 + jnp.exp(-jnp.abs(x)))


def _sigmoid(x):
    return 1.0 / (1.0 + jnp.exp(-x))


def _gelu(x):
    return 0.5 * x * (1.0 + jnp.tanh(0.7978845608028654 * (x + 0.044715 * (x * x * x))))


def _tok_call(name, fn, L, TB, tok_in, const_in, tok_out, acc_out=(), deps=()):
    nb = L // TB
    g8 = TB // 8
    in_specs, args = [], []
    for spec in tok_in:
        if len(spec) == 1:
            arr = spec[0]
            in_specs.append(pl.BlockSpec((arr.shape[0], TB, HEAD), lambda i: (0, i, 0)))
            args.append(arr)
            continue
        arr, width, cb = spec[:3]
        mode = spec[3] if len(spec) > 3 else None
        if mode is None:
            in_specs.append(pl.BlockSpec((TB, width), lambda i, cb=cb: (i, cb)))
        elif mode == "prev":
            in_specs.append(pl.BlockSpec((8, width), lambda i, cb=cb: (jnp.maximum(i * g8 - 1, 0), cb)))
        else:
            in_specs.append(pl.BlockSpec((8, width), lambda i, cb=cb: (jnp.minimum((i + 1) * g8, L // 8 - 1), cb)))
        args.append(arr)
    for c in const_in:
        in_specs.append(pl.BlockSpec(c.shape, lambda i, nd=c.ndim: (0,) * nd, pipeline_mode=pl.Buffered(1)))
        args.append(c)
    for d in deps:
        in_specs.append(pl.BlockSpec(d.shape, lambda i, nd=d.ndim: (0,) * nd))
        args.append(d)
    out_shape, out_specs = [], []
    for width, dt in tok_out:
        if width == "heads":
            out_shape.append(jax.ShapeDtypeStruct((N_HEAD, L, HEAD), dt))
            out_specs.append(pl.BlockSpec((N_HEAD, TB, HEAD), lambda i: (0, i, 0)))
            continue
        out_shape.append(jax.ShapeDtypeStruct((L, width), dt))
        out_specs.append(pl.BlockSpec((TB, width), lambda i: (i, 0)))
    for shp in acc_out:
        out_shape.append(jax.ShapeDtypeStruct(shp, F32))
        out_specs.append(pl.BlockSpec(shp, lambda i, nd=len(shp): (0,) * nd))
    n_tok, n_const, n_to = len(tok_in), len(const_in), len(tok_out)

    def body(*refs):
        i = pl.program_id(0)
        tv = [r[...] if len(r.shape) == 2 else jnp.concatenate([r[h] for h in range(r.shape[0])], axis=1)
              for r in refs[:n_tok]]
        cv = [r[...] for r in refs[n_tok:n_tok + n_const]]
        orefs = refs[n_tok + n_const + len(deps):]
        outs = fn(i, tv, cv)
        for r, v in zip(orefs[:n_to], outs[:n_to]):
            if len(r.shape) == 3:
                for h in range(r.shape[0]):
                    r[h] = v[:, h * HEAD:(h + 1) * HEAD].astype(r.dtype)
            else:
                r[...] = v.astype(r.dtype)
        for r, v in zip(orefs[n_to:], outs[n_to:]):
            @pl.when(i == 0)
            def _(r=r):
                r[...] = jnp.zeros(r.shape, r.dtype)

            r[...] += v

    res = _pcall(body, name=name, grid=(nb,), in_specs=in_specs, out_specs=out_specs, out_shape=out_shape,
                 compiler_params=_cparams(1))(*args)
    return res


def _pick_block(n, cap):
    best = None
    for b in range(128, min(n, cap) + 1, 128):
        if n % b == 0:
            best = b
    return best if best is not None else n


def _mm_tn(name, a, b):
    T, M = a.shape
    N = b.shape[1]
    bm, bn, bt = _pick_block(M, 1536), _pick_block(N, 1536), _pick_block(T, 1024)

    def body(a_ref, b_ref, o_ref):
        t = pl.program_id(2)

        @pl.when(t == 0)
        def _():
            o_ref[...] = jnp.zeros(o_ref.shape, F32)

        o_ref[...] += _dot_tn(a_ref[...].astype(BF16), b_ref[...].astype(BF16))

    return _pcall(body, name=name, grid=(M // bm, N // bn, T // bt),
                  in_specs=[pl.BlockSpec((bt, bm), lambda m, n, t: (t, m)), pl.BlockSpec((bt, bn), lambda m, n, t: (t, n))],
                  out_specs=pl.BlockSpec((bm, bn), lambda m, n, t: (m, n)),
                  out_shape=jax.ShapeDtypeStruct((M, N), F32), compiler_params=_cparams(3))(a, b)


def _s5_param_fn(lam_re, lam_im, log_step, bt_re, bt_im):
    dt = jnp.exp(log_step)
    e = jnp.exp(lam_re * dt)
    lb_re = e * jnp.cos(lam_im * dt)
    lb_im = e * jnp.sin(lam_im * dt)
    den = lam_re * lam_re + lam_im * lam_im
    nr, ni = lb_re - 1.0, lb_im
    co_re = (nr * lam_re + ni * lam_im) / den
    co_im = (ni * lam_re - nr * lam_im) / den
    cr, ci = co_re[:, None, :], co_im[:, None, :]
    return lb_re, lb_im, cr * bt_re - ci * bt_im, cr * bt_im + ci * bt_re


def _s5_param_fwd(lam_re, lam_im, log_step, bt_re, bt_im):
    def body(a, b, c, d, e, o1, o2, o3, o4):
        r = _s5_param_fn(a[...], b[...], c[...], d[...], e[...])
        o1[...], o2[...], o3[...], o4[...] = r

    sh = jax.ShapeDtypeStruct
    return _pcall(body, name="s5_param_fwd",
                  out_shape=[sh(lam_re.shape, F32), sh(lam_re.shape, F32), sh(bt_re.shape, F32), sh(bt_re.shape, F32)])(
        lam_re, lam_im, log_step, bt_re, bt_im)


def _s5_param_bwd(lam_re, lam_im, log_step, bt_re, bt_im, d_lb_re, d_lb_im, d_bb_re, d_bb_im):
    def body(a, b, c, d, e, g1, g2, g3, g4, o1, o2, o3, o4, o5):
        _, vjp = jax.vjp(_s5_param_fn, a[...], b[...], c[...], d[...], e[...])
        r = vjp((g1[...], g2[...], g3[...], g4[...]))
        o1[...], o2[...], o3[...], o4[...], o5[...] = r

    sh = jax.ShapeDtypeStruct
    return _pcall(body, name="s5_param_bwd",
                  out_shape=[sh(lam_re.shape, F32), sh(lam_re.shape, F32), sh(log_step.shape, F32),
                             sh(bt_re.shape, F32), sh(bt_re.shape, F32)])(
        lam_re, lam_im, log_step, bt_re, bt_im, d_lb_re, d_lb_im, d_bb_re, d_bb_im)


def _cmul(ar, ai, br, bi):
    return ar * br - ai * bi, ar * bi + ai * br


def _scan_consts(lr, li, reverse):
    n = lr.shape[1]
    sub = lax.broadcasted_iota(jnp.int32, (8, n), 0)
    pows = [(lr, li)]
    for _ in range(7):
        pows.append(_cmul(pows[-1][0], pows[-1][1], lr, li))
    steps = []
    for s in (1, 2, 4):
        m = (sub < 8 - s) if reverse else (sub >= s)
        pr, pi = pows[s - 1]
        steps.append((s, jnp.where(m, jnp.broadcast_to(pr, (8, n)), 0.0), jnp.where(m, jnp.broadcast_to(pi, (8, n)), 0.0)))
    wr = jnp.zeros((8, n), F32)
    wi = jnp.zeros((8, n), F32)
    for r in range(8):
        e = (8 - r) if reverse else (r + 1)
        wr = jnp.where(sub == r, jnp.broadcast_to(pows[e - 1][0], (8, n)), wr)
        wi = jnp.where(sub == r, jnp.broadcast_to(pows[e - 1][1], (8, n)), wi)
    return steps, wr, wi


S5_Q = 4
S5_QL = S5_WIDTH // S5_Q
S5_QS = S5_LANES // S5_Q
S5_NT = S5_LANES // 128
S5_QT = S5_QS // 128


def _s5_power_table(lb_ref, pw_re, pw_im, seg):
    for j in range(S5_NT):
        lr = jnp.broadcast_to(lb_ref[0:1, j * 128:(j + 1) * 128], (8, 128))
        li = jnp.broadcast_to(lb_ref[1:2, j * 128:(j + 1) * 128], (8, 128))

        def step(i, c, lr=lr, li=li, j=j):
            pw_re[j, i] = c[0]
            pw_im[j, i] = c[1]
            return _cmul(c[0], c[1], lr, li)

        lax.fori_loop(0, seg, step, (lr, li))


def _seg_scan(sre, sim, carry, lb_ref, pw_re, pw_im, rows, reverse):
    seg = rows // 8
    sgn = -1.0 if reverse else 1.0
    sub = lax.broadcasted_iota(jnp.int32, (8, 128), 0)
    rows_at = lambda i: pl.ds(pl.multiple_of(i * 8, 8), 8)
    entering = {}
    half_tiles = S5_NT // 2
    for half in range(2):
        tiles = list(range(half * half_tiles, (half + 1) * half_tiles))
        lam8 = [(jnp.broadcast_to(lb_ref[0:1, j * 128:(j + 1) * 128], (8, 128)),
                 sgn * jnp.broadcast_to(lb_ref[1:2, j * 128:(j + 1) * 128], (8, 128))) for j in tiles]

        def p1(ii, c):
            i = (seg - 1 - ii) if reverse else ii
            out = []
            for n, j in enumerate(tiles):
                lr, li = lam8[n]
                cr, ci = c[2 * n], c[2 * n + 1]
                nr = lr * cr - li * ci + sre[j, rows_at(i), :]
                ni = lr * ci + li * cr + sim[j, rows_at(i), :]
                sre[j, rows_at(i), :] = nr
                sim[j, rows_at(i), :] = ni
                out += [nr, ni]
            return tuple(out)

        ends = lax.fori_loop(0, seg, p1, tuple(jnp.zeros((8, 128), F32) for _ in range(2 * len(tiles))))
        cs = []
        for n, j in enumerate(tiles):
            ls = slice(j * 128, (j + 1) * 128)
            steps, wr, wi = _scan_consts(pw_re[j, seg - 1][0:1, :], sgn * pw_im[j, seg - 1][0:1, :], reverse)
            tr, ti = ends[2 * n], ends[2 * n + 1]
            for sft, pr, pi in steps:
                sh = (8 - sft) if reverse else sft
                yr, yi = pltpu.roll(tr, sh, 0), pltpu.roll(ti, sh, 0)
                tr, ti = tr + pr * yr - pi * yi, ti + pr * yi + pi * yr
            cin_r, cin_i = carry[0:1, ls], carry[1:2, ls]
            tr, ti = tr + wr * cin_r - wi * cin_i, ti + wr * cin_i + wi * cin_r
            edge_out, edge_in, sh = (0, 7, 7) if reverse else (7, 0, 1)
            carry[0:1, ls] = tr[edge_out:edge_out + 1, :]
            carry[1:2, ls] = ti[edge_out:edge_out + 1, :]
            cr = jnp.where(sub == edge_in, jnp.broadcast_to(cin_r, (8, 128)), pltpu.roll(tr, sh, 0))
            ci = jnp.where(sub == edge_in, jnp.broadcast_to(cin_i, (8, 128)), pltpu.roll(ti, sh, 0))
            cs += [cr, ci]
            entering[j] = (cr, ci)

        def p2(i, _, lo=0, hi=len(tiles)):
            k = (seg - 1 - i) if reverse else i
            for n, j in list(enumerate(tiles))[lo:hi]:
                pr, pi = pw_re[j, k], pw_im[j, k]
                cr, ci = cs[2 * n], cs[2 * n + 1]
                if reverse:
                    sre[j, rows_at(i), :] = sre[j, rows_at(i), :] + pr * cr + pi * ci
                    sim[j, rows_at(i), :] = sim[j, rows_at(i), :] + pr * ci - pi * cr
                else:
                    sre[j, rows_at(i), :] = sre[j, rows_at(i), :] + pr * cr - pi * ci
                    sim[j, rows_at(i), :] = sim[j, rows_at(i), :] + pr * ci + pi * cr
            return 0

        for lo in range(0, len(tiles), 4):
            lax.fori_loop(0, seg, lambda i, c, lo=lo: p2(i, c, lo, lo + 4), 0, unroll=2)
    return entering


class _SegIO:
    def __init__(self, hbm, buf, sems, rows, width, col0=0):
        self.hbm, self.buf, self.sems, self.rows, self.seg, self.width, self.col0 = hbm, buf, sems, rows, rows // 8, width, col0

    def _copies(self, blk, slot, to_vmem):
        out = []
        for r in range(8):
            h = self.hbm.at[pl.ds(blk * self.rows + r * self.seg, self.seg), pl.ds(self.col0, self.width)]
            v = self.buf.at[slot, :, r, :]
            out.append(pltpu.make_async_copy(h, v, self.sems.at[slot, r]) if to_vmem
                       else pltpu.make_async_copy(v, h, self.sems.at[slot, r]))
        return out

    def start(self, blk, slot, to_vmem):
        for cp in self._copies(blk, slot, to_vmem):
            cp.start()

    def wait(self, blk, slot, to_vmem):
        for cp in self._copies(blk, slot, to_vmem):
            cp.wait()

    def value(self, slot):
        return self.buf[slot].reshape(self.rows, self.width)

    def store(self, slot, val):
        self.buf[slot] = val.reshape(self.seg, 8, self.width)


def _seg_pipeline(i, nb, blk_of, ins, outs, compute):
    slot = i % 2

    @pl.when(i == 0)
    def _():
        for io in ins:
            io.start(blk_of(0), 0, True)

    @pl.when(i + 1 < nb)
    def _():
        for io in ins:
            io.start(blk_of(i + 1), 1 - slot, True)

    for io in ins:
        io.wait(blk_of(i), slot, True)

    @pl.when(i >= 2)
    def _():
        for io in outs:
            io.wait(blk_of(i - 2), slot, False)

    compute(slot)
    for io in outs:
        io.start(blk_of(i), slot, False)

    @pl.when(i == nb - 1)
    def _():
        for io in outs:
            if nb >= 2:
                io.wait(blk_of(i - 1), 1 - slot, False)
            io.wait(blk_of(i), slot, False)


def _s5_scan_fwd(proj, bq_re, bq_im, cq_re, cq_im, lbar, dskip, L, TB):
    nb = L // TB
    seg = TB // 8

    def body(u_hbm, bre, bim, cre, cim, lb_ref, d_ref, y_hbm, ck_ref, sre, sim, carry, pw_re, pw_im,
             ubuf, ybuf, sem_u, sem_y):
        i = pl.program_id(0)
        u_io = _SegIO(u_hbm, ubuf, sem_u, TB, S5_WIDTH)
        y_io = _SegIO(y_hbm, ybuf, sem_y, TB, S5_WIDTH)

        @pl.when(i == 0)
        def _():
            carry[...] = jnp.zeros(carry.shape, F32)
            _s5_power_table(lb_ref, pw_re, pw_im, seg)

        ck_ref[0] = carry[...]

        def compute(slot):
            u = u_io.value(slot)
            ub = u.astype(BF16)
            for q in range(S5_Q):
                uq = ub[:, q * S5_QL:(q + 1) * S5_QL]
                vr, vi = _dot(uq, bre[q]), _dot(uq, bim[q])
                for jj in range(S5_QT):
                    sre[q * S5_QT + jj] = vr[:, jj * 128:(jj + 1) * 128]
                    sim[q * S5_QT + jj] = vi[:, jj * 128:(jj + 1) * 128]
            _seg_scan(sre, sim, carry, lb_ref, pw_re, pw_im, TB, False)
            ys = []
            for q in range(S5_Q):
                sl = slice(q * S5_QL, (q + 1) * S5_QL)
                sr = jnp.concatenate([sre[q * S5_QT + jj] for jj in range(S5_QT)], axis=1).astype(BF16)
                si = jnp.concatenate([sim[q * S5_QT + jj] for jj in range(S5_QT)], axis=1).astype(BF16)
                ys.append(_dot(sr, cre[q]) - _dot(si, cim[q]) + u[:, sl] * d_ref[:, sl])
            y_io.store(slot, jnp.concatenate(ys, axis=1))

        _seg_pipeline(i, nb, lambda st: st, [u_io], [y_io], compute)

    full = lambda a: pl.BlockSpec(a.shape, lambda i, nd=a.ndim: (0,) * nd)
    st = pltpu.VMEM((S5_NT, TB, 128), F32)
    pw = pltpu.VMEM((S5_NT, seg, 8, 128), F32)
    io = pltpu.VMEM((2, seg, 8, S5_WIDTH), F32)
    return _pcall(
        body, name="s5_scan_fwd", grid=(nb,),
        in_specs=[_ANY, full(bq_re), full(bq_im), full(cq_re), full(cq_im), full(lbar), full(dskip)],
        out_specs=[_ANY, pl.BlockSpec((1, 8, S5_LANES), lambda i: (i, 0, 0))],
        out_shape=[jax.ShapeDtypeStruct((L, S5_WIDTH), F32), jax.ShapeDtypeStruct((nb, 8, S5_LANES), F32)],
        scratch_shapes=[st, st, pltpu.VMEM((8, S5_LANES), F32), pw, pw, io, io,
                        pltpu.SemaphoreType.DMA((2, 8)), pltpu.SemaphoreType.DMA((2, 8))],
        compiler_params=_cparams(1))(proj, bq_re, bq_im, cq_re, cq_im, lbar, dskip)


def _s5_scan_bwd(proj, dy, ck, bq_re, bq_im, cq_re, cq_im, lbar, dskip, L, TB):
    nb = L // TB
    seg = TB // 8

    def body(u_hbm, dy_hbm, ck_ref, bre, bim, cre, cim, lb_ref, d_ref,
             du_hbm, dbre, dbim, dcre, dcim, dlb_ref, dd_ref, sre, sim, gre, gim, carry, gcarry, pw_re, pw_im,
             ubuf, dybuf, dubuf, sem_u, sem_dy, sem_du):
        i = pl.program_id(0)
        u_io = _SegIO(u_hbm, ubuf, sem_u, TB, S5_WIDTH)
        dy_io = _SegIO(dy_hbm, dybuf, sem_dy, TB, S5_WIDTH)
        du_io = _SegIO(du_hbm, dubuf, sem_du, TB, S5_WIDTH)

        @pl.when(i == 0)
        def _():
            gcarry[...] = jnp.zeros(gcarry.shape, F32)
            dbre[...] = jnp.zeros(dbre.shape, F32)
            dbim[...] = jnp.zeros(dbim.shape, F32)
            dcre[...] = jnp.zeros(dcre.shape, F32)
            dcim[...] = jnp.zeros(dcim.shape, F32)
            dlb_ref[...] = jnp.zeros(dlb_ref.shape, F32)
            dd_ref[...] = jnp.zeros(dd_ref.shape, F32)
            _s5_power_table(lb_ref, pw_re, pw_im, seg)

        def compute(slot):
            u = u_io.value(slot)
            dy_v = dy_io.value(slot)
            ub = u.astype(BF16)
            dyb = dy_v.astype(BF16)
            carry[...] = ck_ref[0]
            for q in range(S5_Q):
                uq = ub[:, q * S5_QL:(q + 1) * S5_QL]
                dq = dyb[:, q * S5_QL:(q + 1) * S5_QL]
                vr, vi = _dot(uq, bre[q]), _dot(uq, bim[q])
                hr, hi = _dot_nt(dq, cre[q]), -_dot_nt(dq, cim[q])
                for jj in range(S5_QT):
                    ls = slice(jj * 128, (jj + 1) * 128)
                    sre[q * S5_QT + jj] = vr[:, ls]
                    sim[q * S5_QT + jj] = vi[:, ls]
                    gre[q * S5_QT + jj] = hr[:, ls]
                    gim[q * S5_QT + jj] = hi[:, ls]
            entering = _seg_scan(sre, sim, carry, lb_ref, pw_re, pw_im, TB, False)
            _seg_scan(gre, gim, gcarry, lb_ref, pw_re, pw_im, TB, True)

            rows_at = lambda k: pl.ds(pl.multiple_of(k * 8, 8), 8)
            for half in range(2):
                tiles = list(range(half * (S5_NT // 2), (half + 1) * (S5_NT // 2)))
                acc0 = []
                for j in tiles:
                    er, ei = entering[j]
                    gr0, gi0 = gre[j, rows_at(0), :], gim[j, rows_at(0), :]
                    acc0 += [gr0 * er + gi0 * ei, gi0 * er - gr0 * ei]

                def acc_step(k, acc, tiles=tiles):
                    out = []
                    for n, j in enumerate(tiles):
                        gr, gi_ = gre[j, rows_at(k), :], gim[j, rows_at(k), :]
                        spr, spi = sre[j, rows_at(k - 1), :], sim[j, rows_at(k - 1), :]
                        out += [acc[2 * n] + gr * spr + gi_ * spi, acc[2 * n + 1] - gr * spi + gi_ * spr]
                    return tuple(out)

                acc = lax.fori_loop(1, seg, acc_step, tuple(acc0))
                for n, j in enumerate(tiles):
                    ls = slice(j * 128, (j + 1) * 128)
                    dlb_ref[0:1, ls] += jnp.sum(acc[2 * n], axis=0, keepdims=True)
                    dlb_ref[1:2, ls] += jnp.sum(acc[2 * n + 1], axis=0, keepdims=True)

            dd_ref[...] += jnp.sum(dy_v * u, axis=0, keepdims=True)
            dus = []
            for q in range(S5_Q):
                sl = slice(q * S5_QL, (q + 1) * S5_QL)
                cat = lambda ref: jnp.concatenate([ref[q * S5_QT + jj] for jj in range(S5_QT)], axis=1).astype(BF16)
                grq, giq = cat(gre), cat(gim)
                dus.append(_dot_nt(grq, bre[q]) + _dot_nt(giq, bim[q]) + dy_v[:, sl] * d_ref[:, sl])
                dbre[q] += _dot_tn(ub[:, sl], grq)
                dbim[q] += _dot_tn(ub[:, sl], giq)
                dcre[q] += _dot_tn(cat(sre), dyb[:, sl])
                dcim[q] -= _dot_tn(cat(sim), dyb[:, sl])
            du_io.store(slot, jnp.concatenate(dus, axis=1))

        _seg_pipeline(i, nb, lambda st: nb - 1 - st, [u_io, dy_io], [du_io], compute)

    full = lambda a: pl.BlockSpec(a.shape, lambda i, nd=a.ndim: (0,) * nd)
    sh = jax.ShapeDtypeStruct
    outs = [sh((L, S5_WIDTH), F32), sh(bq_re.shape, F32), sh(bq_im.shape, F32), sh(cq_re.shape, F32), sh(cq_im.shape, F32),
            sh((8, S5_LANES), F32), sh((1, S5_WIDTH), F32)]
    fo = lambda s: pl.BlockSpec(s.shape, lambda i, nd=len(s.shape): (0,) * nd)
    st = pltpu.VMEM((S5_NT, TB, 128), F32)
    pw = pltpu.VMEM((S5_NT, seg, 8, 128), F32)
    io = pltpu.VMEM((2, seg, 8, S5_WIDTH), F32)
    sem = pltpu.SemaphoreType.DMA((2, 8))
    return _pcall(
        body, name="s5_scan_bwd", grid=(nb,),
        in_specs=[_ANY, _ANY, pl.BlockSpec((1, 8, S5_LANES), lambda i: (nb - 1 - i, 0, 0)),
                  full(bq_re), full(bq_im), full(cq_re), full(cq_im), full(lbar), full(dskip)],
        out_specs=[_ANY] + [fo(s) for s in outs[1:]],
        out_shape=outs,
        scratch_shapes=[st] * 4 + [pltpu.VMEM((8, S5_LANES), F32)] * 2 + [pw, pw, io, io, io, sem, sem, sem],
        compiler_params=_cparams(1))(proj, dy, ck, bq_re, bq_im, cq_re, cq_im, lbar, dskip)


N_HEAD = RW_WIDTH // HEAD
_NN = (((2,), (1,)), ((0,), (0,)))
_NT = (((2,), (2,)), ((0,), (0,)))
_TN = (((1,), (1,)), ((0,), (0,)))


def _hi_lo(x):
    h = x.astype(BF16)
    return h, (x - h.astype(F32)).astype(BF16)


def _mm_acc(a, b, dims, passes=3):
    dg = lambda p, q: lax.dot_general(p, q, dims, preferred_element_type=F32)
    if passes == 1:
        return dg(a.astype(BF16), b.astype(BF16))
    ah, al = _hi_lo(a)
    bh, bl = _hi_lo(b)
    return dg(ah, bh) + dg(ah, bl) + dg(al, bh)


def _cumsum_rows(x, transpose):
    h, n, _ = x.shape
    ti = lax.broadcasted_iota(jnp.int32, (h, n, n), 1)
    tj = lax.broadcasted_iota(jnp.int32, (h, n, n), 2)
    m = ((tj >= ti) if transpose else (tj <= ti)).astype(BF16)
    acc, rem = None, x
    for s in range(3):
        part = rem.astype(BF16)
        t = lax.dot_general(m, part, _NN, preferred_element_type=F32)
        acc = t if acc is None else acc + t
        if s < 2:
            rem = rem - part.astype(F32)
    return acc


def _slices(x, axis, sizes):
    out, off = [], 0
    for n in sizes:
        out.append(lax.slice_in_dim(x, off, off + n, axis=axis))
        off += n
    return tuple(out)


def _cat_op(axis, sizes, diff):
    plain = lambda *xs: jnp.concatenate(xs, axis=axis)
    if not diff:
        return plain
    f = jax.custom_vjp(plain)
    f.defvjp(lambda *xs: (plain(*xs), None), lambda _, d: _slices(d, axis, sizes))
    return f


def _split_op(axis, sizes, diff):
    plain = lambda x: _slices(x, axis, sizes)
    if not diff:
        return plain
    f = jax.custom_vjp(plain)
    f.defvjp(lambda x: (plain(x), None), lambda _, d: (jnp.concatenate(d, axis=axis),))
    return f


def _mm_ops(diff, passes):
    mm = lambda a, b, dims: _mm_acc(a, b, dims, passes)
    if not diff:
        return (lambda a, b: mm(a, b, _NN), lambda a, b: mm(a, b, _NT), lambda a, b: mm(a, b, _TN))

    @jax.custom_vjp
    def nn(a, b):
        return mm(a, b, _NN)

    nn.defvjp(lambda a, b: (mm(a, b, _NN), (a, b)), lambda r, d: (mm(d, r[1], _NT), mm(r[0], d, _TN)))

    @jax.custom_vjp
    def nt(a, b):
        return mm(a, b, _NT)

    nt.defvjp(lambda a, b: (mm(a, b, _NT), (a, b)), lambda r, d: (mm(d, r[1], _NN), mm(d, r[0], _TN)))

    @jax.custom_vjp
    def tn(a, b):
        return mm(a, b, _TN)

    tn.defvjp(lambda a, b: (mm(a, b, _TN), (a, b)), lambda r, d: (mm(r[1], d, _NT), mm(r[0], d, _NN)))
    return nn, nt, tn


def _cums_op(diff):
    if not diff:
        return lambda x: _cumsum_rows(x, False)

    @jax.custom_vjp
    def cums(x):
        return _cumsum_rows(x, False)

    cums.defvjp(lambda x: (_cumsum_rows(x, False), None), lambda _, d: (_cumsum_rows(d, True),))
    return cums


WKV_PASSES = (1, 1, 1, 1, 1)


WKV_SUB = 4
WKV_BLOCK = CHUNK * WKV_SUB


def _wkv_block(s0, r, w, k, v, a, b, diff):
    p_pair, p_val, p_solve, p_out, p_state = WKV_PASSES
    cums = _cums_op(diff)
    _, nt_pair, _ = _mm_ops(diff, p_pair)
    nn_val, _, _ = _mm_ops(diff, p_val)
    nn_solve, _, _ = _mm_ops(diff, p_solve)
    nn_out, nt_out, _ = _mm_ops(diff, p_out)
    nn_state, _, tn_state = _mm_ops(diff, p_state)
    h, d, n, sub = s0.shape[0], s0.shape[2], CHUNK, WKV_SUB
    hb = h * sub
    to_chunks = lambda t: _cat_op(0, (h,) * sub, diff)(*_split_op(1, (n,) * sub, diff)(t))
    r, w, k, v, a, b = (to_chunks(t) for t in (r, w, k, v, a, b))
    cat_rows2 = _cat_op(1, (n, n), diff)
    cat_lanes2 = _cat_op(2, (n, n), diff)
    split_rows2 = _split_op(1, (n, n), diff)
    split_lanes2 = _split_op(2, (n, n), diff)
    ti = lax.broadcasted_iota(jnp.int32, (hb, n, n), 1)
    tj = lax.broadcasted_iota(jnp.int32, (hb, n, n), 2)
    incl, strict = tj <= ti, tj < ti
    logw = jnp.log(w)
    cum = cums(logw)
    g_in, g_ex, g_inv = jnp.exp(cum), jnp.exp(cum - logw), jnp.exp(-cum)
    ae, re, bi, ki = a * g_ex, r * g_in, b * g_inv, k * g_inv
    top, bot = split_rows2(nt_pair(cat_rows2(ae, re), cat_rows2(bi, ki)))
    tab, tak = split_lanes2(top)
    qb, qk = split_lanes2(bot)
    tab, tak = jnp.where(strict, tab, 0.0), jnp.where(strict, tak, 0.0)
    qb, qk = jnp.where(incl, qb, 0.0), jnp.where(incl, qk, 0.0)
    tak_v, qk_v = split_rows2(nn_val(cat_rows2(tak, qk), v))
    x = cat_lanes2(ae, tak_v)
    npow = tab
    steps = max(1, (n - 1).bit_length())
    for i in range(steps):
        x = x + nn_solve(npow, x)
        if i + 1 < steps:
            npow = nn_solve(npow, npow)
    ae_m, uc = split_lanes2(x)
    qx = nn_out(qb, x)
    q_ae, q_uc = split_lanes2(qx)
    re_m = re + q_ae
    yc = q_uc + qk_v
    g_end = jnp.exp(jnp.sum(logw, axis=1, keepdims=True))
    bg, kg = bi * g_end, ki * g_end
    tm = tn_state(ae_m, bg)
    sc = tn_state(cat_rows2(uc, v), cat_rows2(bg, kg))
    per_chunk = _split_op(0, (h,) * sub, diff)
    re_m, yc, g_end, tm, sc = (per_chunk(t) for t in (re_m, yc, g_end, tm, sc))
    ys, s = [], s0
    for i in range(sub):
        ys.append(nt_out(re_m[i], s) + yc[i])
        s = s * g_end[i] + nn_state(s, tm[i]) + sc[i]
    return _cat_op(1, (n,) * sub, diff)(*ys), s


def _wkv_fwd(r, w, k, v, a, b, L):
    nc = L // WKV_BLOCK

    def body(r_ref, w_ref, k_ref, v_ref, a_ref, b_ref, y_ref, ck_ref, s_ref):
        c = pl.program_id(0)

        @pl.when(c == 0)
        def _():
            s_ref[...] = jnp.zeros(s_ref.shape, F32)

        s0 = s_ref[...]
        ck_ref[0] = s0
        y, s1 = _wkv_block(s0, r_ref[...], w_ref[...], k_ref[...], v_ref[...], a_ref[...], b_ref[...], False)
        y_ref[...] = y
        s_ref[...] = s1

    blk = pl.BlockSpec((N_HEAD, WKV_BLOCK, HEAD), lambda c: (0, c, 0))
    return _pcall(
        body, name="wkv_fwd", grid=(nc,), in_specs=[blk] * 6,
        out_specs=[blk, pl.BlockSpec((1, N_HEAD, HEAD, HEAD), lambda c: (c, 0, 0, 0))],
        out_shape=[jax.ShapeDtypeStruct((N_HEAD, L, HEAD), F32), jax.ShapeDtypeStruct((nc, N_HEAD, HEAD, HEAD), F32)],
        scratch_shapes=[pltpu.VMEM((N_HEAD, HEAD, HEAD), F32)],
        compiler_params=_cparams(1))(r, w, k, v, a, b)


def _wkv_bwd(r, w, k, v, a, b, dy, ck, L, deps=()):
    nc = L // WKV_BLOCK

    def body(r_ref, w_ref, k_ref, v_ref, a_ref, b_ref, dy_ref, ck_ref, *rest):
        dr_ref, dw_ref, dk_ref, dv_ref, da_ref, db_ref, ds_ref = rest[len(deps):]
        c = pl.program_id(0)

        @pl.when(c == 0)
        def _():
            ds_ref[...] = jnp.zeros(ds_ref.shape, F32)

        _, vjp = jax.vjp(lambda *t: _wkv_block(*t, True), ck_ref[0], r_ref[...], w_ref[...], k_ref[...], v_ref[...],
                         a_ref[...], b_ref[...])
        g = vjp((dy_ref[...], ds_ref[...]))
        ds_ref[...] = g[0]
        for o_ref, val in zip((dr_ref, dw_ref, dk_ref, dv_ref, da_ref, db_ref), g[1:]):
            o_ref[...] = val

    blk = pl.BlockSpec((N_HEAD, WKV_BLOCK, HEAD), lambda c: (0, nc - 1 - c, 0))
    sh = jax.ShapeDtypeStruct((N_HEAD, L, HEAD), F32)
    return _pcall(
        body, name="wkv_bwd", grid=(nc,),
        in_specs=[blk] * 7 + [pl.BlockSpec((1, N_HEAD, HEAD, HEAD), lambda c: (nc - 1 - c, 0, 0, 0))]
        + [pl.BlockSpec(d.shape, lambda c, nd=d.ndim: (0,) * nd) for d in deps],
        out_specs=[blk] * 6, out_shape=[sh] * 6,
        scratch_shapes=[pltpu.VMEM((N_HEAD, HEAD, HEAD), F32)],
        compiler_params=_cparams(1))(r, w, k, v, a, b, dy, ck, *deps)


TB = 256


def _bf(x):
    return x.astype(BF16)


def _inproj_fwd(x, norm_mix, w_in, L, deps=()):
    def fn(i, tv, cv):
        xn = _rms(tv[0], cv[0])
        return _dot(_bf(xn), cv[1]), xn

    return _tok_call("inproj_fwd", fn, L, 2 * TB, [(x, D_MODEL, 0)], [norm_mix, w_in], [(IN_COLS, F32), (D_MODEL, BF16)],
                     deps=deps)


def _s5_post_fn(glu_w, wtop, diff=True):
    mg = _mmc(glu_w, diff)
    mt = _mmc(wtop, diff) if wtop is not None else None

    def f(y, glu_b, e):
        z = _gelu(y)
        out = z * _sigmoid(mg(z) + glu_b + e)
        res = mt(out) if mt is not None else out
        return res, (z, out)

    return f


def _s5_post_fwd(y, glu_w, glu_b, L):
    def fn(i, tv, cv):
        out, _ = _s5_post_fn(cv[0], None, False)(tv[0], cv[1], 0.0)
        return (out,)

    return _tok_call("s5_post_fwd", fn, L, 2 * TB, [(y, S5_WIDTH, 0)], [glu_w, glu_b], [(S5_WIDTH, F32)])[0]


def _s5_post_bwd(y, dh1, glu_w, glu_b, wtop, L, deps=()):
    def fn(i, tv, cv):
        e0 = jnp.zeros((TB, S5_WIDTH), F32)
        _, vjp, (z, out) = jax.vjp(_s5_post_fn(cv[0], cv[2]), tv[0], cv[1], e0, has_aux=True)
        dy, db, de = vjp(tv[1])
        return dy, db, _dot_tn(_bf(z), _bf(de)), _dot_tn(_bf(out), _bf(tv[1]))

    return _tok_call("s5_post_bwd", fn, L, TB, [(y, S5_WIDTH, 0), (dh1, D_MODEL, 0)], [glu_w, glu_b, wtop],
                     [(S5_WIDTH, F32)], [(1, S5_WIDTH), (S5_WIDTH, S5_WIDTH), (S5_WIDTH, D_MODEL)], deps=deps)


RW_COLBLK = ((RW_WIDTH, 1), (RW_WIDTH, 2), (RW_WIDTH, 3), (128, 16), (128, 17))
RW_MU = ((0, 512), (512, 1024), (1024, 1536), (1536, 1664), (1664, 1792))


def _rw_pre_fn(w2pad, a2pad, g2, diff=True):
    m_w, m_a, m_g = _mmc(w2pad, diff), _mmc(a2pad, diff), _mmc(g2, diff)
    seg = _segsum(_head_indicator(RW_WIDTH), diff)

    def f(zr, zk, zv, zwa, zg, w0, a0, k_k, k_a, e_w, e_a):
        wl_t = jnp.tanh(zwa)
        wlin = w0 + m_w(wl_t) + e_w
        w = -_softplus(-wlin) - 0.5
        decay = jnp.exp(-jnp.exp(w))
        a = _sigmoid(a0 + m_a(zwa) + e_a)
        sg = _sigmoid(zg)
        g = m_g(sg)
        kk = zk * k_k
        kkn = kk / jnp.maximum(jnp.sqrt(seg(kk * kk)), L2_EPS)
        kf = zk * (1.0 + (a - 1.0) * k_a)
        return (zr, decay, kf, zv, -kkn, kkn * a, g), (wl_t, sg)

    return f


def _rw_shifted(i, tv, mu):
    sub = lax.broadcasted_iota(jnp.int32, (TB, 1), 0)
    zs, dif = [], []
    for n in range(5):
        z = tv[n]
        last = jnp.where(i == 0, 0.0, tv[5 + n][7:8, :])
        prev = jnp.where(sub == 0, last, pltpu.roll(z, 1, 0))
        m = mu[:, RW_MU[n][0]:RW_MU[n][1]]
        zs.append(z + (prev - z) * m)
        dif.append(prev - z)
    return zs, dif


def _rw_tok_in(proj):
    return [(proj, wd, cb) for wd, cb in RW_COLBLK] + [(proj, wd, cb, "prev") for wd, cb in RW_COLBLK]


def _rw_pre_fwd(proj, mu, w0, a0, k_k, k_a, w2pad, a2pad, g2, L):
    def fn(i, tv, cv):
        zs, _ = _rw_shifted(i, tv, cv[0])
        outs, _ = _rw_pre_fn(cv[5], cv[6], cv[7], False)(*zs, cv[1], cv[2], cv[3], cv[4], 0.0, 0.0)
        return outs

    return _tok_call("rw_pre_fwd", fn, L, TB, _rw_tok_in(proj), [mu, w0, a0, k_k, k_a, w2pad, a2pad, g2],
                     [("heads", F32)] * 6 + [(RW_WIDTH, F32)])


def _rw_pre_bwd(proj, cots, mu, w0, a0, k_k, k_a, w2pad, a2pad, g2, L):
    def fn(i, tv, cv):
        zs, dif = _rw_shifted(i, tv[:10], cv[0])
        dr1, dr2, dw, dk1, dk2, dv1, dv2, da, db, dg = tv[10:]
        e0 = jnp.zeros((TB, RW_WIDTH), F32)
        _, vjp, (wl_t, sg) = jax.vjp(_rw_pre_fn(cv[5], cv[6], cv[7]), *zs, cv[1], cv[2], cv[3], cv[4], e0, e0, has_aux=True)
        g = vjp((dr1 + dr2, dw, dk1 + dk2, dv1 + dv2, da, db, dg))
        dzs = jnp.concatenate(g[:5], axis=1)
        dmu = jnp.concatenate([jnp.sum(g[n] * dif[n], axis=0, keepdims=True) for n in range(5)], axis=1)
        lora = (_dot_tn(_bf(wl_t), _bf(g[9])), _dot_tn(_bf(zs[3]), _bf(g[10])), _dot_tn(_bf(sg), _bf(dg)))
        return (dzs, dmu, g[5], g[6], g[7], g[8]) + lora

    tok_in = _rw_tok_in(proj) + [((c,) if c.ndim == 3 else (c, RW_WIDTH, 0)) for c in cots]
    return _tok_call("rw_pre_bwd", fn, L, TB, tok_in, [mu, w0, a0, k_k, k_a, w2pad, a2pad, g2],
                     [(SHIFT_COLS, F32)], [(1, SHIFT_COLS)] + [(1, RW_WIDTH)] * 4 + [(128, RW_WIDTH)] * 3)


def _rw_post_fn(wbot, diff=True):
    seg = _segsum(_head_indicator(RW_WIDTH), diff)
    mb = _mmc(wbot, diff) if wbot is not None else None

    def f(y, r, kf, v, g, ln_w, ln_b, r_k):
        mean = seg(y) * (1.0 / HEAD)
        yc = y - mean
        var = seg(yc * yc) * (1.0 / HEAD)
        yn = yc * lax.rsqrt(var + GN_EPS) * ln_w + ln_b
        bonus = seg(r * kf * r_k) * v
        out = (yn + bonus) * g
        res = mb(out) if mb is not None else out
        return res, out

    return f


def _rw_post_fwd(y, r, kf, v, g, ln_w, ln_b, r_k, L):
    def fn(i, tv, cv):
        out, _ = _rw_post_fn(None, False)(*tv, *cv)
        return (out,)

    return _tok_call("rw_post_fwd", fn, L, 2 * TB, [(t,) for t in (y, r, kf, v)] + [(g, RW_WIDTH, 0)], [ln_w, ln_b, r_k],
                     [(RW_WIDTH, F32)])[0]


def _rw_post_bwd(y, r, kf, v, g, dh1, ln_w, ln_b, r_k, wbot, L):
    def fn(i, tv, cv):
        _, vjp, out = jax.vjp(_rw_post_fn(cv[3]), *tv[:5], cv[0], cv[1], cv[2], has_aux=True)
        gr = vjp(tv[5])
        return gr[0], gr[1], gr[2], gr[3], gr[4], gr[5], gr[6], gr[7], _dot_tn(_bf(out), _bf(tv[5]))

    return _tok_call("rw_post_bwd", fn, L, TB, [(t,) for t in (y, r, kf, v)] + [(g, RW_WIDTH, 0), (dh1, D_MODEL, 0)],
                     [ln_w, ln_b, r_k, wbot], [("heads", F32)] + [(RW_WIDTH, F32)] * 4,
                     [(1, RW_WIDTH)] * 3 + [(RW_WIDTH, D_MODEL)])


def _ffn_fn(w1, w3, w2, diff=True):
    m1, m3, m2 = _mmc(w1, diff), _mmc(w3, diff), _mmc(w2, diff)

    def f(h1, norm_ffn, e1, e3):
        hn = _rms(h1, norm_ffn)
        a1 = m1(hn) + e1
        a3 = m3(hn) + e3
        hm = a1 * _sigmoid(a1) * a3
        return h1 + m2(hm), (hn, hm)

    return f


TB_FFN = 256


def _mixffn_fwd(x, s5_out, rw_out, wtop, wbot, norm_ffn, w1, w3, w2, L):
    def fn(i, tv, cv):
        h1 = tv[0] + _dot(_bf(tv[1]), cv[0]) + _dot(_bf(tv[2]), cv[1])
        h2, _ = _ffn_fn(cv[3], cv[4], cv[5], False)(h1, cv[2], 0.0, 0.0)
        return h1, h2

    return _tok_call("mixffn_fwd", fn, L, 2 * TB_FFN, [(x, D_MODEL, 0), (s5_out, S5_WIDTH, 0), (rw_out, RW_WIDTH, 0)],
                     [wtop, wbot, norm_ffn, w1, w3, w2], [(D_MODEL, F32), (D_MODEL, F32)])


def _ffn_bwd(h1, dh2, norm_ffn, w1, w3, w2, L):
    def fn(i, tv, cv):
        e0 = jnp.zeros((TB_FFN, FFN_HIDDEN), F32)
        _, vjp, (hn, hm) = jax.vjp(_ffn_fn(cv[1], cv[2], cv[3]), tv[0], cv[0], e0, e0, has_aux=True)
        dh1, dn, d1, d3 = vjp(tv[1])
        return dh1, d1, d3, hm, hn, dn

    return _tok_call("ffn_bwd", fn, L, TB_FFN, [(h1, D_MODEL, 0), (dh2, D_MODEL, 0)], [norm_ffn, w1, w3, w2],
                     [(D_MODEL, F32), (FFN_HIDDEN, BF16), (FFN_HIDDEN, BF16), (FFN_HIDDEN, BF16), (D_MODEL, BF16)],
                     [(1, D_MODEL)])


def _ple_loss_fb(h2, p, target, norm_ple, final_norm, wg, wu, L):
    def fn(i, tv, cv):
        mgate, mup = _mmc(cv[2]), _mmc(cv[3], False)

        def f(h2_, norm_ple_, final_norm_, eg, eu):
            hn = _rms(h2_, norm_ple_)
            gate = _sigmoid(mgate(hn) + eg)
            h3 = h2_ + gate * (mup(tv[1]) + eu)
            out = _rms(h3, final_norm_)
            d = out - tv[2]
            return 0.5 * jnp.sum(jnp.mean(d * d, axis=-1, keepdims=True)), hn

        e0 = jnp.zeros(tv[0].shape, F32)
        loss, vjp, hn = jax.vjp(f, tv[0], cv[0], cv[1], e0, e0, has_aux=True)
        dh2, dnp, dfn, deg, deu = vjp(jnp.ones((), F32))
        return (dh2, dh2, jnp.full((8, 128), loss, F32), dnp, dfn,
                _dot_tn(_bf(hn), _bf(deg)), _dot_tn(_bf(tv[1]), _bf(deu)))

    return _tok_call("ple_loss_fb", fn, L, 2 * TB, [(h2, D_MODEL, 0), (p, PLE_DIM, 0), (target, D_MODEL, 0)],
                     [norm_ple, final_norm, wg, wu], [(D_MODEL, F32), (D_MODEL, BF16)],
                     [(8, 128), (1, D_MODEL), (1, D_MODEL), (D_MODEL, D_MODEL), (PLE_DIM, D_MODEL)])


def _inproj_bwd(x, dh1, du, dzs, norm_mix, mu, w_u, w_z, L):
    nb = L // TB

    def fn(i, tv, cv):
        sub = lax.broadcasted_iota(jnp.int32, (TB, 1), 0)
        m = cv[1]
        b = tv[3] * m
        nxt = jnp.where(i == nb - 1, 0.0, tv[4][0:1, :] * m)
        dz = tv[3] * (1.0 - m) + jnp.where(sub == TB - 1, nxt, pltpu.roll(b, TB - 1, 0))
        dub, dzb = _bf(tv[2]), _bf(dz)
        dxn = _dot_nt(dub, cv[2]) + _dot_nt(dzb, cv[3])
        _, vjp = jax.vjp(_rms, tv[0], cv[0])
        dx, dn = vjp(dxn)
        return tv[1] + dx, jnp.concatenate([dub, dzb], axis=1), dn

    return _tok_call("inproj_bwd", fn, L, TB,
                     [(x, D_MODEL, 0), (dh1, D_MODEL, 0), (du, S5_WIDTH, 0), (dzs, SHIFT_COLS, 0), (dzs, SHIFT_COLS, 0, "next")],
                     [norm_mix, mu, w_u, w_z], [(D_MODEL, F32), (IN_COLS, BF16)], [(1, D_MODEL)])


def _eye8(dt):
    return jnp.eye(8, dtype=dt)


def _quarter_b(bb):
    return jnp.einsum("hg,qgcp->qhcgp", _eye8(bb.dtype), bb.reshape(S5_Q, 8, S5_GROUP, S5_STATE)).reshape(S5_Q, S5_QL, S5_QS)


def _unquarter_b(d):
    return jnp.einsum("qhcgp,hg->qgcp", d.reshape(S5_Q, 8, S5_GROUP, 8, S5_STATE), _eye8(d.dtype)).reshape(
        S5_GROUPS, S5_GROUP, S5_STATE)


def _quarter_c(c):
    return jnp.einsum("gh,qgcp->qgphc", _eye8(c.dtype), c.reshape(S5_Q, 8, S5_GROUP, S5_STATE)).reshape(S5_Q, S5_QS, S5_QL)


def _unquarter_c(d):
    return jnp.einsum("qgphc,gh->qgcp", d.reshape(S5_Q, 8, S5_STATE, 8, S5_GROUP), _eye8(d.dtype)).reshape(
        S5_GROUPS, S5_GROUP, S5_STATE)


def _local_step(x, p, target, W, late_weights=None, grads_ready=None, first_dep=None):
    L = x.shape[0]
    r2 = lambda v: v.reshape(1, -1)
    w_in = W["w_in"]
    w2pad = jnp.pad(W["rw_w2"], ((0, 64), (0, 0)))
    a2pad = jnp.pad(W["rw_a2"], ((64, 0), (0, 0)))
    mu = r2(W["rw_shift_mu"])
    rw_vec = [r2(W[n]) for n in ("rw_w0", "rw_a0", "rw_k_k", "rw_k_a")]
    ln_w, ln_b, r_k = r2(W["rw_ln_w"]), r2(W["rw_ln_b"]), r2(W["rw_r_k"])

    lam_re, lam_im = W["s5_lam_re"], W["s5_lam_im"]
    log_step = W["s5_log_step"].reshape(S5_GROUPS, 1)
    bt_re, bt_im = W["s5_b_re"].transpose(0, 2, 1), W["s5_b_im"].transpose(0, 2, 1)
    lb_re, lb_im, bb_re, bb_im = _s5_param_fwd(lam_re, lam_im, log_step, bt_re, bt_im)
    bq_re, bq_im = _quarter_b(bb_re).astype(BF16), _quarter_b(bb_im).astype(BF16)
    cq_re, cq_im = _quarter_c(W["s5_c_re"]).astype(BF16), _quarter_c(W["s5_c_im"]).astype(BF16)
    lbar = jnp.concatenate([lb_re.reshape(1, -1), lb_im.reshape(1, -1), jnp.zeros((6, S5_LANES), F32)], axis=0)
    dskip = r2(W["s5_d"])
    glu_b = r2(W["s5_glu_b"])
    norm_mix, norm_ffn, norm_ple, final_norm = (r2(W[n]) for n in ("norm_mix", "norm_ffn", "norm_ple", "final_norm"))

    proj, xn = _inproj_fwd(x, norm_mix, w_in, L, () if first_dep is None else (first_dep,))
    y_s5, ck5 = _s5_scan_fwd(proj, bq_re, bq_im, cq_re, cq_im, lbar, dskip, L, TB)
    s5_out = _s5_post_fwd(y_s5, W["s5_glu_w"], glu_b, L)
    r, wd, kf, v, a_s, b_s, g = _rw_pre_fwd(proj, mu, *rw_vec, w2pad, a2pad, W["rw_g2"], L)
    scan_in = (r, wd, kf, v, a_s, b_s)
    y_wkv, ckw = _wkv_fwd(*scan_in, L)
    rw_out = _rw_post_fwd(y_wkv, r, kf, v, g, ln_w, ln_b, r_k, L)
    if late_weights is not None:
        W = dict(W, **late_weights(rw_out))
    wtop, wbot = W["w_out"][:S5_WIDTH], W["w_out"][S5_WIDTH:]
    h1, h2 = _mixffn_fwd(x, s5_out, rw_out, wtop, wbot, norm_ffn, W["ffn_w1"], W["ffn_w3"], W["ffn_w2"], L)

    G = {}
    dh2, dh2_bf, loss_acc, G["norm_ple"], G["final_norm"], G["ple_gate_w"], G["ple_up_w"] = _ple_loss_fb(
        h2, p, target, norm_ple, final_norm, W["ple_gate_w"], W["ple_up_w"], L)
    dh1, da1, da3, hm, hn_ffn, G["norm_ffn"] = _ffn_bwd(h1, dh2, norm_ffn, W["ffn_w1"], W["ffn_w3"], W["ffn_w2"], L)
    G["ffn_w1"] = _mm_tn("dw_ffn_w1", hn_ffn, da1)
    G["ffn_w3"] = _mm_tn("dw_ffn_w3", hn_ffn, da3)
    G["ffn_w2"] = _mm_tn("dw_ffn_w2", hm, dh2_bf)
    dep_a = grads_ready(0, G) if grads_ready is not None else None
    dy_s5, G["s5_glu_b"], G["s5_glu_w"], d_wtop = _s5_post_bwd(y_s5, dh1, W["s5_glu_w"], glu_b, wtop, L,
                                                               () if dep_a is None else (dep_a,))
    dy_wkv, dr2, dk2, dv2, dg, G["rw_ln_w"], G["rw_ln_b"], G["rw_r_k"], d_wbot = _rw_post_bwd(
        y_wkv, r, kf, v, g, dh1, ln_w, ln_b, r_k, wbot, L)
    G["w_out"] = jnp.concatenate([d_wtop, d_wbot], axis=0)
    dep = grads_ready(1, G) if grads_ready is not None else None
    dr1, dwd, dk1, dv1, da_s, db_s = _wkv_bwd(*scan_in, dy_wkv, ckw, L, () if dep is None else (dep,))
    (dzs, G["rw_shift_mu"], G["rw_w0"], G["rw_a0"], G["rw_k_k"], G["rw_k_a"], d_w2pad, d_a2pad, G["rw_g2"]) = _rw_pre_bwd(
        proj, (dr1, dr2, dwd, dk1, dk2, dv1, dv2, da_s, db_s, dg), mu, *rw_vec, w2pad, a2pad, W["rw_g2"], L)
    G["rw_w2"], G["rw_a2"] = d_w2pad[:64], d_a2pad[64:]
    du, dbq_re, dbq_im, dcq_re, dcq_im, dlbar, G["s5_d"] = _s5_scan_bwd(
        proj, dy_s5, ck5, bq_re, bq_im, cq_re, cq_im, lbar, dskip, L, TB)
    G["s5_c_re"], G["s5_c_im"] = _unquarter_c(dcq_re), _unquarter_c(dcq_im)
    d_lam_re, d_lam_im, d_ls, d_bt_re, d_bt_im = _s5_param_bwd(
        lam_re, lam_im, log_step, bt_re, bt_im, dlbar[0].reshape(S5_GROUPS, S5_STATE), dlbar[1].reshape(S5_GROUPS, S5_STATE),
        _unquarter_b(dbq_re), _unquarter_b(dbq_im))
    G["s5_lam_re"], G["s5_lam_im"], G["s5_log_step"] = d_lam_re, d_lam_im, d_ls.reshape(S5_GROUPS)
    G["s5_b_re"], G["s5_b_im"] = d_bt_re.transpose(0, 2, 1), d_bt_im.transpose(0, 2, 1)
    dx, dproj, G["norm_mix"] = _inproj_bwd(x, dh1, du, dzs, norm_mix, mu, w_in[:, :S5_WIDTH], w_in[:, S5_WIDTH:], L)
    G["w_in"] = _mm_tn("dw_in", xn, dproj)
    return loss_acc[0, 0], dx, G


def _all_gather(name, shards):
    nt = len(shards)

    def body(*refs):
        x_refs, out_refs = refs[:nt], refs[nt:2 * nt]
        send_sems, recv_sems, local_sems = refs[2 * nt:]
        x, y, c = lax.axis_index("x"), lax.axis_index("y"), lax.axis_index("c")
        me, sibling = (x, y, c), (x, y, 1 - c)
        chips = [(1 - x, y), (x, 1 - y), (1 - x, 1 - y)]

        def rows(t, px, py, pc):
            m_per = shards[t].shape[0]
            return out_refs[t].at[pl.ds((4 * px + 2 * py + pc) * m_per, m_per), :]

        def copy(t, k, block, to, src=None):
            return pltpu.make_async_remote_copy(
                src_ref=rows(t, *block) if src is None else src, dst_ref=rows(t, *block),
                send_sem=send_sems.at[7 * t + k], recv_sem=recv_sems.at[7 * t + k],
                device_id=to, device_id_type=pl.DeviceIdType.MESH)

        mine = [pltpu.make_async_copy(x_refs[t], rows(t, *me), local_sems.at[t]) for t in range(nt)]
        for cp in mine:
            cp.start()
        first = []
        for t in range(nt):
            first.append(copy(t, 0, me, sibling, src=x_refs[t]))
            first += [copy(t, 1 + j, me, (*chip, c), src=x_refs[t]) for j, chip in enumerate(chips)]
        for cp in first:
            cp.start()
        passed = []
        for t in range(nt):
            for j, chip in enumerate(chips):
                copy(t, 1 + j, (*chip, c), me).wait_recv()
                fwd = copy(t, 4 + j, (*chip, c), sibling)
                fwd.start()
                passed.append(fwd)
        for t in range(nt):
            copy(t, 0, sibling, me).wait_recv()
            for j, chip in enumerate(chips):
                copy(t, 4 + j, (*chip, 1 - c), me).wait_recv()
        for cp in first + passed:
            cp.wait_send()
        for cp in mine:
            cp.wait()

    return _pcall(body, name=name,
                  out_shape=[jax.ShapeDtypeStruct((N_DEV * a.shape[0], a.shape[1]), a.dtype) for a in shards],
                  in_specs=[_ANY] * nt, out_specs=[_ANY] * nt,
                  scratch_shapes=[pltpu.SemaphoreType.DMA((7 * nt,)), pltpu.SemaphoreType.DMA((7 * nt,)),
                                  pltpu.SemaphoreType.DMA((nt,))])(*shards)


_HBM = pl.BlockSpec(memory_space=pltpu.HBM)
_SEM = pl.BlockSpec(memory_space=pltpu.SEMAPHORE)
_EFFECT = pltpu.SideEffectType.DATAFLOW_SIDE_EFFECTING


def _peer_of(k):
    x, y, c = lax.axis_index("x"), lax.axis_index("y"), lax.axis_index("c")
    px, py, pc = x ^ ((k >> 2) & 1), y ^ ((k >> 1) & 1), c ^ (k & 1)
    return (px, py, pc), 4 * px + 2 * py + pc, 4 * x + 2 * y + c


def _direct_copy(t, k, src_refs, land_refs, send_sems, recv_sems, rows_of, gather):
    dev, peer, me = _peer_of(k)
    m = rows_of[t]
    src = src_refs[t] if gather else src_refs[t].at[pl.ds(peer * m, m), :]
    return pltpu.make_async_remote_copy(
        src_ref=src, dst_ref=land_refs[t].at[pl.ds(me * m, m), :],
        send_sem=send_sems.at[7 * t + k - 1], recv_sem=recv_sems.at[7 * t + k - 1],
        device_id=dev, device_id_type=pl.DeviceIdType.MESH)


def _direct_landing(t, k, src_refs, land_refs, send_sems, recv_sems, rows_of, gather):
    dev, peer, me = _peer_of(k)
    m = rows_of[t]
    src = src_refs[t] if gather else src_refs[t].at[pl.ds(me * m, m), :]
    return pltpu.make_async_remote_copy(
        src_ref=src, dst_ref=land_refs[t].at[pl.ds(peer * m, m), :],
        send_sem=send_sems.at[7 * t + k - 1], recv_sem=recv_sems.at[7 * t + k - 1],
        device_id=dev, device_id_type=pl.DeviceIdType.MESH)


def _direct_start(name, srcs, gather, dep=None):
    nt = len(srcs)
    rows_of = [a.shape[0] if gather else a.shape[0] // N_DEV for a in srcs]
    lands = [pltpu.with_memory_space_constraint(lax.empty((N_DEV * m, a.shape[1]), a.dtype), pltpu.HBM)
             for a, m in zip(srcs, rows_of)]

    n_dep = 0 if dep is None else 1

    def body(*refs):
        src_refs, land_refs = refs[:nt], refs[nt:2 * nt]
        send_sems, recv_sems = refs[2 * nt + n_dep], refs[2 * nt + n_dep + 1]
        token = refs[-1]
        for t in range(nt):
            for k in range(1, N_DEV):
                _direct_copy(t, k, src_refs, land_refs, send_sems, recv_sems, rows_of, gather).start()
        token[...] = jnp.zeros(token.shape, F32)

    out = _pcall(
        body, name=name,
        out_shape=(pltpu.SemaphoreType.DMA((7 * nt,)), pltpu.SemaphoreType.DMA((7 * nt,)),
                   *[pltpu.HBM(a.shape, a.dtype) for a in srcs], *[pltpu.HBM(a.shape, a.dtype) for a in lands],
                   jax.ShapeDtypeStruct((8, 128), F32)),
        in_specs=(_HBM,) * (2 * nt) + (pl.BlockSpec(memory_space=pl.ANY),) * n_dep,
        out_specs=(_SEM, _SEM) + (_HBM,) * (2 * nt) + (pl.BlockSpec(memory_space=pltpu.VMEM),),
        input_output_aliases={i: 2 + i for i in range(2 * nt)},
        compiler_params=pltpu.CompilerParams(has_side_effects=_EFFECT),
    )(*[pltpu.with_memory_space_constraint(a, pltpu.HBM) for a in srcs], *lands, *(() if dep is None else (dep,)))
    return (out[0], out[1], list(out[2:2 + nt]), list(out[2 + nt:2 + 2 * nt]), rows_of, gather), out[-1]


def _direct_wait(name, handle, after):
    send_sems, recv_sems, srcs, lands, rows_of, gather = handle
    nt = len(srcs)
    after = list(after) if isinstance(after, (list, tuple)) else [after]

    def body(*refs):
        src_refs, land_refs = refs[:nt], refs[nt:2 * nt]
        s_sems, r_sems = refs[2 * nt], refs[2 * nt + 1]
        for t in range(nt):
            for k in range(1, N_DEV):
                _direct_copy(t, k, src_refs, land_refs, s_sems, r_sems, rows_of, gather).wait_send()
                _direct_landing(t, k, src_refs, land_refs, s_sems, r_sems, rows_of, gather).wait_recv()

    out = _pcall(
        body, name=name,
        out_shape=tuple(pltpu.HBM(a.shape, a.dtype) for a in srcs) + tuple(pltpu.HBM(a.shape, a.dtype) for a in lands),
        in_specs=(_HBM,) * (2 * nt) + (_SEM, _SEM) + (pl.BlockSpec(memory_space=pl.ANY),) * len(after),
        out_specs=(_HBM,) * (2 * nt),
        input_output_aliases={i: i for i in range(2 * nt)},
        compiler_params=pltpu.CompilerParams(has_side_effects=_EFFECT),
    )(*srcs, *lands, send_sems, recv_sems, *after)
    return list(out[:nt]), list(out[nt:])


def _adamw_sharded(name, own, parts, w, m, v, rb, deps=()):
    R, N = own.shape

    def body(o_ref, p_ref, w_ref, m_ref, v_ref, *rest):
        g_ref, d_ref, nm_ref, nv_ref = rest[len(deps):]
        me = 4 * lax.axis_index("x") + 2 * lax.axis_index("y") + lax.axis_index("c")
        g = o_ref[...]
        for k in range(1, N_DEV):
            g = g + p_ref[me ^ k].astype(F32)
        nm = ADAM_B1 * m_ref[...] + (1.0 - ADAM_B1) * g
        nv = ADAM_B2 * v_ref[...] + (1.0 - ADAM_B2) * (g * g)
        m_hat = nm / (1.0 - ADAM_B1 ** ADAM_STEP)
        v_hat = nv / (1.0 - ADAM_B2 ** ADAM_STEP)
        g_ref[...] = g
        d_ref[...] = -ADAM_LR * (m_hat / (jnp.sqrt(v_hat) + ADAM_EPS) + ADAM_WD * w_ref[...])
        nm_ref[...] = nm
        nv_ref[...] = nv

    blk = pl.BlockSpec((rb, N), lambda i: (i, 0))
    sh = jax.ShapeDtypeStruct((R, N), F32)
    return _pcall(body, name=name, grid=(R // rb,),
                  in_specs=[blk, pl.BlockSpec((N_DEV, rb, N), lambda i: (0, i, 0)), blk, blk, blk]
                  + [pl.BlockSpec(d.shape, lambda i, nd=d.ndim: (0,) * nd) for d in deps],
                  out_specs=[blk] * 4, out_shape=[sh] * 4, compiler_params=_cparams(1))(own, parts, w, m, v, *deps)


LOSS_SLOT = "loss_partials"
SMALL_CLASSES = (
    (("s5_b_re", 32, 1024), ("s5_b_im", 32, 1024),
     ("norm_mix", 1, 1024), ("norm_ffn", 1, 1024), ("norm_ple", 1, 1024), ("final_norm", 1, 1024)),
    (("s5_d", 1, 512), ("s5_glu_b", 1, 512), ("rw_w0", 1, 512), ("rw_a0", 1, 512), ("rw_k_k", 1, 512), ("rw_k_a", 1, 512),
     ("rw_ln_w", 1, 512), ("rw_ln_b", 1, 512), ("rw_r_k", 1, 512)),
    (("rw_shift_mu", 1, 1792),),
    (("s5_lam_re", 32, 64), ("s5_lam_im", 32, 64), ("s5_c_re", 512, 64), ("s5_c_im", 512, 64)),
    (("s5_log_step", 1, 32), (LOSS_SLOT, 1, 32)),
)


def _class_rows(cls):
    return -(-sum(r for _, r, _ in cls) // 8) * 8


def _stack_class(cls, arrs):
    a = jnp.concatenate(arrs, axis=0) if len(arrs) > 1 else arrs[0]
    pad = _class_rows(cls) - a.shape[0]
    return jnp.pad(a, ((0, pad), (0, 0))) if pad else a


def _adamw_small(grads, w, m, v):
    names = [n for cls in SMALL_CLASSES for n, _, _ in cls]
    n_cls, n_par = len(SMALL_CLASSES), len(names)

    def body(*refs):
        g_refs = refs[:n_cls]
        w_refs, m_refs, v_refs = (refs[n_cls + i * n_par:n_cls + (i + 1) * n_par] for i in range(3))
        o_refs = refs[n_cls + 3 * n_par:]
        p = 0
        for cls, g_ref in zip(SMALL_CLASSES, g_refs):
            rc = _class_rows(cls)
            tot = g_ref[0:rc, :]
            for s_ in range(1, N_DEV):
                tot = tot + g_ref[s_ * rc:(s_ + 1) * rc, :]
            off = 0
            for _, r, _ in cls:
                g = tot[off:off + r, :]
                off += r
                nm = ADAM_B1 * m_refs[p][...] + (1.0 - ADAM_B1) * g
                nv = ADAM_B2 * v_refs[p][...] + (1.0 - ADAM_B2) * (g * g)
                m_hat = nm / (1.0 - ADAM_B1 ** ADAM_STEP)
                v_hat = nv / (1.0 - ADAM_B2 ** ADAM_STEP)
                o_refs[4 * p][...] = g
                o_refs[4 * p + 1][...] = -ADAM_LR * (m_hat / (jnp.sqrt(v_hat) + ADAM_EPS) + ADAM_WD * w_refs[p][...])
                o_refs[4 * p + 2][...] = nm
                o_refs[4 * p + 3][...] = nv
                p += 1

    shapes = [(r, c) for cls in SMALL_CLASSES for _, r, c in cls]
    out = _pcall(body, name="adamw_replicated",
                 out_shape=[jax.ShapeDtypeStruct(sh, F32) for sh in shapes for _ in range(4)],
                 compiler_params=pltpu.CompilerParams(vmem_limit_bytes=VMEM_LIMIT))(*grads, *w, *m, *v)
    return {n: out[4 * i:4 * i + 4] for i, n in enumerate(names)}


EARLY = (("w_in", True),)
LATE = (("ffn_w1", True), ("ffn_w3", True), ("ffn_w2", False), ("ple_gate_w", False), ("w_out", False))
GRAD_STAGES = (LATE[:4], LATE[4:])
MISC = (("s5_glu_w", False), ("rw_w2", True), ("rw_a2", True), ("rw_g2", True), ("ple_up_w", True))
SHARDED_NAMES = tuple(n for n, _ in EARLY + LATE + MISC)
PACK_COLS = 1024
WEIGHT_NAMES = ("norm_mix", "w_in", "s5_lam_re", "s5_lam_im", "s5_log_step", "s5_b_re", "s5_b_im", "s5_c_re", "s5_c_im", "s5_d",
                "s5_glu_w", "s5_glu_b", "rw_shift_mu", "rw_w0", "rw_w2", "rw_a0", "rw_a2", "rw_g2", "rw_k_k", "rw_k_a", "rw_r_k",
                "rw_ln_w", "rw_ln_b", "w_out", "norm_ffn", "ffn_w1", "ffn_w3", "ffn_w2", "norm_ple", "ple_gate_w", "ple_up_w",
                "final_norm")
SMALL_NAMES = tuple(n for n in WEIGHT_NAMES if n not in SHARDED_NAMES)
ARG_NAMES = ("x", "p") + WEIGHT_NAMES + ("loss_target",) + tuple("m_" + n for n in WEIGHT_NAMES) + tuple("v_" + n for n in WEIGHT_NAMES)


def _travel(a, tr):
    return a.T if tr else a


def _pack_misc(blocks):
    lead = blocks[0].shape[:-2]
    return jnp.concatenate([b.reshape(lead + (-1, PACK_COLS)) for b in blocks], axis=len(lead))


def _unpack_misc(packed, shapes):
    lead = packed.shape[:-2]
    out, off = [], 0
    for r, c in shapes:
        n = r * c // PACK_COLS
        out.append(lax.slice_in_dim(packed, off, off + n, axis=len(lead)).reshape(lead + (r, c)))
        off += n
    return out


def _kernel_impl(ins):
    x, p, target = ins["x"][0], ins["p"][0, 0], ins["loss_target"][0]
    me = 4 * lax.axis_index("x") + 2 * lax.axis_index("y") + lax.axis_index("c")
    small = {n: (ins[n] if n == "final_norm" else ins[n][0]) for n in SMALL_NAMES}
    trav = lambda pre, n, tr: _travel(ins[pre + n][0], tr)
    misc_shapes = [trav("", n, tr).shape for n, tr in MISC]

    early = _all_gather("ag_early", [trav("", n, tr).astype(BF16) for n, tr in EARLY]
                        + [_pack_misc([trav("", n, tr).astype(BF16) for n, tr in MISC])])
    late_handle, late_token = _direct_start("ag_late_start", [trav("", n, tr).astype(BF16) for n, tr in LATE], True, early[-1])
    W = dict(small)
    for (n, tr), g in zip(EARLY, early):
        W[n] = _travel(g, tr)
    for (n, tr), g in zip(MISC, _unpack_misc(early[-1].reshape(N_DEV, -1, PACK_COLS), misc_shapes)):
        W[n] = _travel(g.reshape(-1, g.shape[-1]), tr)

    def late_weights(after):
        shards, lands = _direct_wait("ag_late_wait", late_handle, after)
        full = [lax.dynamic_update_slice_in_dim(ld, sh, me * sh.shape[0], axis=0) for ld, sh in zip(lands, shards)]
        return {n: _travel(g, tr) for (n, tr), g in zip(LATE, full)}

    gt = lambda G, n, tr: _travel(G[n], tr)
    started = {}

    def grads_ready(stage, G):
        full = [gt(G, n, tr) for n, tr in GRAD_STAGES[stage]]
        started[stage] = (full, *_direct_start("grad_late_start%d" % stage, [a.astype(BF16) for a in full], False))
        return started[stage][2]

    loss_part, dx, G = _local_step(x, p, target, W, late_weights, grads_ready, late_token)

    misc_g = _pack_misc([gt(G, n, tr).reshape((N_DEV,) + shp) for (n, tr), shp in zip(MISC, misc_shapes)])
    early_full = [gt(G, n, tr) for n, tr in EARLY] + [misc_g.reshape(-1, PACK_COLS)]
    early_handle, early_token = _direct_start("grad_early_start", [a.astype(BF16) for a in early_full], False)
    view2 = lambda a, r, c: a.reshape(r, c)
    G[LOSS_SLOT] = jnp.full((1, 32), loss_part, F32)
    small_own = [_stack_class(cls, [view2(G[n], r, c) for n, r, c in cls]) for cls in SMALL_CLASSES]
    small_handle, small_token = _direct_start("grad_small_start", small_own, True)
    late_src, late_land = [], []
    for stage in range(len(GRAD_STAGES)):
        full, handle, _ = started[stage]
        _, land = _direct_wait("grad_late_wait%d" % stage, handle, small_token)
        late_src += full
        late_land += land

    outs = {}

    def emit(names_shapes, res):
        for tag, val in zip(("grad_", "delta_", "new_m_", "new_v_"), res):
            for n, v in names_shapes(val):
                outs[tag + n] = v

    def sharded_update(n, tr, src, land, deps=()):
        rows = src.shape[0] // N_DEV
        own = lax.dynamic_slice_in_dim(src, me * rows, rows, axis=0)
        res = _adamw_sharded("adamw_" + n, own, land.reshape(N_DEV, rows, land.shape[1]),
                             trav("", n, tr), trav("m_", n, tr), trav("v_", n, tr), _pick_rows(rows), deps)
        emit(lambda val: [(n, _travel(val, tr).reshape(ins[n].shape))], res)
        return list(res)

    for (n, tr), src, land in zip(LATE, late_src, late_land):
        sharded_update(n, tr, src, land, (early_token,))
    _, early_land = _direct_wait("grad_early_wait", early_handle, list(outs.values()))
    for (n, tr), src, land in zip(EARLY, early_full[:-1], early_land[:-1]):
        sharded_update(n, tr, src, land)
    pm = lambda pre: _pack_misc([trav(pre, n, tr) for n, tr in MISC])
    rows = early_full[-1].shape[0] // N_DEV
    res = _adamw_sharded("adamw_misc", lax.dynamic_slice_in_dim(early_full[-1], me * rows, rows, axis=0),
                         early_land[-1].reshape(N_DEV, rows, PACK_COLS), pm(""), pm("m_"), pm("v_"), rows)
    emit(lambda val: [(n, _travel(b, tr).reshape(ins[n].shape)) for (n, tr), b in zip(MISC, _unpack_misc(val, misc_shapes))], res)
    small_src, small_land = _direct_wait("grad_small_wait", small_handle, res[0])
    small_all = [lax.dynamic_update_slice_in_dim(ld, sr, me * sr.shape[0], axis=0) for ld, sr in zip(small_land, small_src)]
    flat_small = [(n, r, c) for cls in SMALL_CLASSES for n, r, c in cls]
    ins = dict(ins, **{pre + LOSS_SLOT: jnp.zeros((1, 32), F32) for pre in ("", "m_", "v_")})
    res = _adamw_small(small_all, *[[view2(ins[pre + n], r, c) for n, r, c in flat_small] for pre in ("", "m_", "v_")])
    loss = res.pop(LOSS_SLOT)[0][0, 0]
    for n, _, _ in flat_small[:-1]:
        for tag, val in zip(("grad_", "delta_", "new_m_", "new_v_"), res[n]):
            outs[tag + n] = val.reshape(ins[n].shape)
    res = [loss, dx[None]]
    for tag in ("grad_", "delta_", "new_m_", "new_v_"):
        res += [outs[tag + n] for n in WEIGHT_NAMES]
    return tuple(res)


def _pick_rows(r):
    best = 8
    for b in range(8, 257, 8):
        if r % b == 0:
            best = b
    return best


def kernel(x, p, norm_mix, w_in, s5_lam_re, s5_lam_im, s5_log_step, s5_b_re, s5_b_im, s5_c_re, s5_c_im, s5_d, s5_glu_w, s5_glu_b, rw_shift_mu, rw_w0, rw_w2, rw_a0, rw_a2, rw_g2, rw_k_k, rw_k_a, rw_r_k, rw_ln_w, rw_ln_b, w_out, norm_ffn, ffn_w1, ffn_w3, ffn_w2, norm_ple, ple_gate_w, ple_up_w, final_norm, loss_target, m_norm_mix, m_w_in, m_s5_lam_re, m_s5_lam_im, m_s5_log_step, m_s5_b_re, m_s5_b_im, m_s5_c_re, m_s5_c_im, m_s5_d, m_s5_glu_w, m_s5_glu_b, m_rw_shift_mu, m_rw_w0, m_rw_w2, m_rw_a0, m_rw_a2, m_rw_g2, m_rw_k_k, m_rw_k_a, m_rw_r_k, m_rw_ln_w, m_rw_ln_b, m_w_out, m_norm_ffn, m_ffn_w1, m_ffn_w3, m_ffn_w2, m_norm_ple, m_ple_gate_w, m_ple_up_w, m_final_norm, v_norm_mix, v_w_in, v_s5_lam_re, v_s5_lam_im, v_s5_log_step, v_s5_b_re, v_s5_b_im, v_s5_c_re, v_s5_c_im, v_s5_d, v_s5_glu_w, v_s5_glu_b, v_rw_shift_mu, v_rw_w0, v_rw_w2, v_rw_a0, v_rw_a2, v_rw_g2, v_rw_k_k, v_rw_k_a, v_rw_r_k, v_rw_ln_w, v_rw_ln_b, v_w_out, v_norm_ffn, v_ffn_w1, v_ffn_w3, v_ffn_w2, v_norm_ple, v_ple_gate_w, v_ple_up_w, v_final_norm):
    return _kernel_impl(dict(zip(ARG_NAMES, (x, p, norm_mix, w_in, s5_lam_re, s5_lam_im, s5_log_step, s5_b_re, s5_b_im, s5_c_re, s5_c_im, s5_d, s5_glu_w, s5_glu_b, rw_shift_mu, rw_w0, rw_w2, rw_a0, rw_a2, rw_g2, rw_k_k, rw_k_a, rw_r_k, rw_ln_w, rw_ln_b, w_out, norm_ffn, ffn_w1, ffn_w3, ffn_w2, norm_ple, ple_gate_w, ple_up_w, final_norm, loss_target, m_norm_mix, m_w_in, m_s5_lam_re, m_s5_lam_im, m_s5_log_step, m_s5_b_re, m_s5_b_im, m_s5_c_re, m_s5_c_im, m_s5_d, m_s5_glu_w, m_s5_glu_b, m_rw_shift_mu, m_rw_w0, m_rw_w2, m_rw_a0, m_rw_a2, m_rw_g2, m_rw_k_k, m_rw_k_a, m_rw_r_k, m_rw_ln_w, m_rw_ln_b, m_w_out, m_norm_ffn, m_ffn_w1, m_ffn_w3, m_ffn_w2, m_norm_ple, m_ple_gate_w, m_ple_up_w, m_final_norm, v_norm_mix, v_w_in, v_s5_lam_re, v_s5_lam_im, v_s5_log_step, v_s5_b_re, v_s5_b_im, v_s5_c_re, v_s5_c_im, v_s5_d, v_s5_glu_w, v_s5_glu_b, v_rw_shift_mu, v_rw_w0, v_rw_w2, v_rw_a0, v_rw_a2, v_rw_g2, v_rw_k_k, v_rw_k_a, v_rw_r_k, v_rw_ln_w, v_rw_ln_b, v_w_out, v_norm_ffn, v_ffn_w1, v_ffn_w3, v_ffn_w2, v_norm_ple, v_ple_gate_w, v_ple_up_w, v_final_norm))))
```

```python
import jax
import jax.numpy as jnp
from jax import lax
from jax.experimental import pallas as pl
from jax.experimental.pallas import tpu as pltpu

F32 = jnp.float32
BF16 = jnp.bfloat16

D_MODEL = 1024
S5_WIDTH = 512
RW_WIDTH = 512
S5_GROUP = 16
S5_GROUPS = 32
S5_STATE = 64
S5_LANES = S5_GROUPS * S5_STATE
HEAD = 64
SHIFT_COLS = 1792
IN_COLS = 2304
FFN_HIDDEN = 2816
PLE_DIM = 256
RMS_EPS = 1e-6
GN_EPS = 64e-5
L2_EPS = 1e-12
CHUNK = 64
N_DEV = 8

ADAM_LR = 0.001
ADAM_B1 = 0.9
ADAM_B2 = 0.999
ADAM_EPS = 1e-08
ADAM_WD = 0.01
ADAM_STEP = 10

VMEM_LIMIT = 56 * 1024 * 1024
_ANY = pl.BlockSpec(memory_space=pl.ANY)


def _pcall(body, **kw):
    return pl.pallas_call(body, **kw)


def _cparams(n_grid):
    return pltpu.CompilerParams(dimension_semantics=("arbitrary",) * n_grid, vmem_limit_bytes=VMEM_LIMIT)


def _dot(a, b):
    return jnp.dot(a, b, preferred_element_type=F32)


def _dot_nt(a, b):
    return lax.dot_general(a, b, (((1,), (1,)), ((), ())), preferred_element_type=F32)


def _dot_tn(a, b):
    return lax.dot_general(a, b, (((0,), (0,)), ((), ())), preferred_element_type=F32)


def _mmc(w, diff=True, tr=False):
    fw, bw = (_dot_nt, _dot) if tr else (_dot, _dot_nt)
    if not diff:
        return lambda x: fw(x.astype(BF16), w)

    @jax.custom_vjp
    def f(x):
        return fw(x.astype(BF16), w)

    def fwd(x):
        return fw(x.astype(BF16), w), None

    def bwd(_, dy):
        return (bw(dy.astype(BF16), w),)

    f.defvjp(fwd, bwd)
    return f


def _split_dot(x, m, n_split):
    acc = None
    rem = x
    for s in range(n_split):
        part = rem.astype(BF16)
        t = _dot(part, m)
        acc = t if acc is None else acc + t
        if s + 1 < n_split:
            rem = rem - part.astype(F32)
    return acc


def _segsum(m, diff=True):
    if not diff:
        return lambda x: _split_dot(x, m, 2)

    @jax.custom_vjp
    def f(x):
        return _split_dot(x, m, 2)

    def fwd(x):
        return _split_dot(x, m, 2), None

    def bwd(_, dy):
        return (_split_dot(dy, m, 2),)

    f.defvjp(fwd, bwd)
    return f


def _head_indicator(n):
    r = lax.broadcasted_iota(jnp.int32, (n, n), 0) // HEAD
    c = lax.broadcasted_iota(jnp.int32, (n, n), 1) // HEAD
    return (r == c).astype(BF16)


def _rms(x, g):
    return x * lax.rsqrt(jnp.mean(x * x, axis=-1, keepdims=True) + RMS_EPS) * g


def _softplus(x):
    return jnp.maximum(x, 0.0) + jnp.log(1.0 + jnp.exp(-jnp.abs(x)))


def _sigmoid(x):
    return 1.0 / (1.0 + jnp.exp(-x))


def _gelu(x):
    return 0.5 * x * (1.0 + jnp.tanh(0.7978845608028654 * (x + 0.044715 * (x * x * x))))


def _tok_call(name, fn, L, TB, tok_in, const_in, tok_out, acc_out=(), deps=()):
    nb = L // TB
    g8 = TB // 8
    in_specs, args = [], []
    for spec in tok_in:
        if len(spec) == 1:
            arr = spec[0]
            in_specs.append(pl.BlockSpec((arr.shape[0], TB, HEAD), lambda i: (0, i, 0)))
            args.append(arr)
            continue
        arr, width, cb = spec[:3]
        mode = spec[3] if len(spec) > 3 else None
        if mode is None:
            in_specs.append(pl.BlockSpec((TB, width), lambda i, cb=cb: (i, cb)))
        elif mode == "prev":
            in_specs.append(pl.BlockSpec((8, width), lambda i, cb=cb: (jnp.maximum(i * g8 - 1, 0), cb)))
        else:
            in_specs.append(pl.BlockSpec((8, width), lambda i, cb=cb: (jnp.minimum((i + 1) * g8, L // 8 - 1), cb)))
        args.append(arr)
    for c in const_in:
        in_specs.append(pl.BlockSpec(c.shape, lambda i, nd=c.ndim: (0,) * nd, pipeline_mode=pl.Buffered(1)))
        args.append(c)
    for d in deps:
        in_specs.append(pl.BlockSpec(d.shape, lambda i, nd=d.ndim: (0,) * nd))
        args.append(d)
    out_shape, out_specs = [], []
    for width, dt in tok_out:
        if width == "heads":
            out_shape.append(jax.ShapeDtypeStruct((N_HEAD, L, HEAD), dt))
            out_specs.append(pl.BlockSpec((N_HEAD, TB, HEAD), lambda i: (0, i, 0)))
            continue
        out_shape.append(jax.ShapeDtypeStruct((L, width), dt))
        out_specs.append(pl.BlockSpec((TB, width), lambda i: (i, 0)))
    for shp in acc_out:
        out_shape.append(jax.ShapeDtypeStruct(shp, F32))
        out_specs.append(pl.BlockSpec(shp, lambda i, nd=len(shp): (0,) * nd))
    n_tok, n_const, n_to = len(tok_in), len(const_in), len(tok_out)

    def body(*refs):
        i = pl.program_id(0)
        tv = [r[...] if len(r.shape) == 2 else jnp.concatenate([r[h] for h in range(r.shape[0])], axis=1)
              for r in refs[:n_tok]]
        cv = [r[...] for r in refs[n_tok:n_tok + n_const]]
        orefs = refs[n_tok + n_const + len(deps):]
        outs = fn(i, tv, cv)
        for r, v in zip(orefs[:n_to], outs[:n_to]):
            if len(r.shape) == 3:
                for h in range(r.shape[0]):
                    r[h] = v[:, h * HEAD:(h + 1) * HEAD].astype(r.dtype)
            else:
                r[...] = v.astype(r.dtype)
        for r, v in zip(orefs[n_to:], outs[n_to:]):
            @pl.when(i == 0)
            def _(r=r):
                r[...] = jnp.zeros(r.shape, r.dtype)

            r[...] += v

    res = _pcall(body, name=name, grid=(nb,), in_specs=in_specs, out_specs=out_specs, out_shape=out_shape,
                 compiler_params=_cparams(1))(*args)
    return res


def _pick_block(n, cap):
    best = None
    for b in range(128, min(n, cap) + 1, 128):
        if n % b == 0:
            best = b
    return best if best is not None else n


def _mm_tn(name, a, b):
    T, M = a.shape
    N = b.shape[1]
    bm, bn, bt = _pick_block(M, 1536), _pick_block(N, 1536), _pick_block(T, 1024)

    def body(a_ref, b_ref, o_ref):
        t = pl.program_id(2)

        @pl.when(t == 0)
        def _():
            o_ref[...] = jnp.zeros(o_ref.shape, F32)

        o_ref[...] += _dot_tn(a_ref[...].astype(BF16), b_ref[...].astype(BF16))

    return _pcall(body, name=name, grid=(M // bm, N // bn, T // bt),
                  in_specs=[pl.BlockSpec((bt, bm), lambda m, n, t: (t, m)), pl.BlockSpec((bt, bn), lambda m, n, t: (t, n))],
                  out_specs=pl.BlockSpec((bm, bn), lambda m, n, t: (m, n)),
                  out_shape=jax.ShapeDtypeStruct((M, N), F32), compiler_params=_cparams(3))(a, b)


def _s5_param_fn(lam_re, lam_im, log_step, bt_re, bt_im):
    dt = jnp.exp(log_step)
    e = jnp.exp(lam_re * dt)
    lb_re = e * jnp.cos(lam_im * dt)
    lb_im = e * jnp.sin(lam_im * dt)
    den = lam_re * lam_re + lam_im * lam_im
    nr, ni = lb_re - 1.0, lb_im
    co_re = (nr * lam_re + ni * lam_im) / den
    co_im = (ni * lam_re - nr * lam_im) / den
    cr, ci = co_re[:, None, :], co_im[:, None, :]
    return lb_re, lb_im, cr * bt_re - ci * bt_im, cr * bt_im + ci * bt_re


def _s5_param_fwd(lam_re, lam_im, log_step, bt_re, bt_im):
    def body(a, b, c, d, e, o1, o2, o3, o4):
        r = _s5_param_fn(a[...], b[...], c[...], d[...], e[...])
        o1[...], o2[...], o3[...], o4[...] = r

    sh = jax.ShapeDtypeStruct
    return _pcall(body, name="s5_param_fwd",
                  out_shape=[sh(lam_re.shape, F32), sh(lam_re.shape, F32), sh(bt_re.shape, F32), sh(bt_re.shape, F32)])(
        lam_re, lam_im, log_step, bt_re, bt_im)


def _s5_param_bwd(lam_re, lam_im, log_step, bt_re, bt_im, d_lb_re, d_lb_im, d_bb_re, d_bb_im):
    def body(a, b, c, d, e, g1, g2, g3, g4, o1, o2, o3, o4, o5):
        _, vjp = jax.vjp(_s5_param_fn, a[...], b[...], c[...], d[...], e[...])
        r = vjp((g1[...], g2[...], g3[...], g4[...]))
        o1[...], o2[...], o3[...], o4[...], o5[...] = r

    sh = jax.ShapeDtypeStruct
    return _pcall(body, name="s5_param_bwd",
                  out_shape=[sh(lam_re.shape, F32), sh(lam_re.shape, F32), sh(log_step.shape, F32),
                             sh(bt_re.shape, F32), sh(bt_re.shape, F32)])(
        lam_re, lam_im, log_step, bt_re, bt_im, d_lb_re, d_lb_im, d_bb_re, d_bb_im)


def _cmul(ar, ai, br, bi):
    return ar * br - ai * bi, ar * bi + ai * br


def _scan_consts(lr, li, reverse):
    n = lr.shape[1]
    sub = lax.broadcasted_iota(jnp.int32, (8, n), 0)
    pows = [(lr, li)]
    for _ in range(7):
        pows.append(_cmul(pows[-1][0], pows[-1][1], lr, li))
    steps = []
    for s in (1, 2, 4):
        m = (sub < 8 - s) if reverse else (sub >= s)
        pr, pi = pows[s - 1]
        steps.append((s, jnp.where(m, jnp.broadcast_to(pr, (8, n)), 0.0), jnp.where(m, jnp.broadcast_to(pi, (8, n)), 0.0)))
    wr = jnp.zeros((8, n), F32)
    wi = jnp.zeros((8, n), F32)
    for r in range(8):
        e = (8 - r) if reverse else (r + 1)
        wr = jnp.where(sub == r, jnp.broadcast_to(pows[e - 1][0], (8, n)), wr)
        wi = jnp.where(sub == r, jnp.broadcast_to(pows[e - 1][1], (8, n)), wi)
    return steps, wr, wi


S5_Q = 4
S5_QL = S5_WIDTH // S5_Q
S5_QS = S5_LANES // S5_Q
S5_NT = S5_LANES // 128
S5_QT = S5_QS // 128


def _s5_power_table(lb_ref, pw_re, pw_im, seg):
    for j in range(S5_NT):
        lr = jnp.broadcast_to(lb_ref[0:1, j * 128:(j + 1) * 128], (8, 128))
        li = jnp.broadcast_to(lb_ref[1:2, j * 128:(j + 1) * 128], (8, 128))

        def step(i, c, lr=lr, li=li, j=j):
            pw_re[j, i] = c[0]
            pw_im[j, i] = c[1]
            return _cmul(c[0], c[1], lr, li)

        lax.fori_loop(0, seg, step, (lr, li))


def _seg_scan(sre, sim, carry, lb_ref, pw_re, pw_im, rows, reverse):
    seg = rows // 8
    sgn = -1.0 if reverse else 1.0
    sub = lax.broadcasted_iota(jnp.int32, (8, 128), 0)
    rows_at = lambda i: pl.ds(pl.multiple_of(i * 8, 8), 8)
    entering = {}
    half_tiles = S5_NT // 2
    for half in range(2):
        tiles = list(range(half * half_tiles, (half + 1) * half_tiles))
        lam8 = [(jnp.broadcast_to(lb_ref[0:1, j * 128:(j + 1) * 128], (8, 128)),
                 sgn * jnp.broadcast_to(lb_ref[1:2, j * 128:(j + 1) * 128], (8, 128))) for j in tiles]

        def p1(ii, c):
            i = (seg - 1 - ii) if reverse else ii
            out = []
            for n, j in enumerate(tiles):
                lr, li = lam8[n]
                cr, ci = c[2 * n], c[2 * n + 1]
                nr = lr * cr - li * ci + sre[j, rows_at(i), :]
                ni = lr * ci + li * cr + sim[j, rows_at(i), :]
                sre[j, rows_at(i), :] = nr
                sim[j, rows_at(i), :] = ni
                out += [nr, ni]
            return tuple(out)

        ends = lax.fori_loop(0, seg, p1, tuple(jnp.zeros((8, 128), F32) for _ in range(2 * len(tiles))))
        cs = []
        for n, j in enumerate(tiles):
            ls = slice(j * 128, (j + 1) * 128)
            steps, wr, wi = _scan_consts(pw_re[j, seg - 1][0:1, :], sgn * pw_im[j, seg - 1][0:1, :], reverse)
            tr, ti = ends[2 * n], ends[2 * n + 1]
            for sft, pr, pi in steps:
                sh = (8 - sft) if reverse else sft
                yr, yi = pltpu.roll(tr, sh, 0), pltpu.roll(ti, sh, 0)
                tr, ti = tr + pr * yr - pi * yi, ti + pr * yi + pi * yr
            cin_r, cin_i = carry[0:1, ls], carry[1:2, ls]
            tr, ti = tr + wr * cin_r - wi * cin_i, ti + wr * cin_i + wi * cin_r
            edge_out, edge_in, sh = (0, 7, 7) if reverse else (7, 0, 1)
            carry[0:1, ls] = tr[edge_out:edge_out + 1, :]
            carry[1:2, ls] = ti[edge_out:edge_out + 1, :]
            cr = jnp.where(sub == edge_in, jnp.broadcast_to(cin_r, (8, 128)), pltpu.roll(tr, sh, 0))
            ci = jnp.where(sub == edge_in, jnp.broadcast_to(cin_i, (8, 128)), pltpu.roll(ti, sh, 0))
            cs += [cr, ci]
            entering[j] = (cr, ci)

        def p2(i, _, lo=0, hi=len(tiles)):
            k = (seg - 1 - i) if reverse else i
            for n, j in list(enumerate(tiles))[lo:hi]:
                pr, pi = pw_re[j, k], pw_im[j, k]
                cr, ci = cs[2 * n], cs[2 * n + 1]
                if reverse:
                    sre[j, rows_at(i), :] = sre[j, rows_at(i), :] + pr * cr + pi * ci
                    sim[j, rows_at(i), :] = sim[j, rows_at(i), :] + pr * ci - pi * cr
                else:
                    sre[j, rows_at(i), :] = sre[j, rows_at(i), :] + pr * cr - pi * ci
                    sim[j, rows_at(i), :] = sim[j, rows_at(i), :] + pr * ci + pi * cr
            return 0

        for lo in range(0, len(tiles), 4):
            lax.fori_loop(0, seg, lambda i, c, lo=lo: p2(i, c, lo, lo + 4), 0, unroll=2)
    return entering


class _SegIO:
    def __init__(self, hbm, buf, sems, rows, width, col0=0):
        self.hbm, self.buf, self.sems, self.rows, self.seg, self.width, self.col0 = hbm, buf, sems, rows, rows // 8, width, col0

    def _copies(self, blk, slot, to_vmem):
        out = []
        for r in range(8):
            h = self.hbm.at[pl.ds(blk * self.rows + r * self.seg, self.seg), pl.ds(self.col0, self.width)]
            v = self.buf.at[slot, :, r, :]
            out.append(pltpu.make_async_copy(h, v, self.sems.at[slot, r]) if to_vmem
                       else pltpu.make_async_copy(v, h, self.sems.at[slot, r]))
        return out

    def start(self, blk, slot, to_vmem):
        for cp in self._copies(blk, slot, to_vmem):
            cp.start()

    def wait(self, blk, slot, to_vmem):
        for cp in self._copies(blk, slot, to_vmem):
            cp.wait()

    def value(self, slot):
        return self.buf[slot].reshape(self.rows, self.width)

    def store(self, slot, val):
        self.buf[slot] = val.reshape(self.seg, 8, self.width)


def _seg_pipeline(i, nb, blk_of, ins, outs, compute):
    slot = i % 2

    @pl.when(i == 0)
    def _():
        for io in ins:
            io.start(blk_of(0), 0, True)

    @pl.when(i + 1 < nb)
    def _():
        for io in ins:
            io.start(blk_of(i + 1), 1 - slot, True)

    for io in ins:
        io.wait(blk_of(i), slot, True)

    @pl.when(i >= 2)
    def _():
        for io in outs:
            io.wait(blk_of(i - 2), slot, False)

    compute(slot)
    for io in outs:
        io.start(blk_of(i), slot, False)

    @pl.when(i == nb - 1)
    def _():
        for io in outs:
            if nb >= 2:
                io.wait(blk_of(i - 1), 1 - slot, False)
            io.wait(blk_of(i), slot, False)


def _s5_scan_fwd(proj, bq_re, bq_im, cq_re, cq_im, lbar, dskip, L, TB):
    nb = L // TB
    seg = TB // 8

    def body(u_hbm, bre, bim, cre, cim, lb_ref, d_ref, y_hbm, ck_ref, sre, sim, carry, pw_re, pw_im,
             ubuf, ybuf, sem_u, sem_y):
        i = pl.program_id(0)
        u_io = _SegIO(u_hbm, ubuf, sem_u, TB, S5_WIDTH)
        y_io = _SegIO(y_hbm, ybuf, sem_y, TB, S5_WIDTH)

        @pl.when(i == 0)
        def _():
            carry[...] = jnp.zeros(carry.shape, F32)
            _s5_power_table(lb_ref, pw_re, pw_im, seg)

        ck_ref[0] = carry[...]

        def compute(slot):
            u = u_io.value(slot)
            ub = u.astype(BF16)
            for q in range(S5_Q):
                uq = ub[:, q * S5_QL:(q + 1) * S5_QL]
                vr, vi = _dot(uq, bre[q]), _dot(uq, bim[q])
                for jj in range(S5_QT):
                    sre[q * S5_QT + jj] = vr[:, jj * 128:(jj + 1) * 128]
                    sim[q * S5_QT + jj] = vi[:, jj * 128:(jj + 1) * 128]
            _seg_scan(sre, sim, carry, lb_ref, pw_re, pw_im, TB, False)
            ys = []
            for q in range(S5_Q):
                sl = slice(q * S5_QL, (q + 1) * S5_QL)
                sr = jnp.concatenate([sre[q * S5_QT + jj] for jj in range(S5_QT)], axis=1).astype(BF16)
                si = jnp.concatenate([sim[q * S5_QT + jj] for jj in range(S5_QT)], axis=1).astype(BF16)
                ys.append(_dot(sr, cre[q]) - _dot(si, cim[q]) + u[:, sl] * d_ref[:, sl])
            y_io.store(slot, jnp.concatenate(ys, axis=1))

        _seg_pipeline(i, nb, lambda st: st, [u_io], [y_io], compute)

    full = lambda a: pl.BlockSpec(a.shape, lambda i, nd=a.ndim: (0,) * nd)
    st = pltpu.VMEM((S5_NT, TB, 128), F32)
    pw = pltpu.VMEM((S5_NT, seg, 8, 128), F32)
    io = pltpu.VMEM((2, seg, 8, S5_WIDTH), F32)
    return _pcall(
        body, name="s5_scan_fwd", grid=(nb,),
        in_specs=[_ANY, full(bq_re), full(bq_im), full(cq_re), full(cq_im), full(lbar), full(dskip)],
        out_specs=[_ANY, pl.BlockSpec((1, 8, S5_LANES), lambda i: (i, 0, 0))],
        out_shape=[jax.ShapeDtypeStruct((L, S5_WIDTH), F32), jax.ShapeDtypeStruct((nb, 8, S5_LANES), F32)],
        scratch_shapes=[st, st, pltpu.VMEM((8, S5_LANES), F32), pw, pw, io, io,
                        pltpu.SemaphoreType.DMA((2, 8)), pltpu.SemaphoreType.DMA((2, 8))],
        compiler_params=_cparams(1))(proj, bq_re, bq_im, cq_re, cq_im, lbar, dskip)


def _s5_scan_bwd(proj, dy, ck, bq_re, bq_im, cq_re, cq_im, lbar, dskip, L, TB):
    nb = L // TB
    seg = TB // 8

    def body(u_hbm, dy_hbm, ck_ref, bre, bim, cre, cim, lb_ref, d_ref,
             du_hbm, dbre, dbim, dcre, dcim, dlb_ref, dd_ref, sre, sim, gre, gim, carry, gcarry, pw_re, pw_im,
             ubuf, dybuf, dubuf, sem_u, sem_dy, sem_du):
        i = pl.program_id(0)
        u_io = _SegIO(u_hbm, ubuf, sem_u, TB, S5_WIDTH)
        dy_io = _SegIO(dy_hbm, dybuf, sem_dy, TB, S5_WIDTH)
        du_io = _SegIO(du_hbm, dubuf, sem_du, TB, S5_WIDTH)

        @pl.when(i == 0)
        def _():
            gcarry[...] = jnp.zeros(gcarry.shape, F32)
            dbre[...] = jnp.zeros(dbre.shape, F32)
            dbim[...] = jnp.zeros(dbim.shape, F32)
            dcre[...] = jnp.zeros(dcre.shape, F32)
            dcim[...] = jnp.zeros(dcim.shape, F32)
            dlb_ref[...] = jnp.zeros(dlb_ref.shape, F32)
            dd_ref[...] = jnp.zeros(dd_ref.shape, F32)
            _s5_power_table(lb_ref, pw_re, pw_im, seg)

        def compute(slot):
            u = u_io.value(slot)
            dy_v = dy_io.value(slot)
            ub = u.astype(BF16)
            dyb = dy_v.astype(BF16)
            carry[...] = ck_ref[0]
            for q in range(S5_Q):
                uq = ub[:, q * S5_QL:(q + 1) * S5_QL]
                dq = dyb[:, q * S5_QL:(q + 1) * S5_QL]
                vr, vi = _dot(uq, bre[q]), _dot(uq, bim[q])
                hr, hi = _dot_nt(dq, cre[q]), -_dot_nt(dq, cim[q])
                for jj in range(S5_QT):
                    ls = slice(jj * 128, (jj + 1) * 128)
                    sre[q * S5_QT + jj] = vr[:, ls]
                    sim[q * S5_QT + jj] = vi[:, ls]
                    gre[q * S5_QT + jj] = hr[:, ls]
                    gim[q * S5_QT + jj] = hi[:, ls]
            entering = _seg_scan(sre, sim, carry, lb_ref, pw_re, pw_im, TB, False)
            _seg_scan(gre, gim, gcarry, lb_ref, pw_re, pw_im, TB, True)

            rows_at = lambda k: pl.ds(pl.multiple_of(k * 8, 8), 8)
            for half in range(2):
                tiles = list(range(half * (S5_NT // 2), (half + 1) * (S5_NT // 2)))
                acc0 = []
                for j in tiles:
                    er, ei = entering[j]
                    gr0, gi0 = gre[j, rows_at(0), :], gim[j, rows_at(0), :]
                    acc0 += [gr0 * er + gi0 * ei, gi0 * er - gr0 * ei]

                def acc_step(k, acc, tiles=tiles):
                    out = []
                    for n, j in enumerate(tiles):
                        gr, gi_ = gre[j, rows_at(k), :], gim[j, rows_at(k), :]
                        spr, spi = sre[j, rows_at(k - 1), :], sim[j, rows_at(k - 1), :]
                        out += [acc[2 * n] + gr * spr + gi_ * spi, acc[2 * n + 1] - gr * spi + gi_ * spr]
                    return tuple(out)

                acc = lax.fori_loop(1, seg, acc_step, tuple(acc0))
                for n, j in enumerate(tiles):
                    ls = slice(j * 128, (j + 1) * 128)
                    dlb_ref[0:1, ls] += jnp.sum(acc[2 * n], axis=0, keepdims=True)
                    dlb_ref[1:2, ls] += jnp.sum(acc[2 * n + 1], axis=0, keepdims=True)

            dd_ref[...] += jnp.sum(dy_v * u, axis=0, keepdims=True)
            dus = []
            for q in range(S5_Q):
                sl = slice(q * S5_QL, (q + 1) * S5_QL)
                cat = lambda ref: jnp.concatenate([ref[q * S5_QT + jj] for jj in range(S5_QT)], axis=1).astype(BF16)
                grq, giq = cat(gre), cat(gim)
                dus.append(_dot_nt(grq, bre[q]) + _dot_nt(giq, bim[q]) + dy_v[:, sl] * d_ref[:, sl])
                dbre[q] += _dot_tn(ub[:, sl], grq)
                dbim[q] += _dot_tn(ub[:, sl], giq)
                dcre[q] += _dot_tn(cat(sre), dyb[:, sl])
                dcim[q] -= _dot_tn(cat(sim), dyb[:, sl])
            du_io.store(slot, jnp.concatenate(dus, axis=1))

        _seg_pipeline(i, nb, lambda st: nb - 1 - st, [u_io, dy_io], [du_io], compute)

    full = lambda a: pl.BlockSpec(a.shape, lambda i, nd=a.ndim: (0,) * nd)
    sh = jax.ShapeDtypeStruct
    outs = [sh((L, S5_WIDTH), F32), sh(bq_re.shape, F32), sh(bq_im.shape, F32), sh(cq_re.shape, F32), sh(cq_im.shape, F32),
            sh((8, S5_LANES), F32), sh((1, S5_WIDTH), F32)]
    fo = lambda s: pl.BlockSpec(s.shape, lambda i, nd=len(s.shape): (0,) * nd)
    st = pltpu.VMEM((S5_NT, TB, 128), F32)
    pw = pltpu.VMEM((S5_NT, seg, 8, 128), F32)
    io = pltpu.VMEM((2, seg, 8, S5_WIDTH), F32)
    sem = pltpu.SemaphoreType.DMA((2, 8))
    return _pcall(
        body, name="s5_scan_bwd", grid=(nb,),
        in_specs=[_ANY, _ANY, pl.BlockSpec((1, 8, S5_LANES), lambda i: (nb - 1 - i, 0, 0)),
                  full(bq_re), full(bq_im), full(cq_re), full(cq_im), full(lbar), full(dskip)],
        out_specs=[_ANY] + [fo(s) for s in outs[1:]],
        out_shape=outs,
        scratch_shapes=[st] * 4 + [pltpu.VMEM((8, S5_LANES), F32)] * 2 + [pw, pw, io, io, io, sem, sem, sem],
        compiler_params=_cparams(1))(proj, dy, ck, bq_re, bq_im, cq_re, cq_im, lbar, dskip)


N_HEAD = RW_WIDTH // HEAD
_NN = (((2,), (1,)), ((0,), (0,)))
_NT = (((2,), (2,)), ((0,), (0,)))
_TN = (((1,), (1,)), ((0,), (0,)))


def _hi_lo(x):
    h = x.astype(BF16)
    return h, (x - h.astype(F32)).astype(BF16)


def _mm_acc(a, b, dims, passes=3):
    dg = lambda p, q: lax.dot_general(p, q, dims, preferred_element_type=F32)
    if passes == 1:
        return dg(a.astype(BF16), b.astype(BF16))
    ah, al = _hi_lo(a)
    bh, bl = _hi_lo(b)
    return dg(ah, bh) + dg(ah, bl) + dg(al, bh)


def _cumsum_rows(x, transpose):
    h, n, _ = x.shape
    ti = lax.broadcasted_iota(jnp.int32, (h, n, n), 1)
    tj = lax.broadcasted_iota(jnp.int32, (h, n, n), 2)
    m = ((tj >= ti) if transpose else (tj <= ti)).astype(BF16)
    acc, rem = None, x
    for s in range(3):
        part = rem.astype(BF16)
        t = lax.dot_general(m, part, _NN, preferred_element_type=F32)
        acc = t if acc is None else acc + t
        if s < 2:
            rem = rem - part.astype(F32)
    return acc


def _slices(x, axis, sizes):
    out, off = [], 0
    for n in sizes:
        out.append(lax.slice_in_dim(x, off, off + n, axis=axis))
        off += n
    return tuple(out)


def _cat_op(axis, sizes, diff):
    plain = lambda *xs: jnp.concatenate(xs, axis=axis)
    if not diff:
        return plain
    f = jax.custom_vjp(plain)
    f.defvjp(lambda *xs: (plain(*xs), None), lambda _, d: _slices(d, axis, sizes))
    return f


def _split_op(axis, sizes, diff):
    plain = lambda x: _slices(x, axis, sizes)
    if not diff:
        return plain
    f = jax.custom_vjp(plain)
    f.defvjp(lambda x: (plain(x), None), lambda _, d: (jnp.concatenate(d, axis=axis),))
    return f


def _mm_ops(diff, passes):
    mm = lambda a, b, dims: _mm_acc(a, b, dims, passes)
    if not diff:
        return (lambda a, b: mm(a, b, _NN), lambda a, b: mm(a, b, _NT), lambda a, b: mm(a, b, _TN))

    @jax.custom_vjp
    def nn(a, b):
        return mm(a, b, _NN)

    nn.defvjp(lambda a, b: (mm(a, b, _NN), (a, b)), lambda r, d: (mm(d, r[1], _NT), mm(r[0], d, _TN)))

    @jax.custom_vjp
    def nt(a, b):
        return mm(a, b, _NT)

    nt.defvjp(lambda a, b: (mm(a, b, _NT), (a, b)), lambda r, d: (mm(d, r[1], _NN), mm(d, r[0], _TN)))

    @jax.custom_vjp
    def tn(a, b):
        return mm(a, b, _TN)

    tn.defvjp(lambda a, b: (mm(a, b, _TN), (a, b)), lambda r, d: (mm(r[1], d, _NT), mm(r[0], d, _NN)))
    return nn, nt, tn


def _cums_op(diff):
    if not diff:
        return lambda x: _cumsum_rows(x, False)

    @jax.custom_vjp
    def cums(x):
        return _cumsum_rows(x, False)

    cums.defvjp(lambda x: (_cumsum_rows(x, False), None), lambda _, d: (_cumsum_rows(d, True),))
    return cums


WKV_PASSES = (1, 1, 1, 1, 1)


WKV_SUB = 4
WKV_BLOCK = CHUNK * WKV_SUB


def _wkv_block(s0, r, w, k, v, a, b, diff):
    p_pair, p_val, p_solve, p_out, p_state = WKV_PASSES
    cums = _cums_op(diff)
    _, nt_pair, _ = _mm_ops(diff, p_pair)
    nn_val, _, _ = _mm_ops(diff, p_val)
    nn_solve, _, _ = _mm_ops(diff, p_solve)
    nn_out, nt_out, _ = _mm_ops(diff, p_out)
    nn_state, _, tn_state = _mm_ops(diff, p_state)
    h, d, n, sub = s0.shape[0], s0.shape[2], CHUNK, WKV_SUB
    hb = h * sub
    to_chunks = lambda t: _cat_op(0, (h,) * sub, diff)(*_split_op(1, (n,) * sub, diff)(t))
    r, w, k, v, a, b = (to_chunks(t) for t in (r, w, k, v, a, b))
    cat_rows2 = _cat_op(1, (n, n), diff)
    cat_lanes2 = _cat_op(2, (n, n), diff)
    split_rows2 = _split_op(1, (n, n), diff)
    split_lanes2 = _split_op(2, (n, n), diff)
    ti = lax.broadcasted_iota(jnp.int32, (hb, n, n), 1)
    tj = lax.broadcasted_iota(jnp.int32, (hb, n, n), 2)
    incl, strict = tj <= ti, tj < ti
    logw = jnp.log(w)
    cum = cums(logw)
    g_in, g_ex, g_inv = jnp.exp(cum), jnp.exp(cum - logw), jnp.exp(-cum)
    ae, re, bi, ki = a * g_ex, r * g_in, b * g_inv, k * g_inv
    top, bot = split_rows2(nt_pair(cat_rows2(ae, re), cat_rows2(bi, ki)))
    tab, tak = split_lanes2(top)
    qb, qk = split_lanes2(bot)
    tab, tak = jnp.where(strict, tab, 0.0), jnp.where(strict, tak, 0.0)
    qb, qk = jnp.where(incl, qb, 0.0), jnp.where(incl, qk, 0.0)
    tak_v, qk_v = split_rows2(nn_val(cat_rows2(tak, qk), v))
    x = cat_lanes2(ae, tak_v)
    npow = tab
    steps = max(1, (n - 1).bit_length())
    for i in range(steps):
        x = x + nn_solve(npow, x)
        if i + 1 < steps:
            npow = nn_solve(npow, npow)
    ae_m, uc = split_lanes2(x)
    qx = nn_out(qb, x)
    q_ae, q_uc = split_lanes2(qx)
    re_m = re + q_ae
    yc = q_uc + qk_v
    g_end = jnp.exp(jnp.sum(logw, axis=1, keepdims=True))
    bg, kg = bi * g_end, ki * g_end
    tm = tn_state(ae_m, bg)
    sc = tn_state(cat_rows2(uc, v), cat_rows2(bg, kg))
    per_chunk = _split_op(0, (h,) * sub, diff)
    re_m, yc, g_end, tm, sc = (per_chunk(t) for t in (re_m, yc, g_end, tm, sc))
    ys, s = [], s0
    for i in range(sub):
        ys.append(nt_out(re_m[i], s) + yc[i])
        s = s * g_end[i] + nn_state(s, tm[i]) + sc[i]
    return _cat_op(1, (n,) * sub, diff)(*ys), s


def _wkv_fwd(r, w, k, v, a, b, L):
    nc = L // WKV_BLOCK

    def body(r_ref, w_ref, k_ref, v_ref, a_ref, b_ref, y_ref, ck_ref, s_ref):
        c = pl.program_id(0)

        @pl.when(c == 0)
        def _():
            s_ref[...] = jnp.zeros(s_ref.shape, F32)

        s0 = s_ref[...]
        ck_ref[0] = s0
        y, s1 = _wkv_block(s0, r_ref[...], w_ref[...], k_ref[...], v_ref[...], a_ref[...], b_ref[...], False)
        y_ref[...] = y
        s_ref[...] = s1

    blk = pl.BlockSpec((N_HEAD, WKV_BLOCK, HEAD), lambda c: (0, c, 0))
    return _pcall(
        body, name="wkv_fwd", grid=(nc,), in_specs=[blk] * 6,
        out_specs=[blk, pl.BlockSpec((1, N_HEAD, HEAD, HEAD), lambda c: (c, 0, 0, 0))],
        out_shape=[jax.ShapeDtypeStruct((N_HEAD, L, HEAD), F32), jax.ShapeDtypeStruct((nc, N_HEAD, HEAD, HEAD), F32)],
        scratch_shapes=[pltpu.VMEM((N_HEAD, HEAD, HEAD), F32)],
        compiler_params=_cparams(1))(r, w, k, v, a, b)


def _wkv_bwd(r, w, k, v, a, b, dy, ck, L, deps=()):
    nc = L // WKV_BLOCK

    def body(r_ref, w_ref, k_ref, v_ref, a_ref, b_ref, dy_ref, ck_ref, *rest):
        dr_ref, dw_ref, dk_ref, dv_ref, da_ref, db_ref, ds_ref = rest[len(deps):]
        c = pl.program_id(0)

        @pl.when(c == 0)
        def _():
            ds_ref[...] = jnp.zeros(ds_ref.shape, F32)

        _, vjp = jax.vjp(lambda *t: _wkv_block(*t, True), ck_ref[0], r_ref[...], w_ref[...], k_ref[...], v_ref[...],
                         a_ref[...], b_ref[...])
        g = vjp((dy_ref[...], ds_ref[...]))
        ds_ref[...] = g[0]
        for o_ref, val in zip((dr_ref, dw_ref, dk_ref, dv_ref, da_ref, db_ref), g[1:]):
            o_ref[...] = val

    blk = pl.BlockSpec((N_HEAD, WKV_BLOCK, HEAD), lambda c: (0, nc - 1 - c, 0))
    sh = jax.ShapeDtypeStruct((N_HEAD, L, HEAD), F32)
    return _pcall(
        body, name="wkv_bwd", grid=(nc,),
        in_specs=[blk] * 7 + [pl.BlockSpec((1, N_HEAD, HEAD, HEAD), lambda c: (nc - 1 - c, 0, 0, 0))]
        + [pl.BlockSpec(d.shape, lambda c, nd=d.ndim: (0,) * nd) for d in deps],
        out_specs=[blk] * 6, out_shape=[sh] * 6,
        scratch_shapes=[pltpu.VMEM((N_HEAD, HEAD, HEAD), F32)],
        compiler_params=_cparams(1))(r, w, k, v, a, b, dy, ck, *deps)


TB = 256


def _bf(x):
    return x.astype(BF16)


def _inproj_fwd(x, norm_mix, w_in, L, deps=()):
    def fn(i, tv, cv):
        xn = _rms(tv[0], cv[0])
        return _dot(_bf(xn), cv[1]), xn

    return _tok_call("inproj_fwd", fn, L, 2 * TB, [(x, D_MODEL, 0)], [norm_mix, w_in], [(IN_COLS, F32), (D_MODEL, BF16)],
                     deps=deps)


def _s5_post_fn(glu_w, wtop, diff=True):
    mg = _mmc(glu_w, diff)
    mt = _mmc(wtop, diff) if wtop is not None else None

    def f(y, glu_b, e):
        z = _gelu(y)
        out = z * _sigmoid(mg(z) + glu_b + e)
        res = mt(out) if mt is not None else out
        return res, (z, out)

    return f


def _s5_post_fwd(y, glu_w, glu_b, L):
    def fn(i, tv, cv):
        out, _ = _s5_post_fn(cv[0], None, False)(tv[0], cv[1], 0.0)
        return (out,)

    return _tok_call("s5_post_fwd", fn, L, 2 * TB, [(y, S5_WIDTH, 0)], [glu_w, glu_b], [(S5_WIDTH, F32)])[0]


def _s5_post_bwd(y, dh1, glu_w, glu_b, wtop, L, deps=()):
    def fn(i, tv, cv):
        e0 = jnp.zeros(tv[0].shape, F32)
        _, vjp, (z, out) = jax.vjp(_s5_post_fn(cv[0], cv[2]), tv[0], cv[1], e0, has_aux=True)
        dy, db, de = vjp(tv[1])
        return dy, db, _dot_tn(_bf(z), _bf(de)), _dot_tn(_bf(out), _bf(tv[1]))

    return _tok_call("s5_post_bwd", fn, L, 2 * TB, [(y, S5_WIDTH, 0), (dh1, D_MODEL, 0)], [glu_w, glu_b, wtop],
                     [(S5_WIDTH, F32)], [(1, S5_WIDTH), (S5_WIDTH, S5_WIDTH), (S5_WIDTH, D_MODEL)], deps=deps)


RW_COLBLK = ((RW_WIDTH, 1), (RW_WIDTH, 2), (RW_WIDTH, 3), (128, 16), (128, 17))
RW_MU = ((0, 512), (512, 1024), (1024, 1536), (1536, 1664), (1664, 1792))


def _rw_pre_fn(w2pad, a2pad, g2, diff=True):
    m_w, m_a, m_g = _mmc(w2pad, diff), _mmc(a2pad, diff), _mmc(g2, diff)
    seg = _segsum(_head_indicator(RW_WIDTH), diff)

    def f(zr, zk, zv, zwa, zg, w0, a0, k_k, k_a, e_w, e_a):
        wl_t = jnp.tanh(zwa)
        wlin = w0 + m_w(wl_t) + e_w
        w = -_softplus(-wlin) - 0.5
        decay = jnp.exp(-jnp.exp(w))
        a = _sigmoid(a0 + m_a(zwa) + e_a)
        sg = _sigmoid(zg)
        g = m_g(sg)
        kk = zk * k_k
        kkn = kk / jnp.maximum(jnp.sqrt(seg(kk * kk)), L2_EPS)
        kf = zk * (1.0 + (a - 1.0) * k_a)
        return (zr, decay, kf, zv, -kkn, kkn * a, g), (wl_t, sg)

    return f


def _rw_shifted(i, tv, mu):
    sub = lax.broadcasted_iota(jnp.int32, (tv[0].shape[0], 1), 0)
    zs, dif = [], []
    for n in range(5):
        z = tv[n]
        last = jnp.where(i == 0, 0.0, tv[5 + n][7:8, :])
        prev = jnp.where(sub == 0, last, pltpu.roll(z, 1, 0))
        m = mu[:, RW_MU[n][0]:RW_MU[n][1]]
        zs.append(z + (prev - z) * m)
        dif.append(prev - z)
    return zs, dif


def _rw_tok_in(proj):
    return [(proj, wd, cb) for wd, cb in RW_COLBLK] + [(proj, wd, cb, "prev") for wd, cb in RW_COLBLK]


def _rw_pre_fwd(proj, mu, w0, a0, k_k, k_a, w2pad, a2pad, g2, L):
    def fn(i, tv, cv):
        zs, _ = _rw_shifted(i, tv, cv[0])
        outs, _ = _rw_pre_fn(cv[5], cv[6], cv[7], False)(*zs, cv[1], cv[2], cv[3], cv[4], 0.0, 0.0)
        return outs

    return _tok_call("rw_pre_fwd", fn, L, 2 * TB, _rw_tok_in(proj), [mu, w0, a0, k_k, k_a, w2pad, a2pad, g2],
                     [("heads", F32)] * 6 + [(RW_WIDTH, F32)])


def _rw_pre_bwd(proj, cots, mu, w0, a0, k_k, k_a, w2pad, a2pad, g2, L):
    def fn(i, tv, cv):
        zs, dif = _rw_shifted(i, tv[:10], cv[0])
        dr1, dr2, dw, dk1, dk2, dv1, dv2, da, db, dg = tv[10:]
        e0 = jnp.zeros((TB, RW_WIDTH), F32)
        _, vjp, (wl_t, sg) = jax.vjp(_rw_pre_fn(cv[5], cv[6], cv[7]), *zs, cv[1], cv[2], cv[3], cv[4], e0, e0, has_aux=True)
        g = vjp((dr1 + dr2, dw, dk1 + dk2, dv1 + dv2, da, db, dg))
        dzs = jnp.concatenate(g[:5], axis=1)
        dmu = jnp.concatenate([jnp.sum(g[n] * dif[n], axis=0, keepdims=True) for n in range(5)], axis=1)
        lora = (_dot_tn(_bf(wl_t), _bf(g[9])), _dot_tn(_bf(zs[3]), _bf(g[10])), _dot_tn(_bf(sg), _bf(dg)))
        return (dzs, dmu, g[5], g[6], g[7], g[8]) + lora

    tok_in = _rw_tok_in(proj) + [((c,) if c.ndim == 3 else (c, RW_WIDTH, 0)) for c in cots]
    return _tok_call("rw_pre_bwd", fn, L, TB, tok_in, [mu, w0, a0, k_k, k_a, w2pad, a2pad, g2],
                     [(SHIFT_COLS, F32)], [(1, SHIFT_COLS)] + [(1, RW_WIDTH)] * 4 + [(128, RW_WIDTH)] * 3)


def _rw_post_fn(wbot, diff=True):
    seg = _segsum(_head_indicator(RW_WIDTH), diff)
    mb = _mmc(wbot, diff) if wbot is not None else None

    def f(y, r, kf, v, g, ln_w, ln_b, r_k):
        mean = seg(y) * (1.0 / HEAD)
        yc = y - mean
        var = seg(yc * yc) * (1.0 / HEAD)
        yn = yc * lax.rsqrt(var + GN_EPS) * ln_w + ln_b
        bonus = seg(r * kf * r_k) * v
        out = (yn + bonus) * g
        res = mb(out) if mb is not None else out
        return res, out

    return f


def _rw_post_fwd(y, r, kf, v, g, ln_w, ln_b, r_k, L):
    def fn(i, tv, cv):
        out, _ = _rw_post_fn(None, False)(*tv, *cv)
        return (out,)

    return _tok_call("rw_post_fwd", fn, L, 2 * TB, [(t,) for t in (y, r, kf, v)] + [(g, RW_WIDTH, 0)], [ln_w, ln_b, r_k],
                     [(RW_WIDTH, F32)])[0]


def _rw_post_bwd(y, r, kf, v, g, dh1, ln_w, ln_b, r_k, wbot, L):
    def fn(i, tv, cv):
        _, vjp, out = jax.vjp(_rw_post_fn(cv[3]), *tv[:5], cv[0], cv[1], cv[2], has_aux=True)
        gr = vjp(tv[5])
        return gr[0], gr[1], gr[2], gr[3], gr[4], gr[5], gr[6], gr[7], _dot_tn(_bf(out), _bf(tv[5]))

    return _tok_call("rw_post_bwd", fn, L, 2 * TB, [(t,) for t in (y, r, kf, v)] + [(g, RW_WIDTH, 0), (dh1, D_MODEL, 0)],
                     [ln_w, ln_b, r_k, wbot], [("heads", F32)] + [(RW_WIDTH, F32)] * 4,
                     [(1, RW_WIDTH)] * 3 + [(RW_WIDTH, D_MODEL)])


def _ffn_fn(w1, w3, w2, diff=True):
    m1, m3, m2 = _mmc(w1, diff), _mmc(w3, diff), _mmc(w2, diff)

    def f(h1, norm_ffn, e1, e3):
        hn = _rms(h1, norm_ffn)
        a1 = m1(hn) + e1
        a3 = m3(hn) + e3
        hm = a1 * _sigmoid(a1) * a3
        return h1 + m2(hm), (hn, hm)

    return f


TB_FFN = 256


def _mixffn_fwd(x, s5_out, rw_out, wtop, wbot, norm_ffn, w1, w3, w2, L):
    def fn(i, tv, cv):
        h1 = tv[0] + _dot(_bf(tv[1]), cv[0]) + _dot(_bf(tv[2]), cv[1])
        h2, _ = _ffn_fn(cv[3], cv[4], cv[5], False)(h1, cv[2], 0.0, 0.0)
        return h1, h2

    return _tok_call("mixffn_fwd", fn, L, 2 * TB_FFN, [(x, D_MODEL, 0), (s5_out, S5_WIDTH, 0), (rw_out, RW_WIDTH, 0)],
                     [wtop, wbot, norm_ffn, w1, w3, w2], [(D_MODEL, F32), (D_MODEL, F32)])


def _ffn_bwd(h1, dh2, norm_ffn, w1, w3, w2, L):
    def fn(i, tv, cv):
        e0 = jnp.zeros((TB_FFN, FFN_HIDDEN), F32)
        _, vjp, (hn, hm) = jax.vjp(_ffn_fn(cv[1], cv[2], cv[3]), tv[0], cv[0], e0, e0, has_aux=True)
        dh1, dn, d1, d3 = vjp(tv[1])
        return dh1, d1, d3, hm, hn, dn

    return _tok_call("ffn_bwd", fn, L, TB_FFN, [(h1, D_MODEL, 0), (dh2, D_MODEL, 0)], [norm_ffn, w1, w3, w2],
                     [(D_MODEL, F32), (FFN_HIDDEN, BF16), (FFN_HIDDEN, BF16), (FFN_HIDDEN, BF16), (D_MODEL, BF16)],
                     [(1, D_MODEL)])


def _ple_loss_fb(h2, p, target, norm_ple, final_norm, wg, wu, L):
    def fn(i, tv, cv):
        mgate, mup = _mmc(cv[2]), _mmc(cv[3], False)

        def f(h2_, norm_ple_, final_norm_, eg, eu):
            hn = _rms(h2_, norm_ple_)
            gate = _sigmoid(mgate(hn) + eg)
            h3 = h2_ + gate * (mup(tv[1]) + eu)
            out = _rms(h3, final_norm_)
            d = out - tv[2]
            return 0.5 * jnp.sum(jnp.mean(d * d, axis=-1, keepdims=True)), hn

        e0 = jnp.zeros(tv[0].shape, F32)
        loss, vjp, hn = jax.vjp(f, tv[0], cv[0], cv[1], e0, e0, has_aux=True)
        dh2, dnp, dfn, deg, deu = vjp(jnp.ones((), F32))
        return (dh2, dh2, jnp.full((8, 128), loss, F32), dnp, dfn,
                _dot_tn(_bf(hn), _bf(deg)), _dot_tn(_bf(tv[1]), _bf(deu)))

    return _tok_call("ple_loss_fb", fn, L, 2 * TB, [(h2, D_MODEL, 0), (p, PLE_DIM, 0), (target, D_MODEL, 0)],
                     [norm_ple, final_norm, wg, wu], [(D_MODEL, F32), (D_MODEL, BF16)],
                     [(8, 128), (1, D_MODEL), (1, D_MODEL), (D_MODEL, D_MODEL), (PLE_DIM, D_MODEL)])


def _inproj_bwd(x, dh1, du, dzs, norm_mix, mu, w_u, w_z, L):
    nb = L // TB

    def fn(i, tv, cv):
        sub = lax.broadcasted_iota(jnp.int32, (TB, 1), 0)
        m = cv[1]
        b = tv[3] * m
        nxt = jnp.where(i == nb - 1, 0.0, tv[4][0:1, :] * m)
        dz = tv[3] * (1.0 - m) + jnp.where(sub == TB - 1, nxt, pltpu.roll(b, TB - 1, 0))
        dub, dzb = _bf(tv[2]), _bf(dz)
        dxn = _dot_nt(dub, cv[2]) + _dot_nt(dzb, cv[3])
        _, vjp = jax.vjp(_rms, tv[0], cv[0])
        dx, dn = vjp(dxn)
        return tv[1] + dx, jnp.concatenate([dub, dzb], axis=1), dn

    return _tok_call("inproj_bwd", fn, L, TB,
                     [(x, D_MODEL, 0), (dh1, D_MODEL, 0), (du, S5_WIDTH, 0), (dzs, SHIFT_COLS, 0), (dzs, SHIFT_COLS, 0, "next")],
                     [norm_mix, mu, w_u, w_z], [(D_MODEL, F32), (IN_COLS, BF16)], [(1, D_MODEL)])


def _eye8(dt):
    return jnp.eye(8, dtype=dt)


def _quarter_b(bb):
    return jnp.einsum("hg,qgcp->qhcgp", _eye8(bb.dtype), bb.reshape(S5_Q, 8, S5_GROUP, S5_STATE)).reshape(S5_Q, S5_QL, S5_QS)


def _unquarter_b(d):
    return jnp.einsum("qhcgp,hg->qgcp", d.reshape(S5_Q, 8, S5_GROUP, 8, S5_STATE), _eye8(d.dtype)).reshape(
        S5_GROUPS, S5_GROUP, S5_STATE)


def _quarter_c(c):
    return jnp.einsum("gh,qgcp->qgphc", _eye8(c.dtype), c.reshape(S5_Q, 8, S5_GROUP, S5_STATE)).reshape(S5_Q, S5_QS, S5_QL)


def _unquarter_c(d):
    return jnp.einsum("qgphc,gh->qgcp", d.reshape(S5_Q, 8, S5_STATE, 8, S5_GROUP), _eye8(d.dtype)).reshape(
        S5_GROUPS, S5_GROUP, S5_STATE)


def _local_step(x, p, target, W, late_weights=None, grads_ready=None, first_dep=None):
    L = x.shape[0]
    r2 = lambda v: v.reshape(1, -1)
    w_in = W["w_in"]
    w2pad = jnp.pad(W["rw_w2"], ((0, 64), (0, 0)))
    a2pad = jnp.pad(W["rw_a2"], ((64, 0), (0, 0)))
    mu = r2(W["rw_shift_mu"])
    rw_vec = [r2(W[n]) for n in ("rw_w0", "rw_a0", "rw_k_k", "rw_k_a")]
    ln_w, ln_b, r_k = r2(W["rw_ln_w"]), r2(W["rw_ln_b"]), r2(W["rw_r_k"])

    lam_re, lam_im = W["s5_lam_re"], W["s5_lam_im"]
    log_step = W["s5_log_step"].reshape(S5_GROUPS, 1)
    bt_re, bt_im = W["s5_b_re"].transpose(0, 2, 1), W["s5_b_im"].transpose(0, 2, 1)
    lb_re, lb_im, bb_re, bb_im = _s5_param_fwd(lam_re, lam_im, log_step, bt_re, bt_im)
    bq_re, bq_im = _quarter_b(bb_re).astype(BF16), _quarter_b(bb_im).astype(BF16)
    cq_re, cq_im = _quarter_c(W["s5_c_re"]).astype(BF16), _quarter_c(W["s5_c_im"]).astype(BF16)
    lbar = jnp.concatenate([lb_re.reshape(1, -1), lb_im.reshape(1, -1), jnp.zeros((6, S5_LANES), F32)], axis=0)
    dskip = r2(W["s5_d"])
    glu_b = r2(W["s5_glu_b"])
    norm_mix, norm_ffn, norm_ple, final_norm = (r2(W[n]) for n in ("norm_mix", "norm_ffn", "norm_ple", "final_norm"))

    proj, xn = _inproj_fwd(x, norm_mix, w_in, L, () if first_dep is None else (first_dep,))
    y_s5, ck5 = _s5_scan_fwd(proj, bq_re, bq_im, cq_re, cq_im, lbar, dskip, L, TB)
    s5_out = _s5_post_fwd(y_s5, W["s5_glu_w"], glu_b, L)
    r, wd, kf, v, a_s, b_s, g = _rw_pre_fwd(proj, mu, *rw_vec, w2pad, a2pad, W["rw_g2"], L)
    scan_in = (r, wd, kf, v, a_s, b_s)
    y_wkv, ckw = _wkv_fwd(*scan_in, L)
    rw_out = _rw_post_fwd(y_wkv, r, kf, v, g, ln_w, ln_b, r_k, L)
    if late_weights is not None:
        W = dict(W, **late_weights(rw_out))
    wtop, wbot = W["w_out"][:S5_WIDTH], W["w_out"][S5_WIDTH:]
    h1, h2 = _mixffn_fwd(x, s5_out, rw_out, wtop, wbot, norm_ffn, W["ffn_w1"], W["ffn_w3"], W["ffn_w2"], L)

    G = {}
    dh2, dh2_bf, loss_acc, G["norm_ple"], G["final_norm"], G["ple_gate_w"], G["ple_up_w"] = _ple_loss_fb(
        h2, p, target, norm_ple, final_norm, W["ple_gate_w"], W["ple_up_w"], L)
    dh1, da1, da3, hm, hn_ffn, G["norm_ffn"] = _ffn_bwd(h1, dh2, norm_ffn, W["ffn_w1"], W["ffn_w3"], W["ffn_w2"], L)
    G["ffn_w1"] = _mm_tn("dw_ffn_w1", hn_ffn, da1)
    G["ffn_w3"] = _mm_tn("dw_ffn_w3", hn_ffn, da3)
    G["ffn_w2"] = _mm_tn("dw_ffn_w2", hm, dh2_bf)
    dep_a = grads_ready(0, G) if grads_ready is not None else None
    dy_s5, G["s5_glu_b"], G["s5_glu_w"], d_wtop = _s5_post_bwd(y_s5, dh1, W["s5_glu_w"], glu_b, wtop, L,
                                                               () if dep_a is None else (dep_a,))
    dy_wkv, dr2, dk2, dv2, dg, G["rw_ln_w"], G["rw_ln_b"], G["rw_r_k"], d_wbot = _rw_post_bwd(
        y_wkv, r, kf, v, g, dh1, ln_w, ln_b, r_k, wbot, L)
    G["w_out"] = jnp.concatenate([d_wtop, d_wbot], axis=0)
    dep = grads_ready(1, G) if grads_ready is not None else None
    dr1, dwd, dk1, dv1, da_s, db_s = _wkv_bwd(*scan_in, dy_wkv, ckw, L, () if dep is None else (dep,))
    (dzs, G["rw_shift_mu"], G["rw_w0"], G["rw_a0"], G["rw_k_k"], G["rw_k_a"], d_w2pad, d_a2pad, G["rw_g2"]) = _rw_pre_bwd(
        proj, (dr1, dr2, dwd, dk1, dk2, dv1, dv2, da_s, db_s, dg), mu, *rw_vec, w2pad, a2pad, W["rw_g2"], L)
    G["rw_w2"], G["rw_a2"] = d_w2pad[:64], d_a2pad[64:]
    du, dbq_re, dbq_im, dcq_re, dcq_im, dlbar, G["s5_d"] = _s5_scan_bwd(
        proj, dy_s5, ck5, bq_re, bq_im, cq_re, cq_im, lbar, dskip, L, TB)
    G["s5_c_re"], G["s5_c_im"] = _unquarter_c(dcq_re), _unquarter_c(dcq_im)
    d_lam_re, d_lam_im, d_ls, d_bt_re, d_bt_im = _s5_param_bwd(
        lam_re, lam_im, log_step, bt_re, bt_im, dlbar[0].reshape(S5_GROUPS, S5_STATE), dlbar[1].reshape(S5_GROUPS, S5_STATE),
        _unquarter_b(dbq_re), _unquarter_b(dbq_im))
    G["s5_lam_re"], G["s5_lam_im"], G["s5_log_step"] = d_lam_re, d_lam_im, d_ls.reshape(S5_GROUPS)
    G["s5_b_re"], G["s5_b_im"] = d_bt_re.transpose(0, 2, 1), d_bt_im.transpose(0, 2, 1)
    dx, dproj, G["norm_mix"] = _inproj_bwd(x, dh1, du, dzs, norm_mix, mu, w_in[:, :S5_WIDTH], w_in[:, S5_WIDTH:], L)
    G["w_in"] = _mm_tn("dw_in", xn, dproj)
    return loss_acc[0, 0], dx, G


def _all_gather(name, shards):
    nt = len(shards)

    def body(*refs):
        x_refs, out_refs = refs[:nt], refs[nt:2 * nt]
        send_sems, recv_sems, local_sems = refs[2 * nt:]
        x, y, c = lax.axis_index("x"), lax.axis_index("y"), lax.axis_index("c")
        me, sibling = (x, y, c), (x, y, 1 - c)
        chips = [(1 - x, y), (x, 1 - y), (1 - x, 1 - y)]

        def rows(t, px, py, pc):
            m_per = shards[t].shape[0]
            return out_refs[t].at[pl.ds((4 * px + 2 * py + pc) * m_per, m_per), :]

        def copy(t, k, block, to, src=None):
            return pltpu.make_async_remote_copy(
                src_ref=rows(t, *block) if src is None else src, dst_ref=rows(t, *block),
                send_sem=send_sems.at[7 * t + k], recv_sem=recv_sems.at[7 * t + k],
                device_id=to, device_id_type=pl.DeviceIdType.MESH)

        mine = [pltpu.make_async_copy(x_refs[t], rows(t, *me), local_sems.at[t]) for t in range(nt)]
        for cp in mine:
            cp.start()
        first = []
        for t in range(nt):
            first.append(copy(t, 0, me, sibling, src=x_refs[t]))
            first += [copy(t, 1 + j, me, (*chip, c), src=x_refs[t]) for j, chip in enumerate(chips)]
        for cp in first:
            cp.start()
        passed = []
        for t in range(nt):
            for j, chip in enumerate(chips):
                copy(t, 1 + j, (*chip, c), me).wait_recv()
                fwd = copy(t, 4 + j, (*chip, c), sibling)
                fwd.start()
                passed.append(fwd)
        for t in range(nt):
            copy(t, 0, sibling, me).wait_recv()
            for j, chip in enumerate(chips):
                copy(t, 4 + j, (*chip, 1 - c), me).wait_recv()
        for cp in first + passed:
            cp.wait_send()
        for cp in mine:
            cp.wait()

    return _pcall(body, name=name,
                  out_shape=[jax.ShapeDtypeStruct((N_DEV * a.shape[0], a.shape[1]), a.dtype) for a in shards],
                  in_specs=[_ANY] * nt, out_specs=[_ANY] * nt,
                  scratch_shapes=[pltpu.SemaphoreType.DMA((7 * nt,)), pltpu.SemaphoreType.DMA((7 * nt,)),
                                  pltpu.SemaphoreType.DMA((nt,))])(*shards)


_HBM = pl.BlockSpec(memory_space=pltpu.HBM)
_SEM = pl.BlockSpec(memory_space=pltpu.SEMAPHORE)
_EFFECT = pltpu.SideEffectType.DATAFLOW_SIDE_EFFECTING


def _peer_of(k):
    x, y, c = lax.axis_index("x"), lax.axis_index("y"), lax.axis_index("c")
    px, py, pc = x ^ ((k >> 2) & 1), y ^ ((k >> 1) & 1), c ^ (k & 1)
    return (px, py, pc), 4 * px + 2 * py + pc, 4 * x + 2 * y + c


def _direct_copy(t, k, src_refs, land_refs, send_sems, recv_sems, rows_of, gather):
    dev, peer, me = _peer_of(k)
    m = rows_of[t]
    src = src_refs[t] if gather else src_refs[t].at[pl.ds(peer * m, m), :]
    return pltpu.make_async_remote_copy(
        src_ref=src, dst_ref=land_refs[t].at[pl.ds(me * m, m), :],
        send_sem=send_sems.at[7 * t + k - 1], recv_sem=recv_sems.at[7 * t + k - 1],
        device_id=dev, device_id_type=pl.DeviceIdType.MESH)


def _direct_landing(t, k, src_refs, land_refs, send_sems, recv_sems, rows_of, gather):
    dev, peer, me = _peer_of(k)
    m = rows_of[t]
    src = src_refs[t] if gather else src_refs[t].at[pl.ds(me * m, m), :]
    return pltpu.make_async_remote_copy(
        src_ref=src, dst_ref=land_refs[t].at[pl.ds(peer * m, m), :],
        send_sem=send_sems.at[7 * t + k - 1], recv_sem=recv_sems.at[7 * t + k - 1],
        device_id=dev, device_id_type=pl.DeviceIdType.MESH)


def _direct_start(name, srcs, gather, dep=None):
    nt = len(srcs)
    rows_of = [a.shape[0] if gather else a.shape[0] // N_DEV for a in srcs]
    lands = [pltpu.with_memory_space_constraint(lax.empty((N_DEV * m, a.shape[1]), a.dtype), pltpu.HBM)
             for a, m in zip(srcs, rows_of)]

    n_dep = 0 if dep is None else 1

    def body(*refs):
        src_refs, land_refs = refs[:nt], refs[nt:2 * nt]
        send_sems, recv_sems = refs[2 * nt + n_dep], refs[2 * nt + n_dep + 1]
        token = refs[-1]
        for t in range(nt):
            for k in range(1, N_DEV):
                _direct_copy(t, k, src_refs, land_refs, send_sems, recv_sems, rows_of, gather).start()
        token[...] = jnp.zeros(token.shape, F32)

    out = _pcall(
        body, name=name,
        out_shape=(pltpu.SemaphoreType.DMA((7 * nt,)), pltpu.SemaphoreType.DMA((7 * nt,)),
                   *[pltpu.HBM(a.shape, a.dtype) for a in srcs], *[pltpu.HBM(a.shape, a.dtype) for a in lands],
                   jax.ShapeDtypeStruct((8, 128), F32)),
        in_specs=(_HBM,) * (2 * nt) + (pl.BlockSpec(memory_space=pl.ANY),) * n_dep,
        out_specs=(_SEM, _SEM) + (_HBM,) * (2 * nt) + (pl.BlockSpec(memory_space=pltpu.VMEM),),
        input_output_aliases={i: 2 + i for i in range(2 * nt)},
        compiler_params=pltpu.CompilerParams(has_side_effects=_EFFECT),
    )(*[pltpu.with_memory_space_constraint(a, pltpu.HBM) for a in srcs], *lands, *(() if dep is None else (dep,)))
    return (out[0], out[1], list(out[2:2 + nt]), list(out[2 + nt:2 + 2 * nt]), rows_of, gather), out[-1]


def _direct_wait(name, handle, after):
    send_sems, recv_sems, srcs, lands, rows_of, gather = handle
    nt = len(srcs)
    after = list(after) if isinstance(after, (list, tuple)) else [after]

    def body(*refs):
        src_refs, land_refs = refs[:nt], refs[nt:2 * nt]
        s_sems, r_sems = refs[2 * nt], refs[2 * nt + 1]
        for t in range(nt):
            for k in range(1, N_DEV):
                _direct_copy(t, k, src_refs, land_refs, s_sems, r_sems, rows_of, gather).wait_send()
                _direct_landing(t, k, src_refs, land_refs, s_sems, r_sems, rows_of, gather).wait_recv()

    out = _pcall(
        body, name=name,
        out_shape=tuple(pltpu.HBM(a.shape, a.dtype) for a in srcs) + tuple(pltpu.HBM(a.shape, a.dtype) for a in lands),
        in_specs=(_HBM,) * (2 * nt) + (_SEM, _SEM) + (pl.BlockSpec(memory_space=pl.ANY),) * len(after),
        out_specs=(_HBM,) * (2 * nt),
        input_output_aliases={i: i for i in range(2 * nt)},
        compiler_params=pltpu.CompilerParams(has_side_effects=_EFFECT),
    )(*srcs, *lands, send_sems, recv_sems, *after)
    return list(out[:nt]), list(out[nt:])


def _adamw_sharded(name, own, parts, w, m, v, rb, deps=()):
    R, N = own.shape

    def body(o_ref, p_ref, w_ref, m_ref, v_ref, *rest):
        g_ref, d_ref, nm_ref, nv_ref = rest[len(deps):]
        me = 4 * lax.axis_index("x") + 2 * lax.axis_index("y") + lax.axis_index("c")
        g = o_ref[...]
        for k in range(1, N_DEV):
            g = g + p_ref[me ^ k].astype(F32)
        nm = ADAM_B1 * m_ref[...] + (1.0 - ADAM_B1) * g
        nv = ADAM_B2 * v_ref[...] + (1.0 - ADAM_B2) * (g * g)
        m_hat = nm / (1.0 - ADAM_B1 ** ADAM_STEP)
        v_hat = nv / (1.0 - ADAM_B2 ** ADAM_STEP)
        g_ref[...] = g
        d_ref[...] = -ADAM_LR * (m_hat / (jnp.sqrt(v_hat) + ADAM_EPS) + ADAM_WD * w_ref[...])
        nm_ref[...] = nm
        nv_ref[...] = nv

    blk = pl.BlockSpec((rb, N), lambda i: (i, 0))
    sh = jax.ShapeDtypeStruct((R, N), F32)
    return _pcall(body, name=name, grid=(R // rb,),
                  in_specs=[blk, pl.BlockSpec((N_DEV, rb, N), lambda i: (0, i, 0)), blk, blk, blk]
                  + [pl.BlockSpec(d.shape, lambda i, nd=d.ndim: (0,) * nd) for d in deps],
                  out_specs=[blk] * 4, out_shape=[sh] * 4, compiler_params=_cparams(1))(own, parts, w, m, v, *deps)


LOSS_SLOT = "loss_partials"
SMALL_CLASSES = (
    (("s5_b_re", 32, 1024), ("s5_b_im", 32, 1024),
     ("norm_mix", 1, 1024), ("norm_ffn", 1, 1024), ("norm_ple", 1, 1024), ("final_norm", 1, 1024)),
    (("s5_d", 1, 512), ("s5_glu_b", 1, 512), ("rw_w0", 1, 512), ("rw_a0", 1, 512), ("rw_k_k", 1, 512), ("rw_k_a", 1, 512),
     ("rw_ln_w", 1, 512), ("rw_ln_b", 1, 512), ("rw_r_k", 1, 512)),
    (("rw_shift_mu", 1, 1792),),
    (("s5_lam_re", 32, 64), ("s5_lam_im", 32, 64), ("s5_c_re", 512, 64), ("s5_c_im", 512, 64)),
    (("s5_log_step", 1, 32), (LOSS_SLOT, 1, 32)),
)


def _class_rows(cls):
    return -(-sum(r for _, r, _ in cls) // 8) * 8


def _stack_class(cls, arrs):
    a = jnp.concatenate(arrs, axis=0) if len(arrs) > 1 else arrs[0]
    pad = _class_rows(cls) - a.shape[0]
    return jnp.pad(a, ((0, pad), (0, 0))) if pad else a


def _adamw_small(grads, w, m, v):
    names = [n for cls in SMALL_CLASSES for n, _, _ in cls]
    n_cls, n_par = len(SMALL_CLASSES), len(names)

    def body(*refs):
        g_refs = refs[:n_cls]
        w_refs, m_refs, v_refs = (refs[n_cls + i * n_par:n_cls + (i + 1) * n_par] for i in range(3))
        o_refs = refs[n_cls + 3 * n_par:]
        p = 0
        for cls, g_ref in zip(SMALL_CLASSES, g_refs):
            rc = _class_rows(cls)
            tot = g_ref[0:rc, :]
            for s_ in range(1, N_DEV):
                tot = tot + g_ref[s_ * rc:(s_ + 1) * rc, :]
            off = 0
            for _, r, _ in cls:
                g = tot[off:off + r, :]
                off += r
                nm = ADAM_B1 * m_refs[p][...] + (1.0 - ADAM_B1) * g
                nv = ADAM_B2 * v_refs[p][...] + (1.0 - ADAM_B2) * (g * g)
                m_hat = nm / (1.0 - ADAM_B1 ** ADAM_STEP)
                v_hat = nv / (1.0 - ADAM_B2 ** ADAM_STEP)
                o_refs[4 * p][...] = g
                o_refs[4 * p + 1][...] = -ADAM_LR * (m_hat / (jnp.sqrt(v_hat) + ADAM_EPS) + ADAM_WD * w_refs[p][...])
                o_refs[4 * p + 2][...] = nm
                o_refs[4 * p + 3][...] = nv
                p += 1

    shapes = [(r, c) for cls in SMALL_CLASSES for _, r, c in cls]
    out = _pcall(body, name="adamw_replicated",
                 out_shape=[jax.ShapeDtypeStruct(sh, F32) for sh in shapes for _ in range(4)],
                 compiler_params=pltpu.CompilerParams(vmem_limit_bytes=VMEM_LIMIT))(*grads, *w, *m, *v)
    return {n: out[4 * i:4 * i + 4] for i, n in enumerate(names)}


EARLY = (("w_in", True),)
LATE = (("ffn_w1", True), ("ffn_w3", True), ("ffn_w2", False), ("ple_gate_w", False), ("w_out", False))
GRAD_STAGES = (LATE[:4], LATE[4:])
MISC = (("s5_glu_w", False), ("rw_w2", True), ("rw_a2", True), ("rw_g2", True), ("ple_up_w", True))
SHARDED_NAMES = tuple(n for n, _ in EARLY + LATE + MISC)
PACK_COLS = 1024
WEIGHT_NAMES = ("norm_mix", "w_in", "s5_lam_re", "s5_lam_im", "s5_log_step", "s5_b_re", "s5_b_im", "s5_c_re", "s5_c_im", "s5_d",
                "s5_glu_w", "s5_glu_b", "rw_shift_mu", "rw_w0", "rw_w2", "rw_a0", "rw_a2", "rw_g2", "rw_k_k", "rw_k_a", "rw_r_k",
                "rw_ln_w", "rw_ln_b", "w_out", "norm_ffn", "ffn_w1", "ffn_w3", "ffn_w2", "norm_ple", "ple_gate_w", "ple_up_w",
                "final_norm")
SMALL_NAMES = tuple(n for n in WEIGHT_NAMES if n not in SHARDED_NAMES)
ARG_NAMES = ("x", "p") + WEIGHT_NAMES + ("loss_target",) + tuple("m_" + n for n in WEIGHT_NAMES) + tuple("v_" + n for n in WEIGHT_NAMES)


def _travel(a, tr):
    return a.T if tr else a


def _pack_misc(blocks):
    lead = blocks[0].shape[:-2]
    return jnp.concatenate([b.reshape(lead + (-1, PACK_COLS)) for b in blocks], axis=len(lead))


def _unpack_misc(packed, shapes):
    lead = packed.shape[:-2]
    out, off = [], 0
    for r, c in shapes:
        n = r * c // PACK_COLS
        out.append(lax.slice_in_dim(packed, off, off + n, axis=len(lead)).reshape(lead + (r, c)))
        off += n
    return out


def _kernel_impl(ins):
    x, p, target = ins["x"][0], ins["p"][0, 0], ins["loss_target"][0]
    me = 4 * lax.axis_index("x") + 2 * lax.axis_index("y") + lax.axis_index("c")
    small = {n: (ins[n] if n == "final_norm" else ins[n][0]) for n in SMALL_NAMES}
    trav = lambda pre, n, tr: _travel(ins[pre + n][0], tr)
    misc_shapes = [trav("", n, tr).shape for n, tr in MISC]

    early = _all_gather("ag_early", [trav("", n, tr).astype(BF16) for n, tr in EARLY]
                        + [_pack_misc([trav("", n, tr).astype(BF16) for n, tr in MISC])])
    late_handle, late_token = _direct_start("ag_late_start", [trav("", n, tr).astype(BF16) for n, tr in LATE], True, early[-1])
    W = dict(small)
    for (n, tr), g in zip(EARLY, early):
        W[n] = _travel(g, tr)
    for (n, tr), g in zip(MISC, _unpack_misc(early[-1].reshape(N_DEV, -1, PACK_COLS), misc_shapes)):
        W[n] = _travel(g.reshape(-1, g.shape[-1]), tr)

    def late_weights(after):
        shards, lands = _direct_wait("ag_late_wait", late_handle, after)
        full = [lax.dynamic_update_slice_in_dim(ld, sh, me * sh.shape[0], axis=0) for ld, sh in zip(lands, shards)]
        return {n: _travel(g, tr) for (n, tr), g in zip(LATE, full)}

    gt = lambda G, n, tr: _travel(G[n], tr)
    started = {}

    def grads_ready(stage, G):
        full = [gt(G, n, tr) for n, tr in GRAD_STAGES[stage]]
        started[stage] = (full, *_direct_start("grad_late_start%d" % stage, [a.astype(BF16) for a in full], False))
        return started[stage][2]

    loss_part, dx, G = _local_step(x, p, target, W, late_weights, grads_ready, late_token)

    misc_g = _pack_misc([gt(G, n, tr).reshape((N_DEV,) + shp) for (n, tr), shp in zip(MISC, misc_shapes)])
    early_full = [gt(G, n, tr) for n, tr in EARLY] + [misc_g.reshape(-1, PACK_COLS)]
    early_handle, early_token = _direct_start("grad_early_start", [a.astype(BF16) for a in early_full], False)
    view2 = lambda a, r, c: a.reshape(r, c)
    G[LOSS_SLOT] = jnp.full((1, 32), loss_part, F32)
    small_own = [_stack_class(cls, [view2(G[n], r, c) for n, r, c in cls]) for cls in SMALL_CLASSES]
    small_handle, small_token = _direct_start("grad_small_start", small_own, True)
    late_src, late_land = [], []
    for stage in range(len(GRAD_STAGES)):
        full, handle, _ = started[stage]
        _, land = _direct_wait("grad_late_wait%d" % stage, handle, small_token)
        late_src += full
        late_land += land

    outs = {}

    def emit(names_shapes, res):
        for tag, val in zip(("grad_", "delta_", "new_m_", "new_v_"), res):
            for n, v in names_shapes(val):
                outs[tag + n] = v

    def sharded_update(n, tr, src, land, deps=()):
        rows = src.shape[0] // N_DEV
        own = lax.dynamic_slice_in_dim(src, me * rows, rows, axis=0)
        res = _adamw_sharded("adamw_" + n, own, land.reshape(N_DEV, rows, land.shape[1]),
                             trav("", n, tr), trav("m_", n, tr), trav("v_", n, tr), _pick_rows(rows), deps)
        emit(lambda val: [(n, _travel(val, tr).reshape(ins[n].shape))], res)
        return list(res)

    for (n, tr), src, land in zip(LATE, late_src, late_land):
        sharded_update(n, tr, src, land, (early_token,))
    _, early_land = _direct_wait("grad_early_wait", early_handle, list(outs.values()))
    for (n, tr), src, land in zip(EARLY, early_full[:-1], early_land[:-1]):
        sharded_update(n, tr, src, land)
    pm = lambda pre: _pack_misc([trav(pre, n, tr) for n, tr in MISC])
    rows = early_full[-1].shape[0] // N_DEV
    res = _adamw_sharded("adamw_misc", lax.dynamic_slice_in_dim(early_full[-1], me * rows, rows, axis=0),
                         early_land[-1].reshape(N_DEV, rows, PACK_COLS), pm(""), pm("m_"), pm("v_"), rows)
    emit(lambda val: [(n, _travel(b, tr).reshape(ins[n].shape)) for (n, tr), b in zip(MISC, _unpack_misc(val, misc_shapes))], res)
    small_src, small_land = _direct_wait("grad_small_wait", small_handle, res[0])
    small_all = [lax.dynamic_update_slice_in_dim(ld, sr, me * sr.shape[0], axis=0) for ld, sr in zip(small_land, small_src)]
    flat_small = [(n, r, c) for cls in SMALL_CLASSES for n, r, c in cls]
    ins = dict(ins, **{pre + LOSS_SLOT: jnp.zeros((1, 32), F32) for pre in ("", "m_", "v_")})
    res = _adamw_small(small_all, *[[view2(ins[pre + n], r, c) for n, r, c in flat_small] for pre in ("", "m_", "v_")])
    loss = res.pop(LOSS_SLOT)[0][0, 0]
    for n, _, _ in flat_small[:-1]:
        for tag, val in zip(("grad_", "delta_", "new_m_", "new_v_"), res[n]):
            outs[tag + n] = val.reshape(ins[n].shape)
    res = [loss, dx[None]]
    for tag in ("grad_", "delta_", "new_m_", "new_v_"):
        res += [outs[tag + n] for n in WEIGHT_NAMES]
    return tuple(res)


def _pick_rows(r):
    best = 8
    for b in range(8, 257, 8):
        if r % b == 0:
            best = b
    return best


def kernel(x, p, norm_mix, w_in, s5_lam_re, s5_lam_im, s5_log_step, s5_b_re, s5_b_im, s5_c_re, s5_c_im, s5_d, s5_glu_w, s5_glu_b, rw_shift_mu, rw_w0, rw_w2, rw_a0, rw_a2, rw_g2, rw_k_k, rw_k_a, rw_r_k, rw_ln_w, rw_ln_b, w_out, norm_ffn, ffn_w1, ffn_w3, ffn_w2, norm_ple, ple_gate_w, ple_up_w, final_norm, loss_target, m_norm_mix, m_w_in, m_s5_lam_re, m_s5_lam_im, m_s5_log_step, m_s5_b_re, m_s5_b_im, m_s5_c_re, m_s5_c_im, m_s5_d, m_s5_glu_w, m_s5_glu_b, m_rw_shift_mu, m_rw_w0, m_rw_w2, m_rw_a0, m_rw_a2, m_rw_g2, m_rw_k_k, m_rw_k_a, m_rw_r_k, m_rw_ln_w, m_rw_ln_b, m_w_out, m_norm_ffn, m_ffn_w1, m_ffn_w3, m_ffn_w2, m_norm_ple, m_ple_gate_w, m_ple_up_w, m_final_norm, v_norm_mix, v_w_in, v_s5_lam_re, v_s5_lam_im, v_s5_log_step, v_s5_b_re, v_s5_b_im, v_s5_c_re, v_s5_c_im, v_s5_d, v_s5_glu_w, v_s5_glu_b, v_rw_shift_mu, v_rw_w0, v_rw_w2, v_rw_a0, v_rw_a2, v_rw_g2, v_rw_k_k, v_rw_k_a, v_rw_r_k, v_rw_ln_w, v_rw_ln_b, v_w_out, v_norm_ffn, v_ffn_w1, v_ffn_w3, v_ffn_w2, v_norm_ple, v_ple_gate_w, v_ple_up_w, v_final_norm):
    return _kernel_impl(dict(zip(ARG_NAMES, (x, p, norm_mix, w_in, s5_lam_re, s5_lam_im, s5_log_step, s5_b_re, s5_b_im, s5_c_re, s5_c_im, s5_d, s5_glu_w, s5_glu_b, rw_shift_mu, rw_w0, rw_w2, rw_a0, rw_a2, rw_g2, rw_k_k, rw_k_a, rw_r_k, rw_ln_w, rw_ln_b, w_out, norm_ffn, ffn_w1, ffn_w3, ffn_w2, norm_ple, ple_gate_w, ple_up_w, final_norm, loss_target, m_norm_mix, m_w_in, m_s5_lam_re, m_s5_lam_im, m_s5_log_step, m_s5_b_re, m_s5_b_im, m_s5_c_re, m_s5_c_im, m_s5_d, m_s5_glu_w, m_s5_glu_b, m_rw_shift_mu, m_rw_w0, m_rw_w2, m_rw_a0, m_rw_a2, m_rw_g2, m_rw_k_k, m_rw_k_a, m_rw_r_k, m_rw_ln_w, m_rw_ln_b, m_w_out, m_norm_ffn, m_ffn_w1, m_ffn_w3, m_ffn_w2, m_norm_ple, m_ple_gate_w, m_ple_up_w, m_final_norm, v_norm_mix, v_w_in, v_s5_lam_re, v_s5_lam_im, v_s5_log_step, v_s5_b_re, v_s5_b_im, v_s5_c_re, v_s5_c_im, v_s5_d, v_s5_glu_w, v_s5_glu_b, v_rw_shift_mu, v_rw_w0, v_rw_w2, v_rw_a0, v_rw_a2, v_rw_g2, v_rw_k_k, v_rw_k_a, v_rw_r_k, v_rw_ln_w, v_rw_ln_b, v_w_out, v_norm_ffn, v_ffn_w1, v_ffn_w3, v_ffn_w2, v_norm_ple, v_ple_gate_w, v_ple_up_w, v_final_norm))))
```

```python
import jax
import jax.numpy as jnp
from jax import lax
from jax.experimental import pallas as pl
from jax.experimental.pallas import tpu as pltpu

F32 = jnp.float32
BF16 = jnp.bfloat16

D_MODEL = 1024
S5_WIDTH = 512
RW_WIDTH = 512
S5_GROUP = 16
S5_GROUPS = 32
S5_STATE = 64
S5_LANES = S5_GROUPS * S5_STATE
HEAD = 64
SHIFT_COLS = 1792
IN_COLS = 2304
FFN_HIDDEN = 2816
PLE_DIM = 256
RMS_EPS = 1e-6
GN_EPS = 64e-5
L2_EPS = 1e-12
CHUNK = 64
N_DEV = 8

ADAM_LR = 0.001
ADAM_B1 = 0.9
ADAM_B2 = 0.999
ADAM_EPS = 1e-08
ADAM_WD = 0.01
ADAM_STEP = 10

VMEM_LIMIT = 56 * 1024 * 1024
_ANY = pl.BlockSpec(memory_space=pl.ANY)


def _pcall(body, **kw):
    return pl.pallas_call(body, **kw)


def _cparams(n_grid):
    return pltpu.CompilerParams(dimension_semantics=("arbitrary",) * n_grid, vmem_limit_bytes=VMEM_LIMIT)


def _dot(a, b):
    return jnp.dot(a, b, preferred_element_type=F32)


def _dot_nt(a, b):
    return lax.dot_general(a, b, (((1,), (1,)), ((), ())), preferred_element_type=F32)


def _dot_tn(a, b):
    return lax.dot_general(a, b, (((0,), (0,)), ((), ())), preferred_element_type=F32)


def _mmc(w, diff=True, tr=False):
    fw, bw = (_dot_nt, _dot) if tr else (_dot, _dot_nt)
    if not diff:
        return lambda x: fw(x.astype(BF16), w)

    @jax.custom_vjp
    def f(x):
        return fw(x.astype(BF16), w)

    def fwd(x):
        return fw(x.astype(BF16), w), None

    def bwd(_, dy):
        return (bw(dy.astype(BF16), w),)

    f.defvjp(fwd, bwd)
    return f


def _split_dot(x, m, n_split):
    acc = None
    rem = x
    for s in range(n_split):
        part = rem.astype(BF16)
        t = _dot(part, m)
        acc = t if acc is None else acc + t
        if s + 1 < n_split:
            rem = rem - part.astype(F32)
    return acc


def _segsum(m, diff=True):
    if not diff:
        return lambda x: _split_dot(x, m, 2)

    @jax.custom_vjp
    def f(x):
        return _split_dot(x, m, 2)

    def fwd(x):
        return _split_dot(x, m, 2), None

    def bwd(_, dy):
        return (_split_dot(dy, m, 2),)

    f.defvjp(fwd, bwd)
    return f


def _head_indicator(n):
    r = lax.broadcasted_iota(jnp.int32, (n, n), 0) // HEAD
    c = lax.broadcasted_iota(jnp.int32, (n, n), 1) // HEAD
    return (r == c).astype(BF16)


def _rms(x, g):
    return x * lax.rsqrt(jnp.mean(x * x, axis=-1, keepdims=True) + RMS_EPS) * g


def _softplus(x):
    return jnp.maximum(x, 0.0) + jnp.log(1.0 + jnp.exp(-jnp.abs(x)))


def _sigmoid(x):
    return 1.0 / (1.0 + jnp.exp(-x))


def _gelu(x):
    return 0.5 * x * (1.0 + jnp.tanh(0.7978845608028654 * (x + 0.044715 * (x * x * x))))


def _tok_call(name, fn, L, TB, tok_in, const_in, tok_out, acc_out=(), deps=()):
    nb = L // TB
    g8 = TB // 8
    in_specs, args = [], []
    for spec in tok_in:
        if len(spec) == 1:
            arr = spec[0]
            in_specs.append(pl.BlockSpec((arr.shape[0], TB, HEAD), lambda i: (0, i, 0)))
            args.append(arr)
            continue
        arr, width, cb = spec[:3]
        mode = spec[3] if len(spec) > 3 else None
        if mode is None:
            in_specs.append(pl.BlockSpec((TB, width), lambda i, cb=cb: (i, cb)))
        elif mode == "prev":
            in_specs.append(pl.BlockSpec((8, width), lambda i, cb=cb: (jnp.maximum(i * g8 - 1, 0), cb)))
        else:
            in_specs.append(pl.BlockSpec((8, width), lambda i, cb=cb: (jnp.minimum((i + 1) * g8, L // 8 - 1), cb)))
        args.append(arr)
    for c in const_in:
        in_specs.append(pl.BlockSpec(c.shape, lambda i, nd=c.ndim: (0,) * nd, pipeline_mode=pl.Buffered(1)))
        args.append(c)
    for d in deps:
        in_specs.append(pl.BlockSpec(d.shape, lambda i, nd=d.ndim: (0,) * nd))
        args.append(d)
    out_shape, out_specs = [], []
    for width, dt in tok_out:
        if width == "heads":
            out_shape.append(jax.ShapeDtypeStruct((N_HEAD, L, HEAD), dt))
            out_specs.append(pl.BlockSpec((N_HEAD, TB, HEAD), lambda i: (0, i, 0)))
            continue
        out_shape.append(jax.ShapeDtypeStruct((L, width), dt))
        out_specs.append(pl.BlockSpec((TB, width), lambda i: (i, 0)))
    for shp in acc_out:
        out_shape.append(jax.ShapeDtypeStruct(shp, F32))
        out_specs.append(pl.BlockSpec(shp, lambda i, nd=len(shp): (0,) * nd))
    n_tok, n_const, n_to = len(tok_in), len(const_in), len(tok_out)

    def body(*refs):
        i = pl.program_id(0)
        tv = [r[...] if len(r.shape) == 2 else jnp.concatenate([r[h] for h in range(r.shape[0])], axis=1)
              for r in refs[:n_tok]]
        cv = [r[...] for r in refs[n_tok:n_tok + n_const]]
        orefs = refs[n_tok + n_const + len(deps):]
        outs = fn(i, tv, cv)
        for r, v in zip(orefs[:n_to], outs[:n_to]):
            if len(r.shape) == 3:
                for h in range(r.shape[0]):
                    r[h] = v[:, h * HEAD:(h + 1) * HEAD].astype(r.dtype)
            else:
                r[...] = v.astype(r.dtype)
        for r, v in zip(orefs[n_to:], outs[n_to:]):
            @pl.when(i == 0)
            def _(r=r):
                r[...] = jnp.zeros(r.shape, r.dtype)

            r[...] += v

    res = _pcall(body, name=name, grid=(nb,), in_specs=in_specs, out_specs=out_specs, out_shape=out_shape,
                 compiler_params=_cparams(1))(*args)
    return res


def _pick_block(n, cap):
    best = None
    for b in range(128, min(n, cap) + 1, 128):
        if n % b == 0:
            best = b
    return best if best is not None else n


def _mm_tn(name, a, b):
    T, M = a.shape
    N = b.shape[1]
    bm, bn, bt = _pick_block(M, 1536), _pick_block(N, 1536), _pick_block(T, 1024)

    def body(a_ref, b_ref, o_ref):
        t = pl.program_id(2)

        @pl.when(t == 0)
        def _():
            o_ref[...] = jnp.zeros(o_ref.shape, F32)

        o_ref[...] += _dot_tn(a_ref[...].astype(BF16), b_ref[...].astype(BF16))

    return _pcall(body, name=name, grid=(M // bm, N // bn, T // bt),
                  in_specs=[pl.BlockSpec((bt, bm), lambda m, n, t: (t, m)), pl.BlockSpec((bt, bn), lambda m, n, t: (t, n))],
                  out_specs=pl.BlockSpec((bm, bn), lambda m, n, t: (m, n)),
                  out_shape=jax.ShapeDtypeStruct((M, N), F32), compiler_params=_cparams(3))(a, b)


def _s5_param_fn(lam_re, lam_im, log_step, bt_re, bt_im):
    dt = jnp.exp(log_step)
    e = jnp.exp(lam_re * dt)
    lb_re = e * jnp.cos(lam_im * dt)
    lb_im = e * jnp.sin(lam_im * dt)
    den = lam_re * lam_re + lam_im * lam_im
    nr, ni = lb_re - 1.0, lb_im
    co_re = (nr * lam_re + ni * lam_im) / den
    co_im = (ni * lam_re - nr * lam_im) / den
    cr, ci = co_re[:, None, :], co_im[:, None, :]
    return lb_re, lb_im, cr * bt_re - ci * bt_im, cr * bt_im + ci * bt_re


def _s5_param_fwd(lam_re, lam_im, log_step, bt_re, bt_im):
    def body(a, b, c, d, e, o1, o2, o3, o4):
        r = _s5_param_fn(a[...], b[...], c[...], d[...], e[...])
        o1[...], o2[...], o3[...], o4[...] = r

    sh = jax.ShapeDtypeStruct
    return _pcall(body, name="s5_param_fwd",
                  out_shape=[sh(lam_re.shape, F32), sh(lam_re.shape, F32), sh(bt_re.shape, F32), sh(bt_re.shape, F32)])(
        lam_re, lam_im, log_step, bt_re, bt_im)


def _s5_param_bwd(lam_re, lam_im, log_step, bt_re, bt_im, d_lb_re, d_lb_im, d_bb_re, d_bb_im):
    def body(a, b, c, d, e, g1, g2, g3, g4, o1, o2, o3, o4, o5):
        _, vjp = jax.vjp(_s5_param_fn, a[...], b[...], c[...], d[...], e[...])
        r = vjp((g1[...], g2[...], g3[...], g4[...]))
        o1[...], o2[...], o3[...], o4[...], o5[...] = r

    sh = jax.ShapeDtypeStruct
    return _pcall(body, name="s5_param_bwd",
                  out_shape=[sh(lam_re.shape, F32), sh(lam_re.shape, F32), sh(log_step.shape, F32),
                             sh(bt_re.shape, F32), sh(bt_re.shape, F32)])(
        lam_re, lam_im, log_step, bt_re, bt_im, d_lb_re, d_lb_im, d_bb_re, d_bb_im)


def _cmul(ar, ai, br, bi):
    return ar * br - ai * bi, ar * bi + ai * br


def _scan_consts(lr, li, reverse):
    n = lr.shape[1]
    sub = lax.broadcasted_iota(jnp.int32, (8, n), 0)
    pows = [(lr, li)]
    for _ in range(7):
        pows.append(_cmul(pows[-1][0], pows[-1][1], lr, li))
    steps = []
    for s in (1, 2, 4):
        m = (sub < 8 - s) if reverse else (sub >= s)
        pr, pi = pows[s - 1]
        steps.append((s, jnp.where(m, jnp.broadcast_to(pr, (8, n)), 0.0), jnp.where(m, jnp.broadcast_to(pi, (8, n)), 0.0)))
    wr = jnp.zeros((8, n), F32)
    wi = jnp.zeros((8, n), F32)
    for r in range(8):
        e = (8 - r) if reverse else (r + 1)
        wr = jnp.where(sub == r, jnp.broadcast_to(pows[e - 1][0], (8, n)), wr)
        wi = jnp.where(sub == r, jnp.broadcast_to(pows[e - 1][1], (8, n)), wi)
    return steps, wr, wi


S5_Q = 4
S5_QL = S5_WIDTH // S5_Q
S5_QS = S5_LANES // S5_Q
S5_NT = S5_LANES // 128
S5_QT = S5_QS // 128


def _s5_power_table(lb_ref, pw_re, pw_im, seg):
    for j in range(S5_NT):
        lr = jnp.broadcast_to(lb_ref[0:1, j * 128:(j + 1) * 128], (8, 128))
        li = jnp.broadcast_to(lb_ref[1:2, j * 128:(j + 1) * 128], (8, 128))

        def step(i, c, lr=lr, li=li, j=j):
            pw_re[j, i] = c[0]
            pw_im[j, i] = c[1]
            return _cmul(c[0], c[1], lr, li)

        lax.fori_loop(0, seg, step, (lr, li))


def _seg_scan(sre, sim, carry, lb_ref, pw_re, pw_im, rows, reverse):
    seg = rows // 8
    sgn = -1.0 if reverse else 1.0
    sub = lax.broadcasted_iota(jnp.int32, (8, 128), 0)
    rows_at = lambda i: pl.ds(pl.multiple_of(i * 8, 8), 8)
    entering = {}
    half_tiles = S5_NT // 2
    for half in range(2):
        tiles = list(range(half * half_tiles, (half + 1) * half_tiles))
        lam8 = [(jnp.broadcast_to(lb_ref[0:1, j * 128:(j + 1) * 128], (8, 128)),
                 sgn * jnp.broadcast_to(lb_ref[1:2, j * 128:(j + 1) * 128], (8, 128))) for j in tiles]

        def p1(ii, c):
            i = (seg - 1 - ii) if reverse else ii
            out = []
            for n, j in enumerate(tiles):
                lr, li = lam8[n]
                cr, ci = c[2 * n], c[2 * n + 1]
                nr = lr * cr - li * ci + sre[j, rows_at(i), :]
                ni = lr * ci + li * cr + sim[j, rows_at(i), :]
                sre[j, rows_at(i), :] = nr
                sim[j, rows_at(i), :] = ni
                out += [nr, ni]
            return tuple(out)

        ends = lax.fori_loop(0, seg, p1, tuple(jnp.zeros((8, 128), F32) for _ in range(2 * len(tiles))))
        cs = []
        for n, j in enumerate(tiles):
            ls = slice(j * 128, (j + 1) * 128)
            steps, wr, wi = _scan_consts(pw_re[j, seg - 1][0:1, :], sgn * pw_im[j, seg - 1][0:1, :], reverse)
            tr, ti = ends[2 * n], ends[2 * n + 1]
            for sft, pr, pi in steps:
                sh = (8 - sft) if reverse else sft
                yr, yi = pltpu.roll(tr, sh, 0), pltpu.roll(ti, sh, 0)
                tr, ti = tr + pr * yr - pi * yi, ti + pr * yi + pi * yr
            cin_r, cin_i = carry[0:1, ls], carry[1:2, ls]
            tr, ti = tr + wr * cin_r - wi * cin_i, ti + wr * cin_i + wi * cin_r
            edge_out, edge_in, sh = (0, 7, 7) if reverse else (7, 0, 1)
            carry[0:1, ls] = tr[edge_out:edge_out + 1, :]
            carry[1:2, ls] = ti[edge_out:edge_out + 1, :]
            cr = jnp.where(sub == edge_in, jnp.broadcast_to(cin_r, (8, 128)), pltpu.roll(tr, sh, 0))
            ci = jnp.where(sub == edge_in, jnp.broadcast_to(cin_i, (8, 128)), pltpu.roll(ti, sh, 0))
            cs += [cr, ci]
            entering[j] = (cr, ci)

        def p2(i, _, lo=0, hi=len(tiles)):
            k = (seg - 1 - i) if reverse else i
            for n, j in list(enumerate(tiles))[lo:hi]:
                pr, pi = pw_re[j, k], pw_im[j, k]
                cr, ci = cs[2 * n], cs[2 * n + 1]
                if reverse:
                    sre[j, rows_at(i), :] = sre[j, rows_at(i), :] + pr * cr + pi * ci
                    sim[j, rows_at(i), :] = sim[j, rows_at(i), :] + pr * ci - pi * cr
                else:
                    sre[j, rows_at(i), :] = sre[j, rows_at(i), :] + pr * cr - pi * ci
                    sim[j, rows_at(i), :] = sim[j, rows_at(i), :] + pr * ci + pi * cr
            return 0

        for lo in range(0, len(tiles), 4):
            lax.fori_loop(0, seg, lambda i, c, lo=lo: p2(i, c, lo, lo + 4), 0, unroll=2)
    return entering


class _SegIO:
    def __init__(self, hbm, buf, sems, rows, width, col0=0):
        self.hbm, self.buf, self.sems, self.rows, self.seg, self.width, self.col0 = hbm, buf, sems, rows, rows // 8, width, col0

    def _copies(self, blk, slot, to_vmem):
        out = []
        for r in range(8):
            h = self.hbm.at[pl.ds(blk * self.rows + r * self.seg, self.seg), pl.ds(self.col0, self.width)]
            v = self.buf.at[slot, :, r, :]
            out.append(pltpu.make_async_copy(h, v, self.sems.at[slot, r]) if to_vmem
                       else pltpu.make_async_copy(v, h, self.sems.at[slot, r]))
        return out

    def start(self, blk, slot, to_vmem):
        for cp in self._copies(blk, slot, to_vmem):
            cp.start()

    def wait(self, blk, slot, to_vmem):
        for cp in self._copies(blk, slot, to_vmem):
            cp.wait()

    def value(self, slot):
        return self.buf[slot].reshape(self.rows, self.width)

    def store(self, slot, val):
        self.buf[slot] = val.reshape(self.seg, 8, self.width)


def _seg_pipeline(i, nb, blk_of, ins, outs, compute):
    slot = i % 2

    @pl.when(i == 0)
    def _():
        for io in ins:
            io.start(blk_of(0), 0, True)

    @pl.when(i + 1 < nb)
    def _():
        for io in ins:
            io.start(blk_of(i + 1), 1 - slot, True)

    for io in ins:
        io.wait(blk_of(i), slot, True)

    @pl.when(i >= 2)
    def _():
        for io in outs:
            io.wait(blk_of(i - 2), slot, False)

    compute(slot)
    for io in outs:
        io.start(blk_of(i), slot, False)

    @pl.when(i == nb - 1)
    def _():
        for io in outs:
            if nb >= 2:
                io.wait(blk_of(i - 1), 1 - slot, False)
            io.wait(blk_of(i), slot, False)


def _s5_scan_fwd(proj, bq_re, bq_im, cq_re, cq_im, lbar, dskip, L, TB):
    nb = L // TB
    seg = TB // 8

    def body(u_hbm, bre, bim, cre, cim, lb_ref, d_ref, y_hbm, ck_ref, sre, sim, carry, pw_re, pw_im,
             ubuf, ybuf, sem_u, sem_y):
        i = pl.program_id(0)
        u_io = _SegIO(u_hbm, ubuf, sem_u, TB, S5_WIDTH)
        y_io = _SegIO(y_hbm, ybuf, sem_y, TB, S5_WIDTH)

        @pl.when(i == 0)
        def _():
            carry[...] = jnp.zeros(carry.shape, F32)
            _s5_power_table(lb_ref, pw_re, pw_im, seg)

        ck_ref[0] = carry[...]

        def compute(slot):
            u = u_io.value(slot)
            ub = u.astype(BF16)
            for q in range(S5_Q):
                uq = ub[:, q * S5_QL:(q + 1) * S5_QL]
                vr, vi = _dot(uq, bre[q]), _dot(uq, bim[q])
                for jj in range(S5_QT):
                    sre[q * S5_QT + jj] = vr[:, jj * 128:(jj + 1) * 128]
                    sim[q * S5_QT + jj] = vi[:, jj * 128:(jj + 1) * 128]
            _seg_scan(sre, sim, carry, lb_ref, pw_re, pw_im, TB, False)
            ys = []
            for q in range(S5_Q):
                sl = slice(q * S5_QL, (q + 1) * S5_QL)
                sr = jnp.concatenate([sre[q * S5_QT + jj] for jj in range(S5_QT)], axis=1).astype(BF16)
                si = jnp.concatenate([sim[q * S5_QT + jj] for jj in range(S5_QT)], axis=1).astype(BF16)
                ys.append(_dot(sr, cre[q]) - _dot(si, cim[q]) + u[:, sl] * d_ref[:, sl])
            y_io.store(slot, jnp.concatenate(ys, axis=1))

        _seg_pipeline(i, nb, lambda st: st, [u_io], [y_io], compute)

    full = lambda a: pl.BlockSpec(a.shape, lambda i, nd=a.ndim: (0,) * nd)
    st = pltpu.VMEM((S5_NT, TB, 128), F32)
    pw = pltpu.VMEM((S5_NT, seg, 8, 128), F32)
    io = pltpu.VMEM((2, seg, 8, S5_WIDTH), F32)
    return _pcall(
        body, name="s5_scan_fwd", grid=(nb,),
        in_specs=[_ANY, full(bq_re), full(bq_im), full(cq_re), full(cq_im), full(lbar), full(dskip)],
        out_specs=[_ANY, pl.BlockSpec((1, 8, S5_LANES), lambda i: (i, 0, 0))],
        out_shape=[jax.ShapeDtypeStruct((L, S5_WIDTH), F32), jax.ShapeDtypeStruct((nb, 8, S5_LANES), F32)],
        scratch_shapes=[st, st, pltpu.VMEM((8, S5_LANES), F32), pw, pw, io, io,
                        pltpu.SemaphoreType.DMA((2, 8)), pltpu.SemaphoreType.DMA((2, 8))],
        compiler_params=_cparams(1))(proj, bq_re, bq_im, cq_re, cq_im, lbar, dskip)


def _s5_scan_bwd(proj, dy, ck, bq_re, bq_im, cq_re, cq_im, lbar, dskip, L, TB):
    nb = L // TB
    seg = TB // 8

    def body(u_hbm, dy_hbm, ck_ref, bre, bim, cre, cim, lb_ref, d_ref,
             du_hbm, dbre, dbim, dcre, dcim, dlb_ref, dd_ref, sre, sim, gre, gim, carry, gcarry, pw_re, pw_im,
             ubuf, dybuf, dubuf, sem_u, sem_dy, sem_du):
        i = pl.program_id(0)
        u_io = _SegIO(u_hbm, ubuf, sem_u, TB, S5_WIDTH)
        dy_io = _SegIO(dy_hbm, dybuf, sem_dy, TB, S5_WIDTH)
        du_io = _SegIO(du_hbm, dubuf, sem_du, TB, S5_WIDTH)

        @pl.when(i == 0)
        def _():
            gcarry[...] = jnp.zeros(gcarry.shape, F32)
            dbre[...] = jnp.zeros(dbre.shape, F32)
            dbim[...] = jnp.zeros(dbim.shape, F32)
            dcre[...] = jnp.zeros(dcre.shape, F32)
            dcim[...] = jnp.zeros(dcim.shape, F32)
            dlb_ref[...] = jnp.zeros(dlb_ref.shape, F32)
            dd_ref[...] = jnp.zeros(dd_ref.shape, F32)
            _s5_power_table(lb_ref, pw_re, pw_im, seg)

        def compute(slot):
            u = u_io.value(slot)
            dy_v = dy_io.value(slot)
            ub = u.astype(BF16)
            dyb = dy_v.astype(BF16)
            carry[...] = ck_ref[0]
            for q in range(S5_Q):
                uq = ub[:, q * S5_QL:(q + 1) * S5_QL]
                dq = dyb[:, q * S5_QL:(q + 1) * S5_QL]
                vr, vi = _dot(uq, bre[q]), _dot(uq, bim[q])
                hr, hi = _dot_nt(dq, cre[q]), -_dot_nt(dq, cim[q])
                for jj in range(S5_QT):
                    ls = slice(jj * 128, (jj + 1) * 128)
                    sre[q * S5_QT + jj] = vr[:, ls]
                    sim[q * S5_QT + jj] = vi[:, ls]
                    gre[q * S5_QT + jj] = hr[:, ls]
                    gim[q * S5_QT + jj] = hi[:, ls]
            entering = _seg_scan(sre, sim, carry, lb_ref, pw_re, pw_im, TB, False)
            _seg_scan(gre, gim, gcarry, lb_ref, pw_re, pw_im, TB, True)

            rows_at = lambda k: pl.ds(pl.multiple_of(k * 8, 8), 8)
            for half in range(2):
                tiles = list(range(half * (S5_NT // 2), (half + 1) * (S5_NT // 2)))
                acc0 = []
                for j in tiles:
                    er, ei = entering[j]
                    gr0, gi0 = gre[j, rows_at(0), :], gim[j, rows_at(0), :]
                    acc0 += [gr0 * er + gi0 * ei, gi0 * er - gr0 * ei]

                def acc_step(k, acc, tiles=tiles):
                    out = []
                    for n, j in enumerate(tiles):
                        gr, gi_ = gre[j, rows_at(k), :], gim[j, rows_at(k), :]
                        spr, spi = sre[j, rows_at(k - 1), :], sim[j, rows_at(k - 1), :]
                        out += [acc[2 * n] + gr * spr + gi_ * spi, acc[2 * n + 1] - gr * spi + gi_ * spr]
                    return tuple(out)

                acc = lax.fori_loop(1, seg, acc_step, tuple(acc0))
                for n, j in enumerate(tiles):
                    ls = slice(j * 128, (j + 1) * 128)
                    dlb_ref[0:1, ls] += jnp.sum(acc[2 * n], axis=0, keepdims=True)
                    dlb_ref[1:2, ls] += jnp.sum(acc[2 * n + 1], axis=0, keepdims=True)

            dd_ref[...] += jnp.sum(dy_v * u, axis=0, keepdims=True)
            dus = []
            for q in range(S5_Q):
                sl = slice(q * S5_QL, (q + 1) * S5_QL)
                cat = lambda ref: jnp.concatenate([ref[q * S5_QT + jj] for jj in range(S5_QT)], axis=1).astype(BF16)
                grq, giq = cat(gre), cat(gim)
                dus.append(_dot_nt(grq, bre[q]) + _dot_nt(giq, bim[q]) + dy_v[:, sl] * d_ref[:, sl])
                dbre[q] += _dot_tn(ub[:, sl], grq)
                dbim[q] += _dot_tn(ub[:, sl], giq)
                dcre[q] += _dot_tn(cat(sre), dyb[:, sl])
                dcim[q] -= _dot_tn(cat(sim), dyb[:, sl])
            du_io.store(slot, jnp.concatenate(dus, axis=1))

        _seg_pipeline(i, nb, lambda st: nb - 1 - st, [u_io, dy_io], [du_io], compute)

    full = lambda a: pl.BlockSpec(a.shape, lambda i, nd=a.ndim: (0,) * nd)
    sh = jax.ShapeDtypeStruct
    outs = [sh((L, S5_WIDTH), F32), sh(bq_re.shape, F32), sh(bq_im.shape, F32), sh(cq_re.shape, F32), sh(cq_im.shape, F32),
            sh((8, S5_LANES), F32), sh((1, S5_WIDTH), F32)]
    fo = lambda s: pl.BlockSpec(s.shape, lambda i, nd=len(s.shape): (0,) * nd)
    st = pltpu.VMEM((S5_NT, TB, 128), F32)
    pw = pltpu.VMEM((S5_NT, seg, 8, 128), F32)
    io = pltpu.VMEM((2, seg, 8, S5_WIDTH), F32)
    sem = pltpu.SemaphoreType.DMA((2, 8))
    return _pcall(
        body, name="s5_scan_bwd", grid=(nb,),
        in_specs=[_ANY, _ANY, pl.BlockSpec((1, 8, S5_LANES), lambda i: (nb - 1 - i, 0, 0)),
                  full(bq_re), full(bq_im), full(cq_re), full(cq_im), full(lbar), full(dskip)],
        out_specs=[_ANY] + [fo(s) for s in outs[1:]],
        out_shape=outs,
        scratch_shapes=[st] * 4 + [pltpu.VMEM((8, S5_LANES), F32)] * 2 + [pw, pw, io, io, io, sem, sem, sem],
        compiler_params=_cparams(1))(proj, dy, ck, bq_re, bq_im, cq_re, cq_im, lbar, dskip)


N_HEAD = RW_WIDTH // HEAD
_NN = (((2,), (1,)), ((0,), (0,)))
_NT = (((2,), (2,)), ((0,), (0,)))
_TN = (((1,), (1,)), ((0,), (0,)))


def _hi_lo(x):
    h = x.astype(BF16)
    return h, (x - h.astype(F32)).astype(BF16)


def _mm_acc(a, b, dims, passes=3):
    dg = lambda p, q: lax.dot_general(p, q, dims, preferred_element_type=F32)
    if passes == 1:
        return dg(a.astype(BF16), b.astype(BF16))
    ah, al = _hi_lo(a)
    bh, bl = _hi_lo(b)
    return dg(ah, bh) + dg(ah, bl) + dg(al, bh)


def _cumsum_rows(x, transpose):
    h, n, _ = x.shape
    ti = lax.broadcasted_iota(jnp.int32, (h, n, n), 1)
    tj = lax.broadcasted_iota(jnp.int32, (h, n, n), 2)
    m = ((tj >= ti) if transpose else (tj <= ti)).astype(BF16)
    acc, rem = None, x
    for s in range(3):
        part = rem.astype(BF16)
        t = lax.dot_general(m, part, _NN, preferred_element_type=F32)
        acc = t if acc is None else acc + t
        if s < 2:
            rem = rem - part.astype(F32)
    return acc


def _slices(x, axis, sizes):
    out, off = [], 0
    for n in sizes:
        out.append(lax.slice_in_dim(x, off, off + n, axis=axis))
        off += n
    return tuple(out)


def _cat_op(axis, sizes, diff):
    plain = lambda *xs: jnp.concatenate(xs, axis=axis)
    if not diff:
        return plain
    f = jax.custom_vjp(plain)
    f.defvjp(lambda *xs: (plain(*xs), None), lambda _, d: _slices(d, axis, sizes))
    return f


def _split_op(axis, sizes, diff):
    plain = lambda x: _slices(x, axis, sizes)
    if not diff:
        return plain
    f = jax.custom_vjp(plain)
    f.defvjp(lambda x: (plain(x), None), lambda _, d: (jnp.concatenate(d, axis=axis),))
    return f


def _mm_ops(diff, passes):
    mm = lambda a, b, dims: _mm_acc(a, b, dims, passes)
    if not diff:
        return (lambda a, b: mm(a, b, _NN), lambda a, b: mm(a, b, _NT), lambda a, b: mm(a, b, _TN))

    @jax.custom_vjp
    def nn(a, b):
        return mm(a, b, _NN)

    nn.defvjp(lambda a, b: (mm(a, b, _NN), (a, b)), lambda r, d: (mm(d, r[1], _NT), mm(r[0], d, _TN)))

    @jax.custom_vjp
    def nt(a, b):
        return mm(a, b, _NT)

    nt.defvjp(lambda a, b: (mm(a, b, _NT), (a, b)), lambda r, d: (mm(d, r[1], _NN), mm(d, r[0], _TN)))

    @jax.custom_vjp
    def tn(a, b):
        return mm(a, b, _TN)

    tn.defvjp(lambda a, b: (mm(a, b, _TN), (a, b)), lambda r, d: (mm(r[1], d, _NT), mm(r[0], d, _NN)))
    return nn, nt, tn


def _cums_op(diff):
    if not diff:
        return lambda x: _cumsum_rows(x, False)

    @jax.custom_vjp
    def cums(x):
        return _cumsum_rows(x, False)

    cums.defvjp(lambda x: (_cumsum_rows(x, False), None), lambda _, d: (_cumsum_rows(d, True),))
    return cums


WKV_PASSES = (1, 1, 1, 1, 1)


WKV_SUB = 4
WKV_BLOCK = CHUNK * WKV_SUB


def _wkv_block(s0, r, w, k, v, a, b, diff):
    p_pair, p_val, p_solve, p_out, p_state = WKV_PASSES
    cums = _cums_op(diff)
    _, nt_pair, _ = _mm_ops(diff, p_pair)
    nn_val, _, _ = _mm_ops(diff, p_val)
    nn_solve, _, _ = _mm_ops(diff, p_solve)
    nn_out, nt_out, _ = _mm_ops(diff, p_out)
    nn_state, _, tn_state = _mm_ops(diff, p_state)
    h, d, n, sub = s0.shape[0], s0.shape[2], CHUNK, WKV_SUB
    hb = h * sub
    to_chunks = lambda t: _cat_op(0, (h,) * sub, diff)(*_split_op(1, (n,) * sub, diff)(t))
    r, w, k, v, a, b = (to_chunks(t) for t in (r, w, k, v, a, b))
    cat_rows2 = _cat_op(1, (n, n), diff)
    cat_lanes2 = _cat_op(2, (n, n), diff)
    split_rows2 = _split_op(1, (n, n), diff)
    split_lanes2 = _split_op(2, (n, n), diff)
    ti = lax.broadcasted_iota(jnp.int32, (hb, n, n), 1)
    tj = lax.broadcasted_iota(jnp.int32, (hb, n, n), 2)
    incl, strict = tj <= ti, tj < ti
    logw = jnp.log(w)
    cum = cums(logw)
    g_in, g_ex, g_inv = jnp.exp(cum), jnp.exp(cum - logw), jnp.exp(-cum)
    ae, re, bi, ki = a * g_ex, r * g_in, b * g_inv, k * g_inv
    top, bot = split_rows2(nt_pair(cat_rows2(ae, re), cat_rows2(bi, ki)))
    tab, tak = split_lanes2(top)
    qb, qk = split_lanes2(bot)
    tab, tak = jnp.where(strict, tab, 0.0), jnp.where(strict, tak, 0.0)
    qb, qk = jnp.where(incl, qb, 0.0), jnp.where(incl, qk, 0.0)
    tak_v, qk_v = split_rows2(nn_val(cat_rows2(tak, qk), v))
    x = cat_lanes2(ae, tak_v)
    npow = tab
    steps = max(1, (n - 1).bit_length())
    for i in range(steps):
        x = x + nn_solve(npow, x)
        if i + 1 < steps:
            npow = nn_solve(npow, npow)
    ae_m, uc = split_lanes2(x)
    qx = nn_out(qb, x)
    q_ae, q_uc = split_lanes2(qx)
    re_m = re + q_ae
    yc = q_uc + qk_v
    g_end = jnp.exp(jnp.sum(logw, axis=1, keepdims=True))
    bg, kg = bi * g_end, ki * g_end
    tm = tn_state(ae_m, bg)
    sc = tn_state(cat_rows2(uc, v), cat_rows2(bg, kg))
    per_chunk = _split_op(0, (h,) * sub, diff)
    re_m, yc, g_end, tm, sc = (per_chunk(t) for t in (re_m, yc, g_end, tm, sc))
    ys, s = [], s0
    for i in range(sub):
        ys.append(nt_out(re_m[i], s) + yc[i])
        s = s * g_end[i] + nn_state(s, tm[i]) + sc[i]
    return _cat_op(1, (n,) * sub, diff)(*ys), s


def _wkv_fwd(r, w, k, v, a, b, L):
    nc = L // WKV_BLOCK

    def body(r_ref, w_ref, k_ref, v_ref, a_ref, b_ref, y_ref, ck_ref, s_ref):
        c = pl.program_id(0)

        @pl.when(c == 0)
        def _():
            s_ref[...] = jnp.zeros(s_ref.shape, F32)

        s0 = s_ref[...]
        ck_ref[0] = s0
        y, s1 = _wkv_block(s0, r_ref[...], w_ref[...], k_ref[...], v_ref[...], a_ref[...], b_ref[...], False)
        y_ref[...] = y
        s_ref[...] = s1

    blk = pl.BlockSpec((N_HEAD, WKV_BLOCK, HEAD), lambda c: (0, c, 0))
    return _pcall(
        body, name="wkv_fwd", grid=(nc,), in_specs=[blk] * 6,
        out_specs=[blk, pl.BlockSpec((1, N_HEAD, HEAD, HEAD), lambda c: (c, 0, 0, 0))],
        out_shape=[jax.ShapeDtypeStruct((N_HEAD, L, HEAD), F32), jax.ShapeDtypeStruct((nc, N_HEAD, HEAD, HEAD), F32)],
        scratch_shapes=[pltpu.VMEM((N_HEAD, HEAD, HEAD), F32)],
        compiler_params=_cparams(1))(r, w, k, v, a, b)


def _wkv_bwd(r, w, k, v, a, b, dy, ck, L, deps=()):
    nc = L // WKV_BLOCK

    def body(r_ref, w_ref, k_ref, v_ref, a_ref, b_ref, dy_ref, ck_ref, *rest):
        dr_ref, dw_ref, dk_ref, dv_ref, da_ref, db_ref, ds_ref = rest[len(deps):]
        c = pl.program_id(0)

        @pl.when(c == 0)
        def _():
            ds_ref[...] = jnp.zeros(ds_ref.shape, F32)

        _, vjp = jax.vjp(lambda *t: _wkv_block(*t, True), ck_ref[0], r_ref[...], w_ref[...], k_ref[...], v_ref[...],
                         a_ref[...], b_ref[...])
        g = vjp((dy_ref[...], ds_ref[...]))
        ds_ref[...] = g[0]
        for o_ref, val in zip((dr_ref, dw_ref, dk_ref, dv_ref, da_ref, db_ref), g[1:]):
            o_ref[...] = val

    blk = pl.BlockSpec((N_HEAD, WKV_BLOCK, HEAD), lambda c: (0, nc - 1 - c, 0))
    sh = jax.ShapeDtypeStruct((N_HEAD, L, HEAD), F32)
    return _pcall(
        body, name="wkv_bwd", grid=(nc,),
        in_specs=[blk] * 7 + [pl.BlockSpec((1, N_HEAD, HEAD, HEAD), lambda c: (nc - 1 - c, 0, 0, 0))]
        + [pl.BlockSpec(d.shape, lambda c, nd=d.ndim: (0,) * nd) for d in deps],
        out_specs=[blk] * 6, out_shape=[sh] * 6,
        scratch_shapes=[pltpu.VMEM((N_HEAD, HEAD, HEAD), F32)],
        compiler_params=_cparams(1))(r, w, k, v, a, b, dy, ck, *deps)


TB = 256


def _bf(x):
    return x.astype(BF16)


def _inproj_fwd(x, norm_mix, w_in, L, deps=()):
    def fn(i, tv, cv):
        xn = _rms(tv[0], cv[0])
        return _dot(_bf(xn), cv[1]), xn

    return _tok_call("inproj_fwd", fn, L, 2 * TB, [(x, D_MODEL, 0)], [norm_mix, w_in], [(IN_COLS, F32), (D_MODEL, BF16)],
                     deps=deps)


def _s5_post_fn(glu_w, wtop, diff=True):
    mg = _mmc(glu_w, diff)
    mt = _mmc(wtop, diff) if wtop is not None else None

    def f(y, glu_b, e):
        z = _gelu(y)
        out = z * _sigmoid(mg(z) + glu_b + e)
        res = mt(out) if mt is not None else out
        return res, (z, out)

    return f


def _s5_post_fwd(y, glu_w, glu_b, L):
    def fn(i, tv, cv):
        out, _ = _s5_post_fn(cv[0], None, False)(tv[0], cv[1], 0.0)
        return (out,)

    return _tok_call("s5_post_fwd", fn, L, 2 * TB, [(y, S5_WIDTH, 0)], [glu_w, glu_b], [(S5_WIDTH, F32)])[0]


def _s5_post_bwd(y, dh1, glu_w, glu_b, wtop, L, deps=()):
    def fn(i, tv, cv):
        e0 = jnp.zeros(tv[0].shape, F32)
        _, vjp, (z, out) = jax.vjp(_s5_post_fn(cv[0], cv[2]), tv[0], cv[1], e0, has_aux=True)
        dy, db, de = vjp(tv[1])
        return dy, db, _dot_tn(_bf(z), _bf(de)), _dot_tn(_bf(out), _bf(tv[1]))

    return _tok_call("s5_post_bwd", fn, L, 2 * TB, [(y, S5_WIDTH, 0), (dh1, D_MODEL, 0)], [glu_w, glu_b, wtop],
                     [(S5_WIDTH, F32)], [(1, S5_WIDTH), (S5_WIDTH, S5_WIDTH), (S5_WIDTH, D_MODEL)], deps=deps)


RW_COLBLK = ((RW_WIDTH, 1), (RW_WIDTH, 2), (RW_WIDTH, 3), (128, 16), (128, 17))
RW_MU = ((0, 512), (512, 1024), (1024, 1536), (1536, 1664), (1664, 1792))


def _rw_pre_fn(w2pad, a2pad, g2, diff=True):
    m_w, m_a, m_g = _mmc(w2pad, diff), _mmc(a2pad, diff), _mmc(g2, diff)
    seg = _segsum(_head_indicator(RW_WIDTH), diff)

    def f(zr, zk, zv, zwa, zg, w0, a0, k_k, k_a, e_w, e_a):
        wl_t = jnp.tanh(zwa)
        wlin = w0 + m_w(wl_t) + e_w
        w = -_softplus(-wlin) - 0.5
        decay = jnp.exp(-jnp.exp(w))
        a = _sigmoid(a0 + m_a(zwa) + e_a)
        sg = _sigmoid(zg)
        g = m_g(sg)
        kk = zk * k_k
        kkn = kk / jnp.maximum(jnp.sqrt(seg(kk * kk)), L2_EPS)
        kf = zk * (1.0 + (a - 1.0) * k_a)
        return (zr, decay, kf, zv, -kkn, kkn * a, g), (wl_t, sg)

    return f


def _rw_shifted(i, tv, mu):
    sub = lax.broadcasted_iota(jnp.int32, (tv[0].shape[0], 1), 0)
    zs, dif = [], []
    for n in range(5):
        z = tv[n]
        last = jnp.where(i == 0, 0.0, tv[5 + n][7:8, :])
        prev = jnp.where(sub == 0, last, pltpu.roll(z, 1, 0))
        m = mu[:, RW_MU[n][0]:RW_MU[n][1]]
        zs.append(z + (prev - z) * m)
        dif.append(prev - z)
    return zs, dif


def _rw_tok_in(proj):
    return [(proj, wd, cb) for wd, cb in RW_COLBLK] + [(proj, wd, cb, "prev") for wd, cb in RW_COLBLK]


def _rw_pre_fwd(proj, mu, w0, a0, k_k, k_a, w2pad, a2pad, g2, L):
    def fn(i, tv, cv):
        zs, _ = _rw_shifted(i, tv, cv[0])
        outs, _ = _rw_pre_fn(cv[5], cv[6], cv[7], False)(*zs, cv[1], cv[2], cv[3], cv[4], 0.0, 0.0)
        return outs

    return _tok_call("rw_pre_fwd", fn, L, 2 * TB, _rw_tok_in(proj), [mu, w0, a0, k_k, k_a, w2pad, a2pad, g2],
                     [("heads", F32)] * 6 + [(RW_WIDTH, F32)])


def _rw_pre_bwd(proj, cots, mu, w0, a0, k_k, k_a, w2pad, a2pad, g2, L):
    def fn(i, tv, cv):
        zs, dif = _rw_shifted(i, tv[:10], cv[0])
        dr1, dr2, dw, dk1, dk2, dv1, dv2, da, db, dg = tv[10:]
        e0 = jnp.zeros((TB, RW_WIDTH), F32)
        _, vjp, (wl_t, sg) = jax.vjp(_rw_pre_fn(cv[5], cv[6], cv[7]), *zs, cv[1], cv[2], cv[3], cv[4], e0, e0, has_aux=True)
        g = vjp((dr1 + dr2, dw, dk1 + dk2, dv1 + dv2, da, db, dg))
        dzs = jnp.concatenate(g[:5], axis=1)
        dmu = jnp.concatenate([jnp.sum(g[n] * dif[n], axis=0, keepdims=True) for n in range(5)], axis=1)
        lora = (_dot_tn(_bf(wl_t), _bf(g[9])), _dot_tn(_bf(zs[3]), _bf(g[10])), _dot_tn(_bf(sg), _bf(dg)))
        return (dzs, dmu, g[5], g[6], g[7], g[8]) + lora

    tok_in = _rw_tok_in(proj) + [((c,) if c.ndim == 3 else (c, RW_WIDTH, 0)) for c in cots]
    return _tok_call("rw_pre_bwd", fn, L, TB, tok_in, [mu, w0, a0, k_k, k_a, w2pad, a2pad, g2],
                     [(SHIFT_COLS, F32)], [(1, SHIFT_COLS)] + [(1, RW_WIDTH)] * 4 + [(128, RW_WIDTH)] * 3)


def _rw_post_fn(wbot, diff=True):
    seg = _segsum(_head_indicator(RW_WIDTH), diff)
    mb = _mmc(wbot, diff) if wbot is not None else None

    def f(y, r, kf, v, g, ln_w, ln_b, r_k):
        mean = seg(y) * (1.0 / HEAD)
        yc = y - mean
        var = seg(yc * yc) * (1.0 / HEAD)
        yn = yc * lax.rsqrt(var + GN_EPS) * ln_w + ln_b
        bonus = seg(r * kf * r_k) * v
        out = (yn + bonus) * g
        res = mb(out) if mb is not None else out
        return res, out

    return f


def _rw_post_fwd(y, r, kf, v, g, ln_w, ln_b, r_k, L):
    def fn(i, tv, cv):
        out, _ = _rw_post_fn(None, False)(*tv, *cv)
        return (out,)

    return _tok_call("rw_post_fwd", fn, L, 2 * TB, [(t,) for t in (y, r, kf, v)] + [(g, RW_WIDTH, 0)], [ln_w, ln_b, r_k],
                     [(RW_WIDTH, F32)])[0]


def _rw_post_bwd(y, r, kf, v, g, dh1, ln_w, ln_b, r_k, wbot, L):
    def fn(i, tv, cv):
        _, vjp, out = jax.vjp(_rw_post_fn(cv[3]), *tv[:5], cv[0], cv[1], cv[2], has_aux=True)
        gr = vjp(tv[5])
        return gr[0], gr[1], gr[2], gr[3], gr[4], gr[5], gr[6], gr[7], _dot_tn(_bf(out), _bf(tv[5]))

    return _tok_call("rw_post_bwd", fn, L, 2 * TB, [(t,) for t in (y, r, kf, v)] + [(g, RW_WIDTH, 0), (dh1, D_MODEL, 0)],
                     [ln_w, ln_b, r_k, wbot], [("heads", F32)] + [(RW_WIDTH, F32)] * 4,
                     [(1, RW_WIDTH)] * 3 + [(RW_WIDTH, D_MODEL)])


def _ffn_fn(w1, w3, w2, diff=True):
    m1, m3, m2 = _mmc(w1, diff), _mmc(w3, diff), _mmc(w2, diff)

    def f(h1, norm_ffn, e1, e3):
        hn = _rms(h1, norm_ffn)
        a1 = m1(hn) + e1
        a3 = m3(hn) + e3
        hm = a1 * _sigmoid(a1) * a3
        return h1 + m2(hm), (hn, hm)

    return f


TB_FFN = 256


def _mixffn_fwd(x, s5_out, rw_out, wtop, wbot, norm_ffn, w1, w3, w2, L):
    def fn(i, tv, cv):
        h1 = tv[0] + _dot(_bf(tv[1]), cv[0]) + _dot(_bf(tv[2]), cv[1])
        h2, _ = _ffn_fn(cv[3], cv[4], cv[5], False)(h1, cv[2], 0.0, 0.0)
        return h1, h2

    return _tok_call("mixffn_fwd", fn, L, 2 * TB_FFN, [(x, D_MODEL, 0), (s5_out, S5_WIDTH, 0), (rw_out, RW_WIDTH, 0)],
                     [wtop, wbot, norm_ffn, w1, w3, w2], [(D_MODEL, F32), (D_MODEL, F32)])


def _ffn_bwd(h1, dh2, norm_ffn, w1, w3, w2, L):
    def fn(i, tv, cv):
        e0 = jnp.zeros((TB_FFN, FFN_HIDDEN), F32)
        _, vjp, (hn, hm) = jax.vjp(_ffn_fn(cv[1], cv[2], cv[3]), tv[0], cv[0], e0, e0, has_aux=True)
        dh1, dn, d1, d3 = vjp(tv[1])
        return dh1, d1, d3, hm, hn, dn

    return _tok_call("ffn_bwd", fn, L, TB_FFN, [(h1, D_MODEL, 0), (dh2, D_MODEL, 0)], [norm_ffn, w1, w3, w2],
                     [(D_MODEL, F32), (FFN_HIDDEN, BF16), (FFN_HIDDEN, BF16), (FFN_HIDDEN, BF16), (D_MODEL, BF16)],
                     [(1, D_MODEL)])


def _ple_loss_fb(h2, p, target, norm_ple, final_norm, wg, wu, L):
    def fn(i, tv, cv):
        mgate, mup = _mmc(cv[2]), _mmc(cv[3], False)

        def f(h2_, norm_ple_, final_norm_, eg, eu):
            hn = _rms(h2_, norm_ple_)
            gate = _sigmoid(mgate(hn) + eg)
            h3 = h2_ + gate * (mup(tv[1]) + eu)
            out = _rms(h3, final_norm_)
            d = out - tv[2]
            return 0.5 * jnp.sum(jnp.mean(d * d, axis=-1, keepdims=True)), hn

        e0 = jnp.zeros(tv[0].shape, F32)
        loss, vjp, hn = jax.vjp(f, tv[0], cv[0], cv[1], e0, e0, has_aux=True)
        dh2, dnp, dfn, deg, deu = vjp(jnp.ones((), F32))
        return (dh2, dh2, jnp.full((8, 128), loss, F32), dnp, dfn,
                _dot_tn(_bf(hn), _bf(deg)), _dot_tn(_bf(tv[1]), _bf(deu)))

    return _tok_call("ple_loss_fb", fn, L, 2 * TB, [(h2, D_MODEL, 0), (p, PLE_DIM, 0), (target, D_MODEL, 0)],
                     [norm_ple, final_norm, wg, wu], [(D_MODEL, F32), (D_MODEL, BF16)],
                     [(8, 128), (1, D_MODEL), (1, D_MODEL), (D_MODEL, D_MODEL), (PLE_DIM, D_MODEL)])


def _inproj_bwd(x, dh1, du, dzs, norm_mix, mu, w_u, w_z, L):
    tb = 2 * TB
    nb = L // tb

    def fn(i, tv, cv):
        sub = lax.broadcasted_iota(jnp.int32, (tb, 1), 0)
        m = cv[1]
        b = tv[3] * m
        nxt = jnp.where(i == nb - 1, 0.0, tv[4][0:1, :] * m)
        dz = tv[3] * (1.0 - m) + jnp.where(sub == tb - 1, nxt, pltpu.roll(b, tb - 1, 0))
        dub, dzb = _bf(tv[2]), _bf(dz)
        dxn = _dot_nt(dub, cv[2]) + _dot_nt(dzb, cv[3])
        _, vjp = jax.vjp(_rms, tv[0], cv[0])
        dx, dn = vjp(dxn)
        return tv[1] + dx, jnp.concatenate([dub, dzb], axis=1), dn

    return _tok_call("inproj_bwd", fn, L, tb,
                     [(x, D_MODEL, 0), (dh1, D_MODEL, 0), (du, S5_WIDTH, 0), (dzs, SHIFT_COLS, 0), (dzs, SHIFT_COLS, 0, "next")],
                     [norm_mix, mu, w_u, w_z], [(D_MODEL, F32), (IN_COLS, BF16)], [(1, D_MODEL)])


def _eye8(dt):
    return jnp.eye(8, dtype=dt)


def _quarter_b(bb):
    return jnp.einsum("hg,qgcp->qhcgp", _eye8(bb.dtype), bb.reshape(S5_Q, 8, S5_GROUP, S5_STATE)).reshape(S5_Q, S5_QL, S5_QS)


def _unquarter_b(d):
    return jnp.einsum("qhcgp,hg->qgcp", d.reshape(S5_Q, 8, S5_GROUP, 8, S5_STATE), _eye8(d.dtype)).reshape(
        S5_GROUPS, S5_GROUP, S5_STATE)


def _quarter_c(c):
    return jnp.einsum("gh,qgcp->qgphc", _eye8(c.dtype), c.reshape(S5_Q, 8, S5_GROUP, S5_STATE)).reshape(S5_Q, S5_QS, S5_QL)


def _unquarter_c(d):
    return jnp.einsum("qgphc,gh->qgcp", d.reshape(S5_Q, 8, S5_STATE, 8, S5_GROUP), _eye8(d.dtype)).reshape(
        S5_GROUPS, S5_GROUP, S5_STATE)


def _local_step(x, p, target, W, late_weights=None, grads_ready=None, first_dep=None):
    L = x.shape[0]
    r2 = lambda v: v.reshape(1, -1)
    w_in = W["w_in"]
    w2pad = jnp.pad(W["rw_w2"], ((0, 64), (0, 0)))
    a2pad = jnp.pad(W["rw_a2"], ((64, 0), (0, 0)))
    mu = r2(W["rw_shift_mu"])
    rw_vec = [r2(W[n]) for n in ("rw_w0", "rw_a0", "rw_k_k", "rw_k_a")]
    ln_w, ln_b, r_k = r2(W["rw_ln_w"]), r2(W["rw_ln_b"]), r2(W["rw_r_k"])

    lam_re, lam_im = W["s5_lam_re"], W["s5_lam_im"]
    log_step = W["s5_log_step"].reshape(S5_GROUPS, 1)
    bt_re, bt_im = W["s5_b_re"].transpose(0, 2, 1), W["s5_b_im"].transpose(0, 2, 1)
    lb_re, lb_im, bb_re, bb_im = _s5_param_fwd(lam_re, lam_im, log_step, bt_re, bt_im)
    bq_re, bq_im = _quarter_b(bb_re).astype(BF16), _quarter_b(bb_im).astype(BF16)
    cq_re, cq_im = _quarter_c(W["s5_c_re"]).astype(BF16), _quarter_c(W["s5_c_im"]).astype(BF16)
    lbar = jnp.concatenate([lb_re.reshape(1, -1), lb_im.reshape(1, -1), jnp.zeros((6, S5_LANES), F32)], axis=0)
    dskip = r2(W["s5_d"])
    glu_b = r2(W["s5_glu_b"])
    norm_mix, norm_ffn, norm_ple, final_norm = (r2(W[n]) for n in ("norm_mix", "norm_ffn", "norm_ple", "final_norm"))

    proj, xn = _inproj_fwd(x, norm_mix, w_in, L, () if first_dep is None else (first_dep,))
    y_s5, ck5 = _s5_scan_fwd(proj, bq_re, bq_im, cq_re, cq_im, lbar, dskip, L, TB)
    s5_out = _s5_post_fwd(y_s5, W["s5_glu_w"], glu_b, L)
    r, wd, kf, v, a_s, b_s, g = _rw_pre_fwd(proj, mu, *rw_vec, w2pad, a2pad, W["rw_g2"], L)
    scan_in = (r, wd, kf, v, a_s, b_s)
    y_wkv, ckw = _wkv_fwd(*scan_in, L)
    rw_out = _rw_post_fwd(y_wkv, r, kf, v, g, ln_w, ln_b, r_k, L)
    if late_weights is not None:
        W = dict(W, **late_weights(rw_out))
    wtop, wbot = W["w_out"][:S5_WIDTH], W["w_out"][S5_WIDTH:]
    h1, h2 = _mixffn_fwd(x, s5_out, rw_out, wtop, wbot, norm_ffn, W["ffn_w1"], W["ffn_w3"], W["ffn_w2"], L)

    G = {}
    dh2, dh2_bf, loss_acc, G["norm_ple"], G["final_norm"], G["ple_gate_w"], G["ple_up_w"] = _ple_loss_fb(
        h2, p, target, norm_ple, final_norm, W["ple_gate_w"], W["ple_up_w"], L)
    dh1, da1, da3, hm, hn_ffn, G["norm_ffn"] = _ffn_bwd(h1, dh2, norm_ffn, W["ffn_w1"], W["ffn_w3"], W["ffn_w2"], L)
    G["ffn_w1"] = _mm_tn("dw_ffn_w1", hn_ffn, da1)
    G["ffn_w3"] = _mm_tn("dw_ffn_w3", hn_ffn, da3)
    G["ffn_w2"] = _mm_tn("dw_ffn_w2", hm, dh2_bf)
    dep_a = grads_ready(0, G) if grads_ready is not None else None
    dy_s5, G["s5_glu_b"], G["s5_glu_w"], d_wtop = _s5_post_bwd(y_s5, dh1, W["s5_glu_w"], glu_b, wtop, L,
                                                               () if dep_a is None else (dep_a,))
    dy_wkv, dr2, dk2, dv2, dg, G["rw_ln_w"], G["rw_ln_b"], G["rw_r_k"], d_wbot = _rw_post_bwd(
        y_wkv, r, kf, v, g, dh1, ln_w, ln_b, r_k, wbot, L)
    G["w_out"] = jnp.concatenate([d_wtop, d_wbot], axis=0)
    dep = grads_ready(1, G) if grads_ready is not None else None
    dr1, dwd, dk1, dv1, da_s, db_s = _wkv_bwd(*scan_in, dy_wkv, ckw, L, () if dep is None else (dep,))
    (dzs, G["rw_shift_mu"], G["rw_w0"], G["rw_a0"], G["rw_k_k"], G["rw_k_a"], d_w2pad, d_a2pad, G["rw_g2"]) = _rw_pre_bwd(
        proj, (dr1, dr2, dwd, dk1, dk2, dv1, dv2, da_s, db_s, dg), mu, *rw_vec, w2pad, a2pad, W["rw_g2"], L)
    G["rw_w2"], G["rw_a2"] = d_w2pad[:64], d_a2pad[64:]
    du, dbq_re, dbq_im, dcq_re, dcq_im, dlbar, G["s5_d"] = _s5_scan_bwd(
        proj, dy_s5, ck5, bq_re, bq_im, cq_re, cq_im, lbar, dskip, L, TB)
    G["s5_c_re"], G["s5_c_im"] = _unquarter_c(dcq_re), _unquarter_c(dcq_im)
    d_lam_re, d_lam_im, d_ls, d_bt_re, d_bt_im = _s5_param_bwd(
        lam_re, lam_im, log_step, bt_re, bt_im, dlbar[0].reshape(S5_GROUPS, S5_STATE), dlbar[1].reshape(S5_GROUPS, S5_STATE),
        _unquarter_b(dbq_re), _unquarter_b(dbq_im))
    G["s5_lam_re"], G["s5_lam_im"], G["s5_log_step"] = d_lam_re, d_lam_im, d_ls.reshape(S5_GROUPS)
    G["s5_b_re"], G["s5_b_im"] = d_bt_re.transpose(0, 2, 1), d_bt_im.transpose(0, 2, 1)
    dx, dproj, G["norm_mix"] = _inproj_bwd(x, dh1, du, dzs, norm_mix, mu, w_in[:, :S5_WIDTH], w_in[:, S5_WIDTH:], L)
    G["w_in"] = _mm_tn("dw_in", xn, dproj)
    return loss_acc[0, 0], dx, G


def _all_gather(name, shards):
    nt = len(shards)

    def body(*refs):
        x_refs, out_refs = refs[:nt], refs[nt:2 * nt]
        send_sems, recv_sems, local_sems = refs[2 * nt:]
        x, y, c = lax.axis_index("x"), lax.axis_index("y"), lax.axis_index("c")
        me, sibling = (x, y, c), (x, y, 1 - c)
        chips = [(1 - x, y), (x, 1 - y), (1 - x, 1 - y)]

        def rows(t, px, py, pc):
            m_per = shards[t].shape[0]
            return out_refs[t].at[pl.ds((4 * px + 2 * py + pc) * m_per, m_per), :]

        def copy(t, k, block, to, src=None):
            return pltpu.make_async_remote_copy(
                src_ref=rows(t, *block) if src is None else src, dst_ref=rows(t, *block),
                send_sem=send_sems.at[7 * t + k], recv_sem=recv_sems.at[7 * t + k],
                device_id=to, device_id_type=pl.DeviceIdType.MESH)

        mine = [pltpu.make_async_copy(x_refs[t], rows(t, *me), local_sems.at[t]) for t in range(nt)]
        for cp in mine:
            cp.start()
        first = []
        for t in range(nt):
            first.append(copy(t, 0, me, sibling, src=x_refs[t]))
            first += [copy(t, 1 + j, me, (*chip, c), src=x_refs[t]) for j, chip in enumerate(chips)]
        for cp in first:
            cp.start()
        passed = []
        for t in range(nt):
            for j, chip in enumerate(chips):
                copy(t, 1 + j, (*chip, c), me).wait_recv()
                fwd = copy(t, 4 + j, (*chip, c), sibling)
                fwd.start()
                passed.append(fwd)
        for t in range(nt):
            copy(t, 0, sibling, me).wait_recv()
            for j, chip in enumerate(chips):
                copy(t, 4 + j, (*chip, 1 - c), me).wait_recv()
        for cp in first + passed:
            cp.wait_send()
        for cp in mine:
            cp.wait()

    return _pcall(body, name=name,
                  out_shape=[jax.ShapeDtypeStruct((N_DEV * a.shape[0], a.shape[1]), a.dtype) for a in shards],
                  in_specs=[_ANY] * nt, out_specs=[_ANY] * nt,
                  scratch_shapes=[pltpu.SemaphoreType.DMA((7 * nt,)), pltpu.SemaphoreType.DMA((7 * nt,)),
                                  pltpu.SemaphoreType.DMA((nt,))])(*shards)


_HBM = pl.BlockSpec(memory_space=pltpu.HBM)
_SEM = pl.BlockSpec(memory_space=pltpu.SEMAPHORE)
_EFFECT = pltpu.SideEffectType.DATAFLOW_SIDE_EFFECTING


def _peer_of(k):
    x, y, c = lax.axis_index("x"), lax.axis_index("y"), lax.axis_index("c")
    px, py, pc = x ^ ((k >> 2) & 1), y ^ ((k >> 1) & 1), c ^ (k & 1)
    return (px, py, pc), 4 * px + 2 * py + pc, 4 * x + 2 * y + c


def _direct_copy(t, k, src_refs, land_refs, send_sems, recv_sems, rows_of, gather):
    dev, peer, me = _peer_of(k)
    m = rows_of[t]
    src = src_refs[t] if gather else src_refs[t].at[pl.ds(peer * m, m), :]
    return pltpu.make_async_remote_copy(
        src_ref=src, dst_ref=land_refs[t].at[pl.ds(me * m, m), :],
        send_sem=send_sems.at[7 * t + k - 1], recv_sem=recv_sems.at[7 * t + k - 1],
        device_id=dev, device_id_type=pl.DeviceIdType.MESH)


def _direct_landing(t, k, src_refs, land_refs, send_sems, recv_sems, rows_of, gather):
    dev, peer, me = _peer_of(k)
    m = rows_of[t]
    src = src_refs[t] if gather else src_refs[t].at[pl.ds(me * m, m), :]
    return pltpu.make_async_remote_copy(
        src_ref=src, dst_ref=land_refs[t].at[pl.ds(peer * m, m), :],
        send_sem=send_sems.at[7 * t + k - 1], recv_sem=recv_sems.at[7 * t + k - 1],
        device_id=dev, device_id_type=pl.DeviceIdType.MESH)


def _direct_start(name, srcs, gather, dep=None):
    nt = len(srcs)
    rows_of = [a.shape[0] if gather else a.shape[0] // N_DEV for a in srcs]
    lands = [pltpu.with_memory_space_constraint(lax.empty((N_DEV * m, a.shape[1]), a.dtype), pltpu.HBM)
             for a, m in zip(srcs, rows_of)]

    n_dep = 0 if dep is None else 1

    def body(*refs):
        src_refs, land_refs = refs[:nt], refs[nt:2 * nt]
        send_sems, recv_sems = refs[2 * nt + n_dep], refs[2 * nt + n_dep + 1]
        token = refs[-1]
        for t in range(nt):
            for k in range(1, N_DEV):
                _direct_copy(t, k, src_refs, land_refs, send_sems, recv_sems, rows_of, gather).start()
        token[...] = jnp.zeros(token.shape, F32)

    out = _pcall(
        body, name=name,
        out_shape=(pltpu.SemaphoreType.DMA((7 * nt,)), pltpu.SemaphoreType.DMA((7 * nt,)),
                   *[pltpu.HBM(a.shape, a.dtype) for a in srcs], *[pltpu.HBM(a.shape, a.dtype) for a in lands],
                   jax.ShapeDtypeStruct((8, 128), F32)),
        in_specs=(_HBM,) * (2 * nt) + (pl.BlockSpec(memory_space=pl.ANY),) * n_dep,
        out_specs=(_SEM, _SEM) + (_HBM,) * (2 * nt) + (pl.BlockSpec(memory_space=pltpu.VMEM),),
        input_output_aliases={i: 2 + i for i in range(2 * nt)},
        compiler_params=pltpu.CompilerParams(has_side_effects=_EFFECT),
    )(*[pltpu.with_memory_space_constraint(a, pltpu.HBM) for a in srcs], *lands, *(() if dep is None else (dep,)))
    return (out[0], out[1], list(out[2:2 + nt]), list(out[2 + nt:2 + 2 * nt]), rows_of, gather), out[-1]


def _direct_wait(name, handle, after):
    send_sems, recv_sems, srcs, lands, rows_of, gather = handle
    nt = len(srcs)
    after = list(after) if isinstance(after, (list, tuple)) else [after]

    def body(*refs):
        src_refs, land_refs = refs[:nt], refs[nt:2 * nt]
        s_sems, r_sems = refs[2 * nt], refs[2 * nt + 1]
        for t in range(nt):
            for k in range(1, N_DEV):
                _direct_copy(t, k, src_refs, land_refs, s_sems, r_sems, rows_of, gather).wait_send()
                _direct_landing(t, k, src_refs, land_refs, s_sems, r_sems, rows_of, gather).wait_recv()

    out = _pcall(
        body, name=name,
        out_shape=tuple(pltpu.HBM(a.shape, a.dtype) for a in srcs) + tuple(pltpu.HBM(a.shape, a.dtype) for a in lands),
        in_specs=(_HBM,) * (2 * nt) + (_SEM, _SEM) + (pl.BlockSpec(memory_space=pl.ANY),) * len(after),
        out_specs=(_HBM,) * (2 * nt),
        input_output_aliases={i: i for i in range(2 * nt)},
        compiler_params=pltpu.CompilerParams(has_side_effects=_EFFECT),
    )(*srcs, *lands, send_sems, recv_sems, *after)
    return list(out[:nt]), list(out[nt:])


def _adamw_sharded(name, own, parts, w, m, v, rb, deps=()):
    R, N = own.shape

    def body(o_ref, p_ref, w_ref, m_ref, v_ref, *rest):
        g_ref, d_ref, nm_ref, nv_ref = rest[len(deps):]
        me = 4 * lax.axis_index("x") + 2 * lax.axis_index("y") + lax.axis_index("c")
        g = o_ref[...]
        for k in range(1, N_DEV):
            g = g + p_ref[me ^ k].astype(F32)
        nm = ADAM_B1 * m_ref[...] + (1.0 - ADAM_B1) * g
        nv = ADAM_B2 * v_ref[...] + (1.0 - ADAM_B2) * (g * g)
        m_hat = nm / (1.0 - ADAM_B1 ** ADAM_STEP)
        v_hat = nv / (1.0 - ADAM_B2 ** ADAM_STEP)
        g_ref[...] = g
        d_ref[...] = -ADAM_LR * (m_hat / (jnp.sqrt(v_hat) + ADAM_EPS) + ADAM_WD * w_ref[...])
        nm_ref[...] = nm
        nv_ref[...] = nv

    blk = pl.BlockSpec((rb, N), lambda i: (i, 0))
    sh = jax.ShapeDtypeStruct((R, N), F32)
    return _pcall(body, name=name, grid=(R // rb,),
                  in_specs=[blk, pl.BlockSpec((N_DEV, rb, N), lambda i: (0, i, 0)), blk, blk, blk]
                  + [pl.BlockSpec(d.shape, lambda i, nd=d.ndim: (0,) * nd) for d in deps],
                  out_specs=[blk] * 4, out_shape=[sh] * 4, compiler_params=_cparams(1))(own, parts, w, m, v, *deps)


LOSS_SLOT = "loss_partials"
SMALL_CLASSES = (
    (("s5_b_re", 32, 1024), ("s5_b_im", 32, 1024),
     ("norm_mix", 1, 1024), ("norm_ffn", 1, 1024), ("norm_ple", 1, 1024), ("final_norm", 1, 1024)),
    (("s5_d", 1, 512), ("s5_glu_b", 1, 512), ("rw_w0", 1, 512), ("rw_a0", 1, 512), ("rw_k_k", 1, 512), ("rw_k_a", 1, 512),
     ("rw_ln_w", 1, 512), ("rw_ln_b", 1, 512), ("rw_r_k", 1, 512)),
    (("rw_shift_mu", 1, 1792),),
    (("s5_lam_re", 32, 64), ("s5_lam_im", 32, 64), ("s5_c_re", 512, 64), ("s5_c_im", 512, 64)),
    (("s5_log_step", 1, 32), (LOSS_SLOT, 1, 32)),
)


def _class_rows(cls):
    return -(-sum(r for _, r, _ in cls) // 8) * 8


def _stack_class(cls, arrs):
    a = jnp.concatenate(arrs, axis=0) if len(arrs) > 1 else arrs[0]
    pad = _class_rows(cls) - a.shape[0]
    return jnp.pad(a, ((0, pad), (0, 0))) if pad else a


def _adamw_small(grads, w, m, v):
    names = [n for cls in SMALL_CLASSES for n, _, _ in cls]
    n_cls, n_par = len(SMALL_CLASSES), len(names)

    def body(*refs):
        g_refs = refs[:n_cls]
        w_refs, m_refs, v_refs = (refs[n_cls + i * n_par:n_cls + (i + 1) * n_par] for i in range(3))
        o_refs = refs[n_cls + 3 * n_par:]
        p = 0
        for cls, g_ref in zip(SMALL_CLASSES, g_refs):
            rc = _class_rows(cls)
            tot = g_ref[0:rc, :]
            for s_ in range(1, N_DEV):
                tot = tot + g_ref[s_ * rc:(s_ + 1) * rc, :]
            off = 0
            for _, r, _ in cls:
                g = tot[off:off + r, :]
                off += r
                nm = ADAM_B1 * m_refs[p][...] + (1.0 - ADAM_B1) * g
                nv = ADAM_B2 * v_refs[p][...] + (1.0 - ADAM_B2) * (g * g)
                m_hat = nm / (1.0 - ADAM_B1 ** ADAM_STEP)
                v_hat = nv / (1.0 - ADAM_B2 ** ADAM_STEP)
                o_refs[4 * p][...] = g
                o_refs[4 * p + 1][...] = -ADAM_LR * (m_hat / (jnp.sqrt(v_hat) + ADAM_EPS) + ADAM_WD * w_refs[p][...])
                o_refs[4 * p + 2][...] = nm
                o_refs[4 * p + 3][...] = nv
                p += 1

    shapes = [(r, c) for cls in SMALL_CLASSES for _, r, c in cls]
    out = _pcall(body, name="adamw_replicated",
                 out_shape=[jax.ShapeDtypeStruct(sh, F32) for sh in shapes for _ in range(4)],
                 compiler_params=pltpu.CompilerParams(vmem_limit_bytes=VMEM_LIMIT))(*grads, *w, *m, *v)
    return {n: out[4 * i:4 * i + 4] for i, n in enumerate(names)}


EARLY = (("w_in", True),)
LATE = (("ffn_w1", True), ("ffn_w3", True), ("ffn_w2", False), ("ple_gate_w", False), ("w_out", False))
GRAD_STAGES = (LATE[:4], LATE[4:])
MISC = (("s5_glu_w", False), ("rw_w2", True), ("rw_a2", True), ("rw_g2", True), ("ple_up_w", True))
SHARDED_NAMES = tuple(n for n, _ in EARLY + LATE + MISC)
PACK_COLS = 1024
WEIGHT_NAMES = ("norm_mix", "w_in", "s5_lam_re", "s5_lam_im", "s5_log_step", "s5_b_re", "s5_b_im", "s5_c_re", "s5_c_im", "s5_d",
                "s5_glu_w", "s5_glu_b", "rw_shift_mu", "rw_w0", "rw_w2", "rw_a0", "rw_a2", "rw_g2", "rw_k_k", "rw_k_a", "rw_r_k",
                "rw_ln_w", "rw_ln_b", "w_out", "norm_ffn", "ffn_w1", "ffn_w3", "ffn_w2", "norm_ple", "ple_gate_w", "ple_up_w",
                "final_norm")
SMALL_NAMES = tuple(n for n in WEIGHT_NAMES if n not in SHARDED_NAMES)
ARG_NAMES = ("x", "p") + WEIGHT_NAMES + ("loss_target",) + tuple("m_" + n for n in WEIGHT_NAMES) + tuple("v_" + n for n in WEIGHT_NAMES)


def _travel(a, tr):
    return a.T if tr else a


def _pack_misc(blocks):
    lead = blocks[0].shape[:-2]
    return jnp.concatenate([b.reshape(lead + (-1, PACK_COLS)) for b in blocks], axis=len(lead))


def _unpack_misc(packed, shapes):
    lead = packed.shape[:-2]
    out, off = [], 0
    for r, c in shapes:
        n = r * c // PACK_COLS
        out.append(lax.slice_in_dim(packed, off, off + n, axis=len(lead)).reshape(lead + (r, c)))
        off += n
    return out


def _kernel_impl(ins):
    x, p, target = ins["x"][0], ins["p"][0, 0], ins["loss_target"][0]
    me = 4 * lax.axis_index("x") + 2 * lax.axis_index("y") + lax.axis_index("c")
    small = {n: (ins[n] if n == "final_norm" else ins[n][0]) for n in SMALL_NAMES}
    trav = lambda pre, n, tr: _travel(ins[pre + n][0], tr)
    misc_shapes = [trav("", n, tr).shape for n, tr in MISC]

    early = _all_gather("ag_early", [trav("", n, tr).astype(BF16) for n, tr in EARLY]
                        + [_pack_misc([trav("", n, tr).astype(BF16) for n, tr in MISC])])
    late_handle, late_token = _direct_start("ag_late_start", [trav("", n, tr).astype(BF16) for n, tr in LATE], True, early[-1])
    W = dict(small)
    for (n, tr), g in zip(EARLY, early):
        W[n] = _travel(g, tr)
    for (n, tr), g in zip(MISC, _unpack_misc(early[-1].reshape(N_DEV, -1, PACK_COLS), misc_shapes)):
        W[n] = _travel(g.reshape(-1, g.shape[-1]), tr)

    def late_weights(after):
        shards, lands = _direct_wait("ag_late_wait", late_handle, after)
        full = [lax.dynamic_update_slice_in_dim(ld, sh, me * sh.shape[0], axis=0) for ld, sh in zip(lands, shards)]
        return {n: _travel(g, tr) for (n, tr), g in zip(LATE, full)}

    gt = lambda G, n, tr: _travel(G[n], tr)
    started = {}

    def grads_ready(stage, G):
        full = [gt(G, n, tr) for n, tr in GRAD_STAGES[stage]]
        started[stage] = (full, *_direct_start("grad_late_start%d" % stage, [a.astype(BF16) for a in full], False))
        return started[stage][2]

    loss_part, dx, G = _local_step(x, p, target, W, late_weights, grads_ready, late_token)

    misc_g = _pack_misc([gt(G, n, tr).reshape((N_DEV,) + shp) for (n, tr), shp in zip(MISC, misc_shapes)])
    early_full = [gt(G, n, tr) for n, tr in EARLY] + [misc_g.reshape(-1, PACK_COLS)]
    early_handle, early_token = _direct_start("grad_early_start", [a.astype(BF16) for a in early_full], False)
    view2 = lambda a, r, c: a.reshape(r, c)
    G[LOSS_SLOT] = jnp.full((1, 32), loss_part, F32)
    small_own = [_stack_class(cls, [view2(G[n], r, c) for n, r, c in cls]) for cls in SMALL_CLASSES]
    small_handle, small_token = _direct_start("grad_small_start", small_own, True)
    late_src, late_land = [], []
    for stage in range(len(GRAD_STAGES)):
        full, handle, _ = started[stage]
        _, land = _direct_wait("grad_late_wait%d" % stage, handle, small_token)
        late_src += full
        late_land += land

    outs = {}

    def emit(names_shapes, res):
        for tag, val in zip(("grad_", "delta_", "new_m_", "new_v_"), res):
            for n, v in names_shapes(val):
                outs[tag + n] = v

    def sharded_update(n, tr, src, land, deps=()):
        rows = src.shape[0] // N_DEV
        own = lax.dynamic_slice_in_dim(src, me * rows, rows, axis=0)
        res = _adamw_sharded("adamw_" + n, own, land.reshape(N_DEV, rows, land.shape[1]),
                             trav("", n, tr), trav("m_", n, tr), trav("v_", n, tr), _pick_rows(rows), deps)
        emit(lambda val: [(n, _travel(val, tr).reshape(ins[n].shape))], res)
        return list(res)

    for (n, tr), src, land in zip(LATE, late_src, late_land):
        sharded_update(n, tr, src, land, (early_token,))
    _, early_land = _direct_wait("grad_early_wait", early_handle, list(outs.values()))
    for (n, tr), src, land in zip(EARLY, early_full[:-1], early_land[:-1]):
        sharded_update(n, tr, src, land)
    pm = lambda pre: _pack_misc([trav(pre, n, tr) for n, tr in MISC])
    rows = early_full[-1].shape[0] // N_DEV
    res = _adamw_sharded("adamw_misc", lax.dynamic_slice_in_dim(early_full[-1], me * rows, rows, axis=0),
                         early_land[-1].reshape(N_DEV, rows, PACK_COLS), pm(""), pm("m_"), pm("v_"), rows)
    emit(lambda val: [(n, _travel(b, tr).reshape(ins[n].shape)) for (n, tr), b in zip(MISC, _unpack_misc(val, misc_shapes))], res)
    small_src, small_land = _direct_wait("grad_small_wait", small_handle, res[0])
    small_all = [lax.dynamic_update_slice_in_dim(ld, sr, me * sr.shape[0], axis=0) for ld, sr in zip(small_land, small_src)]
    flat_small = [(n, r, c) for cls in SMALL_CLASSES for n, r, c in cls]
    ins = dict(ins, **{pre + LOSS_SLOT: jnp.zeros((1, 32), F32) for pre in ("", "m_", "v_")})
    res = _adamw_small(small_all, *[[view2(ins[pre + n], r, c) for n, r, c in flat_small] for pre in ("", "m_", "v_")])
    loss = res.pop(LOSS_SLOT)[0][0, 0]
    for n, _, _ in flat_small[:-1]:
        for tag, val in zip(("grad_", "delta_", "new_m_", "new_v_"), res[n]):
            outs[tag + n] = val.reshape(ins[n].shape)
    res = [loss, dx[None]]
    for tag in ("grad_", "delta_", "new_m_", "new_v_"):
        res += [outs[tag + n] for n in WEIGHT_NAMES]
    return tuple(res)


def _pick_rows(r):
    best = 8
    for b in range(8, 257, 8):
        if r % b == 0:
            best = b
    return best


def kernel(x, p, norm_mix, w_in, s5_lam_re, s5_lam_im, s5_log_step, s5_b_re, s5_b_im, s5_c_re, s5_c_im, s5_d, s5_glu_w, s5_glu_b, rw_shift_mu, rw_w0, rw_w2, rw_a0, rw_a2, rw_g2, rw_k_k, rw_k_a, rw_r_k, rw_ln_w, rw_ln_b, w_out, norm_ffn, ffn_w1, ffn_w3, ffn_w2, norm_ple, ple_gate_w, ple_up_w, final_norm, loss_target, m_norm_mix, m_w_in, m_s5_lam_re, m_s5_lam_im, m_s5_log_step, m_s5_b_re, m_s5_b_im, m_s5_c_re, m_s5_c_im, m_s5_d, m_s5_glu_w, m_s5_glu_b, m_rw_shift_mu, m_rw_w0, m_rw_w2, m_rw_a0, m_rw_a2, m_rw_g2, m_rw_k_k, m_rw_k_a, m_rw_r_k, m_rw_ln_w, m_rw_ln_b, m_w_out, m_norm_ffn, m_ffn_w1, m_ffn_w3, m_ffn_w2, m_norm_ple, m_ple_gate_w, m_ple_up_w, m_final_norm, v_norm_mix, v_w_in, v_s5_lam_re, v_s5_lam_im, v_s5_log_step, v_s5_b_re, v_s5_b_im, v_s5_c_re, v_s5_c_im, v_s5_d, v_s5_glu_w, v_s5_glu_b, v_rw_shift_mu, v_rw_w0, v_rw_w2, v_rw_a0, v_rw_a2, v_rw_g2, v_rw_k_k, v_rw_k_a, v_rw_r_k, v_rw_ln_w, v_rw_ln_b, v_w_out, v_norm_ffn, v_ffn_w1, v_ffn_w3, v_ffn_w2, v_norm_ple, v_ple_gate_w, v_ple_up_w, v_final_norm):
    return _kernel_impl(dict(zip(ARG_NAMES, (x, p, norm_mix, w_in, s5_lam_re, s5_lam_im, s5_log_step, s5_b_re, s5_b_im, s5_c_re, s5_c_im, s5_d, s5_glu_w, s5_glu_b, rw_shift_mu, rw_w0, rw_w2, rw_a0, rw_a2, rw_g2, rw_k_k, rw_k_a, rw_r_k, rw_ln_w, rw_ln_b, w_out, norm_ffn, ffn_w1, ffn_w3, ffn_w2, norm_ple, ple_gate_w, ple_up_w, final_norm, loss_target, m_norm_mix, m_w_in, m_s5_lam_re, m_s5_lam_im, m_s5_log_step, m_s5_b_re, m_s5_b_im, m_s5_c_re, m_s5_c_im, m_s5_d, m_s5_glu_w, m_s5_glu_b, m_rw_shift_mu, m_rw_w0, m_rw_w2, m_rw_a0, m_rw_a2, m_rw_g2, m_rw_k_k, m_rw_k_a, m_rw_r_k, m_rw_ln_w, m_rw_ln_b, m_w_out, m_norm_ffn, m_ffn_w1, m_ffn_w3, m_ffn_w2, m_norm_ple, m_ple_gate_w, m_ple_up_w, m_final_norm, v_norm_mix, v_w_in, v_s5_lam_re, v_s5_lam_im, v_s5_log_step, v_s5_b_re, v_s5_b_im, v_s5_c_re, v_s5_c_im, v_s5_d, v_s5_glu_w, v_s5_glu_b, v_rw_shift_mu, v_rw_w0, v_rw_w2, v_rw_a0, v_rw_a2, v_rw_g2, v_rw_k_k, v_rw_k_a, v_rw_r_k, v_rw_ln_w, v_rw_ln_b, v_w_out, v_norm_ffn, v_ffn_w1, v_ffn_w3, v_ffn_w2, v_norm_ple, v_ple_gate_w, v_ple_up_w, v_final_norm))))
```

```python
import jax
import jax.numpy as jnp
from jax import lax
from jax.experimental import pallas as pl
from jax.experimental.pallas import tpu as pltpu

F32 = jnp.float32
BF16 = jnp.bfloat16

D_MODEL = 1024
S5_WIDTH = 512
RW_WIDTH = 512
S5_GROUP = 16
S5_GROUPS = 32
S5_STATE = 64
S5_LANES = S5_GROUPS * S5_STATE
HEAD = 64
SHIFT_COLS = 1792
IN_COLS = 2304
FFN_HIDDEN = 2816
PLE_DIM = 256
RMS_EPS = 1e-6
GN_EPS = 64e-5
L2_EPS = 1e-12
CHUNK = 64
N_DEV = 8

ADAM_LR = 0.001
ADAM_B1 = 0.9
ADAM_B2 = 0.999
ADAM_EPS = 1e-08
ADAM_WD = 0.01
ADAM_STEP = 10

VMEM_LIMIT = 56 * 1024 * 1024
_ANY = pl.BlockSpec(memory_space=pl.ANY)


def _pcall(body, **kw):
    return pl.pallas_call(body, **kw)


def _cparams(n_grid):
    return pltpu.CompilerParams(dimension_semantics=("arbitrary",) * n_grid, vmem_limit_bytes=VMEM_LIMIT)


def _dot(a, b):
    return jnp.dot(a, b, preferred_element_type=F32)


def _dot_nt(a, b):
    return lax.dot_general(a, b, (((1,), (1,)), ((), ())), preferred_element_type=F32)


def _dot_tn(a, b):
    return lax.dot_general(a, b, (((0,), (0,)), ((), ())), preferred_element_type=F32)


def _mmc(w, diff=True, tr=False):
    fw, bw = (_dot_nt, _dot) if tr else (_dot, _dot_nt)
    if not diff:
        return lambda x: fw(x.astype(BF16), w)

    @jax.custom_vjp
    def f(x):
        return fw(x.astype(BF16), w)

    def fwd(x):
        return fw(x.astype(BF16), w), None

    def bwd(_, dy):
        return (bw(dy.astype(BF16), w),)

    f.defvjp(fwd, bwd)
    return f


def _split_dot(x, m, n_split):
    acc = None
    rem = x
    for s in range(n_split):
        part = rem.astype(BF16)
        t = _dot(part, m)
        acc = t if acc is None else acc + t
        if s + 1 < n_split:
            rem = rem - part.astype(F32)
    return acc


def _segsum(m, diff=True):
    if not diff:
        return lambda x: _split_dot(x, m, 2)

    @jax.custom_vjp
    def f(x):
        return _split_dot(x, m, 2)

    def fwd(x):
        return _split_dot(x, m, 2), None

    def bwd(_, dy):
        return (_split_dot(dy, m, 2),)

    f.defvjp(fwd, bwd)
    return f


def _head_indicator(n):
    r = lax.broadcasted_iota(jnp.int32, (n, n), 0) // HEAD
    c = lax.broadcasted_iota(jnp.int32, (n, n), 1) // HEAD
    return (r == c).astype(BF16)


def _rms(x, g):
    return x * lax.rsqrt(jnp.mean(x * x, axis=-1, keepdims=True) + RMS_EPS) * g


def _softplus(x):
    return jnp.maximum(x, 0.0) + jnp.log(1.0 + jnp.exp(-jnp.abs(x)))


def _sigmoid(x):
    return 1.0 / (1.0 + jnp.exp(-x))


def _gelu(x):
    return 0.5 * x * (1.0 + jnp.tanh(0.7978845608028654 * (x + 0.044715 * (x * x * x))))


def _tok_call(name, fn, L, TB, tok_in, const_in, tok_out, acc_out=(), deps=()):
    nb = L // TB
    g8 = TB // 8
    in_specs, args = [], []
    for spec in tok_in:
        if len(spec) == 1:
            arr = spec[0]
            in_specs.append(pl.BlockSpec((arr.shape[0], TB, HEAD), lambda i: (0, i, 0)))
            args.append(arr)
            continue
        arr, width, cb = spec[:3]
        mode = spec[3] if len(spec) > 3 else None
        if mode is None:
            in_specs.append(pl.BlockSpec((TB, width), lambda i, cb=cb: (i, cb)))
        elif mode == "prev":
            in_specs.append(pl.BlockSpec((8, width), lambda i, cb=cb: (jnp.maximum(i * g8 - 1, 0), cb)))
        else:
            in_specs.append(pl.BlockSpec((8, width), lambda i, cb=cb: (jnp.minimum((i + 1) * g8, L // 8 - 1), cb)))
        args.append(arr)
    for c in const_in:
        in_specs.append(pl.BlockSpec(c.shape, lambda i, nd=c.ndim: (0,) * nd, pipeline_mode=pl.Buffered(1)))
        args.append(c)
    for d in deps:
        in_specs.append(pl.BlockSpec(d.shape, lambda i, nd=d.ndim: (0,) * nd))
        args.append(d)
    out_shape, out_specs = [], []
    for width, dt in tok_out:
        if width == "heads":
            out_shape.append(jax.ShapeDtypeStruct((N_HEAD, L, HEAD), dt))
            out_specs.append(pl.BlockSpec((N_HEAD, TB, HEAD), lambda i: (0, i, 0)))
            continue
        out_shape.append(jax.ShapeDtypeStruct((L, width), dt))
        out_specs.append(pl.BlockSpec((TB, width), lambda i: (i, 0)))
    for shp in acc_out:
        out_shape.append(jax.ShapeDtypeStruct(shp, F32))
        out_specs.append(pl.BlockSpec(shp, lambda i, nd=len(shp): (0,) * nd))
    n_tok, n_const, n_to = len(tok_in), len(const_in), len(tok_out)

    def body(*refs):
        i = pl.program_id(0)
        tv = [r[...] if len(r.shape) == 2 else jnp.concatenate([r[h] for h in range(r.shape[0])], axis=1)
              for r in refs[:n_tok]]
        cv = [r[...] for r in refs[n_tok:n_tok + n_const]]
        orefs = refs[n_tok + n_const + len(deps):]
        outs = fn(i, tv, cv)
        for r, v in zip(orefs[:n_to], outs[:n_to]):
            if len(r.shape) == 3:
                for h in range(r.shape[0]):
                    r[h] = v[:, h * HEAD:(h + 1) * HEAD].astype(r.dtype)
            else:
                r[...] = v.astype(r.dtype)
        for r, v in zip(orefs[n_to:], outs[n_to:]):
            @pl.when(i == 0)
            def _(r=r):
                r[...] = jnp.zeros(r.shape, r.dtype)

            r[...] += v

    res = _pcall(body, name=name, grid=(nb,), in_specs=in_specs, out_specs=out_specs, out_shape=out_shape,
                 compiler_params=_cparams(1))(*args)
    return res


def _pick_block(n, cap):
    best = None
    for b in range(128, min(n, cap) + 1, 128):
        if n % b == 0:
            best = b
    return best if best is not None else n


def _mm_tn(name, a, b):
    T, M = a.shape
    N = b.shape[1]
    bm, bn, bt = _pick_block(M, 1536), _pick_block(N, 1536), _pick_block(T, 1024)

    def body(a_ref, b_ref, o_ref):
        t = pl.program_id(2)

        @pl.when(t == 0)
        def _():
            o_ref[...] = jnp.zeros(o_ref.shape, F32)

        o_ref[...] += _dot_tn(a_ref[...].astype(BF16), b_ref[...].astype(BF16))

    return _pcall(body, name=name, grid=(M // bm, N // bn, T // bt),
                  in_specs=[pl.BlockSpec((bt, bm), lambda m, n, t: (t, m)), pl.BlockSpec((bt, bn), lambda m, n, t: (t, n))],
                  out_specs=pl.BlockSpec((bm, bn), lambda m, n, t: (m, n)),
                  out_shape=jax.ShapeDtypeStruct((M, N), F32), compiler_params=_cparams(3))(a, b)


def _s5_param_fn(lam_re, lam_im, log_step, bt_re, bt_im):
    dt = jnp.exp(log_step)
    e = jnp.exp(lam_re * dt)
    lb_re = e * jnp.cos(lam_im * dt)
    lb_im = e * jnp.sin(lam_im * dt)
    den = lam_re * lam_re + lam_im * lam_im
    nr, ni = lb_re - 1.0, lb_im
    co_re = (nr * lam_re + ni * lam_im) / den
    co_im = (ni * lam_re - nr * lam_im) / den
    cr, ci = co_re[:, None, :], co_im[:, None, :]
    return lb_re, lb_im, cr * bt_re - ci * bt_im, cr * bt_im + ci * bt_re


def _s5_param_fwd(lam_re, lam_im, log_step, bt_re, bt_im):
    def body(a, b, c, d, e, o1, o2, o3, o4):
        r = _s5_param_fn(a[...], b[...], c[...], d[...], e[...])
        o1[...], o2[...], o3[...], o4[...] = r

    sh = jax.ShapeDtypeStruct
    return _pcall(body, name="s5_param_fwd",
                  out_shape=[sh(lam_re.shape, F32), sh(lam_re.shape, F32), sh(bt_re.shape, F32), sh(bt_re.shape, F32)])(
        lam_re, lam_im, log_step, bt_re, bt_im)


def _s5_param_bwd(lam_re, lam_im, log_step, bt_re, bt_im, d_lb_re, d_lb_im, d_bb_re, d_bb_im):
    def body(a, b, c, d, e, g1, g2, g3, g4, o1, o2, o3, o4, o5):
        _, vjp = jax.vjp(_s5_param_fn, a[...], b[...], c[...], d[...], e[...])
        r = vjp((g1[...], g2[...], g3[...], g4[...]))
        o1[...], o2[...], o3[...], o4[...], o5[...] = r

    sh = jax.ShapeDtypeStruct
    return _pcall(body, name="s5_param_bwd",
                  out_shape=[sh(lam_re.shape, F32), sh(lam_re.shape, F32), sh(log_step.shape, F32),
                             sh(bt_re.shape, F32), sh(bt_re.shape, F32)])(
        lam_re, lam_im, log_step, bt_re, bt_im, d_lb_re, d_lb_im, d_bb_re, d_bb_im)


def _cmul(ar, ai, br, bi):
    return ar * br - ai * bi, ar * bi + ai * br


def _scan_consts(lr, li, reverse):
    n = lr.shape[1]
    sub = lax.broadcasted_iota(jnp.int32, (8, n), 0)
    pows = [(lr, li)]
    for _ in range(7):
        pows.append(_cmul(pows[-1][0], pows[-1][1], lr, li))
    steps = []
    for s in (1, 2, 4):
        m = (sub < 8 - s) if reverse else (sub >= s)
        pr, pi = pows[s - 1]
        steps.append((s, jnp.where(m, jnp.broadcast_to(pr, (8, n)), 0.0), jnp.where(m, jnp.broadcast_to(pi, (8, n)), 0.0)))
    wr = jnp.zeros((8, n), F32)
    wi = jnp.zeros((8, n), F32)
    for r in range(8):
        e = (8 - r) if reverse else (r + 1)
        wr = jnp.where(sub == r, jnp.broadcast_to(pows[e - 1][0], (8, n)), wr)
        wi = jnp.where(sub == r, jnp.broadcast_to(pows[e - 1][1], (8, n)), wi)
    return steps, wr, wi


S5_Q = 4
S5_QL = S5_WIDTH // S5_Q
S5_QS = S5_LANES // S5_Q
S5_NT = S5_LANES // 128
S5_QT = S5_QS // 128


def _s5_power_table(lb_ref, pw_re, pw_im, seg):
    for j in range(S5_NT):
        lr = jnp.broadcast_to(lb_ref[0:1, j * 128:(j + 1) * 128], (8, 128))
        li = jnp.broadcast_to(lb_ref[1:2, j * 128:(j + 1) * 128], (8, 128))

        def step(i, c, lr=lr, li=li, j=j):
            pw_re[j, i] = c[0]
            pw_im[j, i] = c[1]
            return _cmul(c[0], c[1], lr, li)

        lax.fori_loop(0, seg, step, (lr, li))


def _seg_scan(sre, sim, carry, lb_ref, pw_re, pw_im, rows, reverse):
    seg = rows // 8
    sgn = -1.0 if reverse else 1.0
    sub = lax.broadcasted_iota(jnp.int32, (8, 128), 0)
    rows_at = lambda i: pl.ds(pl.multiple_of(i * 8, 8), 8)
    entering = {}
    half_tiles = S5_NT // 2
    for half in range(2):
        tiles = list(range(half * half_tiles, (half + 1) * half_tiles))
        lam8 = [(jnp.broadcast_to(lb_ref[0:1, j * 128:(j + 1) * 128], (8, 128)),
                 sgn * jnp.broadcast_to(lb_ref[1:2, j * 128:(j + 1) * 128], (8, 128))) for j in tiles]

        def p1(ii, c):
            i = (seg - 1 - ii) if reverse else ii
            out = []
            for n, j in enumerate(tiles):
                lr, li = lam8[n]
                cr, ci = c[2 * n], c[2 * n + 1]
                nr = lr * cr - li * ci + sre[j, rows_at(i), :]
                ni = lr * ci + li * cr + sim[j, rows_at(i), :]
                sre[j, rows_at(i), :] = nr
                sim[j, rows_at(i), :] = ni
                out += [nr, ni]
            return tuple(out)

        ends = lax.fori_loop(0, seg, p1, tuple(jnp.zeros((8, 128), F32) for _ in range(2 * len(tiles))))
        cs = []
        for n, j in enumerate(tiles):
            ls = slice(j * 128, (j + 1) * 128)
            steps, wr, wi = _scan_consts(pw_re[j, seg - 1][0:1, :], sgn * pw_im[j, seg - 1][0:1, :], reverse)
            tr, ti = ends[2 * n], ends[2 * n + 1]
            for sft, pr, pi in steps:
                sh = (8 - sft) if reverse else sft
                yr, yi = pltpu.roll(tr, sh, 0), pltpu.roll(ti, sh, 0)
                tr, ti = tr + pr * yr - pi * yi, ti + pr * yi + pi * yr
            cin_r, cin_i = carry[0:1, ls], carry[1:2, ls]
            tr, ti = tr + wr * cin_r - wi * cin_i, ti + wr * cin_i + wi * cin_r
            edge_out, edge_in, sh = (0, 7, 7) if reverse else (7, 0, 1)
            carry[0:1, ls] = tr[edge_out:edge_out + 1, :]
            carry[1:2, ls] = ti[edge_out:edge_out + 1, :]
            cr = jnp.where(sub == edge_in, jnp.broadcast_to(cin_r, (8, 128)), pltpu.roll(tr, sh, 0))
            ci = jnp.where(sub == edge_in, jnp.broadcast_to(cin_i, (8, 128)), pltpu.roll(ti, sh, 0))
            cs += [cr, ci]
            entering[j] = (cr, ci)

        def p2(i, _, lo=0, hi=len(tiles)):
            k = (seg - 1 - i) if reverse else i
            for n, j in list(enumerate(tiles))[lo:hi]:
                pr, pi = pw_re[j, k], pw_im[j, k]
                cr, ci = cs[2 * n], cs[2 * n + 1]
                if reverse:
                    sre[j, rows_at(i), :] = sre[j, rows_at(i), :] + pr * cr + pi * ci
                    sim[j, rows_at(i), :] = sim[j, rows_at(i), :] + pr * ci - pi * cr
                else:
                    sre[j, rows_at(i), :] = sre[j, rows_at(i), :] + pr * cr - pi * ci
                    sim[j, rows_at(i), :] = sim[j, rows_at(i), :] + pr * ci + pi * cr
            return 0

        for lo in range(0, len(tiles), 4):
            lax.fori_loop(0, seg, lambda i, c, lo=lo: p2(i, c, lo, lo + 4), 0, unroll=2)
    return entering


class _SegIO:
    def __init__(self, hbm, buf, sems, rows, width, col0=0):
        self.hbm, self.buf, self.sems, self.rows, self.seg, self.width, self.col0 = hbm, buf, sems, rows, rows // 8, width, col0

    def _copies(self, blk, slot, to_vmem):
        out = []
        for r in range(8):
            h = self.hbm.at[pl.ds(blk * self.rows + r * self.seg, self.seg), pl.ds(self.col0, self.width)]
            v = self.buf.at[slot, :, r, :]
            out.append(pltpu.make_async_copy(h, v, self.sems.at[slot, r]) if to_vmem
                       else pltpu.make_async_copy(v, h, self.sems.at[slot, r]))
        return out

    def start(self, blk, slot, to_vmem):
        for cp in self._copies(blk, slot, to_vmem):
            cp.start()

    def wait(self, blk, slot, to_vmem):
        for cp in self._copies(blk, slot, to_vmem):
            cp.wait()

    def value(self, slot):
        return self.buf[slot].reshape(self.rows, self.width)

    def store(self, slot, val):
        self.buf[slot] = val.reshape(self.seg, 8, self.width)


def _seg_pipeline(i, nb, blk_of, ins, outs, compute):
    slot = i % 2

    @pl.when(i == 0)
    def _():
        for io in ins:
            io.start(blk_of(0), 0, True)

    @pl.when(i + 1 < nb)
    def _():
        for io in ins:
            io.start(blk_of(i + 1), 1 - slot, True)

    for io in ins:
        io.wait(blk_of(i), slot, True)

    @pl.when(i >= 2)
    def _():
        for io in outs:
            io.wait(blk_of(i - 2), slot, False)

    compute(slot)
    for io in outs:
        io.start(blk_of(i), slot, False)

    @pl.when(i == nb - 1)
    def _():
        for io in outs:
            if nb >= 2:
                io.wait(blk_of(i - 1), 1 - slot, False)
            io.wait(blk_of(i), slot, False)


def _s5_scan_fwd(proj, bq_re, bq_im, cq_re, cq_im, lbar, dskip, L, TB):
    nb = L // TB
    seg = TB // 8

    def body(u_hbm, bre, bim, cre, cim, lb_ref, d_ref, y_hbm, ck_ref, sre, sim, carry, pw_re, pw_im,
             ubuf, ybuf, sem_u, sem_y):
        i = pl.program_id(0)
        u_io = _SegIO(u_hbm, ubuf, sem_u, TB, S5_WIDTH)
        y_io = _SegIO(y_hbm, ybuf, sem_y, TB, S5_WIDTH)

        @pl.when(i == 0)
        def _():
            carry[...] = jnp.zeros(carry.shape, F32)
            _s5_power_table(lb_ref, pw_re, pw_im, seg)

        ck_ref[0] = carry[...]

        def compute(slot):
            u = u_io.value(slot)
            ub = u.astype(BF16)
            for q in range(S5_Q):
                uq = ub[:, q * S5_QL:(q + 1) * S5_QL]
                vr, vi = _dot(uq, bre[q]), _dot(uq, bim[q])
                for jj in range(S5_QT):
                    sre[q * S5_QT + jj] = vr[:, jj * 128:(jj + 1) * 128]
                    sim[q * S5_QT + jj] = vi[:, jj * 128:(jj + 1) * 128]
            _seg_scan(sre, sim, carry, lb_ref, pw_re, pw_im, TB, False)
            ys = []
            for q in range(S5_Q):
                sl = slice(q * S5_QL, (q + 1) * S5_QL)
                sr = jnp.concatenate([sre[q * S5_QT + jj] for jj in range(S5_QT)], axis=1).astype(BF16)
                si = jnp.concatenate([sim[q * S5_QT + jj] for jj in range(S5_QT)], axis=1).astype(BF16)
                ys.append(_dot(sr, cre[q]) - _dot(si, cim[q]) + u[:, sl] * d_ref[:, sl])
            y_io.store(slot, jnp.concatenate(ys, axis=1))

        _seg_pipeline(i, nb, lambda st: st, [u_io], [y_io], compute)

    full = lambda a: pl.BlockSpec(a.shape, lambda i, nd=a.ndim: (0,) * nd)
    st = pltpu.VMEM((S5_NT, TB, 128), F32)
    pw = pltpu.VMEM((S5_NT, seg, 8, 128), F32)
    io = pltpu.VMEM((2, seg, 8, S5_WIDTH), F32)
    return _pcall(
        body, name="s5_scan_fwd", grid=(nb,),
        in_specs=[_ANY, full(bq_re), full(bq_im), full(cq_re), full(cq_im), full(lbar), full(dskip)],
        out_specs=[_ANY, pl.BlockSpec((1, 8, S5_LANES), lambda i: (i, 0, 0))],
        out_shape=[jax.ShapeDtypeStruct((L, S5_WIDTH), F32), jax.ShapeDtypeStruct((nb, 8, S5_LANES), F32)],
        scratch_shapes=[st, st, pltpu.VMEM((8, S5_LANES), F32), pw, pw, io, io,
                        pltpu.SemaphoreType.DMA((2, 8)), pltpu.SemaphoreType.DMA((2, 8))],
        compiler_params=_cparams(1))(proj, bq_re, bq_im, cq_re, cq_im, lbar, dskip)


def _s5_scan_bwd(proj, dy, ck, bq_re, bq_im, cq_re, cq_im, lbar, dskip, L, TB):
    nb = L // TB
    seg = TB // 8

    def body(u_hbm, dy_hbm, ck_ref, bre, bim, cre, cim, lb_ref, d_ref,
             du_hbm, dbre, dbim, dcre, dcim, dlb_ref, dd_ref, sre, sim, gre, gim, carry, gcarry, pw_re, pw_im,
             ubuf, dybuf, dubuf, sem_u, sem_dy, sem_du):
        i = pl.program_id(0)
        u_io = _SegIO(u_hbm, ubuf, sem_u, TB, S5_WIDTH)
        dy_io = _SegIO(dy_hbm, dybuf, sem_dy, TB, S5_WIDTH)
        du_io = _SegIO(du_hbm, dubuf, sem_du, TB, S5_WIDTH)

        @pl.when(i == 0)
        def _():
            gcarry[...] = jnp.zeros(gcarry.shape, F32)
            dbre[...] = jnp.zeros(dbre.shape, F32)
            dbim[...] = jnp.zeros(dbim.shape, F32)
            dcre[...] = jnp.zeros(dcre.shape, F32)
            dcim[...] = jnp.zeros(dcim.shape, F32)
            dlb_ref[...] = jnp.zeros(dlb_ref.shape, F32)
            dd_ref[...] = jnp.zeros(dd_ref.shape, F32)
            _s5_power_table(lb_ref, pw_re, pw_im, seg)

        def compute(slot):
            u = u_io.value(slot)
            dy_v = dy_io.value(slot)
            ub = u.astype(BF16)
            dyb = dy_v.astype(BF16)
            carry[...] = ck_ref[0]
            for q in range(S5_Q):
                uq = ub[:, q * S5_QL:(q + 1) * S5_QL]
                dq = dyb[:, q * S5_QL:(q + 1) * S5_QL]
                vr, vi = _dot(uq, bre[q]), _dot(uq, bim[q])
                hr, hi = _dot_nt(dq, cre[q]), -_dot_nt(dq, cim[q])
                for jj in range(S5_QT):
                    ls = slice(jj * 128, (jj + 1) * 128)
                    sre[q * S5_QT + jj] = vr[:, ls]
                    sim[q * S5_QT + jj] = vi[:, ls]
                    gre[q * S5_QT + jj] = hr[:, ls]
                    gim[q * S5_QT + jj] = hi[:, ls]
            entering = _seg_scan(sre, sim, carry, lb_ref, pw_re, pw_im, TB, False)
            _seg_scan(gre, gim, gcarry, lb_ref, pw_re, pw_im, TB, True)

            rows_at = lambda k: pl.ds(pl.multiple_of(k * 8, 8), 8)
            for half in range(2):
                tiles = list(range(half * (S5_NT // 2), (half + 1) * (S5_NT // 2)))
                acc0 = []
                for j in tiles:
                    er, ei = entering[j]
                    gr0, gi0 = gre[j, rows_at(0), :], gim[j, rows_at(0), :]
                    acc0 += [gr0 * er + gi0 * ei, gi0 * er - gr0 * ei]

                def acc_step(k, acc, tiles=tiles):
                    out = []
                    for n, j in enumerate(tiles):
                        gr, gi_ = gre[j, rows_at(k), :], gim[j, rows_at(k), :]
                        spr, spi = sre[j, rows_at(k - 1), :], sim[j, rows_at(k - 1), :]
                        out += [acc[2 * n] + gr * spr + gi_ * spi, acc[2 * n + 1] - gr * spi + gi_ * spr]
                    return tuple(out)

                acc = lax.fori_loop(1, seg, acc_step, tuple(acc0))
                for n, j in enumerate(tiles):
                    ls = slice(j * 128, (j + 1) * 128)
                    dlb_ref[0:1, ls] += jnp.sum(acc[2 * n], axis=0, keepdims=True)
                    dlb_ref[1:2, ls] += jnp.sum(acc[2 * n + 1], axis=0, keepdims=True)

            dd_ref[...] += jnp.sum(dy_v * u, axis=0, keepdims=True)
            dus = []
            for q in range(S5_Q):
                sl = slice(q * S5_QL, (q + 1) * S5_QL)
                cat = lambda ref: jnp.concatenate([ref[q * S5_QT + jj] for jj in range(S5_QT)], axis=1).astype(BF16)
                grq, giq = cat(gre), cat(gim)
                dus.append(_dot_nt(grq, bre[q]) + _dot_nt(giq, bim[q]) + dy_v[:, sl] * d_ref[:, sl])
                dbre[q] += _dot_tn(ub[:, sl], grq)
                dbim[q] += _dot_tn(ub[:, sl], giq)
                dcre[q] += _dot_tn(cat(sre), dyb[:, sl])
                dcim[q] -= _dot_tn(cat(sim), dyb[:, sl])
            du_io.store(slot, jnp.concatenate(dus, axis=1))

        _seg_pipeline(i, nb, lambda st: nb - 1 - st, [u_io, dy_io], [du_io], compute)

    full = lambda a: pl.BlockSpec(a.shape, lambda i, nd=a.ndim: (0,) * nd)
    sh = jax.ShapeDtypeStruct
    outs = [sh((L, S5_WIDTH), F32), sh(bq_re.shape, F32), sh(bq_im.shape, F32), sh(cq_re.shape, F32), sh(cq_im.shape, F32),
            sh((8, S5_LANES), F32), sh((1, S5_WIDTH), F32)]
    fo = lambda s: pl.BlockSpec(s.shape, lambda i, nd=len(s.shape): (0,) * nd)
    st = pltpu.VMEM((S5_NT, TB, 128), F32)
    pw = pltpu.VMEM((S5_NT, seg, 8, 128), F32)
    io = pltpu.VMEM((2, seg, 8, S5_WIDTH), F32)
    sem = pltpu.SemaphoreType.DMA((2, 8))
    return _pcall(
        body, name="s5_scan_bwd", grid=(nb,),
        in_specs=[_ANY, _ANY, pl.BlockSpec((1, 8, S5_LANES), lambda i: (nb - 1 - i, 0, 0)),
                  full(bq_re), full(bq_im), full(cq_re), full(cq_im), full(lbar), full(dskip)],
        out_specs=[_ANY] + [fo(s) for s in outs[1:]],
        out_shape=outs,
        scratch_shapes=[st] * 4 + [pltpu.VMEM((8, S5_LANES), F32)] * 2 + [pw, pw, io, io, io, sem, sem, sem],
        compiler_params=_cparams(1))(proj, dy, ck, bq_re, bq_im, cq_re, cq_im, lbar, dskip)


N_HEAD = RW_WIDTH // HEAD
_NN = (((2,), (1,)), ((0,), (0,)))
_NT = (((2,), (2,)), ((0,), (0,)))
_TN = (((1,), (1,)), ((0,), (0,)))


def _hi_lo(x):
    h = x.astype(BF16)
    return h, (x - h.astype(F32)).astype(BF16)


def _mm_acc(a, b, dims, passes=3):
    dg = lambda p, q: lax.dot_general(p, q, dims, preferred_element_type=F32)
    if passes == 1:
        return dg(a.astype(BF16), b.astype(BF16))
    ah, al = _hi_lo(a)
    bh, bl = _hi_lo(b)
    return dg(ah, bh) + dg(ah, bl) + dg(al, bh)


def _cumsum_rows(x, transpose):
    h, n, _ = x.shape
    ti = lax.broadcasted_iota(jnp.int32, (h, n, n), 1)
    tj = lax.broadcasted_iota(jnp.int32, (h, n, n), 2)
    m = ((tj >= ti) if transpose else (tj <= ti)).astype(BF16)
    acc, rem = None, x
    for s in range(3):
        part = rem.astype(BF16)
        t = lax.dot_general(m, part, _NN, preferred_element_type=F32)
        acc = t if acc is None else acc + t
        if s < 2:
            rem = rem - part.astype(F32)
    return acc


def _slices(x, axis, sizes):
    out, off = [], 0
    for n in sizes:
        out.append(lax.slice_in_dim(x, off, off + n, axis=axis))
        off += n
    return tuple(out)


def _cat_op(axis, sizes, diff):
    plain = lambda *xs: jnp.concatenate(xs, axis=axis)
    if not diff:
        return plain
    f = jax.custom_vjp(plain)
    f.defvjp(lambda *xs: (plain(*xs), None), lambda _, d: _slices(d, axis, sizes))
    return f


def _split_op(axis, sizes, diff):
    plain = lambda x: _slices(x, axis, sizes)
    if not diff:
        return plain
    f = jax.custom_vjp(plain)
    f.defvjp(lambda x: (plain(x), None), lambda _, d: (jnp.concatenate(d, axis=axis),))
    return f


def _mm_ops(diff, passes):
    mm = lambda a, b, dims: _mm_acc(a, b, dims, passes)
    if not diff:
        return (lambda a, b: mm(a, b, _NN), lambda a, b: mm(a, b, _NT), lambda a, b: mm(a, b, _TN))

    @jax.custom_vjp
    def nn(a, b):
        return mm(a, b, _NN)

    nn.defvjp(lambda a, b: (mm(a, b, _NN), (a, b)), lambda r, d: (mm(d, r[1], _NT), mm(r[0], d, _TN)))

    @jax.custom_vjp
    def nt(a, b):
        return mm(a, b, _NT)

    nt.defvjp(lambda a, b: (mm(a, b, _NT), (a, b)), lambda r, d: (mm(d, r[1], _NN), mm(d, r[0], _TN)))

    @jax.custom_vjp
    def tn(a, b):
        return mm(a, b, _TN)

    tn.defvjp(lambda a, b: (mm(a, b, _TN), (a, b)), lambda r, d: (mm(r[1], d, _NT), mm(r[0], d, _NN)))
    return nn, nt, tn


def _cums_op(diff):
    if not diff:
        return lambda x: _cumsum_rows(x, False)

    @jax.custom_vjp
    def cums(x):
        return _cumsum_rows(x, False)

    cums.defvjp(lambda x: (_cumsum_rows(x, False), None), lambda _, d: (_cumsum_rows(d, True),))
    return cums


WKV_PASSES = (1, 1, 1, 1, 1)


WKV_SUB = 4
WKV_BLOCK = CHUNK * WKV_SUB


def _wkv_block(s0, r, w, k, v, a, b, diff):
    p_pair, p_val, p_solve, p_out, p_state = WKV_PASSES
    cums = _cums_op(diff)
    _, nt_pair, _ = _mm_ops(diff, p_pair)
    nn_val, _, _ = _mm_ops(diff, p_val)
    nn_solve, _, _ = _mm_ops(diff, p_solve)
    nn_out, nt_out, _ = _mm_ops(diff, p_out)
    nn_state, _, tn_state = _mm_ops(diff, p_state)
    h, d, n, sub = s0.shape[0], s0.shape[2], CHUNK, WKV_SUB
    hb = h * sub
    to_chunks = lambda t: _cat_op(0, (h,) * sub, diff)(*_split_op(1, (n,) * sub, diff)(t))
    r, w, k, v, a, b = (to_chunks(t) for t in (r, w, k, v, a, b))
    cat_rows2 = _cat_op(1, (n, n), diff)
    cat_lanes2 = _cat_op(2, (n, n), diff)
    split_rows2 = _split_op(1, (n, n), diff)
    split_lanes2 = _split_op(2, (n, n), diff)
    ti = lax.broadcasted_iota(jnp.int32, (hb, n, n), 1)
    tj = lax.broadcasted_iota(jnp.int32, (hb, n, n), 2)
    incl, strict = tj <= ti, tj < ti
    logw = jnp.log(w)
    cum = cums(logw)
    g_in, g_ex, g_inv = jnp.exp(cum), jnp.exp(cum - logw), jnp.exp(-cum)
    ae, re, bi, ki = a * g_ex, r * g_in, b * g_inv, k * g_inv
    top, bot = split_rows2(nt_pair(cat_rows2(ae, re), cat_rows2(bi, ki)))
    tab, tak = split_lanes2(top)
    qb, qk = split_lanes2(bot)
    tab, tak = jnp.where(strict, tab, 0.0), jnp.where(strict, tak, 0.0)
    qb, qk = jnp.where(incl, qb, 0.0), jnp.where(incl, qk, 0.0)
    tak_v, qk_v = split_rows2(nn_val(cat_rows2(tak, qk), v))
    x = cat_lanes2(ae, tak_v)
    npow = tab
    steps = max(1, (n - 1).bit_length())
    for i in range(steps):
        x = x + nn_solve(npow, x)
        if i + 1 < steps:
            npow = nn_solve(npow, npow)
    ae_m, uc = split_lanes2(x)
    qx = nn_out(qb, x)
    q_ae, q_uc = split_lanes2(qx)
    re_m = re + q_ae
    yc = q_uc + qk_v
    g_end = jnp.exp(jnp.sum(logw, axis=1, keepdims=True))
    bg, kg = bi * g_end, ki * g_end
    tm = tn_state(ae_m, bg)
    sc = tn_state(cat_rows2(uc, v), cat_rows2(bg, kg))
    per_chunk = _split_op(0, (h,) * sub, diff)
    re_m, yc, g_end, tm, sc = (per_chunk(t) for t in (re_m, yc, g_end, tm, sc))
    ys, s = [], s0
    for i in range(sub):
        ys.append(nt_out(re_m[i], s) + yc[i])
        s = s * g_end[i] + nn_state(s, tm[i]) + sc[i]
    return _cat_op(1, (n,) * sub, diff)(*ys), s


def _wkv_fwd(r, w, k, v, a, b, L):
    nc = L // WKV_BLOCK

    def body(r_ref, w_ref, k_ref, v_ref, a_ref, b_ref, y_ref, ck_ref, s_ref):
        c = pl.program_id(0)

        @pl.when(c == 0)
        def _():
            s_ref[...] = jnp.zeros(s_ref.shape, F32)

        s0 = s_ref[...]
        ck_ref[0] = s0
        y, s1 = _wkv_block(s0, r_ref[...], w_ref[...], k_ref[...], v_ref[...], a_ref[...], b_ref[...], False)
        y_ref[...] = y
        s_ref[...] = s1

    blk = pl.BlockSpec((N_HEAD, WKV_BLOCK, HEAD), lambda c: (0, c, 0))
    return _pcall(
        body, name="wkv_fwd", grid=(nc,), in_specs=[blk] * 6,
        out_specs=[blk, pl.BlockSpec((1, N_HEAD, HEAD, HEAD), lambda c: (c, 0, 0, 0))],
        out_shape=[jax.ShapeDtypeStruct((N_HEAD, L, HEAD), F32), jax.ShapeDtypeStruct((nc, N_HEAD, HEAD, HEAD), F32)],
        scratch_shapes=[pltpu.VMEM((N_HEAD, HEAD, HEAD), F32)],
        compiler_params=_cparams(1))(r, w, k, v, a, b)


def _wkv_bwd(r, w, k, v, a, b, dy, ck, L, deps=()):
    nc = L // WKV_BLOCK

    def body(r_ref, w_ref, k_ref, v_ref, a_ref, b_ref, dy_ref, ck_ref, *rest):
        dr_ref, dw_ref, dk_ref, dv_ref, da_ref, db_ref, ds_ref = rest[len(deps):]
        c = pl.program_id(0)

        @pl.when(c == 0)
        def _():
            ds_ref[...] = jnp.zeros(ds_ref.shape, F32)

        _, vjp = jax.vjp(lambda *t: _wkv_block(*t, True), ck_ref[0], r_ref[...], w_ref[...], k_ref[...], v_ref[...],
                         a_ref[...], b_ref[...])
        g = vjp((dy_ref[...], ds_ref[...]))
        ds_ref[...] = g[0]
        for o_ref, val in zip((dr_ref, dw_ref, dk_ref, dv_ref, da_ref, db_ref), g[1:]):
            o_ref[...] = val

    blk = pl.BlockSpec((N_HEAD, WKV_BLOCK, HEAD), lambda c: (0, nc - 1 - c, 0))
    sh = jax.ShapeDtypeStruct((N_HEAD, L, HEAD), F32)
    return _pcall(
        body, name="wkv_bwd", grid=(nc,),
        in_specs=[blk] * 7 + [pl.BlockSpec((1, N_HEAD, HEAD, HEAD), lambda c: (nc - 1 - c, 0, 0, 0))]
        + [pl.BlockSpec(d.shape, lambda c, nd=d.ndim: (0,) * nd) for d in deps],
        out_specs=[blk] * 6, out_shape=[sh] * 6,
        scratch_shapes=[pltpu.VMEM((N_HEAD, HEAD, HEAD), F32)],
        compiler_params=_cparams(1))(r, w, k, v, a, b, dy, ck, *deps)


TB = 256


def _bf(x):
    return x.astype(BF16)


def _inproj_fwd(x, norm_mix, w_in, L, deps=()):
    def fn(i, tv, cv):
        xn = _rms(tv[0], cv[0])
        return _dot(_bf(xn), cv[1]), xn

    return _tok_call("inproj_fwd", fn, L, 2 * TB, [(x, D_MODEL, 0)], [norm_mix, w_in], [(IN_COLS, F32), (D_MODEL, BF16)],
                     deps=deps)


def _s5_post_fn(glu_w, wtop, diff=True):
    mg = _mmc(glu_w, diff)
    mt = _mmc(wtop, diff) if wtop is not None else None

    def f(y, glu_b, e):
        z = _gelu(y)
        out = z * _sigmoid(mg(z) + glu_b + e)
        res = mt(out) if mt is not None else out
        return res, (z, out)

    return f


def _s5_post_fwd(y, glu_w, glu_b, L):
    def fn(i, tv, cv):
        out, _ = _s5_post_fn(cv[0], None, False)(tv[0], cv[1], 0.0)
        return (out,)

    return _tok_call("s5_post_fwd", fn, L, 2 * TB, [(y, S5_WIDTH, 0)], [glu_w, glu_b], [(S5_WIDTH, F32)])[0]


def _s5_post_bwd(y, dh1, glu_w, glu_b, wtop, L, deps=()):
    def fn(i, tv, cv):
        e0 = jnp.zeros(tv[0].shape, F32)
        _, vjp, (z, out) = jax.vjp(_s5_post_fn(cv[0], cv[2]), tv[0], cv[1], e0, has_aux=True)
        dy, db, de = vjp(tv[1])
        return dy, db, _dot_tn(_bf(z), _bf(de)), _dot_tn(_bf(out), _bf(tv[1]))

    return _tok_call("s5_post_bwd", fn, L, 2 * TB, [(y, S5_WIDTH, 0), (dh1, D_MODEL, 0)], [glu_w, glu_b, wtop],
                     [(S5_WIDTH, F32)], [(1, S5_WIDTH), (S5_WIDTH, S5_WIDTH), (S5_WIDTH, D_MODEL)], deps=deps)


RW_COLBLK = ((RW_WIDTH, 1), (RW_WIDTH, 2), (RW_WIDTH, 3), (128, 16), (128, 17))
RW_MU = ((0, 512), (512, 1024), (1024, 1536), (1536, 1664), (1664, 1792))


def _rw_pre_fn(w2pad, a2pad, g2, diff=True):
    m_w, m_a, m_g = _mmc(w2pad, diff), _mmc(a2pad, diff), _mmc(g2, diff)
    seg = _segsum(_head_indicator(RW_WIDTH), diff)

    def f(zr, zk, zv, zwa, zg, w0, a0, k_k, k_a, e_w, e_a):
        wl_t = jnp.tanh(zwa)
        wlin = w0 + m_w(wl_t) + e_w
        w = -_softplus(-wlin) - 0.5
        decay = jnp.exp(-jnp.exp(w))
        a = _sigmoid(a0 + m_a(zwa) + e_a)
        sg = _sigmoid(zg)
        g = m_g(sg)
        kk = zk * k_k
        kkn = kk / jnp.maximum(jnp.sqrt(seg(kk * kk)), L2_EPS)
        kf = zk * (1.0 + (a - 1.0) * k_a)
        return (zr, decay, kf, zv, -kkn, kkn * a, g), (wl_t, sg)

    return f


def _rw_shifted(i, tv, mu):
    sub = lax.broadcasted_iota(jnp.int32, (tv[0].shape[0], 1), 0)
    zs, dif = [], []
    for n in range(5):
        z = tv[n]
        last = jnp.where(i == 0, 0.0, tv[5 + n][7:8, :])
        prev = jnp.where(sub == 0, last, pltpu.roll(z, 1, 0))
        m = mu[:, RW_MU[n][0]:RW_MU[n][1]]
        zs.append(z + (prev - z) * m)
        dif.append(prev - z)
    return zs, dif


def _rw_tok_in(proj):
    return [(proj, wd, cb) for wd, cb in RW_COLBLK] + [(proj, wd, cb, "prev") for wd, cb in RW_COLBLK]


def _rw_pre_fwd(proj, mu, w0, a0, k_k, k_a, w2pad, a2pad, g2, L):
    def fn(i, tv, cv):
        zs, _ = _rw_shifted(i, tv, cv[0])
        outs, _ = _rw_pre_fn(cv[5], cv[6], cv[7], False)(*zs, cv[1], cv[2], cv[3], cv[4], 0.0, 0.0)
        return outs

    return _tok_call("rw_pre_fwd", fn, L, 2 * TB, _rw_tok_in(proj), [mu, w0, a0, k_k, k_a, w2pad, a2pad, g2],
                     [("heads", F32)] * 6 + [(RW_WIDTH, F32)])


def _rw_pre_bwd(proj, cots, mu, w0, a0, k_k, k_a, w2pad, a2pad, g2, L):
    def fn(i, tv, cv):
        zs, dif = _rw_shifted(i, tv[:10], cv[0])
        dr1, dr2, dw, dk1, dk2, dv1, dv2, da, db, dg = tv[10:]
        e0 = jnp.zeros((TB, RW_WIDTH), F32)
        _, vjp, (wl_t, sg) = jax.vjp(_rw_pre_fn(cv[5], cv[6], cv[7]), *zs, cv[1], cv[2], cv[3], cv[4], e0, e0, has_aux=True)
        g = vjp((dr1 + dr2, dw, dk1 + dk2, dv1 + dv2, da, db, dg))
        dzs = jnp.concatenate(g[:5], axis=1)
        dmu = jnp.concatenate([jnp.sum(g[n] * dif[n], axis=0, keepdims=True) for n in range(5)], axis=1)
        lora = (_dot_tn(_bf(wl_t), _bf(g[9])), _dot_tn(_bf(zs[3]), _bf(g[10])), _dot_tn(_bf(sg), _bf(dg)))
        return (dzs, dmu, g[5], g[6], g[7], g[8]) + lora

    tok_in = _rw_tok_in(proj) + [((c,) if c.ndim == 3 else (c, RW_WIDTH, 0)) for c in cots]
    return _tok_call("rw_pre_bwd", fn, L, TB, tok_in, [mu, w0, a0, k_k, k_a, w2pad, a2pad, g2],
                     [(SHIFT_COLS, F32)], [(1, SHIFT_COLS)] + [(1, RW_WIDTH)] * 4 + [(128, RW_WIDTH)] * 3)


def _rw_post_fn(wbot, diff=True):
    seg = _segsum(_head_indicator(RW_WIDTH), diff)
    mb = _mmc(wbot, diff) if wbot is not None else None

    def f(y, r, kf, v, g, ln_w, ln_b, r_k):
        mean = seg(y) * (1.0 / HEAD)
        yc = y - mean
        var = seg(yc * yc) * (1.0 / HEAD)
        yn = yc * lax.rsqrt(var + GN_EPS) * ln_w + ln_b
        bonus = seg(r * kf * r_k) * v
        out = (yn + bonus) * g
        res = mb(out) if mb is not None else out
        return res, out

    return f


def _rw_post_fwd(y, r, kf, v, g, ln_w, ln_b, r_k, L):
    def fn(i, tv, cv):
        out, _ = _rw_post_fn(None, False)(*tv, *cv)
        return (out,)

    return _tok_call("rw_post_fwd", fn, L, 2 * TB, [(t,) for t in (y, r, kf, v)] + [(g, RW_WIDTH, 0)], [ln_w, ln_b, r_k],
                     [(RW_WIDTH, F32)])[0]


def _rw_post_bwd(y, r, kf, v, g, dh1, ln_w, ln_b, r_k, wbot, L):
    def fn(i, tv, cv):
        _, vjp, out = jax.vjp(_rw_post_fn(cv[3]), *tv[:5], cv[0], cv[1], cv[2], has_aux=True)
        gr = vjp(tv[5])
        return gr[0], gr[1], gr[2], gr[3], gr[4], gr[5], gr[6], gr[7], _dot_tn(_bf(out), _bf(tv[5]))

    return _tok_call("rw_post_bwd", fn, L, 2 * TB, [(t,) for t in (y, r, kf, v)] + [(g, RW_WIDTH, 0), (dh1, D_MODEL, 0)],
                     [ln_w, ln_b, r_k, wbot], [("heads", F32)] + [(RW_WIDTH, F32)] * 4,
                     [(1, RW_WIDTH)] * 3 + [(RW_WIDTH, D_MODEL)])


TB_FFN = 256
FFN_FWD_TB = 256


def _swiglu(a1, a3):
    return a1 * _sigmoid(a1) * a3


def _mixffn_fwd(x, s5_out, rw_out, wtop, wbot, norm_ffn, w1, w3, w2, L):
    def fn(i, tv, cv):
        h1 = tv[0] + _dot(_bf(tv[1]), cv[0]) + _dot(_bf(tv[2]), cv[1])
        hn = _bf(_rms(h1, cv[2]))
        a1, a3 = _dot(hn, cv[3]), _dot(hn, cv[4])
        hm = _bf(_swiglu(a1, a3))
        return h1, h1 + _dot(hm, cv[5]), a1, a3, hm, hn

    return _tok_call("mixffn_fwd", fn, L, FFN_FWD_TB, [(x, D_MODEL, 0), (s5_out, S5_WIDTH, 0), (rw_out, RW_WIDTH, 0)],
                     [wtop, wbot, norm_ffn, w1, w3, w2],
                     [(D_MODEL, F32), (D_MODEL, F32), (FFN_HIDDEN, BF16), (FFN_HIDDEN, BF16), (FFN_HIDDEN, BF16), (D_MODEL, BF16)])


def _ffn_bwd(h1, dh2, a1, a3, norm_ffn, w1, w3, w2, L):
    def fn(i, tv, cv):
        dhm = _dot_nt(_bf(tv[1]), cv[3])
        _, vjp_act = jax.vjp(_swiglu, tv[2].astype(F32), tv[3].astype(F32))
        d1, d3 = vjp_act(dhm)
        dhn = _dot_nt(_bf(d1), cv[1]) + _dot_nt(_bf(d3), cv[2])
        _, vjp_norm = jax.vjp(_rms, tv[0], cv[0])
        dh1, dn = vjp_norm(dhn)
        return tv[1] + dh1, d1, d3, dn

    return _tok_call("ffn_bwd", fn, L, TB_FFN,
                     [(h1, D_MODEL, 0), (dh2, D_MODEL, 0), (a1, FFN_HIDDEN, 0), (a3, FFN_HIDDEN, 0)], [norm_ffn, w1, w3, w2],
                     [(D_MODEL, F32), (FFN_HIDDEN, BF16), (FFN_HIDDEN, BF16)], [(1, D_MODEL)])


def _ple_loss_fb(h2, p, target, norm_ple, final_norm, wg, wu, L):
    def fn(i, tv, cv):
        mgate, mup = _mmc(cv[2]), _mmc(cv[3], False)

        def f(h2_, norm_ple_, final_norm_, eg, eu):
            hn = _rms(h2_, norm_ple_)
            gate = _sigmoid(mgate(hn) + eg)
            h3 = h2_ + gate * (mup(tv[1]) + eu)
            out = _rms(h3, final_norm_)
            d = out - tv[2]
            return 0.5 * jnp.sum(jnp.mean(d * d, axis=-1, keepdims=True)), hn

        e0 = jnp.zeros(tv[0].shape, F32)
        loss, vjp, hn = jax.vjp(f, tv[0], cv[0], cv[1], e0, e0, has_aux=True)
        dh2, dnp, dfn, deg, deu = vjp(jnp.ones((), F32))
        return (dh2, dh2, jnp.full((8, 128), loss, F32), dnp, dfn,
                _dot_tn(_bf(hn), _bf(deg)), _dot_tn(_bf(tv[1]), _bf(deu)))

    return _tok_call("ple_loss_fb", fn, L, 2 * TB, [(h2, D_MODEL, 0), (p, PLE_DIM, 0), (target, D_MODEL, 0)],
                     [norm_ple, final_norm, wg, wu], [(D_MODEL, F32), (D_MODEL, BF16)],
                     [(8, 128), (1, D_MODEL), (1, D_MODEL), (D_MODEL, D_MODEL), (PLE_DIM, D_MODEL)])


def _inproj_bwd(x, dh1, du, dzs, norm_mix, mu, w_u, w_z, L):
    tb = 2 * TB
    nb = L // tb

    def fn(i, tv, cv):
        sub = lax.broadcasted_iota(jnp.int32, (tb, 1), 0)
        m = cv[1]
        b = tv[3] * m
        nxt = jnp.where(i == nb - 1, 0.0, tv[4][0:1, :] * m)
        dz = tv[3] * (1.0 - m) + jnp.where(sub == tb - 1, nxt, pltpu.roll(b, tb - 1, 0))
        dub, dzb = _bf(tv[2]), _bf(dz)
        dxn = _dot_nt(dub, cv[2]) + _dot_nt(dzb, cv[3])
        _, vjp = jax.vjp(_rms, tv[0], cv[0])
        dx, dn = vjp(dxn)
        return tv[1] + dx, jnp.concatenate([dub, dzb], axis=1), dn

    return _tok_call("inproj_bwd", fn, L, tb,
                     [(x, D_MODEL, 0), (dh1, D_MODEL, 0), (du, S5_WIDTH, 0), (dzs, SHIFT_COLS, 0), (dzs, SHIFT_COLS, 0, "next")],
                     [norm_mix, mu, w_u, w_z], [(D_MODEL, F32), (IN_COLS, BF16)], [(1, D_MODEL)])


def _eye8(dt):
    return jnp.eye(8, dtype=dt)


def _quarter_b(bb):
    return jnp.einsum("hg,qgcp->qhcgp", _eye8(bb.dtype), bb.reshape(S5_Q, 8, S5_GROUP, S5_STATE)).reshape(S5_Q, S5_QL, S5_QS)


def _unquarter_b(d):
    return jnp.einsum("qhcgp,hg->qgcp", d.reshape(S5_Q, 8, S5_GROUP, 8, S5_STATE), _eye8(d.dtype)).reshape(
        S5_GROUPS, S5_GROUP, S5_STATE)


def _quarter_c(c):
    return jnp.einsum("gh,qgcp->qgphc", _eye8(c.dtype), c.reshape(S5_Q, 8, S5_GROUP, S5_STATE)).reshape(S5_Q, S5_QS, S5_QL)


def _unquarter_c(d):
    return jnp.einsum("qgphc,gh->qgcp", d.reshape(S5_Q, 8, S5_STATE, 8, S5_GROUP), _eye8(d.dtype)).reshape(
        S5_GROUPS, S5_GROUP, S5_STATE)


def _local_step(x, p, target, W, late_weights=None, grads_ready=None, first_dep=None):
    L = x.shape[0]
    r2 = lambda v: v.reshape(1, -1)
    w_in = W["w_in"]
    w2pad = jnp.pad(W["rw_w2"], ((0, 64), (0, 0)))
    a2pad = jnp.pad(W["rw_a2"], ((64, 0), (0, 0)))
    mu = r2(W["rw_shift_mu"])
    rw_vec = [r2(W[n]) for n in ("rw_w0", "rw_a0", "rw_k_k", "rw_k_a")]
    ln_w, ln_b, r_k = r2(W["rw_ln_w"]), r2(W["rw_ln_b"]), r2(W["rw_r_k"])

    lam_re, lam_im = W["s5_lam_re"], W["s5_lam_im"]
    log_step = W["s5_log_step"].reshape(S5_GROUPS, 1)
    bt_re, bt_im = W["s5_b_re"].transpose(0, 2, 1), W["s5_b_im"].transpose(0, 2, 1)
    lb_re, lb_im, bb_re, bb_im = _s5_param_fwd(lam_re, lam_im, log_step, bt_re, bt_im)
    bq_re, bq_im = _quarter_b(bb_re).astype(BF16), _quarter_b(bb_im).astype(BF16)
    cq_re, cq_im = _quarter_c(W["s5_c_re"]).astype(BF16), _quarter_c(W["s5_c_im"]).astype(BF16)
    lbar = jnp.concatenate([lb_re.reshape(1, -1), lb_im.reshape(1, -1), jnp.zeros((6, S5_LANES), F32)], axis=0)
    dskip = r2(W["s5_d"])
    glu_b = r2(W["s5_glu_b"])
    norm_mix, norm_ffn, norm_ple, final_norm = (r2(W[n]) for n in ("norm_mix", "norm_ffn", "norm_ple", "final_norm"))

    proj, xn = _inproj_fwd(x, norm_mix, w_in, L, () if first_dep is None else (first_dep,))
    y_s5, ck5 = _s5_scan_fwd(proj, bq_re, bq_im, cq_re, cq_im, lbar, dskip, L, TB)
    s5_out = _s5_post_fwd(y_s5, W["s5_glu_w"], glu_b, L)
    r, wd, kf, v, a_s, b_s, g = _rw_pre_fwd(proj, mu, *rw_vec, w2pad, a2pad, W["rw_g2"], L)
    scan_in = (r, wd, kf, v, a_s, b_s)
    y_wkv, ckw = _wkv_fwd(*scan_in, L)
    rw_out = _rw_post_fwd(y_wkv, r, kf, v, g, ln_w, ln_b, r_k, L)
    if late_weights is not None:
        W = dict(W, **late_weights(rw_out))
    wtop, wbot = W["w_out"][:S5_WIDTH], W["w_out"][S5_WIDTH:]
    h1, h2, a1_bf, a3_bf, hm, hn_ffn = _mixffn_fwd(x, s5_out, rw_out, wtop, wbot, norm_ffn, W["ffn_w1"], W["ffn_w3"], W["ffn_w2"], L)

    G = {}
    dh2, dh2_bf, loss_acc, G["norm_ple"], G["final_norm"], G["ple_gate_w"], G["ple_up_w"] = _ple_loss_fb(
        h2, p, target, norm_ple, final_norm, W["ple_gate_w"], W["ple_up_w"], L)
    dh1, da1, da3, G["norm_ffn"] = _ffn_bwd(h1, dh2, a1_bf, a3_bf, norm_ffn, W["ffn_w1"], W["ffn_w3"], W["ffn_w2"], L)
    G["ffn_w1"] = _mm_tn("dw_ffn_w1", hn_ffn, da1)
    G["ffn_w3"] = _mm_tn("dw_ffn_w3", hn_ffn, da3)
    G["ffn_w2"] = _mm_tn("dw_ffn_w2", hm, dh2_bf)
    dep_a = grads_ready(0, G) if grads_ready is not None else None
    dy_s5, G["s5_glu_b"], G["s5_glu_w"], d_wtop = _s5_post_bwd(y_s5, dh1, W["s5_glu_w"], glu_b, wtop, L,
                                                               () if dep_a is None else (dep_a,))
    dy_wkv, dr2, dk2, dv2, dg, G["rw_ln_w"], G["rw_ln_b"], G["rw_r_k"], d_wbot = _rw_post_bwd(
        y_wkv, r, kf, v, g, dh1, ln_w, ln_b, r_k, wbot, L)
    G["w_out"] = jnp.concatenate([d_wtop, d_wbot], axis=0)
    dep = grads_ready(1, G) if grads_ready is not None else None
    dr1, dwd, dk1, dv1, da_s, db_s = _wkv_bwd(*scan_in, dy_wkv, ckw, L, () if dep is None else (dep,))
    (dzs, G["rw_shift_mu"], G["rw_w0"], G["rw_a0"], G["rw_k_k"], G["rw_k_a"], d_w2pad, d_a2pad, G["rw_g2"]) = _rw_pre_bwd(
        proj, (dr1, dr2, dwd, dk1, dk2, dv1, dv2, da_s, db_s, dg), mu, *rw_vec, w2pad, a2pad, W["rw_g2"], L)
    G["rw_w2"], G["rw_a2"] = d_w2pad[:64], d_a2pad[64:]
    du, dbq_re, dbq_im, dcq_re, dcq_im, dlbar, G["s5_d"] = _s5_scan_bwd(
        proj, dy_s5, ck5, bq_re, bq_im, cq_re, cq_im, lbar, dskip, L, TB)
    G["s5_c_re"], G["s5_c_im"] = _unquarter_c(dcq_re), _unquarter_c(dcq_im)
    d_lam_re, d_lam_im, d_ls, d_bt_re, d_bt_im = _s5_param_bwd(
        lam_re, lam_im, log_step, bt_re, bt_im, dlbar[0].reshape(S5_GROUPS, S5_STATE), dlbar[1].reshape(S5_GROUPS, S5_STATE),
        _unquarter_b(dbq_re), _unquarter_b(dbq_im))
    G["s5_lam_re"], G["s5_lam_im"], G["s5_log_step"] = d_lam_re, d_lam_im, d_ls.reshape(S5_GROUPS)
    G["s5_b_re"], G["s5_b_im"] = d_bt_re.transpose(0, 2, 1), d_bt_im.transpose(0, 2, 1)
    dx, dproj, G["norm_mix"] = _inproj_bwd(x, dh1, du, dzs, norm_mix, mu, w_in[:, :S5_WIDTH], w_in[:, S5_WIDTH:], L)
    G["w_in"] = _mm_tn("dw_in", xn, dproj)
    return loss_acc[0, 0], dx, G


def _all_gather(name, shards):
    nt = len(shards)

    def body(*refs):
        x_refs, out_refs = refs[:nt], refs[nt:2 * nt]
        send_sems, recv_sems, local_sems = refs[2 * nt:]
        x, y, c = lax.axis_index("x"), lax.axis_index("y"), lax.axis_index("c")
        me, sibling = (x, y, c), (x, y, 1 - c)
        chips = [(1 - x, y), (x, 1 - y), (1 - x, 1 - y)]

        def rows(t, px, py, pc):
            m_per = shards[t].shape[0]
            return out_refs[t].at[pl.ds((4 * px + 2 * py + pc) * m_per, m_per), :]

        def copy(t, k, block, to, src=None):
            return pltpu.make_async_remote_copy(
                src_ref=rows(t, *block) if src is None else src, dst_ref=rows(t, *block),
                send_sem=send_sems.at[7 * t + k], recv_sem=recv_sems.at[7 * t + k],
                device_id=to, device_id_type=pl.DeviceIdType.MESH)

        mine = [pltpu.make_async_copy(x_refs[t], rows(t, *me), local_sems.at[t]) for t in range(nt)]
        for cp in mine:
            cp.start()
        first = []
        for t in range(nt):
            first.append(copy(t, 0, me, sibling, src=x_refs[t]))
            first += [copy(t, 1 + j, me, (*chip, c), src=x_refs[t]) for j, chip in enumerate(chips)]
        for cp in first:
            cp.start()
        passed = []
        for t in range(nt):
            for j, chip in enumerate(chips):
                copy(t, 1 + j, (*chip, c), me).wait_recv()
                fwd = copy(t, 4 + j, (*chip, c), sibling)
                fwd.start()
                passed.append(fwd)
        for t in range(nt):
            copy(t, 0, sibling, me).wait_recv()
            for j, chip in enumerate(chips):
                copy(t, 4 + j, (*chip, 1 - c), me).wait_recv()
        for cp in first + passed:
            cp.wait_send()
        for cp in mine:
            cp.wait()

    return _pcall(body, name=name,
                  out_shape=[jax.ShapeDtypeStruct((N_DEV * a.shape[0], a.shape[1]), a.dtype) for a in shards],
                  in_specs=[_ANY] * nt, out_specs=[_ANY] * nt,
                  scratch_shapes=[pltpu.SemaphoreType.DMA((7 * nt,)), pltpu.SemaphoreType.DMA((7 * nt,)),
                                  pltpu.SemaphoreType.DMA((nt,))])(*shards)


_HBM = pl.BlockSpec(memory_space=pltpu.HBM)
_SEM = pl.BlockSpec(memory_space=pltpu.SEMAPHORE)
_EFFECT = pltpu.SideEffectType.DATAFLOW_SIDE_EFFECTING


def _peer_of(k):
    x, y, c = lax.axis_index("x"), lax.axis_index("y"), lax.axis_index("c")
    px, py, pc = x ^ ((k >> 2) & 1), y ^ ((k >> 1) & 1), c ^ (k & 1)
    return (px, py, pc), 4 * px + 2 * py + pc, 4 * x + 2 * y + c


def _direct_copy(t, k, src_refs, land_refs, send_sems, recv_sems, rows_of, gather):
    dev, peer, me = _peer_of(k)
    m = rows_of[t]
    src = src_refs[t] if gather else src_refs[t].at[pl.ds(peer * m, m), :]
    return pltpu.make_async_remote_copy(
        src_ref=src, dst_ref=land_refs[t].at[pl.ds(me * m, m), :],
        send_sem=send_sems.at[7 * t + k - 1], recv_sem=recv_sems.at[7 * t + k - 1],
        device_id=dev, device_id_type=pl.DeviceIdType.MESH)


def _direct_landing(t, k, src_refs, land_refs, send_sems, recv_sems, rows_of, gather):
    dev, peer, me = _peer_of(k)
    m = rows_of[t]
    src = src_refs[t] if gather else src_refs[t].at[pl.ds(me * m, m), :]
    return pltpu.make_async_remote_copy(
        src_ref=src, dst_ref=land_refs[t].at[pl.ds(peer * m, m), :],
        send_sem=send_sems.at[7 * t + k - 1], recv_sem=recv_sems.at[7 * t + k - 1],
        device_id=dev, device_id_type=pl.DeviceIdType.MESH)


def _direct_start(name, srcs, gather, dep=None):
    nt = len(srcs)
    rows_of = [a.shape[0] if gather else a.shape[0] // N_DEV for a in srcs]
    lands = [pltpu.with_memory_space_constraint(lax.empty((N_DEV * m, a.shape[1]), a.dtype), pltpu.HBM)
             for a, m in zip(srcs, rows_of)]

    n_dep = 0 if dep is None else 1

    def body(*refs):
        src_refs, land_refs = refs[:nt], refs[nt:2 * nt]
        send_sems, recv_sems = refs[2 * nt + n_dep], refs[2 * nt + n_dep + 1]
        token = refs[-1]
        for t in range(nt):
            for k in range(1, N_DEV):
                _direct_copy(t, k, src_refs, land_refs, send_sems, recv_sems, rows_of, gather).start()
        token[...] = jnp.zeros(token.shape, F32)

    out = _pcall(
        body, name=name,
        out_shape=(pltpu.SemaphoreType.DMA((7 * nt,)), pltpu.SemaphoreType.DMA((7 * nt,)),
                   *[pltpu.HBM(a.shape, a.dtype) for a in srcs], *[pltpu.HBM(a.shape, a.dtype) for a in lands],
                   jax.ShapeDtypeStruct((8, 128), F32)),
        in_specs=(_HBM,) * (2 * nt) + (pl.BlockSpec(memory_space=pl.ANY),) * n_dep,
        out_specs=(_SEM, _SEM) + (_HBM,) * (2 * nt) + (pl.BlockSpec(memory_space=pltpu.VMEM),),
        input_output_aliases={i: 2 + i for i in range(2 * nt)},
        compiler_params=pltpu.CompilerParams(has_side_effects=_EFFECT),
    )(*[pltpu.with_memory_space_constraint(a, pltpu.HBM) for a in srcs], *lands, *(() if dep is None else (dep,)))
    return (out[0], out[1], list(out[2:2 + nt]), list(out[2 + nt:2 + 2 * nt]), rows_of, gather), out[-1]


def _direct_wait(name, handle, after):
    send_sems, recv_sems, srcs, lands, rows_of, gather = handle
    nt = len(srcs)
    after = list(after) if isinstance(after, (list, tuple)) else [after]

    def body(*refs):
        src_refs, land_refs = refs[:nt], refs[nt:2 * nt]
        s_sems, r_sems = refs[2 * nt], refs[2 * nt + 1]
        for t in range(nt):
            for k in range(1, N_DEV):
                _direct_copy(t, k, src_refs, land_refs, s_sems, r_sems, rows_of, gather).wait_send()
                _direct_landing(t, k, src_refs, land_refs, s_sems, r_sems, rows_of, gather).wait_recv()

    out = _pcall(
        body, name=name,
        out_shape=tuple(pltpu.HBM(a.shape, a.dtype) for a in srcs) + tuple(pltpu.HBM(a.shape, a.dtype) for a in lands),
        in_specs=(_HBM,) * (2 * nt) + (_SEM, _SEM) + (pl.BlockSpec(memory_space=pl.ANY),) * len(after),
        out_specs=(_HBM,) * (2 * nt),
        input_output_aliases={i: i for i in range(2 * nt)},
        compiler_params=pltpu.CompilerParams(has_side_effects=_EFFECT),
    )(*srcs, *lands, send_sems, recv_sems, *after)
    return list(out[:nt]), list(out[nt:])


def _adamw_sharded(name, own, parts, w, m, v, rb, deps=()):
    R, N = own.shape

    def body(o_ref, p_ref, w_ref, m_ref, v_ref, *rest):
        g_ref, d_ref, nm_ref, nv_ref = rest[len(deps):]
        me = 4 * lax.axis_index("x") + 2 * lax.axis_index("y") + lax.axis_index("c")
        g = o_ref[...]
        for k in range(1, N_DEV):
            g = g + p_ref[me ^ k].astype(F32)
        nm = ADAM_B1 * m_ref[...] + (1.0 - ADAM_B1) * g
        nv = ADAM_B2 * v_ref[...] + (1.0 - ADAM_B2) * (g * g)
        m_hat = nm / (1.0 - ADAM_B1 ** ADAM_STEP)
        v_hat = nv / (1.0 - ADAM_B2 ** ADAM_STEP)
        g_ref[...] = g
        d_ref[...] = -ADAM_LR * (m_hat / (jnp.sqrt(v_hat) + ADAM_EPS) + ADAM_WD * w_ref[...])
        nm_ref[...] = nm
        nv_ref[...] = nv

    blk = pl.BlockSpec((rb, N), lambda i: (i, 0))
    sh = jax.ShapeDtypeStruct((R, N), F32)
    return _pcall(body, name=name, grid=(R // rb,),
                  in_specs=[blk, pl.BlockSpec((N_DEV, rb, N), lambda i: (0, i, 0)), blk, blk, blk]
                  + [pl.BlockSpec(d.shape, lambda i, nd=d.ndim: (0,) * nd) for d in deps],
                  out_specs=[blk] * 4, out_shape=[sh] * 4, compiler_params=_cparams(1))(own, parts, w, m, v, *deps)


LOSS_SLOT = "loss_partials"
SMALL_CLASSES = (
    (("s5_b_re", 32, 1024), ("s5_b_im", 32, 1024),
     ("norm_mix", 1, 1024), ("norm_ffn", 1, 1024), ("norm_ple", 1, 1024), ("final_norm", 1, 1024)),
    (("s5_d", 1, 512), ("s5_glu_b", 1, 512), ("rw_w0", 1, 512), ("rw_a0", 1, 512), ("rw_k_k", 1, 512), ("rw_k_a", 1, 512),
     ("rw_ln_w", 1, 512), ("rw_ln_b", 1, 512), ("rw_r_k", 1, 512)),
    (("rw_shift_mu", 1, 1792),),
    (("s5_lam_re", 32, 64), ("s5_lam_im", 32, 64), ("s5_c_re", 512, 64), ("s5_c_im", 512, 64)),
    (("s5_log_step", 1, 32), (LOSS_SLOT, 1, 32)),
)


def _class_rows(cls):
    return -(-sum(r for _, r, _ in cls) // 8) * 8


def _stack_class(cls, arrs):
    a = jnp.concatenate(arrs, axis=0) if len(arrs) > 1 else arrs[0]
    pad = _class_rows(cls) - a.shape[0]
    return jnp.pad(a, ((0, pad), (0, 0))) if pad else a


def _adamw_small(grads, w, m, v):
    names = [n for cls in SMALL_CLASSES for n, _, _ in cls]
    n_cls, n_par = len(SMALL_CLASSES), len(names)

    def body(*refs):
        g_refs = refs[:n_cls]
        w_refs, m_refs, v_refs = (refs[n_cls + i * n_par:n_cls + (i + 1) * n_par] for i in range(3))
        o_refs = refs[n_cls + 3 * n_par:]
        p = 0
        for cls, g_ref in zip(SMALL_CLASSES, g_refs):
            rc = _class_rows(cls)
            tot = g_ref[0:rc, :]
            for s_ in range(1, N_DEV):
                tot = tot + g_ref[s_ * rc:(s_ + 1) * rc, :]
            off = 0
            for _, r, _ in cls:
                g = tot[off:off + r, :]
                off += r
                nm = ADAM_B1 * m_refs[p][...] + (1.0 - ADAM_B1) * g
                nv = ADAM_B2 * v_refs[p][...] + (1.0 - ADAM_B2) * (g * g)
                m_hat = nm / (1.0 - ADAM_B1 ** ADAM_STEP)
                v_hat = nv / (1.0 - ADAM_B2 ** ADAM_STEP)
                o_refs[4 * p][...] = g
                o_refs[4 * p + 1][...] = -ADAM_LR * (m_hat / (jnp.sqrt(v_hat) + ADAM_EPS) + ADAM_WD * w_refs[p][...])
                o_refs[4 * p + 2][...] = nm
                o_refs[4 * p + 3][...] = nv
                p += 1

    shapes = [(r, c) for cls in SMALL_CLASSES for _, r, c in cls]
    out = _pcall(body, name="adamw_replicated",
                 out_shape=[jax.ShapeDtypeStruct(sh, F32) for sh in shapes for _ in range(4)],
                 compiler_params=pltpu.CompilerParams(vmem_limit_bytes=VMEM_LIMIT))(*grads, *w, *m, *v)
    return {n: out[4 * i:4 * i + 4] for i, n in enumerate(names)}


EARLY = (("w_in", True),)
LATE = (("ffn_w1", True), ("ffn_w3", True), ("ffn_w2", False), ("ple_gate_w", False), ("w_out", False))
GRAD_STAGES = (LATE[:4], LATE[4:])
MISC = (("s5_glu_w", False), ("rw_w2", True), ("rw_a2", True), ("rw_g2", True), ("ple_up_w", True))
SHARDED_NAMES = tuple(n for n, _ in EARLY + LATE + MISC)
PACK_COLS = 1024
WEIGHT_NAMES = ("norm_mix", "w_in", "s5_lam_re", "s5_lam_im", "s5_log_step", "s5_b_re", "s5_b_im", "s5_c_re", "s5_c_im", "s5_d",
                "s5_glu_w", "s5_glu_b", "rw_shift_mu", "rw_w0", "rw_w2", "rw_a0", "rw_a2", "rw_g2", "rw_k_k", "rw_k_a", "rw_r_k",
                "rw_ln_w", "rw_ln_b", "w_out", "norm_ffn", "ffn_w1", "ffn_w3", "ffn_w2", "norm_ple", "ple_gate_w", "ple_up_w",
                "final_norm")
SMALL_NAMES = tuple(n for n in WEIGHT_NAMES if n not in SHARDED_NAMES)
ARG_NAMES = ("x", "p") + WEIGHT_NAMES + ("loss_target",) + tuple("m_" + n for n in WEIGHT_NAMES) + tuple("v_" + n for n in WEIGHT_NAMES)


def _travel(a, tr):
    return a.T if tr else a


def _pack_misc(blocks):
    lead = blocks[0].shape[:-2]
    return jnp.concatenate([b.reshape(lead + (-1, PACK_COLS)) for b in blocks], axis=len(lead))


def _unpack_misc(packed, shapes):
    lead = packed.shape[:-2]
    out, off = [], 0
    for r, c in shapes:
        n = r * c // PACK_COLS
        out.append(lax.slice_in_dim(packed, off, off + n, axis=len(lead)).reshape(lead + (r, c)))
        off += n
    return out


def _kernel_impl(ins):
    x, p, target = ins["x"][0], ins["p"][0, 0], ins["loss_target"][0]
    me = 4 * lax.axis_index("x") + 2 * lax.axis_index("y") + lax.axis_index("c")
    small = {n: (ins[n] if n == "final_norm" else ins[n][0]) for n in SMALL_NAMES}
    trav = lambda pre, n, tr: _travel(ins[pre + n][0], tr)
    misc_shapes = [trav("", n, tr).shape for n, tr in MISC]

    early = _all_gather("ag_early", [trav("", n, tr).astype(BF16) for n, tr in EARLY]
                        + [_pack_misc([trav("", n, tr).astype(BF16) for n, tr in MISC])])
    late_handle, late_token = _direct_start("ag_late_start", [trav("", n, tr).astype(BF16) for n, tr in LATE], True, early[-1])
    W = dict(small)
    for (n, tr), g in zip(EARLY, early):
        W[n] = _travel(g, tr)
    for (n, tr), g in zip(MISC, _unpack_misc(early[-1].reshape(N_DEV, -1, PACK_COLS), misc_shapes)):
        W[n] = _travel(g.reshape(-1, g.shape[-1]), tr)

    def late_weights(after):
        shards, lands = _direct_wait("ag_late_wait", late_handle, after)
        full = [lax.dynamic_update_slice_in_dim(ld, sh, me * sh.shape[0], axis=0) for ld, sh in zip(lands, shards)]
        return {n: _travel(g, tr) for (n, tr), g in zip(LATE, full)}

    gt = lambda G, n, tr: _travel(G[n], tr)
    started = {}

    def grads_ready(stage, G):
        full = [gt(G, n, tr) for n, tr in GRAD_STAGES[stage]]
        started[stage] = (full, *_direct_start("grad_late_start%d" % stage, [a.astype(BF16) for a in full], False))
        return started[stage][2]

    loss_part, dx, G = _local_step(x, p, target, W, late_weights, grads_ready, late_token)

    misc_g = _pack_misc([gt(G, n, tr).reshape((N_DEV,) + shp) for (n, tr), shp in zip(MISC, misc_shapes)])
    early_full = [gt(G, n, tr) for n, tr in EARLY] + [misc_g.reshape(-1, PACK_COLS)]
    early_handle, early_token = _direct_start("grad_early_start", [a.astype(BF16) for a in early_full], False)
    view2 = lambda a, r, c: a.reshape(r, c)
    G[LOSS_SLOT] = jnp.full((1, 32), loss_part, F32)
    small_own = [_stack_class(cls, [view2(G[n], r, c) for n, r, c in cls]) for cls in SMALL_CLASSES]
    small_handle, small_token = _direct_start("grad_small_start", small_own, True)
    late_src, late_land = [], []
    for stage in range(len(GRAD_STAGES)):
        full, handle, _ = started[stage]
        _, land = _direct_wait("grad_late_wait%d" % stage, handle, small_token)
        late_src += full
        late_land += land

    outs = {}

    def emit(names_shapes, res):
        for tag, val in zip(("grad_", "delta_", "new_m_", "new_v_"), res):
            for n, v in names_shapes(val):
                outs[tag + n] = v

    def sharded_update(n, tr, src, land, deps=()):
        rows = src.shape[0] // N_DEV
        own = lax.dynamic_slice_in_dim(src, me * rows, rows, axis=0)
        res = _adamw_sharded("adamw_" + n, own, land.reshape(N_DEV, rows, land.shape[1]),
                             trav("", n, tr), trav("m_", n, tr), trav("v_", n, tr), _pick_rows(rows), deps)
        emit(lambda val: [(n, _travel(val, tr).reshape(ins[n].shape))], res)
        return list(res)

    for (n, tr), src, land in zip(LATE, late_src, late_land):
        sharded_update(n, tr, src, land, (early_token,))
    _, early_land = _direct_wait("grad_early_wait", early_handle, list(outs.values()))
    for (n, tr), src, land in zip(EARLY, early_full[:-1], early_land[:-1]):
        sharded_update(n, tr, src, land)
    pm = lambda pre: _pack_misc([trav(pre, n, tr) for n, tr in MISC])
    rows = early_full[-1].shape[0] // N_DEV
    res = _adamw_sharded("adamw_misc", lax.dynamic_slice_in_dim(early_full[-1], me * rows, rows, axis=0),
                         early_land[-1].reshape(N_DEV, rows, PACK_COLS), pm(""), pm("m_"), pm("v_"), rows)
    emit(lambda val: [(n, _travel(b, tr).reshape(ins[n].shape)) for (n, tr), b in zip(MISC, _unpack_misc(val, misc_shapes))], res)
    small_src, small_land = _direct_wait("grad_small_wait", small_handle, res[0])
    small_all = [lax.dynamic_update_slice_in_dim(ld, sr, me * sr.shape[0], axis=0) for ld, sr in zip(small_land, small_src)]
    flat_small = [(n, r, c) for cls in SMALL_CLASSES for n, r, c in cls]
    ins = dict(ins, **{pre + LOSS_SLOT: jnp.zeros((1, 32), F32) for pre in ("", "m_", "v_")})
    res = _adamw_small(small_all, *[[view2(ins[pre + n], r, c) for n, r, c in flat_small] for pre in ("", "m_", "v_")])
    loss = res.pop(LOSS_SLOT)[0][0, 0]
    for n, _, _ in flat_small[:-1]:
        for tag, val in zip(("grad_", "delta_", "new_m_", "new_v_"), res[n]):
            outs[tag + n] = val.reshape(ins[n].shape)
    res = [loss, dx[None]]
    for tag in ("grad_", "delta_", "new_m_", "new_v_"):
        res += [outs[tag + n] for n in WEIGHT_NAMES]
    return tuple(res)


def _pick_rows(r):
    best = 8
    for b in range(8, 257, 8):
        if r % b == 0:
            best = b
    return best


def kernel(x, p, norm_mix, w_in, s5_lam_re, s5_lam_im, s5_log_step, s5_b_re, s5_b_im, s5_c_re, s5_c_im, s5_d, s5_glu_w, s5_glu_b, rw_shift_mu, rw_w0, rw_w2, rw_a0, rw_a2, rw_g2, rw_k_k, rw_k_a, rw_r_k, rw_ln_w, rw_ln_b, w_out, norm_ffn, ffn_w1, ffn_w3, ffn_w2, norm_ple, ple_gate_w, ple_up_w, final_norm, loss_target, m_norm_mix, m_w_in, m_s5_lam_re, m_s5_lam_im, m_s5_log_step, m_s5_b_re, m_s5_b_im, m_s5_c_re, m_s5_c_im, m_s5_d, m_s5_glu_w, m_s5_glu_b, m_rw_shift_mu, m_rw_w0, m_rw_w2, m_rw_a0, m_rw_a2, m_rw_g2, m_rw_k_k, m_rw_k_a, m_rw_r_k, m_rw_ln_w, m_rw_ln_b, m_w_out, m_norm_ffn, m_ffn_w1, m_ffn_w3, m_ffn_w2, m_norm_ple, m_ple_gate_w, m_ple_up_w, m_final_norm, v_norm_mix, v_w_in, v_s5_lam_re, v_s5_lam_im, v_s5_log_step, v_s5_b_re, v_s5_b_im, v_s5_c_re, v_s5_c_im, v_s5_d, v_s5_glu_w, v_s5_glu_b, v_rw_shift_mu, v_rw_w0, v_rw_w2, v_rw_a0, v_rw_a2, v_rw_g2, v_rw_k_k, v_rw_k_a, v_rw_r_k, v_rw_ln_w, v_rw_ln_b, v_w_out, v_norm_ffn, v_ffn_w1, v_ffn_w3, v_ffn_w2, v_norm_ple, v_ple_gate_w, v_ple_up_w, v_final_norm):
    return _kernel_impl(dict(zip(ARG_NAMES, (x, p, norm_mix, w_in, s5_lam_re, s5_lam_im, s5_log_step, s5_b_re, s5_b_im, s5_c_re, s5_c_im, s5_d, s5_glu_w, s5_glu_b, rw_shift_mu, rw_w0, rw_w2, rw_a0, rw_a2, rw_g2, rw_k_k, rw_k_a, rw_r_k, rw_ln_w, rw_ln_b, w_out, norm_ffn, ffn_w1, ffn_w3, ffn_w2, norm_ple, ple_gate_w, ple_up_w, final_norm, loss_target, m_norm_mix, m_w_in, m_s5_lam_re, m_s5_lam_im, m_s5_log_step, m_s5_b_re, m_s5_b_im, m_s5_c_re, m_s5_c_im, m_s5_d, m_s5_glu_w, m_s5_glu_b, m_rw_shift_mu, m_rw_w0, m_rw_w2, m_rw_a0, m_rw_a2, m_rw_g2, m_rw_k_k, m_rw_k_a, m_rw_r_k, m_rw_ln_w, m_rw_ln_b, m_w_out, m_norm_ffn, m_ffn_w1, m_ffn_w3, m_ffn_w2, m_norm_ple, m_ple_gate_w, m_ple_up_w, m_final_norm, v_norm_mix, v_w_in, v_s5_lam_re, v_s5_lam_im, v_s5_log_step, v_s5_b_re, v_s5_b_im, v_s5_c_re, v_s5_c_im, v_s5_d, v_s5_glu_w, v_s5_glu_b, v_rw_shift_mu, v_rw_w0, v_rw_w2, v_rw_a0, v_rw_a2, v_rw_g2, v_rw_k_k, v_rw_k_a, v_rw_r_k, v_rw_ln_w, v_rw_ln_b, v_w_out, v_norm_ffn, v_ffn_w1, v_ffn_w3, v_ffn_w2, v_norm_ple, v_ple_gate_w, v_ple_up_w, v_final_norm))))
```
